```python
import jax, jax.numpy as jnp
from jax import lax
import numpy as np

D_MODEL = 1024
BATCH = 16
SEQ = 2048
DEPTH = 1

CHUNK = 64
D_HEAD = 64
A_HEADS = 8
A_PREV_CHUNKS = 8
MAX_REL = 128
B_Q_HEADS = 8
B_KV_HEADS = 2
B_GROUP = B_Q_HEADS // B_KV_HEADS
B_WINDOW = 128
B_PREV_CHUNKS = B_WINDOW // CHUNK
A_WIDTH = A_HEADS * D_HEAD
B_Q_WIDTH = B_Q_HEADS * D_HEAD
B_KV_WIDTH = B_KV_HEADS * D_HEAD
D_FF = 2816
REL_TABLE = (CHUNK - 1) + MAX_REL + 1
IN_WIDTH = 3 * A_WIDTH + B_Q_WIDTH + 2 * B_KV_WIDTH + 2 * D_MODEL
EPS = 1e-6
NEG_INF = -1e30

kernel_name = "streaming_hybrid_gated_chunk_attention_block"


def rms_norm(x, g):
    xf = x.astype(jnp.float32)
    y = xf * lax.rsqrt(jnp.mean(xf * xf, axis=-1, keepdims=True) + EPS)
    return y.astype(x.dtype) * g


def swiglu(h, w_gate, w_up, w_down):
    return (jax.nn.silu(h @ w_gate) * (h @ w_up)) @ w_down


def alibi_slopes(n):
    return np.array([2.0 ** (-8.0 * (i + 1) / n) for i in range(n)], dtype=np.float32)


def banded_chunk_attention(q, k, v, n_prev, bias, sinks=None):
    b, s, hkv, g, dh = q.shape
    n_chunks = s // CHUNK
    pad = n_prev * CHUNK
    band = (n_prev + 1) * CHUNK
    scale = 1.0 / np.sqrt(dh)
    kp = jnp.pad(k, ((0, 0), (pad, 0), (0, 0), (0, 0)))
    vp = jnp.pad(v, ((0, 0), (pad, 0), (0, 0), (0, 0)))
    key_off = jnp.arange(band)

    def one_chunk(c):
        start = c * CHUNK
        qc = lax.dynamic_slice_in_dim(q, start, CHUNK, axis=1)
        kc = lax.dynamic_slice_in_dim(kp, start, band, axis=1)
        vc = lax.dynamic_slice_in_dim(vp, start, band, axis=1)
        sc = jnp.einsum('bqhgd,bkhd->bhgqk', qc, kc).astype(jnp.float32) * scale + bias
        valid = (start - pad + key_off) >= 0
        sc = jnp.where(valid, sc, NEG_INF)
        if sinks is not None:
            sink_col = jnp.broadcast_to(sinks.astype(jnp.float32)[None, :, :, None, None],
                                        sc.shape[:-1] + (1,))
            p = jax.nn.softmax(jnp.concatenate([sc, sink_col], axis=-1), axis=-1)[..., :band]
        else:
            p = jax.nn.softmax(sc, axis=-1)
        return jnp.einsum('bhgqk,bkhd->bqhgd', p.astype(vc.dtype), vc)

    out = lax.map(one_chunk, jnp.arange(n_chunks))
    return jnp.moveaxis(out, 0, 1).reshape(b, s, hkv * g * dh)


def _fwd_setup_inputs(seed: int = 0) -> dict:
    key = jax.random.key(seed)
    ks = jax.random.split(key, 20)
    f32 = jnp.float32

    def w(k, shape, fan_in):
        return jax.random.normal(k, shape, f32) * fan_in ** -0.5

    def gain(k):
        return 1.0 + 0.01 * jax.random.normal(k, (D_MODEL,), f32)

    return {
        "x": jax.random.normal(ks[0], (BATCH, SEQ, D_MODEL), f32),
        "ffn1_norm": gain(ks[1]),
        "ffn1_w_gate": w(ks[2], (D_MODEL, D_FF), D_MODEL),
        "ffn1_w_up": w(ks[3], (D_MODEL, D_FF), D_MODEL),
        "ffn1_w_down": w(ks[4], (D_FF, D_MODEL), D_FF),
        "mix_norm": gain(ks[5]),
        "w_in": w(ks[6], (D_MODEL, IN_WIDTH), D_MODEL),
        "rel_bias": 0.5 * jax.random.normal(ks[7], (A_HEADS, REL_TABLE), f32),
        "sinks": jax.random.normal(ks[8], (B_Q_HEADS,), f32),
        "w_proj_a": w(ks[9], (A_WIDTH, D_MODEL), A_WIDTH),
        "w_proj_b": w(ks[10], (B_Q_WIDTH, D_MODEL), B_Q_WIDTH),
        "w_out": w(ks[11], (D_MODEL, D_MODEL), D_MODEL),
        "ffn2_norm": gain(ks[12]),
        "ffn2_w_gate": w(ks[13], (D_MODEL, D_FF), D_MODEL),
        "ffn2_w_up": w(ks[14], (D_MODEL, D_FF), D_MODEL),
        "ffn2_w_down": w(ks[15], (D_FF, D_MODEL), D_FF),
        "final_norm": gain(ks[16]),
    }


def _fwd_reference(x, ffn1_norm, ffn1_w_gate, ffn1_w_up, ffn1_w_down, mix_norm, w_in,
              rel_bias, sinks, w_proj_a, w_proj_b, w_out, ffn2_norm, ffn2_w_gate,
              ffn2_w_up, ffn2_w_down, final_norm):
    b, s, _ = x.shape

    qi = np.arange(CHUNK)[:, None]
    kj_a = np.arange((A_PREV_CHUNKS + 1) * CHUNK)[None, :]
    rel_a = qi - kj_a + A_PREV_CHUNKS * CHUNK
    idx_a = np.clip(rel_a, -(CHUNK - 1), MAX_REL) + (CHUNK - 1)
    kj_b = np.arange((B_PREV_CHUNKS + 1) * CHUNK)[None, :]
    dist_b = np.abs(qi - kj_b + B_PREV_CHUNKS * CHUNK).astype(np.float32)
    slopes = jnp.asarray(alibi_slopes(B_Q_HEADS)).reshape(B_KV_HEADS, B_GROUP)

    for _layer in range(DEPTH):
        x = x + 0.5 * swiglu(rms_norm(x, ffn1_norm), ffn1_w_gate, ffn1_w_up, ffn1_w_down)

        h = rms_norm(x, mix_norm)
        proj = h @ w_in
        cuts = np.cumsum([A_WIDTH, A_WIDTH, A_WIDTH, B_Q_WIDTH, B_KV_WIDTH, B_KV_WIDTH, D_MODEL])
        qa, ka, va, qb, kb, vb, gate_a, gate_b = jnp.split(proj, cuts, axis=-1)

        bias_a = rel_bias[:, idx_a].astype(jnp.float32)[:, None]
        ya = banded_chunk_attention(
            qa.reshape(b, s, A_HEADS, 1, D_HEAD),
            ka.reshape(b, s, A_HEADS, D_HEAD),
            va.reshape(b, s, A_HEADS, D_HEAD),
            A_PREV_CHUNKS, bias_a)
        ya = ya @ w_proj_a

        bias_b = -slopes[:, :, None, None] * jnp.asarray(dist_b)[None, None]
        yb = banded_chunk_attention(
            qb.reshape(b, s, B_KV_HEADS, B_GROUP, D_HEAD),
            kb.reshape(b, s, B_KV_HEADS, D_HEAD),
            vb.reshape(b, s, B_KV_HEADS, D_HEAD),
            B_PREV_CHUNKS, bias_b, sinks.reshape(B_KV_HEADS, B_GROUP))
        yb = yb @ w_proj_b

        merged = jax.nn.sigmoid(gate_a) * ya + jax.nn.sigmoid(gate_b) * yb
        x = x + merged @ w_out

        x = x + 0.5 * swiglu(rms_norm(x, ffn2_norm), ffn2_w_gate, ffn2_w_up, ffn2_w_down)

    return rms_norm(x, final_norm)


import jax as _jax
import jax.numpy as _jnp

TWIN_FORMAT = 'train_step'
FWD_PARAMS = ['x', 'ffn1_norm', 'ffn1_w_gate', 'ffn1_w_up', 'ffn1_w_down', 'mix_norm', 'w_in', 'rel_bias', 'sinks', 'w_proj_a', 'w_proj_b', 'w_out', 'ffn2_norm', 'ffn2_w_gate', 'ffn2_w_up', 'ffn2_w_down', 'final_norm']
TWIN_WEIGHTS = ['ffn1_norm', 'ffn1_w_gate', 'ffn1_w_up', 'ffn1_w_down', 'mix_norm', 'w_in', 'rel_bias', 'sinks', 'w_proj_a', 'w_proj_b', 'w_out', 'ffn2_norm', 'ffn2_w_gate', 'ffn2_w_up', 'ffn2_w_down', 'final_norm']
TWIN_DIFF_INPUT = 'x'
TWIN_INPUTS = ['x', 'ffn1_norm', 'ffn1_w_gate', 'ffn1_w_up', 'ffn1_w_down', 'mix_norm', 'w_in', 'rel_bias', 'sinks', 'w_proj_a', 'w_proj_b', 'w_out', 'ffn2_norm', 'ffn2_w_gate', 'ffn2_w_up', 'ffn2_w_down', 'final_norm', 'loss_target', 'm_ffn1_norm', 'm_ffn1_w_gate', 'm_ffn1_w_up', 'm_ffn1_w_down', 'm_mix_norm', 'm_w_in', 'm_rel_bias', 'm_sinks', 'm_w_proj_a', 'm_w_proj_b', 'm_w_out', 'm_ffn2_norm', 'm_ffn2_w_gate', 'm_ffn2_w_up', 'm_ffn2_w_down', 'm_final_norm', 'v_ffn1_norm', 'v_ffn1_w_gate', 'v_ffn1_w_up', 'v_ffn1_w_down', 'v_mix_norm', 'v_w_in', 'v_rel_bias', 'v_sinks', 'v_w_proj_a', 'v_w_proj_b', 'v_w_out', 'v_ffn2_norm', 'v_ffn2_w_gate', 'v_ffn2_w_up', 'v_ffn2_w_down', 'v_final_norm']
TWIN_OUTPUTS = ['loss', 'grad_x', 'grad_ffn1_norm', 'grad_ffn1_w_gate', 'grad_ffn1_w_up', 'grad_ffn1_w_down', 'grad_mix_norm', 'grad_w_in', 'grad_rel_bias', 'grad_sinks', 'grad_w_proj_a', 'grad_w_proj_b', 'grad_w_out', 'grad_ffn2_norm', 'grad_ffn2_w_gate', 'grad_ffn2_w_up', 'grad_ffn2_w_down', 'grad_final_norm', 'delta_ffn1_norm', 'delta_ffn1_w_gate', 'delta_ffn1_w_up', 'delta_ffn1_w_down', 'delta_mix_norm', 'delta_w_in', 'delta_rel_bias', 'delta_sinks', 'delta_w_proj_a', 'delta_w_proj_b', 'delta_w_out', 'delta_ffn2_norm', 'delta_ffn2_w_gate', 'delta_ffn2_w_up', 'delta_ffn2_w_down', 'delta_final_norm', 'new_m_ffn1_norm', 'new_m_ffn1_w_gate', 'new_m_ffn1_w_up', 'new_m_ffn1_w_down', 'new_m_mix_norm', 'new_m_w_in', 'new_m_rel_bias', 'new_m_sinks', 'new_m_w_proj_a', 'new_m_w_proj_b', 'new_m_w_out', 'new_m_ffn2_norm', 'new_m_ffn2_w_gate', 'new_m_ffn2_w_up', 'new_m_ffn2_w_down', 'new_m_final_norm', 'new_v_ffn1_norm', 'new_v_ffn1_w_gate', 'new_v_ffn1_w_up', 'new_v_ffn1_w_down', 'new_v_mix_norm', 'new_v_w_in', 'new_v_rel_bias', 'new_v_sinks', 'new_v_w_proj_a', 'new_v_w_proj_b', 'new_v_w_out', 'new_v_ffn2_norm', 'new_v_ffn2_w_gate', 'new_v_ffn2_w_up', 'new_v_ffn2_w_down', 'new_v_final_norm']
TWIN_LEAF_KINDS = {'loss': 'loss', 'grad_x': 'grad_x', 'grad_ffn1_norm': 'grad_w', 'grad_ffn1_w_gate': 'grad_w', 'grad_ffn1_w_up': 'grad_w', 'grad_ffn1_w_down': 'grad_w', 'grad_mix_norm': 'grad_w', 'grad_w_in': 'grad_w', 'grad_rel_bias': 'grad_w', 'grad_sinks': 'grad_w', 'grad_w_proj_a': 'grad_w', 'grad_w_proj_b': 'grad_w', 'grad_w_out': 'grad_w', 'grad_ffn2_norm': 'grad_w', 'grad_ffn2_w_gate': 'grad_w', 'grad_ffn2_w_up': 'grad_w', 'grad_ffn2_w_down': 'grad_w', 'grad_final_norm': 'grad_w', 'delta_ffn1_norm': 'delta_w', 'delta_ffn1_w_gate': 'delta_w', 'delta_ffn1_w_up': 'delta_w', 'delta_ffn1_w_down': 'delta_w', 'delta_mix_norm': 'delta_w', 'delta_w_in': 'delta_w', 'delta_rel_bias': 'delta_w', 'delta_sinks': 'delta_w', 'delta_w_proj_a': 'delta_w', 'delta_w_proj_b': 'delta_w', 'delta_w_out': 'delta_w', 'delta_ffn2_norm': 'delta_w', 'delta_ffn2_w_gate': 'delta_w', 'delta_ffn2_w_up': 'delta_w', 'delta_ffn2_w_down': 'delta_w', 'delta_final_norm': 'delta_w', 'new_m_ffn1_norm': 'new_m', 'new_m_ffn1_w_gate': 'new_m', 'new_m_ffn1_w_up': 'new_m', 'new_m_ffn1_w_down': 'new_m', 'new_m_mix_norm': 'new_m', 'new_m_w_in': 'new_m', 'new_m_rel_bias': 'new_m', 'new_m_sinks': 'new_m', 'new_m_w_proj_a': 'new_m', 'new_m_w_proj_b': 'new_m', 'new_m_w_out': 'new_m', 'new_m_ffn2_norm': 'new_m', 'new_m_ffn2_w_gate': 'new_m', 'new_m_ffn2_w_up': 'new_m', 'new_m_ffn2_w_down': 'new_m', 'new_m_final_norm': 'new_m', 'new_v_ffn1_norm': 'new_v', 'new_v_ffn1_w_gate': 'new_v', 'new_v_ffn1_w_up': 'new_v', 'new_v_ffn1_w_down': 'new_v', 'new_v_mix_norm': 'new_v', 'new_v_w_in': 'new_v', 'new_v_rel_bias': 'new_v', 'new_v_sinks': 'new_v', 'new_v_w_proj_a': 'new_v', 'new_v_w_proj_b': 'new_v', 'new_v_w_out': 'new_v', 'new_v_ffn2_norm': 'new_v', 'new_v_ffn2_w_gate': 'new_v', 'new_v_ffn2_w_up': 'new_v', 'new_v_ffn2_w_down': 'new_v', 'new_v_final_norm': 'new_v'}


def _forward(args):
    return _fwd_reference(*[args[k] for k in FWD_PARAMS])


def _output_shape():
    out = _jax.eval_shape(lambda: _forward(_fwd_setup_inputs(0)))
    return out.shape, out.dtype

N_MICROBATCH = 1
ADAM_LR = 0.001
ADAM_B1 = 0.9
ADAM_B2 = 0.999
ADAM_EPS = 1e-08
ADAM_WD = 0.01
ADAM_STEP = 10
PER_EXAMPLE_BATCH_AXIS = {'x': 0, 'loss_target': 0}
SHARED_INPUTS = []
_WEIGHT_DTYPES = {'ffn1_norm': _jnp.float32, 'ffn1_w_gate': _jnp.float32, 'ffn1_w_up': _jnp.float32, 'ffn1_w_down': _jnp.float32, 'mix_norm': _jnp.float32, 'w_in': _jnp.float32, 'rel_bias': _jnp.float32, 'sinks': _jnp.float32, 'w_proj_a': _jnp.float32, 'w_proj_b': _jnp.float32, 'w_out': _jnp.float32, 'ffn2_norm': _jnp.float32, 'ffn2_w_gate': _jnp.float32, 'ffn2_w_up': _jnp.float32, 'ffn2_w_down': _jnp.float32, 'final_norm': _jnp.float32}
MOMENT_SCALE = {'ffn1_norm': 8.651396e-02, 'ffn1_w_gate': 3.337965e-02, 'ffn1_w_up': 3.226273e-02, 'ffn1_w_down': 5.353924e-02, 'mix_norm': 4.784721e-02, 'w_in': 2.282769e-02, 'rel_bias': 1.113533e-02, 'sinks': 2.916084e-02, 'w_proj_a': 1.545862e-02, 'w_proj_b': 2.463227e-02, 'w_out': 2.909558e-02, 'ffn2_norm': 7.864135e-02, 'ffn2_w_gate': 3.011568e-02, 'ffn2_w_up': 2.910459e-02, 'ffn2_w_down': 4.826999e-02, 'final_norm': 3.197909e+01}


def _to_microbatches(a, axis):
    t = _jnp.moveaxis(a, axis, 0)
    t = t.reshape((N_MICROBATCH, t.shape[0] // N_MICROBATCH) + t.shape[1:])
    return _jnp.moveaxis(t, 1, axis + 1)


def setup_inputs(seed: int = 0) -> dict:
    inp = _fwd_setup_inputs(seed)
    key = _jax.random.fold_in(_jax.random.key(seed), 7919)
    shape, _ = _output_shape()
    out = dict(inp)
    out["loss_target"] = _jax.random.normal(_jax.random.fold_in(key, 0), shape, _jnp.float32)
    for i, name in enumerate(TWIN_WEIGHTS):
        w = inp[name].astype(_jnp.float32)
        if MOMENT_SCALE is None:
            s = _jnp.sqrt(_jnp.mean(_jnp.square(w)) + 1e-30)
        else:
            s = MOMENT_SCALE[name]
        km, kv = _jax.random.split(_jax.random.fold_in(key, i + 1))
        out[name] = w
        out["m_" + name] = s * _jax.random.normal(km, w.shape, _jnp.float32)
        out["v_" + name] = (s * s) * _jax.random.uniform(kv, w.shape, _jnp.float32, 0.5, 1.5)
    if N_MICROBATCH > 1:
        for name, axis in PER_EXAMPLE_BATCH_AXIS.items():
            out[name] = _to_microbatches(out[name], axis)
    return {'x': out['x'], 'ffn1_norm': out['ffn1_norm'], 'ffn1_w_gate': out['ffn1_w_gate'], 'ffn1_w_up': out['ffn1_w_up'], 'ffn1_w_down': out['ffn1_w_down'], 'mix_norm': out['mix_norm'], 'w_in': out['w_in'], 'rel_bias': out['rel_bias'], 'sinks': out['sinks'], 'w_proj_a': out['w_proj_a'], 'w_proj_b': out['w_proj_b'], 'w_out': out['w_out'], 'ffn2_norm': out['ffn2_norm'], 'ffn2_w_gate': out['ffn2_w_gate'], 'ffn2_w_up': out['ffn2_w_up'], 'ffn2_w_down': out['ffn2_w_down'], 'final_norm': out['final_norm'], 'loss_target': out['loss_target'], 'm_ffn1_norm': out['m_ffn1_norm'], 'm_ffn1_w_gate': out['m_ffn1_w_gate'], 'm_ffn1_w_up': out['m_ffn1_w_up'], 'm_ffn1_w_down': out['m_ffn1_w_down'], 'm_mix_norm': out['m_mix_norm'], 'm_w_in': out['m_w_in'], 'm_rel_bias': out['m_rel_bias'], 'm_sinks': out['m_sinks'], 'm_w_proj_a': out['m_w_proj_a'], 'm_w_proj_b': out['m_w_proj_b'], 'm_w_out': out['m_w_out'], 'm_ffn2_norm': out['m_ffn2_norm'], 'm_ffn2_w_gate': out['m_ffn2_w_gate'], 'm_ffn2_w_up': out['m_ffn2_w_up'], 'm_ffn2_w_down': out['m_ffn2_w_down'], 'm_final_norm': out['m_final_norm'], 'v_ffn1_norm': out['v_ffn1_norm'], 'v_ffn1_w_gate': out['v_ffn1_w_gate'], 'v_ffn1_w_up': out['v_ffn1_w_up'], 'v_ffn1_w_down': out['v_ffn1_w_down'], 'v_mix_norm': out['v_mix_norm'], 'v_w_in': out['v_w_in'], 'v_rel_bias': out['v_rel_bias'], 'v_sinks': out['v_sinks'], 'v_w_proj_a': out['v_w_proj_a'], 'v_w_proj_b': out['v_w_proj_b'], 'v_w_out': out['v_w_out'], 'v_ffn2_norm': out['v_ffn2_norm'], 'v_ffn2_w_gate': out['v_ffn2_w_gate'], 'v_ffn2_w_up': out['v_ffn2_w_up'], 'v_ffn2_w_down': out['v_ffn2_w_down'], 'v_final_norm': out['v_final_norm']}


def _loss(weights, diff, rest, loss_target):
    with _jax.named_scope("forward"):
        args = {**rest, TWIN_DIFF_INPUT: diff, **{k: w.astype(_WEIGHT_DTYPES[k]) for k, w in weights.items()}}
        y = _forward(args)
    with _jax.named_scope("loss_head"):
        err = _jnp.square(y.astype(_jnp.float32) - loss_target)
        return 0.5 * _jnp.sum(_jnp.mean(err, axis=-1)) if err.ndim else 0.5 * err


def _adamw(w, g, m, v):
    m = ADAM_B1 * m + (1.0 - ADAM_B1) * g
    v = ADAM_B2 * v + (1.0 - ADAM_B2) * _jnp.square(g)
    m_hat = m / (1.0 - ADAM_B1 ** ADAM_STEP)
    v_hat = v / (1.0 - ADAM_B2 ** ADAM_STEP)
    delta = -ADAM_LR * (m_hat / (_jnp.sqrt(v_hat) + ADAM_EPS) + ADAM_WD * w)
    return delta, m, v


def reference(x, ffn1_norm, ffn1_w_gate, ffn1_w_up, ffn1_w_down, mix_norm, w_in, rel_bias, sinks, w_proj_a, w_proj_b, w_out, ffn2_norm, ffn2_w_gate, ffn2_w_up, ffn2_w_down, final_norm, loss_target, m_ffn1_norm, m_ffn1_w_gate, m_ffn1_w_up, m_ffn1_w_down, m_mix_norm, m_w_in, m_rel_bias, m_sinks, m_w_proj_a, m_w_proj_b, m_w_out, m_ffn2_norm, m_ffn2_w_gate, m_ffn2_w_up, m_ffn2_w_down, m_final_norm, v_ffn1_norm, v_ffn1_w_gate, v_ffn1_w_up, v_ffn1_w_down, v_mix_norm, v_w_in, v_rel_bias, v_sinks, v_w_proj_a, v_w_proj_b, v_w_out, v_ffn2_norm, v_ffn2_w_gate, v_ffn2_w_up, v_ffn2_w_down, v_final_norm):
    given = dict(x=x, ffn1_norm=ffn1_norm, ffn1_w_gate=ffn1_w_gate, ffn1_w_up=ffn1_w_up, ffn1_w_down=ffn1_w_down, mix_norm=mix_norm, w_in=w_in, rel_bias=rel_bias, sinks=sinks, w_proj_a=w_proj_a, w_proj_b=w_proj_b, w_out=w_out, ffn2_norm=ffn2_norm, ffn2_w_gate=ffn2_w_gate, ffn2_w_up=ffn2_w_up, ffn2_w_down=ffn2_w_down, final_norm=final_norm, loss_target=loss_target, m_ffn1_norm=m_ffn1_norm, m_ffn1_w_gate=m_ffn1_w_gate, m_ffn1_w_up=m_ffn1_w_up, m_ffn1_w_down=m_ffn1_w_down, m_mix_norm=m_mix_norm, m_w_in=m_w_in, m_rel_bias=m_rel_bias, m_sinks=m_sinks, m_w_proj_a=m_w_proj_a, m_w_proj_b=m_w_proj_b, m_w_out=m_w_out, m_ffn2_norm=m_ffn2_norm, m_ffn2_w_gate=m_ffn2_w_gate, m_ffn2_w_up=m_ffn2_w_up, m_ffn2_w_down=m_ffn2_w_down, m_final_norm=m_final_norm, v_ffn1_norm=v_ffn1_norm, v_ffn1_w_gate=v_ffn1_w_gate, v_ffn1_w_up=v_ffn1_w_up, v_ffn1_w_down=v_ffn1_w_down, v_mix_norm=v_mix_norm, v_w_in=v_w_in, v_rel_bias=v_rel_bias, v_sinks=v_sinks, v_w_proj_a=v_w_proj_a, v_w_proj_b=v_w_proj_b, v_w_out=v_w_out, v_ffn2_norm=v_ffn2_norm, v_ffn2_w_gate=v_ffn2_w_gate, v_ffn2_w_up=v_ffn2_w_up, v_ffn2_w_down=v_ffn2_w_down, v_final_norm=v_final_norm)
    weights = {n: given[n] for n in TWIN_WEIGHTS}
    shared = {n: given[n] for n in SHARED_INPUTS}
    per_example = {n: given[n] for n in ['x']}
    grad_fn = _jax.value_and_grad(_loss, argnums=(0, 1))

    def one_microbatch(ex, loss_target):
        ex = dict(ex)
        diff = ex.pop(TWIN_DIFF_INPUT)
        return grad_fn(weights, diff, {**shared, **ex}, loss_target)

    if N_MICROBATCH == 1:
        loss, (grad_w, grad_x) = one_microbatch(per_example, given["loss_target"])
    else:
        def body(carry, xs):
            loss_sum, grad_sum = carry
            l_k, (gw_k, gx_k) = one_microbatch(xs[0], xs[1])
            with _jax.named_scope("update"):
                return (loss_sum + l_k, _jax.tree.map(_jnp.add, grad_sum, gw_k)), gx_k

        init = (_jnp.zeros((), _jnp.float32), _jax.tree.map(_jnp.zeros_like, weights))
        (loss, grad_w), grad_x = _jax.lax.scan(body, init, (per_example, given["loss_target"]))
    with _jax.named_scope("update"):
        delta_w, new_m, new_v = {}, {}, {}
        for n in TWIN_WEIGHTS:
            delta_w[n], new_m[n], new_v[n] = _adamw(weights[n], grad_w[n], given["m_" + n], given["v_" + n])
    return (loss, grad_x, *[grad_w[n] for n in TWIN_WEIGHTS], *[delta_w[n] for n in TWIN_WEIGHTS],
            *[new_m[n] for n in TWIN_WEIGHTS], *[new_v[n] for n in TWIN_WEIGHTS])
```

```python
import functools

import numpy as np
import jax
import jax.numpy as jnp
from jax import lax
from jax.experimental import pallas as pl
from jax.experimental.pallas import tpu as pltpu

F32 = jnp.float32
BF16 = jnp.bfloat16

D_MODEL = 1024
D_FF = 2816
CHUNK = 64
D_HEAD = 64
A_HEADS = 8
A_PREV = 8
MAX_REL = 128
B_Q_HEADS = 8
B_KV_HEADS = 2
B_GROUP = B_Q_HEADS // B_KV_HEADS
B_PREV = 2
REL_TABLE = (CHUNK - 1) + MAX_REL + 1
A_WIDTH = A_HEADS * D_HEAD
B_Q_WIDTH = B_Q_HEADS * D_HEAD
B_KV_WIDTH = B_KV_HEADS * D_HEAD
QKV_A = 3 * A_WIDTH
QKV_B = B_Q_WIDTH + 2 * B_KV_WIDTH
IN_WIDTH = QKV_A + QKV_B + 2 * D_MODEL
EPS = 1e-6
NEG_INF = -1e30
SCALE = 1.0 / 8.0

ADAM_LR = 0.001
ADAM_B1 = 0.9
ADAM_B2 = 0.999
ADAM_EPS = 1e-08
ADAM_WD = 0.01
ADAM_STEP = 10

N_DEV = 8
N_CHIP = 4
MESH = pl.DeviceIdType.MESH

TQ = 256
TM = 256
FC = 256
VMEM_LIMIT = 56 << 20


def _cparams(sem, vmem=VMEM_LIMIT):
    return pltpu.CompilerParams(dimension_semantics=sem, vmem_limit_bytes=vmem)


def _dot_nt(a, b):
    return lax.dot_general(a, b, (((1,), (1,)), ((), ())), preferred_element_type=F32)


def _dot_nn(a, b):
    return lax.dot_general(a, b, (((1,), (0,)), ((), ())), preferred_element_type=F32)


def _dot_tn(a, b):
    return lax.dot_general(a, b, (((0,), (0,)), ((), ())), preferred_element_type=F32)


def _resident(shape):
    nd = len(shape)
    return pl.BlockSpec(shape, lambda *_: (0,) * nd, pipeline_mode=pl.Buffered(1))


def _rows(tm, width):
    return pl.BlockSpec((tm, width), lambda i: (i, 0))


def _colsum8(v):
    tm, n = v.shape
    return jnp.sum(v.reshape(tm // 8, 8, n), axis=0)


def _rms(x):
    r = lax.rsqrt(jnp.mean(x * x, axis=-1, keepdims=True) + EPS)
    return x * r, r


def _rms_bwd(dh, xh, r, gamma):
    dxh = dh * gamma
    dx = r * (dxh - xh * jnp.mean(dxh * xh, axis=-1, keepdims=True))
    return dx, _colsum8(dh * xh)


def _ffn_fwd(x, gamma, wg_t, wu_t, wd, name):
    t = x.shape[0]
    f = wg_t.shape[0]

    def body(x_ref, gam_ref, wg_ref, wu_ref, wd_ref, h_ref, g_ref, u_ref, a_ref, y_ref):
        xv = x_ref[...]
        xh, _ = _rms(xv)
        h = (xh * gam_ref[...]).astype(BF16)
        h_ref[...] = h
        for j in range(f // FC):
            sl = slice(j * FC, (j + 1) * FC)
            g = _dot_nt(h, wg_ref[sl, :])
            u = _dot_nt(h, wu_ref[sl, :])
            g_ref[:, sl] = g.astype(BF16)
            u_ref[:, sl] = u.astype(BF16)
            a_ref[:, sl] = (g * jax.nn.sigmoid(g) * u).astype(BF16)
        y_ref[...] = xv + 0.5 * _dot_nn(a_ref[...], wd_ref[...])

    return pl.pallas_call(
        body,
        name=name,
        grid=(t // TM,),
        in_specs=[_rows(TM, D_MODEL), _resident((1, D_MODEL)), _resident((f, D_MODEL)), _resident((f, D_MODEL)),
                  _resident((f, D_MODEL))],
        out_specs=[_rows(TM, D_MODEL), _rows(TM, f), _rows(TM, f), _rows(TM, f), _rows(TM, D_MODEL)],
        out_shape=[jax.ShapeDtypeStruct((t, D_MODEL), BF16), jax.ShapeDtypeStruct((t, f), BF16),
                   jax.ShapeDtypeStruct((t, f), BF16), jax.ShapeDtypeStruct((t, f), BF16),
                   jax.ShapeDtypeStruct((t, D_MODEL), F32)],
        compiler_params=_cparams(("parallel",)),
    )(x, gamma, wg_t, wu_t, wd)


def _ffn_bwd(d, x, gamma, g_act, u_act, wg_t, wu_t, wd, name):
    t = x.shape[0]
    f = wg_t.shape[0]

    def body(d_ref, x_ref, gam_ref, g_ref, u_ref, wg_ref, wu_ref, wd_ref, dx_ref, dg_ref, du_ref, db_ref, dgam_ref):
        dv = d_ref[...]
        db = (0.5 * dv).astype(BF16)
        db_ref[...] = db
        for j in range(f // FC):
            sl = slice(j * FC, (j + 1) * FC)
            da = _dot_nt(db, wd_ref[sl, :])
            g = g_ref[:, sl].astype(F32)
            u = u_ref[:, sl].astype(F32)
            s = jax.nn.sigmoid(g)
            dg_ref[:, sl] = (da * u * (s * (1.0 + g * (1.0 - s)))).astype(BF16)
            du_ref[:, sl] = (da * (g * s)).astype(BF16)
        dh = _dot_nn(dg_ref[...], wg_ref[...]) + _dot_nn(du_ref[...], wu_ref[...])
        xh, r = _rms(x_ref[...])
        dxn, dgam = _rms_bwd(dh, xh, r, gam_ref[...])
        dx_ref[...] = dv + dxn

        @pl.when(pl.program_id(0) == 0)
        def _():
            dgam_ref[...] = jnp.zeros_like(dgam_ref)

        dgam_ref[...] += dgam

    return pl.pallas_call(
        body,
        name=name,
        grid=(t // TM,),
        in_specs=[_rows(TM, D_MODEL), _rows(TM, D_MODEL), _resident((1, D_MODEL)), _rows(TM, f), _rows(TM, f),
                  _resident((f, D_MODEL)), _resident((f, D_MODEL)), _resident((f, D_MODEL))],
        out_specs=[_rows(TM, D_MODEL), _rows(TM, f), _rows(TM, f), _rows(TM, D_MODEL),
                   pl.BlockSpec((8, D_MODEL), lambda i: (0, 0))],
        out_shape=[jax.ShapeDtypeStruct((t, D_MODEL), F32), jax.ShapeDtypeStruct((t, f), BF16),
                   jax.ShapeDtypeStruct((t, f), BF16), jax.ShapeDtypeStruct((t, D_MODEL), BF16),
                   jax.ShapeDtypeStruct((8, D_MODEL), F32)],
        compiler_params=_cparams(("arbitrary",)),
    )(d, x, gamma, g_act, u_act, wg_t, wu_t, wd)


def _mm_tn(a, b, tile_m, name, tk=512):
    t, m = a.shape
    n = b.shape[1]
    nk = t // tk

    def body(a_ref, b_ref, o_ref, acc_ref):
        k = pl.program_id(1)

        @pl.when(k == 0)
        def _():
            acc_ref[...] = jnp.zeros_like(acc_ref)

        acc_ref[...] += _dot_tn(a_ref[...], b_ref[...])

        @pl.when(k == nk - 1)
        def _():
            o_ref[...] = acc_ref[...].astype(o_ref.dtype)

    return pl.pallas_call(
        body,
        name=name,
        grid=(m // tile_m, nk),
        in_specs=[pl.BlockSpec((tk, tile_m), lambda i, k: (k, i)), pl.BlockSpec((tk, n), lambda i, k: (k, 0))],
        out_specs=pl.BlockSpec((tile_m, n), lambda i, k: (i, 0)),
        out_shape=jax.ShapeDtypeStruct((m, n), BF16),
        scratch_shapes=[pltpu.VMEM((tile_m, n), F32)],
        compiler_params=_cparams(("parallel", "arbitrary")),
    )(a, b)


def _proj_fwd(x, gamma, win_t):
    t = x.shape[0]

    def body(x_ref, gam_ref, w_ref, h_ref, qa_ref, qb_ref, gt_ref):
        xh, _ = _rms(x_ref[...])
        h = (xh * gam_ref[...]).astype(BF16)
        h_ref[...] = h
        for j in range(QKV_A // FC):
            qa_ref[:, j * FC:(j + 1) * FC] = _dot_nt(h, w_ref[j * FC:(j + 1) * FC, :]).astype(BF16)
        for j in range(QKV_B // FC):
            lo = QKV_A + j * FC
            qb_ref[:, j * FC:(j + 1) * FC] = _dot_nt(h, w_ref[lo:lo + FC, :]).astype(BF16)
        for j in range(2 * D_MODEL // FC):
            lo = QKV_A + QKV_B + j * FC
            gt_ref[:, j * FC:(j + 1) * FC] = _dot_nt(h, w_ref[lo:lo + FC, :])

    return pl.pallas_call(
        body,
        name="proj_fwd",
        grid=(t // TM,),
        in_specs=[_rows(TM, D_MODEL), _resident((1, D_MODEL)), _resident((IN_WIDTH, D_MODEL))],
        out_specs=[_rows(TM, D_MODEL), _rows(TM, QKV_A), _rows(TM, QKV_B), _rows(TM, 2 * D_MODEL)],
        out_shape=[jax.ShapeDtypeStruct((t, D_MODEL), BF16), jax.ShapeDtypeStruct((t, QKV_A), BF16),
                   jax.ShapeDtypeStruct((t, QKV_B), BF16), jax.ShapeDtypeStruct((t, 2 * D_MODEL), F32)],
        compiler_params=_cparams(("parallel",)),
    )(x, gamma, win_t)


def _proj_bwd(d, x, gamma, dqa, dqb, dgt, win_t):
    t = x.shape[0]

    def body(d_ref, x_ref, gam_ref, dqa_ref, dqb_ref, dgt_ref, w_ref, dx_ref, dgam_ref):
        dh = _dot_nn(dqa_ref[...], w_ref[0:QKV_A, :])
        dh += _dot_nn(dqb_ref[...], w_ref[QKV_A:QKV_A + QKV_B, :])
        dh += _dot_nn(dgt_ref[...], w_ref[QKV_A + QKV_B:IN_WIDTH, :])
        xh, r = _rms(x_ref[...])
        dxn, dgam = _rms_bwd(dh, xh, r, gam_ref[...])
        dx_ref[...] = d_ref[...] + dxn

        @pl.when(pl.program_id(0) == 0)
        def _():
            dgam_ref[...] = jnp.zeros_like(dgam_ref)

        dgam_ref[...] += dgam

    return pl.pallas_call(
        body,
        name="proj_bwd",
        grid=(t // TM,),
        in_specs=[_rows(TM, D_MODEL), _rows(TM, D_MODEL), _resident((1, D_MODEL)), _rows(TM, QKV_A), _rows(TM, QKV_B),
                  _rows(TM, 2 * D_MODEL), _resident((IN_WIDTH, D_MODEL))],
        out_specs=[_rows(TM, D_MODEL), pl.BlockSpec((8, D_MODEL), lambda i: (0, 0))],
        out_shape=[jax.ShapeDtypeStruct((t, D_MODEL), F32), jax.ShapeDtypeStruct((8, D_MODEL), F32)],
        compiler_params=_cparams(("arbitrary",)),
    )(d, x, gamma, dqa, dqb, dgt, win_t)


def _band_softmax(q, kk, bias_ref, sink_ref, qs, pad):
    s = _dot_nt(q, kk) * SCALE + bias_ref[...]
    col = lax.broadcasted_iota(jnp.int32, s.shape, 1)
    s = jnp.where(col + qs >= pad, s, NEG_INF)
    m = jnp.max(s, axis=-1, keepdims=True)
    if sink_ref is not None:
        m = jnp.maximum(m, sink_ref[...])
    p = jnp.exp(s - m)
    den = jnp.sum(p, axis=-1, keepdims=True)
    if sink_ref is not None:
        den = den + jnp.exp(sink_ref[...] - m)
    return p, m, 1.0 / den


def _attn_fwd(q, k, v, bias, sink, n_prev, name):
    bsz, h, g, s_len, dh = q.shape
    pad = n_prev * CHUNK
    band = TQ + pad
    rows = g * TQ
    has_sink = sink is not None

    def body(*refs):
        if has_sink:
            q_ref, k_ref, v_ref, b_ref, s_ref, o_ref = refs
        else:
            q_ref, k_ref, v_ref, b_ref, o_ref = refs
            s_ref = None
        qs = pl.multiple_of(pl.program_id(2) * TQ, TQ)
        qq = q_ref[...].reshape(rows, dh)
        kk = k_ref[pl.ds(qs, band), :]
        vv = v_ref[pl.ds(qs, band), :]
        p, _, inv = _band_softmax(qq, kk, b_ref, s_ref, qs, pad)
        o = _dot_nn(p.astype(BF16), vv) * inv
        o_ref[...] = o.reshape(g, TQ, dh).astype(BF16)

    in_specs = [pl.BlockSpec((None, None, g, TQ, dh), lambda b, hh, i: (b, hh, 0, i, 0)),
                pl.BlockSpec((None, None, pad + s_len, dh), lambda b, hh, i: (b, hh, 0, 0)),
                pl.BlockSpec((None, None, pad + s_len, dh), lambda b, hh, i: (b, hh, 0, 0)),
                pl.BlockSpec((None, rows, band), lambda b, hh, i: (hh, 0, 0))]
    args = [q, k, v, bias]
    if has_sink:
        in_specs.append(pl.BlockSpec((None, rows, 1), lambda b, hh, i: (hh, 0, 0)))
        args.append(sink)
    return pl.pallas_call(
        body,
        name=name,
        grid=(bsz, h, s_len // TQ),
        in_specs=in_specs,
        out_specs=pl.BlockSpec((None, None, g, TQ, dh), lambda b, hh, i: (b, hh, 0, i, 0)),
        out_shape=jax.ShapeDtypeStruct(q.shape, BF16),
        compiler_params=_cparams(("parallel", "parallel", "parallel")),
    )(*args)


def _attn_bwd(q, k, v, bias, sink, do, n_prev, want_dbias, name):
    bsz, h, g, s_len, dh = q.shape
    pad = n_prev * CHUNK
    band = TQ + pad
    rows = g * TQ
    has_sink = sink is not None
    n_i = s_len // TQ

    def body(*refs):
        refs = list(refs)
        q_ref, k_ref, v_ref, b_ref = refs[:4]
        refs = refs[4:]
        s_ref = refs.pop(0) if has_sink else None
        do_ref, dq_ref, dk_ref, dv_ref = refs[:4]
        refs = refs[4:]
        dbias_ref = refs.pop(0) if want_dbias else None
        dsink_ref = refs.pop(0) if has_sink else None
        dk_acc, dv_acc = refs
        b = pl.program_id(1)
        i = pl.program_id(2)
        qs = pl.multiple_of(i * TQ, TQ)
        qq = q_ref[...].reshape(rows, dh)
        dd = do_ref[...].reshape(rows, dh)
        kk = k_ref[pl.ds(qs, band), :]
        vv = v_ref[pl.ds(qs, band), :]
        p, m, inv = _band_softmax(qq, kk, b_ref, s_ref, qs, pad)
        pn = p * inv
        dp = _dot_nt(dd, vv)
        delta = jnp.sum(pn * dp, axis=-1, keepdims=True)
        ds = pn * (dp - delta)
        dsb = ds.astype(BF16)
        dq_ref[...] = (_dot_nn(dsb, kk) * SCALE).reshape(g, TQ, dh).astype(BF16)

        @pl.when(i == 0)
        def _():
            dk_acc[...] = jnp.zeros_like(dk_acc)
            dv_acc[...] = jnp.zeros_like(dv_acc)

        dk_acc[pl.ds(qs, band), :] += _dot_tn(dsb, qq) * SCALE
        dv_acc[pl.ds(qs, band), :] += _dot_tn(pn.astype(BF16), dd)

        @pl.when(i == n_i - 1)
        def _():
            dk_ref[...] = dk_acc[pad:, :].astype(BF16)
            dv_ref[...] = dv_acc[pad:, :].astype(BF16)

        first = jnp.logical_and(b == 0, i == 0)
        if want_dbias:
            @pl.when(first)
            def _():
                dbias_ref[...] = jnp.zeros_like(dbias_ref)

            dbias_ref[...] += ds
        if has_sink:
            @pl.when(first)
            def _():
                dsink_ref[...] = jnp.zeros_like(dsink_ref)

            dsk = -(jnp.exp(s_ref[...] - m) * inv) * delta
            row = lax.broadcasted_iota(jnp.int32, (8, 128), 0)
            upd = jnp.zeros((8, 128), F32)
            for gi in range(g):
                tot = jnp.sum(dsk[gi * TQ:(gi + 1) * TQ, :], axis=0, keepdims=True)
                upd = upd + jnp.where(row == gi, tot, 0.0)
            dsink_ref[...] += upd

    qspec = pl.BlockSpec((None, None, g, TQ, dh), lambda hh, b, i: (b, hh, 0, i, 0))
    kspec = pl.BlockSpec((None, None, pad + s_len, dh), lambda hh, b, i: (b, hh, 0, 0))
    in_specs = [qspec, kspec, kspec, pl.BlockSpec((None, rows, band), lambda hh, b, i: (hh, 0, 0))]
    args = [q, k, v, bias]
    if has_sink:
        in_specs.append(pl.BlockSpec((None, rows, 1), lambda hh, b, i: (hh, 0, 0)))
        args.append(sink)
    in_specs.append(qspec)
    args.append(do)
    dkspec = pl.BlockSpec((None, None, s_len, dh), lambda hh, b, i: (b, hh, 0, 0))
    out_specs = [qspec, dkspec, dkspec]
    out_shape = [jax.ShapeDtypeStruct(q.shape, BF16), jax.ShapeDtypeStruct((bsz, h, s_len, dh), BF16),
                 jax.ShapeDtypeStruct((bsz, h, s_len, dh), BF16)]
    if want_dbias:
        out_specs.append(pl.BlockSpec((None, rows, band), lambda hh, b, i: (hh, 0, 0)))
        out_shape.append(jax.ShapeDtypeStruct((h, rows, band), F32))
    if has_sink:
        out_specs.append(pl.BlockSpec((None, 8, 128), lambda hh, b, i: (hh, 0, 0)))
        out_shape.append(jax.ShapeDtypeStruct((h, 8, 128), F32))
    return pl.pallas_call(
        body,
        name=name,
        grid=(h, bsz, n_i),
        in_specs=in_specs,
        out_specs=out_specs,
        out_shape=out_shape,
        scratch_shapes=[pltpu.VMEM((pad + s_len, dh), F32), pltpu.VMEM((pad + s_len, dh), F32)],
        compiler_params=_cparams(("arbitrary", "arbitrary", "arbitrary")),
    )(*args)


REL_COLS = 3 * 128


def _relbias_grad(dbias):
    h, rows, band = dbias.shape
    off = band - REL_COLS

    def body(d_ref, o_ref):
        x = d_ref[...]
        r = lax.broadcasted_iota(jnp.int32, x.shape, 0)
        c = lax.broadcasted_iota(jnp.int32, x.shape, 1) - r
        x = jnp.where(jnp.logical_and(c >= 1, c < REL_TABLE), x, 0.0)
        for bit in range(8):
            sh = 1 << bit
            x = jnp.where((r & sh) != 0, pltpu.roll(x, REL_COLS - sh, 1), x)
        diag = jnp.sum(x, axis=0, keepdims=True)
        lane = lax.broadcasted_iota(jnp.int32, diag.shape, 1)
        diag = jnp.where(jnp.logical_and(lane >= 1, lane < REL_TABLE), diag, 0.0)
        rest = -jnp.sum(diag, axis=1, keepdims=True)
        o_ref[...] = jnp.broadcast_to(jnp.where(lane == 0, rest, diag), o_ref.shape)

    return pl.pallas_call(
        body,
        name="relbias_grad",
        grid=(h,),
        in_specs=[pl.BlockSpec((None, rows, REL_COLS), lambda hh: (hh, 0, off // REL_COLS))],
        out_specs=pl.BlockSpec((None, 8, REL_COLS), lambda hh: (hh, 0, 0)),
        out_shape=jax.ShapeDtypeStruct((h, 8, REL_COLS), F32),
        compiler_params=_cparams(("parallel",)),
    )(dbias)


def _mix_out_fwd(x, oa, ob, gates, proj_t, wout):
    t = x.shape[0]

    def body(x_ref, oa_ref, ob_ref, gt_ref, pt_ref, wo_ref, y_ref, ya_ref, yb_ref, mg_ref):
        ya = _dot_nt(oa_ref[...], pt_ref[:, 0:A_WIDTH])
        yb = _dot_nt(ob_ref[...], pt_ref[:, A_WIDTH:A_WIDTH + B_Q_WIDTH])
        ya_ref[...] = ya.astype(BF16)
        yb_ref[...] = yb.astype(BF16)
        mg = jax.nn.sigmoid(gt_ref[:, 0:D_MODEL]) * ya + jax.nn.sigmoid(gt_ref[:, D_MODEL:2 * D_MODEL]) * yb
        mgb = mg.astype(BF16)
        mg_ref[...] = mgb
        y_ref[...] = x_ref[...] + _dot_nn(mgb, wo_ref[...])

    return pl.pallas_call(
        body,
        name="mix_out_fwd",
        grid=(t // TM,),
        in_specs=[_rows(TM, D_MODEL), _rows(TM, A_WIDTH), _rows(TM, B_Q_WIDTH), _rows(TM, 2 * D_MODEL),
                  _resident((D_MODEL, A_WIDTH + B_Q_WIDTH)), _resident((D_MODEL, D_MODEL))],
        out_specs=[_rows(TM, D_MODEL), _rows(TM, D_MODEL), _rows(TM, D_MODEL), _rows(TM, D_MODEL)],
        out_shape=[jax.ShapeDtypeStruct((t, D_MODEL), F32), jax.ShapeDtypeStruct((t, D_MODEL), BF16),
                   jax.ShapeDtypeStruct((t, D_MODEL), BF16), jax.ShapeDtypeStruct((t, D_MODEL), BF16)],
        compiler_params=_cparams(("parallel",)),
    )(x, oa, ob, gates, proj_t, wout)


def _mix_out_bwd(d, gates, ya, yb, proj_t, wout):
    t = d.shape[0]

    def body(d_ref, gt_ref, ya_ref, yb_ref, pt_ref, wo_ref, db_ref, dya_ref, dyb_ref, doa_ref, dob_ref, dgt_ref):
        db = d_ref[...].astype(BF16)
        db_ref[...] = db
        dmg = _dot_nt(db, wo_ref[...])
        sa = jax.nn.sigmoid(gt_ref[:, 0:D_MODEL])
        sb = jax.nn.sigmoid(gt_ref[:, D_MODEL:2 * D_MODEL])
        dya = (dmg * sa).astype(BF16)
        dyb = (dmg * sb).astype(BF16)
        dya_ref[...] = dya
        dyb_ref[...] = dyb
        dgt_ref[:, 0:D_MODEL] = (dmg * ya_ref[...].astype(F32) * (sa * (1.0 - sa))).astype(BF16)
        dgt_ref[:, D_MODEL:2 * D_MODEL] = (dmg * yb_ref[...].astype(F32) * (sb * (1.0 - sb))).astype(BF16)
        doa_ref[...] = _dot_nn(dya, pt_ref[:, 0:A_WIDTH]).astype(BF16)
        dob_ref[...] = _dot_nn(dyb, pt_ref[:, A_WIDTH:A_WIDTH + B_Q_WIDTH]).astype(BF16)

    return pl.pallas_call(
        body,
        name="mix_out_bwd",
        grid=(t // TM,),
        in_specs=[_rows(TM, D_MODEL), _rows(TM, 2 * D_MODEL), _rows(TM, D_MODEL), _rows(TM, D_MODEL),
                  _resident((D_MODEL, A_WIDTH + B_Q_WIDTH)), _resident((D_MODEL, D_MODEL))],
        out_specs=[_rows(TM, D_MODEL), _rows(TM, D_MODEL), _rows(TM, D_MODEL), _rows(TM, A_WIDTH),
                   _rows(TM, B_Q_WIDTH), _rows(TM, 2 * D_MODEL)],
        out_shape=[jax.ShapeDtypeStruct((t, D_MODEL), BF16), jax.ShapeDtypeStruct((t, D_MODEL), BF16),
                   jax.ShapeDtypeStruct((t, D_MODEL), BF16), jax.ShapeDtypeStruct((t, A_WIDTH), BF16),
                   jax.ShapeDtypeStruct((t, B_Q_WIDTH), BF16), jax.ShapeDtypeStruct((t, 2 * D_MODEL), BF16)],
        compiler_params=_cparams(("parallel",)),
    )(d, gates, ya, yb, proj_t, wout)


def _loss_head(x, gamma, target):
    t = x.shape[0]

    def body(x_ref, gam_ref, t_ref, dx_ref, dgam_ref, loss_ref):
        xh, r = _rms(x_ref[...])
        gam = gam_ref[...]
        e = xh * gam - t_ref[...]
        dy = e * (1.0 / D_MODEL)
        dxn, dgam = _rms_bwd(dy, xh, r, gam)
        dx_ref[...] = dxn

        @pl.when(pl.program_id(0) == 0)
        def _():
            dgam_ref[...] = jnp.zeros_like(dgam_ref)
            loss_ref[...] = jnp.zeros_like(loss_ref)

        dgam_ref[...] += dgam
        loss_ref[...] += _colsum8(e * e) * (0.5 / D_MODEL)

    return pl.pallas_call(
        body,
        name="loss_head",
        grid=(t // TM,),
        in_specs=[_rows(TM, D_MODEL), _resident((1, D_MODEL)), _rows(TM, D_MODEL)],
        out_specs=[_rows(TM, D_MODEL), pl.BlockSpec((8, D_MODEL), lambda i: (0, 0)),
                   pl.BlockSpec((8, D_MODEL), lambda i: (0, 0))],
        out_shape=[jax.ShapeDtypeStruct((t, D_MODEL), F32), jax.ShapeDtypeStruct((8, D_MODEL), F32),
                   jax.ShapeDtypeStruct((8, D_MODEL), F32)],
        compiler_params=_cparams(("arbitrary",)),
    )(x, gamma, target)


def _hbm():
    return pl.BlockSpec(memory_space=pltpu.HBM)


def _place():
    x, y, c = lax.axis_index("x"), lax.axis_index("y"), lax.axis_index("c")
    chips = [(1 - x, y), (x, 1 - y), (1 - x, 1 - y)]
    return x, y, c, chips


def _all_gather(shards):
    n = len(shards)
    per = 7

    def body(*refs):
        ins, outs = refs[:n], refs[n:2 * n]
        send_sems, recv_sems, local_sems = refs[2 * n:]
        x, y, c, chips = _place()
        me, sibling = (x, y, c), (x, y, 1 - c)

        def rows(k, p):
            r = ins[k].shape[0]
            return outs[k].at[pl.ds((4 * p[0] + 2 * p[1] + p[2]) * r, r), :]

        def copy(k, slot, block, to, src=None):
            return pltpu.make_async_remote_copy(
                src_ref=rows(k, block) if src is None else src, dst_ref=rows(k, block),
                send_sem=send_sems.at[k * per + slot], recv_sem=recv_sems.at[k * per + slot],
                device_id=to, device_id_type=MESH)

        mine = [pltpu.make_async_copy(ins[k], rows(k, me), local_sems.at[k]) for k in range(n)]
        first = []
        for k in range(n):
            mine[k].start()
            first.append(copy(k, 0, me, sibling, src=ins[k]))
            first += [copy(k, 1 + j, me, (*chip, c), src=ins[k]) for j, chip in enumerate(chips)]
        for cp in first:
            cp.start()
        passed = []
        for j, chip in enumerate(chips):
            for k in range(n):
                copy(k, 1 + j, (*chip, c), me).wait_recv()
                fwd = copy(k, 4 + j, (*chip, c), sibling)
                fwd.start()
                passed.append(fwd)
        for k in range(n):
            copy(k, 0, sibling, me).wait_recv()
            for j, chip in enumerate(chips):
                copy(k, 4 + j, (*chip, 1 - c), me).wait_recv()
        for cp in first + passed:
            cp.wait_send()
        for cp in mine:
            cp.wait()

    return pl.pallas_call(
        body,
        name="all_gather_weights",
        in_specs=[_hbm()] * n,
        out_specs=[_hbm()] * n,
        out_shape=[jax.ShapeDtypeStruct((N_DEV * s.shape[0], s.shape[1]), s.dtype) for s in shards],
        scratch_shapes=[pltpu.SemaphoreType.DMA((n * per,)), pltpu.SemaphoreType.DMA((n * per,)),
                        pltpu.SemaphoreType.DMA((n,))],
    )(*shards)


def _pair_exchange(grads):
    n = len(grads)

    def body(*refs):
        ins, outs = refs[:n], refs[n:2 * n]
        send_sems, recv_sems = refs[2 * n:]
        x, y, c, _ = _place()
        copies = []
        for k in range(n):
            r = ins[k].shape[0] // N_DEV
            for q in range(N_CHIP):
                copies.append(pltpu.make_async_remote_copy(
                    src_ref=ins[k].at[pl.ds((2 * q + 1 - c) * r, r), :], dst_ref=outs[k].at[pl.ds(q * r, r), :],
                    send_sem=send_sems.at[k * N_CHIP + q], recv_sem=recv_sems.at[k * N_CHIP + q],
                    device_id=(x, y, 1 - c), device_id_type=MESH))
        for cp in copies:
            cp.start()
        for cp in copies:
            cp.wait_recv()
        for cp in copies:
            cp.wait_send()

    return pl.pallas_call(
        body,
        name="grad_pair_exchange",
        in_specs=[_hbm()] * n,
        out_specs=[_hbm()] * n,
        out_shape=[jax.ShapeDtypeStruct((g.shape[0] // 2, g.shape[1]), g.dtype) for g in grads],
        scratch_shapes=[pltpu.SemaphoreType.DMA((n * N_CHIP,)), pltpu.SemaphoreType.DMA((n * N_CHIP,))],
    )(*grads)


def _chip_exchange(psums):
    n = len(psums)

    def body(*refs):
        ins, outs = refs[:n], refs[n:2 * n]
        send_sems, recv_sems = refs[2 * n:]
        _, _, c, chips = _place()
        copies = []
        for k in range(n):
            r = ins[k].shape[0] // N_CHIP
            for j, chip in enumerate(chips):
                copies.append(pltpu.make_async_remote_copy(
                    src_ref=ins[k].at[pl.ds((2 * chip[0] + chip[1]) * r, r), :], dst_ref=outs[k].at[pl.ds(j * r, r), :],
                    send_sem=send_sems.at[k * 3 + j], recv_sem=recv_sems.at[k * 3 + j],
                    device_id=(*chip, c), device_id_type=MESH))
        for cp in copies:
            cp.start()
        for cp in copies:
            cp.wait_recv()
        for cp in copies:
            cp.wait_send()

    return pl.pallas_call(
        body,
        name="grad_chip_exchange",
        in_specs=[_hbm()] * n,
        out_specs=[_hbm()] * n,
        out_shape=[jax.ShapeDtypeStruct((3 * p.shape[0] // N_CHIP, p.shape[1]), p.dtype) for p in psums],
        scratch_shapes=[pltpu.SemaphoreType.DMA((n * 3,)), pltpu.SemaphoreType.DMA((n * 3,))],
    )(*psums)


def _pair_sum(core, grads, recvd, name):
    n = len(grads)
    r = grads[0].shape[0] // N_DEV
    cdim = grads[0].shape[1]
    tr = r // 2 if r % 32 == 0 else r
    nt = r // tr

    def body(core_ref, *refs):
        del core_ref
        for k in range(n):
            refs[2 * n + k][...] = (refs[k][...].astype(F32) + refs[n + k][...].astype(F32)).astype(BF16)

    gspec = pl.BlockSpec((tr, cdim), lambda q, i, core_ref: ((2 * q + core_ref[0]) * nt + i, 0))
    rspec = pl.BlockSpec((tr, cdim), lambda q, i, core_ref: (q * nt + i, 0))
    return pl.pallas_call(
        body,
        name=name,
        grid_spec=pltpu.PrefetchScalarGridSpec(
            num_scalar_prefetch=1, grid=(N_CHIP, nt), in_specs=[gspec] * n + [rspec] * n, out_specs=[rspec] * n),
        out_shape=[jax.ShapeDtypeStruct((N_CHIP * r, cdim), BF16) for _ in range(n)],
        compiler_params=_cparams(("parallel", "parallel")),
    )(core, *grads, *recvd)


def _final_sum(chip, psums, recvd, name):
    n = len(psums)
    r = psums[0].shape[0] // N_CHIP
    cdim = psums[0].shape[1]
    tr = r // 2 if r % 32 == 0 else r
    nt = r // tr

    def body(chip_ref, *refs):
        del chip_ref
        for k in range(n):
            got = refs[n + k]
            tot = refs[k][...].astype(F32) + got[0].astype(F32)
            tot = tot + got[1].astype(F32)
            tot = tot + got[2].astype(F32)
            refs[2 * n + k][...] = tot

    pspec = pl.BlockSpec((tr, cdim), lambda i, chip_ref: (chip_ref[0] * nt + i, 0))
    rspec = pl.BlockSpec((3, tr, cdim), lambda i, chip_ref: (0, i, 0))
    ospec = pl.BlockSpec((tr, cdim), lambda i, chip_ref: (i, 0))
    return pl.pallas_call(
        body,
        name=name,
        grid_spec=pltpu.PrefetchScalarGridSpec(
            num_scalar_prefetch=1, grid=(nt,), in_specs=[pspec] * n + [rspec] * n, out_specs=[ospec] * n),
        out_shape=[jax.ShapeDtypeStruct((r, cdim), F32) for _ in range(n)],
        compiler_params=_cparams(("parallel",)),
    )(chip, *psums, *[g.reshape(3, r, cdim) for g in recvd])


SMALL_ROWS = 16


def _all_reduce_small(part):
    def body(p_ref, o_ref, buf, send_sems, recv_sems):
        x, y, c, _ = _place()
        me = 4 * x + 2 * y + c
        buf[me] = p_ref[...]
        copies = []
        for d in range(1, N_DEV):
            peer = me ^ d
            copies.append(pltpu.make_async_remote_copy(
                src_ref=p_ref, dst_ref=buf.at[me], send_sem=send_sems.at[d - 1], recv_sem=recv_sems.at[d - 1],
                device_id=(peer // 4, (peer // 2) % 2, peer % 2), device_id_type=MESH))
        for cp in copies:
            cp.start()
        for cp in copies:
            cp.wait_recv()
        for cp in copies:
            cp.wait_send()
        tot = buf[0]
        for d in range(1, N_DEV):
            tot = tot + buf[d]
        o_ref[...] = tot

    return pl.pallas_call(
        body,
        name="all_reduce_small",
        in_specs=[pl.BlockSpec(memory_space=pltpu.VMEM)],
        out_specs=pl.BlockSpec(memory_space=pltpu.VMEM),
        out_shape=jax.ShapeDtypeStruct(part.shape, F32),
        scratch_shapes=[pltpu.VMEM((N_DEV,) + part.shape, F32), pltpu.SemaphoreType.DMA((N_DEV - 1,)),
                        pltpu.SemaphoreType.DMA((N_DEV - 1,))],
    )(part)


def _adamw(ws, gs, ms, vs, name):
    n = len(ws)
    r, cdim = ws[0].shape
    tr = r
    for cand in (512, 256, 128, 176, 64):
        if r % cand == 0 and r > cand:
            tr = cand
            break
    c1 = 1.0 - ADAM_B1 ** ADAM_STEP
    c2 = 1.0 - ADAM_B2 ** ADAM_STEP

    def body(*refs):
        for k in range(n):
            w, g, m, v = (refs[j * n + k][...] for j in range(4))
            m2 = ADAM_B1 * m + (1.0 - ADAM_B1) * g
            v2 = ADAM_B2 * v + (1.0 - ADAM_B2) * (g * g)
            delta = -ADAM_LR * ((m2 / c1) / (jnp.sqrt(v2 / c2) + ADAM_EPS) + ADAM_WD * w)
            refs[4 * n + k][...] = delta
            refs[5 * n + k][...] = m2
            refs[6 * n + k][...] = v2

    spec = pl.BlockSpec((tr, cdim), lambda i: (i, 0))
    outs = pl.pallas_call(
        body,
        name=name,
        grid=(r // tr,),
        in_specs=[spec] * (4 * n),
        out_specs=[spec] * (3 * n),
        out_shape=[jax.ShapeDtypeStruct((r, cdim), F32)] * (3 * n),
        compiler_params=_cparams(("parallel",)),
    )(*ws, *gs, *ms, *vs)
    return outs[:n], outs[n:2 * n], outs[2 * n:]


def _band_allowed(n_prev):
    pad = n_prev * CHUNK
    qc = (np.arange(TQ)[:, None] + pad) // CHUNK
    kc = np.arange(TQ + pad)[None, :] // CHUNK
    return (kc <= qc) & (kc >= qc - n_prev)


def _rel_index_a():
    pad = A_PREV * CHUNK
    rel = np.arange(TQ)[:, None] - np.arange(TQ + pad)[None, :] + pad
    return np.clip(rel, -(CHUNK - 1), MAX_REL) + (CHUNK - 1)


def _bias_b():
    pad = B_PREV * CHUNK
    slopes = np.array([2.0 ** (-8.0 * (i + 1) / B_Q_HEADS) for i in range(B_Q_HEADS)], dtype=np.float32)
    dist = np.abs(np.arange(TQ)[:, None] - np.arange(TQ + pad)[None, :] + pad).astype(np.float32)
    bias = -slopes.reshape(B_KV_HEADS, B_GROUP, 1, 1) * dist[None, None]
    bias = np.where(_band_allowed(B_PREV)[None, None], bias, np.float32(NEG_INF)).astype(np.float32)
    return bias.reshape(B_KV_HEADS, B_GROUP * TQ, TQ + pad)


def _heads_major(a, bsz, s_len, lead):
    nl = len(lead)
    a = a.reshape((bsz, s_len) + tuple(lead) + (D_HEAD,))
    perm = (0,) + tuple(range(2, 2 + nl)) + (1, 2 + nl)
    return a.transpose(perm)


def _tokens_major(a):
    nl = a.ndim - 3
    perm = (0, 1 + nl) + tuple(range(1, 1 + nl)) + (2 + nl,)
    a = a.transpose(perm)
    return a.reshape(a.shape[0] * a.shape[1], -1)


def _pad_front(a, pad):
    return jnp.pad(a, ((0, 0), (0, 0), (pad, 0), (0, 0)))


def kernel(x, ffn1_norm, ffn1_w_gate, ffn1_w_up, ffn1_w_down, mix_norm, w_in, rel_bias, sinks, w_proj_a, w_proj_b, w_out, ffn2_norm, ffn2_w_gate, ffn2_w_up, ffn2_w_down, final_norm, loss_target, m_ffn1_norm, m_ffn1_w_gate, m_ffn1_w_up, m_ffn1_w_down, m_mix_norm, m_w_in, m_rel_bias, m_sinks, m_w_proj_a, m_w_proj_b, m_w_out, m_ffn2_norm, m_ffn2_w_gate, m_ffn2_w_up, m_ffn2_w_down, m_final_norm, v_ffn1_norm, v_ffn1_w_gate, v_ffn1_w_up, v_ffn1_w_down, v_mix_norm, v_w_in, v_rel_bias, v_sinks, v_w_proj_a, v_w_proj_b, v_w_out, v_ffn2_norm, v_ffn2_w_gate, v_ffn2_w_up, v_ffn2_w_down, v_final_norm):
    bsz, s_len, _ = x.shape
    t = bsz * s_len
    core = lax.axis_index("c").astype(jnp.int32).reshape(1)
    chip = (2 * lax.axis_index("x") + lax.axis_index("y")).astype(jnp.int32).reshape(1)

    def row_form(w):
        return w.astype(BF16).T

    shards = [row_form(ffn1_w_gate), row_form(ffn1_w_up), ffn1_w_down.astype(BF16), row_form(w_in),
              jnp.concatenate([row_form(w_proj_a), row_form(w_proj_b)], axis=1), w_out.astype(BF16),
              row_form(ffn2_w_gate), row_form(ffn2_w_up), ffn2_w_down.astype(BF16)]
    wg1, wu1, wd1, win_t, proj_t, wout, wg2, wu2, wd2 = _all_gather(shards)

    x0 = x.reshape(t, D_MODEL)
    tgt = loss_target.reshape(t, D_MODEL)
    gam1, gam2, gam3, gam4 = (g.reshape(1, D_MODEL) for g in (ffn1_norm, mix_norm, ffn2_norm, final_norm))

    h1, g1, u1, a1, x1 = _ffn_fwd(x0, gam1, wg1, wu1, wd1, "ffn1_fwd")
    h2, qkv_a, qkv_b, gates = _proj_fwd(x1, gam2, win_t)

    pad_a, pad_b = A_PREV * CHUNK, B_PREV * CHUNK
    qa = _heads_major(qkv_a[:, 0:A_WIDTH], bsz, s_len, (A_HEADS, 1))
    ka = _pad_front(_heads_major(qkv_a[:, A_WIDTH:2 * A_WIDTH], bsz, s_len, (A_HEADS,)), pad_a)
    va = _pad_front(_heads_major(qkv_a[:, 2 * A_WIDTH:3 * A_WIDTH], bsz, s_len, (A_HEADS,)), pad_a)
    qb = _heads_major(qkv_b[:, 0:B_Q_WIDTH], bsz, s_len, (B_KV_HEADS, B_GROUP))
    kb = _pad_front(_heads_major(qkv_b[:, B_Q_WIDTH:B_Q_WIDTH + B_KV_WIDTH], bsz, s_len, (B_KV_HEADS,)), pad_b)
    vb = _pad_front(_heads_major(qkv_b[:, B_Q_WIDTH + B_KV_WIDTH:QKV_B], bsz, s_len, (B_KV_HEADS,)), pad_b)

    bias_a = jnp.where(_band_allowed(A_PREV)[None], rel_bias[:, _rel_index_a()], NEG_INF).astype(F32)
    bias_b = jnp.asarray(_bias_b())
    sink_rows = jnp.repeat(sinks.reshape(B_KV_HEADS, B_GROUP), TQ, axis=1).reshape(B_KV_HEADS, B_GROUP * TQ, 1)

    oa_h = _attn_fwd(qa, ka, va, bias_a, None, A_PREV, "attn_a_fwd")
    ob_h = _attn_fwd(qb, kb, vb, bias_b, sink_rows, B_PREV, "attn_b_fwd")
    oa = _tokens_major(oa_h)
    ob = _tokens_major(ob_h)
    x2, ya, yb, mg = _mix_out_fwd(x1, oa, ob, gates, proj_t, wout)
    h3, g2, u2, a2, x3 = _ffn_fwd(x2, gam3, wg2, wu2, wd2, "ffn2_fwd")

    dx3, dgam4, loss_part = _loss_head(x3, gam4, tgt)

    dx2, dg2, du2, db2, dgam3 = _ffn_bwd(dx3, x2, gam3, g2, u2, wg2, wu2, wd2, "ffn2_bwd")
    half_f = D_FF // 2
    gw_g2 = _mm_tn(dg2, h3, half_f, "grad_ffn2_gate")
    gw_u2 = _mm_tn(du2, h3, half_f, "grad_ffn2_up")
    gw_d2 = _mm_tn(a2, db2, half_f, "grad_ffn2_down")

    dxb, dya, dyb, doa, dob, dgates = _mix_out_bwd(dx2, gates, ya, yb, proj_t, wout)
    gw_out = _mm_tn(mg, dxb, D_MODEL, "grad_w_out")
    gw_proj = jnp.concatenate([_mm_tn(dya, oa, D_MODEL, "grad_proj_a"), _mm_tn(dyb, ob, D_MODEL, "grad_proj_b")], axis=1)

    doa_h = _heads_major(doa, bsz, s_len, (A_HEADS, 1))
    dob_h = _heads_major(dob, bsz, s_len, (B_KV_HEADS, B_GROUP))
    dqa_h, dka_h, dva_h, dbias_a = _attn_bwd(qa, ka, va, bias_a, None, doa_h, A_PREV, True, "attn_a_bwd")
    dqb_h, dkb_h, dvb_h, dsink = _attn_bwd(qb, kb, vb, bias_b, sink_rows, dob_h, B_PREV, False, "attn_b_bwd")
    drel_lanes = _relbias_grad(dbias_a)
    dqkv_a = jnp.concatenate([_tokens_major(dqa_h), _tokens_major(dka_h), _tokens_major(dva_h)], axis=1)
    dqkv_b = jnp.concatenate([_tokens_major(dqb_h), _tokens_major(dkb_h), _tokens_major(dvb_h)], axis=1)

    dx1, dgam2 = _proj_bwd(dx2, x1, gam2, dqkv_a, dqkv_b, dgates, win_t)
    dproj = jnp.concatenate([dqkv_a, dqkv_b, dgates], axis=1)
    gw_in = _mm_tn(dproj, h2, IN_WIDTH // 2, "grad_w_in")

    dx0, dg1, du1, db1, dgam1 = _ffn_bwd(dx1, x0, gam1, g1, u1, wg1, wu1, wd1, "ffn1_bwd")
    gw_g1 = _mm_tn(dg1, h1, half_f, "grad_ffn1_gate")
    gw_u1 = _mm_tn(du1, h1, half_f, "grad_ffn1_up")
    gw_d1 = _mm_tn(a1, db1, half_f, "grad_ffn1_down")

    full = [gw_g1, gw_u1, gw_d1, gw_g2, gw_u2, gw_d2, gw_in, gw_proj, gw_out]
    from_sibling = _pair_exchange(full)
    pair = (_pair_sum(core, full[0:6], from_sibling[0:6], "pair_sum_ffn")
            + _pair_sum(core, full[6:7], from_sibling[6:7], "pair_sum_w_in")
            + _pair_sum(core, full[7:9], from_sibling[7:9], "pair_sum_mix"))
    from_chips = _chip_exchange(pair)
    gsum = (_final_sum(chip, pair[0:6], from_chips[0:6], "grad_sum_ffn")
            + _final_sum(chip, pair[6:7], from_chips[6:7], "grad_sum_w_in")
            + _final_sum(chip, pair[7:9], from_chips[7:9], "grad_sum_mix"))
    g_g1, g_u1, g_d1, g_g2, g_u2, g_d2, g_in, g_proj, g_out = gsum
    grads = {
        "ffn1_w_gate": g_g1.T, "ffn1_w_up": g_u1.T, "ffn1_w_down": g_d1, "w_in": g_in.T,
        "w_proj_a": g_proj[:, 0:A_WIDTH].T, "w_proj_b": g_proj[:, A_WIDTH:].T, "w_out": g_out,
        "ffn2_w_gate": g_g2.T, "ffn2_w_up": g_u2.T, "ffn2_w_down": g_d2,
    }

    def row_of(v):
        return jnp.pad(v.reshape(1, -1), ((0, 0), (0, D_MODEL - v.size)))

    def table_rows(v):
        return jnp.pad(v, ((0, 0), (0, D_MODEL - REL_TABLE)))

    drel_local = jnp.flip(drel_lanes[:, 0, 0:REL_TABLE], axis=1)
    small_part = jnp.concatenate(
        [jnp.sum(dgam1, axis=0, keepdims=True), jnp.sum(dgam2, axis=0, keepdims=True),
         jnp.sum(dgam3, axis=0, keepdims=True), jnp.sum(dgam4, axis=0, keepdims=True),
         row_of(jnp.sum(loss_part)), row_of(dsink[:, 0:B_GROUP, 0]), jnp.zeros((2, D_MODEL), F32),
         table_rows(drel_local)], axis=0)
    small = _all_reduce_small(small_part)
    loss = small[4, 0]

    def pack(n1, n2, n3, n4, sk, tb):
        return jnp.concatenate([n1.reshape(1, -1), n2.reshape(1, -1), n3.reshape(1, -1), n4.reshape(1, -1),
                                jnp.zeros((1, D_MODEL), F32), row_of(sk), jnp.zeros((2, D_MODEL), F32), table_rows(tb)],
                               axis=0)

    live = np.zeros((SMALL_ROWS, D_MODEL), np.float32)
    live[0:4] = 1.0
    live[5, 0:B_Q_HEADS] = 1.0
    live[8:16, 0:REL_TABLE] = 1.0
    small_g = small * jnp.asarray(live)
    sw = pack(ffn1_norm, mix_norm, ffn2_norm, final_norm, sinks, rel_bias)
    sm = pack(m_ffn1_norm, m_mix_norm, m_ffn2_norm, m_final_norm, m_sinks, m_rel_bias)
    sv = pack(v_ffn1_norm, v_mix_norm, v_ffn2_norm, v_final_norm, v_sinks, v_rel_bias)
    (sd,), (snm,), (snv,) = _adamw([sw], [small_g], [sm], [sv], "adamw_small")

    def unpack(p):
        return {"ffn1_norm": p[0], "mix_norm": p[1], "ffn2_norm": p[2], "final_norm": p[3],
                "sinks": p[5, 0:B_Q_HEADS], "rel_bias": p[8:16, 0:REL_TABLE]}

    grads.update(unpack(small_g))
    delta, new_m, new_v = unpack(sd), unpack(snm), unpack(snv)

    wmv = {
        "ffn1_w_gate": (ffn1_w_gate, m_ffn1_w_gate, v_ffn1_w_gate), "ffn1_w_up": (ffn1_w_up, m_ffn1_w_up, v_ffn1_w_up),
        "ffn1_w_down": (ffn1_w_down, m_ffn1_w_down, v_ffn1_w_down), "w_in": (w_in, m_w_in, v_w_in),
        "w_proj_a": (w_proj_a, m_w_proj_a, v_w_proj_a), "w_proj_b": (w_proj_b, m_w_proj_b, v_w_proj_b),
        "w_out": (w_out, m_w_out, v_w_out),
        "ffn2_w_gate": (ffn2_w_gate, m_ffn2_w_gate, v_ffn2_w_gate), "ffn2_w_up": (ffn2_w_up, m_ffn2_w_up, v_ffn2_w_up),
        "ffn2_w_down": (ffn2_w_down, m_ffn2_w_down, v_ffn2_w_down),
    }
    groups = [("adamw_ffn_up", ["ffn1_w_gate", "ffn1_w_up", "ffn2_w_gate", "ffn2_w_up"]),
              ("adamw_ffn_down", ["ffn1_w_down", "ffn2_w_down"]), ("adamw_w_in", ["w_in"]),
              ("adamw_proj", ["w_proj_a", "w_proj_b"]), ("adamw_w_out", ["w_out"])]
    for gname, names in groups:
        ds_, ms_, vs_ = _adamw([wmv[n][0] for n in names], [grads[n] for n in names], [wmv[n][1] for n in names],
                               [wmv[n][2] for n in names], gname)
        for n, d_, m_, v_ in zip(names, ds_, ms_, vs_):
            delta[n], new_m[n], new_v[n] = d_, m_, v_

    order = ["ffn1_norm", "ffn1_w_gate", "ffn1_w_up", "ffn1_w_down", "mix_norm", "w_in", "rel_bias", "sinks",
             "w_proj_a", "w_proj_b", "w_out", "ffn2_norm", "ffn2_w_gate", "ffn2_w_up", "ffn2_w_down", "final_norm"]
    grad_x = dx0.reshape(bsz, s_len, D_MODEL)
    return (loss, grad_x, *[grads[n] for n in order], *[delta[n] for n in order], *[new_m[n] for n in order],
            *[new_v[n] for n in order])
```

```python
import numpy as np
import jax
import jax.numpy as jnp
from jax import lax
from jax.experimental import pallas as pl
from jax.experimental.pallas import tpu as pltpu

F32 = jnp.float32
BF16 = jnp.bfloat16

D_MODEL = 1024
D_FF = 2816
CHUNK = 64
D_HEAD = 64
A_HEADS = 8
A_PREV = 8
MAX_REL = 128
B_Q_HEADS = 8
B_KV_HEADS = 2
B_GROUP = B_Q_HEADS // B_KV_HEADS
B_PREV = 2
REL_TABLE = (CHUNK - 1) + MAX_REL + 1
A_WIDTH = A_HEADS * D_HEAD
B_Q_WIDTH = B_Q_HEADS * D_HEAD
B_KV_WIDTH = B_KV_HEADS * D_HEAD
QKV_A = 3 * A_WIDTH
QKV_B = B_Q_WIDTH + 2 * B_KV_WIDTH
IN_WIDTH = QKV_A + QKV_B + 2 * D_MODEL
EPS = 1e-6
NEG_INF = -1e30
SCALE = 1.0 / 8.0

ADAM_LR = 0.001
ADAM_B1 = 0.9
ADAM_B2 = 0.999
ADAM_EPS = 1e-08
ADAM_WD = 0.01
ADAM_STEP = 10

N_DEV = 8
N_CHIP = 4
MESH = pl.DeviceIdType.MESH

LANES = 128
TQ = 256
TM = 256
FC = 256
VMEM_LIMIT = 56 << 20


def _cparams(sem, vmem=VMEM_LIMIT):
    return pltpu.CompilerParams(dimension_semantics=sem, vmem_limit_bytes=vmem)


def _dot_nt(a, b):
    return lax.dot_general(a, b, (((1,), (1,)), ((), ())), preferred_element_type=F32)


def _dot_nn(a, b):
    return lax.dot_general(a, b, (((1,), (0,)), ((), ())), preferred_element_type=F32)


def _dot_tn(a, b):
    return lax.dot_general(a, b, (((0,), (0,)), ((), ())), preferred_element_type=F32)


def _resident(shape):
    nd = len(shape)
    return pl.BlockSpec(shape, lambda *_: (0,) * nd, pipeline_mode=pl.Buffered(1))


def _rows(tm, width):
    return pl.BlockSpec((tm, width), lambda i: (i, 0))


def _colsum8(v):
    tm, n = v.shape
    return jnp.sum(v.reshape(tm // 8, 8, n), axis=0)


def _rms(x):
    r = lax.rsqrt(jnp.mean(x * x, axis=-1, keepdims=True) + EPS)
    return x * r, r


def _rms_bwd(dh, xh, r, gamma):
    dxh = dh * gamma
    dx = r * (dxh - xh * jnp.mean(dxh * xh, axis=-1, keepdims=True))
    return dx, _colsum8(dh * xh)


def _ffn_fwd(x, gamma, wg_t, wu_t, wd, name):
    t = x.shape[0]
    f = wg_t.shape[0]

    def body(x_ref, gam_ref, wg_ref, wu_ref, wd_ref, h_ref, g_ref, u_ref, a_ref, y_ref):
        xv = x_ref[...]
        xh, _ = _rms(xv)
        h = (xh * gam_ref[...]).astype(BF16)
        h_ref[...] = h
        for j in range(f // FC):
            sl = slice(j * FC, (j + 1) * FC)
            g = _dot_nt(h, wg_ref[sl, :])
            u = _dot_nt(h, wu_ref[sl, :])
            g_ref[:, sl] = g.astype(BF16)
            u_ref[:, sl] = u.astype(BF16)
            a_ref[:, sl] = (g * jax.nn.sigmoid(g) * u).astype(BF16)
        y_ref[...] = xv + 0.5 * _dot_nn(a_ref[...], wd_ref[...])

    return pl.pallas_call(
        body,
        name=name,
        grid=(t // TM,),
        in_specs=[_rows(TM, D_MODEL), _resident((1, D_MODEL)), _resident((f, D_MODEL)), _resident((f, D_MODEL)),
                  _resident((f, D_MODEL))],
        out_specs=[_rows(TM, D_MODEL), _rows(TM, f), _rows(TM, f), _rows(TM, f), _rows(TM, D_MODEL)],
        out_shape=[jax.ShapeDtypeStruct((t, D_MODEL), BF16), jax.ShapeDtypeStruct((t, f), BF16),
                   jax.ShapeDtypeStruct((t, f), BF16), jax.ShapeDtypeStruct((t, f), BF16),
                   jax.ShapeDtypeStruct((t, D_MODEL), F32)],
        compiler_params=_cparams(("parallel",)),
    )(x, gamma, wg_t, wu_t, wd)


def _ffn_bwd(d, x, gamma, g_act, u_act, wg_t, wu_t, wd, name):
    t = x.shape[0]
    f = wg_t.shape[0]

    def body(d_ref, x_ref, gam_ref, g_ref, u_ref, wg_ref, wu_ref, wd_ref, dx_ref, dg_ref, du_ref, db_ref, dgam_ref):
        dv = d_ref[...]
        db = (0.5 * dv).astype(BF16)
        db_ref[...] = db
        for j in range(f // FC):
            sl = slice(j * FC, (j + 1) * FC)
            da = _dot_nt(db, wd_ref[sl, :])
            g = g_ref[:, sl].astype(F32)
            u = u_ref[:, sl].astype(F32)
            s = jax.nn.sigmoid(g)
            dg_ref[:, sl] = (da * u * (s * (1.0 + g * (1.0 - s)))).astype(BF16)
            du_ref[:, sl] = (da * (g * s)).astype(BF16)
        dh = _dot_nn(dg_ref[...], wg_ref[...]) + _dot_nn(du_ref[...], wu_ref[...])
        xh, r = _rms(x_ref[...])
        dxn, dgam = _rms_bwd(dh, xh, r, gam_ref[...])
        dx_ref[...] = dv + dxn

        @pl.when(pl.program_id(0) == 0)
        def _():
            dgam_ref[...] = jnp.zeros_like(dgam_ref)

        dgam_ref[...] += dgam

    return pl.pallas_call(
        body,
        name=name,
        grid=(t // TM,),
        in_specs=[_rows(TM, D_MODEL), _rows(TM, D_MODEL), _resident((1, D_MODEL)), _rows(TM, f), _rows(TM, f),
                  _resident((f, D_MODEL)), _resident((f, D_MODEL)), _resident((f, D_MODEL))],
        out_specs=[_rows(TM, D_MODEL), _rows(TM, f), _rows(TM, f), _rows(TM, D_MODEL),
                   pl.BlockSpec((8, D_MODEL), lambda i: (0, 0))],
        out_shape=[jax.ShapeDtypeStruct((t, D_MODEL), F32), jax.ShapeDtypeStruct((t, f), BF16),
                   jax.ShapeDtypeStruct((t, f), BF16), jax.ShapeDtypeStruct((t, D_MODEL), BF16),
                   jax.ShapeDtypeStruct((8, D_MODEL), F32)],
        compiler_params=_cparams(("arbitrary",)),
    )(d, x, gamma, g_act, u_act, wg_t, wu_t, wd)


def _mm_tn(pieces, b, name, tile=256):
    t, n = b.shape
    npc = len(pieces)
    counts = [p.shape[1] // tile for p in pieces]
    los = [sum(counts[:k]) for k in range(npc)]
    total = sum(counts)

    def body(*refs):
        a_refs, b_ref, o_ref = refs[:npc], refs[npc], refs[npc + 1]
        i = pl.program_id(0)
        for k in range(npc):
            @pl.when(jnp.logical_and(i >= los[k], i < los[k] + counts[k]))
            def _(k=k):
                o_ref[...] = _dot_tn(a_refs[k][...], b_ref[...]).astype(BF16)

    def a_spec(k):
        return pl.BlockSpec((t, tile), lambda i: (0, jnp.clip(i - los[k], 0, counts[k] - 1)))

    return pl.pallas_call(
        body,
        name=name,
        grid=(total,),
        in_specs=[a_spec(k) for k in range(npc)] + [_resident((t, n))],
        out_specs=pl.BlockSpec((tile, n), lambda i: (i, 0)),
        out_shape=jax.ShapeDtypeStruct((total * tile, n), BF16),
        compiler_params=_cparams(("parallel",)),
    )(*pieces, b)


def _mm_tn_proj(dya, dyb, oa, ob, tile=256):
    t = dya.shape[0]

    def body(dya_ref, dyb_ref, oa_ref, ob_ref, o_ref):
        o_ref[:, 0:A_WIDTH] = _dot_tn(dya_ref[...], oa_ref[...]).astype(BF16)
        o_ref[:, A_WIDTH:A_WIDTH + B_Q_WIDTH] = _dot_tn(dyb_ref[...], ob_ref[...]).astype(BF16)

    col = pl.BlockSpec((t, tile), lambda i: (0, i))
    return pl.pallas_call(
        body,
        name="grad_proj",
        grid=(D_MODEL // tile,),
        in_specs=[col, col, _resident((t, A_WIDTH)), _resident((t, B_Q_WIDTH))],
        out_specs=pl.BlockSpec((tile, A_WIDTH + B_Q_WIDTH), lambda i: (i, 0)),
        out_shape=jax.ShapeDtypeStruct((D_MODEL, A_WIDTH + B_Q_WIDTH), BF16),
        compiler_params=_cparams(("parallel",)),
    )(dya, dyb, oa, ob)


def _proj_fwd(x, gamma, win_t):
    t = x.shape[0]

    def body(x_ref, gam_ref, w_ref, h_ref, qa_ref, qb_ref, gt_ref):
        xh, _ = _rms(x_ref[...])
        h = (xh * gam_ref[...]).astype(BF16)
        h_ref[...] = h
        for j in range(QKV_A // FC):
            qa_ref[:, j * FC:(j + 1) * FC] = _dot_nt(h, w_ref[j * FC:(j + 1) * FC, :]).astype(BF16)
        for j in range(QKV_B // FC):
            lo = QKV_A + j * FC
            qb_ref[:, j * FC:(j + 1) * FC] = _dot_nt(h, w_ref[lo:lo + FC, :]).astype(BF16)
        for j in range(2 * D_MODEL // FC):
            lo = QKV_A + QKV_B + j * FC
            gt_ref[:, j * FC:(j + 1) * FC] = _dot_nt(h, w_ref[lo:lo + FC, :])

    return pl.pallas_call(
        body,
        name="proj_fwd",
        grid=(t // TM,),
        in_specs=[_rows(TM, D_MODEL), _resident((1, D_MODEL)), _resident((IN_WIDTH, D_MODEL))],
        out_specs=[_rows(TM, D_MODEL), _rows(TM, QKV_A), _rows(TM, QKV_B), _rows(TM, 2 * D_MODEL)],
        out_shape=[jax.ShapeDtypeStruct((t, D_MODEL), BF16), jax.ShapeDtypeStruct((t, QKV_A), BF16),
                   jax.ShapeDtypeStruct((t, QKV_B), BF16), jax.ShapeDtypeStruct((t, 2 * D_MODEL), F32)],
        compiler_params=_cparams(("parallel",)),
    )(x, gamma, win_t)


def _proj_bwd(d, x, gamma, pieces, win_t):
    t = x.shape[0]
    npc = len(pieces)
    widths = [p.shape[1] for p in pieces]
    los = [sum(widths[:k]) for k in range(npc)]

    def body(*refs):
        d_ref, x_ref, gam_ref = refs[:3]
        p_refs = refs[3:3 + npc]
        w_ref, dx_ref, dgam_ref = refs[3 + npc:]
        dh = _dot_nn(p_refs[0][...], w_ref[0:widths[0], :])
        for k in range(1, npc):
            dh += _dot_nn(p_refs[k][...], w_ref[los[k]:los[k] + widths[k], :])
        xh, r = _rms(x_ref[...])
        dxn, dgam = _rms_bwd(dh, xh, r, gam_ref[...])
        dx_ref[...] = d_ref[...] + dxn

        @pl.when(pl.program_id(0) == 0)
        def _():
            dgam_ref[...] = jnp.zeros_like(dgam_ref)

        dgam_ref[...] += dgam

    return pl.pallas_call(
        body,
        name="proj_bwd",
        grid=(t // TM,),
        in_specs=[_rows(TM, D_MODEL), _rows(TM, D_MODEL), _resident((1, D_MODEL))] + [_rows(TM, w) for w in widths]
        + [_resident((IN_WIDTH, D_MODEL))],
        out_specs=[_rows(TM, D_MODEL), pl.BlockSpec((8, D_MODEL), lambda i: (0, 0))],
        out_shape=[jax.ShapeDtypeStruct((t, D_MODEL), F32), jax.ShapeDtypeStruct((8, D_MODEL), F32)],
        compiler_params=_cparams(("arbitrary",)),
    )(d, x, gamma, *pieces, win_t)


def _lane_half(shape):
    return lax.broadcasted_iota(jnp.int32, shape, len(shape) - 1) // D_HEAD


def _band_softmax(q, kk, bias, sink, qs, pad):
    s = _dot_nt(q, kk) * SCALE + bias
    col = lax.broadcasted_iota(jnp.int32, s.shape, 1)
    s = jnp.where(col + qs >= pad, s, NEG_INF)
    m = jnp.max(s, axis=-1, keepdims=True)
    if sink is not None:
        m = jnp.maximum(m, sink)
    p = jnp.exp(s - m)
    den = jnp.sum(p, axis=-1, keepdims=True)
    if sink is not None:
        den = den + jnp.exp(sink - m)
    return p, m, 1.0 / den


def _fill_padded(dst, src, pad):
    dst[0:pad, :] = jnp.zeros((pad, LANES), dst.dtype)
    dst[pad:, :] = src


def _attn_a_fwd(qkv, bias):
    bsz, s_len, _ = qkv.shape
    pad = A_PREV * CHUNK
    band = TQ + pad
    npair = A_HEADS // 2

    def body(q_ref, k_ref, v_ref, b_ref, o_ref, kp, vp):
        i = pl.program_id(2)

        @pl.when(i == 0)
        def _():
            _fill_padded(kp, k_ref[...], pad)
            _fill_padded(vp, v_ref[...], pad)

        qs = pl.multiple_of(i * TQ, TQ)
        kk = kp[pl.ds(qs, band), :]
        vv = vp[pl.ds(qs, band), :]
        q = q_ref[...]
        half = _lane_half((1, LANES))
        outs = []
        for j in range(2):
            qm = jnp.where(half == j, q, jnp.zeros_like(q))
            p, _, inv = _band_softmax(qm, kk, b_ref[j], None, qs, pad)
            outs.append(_dot_nn(p.astype(BF16), vv) * inv)
        o_ref[...] = jnp.where(half == 0, outs[0], outs[1]).astype(BF16)

    return pl.pallas_call(
        body,
        name="attn_a_fwd",
        grid=(bsz, npair, s_len // TQ),
        in_specs=[pl.BlockSpec((None, TQ, LANES), lambda b, hp, i: (b, i, hp)),
                  pl.BlockSpec((None, s_len, LANES), lambda b, hp, i: (b, 0, npair + hp)),
                  pl.BlockSpec((None, s_len, LANES), lambda b, hp, i: (b, 0, 2 * npair + hp)),
                  pl.BlockSpec((2, TQ, band), lambda b, hp, i: (hp, 0, 0))],
        out_specs=pl.BlockSpec((None, TQ, LANES), lambda b, hp, i: (b, i, hp)),
        out_shape=jax.ShapeDtypeStruct((bsz, s_len, A_WIDTH), BF16),
        scratch_shapes=[pltpu.VMEM((pad + s_len, LANES), BF16), pltpu.VMEM((pad + s_len, LANES), BF16)],
        compiler_params=_cparams(("arbitrary", "arbitrary", "arbitrary")),
    )(qkv, qkv, qkv, bias)


def _attn_a_bwd(qkv, bias, do):
    bsz, s_len, _ = qkv.shape
    pad = A_PREV * CHUNK
    band = TQ + pad
    npair = A_HEADS // 2
    n_i = s_len // TQ

    def body(q_ref, k_ref, v_ref, b_ref, do_ref, dq_ref, dk_ref, dv_ref, dbias_ref, kp, vp, dk_acc, dv_acc):
        b = pl.program_id(1)
        i = pl.program_id(2)

        @pl.when(i == 0)
        def _():
            _fill_padded(kp, k_ref[...], pad)
            _fill_padded(vp, v_ref[...], pad)
            dk_acc[...] = jnp.zeros_like(dk_acc)
            dv_acc[...] = jnp.zeros_like(dv_acc)

        @pl.when(jnp.logical_and(b == 0, i == 0))
        def _():
            dbias_ref[...] = jnp.zeros_like(dbias_ref)

        qs = pl.multiple_of(i * TQ, TQ)
        kk = kp[pl.ds(qs, band), :]
        vv = vp[pl.ds(qs, band), :]
        q = q_ref[...]
        dd = do_ref[...]
        half = _lane_half((1, LANES))
        dqs, dks, dvs = [], [], []
        for j in range(2):
            qm = jnp.where(half == j, q, jnp.zeros_like(q))
            dm = jnp.where(half == j, dd, jnp.zeros_like(dd))
            p, _, inv = _band_softmax(qm, kk, b_ref[j], None, qs, pad)
            pn = p * inv
            dp = _dot_nt(dm, vv)
            delta = jnp.sum(pn * dp, axis=-1, keepdims=True)
            ds = pn * (dp - delta)
            dbias_ref[j] += ds
            dsb = ds.astype(BF16)
            dqs.append(_dot_nn(dsb, kk))
            dks.append(_dot_tn(dsb, q))
            dvs.append(_dot_tn(pn.astype(BF16), dd))
        dq_ref[...] = (jnp.where(half == 0, dqs[0], dqs[1]) * SCALE).astype(BF16)
        dk_acc[pl.ds(qs, band), :] += jnp.where(half == 0, dks[0], dks[1]) * SCALE
        dv_acc[pl.ds(qs, band), :] += jnp.where(half == 0, dvs[0], dvs[1])

        @pl.when(i == n_i - 1)
        def _():
            dk_ref[...] = dk_acc[pad:, :].astype(BF16)
            dv_ref[...] = dv_acc[pad:, :].astype(BF16)

    qspec = pl.BlockSpec((None, TQ, LANES), lambda hp, b, i: (b, i, hp))
    kvout = pl.BlockSpec((None, s_len, LANES), lambda hp, b, i: (b, 0, hp))
    wide = jax.ShapeDtypeStruct((bsz, s_len, A_WIDTH), BF16)
    return pl.pallas_call(
        body,
        name="attn_a_bwd",
        grid=(npair, bsz, n_i),
        in_specs=[qspec,
                  pl.BlockSpec((None, s_len, LANES), lambda hp, b, i: (b, 0, npair + hp)),
                  pl.BlockSpec((None, s_len, LANES), lambda hp, b, i: (b, 0, 2 * npair + hp)),
                  pl.BlockSpec((2, TQ, band), lambda hp, b, i: (hp, 0, 0)),
                  qspec],
        out_specs=[qspec, kvout, kvout, pl.BlockSpec((2, TQ, band), lambda hp, b, i: (hp, 0, 0))],
        out_shape=[wide, wide, wide, jax.ShapeDtypeStruct((A_HEADS, TQ, band), F32)],
        scratch_shapes=[pltpu.VMEM((pad + s_len, LANES), BF16), pltpu.VMEM((pad + s_len, LANES), BF16),
                        pltpu.VMEM((pad + s_len, LANES), F32), pltpu.VMEM((pad + s_len, LANES), F32)],
        compiler_params=_cparams(("arbitrary", "arbitrary", "arbitrary")),
    )(qkv, qkv, qkv, bias, do)


def _stack_group(x, half):
    parts = []
    for g in range(B_GROUP):
        blk = x[:, LANES * (g // 2):LANES * (g // 2) + LANES]
        parts.append(jnp.where(half == (g % 2), blk, jnp.zeros_like(blk)))
    return jnp.concatenate(parts, axis=0)


def _unstack_group(y, half):
    left = jnp.where(half == 0, y[0:TQ], y[TQ:2 * TQ])
    right = jnp.where(half == 0, y[2 * TQ:3 * TQ], y[3 * TQ:4 * TQ])
    return left, right


def _fill_padded_dup(dst, src, pad, h, half):
    other = pltpu.roll(src, D_HEAD, 1)
    _fill_padded(dst, jnp.where(half == h, src, other), pad)


def _attn_b_fwd(qkv, bias, sink):
    bsz, s_len, _ = qkv.shape
    pad = B_PREV * CHUNK
    band = TQ + pad
    kcol = B_Q_WIDTH // LANES

    def body(q_ref, k_ref, v_ref, b_ref, s_ref, o_ref, kp, vp):
        h = pl.program_id(1)
        i = pl.program_id(2)
        half = _lane_half((1, LANES))

        @pl.when(i == 0)
        def _():
            _fill_padded_dup(kp, k_ref[...], pad, h, half)
            _fill_padded_dup(vp, v_ref[...], pad, h, half)

        qs = pl.multiple_of(i * TQ, TQ)
        kk = kp[pl.ds(qs, band), :]
        vv = vp[pl.ds(qs, band), :]
        qq = _stack_group(q_ref[...], half)
        p, _, inv = _band_softmax(qq, kk, b_ref[...], s_ref[...], qs, pad)
        o = _dot_nn(p.astype(BF16), vv) * inv
        left, right = _unstack_group(o, half)
        o_ref[:, 0:LANES] = left.astype(BF16)
        o_ref[:, LANES:2 * LANES] = right.astype(BF16)

    return pl.pallas_call(
        body,
        name="attn_b_fwd",
        grid=(bsz, B_KV_HEADS, s_len // TQ),
        in_specs=[pl.BlockSpec((None, TQ, 2 * LANES), lambda b, h, i: (b, i, h)),
                  pl.BlockSpec((None, s_len, LANES), lambda b, h, i: (b, 0, kcol)),
                  pl.BlockSpec((None, s_len, LANES), lambda b, h, i: (b, 0, kcol + 1)),
                  pl.BlockSpec((None, B_GROUP * TQ, band), lambda b, h, i: (h, 0, 0)),
                  pl.BlockSpec((None, B_GROUP * TQ, 1), lambda b, h, i: (h, 0, 0))],
        out_specs=pl.BlockSpec((None, TQ, 2 * LANES), lambda b, h, i: (b, i, h)),
        out_shape=jax.ShapeDtypeStruct((bsz, s_len, B_Q_WIDTH), BF16),
        scratch_shapes=[pltpu.VMEM((pad + s_len, LANES), BF16), pltpu.VMEM((pad + s_len, LANES), BF16)],
        compiler_params=_cparams(("arbitrary", "arbitrary", "arbitrary")),
    )(qkv, qkv, qkv, bias, sink)


def _attn_b_bwd(qkv, bias, sink, do):
    bsz, s_len, _ = qkv.shape
    pad = B_PREV * CHUNK
    band = TQ + pad
    kcol = B_Q_WIDTH // LANES
    n_i = s_len // TQ

    def body(q_ref, k_ref, v_ref, b_ref, s_ref, do_ref, dq_ref, dkv_ref, dsink_ref, kp, vp, dk_acc, dv_acc):
        b = pl.program_id(0)
        h = pl.program_id(1)
        i = pl.program_id(2)
        half = _lane_half((1, LANES))

        @pl.when(i == 0)
        def _():
            _fill_padded_dup(kp, k_ref[...], pad, h, half)
            _fill_padded_dup(vp, v_ref[...], pad, h, half)

        @pl.when(jnp.logical_and(h == 0, i == 0))
        def _():
            dk_acc[...] = jnp.zeros_like(dk_acc)
            dv_acc[...] = jnp.zeros_like(dv_acc)

        @pl.when(jnp.logical_and(b == 0, jnp.logical_and(h == 0, i == 0)))
        def _():
            dsink_ref[...] = jnp.zeros_like(dsink_ref)

        qs = pl.multiple_of(i * TQ, TQ)
        kk = kp[pl.ds(qs, band), :]
        vv = vp[pl.ds(qs, band), :]
        qq = _stack_group(q_ref[...], half)
        dd = _stack_group(do_ref[...], half)
        sink_rows = s_ref[...]
        p, m, inv = _band_softmax(qq, kk, b_ref[...], sink_rows, qs, pad)
        pn = p * inv
        dp = _dot_nt(dd, vv)
        delta = jnp.sum(pn * dp, axis=-1, keepdims=True)
        ds = pn * (dp - delta)
        dsb = ds.astype(BF16)
        left, right = _unstack_group(_dot_nn(dsb, kk) * SCALE, half)
        dq_ref[:, 0:LANES] = left.astype(BF16)
        dq_ref[:, LANES:2 * LANES] = right.astype(BF16)
        dk2 = _dot_tn(dsb, qq) * SCALE
        dv2 = _dot_tn(pn.astype(BF16), dd)
        dk_acc[pl.ds(qs, band), :] += jnp.where(half == h, dk2 + pltpu.roll(dk2, D_HEAD, 1), 0.0)
        dv_acc[pl.ds(qs, band), :] += jnp.where(half == h, dv2 + pltpu.roll(dv2, D_HEAD, 1), 0.0)

        @pl.when(jnp.logical_and(h == B_KV_HEADS - 1, i == n_i - 1))
        def _():
            dkv_ref[:, 0:LANES] = dk_acc[pad:, :].astype(BF16)
            dkv_ref[:, LANES:2 * LANES] = dv_acc[pad:, :].astype(BF16)

        dsk = -(jnp.exp(sink_rows - m) * inv) * delta
        row = lax.broadcasted_iota(jnp.int32, (8, LANES), 0)
        upd = jnp.zeros((8, LANES), F32)
        for g in range(B_GROUP):
            tot = jnp.sum(dsk[g * TQ:(g + 1) * TQ, :], axis=0, keepdims=True)
            upd = upd + jnp.where(row == g, tot, 0.0)
        dsink_ref[h] += upd

    qspec = pl.BlockSpec((None, TQ, 2 * LANES), lambda b, h, i: (b, i, h))
    return pl.pallas_call(
        body,
        name="attn_b_bwd",
        grid=(bsz, B_KV_HEADS, n_i),
        in_specs=[qspec,
                  pl.BlockSpec((None, s_len, LANES), lambda b, h, i: (b, 0, kcol)),
                  pl.BlockSpec((None, s_len, LANES), lambda b, h, i: (b, 0, kcol + 1)),
                  pl.BlockSpec((None, B_GROUP * TQ, band), lambda b, h, i: (h, 0, 0)),
                  pl.BlockSpec((None, B_GROUP * TQ, 1), lambda b, h, i: (h, 0, 0)),
                  qspec],
        out_specs=[qspec, pl.BlockSpec((None, s_len, 2 * LANES), lambda b, h, i: (b, 0, 0)),
                   pl.BlockSpec((B_KV_HEADS, 8, LANES), lambda b, h, i: (0, 0, 0))],
        out_shape=[jax.ShapeDtypeStruct((bsz, s_len, B_Q_WIDTH), BF16),
                   jax.ShapeDtypeStruct((bsz, s_len, 2 * B_KV_WIDTH), BF16),
                   jax.ShapeDtypeStruct((B_KV_HEADS, 8, LANES), F32)],
        scratch_shapes=[pltpu.VMEM((pad + s_len, LANES), BF16), pltpu.VMEM((pad + s_len, LANES), BF16),
                        pltpu.VMEM((pad + s_len, LANES), F32), pltpu.VMEM((pad + s_len, LANES), F32)],
        compiler_params=_cparams(("arbitrary", "arbitrary", "arbitrary")),
    )(qkv, qkv, qkv, bias, sink, do)


REL_COLS = 3 * 128
REL_WRAP = 512


def _bias_a_build(tv):
    h = tv.shape[0]
    pad = A_PREV * CHUNK
    band = TQ + pad

    def body(tv_ref, o_ref):
        row = tv_ref[...]
        x = jnp.broadcast_to(row, (TQ, REL_WRAP))
        r = lax.broadcasted_iota(jnp.int32, x.shape, 0)
        for bit in range(8):
            sh = 1 << bit
            x = jnp.where((r & sh) != 0, pltpu.roll(x, sh, 1), x)
        far = jnp.broadcast_to(row[:, 0:1], (TQ, band - REL_COLS))
        full = jnp.concatenate([far, x[:, REL_WRAP // 2:REL_WRAP], x[:, 0:REL_COLS - REL_WRAP // 2]], axis=1)
        qc = (lax.broadcasted_iota(jnp.int32, full.shape, 0) + pad) // CHUNK
        kc = lax.broadcasted_iota(jnp.int32, full.shape, 1) // CHUNK
        ok = jnp.logical_and(kc <= qc, kc >= qc - A_PREV)
        o_ref[...] = jnp.where(ok, full, NEG_INF)

    return pl.pallas_call(
        body,
        name="bias_a_build",
        grid=(h,),
        in_specs=[pl.BlockSpec((None, 1, REL_WRAP), lambda hh: (hh, 0, 0))],
        out_specs=pl.BlockSpec((None, TQ, band), lambda hh: (hh, 0, 0)),
        out_shape=jax.ShapeDtypeStruct((h, TQ, band), F32),
        compiler_params=_cparams(("parallel",)),
    )(tv)


def _relbias_grad(dbias):
    h, rows, band = dbias.shape
    off = band - REL_COLS

    def body(d_ref, o_ref):
        x = d_ref[...]
        r = lax.broadcasted_iota(jnp.int32, x.shape, 0)
        c = lax.broadcasted_iota(jnp.int32, x.shape, 1) - r
        x = jnp.where(jnp.logical_and(c >= 1, c < REL_TABLE), x, 0.0)
        for bit in range(8):
            sh = 1 << bit
            x = jnp.where((r & sh) != 0, pltpu.roll(x, REL_COLS - sh, 1), x)
        diag = jnp.sum(x, axis=0, keepdims=True)
        lane = lax.broadcasted_iota(jnp.int32, diag.shape, 1)
        diag = jnp.where(jnp.logical_and(lane >= 1, lane < REL_TABLE), diag, 0.0)
        rest = -jnp.sum(diag, axis=1, keepdims=True)
        o_ref[...] = jnp.broadcast_to(jnp.where(lane == 0, rest, diag), o_ref.shape)

    return pl.pallas_call(
        body,
        name="relbias_grad",
        grid=(h,),
        in_specs=[pl.BlockSpec((None, rows, REL_COLS), lambda hh: (hh, 0, off // REL_COLS))],
        out_specs=pl.BlockSpec((None, 8, REL_COLS), lambda hh: (hh, 0, 0)),
        out_shape=jax.ShapeDtypeStruct((h, 8, REL_COLS), F32),
        compiler_params=_cparams(("parallel",)),
    )(dbias)


def _mix_out_fwd(x, oa, ob, gates, proj_t, wout):
    t = x.shape[0]

    def body(x_ref, oa_ref, ob_ref, gt_ref, pt_ref, wo_ref, y_ref, ya_ref, yb_ref, mg_ref):
        ya = _dot_nt(oa_ref[...], pt_ref[:, 0:A_WIDTH])
        yb = _dot_nt(ob_ref[...], pt_ref[:, A_WIDTH:A_WIDTH + B_Q_WIDTH])
        ya_ref[...] = ya.astype(BF16)
        yb_ref[...] = yb.astype(BF16)
        mg = jax.nn.sigmoid(gt_ref[:, 0:D_MODEL]) * ya + jax.nn.sigmoid(gt_ref[:, D_MODEL:2 * D_MODEL]) * yb
        mgb = mg.astype(BF16)
        mg_ref[...] = mgb
        y_ref[...] = x_ref[...] + _dot_nn(mgb, wo_ref[...])

    return pl.pallas_call(
        body,
        name="mix_out_fwd",
        grid=(t // TM,),
        in_specs=[_rows(TM, D_MODEL), _rows(TM, A_WIDTH), _rows(TM, B_Q_WIDTH), _rows(TM, 2 * D_MODEL),
                  _resident((D_MODEL, A_WIDTH + B_Q_WIDTH)), _resident((D_MODEL, D_MODEL))],
        out_specs=[_rows(TM, D_MODEL), _rows(TM, D_MODEL), _rows(TM, D_MODEL), _rows(TM, D_MODEL)],
        out_shape=[jax.ShapeDtypeStruct((t, D_MODEL), F32), jax.ShapeDtypeStruct((t, D_MODEL), BF16),
                   jax.ShapeDtypeStruct((t, D_MODEL), BF16), jax.ShapeDtypeStruct((t, D_MODEL), BF16)],
        compiler_params=_cparams(("parallel",)),
    )(x, oa, ob, gates, proj_t, wout)


def _mix_out_bwd(d, gates, ya, yb, proj_t, wout):
    t = d.shape[0]

    def body(d_ref, gt_ref, ya_ref, yb_ref, pt_ref, wo_ref, db_ref, dya_ref, dyb_ref, doa_ref, dob_ref, dgt_ref):
        db = d_ref[...].astype(BF16)
        db_ref[...] = db
        dmg = _dot_nt(db, wo_ref[...])
        sa = jax.nn.sigmoid(gt_ref[:, 0:D_MODEL])
        sb = jax.nn.sigmoid(gt_ref[:, D_MODEL:2 * D_MODEL])
        dya = (dmg * sa).astype(BF16)
        dyb = (dmg * sb).astype(BF16)
        dya_ref[...] = dya
        dyb_ref[...] = dyb
        dgt_ref[:, 0:D_MODEL] = (dmg * ya_ref[...].astype(F32) * (sa * (1.0 - sa))).astype(BF16)
        dgt_ref[:, D_MODEL:2 * D_MODEL] = (dmg * yb_ref[...].astype(F32) * (sb * (1.0 - sb))).astype(BF16)
        doa_ref[...] = _dot_nn(dya, pt_ref[:, 0:A_WIDTH]).astype(BF16)
        dob_ref[...] = _dot_nn(dyb, pt_ref[:, A_WIDTH:A_WIDTH + B_Q_WIDTH]).astype(BF16)

    return pl.pallas_call(
        body,
        name="mix_out_bwd",
        grid=(t // TM,),
        in_specs=[_rows(TM, D_MODEL), _rows(TM, 2 * D_MODEL), _rows(TM, D_MODEL), _rows(TM, D_MODEL),
                  _resident((D_MODEL, A_WIDTH + B_Q_WIDTH)), _resident((D_MODEL, D_MODEL))],
        out_specs=[_rows(TM, D_MODEL), _rows(TM, D_MODEL), _rows(TM, D_MODEL), _rows(TM, A_WIDTH),
                   _rows(TM, B_Q_WIDTH), _rows(TM, 2 * D_MODEL)],
        out_shape=[jax.ShapeDtypeStruct((t, D_MODEL), BF16), jax.ShapeDtypeStruct((t, D_MODEL), BF16),
                   jax.ShapeDtypeStruct((t, D_MODEL), BF16), jax.ShapeDtypeStruct((t, A_WIDTH), BF16),
                   jax.ShapeDtypeStruct((t, B_Q_WIDTH), BF16), jax.ShapeDtypeStruct((t, 2 * D_MODEL), BF16)],
        compiler_params=_cparams(("parallel",)),
    )(d, gates, ya, yb, proj_t, wout)


def _loss_head(x, gamma, target):
    t = x.shape[0]

    def body(x_ref, gam_ref, t_ref, dx_ref, dgam_ref, loss_ref):
        xh, r = _rms(x_ref[...])
        gam = gam_ref[...]
        e = xh * gam - t_ref[...]
        dy = e * (1.0 / D_MODEL)
        dxn, dgam = _rms_bwd(dy, xh, r, gam)
        dx_ref[...] = dxn

        @pl.when(pl.program_id(0) == 0)
        def _():
            dgam_ref[...] = jnp.zeros_like(dgam_ref)
            loss_ref[...] = jnp.zeros_like(loss_ref)

        dgam_ref[...] += dgam
        loss_ref[...] += _colsum8(e * e) * (0.5 / D_MODEL)

    return pl.pallas_call(
        body,
        name="loss_head",
        grid=(t // TM,),
        in_specs=[_rows(TM, D_MODEL), _resident((1, D_MODEL)), _rows(TM, D_MODEL)],
        out_specs=[_rows(TM, D_MODEL), pl.BlockSpec((8, D_MODEL), lambda i: (0, 0)),
                   pl.BlockSpec((8, D_MODEL), lambda i: (0, 0))],
        out_shape=[jax.ShapeDtypeStruct((t, D_MODEL), F32), jax.ShapeDtypeStruct((8, D_MODEL), F32),
                   jax.ShapeDtypeStruct((8, D_MODEL), F32)],
        compiler_params=_cparams(("arbitrary",)),
    )(x, gamma, target)


def _hbm():
    return pl.BlockSpec(memory_space=pltpu.HBM)


def _place():
    x, y, c = lax.axis_index("x"), lax.axis_index("y"), lax.axis_index("c")
    chips = [(1 - x, y), (x, 1 - y), (1 - x, 1 - y)]
    return x, y, c, chips


def _all_gather(shards):
    n = len(shards)
    per = 7

    def body(*refs):
        ins, outs = refs[:n], refs[n:2 * n]
        send_sems, recv_sems, local_sems = refs[2 * n:]
        x, y, c, chips = _place()
        me, sibling = (x, y, c), (x, y, 1 - c)

        def rows(k, p):
            r = ins[k].shape[0]
            return outs[k].at[pl.ds((4 * p[0] + 2 * p[1] + p[2]) * r, r), :]

        def copy(k, slot, block, to, src=None):
            return pltpu.make_async_remote_copy(
                src_ref=rows(k, block) if src is None else src, dst_ref=rows(k, block),
                send_sem=send_sems.at[k * per + slot], recv_sem=recv_sems.at[k * per + slot],
                device_id=to, device_id_type=MESH)

        mine = [pltpu.make_async_copy(ins[k], rows(k, me), local_sems.at[k]) for k in range(n)]
        first = []
        for k in range(n):
            mine[k].start()
            first.append(copy(k, 0, me, sibling, src=ins[k]))
            first += [copy(k, 1 + j, me, (*chip, c), src=ins[k]) for j, chip in enumerate(chips)]
        for cp in first:
            cp.start()
        passed = []
        for j, chip in enumerate(chips):
            for k in range(n):
                copy(k, 1 + j, (*chip, c), me).wait_recv()
                fwd = copy(k, 4 + j, (*chip, c), sibling)
                fwd.start()
                passed.append(fwd)
        for k in range(n):
            copy(k, 0, sibling, me).wait_recv()
            for j, chip in enumerate(chips):
                copy(k, 4 + j, (*chip, 1 - c), me).wait_recv()
        for cp in first + passed:
            cp.wait_send()
        for cp in mine:
            cp.wait()

    return pl.pallas_call(
        body,
        name="all_gather_weights",
        in_specs=[_hbm()] * n,
        out_specs=[_hbm()] * n,
        out_shape=[jax.ShapeDtypeStruct((N_DEV * s.shape[0], s.shape[1]), s.dtype) for s in shards],
        scratch_shapes=[pltpu.SemaphoreType.DMA((n * per,)), pltpu.SemaphoreType.DMA((n * per,)),
                        pltpu.SemaphoreType.DMA((n,))],
    )(*shards)


def _pair_exchange(grads):
    n = len(grads)

    def body(*refs):
        ins, outs = refs[:n], refs[n:2 * n]
        send_sems, recv_sems = refs[2 * n:]
        x, y, c, _ = _place()
        copies = []
        for k in range(n):
            r = ins[k].shape[0] // N_DEV
            for q in range(N_CHIP):
                copies.append(pltpu.make_async_remote_copy(
                    src_ref=ins[k].at[pl.ds((2 * q + 1 - c) * r, r), :], dst_ref=outs[k].at[pl.ds(q * r, r), :],
                    send_sem=send_sems.at[k * N_CHIP + q], recv_sem=recv_sems.at[k * N_CHIP + q],
                    device_id=(x, y, 1 - c), device_id_type=MESH))
        for cp in copies:
            cp.start()
        for cp in copies:
            cp.wait_recv()
        for cp in copies:
            cp.wait_send()

    return pl.pallas_call(
        body,
        name="grad_pair_exchange",
        in_specs=[_hbm()] * n,
        out_specs=[_hbm()] * n,
        out_shape=[jax.ShapeDtypeStruct((g.shape[0] // 2, g.shape[1]), g.dtype) for g in grads],
        scratch_shapes=[pltpu.SemaphoreType.DMA((n * N_CHIP,)), pltpu.SemaphoreType.DMA((n * N_CHIP,))],
    )(*grads)


def _chip_exchange(psums):
    n = len(psums)

    def body(*refs):
        ins, outs = refs[:n], refs[n:2 * n]
        send_sems, recv_sems = refs[2 * n:]
        _, _, c, chips = _place()
        copies = []
        for k in range(n):
            r = ins[k].shape[0] // N_CHIP
            for j, chip in enumerate(chips):
                copies.append(pltpu.make_async_remote_copy(
                    src_ref=ins[k].at[pl.ds((2 * chip[0] + chip[1]) * r, r), :], dst_ref=outs[k].at[pl.ds(j * r, r), :],
                    send_sem=send_sems.at[k * 3 + j], recv_sem=recv_sems.at[k * 3 + j],
                    device_id=(*chip, c), device_id_type=MESH))
        for cp in copies:
            cp.start()
        for cp in copies:
            cp.wait_recv()
        for cp in copies:
            cp.wait_send()

    return pl.pallas_call(
        body,
        name="grad_chip_exchange",
        in_specs=[_hbm()] * n,
        out_specs=[_hbm()] * n,
        out_shape=[jax.ShapeDtypeStruct((3 * p.shape[0] // N_CHIP, p.shape[1]), p.dtype) for p in psums],
        scratch_shapes=[pltpu.SemaphoreType.DMA((n * 3,)), pltpu.SemaphoreType.DMA((n * 3,))],
    )(*psums)


def _pair_sum(core, grads, recvd, name):
    n = len(grads)
    r = grads[0].shape[0] // N_DEV
    cdim = grads[0].shape[1]
    tr = r // 2 if r % 32 == 0 else r
    nt = r // tr

    def body(core_ref, *refs):
        del core_ref
        for k in range(n):
            refs[2 * n + k][...] = (refs[k][...].astype(F32) + refs[n + k][...].astype(F32)).astype(BF16)

    gspec = pl.BlockSpec((tr, cdim), lambda q, i, core_ref: ((2 * q + core_ref[0]) * nt + i, 0))
    rspec = pl.BlockSpec((tr, cdim), lambda q, i, core_ref: (q * nt + i, 0))
    return pl.pallas_call(
        body,
        name=name,
        grid_spec=pltpu.PrefetchScalarGridSpec(
            num_scalar_prefetch=1, grid=(N_CHIP, nt), in_specs=[gspec] * n + [rspec] * n, out_specs=[rspec] * n),
        out_shape=[jax.ShapeDtypeStruct((N_CHIP * r, cdim), BF16) for _ in range(n)],
        compiler_params=_cparams(("parallel", "parallel")),
    )(core, *grads, *recvd)


def _final_sum(chip, psums, recvd, name):
    n = len(psums)
    r = psums[0].shape[0] // N_CHIP
    cdim = psums[0].shape[1]
    tr = r // 2 if r % 32 == 0 else r
    nt = r // tr

    def body(chip_ref, *refs):
        del chip_ref
        for k in range(n):
            got = refs[n + k]
            tot = refs[k][...].astype(F32) + got[0].astype(F32)
            tot = tot + got[1].astype(F32)
            tot = tot + got[2].astype(F32)
            refs[2 * n + k][...] = tot

    pspec = pl.BlockSpec((tr, cdim), lambda i, chip_ref: (chip_ref[0] * nt + i, 0))
    rspec = pl.BlockSpec((3, tr, cdim), lambda i, chip_ref: (0, i, 0))
    ospec = pl.BlockSpec((tr, cdim), lambda i, chip_ref: (i, 0))
    return pl.pallas_call(
        body,
        name=name,
        grid_spec=pltpu.PrefetchScalarGridSpec(
            num_scalar_prefetch=1, grid=(nt,), in_specs=[pspec] * n + [rspec] * n, out_specs=[ospec] * n),
        out_shape=[jax.ShapeDtypeStruct((r, cdim), F32) for _ in range(n)],
        compiler_params=_cparams(("parallel",)),
    )(chip, *psums, *[g.reshape(3, r, cdim) for g in recvd])


SMALL_ROWS = 16


def _all_reduce_small(part):
    def body(p_ref, o_ref, buf, send_sems, recv_sems):
        x, y, c, _ = _place()
        me = 4 * x + 2 * y + c
        buf[me] = p_ref[...]
        copies = []
        for d in range(1, N_DEV):
            peer = me ^ d
            copies.append(pltpu.make_async_remote_copy(
                src_ref=p_ref, dst_ref=buf.at[me], send_sem=send_sems.at[d - 1], recv_sem=recv_sems.at[d - 1],
                device_id=(peer // 4, (peer // 2) % 2, peer % 2), device_id_type=MESH))
        for cp in copies:
            cp.start()
        for cp in copies:
            cp.wait_recv()
        for cp in copies:
            cp.wait_send()
        tot = buf[0]
        for d in range(1, N_DEV):
            tot = tot + buf[d]
        o_ref[...] = tot

    return pl.pallas_call(
        body,
        name="all_reduce_small",
        in_specs=[pl.BlockSpec(memory_space=pltpu.VMEM)],
        out_specs=pl.BlockSpec(memory_space=pltpu.VMEM),
        out_shape=jax.ShapeDtypeStruct(part.shape, F32),
        scratch_shapes=[pltpu.VMEM((N_DEV,) + part.shape, F32), pltpu.SemaphoreType.DMA((N_DEV - 1,)),
                        pltpu.SemaphoreType.DMA((N_DEV - 1,))],
    )(part)


def _adamw(ws, gs, ms, vs, name):
    n = len(ws)
    r, cdim = ws[0].shape
    tr = r
    for cand in (512, 256, 128, 176, 64):
        if r % cand == 0 and r > cand:
            tr = cand
            break
    c1 = 1.0 - ADAM_B1 ** ADAM_STEP
    c2 = 1.0 - ADAM_B2 ** ADAM_STEP

    def body(*refs):
        for k in range(n):
            w, g, m, v = (refs[j * n + k][...] for j in range(4))
            m2 = ADAM_B1 * m + (1.0 - ADAM_B1) * g
            v2 = ADAM_B2 * v + (1.0 - ADAM_B2) * (g * g)
            delta = -ADAM_LR * ((m2 / c1) / (jnp.sqrt(v2 / c2) + ADAM_EPS) + ADAM_WD * w)
            refs[4 * n + k][...] = delta
            refs[5 * n + k][...] = m2
            refs[6 * n + k][...] = v2

    spec = pl.BlockSpec((tr, cdim), lambda i: (i, 0))
    outs = pl.pallas_call(
        body,
        name=name,
        grid=(r // tr,),
        in_specs=[spec] * (4 * n),
        out_specs=[spec] * (3 * n),
        out_shape=[jax.ShapeDtypeStruct((r, cdim), F32)] * (3 * n),
        compiler_params=_cparams(("parallel",)),
    )(*ws, *gs, *ms, *vs)
    return outs[:n], outs[n:2 * n], outs[2 * n:]


def _band_allowed(n_prev):
    pad = n_prev * CHUNK
    qc = (np.arange(TQ)[:, None] + pad) // CHUNK
    kc = np.arange(TQ + pad)[None, :] // CHUNK
    return (kc <= qc) & (kc >= qc - n_prev)


def _bias_b():
    pad = B_PREV * CHUNK
    slopes = np.array([2.0 ** (-8.0 * (i + 1) / B_Q_HEADS) for i in range(B_Q_HEADS)], dtype=np.float32)
    dist = np.abs(np.arange(TQ)[:, None] - np.arange(TQ + pad)[None, :] + pad).astype(np.float32)
    bias = -slopes.reshape(B_KV_HEADS, B_GROUP, 1, 1) * dist[None, None]
    bias = np.where(_band_allowed(B_PREV)[None, None], bias, np.float32(NEG_INF)).astype(np.float32)
    return bias.reshape(B_KV_HEADS, B_GROUP * TQ, TQ + pad)


def kernel(x, ffn1_norm, ffn1_w_gate, ffn1_w_up, ffn1_w_down, mix_norm, w_in, rel_bias, sinks, w_proj_a, w_proj_b, w_out, ffn2_norm, ffn2_w_gate, ffn2_w_up, ffn2_w_down, final_norm, loss_target, m_ffn1_norm, m_ffn1_w_gate, m_ffn1_w_up, m_ffn1_w_down, m_mix_norm, m_w_in, m_rel_bias, m_sinks, m_w_proj_a, m_w_proj_b, m_w_out, m_ffn2_norm, m_ffn2_w_gate, m_ffn2_w_up, m_ffn2_w_down, m_final_norm, v_ffn1_norm, v_ffn1_w_gate, v_ffn1_w_up, v_ffn1_w_down, v_mix_norm, v_w_in, v_rel_bias, v_sinks, v_w_proj_a, v_w_proj_b, v_w_out, v_ffn2_norm, v_ffn2_w_gate, v_ffn2_w_up, v_ffn2_w_down, v_final_norm):
    bsz, s_len, _ = x.shape
    t = bsz * s_len
    core = lax.axis_index("c").astype(jnp.int32).reshape(1)
    chip = (2 * lax.axis_index("x") + lax.axis_index("y")).astype(jnp.int32).reshape(1)

    def row_form(w):
        return w.astype(BF16).T

    shards = [row_form(ffn1_w_gate), row_form(ffn1_w_up), ffn1_w_down.astype(BF16), row_form(w_in),
              jnp.concatenate([row_form(w_proj_a), row_form(w_proj_b)], axis=1), w_out.astype(BF16),
              row_form(ffn2_w_gate), row_form(ffn2_w_up), ffn2_w_down.astype(BF16)]
    wg1, wu1, wd1, win_t, proj_t, wout, wg2, wu2, wd2 = _all_gather(shards)

    x0 = x.reshape(t, D_MODEL)
    tgt = loss_target.reshape(t, D_MODEL)
    gam1, gam2, gam3, gam4 = (g.reshape(1, D_MODEL) for g in (ffn1_norm, mix_norm, ffn2_norm, final_norm))

    h1, g1, u1, a1, x1 = _ffn_fwd(x0, gam1, wg1, wu1, wd1, "ffn1_fwd")
    h2, qkv_a, qkv_b, gates = _proj_fwd(x1, gam2, win_t)
    qkv_a3 = qkv_a.reshape(bsz, s_len, QKV_A)
    qkv_b3 = qkv_b.reshape(bsz, s_len, QKV_B)

    far = jnp.broadcast_to(rel_bias[:, REL_TABLE - 1:REL_TABLE], (A_HEADS, REL_WRAP // 2))
    tv = jnp.concatenate([far, jnp.flip(rel_bias, axis=1), jnp.zeros((A_HEADS, REL_WRAP // 2 - REL_TABLE), F32)], axis=1)
    bias_a = _bias_a_build(tv.reshape(A_HEADS, 1, REL_WRAP))
    bias_b = jnp.asarray(_bias_b())
    sink_rows = jnp.repeat(sinks.reshape(B_KV_HEADS, B_GROUP), TQ, axis=1).reshape(B_KV_HEADS, B_GROUP * TQ, 1)

    oa = _attn_a_fwd(qkv_a3, bias_a).reshape(t, A_WIDTH)
    ob = _attn_b_fwd(qkv_b3, bias_b, sink_rows).reshape(t, B_Q_WIDTH)
    x2, ya, yb, mg = _mix_out_fwd(x1, oa, ob, gates, proj_t, wout)
    h3, g2, u2, a2, x3 = _ffn_fwd(x2, gam3, wg2, wu2, wd2, "ffn2_fwd")

    dx3, dgam4, loss_part = _loss_head(x3, gam4, tgt)

    dx2, dg2, du2, db2, dgam3 = _ffn_bwd(dx3, x2, gam3, g2, u2, wg2, wu2, wd2, "ffn2_bwd")
    gw_g2 = _mm_tn([dg2], h3, "grad_ffn2_gate")
    gw_u2 = _mm_tn([du2], h3, "grad_ffn2_up")
    gw_d2 = _mm_tn([a2], db2, "grad_ffn2_down")

    dxb, dya, dyb, doa, dob, dgates = _mix_out_bwd(dx2, gates, ya, yb, proj_t, wout)
    gw_out = _mm_tn([mg], dxb, "grad_w_out")
    gw_proj = _mm_tn_proj(dya, dyb, oa, ob)

    dqa, dka, dva, dbias_a = _attn_a_bwd(qkv_a3, bias_a, doa.reshape(bsz, s_len, A_WIDTH))
    dqb, dkvb, dsink = _attn_b_bwd(qkv_b3, bias_b, sink_rows, dob.reshape(bsz, s_len, B_Q_WIDTH))
    drel_lanes = _relbias_grad(dbias_a)
    dproj = [dqa.reshape(t, A_WIDTH), dka.reshape(t, A_WIDTH), dva.reshape(t, A_WIDTH), dqb.reshape(t, B_Q_WIDTH),
             dkvb.reshape(t, 2 * B_KV_WIDTH), dgates]

    dx1, dgam2 = _proj_bwd(dx2, x1, gam2, dproj, win_t)
    gw_in = _mm_tn(dproj, h2, "grad_w_in")

    dx0, dg1, du1, db1, dgam1 = _ffn_bwd(dx1, x0, gam1, g1, u1, wg1, wu1, wd1, "ffn1_bwd")
    gw_g1 = _mm_tn([dg1], h1, "grad_ffn1_gate")
    gw_u1 = _mm_tn([du1], h1, "grad_ffn1_up")
    gw_d1 = _mm_tn([a1], db1, "grad_ffn1_down")

    full = [gw_g1, gw_u1, gw_d1, gw_g2, gw_u2, gw_d2, gw_in, gw_proj, gw_out]
    from_sibling = _pair_exchange(full)
    pair = (_pair_sum(core, full[0:6], from_sibling[0:6], "pair_sum_ffn")
            + _pair_sum(core, full[6:7], from_sibling[6:7], "pair_sum_w_in")
            + _pair_sum(core, full[7:9], from_sibling[7:9], "pair_sum_mix"))
    from_chips = _chip_exchange(pair)
    gsum = (_final_sum(chip, pair[0:6], from_chips[0:6], "grad_sum_ffn")
            + _final_sum(chip, pair[6:7], from_chips[6:7], "grad_sum_w_in")
            + _final_sum(chip, pair[7:9], from_chips[7:9], "grad_sum_mix"))
    g_g1, g_u1, g_d1, g_g2, g_u2, g_d2, g_in, g_proj, g_out = gsum
    grads = {
        "ffn1_w_gate": g_g1.T, "ffn1_w_up": g_u1.T, "ffn1_w_down": g_d1, "w_in": g_in.T,
        "w_proj_a": g_proj[:, 0:A_WIDTH].T, "w_proj_b": g_proj[:, A_WIDTH:].T, "w_out": g_out,
        "ffn2_w_gate": g_g2.T, "ffn2_w_up": g_u2.T, "ffn2_w_down": g_d2,
    }

    def row_of(v):
        return jnp.pad(v.reshape(1, -1), ((0, 0), (0, D_MODEL - v.size)))

    def table_rows(v):
        return jnp.pad(v, ((0, 0), (0, D_MODEL - REL_TABLE)))

    drel_local = jnp.flip(drel_lanes[:, 0, 0:REL_TABLE], axis=1)
    small_part = jnp.concatenate(
        [jnp.sum(dgam1, axis=0, keepdims=True), jnp.sum(dgam2, axis=0, keepdims=True),
         jnp.sum(dgam3, axis=0, keepdims=True), jnp.sum(dgam4, axis=0, keepdims=True),
         row_of(jnp.sum(loss_part)), row_of(dsink[:, 0:B_GROUP, 0]), jnp.zeros((2, D_MODEL), F32),
         table_rows(drel_local)], axis=0)
    small = _all_reduce_small(small_part)
    loss = small[4, 0]

    def pack(n1, n2, n3, n4, sk, tb):
        return jnp.concatenate([n1.reshape(1, -1), n2.reshape(1, -1), n3.reshape(1, -1), n4.reshape(1, -1),
                                jnp.zeros((1, D_MODEL), F32), row_of(sk), jnp.zeros((2, D_MODEL), F32), table_rows(tb)],
                               axis=0)

    live = np.zeros((SMALL_ROWS, D_MODEL), np.float32)
    live[0:4] = 1.0
    live[5, 0:B_Q_HEADS] = 1.0
    live[8:16, 0:REL_TABLE] = 1.0
    small_g = small * jnp.asarray(live)
    sw = pack(ffn1_norm, mix_norm, ffn2_norm, final_norm, sinks, rel_bias)
    sm = pack(m_ffn1_norm, m_mix_norm, m_ffn2_norm, m_final_norm, m_sinks, m_rel_bias)
    sv = pack(v_ffn1_norm, v_mix_norm, v_ffn2_norm, v_final_norm, v_sinks, v_rel_bias)
    (sd,), (snm,), (snv,) = _adamw([sw], [small_g], [sm], [sv], "adamw_small")

    def unpack(p):
        return {"ffn1_norm": p[0], "mix_norm": p[1], "ffn2_norm": p[2], "final_norm": p[3],
                "sinks": p[5, 0:B_Q_HEADS], "rel_bias": p[8:16, 0:REL_TABLE]}

    grads.update(unpack(small_g))
    delta, new_m, new_v = unpack(sd), unpack(snm), unpack(snv)

    wmv = {
        "ffn1_w_gate": (ffn1_w_gate, m_ffn1_w_gate, v_ffn1_w_gate), "ffn1_w_up": (ffn1_w_up, m_ffn1_w_up, v_ffn1_w_up),
        "ffn1_w_down": (ffn1_w_down, m_ffn1_w_down, v_ffn1_w_down), "w_in": (w_in, m_w_in, v_w_in),
        "w_proj_a": (w_proj_a, m_w_proj_a, v_w_proj_a), "w_proj_b": (w_proj_b, m_w_proj_b, v_w_proj_b),
        "w_out": (w_out, m_w_out, v_w_out),
        "ffn2_w_gate": (ffn2_w_gate, m_ffn2_w_gate, v_ffn2_w_gate), "ffn2_w_up": (ffn2_w_up, m_ffn2_w_up, v_ffn2_w_up),
        "ffn2_w_down": (ffn2_w_down, m_ffn2_w_down, v_ffn2_w_down),
    }
    groups = [("adamw_ffn_up", ["ffn1_w_gate", "ffn1_w_up", "ffn2_w_gate", "ffn2_w_up"]),
              ("adamw_ffn_down", ["ffn1_w_down", "ffn2_w_down"]), ("adamw_w_in", ["w_in"]),
              ("adamw_proj", ["w_proj_a", "w_proj_b"]), ("adamw_w_out", ["w_out"])]
    for gname, names in groups:
        ds_, ms_, vs_ = _adamw([wmv[n][0] for n in names], [grads[n] for n in names], [wmv[n][1] for n in names],
                               [wmv[n][2] for n in names], gname)
        for n, d_, m_, v_ in zip(names, ds_, ms_, vs_):
            delta[n], new_m[n], new_v[n] = d_, m_, v_

    order = ["ffn1_norm", "ffn1_w_gate", "ffn1_w_up", "ffn1_w_down", "mix_norm", "w_in", "rel_bias", "sinks",
             "w_proj_a", "w_proj_b", "w_out", "ffn2_norm", "ffn2_w_gate", "ffn2_w_up", "ffn2_w_down", "final_norm"]
    grad_x = dx0.reshape(bsz, s_len, D_MODEL)
    return (loss, grad_x, *[grads[n] for n in order], *[delta[n] for n in order], *[new_m[n] for n in order],
            *[new_v[n] for n in order])
```

```python
import numpy as np
import jax
import jax.numpy as jnp
from jax import lax
from jax.experimental import pallas as pl
from jax.experimental.pallas import tpu as pltpu

F32 = jnp.float32
BF16 = jnp.bfloat16

D_MODEL = 1024
D_FF = 2816
CHUNK = 64
D_HEAD = 64
A_HEADS = 8
A_PREV = 8
MAX_REL = 128
B_Q_HEADS = 8
B_KV_HEADS = 2
B_GROUP = B_Q_HEADS // B_KV_HEADS
B_PREV = 2
REL_TABLE = (CHUNK - 1) + MAX_REL + 1
A_WIDTH = A_HEADS * D_HEAD
B_Q_WIDTH = B_Q_HEADS * D_HEAD
B_KV_WIDTH = B_KV_HEADS * D_HEAD
QKV_A = 3 * A_WIDTH
QKV_B = B_Q_WIDTH + 2 * B_KV_WIDTH
IN_WIDTH = QKV_A + QKV_B + 2 * D_MODEL
EPS = 1e-6
NEG_INF = -1e30
SCALE = 1.0 / 8.0

ADAM_LR = 0.001
ADAM_B1 = 0.9
ADAM_B2 = 0.999
ADAM_EPS = 1e-08
ADAM_WD = 0.01
ADAM_STEP = 10

N_DEV = 8
N_CHIP = 4
MESH = pl.DeviceIdType.MESH

LANES = 128
TQ = 256
TM = 256
FC = 256
VMEM_LIMIT = 56 << 20


def _cparams(sem, vmem=VMEM_LIMIT):
    return pltpu.CompilerParams(dimension_semantics=sem, vmem_limit_bytes=vmem)


def _dot_nt(a, b):
    return lax.dot_general(a, b, (((1,), (1,)), ((), ())), preferred_element_type=F32)


def _dot_nn(a, b):
    return lax.dot_general(a, b, (((1,), (0,)), ((), ())), preferred_element_type=F32)


def _dot_tn(a, b):
    return lax.dot_general(a, b, (((0,), (0,)), ((), ())), preferred_element_type=F32)


def _resident(shape):
    nd = len(shape)
    return pl.BlockSpec(shape, lambda *_: (0,) * nd, pipeline_mode=pl.Buffered(1))


def _rows(tm, width):
    return pl.BlockSpec((tm, width), lambda i: (i, 0))


def _colsum8(v):
    tm, n = v.shape
    return jnp.sum(v.reshape(tm // 8, 8, n), axis=0)


def _rms(x):
    r = lax.rsqrt(jnp.mean(x * x, axis=-1, keepdims=True) + EPS)
    return x * r, r


def _rms_bwd(dh, xh, r, gamma):
    dxh = dh * gamma
    dx = r * (dxh - xh * jnp.mean(dxh * xh, axis=-1, keepdims=True))
    return dx, _colsum8(dh * xh)


def _hbm():
    return pl.BlockSpec(memory_space=pltpu.HBM)


def _call(body, *, name, grid, in_specs, out_specs, out_shape, args, sem, scratch_shapes=(), hosted=()):
    in_specs, out_specs, out_shape = list(in_specs), list(out_specs), list(out_shape)
    scratch_shapes = list(scratch_shapes)
    if not hosted:
        return pl.pallas_call(body, name=name, grid=grid, in_specs=in_specs, out_specs=out_specs, out_shape=out_shape,
                              scratch_shapes=scratch_shapes, compiler_params=_cparams(sem))(*args)
    n_in, n_out, n_scr = len(in_specs), len(out_specs), len(scratch_shapes)
    x_in = [a for x in hosted for a in x.inputs]
    x_out = [s for x in hosted for s in x.out_shape]
    x_scr = [s for x in hosted for s in x.scratch]
    steps = int(np.prod(grid))
    forward_step = max(steps - 3, 0)

    def wrapped(*refs):
        pos = [0]

        def take(k):
            pos[0] += k
            return refs[pos[0] - k:pos[0]]

        ins, xin, outs, xout, scr, xscr = (take(k) for k in (n_in, len(x_in), n_out, len(x_out), n_scr, len(x_scr)))
        step = 0
        for axis, extent in enumerate(grid):
            step = step * extent + pl.program_id(axis)
        own, oi, oo, osc = [], 0, 0, 0
        for x in hosted:
            own.append((xin[oi:oi + len(x.inputs)], xout[oo:oo + len(x.out_shape)], xscr[osc:osc + len(x.scratch)]))
            oi, oo, osc = oi + len(x.inputs), oo + len(x.out_shape), osc + len(x.scratch)

        def phase(method):
            for x, (i_, o_, s_) in zip(hosted, own):
                getattr(x, method)(i_, o_, s_)

        pl.when(step == 0)(lambda: phase("start"))
        body(*ins, *outs, *scr)
        pl.when(step == forward_step)(lambda: phase("forward"))
        pl.when(step == steps - 1)(lambda: phase("finish"))

    res = pl.pallas_call(
        wrapped, name=name, grid=grid, in_specs=in_specs + [_hbm()] * len(x_in),
        out_specs=out_specs + [_hbm()] * len(x_out), out_shape=out_shape + x_out,
        scratch_shapes=scratch_shapes + x_scr, compiler_params=_cparams(("arbitrary",) * len(grid)))(*args, *x_in)
    rest = list(res[n_out:])
    for x in hosted:
        x.result, rest = rest[:len(x.out_shape)], rest[len(x.out_shape):]
    return list(res[:n_out])


def _ffn_fwd(x, gamma, wg_t, wu_t, wd, name, hosted=()):
    t = x.shape[0]
    f = wg_t.shape[0]

    def body(x_ref, gam_ref, wg_ref, wu_ref, wd_ref, h_ref, g_ref, u_ref, a_ref, y_ref):
        xv = x_ref[...]
        xh, _ = _rms(xv)
        h = (xh * gam_ref[...]).astype(BF16)
        h_ref[...] = h
        for j in range(f // FC):
            sl = slice(j * FC, (j + 1) * FC)
            g = _dot_nt(h, wg_ref[sl, :])
            u = _dot_nt(h, wu_ref[sl, :])
            g_ref[:, sl] = g.astype(BF16)
            u_ref[:, sl] = u.astype(BF16)
            a_ref[:, sl] = (g * jax.nn.sigmoid(g) * u).astype(BF16)
        y_ref[...] = xv + 0.5 * _dot_nn(a_ref[...], wd_ref[...])

    return _call(
        body,
        name=name,
        grid=(t // TM,),
        in_specs=[_rows(TM, D_MODEL), _resident((1, D_MODEL)), _resident((f, D_MODEL)), _resident((f, D_MODEL)),
                  _resident((f, D_MODEL))],
        out_specs=[_rows(TM, D_MODEL), _rows(TM, f), _rows(TM, f), _rows(TM, f), _rows(TM, D_MODEL)],
        out_shape=[jax.ShapeDtypeStruct((t, D_MODEL), BF16), jax.ShapeDtypeStruct((t, f), BF16),
                   jax.ShapeDtypeStruct((t, f), BF16), jax.ShapeDtypeStruct((t, f), BF16),
                   jax.ShapeDtypeStruct((t, D_MODEL), F32)],
        args=(x, gamma, wg_t, wu_t, wd), sem=("parallel",), hosted=hosted)


def _ffn_bwd(d, x, gamma, g_act, u_act, wg_t, wu_t, wd, name, hosted=()):
    t = x.shape[0]
    f = wg_t.shape[0]

    def body(d_ref, x_ref, gam_ref, g_ref, u_ref, wg_ref, wu_ref, wd_ref, dx_ref, dg_ref, du_ref, db_ref, dgam_ref):
        dv = d_ref[...]
        db = (0.5 * dv).astype(BF16)
        db_ref[...] = db
        for j in range(f // FC):
            sl = slice(j * FC, (j + 1) * FC)
            da = _dot_nt(db, wd_ref[sl, :])
            g = g_ref[:, sl].astype(F32)
            u = u_ref[:, sl].astype(F32)
            s = jax.nn.sigmoid(g)
            dg_ref[:, sl] = (da * u * (s * (1.0 + g * (1.0 - s)))).astype(BF16)
            du_ref[:, sl] = (da * (g * s)).astype(BF16)
        dh = _dot_nn(dg_ref[...], wg_ref[...]) + _dot_nn(du_ref[...], wu_ref[...])
        xh, r = _rms(x_ref[...])
        dxn, dgam = _rms_bwd(dh, xh, r, gam_ref[...])
        dx_ref[...] = dv + dxn

        @pl.when(pl.program_id(0) == 0)
        def _():
            dgam_ref[...] = jnp.zeros_like(dgam_ref)

        dgam_ref[...] += dgam

    return _call(
        body,
        name=name,
        grid=(t // TM,),
        in_specs=[_rows(TM, D_MODEL), _rows(TM, D_MODEL), _resident((1, D_MODEL)), _rows(TM, f), _rows(TM, f),
                  _resident((f, D_MODEL)), _resident((f, D_MODEL)), _resident((f, D_MODEL))],
        out_specs=[_rows(TM, D_MODEL), _rows(TM, f), _rows(TM, f), _rows(TM, D_MODEL),
                   pl.BlockSpec((8, D_MODEL), lambda i: (0, 0))],
        out_shape=[jax.ShapeDtypeStruct((t, D_MODEL), F32), jax.ShapeDtypeStruct((t, f), BF16),
                   jax.ShapeDtypeStruct((t, f), BF16), jax.ShapeDtypeStruct((t, D_MODEL), BF16),
                   jax.ShapeDtypeStruct((8, D_MODEL), F32)],
        args=(d, x, gamma, g_act, u_act, wg_t, wu_t, wd), sem=("arbitrary",), hosted=hosted)


def _mm_tn(pieces, b, name, tile=256, hosted=()):
    t, n = b.shape
    npc = len(pieces)
    counts = [p.shape[1] // tile for p in pieces]
    los = [sum(counts[:k]) for k in range(npc)]
    total = sum(counts)

    def body(*refs):
        a_refs, b_ref, o_ref = refs[:npc], refs[npc], refs[npc + 1]
        i = pl.program_id(0)
        for k in range(npc):
            @pl.when(jnp.logical_and(i >= los[k], i < los[k] + counts[k]))
            def _(k=k):
                o_ref[...] = _dot_tn(a_refs[k][...], b_ref[...]).astype(BF16)

    def a_spec(k):
        return pl.BlockSpec((t, tile), lambda i: (0, jnp.clip(i - los[k], 0, counts[k] - 1)))

    return _call(
        body,
        name=name,
        grid=(total,),
        in_specs=[a_spec(k) for k in range(npc)] + [_resident((t, n))],
        out_specs=[pl.BlockSpec((tile, n), lambda i: (i, 0))],
        out_shape=[jax.ShapeDtypeStruct((total * tile, n), BF16)],
        args=(*pieces, b), sem=("parallel",), hosted=hosted)[0]


def _mm_tn_proj(dya, dyb, oa, ob, tile=256):
    t = dya.shape[0]

    def body(dya_ref, dyb_ref, oa_ref, ob_ref, o_ref):
        o_ref[:, 0:A_WIDTH] = _dot_tn(dya_ref[...], oa_ref[...]).astype(BF16)
        o_ref[:, A_WIDTH:A_WIDTH + B_Q_WIDTH] = _dot_tn(dyb_ref[...], ob_ref[...]).astype(BF16)

    col = pl.BlockSpec((t, tile), lambda i: (0, i))
    return pl.pallas_call(
        body,
        name="grad_proj",
        grid=(D_MODEL // tile,),
        in_specs=[col, col, _resident((t, A_WIDTH)), _resident((t, B_Q_WIDTH))],
        out_specs=pl.BlockSpec((tile, A_WIDTH + B_Q_WIDTH), lambda i: (i, 0)),
        out_shape=jax.ShapeDtypeStruct((D_MODEL, A_WIDTH + B_Q_WIDTH), BF16),
        compiler_params=_cparams(("parallel",)),
    )(dya, dyb, oa, ob)


def _proj_fwd(x, gamma, win_t, hosted=()):
    t = x.shape[0]

    def body(x_ref, gam_ref, w_ref, h_ref, qa_ref, qb_ref, gt_ref):
        xh, _ = _rms(x_ref[...])
        h = (xh * gam_ref[...]).astype(BF16)
        h_ref[...] = h
        for j in range(QKV_A // FC):
            qa_ref[:, j * FC:(j + 1) * FC] = _dot_nt(h, w_ref[j * FC:(j + 1) * FC, :]).astype(BF16)
        for j in range(QKV_B // FC):
            lo = QKV_A + j * FC
            qb_ref[:, j * FC:(j + 1) * FC] = _dot_nt(h, w_ref[lo:lo + FC, :]).astype(BF16)
        for j in range(2 * D_MODEL // FC):
            lo = QKV_A + QKV_B + j * FC
            gt_ref[:, j * FC:(j + 1) * FC] = _dot_nt(h, w_ref[lo:lo + FC, :])

    return _call(
        body,
        name="proj_fwd",
        grid=(t // TM,),
        in_specs=[_rows(TM, D_MODEL), _resident((1, D_MODEL)), _resident((IN_WIDTH, D_MODEL))],
        out_specs=[_rows(TM, D_MODEL), _rows(TM, QKV_A), _rows(TM, QKV_B), _rows(TM, 2 * D_MODEL)],
        out_shape=[jax.ShapeDtypeStruct((t, D_MODEL), BF16), jax.ShapeDtypeStruct((t, QKV_A), BF16),
                   jax.ShapeDtypeStruct((t, QKV_B), BF16), jax.ShapeDtypeStruct((t, 2 * D_MODEL), F32)],
        args=(x, gamma, win_t), sem=("parallel",), hosted=hosted)


def _proj_bwd(d, x, gamma, pieces, win_t, hosted=()):
    t = x.shape[0]
    npc = len(pieces)
    widths = [p.shape[1] for p in pieces]
    los = [sum(widths[:k]) for k in range(npc)]

    def body(*refs):
        d_ref, x_ref, gam_ref = refs[:3]
        p_refs = refs[3:3 + npc]
        w_ref, dx_ref, db_ref, dgam_ref = refs[3 + npc:]
        dh = _dot_nn(p_refs[0][...], w_ref[0:widths[0], :])
        for k in range(1, npc):
            dh += _dot_nn(p_refs[k][...], w_ref[los[k]:los[k] + widths[k], :])
        xh, r = _rms(x_ref[...])
        dxn, dgam = _rms_bwd(dh, xh, r, gam_ref[...])
        dx = d_ref[...] + dxn
        dx_ref[...] = dx
        db_ref[...] = (0.5 * dx).astype(BF16)

        @pl.when(pl.program_id(0) == 0)
        def _():
            dgam_ref[...] = jnp.zeros_like(dgam_ref)

        dgam_ref[...] += dgam

    return _call(
        body,
        name="proj_bwd",
        grid=(t // TM,),
        in_specs=[_rows(TM, D_MODEL), _rows(TM, D_MODEL), _resident((1, D_MODEL))] + [_rows(TM, w) for w in widths]
        + [_resident((IN_WIDTH, D_MODEL))],
        out_specs=[_rows(TM, D_MODEL), _rows(TM, D_MODEL), pl.BlockSpec((8, D_MODEL), lambda i: (0, 0))],
        out_shape=[jax.ShapeDtypeStruct((t, D_MODEL), F32), jax.ShapeDtypeStruct((t, D_MODEL), BF16),
                   jax.ShapeDtypeStruct((8, D_MODEL), F32)],
        args=(d, x, gamma, *pieces, win_t), sem=("arbitrary",), hosted=hosted)


def _lane_half(shape):
    return lax.broadcasted_iota(jnp.int32, shape, len(shape) - 1) // D_HEAD


def _band_softmax(q, kk, bias, sink, qs, pad):
    s = _dot_nt(q, kk) * SCALE + bias
    col = lax.broadcasted_iota(jnp.int32, s.shape, 1)
    s = jnp.where(col + qs >= pad, s, NEG_INF)
    m = jnp.max(s, axis=-1, keepdims=True)
    if sink is not None:
        m = jnp.maximum(m, sink)
    p = jnp.exp(s - m)
    den = jnp.sum(p, axis=-1, keepdims=True)
    if sink is not None:
        den = den + jnp.exp(sink - m)
    return p, m, 1.0 / den


def _fill_padded(dst, src, pad):
    dst[0:pad, :] = jnp.zeros((pad, LANES), dst.dtype)
    dst[pad:, :] = src


def _attn_a_fwd(qkv, bias, hosted=()):
    bsz, s_len, _ = qkv.shape
    pad = A_PREV * CHUNK
    band = TQ + pad
    npair = A_HEADS // 2

    def body(q_ref, k_ref, v_ref, b_ref, o_ref, kp, vp):
        i = pl.program_id(2)

        @pl.when(i == 0)
        def _():
            _fill_padded(kp, k_ref[...], pad)
            _fill_padded(vp, v_ref[...], pad)

        qs = pl.multiple_of(i * TQ, TQ)
        kk = kp[pl.ds(qs, band), :]
        vv = vp[pl.ds(qs, band), :]
        q = q_ref[...]
        half = _lane_half((1, LANES))
        outs = []
        for j in range(2):
            qm = jnp.where(half == j, q, jnp.zeros_like(q))
            p, _, inv = _band_softmax(qm, kk, b_ref[j], None, qs, pad)
            outs.append(_dot_nn(p.astype(BF16), vv) * inv)
        o_ref[...] = jnp.where(half == 0, outs[0], outs[1]).astype(BF16)

    return _call(
        body,
        name="attn_a_fwd",
        grid=(bsz, npair, s_len // TQ),
        in_specs=[pl.BlockSpec((None, TQ, LANES), lambda b, hp, i: (b, i, hp)),
                  pl.BlockSpec((None, s_len, LANES), lambda b, hp, i: (b, 0, npair + hp)),
                  pl.BlockSpec((None, s_len, LANES), lambda b, hp, i: (b, 0, 2 * npair + hp)),
                  pl.BlockSpec((2, TQ, band), lambda b, hp, i: (hp, 0, 0))],
        out_specs=[pl.BlockSpec((None, TQ, LANES), lambda b, hp, i: (b, i, hp))],
        out_shape=[jax.ShapeDtypeStruct((bsz, s_len, A_WIDTH), BF16)],
        scratch_shapes=[pltpu.VMEM((pad + s_len, LANES), BF16), pltpu.VMEM((pad + s_len, LANES), BF16)],
        args=(qkv, qkv, qkv, bias), sem=("arbitrary", "arbitrary", "arbitrary"), hosted=hosted)[0]


def _attn_a_bwd(qkv, bias, do, hosted=()):
    bsz, s_len, _ = qkv.shape
    pad = A_PREV * CHUNK
    band = TQ + pad
    npair = A_HEADS // 2
    n_i = s_len // TQ

    def body(q_ref, k_ref, v_ref, b_ref, do_ref, dq_ref, dk_ref, dv_ref, dbias_ref, kp, vp, dk_acc, dv_acc):
        b = pl.program_id(1)
        i = pl.program_id(2)

        @pl.when(i == 0)
        def _():
            _fill_padded(kp, k_ref[...], pad)
            _fill_padded(vp, v_ref[...], pad)
            dk_acc[...] = jnp.zeros_like(dk_acc)
            dv_acc[...] = jnp.zeros_like(dv_acc)

        @pl.when(jnp.logical_and(b == 0, i == 0))
        def _():
            dbias_ref[...] = jnp.zeros_like(dbias_ref)

        qs = pl.multiple_of(i * TQ, TQ)
        kk = kp[pl.ds(qs, band), :]
        vv = vp[pl.ds(qs, band), :]
        q = q_ref[...]
        dd = do_ref[...]
        half = _lane_half((1, LANES))
        dqs, dks, dvs = [], [], []
        for j in range(2):
            qm = jnp.where(half == j, q, jnp.zeros_like(q))
            dm = jnp.where(half == j, dd, jnp.zeros_like(dd))
            p, _, inv = _band_softmax(qm, kk, b_ref[j], None, qs, pad)
            pn = p * inv
            dp = _dot_nt(dm, vv)
            delta = jnp.sum(pn * dp, axis=-1, keepdims=True)
            ds = pn * (dp - delta)
            dbias_ref[j] += ds
            dsb = ds.astype(BF16)
            dqs.append(_dot_nn(dsb, kk))
            dks.append(_dot_tn(dsb, q))
            dvs.append(_dot_tn(pn.astype(BF16), dd))
        dq_ref[...] = (jnp.where(half == 0, dqs[0], dqs[1]) * SCALE).astype(BF16)
        dk_acc[pl.ds(qs, band), :] += jnp.where(half == 0, dks[0], dks[1]) * SCALE
        dv_acc[pl.ds(qs, band), :] += jnp.where(half == 0, dvs[0], dvs[1])

        @pl.when(i == n_i - 1)
        def _():
            dk_ref[...] = dk_acc[pad:, :].astype(BF16)
            dv_ref[...] = dv_acc[pad:, :].astype(BF16)

    qspec = pl.BlockSpec((None, TQ, LANES), lambda hp, b, i: (b, i, hp))
    kvout = pl.BlockSpec((None, s_len, LANES), lambda hp, b, i: (b, 0, hp))
    wide = jax.ShapeDtypeStruct((bsz, s_len, A_WIDTH), BF16)
    return _call(
        body,
        name="attn_a_bwd",
        grid=(npair, bsz, n_i),
        in_specs=[qspec,
                  pl.BlockSpec((None, s_len, LANES), lambda hp, b, i: (b, 0, npair + hp)),
                  pl.BlockSpec((None, s_len, LANES), lambda hp, b, i: (b, 0, 2 * npair + hp)),
                  pl.BlockSpec((2, TQ, band), lambda hp, b, i: (hp, 0, 0)),
                  qspec],
        out_specs=[qspec, kvout, kvout, pl.BlockSpec((2, TQ, band), lambda hp, b, i: (hp, 0, 0))],
        out_shape=[wide, wide, wide, jax.ShapeDtypeStruct((A_HEADS, TQ, band), F32)],
        scratch_shapes=[pltpu.VMEM((pad + s_len, LANES), BF16), pltpu.VMEM((pad + s_len, LANES), BF16),
                        pltpu.VMEM((pad + s_len, LANES), F32), pltpu.VMEM((pad + s_len, LANES), F32)],
        args=(qkv, qkv, qkv, bias, do), sem=("arbitrary", "arbitrary", "arbitrary"), hosted=hosted)


def _stack_group(x, half):
    parts = []
    for g in range(B_GROUP):
        blk = x[:, LANES * (g // 2):LANES * (g // 2) + LANES]
        parts.append(jnp.where(half == (g % 2), blk, jnp.zeros_like(blk)))
    return jnp.concatenate(parts, axis=0)


def _unstack_group(y, half):
    left = jnp.where(half == 0, y[0:TQ], y[TQ:2 * TQ])
    right = jnp.where(half == 0, y[2 * TQ:3 * TQ], y[3 * TQ:4 * TQ])
    return left, right


def _fill_padded_dup(dst, src, pad, h, half):
    other = pltpu.roll(src, D_HEAD, 1)
    _fill_padded(dst, jnp.where(half == h, src, other), pad)


def _attn_b_fwd(qkv, bias, sink):
    bsz, s_len, _ = qkv.shape
    pad = B_PREV * CHUNK
    band = TQ + pad
    kcol = B_Q_WIDTH // LANES

    def body(q_ref, k_ref, v_ref, b_ref, s_ref, o_ref, kp, vp):
        h = pl.program_id(1)
        i = pl.program_id(2)
        half = _lane_half((1, LANES))

        @pl.when(i == 0)
        def _():
            _fill_padded_dup(kp, k_ref[...], pad, h, half)
            _fill_padded_dup(vp, v_ref[...], pad, h, half)

        qs = pl.multiple_of(i * TQ, TQ)
        kk = kp[pl.ds(qs, band), :]
        vv = vp[pl.ds(qs, band), :]
        qq = _stack_group(q_ref[...], half)
        p, _, inv = _band_softmax(qq, kk, b_ref[...], s_ref[...], qs, pad)
        o = _dot_nn(p.astype(BF16), vv) * inv
        left, right = _unstack_group(o, half)
        o_ref[:, 0:LANES] = left.astype(BF16)
        o_ref[:, LANES:2 * LANES] = right.astype(BF16)

    return pl.pallas_call(
        body,
        name="attn_b_fwd",
        grid=(bsz, B_KV_HEADS, s_len // TQ),
        in_specs=[pl.BlockSpec((None, TQ, 2 * LANES), lambda b, h, i: (b, i, h)),
                  pl.BlockSpec((None, s_len, LANES), lambda b, h, i: (b, 0, kcol)),
                  pl.BlockSpec((None, s_len, LANES), lambda b, h, i: (b, 0, kcol + 1)),
                  pl.BlockSpec((None, B_GROUP * TQ, band), lambda b, h, i: (h, 0, 0)),
                  pl.BlockSpec((None, B_GROUP * TQ, 1), lambda b, h, i: (h, 0, 0))],
        out_specs=pl.BlockSpec((None, TQ, 2 * LANES), lambda b, h, i: (b, i, h)),
        out_shape=jax.ShapeDtypeStruct((bsz, s_len, B_Q_WIDTH), BF16),
        scratch_shapes=[pltpu.VMEM((pad + s_len, LANES), BF16), pltpu.VMEM((pad + s_len, LANES), BF16)],
        compiler_params=_cparams(("arbitrary", "arbitrary", "arbitrary")),
    )(qkv, qkv, qkv, bias, sink)


def _attn_b_bwd(qkv, bias, sink, do, hosted=()):
    bsz, s_len, _ = qkv.shape
    pad = B_PREV * CHUNK
    band = TQ + pad
    kcol = B_Q_WIDTH // LANES
    n_i = s_len // TQ

    def body(q_ref, k_ref, v_ref, b_ref, s_ref, do_ref, dq_ref, dkv_ref, dsink_ref, kp, vp, dk_acc, dv_acc):
        b = pl.program_id(0)
        h = pl.program_id(1)
        i = pl.program_id(2)
        half = _lane_half((1, LANES))

        @pl.when(i == 0)
        def _():
            _fill_padded_dup(kp, k_ref[...], pad, h, half)
            _fill_padded_dup(vp, v_ref[...], pad, h, half)

        @pl.when(jnp.logical_and(h == 0, i == 0))
        def _():
            dk_acc[...] = jnp.zeros_like(dk_acc)
            dv_acc[...] = jnp.zeros_like(dv_acc)

        @pl.when(jnp.logical_and(b == 0, jnp.logical_and(h == 0, i == 0)))
        def _():
            dsink_ref[...] = jnp.zeros_like(dsink_ref)

        qs = pl.multiple_of(i * TQ, TQ)
        kk = kp[pl.ds(qs, band), :]
        vv = vp[pl.ds(qs, band), :]
        qq = _stack_group(q_ref[...], half)
        dd = _stack_group(do_ref[...], half)
        sink_rows = s_ref[...]
        p, m, inv = _band_softmax(qq, kk, b_ref[...], sink_rows, qs, pad)
        pn = p * inv
        dp = _dot_nt(dd, vv)
        delta = jnp.sum(pn * dp, axis=-1, keepdims=True)
        ds = pn * (dp - delta)
        dsb = ds.astype(BF16)
        left, right = _unstack_group(_dot_nn(dsb, kk) * SCALE, half)
        dq_ref[:, 0:LANES] = left.astype(BF16)
        dq_ref[:, LANES:2 * LANES] = right.astype(BF16)
        dk2 = _dot_tn(dsb, qq) * SCALE
        dv2 = _dot_tn(pn.astype(BF16), dd)
        dk_acc[pl.ds(qs, band), :] += jnp.where(half == h, dk2 + pltpu.roll(dk2, D_HEAD, 1), 0.0)
        dv_acc[pl.ds(qs, band), :] += jnp.where(half == h, dv2 + pltpu.roll(dv2, D_HEAD, 1), 0.0)

        @pl.when(jnp.logical_and(h == B_KV_HEADS - 1, i == n_i - 1))
        def _():
            dkv_ref[:, 0:LANES] = dk_acc[pad:, :].astype(BF16)
            dkv_ref[:, LANES:2 * LANES] = dv_acc[pad:, :].astype(BF16)

        dsk = -(jnp.exp(sink_rows - m) * inv) * delta
        row = lax.broadcasted_iota(jnp.int32, (8, LANES), 0)
        upd = jnp.zeros((8, LANES), F32)
        for g in range(B_GROUP):
            tot = jnp.sum(dsk[g * TQ:(g + 1) * TQ, :], axis=0, keepdims=True)
            upd = upd + jnp.where(row == g, tot, 0.0)
        dsink_ref[h] += upd

    qspec = pl.BlockSpec((None, TQ, 2 * LANES), lambda b, h, i: (b, i, h))
    return _call(
        body,
        name="attn_b_bwd",
        grid=(bsz, B_KV_HEADS, n_i),
        in_specs=[qspec,
                  pl.BlockSpec((None, s_len, LANES), lambda b, h, i: (b, 0, kcol)),
                  pl.BlockSpec((None, s_len, LANES), lambda b, h, i: (b, 0, kcol + 1)),
                  pl.BlockSpec((None, B_GROUP * TQ, band), lambda b, h, i: (h, 0, 0)),
                  pl.BlockSpec((None, B_GROUP * TQ, 1), lambda b, h, i: (h, 0, 0)),
                  qspec],
        out_specs=[qspec, pl.BlockSpec((None, s_len, 2 * LANES), lambda b, h, i: (b, 0, 0)),
                   pl.BlockSpec((B_KV_HEADS, 8, LANES), lambda b, h, i: (0, 0, 0))],
        out_shape=[jax.ShapeDtypeStruct((bsz, s_len, B_Q_WIDTH), BF16),
                   jax.ShapeDtypeStruct((bsz, s_len, 2 * B_KV_WIDTH), BF16),
                   jax.ShapeDtypeStruct((B_KV_HEADS, 8, LANES), F32)],
        scratch_shapes=[pltpu.VMEM((pad + s_len, LANES), BF16), pltpu.VMEM((pad + s_len, LANES), BF16),
                        pltpu.VMEM((pad + s_len, LANES), F32), pltpu.VMEM((pad + s_len, LANES), F32)],
        args=(qkv, qkv, qkv, bias, sink, do), sem=("arbitrary", "arbitrary", "arbitrary"), hosted=hosted)


REL_COLS = 3 * 128
REL_WRAP = 512


def _bias_a_build(tv):
    h = tv.shape[0]
    pad = A_PREV * CHUNK
    band = TQ + pad

    def body(tv_ref, o_ref):
        row = tv_ref[...]
        x = jnp.broadcast_to(row, (TQ, REL_WRAP))
        r = lax.broadcasted_iota(jnp.int32, x.shape, 0)
        for bit in range(8):
            sh = 1 << bit
            x = jnp.where((r & sh) != 0, pltpu.roll(x, sh, 1), x)
        far = jnp.broadcast_to(row[:, 0:1], (TQ, band - REL_COLS))
        full = jnp.concatenate([far, x[:, REL_WRAP // 2:REL_WRAP], x[:, 0:REL_COLS - REL_WRAP // 2]], axis=1)
        qc = (lax.broadcasted_iota(jnp.int32, full.shape, 0) + pad) // CHUNK
        kc = lax.broadcasted_iota(jnp.int32, full.shape, 1) // CHUNK
        ok = jnp.logical_and(kc <= qc, kc >= qc - A_PREV)
        o_ref[...] = jnp.where(ok, full, NEG_INF)

    return pl.pallas_call(
        body,
        name="bias_a_build",
        grid=(h,),
        in_specs=[pl.BlockSpec((None, 1, REL_WRAP), lambda hh: (hh, 0, 0))],
        out_specs=pl.BlockSpec((None, TQ, band), lambda hh: (hh, 0, 0)),
        out_shape=jax.ShapeDtypeStruct((h, TQ, band), F32),
        compiler_params=_cparams(("parallel",)),
    )(tv)


def _relbias_grad(dbias):
    h, rows, band = dbias.shape
    off = band - REL_COLS

    def body(d_ref, o_ref):
        x = d_ref[...]
        r = lax.broadcasted_iota(jnp.int32, x.shape, 0)
        c = lax.broadcasted_iota(jnp.int32, x.shape, 1) - r
        x = jnp.where(jnp.logical_and(c >= 1, c < REL_TABLE), x, 0.0)
        for bit in range(8):
            sh = 1 << bit
            x = jnp.where((r & sh) != 0, pltpu.roll(x, REL_COLS - sh, 1), x)
        diag = jnp.sum(x, axis=0, keepdims=True)
        lane = lax.broadcasted_iota(jnp.int32, diag.shape, 1)
        diag = jnp.where(jnp.logical_and(lane >= 1, lane < REL_TABLE), diag, 0.0)
        rest = -jnp.sum(diag, axis=1, keepdims=True)
        o_ref[...] = jnp.broadcast_to(jnp.where(lane == 0, rest, diag), o_ref.shape)

    return pl.pallas_call(
        body,
        name="relbias_grad",
        grid=(h,),
        in_specs=[pl.BlockSpec((None, rows, REL_COLS), lambda hh: (hh, 0, off // REL_COLS))],
        out_specs=pl.BlockSpec((None, 8, REL_COLS), lambda hh: (hh, 0, 0)),
        out_shape=jax.ShapeDtypeStruct((h, 8, REL_COLS), F32),
        compiler_params=_cparams(("parallel",)),
    )(dbias)


def _mix_out_fwd(x, oa, ob, gates, proj_t, wout):
    t = x.shape[0]

    def body(x_ref, oa_ref, ob_ref, gt_ref, pt_ref, wo_ref, y_ref, ya_ref, yb_ref, mg_ref):
        ya = _dot_nt(oa_ref[...], pt_ref[:, 0:A_WIDTH])
        yb = _dot_nt(ob_ref[...], pt_ref[:, A_WIDTH:A_WIDTH + B_Q_WIDTH])
        ya_ref[...] = ya.astype(BF16)
        yb_ref[...] = yb.astype(BF16)
        mg = jax.nn.sigmoid(gt_ref[:, 0:D_MODEL]) * ya + jax.nn.sigmoid(gt_ref[:, D_MODEL:2 * D_MODEL]) * yb
        mgb = mg.astype(BF16)
        mg_ref[...] = mgb
        y_ref[...] = x_ref[...] + _dot_nn(mgb, wo_ref[...])

    return pl.pallas_call(
        body,
        name="mix_out_fwd",
        grid=(t // TM,),
        in_specs=[_rows(TM, D_MODEL), _rows(TM, A_WIDTH), _rows(TM, B_Q_WIDTH), _rows(TM, 2 * D_MODEL),
                  _resident((D_MODEL, A_WIDTH + B_Q_WIDTH)), _resident((D_MODEL, D_MODEL))],
        out_specs=[_rows(TM, D_MODEL), _rows(TM, D_MODEL), _rows(TM, D_MODEL), _rows(TM, D_MODEL)],
        out_shape=[jax.ShapeDtypeStruct((t, D_MODEL), F32), jax.ShapeDtypeStruct((t, D_MODEL), BF16),
                   jax.ShapeDtypeStruct((t, D_MODEL), BF16), jax.ShapeDtypeStruct((t, D_MODEL), BF16)],
        compiler_params=_cparams(("parallel",)),
    )(x, oa, ob, gates, proj_t, wout)


def _mix_out_bwd(d, gates, ya, yb, proj_t, wout, hosted=()):
    t = d.shape[0]

    def body(d_ref, gt_ref, ya_ref, yb_ref, pt_ref, wo_ref, db_ref, dya_ref, dyb_ref, doa_ref, dob_ref, dgt_ref):
        db = d_ref[...].astype(BF16)
        db_ref[...] = db
        dmg = _dot_nt(db, wo_ref[...])
        sa = jax.nn.sigmoid(gt_ref[:, 0:D_MODEL])
        sb = jax.nn.sigmoid(gt_ref[:, D_MODEL:2 * D_MODEL])
        dya = (dmg * sa).astype(BF16)
        dyb = (dmg * sb).astype(BF16)
        dya_ref[...] = dya
        dyb_ref[...] = dyb
        dgt_ref[:, 0:D_MODEL] = (dmg * ya_ref[...].astype(F32) * (sa * (1.0 - sa))).astype(BF16)
        dgt_ref[:, D_MODEL:2 * D_MODEL] = (dmg * yb_ref[...].astype(F32) * (sb * (1.0 - sb))).astype(BF16)
        doa_ref[...] = _dot_nn(dya, pt_ref[:, 0:A_WIDTH]).astype(BF16)
        dob_ref[...] = _dot_nn(dyb, pt_ref[:, A_WIDTH:A_WIDTH + B_Q_WIDTH]).astype(BF16)

    return _call(
        body,
        name="mix_out_bwd",
        grid=(t // TM,),
        in_specs=[_rows(TM, D_MODEL), _rows(TM, 2 * D_MODEL), _rows(TM, D_MODEL), _rows(TM, D_MODEL),
                  _resident((D_MODEL, A_WIDTH + B_Q_WIDTH)), _resident((D_MODEL, D_MODEL))],
        out_specs=[_rows(TM, D_MODEL), _rows(TM, D_MODEL), _rows(TM, D_MODEL), _rows(TM, A_WIDTH),
                   _rows(TM, B_Q_WIDTH), _rows(TM, 2 * D_MODEL)],
        out_shape=[jax.ShapeDtypeStruct((t, D_MODEL), BF16), jax.ShapeDtypeStruct((t, D_MODEL), BF16),
                   jax.ShapeDtypeStruct((t, D_MODEL), BF16), jax.ShapeDtypeStruct((t, A_WIDTH), BF16),
                   jax.ShapeDtypeStruct((t, B_Q_WIDTH), BF16), jax.ShapeDtypeStruct((t, 2 * D_MODEL), BF16)],
        args=(d, gates, ya, yb, proj_t, wout), sem=("parallel",), hosted=hosted)


def _loss_head(x, gamma, target):
    t = x.shape[0]

    def body(x_ref, gam_ref, t_ref, dx_ref, dgam_ref, loss_ref):
        xh, r = _rms(x_ref[...])
        gam = gam_ref[...]
        e = xh * gam - t_ref[...]
        dy = e * (1.0 / D_MODEL)
        dxn, dgam = _rms_bwd(dy, xh, r, gam)
        dx_ref[...] = dxn

        @pl.when(pl.program_id(0) == 0)
        def _():
            dgam_ref[...] = jnp.zeros_like(dgam_ref)
            loss_ref[...] = jnp.zeros_like(loss_ref)

        dgam_ref[...] += dgam
        loss_ref[...] += _colsum8(e * e) * (0.5 / D_MODEL)

    return pl.pallas_call(
        body,
        name="loss_head",
        grid=(t // TM,),
        in_specs=[_rows(TM, D_MODEL), _resident((1, D_MODEL)), _rows(TM, D_MODEL)],
        out_specs=[_rows(TM, D_MODEL), pl.BlockSpec((8, D_MODEL), lambda i: (0, 0)),
                   pl.BlockSpec((8, D_MODEL), lambda i: (0, 0))],
        out_shape=[jax.ShapeDtypeStruct((t, D_MODEL), F32), jax.ShapeDtypeStruct((8, D_MODEL), F32),
                   jax.ShapeDtypeStruct((8, D_MODEL), F32)],
        compiler_params=_cparams(("arbitrary",)),
    )(x, gamma, target)


def _place():
    x, y, c = lax.axis_index("x"), lax.axis_index("y"), lax.axis_index("c")
    chips = [(1 - x, y), (x, 1 - y), (1 - x, 1 - y)]
    return x, y, c, chips


class _Gather:
    per = 7

    def __init__(self, shards):
        n = len(shards)
        self.inputs = list(shards)
        self.out_shape = [jax.ShapeDtypeStruct((N_DEV * s.shape[0], s.shape[1]), s.dtype) for s in shards]
        self.scratch = [pltpu.SemaphoreType.DMA((n * self.per,)), pltpu.SemaphoreType.DMA((n * self.per,)),
                        pltpu.SemaphoreType.DMA((n,))]
        self.result = None

    def _parts(self, ins, outs, sems):
        send_sems, recv_sems, local_sems = sems
        x, y, c, chips = _place()
        me, sibling = (x, y, c), (x, y, 1 - c)
        n = len(ins)

        def rows(k, p):
            r = ins[k].shape[0]
            return outs[k].at[pl.ds((4 * p[0] + 2 * p[1] + p[2]) * r, r), :]

        def copy(k, slot, block, to, src=None):
            return pltpu.make_async_remote_copy(
                src_ref=rows(k, block) if src is None else src, dst_ref=rows(k, block),
                send_sem=send_sems.at[k * self.per + slot], recv_sem=recv_sems.at[k * self.per + slot],
                device_id=to, device_id_type=MESH)

        mine = [pltpu.make_async_copy(ins[k], rows(k, me), local_sems.at[k]) for k in range(n)]
        first = []
        for k in range(n):
            first.append(copy(k, 0, me, sibling, src=ins[k]))
            first += [copy(k, 1 + j, me, (*chip, c), src=ins[k]) for j, chip in enumerate(chips)]
        passed = [copy(k, 4 + j, (*chip, c), sibling) for j, chip in enumerate(chips) for k in range(n)]
        return n, c, me, sibling, chips, copy, mine, first, passed

    def start(self, ins, outs, sems):
        _, _, _, _, _, _, mine, first, _ = self._parts(ins, outs, sems)
        for cp in mine + first:
            cp.start()

    def forward(self, ins, outs, sems):
        n, c, me, _, chips, copy, _, _, passed = self._parts(ins, outs, sems)
        for j, chip in enumerate(chips):
            for k in range(n):
                copy(k, 1 + j, (*chip, c), me).wait_recv()
                passed[j * n + k].start()

    def finish(self, ins, outs, sems):
        n, c, me, sibling, chips, copy, mine, first, passed = self._parts(ins, outs, sems)
        for k in range(n):
            copy(k, 0, sibling, me).wait_recv()
            for j, chip in enumerate(chips):
                copy(k, 4 + j, (*chip, 1 - c), me).wait_recv()
        for cp in first + passed:
            cp.wait_send()
        for cp in mine:
            cp.wait()


class _PairExchange:
    def __init__(self, grads):
        n = len(grads)
        self.inputs = list(grads)
        self.out_shape = [jax.ShapeDtypeStruct((g.shape[0] // 2, g.shape[1]), g.dtype) for g in grads]
        self.scratch = [pltpu.SemaphoreType.DMA((n * N_CHIP,)), pltpu.SemaphoreType.DMA((n * N_CHIP,))]
        self.result = None

    def _copies(self, ins, outs, sems):
        send_sems, recv_sems = sems
        x, y, c, _ = _place()
        copies = []
        for k in range(len(ins)):
            r = ins[k].shape[0] // N_DEV
            for q in range(N_CHIP):
                copies.append(pltpu.make_async_remote_copy(
                    src_ref=ins[k].at[pl.ds((2 * q + 1 - c) * r, r), :], dst_ref=outs[k].at[pl.ds(q * r, r), :],
                    send_sem=send_sems.at[k * N_CHIP + q], recv_sem=recv_sems.at[k * N_CHIP + q],
                    device_id=(x, y, 1 - c), device_id_type=MESH))
        return copies

    def start(self, ins, outs, sems):
        for cp in self._copies(ins, outs, sems):
            cp.start()

    def forward(self, ins, outs, sems):
        pass

    def finish(self, ins, outs, sems):
        copies = self._copies(ins, outs, sems)
        for cp in copies:
            cp.wait_recv()
        for cp in copies:
            cp.wait_send()


class _ChipExchange(_PairExchange):
    def __init__(self, psums):
        n = len(psums)
        self.inputs = list(psums)
        self.out_shape = [jax.ShapeDtypeStruct((3 * p.shape[0] // N_CHIP, p.shape[1]), p.dtype) for p in psums]
        self.scratch = [pltpu.SemaphoreType.DMA((n * 3,)), pltpu.SemaphoreType.DMA((n * 3,))]
        self.result = None

    def _copies(self, ins, outs, sems):
        send_sems, recv_sems = sems
        _, _, c, chips = _place()
        copies = []
        for k in range(len(ins)):
            r = ins[k].shape[0] // N_CHIP
            for j, chip in enumerate(chips):
                copies.append(pltpu.make_async_remote_copy(
                    src_ref=ins[k].at[pl.ds((2 * chip[0] + chip[1]) * r, r), :], dst_ref=outs[k].at[pl.ds(j * r, r), :],
                    send_sem=send_sems.at[k * 3 + j], recv_sem=recv_sems.at[k * 3 + j],
                    device_id=(*chip, c), device_id_type=MESH))
        return copies


def _exchange_alone(xchg, name):
    n_in, n_out = len(xchg.inputs), len(xchg.out_shape)

    def body(*refs):
        ins, outs, sems = refs[:n_in], refs[n_in:n_in + n_out], refs[n_in + n_out:]
        xchg.start(ins, outs, sems)
        xchg.forward(ins, outs, sems)
        xchg.finish(ins, outs, sems)

    xchg.result = list(pl.pallas_call(
        body, name=name, in_specs=[_hbm()] * n_in, out_specs=[_hbm()] * n_out, out_shape=xchg.out_shape,
        scratch_shapes=xchg.scratch)(*xchg.inputs))
    return xchg.result


def _pair_sum(core, grads, recvd, name):
    n = len(grads)
    r = grads[0].shape[0] // N_DEV
    cdim = grads[0].shape[1]
    tr = r // 2 if r % 32 == 0 else r
    nt = r // tr

    def body(core_ref, *refs):
        del core_ref
        for k in range(n):
            refs[2 * n + k][...] = (refs[k][...].astype(F32) + refs[n + k][...].astype(F32)).astype(BF16)

    gspec = pl.BlockSpec((tr, cdim), lambda q, i, core_ref: ((2 * q + core_ref[0]) * nt + i, 0))
    rspec = pl.BlockSpec((tr, cdim), lambda q, i, core_ref: (q * nt + i, 0))
    return pl.pallas_call(
        body,
        name=name,
        grid_spec=pltpu.PrefetchScalarGridSpec(
            num_scalar_prefetch=1, grid=(N_CHIP, nt), in_specs=[gspec] * n + [rspec] * n, out_specs=[rspec] * n),
        out_shape=[jax.ShapeDtypeStruct((N_CHIP * r, cdim), BF16) for _ in range(n)],
        compiler_params=_cparams(("parallel", "parallel")),
    )(core, *grads, *recvd)


def _final_sum(chip, psums, recvd, name):
    n = len(psums)
    r = psums[0].shape[0] // N_CHIP
    cdim = psums[0].shape[1]
    tr = r // 2 if r % 32 == 0 else r
    nt = r // tr

    def body(chip_ref, *refs):
        del chip_ref
        for k in range(n):
            got = refs[n + k]
            tot = refs[k][...].astype(F32) + got[0].astype(F32)
            tot = tot + got[1].astype(F32)
            tot = tot + got[2].astype(F32)
            refs[2 * n + k][...] = tot

    pspec = pl.BlockSpec((tr, cdim), lambda i, chip_ref: (chip_ref[0] * nt + i, 0))
    rspec = pl.BlockSpec((3, tr, cdim), lambda i, chip_ref: (0, i, 0))
    ospec = pl.BlockSpec((tr, cdim), lambda i, chip_ref: (i, 0))
    return pl.pallas_call(
        body,
        name=name,
        grid_spec=pltpu.PrefetchScalarGridSpec(
            num_scalar_prefetch=1, grid=(nt,), in_specs=[pspec] * n + [rspec] * n, out_specs=[ospec] * n),
        out_shape=[jax.ShapeDtypeStruct((r, cdim), F32) for _ in range(n)],
        compiler_params=_cparams(("parallel",)),
    )(chip, *psums, *[g.reshape(3, r, cdim) for g in recvd])


SMALL_ROWS = 16


def _all_reduce_small(part):
    def body(p_ref, o_ref, buf, send_sems, recv_sems):
        x, y, c, _ = _place()
        me = 4 * x + 2 * y + c
        buf[me] = p_ref[...]
        copies = []
        for d in range(1, N_DEV):
            peer = me ^ d
            copies.append(pltpu.make_async_remote_copy(
                src_ref=p_ref, dst_ref=buf.at[me], send_sem=send_sems.at[d - 1], recv_sem=recv_sems.at[d - 1],
                device_id=(peer // 4, (peer // 2) % 2, peer % 2), device_id_type=MESH))
        for cp in copies:
            cp.start()
        for cp in copies:
            cp.wait_recv()
        for cp in copies:
            cp.wait_send()
        tot = buf[0]
        for d in range(1, N_DEV):
            tot = tot + buf[d]
        o_ref[...] = tot

    return pl.pallas_call(
        body,
        name="all_reduce_small",
        in_specs=[pl.BlockSpec(memory_space=pltpu.VMEM)],
        out_specs=pl.BlockSpec(memory_space=pltpu.VMEM),
        out_shape=jax.ShapeDtypeStruct(part.shape, F32),
        scratch_shapes=[pltpu.VMEM((N_DEV,) + part.shape, F32), pltpu.SemaphoreType.DMA((N_DEV - 1,)),
                        pltpu.SemaphoreType.DMA((N_DEV - 1,))],
    )(part)


def _adamw(ws, gs, ms, vs, name):
    n = len(ws)
    r, cdim = ws[0].shape
    tr = r
    for cand in (512, 256, 128, 176, 64):
        if r % cand == 0 and r > cand:
            tr = cand
            break
    c1 = 1.0 - ADAM_B1 ** ADAM_STEP
    c2 = 1.0 - ADAM_B2 ** ADAM_STEP

    def body(*refs):
        for k in range(n):
            w, g, m, v = (refs[j * n + k][...] for j in range(4))
            m2 = ADAM_B1 * m + (1.0 - ADAM_B1) * g
            v2 = ADAM_B2 * v + (1.0 - ADAM_B2) * (g * g)
            delta = -ADAM_LR * ((m2 / c1) / (jnp.sqrt(v2 / c2) + ADAM_EPS) + ADAM_WD * w)
            refs[4 * n + k][...] = delta
            refs[5 * n + k][...] = m2
            refs[6 * n + k][...] = v2

    spec = pl.BlockSpec((tr, cdim), lambda i: (i, 0))
    outs = pl.pallas_call(
        body,
        name=name,
        grid=(r // tr,),
        in_specs=[spec] * (4 * n),
        out_specs=[spec] * (3 * n),
        out_shape=[jax.ShapeDtypeStruct((r, cdim), F32)] * (3 * n),
        compiler_params=_cparams(("parallel",)),
    )(*ws, *gs, *ms, *vs)
    return outs[:n], outs[n:2 * n], outs[2 * n:]


def _band_allowed(n_prev):
    pad = n_prev * CHUNK
    qc = (np.arange(TQ)[:, None] + pad) // CHUNK
    kc = np.arange(TQ + pad)[None, :] // CHUNK
    return (kc <= qc) & (kc >= qc - n_prev)


def _bias_b():
    pad = B_PREV * CHUNK
    slopes = np.array([2.0 ** (-8.0 * (i + 1) / B_Q_HEADS) for i in range(B_Q_HEADS)], dtype=np.float32)
    dist = np.abs(np.arange(TQ)[:, None] - np.arange(TQ + pad)[None, :] + pad).astype(np.float32)
    bias = -slopes.reshape(B_KV_HEADS, B_GROUP, 1, 1) * dist[None, None]
    bias = np.where(_band_allowed(B_PREV)[None, None], bias, np.float32(NEG_INF)).astype(np.float32)
    return bias.reshape(B_KV_HEADS, B_GROUP * TQ, TQ + pad)


def kernel(x, ffn1_norm, ffn1_w_gate, ffn1_w_up, ffn1_w_down, mix_norm, w_in, rel_bias, sinks, w_proj_a, w_proj_b, w_out, ffn2_norm, ffn2_w_gate, ffn2_w_up, ffn2_w_down, final_norm, loss_target, m_ffn1_norm, m_ffn1_w_gate, m_ffn1_w_up, m_ffn1_w_down, m_mix_norm, m_w_in, m_rel_bias, m_sinks, m_w_proj_a, m_w_proj_b, m_w_out, m_ffn2_norm, m_ffn2_w_gate, m_ffn2_w_up, m_ffn2_w_down, m_final_norm, v_ffn1_norm, v_ffn1_w_gate, v_ffn1_w_up, v_ffn1_w_down, v_mix_norm, v_w_in, v_rel_bias, v_sinks, v_w_proj_a, v_w_proj_b, v_w_out, v_ffn2_norm, v_ffn2_w_gate, v_ffn2_w_up, v_ffn2_w_down, v_final_norm):
    bsz, s_len, _ = x.shape
    t = bsz * s_len
    core = lax.axis_index("c").astype(jnp.int32).reshape(1)
    chip = (2 * lax.axis_index("x") + lax.axis_index("y")).astype(jnp.int32).reshape(1)

    def row_form(w):
        return w.astype(BF16).T

    wg1, wu1, wd1 = _exchange_alone(
        _Gather([row_form(ffn1_w_gate), row_form(ffn1_w_up), ffn1_w_down.astype(BF16)]), "gather_ffn1")
    gather_mix = _Gather([row_form(w_in), jnp.concatenate([row_form(w_proj_a), row_form(w_proj_b)], axis=1),
                          w_out.astype(BF16)])
    gather_ffn2_gate = _Gather([row_form(ffn2_w_gate)])
    gather_ffn2_rest = _Gather([row_form(ffn2_w_up), ffn2_w_down.astype(BF16)])

    x0 = x.reshape(t, D_MODEL)
    tgt = loss_target.reshape(t, D_MODEL)
    gam1, gam2, gam3, gam4 = (g.reshape(1, D_MODEL) for g in (ffn1_norm, mix_norm, ffn2_norm, final_norm))

    h1, g1, u1, a1, x1 = _ffn_fwd(x0, gam1, wg1, wu1, wd1, "ffn1_fwd", hosted=[gather_mix])
    win_t, proj_t, wout = gather_mix.result
    h2, qkv_a, qkv_b, gates = _proj_fwd(x1, gam2, win_t, hosted=[gather_ffn2_gate])
    (wg2,) = gather_ffn2_gate.result
    qkv_a3 = qkv_a.reshape(bsz, s_len, QKV_A)
    qkv_b3 = qkv_b.reshape(bsz, s_len, QKV_B)

    far = jnp.broadcast_to(rel_bias[:, REL_TABLE - 1:REL_TABLE], (A_HEADS, REL_WRAP // 2))
    tv = jnp.concatenate([far, jnp.flip(rel_bias, axis=1), jnp.zeros((A_HEADS, REL_WRAP // 2 - REL_TABLE), F32)], axis=1)
    bias_a = _bias_a_build(tv.reshape(A_HEADS, 1, REL_WRAP))
    bias_b = jnp.asarray(_bias_b())
    sink_rows = jnp.repeat(sinks.reshape(B_KV_HEADS, B_GROUP), TQ, axis=1).reshape(B_KV_HEADS, B_GROUP * TQ, 1)

    oa = _attn_a_fwd(qkv_a3, bias_a, hosted=[gather_ffn2_rest]).reshape(t, A_WIDTH)
    wu2, wd2 = gather_ffn2_rest.result
    ob = _attn_b_fwd(qkv_b3, bias_b, sink_rows).reshape(t, B_Q_WIDTH)
    x2, ya, yb, mg = _mix_out_fwd(x1, oa, ob, gates, proj_t, wout)
    h3, g2, u2, a2, x3 = _ffn_fwd(x2, gam3, wg2, wu2, wd2, "ffn2_fwd")

    dx3, dgam4, loss_part = _loss_head(x3, gam4, tgt)

    dx2, dg2, du2, db2, dgam3 = _ffn_bwd(dx3, x2, gam3, g2, u2, wg2, wu2, wd2, "ffn2_bwd")
    gw_ffn2 = [_mm_tn([dg2], h3, "grad_ffn2_gate"), _mm_tn([du2], h3, "grad_ffn2_up"),
               _mm_tn([a2], db2, "grad_ffn2_down")]
    pairx_ffn2 = _PairExchange(gw_ffn2)
    dxb, dya, dyb, doa, dob, dgates = _mix_out_bwd(dx2, gates, ya, yb, proj_t, wout, hosted=[pairx_ffn2])
    psum_ffn2 = _pair_sum(core, gw_ffn2, pairx_ffn2.result, "pair_sum_ffn2")
    gw_out = _mm_tn([mg], dxb, "grad_w_out")
    gw_proj = _mm_tn_proj(dya, dyb, oa, ob)

    chipx_ffn2 = _ChipExchange(psum_ffn2)
    dqa, dka, dva, dbias_a = _attn_a_bwd(qkv_a3, bias_a, doa.reshape(bsz, s_len, A_WIDTH), hosted=[chipx_ffn2])
    pairx_out = _PairExchange([gw_proj, gw_out])
    dqb, dkvb, dsink = _attn_b_bwd(qkv_b3, bias_b, sink_rows, dob.reshape(bsz, s_len, B_Q_WIDTH), hosted=[pairx_out])
    drel_lanes = _relbias_grad(dbias_a)
    dproj = [dqa.reshape(t, A_WIDTH), dka.reshape(t, A_WIDTH), dva.reshape(t, A_WIDTH), dqb.reshape(t, B_Q_WIDTH),
             dkvb.reshape(t, 2 * B_KV_WIDTH), dgates]

    dx1, db1, dgam2 = _proj_bwd(dx2, x1, gam2, dproj, win_t)
    gw_in = _mm_tn(dproj, h2, "grad_w_in")
    pairx_in = _PairExchange([gw_in])
    gw_d1 = _mm_tn([a1], db1, "grad_ffn1_down", hosted=[pairx_in])
    psum_mix = (_pair_sum(core, [gw_in], pairx_in.result, "pair_sum_w_in")
                + _pair_sum(core, [gw_proj, gw_out], pairx_out.result, "pair_sum_mix"))

    chipx_mix = _ChipExchange(psum_mix)
    dx0, dg1, du1, _, dgam1 = _ffn_bwd(dx1, x0, gam1, g1, u1, wg1, wu1, wd1, "ffn1_bwd", hosted=[chipx_mix])
    pairx_d1 = _PairExchange([gw_d1])
    gw_g1 = _mm_tn([dg1], h1, "grad_ffn1_gate", hosted=[pairx_d1])
    psum_d1 = _pair_sum(core, [gw_d1], pairx_d1.result, "pair_sum_ffn1_down")
    chipx_d1 = _ChipExchange(psum_d1)
    pairx_g1 = _PairExchange([gw_g1])
    gw_u1 = _mm_tn([du1], h1, "grad_ffn1_up", hosted=[chipx_d1, pairx_g1])
    from_sibling_u1 = _exchange_alone(_PairExchange([gw_u1]), "pair_exchange_ffn1_up")
    psum_gu = _pair_sum(core, [gw_g1, gw_u1], pairx_g1.result + from_sibling_u1, "pair_sum_ffn1_up")
    from_chips_gu = _exchange_alone(_ChipExchange(psum_gu), "chip_exchange_ffn1_up")

    g_g1, g_u1, g_d1, g_g2, g_u2, g_d2 = _final_sum(
        chip, psum_gu + psum_d1 + psum_ffn2, from_chips_gu + chipx_d1.result + chipx_ffn2.result, "grad_sum_ffn")
    (g_in,) = _final_sum(chip, psum_mix[0:1], chipx_mix.result[0:1], "grad_sum_w_in")
    g_proj, g_out = _final_sum(chip, psum_mix[1:3], chipx_mix.result[1:3], "grad_sum_mix")
    grads = {
        "ffn1_w_gate": g_g1.T, "ffn1_w_up": g_u1.T, "ffn1_w_down": g_d1, "w_in": g_in.T,
        "w_proj_a": g_proj[:, 0:A_WIDTH].T, "w_proj_b": g_proj[:, A_WIDTH:].T, "w_out": g_out,
        "ffn2_w_gate": g_g2.T, "ffn2_w_up": g_u2.T, "ffn2_w_down": g_d2,
    }

    def row_of(v):
        return jnp.pad(v.reshape(1, -1), ((0, 0), (0, D_MODEL - v.size)))

    def table_rows(v):
        return jnp.pad(v, ((0, 0), (0, D_MODEL - REL_TABLE)))

    drel_local = jnp.flip(drel_lanes[:, 0, 0:REL_TABLE], axis=1)
    small_part = jnp.concatenate(
        [jnp.sum(dgam1, axis=0, keepdims=True), jnp.sum(dgam2, axis=0, keepdims=True),
         jnp.sum(dgam3, axis=0, keepdims=True), jnp.sum(dgam4, axis=0, keepdims=True),
         row_of(jnp.sum(loss_part)), row_of(dsink[:, 0:B_GROUP, 0]), jnp.zeros((2, D_MODEL), F32),
         table_rows(drel_local)], axis=0)
    small = _all_reduce_small(small_part)
    loss = small[4, 0]

    def pack(n1, n2, n3, n4, sk, tb):
        return jnp.concatenate([n1.reshape(1, -1), n2.reshape(1, -1), n3.reshape(1, -1), n4.reshape(1, -1),
                                jnp.zeros((1, D_MODEL), F32), row_of(sk), jnp.zeros((2, D_MODEL), F32), table_rows(tb)],
                               axis=0)

    live = np.zeros((SMALL_ROWS, D_MODEL), np.float32)
    live[0:4] = 1.0
    live[5, 0:B_Q_HEADS] = 1.0
    live[8:16, 0:REL_TABLE] = 1.0
    small_g = small * jnp.asarray(live)
    sw = pack(ffn1_norm, mix_norm, ffn2_norm, final_norm, sinks, rel_bias)
    sm = pack(m_ffn1_norm, m_mix_norm, m_ffn2_norm, m_final_norm, m_sinks, m_rel_bias)
    sv = pack(v_ffn1_norm, v_mix_norm, v_ffn2_norm, v_final_norm, v_sinks, v_rel_bias)
    (sd,), (snm,), (snv,) = _adamw([sw], [small_g], [sm], [sv], "adamw_small")

    def unpack(p):
        return {"ffn1_norm": p[0], "mix_norm": p[1], "ffn2_norm": p[2], "final_norm": p[3],
                "sinks": p[5, 0:B_Q_HEADS], "rel_bias": p[8:16, 0:REL_TABLE]}

    grads.update(unpack(small_g))
    delta, new_m, new_v = unpack(sd), unpack(snm), unpack(snv)

    wmv = {
        "ffn1_w_gate": (ffn1_w_gate, m_ffn1_w_gate, v_ffn1_w_gate), "ffn1_w_up": (ffn1_w_up, m_ffn1_w_up, v_ffn1_w_up),
        "ffn1_w_down": (ffn1_w_down, m_ffn1_w_down, v_ffn1_w_down), "w_in": (w_in, m_w_in, v_w_in),
        "w_proj_a": (w_proj_a, m_w_proj_a, v_w_proj_a), "w_proj_b": (w_proj_b, m_w_proj_b, v_w_proj_b),
        "w_out": (w_out, m_w_out, v_w_out),
        "ffn2_w_gate": (ffn2_w_gate, m_ffn2_w_gate, v_ffn2_w_gate), "ffn2_w_up": (ffn2_w_up, m_ffn2_w_up, v_ffn2_w_up),
        "ffn2_w_down": (ffn2_w_down, m_ffn2_w_down, v_ffn2_w_down),
    }
    groups = [("adamw_ffn_up", ["ffn1_w_gate", "ffn1_w_up", "ffn2_w_gate", "ffn2_w_up"]),
              ("adamw_ffn_down", ["ffn1_w_down", "ffn2_w_down"]), ("adamw_w_in", ["w_in"]),
              ("adamw_proj", ["w_proj_a", "w_proj_b"]), ("adamw_w_out", ["w_out"])]
    for gname, names in groups:
        ds_, ms_, vs_ = _adamw([wmv[n][0] for n in names], [grads[n] for n in names], [wmv[n][1] for n in names],
                               [wmv[n][2] for n in names], gname)
        for n, d_, m_, v_ in zip(names, ds_, ms_, vs_):
            delta[n], new_m[n], new_v[n] = d_, m_, v_

    order = ["ffn1_norm", "ffn1_w_gate", "ffn1_w_up", "ffn1_w_down", "mix_norm", "w_in", "rel_bias", "sinks",
             "w_proj_a", "w_proj_b", "w_out", "ffn2_norm", "ffn2_w_gate", "ffn2_w_up", "ffn2_w_down", "final_norm"]
    grad_x = dx0.reshape(bsz, s_len, D_MODEL)
    return (loss, grad_x, *[grads[n] for n in order], *[delta[n] for n in order], *[new_m[n] for n in order],
            *[new_v[n] for n in order])
```

```python
import numpy as np
import jax
import jax.numpy as jnp
from jax import lax
from jax.experimental import pallas as pl
from jax.experimental.pallas import tpu as pltpu

F32 = jnp.float32
BF16 = jnp.bfloat16

D_MODEL = 1024
D_FF = 2816
CHUNK = 64
D_HEAD = 64
A_HEADS = 8
A_PREV = 8
MAX_REL = 128
B_Q_HEADS = 8
B_KV_HEADS = 2
B_GROUP = B_Q_HEADS // B_KV_HEADS
B_PREV = 2
REL_TABLE = (CHUNK - 1) + MAX_REL + 1
A_WIDTH = A_HEADS * D_HEAD
B_Q_WIDTH = B_Q_HEADS * D_HEAD
B_KV_WIDTH = B_KV_HEADS * D_HEAD
QKV_A = 3 * A_WIDTH
QKV_B = B_Q_WIDTH + 2 * B_KV_WIDTH
IN_WIDTH = QKV_A + QKV_B + 2 * D_MODEL
EPS = 1e-6
NEG_INF = -1e30
SCALE = 1.0 / 8.0

ADAM_LR = 0.001
ADAM_B1 = 0.9
ADAM_B2 = 0.999
ADAM_EPS = 1e-08
ADAM_WD = 0.01
ADAM_STEP = 10

N_DEV = 8
N_CHIP = 4
MESH = pl.DeviceIdType.MESH

LANES = 128
TQ = 256
TM = 256
FC = 256
VMEM_LIMIT = 56 << 20


def _cparams(sem, vmem=VMEM_LIMIT):
    return pltpu.CompilerParams(dimension_semantics=sem, vmem_limit_bytes=vmem)


def _dot_nt(a, b):
    return lax.dot_general(a, b, (((1,), (1,)), ((), ())), preferred_element_type=F32)


def _dot_nn(a, b):
    return lax.dot_general(a, b, (((1,), (0,)), ((), ())), preferred_element_type=F32)


def _dot_tn(a, b):
    return lax.dot_general(a, b, (((0,), (0,)), ((), ())), preferred_element_type=F32)


def _resident(shape):
    nd = len(shape)
    return pl.BlockSpec(shape, lambda *_: (0,) * nd, pipeline_mode=pl.Buffered(1))


def _rows(tm, width):
    return pl.BlockSpec((tm, width), lambda i: (i, 0))


def _colsum8(v):
    tm, n = v.shape
    return jnp.sum(v.reshape(tm // 8, 8, n), axis=0)


def _rms(x):
    r = lax.rsqrt(jnp.mean(x * x, axis=-1, keepdims=True) + EPS)
    return x * r, r


def _rms_bwd(dh, xh, r, gamma):
    dxh = dh * gamma
    dx = r * (dxh - xh * jnp.mean(dxh * xh, axis=-1, keepdims=True))
    return dx, _colsum8(dh * xh)


def _hbm():
    return pl.BlockSpec(memory_space=pltpu.HBM)


def _call(body, *, name, grid, in_specs, out_specs, out_shape, args, sem, scratch_shapes=(), hosted=()):
    in_specs, out_specs, out_shape = list(in_specs), list(out_specs), list(out_shape)
    scratch_shapes = list(scratch_shapes)
    if not hosted:
        return pl.pallas_call(body, name=name, grid=grid, in_specs=in_specs, out_specs=out_specs, out_shape=out_shape,
                              scratch_shapes=scratch_shapes, compiler_params=_cparams(sem))(*args)
    n_in, n_out, n_scr = len(in_specs), len(out_specs), len(scratch_shapes)
    x_in = [a for x in hosted for a in x.inputs]
    x_out = [s for x in hosted for s in x.out_shape]
    x_scr = [s for x in hosted for s in x.scratch]
    steps = int(np.prod(grid))
    forward_step = max(steps - 3, 0)

    def wrapped(*refs):
        pos = [0]

        def take(k):
            pos[0] += k
            return refs[pos[0] - k:pos[0]]

        ins, xin, outs, xout, scr, xscr = (take(k) for k in (n_in, len(x_in), n_out, len(x_out), n_scr, len(x_scr)))
        step = 0
        for axis, extent in enumerate(grid):
            step = step * extent + pl.program_id(axis)
        own, oi, oo, osc = [], 0, 0, 0
        for x in hosted:
            own.append((xin[oi:oi + len(x.inputs)], xout[oo:oo + len(x.out_shape)], xscr[osc:osc + len(x.scratch)]))
            oi, oo, osc = oi + len(x.inputs), oo + len(x.out_shape), osc + len(x.scratch)

        def phase(method):
            for x, (i_, o_, s_) in zip(hosted, own):
                getattr(x, method)(i_, o_, s_)

        pl.when(step == 0)(lambda: phase("start"))
        body(*ins, *outs, *scr)
        pl.when(step == forward_step)(lambda: phase("forward"))
        pl.when(step == steps - 1)(lambda: phase("finish"))

    res = pl.pallas_call(
        wrapped, name=name, grid=grid, in_specs=in_specs + [_hbm()] * len(x_in),
        out_specs=out_specs + [_hbm()] * len(x_out), out_shape=out_shape + x_out,
        scratch_shapes=scratch_shapes + x_scr, compiler_params=_cparams(("arbitrary",) * len(grid)))(*args, *x_in)
    rest = list(res[n_out:])
    for x in hosted:
        x.result, rest = rest[:len(x.out_shape)], rest[len(x.out_shape):]
    return list(res[:n_out])


def _ffn_fwd(x, gamma, wg_t, wu_t, wd, name, hosted=()):
    t = x.shape[0]
    f = wg_t.shape[0]

    def body(x_ref, gam_ref, wg_ref, wu_ref, wd_ref, h_ref, g_ref, u_ref, a_ref, y_ref):
        xv = x_ref[...]
        xh, _ = _rms(xv)
        h = (xh * gam_ref[...]).astype(BF16)
        h_ref[...] = h
        for j in range(f // FC):
            sl = slice(j * FC, (j + 1) * FC)
            g = _dot_nt(h, wg_ref[sl, :])
            u = _dot_nt(h, wu_ref[sl, :])
            g_ref[:, sl] = g.astype(BF16)
            u_ref[:, sl] = u.astype(BF16)
            a_ref[:, sl] = (g * jax.nn.sigmoid(g) * u).astype(BF16)
        y_ref[...] = xv + 0.5 * _dot_nn(a_ref[...], wd_ref[...])

    return _call(
        body,
        name=name,
        grid=(t // TM,),
        in_specs=[_rows(TM, D_MODEL), _resident((1, D_MODEL)), _resident((f, D_MODEL)), _resident((f, D_MODEL)),
                  _resident((f, D_MODEL))],
        out_specs=[_rows(TM, D_MODEL), _rows(TM, f), _rows(TM, f), _rows(TM, f), _rows(TM, D_MODEL)],
        out_shape=[jax.ShapeDtypeStruct((t, D_MODEL), BF16), jax.ShapeDtypeStruct((t, f), BF16),
                   jax.ShapeDtypeStruct((t, f), BF16), jax.ShapeDtypeStruct((t, f), BF16),
                   jax.ShapeDtypeStruct((t, D_MODEL), F32)],
        args=(x, gamma, wg_t, wu_t, wd), sem=("parallel",), hosted=hosted)


def _ffn_bwd(d, x, gamma, g_act, u_act, wg_t, wu_t, wd, name, hosted=()):
    t = x.shape[0]
    f = wg_t.shape[0]

    def body(d_ref, x_ref, gam_ref, g_ref, u_ref, wg_ref, wu_ref, wd_ref, dx_ref, dg_ref, du_ref, db_ref, dgam_ref):
        dv = d_ref[...]
        db = (0.5 * dv).astype(BF16)
        db_ref[...] = db
        for j in range(f // FC):
            sl = slice(j * FC, (j + 1) * FC)
            da = _dot_nt(db, wd_ref[sl, :])
            g = g_ref[:, sl].astype(F32)
            u = u_ref[:, sl].astype(F32)
            s = jax.nn.sigmoid(g)
            dg_ref[:, sl] = (da * u * (s * (1.0 + g * (1.0 - s)))).astype(BF16)
            du_ref[:, sl] = (da * (g * s)).astype(BF16)
        dh = _dot_nn(dg_ref[...], wg_ref[...]) + _dot_nn(du_ref[...], wu_ref[...])
        xh, r = _rms(x_ref[...])
        dxn, dgam = _rms_bwd(dh, xh, r, gam_ref[...])
        dx_ref[...] = dv + dxn

        @pl.when(pl.program_id(0) == 0)
        def _():
            dgam_ref[...] = jnp.zeros_like(dgam_ref)

        dgam_ref[...] += dgam

    return _call(
        body,
        name=name,
        grid=(t // TM,),
        in_specs=[_rows(TM, D_MODEL), _rows(TM, D_MODEL), _resident((1, D_MODEL)), _rows(TM, f), _rows(TM, f),
                  _resident((f, D_MODEL)), _resident((f, D_MODEL)), _resident((f, D_MODEL))],
        out_specs=[_rows(TM, D_MODEL), _rows(TM, f), _rows(TM, f), _rows(TM, D_MODEL),
                   pl.BlockSpec((8, D_MODEL), lambda i: (0, 0))],
        out_shape=[jax.ShapeDtypeStruct((t, D_MODEL), F32), jax.ShapeDtypeStruct((t, f), BF16),
                   jax.ShapeDtypeStruct((t, f), BF16), jax.ShapeDtypeStruct((t, D_MODEL), BF16),
                   jax.ShapeDtypeStruct((8, D_MODEL), F32)],
        args=(d, x, gamma, g_act, u_act, wg_t, wu_t, wd), sem=("arbitrary",), hosted=hosted)


def _mm_tn(pieces, b, name, tile=256, hosted=()):
    t, n = b.shape
    npc = len(pieces)
    counts = [p.shape[1] // tile for p in pieces]
    los = [sum(counts[:k]) for k in range(npc)]
    total = sum(counts)

    def body(*refs):
        a_refs, b_ref, o_ref = refs[:npc], refs[npc], refs[npc + 1]
        i = pl.program_id(0)
        for k in range(npc):
            @pl.when(jnp.logical_and(i >= los[k], i < los[k] + counts[k]))
            def _(k=k):
                o_ref[...] = _dot_tn(a_refs[k][...], b_ref[...]).astype(BF16)

    def a_spec(k):
        return pl.BlockSpec((t, tile), lambda i: (0, jnp.clip(i - los[k], 0, counts[k] - 1)))

    return _call(
        body,
        name=name,
        grid=(total,),
        in_specs=[a_spec(k) for k in range(npc)] + [_resident((t, n))],
        out_specs=[pl.BlockSpec((tile, n), lambda i: (i, 0))],
        out_shape=[jax.ShapeDtypeStruct((total * tile, n), BF16)],
        args=(*pieces, b), sem=("parallel",), hosted=hosted)[0]


def _mm_tn_proj(dya, dyb, oa, ob, tile=256):
    t = dya.shape[0]

    def body(dya_ref, dyb_ref, oa_ref, ob_ref, o_ref):
        o_ref[:, 0:A_WIDTH] = _dot_tn(dya_ref[...], oa_ref[...]).astype(BF16)
        o_ref[:, A_WIDTH:A_WIDTH + B_Q_WIDTH] = _dot_tn(dyb_ref[...], ob_ref[...]).astype(BF16)

    col = pl.BlockSpec((t, tile), lambda i: (0, i))
    return pl.pallas_call(
        body,
        name="grad_proj",
        grid=(D_MODEL // tile,),
        in_specs=[col, col, _resident((t, A_WIDTH)), _resident((t, B_Q_WIDTH))],
        out_specs=pl.BlockSpec((tile, A_WIDTH + B_Q_WIDTH), lambda i: (i, 0)),
        out_shape=jax.ShapeDtypeStruct((D_MODEL, A_WIDTH + B_Q_WIDTH), BF16),
        compiler_params=_cparams(("parallel",)),
    )(dya, dyb, oa, ob)


def _proj_fwd(x, gamma, win_t, hosted=()):
    t = x.shape[0]

    def body(x_ref, gam_ref, w_ref, h_ref, qa_ref, qb_ref, gt_ref):
        xh, _ = _rms(x_ref[...])
        h = (xh * gam_ref[...]).astype(BF16)
        h_ref[...] = h
        for j in range(QKV_A // FC):
            qa_ref[:, j * FC:(j + 1) * FC] = _dot_nt(h, w_ref[j * FC:(j + 1) * FC, :]).astype(BF16)
        for j in range(QKV_B // FC):
            lo = QKV_A + j * FC
            qb_ref[:, j * FC:(j + 1) * FC] = _dot_nt(h, w_ref[lo:lo + FC, :]).astype(BF16)
        for j in range(2 * D_MODEL // FC):
            lo = QKV_A + QKV_B + j * FC
            gt_ref[:, j * FC:(j + 1) * FC] = _dot_nt(h, w_ref[lo:lo + FC, :])

    return _call(
        body,
        name="proj_fwd",
        grid=(t // TM,),
        in_specs=[_rows(TM, D_MODEL), _resident((1, D_MODEL)), _resident((IN_WIDTH, D_MODEL))],
        out_specs=[_rows(TM, D_MODEL), _rows(TM, QKV_A), _rows(TM, QKV_B), _rows(TM, 2 * D_MODEL)],
        out_shape=[jax.ShapeDtypeStruct((t, D_MODEL), BF16), jax.ShapeDtypeStruct((t, QKV_A), BF16),
                   jax.ShapeDtypeStruct((t, QKV_B), BF16), jax.ShapeDtypeStruct((t, 2 * D_MODEL), F32)],
        args=(x, gamma, win_t), sem=("parallel",), hosted=hosted)


def _proj_bwd(d, x, gamma, pieces, win_t, hosted=()):
    t = x.shape[0]
    npc = len(pieces)
    widths = [p.shape[1] for p in pieces]
    los = [sum(widths[:k]) for k in range(npc)]

    def body(*refs):
        d_ref, x_ref, gam_ref = refs[:3]
        p_refs = refs[3:3 + npc]
        w_ref, dx_ref, db_ref, dgam_ref = refs[3 + npc:]
        dh = _dot_nn(p_refs[0][...], w_ref[0:widths[0], :])
        for k in range(1, npc):
            dh += _dot_nn(p_refs[k][...], w_ref[los[k]:los[k] + widths[k], :])
        xh, r = _rms(x_ref[...])
        dxn, dgam = _rms_bwd(dh, xh, r, gam_ref[...])
        dx = d_ref[...] + dxn
        dx_ref[...] = dx
        db_ref[...] = (0.5 * dx).astype(BF16)

        @pl.when(pl.program_id(0) == 0)
        def _():
            dgam_ref[...] = jnp.zeros_like(dgam_ref)

        dgam_ref[...] += dgam

    return _call(
        body,
        name="proj_bwd",
        grid=(t // TM,),
        in_specs=[_rows(TM, D_MODEL), _rows(TM, D_MODEL), _resident((1, D_MODEL))] + [_rows(TM, w) for w in widths]
        + [_resident((IN_WIDTH, D_MODEL))],
        out_specs=[_rows(TM, D_MODEL), _rows(TM, D_MODEL), pl.BlockSpec((8, D_MODEL), lambda i: (0, 0))],
        out_shape=[jax.ShapeDtypeStruct((t, D_MODEL), F32), jax.ShapeDtypeStruct((t, D_MODEL), BF16),
                   jax.ShapeDtypeStruct((8, D_MODEL), F32)],
        args=(d, x, gamma, *pieces, win_t), sem=("arbitrary",), hosted=hosted)


def _lane_half(shape):
    return lax.broadcasted_iota(jnp.int32, shape, len(shape) - 1) // D_HEAD


def _band_softmax(q, kk, bias, sink, qs, pad):
    s = _dot_nt(q, kk) * SCALE + bias
    col = lax.broadcasted_iota(jnp.int32, s.shape, 1)
    s = jnp.where(col + qs >= pad, s, NEG_INF)
    m = jnp.max(s, axis=-1, keepdims=True)
    if sink is not None:
        m = jnp.maximum(m, sink)
    p = jnp.exp(s - m)
    den = jnp.sum(p, axis=-1, keepdims=True)
    if sink is not None:
        den = den + jnp.exp(sink - m)
    return p, m, 1.0 / den


def _fill_padded(dst, src, pad):
    dst[0:pad, :] = jnp.zeros((pad, LANES), dst.dtype)
    dst[pad:, :] = src


def _attn_a_fwd(qkv, bias, hosted=()):
    bsz, s_len, _ = qkv.shape
    pad = A_PREV * CHUNK
    band = TQ + pad
    npair = A_HEADS // 2

    def body(q_ref, k_ref, v_ref, b_ref, o_ref, kp, vp):
        i = pl.program_id(2)

        @pl.when(i == 0)
        def _():
            _fill_padded(kp, k_ref[...], pad)
            _fill_padded(vp, v_ref[...], pad)

        qs = pl.multiple_of(i * TQ, TQ)
        kk = kp[pl.ds(qs, band), :]
        vv = vp[pl.ds(qs, band), :]
        q = q_ref[...]
        half = _lane_half((1, LANES))
        outs = []
        for j in range(2):
            qm = jnp.where(half == j, q, jnp.zeros_like(q))
            p, _, inv = _band_softmax(qm, kk, b_ref[j], None, qs, pad)
            outs.append(_dot_nn(p.astype(BF16), vv) * inv)
        o_ref[...] = jnp.where(half == 0, outs[0], outs[1]).astype(BF16)

    return _call(
        body,
        name="attn_a_fwd",
        grid=(bsz, npair, s_len // TQ),
        in_specs=[pl.BlockSpec((None, TQ, LANES), lambda b, hp, i: (b, i, hp)),
                  pl.BlockSpec((None, s_len, LANES), lambda b, hp, i: (b, 0, npair + hp)),
                  pl.BlockSpec((None, s_len, LANES), lambda b, hp, i: (b, 0, 2 * npair + hp)),
                  pl.BlockSpec((2, TQ, band), lambda b, hp, i: (hp, 0, 0))],
        out_specs=[pl.BlockSpec((None, TQ, LANES), lambda b, hp, i: (b, i, hp))],
        out_shape=[jax.ShapeDtypeStruct((bsz, s_len, A_WIDTH), BF16)],
        scratch_shapes=[pltpu.VMEM((pad + s_len, LANES), BF16), pltpu.VMEM((pad + s_len, LANES), BF16)],
        args=(qkv, qkv, qkv, bias), sem=("arbitrary", "arbitrary", "arbitrary"), hosted=hosted)[0]


def _attn_a_bwd(qkv, bias, do, hosted=()):
    bsz, s_len, _ = qkv.shape
    pad = A_PREV * CHUNK
    band = TQ + pad
    npair = A_HEADS // 2
    n_i = s_len // TQ

    def body(q_ref, k_ref, v_ref, b_ref, do_ref, dq_ref, dk_ref, dv_ref, dbias_ref, kp, vp, dk_acc, dv_acc):
        b = pl.program_id(1)
        i = pl.program_id(2)

        @pl.when(i == 0)
        def _():
            _fill_padded(kp, k_ref[...], pad)
            _fill_padded(vp, v_ref[...], pad)
            dk_acc[...] = jnp.zeros_like(dk_acc)
            dv_acc[...] = jnp.zeros_like(dv_acc)

        @pl.when(jnp.logical_and(b == 0, i == 0))
        def _():
            dbias_ref[...] = jnp.zeros_like(dbias_ref)

        qs = pl.multiple_of(i * TQ, TQ)
        kk = kp[pl.ds(qs, band), :]
        vv = vp[pl.ds(qs, band), :]
        q = q_ref[...]
        dd = do_ref[...]
        half = _lane_half((1, LANES))
        dqs, dks, dvs = [], [], []
        for j in range(2):
            qm = jnp.where(half == j, q, jnp.zeros_like(q))
            dm = jnp.where(half == j, dd, jnp.zeros_like(dd))
            p, _, inv = _band_softmax(qm, kk, b_ref[j], None, qs, pad)
            pn = p * inv
            dp = _dot_nt(dm, vv)
            delta = jnp.sum(pn * dp, axis=-1, keepdims=True)
            ds = pn * (dp - delta)
            dbias_ref[j] += ds
            dsb = ds.astype(BF16)
            dqs.append(_dot_nn(dsb, kk))
            dks.append(_dot_tn(dsb, q))
            dvs.append(_dot_tn(pn.astype(BF16), dd))
        dq_ref[...] = (jnp.where(half == 0, dqs[0], dqs[1]) * SCALE).astype(BF16)
        dk_acc[pl.ds(qs, band), :] += jnp.where(half == 0, dks[0], dks[1]) * SCALE
        dv_acc[pl.ds(qs, band), :] += jnp.where(half == 0, dvs[0], dvs[1])

        @pl.when(i == n_i - 1)
        def _():
            dk_ref[...] = dk_acc[pad:, :].astype(BF16)
            dv_ref[...] = dv_acc[pad:, :].astype(BF16)

    qspec = pl.BlockSpec((None, TQ, LANES), lambda hp, b, i: (b, i, hp))
    kvout = pl.BlockSpec((None, s_len, LANES), lambda hp, b, i: (b, 0, hp))
    wide = jax.ShapeDtypeStruct((bsz, s_len, A_WIDTH), BF16)
    return _call(
        body,
        name="attn_a_bwd",
        grid=(npair, bsz, n_i),
        in_specs=[qspec,
                  pl.BlockSpec((None, s_len, LANES), lambda hp, b, i: (b, 0, npair + hp)),
                  pl.BlockSpec((None, s_len, LANES), lambda hp, b, i: (b, 0, 2 * npair + hp)),
                  pl.BlockSpec((2, TQ, band), lambda hp, b, i: (hp, 0, 0)),
                  qspec],
        out_specs=[qspec, kvout, kvout, pl.BlockSpec((2, TQ, band), lambda hp, b, i: (hp, 0, 0))],
        out_shape=[wide, wide, wide, jax.ShapeDtypeStruct((A_HEADS, TQ, band), F32)],
        scratch_shapes=[pltpu.VMEM((pad + s_len, LANES), BF16), pltpu.VMEM((pad + s_len, LANES), BF16),
                        pltpu.VMEM((pad + s_len, LANES), F32), pltpu.VMEM((pad + s_len, LANES), F32)],
        args=(qkv, qkv, qkv, bias, do), sem=("arbitrary", "arbitrary", "arbitrary"), hosted=hosted)


TQB = 128


def _stack_group(x, half):
    parts = []
    for g in range(B_GROUP):
        blk = x[:, LANES * (g // 2):LANES * (g // 2) + LANES]
        parts.append(jnp.where(half == (g % 2), blk, jnp.zeros_like(blk)))
    return jnp.concatenate(parts, axis=0)


def _unstack_group(y, half):
    left = jnp.where(half == 0, y[0:TQB], y[TQB:2 * TQB])
    right = jnp.where(half == 0, y[2 * TQB:3 * TQB], y[3 * TQB:4 * TQB])
    return left, right


def _fill_padded_dup(dst, src, pad, half):
    other = pltpu.roll(src, D_HEAD, 1)
    for h in range(B_KV_HEADS):
        dst[h, 0:pad, :] = jnp.zeros((pad, LANES), dst.dtype)
        dst[h, pad:, :] = jnp.where(half == h, src, other)


def _attn_b_fwd(qkv, bias, sink):
    bsz, s_len, _ = qkv.shape
    pad = B_PREV * CHUNK
    band = TQB + pad
    kcol = B_Q_WIDTH // LANES
    rows = B_GROUP * TQB

    def body(q_ref, k_ref, v_ref, b_ref, s_ref, o_ref, kp, vp):
        i = pl.program_id(1)
        half = _lane_half((1, LANES))

        @pl.when(i == 0)
        def _():
            _fill_padded_dup(kp, k_ref[...], pad, half)
            _fill_padded_dup(vp, v_ref[...], pad, half)

        qs = pl.multiple_of(i * TQB, TQB)
        for h in range(B_KV_HEADS):
            lo = 2 * LANES * h
            kk = kp[h, pl.ds(qs, band), :]
            vv = vp[h, pl.ds(qs, band), :]
            qq = _stack_group(q_ref[:, lo:lo + 2 * LANES], half)
            p, _, inv = _band_softmax(qq, kk, b_ref[h], s_ref[h], qs, pad)
            o = _dot_nn(p.astype(BF16), vv) * inv
            left, right = _unstack_group(o, half)
            o_ref[:, lo:lo + LANES] = left.astype(BF16)
            o_ref[:, lo + LANES:lo + 2 * LANES] = right.astype(BF16)

    return pl.pallas_call(
        body,
        name="attn_b_fwd",
        grid=(bsz, s_len // TQB),
        in_specs=[pl.BlockSpec((None, TQB, B_Q_WIDTH), lambda b, i: (b, i, 0)),
                  pl.BlockSpec((None, s_len, LANES), lambda b, i: (b, 0, kcol)),
                  pl.BlockSpec((None, s_len, LANES), lambda b, i: (b, 0, kcol + 1)),
                  pl.BlockSpec((B_KV_HEADS, rows, band), lambda b, i: (0, 0, 0)),
                  pl.BlockSpec((B_KV_HEADS, rows, 1), lambda b, i: (0, 0, 0))],
        out_specs=pl.BlockSpec((None, TQB, B_Q_WIDTH), lambda b, i: (b, i, 0)),
        out_shape=jax.ShapeDtypeStruct((bsz, s_len, B_Q_WIDTH), BF16),
        scratch_shapes=[pltpu.VMEM((B_KV_HEADS, pad + s_len, LANES), BF16),
                        pltpu.VMEM((B_KV_HEADS, pad + s_len, LANES), BF16)],
        compiler_params=_cparams(("arbitrary", "arbitrary")),
    )(qkv, qkv, qkv, bias, sink)


def _attn_b_bwd(qkv, bias, sink, do, hosted=()):
    bsz, s_len, _ = qkv.shape
    pad = B_PREV * CHUNK
    band = TQB + pad
    kcol = B_Q_WIDTH // LANES
    n_i = s_len // TQB
    rows = B_GROUP * TQB

    def body(q_ref, k_ref, v_ref, b_ref, s_ref, do_ref, dq_ref, dkv_ref, dsink_ref, kp, vp, dk_acc, dv_acc):
        b = pl.program_id(0)
        i = pl.program_id(1)
        half = _lane_half((1, LANES))

        @pl.when(i == 0)
        def _():
            _fill_padded_dup(kp, k_ref[...], pad, half)
            _fill_padded_dup(vp, v_ref[...], pad, half)
            dk_acc[...] = jnp.zeros_like(dk_acc)
            dv_acc[...] = jnp.zeros_like(dv_acc)

        @pl.when(jnp.logical_and(b == 0, i == 0))
        def _():
            dsink_ref[...] = jnp.zeros_like(dsink_ref)

        qs = pl.multiple_of(i * TQB, TQB)
        row = lax.broadcasted_iota(jnp.int32, (8, LANES), 0)
        dks, dvs = [], []
        for h in range(B_KV_HEADS):
            lo = 2 * LANES * h
            kk = kp[h, pl.ds(qs, band), :]
            vv = vp[h, pl.ds(qs, band), :]
            qq = _stack_group(q_ref[:, lo:lo + 2 * LANES], half)
            dd = _stack_group(do_ref[:, lo:lo + 2 * LANES], half)
            sink_rows = s_ref[h]
            p, m, inv = _band_softmax(qq, kk, b_ref[h], sink_rows, qs, pad)
            pn = p * inv
            dp = _dot_nt(dd, vv)
            delta = jnp.sum(pn * dp, axis=-1, keepdims=True)
            ds = pn * (dp - delta)
            dsb = ds.astype(BF16)
            left, right = _unstack_group(_dot_nn(dsb, kk) * SCALE, half)
            dq_ref[:, lo:lo + LANES] = left.astype(BF16)
            dq_ref[:, lo + LANES:lo + 2 * LANES] = right.astype(BF16)
            dk2 = _dot_tn(dsb, qq) * SCALE
            dv2 = _dot_tn(pn.astype(BF16), dd)
            dks.append(dk2 + pltpu.roll(dk2, D_HEAD, 1))
            dvs.append(dv2 + pltpu.roll(dv2, D_HEAD, 1))
            dsk = -(jnp.exp(sink_rows - m) * inv) * delta
            upd = jnp.zeros((8, LANES), F32)
            for g in range(B_GROUP):
                tot = jnp.sum(dsk[g * TQB:(g + 1) * TQB, :], axis=0, keepdims=True)
                upd = upd + jnp.where(row == g, tot, 0.0)
            dsink_ref[h] += upd
        dk_acc[pl.ds(qs, band), :] += jnp.where(half == 0, dks[0], dks[1])
        dv_acc[pl.ds(qs, band), :] += jnp.where(half == 0, dvs[0], dvs[1])

        @pl.when(i == n_i - 1)
        def _():
            dkv_ref[:, 0:LANES] = dk_acc[pad:, :].astype(BF16)
            dkv_ref[:, LANES:2 * LANES] = dv_acc[pad:, :].astype(BF16)

    qspec = pl.BlockSpec((None, TQB, B_Q_WIDTH), lambda b, i: (b, i, 0))
    return _call(
        body,
        name="attn_b_bwd",
        grid=(bsz, n_i),
        in_specs=[qspec,
                  pl.BlockSpec((None, s_len, LANES), lambda b, i: (b, 0, kcol)),
                  pl.BlockSpec((None, s_len, LANES), lambda b, i: (b, 0, kcol + 1)),
                  pl.BlockSpec((B_KV_HEADS, rows, band), lambda b, i: (0, 0, 0)),
                  pl.BlockSpec((B_KV_HEADS, rows, 1), lambda b, i: (0, 0, 0)),
                  qspec],
        out_specs=[qspec, pl.BlockSpec((None, s_len, 2 * LANES), lambda b, i: (b, 0, 0)),
                   pl.BlockSpec((B_KV_HEADS, 8, LANES), lambda b, i: (0, 0, 0))],
        out_shape=[jax.ShapeDtypeStruct((bsz, s_len, B_Q_WIDTH), BF16),
                   jax.ShapeDtypeStruct((bsz, s_len, 2 * B_KV_WIDTH), BF16),
                   jax.ShapeDtypeStruct((B_KV_HEADS, 8, LANES), F32)],
        scratch_shapes=[pltpu.VMEM((B_KV_HEADS, pad + s_len, LANES), BF16),
                        pltpu.VMEM((B_KV_HEADS, pad + s_len, LANES), BF16),
                        pltpu.VMEM((pad + s_len, LANES), F32), pltpu.VMEM((pad + s_len, LANES), F32)],
        args=(qkv, qkv, qkv, bias, sink, do), sem=("arbitrary", "arbitrary"), hosted=hosted)


REL_COLS = 3 * 128
REL_WRAP = 512


def _bias_a_build(tv):
    h = tv.shape[0]
    pad = A_PREV * CHUNK
    band = TQ + pad

    def body(tv_ref, o_ref):
        row = tv_ref[...]
        x = jnp.broadcast_to(row, (TQ, REL_WRAP))
        r = lax.broadcasted_iota(jnp.int32, x.shape, 0)
        for bit in range(8):
            sh = 1 << bit
            x = jnp.where((r & sh) != 0, pltpu.roll(x, sh, 1), x)
        far = jnp.broadcast_to(row[:, 0:1], (TQ, band - REL_COLS))
        full = jnp.concatenate([far, x[:, REL_WRAP // 2:REL_WRAP], x[:, 0:REL_COLS - REL_WRAP // 2]], axis=1)
        qc = (lax.broadcasted_iota(jnp.int32, full.shape, 0) + pad) // CHUNK
        kc = lax.broadcasted_iota(jnp.int32, full.shape, 1) // CHUNK
        ok = jnp.logical_and(kc <= qc, kc >= qc - A_PREV)
        o_ref[...] = jnp.where(ok, full, NEG_INF)

    return pl.pallas_call(
        body,
        name="bias_a_build",
        grid=(h,),
        in_specs=[pl.BlockSpec((None, 1, REL_WRAP), lambda hh: (hh, 0, 0))],
        out_specs=pl.BlockSpec((None, TQ, band), lambda hh: (hh, 0, 0)),
        out_shape=jax.ShapeDtypeStruct((h, TQ, band), F32),
        compiler_params=_cparams(("parallel",)),
    )(tv)


def _relbias_grad(dbias):
    h, rows, band = dbias.shape
    off = band - REL_COLS

    def body(d_ref, o_ref):
        x = d_ref[...]
        r = lax.broadcasted_iota(jnp.int32, x.shape, 0)
        c = lax.broadcasted_iota(jnp.int32, x.shape, 1) - r
        x = jnp.where(jnp.logical_and(c >= 1, c < REL_TABLE), x, 0.0)
        for bit in range(8):
            sh = 1 << bit
            x = jnp.where((r & sh) != 0, pltpu.roll(x, REL_COLS - sh, 1), x)
        diag = jnp.sum(x, axis=0, keepdims=True)
        lane = lax.broadcasted_iota(jnp.int32, diag.shape, 1)
        diag = jnp.where(jnp.logical_and(lane >= 1, lane < REL_TABLE), diag, 0.0)
        rest = -jnp.sum(diag, axis=1, keepdims=True)
        o_ref[...] = jnp.broadcast_to(jnp.where(lane == 0, rest, diag), o_ref.shape)

    return pl.pallas_call(
        body,
        name="relbias_grad",
        grid=(h,),
        in_specs=[pl.BlockSpec((None, rows, REL_COLS), lambda hh: (hh, 0, off // REL_COLS))],
        out_specs=pl.BlockSpec((None, 8, REL_COLS), lambda hh: (hh, 0, 0)),
        out_shape=jax.ShapeDtypeStruct((h, 8, REL_COLS), F32),
        compiler_params=_cparams(("parallel",)),
    )(dbias)


def _mix_out_fwd(x, oa, ob, gates, proj_t, wout):
    t = x.shape[0]

    def body(x_ref, oa_ref, ob_ref, gt_ref, pt_ref, wo_ref, y_ref, ya_ref, yb_ref, mg_ref):
        ya = _dot_nt(oa_ref[...], pt_ref[:, 0:A_WIDTH])
        yb = _dot_nt(ob_ref[...], pt_ref[:, A_WIDTH:A_WIDTH + B_Q_WIDTH])
        ya_ref[...] = ya.astype(BF16)
        yb_ref[...] = yb.astype(BF16)
        mg = jax.nn.sigmoid(gt_ref[:, 0:D_MODEL]) * ya + jax.nn.sigmoid(gt_ref[:, D_MODEL:2 * D_MODEL]) * yb
        mgb = mg.astype(BF16)
        mg_ref[...] = mgb
        y_ref[...] = x_ref[...] + _dot_nn(mgb, wo_ref[...])

    return pl.pallas_call(
        body,
        name="mix_out_fwd",
        grid=(t // TM,),
        in_specs=[_rows(TM, D_MODEL), _rows(TM, A_WIDTH), _rows(TM, B_Q_WIDTH), _rows(TM, 2 * D_MODEL),
                  _resident((D_MODEL, A_WIDTH + B_Q_WIDTH)), _resident((D_MODEL, D_MODEL))],
        out_specs=[_rows(TM, D_MODEL), _rows(TM, D_MODEL), _rows(TM, D_MODEL), _rows(TM, D_MODEL)],
        out_shape=[jax.ShapeDtypeStruct((t, D_MODEL), F32), jax.ShapeDtypeStruct((t, D_MODEL), BF16),
                   jax.ShapeDtypeStruct((t, D_MODEL), BF16), jax.ShapeDtypeStruct((t, D_MODEL), BF16)],
        compiler_params=_cparams(("parallel",)),
    )(x, oa, ob, gates, proj_t, wout)


def _mix_out_bwd(d, gates, ya, yb, proj_t, wout, hosted=()):
    t = d.shape[0]

    def body(d_ref, gt_ref, ya_ref, yb_ref, pt_ref, wo_ref, db_ref, dya_ref, dyb_ref, doa_ref, dob_ref, dgt_ref):
        db = d_ref[...].astype(BF16)
        db_ref[...] = db
        dmg = _dot_nt(db, wo_ref[...])
        sa = jax.nn.sigmoid(gt_ref[:, 0:D_MODEL])
        sb = jax.nn.sigmoid(gt_ref[:, D_MODEL:2 * D_MODEL])
        dya = (dmg * sa).astype(BF16)
        dyb = (dmg * sb).astype(BF16)
        dya_ref[...] = dya
        dyb_ref[...] = dyb
        dgt_ref[:, 0:D_MODEL] = (dmg * ya_ref[...].astype(F32) * (sa * (1.0 - sa))).astype(BF16)
        dgt_ref[:, D_MODEL:2 * D_MODEL] = (dmg * yb_ref[...].astype(F32) * (sb * (1.0 - sb))).astype(BF16)
        doa_ref[...] = _dot_nn(dya, pt_ref[:, 0:A_WIDTH]).astype(BF16)
        dob_ref[...] = _dot_nn(dyb, pt_ref[:, A_WIDTH:A_WIDTH + B_Q_WIDTH]).astype(BF16)

    return _call(
        body,
        name="mix_out_bwd",
        grid=(t // TM,),
        in_specs=[_rows(TM, D_MODEL), _rows(TM, 2 * D_MODEL), _rows(TM, D_MODEL), _rows(TM, D_MODEL),
                  _resident((D_MODEL, A_WIDTH + B_Q_WIDTH)), _resident((D_MODEL, D_MODEL))],
        out_specs=[_rows(TM, D_MODEL), _rows(TM, D_MODEL), _rows(TM, D_MODEL), _rows(TM, A_WIDTH),
                   _rows(TM, B_Q_WIDTH), _rows(TM, 2 * D_MODEL)],
        out_shape=[jax.ShapeDtypeStruct((t, D_MODEL), BF16), jax.ShapeDtypeStruct((t, D_MODEL), BF16),
                   jax.ShapeDtypeStruct((t, D_MODEL), BF16), jax.ShapeDtypeStruct((t, A_WIDTH), BF16),
                   jax.ShapeDtypeStruct((t, B_Q_WIDTH), BF16), jax.ShapeDtypeStruct((t, 2 * D_MODEL), BF16)],
        args=(d, gates, ya, yb, proj_t, wout), sem=("parallel",), hosted=hosted)


def _loss_head(x, gamma, target):
    t = x.shape[0]

    def body(x_ref, gam_ref, t_ref, dx_ref, dgam_ref, loss_ref):
        xh, r = _rms(x_ref[...])
        gam = gam_ref[...]
        e = xh * gam - t_ref[...]
        dy = e * (1.0 / D_MODEL)
        dxn, dgam = _rms_bwd(dy, xh, r, gam)
        dx_ref[...] = dxn

        @pl.when(pl.program_id(0) == 0)
        def _():
            dgam_ref[...] = jnp.zeros_like(dgam_ref)
            loss_ref[...] = jnp.zeros_like(loss_ref)

        dgam_ref[...] += dgam
        loss_ref[...] += _colsum8(e * e) * (0.5 / D_MODEL)

    return pl.pallas_call(
        body,
        name="loss_head",
        grid=(t // TM,),
        in_specs=[_rows(TM, D_MODEL), _resident((1, D_MODEL)), _rows(TM, D_MODEL)],
        out_specs=[_rows(TM, D_MODEL), pl.BlockSpec((8, D_MODEL), lambda i: (0, 0)),
                   pl.BlockSpec((8, D_MODEL), lambda i: (0, 0))],
        out_shape=[jax.ShapeDtypeStruct((t, D_MODEL), F32), jax.ShapeDtypeStruct((8, D_MODEL), F32),
                   jax.ShapeDtypeStruct((8, D_MODEL), F32)],
        compiler_params=_cparams(("arbitrary",)),
    )(x, gamma, target)


def _place():
    x, y, c = lax.axis_index("x"), lax.axis_index("y"), lax.axis_index("c")
    chips = [(1 - x, y), (x, 1 - y), (1 - x, 1 - y)]
    return x, y, c, chips


class _Gather:
    per = 7

    def __init__(self, shards):
        n = len(shards)
        self.inputs = list(shards)
        self.out_shape = [jax.ShapeDtypeStruct((N_DEV * s.shape[0], s.shape[1]), s.dtype) for s in shards]
        self.scratch = [pltpu.SemaphoreType.DMA((n * self.per,)), pltpu.SemaphoreType.DMA((n * self.per,)),
                        pltpu.SemaphoreType.DMA((n,))]
        self.result = None

    def _parts(self, ins, outs, sems):
        send_sems, recv_sems, local_sems = sems
        x, y, c, chips = _place()
        me, sibling = (x, y, c), (x, y, 1 - c)
        n = len(ins)

        def rows(k, p):
            r = ins[k].shape[0]
            return outs[k].at[pl.ds((4 * p[0] + 2 * p[1] + p[2]) * r, r), :]

        def copy(k, slot, block, to, src=None):
            return pltpu.make_async_remote_copy(
                src_ref=rows(k, block) if src is None else src, dst_ref=rows(k, block),
                send_sem=send_sems.at[k * self.per + slot], recv_sem=recv_sems.at[k * self.per + slot],
                device_id=to, device_id_type=MESH)

        mine = [pltpu.make_async_copy(ins[k], rows(k, me), local_sems.at[k]) for k in range(n)]
        first = []
        for k in range(n):
            first.append(copy(k, 0, me, sibling, src=ins[k]))
            first += [copy(k, 1 + j, me, (*chip, c), src=ins[k]) for j, chip in enumerate(chips)]
        passed = [copy(k, 4 + j, (*chip, c), sibling) for j, chip in enumerate(chips) for k in range(n)]
        return n, c, me, sibling, chips, copy, mine, first, passed

    def start(self, ins, outs, sems):
        _, _, _, _, _, _, mine, first, _ = self._parts(ins, outs, sems)
        for cp in mine + first:
            cp.start()

    def forward(self, ins, outs, sems):
        n, c, me, _, chips, copy, _, _, passed = self._parts(ins, outs, sems)
        for j, chip in enumerate(chips):
            for k in range(n):
                copy(k, 1 + j, (*chip, c), me).wait_recv()
                passed[j * n + k].start()

    def finish(self, ins, outs, sems):
        n, c, me, sibling, chips, copy, mine, first, passed = self._parts(ins, outs, sems)
        for k in range(n):
            copy(k, 0, sibling, me).wait_recv()
            for j, chip in enumerate(chips):
                copy(k, 4 + j, (*chip, 1 - c), me).wait_recv()
        for cp in first + passed:
            cp.wait_send()
        for cp in mine:
            cp.wait()


class _PairExchange:
    def __init__(self, grads):
        n = len(grads)
        self.inputs = list(grads)
        self.out_shape = [jax.ShapeDtypeStruct((g.shape[0] // 2, g.shape[1]), g.dtype) for g in grads]
        self.scratch = [pltpu.SemaphoreType.DMA((n * N_CHIP,)), pltpu.SemaphoreType.DMA((n * N_CHIP,))]
        self.result = None

    def _copies(self, ins, outs, sems):
        send_sems, recv_sems = sems
        x, y, c, _ = _place()
        copies = []
        for k in range(len(ins)):
            r = ins[k].shape[0] // N_DEV
            for q in range(N_CHIP):
                copies.append(pltpu.make_async_remote_copy(
                    src_ref=ins[k].at[pl.ds((2 * q + 1 - c) * r, r), :], dst_ref=outs[k].at[pl.ds(q * r, r), :],
                    send_sem=send_sems.at[k * N_CHIP + q], recv_sem=recv_sems.at[k * N_CHIP + q],
                    device_id=(x, y, 1 - c), device_id_type=MESH))
        return copies

    def start(self, ins, outs, sems):
        for cp in self._copies(ins, outs, sems):
            cp.start()

    def forward(self, ins, outs, sems):
        pass

    def finish(self, ins, outs, sems):
        copies = self._copies(ins, outs, sems)
        for cp in copies:
            cp.wait_recv()
        for cp in copies:
            cp.wait_send()


class _ChipExchange(_PairExchange):
    def __init__(self, psums):
        n = len(psums)
        self.inputs = list(psums)
        self.out_shape = [jax.ShapeDtypeStruct((3 * p.shape[0] // N_CHIP, p.shape[1]), p.dtype) for p in psums]
        self.scratch = [pltpu.SemaphoreType.DMA((n * 3,)), pltpu.SemaphoreType.DMA((n * 3,))]
        self.result = None

    def _copies(self, ins, outs, sems):
        send_sems, recv_sems = sems
        _, _, c, chips = _place()
        copies = []
        for k in range(len(ins)):
            r = ins[k].shape[0] // N_CHIP
            for j, chip in enumerate(chips):
                copies.append(pltpu.make_async_remote_copy(
                    src_ref=ins[k].at[pl.ds((2 * chip[0] + chip[1]) * r, r), :], dst_ref=outs[k].at[pl.ds(j * r, r), :],
                    send_sem=send_sems.at[k * 3 + j], recv_sem=recv_sems.at[k * 3 + j],
                    device_id=(*chip, c), device_id_type=MESH))
        return copies


def _exchange_alone(xchg, name):
    n_in, n_out = len(xchg.inputs), len(xchg.out_shape)

    def body(*refs):
        ins, outs, sems = refs[:n_in], refs[n_in:n_in + n_out], refs[n_in + n_out:]
        xchg.start(ins, outs, sems)
        xchg.forward(ins, outs, sems)
        xchg.finish(ins, outs, sems)

    xchg.result = list(pl.pallas_call(
        body, name=name, in_specs=[_hbm()] * n_in, out_specs=[_hbm()] * n_out, out_shape=xchg.out_shape,
        scratch_shapes=xchg.scratch)(*xchg.inputs))
    return xchg.result


def _pair_sum(core, grads, recvd, name):
    n = len(grads)
    r = grads[0].shape[0] // N_DEV
    cdim = grads[0].shape[1]
    tr = r // 2 if r % 32 == 0 else r
    nt = r // tr

    def body(core_ref, *refs):
        del core_ref
        for k in range(n):
            refs[2 * n + k][...] = (refs[k][...].astype(F32) + refs[n + k][...].astype(F32)).astype(BF16)

    gspec = pl.BlockSpec((tr, cdim), lambda q, i, core_ref: ((2 * q + core_ref[0]) * nt + i, 0))
    rspec = pl.BlockSpec((tr, cdim), lambda q, i, core_ref: (q * nt + i, 0))
    return pl.pallas_call(
        body,
        name=name,
        grid_spec=pltpu.PrefetchScalarGridSpec(
            num_scalar_prefetch=1, grid=(N_CHIP, nt), in_specs=[gspec] * n + [rspec] * n, out_specs=[rspec] * n),
        out_shape=[jax.ShapeDtypeStruct((N_CHIP * r, cdim), BF16) for _ in range(n)],
        compiler_params=_cparams(("parallel", "parallel")),
    )(core, *grads, *recvd)


def _final_sum(chip, psums, recvd, name):
    n = len(psums)
    r = psums[0].shape[0] // N_CHIP
    cdim = psums[0].shape[1]
    tr = r // 2 if r % 32 == 0 else r
    nt = r // tr

    def body(chip_ref, *refs):
        del chip_ref
        for k in range(n):
            got = refs[n + k]
            tot = refs[k][...].astype(F32) + got[0].astype(F32)
            tot = tot + got[1].astype(F32)
            tot = tot + got[2].astype(F32)
            refs[2 * n + k][...] = tot

    pspec = pl.BlockSpec((tr, cdim), lambda i, chip_ref: (chip_ref[0] * nt + i, 0))
    rspec = pl.BlockSpec((3, tr, cdim), lambda i, chip_ref: (0, i, 0))
    ospec = pl.BlockSpec((tr, cdim), lambda i, chip_ref: (i, 0))
    return pl.pallas_call(
        body,
        name=name,
        grid_spec=pltpu.PrefetchScalarGridSpec(
            num_scalar_prefetch=1, grid=(nt,), in_specs=[pspec] * n + [rspec] * n, out_specs=[ospec] * n),
        out_shape=[jax.ShapeDtypeStruct((r, cdim), F32) for _ in range(n)],
        compiler_params=_cparams(("parallel",)),
    )(chip, *psums, *[g.reshape(3, r, cdim) for g in recvd])


SMALL_ROWS = 16


def _all_reduce_small(part):
    def body(p_ref, o_ref, buf, send_sems, recv_sems):
        x, y, c, _ = _place()
        me = 4 * x + 2 * y + c
        buf[me] = p_ref[...]
        copies = []
        for d in range(1, N_DEV):
            peer = me ^ d
            copies.append(pltpu.make_async_remote_copy(
                src_ref=p_ref, dst_ref=buf.at[me], send_sem=send_sems.at[d - 1], recv_sem=recv_sems.at[d - 1],
                device_id=(peer // 4, (peer // 2) % 2, peer % 2), device_id_type=MESH))
        for cp in copies:
            cp.start()
        for cp in copies:
            cp.wait_recv()
        for cp in copies:
            cp.wait_send()
        tot = buf[0]
        for d in range(1, N_DEV):
            tot = tot + buf[d]
        o_ref[...] = tot

    return pl.pallas_call(
        body,
        name="all_reduce_small",
        in_specs=[pl.BlockSpec(memory_space=pltpu.VMEM)],
        out_specs=pl.BlockSpec(memory_space=pltpu.VMEM),
        out_shape=jax.ShapeDtypeStruct(part.shape, F32),
        scratch_shapes=[pltpu.VMEM((N_DEV,) + part.shape, F32), pltpu.SemaphoreType.DMA((N_DEV - 1,)),
                        pltpu.SemaphoreType.DMA((N_DEV - 1,))],
    )(part)


def _adamw(ws, gs, ms, vs, name):
    n = len(ws)
    r, cdim = ws[0].shape
    tr = r
    for cand in (512, 256, 128, 176, 64):
        if r % cand == 0 and r > cand:
            tr = cand
            break
    c1 = 1.0 - ADAM_B1 ** ADAM_STEP
    c2 = 1.0 - ADAM_B2 ** ADAM_STEP

    def body(*refs):
        for k in range(n):
            w, g, m, v = (refs[j * n + k][...] for j in range(4))
            m2 = ADAM_B1 * m + (1.0 - ADAM_B1) * g
            v2 = ADAM_B2 * v + (1.0 - ADAM_B2) * (g * g)
            delta = -ADAM_LR * ((m2 / c1) / (jnp.sqrt(v2 / c2) + ADAM_EPS) + ADAM_WD * w)
            refs[4 * n + k][...] = delta
            refs[5 * n + k][...] = m2
            refs[6 * n + k][...] = v2

    spec = pl.BlockSpec((tr, cdim), lambda i: (i, 0))
    outs = pl.pallas_call(
        body,
        name=name,
        grid=(r // tr,),
        in_specs=[spec] * (4 * n),
        out_specs=[spec] * (3 * n),
        out_shape=[jax.ShapeDtypeStruct((r, cdim), F32)] * (3 * n),
        compiler_params=_cparams(("parallel",)),
    )(*ws, *gs, *ms, *vs)
    return outs[:n], outs[n:2 * n], outs[2 * n:]


def _bias_b():
    pad = B_PREV * CHUNK
    slopes = np.array([2.0 ** (-8.0 * (i + 1) / B_Q_HEADS) for i in range(B_Q_HEADS)], dtype=np.float32)
    dist = np.abs(np.arange(TQB)[:, None] - np.arange(TQB + pad)[None, :] + pad).astype(np.float32)
    bias = -slopes.reshape(B_KV_HEADS, B_GROUP, 1, 1) * dist[None, None]
    qc = (np.arange(TQB)[:, None] + pad) // CHUNK
    kc = np.arange(TQB + pad)[None, :] // CHUNK
    allowed = (kc <= qc) & (kc >= qc - B_PREV)
    bias = np.where(allowed[None, None], bias, np.float32(NEG_INF)).astype(np.float32)
    return bias.reshape(B_KV_HEADS, B_GROUP * TQB, TQB + pad)


def kernel(x, ffn1_norm, ffn1_w_gate, ffn1_w_up, ffn1_w_down, mix_norm, w_in, rel_bias, sinks, w_proj_a, w_proj_b, w_out, ffn2_norm, ffn2_w_gate, ffn2_w_up, ffn2_w_down, final_norm, loss_target, m_ffn1_norm, m_ffn1_w_gate, m_ffn1_w_up, m_ffn1_w_down, m_mix_norm, m_w_in, m_rel_bias, m_sinks, m_w_proj_a, m_w_proj_b, m_w_out, m_ffn2_norm, m_ffn2_w_gate, m_ffn2_w_up, m_ffn2_w_down, m_final_norm, v_ffn1_norm, v_ffn1_w_gate, v_ffn1_w_up, v_ffn1_w_down, v_mix_norm, v_w_in, v_rel_bias, v_sinks, v_w_proj_a, v_w_proj_b, v_w_out, v_ffn2_norm, v_ffn2_w_gate, v_ffn2_w_up, v_ffn2_w_down, v_final_norm):
    bsz, s_len, _ = x.shape
    t = bsz * s_len
    core = lax.axis_index("c").astype(jnp.int32).reshape(1)
    chip = (2 * lax.axis_index("x") + lax.axis_index("y")).astype(jnp.int32).reshape(1)

    def row_form(w):
        return w.astype(BF16).T

    wg1, wu1, wd1 = _exchange_alone(
        _Gather([row_form(ffn1_w_gate), row_form(ffn1_w_up), ffn1_w_down.astype(BF16)]), "gather_ffn1")
    gather_mix = _Gather([row_form(w_in), jnp.concatenate([row_form(w_proj_a), row_form(w_proj_b)], axis=1),
                          w_out.astype(BF16)])
    gather_ffn2_gate = _Gather([row_form(ffn2_w_gate)])
    gather_ffn2_rest = _Gather([row_form(ffn2_w_up), ffn2_w_down.astype(BF16)])

    x0 = x.reshape(t, D_MODEL)
    tgt = loss_target.reshape(t, D_MODEL)
    gam1, gam2, gam3, gam4 = (g.reshape(1, D_MODEL) for g in (ffn1_norm, mix_norm, ffn2_norm, final_norm))

    h1, g1, u1, a1, x1 = _ffn_fwd(x0, gam1, wg1, wu1, wd1, "ffn1_fwd", hosted=[gather_mix])
    win_t, proj_t, wout = gather_mix.result
    h2, qkv_a, qkv_b, gates = _proj_fwd(x1, gam2, win_t, hosted=[gather_ffn2_gate])
    (wg2,) = gather_ffn2_gate.result
    qkv_a3 = qkv_a.reshape(bsz, s_len, QKV_A)
    qkv_b3 = qkv_b.reshape(bsz, s_len, QKV_B)

    far = jnp.broadcast_to(rel_bias[:, REL_TABLE - 1:REL_TABLE], (A_HEADS, REL_WRAP // 2))
    tv = jnp.concatenate([far, jnp.flip(rel_bias, axis=1), jnp.zeros((A_HEADS, REL_WRAP // 2 - REL_TABLE), F32)], axis=1)
    bias_a = _bias_a_build(tv.reshape(A_HEADS, 1, REL_WRAP))
    bias_b = jnp.asarray(_bias_b())
    sink_rows = jnp.repeat(sinks.reshape(B_KV_HEADS, B_GROUP), TQB, axis=1).reshape(B_KV_HEADS, B_GROUP * TQB, 1)

    oa = _attn_a_fwd(qkv_a3, bias_a, hosted=[gather_ffn2_rest]).reshape(t, A_WIDTH)
    wu2, wd2 = gather_ffn2_rest.result
    ob = _attn_b_fwd(qkv_b3, bias_b, sink_rows).reshape(t, B_Q_WIDTH)
    x2, ya, yb, mg = _mix_out_fwd(x1, oa, ob, gates, proj_t, wout)
    h3, g2, u2, a2, x3 = _ffn_fwd(x2, gam3, wg2, wu2, wd2, "ffn2_fwd")

    dx3, dgam4, loss_part = _loss_head(x3, gam4, tgt)

    dx2, dg2, du2, db2, dgam3 = _ffn_bwd(dx3, x2, gam3, g2, u2, wg2, wu2, wd2, "ffn2_bwd")
    gw_ffn2 = [_mm_tn([dg2], h3, "grad_ffn2_gate"), _mm_tn([du2], h3, "grad_ffn2_up"),
               _mm_tn([a2], db2, "grad_ffn2_down")]
    pairx_ffn2 = _PairExchange(gw_ffn2)
    dxb, dya, dyb, doa, dob, dgates = _mix_out_bwd(dx2, gates, ya, yb, proj_t, wout, hosted=[pairx_ffn2])
    psum_ffn2 = _pair_sum(core, gw_ffn2, pairx_ffn2.result, "pair_sum_ffn2")
    gw_out = _mm_tn([mg], dxb, "grad_w_out")
    gw_proj = _mm_tn_proj(dya, dyb, oa, ob)

    chipx_ffn2 = _ChipExchange(psum_ffn2)
    dqa, dka, dva, dbias_a = _attn_a_bwd(qkv_a3, bias_a, doa.reshape(bsz, s_len, A_WIDTH), hosted=[chipx_ffn2])
    pairx_out = _PairExchange([gw_proj, gw_out])
    dqb, dkvb, dsink = _attn_b_bwd(qkv_b3, bias_b, sink_rows, dob.reshape(bsz, s_len, B_Q_WIDTH), hosted=[pairx_out])
    drel_lanes = _relbias_grad(dbias_a)
    dproj = [dqa.reshape(t, A_WIDTH), dka.reshape(t, A_WIDTH), dva.reshape(t, A_WIDTH), dqb.reshape(t, B_Q_WIDTH),
             dkvb.reshape(t, 2 * B_KV_WIDTH), dgates]

    dx1, db1, dgam2 = _proj_bwd(dx2, x1, gam2, dproj, win_t)
    gw_in = _mm_tn(dproj, h2, "grad_w_in")
    pairx_in = _PairExchange([gw_in])
    gw_d1 = _mm_tn([a1], db1, "grad_ffn1_down", hosted=[pairx_in])
    psum_mix = (_pair_sum(core, [gw_in], pairx_in.result, "pair_sum_w_in")
                + _pair_sum(core, [gw_proj, gw_out], pairx_out.result, "pair_sum_mix"))

    chipx_mix = _ChipExchange(psum_mix)
    pairx_d1 = _PairExchange([gw_d1])
    dx0, dg1, du1, _, dgam1 = _ffn_bwd(dx1, x0, gam1, g1, u1, wg1, wu1, wd1, "ffn1_bwd", hosted=[chipx_mix, pairx_d1])
    psum_d1 = _pair_sum(core, [gw_d1], pairx_d1.result, "pair_sum_ffn1_down")
    chipx_d1 = _ChipExchange(psum_d1)
    gw_g1 = _mm_tn([dg1], h1, "grad_ffn1_gate", hosted=[chipx_d1])
    from_sibling_g1 = _exchange_alone(_PairExchange([gw_g1]), "pair_exchange_ffn1_gate")
    psum_g1 = _pair_sum(core, [gw_g1], from_sibling_g1, "pair_sum_ffn1_gate")
    chipx_g1 = _ChipExchange(psum_g1)
    gw_u1 = _mm_tn([du1], h1, "grad_ffn1_up", hosted=[chipx_g1])
    from_sibling_u1 = _exchange_alone(_PairExchange([gw_u1]), "pair_exchange_ffn1_up")
    psum_u1 = _pair_sum(core, [gw_u1], from_sibling_u1, "pair_sum_ffn1_up")
    from_chips_u1 = _exchange_alone(_ChipExchange(psum_u1), "chip_exchange_ffn1_up")

    g_g1, g_u1, g_d1, g_g2, g_u2, g_d2 = _final_sum(
        chip, psum_g1 + psum_u1 + psum_d1 + psum_ffn2,
        chipx_g1.result + from_chips_u1 + chipx_d1.result + chipx_ffn2.result, "grad_sum_ffn")
    (g_in,) = _final_sum(chip, psum_mix[0:1], chipx_mix.result[0:1], "grad_sum_w_in")
    g_proj, g_out = _final_sum(chip, psum_mix[1:3], chipx_mix.result[1:3], "grad_sum_mix")
    grads = {
        "ffn1_w_gate": g_g1.T, "ffn1_w_up": g_u1.T, "ffn1_w_down": g_d1, "w_in": g_in.T,
        "w_proj_a": g_proj[:, 0:A_WIDTH].T, "w_proj_b": g_proj[:, A_WIDTH:].T, "w_out": g_out,
        "ffn2_w_gate": g_g2.T, "ffn2_w_up": g_u2.T, "ffn2_w_down": g_d2,
    }

    def row_of(v):
        return jnp.pad(v.reshape(1, -1), ((0, 0), (0, D_MODEL - v.size)))

    def table_rows(v):
        return jnp.pad(v, ((0, 0), (0, D_MODEL - REL_TABLE)))

    drel_local = jnp.flip(drel_lanes[:, 0, 0:REL_TABLE], axis=1)
    small_part = jnp.concatenate(
        [jnp.sum(dgam1, axis=0, keepdims=True), jnp.sum(dgam2, axis=0, keepdims=True),
         jnp.sum(dgam3, axis=0, keepdims=True), jnp.sum(dgam4, axis=0, keepdims=True),
         row_of(jnp.sum(loss_part)), row_of(dsink[:, 0:B_GROUP, 0]), jnp.zeros((2, D_MODEL), F32),
         table_rows(drel_local)], axis=0)
    small = _all_reduce_small(small_part)
    loss = small[4, 0]

    def pack(n1, n2, n3, n4, sk, tb):
        return jnp.concatenate([n1.reshape(1, -1), n2.reshape(1, -1), n3.reshape(1, -1), n4.reshape(1, -1),
                                jnp.zeros((1, D_MODEL), F32), row_of(sk), jnp.zeros((2, D_MODEL), F32), table_rows(tb)],
                               axis=0)

    live = np.zeros((SMALL_ROWS, D_MODEL), np.float32)
    live[0:4] = 1.0
    live[5, 0:B_Q_HEADS] = 1.0
    live[8:16, 0:REL_TABLE] = 1.0
    small_g = small * jnp.asarray(live)
    sw = pack(ffn1_norm, mix_norm, ffn2_norm, final_norm, sinks, rel_bias)
    sm = pack(m_ffn1_norm, m_mix_norm, m_ffn2_norm, m_final_norm, m_sinks, m_rel_bias)
    sv = pack(v_ffn1_norm, v_mix_norm, v_ffn2_norm, v_final_norm, v_sinks, v_rel_bias)
    (sd,), (snm,), (snv,) = _adamw([sw], [small_g], [sm], [sv], "adamw_small")

    def unpack(p):
        return {"ffn1_norm": p[0], "mix_norm": p[1], "ffn2_norm": p[2], "final_norm": p[3],
                "sinks": p[5, 0:B_Q_HEADS], "rel_bias": p[8:16, 0:REL_TABLE]}

    grads.update(unpack(small_g))
    delta, new_m, new_v = unpack(sd), unpack(snm), unpack(snv)

    wmv = {
        "ffn1_w_gate": (ffn1_w_gate, m_ffn1_w_gate, v_ffn1_w_gate), "ffn1_w_up": (ffn1_w_up, m_ffn1_w_up, v_ffn1_w_up),
        "ffn1_w_down": (ffn1_w_down, m_ffn1_w_down, v_ffn1_w_down), "w_in": (w_in, m_w_in, v_w_in),
        "w_proj_a": (w_proj_a, m_w_proj_a, v_w_proj_a), "w_proj_b": (w_proj_b, m_w_proj_b, v_w_proj_b),
        "w_out": (w_out, m_w_out, v_w_out),
        "ffn2_w_gate": (ffn2_w_gate, m_ffn2_w_gate, v_ffn2_w_gate), "ffn2_w_up": (ffn2_w_up, m_ffn2_w_up, v_ffn2_w_up),
        "ffn2_w_down": (ffn2_w_down, m_ffn2_w_down, v_ffn2_w_down),
    }
    groups = [("adamw_ffn_up", ["ffn1_w_gate", "ffn1_w_up", "ffn2_w_gate", "ffn2_w_up"]),
              ("adamw_ffn_down", ["ffn1_w_down", "ffn2_w_down"]), ("adamw_w_in", ["w_in"]),
              ("adamw_proj", ["w_proj_a", "w_proj_b"]), ("adamw_w_out", ["w_out"])]
    for gname, names in groups:
        ds_, ms_, vs_ = _adamw([wmv[n][0] for n in names], [grads[n] for n in names], [wmv[n][1] for n in names],
                               [wmv[n][2] for n in names], gname)
        for n, d_, m_, v_ in zip(names, ds_, ms_, vs_):
            delta[n], new_m[n], new_v[n] = d_, m_, v_

    order = ["ffn1_norm", "ffn1_w_gate", "ffn1_w_up", "ffn1_w_down", "mix_norm", "w_in", "rel_bias", "sinks",
             "w_proj_a", "w_proj_b", "w_out", "ffn2_norm", "ffn2_w_gate", "ffn2_w_up", "ffn2_w_down", "final_norm"]
    grad_x = dx0.reshape(bsz, s_len, D_MODEL)
    return (loss, grad_x, *[grads[n] for n in order], *[delta[n] for n in order], *[new_m[n] for n in order],
            *[new_v[n] for n in order])
```

```python
import numpy as np
import jax
import jax.numpy as jnp
from jax import lax
from jax.experimental import pallas as pl
from jax.experimental.pallas import tpu as pltpu

F32 = jnp.float32
BF16 = jnp.bfloat16

D_MODEL = 1024
D_FF = 2816
CHUNK = 64
D_HEAD = 64
A_HEADS = 8
A_PREV = 8
MAX_REL = 128
B_Q_HEADS = 8
B_KV_HEADS = 2
B_GROUP = B_Q_HEADS // B_KV_HEADS
B_PREV = 2
REL_TABLE = (CHUNK - 1) + MAX_REL + 1
A_WIDTH = A_HEADS * D_HEAD
B_Q_WIDTH = B_Q_HEADS * D_HEAD
B_KV_WIDTH = B_KV_HEADS * D_HEAD
QKV_A = 3 * A_WIDTH
QKV_B = B_Q_WIDTH + 2 * B_KV_WIDTH
IN_WIDTH = QKV_A + QKV_B + 2 * D_MODEL
EPS = 1e-6
NEG_INF = -1e30
SCALE = 1.0 / 8.0

ADAM_LR = 0.001
ADAM_B1 = 0.9
ADAM_B2 = 0.999
ADAM_EPS = 1e-08
ADAM_WD = 0.01
ADAM_STEP = 10

N_DEV = 8
N_CHIP = 4
MESH = pl.DeviceIdType.MESH

LANES = 128
TQ = 256
TM = 256
FC = 256
VMEM_LIMIT = 56 << 20


def _cparams(sem, vmem=VMEM_LIMIT):
    return pltpu.CompilerParams(dimension_semantics=sem, vmem_limit_bytes=vmem)


def _dot_nt(a, b):
    return lax.dot_general(a, b, (((1,), (1,)), ((), ())), preferred_element_type=F32)


def _dot_nn(a, b):
    return lax.dot_general(a, b, (((1,), (0,)), ((), ())), preferred_element_type=F32)


def _dot_tn(a, b):
    return lax.dot_general(a, b, (((0,), (0,)), ((), ())), preferred_element_type=F32)


def _resident(shape):
    nd = len(shape)
    return pl.BlockSpec(shape, lambda *_: (0,) * nd, pipeline_mode=pl.Buffered(1))


def _rows(tm, width):
    return pl.BlockSpec((tm, width), lambda i: (i, 0))


def _colsum8(v):
    tm, n = v.shape
    return jnp.sum(v.reshape(tm // 8, 8, n), axis=0)


def _rms(x):
    r = lax.rsqrt(jnp.mean(x * x, axis=-1, keepdims=True) + EPS)
    return x * r, r


def _rms_bwd(dh, xh, r, gamma):
    dxh = dh * gamma
    dx = r * (dxh - xh * jnp.mean(dxh * xh, axis=-1, keepdims=True))
    return dx, _colsum8(dh * xh)


def _hbm():
    return pl.BlockSpec(memory_space=pltpu.HBM)


def _call(body, *, name, grid, in_specs, out_specs, out_shape, args, sem, scratch_shapes=(), hosted=()):
    in_specs, out_specs, out_shape = list(in_specs), list(out_specs), list(out_shape)
    scratch_shapes = list(scratch_shapes)
    if not hosted:
        return pl.pallas_call(body, name=name, grid=grid, in_specs=in_specs, out_specs=out_specs, out_shape=out_shape,
                              scratch_shapes=scratch_shapes, compiler_params=_cparams(sem))(*args)
    n_in, n_out, n_scr = len(in_specs), len(out_specs), len(scratch_shapes)
    x_in = [a for x in hosted for a in x.inputs]
    x_out = [s for x in hosted for s in x.out_shape]
    x_scr = [s for x in hosted for s in x.scratch]
    steps = int(np.prod(grid))
    forward_step = max(steps - 3, 0)

    def wrapped(*refs):
        pos = [0]

        def take(k):
            pos[0] += k
            return refs[pos[0] - k:pos[0]]

        ins, xin, outs, xout, scr, xscr = (take(k) for k in (n_in, len(x_in), n_out, len(x_out), n_scr, len(x_scr)))
        step = 0
        for axis, extent in enumerate(grid):
            step = step * extent + pl.program_id(axis)
        own, oi, oo, osc = [], 0, 0, 0
        for x in hosted:
            own.append((xin[oi:oi + len(x.inputs)], xout[oo:oo + len(x.out_shape)], xscr[osc:osc + len(x.scratch)]))
            oi, oo, osc = oi + len(x.inputs), oo + len(x.out_shape), osc + len(x.scratch)

        def phase(method):
            for x, (i_, o_, s_) in zip(hosted, own):
                getattr(x, method)(i_, o_, s_)

        pl.when(step == 0)(lambda: phase("start"))
        body(*ins, *outs, *scr)
        pl.when(step == forward_step)(lambda: phase("forward"))
        pl.when(step == steps - 1)(lambda: phase("finish"))

    res = pl.pallas_call(
        wrapped, name=name, grid=grid, in_specs=in_specs + [_hbm()] * len(x_in),
        out_specs=out_specs + [_hbm()] * len(x_out), out_shape=out_shape + x_out,
        scratch_shapes=scratch_shapes + x_scr, compiler_params=_cparams(("arbitrary",) * len(grid)))(*args, *x_in)
    rest = list(res[n_out:])
    for x in hosted:
        x.result, rest = rest[:len(x.out_shape)], rest[len(x.out_shape):]
    return list(res[:n_out])


def _ffn_fwd(x, gamma, wg_t, wu_t, wd, name, hosted=()):
    t = x.shape[0]
    f = wg_t.shape[0]

    def body(x_ref, gam_ref, wg_ref, wu_ref, wd_ref, h_ref, g_ref, u_ref, a_ref, y_ref):
        xv = x_ref[...]
        xh, _ = _rms(xv)
        h = (xh * gam_ref[...]).astype(BF16)
        h_ref[...] = h
        for j in range(f // FC):
            sl = slice(j * FC, (j + 1) * FC)
            g = _dot_nt(h, wg_ref[sl, :])
            u = _dot_nt(h, wu_ref[sl, :])
            g_ref[:, sl] = g.astype(BF16)
            u_ref[:, sl] = u.astype(BF16)
            a_ref[:, sl] = (g * jax.nn.sigmoid(g) * u).astype(BF16)
        y_ref[...] = xv + 0.5 * _dot_nn(a_ref[...], wd_ref[...])

    return _call(
        body,
        name=name,
        grid=(t // TM,),
        in_specs=[_rows(TM, D_MODEL), _resident((1, D_MODEL)), _resident((f, D_MODEL)), _resident((f, D_MODEL)),
                  _resident((f, D_MODEL))],
        out_specs=[_rows(TM, D_MODEL), _rows(TM, f), _rows(TM, f), _rows(TM, f), _rows(TM, D_MODEL)],
        out_shape=[jax.ShapeDtypeStruct((t, D_MODEL), BF16), jax.ShapeDtypeStruct((t, f), BF16),
                   jax.ShapeDtypeStruct((t, f), BF16), jax.ShapeDtypeStruct((t, f), BF16),
                   jax.ShapeDtypeStruct((t, D_MODEL), F32)],
        args=(x, gamma, wg_t, wu_t, wd), sem=("parallel",), hosted=hosted)


def _ffn_bwd(d, x, gamma, g_act, u_act, wg_t, wu_t, wd, name, hosted=()):
    t = x.shape[0]
    f = wg_t.shape[0]

    def body(d_ref, x_ref, gam_ref, g_ref, u_ref, wg_ref, wu_ref, wd_ref, dx_ref, dg_ref, du_ref, db_ref, dgam_ref):
        dv = d_ref[...]
        db = (0.5 * dv).astype(BF16)
        db_ref[...] = db
        for j in range(f // FC):
            sl = slice(j * FC, (j + 1) * FC)
            da = _dot_nt(db, wd_ref[sl, :])
            g = g_ref[:, sl].astype(F32)
            u = u_ref[:, sl].astype(F32)
            s = jax.nn.sigmoid(g)
            dg_ref[:, sl] = (da * u * (s * (1.0 + g * (1.0 - s)))).astype(BF16)
            du_ref[:, sl] = (da * (g * s)).astype(BF16)
        dh = _dot_nn(dg_ref[...], wg_ref[...]) + _dot_nn(du_ref[...], wu_ref[...])
        xh, r = _rms(x_ref[...])
        dxn, dgam = _rms_bwd(dh, xh, r, gam_ref[...])
        dx_ref[...] = dv + dxn

        @pl.when(pl.program_id(0) == 0)
        def _():
            dgam_ref[...] = jnp.zeros_like(dgam_ref)

        dgam_ref[...] += dgam

    return _call(
        body,
        name=name,
        grid=(t // TM,),
        in_specs=[_rows(TM, D_MODEL), _rows(TM, D_MODEL), _resident((1, D_MODEL)), _rows(TM, f), _rows(TM, f),
                  _resident((f, D_MODEL)), _resident((f, D_MODEL)), _resident((f, D_MODEL))],
        out_specs=[_rows(TM, D_MODEL), _rows(TM, f), _rows(TM, f), _rows(TM, D_MODEL),
                   pl.BlockSpec((8, D_MODEL), lambda i: (0, 0))],
        out_shape=[jax.ShapeDtypeStruct((t, D_MODEL), F32), jax.ShapeDtypeStruct((t, f), BF16),
                   jax.ShapeDtypeStruct((t, f), BF16), jax.ShapeDtypeStruct((t, D_MODEL), BF16),
                   jax.ShapeDtypeStruct((8, D_MODEL), F32)],
        args=(d, x, gamma, g_act, u_act, wg_t, wu_t, wd), sem=("arbitrary",), hosted=hosted)


def _mm_tn(pieces, b, name, tile=256, hosted=()):
    t, n = b.shape
    npc = len(pieces)
    counts = [p.shape[1] // tile for p in pieces]
    los = [sum(counts[:k]) for k in range(npc)]
    total = sum(counts)

    def body(*refs):
        a_refs, b_ref, o_ref = refs[:npc], refs[npc], refs[npc + 1]
        i = pl.program_id(0)
        for k in range(npc):
            @pl.when(jnp.logical_and(i >= los[k], i < los[k] + counts[k]))
            def _(k=k):
                o_ref[...] = _dot_tn(a_refs[k][...], b_ref[...]).astype(BF16)

    def a_spec(k):
        return pl.BlockSpec((t, tile), lambda i: (0, jnp.clip(i - los[k], 0, counts[k] - 1)))

    return _call(
        body,
        name=name,
        grid=(total,),
        in_specs=[a_spec(k) for k in range(npc)] + [_resident((t, n))],
        out_specs=[pl.BlockSpec((tile, n), lambda i: (i, 0))],
        out_shape=[jax.ShapeDtypeStruct((total * tile, n), BF16)],
        args=(*pieces, b), sem=("parallel",), hosted=hosted)[0]


def _mm_tn_proj(dya, dyb, oa, ob, tile=256):
    t = dya.shape[0]

    def body(dya_ref, dyb_ref, oa_ref, ob_ref, o_ref):
        o_ref[:, 0:A_WIDTH] = _dot_tn(dya_ref[...], oa_ref[...]).astype(BF16)
        o_ref[:, A_WIDTH:A_WIDTH + B_Q_WIDTH] = _dot_tn(dyb_ref[...], ob_ref[...]).astype(BF16)

    col = pl.BlockSpec((t, tile), lambda i: (0, i))
    return pl.pallas_call(
        body,
        name="grad_proj",
        grid=(D_MODEL // tile,),
        in_specs=[col, col, _resident((t, A_WIDTH)), _resident((t, B_Q_WIDTH))],
        out_specs=pl.BlockSpec((tile, A_WIDTH + B_Q_WIDTH), lambda i: (i, 0)),
        out_shape=jax.ShapeDtypeStruct((D_MODEL, A_WIDTH + B_Q_WIDTH), BF16),
        compiler_params=_cparams(("parallel",)),
    )(dya, dyb, oa, ob)


def _proj_fwd(x, gamma, win_t, hosted=()):
    t = x.shape[0]

    def body(x_ref, gam_ref, w_ref, h_ref, qa_ref, qb_ref, gt_ref):
        xh, _ = _rms(x_ref[...])
        h = (xh * gam_ref[...]).astype(BF16)
        h_ref[...] = h
        for j in range(QKV_A // FC):
            qa_ref[:, j * FC:(j + 1) * FC] = _dot_nt(h, w_ref[j * FC:(j + 1) * FC, :]).astype(BF16)
        for j in range(QKV_B // FC):
            lo = QKV_A + j * FC
            qb_ref[:, j * FC:(j + 1) * FC] = _dot_nt(h, w_ref[lo:lo + FC, :]).astype(BF16)
        for j in range(2 * D_MODEL // FC):
            lo = QKV_A + QKV_B + j * FC
            gt_ref[:, j * FC:(j + 1) * FC] = _dot_nt(h, w_ref[lo:lo + FC, :])

    return _call(
        body,
        name="proj_fwd",
        grid=(t // TM,),
        in_specs=[_rows(TM, D_MODEL), _resident((1, D_MODEL)), _resident((IN_WIDTH, D_MODEL))],
        out_specs=[_rows(TM, D_MODEL), _rows(TM, QKV_A), _rows(TM, QKV_B), _rows(TM, 2 * D_MODEL)],
        out_shape=[jax.ShapeDtypeStruct((t, D_MODEL), BF16), jax.ShapeDtypeStruct((t, QKV_A), BF16),
                   jax.ShapeDtypeStruct((t, QKV_B), BF16), jax.ShapeDtypeStruct((t, 2 * D_MODEL), F32)],
        args=(x, gamma, win_t), sem=("parallel",), hosted=hosted)


def _proj_bwd(d, x, gamma, pieces, win_t, hosted=()):
    t = x.shape[0]
    npc = len(pieces)
    widths = [p.shape[1] for p in pieces]
    los = [sum(widths[:k]) for k in range(npc)]

    def body(*refs):
        d_ref, x_ref, gam_ref = refs[:3]
        p_refs = refs[3:3 + npc]
        w_ref, dx_ref, db_ref, dgam_ref = refs[3 + npc:]
        dh = _dot_nn(p_refs[0][...], w_ref[0:widths[0], :])
        for k in range(1, npc):
            dh += _dot_nn(p_refs[k][...], w_ref[los[k]:los[k] + widths[k], :])
        xh, r = _rms(x_ref[...])
        dxn, dgam = _rms_bwd(dh, xh, r, gam_ref[...])
        dx = d_ref[...] + dxn
        dx_ref[...] = dx
        db_ref[...] = (0.5 * dx).astype(BF16)

        @pl.when(pl.program_id(0) == 0)
        def _():
            dgam_ref[...] = jnp.zeros_like(dgam_ref)

        dgam_ref[...] += dgam

    return _call(
        body,
        name="proj_bwd",
        grid=(t // TM,),
        in_specs=[_rows(TM, D_MODEL), _rows(TM, D_MODEL), _resident((1, D_MODEL))] + [_rows(TM, w) for w in widths]
        + [_resident((IN_WIDTH, D_MODEL))],
        out_specs=[_rows(TM, D_MODEL), _rows(TM, D_MODEL), pl.BlockSpec((8, D_MODEL), lambda i: (0, 0))],
        out_shape=[jax.ShapeDtypeStruct((t, D_MODEL), F32), jax.ShapeDtypeStruct((t, D_MODEL), BF16),
                   jax.ShapeDtypeStruct((8, D_MODEL), F32)],
        args=(d, x, gamma, *pieces, win_t), sem=("arbitrary",), hosted=hosted)


def _lane_half(shape):
    return lax.broadcasted_iota(jnp.int32, shape, len(shape) - 1) // D_HEAD


def _band_softmax(q, kk, bias, sink, qs, pad):
    s = _dot_nt(q, kk) * SCALE + bias
    col = lax.broadcasted_iota(jnp.int32, s.shape, 1)
    s = jnp.where(col + qs >= pad, s, NEG_INF)
    m = jnp.max(s, axis=-1, keepdims=True)
    if sink is not None:
        m = jnp.maximum(m, sink)
    p = jnp.exp(s - m)
    den = jnp.sum(p, axis=-1, keepdims=True)
    if sink is not None:
        den = den + jnp.exp(sink - m)
    return p, m, 1.0 / den


def _fill_padded(dst, src, pad):
    dst[0:pad, :] = jnp.zeros((pad, LANES), dst.dtype)
    dst[pad:, :] = src


def _attn_a_fwd(qkv, bias, hosted=()):
    bsz, s_len, _ = qkv.shape
    pad = A_PREV * CHUNK
    band = TQ + pad
    npair = A_HEADS // 2

    def body(q_ref, k_ref, v_ref, b_ref, o_ref, kp, vp):
        i = pl.program_id(2)

        @pl.when(i == 0)
        def _():
            _fill_padded(kp, k_ref[...], pad)
            _fill_padded(vp, v_ref[...], pad)

        qs = pl.multiple_of(i * TQ, TQ)
        kk = kp[pl.ds(qs, band), :]
        vv = vp[pl.ds(qs, band), :]
        q = q_ref[...]
        half = _lane_half((1, LANES))
        outs = []
        for j in range(2):
            qm = jnp.where(half == j, q, jnp.zeros_like(q))
            p, _, inv = _band_softmax(qm, kk, b_ref[j], None, qs, pad)
            outs.append(_dot_nn(p.astype(BF16), vv) * inv)
        o_ref[...] = jnp.where(half == 0, outs[0], outs[1]).astype(BF16)

    return _call(
        body,
        name="attn_a_fwd",
        grid=(bsz, npair, s_len // TQ),
        in_specs=[pl.BlockSpec((None, TQ, LANES), lambda b, hp, i: (b, i, hp)),
                  pl.BlockSpec((None, s_len, LANES), lambda b, hp, i: (b, 0, npair + hp)),
                  pl.BlockSpec((None, s_len, LANES), lambda b, hp, i: (b, 0, 2 * npair + hp)),
                  pl.BlockSpec((2, TQ, band), lambda b, hp, i: (hp, 0, 0))],
        out_specs=[pl.BlockSpec((None, TQ, LANES), lambda b, hp, i: (b, i, hp))],
        out_shape=[jax.ShapeDtypeStruct((bsz, s_len, A_WIDTH), BF16)],
        scratch_shapes=[pltpu.VMEM((pad + s_len, LANES), BF16), pltpu.VMEM((pad + s_len, LANES), BF16)],
        args=(qkv, qkv, qkv, bias), sem=("arbitrary", "arbitrary", "arbitrary"), hosted=hosted)[0]


def _attn_a_bwd(qkv, bias, do, hosted=()):
    bsz, s_len, _ = qkv.shape
    pad = A_PREV * CHUNK
    band = TQ + pad
    npair = A_HEADS // 2
    n_i = s_len // TQ

    def body(q_ref, k_ref, v_ref, b_ref, do_ref, dq_ref, dk_ref, dv_ref, dbias_ref, kp, vp, dk_acc, dv_acc):
        b = pl.program_id(1)
        i = pl.program_id(2)

        @pl.when(i == 0)
        def _():
            _fill_padded(kp, k_ref[...], pad)
            _fill_padded(vp, v_ref[...], pad)
            dk_acc[...] = jnp.zeros_like(dk_acc)
            dv_acc[...] = jnp.zeros_like(dv_acc)

        @pl.when(jnp.logical_and(b == 0, i == 0))
        def _():
            dbias_ref[...] = jnp.zeros_like(dbias_ref)

        qs = pl.multiple_of(i * TQ, TQ)
        kk = kp[pl.ds(qs, band), :]
        vv = vp[pl.ds(qs, band), :]
        q = q_ref[...]
        dd = do_ref[...]
        half = _lane_half((1, LANES))
        dqs, dks, dvs = [], [], []
        for j in range(2):
            qm = jnp.where(half == j, q, jnp.zeros_like(q))
            dm = jnp.where(half == j, dd, jnp.zeros_like(dd))
            p, _, inv = _band_softmax(qm, kk, b_ref[j], None, qs, pad)
            pn = p * inv
            dp = _dot_nt(dm, vv)
            delta = jnp.sum(pn * dp, axis=-1, keepdims=True)
            ds = pn * (dp - delta)
            dbias_ref[j] += ds
            dsb = ds.astype(BF16)
            dqs.append(_dot_nn(dsb, kk))
            dks.append(_dot_tn(dsb, q))
            dvs.append(_dot_tn(pn.astype(BF16), dd))
        dq_ref[...] = (jnp.where(half == 0, dqs[0], dqs[1]) * SCALE).astype(BF16)
        dk_acc[pl.ds(qs, band), :] += jnp.where(half == 0, dks[0], dks[1]) * SCALE
        dv_acc[pl.ds(qs, band), :] += jnp.where(half == 0, dvs[0], dvs[1])

        @pl.when(i == n_i - 1)
        def _():
            dk_ref[...] = dk_acc[pad:, :].astype(BF16)
            dv_ref[...] = dv_acc[pad:, :].astype(BF16)

    qspec = pl.BlockSpec((None, TQ, LANES), lambda hp, b, i: (b, i, hp))
    kvout = pl.BlockSpec((None, s_len, LANES), lambda hp, b, i: (b, 0, hp))
    wide = jax.ShapeDtypeStruct((bsz, s_len, A_WIDTH), BF16)
    return _call(
        body,
        name="attn_a_bwd",
        grid=(npair, bsz, n_i),
        in_specs=[qspec,
                  pl.BlockSpec((None, s_len, LANES), lambda hp, b, i: (b, 0, npair + hp)),
                  pl.BlockSpec((None, s_len, LANES), lambda hp, b, i: (b, 0, 2 * npair + hp)),
                  pl.BlockSpec((2, TQ, band), lambda hp, b, i: (hp, 0, 0)),
                  qspec],
        out_specs=[qspec, kvout, kvout, pl.BlockSpec((2, TQ, band), lambda hp, b, i: (hp, 0, 0))],
        out_shape=[wide, wide, wide, jax.ShapeDtypeStruct((A_HEADS, TQ, band), F32)],
        scratch_shapes=[pltpu.VMEM((pad + s_len, LANES), BF16), pltpu.VMEM((pad + s_len, LANES), BF16),
                        pltpu.VMEM((pad + s_len, LANES), F32), pltpu.VMEM((pad + s_len, LANES), F32)],
        args=(qkv, qkv, qkv, bias, do), sem=("arbitrary", "arbitrary", "arbitrary"), hosted=hosted)


def _fill_padded_dup(dst, src, pad, h, half):
    other = pltpu.roll(src, D_HEAD, 1)
    _fill_padded(dst, jnp.where(half == h, src, other), pad)


def _attn_b_fwd(qkv, bias, sink):
    bsz, s_len, _ = qkv.shape
    pad = B_PREV * CHUNK
    band = TQ + pad
    kcol = B_Q_WIDTH // LANES
    npair = B_Q_HEADS // 2

    def body(q_ref, k_ref, v_ref, b_ref, s_ref, o_ref, kp, vp):
        h = pl.program_id(1) // (B_GROUP // 2)
        i = pl.program_id(2)
        half = _lane_half((1, LANES))

        @pl.when(i == 0)
        def _():
            _fill_padded_dup(kp, k_ref[...], pad, h, half)
            _fill_padded_dup(vp, v_ref[...], pad, h, half)

        qs = pl.multiple_of(i * TQ, TQ)
        kk = kp[pl.ds(qs, band), :]
        vv = vp[pl.ds(qs, band), :]
        q = q_ref[...]
        outs = []
        for j in range(2):
            qm = jnp.where(half == j, q, jnp.zeros_like(q))
            p, _, inv = _band_softmax(qm, kk, b_ref[j], s_ref[j][0:1, 0:1], qs, pad)
            outs.append(_dot_nn(p.astype(BF16), vv) * inv)
        o_ref[...] = jnp.where(half == 0, outs[0], outs[1]).astype(BF16)

    return pl.pallas_call(
        body,
        name="attn_b_fwd",
        grid=(bsz, npair, s_len // TQ),
        in_specs=[pl.BlockSpec((None, TQ, LANES), lambda b, hp, i: (b, i, hp)),
                  pl.BlockSpec((None, s_len, LANES), lambda b, hp, i: (b, 0, kcol)),
                  pl.BlockSpec((None, s_len, LANES), lambda b, hp, i: (b, 0, kcol + 1)),
                  pl.BlockSpec((2, TQ, band), lambda b, hp, i: (hp, 0, 0)),
                  pl.BlockSpec((2, 8, LANES), lambda b, hp, i: (hp, 0, 0))],
        out_specs=pl.BlockSpec((None, TQ, LANES), lambda b, hp, i: (b, i, hp)),
        out_shape=jax.ShapeDtypeStruct((bsz, s_len, B_Q_WIDTH), BF16),
        scratch_shapes=[pltpu.VMEM((pad + s_len, LANES), BF16), pltpu.VMEM((pad + s_len, LANES), BF16)],
        compiler_params=_cparams(("arbitrary", "arbitrary", "arbitrary")),
    )(qkv, qkv, qkv, bias, sink)


def _attn_b_bwd(qkv, bias, sink, do, hosted=()):
    bsz, s_len, _ = qkv.shape
    pad = B_PREV * CHUNK
    band = TQ + pad
    kcol = B_Q_WIDTH // LANES
    npair = B_Q_HEADS // 2
    n_i = s_len // TQ

    def body(q_ref, k_ref, v_ref, b_ref, s_ref, do_ref, dq_ref, dkv_ref, dsink_ref, kp, vp, dk_acc, dv_acc):
        b = pl.program_id(0)
        hp = pl.program_id(1)
        i = pl.program_id(2)
        h = hp // (B_GROUP // 2)
        half = _lane_half((1, LANES))

        @pl.when(i == 0)
        def _():
            _fill_padded_dup(kp, k_ref[...], pad, h, half)
            _fill_padded_dup(vp, v_ref[...], pad, h, half)

        @pl.when(jnp.logical_and(hp == 0, i == 0))
        def _():
            dk_acc[...] = jnp.zeros_like(dk_acc)
            dv_acc[...] = jnp.zeros_like(dv_acc)

        @pl.when(jnp.logical_and(b == 0, jnp.logical_and(hp == 0, i == 0)))
        def _():
            dsink_ref[...] = jnp.zeros_like(dsink_ref)

        qs = pl.multiple_of(i * TQ, TQ)
        kk = kp[pl.ds(qs, band), :]
        vv = vp[pl.ds(qs, band), :]
        q = q_ref[...]
        dd = do_ref[...]
        dqs, dks, dvs = [], [], []
        for j in range(2):
            qm = jnp.where(half == j, q, jnp.zeros_like(q))
            dm = jnp.where(half == j, dd, jnp.zeros_like(dd))
            sink = s_ref[j][0:1, 0:1]
            p, m, inv = _band_softmax(qm, kk, b_ref[j], sink, qs, pad)
            pn = p * inv
            dp = _dot_nt(dm, vv)
            delta = jnp.sum(pn * dp, axis=-1, keepdims=True)
            ds = pn * (dp - delta)
            dsb = ds.astype(BF16)
            dqs.append(_dot_nn(dsb, kk))
            dks.append(_dot_tn(dsb, q))
            dvs.append(_dot_tn(pn.astype(BF16), dd))
            dsk = jnp.sum(-(jnp.exp(sink - m) * inv) * delta, axis=0, keepdims=True)
            dsink_ref[2 * hp + j] += jnp.broadcast_to(dsk, (8, LANES))
        dq_ref[...] = (jnp.where(half == 0, dqs[0], dqs[1]) * SCALE).astype(BF16)
        dk2 = jnp.where(half == 0, dks[0], dks[1]) * SCALE
        dv2 = jnp.where(half == 0, dvs[0], dvs[1])
        dk_acc[pl.ds(qs, band), :] += jnp.where(half == h, dk2 + pltpu.roll(dk2, D_HEAD, 1), 0.0)
        dv_acc[pl.ds(qs, band), :] += jnp.where(half == h, dv2 + pltpu.roll(dv2, D_HEAD, 1), 0.0)

        @pl.when(jnp.logical_and(hp == npair - 1, i == n_i - 1))
        def _():
            dkv_ref[:, 0:LANES] = dk_acc[pad:, :].astype(BF16)
            dkv_ref[:, LANES:2 * LANES] = dv_acc[pad:, :].astype(BF16)

    qspec = pl.BlockSpec((None, TQ, LANES), lambda b, hp, i: (b, i, hp))
    return _call(
        body,
        name="attn_b_bwd",
        grid=(bsz, npair, n_i),
        in_specs=[qspec,
                  pl.BlockSpec((None, s_len, LANES), lambda b, hp, i: (b, 0, kcol)),
                  pl.BlockSpec((None, s_len, LANES), lambda b, hp, i: (b, 0, kcol + 1)),
                  pl.BlockSpec((2, TQ, band), lambda b, hp, i: (hp, 0, 0)),
                  pl.BlockSpec((2, 8, LANES), lambda b, hp, i: (hp, 0, 0)),
                  qspec],
        out_specs=[qspec, pl.BlockSpec((None, s_len, 2 * LANES), lambda b, hp, i: (b, 0, 0)),
                   pl.BlockSpec((B_Q_HEADS, 8, LANES), lambda b, hp, i: (0, 0, 0))],
        out_shape=[jax.ShapeDtypeStruct((bsz, s_len, B_Q_WIDTH), BF16),
                   jax.ShapeDtypeStruct((bsz, s_len, 2 * B_KV_WIDTH), BF16),
                   jax.ShapeDtypeStruct((B_Q_HEADS, 8, LANES), F32)],
        scratch_shapes=[pltpu.VMEM((pad + s_len, LANES), BF16), pltpu.VMEM((pad + s_len, LANES), BF16),
                        pltpu.VMEM((pad + s_len, LANES), F32), pltpu.VMEM((pad + s_len, LANES), F32)],
        args=(qkv, qkv, qkv, bias, sink, do), sem=("arbitrary", "arbitrary", "arbitrary"), hosted=hosted)


REL_COLS = 3 * 128
REL_WRAP = 512


def _bias_a_build(tv):
    h = tv.shape[0]
    pad = A_PREV * CHUNK
    band = TQ + pad

    def body(tv_ref, o_ref):
        row = tv_ref[...]
        x = jnp.broadcast_to(row, (TQ, REL_WRAP))
        r = lax.broadcasted_iota(jnp.int32, x.shape, 0)
        for bit in range(8):
            sh = 1 << bit
            x = jnp.where((r & sh) != 0, pltpu.roll(x, sh, 1), x)
        far = jnp.broadcast_to(row[:, 0:1], (TQ, band - REL_COLS))
        full = jnp.concatenate([far, x[:, REL_WRAP // 2:REL_WRAP], x[:, 0:REL_COLS - REL_WRAP // 2]], axis=1)
        qc = (lax.broadcasted_iota(jnp.int32, full.shape, 0) + pad) // CHUNK
        kc = lax.broadcasted_iota(jnp.int32, full.shape, 1) // CHUNK
        ok = jnp.logical_and(kc <= qc, kc >= qc - A_PREV)
        o_ref[...] = jnp.where(ok, full, NEG_INF)

    return pl.pallas_call(
        body,
        name="bias_a_build",
        grid=(h,),
        in_specs=[pl.BlockSpec((None, 1, REL_WRAP), lambda hh: (hh, 0, 0))],
        out_specs=pl.BlockSpec((None, TQ, band), lambda hh: (hh, 0, 0)),
        out_shape=jax.ShapeDtypeStruct((h, TQ, band), F32),
        compiler_params=_cparams(("parallel",)),
    )(tv)


def _relbias_grad(dbias):
    h, rows, band = dbias.shape
    off = band - REL_COLS

    def body(d_ref, o_ref):
        x = d_ref[...]
        r = lax.broadcasted_iota(jnp.int32, x.shape, 0)
        c = lax.broadcasted_iota(jnp.int32, x.shape, 1) - r
        x = jnp.where(jnp.logical_and(c >= 1, c < REL_TABLE), x, 0.0)
        for bit in range(8):
            sh = 1 << bit
            x = jnp.where((r & sh) != 0, pltpu.roll(x, REL_COLS - sh, 1), x)
        diag = jnp.sum(x, axis=0, keepdims=True)
        lane = lax.broadcasted_iota(jnp.int32, diag.shape, 1)
        diag = jnp.where(jnp.logical_and(lane >= 1, lane < REL_TABLE), diag, 0.0)
        rest = -jnp.sum(diag, axis=1, keepdims=True)
        o_ref[...] = jnp.broadcast_to(jnp.where(lane == 0, rest, diag), o_ref.shape)

    return pl.pallas_call(
        body,
        name="relbias_grad",
        grid=(h,),
        in_specs=[pl.BlockSpec((None, rows, REL_COLS), lambda hh: (hh, 0, off // REL_COLS))],
        out_specs=pl.BlockSpec((None, 8, REL_COLS), lambda hh: (hh, 0, 0)),
        out_shape=jax.ShapeDtypeStruct((h, 8, REL_COLS), F32),
        compiler_params=_cparams(("parallel",)),
    )(dbias)


def _mix_out_fwd(x, oa, ob, gates, proj_t, wout):
    t = x.shape[0]

    def body(x_ref, oa_ref, ob_ref, gt_ref, pt_ref, wo_ref, y_ref, ya_ref, yb_ref, mg_ref):
        ya = _dot_nt(oa_ref[...], pt_ref[:, 0:A_WIDTH])
        yb = _dot_nt(ob_ref[...], pt_ref[:, A_WIDTH:A_WIDTH + B_Q_WIDTH])
        ya_ref[...] = ya.astype(BF16)
        yb_ref[...] = yb.astype(BF16)
        mg = jax.nn.sigmoid(gt_ref[:, 0:D_MODEL]) * ya + jax.nn.sigmoid(gt_ref[:, D_MODEL:2 * D_MODEL]) * yb
        mgb = mg.astype(BF16)
        mg_ref[...] = mgb
        y_ref[...] = x_ref[...] + _dot_nn(mgb, wo_ref[...])

    return pl.pallas_call(
        body,
        name="mix_out_fwd",
        grid=(t // TM,),
        in_specs=[_rows(TM, D_MODEL), _rows(TM, A_WIDTH), _rows(TM, B_Q_WIDTH), _rows(TM, 2 * D_MODEL),
                  _resident((D_MODEL, A_WIDTH + B_Q_WIDTH)), _resident((D_MODEL, D_MODEL))],
        out_specs=[_rows(TM, D_MODEL), _rows(TM, D_MODEL), _rows(TM, D_MODEL), _rows(TM, D_MODEL)],
        out_shape=[jax.ShapeDtypeStruct((t, D_MODEL), F32), jax.ShapeDtypeStruct((t, D_MODEL), BF16),
                   jax.ShapeDtypeStruct((t, D_MODEL), BF16), jax.ShapeDtypeStruct((t, D_MODEL), BF16)],
        compiler_params=_cparams(("parallel",)),
    )(x, oa, ob, gates, proj_t, wout)


def _mix_out_bwd(d, gates, ya, yb, proj_t, wout, hosted=()):
    t = d.shape[0]

    def body(d_ref, gt_ref, ya_ref, yb_ref, pt_ref, wo_ref, db_ref, dya_ref, dyb_ref, doa_ref, dob_ref, dgt_ref):
        db = d_ref[...].astype(BF16)
        db_ref[...] = db
        dmg = _dot_nt(db, wo_ref[...])
        sa = jax.nn.sigmoid(gt_ref[:, 0:D_MODEL])
        sb = jax.nn.sigmoid(gt_ref[:, D_MODEL:2 * D_MODEL])
        dya = (dmg * sa).astype(BF16)
        dyb = (dmg * sb).astype(BF16)
        dya_ref[...] = dya
        dyb_ref[...] = dyb
        dgt_ref[:, 0:D_MODEL] = (dmg * ya_ref[...].astype(F32) * (sa * (1.0 - sa))).astype(BF16)
        dgt_ref[:, D_MODEL:2 * D_MODEL] = (dmg * yb_ref[...].astype(F32) * (sb * (1.0 - sb))).astype(BF16)
        doa_ref[...] = _dot_nn(dya, pt_ref[:, 0:A_WIDTH]).astype(BF16)
        dob_ref[...] = _dot_nn(dyb, pt_ref[:, A_WIDTH:A_WIDTH + B_Q_WIDTH]).astype(BF16)

    return _call(
        body,
        name="mix_out_bwd",
        grid=(t // TM,),
        in_specs=[_rows(TM, D_MODEL), _rows(TM, 2 * D_MODEL), _rows(TM, D_MODEL), _rows(TM, D_MODEL),
                  _resident((D_MODEL, A_WIDTH + B_Q_WIDTH)), _resident((D_MODEL, D_MODEL))],
        out_specs=[_rows(TM, D_MODEL), _rows(TM, D_MODEL), _rows(TM, D_MODEL), _rows(TM, A_WIDTH),
                   _rows(TM, B_Q_WIDTH), _rows(TM, 2 * D_MODEL)],
        out_shape=[jax.ShapeDtypeStruct((t, D_MODEL), BF16), jax.ShapeDtypeStruct((t, D_MODEL), BF16),
                   jax.ShapeDtypeStruct((t, D_MODEL), BF16), jax.ShapeDtypeStruct((t, A_WIDTH), BF16),
                   jax.ShapeDtypeStruct((t, B_Q_WIDTH), BF16), jax.ShapeDtypeStruct((t, 2 * D_MODEL), BF16)],
        args=(d, gates, ya, yb, proj_t, wout), sem=("parallel",), hosted=hosted)


def _loss_head(x, gamma, target):
    t = x.shape[0]

    def body(x_ref, gam_ref, t_ref, dx_ref, dgam_ref, loss_ref):
        xh, r = _rms(x_ref[...])
        gam = gam_ref[...]
        e = xh * gam - t_ref[...]
        dy = e * (1.0 / D_MODEL)
        dxn, dgam = _rms_bwd(dy, xh, r, gam)
        dx_ref[...] = dxn

        @pl.when(pl.program_id(0) == 0)
        def _():
            dgam_ref[...] = jnp.zeros_like(dgam_ref)
            loss_ref[...] = jnp.zeros_like(loss_ref)

        dgam_ref[...] += dgam
        loss_ref[...] += _colsum8(e * e) * (0.5 / D_MODEL)

    return pl.pallas_call(
        body,
        name="loss_head",
        grid=(t // TM,),
        in_specs=[_rows(TM, D_MODEL), _resident((1, D_MODEL)), _rows(TM, D_MODEL)],
        out_specs=[_rows(TM, D_MODEL), pl.BlockSpec((8, D_MODEL), lambda i: (0, 0)),
                   pl.BlockSpec((8, D_MODEL), lambda i: (0, 0))],
        out_shape=[jax.ShapeDtypeStruct((t, D_MODEL), F32), jax.ShapeDtypeStruct((8, D_MODEL), F32),
                   jax.ShapeDtypeStruct((8, D_MODEL), F32)],
        compiler_params=_cparams(("arbitrary",)),
    )(x, gamma, target)


def _place():
    x, y, c = lax.axis_index("x"), lax.axis_index("y"), lax.axis_index("c")
    chips = [(1 - x, y), (x, 1 - y), (1 - x, 1 - y)]
    return x, y, c, chips


class _Gather:
    per = 7

    def __init__(self, shards):
        n = len(shards)
        self.inputs = list(shards)
        self.out_shape = [jax.ShapeDtypeStruct((N_DEV * s.shape[0], s.shape[1]), s.dtype) for s in shards]
        self.scratch = [pltpu.SemaphoreType.DMA((n * self.per,)), pltpu.SemaphoreType.DMA((n * self.per,)),
                        pltpu.SemaphoreType.DMA((n,))]
        self.result = None

    def _parts(self, ins, outs, sems):
        send_sems, recv_sems, local_sems = sems
        x, y, c, chips = _place()
        me, sibling = (x, y, c), (x, y, 1 - c)
        n = len(ins)

        def rows(k, p):
            r = ins[k].shape[0]
            return outs[k].at[pl.ds((4 * p[0] + 2 * p[1] + p[2]) * r, r), :]

        def copy(k, slot, block, to, src=None):
            return pltpu.make_async_remote_copy(
                src_ref=rows(k, block) if src is None else src, dst_ref=rows(k, block),
                send_sem=send_sems.at[k * self.per + slot], recv_sem=recv_sems.at[k * self.per + slot],
                device_id=to, device_id_type=MESH)

        mine = [pltpu.make_async_copy(ins[k], rows(k, me), local_sems.at[k]) for k in range(n)]
        first = []
        for k in range(n):
            first.append(copy(k, 0, me, sibling, src=ins[k]))
            first += [copy(k, 1 + j, me, (*chip, c), src=ins[k]) for j, chip in enumerate(chips)]
        passed = [copy(k, 4 + j, (*chip, c), sibling) for j, chip in enumerate(chips) for k in range(n)]
        return n, c, me, sibling, chips, copy, mine, first, passed

    def start(self, ins, outs, sems):
        _, _, _, _, _, _, mine, first, _ = self._parts(ins, outs, sems)
        for cp in mine + first:
            cp.start()

    def forward(self, ins, outs, sems):
        n, c, me, _, chips, copy, _, _, passed = self._parts(ins, outs, sems)
        for j, chip in enumerate(chips):
            for k in range(n):
                copy(k, 1 + j, (*chip, c), me).wait_recv()
                passed[j * n + k].start()

    def finish(self, ins, outs, sems):
        n, c, me, sibling, chips, copy, mine, first, passed = self._parts(ins, outs, sems)
        for k in range(n):
            copy(k, 0, sibling, me).wait_recv()
            for j, chip in enumerate(chips):
                copy(k, 4 + j, (*chip, 1 - c), me).wait_recv()
        for cp in first + passed:
            cp.wait_send()
        for cp in mine:
            cp.wait()


class _PairExchange:
    def __init__(self, grads):
        n = len(grads)
        self.inputs = list(grads)
        self.out_shape = [jax.ShapeDtypeStruct((g.shape[0] // 2, g.shape[1]), g.dtype) for g in grads]
        self.scratch = [pltpu.SemaphoreType.DMA((n * N_CHIP,)), pltpu.SemaphoreType.DMA((n * N_CHIP,))]
        self.result = None

    def _copies(self, ins, outs, sems):
        send_sems, recv_sems = sems
        x, y, c, _ = _place()
        copies = []
        for k in range(len(ins)):
            r = ins[k].shape[0] // N_DEV
            for q in range(N_CHIP):
                copies.append(pltpu.make_async_remote_copy(
                    src_ref=ins[k].at[pl.ds((2 * q + 1 - c) * r, r), :], dst_ref=outs[k].at[pl.ds(q * r, r), :],
                    send_sem=send_sems.at[k * N_CHIP + q], recv_sem=recv_sems.at[k * N_CHIP + q],
                    device_id=(x, y, 1 - c), device_id_type=MESH))
        return copies

    def start(self, ins, outs, sems):
        for cp in self._copies(ins, outs, sems):
            cp.start()

    def forward(self, ins, outs, sems):
        pass

    def finish(self, ins, outs, sems):
        copies = self._copies(ins, outs, sems)
        for cp in copies:
            cp.wait_recv()
        for cp in copies:
            cp.wait_send()


class _ChipExchange(_PairExchange):
    def __init__(self, psums):
        n = len(psums)
        self.inputs = list(psums)
        self.out_shape = [jax.ShapeDtypeStruct((3 * p.shape[0] // N_CHIP, p.shape[1]), p.dtype) for p in psums]
        self.scratch = [pltpu.SemaphoreType.DMA((n * 3,)), pltpu.SemaphoreType.DMA((n * 3,))]
        self.result = None

    def _copies(self, ins, outs, sems):
        send_sems, recv_sems = sems
        _, _, c, chips = _place()
        copies = []
        for k in range(len(ins)):
            r = ins[k].shape[0] // N_CHIP
            for j, chip in enumerate(chips):
                copies.append(pltpu.make_async_remote_copy(
                    src_ref=ins[k].at[pl.ds((2 * chip[0] + chip[1]) * r, r), :], dst_ref=outs[k].at[pl.ds(j * r, r), :],
                    send_sem=send_sems.at[k * 3 + j], recv_sem=recv_sems.at[k * 3 + j],
                    device_id=(*chip, c), device_id_type=MESH))
        return copies


def _exchange_alone(xchg, name):
    n_in, n_out = len(xchg.inputs), len(xchg.out_shape)

    def body(*refs):
        ins, outs, sems = refs[:n_in], refs[n_in:n_in + n_out], refs[n_in + n_out:]
        xchg.start(ins, outs, sems)
        xchg.forward(ins, outs, sems)
        xchg.finish(ins, outs, sems)

    xchg.result = list(pl.pallas_call(
        body, name=name, in_specs=[_hbm()] * n_in, out_specs=[_hbm()] * n_out, out_shape=xchg.out_shape,
        scratch_shapes=xchg.scratch)(*xchg.inputs))
    return xchg.result


def _pair_sum(core, grads, recvd, name):
    n = len(grads)
    r = grads[0].shape[0] // N_DEV
    cdim = grads[0].shape[1]
    tr = r // 2 if r % 32 == 0 else r
    nt = r // tr

    def body(core_ref, *refs):
        del core_ref
        for k in range(n):
            refs[2 * n + k][...] = (refs[k][...].astype(F32) + refs[n + k][...].astype(F32)).astype(BF16)

    gspec = pl.BlockSpec((tr, cdim), lambda q, i, core_ref: ((2 * q + core_ref[0]) * nt + i, 0))
    rspec = pl.BlockSpec((tr, cdim), lambda q, i, core_ref: (q * nt + i, 0))
    return pl.pallas_call(
        body,
        name=name,
        grid_spec=pltpu.PrefetchScalarGridSpec(
            num_scalar_prefetch=1, grid=(N_CHIP, nt), in_specs=[gspec] * n + [rspec] * n, out_specs=[rspec] * n),
        out_shape=[jax.ShapeDtypeStruct((N_CHIP * r, cdim), BF16) for _ in range(n)],
        compiler_params=_cparams(("parallel", "parallel")),
    )(core, *grads, *recvd)


def _final_sum(chip, psums, recvd, name):
    n = len(psums)
    r = psums[0].shape[0] // N_CHIP
    cdim = psums[0].shape[1]
    tr = r // 2 if r % 32 == 0 else r
    nt = r // tr

    def body(chip_ref, *refs):
        del chip_ref
        for k in range(n):
            got = refs[n + k]
            tot = refs[k][...].astype(F32) + got[0].astype(F32)
            tot = tot + got[1].astype(F32)
            tot = tot + got[2].astype(F32)
            refs[2 * n + k][...] = tot

    pspec = pl.BlockSpec((tr, cdim), lambda i, chip_ref: (chip_ref[0] * nt + i, 0))
    rspec = pl.BlockSpec((3, tr, cdim), lambda i, chip_ref: (0, i, 0))
    ospec = pl.BlockSpec((tr, cdim), lambda i, chip_ref: (i, 0))
    return pl.pallas_call(
        body,
        name=name,
        grid_spec=pltpu.PrefetchScalarGridSpec(
            num_scalar_prefetch=1, grid=(nt,), in_specs=[pspec] * n + [rspec] * n, out_specs=[ospec] * n),
        out_shape=[jax.ShapeDtypeStruct((r, cdim), F32) for _ in range(n)],
        compiler_params=_cparams(("parallel",)),
    )(chip, *psums, *[g.reshape(3, r, cdim) for g in recvd])


SMALL_ROWS = 16


def _all_reduce_small(part):
    def body(p_ref, o_ref, buf, send_sems, recv_sems):
        x, y, c, _ = _place()
        me = 4 * x + 2 * y + c
        buf[me] = p_ref[...]
        copies = []
        for d in range(1, N_DEV):
            peer = me ^ d
            copies.append(pltpu.make_async_remote_copy(
                src_ref=p_ref, dst_ref=buf.at[me], send_sem=send_sems.at[d - 1], recv_sem=recv_sems.at[d - 1],
                device_id=(peer // 4, (peer // 2) % 2, peer % 2), device_id_type=MESH))
        for cp in copies:
            cp.start()
        for cp in copies:
            cp.wait_recv()
        for cp in copies:
            cp.wait_send()
        tot = buf[0]
        for d in range(1, N_DEV):
            tot = tot + buf[d]
        o_ref[...] = tot

    return pl.pallas_call(
        body,
        name="all_reduce_small",
        in_specs=[pl.BlockSpec(memory_space=pltpu.VMEM)],
        out_specs=pl.BlockSpec(memory_space=pltpu.VMEM),
        out_shape=jax.ShapeDtypeStruct(part.shape, F32),
        scratch_shapes=[pltpu.VMEM((N_DEV,) + part.shape, F32), pltpu.SemaphoreType.DMA((N_DEV - 1,)),
                        pltpu.SemaphoreType.DMA((N_DEV - 1,))],
    )(part)


def _adamw(ws, gs, ms, vs, name):
    n = len(ws)
    r, cdim = ws[0].shape
    tr = r
    for cand in (512, 256, 128, 176, 64):
        if r % cand == 0 and r > cand:
            tr = cand
            break
    c1 = 1.0 - ADAM_B1 ** ADAM_STEP
    c2 = 1.0 - ADAM_B2 ** ADAM_STEP

    def body(*refs):
        for k in range(n):
            w, g, m, v = (refs[j * n + k][...] for j in range(4))
            m2 = ADAM_B1 * m + (1.0 - ADAM_B1) * g
            v2 = ADAM_B2 * v + (1.0 - ADAM_B2) * (g * g)
            delta = -ADAM_LR * ((m2 / c1) / (jnp.sqrt(v2 / c2) + ADAM_EPS) + ADAM_WD * w)
            refs[4 * n + k][...] = delta
            refs[5 * n + k][...] = m2
            refs[6 * n + k][...] = v2

    spec = pl.BlockSpec((tr, cdim), lambda i: (i, 0))
    outs = pl.pallas_call(
        body,
        name=name,
        grid=(r // tr,),
        in_specs=[spec] * (4 * n),
        out_specs=[spec] * (3 * n),
        out_shape=[jax.ShapeDtypeStruct((r, cdim), F32)] * (3 * n),
        compiler_params=_cparams(("parallel",)),
    )(*ws, *gs, *ms, *vs)
    return outs[:n], outs[n:2 * n], outs[2 * n:]


def _bias_b():
    pad = B_PREV * CHUNK
    slopes = np.array([2.0 ** (-8.0 * (i + 1) / B_Q_HEADS) for i in range(B_Q_HEADS)], dtype=np.float32)
    dist = np.abs(np.arange(TQ)[:, None] - np.arange(TQ + pad)[None, :] + pad).astype(np.float32)
    bias = -slopes.reshape(B_Q_HEADS, 1, 1) * dist[None]
    qc = (np.arange(TQ)[:, None] + pad) // CHUNK
    kc = np.arange(TQ + pad)[None, :] // CHUNK
    allowed = (kc <= qc) & (kc >= qc - B_PREV)
    return np.where(allowed[None], bias, np.float32(NEG_INF)).astype(np.float32)


def kernel(x, ffn1_norm, ffn1_w_gate, ffn1_w_up, ffn1_w_down, mix_norm, w_in, rel_bias, sinks, w_proj_a, w_proj_b, w_out, ffn2_norm, ffn2_w_gate, ffn2_w_up, ffn2_w_down, final_norm, loss_target, m_ffn1_norm, m_ffn1_w_gate, m_ffn1_w_up, m_ffn1_w_down, m_mix_norm, m_w_in, m_rel_bias, m_sinks, m_w_proj_a, m_w_proj_b, m_w_out, m_ffn2_norm, m_ffn2_w_gate, m_ffn2_w_up, m_ffn2_w_down, m_final_norm, v_ffn1_norm, v_ffn1_w_gate, v_ffn1_w_up, v_ffn1_w_down, v_mix_norm, v_w_in, v_rel_bias, v_sinks, v_w_proj_a, v_w_proj_b, v_w_out, v_ffn2_norm, v_ffn2_w_gate, v_ffn2_w_up, v_ffn2_w_down, v_final_norm):
    bsz, s_len, _ = x.shape
    t = bsz * s_len
    core = lax.axis_index("c").astype(jnp.int32).reshape(1)
    chip = (2 * lax.axis_index("x") + lax.axis_index("y")).astype(jnp.int32).reshape(1)

    def row_form(w):
        return w.astype(BF16).T

    wg1, wu1, wd1 = _exchange_alone(
        _Gather([row_form(ffn1_w_gate), row_form(ffn1_w_up), ffn1_w_down.astype(BF16)]), "gather_ffn1")
    gather_mix = _Gather([row_form(w_in), jnp.concatenate([row_form(w_proj_a), row_form(w_proj_b)], axis=1),
                          w_out.astype(BF16)])
    gather_ffn2_gate = _Gather([row_form(ffn2_w_gate)])
    gather_ffn2_rest = _Gather([row_form(ffn2_w_up), ffn2_w_down.astype(BF16)])

    x0 = x.reshape(t, D_MODEL)
    tgt = loss_target.reshape(t, D_MODEL)
    gam1, gam2, gam3, gam4 = (g.reshape(1, D_MODEL) for g in (ffn1_norm, mix_norm, ffn2_norm, final_norm))

    h1, g1, u1, a1, x1 = _ffn_fwd(x0, gam1, wg1, wu1, wd1, "ffn1_fwd", hosted=[gather_mix])
    win_t, proj_t, wout = gather_mix.result
    h2, qkv_a, qkv_b, gates = _proj_fwd(x1, gam2, win_t, hosted=[gather_ffn2_gate])
    (wg2,) = gather_ffn2_gate.result
    qkv_a3 = qkv_a.reshape(bsz, s_len, QKV_A)
    qkv_b3 = qkv_b.reshape(bsz, s_len, QKV_B)

    far = jnp.broadcast_to(rel_bias[:, REL_TABLE - 1:REL_TABLE], (A_HEADS, REL_WRAP // 2))
    tv = jnp.concatenate([far, jnp.flip(rel_bias, axis=1), jnp.zeros((A_HEADS, REL_WRAP // 2 - REL_TABLE), F32)], axis=1)
    bias_a = _bias_a_build(tv.reshape(A_HEADS, 1, REL_WRAP))
    bias_b = jnp.asarray(_bias_b())
    sink_rows = jnp.broadcast_to(sinks.reshape(B_Q_HEADS, 1, 1), (B_Q_HEADS, 8, LANES))

    oa = _attn_a_fwd(qkv_a3, bias_a, hosted=[gather_ffn2_rest]).reshape(t, A_WIDTH)
    wu2, wd2 = gather_ffn2_rest.result
    ob = _attn_b_fwd(qkv_b3, bias_b, sink_rows).reshape(t, B_Q_WIDTH)
    x2, ya, yb, mg = _mix_out_fwd(x1, oa, ob, gates, proj_t, wout)
    h3, g2, u2, a2, x3 = _ffn_fwd(x2, gam3, wg2, wu2, wd2, "ffn2_fwd")

    dx3, dgam4, loss_part = _loss_head(x3, gam4, tgt)

    dx2, dg2, du2, db2, dgam3 = _ffn_bwd(dx3, x2, gam3, g2, u2, wg2, wu2, wd2, "ffn2_bwd")
    gw_ffn2 = [_mm_tn([dg2], h3, "grad_ffn2_gate"), _mm_tn([du2], h3, "grad_ffn2_up"),
               _mm_tn([a2], db2, "grad_ffn2_down")]
    pairx_ffn2 = _PairExchange(gw_ffn2)
    dxb, dya, dyb, doa, dob, dgates = _mix_out_bwd(dx2, gates, ya, yb, proj_t, wout, hosted=[pairx_ffn2])
    psum_ffn2 = _pair_sum(core, gw_ffn2, pairx_ffn2.result, "pair_sum_ffn2")
    gw_out = _mm_tn([mg], dxb, "grad_w_out")
    gw_proj = _mm_tn_proj(dya, dyb, oa, ob)

    chipx_ffn2 = _ChipExchange(psum_ffn2)
    dqa, dka, dva, dbias_a = _attn_a_bwd(qkv_a3, bias_a, doa.reshape(bsz, s_len, A_WIDTH), hosted=[chipx_ffn2])
    pairx_out = _PairExchange([gw_proj, gw_out])
    dqb, dkvb, dsink = _attn_b_bwd(qkv_b3, bias_b, sink_rows, dob.reshape(bsz, s_len, B_Q_WIDTH), hosted=[pairx_out])
    drel_lanes = _relbias_grad(dbias_a)
    dproj = [dqa.reshape(t, A_WIDTH), dka.reshape(t, A_WIDTH), dva.reshape(t, A_WIDTH), dqb.reshape(t, B_Q_WIDTH),
             dkvb.reshape(t, 2 * B_KV_WIDTH), dgates]

    dx1, db1, dgam2 = _proj_bwd(dx2, x1, gam2, dproj, win_t)
    gw_in = _mm_tn(dproj, h2, "grad_w_in")
    pairx_in = _PairExchange([gw_in])
    gw_d1 = _mm_tn([a1], db1, "grad_ffn1_down", hosted=[pairx_in])
    psum_mix = (_pair_sum(core, [gw_in], pairx_in.result, "pair_sum_w_in")
                + _pair_sum(core, [gw_proj, gw_out], pairx_out.result, "pair_sum_mix"))

    chipx_mix = _ChipExchange(psum_mix)
    pairx_d1 = _PairExchange([gw_d1])
    dx0, dg1, du1, _, dgam1 = _ffn_bwd(dx1, x0, gam1, g1, u1, wg1, wu1, wd1, "ffn1_bwd", hosted=[chipx_mix, pairx_d1])
    psum_d1 = _pair_sum(core, [gw_d1], pairx_d1.result, "pair_sum_ffn1_down")
    chipx_d1 = _ChipExchange(psum_d1)
    gw_g1 = _mm_tn([dg1], h1, "grad_ffn1_gate", hosted=[chipx_d1])
    from_sibling_g1 = _exchange_alone(_PairExchange([gw_g1]), "pair_exchange_ffn1_gate")
    psum_g1 = _pair_sum(core, [gw_g1], from_sibling_g1, "pair_sum_ffn1_gate")
    chipx_g1 = _ChipExchange(psum_g1)
    gw_u1 = _mm_tn([du1], h1, "grad_ffn1_up", hosted=[chipx_g1])
    from_sibling_u1 = _exchange_alone(_PairExchange([gw_u1]), "pair_exchange_ffn1_up")
    psum_u1 = _pair_sum(core, [gw_u1], from_sibling_u1, "pair_sum_ffn1_up")
    from_chips_u1 = _exchange_alone(_ChipExchange(psum_u1), "chip_exchange_ffn1_up")

    g_g1, g_u1, g_d1, g_g2, g_u2, g_d2 = _final_sum(
        chip, psum_g1 + psum_u1 + psum_d1 + psum_ffn2,
        chipx_g1.result + from_chips_u1 + chipx_d1.result + chipx_ffn2.result, "grad_sum_ffn")
    (g_in,) = _final_sum(chip, psum_mix[0:1], chipx_mix.result[0:1], "grad_sum_w_in")
    g_proj, g_out = _final_sum(chip, psum_mix[1:3], chipx_mix.result[1:3], "grad_sum_mix")
    grads = {
        "ffn1_w_gate": g_g1.T, "ffn1_w_up": g_u1.T, "ffn1_w_down": g_d1, "w_in": g_in.T,
        "w_proj_a": g_proj[:, 0:A_WIDTH].T, "w_proj_b": g_proj[:, A_WIDTH:].T, "w_out": g_out,
        "ffn2_w_gate": g_g2.T, "ffn2_w_up": g_u2.T, "ffn2_w_down": g_d2,
    }

    def row_of(v):
        return jnp.pad(v.reshape(1, -1), ((0, 0), (0, D_MODEL - v.size)))

    def table_rows(v):
        return jnp.pad(v, ((0, 0), (0, D_MODEL - REL_TABLE)))

    drel_local = jnp.flip(drel_lanes[:, 0, 0:REL_TABLE], axis=1)
    small_part = jnp.concatenate(
        [jnp.sum(dgam1, axis=0, keepdims=True), jnp.sum(dgam2, axis=0, keepdims=True),
         jnp.sum(dgam3, axis=0, keepdims=True), jnp.sum(dgam4, axis=0, keepdims=True),
         row_of(jnp.sum(loss_part)), row_of(dsink[:, 0, 0]), jnp.zeros((2, D_MODEL), F32),
         table_rows(drel_local)], axis=0)
    small = _all_reduce_small(small_part)
    loss = small[4, 0]

    def pack(n1, n2, n3, n4, sk, tb):
        return jnp.concatenate([n1.reshape(1, -1), n2.reshape(1, -1), n3.reshape(1, -1), n4.reshape(1, -1),
                                jnp.zeros((1, D_MODEL), F32), row_of(sk), jnp.zeros((2, D_MODEL), F32), table_rows(tb)],
                               axis=0)

    live = np.zeros((SMALL_ROWS, D_MODEL), np.float32)
    live[0:4] = 1.0
    live[5, 0:B_Q_HEADS] = 1.0
    live[8:16, 0:REL_TABLE] = 1.0
    small_g = small * jnp.asarray(live)
    sw = pack(ffn1_norm, mix_norm, ffn2_norm, final_norm, sinks, rel_bias)
    sm = pack(m_ffn1_norm, m_mix_norm, m_ffn2_norm, m_final_norm, m_sinks, m_rel_bias)
    sv = pack(v_ffn1_norm, v_mix_norm, v_ffn2_norm, v_final_norm, v_sinks, v_rel_bias)
    (sd,), (snm,), (snv,) = _adamw([sw], [small_g], [sm], [sv], "adamw_small")

    def unpack(p):
        return {"ffn1_norm": p[0], "mix_norm": p[1], "ffn2_norm": p[2], "final_norm": p[3],
                "sinks": p[5, 0:B_Q_HEADS], "rel_bias": p[8:16, 0:REL_TABLE]}

    grads.update(unpack(small_g))
    delta, new_m, new_v = unpack(sd), unpack(snm), unpack(snv)

    wmv = {
        "ffn1_w_gate": (ffn1_w_gate, m_ffn1_w_gate, v_ffn1_w_gate), "ffn1_w_up": (ffn1_w_up, m_ffn1_w_up, v_ffn1_w_up),
        "ffn1_w_down": (ffn1_w_down, m_ffn1_w_down, v_ffn1_w_down), "w_in": (w_in, m_w_in, v_w_in),
        "w_proj_a": (w_proj_a, m_w_proj_a, v_w_proj_a), "w_proj_b": (w_proj_b, m_w_proj_b, v_w_proj_b),
        "w_out": (w_out, m_w_out, v_w_out),
        "ffn2_w_gate": (ffn2_w_gate, m_ffn2_w_gate, v_ffn2_w_gate), "ffn2_w_up": (ffn2_w_up, m_ffn2_w_up, v_ffn2_w_up),
        "ffn2_w_down": (ffn2_w_down, m_ffn2_w_down, v_ffn2_w_down),
    }
    groups = [("adamw_ffn_up", ["ffn1_w_gate", "ffn1_w_up", "ffn2_w_gate", "ffn2_w_up"]),
              ("adamw_ffn_down", ["ffn1_w_down", "ffn2_w_down"]), ("adamw_w_in", ["w_in"]),
              ("adamw_proj", ["w_proj_a", "w_proj_b"]), ("adamw_w_out", ["w_out"])]
    for gname, names in groups:
        ds_, ms_, vs_ = _adamw([wmv[n][0] for n in names], [grads[n] for n in names], [wmv[n][1] for n in names],
                               [wmv[n][2] for n in names], gname)
        for n, d_, m_, v_ in zip(names, ds_, ms_, vs_):
            delta[n], new_m[n], new_v[n] = d_, m_, v_

    order = ["ffn1_norm", "ffn1_w_gate", "ffn1_w_up", "ffn1_w_down", "mix_norm", "w_in", "rel_bias", "sinks",
             "w_proj_a", "w_proj_b", "w_out", "ffn2_norm", "ffn2_w_gate", "ffn2_w_up", "ffn2_w_down", "final_norm"]
    grad_x = dx0.reshape(bsz, s_len, D_MODEL)
    return (loss, grad_x, *[grads[n] for n in order], *[delta[n] for n in order], *[new_m[n] for n in order],
            *[new_v[n] for n in order])
```

```python
import numpy as np
import jax
import jax.numpy as jnp
from jax import lax
from jax.experimental import pallas as pl
from jax.experimental.pallas import tpu as pltpu

F32 = jnp.float32
BF16 = jnp.bfloat16

D_MODEL = 1024
D_FF = 2816
CHUNK = 64
D_HEAD = 64
A_HEADS = 8
A_PREV = 8
MAX_REL = 128
B_Q_HEADS = 8
B_KV_HEADS = 2
B_GROUP = B_Q_HEADS // B_KV_HEADS
B_PREV = 2
REL_TABLE = (CHUNK - 1) + MAX_REL + 1
A_WIDTH = A_HEADS * D_HEAD
B_Q_WIDTH = B_Q_HEADS * D_HEAD
B_KV_WIDTH = B_KV_HEADS * D_HEAD
QKV_A = 3 * A_WIDTH
QKV_B = B_Q_WIDTH + 2 * B_KV_WIDTH
IN_WIDTH = QKV_A + QKV_B + 2 * D_MODEL
EPS = 1e-6
NEG_INF = -1e30
SCALE = 1.0 / 8.0

ADAM_LR = 0.001
ADAM_B1 = 0.9
ADAM_B2 = 0.999
ADAM_EPS = 1e-08
ADAM_WD = 0.01
ADAM_STEP = 10

N_DEV = 8
N_CHIP = 4
MESH = pl.DeviceIdType.MESH

LANES = 128
TQ = 256
TM = 256
TM_FWD = 512
FC = 256
VMEM_LIMIT = 56 << 20


def _cparams(sem, vmem=VMEM_LIMIT):
    return pltpu.CompilerParams(dimension_semantics=sem, vmem_limit_bytes=vmem)


def _dot_nt(a, b):
    return lax.dot_general(a, b, (((1,), (1,)), ((), ())), preferred_element_type=F32)


def _dot_nn(a, b):
    return lax.dot_general(a, b, (((1,), (0,)), ((), ())), preferred_element_type=F32)


def _dot_tn(a, b):
    return lax.dot_general(a, b, (((0,), (0,)), ((), ())), preferred_element_type=F32)


def _resident(shape):
    nd = len(shape)
    return pl.BlockSpec(shape, lambda *_: (0,) * nd, pipeline_mode=pl.Buffered(1))


def _rows(tm, width):
    return pl.BlockSpec((tm, width), lambda i: (i, 0))


def _colsum8(v):
    tm, n = v.shape
    return jnp.sum(v.reshape(tm // 8, 8, n), axis=0)


def _rms(x):
    r = lax.rsqrt(jnp.mean(x * x, axis=-1, keepdims=True) + EPS)
    return x * r, r


def _rms_bwd(dh, xh, r, gamma):
    dxh = dh * gamma
    dx = r * (dxh - xh * jnp.mean(dxh * xh, axis=-1, keepdims=True))
    return dx, _colsum8(dh * xh)


def _hbm():
    return pl.BlockSpec(memory_space=pltpu.HBM)


def _call(body, *, name, grid, in_specs, out_specs, out_shape, args, sem, scratch_shapes=(), hosted=()):
    in_specs, out_specs, out_shape = list(in_specs), list(out_specs), list(out_shape)
    scratch_shapes = list(scratch_shapes)
    if not hosted:
        return pl.pallas_call(body, name=name, grid=grid, in_specs=in_specs, out_specs=out_specs, out_shape=out_shape,
                              scratch_shapes=scratch_shapes, compiler_params=_cparams(sem))(*args)
    n_in, n_out, n_scr = len(in_specs), len(out_specs), len(scratch_shapes)
    x_in = [a for x in hosted for a in x.inputs]
    x_out = [s for x in hosted for s in x.out_shape]
    x_scr = [s for x in hosted for s in x.scratch]
    steps = int(np.prod(grid))
    forward_step = max(steps - 3, 0)

    def wrapped(*refs):
        pos = [0]

        def take(k):
            pos[0] += k
            return refs[pos[0] - k:pos[0]]

        ins, xin, outs, xout, scr, xscr = (take(k) for k in (n_in, len(x_in), n_out, len(x_out), n_scr, len(x_scr)))
        step = 0
        for axis, extent in enumerate(grid):
            step = step * extent + pl.program_id(axis)
        own, oi, oo, osc = [], 0, 0, 0
        for x in hosted:
            own.append((xin[oi:oi + len(x.inputs)], xout[oo:oo + len(x.out_shape)], xscr[osc:osc + len(x.scratch)]))
            oi, oo, osc = oi + len(x.inputs), oo + len(x.out_shape), osc + len(x.scratch)

        def phase(method):
            for x, (i_, o_, s_) in zip(hosted, own):
                getattr(x, method)(i_, o_, s_)

        pl.when(step == 0)(lambda: phase("start"))
        body(*ins, *outs, *scr)
        pl.when(step == forward_step)(lambda: phase("forward"))
        pl.when(step == steps - 1)(lambda: phase("finish"))

    res = pl.pallas_call(
        wrapped, name=name, grid=grid, in_specs=in_specs + [_hbm()] * len(x_in),
        out_specs=out_specs + [_hbm()] * len(x_out), out_shape=out_shape + x_out,
        scratch_shapes=scratch_shapes + x_scr, compiler_params=_cparams(("arbitrary",) * len(grid)))(*args, *x_in)
    rest = list(res[n_out:])
    for x in hosted:
        x.result, rest = rest[:len(x.out_shape)], rest[len(x.out_shape):]
    return list(res[:n_out])


def _ffn_fwd(x, gamma, wg_t, wu_t, wd, name, hosted=()):
    t = x.shape[0]
    f = wg_t.shape[0]

    def body(x_ref, gam_ref, wg_ref, wu_ref, wd_ref, h_ref, g_ref, u_ref, a_ref, y_ref):
        xv = x_ref[...]
        xh, _ = _rms(xv)
        h = (xh * gam_ref[...]).astype(BF16)
        h_ref[...] = h
        for j in range(f // FC):
            sl = slice(j * FC, (j + 1) * FC)
            g = _dot_nt(h, wg_ref[sl, :])
            u = _dot_nt(h, wu_ref[sl, :])
            g_ref[:, sl] = g.astype(BF16)
            u_ref[:, sl] = u.astype(BF16)
            a_ref[:, sl] = (g * jax.nn.sigmoid(g) * u).astype(BF16)
        y_ref[...] = xv + 0.5 * _dot_nn(a_ref[...], wd_ref[...])

    return _call(
        body,
        name=name,
        grid=(t // TM_FWD,),
        in_specs=[_rows(TM_FWD, D_MODEL), _resident((1, D_MODEL)), _resident((f, D_MODEL)), _resident((f, D_MODEL)),
                  _resident((f, D_MODEL))],
        out_specs=[_rows(TM_FWD, D_MODEL), _rows(TM_FWD, f), _rows(TM_FWD, f), _rows(TM_FWD, f),
                   _rows(TM_FWD, D_MODEL)],
        out_shape=[jax.ShapeDtypeStruct((t, D_MODEL), BF16), jax.ShapeDtypeStruct((t, f), BF16),
                   jax.ShapeDtypeStruct((t, f), BF16), jax.ShapeDtypeStruct((t, f), BF16),
                   jax.ShapeDtypeStruct((t, D_MODEL), F32)],
        args=(x, gamma, wg_t, wu_t, wd), sem=("parallel",), hosted=hosted)


def _ffn_bwd(d, x, gamma, g_act, u_act, wg_t, wu_t, wd, name, hosted=()):
    t = x.shape[0]
    f = wg_t.shape[0]

    def body(d_ref, x_ref, gam_ref, g_ref, u_ref, wg_ref, wu_ref, wd_ref, dx_ref, dg_ref, du_ref, db_ref, dgam_ref):
        dv = d_ref[...]
        db = (0.5 * dv).astype(BF16)
        db_ref[...] = db
        for j in range(f // FC):
            sl = slice(j * FC, (j + 1) * FC)
            da = _dot_nt(db, wd_ref[sl, :])
            g = g_ref[:, sl].astype(F32)
            u = u_ref[:, sl].astype(F32)
            s = jax.nn.sigmoid(g)
            dg_ref[:, sl] = (da * u * (s * (1.0 + g * (1.0 - s)))).astype(BF16)
            du_ref[:, sl] = (da * (g * s)).astype(BF16)
        dh = _dot_nn(dg_ref[...], wg_ref[...]) + _dot_nn(du_ref[...], wu_ref[...])
        xh, r = _rms(x_ref[...])
        dxn, dgam = _rms_bwd(dh, xh, r, gam_ref[...])
        dx_ref[...] = dv + dxn

        @pl.when(pl.program_id(0) == 0)
        def _():
            dgam_ref[...] = jnp.zeros_like(dgam_ref)

        dgam_ref[...] += dgam

    return _call(
        body,
        name=name,
        grid=(t // TM,),
        in_specs=[_rows(TM, D_MODEL), _rows(TM, D_MODEL), _resident((1, D_MODEL)), _rows(TM, f), _rows(TM, f),
                  _resident((f, D_MODEL)), _resident((f, D_MODEL)), _resident((f, D_MODEL))],
        out_specs=[_rows(TM, D_MODEL), _rows(TM, f), _rows(TM, f), _rows(TM, D_MODEL),
                   pl.BlockSpec((8, D_MODEL), lambda i: (0, 0))],
        out_shape=[jax.ShapeDtypeStruct((t, D_MODEL), F32), jax.ShapeDtypeStruct((t, f), BF16),
                   jax.ShapeDtypeStruct((t, f), BF16), jax.ShapeDtypeStruct((t, D_MODEL), BF16),
                   jax.ShapeDtypeStruct((8, D_MODEL), F32)],
        args=(d, x, gamma, g_act, u_act, wg_t, wu_t, wd), sem=("arbitrary",), hosted=hosted)


def _mm_tn(pieces, b, name, tile=256, hosted=()):
    t, n = b.shape
    npc = len(pieces)
    counts = [p.shape[1] // tile for p in pieces]
    los = [sum(counts[:k]) for k in range(npc)]
    total = sum(counts)

    def body(*refs):
        a_refs, b_ref, o_ref = refs[:npc], refs[npc], refs[npc + 1]
        i = pl.program_id(0)
        for k in range(npc):
            @pl.when(jnp.logical_and(i >= los[k], i < los[k] + counts[k]))
            def _(k=k):
                o_ref[...] = _dot_tn(a_refs[k][...], b_ref[...]).astype(BF16)

    def a_spec(k):
        return pl.BlockSpec((t, tile), lambda i: (0, jnp.clip(i - los[k], 0, counts[k] - 1)))

    return _call(
        body,
        name=name,
        grid=(total,),
        in_specs=[a_spec(k) for k in range(npc)] + [_resident((t, n))],
        out_specs=[pl.BlockSpec((tile, n), lambda i: (i, 0))],
        out_shape=[jax.ShapeDtypeStruct((total * tile, n), BF16)],
        args=(*pieces, b), sem=("parallel",), hosted=hosted)[0]


def _mm_tn_proj(dya, dyb, oa, ob, tile=256):
    t = dya.shape[0]

    def body(dya_ref, dyb_ref, oa_ref, ob_ref, o_ref):
        o_ref[:, 0:A_WIDTH] = _dot_tn(dya_ref[...], oa_ref[...]).astype(BF16)
        o_ref[:, A_WIDTH:A_WIDTH + B_Q_WIDTH] = _dot_tn(dyb_ref[...], ob_ref[...]).astype(BF16)

    col = pl.BlockSpec((t, tile), lambda i: (0, i))
    return pl.pallas_call(
        body,
        name="grad_proj",
        grid=(D_MODEL // tile,),
        in_specs=[col, col, _resident((t, A_WIDTH)), _resident((t, B_Q_WIDTH))],
        out_specs=pl.BlockSpec((tile, A_WIDTH + B_Q_WIDTH), lambda i: (i, 0)),
        out_shape=jax.ShapeDtypeStruct((D_MODEL, A_WIDTH + B_Q_WIDTH), BF16),
        compiler_params=_cparams(("parallel",)),
    )(dya, dyb, oa, ob)


def _proj_fwd(x, gamma, win_t, hosted=()):
    t = x.shape[0]

    def body(x_ref, gam_ref, w_ref, h_ref, qa_ref, qb_ref, gt_ref):
        xh, _ = _rms(x_ref[...])
        h = (xh * gam_ref[...]).astype(BF16)
        h_ref[...] = h
        for j in range(QKV_A // FC):
            qa_ref[:, j * FC:(j + 1) * FC] = _dot_nt(h, w_ref[j * FC:(j + 1) * FC, :]).astype(BF16)
        for j in range(QKV_B // FC):
            lo = QKV_A + j * FC
            qb_ref[:, j * FC:(j + 1) * FC] = _dot_nt(h, w_ref[lo:lo + FC, :]).astype(BF16)
        for j in range(2 * D_MODEL // FC):
            lo = QKV_A + QKV_B + j * FC
            gt_ref[:, j * FC:(j + 1) * FC] = _dot_nt(h, w_ref[lo:lo + FC, :])

    return _call(
        body,
        name="proj_fwd",
        grid=(t // TM_FWD,),
        in_specs=[_rows(TM_FWD, D_MODEL), _resident((1, D_MODEL)), _resident((IN_WIDTH, D_MODEL))],
        out_specs=[_rows(TM_FWD, D_MODEL), _rows(TM_FWD, QKV_A), _rows(TM_FWD, QKV_B), _rows(TM_FWD, 2 * D_MODEL)],
        out_shape=[jax.ShapeDtypeStruct((t, D_MODEL), BF16), jax.ShapeDtypeStruct((t, QKV_A), BF16),
                   jax.ShapeDtypeStruct((t, QKV_B), BF16), jax.ShapeDtypeStruct((t, 2 * D_MODEL), F32)],
        args=(x, gamma, win_t), sem=("parallel",), hosted=hosted)


def _proj_bwd(d, x, gamma, pieces, win_t, hosted=()):
    t = x.shape[0]
    npc = len(pieces)
    widths = [p.shape[1] for p in pieces]
    los = [sum(widths[:k]) for k in range(npc)]

    def body(*refs):
        d_ref, x_ref, gam_ref = refs[:3]
        p_refs = refs[3:3 + npc]
        w_ref, dx_ref, db_ref, dgam_ref = refs[3 + npc:]
        dh = _dot_nn(p_refs[0][...], w_ref[0:widths[0], :])
        for k in range(1, npc):
            dh += _dot_nn(p_refs[k][...], w_ref[los[k]:los[k] + widths[k], :])
        xh, r = _rms(x_ref[...])
        dxn, dgam = _rms_bwd(dh, xh, r, gam_ref[...])
        dx = d_ref[...] + dxn
        dx_ref[...] = dx
        db_ref[...] = (0.5 * dx).astype(BF16)

        @pl.when(pl.program_id(0) == 0)
        def _():
            dgam_ref[...] = jnp.zeros_like(dgam_ref)

        dgam_ref[...] += dgam

    return _call(
        body,
        name="proj_bwd",
        grid=(t // TM,),
        in_specs=[_rows(TM, D_MODEL), _rows(TM, D_MODEL), _resident((1, D_MODEL))] + [_rows(TM, w) for w in widths]
        + [_resident((IN_WIDTH, D_MODEL))],
        out_specs=[_rows(TM, D_MODEL), _rows(TM, D_MODEL), pl.BlockSpec((8, D_MODEL), lambda i: (0, 0))],
        out_shape=[jax.ShapeDtypeStruct((t, D_MODEL), F32), jax.ShapeDtypeStruct((t, D_MODEL), BF16),
                   jax.ShapeDtypeStruct((8, D_MODEL), F32)],
        args=(d, x, gamma, *pieces, win_t), sem=("arbitrary",), hosted=hosted)


def _lane_half(shape):
    return lax.broadcasted_iota(jnp.int32, shape, len(shape) - 1) // D_HEAD


def _band_softmax(q, kk, bias, sink, qs, pad):
    s = _dot_nt(q, kk) * SCALE + bias
    col = lax.broadcasted_iota(jnp.int32, s.shape, 1)
    s = jnp.where(col + qs >= pad, s, NEG_INF)
    m = jnp.max(s, axis=-1, keepdims=True)
    if sink is not None:
        m = jnp.maximum(m, sink)
    p = jnp.exp(s - m)
    den = jnp.sum(p, axis=-1, keepdims=True)
    if sink is not None:
        den = den + jnp.exp(sink - m)
    return p, m, 1.0 / den


def _fill_padded(dst, src, pad):
    dst[0:pad, :] = jnp.zeros((pad,) + dst.shape[1:], dst.dtype)
    dst[pad:, :] = src


FWD_PAIRS = 4
BWD_PAIRS = 2


def _attn_a_fwd(qkv, bias, hosted=()):
    bsz, s_len, _ = qkv.shape
    pad = A_PREV * CHUNK
    band = TQ + pad
    pp = FWD_PAIRS
    w = pp * LANES
    nb = A_WIDTH // w

    def body(q_ref, k_ref, v_ref, b_ref, o_ref, kp, vp):
        i = pl.program_id(2)

        @pl.when(i == 0)
        def _():
            _fill_padded(kp, k_ref[...], pad)
            _fill_padded(vp, v_ref[...], pad)

        qs = pl.multiple_of(i * TQ, TQ)
        half = _lane_half((1, LANES))
        for pr in range(pp):
            sl = slice(pr * LANES, (pr + 1) * LANES)
            kk = kp[pl.ds(qs, band), sl]
            vv = vp[pl.ds(qs, band), sl]
            q = q_ref[:, sl]
            outs = []
            for j in range(2):
                qm = jnp.where(half == j, q, jnp.zeros_like(q))
                p, _, inv = _band_softmax(qm, kk, b_ref[2 * pr + j], None, qs, pad)
                outs.append(_dot_nn(p.astype(BF16), vv) * inv)
            o_ref[:, sl] = jnp.where(half == 0, outs[0], outs[1]).astype(BF16)

    return _call(
        body,
        name="attn_a_fwd",
        grid=(bsz, nb, s_len // TQ),
        in_specs=[pl.BlockSpec((None, TQ, w), lambda b, g, i: (b, i, g)),
                  pl.BlockSpec((None, s_len, w), lambda b, g, i: (b, 0, nb + g)),
                  pl.BlockSpec((None, s_len, w), lambda b, g, i: (b, 0, 2 * nb + g)),
                  pl.BlockSpec((2 * pp, TQ, band), lambda b, g, i: (g, 0, 0))],
        out_specs=[pl.BlockSpec((None, TQ, w), lambda b, g, i: (b, i, g))],
        out_shape=[jax.ShapeDtypeStruct((bsz, s_len, A_WIDTH), BF16)],
        scratch_shapes=[pltpu.VMEM((pad + s_len, w), BF16), pltpu.VMEM((pad + s_len, w), BF16)],
        args=(qkv, qkv, qkv, bias), sem=("arbitrary", "arbitrary", "arbitrary"), hosted=hosted)[0]


def _attn_a_bwd(qkv, bias, do, hosted=()):
    bsz, s_len, _ = qkv.shape
    pad = A_PREV * CHUNK
    band = TQ + pad
    n_i = s_len // TQ
    pp = BWD_PAIRS
    w = pp * LANES
    nb = A_WIDTH // w

    def body(q_ref, k_ref, v_ref, b_ref, do_ref, dq_ref, dk_ref, dv_ref, dbias_ref, kp, vp, dk_acc, dv_acc):
        b = pl.program_id(1)
        i = pl.program_id(2)

        @pl.when(i == 0)
        def _():
            _fill_padded(kp, k_ref[...], pad)
            _fill_padded(vp, v_ref[...], pad)
            dk_acc[...] = jnp.zeros_like(dk_acc)
            dv_acc[...] = jnp.zeros_like(dv_acc)

        @pl.when(jnp.logical_and(b == 0, i == 0))
        def _():
            dbias_ref[...] = jnp.zeros_like(dbias_ref)

        qs = pl.multiple_of(i * TQ, TQ)
        half = _lane_half((1, LANES))
        for pr in range(pp):
            sl = slice(pr * LANES, (pr + 1) * LANES)
            kk = kp[pl.ds(qs, band), sl]
            vv = vp[pl.ds(qs, band), sl]
            q = q_ref[:, sl]
            dd = do_ref[:, sl]
            dqs, dks, dvs = [], [], []
            for j in range(2):
                qm = jnp.where(half == j, q, jnp.zeros_like(q))
                dm = jnp.where(half == j, dd, jnp.zeros_like(dd))
                p, _, inv = _band_softmax(qm, kk, b_ref[2 * pr + j], None, qs, pad)
                pn = p * inv
                dp = _dot_nt(dm, vv)
                delta = jnp.sum(pn * dp, axis=-1, keepdims=True)
                ds = pn * (dp - delta)
                dbias_ref[2 * pr + j] += ds[:, band - REL_COLS:]
                dsb = ds.astype(BF16)
                dqs.append(_dot_nn(dsb, kk))
                dks.append(_dot_tn(dsb, q))
                dvs.append(_dot_tn(pn.astype(BF16), dd))
            dq_ref[:, sl] = (jnp.where(half == 0, dqs[0], dqs[1]) * SCALE).astype(BF16)
            dk_acc[pl.ds(qs, band), sl] += jnp.where(half == 0, dks[0], dks[1]) * SCALE
            dv_acc[pl.ds(qs, band), sl] += jnp.where(half == 0, dvs[0], dvs[1])

        @pl.when(i == n_i - 1)
        def _():
            dk_ref[...] = dk_acc[pad:, :].astype(BF16)
            dv_ref[...] = dv_acc[pad:, :].astype(BF16)

    qspec = pl.BlockSpec((None, TQ, w), lambda g, b, i: (b, i, g))
    kvout = pl.BlockSpec((None, s_len, w), lambda g, b, i: (b, 0, g))
    wide = jax.ShapeDtypeStruct((bsz, s_len, A_WIDTH), BF16)
    return _call(
        body,
        name="attn_a_bwd",
        grid=(nb, bsz, n_i),
        in_specs=[qspec,
                  pl.BlockSpec((None, s_len, w), lambda g, b, i: (b, 0, nb + g)),
                  pl.BlockSpec((None, s_len, w), lambda g, b, i: (b, 0, 2 * nb + g)),
                  pl.BlockSpec((2 * pp, TQ, band), lambda g, b, i: (g, 0, 0)),
                  qspec],
        out_specs=[qspec, kvout, kvout, pl.BlockSpec((2 * pp, TQ, REL_COLS), lambda g, b, i: (g, 0, 0))],
        out_shape=[wide, wide, wide, jax.ShapeDtypeStruct((A_HEADS, TQ, REL_COLS), F32)],
        scratch_shapes=[pltpu.VMEM((pad + s_len, w), BF16), pltpu.VMEM((pad + s_len, w), BF16),
                        pltpu.VMEM((pad + s_len, w), F32), pltpu.VMEM((pad + s_len, w), F32)],
        args=(qkv, qkv, qkv, bias, do), sem=("arbitrary", "arbitrary", "arbitrary"), hosted=hosted)


def _fill_padded_dup(dst, src, pad, h, half):
    other = pltpu.roll(src, D_HEAD, 1)
    _fill_padded(dst, jnp.where(half == h, src, other), pad)


def _attn_b_fwd(qkv, bias, sink):
    bsz, s_len, _ = qkv.shape
    pad = B_PREV * CHUNK
    band = TQ + pad
    kcol = B_Q_WIDTH // LANES
    npair = B_Q_HEADS // 2

    def body(q_ref, k_ref, v_ref, b_ref, s_ref, o_ref, kp, vp):
        i = pl.program_id(1)
        half = _lane_half((1, LANES))

        @pl.when(i == 0)
        def _():
            for h in range(B_KV_HEADS):
                _fill_padded_dup(kp.at[h], k_ref[...], pad, h, half)
                _fill_padded_dup(vp.at[h], v_ref[...], pad, h, half)

        qs = pl.multiple_of(i * TQ, TQ)
        for pr in range(npair):
            h = pr // (B_GROUP // 2)
            sl = slice(pr * LANES, (pr + 1) * LANES)
            kk = kp[h, pl.ds(qs, band), :]
            vv = vp[h, pl.ds(qs, band), :]
            q = q_ref[:, sl]
            outs = []
            for j in range(2):
                qm = jnp.where(half == j, q, jnp.zeros_like(q))
                p, _, inv = _band_softmax(qm, kk, b_ref[2 * pr + j], s_ref[2 * pr + j][0:1, 0:1], qs, pad)
                outs.append(_dot_nn(p.astype(BF16), vv) * inv)
            o_ref[:, sl] = jnp.where(half == 0, outs[0], outs[1]).astype(BF16)

    return pl.pallas_call(
        body,
        name="attn_b_fwd",
        grid=(bsz, s_len // TQ),
        in_specs=[pl.BlockSpec((None, TQ, B_Q_WIDTH), lambda b, i: (b, i, 0)),
                  pl.BlockSpec((None, s_len, LANES), lambda b, i: (b, 0, kcol)),
                  pl.BlockSpec((None, s_len, LANES), lambda b, i: (b, 0, kcol + 1)),
                  pl.BlockSpec((B_Q_HEADS, TQ, band), lambda b, i: (0, 0, 0)),
                  pl.BlockSpec((B_Q_HEADS, 8, LANES), lambda b, i: (0, 0, 0))],
        out_specs=pl.BlockSpec((None, TQ, B_Q_WIDTH), lambda b, i: (b, i, 0)),
        out_shape=jax.ShapeDtypeStruct((bsz, s_len, B_Q_WIDTH), BF16),
        scratch_shapes=[pltpu.VMEM((B_KV_HEADS, pad + s_len, LANES), BF16),
                        pltpu.VMEM((B_KV_HEADS, pad + s_len, LANES), BF16)],
        compiler_params=_cparams(("arbitrary", "arbitrary")),
    )(qkv, qkv, qkv, bias, sink)


def _attn_b_bwd(qkv, bias, sink, do, hosted=()):
    bsz, s_len, _ = qkv.shape
    pad = B_PREV * CHUNK
    band = TQ + pad
    kcol = B_Q_WIDTH // LANES
    npair = B_Q_HEADS // 2
    n_i = s_len // TQ

    pp = B_GROUP // 2
    w = pp * LANES

    def body(q_ref, k_ref, v_ref, b_ref, s_ref, do_ref, dq_ref, dkv_ref, dsink_ref, kp, vp, dk_acc, dv_acc):
        b = pl.program_id(0)
        h = pl.program_id(1)
        i = pl.program_id(2)
        half = _lane_half((1, LANES))

        @pl.when(i == 0)
        def _():
            _fill_padded_dup(kp, k_ref[...], pad, h, half)
            _fill_padded_dup(vp, v_ref[...], pad, h, half)

        @pl.when(jnp.logical_and(h == 0, i == 0))
        def _():
            dk_acc[...] = jnp.zeros_like(dk_acc)
            dv_acc[...] = jnp.zeros_like(dv_acc)

        @pl.when(jnp.logical_and(b == 0, jnp.logical_and(h == 0, i == 0)))
        def _():
            dsink_ref[...] = jnp.zeros_like(dsink_ref)

        qs = pl.multiple_of(i * TQ, TQ)
        kk = kp[pl.ds(qs, band), :]
        vv = vp[pl.ds(qs, band), :]
        dk2 = jnp.zeros((band, LANES), F32)
        dv2 = jnp.zeros((band, LANES), F32)
        for pr in range(pp):
            sl = slice(pr * LANES, (pr + 1) * LANES)
            q = q_ref[:, sl]
            dd = do_ref[:, sl]
            dqs, dks, dvs = [], [], []
            for j in range(2):
                qm = jnp.where(half == j, q, jnp.zeros_like(q))
                dm = jnp.where(half == j, dd, jnp.zeros_like(dd))
                sink = s_ref[2 * pr + j][0:1, 0:1]
                p, m, inv = _band_softmax(qm, kk, b_ref[2 * pr + j], sink, qs, pad)
                pn = p * inv
                dp = _dot_nt(dm, vv)
                delta = jnp.sum(pn * dp, axis=-1, keepdims=True)
                ds = pn * (dp - delta)
                dsb = ds.astype(BF16)
                dqs.append(_dot_nn(dsb, kk))
                dks.append(_dot_tn(dsb, q))
                dvs.append(_dot_tn(pn.astype(BF16), dd))
                dsk = jnp.sum(-(jnp.exp(sink - m) * inv) * delta, axis=0, keepdims=True)
                dsink_ref[2 * pp * h + 2 * pr + j] += jnp.broadcast_to(dsk, (8, LANES))
            dq_ref[:, sl] = (jnp.where(half == 0, dqs[0], dqs[1]) * SCALE).astype(BF16)
            dk2 = dk2 + jnp.where(half == 0, dks[0], dks[1])
            dv2 = dv2 + jnp.where(half == 0, dvs[0], dvs[1])
        dk_acc[pl.ds(qs, band), :] += jnp.where(half == h, (dk2 + pltpu.roll(dk2, D_HEAD, 1)) * SCALE, 0.0)
        dv_acc[pl.ds(qs, band), :] += jnp.where(half == h, dv2 + pltpu.roll(dv2, D_HEAD, 1), 0.0)

        @pl.when(jnp.logical_and(h == B_KV_HEADS - 1, i == n_i - 1))
        def _():
            dkv_ref[:, 0:LANES] = dk_acc[pad:, :].astype(BF16)
            dkv_ref[:, LANES:2 * LANES] = dv_acc[pad:, :].astype(BF16)

    qspec = pl.BlockSpec((None, TQ, w), lambda b, h, i: (b, i, h))
    return _call(
        body,
        name="attn_b_bwd",
        grid=(bsz, B_KV_HEADS, n_i),
        in_specs=[qspec,
                  pl.BlockSpec((None, s_len, LANES), lambda b, h, i: (b, 0, kcol)),
                  pl.BlockSpec((None, s_len, LANES), lambda b, h, i: (b, 0, kcol + 1)),
                  pl.BlockSpec((2 * pp, TQ, band), lambda b, h, i: (h, 0, 0)),
                  pl.BlockSpec((2 * pp, 8, LANES), lambda b, h, i: (h, 0, 0)),
                  qspec],
        out_specs=[qspec, pl.BlockSpec((None, s_len, 2 * LANES), lambda b, h, i: (b, 0, 0)),
                   pl.BlockSpec((B_Q_HEADS, 8, LANES), lambda b, h, i: (0, 0, 0))],
        out_shape=[jax.ShapeDtypeStruct((bsz, s_len, B_Q_WIDTH), BF16),
                   jax.ShapeDtypeStruct((bsz, s_len, 2 * B_KV_WIDTH), BF16),
                   jax.ShapeDtypeStruct((B_Q_HEADS, 8, LANES), F32)],
        scratch_shapes=[pltpu.VMEM((pad + s_len, LANES), BF16), pltpu.VMEM((pad + s_len, LANES), BF16),
                        pltpu.VMEM((pad + s_len, LANES), F32), pltpu.VMEM((pad + s_len, LANES), F32)],
        args=(qkv, qkv, qkv, bias, sink, do), sem=("arbitrary", "arbitrary", "arbitrary"), hosted=hosted)


REL_COLS = 3 * 128
REL_WRAP = 512


def _bias_a_build(tv):
    h = tv.shape[0]
    pad = A_PREV * CHUNK
    band = TQ + pad

    def body(tv_ref, o_ref):
        row = tv_ref[...]
        x = jnp.broadcast_to(row, (TQ, REL_WRAP))
        r = lax.broadcasted_iota(jnp.int32, x.shape, 0)
        for bit in range(8):
            sh = 1 << bit
            x = jnp.where((r & sh) != 0, pltpu.roll(x, sh, 1), x)
        far = jnp.broadcast_to(row[:, 0:1], (TQ, band - REL_COLS))
        full = jnp.concatenate([far, x[:, REL_WRAP // 2:REL_WRAP], x[:, 0:REL_COLS - REL_WRAP // 2]], axis=1)
        qc = (lax.broadcasted_iota(jnp.int32, full.shape, 0) + pad) // CHUNK
        kc = lax.broadcasted_iota(jnp.int32, full.shape, 1) // CHUNK
        ok = jnp.logical_and(kc <= qc, kc >= qc - A_PREV)
        o_ref[...] = jnp.where(ok, full, NEG_INF)

    return pl.pallas_call(
        body,
        name="bias_a_build",
        grid=(h,),
        in_specs=[pl.BlockSpec((None, 1, REL_WRAP), lambda hh: (hh, 0, 0))],
        out_specs=pl.BlockSpec((None, TQ, band), lambda hh: (hh, 0, 0)),
        out_shape=jax.ShapeDtypeStruct((h, TQ, band), F32),
        compiler_params=_cparams(("parallel",)),
    )(tv)


def _relbias_grad(dbias):
    h, rows, _ = dbias.shape

    def body(d_ref, o_ref):
        x = d_ref[...]
        r = lax.broadcasted_iota(jnp.int32, x.shape, 0)
        c = lax.broadcasted_iota(jnp.int32, x.shape, 1) - r
        x = jnp.where(jnp.logical_and(c >= 1, c < REL_TABLE), x, 0.0)
        for bit in range(8):
            sh = 1 << bit
            x = jnp.where((r & sh) != 0, pltpu.roll(x, REL_COLS - sh, 1), x)
        diag = jnp.sum(x, axis=0, keepdims=True)
        lane = lax.broadcasted_iota(jnp.int32, diag.shape, 1)
        diag = jnp.where(jnp.logical_and(lane >= 1, lane < REL_TABLE), diag, 0.0)
        rest = -jnp.sum(diag, axis=1, keepdims=True)
        o_ref[...] = jnp.broadcast_to(jnp.where(lane == 0, rest, diag), o_ref.shape)

    return pl.pallas_call(
        body,
        name="relbias_grad",
        grid=(h,),
        in_specs=[pl.BlockSpec((None, rows, REL_COLS), lambda hh: (hh, 0, 0))],
        out_specs=pl.BlockSpec((None, 8, REL_COLS), lambda hh: (hh, 0, 0)),
        out_shape=jax.ShapeDtypeStruct((h, 8, REL_COLS), F32),
        compiler_params=_cparams(("parallel",)),
    )(dbias)


def _mix_out_fwd(x, oa, ob, gates, proj_t, wout):
    t = x.shape[0]

    def body(x_ref, oa_ref, ob_ref, gt_ref, pt_ref, wo_ref, y_ref, ya_ref, yb_ref, mg_ref):
        ya = _dot_nt(oa_ref[...], pt_ref[:, 0:A_WIDTH])
        yb = _dot_nt(ob_ref[...], pt_ref[:, A_WIDTH:A_WIDTH + B_Q_WIDTH])
        ya_ref[...] = ya.astype(BF16)
        yb_ref[...] = yb.astype(BF16)
        mg = jax.nn.sigmoid(gt_ref[:, 0:D_MODEL]) * ya + jax.nn.sigmoid(gt_ref[:, D_MODEL:2 * D_MODEL]) * yb
        mgb = mg.astype(BF16)
        mg_ref[...] = mgb
        y_ref[...] = x_ref[...] + _dot_nn(mgb, wo_ref[...])

    return pl.pallas_call(
        body,
        name="mix_out_fwd",
        grid=(t // TM,),
        in_specs=[_rows(TM, D_MODEL), _rows(TM, A_WIDTH), _rows(TM, B_Q_WIDTH), _rows(TM, 2 * D_MODEL),
                  _resident((D_MODEL, A_WIDTH + B_Q_WIDTH)), _resident((D_MODEL, D_MODEL))],
        out_specs=[_rows(TM, D_MODEL), _rows(TM, D_MODEL), _rows(TM, D_MODEL), _rows(TM, D_MODEL)],
        out_shape=[jax.ShapeDtypeStruct((t, D_MODEL), F32), jax.ShapeDtypeStruct((t, D_MODEL), BF16),
                   jax.ShapeDtypeStruct((t, D_MODEL), BF16), jax.ShapeDtypeStruct((t, D_MODEL), BF16)],
        compiler_params=_cparams(("parallel",)),
    )(x, oa, ob, gates, proj_t, wout)


def _mix_out_bwd(d, gates, ya, yb, proj_t, wout, hosted=()):
    t = d.shape[0]

    def body(d_ref, gt_ref, ya_ref, yb_ref, pt_ref, wo_ref, db_ref, dya_ref, dyb_ref, doa_ref, dob_ref, dgt_ref):
        db = d_ref[...].astype(BF16)
        db_ref[...] = db
        dmg = _dot_nt(db, wo_ref[...])
        sa = jax.nn.sigmoid(gt_ref[:, 0:D_MODEL])
        sb = jax.nn.sigmoid(gt_ref[:, D_MODEL:2 * D_MODEL])
        dya = (dmg * sa).astype(BF16)
        dyb = (dmg * sb).astype(BF16)
        dya_ref[...] = dya
        dyb_ref[...] = dyb
        dgt_ref[:, 0:D_MODEL] = (dmg * ya_ref[...].astype(F32) * (sa * (1.0 - sa))).astype(BF16)
        dgt_ref[:, D_MODEL:2 * D_MODEL] = (dmg * yb_ref[...].astype(F32) * (sb * (1.0 - sb))).astype(BF16)
        doa_ref[...] = _dot_nn(dya, pt_ref[:, 0:A_WIDTH]).astype(BF16)
        dob_ref[...] = _dot_nn(dyb, pt_ref[:, A_WIDTH:A_WIDTH + B_Q_WIDTH]).astype(BF16)

    return _call(
        body,
        name="mix_out_bwd",
        grid=(t // TM,),
        in_specs=[_rows(TM, D_MODEL), _rows(TM, 2 * D_MODEL), _rows(TM, D_MODEL), _rows(TM, D_MODEL),
                  _resident((D_MODEL, A_WIDTH + B_Q_WIDTH)), _resident((D_MODEL, D_MODEL))],
        out_specs=[_rows(TM, D_MODEL), _rows(TM, D_MODEL), _rows(TM, D_MODEL), _rows(TM, A_WIDTH),
                   _rows(TM, B_Q_WIDTH), _rows(TM, 2 * D_MODEL)],
        out_shape=[jax.ShapeDtypeStruct((t, D_MODEL), BF16), jax.ShapeDtypeStruct((t, D_MODEL), BF16),
                   jax.ShapeDtypeStruct((t, D_MODEL), BF16), jax.ShapeDtypeStruct((t, A_WIDTH), BF16),
                   jax.ShapeDtypeStruct((t, B_Q_WIDTH), BF16), jax.ShapeDtypeStruct((t, 2 * D_MODEL), BF16)],
        args=(d, gates, ya, yb, proj_t, wout), sem=("parallel",), hosted=hosted)


def _loss_head(x, gamma, target):
    t = x.shape[0]

    def body(x_ref, gam_ref, t_ref, dx_ref, dgam_ref, loss_ref):
        xh, r = _rms(x_ref[...])
        gam = gam_ref[...]
        e = xh * gam - t_ref[...]
        dy = e * (1.0 / D_MODEL)
        dxn, dgam = _rms_bwd(dy, xh, r, gam)
        dx_ref[...] = dxn

        @pl.when(pl.program_id(0) == 0)
        def _():
            dgam_ref[...] = jnp.zeros_like(dgam_ref)
            loss_ref[...] = jnp.zeros_like(loss_ref)

        dgam_ref[...] += dgam
        loss_ref[...] += _colsum8(e * e) * (0.5 / D_MODEL)

    return pl.pallas_call(
        body,
        name="loss_head",
        grid=(t // TM,),
        in_specs=[_rows(TM, D_MODEL), _resident((1, D_MODEL)), _rows(TM, D_MODEL)],
        out_specs=[_rows(TM, D_MODEL), pl.BlockSpec((8, D_MODEL), lambda i: (0, 0)),
                   pl.BlockSpec((8, D_MODEL), lambda i: (0, 0))],
        out_shape=[jax.ShapeDtypeStruct((t, D_MODEL), F32), jax.ShapeDtypeStruct((8, D_MODEL), F32),
                   jax.ShapeDtypeStruct((8, D_MODEL), F32)],
        compiler_params=_cparams(("arbitrary",)),
    )(x, gamma, target)


def _place():
    x, y, c = lax.axis_index("x"), lax.axis_index("y"), lax.axis_index("c")
    chips = [(1 - x, y), (x, 1 - y), (1 - x, 1 - y)]
    return x, y, c, chips


class _Gather:
    per = 7

    def __init__(self, shards):
        n = len(shards)
        self.inputs = list(shards)
        self.out_shape = [jax.ShapeDtypeStruct((N_DEV * s.shape[0], s.shape[1]), s.dtype) for s in shards]
        self.scratch = [pltpu.SemaphoreType.DMA((n * self.per,)), pltpu.SemaphoreType.DMA((n * self.per,)),
                        pltpu.SemaphoreType.DMA((n,))]
        self.result = None

    def _parts(self, ins, outs, sems):
        send_sems, recv_sems, local_sems = sems
        x, y, c, chips = _place()
        me, sibling = (x, y, c), (x, y, 1 - c)
        n = len(ins)

        def rows(k, p):
            r = ins[k].shape[0]
            return outs[k].at[pl.ds((4 * p[0] + 2 * p[1] + p[2]) * r, r), :]

        def copy(k, slot, block, to, src=None):
            return pltpu.make_async_remote_copy(
                src_ref=rows(k, block) if src is None else src, dst_ref=rows(k, block),
                send_sem=send_sems.at[k * self.per + slot], recv_sem=recv_sems.at[k * self.per + slot],
                device_id=to, device_id_type=MESH)

        mine = [pltpu.make_async_copy(ins[k], rows(k, me), local_sems.at[k]) for k in range(n)]
        first = []
        for k in range(n):
            first.append(copy(k, 0, me, sibling, src=ins[k]))
            first += [copy(k, 1 + j, me, (*chip, c), src=ins[k]) for j, chip in enumerate(chips)]
        passed = [copy(k, 4 + j, (*chip, c), sibling) for j, chip in enumerate(chips) for k in range(n)]
        return n, c, me, sibling, chips, copy, mine, first, passed

    def start(self, ins, outs, sems):
        _, _, _, _, _, _, mine, first, _ = self._parts(ins, outs, sems)
        for cp in mine + first:
            cp.start()

    def forward(self, ins, outs, sems):
        n, c, me, _, chips, copy, _, _, passed = self._parts(ins, outs, sems)
        for j, chip in enumerate(chips):
            for k in range(n):
                copy(k, 1 + j, (*chip, c), me).wait_recv()
                passed[j * n + k].start()

    def finish(self, ins, outs, sems):
        n, c, me, sibling, chips, copy, mine, first, passed = self._parts(ins, outs, sems)
        for k in range(n):
            copy(k, 0, sibling, me).wait_recv()
            for j, chip in enumerate(chips):
                copy(k, 4 + j, (*chip, 1 - c), me).wait_recv()
        for cp in first + passed:
            cp.wait_send()
        for cp in mine:
            cp.wait()


class _PairExchange:
    def __init__(self, grads):
        n = len(grads)
        self.inputs = list(grads)
        self.out_shape = [jax.ShapeDtypeStruct((g.shape[0] // 2, g.shape[1]), g.dtype) for g in grads]
        self.scratch = [pltpu.SemaphoreType.DMA((n * N_CHIP,)), pltpu.SemaphoreType.DMA((n * N_CHIP,))]
        self.result = None

    def _copies(self, ins, outs, sems):
        send_sems, recv_sems = sems
        x, y, c, _ = _place()
        copies = []
        for k in range(len(ins)):
            r = ins[k].shape[0] // N_DEV
            for q in range(N_CHIP):
                copies.append(pltpu.make_async_remote_copy(
                    src_ref=ins[k].at[pl.ds((2 * q + 1 - c) * r, r), :], dst_ref=outs[k].at[pl.ds(q * r, r), :],
                    send_sem=send_sems.at[k * N_CHIP + q], recv_sem=recv_sems.at[k * N_CHIP + q],
                    device_id=(x, y, 1 - c), device_id_type=MESH))
        return copies

    def start(self, ins, outs, sems):
        for cp in self._copies(ins, outs, sems):
            cp.start()

    def forward(self, ins, outs, sems):
        pass

    def finish(self, ins, outs, sems):
        copies = self._copies(ins, outs, sems)
        for cp in copies:
            cp.wait_recv()
        for cp in copies:
            cp.wait_send()


class _ChipExchange(_PairExchange):
    def __init__(self, psums):
        n = len(psums)
        self.inputs = list(psums)
        self.out_shape = [jax.ShapeDtypeStruct((3 * p.shape[0] // N_CHIP, p.shape[1]), p.dtype) for p in psums]
        self.scratch = [pltpu.SemaphoreType.DMA((n * 3,)), pltpu.SemaphoreType.DMA((n * 3,))]
        self.result = None

    def _copies(self, ins, outs, sems):
        send_sems, recv_sems = sems
        _, _, c, chips = _place()
        copies = []
        for k in range(len(ins)):
            r = ins[k].shape[0] // N_CHIP
            for j, chip in enumerate(chips):
                copies.append(pltpu.make_async_remote_copy(
                    src_ref=ins[k].at[pl.ds((2 * chip[0] + chip[1]) * r, r), :], dst_ref=outs[k].at[pl.ds(j * r, r), :],
                    send_sem=send_sems.at[k * 3 + j], recv_sem=recv_sems.at[k * 3 + j],
                    device_id=(*chip, c), device_id_type=MESH))
        return copies


def _exchange_alone(xchg, name):
    n_in, n_out = len(xchg.inputs), len(xchg.out_shape)

    def body(*refs):
        ins, outs, sems = refs[:n_in], refs[n_in:n_in + n_out], refs[n_in + n_out:]
        xchg.start(ins, outs, sems)
        xchg.forward(ins, outs, sems)
        xchg.finish(ins, outs, sems)

    xchg.result = list(pl.pallas_call(
        body, name=name, in_specs=[_hbm()] * n_in, out_specs=[_hbm()] * n_out, out_shape=xchg.out_shape,
        scratch_shapes=xchg.scratch)(*xchg.inputs))
    return xchg.result


def _pair_sum(core, grads, recvd, name):
    n = len(grads)
    r = grads[0].shape[0] // N_DEV
    cdim = grads[0].shape[1]
    tr = r // 2 if r % 32 == 0 else r
    nt = r // tr

    def body(core_ref, *refs):
        del core_ref
        for k in range(n):
            refs[2 * n + k][...] = (refs[k][...].astype(F32) + refs[n + k][...].astype(F32)).astype(BF16)

    gspec = pl.BlockSpec((tr, cdim), lambda q, i, core_ref: ((2 * q + core_ref[0]) * nt + i, 0))
    rspec = pl.BlockSpec((tr, cdim), lambda q, i, core_ref: (q * nt + i, 0))
    return pl.pallas_call(
        body,
        name=name,
        grid_spec=pltpu.PrefetchScalarGridSpec(
            num_scalar_prefetch=1, grid=(N_CHIP, nt), in_specs=[gspec] * n + [rspec] * n, out_specs=[rspec] * n),
        out_shape=[jax.ShapeDtypeStruct((N_CHIP * r, cdim), BF16) for _ in range(n)],
        compiler_params=_cparams(("parallel", "parallel")),
    )(core, *grads, *recvd)


def _final_sum(chip, psums, recvd, name):
    n = len(psums)
    r = psums[0].shape[0] // N_CHIP
    cdim = psums[0].shape[1]
    tr = r // 2 if r % 32 == 0 else r
    nt = r // tr

    def body(chip_ref, *refs):
        del chip_ref
        for k in range(n):
            got = refs[n + k]
            tot = refs[k][...].astype(F32) + got[0].astype(F32)
            tot = tot + got[1].astype(F32)
            tot = tot + got[2].astype(F32)
            refs[2 * n + k][...] = tot

    pspec = pl.BlockSpec((tr, cdim), lambda i, chip_ref: (chip_ref[0] * nt + i, 0))
    rspec = pl.BlockSpec((3, tr, cdim), lambda i, chip_ref: (0, i, 0))
    ospec = pl.BlockSpec((tr, cdim), lambda i, chip_ref: (i, 0))
    return pl.pallas_call(
        body,
        name=name,
        grid_spec=pltpu.PrefetchScalarGridSpec(
            num_scalar_prefetch=1, grid=(nt,), in_specs=[pspec] * n + [rspec] * n, out_specs=[ospec] * n),
        out_shape=[jax.ShapeDtypeStruct((r, cdim), F32) for _ in range(n)],
        compiler_params=_cparams(("parallel",)),
    )(chip, *psums, *[g.reshape(3, r, cdim) for g in recvd])


SMALL_ROWS = 16


def _all_reduce_small(part):
    def body(p_ref, o_ref, buf, send_sems, recv_sems):
        x, y, c, _ = _place()
        me = 4 * x + 2 * y + c
        buf[me] = p_ref[...]
        copies = []
        for d in range(1, N_DEV):
            peer = me ^ d
            copies.append(pltpu.make_async_remote_copy(
                src_ref=p_ref, dst_ref=buf.at[me], send_sem=send_sems.at[d - 1], recv_sem=recv_sems.at[d - 1],
                device_id=(peer // 4, (peer // 2) % 2, peer % 2), device_id_type=MESH))
        for cp in copies:
            cp.start()
        for cp in copies:
            cp.wait_recv()
        for cp in copies:
            cp.wait_send()
        tot = buf[0]
        for d in range(1, N_DEV):
            tot = tot + buf[d]
        o_ref[...] = tot

    return pl.pallas_call(
        body,
        name="all_reduce_small",
        in_specs=[pl.BlockSpec(memory_space=pltpu.VMEM)],
        out_specs=pl.BlockSpec(memory_space=pltpu.VMEM),
        out_shape=jax.ShapeDtypeStruct(part.shape, F32),
        scratch_shapes=[pltpu.VMEM((N_DEV,) + part.shape, F32), pltpu.SemaphoreType.DMA((N_DEV - 1,)),
                        pltpu.SemaphoreType.DMA((N_DEV - 1,))],
    )(part)


def _adamw(ws, gs, ms, vs, name):
    n = len(ws)
    r, cdim = ws[0].shape
    tr = r
    for cand in (512, 256, 128, 176, 64):
        if r % cand == 0 and r > cand:
            tr = cand
            break
    c1 = 1.0 - ADAM_B1 ** ADAM_STEP
    c2 = 1.0 - ADAM_B2 ** ADAM_STEP

    def body(*refs):
        for k in range(n):
            w, g, m, v = (refs[j * n + k][...] for j in range(4))
            m2 = ADAM_B1 * m + (1.0 - ADAM_B1) * g
            v2 = ADAM_B2 * v + (1.0 - ADAM_B2) * (g * g)
            delta = -ADAM_LR * ((m2 / c1) / (jnp.sqrt(v2 / c2) + ADAM_EPS) + ADAM_WD * w)
            refs[4 * n + k][...] = delta
            refs[5 * n + k][...] = m2
            refs[6 * n + k][...] = v2

    spec = pl.BlockSpec((tr, cdim), lambda i: (i, 0))
    outs = pl.pallas_call(
        body,
        name=name,
        grid=(r // tr,),
        in_specs=[spec] * (4 * n),
        out_specs=[spec] * (3 * n),
        out_shape=[jax.ShapeDtypeStruct((r, cdim), F32)] * (3 * n),
        compiler_params=_cparams(("parallel",)),
    )(*ws, *gs, *ms, *vs)
    return outs[:n], outs[n:2 * n], outs[2 * n:]


def _bias_b():
    pad = B_PREV * CHUNK
    slopes = np.array([2.0 ** (-8.0 * (i + 1) / B_Q_HEADS) for i in range(B_Q_HEADS)], dtype=np.float32)
    dist = np.abs(np.arange(TQ)[:, None] - np.arange(TQ + pad)[None, :] + pad).astype(np.float32)
    bias = -slopes.reshape(B_Q_HEADS, 1, 1) * dist[None]
    qc = (np.arange(TQ)[:, None] + pad) // CHUNK
    kc = np.arange(TQ + pad)[None, :] // CHUNK
    allowed = (kc <= qc) & (kc >= qc - B_PREV)
    return np.where(allowed[None], bias, np.float32(NEG_INF)).astype(np.float32)


def kernel(x, ffn1_norm, ffn1_w_gate, ffn1_w_up, ffn1_w_down, mix_norm, w_in, rel_bias, sinks, w_proj_a, w_proj_b, w_out, ffn2_norm, ffn2_w_gate, ffn2_w_up, ffn2_w_down, final_norm, loss_target, m_ffn1_norm, m_ffn1_w_gate, m_ffn1_w_up, m_ffn1_w_down, m_mix_norm, m_w_in, m_rel_bias, m_sinks, m_w_proj_a, m_w_proj_b, m_w_out, m_ffn2_norm, m_ffn2_w_gate, m_ffn2_w_up, m_ffn2_w_down, m_final_norm, v_ffn1_norm, v_ffn1_w_gate, v_ffn1_w_up, v_ffn1_w_down, v_mix_norm, v_w_in, v_rel_bias, v_sinks, v_w_proj_a, v_w_proj_b, v_w_out, v_ffn2_norm, v_ffn2_w_gate, v_ffn2_w_up, v_ffn2_w_down, v_final_norm):
    bsz, s_len, _ = x.shape
    t = bsz * s_len
    core = lax.axis_index("c").astype(jnp.int32).reshape(1)
    chip = (2 * lax.axis_index("x") + lax.axis_index("y")).astype(jnp.int32).reshape(1)

    def row_form(w):
        return w.astype(BF16).T

    wg1, wu1, wd1 = _exchange_alone(
        _Gather([row_form(ffn1_w_gate), row_form(ffn1_w_up), ffn1_w_down.astype(BF16)]), "gather_ffn1")
    gather_mix = _Gather([row_form(w_in), jnp.concatenate([row_form(w_proj_a), row_form(w_proj_b)], axis=1),
                          w_out.astype(BF16)])
    gather_ffn2_gate = _Gather([row_form(ffn2_w_gate)])
    gather_ffn2_rest = _Gather([row_form(ffn2_w_up), ffn2_w_down.astype(BF16)])

    x0 = x.reshape(t, D_MODEL)
    tgt = loss_target.reshape(t, D_MODEL)
    gam1, gam2, gam3, gam4 = (g.reshape(1, D_MODEL) for g in (ffn1_norm, mix_norm, ffn2_norm, final_norm))

    h1, g1, u1, a1, x1 = _ffn_fwd(x0, gam1, wg1, wu1, wd1, "ffn1_fwd", hosted=[gather_mix])
    win_t, proj_t, wout = gather_mix.result
    h2, qkv_a, qkv_b, gates = _proj_fwd(x1, gam2, win_t, hosted=[gather_ffn2_gate])
    (wg2,) = gather_ffn2_gate.result
    qkv_a3 = qkv_a.reshape(bsz, s_len, QKV_A)
    qkv_b3 = qkv_b.reshape(bsz, s_len, QKV_B)

    far = jnp.broadcast_to(rel_bias[:, REL_TABLE - 1:REL_TABLE], (A_HEADS, REL_WRAP // 2))
    tv = jnp.concatenate([far, jnp.flip(rel_bias, axis=1), jnp.zeros((A_HEADS, REL_WRAP // 2 - REL_TABLE), F32)], axis=1)
    bias_a = _bias_a_build(tv.reshape(A_HEADS, 1, REL_WRAP))
    bias_b = jnp.asarray(_bias_b())
    sink_rows = jnp.broadcast_to(sinks.reshape(B_Q_HEADS, 1, 1), (B_Q_HEADS, 8, LANES))

    oa = _attn_a_fwd(qkv_a3, bias_a, hosted=[gather_ffn2_rest]).reshape(t, A_WIDTH)
    wu2, wd2 = gather_ffn2_rest.result
    ob = _attn_b_fwd(qkv_b3, bias_b, sink_rows).reshape(t, B_Q_WIDTH)
    x2, ya, yb, mg = _mix_out_fwd(x1, oa, ob, gates, proj_t, wout)
    h3, g2, u2, a2, x3 = _ffn_fwd(x2, gam3, wg2, wu2, wd2, "ffn2_fwd")

    dx3, dgam4, loss_part = _loss_head(x3, gam4, tgt)

    dx2, dg2, du2, db2, dgam3 = _ffn_bwd(dx3, x2, gam3, g2, u2, wg2, wu2, wd2, "ffn2_bwd")
    gw_ffn2 = [_mm_tn([dg2], h3, "grad_ffn2_gate"), _mm_tn([du2], h3, "grad_ffn2_up"),
               _mm_tn([a2], db2, "grad_ffn2_down")]
    pairx_ffn2 = _PairExchange(gw_ffn2)
    dxb, dya, dyb, doa, dob, dgates = _mix_out_bwd(dx2, gates, ya, yb, proj_t, wout, hosted=[pairx_ffn2])
    psum_ffn2 = _pair_sum(core, gw_ffn2, pairx_ffn2.result, "pair_sum_ffn2")
    gw_out = _mm_tn([mg], dxb, "grad_w_out")
    gw_proj = _mm_tn_proj(dya, dyb, oa, ob)

    chipx_ffn2 = _ChipExchange(psum_ffn2)
    dqa, dka, dva, dbias_a = _attn_a_bwd(qkv_a3, bias_a, doa.reshape(bsz, s_len, A_WIDTH), hosted=[chipx_ffn2])
    pairx_out = _PairExchange([gw_proj, gw_out])
    dqb, dkvb, dsink = _attn_b_bwd(qkv_b3, bias_b, sink_rows, dob.reshape(bsz, s_len, B_Q_WIDTH), hosted=[pairx_out])
    drel_lanes = _relbias_grad(dbias_a)
    dproj = [dqa.reshape(t, A_WIDTH), dka.reshape(t, A_WIDTH), dva.reshape(t, A_WIDTH), dqb.reshape(t, B_Q_WIDTH),
             dkvb.reshape(t, 2 * B_KV_WIDTH), dgates]

    dx1, db1, dgam2 = _proj_bwd(dx2, x1, gam2, dproj, win_t)
    gw_in = _mm_tn(dproj, h2, "grad_w_in")
    pairx_in = _PairExchange([gw_in])
    gw_d1 = _mm_tn([a1], db1, "grad_ffn1_down", hosted=[pairx_in])
    psum_mix = (_pair_sum(core, [gw_in], pairx_in.result, "pair_sum_w_in")
                + _pair_sum(core, [gw_proj, gw_out], pairx_out.result, "pair_sum_mix"))

    chipx_mix = _ChipExchange(psum_mix)
    pairx_d1 = _PairExchange([gw_d1])
    dx0, dg1, du1, _, dgam1 = _ffn_bwd(dx1, x0, gam1, g1, u1, wg1, wu1, wd1, "ffn1_bwd", hosted=[chipx_mix, pairx_d1])
    psum_d1 = _pair_sum(core, [gw_d1], pairx_d1.result, "pair_sum_ffn1_down")
    chipx_d1 = _ChipExchange(psum_d1)
    gw_g1 = _mm_tn([dg1], h1, "grad_ffn1_gate", hosted=[chipx_d1])
    from_sibling_g1 = _exchange_alone(_PairExchange([gw_g1]), "pair_exchange_ffn1_gate")
    psum_g1 = _pair_sum(core, [gw_g1], from_sibling_g1, "pair_sum_ffn1_gate")
    chipx_g1 = _ChipExchange(psum_g1)
    gw_u1 = _mm_tn([du1], h1, "grad_ffn1_up", hosted=[chipx_g1])
    from_sibling_u1 = _exchange_alone(_PairExchange([gw_u1]), "pair_exchange_ffn1_up")
    psum_u1 = _pair_sum(core, [gw_u1], from_sibling_u1, "pair_sum_ffn1_up")
    from_chips_u1 = _exchange_alone(_ChipExchange(psum_u1), "chip_exchange_ffn1_up")

    g_g1, g_u1, g_d1, g_g2, g_u2, g_d2 = _final_sum(
        chip, psum_g1 + psum_u1 + psum_d1 + psum_ffn2,
        chipx_g1.result + from_chips_u1 + chipx_d1.result + chipx_ffn2.result, "grad_sum_ffn")
    (g_in,) = _final_sum(chip, psum_mix[0:1], chipx_mix.result[0:1], "grad_sum_w_in")
    g_proj, g_out = _final_sum(chip, psum_mix[1:3], chipx_mix.result[1:3], "grad_sum_mix")
    grads = {
        "ffn1_w_gate": g_g1.T, "ffn1_w_up": g_u1.T, "ffn1_w_down": g_d1, "w_in": g_in.T,
        "w_proj_a": g_proj[:, 0:A_WIDTH].T, "w_proj_b": g_proj[:, A_WIDTH:].T, "w_out": g_out,
        "ffn2_w_gate": g_g2.T, "ffn2_w_up": g_u2.T, "ffn2_w_down": g_d2,
    }

    def row_of(v):
        return jnp.pad(v.reshape(1, -1), ((0, 0), (0, D_MODEL - v.size)))

    def table_rows(v):
        return jnp.pad(v, ((0, 0), (0, D_MODEL - REL_TABLE)))

    drel_local = jnp.flip(drel_lanes[:, 0, 0:REL_TABLE], axis=1)
    small_part = jnp.concatenate(
        [jnp.sum(dgam1, axis=0, keepdims=True), jnp.sum(dgam2, axis=0, keepdims=True),
         jnp.sum(dgam3, axis=0, keepdims=True), jnp.sum(dgam4, axis=0, keepdims=True),
         row_of(jnp.sum(loss_part)), row_of(dsink[:, 0, 0]), jnp.zeros((2, D_MODEL), F32),
         table_rows(drel_local)], axis=0)
    small = _all_reduce_small(small_part)
    loss = small[4, 0]

    def pack(n1, n2, n3, n4, sk, tb):
        return jnp.concatenate([n1.reshape(1, -1), n2.reshape(1, -1), n3.reshape(1, -1), n4.reshape(1, -1),
                                jnp.zeros((1, D_MODEL), F32), row_of(sk), jnp.zeros((2, D_MODEL), F32), table_rows(tb)],
                               axis=0)

    live = np.zeros((SMALL_ROWS, D_MODEL), np.float32)
    live[0:4] = 1.0
    live[5, 0:B_Q_HEADS] = 1.0
    live[8:16, 0:REL_TABLE] = 1.0
    small_g = small * jnp.asarray(live)
    sw = pack(ffn1_norm, mix_norm, ffn2_norm, final_norm, sinks, rel_bias)
    sm = pack(m_ffn1_norm, m_mix_norm, m_ffn2_norm, m_final_norm, m_sinks, m_rel_bias)
    sv = pack(v_ffn1_norm, v_mix_norm, v_ffn2_norm, v_final_norm, v_sinks, v_rel_bias)
    (sd,), (snm,), (snv,) = _adamw([sw], [small_g], [sm], [sv], "adamw_small")

    def unpack(p):
        return {"ffn1_norm": p[0], "mix_norm": p[1], "ffn2_norm": p[2], "final_norm": p[3],
                "sinks": p[5, 0:B_Q_HEADS], "rel_bias": p[8:16, 0:REL_TABLE]}

    grads.update(unpack(small_g))
    delta, new_m, new_v = unpack(sd), unpack(snm), unpack(snv)

    wmv = {
        "ffn1_w_gate": (ffn1_w_gate, m_ffn1_w_gate, v_ffn1_w_gate), "ffn1_w_up": (ffn1_w_up, m_ffn1_w_up, v_ffn1_w_up),
        "ffn1_w_down": (ffn1_w_down, m_ffn1_w_down, v_ffn1_w_down), "w_in": (w_in, m_w_in, v_w_in),
        "w_proj_a": (w_proj_a, m_w_proj_a, v_w_proj_a), "w_proj_b": (w_proj_b, m_w_proj_b, v_w_proj_b),
        "w_out": (w_out, m_w_out, v_w_out),
        "ffn2_w_gate": (ffn2_w_gate, m_ffn2_w_gate, v_ffn2_w_gate), "ffn2_w_up": (ffn2_w_up, m_ffn2_w_up, v_ffn2_w_up),
        "ffn2_w_down": (ffn2_w_down, m_ffn2_w_down, v_ffn2_w_down),
    }
    groups = [("adamw_ffn_up", ["ffn1_w_gate", "ffn1_w_up", "ffn2_w_gate", "ffn2_w_up"]),
              ("adamw_ffn_down", ["ffn1_w_down", "ffn2_w_down"]), ("adamw_w_in", ["w_in"]),
              ("adamw_proj", ["w_proj_a", "w_proj_b"]), ("adamw_w_out", ["w_out"])]
    for gname, names in groups:
        ds_, ms_, vs_ = _adamw([wmv[n][0] for n in names], [grads[n] for n in names], [wmv[n][1] for n in names],
                               [wmv[n][2] for n in names], gname)
        for n, d_, m_, v_ in zip(names, ds_, ms_, vs_):
            delta[n], new_m[n], new_v[n] = d_, m_, v_

    order = ["ffn1_norm", "ffn1_w_gate", "ffn1_w_up", "ffn1_w_down", "mix_norm", "w_in", "rel_bias", "sinks",
             "w_proj_a", "w_proj_b", "w_out", "ffn2_norm", "ffn2_w_gate", "ffn2_w_up", "ffn2_w_down", "final_norm"]
    grad_x = dx0.reshape(bsz, s_len, D_MODEL)
    return (loss, grad_x, *[grads[n] for n in order], *[delta[n] for n in order], *[new_m[n] for n in order],
            *[new_v[n] for n in order])
```

```python
import numpy as np
import jax
import jax.numpy as jnp
from jax import lax
from jax.experimental import pallas as pl
from jax.experimental.pallas import tpu as pltpu

F32 = jnp.float32
BF16 = jnp.bfloat16

D_MODEL = 1024
D_FF = 2816
CHUNK = 64
D_HEAD = 64
A_HEADS = 8
A_PREV = 8
MAX_REL = 128
B_Q_HEADS = 8
B_KV_HEADS = 2
B_GROUP = B_Q_HEADS // B_KV_HEADS
B_PREV = 2
REL_TABLE = (CHUNK - 1) + MAX_REL + 1
A_WIDTH = A_HEADS * D_HEAD
B_Q_WIDTH = B_Q_HEADS * D_HEAD
B_KV_WIDTH = B_KV_HEADS * D_HEAD
QKV_A = 3 * A_WIDTH
QKV_B = B_Q_WIDTH + 2 * B_KV_WIDTH
IN_WIDTH = QKV_A + QKV_B + 2 * D_MODEL
EPS = 1e-6
NEG_INF = -1e30
SCALE = 1.0 / 8.0

ADAM_LR = 0.001
ADAM_B1 = 0.9
ADAM_B2 = 0.999
ADAM_EPS = 1e-08
ADAM_WD = 0.01
ADAM_STEP = 10

N_DEV = 8
N_CHIP = 4
MESH = pl.DeviceIdType.MESH

LANES = 128
TQ = 256
TM = 256
TM_FWD = 256
FC = 256
VMEM_LIMIT = 56 << 20


def _cparams(sem, vmem=VMEM_LIMIT):
    return pltpu.CompilerParams(dimension_semantics=sem, vmem_limit_bytes=vmem)


def _dot_nt(a, b):
    return lax.dot_general(a, b, (((1,), (1,)), ((), ())), preferred_element_type=F32)


def _dot_nn(a, b):
    return lax.dot_general(a, b, (((1,), (0,)), ((), ())), preferred_element_type=F32)


def _dot_tn(a, b):
    return lax.dot_general(a, b, (((0,), (0,)), ((), ())), preferred_element_type=F32)


def _resident(shape):
    nd = len(shape)
    return pl.BlockSpec(shape, lambda *_: (0,) * nd, pipeline_mode=pl.Buffered(1))


def _rows(tm, width):
    return pl.BlockSpec((tm, width), lambda i: (i, 0))


def _colsum8(v):
    tm, n = v.shape
    return jnp.sum(v.reshape(tm // 8, 8, n), axis=0)


def _rms(x):
    r = lax.rsqrt(jnp.mean(x * x, axis=-1, keepdims=True) + EPS)
    return x * r, r


def _rms_bwd(dh, xh, r, gamma):
    dxh = dh * gamma
    dx = r * (dxh - xh * jnp.mean(dxh * xh, axis=-1, keepdims=True))
    return dx, _colsum8(dh * xh)


def _hbm():
    return pl.BlockSpec(memory_space=pltpu.HBM)


def _call(body, *, name, grid, in_specs, out_specs, out_shape, args, sem, scratch_shapes=(), hosted=()):
    in_specs, out_specs, out_shape = list(in_specs), list(out_specs), list(out_shape)
    scratch_shapes = list(scratch_shapes)
    if not hosted:
        return pl.pallas_call(body, name=name, grid=grid, in_specs=in_specs, out_specs=out_specs, out_shape=out_shape,
                              scratch_shapes=scratch_shapes, compiler_params=_cparams(sem))(*args)
    n_in, n_out, n_scr = len(in_specs), len(out_specs), len(scratch_shapes)
    x_in = [a for x in hosted for a in x.inputs]
    x_out = [s for x in hosted for s in x.out_shape]
    x_scr = [s for x in hosted for s in x.scratch]
    steps = int(np.prod(grid))
    forward_step = max(steps - 3, 0)

    def wrapped(*refs):
        pos = [0]

        def take(k):
            pos[0] += k
            return refs[pos[0] - k:pos[0]]

        ins, xin, outs, xout, scr, xscr = (take(k) for k in (n_in, len(x_in), n_out, len(x_out), n_scr, len(x_scr)))
        step = 0
        for axis, extent in enumerate(grid):
            step = step * extent + pl.program_id(axis)
        own, oi, oo, osc = [], 0, 0, 0
        for x in hosted:
            own.append((xin[oi:oi + len(x.inputs)], xout[oo:oo + len(x.out_shape)], xscr[osc:osc + len(x.scratch)]))
            oi, oo, osc = oi + len(x.inputs), oo + len(x.out_shape), osc + len(x.scratch)

        def phase(method):
            for x, (i_, o_, s_) in zip(hosted, own):
                getattr(x, method)(i_, o_, s_)

        pl.when(step == 0)(lambda: phase("start"))
        body(*ins, *outs, *scr)
        pl.when(step == forward_step)(lambda: phase("forward"))
        pl.when(step == steps - 1)(lambda: phase("finish"))

    res = pl.pallas_call(
        wrapped, name=name, grid=grid, in_specs=in_specs + [_hbm()] * len(x_in),
        out_specs=out_specs + [_hbm()] * len(x_out), out_shape=out_shape + x_out,
        scratch_shapes=scratch_shapes + x_scr, compiler_params=_cparams(("arbitrary",) * len(grid)))(*args, *x_in)
    rest = list(res[n_out:])
    for x in hosted:
        x.result, rest = rest[:len(x.out_shape)], rest[len(x.out_shape):]
    return list(res[:n_out])


def _ffn_fwd(x, gamma, wg_t, wu_t, wd, name, hosted=()):
    t = x.shape[0]
    f = wg_t.shape[0]

    def body(x_ref, gam_ref, wg_ref, wu_ref, wd_ref, h_ref, g_ref, u_ref, a_ref, y_ref):
        xv = x_ref[...]
        xh, _ = _rms(xv)
        h = (xh * gam_ref[...]).astype(BF16)
        h_ref[...] = h
        for j in range(f // FC):
            sl = slice(j * FC, (j + 1) * FC)
            g = _dot_nt(h, wg_ref[sl, :])
            u = _dot_nt(h, wu_ref[sl, :])
            g_ref[:, sl] = g.astype(BF16)
            u_ref[:, sl] = u.astype(BF16)
            a_ref[:, sl] = (g * jax.nn.sigmoid(g) * u).astype(BF16)
        y_ref[...] = xv + 0.5 * _dot_nn(a_ref[...], wd_ref[...])

    return _call(
        body,
        name=name,
        grid=(t // TM_FWD,),
        in_specs=[_rows(TM_FWD, D_MODEL), _resident((1, D_MODEL)), _resident((f, D_MODEL)), _resident((f, D_MODEL)),
                  _resident((f, D_MODEL))],
        out_specs=[_rows(TM_FWD, D_MODEL), _rows(TM_FWD, f), _rows(TM_FWD, f), _rows(TM_FWD, f),
                   _rows(TM_FWD, D_MODEL)],
        out_shape=[jax.ShapeDtypeStruct((t, D_MODEL), BF16), jax.ShapeDtypeStruct((t, f), BF16),
                   jax.ShapeDtypeStruct((t, f), BF16), jax.ShapeDtypeStruct((t, f), BF16),
                   jax.ShapeDtypeStruct((t, D_MODEL), F32)],
        args=(x, gamma, wg_t, wu_t, wd), sem=("parallel",), hosted=hosted)


def _ffn_bwd(d, x, gamma, g_act, u_act, wg_t, wu_t, wd, name, hosted=()):
    t = x.shape[0]
    f = wg_t.shape[0]

    def body(d_ref, x_ref, gam_ref, g_ref, u_ref, wg_ref, wu_ref, wd_ref, dx_ref, dg_ref, du_ref, db_ref, dgam_ref):
        dv = d_ref[...]
        db = (0.5 * dv).astype(BF16)
        db_ref[...] = db
        for j in range(f // FC):
            sl = slice(j * FC, (j + 1) * FC)
            da = _dot_nt(db, wd_ref[sl, :])
            g = g_ref[:, sl].astype(F32)
            u = u_ref[:, sl].astype(F32)
            s = jax.nn.sigmoid(g)
            dg_ref[:, sl] = (da * u * (s * (1.0 + g * (1.0 - s)))).astype(BF16)
            du_ref[:, sl] = (da * (g * s)).astype(BF16)
        dh = _dot_nn(dg_ref[...], wg_ref[...]) + _dot_nn(du_ref[...], wu_ref[...])
        xh, r = _rms(x_ref[...])
        dxn, dgam = _rms_bwd(dh, xh, r, gam_ref[...])
        dx_ref[...] = dv + dxn

        @pl.when(pl.program_id(0) == 0)
        def _():
            dgam_ref[...] = jnp.zeros_like(dgam_ref)

        dgam_ref[...] += dgam

    return _call(
        body,
        name=name,
        grid=(t // TM,),
        in_specs=[_rows(TM, D_MODEL), _rows(TM, D_MODEL), _resident((1, D_MODEL)), _rows(TM, f), _rows(TM, f),
                  _resident((f, D_MODEL)), _resident((f, D_MODEL)), _resident((f, D_MODEL))],
        out_specs=[_rows(TM, D_MODEL), _rows(TM, f), _rows(TM, f), _rows(TM, D_MODEL),
                   pl.BlockSpec((8, D_MODEL), lambda i: (0, 0))],
        out_shape=[jax.ShapeDtypeStruct((t, D_MODEL), F32), jax.ShapeDtypeStruct((t, f), BF16),
                   jax.ShapeDtypeStruct((t, f), BF16), jax.ShapeDtypeStruct((t, D_MODEL), BF16),
                   jax.ShapeDtypeStruct((8, D_MODEL), F32)],
        args=(d, x, gamma, g_act, u_act, wg_t, wu_t, wd), sem=("arbitrary",), hosted=hosted)


def _ffn_bwd_act(d, g_act, u_act, wd, name, hosted=()):
    t = d.shape[0]
    f = wd.shape[0]

    def body(d_ref, g_ref, u_ref, wd_ref, dg_ref, du_ref):
        db = (0.5 * d_ref[...]).astype(BF16)
        for j in range(f // FC):
            sl = slice(j * FC, (j + 1) * FC)
            da = _dot_nt(db, wd_ref[sl, :])
            g = g_ref[:, sl].astype(F32)
            u = u_ref[:, sl].astype(F32)
            s = jax.nn.sigmoid(g)
            dg_ref[:, sl] = (da * u * (s * (1.0 + g * (1.0 - s)))).astype(BF16)
            du_ref[:, sl] = (da * (g * s)).astype(BF16)

    return _call(
        body,
        name=name,
        grid=(t // TM,),
        in_specs=[_rows(TM, D_MODEL), _rows(TM, f), _rows(TM, f), _resident((f, D_MODEL))],
        out_specs=[_rows(TM, f), _rows(TM, f)],
        out_shape=[jax.ShapeDtypeStruct((t, f), BF16), jax.ShapeDtypeStruct((t, f), BF16)],
        args=(d, g_act, u_act, wd), sem=("parallel",), hosted=hosted)


def _ffn_bwd_in(d, x, gamma, dg, du, wg_t, wu_t, name, hosted=()):
    t = x.shape[0]
    f = wg_t.shape[0]

    def body(d_ref, x_ref, gam_ref, dg_ref, du_ref, wg_ref, wu_ref, dx_ref, dgam_ref):
        dh = _dot_nn(dg_ref[...], wg_ref[...]) + _dot_nn(du_ref[...], wu_ref[...])
        xh, r = _rms(x_ref[...])
        dxn, dgam = _rms_bwd(dh, xh, r, gam_ref[...])
        dx_ref[...] = d_ref[...] + dxn

        @pl.when(pl.program_id(0) == 0)
        def _():
            dgam_ref[...] = jnp.zeros_like(dgam_ref)

        dgam_ref[...] += dgam

    return _call(
        body,
        name=name,
        grid=(t // TM,),
        in_specs=[_rows(TM, D_MODEL), _rows(TM, D_MODEL), _resident((1, D_MODEL)), _rows(TM, f), _rows(TM, f),
                  _resident((f, D_MODEL)), _resident((f, D_MODEL))],
        out_specs=[_rows(TM, D_MODEL), pl.BlockSpec((8, D_MODEL), lambda i: (0, 0))],
        out_shape=[jax.ShapeDtypeStruct((t, D_MODEL), F32), jax.ShapeDtypeStruct((8, D_MODEL), F32)],
        args=(d, x, gamma, dg, du, wg_t, wu_t), sem=("arbitrary",), hosted=hosted)


def _mm_tn(pieces, b, name, tile=256, hosted=()):
    t, n = b.shape
    npc = len(pieces)
    counts = [p.shape[1] // tile for p in pieces]
    los = [sum(counts[:k]) for k in range(npc)]
    total = sum(counts)

    def body(*refs):
        a_refs, b_ref, o_ref = refs[:npc], refs[npc], refs[npc + 1]
        i = pl.program_id(0)
        for k in range(npc):
            @pl.when(jnp.logical_and(i >= los[k], i < los[k] + counts[k]))
            def _(k=k):
                o_ref[...] = _dot_tn(a_refs[k][...], b_ref[...]).astype(BF16)

    def a_spec(k):
        return pl.BlockSpec((t, tile), lambda i: (0, jnp.clip(i - los[k], 0, counts[k] - 1)))

    return _call(
        body,
        name=name,
        grid=(total,),
        in_specs=[a_spec(k) for k in range(npc)] + [_resident((t, n))],
        out_specs=[pl.BlockSpec((tile, n), lambda i: (i, 0))],
        out_shape=[jax.ShapeDtypeStruct((total * tile, n), BF16)],
        args=(*pieces, b), sem=("parallel",), hosted=hosted)[0]


def _mm_tn_proj(dya, dyb, oa, ob, tile=256):
    t = dya.shape[0]

    def body(dya_ref, dyb_ref, oa_ref, ob_ref, o_ref):
        o_ref[:, 0:A_WIDTH] = _dot_tn(dya_ref[...], oa_ref[...]).astype(BF16)
        o_ref[:, A_WIDTH:A_WIDTH + B_Q_WIDTH] = _dot_tn(dyb_ref[...], ob_ref[...]).astype(BF16)

    col = pl.BlockSpec((t, tile), lambda i: (0, i))
    return pl.pallas_call(
        body,
        name="grad_proj",
        grid=(D_MODEL // tile,),
        in_specs=[col, col, _resident((t, A_WIDTH)), _resident((t, B_Q_WIDTH))],
        out_specs=pl.BlockSpec((tile, A_WIDTH + B_Q_WIDTH), lambda i: (i, 0)),
        out_shape=jax.ShapeDtypeStruct((D_MODEL, A_WIDTH + B_Q_WIDTH), BF16),
        compiler_params=_cparams(("parallel",)),
    )(dya, dyb, oa, ob)


def _proj_fwd(x, gamma, win_t, hosted=()):
    t = x.shape[0]

    def body(x_ref, gam_ref, w_ref, h_ref, qa_ref, qb_ref, gt_ref):
        xh, _ = _rms(x_ref[...])
        h = (xh * gam_ref[...]).astype(BF16)
        h_ref[...] = h
        for j in range(QKV_A // FC):
            qa_ref[:, j * FC:(j + 1) * FC] = _dot_nt(h, w_ref[j * FC:(j + 1) * FC, :]).astype(BF16)
        for j in range(QKV_B // FC):
            lo = QKV_A + j * FC
            qb_ref[:, j * FC:(j + 1) * FC] = _dot_nt(h, w_ref[lo:lo + FC, :]).astype(BF16)
        for j in range(2 * D_MODEL // FC):
            lo = QKV_A + QKV_B + j * FC
            gt_ref[:, j * FC:(j + 1) * FC] = _dot_nt(h, w_ref[lo:lo + FC, :])

    return _call(
        body,
        name="proj_fwd",
        grid=(t // TM_FWD,),
        in_specs=[_rows(TM_FWD, D_MODEL), _resident((1, D_MODEL)), _resident((IN_WIDTH, D_MODEL))],
        out_specs=[_rows(TM_FWD, D_MODEL), _rows(TM_FWD, QKV_A), _rows(TM_FWD, QKV_B), _rows(TM_FWD, 2 * D_MODEL)],
        out_shape=[jax.ShapeDtypeStruct((t, D_MODEL), BF16), jax.ShapeDtypeStruct((t, QKV_A), BF16),
                   jax.ShapeDtypeStruct((t, QKV_B), BF16), jax.ShapeDtypeStruct((t, 2 * D_MODEL), F32)],
        args=(x, gamma, win_t), sem=("parallel",), hosted=hosted)


def _proj_bwd(d, x, gamma, pieces, win_t, hosted=()):
    t = x.shape[0]
    npc = len(pieces)
    widths = [p.shape[1] for p in pieces]
    los = [sum(widths[:k]) for k in range(npc)]

    def body(*refs):
        d_ref, x_ref, gam_ref = refs[:3]
        p_refs = refs[3:3 + npc]
        w_ref, dx_ref, db_ref, dgam_ref = refs[3 + npc:]
        dh = _dot_nn(p_refs[0][...], w_ref[0:widths[0], :])
        for k in range(1, npc):
            dh += _dot_nn(p_refs[k][...], w_ref[los[k]:los[k] + widths[k], :])
        xh, r = _rms(x_ref[...])
        dxn, dgam = _rms_bwd(dh, xh, r, gam_ref[...])
        dx = d_ref[...] + dxn
        dx_ref[...] = dx
        db_ref[...] = (0.5 * dx).astype(BF16)

        @pl.when(pl.program_id(0) == 0)
        def _():
            dgam_ref[...] = jnp.zeros_like(dgam_ref)

        dgam_ref[...] += dgam

    return _call(
        body,
        name="proj_bwd",
        grid=(t // TM,),
        in_specs=[_rows(TM, D_MODEL), _rows(TM, D_MODEL), _resident((1, D_MODEL))] + [_rows(TM, w) for w in widths]
        + [_resident((IN_WIDTH, D_MODEL))],
        out_specs=[_rows(TM, D_MODEL), _rows(TM, D_MODEL), pl.BlockSpec((8, D_MODEL), lambda i: (0, 0))],
        out_shape=[jax.ShapeDtypeStruct((t, D_MODEL), F32), jax.ShapeDtypeStruct((t, D_MODEL), BF16),
                   jax.ShapeDtypeStruct((8, D_MODEL), F32)],
        args=(d, x, gamma, *pieces, win_t), sem=("arbitrary",), hosted=hosted)


def _lane_half(shape):
    return lax.broadcasted_iota(jnp.int32, shape, len(shape) - 1) // D_HEAD


def _band_softmax(q, kk, bias, sink, qs, pad):
    s = _dot_nt(q, kk) * SCALE + bias
    col = lax.broadcasted_iota(jnp.int32, s.shape, 1)
    s = jnp.where(col + qs >= pad, s, NEG_INF)
    m = jnp.max(s, axis=-1, keepdims=True)
    if sink is not None:
        m = jnp.maximum(m, sink)
    p = jnp.exp(s - m)
    den = jnp.sum(p, axis=-1, keepdims=True)
    if sink is not None:
        den = den + jnp.exp(sink - m)
    return p, m, 1.0 / den


def _fill_padded(dst, src, pad):
    dst[0:pad, :] = jnp.zeros((pad,) + dst.shape[1:], dst.dtype)
    dst[pad:, :] = src


FWD_PAIRS = 4
BWD_PAIRS = 2


def _attn_a_fwd(qkv, bias, hosted=()):
    bsz, s_len, _ = qkv.shape
    pad = A_PREV * CHUNK
    band = TQ + pad
    pp = FWD_PAIRS
    w = pp * LANES
    nb = A_WIDTH // w

    def body(q_ref, k_ref, v_ref, b_ref, o_ref, kp, vp):
        i = pl.program_id(2)

        @pl.when(i == 0)
        def _():
            _fill_padded(kp, k_ref[...], pad)
            _fill_padded(vp, v_ref[...], pad)

        qs = pl.multiple_of(i * TQ, TQ)
        half = _lane_half((1, LANES))
        for pr in range(pp):
            sl = slice(pr * LANES, (pr + 1) * LANES)
            kk = kp[pl.ds(qs, band), sl]
            vv = vp[pl.ds(qs, band), sl]
            q = q_ref[:, sl]
            outs = []
            for j in range(2):
                qm = jnp.where(half == j, q, jnp.zeros_like(q))
                p, _, inv = _band_softmax(qm, kk, b_ref[2 * pr + j], None, qs, pad)
                outs.append(_dot_nn(p.astype(BF16), vv) * inv)
            o_ref[:, sl] = jnp.where(half == 0, outs[0], outs[1]).astype(BF16)

    return _call(
        body,
        name="attn_a_fwd",
        grid=(bsz, nb, s_len // TQ),
        in_specs=[pl.BlockSpec((None, TQ, w), lambda b, g, i: (b, i, g)),
                  pl.BlockSpec((None, s_len, w), lambda b, g, i: (b, 0, nb + g)),
                  pl.BlockSpec((None, s_len, w), lambda b, g, i: (b, 0, 2 * nb + g)),
                  pl.BlockSpec((2 * pp, TQ, band), lambda b, g, i: (g, 0, 0))],
        out_specs=[pl.BlockSpec((None, TQ, w), lambda b, g, i: (b, i, g))],
        out_shape=[jax.ShapeDtypeStruct((bsz, s_len, A_WIDTH), BF16)],
        scratch_shapes=[pltpu.VMEM((pad + s_len, w), BF16), pltpu.VMEM((pad + s_len, w), BF16)],
        args=(qkv, qkv, qkv, bias), sem=("arbitrary", "arbitrary", "arbitrary"), hosted=hosted)[0]


def _attn_a_bwd(qkv, bias, do, hosted=()):
    bsz, s_len, _ = qkv.shape
    pad = A_PREV * CHUNK
    band = TQ + pad
    n_i = s_len // TQ
    pp = BWD_PAIRS
    w = pp * LANES
    nb = A_WIDTH // w

    def body(q_ref, k_ref, v_ref, b_ref, do_ref, dq_ref, dk_ref, dv_ref, dbias_ref, kp, vp, dk_acc, dv_acc):
        b = pl.program_id(1)
        i = pl.program_id(2)

        @pl.when(i == 0)
        def _():
            _fill_padded(kp, k_ref[...], pad)
            _fill_padded(vp, v_ref[...], pad)
            dk_acc[...] = jnp.zeros_like(dk_acc)
            dv_acc[...] = jnp.zeros_like(dv_acc)

        @pl.when(jnp.logical_and(b == 0, i == 0))
        def _():
            dbias_ref[...] = jnp.zeros_like(dbias_ref)

        qs = pl.multiple_of(i * TQ, TQ)
        half = _lane_half((1, LANES))
        for pr in range(pp):
            sl = slice(pr * LANES, (pr + 1) * LANES)
            kk = kp[pl.ds(qs, band), sl]
            vv = vp[pl.ds(qs, band), sl]
            q = q_ref[:, sl]
            dd = do_ref[:, sl]
            dqs, dks, dvs = [], [], []
            for j in range(2):
                qm = jnp.where(half == j, q, jnp.zeros_like(q))
                dm = jnp.where(half == j, dd, jnp.zeros_like(dd))
                p, _, inv = _band_softmax(qm, kk, b_ref[2 * pr + j], None, qs, pad)
                pn = p * inv
                dp = _dot_nt(dm, vv)
                delta = jnp.sum(pn * dp, axis=-1, keepdims=True)
                ds = pn * (dp - delta)
                dbias_ref[2 * pr + j] += ds[:, band - REL_COLS:]
                dsb = ds.astype(BF16)
                dqs.append(_dot_nn(dsb, kk))
                dks.append(_dot_tn(dsb, q))
                dvs.append(_dot_tn(pn.astype(BF16), dd))
            dq_ref[:, sl] = (jnp.where(half == 0, dqs[0], dqs[1]) * SCALE).astype(BF16)
            dk_acc[pl.ds(qs, band), sl] += jnp.where(half == 0, dks[0], dks[1]) * SCALE
            dv_acc[pl.ds(qs, band), sl] += jnp.where(half == 0, dvs[0], dvs[1])

        @pl.when(i == n_i - 1)
        def _():
            dk_ref[...] = dk_acc[pad:, :].astype(BF16)
            dv_ref[...] = dv_acc[pad:, :].astype(BF16)

    qspec = pl.BlockSpec((None, TQ, w), lambda g, b, i: (b, i, g))
    kvout = pl.BlockSpec((None, s_len, w), lambda g, b, i: (b, 0, g))
    wide = jax.ShapeDtypeStruct((bsz, s_len, A_WIDTH), BF16)
    return _call(
        body,
        name="attn_a_bwd",
        grid=(nb, bsz, n_i),
        in_specs=[qspec,
                  pl.BlockSpec((None, s_len, w), lambda g, b, i: (b, 0, nb + g)),
                  pl.BlockSpec((None, s_len, w), lambda g, b, i: (b, 0, 2 * nb + g)),
                  pl.BlockSpec((2 * pp, TQ, band), lambda g, b, i: (g, 0, 0)),
                  qspec],
        out_specs=[qspec, kvout, kvout, pl.BlockSpec((2 * pp, TQ, REL_COLS), lambda g, b, i: (g, 0, 0))],
        out_shape=[wide, wide, wide, jax.ShapeDtypeStruct((A_HEADS, TQ, REL_COLS), F32)],
        scratch_shapes=[pltpu.VMEM((pad + s_len, w), BF16), pltpu.VMEM((pad + s_len, w), BF16),
                        pltpu.VMEM((pad + s_len, w), F32), pltpu.VMEM((pad + s_len, w), F32)],
        args=(qkv, qkv, qkv, bias, do), sem=("arbitrary", "arbitrary", "arbitrary"), hosted=hosted)


def _fill_padded_dup(dst, src, pad, h, half):
    other = pltpu.roll(src, D_HEAD, 1)
    _fill_padded(dst, jnp.where(half == h, src, other), pad)


def _attn_b_fwd(qkv, bias, sink):
    bsz, s_len, _ = qkv.shape
    pad = B_PREV * CHUNK
    band = TQ + pad
    kcol = B_Q_WIDTH // LANES
    npair = B_Q_HEADS // 2

    def body(q_ref, k_ref, v_ref, b_ref, s_ref, o_ref, kp, vp):
        i = pl.program_id(1)
        half = _lane_half((1, LANES))

        @pl.when(i == 0)
        def _():
            for h in range(B_KV_HEADS):
                _fill_padded_dup(kp.at[h], k_ref[...], pad, h, half)
                _fill_padded_dup(vp.at[h], v_ref[...], pad, h, half)

        qs = pl.multiple_of(i * TQ, TQ)
        for pr in range(npair):
            h = pr // (B_GROUP // 2)
            sl = slice(pr * LANES, (pr + 1) * LANES)
            kk = kp[h, pl.ds(qs, band), :]
            vv = vp[h, pl.ds(qs, band), :]
            q = q_ref[:, sl]
            outs = []
            for j in range(2):
                qm = jnp.where(half == j, q, jnp.zeros_like(q))
                p, _, inv = _band_softmax(qm, kk, b_ref[2 * pr + j], s_ref[2 * pr + j][0:1, 0:1], qs, pad)
                outs.append(_dot_nn(p.astype(BF16), vv) * inv)
            o_ref[:, sl] = jnp.where(half == 0, outs[0], outs[1]).astype(BF16)

    return pl.pallas_call(
        body,
        name="attn_b_fwd",
        grid=(bsz, s_len // TQ),
        in_specs=[pl.BlockSpec((None, TQ, B_Q_WIDTH), lambda b, i: (b, i, 0)),
                  pl.BlockSpec((None, s_len, LANES), lambda b, i: (b, 0, kcol)),
                  pl.BlockSpec((None, s_len, LANES), lambda b, i: (b, 0, kcol + 1)),
                  pl.BlockSpec((B_Q_HEADS, TQ, band), lambda b, i: (0, 0, 0)),
                  pl.BlockSpec((B_Q_HEADS, 8, LANES), lambda b, i: (0, 0, 0))],
        out_specs=pl.BlockSpec((None, TQ, B_Q_WIDTH), lambda b, i: (b, i, 0)),
        out_shape=jax.ShapeDtypeStruct((bsz, s_len, B_Q_WIDTH), BF16),
        scratch_shapes=[pltpu.VMEM((B_KV_HEADS, pad + s_len, LANES), BF16),
                        pltpu.VMEM((B_KV_HEADS, pad + s_len, LANES), BF16)],
        compiler_params=_cparams(("arbitrary", "arbitrary")),
    )(qkv, qkv, qkv, bias, sink)


def _attn_b_bwd(qkv, bias, sink, do, hosted=()):
    bsz, s_len, _ = qkv.shape
    pad = B_PREV * CHUNK
    band = TQ + pad
    kcol = B_Q_WIDTH // LANES
    npair = B_Q_HEADS // 2
    n_i = s_len // TQ

    pp = B_GROUP // 2
    w = pp * LANES

    def body(q_ref, k_ref, v_ref, b_ref, s_ref, do_ref, dq_ref, dkv_ref, dsink_ref, kp, vp, dk_acc, dv_acc):
        b = pl.program_id(0)
        h = pl.program_id(1)
        i = pl.program_id(2)
        half = _lane_half((1, LANES))

        @pl.when(i == 0)
        def _():
            _fill_padded_dup(kp, k_ref[...], pad, h, half)
            _fill_padded_dup(vp, v_ref[...], pad, h, half)

        @pl.when(jnp.logical_and(h == 0, i == 0))
        def _():
            dk_acc[...] = jnp.zeros_like(dk_acc)
            dv_acc[...] = jnp.zeros_like(dv_acc)

        @pl.when(jnp.logical_and(b == 0, jnp.logical_and(h == 0, i == 0)))
        def _():
            dsink_ref[...] = jnp.zeros_like(dsink_ref)

        qs = pl.multiple_of(i * TQ, TQ)
        kk = kp[pl.ds(qs, band), :]
        vv = vp[pl.ds(qs, band), :]
        dk2 = jnp.zeros((band, LANES), F32)
        dv2 = jnp.zeros((band, LANES), F32)
        for pr in range(pp):
            sl = slice(pr * LANES, (pr + 1) * LANES)
            q = q_ref[:, sl]
            dd = do_ref[:, sl]
            dqs, dks, dvs = [], [], []
            for j in range(2):
                qm = jnp.where(half == j, q, jnp.zeros_like(q))
                dm = jnp.where(half == j, dd, jnp.zeros_like(dd))
                sink = s_ref[2 * pr + j][0:1, 0:1]
                p, m, inv = _band_softmax(qm, kk, b_ref[2 * pr + j], sink, qs, pad)
                pn = p * inv
                dp = _dot_nt(dm, vv)
                delta = jnp.sum(pn * dp, axis=-1, keepdims=True)
                ds = pn * (dp - delta)
                dsb = ds.astype(BF16)
                dqs.append(_dot_nn(dsb, kk))
                dks.append(_dot_tn(dsb, q))
                dvs.append(_dot_tn(pn.astype(BF16), dd))
                dsk = jnp.sum(-(jnp.exp(sink - m) * inv) * delta, axis=0, keepdims=True)
                dsink_ref[2 * pp * h + 2 * pr + j] += jnp.broadcast_to(dsk, (8, LANES))
            dq_ref[:, sl] = (jnp.where(half == 0, dqs[0], dqs[1]) * SCALE).astype(BF16)
            dk2 = dk2 + jnp.where(half == 0, dks[0], dks[1])
            dv2 = dv2 + jnp.where(half == 0, dvs[0], dvs[1])
        dk_acc[pl.ds(qs, band), :] += jnp.where(half == h, (dk2 + pltpu.roll(dk2, D_HEAD, 1)) * SCALE, 0.0)
        dv_acc[pl.ds(qs, band), :] += jnp.where(half == h, dv2 + pltpu.roll(dv2, D_HEAD, 1), 0.0)

        @pl.when(jnp.logical_and(h == B_KV_HEADS - 1, i == n_i - 1))
        def _():
            dkv_ref[:, 0:LANES] = dk_acc[pad:, :].astype(BF16)
            dkv_ref[:, LANES:2 * LANES] = dv_acc[pad:, :].astype(BF16)

    qspec = pl.BlockSpec((None, TQ, w), lambda b, h, i: (b, i, h))
    return _call(
        body,
        name="attn_b_bwd",
        grid=(bsz, B_KV_HEADS, n_i),
        in_specs=[qspec,
                  pl.BlockSpec((None, s_len, LANES), lambda b, h, i: (b, 0, kcol)),
                  pl.BlockSpec((None, s_len, LANES), lambda b, h, i: (b, 0, kcol + 1)),
                  pl.BlockSpec((2 * pp, TQ, band), lambda b, h, i: (h, 0, 0)),
                  pl.BlockSpec((2 * pp, 8, LANES), lambda b, h, i: (h, 0, 0)),
                  qspec],
        out_specs=[qspec, pl.BlockSpec((None, s_len, 2 * LANES), lambda b, h, i: (b, 0, 0)),
                   pl.BlockSpec((B_Q_HEADS, 8, LANES), lambda b, h, i: (0, 0, 0))],
        out_shape=[jax.ShapeDtypeStruct((bsz, s_len, B_Q_WIDTH), BF16),
                   jax.ShapeDtypeStruct((bsz, s_len, 2 * B_KV_WIDTH), BF16),
                   jax.ShapeDtypeStruct((B_Q_HEADS, 8, LANES), F32)],
        scratch_shapes=[pltpu.VMEM((pad + s_len, LANES), BF16), pltpu.VMEM((pad + s_len, LANES), BF16),
                        pltpu.VMEM((pad + s_len, LANES), F32), pltpu.VMEM((pad + s_len, LANES), F32)],
        args=(qkv, qkv, qkv, bias, sink, do), sem=("arbitrary", "arbitrary", "arbitrary"), hosted=hosted)


REL_COLS = 3 * 128
REL_WRAP = 512


def _bias_a_build(tv):
    h = tv.shape[0]
    pad = A_PREV * CHUNK
    band = TQ + pad

    def body(tv_ref, o_ref):
        row = tv_ref[...]
        x = jnp.broadcast_to(row, (TQ, REL_WRAP))
        r = lax.broadcasted_iota(jnp.int32, x.shape, 0)
        for bit in range(8):
            sh = 1 << bit
            x = jnp.where((r & sh) != 0, pltpu.roll(x, sh, 1), x)
        far = jnp.broadcast_to(row[:, 0:1], (TQ, band - REL_COLS))
        full = jnp.concatenate([far, x[:, REL_WRAP // 2:REL_WRAP], x[:, 0:REL_COLS - REL_WRAP // 2]], axis=1)
        qc = (lax.broadcasted_iota(jnp.int32, full.shape, 0) + pad) // CHUNK
        kc = lax.broadcasted_iota(jnp.int32, full.shape, 1) // CHUNK
        ok = jnp.logical_and(kc <= qc, kc >= qc - A_PREV)
        o_ref[...] = jnp.where(ok, full, NEG_INF)

    return pl.pallas_call(
        body,
        name="bias_a_build",
        grid=(h,),
        in_specs=[pl.BlockSpec((None, 1, REL_WRAP), lambda hh: (hh, 0, 0))],
        out_specs=pl.BlockSpec((None, TQ, band), lambda hh: (hh, 0, 0)),
        out_shape=jax.ShapeDtypeStruct((h, TQ, band), F32),
        compiler_params=_cparams(("parallel",)),
    )(tv)


def _relbias_grad(dbias):
    h, rows, _ = dbias.shape

    def body(d_ref, o_ref):
        x = d_ref[...]
        r = lax.broadcasted_iota(jnp.int32, x.shape, 0)
        c = lax.broadcasted_iota(jnp.int32, x.shape, 1) - r
        x = jnp.where(jnp.logical_and(c >= 1, c < REL_TABLE), x, 0.0)
        for bit in range(8):
            sh = 1 << bit
            x = jnp.where((r & sh) != 0, pltpu.roll(x, REL_COLS - sh, 1), x)
        diag = jnp.sum(x, axis=0, keepdims=True)
        lane = lax.broadcasted_iota(jnp.int32, diag.shape, 1)
        diag = jnp.where(jnp.logical_and(lane >= 1, lane < REL_TABLE), diag, 0.0)
        rest = -jnp.sum(diag, axis=1, keepdims=True)
        o_ref[...] = jnp.broadcast_to(jnp.where(lane == 0, rest, diag), o_ref.shape)

    return pl.pallas_call(
        body,
        name="relbias_grad",
        grid=(h,),
        in_specs=[pl.BlockSpec((None, rows, REL_COLS), lambda hh: (hh, 0, 0))],
        out_specs=pl.BlockSpec((None, 8, REL_COLS), lambda hh: (hh, 0, 0)),
        out_shape=jax.ShapeDtypeStruct((h, 8, REL_COLS), F32),
        compiler_params=_cparams(("parallel",)),
    )(dbias)


def _mix_out_fwd(x, oa, ob, gates, proj_t, wout):
    t = x.shape[0]

    def body(x_ref, oa_ref, ob_ref, gt_ref, pt_ref, wo_ref, y_ref, ya_ref, yb_ref, mg_ref):
        ya = _dot_nt(oa_ref[...], pt_ref[:, 0:A_WIDTH])
        yb = _dot_nt(ob_ref[...], pt_ref[:, A_WIDTH:A_WIDTH + B_Q_WIDTH])
        ya_ref[...] = ya.astype(BF16)
        yb_ref[...] = yb.astype(BF16)
        mg = jax.nn.sigmoid(gt_ref[:, 0:D_MODEL]) * ya + jax.nn.sigmoid(gt_ref[:, D_MODEL:2 * D_MODEL]) * yb
        mgb = mg.astype(BF16)
        mg_ref[...] = mgb
        y_ref[...] = x_ref[...] + _dot_nn(mgb, wo_ref[...])

    return pl.pallas_call(
        body,
        name="mix_out_fwd",
        grid=(t // TM,),
        in_specs=[_rows(TM, D_MODEL), _rows(TM, A_WIDTH), _rows(TM, B_Q_WIDTH), _rows(TM, 2 * D_MODEL),
                  _resident((D_MODEL, A_WIDTH + B_Q_WIDTH)), _resident((D_MODEL, D_MODEL))],
        out_specs=[_rows(TM, D_MODEL), _rows(TM, D_MODEL), _rows(TM, D_MODEL), _rows(TM, D_MODEL)],
        out_shape=[jax.ShapeDtypeStruct((t, D_MODEL), F32), jax.ShapeDtypeStruct((t, D_MODEL), BF16),
                   jax.ShapeDtypeStruct((t, D_MODEL), BF16), jax.ShapeDtypeStruct((t, D_MODEL), BF16)],
        compiler_params=_cparams(("parallel",)),
    )(x, oa, ob, gates, proj_t, wout)


def _mix_out_bwd(d, gates, ya, yb, proj_t, wout, hosted=()):
    t = d.shape[0]

    def body(d_ref, gt_ref, ya_ref, yb_ref, pt_ref, wo_ref, db_ref, dya_ref, dyb_ref, doa_ref, dob_ref, dgt_ref):
        db = d_ref[...].astype(BF16)
        db_ref[...] = db
        dmg = _dot_nt(db, wo_ref[...])
        sa = jax.nn.sigmoid(gt_ref[:, 0:D_MODEL])
        sb = jax.nn.sigmoid(gt_ref[:, D_MODEL:2 * D_MODEL])
        dya = (dmg * sa).astype(BF16)
        dyb = (dmg * sb).astype(BF16)
        dya_ref[...] = dya
        dyb_ref[...] = dyb
        dgt_ref[:, 0:D_MODEL] = (dmg * ya_ref[...].astype(F32) * (sa * (1.0 - sa))).astype(BF16)
        dgt_ref[:, D_MODEL:2 * D_MODEL] = (dmg * yb_ref[...].astype(F32) * (sb * (1.0 - sb))).astype(BF16)
        doa_ref[...] = _dot_nn(dya, pt_ref[:, 0:A_WIDTH]).astype(BF16)
        dob_ref[...] = _dot_nn(dyb, pt_ref[:, A_WIDTH:A_WIDTH + B_Q_WIDTH]).astype(BF16)

    return _call(
        body,
        name="mix_out_bwd",
        grid=(t // TM,),
        in_specs=[_rows(TM, D_MODEL), _rows(TM, 2 * D_MODEL), _rows(TM, D_MODEL), _rows(TM, D_MODEL),
                  _resident((D_MODEL, A_WIDTH + B_Q_WIDTH)), _resident((D_MODEL, D_MODEL))],
        out_specs=[_rows(TM, D_MODEL), _rows(TM, D_MODEL), _rows(TM, D_MODEL), _rows(TM, A_WIDTH),
                   _rows(TM, B_Q_WIDTH), _rows(TM, 2 * D_MODEL)],
        out_shape=[jax.ShapeDtypeStruct((t, D_MODEL), BF16), jax.ShapeDtypeStruct((t, D_MODEL), BF16),
                   jax.ShapeDtypeStruct((t, D_MODEL), BF16), jax.ShapeDtypeStruct((t, A_WIDTH), BF16),
                   jax.ShapeDtypeStruct((t, B_Q_WIDTH), BF16), jax.ShapeDtypeStruct((t, 2 * D_MODEL), BF16)],
        args=(d, gates, ya, yb, proj_t, wout), sem=("parallel",), hosted=hosted)


def _loss_head(x, gamma, target):
    t = x.shape[0]

    def body(x_ref, gam_ref, t_ref, dx_ref, dgam_ref, loss_ref):
        xh, r = _rms(x_ref[...])
        gam = gam_ref[...]
        e = xh * gam - t_ref[...]
        dy = e * (1.0 / D_MODEL)
        dxn, dgam = _rms_bwd(dy, xh, r, gam)
        dx_ref[...] = dxn

        @pl.when(pl.program_id(0) == 0)
        def _():
            dgam_ref[...] = jnp.zeros_like(dgam_ref)
            loss_ref[...] = jnp.zeros_like(loss_ref)

        dgam_ref[...] += dgam
        loss_ref[...] += _colsum8(e * e) * (0.5 / D_MODEL)

    return pl.pallas_call(
        body,
        name="loss_head",
        grid=(t // TM,),
        in_specs=[_rows(TM, D_MODEL), _resident((1, D_MODEL)), _rows(TM, D_MODEL)],
        out_specs=[_rows(TM, D_MODEL), pl.BlockSpec((8, D_MODEL), lambda i: (0, 0)),
                   pl.BlockSpec((8, D_MODEL), lambda i: (0, 0))],
        out_shape=[jax.ShapeDtypeStruct((t, D_MODEL), F32), jax.ShapeDtypeStruct((8, D_MODEL), F32),
                   jax.ShapeDtypeStruct((8, D_MODEL), F32)],
        compiler_params=_cparams(("arbitrary",)),
    )(x, gamma, target)


def _place():
    x, y, c = lax.axis_index("x"), lax.axis_index("y"), lax.axis_index("c")
    chips = [(1 - x, y), (x, 1 - y), (1 - x, 1 - y)]
    return x, y, c, chips


class _Gather:
    per = 7

    def __init__(self, shards):
        n = len(shards)
        self.inputs = list(shards)
        self.out_shape = [jax.ShapeDtypeStruct((N_DEV * s.shape[0], s.shape[1]), s.dtype) for s in shards]
        self.scratch = [pltpu.SemaphoreType.DMA((n * self.per,)), pltpu.SemaphoreType.DMA((n * self.per,)),
                        pltpu.SemaphoreType.DMA((n,))]
        self.result = None

    def _parts(self, ins, outs, sems):
        send_sems, recv_sems, local_sems = sems
        x, y, c, chips = _place()
        me, sibling = (x, y, c), (x, y, 1 - c)
        n = len(ins)

        def rows(k, p):
            r = ins[k].shape[0]
            return outs[k].at[pl.ds((4 * p[0] + 2 * p[1] + p[2]) * r, r), :]

        def copy(k, slot, block, to, src=None):
            return pltpu.make_async_remote_copy(
                src_ref=rows(k, block) if src is None else src, dst_ref=rows(k, block),
                send_sem=send_sems.at[k * self.per + slot], recv_sem=recv_sems.at[k * self.per + slot],
                device_id=to, device_id_type=MESH)

        mine = [pltpu.make_async_copy(ins[k], rows(k, me), local_sems.at[k]) for k in range(n)]
        first = []
        for k in range(n):
            first.append(copy(k, 0, me, sibling, src=ins[k]))
            first += [copy(k, 1 + j, me, (*chip, c), src=ins[k]) for j, chip in enumerate(chips)]
        passed = [copy(k, 4 + j, (*chip, c), sibling) for j, chip in enumerate(chips) for k in range(n)]
        return n, c, me, sibling, chips, copy, mine, first, passed

    def start(self, ins, outs, sems):
        _, _, _, _, _, _, mine, first, _ = self._parts(ins, outs, sems)
        for cp in mine + first:
            cp.start()

    def forward(self, ins, outs, sems):
        n, c, me, _, chips, copy, _, _, passed = self._parts(ins, outs, sems)
        for j, chip in enumerate(chips):
            for k in range(n):
                copy(k, 1 + j, (*chip, c), me).wait_recv()
                passed[j * n + k].start()

    def finish(self, ins, outs, sems):
        n, c, me, sibling, chips, copy, mine, first, passed = self._parts(ins, outs, sems)
        for k in range(n):
            copy(k, 0, sibling, me).wait_recv()
            for j, chip in enumerate(chips):
                copy(k, 4 + j, (*chip, 1 - c), me).wait_recv()
        for cp in first + passed:
            cp.wait_send()
        for cp in mine:
            cp.wait()


class _PairExchange:
    def __init__(self, grads):
        n = len(grads)
        self.inputs = list(grads)
        self.out_shape = [jax.ShapeDtypeStruct((g.shape[0] // 2, g.shape[1]), g.dtype) for g in grads]
        self.scratch = [pltpu.SemaphoreType.DMA((n * N_CHIP,)), pltpu.SemaphoreType.DMA((n * N_CHIP,))]
        self.result = None

    def _copies(self, ins, outs, sems):
        send_sems, recv_sems = sems
        x, y, c, _ = _place()
        copies = []
        for k in range(len(ins)):
            r = ins[k].shape[0] // N_DEV
            for q in range(N_CHIP):
                copies.append(pltpu.make_async_remote_copy(
                    src_ref=ins[k].at[pl.ds((2 * q + 1 - c) * r, r), :], dst_ref=outs[k].at[pl.ds(q * r, r), :],
                    send_sem=send_sems.at[k * N_CHIP + q], recv_sem=recv_sems.at[k * N_CHIP + q],
                    device_id=(x, y, 1 - c), device_id_type=MESH))
        return copies

    def start(self, ins, outs, sems):
        for cp in self._copies(ins, outs, sems):
            cp.start()

    def forward(self, ins, outs, sems):
        pass

    def finish(self, ins, outs, sems):
        copies = self._copies(ins, outs, sems)
        for cp in copies:
            cp.wait_recv()
        for cp in copies:
            cp.wait_send()


class _ChipExchange(_PairExchange):
    def __init__(self, psums):
        n = len(psums)
        self.inputs = list(psums)
        self.out_shape = [jax.ShapeDtypeStruct((3 * p.shape[0] // N_CHIP, p.shape[1]), p.dtype) for p in psums]
        self.scratch = [pltpu.SemaphoreType.DMA((n * 3,)), pltpu.SemaphoreType.DMA((n * 3,))]
        self.result = None

    def _copies(self, ins, outs, sems):
        send_sems, recv_sems = sems
        _, _, c, chips = _place()
        copies = []
        for k in range(len(ins)):
            r = ins[k].shape[0] // N_CHIP
            for j, chip in enumerate(chips):
                copies.append(pltpu.make_async_remote_copy(
                    src_ref=ins[k].at[pl.ds((2 * chip[0] + chip[1]) * r, r), :], dst_ref=outs[k].at[pl.ds(j * r, r), :],
                    send_sem=send_sems.at[k * 3 + j], recv_sem=recv_sems.at[k * 3 + j],
                    device_id=(*chip, c), device_id_type=MESH))
        return copies


def _exchange_alone(xchg, name):
    n_in, n_out = len(xchg.inputs), len(xchg.out_shape)

    def body(*refs):
        ins, outs, sems = refs[:n_in], refs[n_in:n_in + n_out], refs[n_in + n_out:]
        xchg.start(ins, outs, sems)
        xchg.forward(ins, outs, sems)
        xchg.finish(ins, outs, sems)

    xchg.result = list(pl.pallas_call(
        body, name=name, in_specs=[_hbm()] * n_in, out_specs=[_hbm()] * n_out, out_shape=xchg.out_shape,
        scratch_shapes=xchg.scratch)(*xchg.inputs))
    return xchg.result


def _pair_sum(core, grads, recvd, name):
    n = len(grads)
    r = grads[0].shape[0] // N_DEV
    cdim = grads[0].shape[1]
    tr = r // 2 if r % 32 == 0 else r
    nt = r // tr

    def body(core_ref, *refs):
        del core_ref
        for k in range(n):
            refs[2 * n + k][...] = (refs[k][...].astype(F32) + refs[n + k][...].astype(F32)).astype(BF16)

    gspec = pl.BlockSpec((tr, cdim), lambda q, i, core_ref: ((2 * q + core_ref[0]) * nt + i, 0))
    rspec = pl.BlockSpec((tr, cdim), lambda q, i, core_ref: (q * nt + i, 0))
    return pl.pallas_call(
        body,
        name=name,
        grid_spec=pltpu.PrefetchScalarGridSpec(
            num_scalar_prefetch=1, grid=(N_CHIP, nt), in_specs=[gspec] * n + [rspec] * n, out_specs=[rspec] * n),
        out_shape=[jax.ShapeDtypeStruct((N_CHIP * r, cdim), BF16) for _ in range(n)],
        compiler_params=_cparams(("parallel", "parallel")),
    )(core, *grads, *recvd)


def _final_sum(chip, psums, recvd, name):
    n = len(psums)
    r = psums[0].shape[0] // N_CHIP
    cdim = psums[0].shape[1]
    tr = r // 2 if r % 32 == 0 else r
    nt = r // tr

    def body(chip_ref, *refs):
        del chip_ref
        for k in range(n):
            got = refs[n + k]
            tot = refs[k][...].astype(F32) + got[0].astype(F32)
            tot = tot + got[1].astype(F32)
            tot = tot + got[2].astype(F32)
            refs[2 * n + k][...] = tot

    pspec = pl.BlockSpec((tr, cdim), lambda i, chip_ref: (chip_ref[0] * nt + i, 0))
    rspec = pl.BlockSpec((3, tr, cdim), lambda i, chip_ref: (0, i, 0))
    ospec = pl.BlockSpec((tr, cdim), lambda i, chip_ref: (i, 0))
    return pl.pallas_call(
        body,
        name=name,
        grid_spec=pltpu.PrefetchScalarGridSpec(
            num_scalar_prefetch=1, grid=(nt,), in_specs=[pspec] * n + [rspec] * n, out_specs=[ospec] * n),
        out_shape=[jax.ShapeDtypeStruct((r, cdim), F32) for _ in range(n)],
        compiler_params=_cparams(("parallel",)),
    )(chip, *psums, *[g.reshape(3, r, cdim) for g in recvd])


SMALL_ROWS = 16


def _all_reduce_small(part):
    def body(p_ref, o_ref, buf, send_sems, recv_sems):
        x, y, c, _ = _place()
        me = 4 * x + 2 * y + c
        buf[me] = p_ref[...]
        copies = []
        for d in range(1, N_DEV):
            peer = me ^ d
            copies.append(pltpu.make_async_remote_copy(
                src_ref=p_ref, dst_ref=buf.at[me], send_sem=send_sems.at[d - 1], recv_sem=recv_sems.at[d - 1],
                device_id=(peer // 4, (peer // 2) % 2, peer % 2), device_id_type=MESH))
        for cp in copies:
            cp.start()
        for cp in copies:
            cp.wait_recv()
        for cp in copies:
            cp.wait_send()
        tot = buf[0]
        for d in range(1, N_DEV):
            tot = tot + buf[d]
        o_ref[...] = tot

    return pl.pallas_call(
        body,
        name="all_reduce_small",
        in_specs=[pl.BlockSpec(memory_space=pltpu.VMEM)],
        out_specs=pl.BlockSpec(memory_space=pltpu.VMEM),
        out_shape=jax.ShapeDtypeStruct(part.shape, F32),
        scratch_shapes=[pltpu.VMEM((N_DEV,) + part.shape, F32), pltpu.SemaphoreType.DMA((N_DEV - 1,)),
                        pltpu.SemaphoreType.DMA((N_DEV - 1,))],
    )(part)


ADAMW_STEPS = 4


def _adamw(ws, gs, ms, vs, name, hosted=()):
    n = len(ws)
    steps = ADAMW_STEPS if all(w.shape[0] % (8 * ADAMW_STEPS) == 0 for w in ws) else 1
    c1 = 1.0 - ADAM_B1 ** ADAM_STEP
    c2 = 1.0 - ADAM_B2 ** ADAM_STEP

    def body(*refs):
        for k in range(n):
            w, g, m, v = (refs[j * n + k][...] for j in range(4))
            m2 = ADAM_B1 * m + (1.0 - ADAM_B1) * g
            v2 = ADAM_B2 * v + (1.0 - ADAM_B2) * (g * g)
            delta = -ADAM_LR * ((m2 / c1) / (jnp.sqrt(v2 / c2) + ADAM_EPS) + ADAM_WD * w)
            refs[4 * n + k][...] = delta
            refs[5 * n + k][...] = m2
            refs[6 * n + k][...] = v2

    specs = [pl.BlockSpec((w.shape[0] // steps, w.shape[1]), lambda i: (i, 0)) for w in ws]
    shapes = [jax.ShapeDtypeStruct(w.shape, F32) for w in ws]
    outs = _call(
        body,
        name=name,
        grid=(steps,),
        in_specs=specs * 4,
        out_specs=specs * 3,
        out_shape=shapes * 3,
        args=(*ws, *gs, *ms, *vs), sem=("parallel",), hosted=hosted)
    return outs[:n], outs[n:2 * n], outs[2 * n:]


def _bias_b():
    pad = B_PREV * CHUNK
    slopes = np.array([2.0 ** (-8.0 * (i + 1) / B_Q_HEADS) for i in range(B_Q_HEADS)], dtype=np.float32)
    dist = np.abs(np.arange(TQ)[:, None] - np.arange(TQ + pad)[None, :] + pad).astype(np.float32)
    bias = -slopes.reshape(B_Q_HEADS, 1, 1) * dist[None]
    qc = (np.arange(TQ)[:, None] + pad) // CHUNK
    kc = np.arange(TQ + pad)[None, :] // CHUNK
    allowed = (kc <= qc) & (kc >= qc - B_PREV)
    return np.where(allowed[None], bias, np.float32(NEG_INF)).astype(np.float32)


def kernel(x, ffn1_norm, ffn1_w_gate, ffn1_w_up, ffn1_w_down, mix_norm, w_in, rel_bias, sinks, w_proj_a, w_proj_b, w_out, ffn2_norm, ffn2_w_gate, ffn2_w_up, ffn2_w_down, final_norm, loss_target, m_ffn1_norm, m_ffn1_w_gate, m_ffn1_w_up, m_ffn1_w_down, m_mix_norm, m_w_in, m_rel_bias, m_sinks, m_w_proj_a, m_w_proj_b, m_w_out, m_ffn2_norm, m_ffn2_w_gate, m_ffn2_w_up, m_ffn2_w_down, m_final_norm, v_ffn1_norm, v_ffn1_w_gate, v_ffn1_w_up, v_ffn1_w_down, v_mix_norm, v_w_in, v_rel_bias, v_sinks, v_w_proj_a, v_w_proj_b, v_w_out, v_ffn2_norm, v_ffn2_w_gate, v_ffn2_w_up, v_ffn2_w_down, v_final_norm):
    bsz, s_len, _ = x.shape
    t = bsz * s_len
    core = lax.axis_index("c").astype(jnp.int32).reshape(1)
    chip = (2 * lax.axis_index("x") + lax.axis_index("y")).astype(jnp.int32).reshape(1)

    def row_form(w):
        return w.astype(BF16).T

    wg1, wu1, wd1 = _exchange_alone(
        _Gather([row_form(ffn1_w_gate), row_form(ffn1_w_up), ffn1_w_down.astype(BF16)]), "gather_ffn1")
    gather_mix = _Gather([row_form(w_in), jnp.concatenate([row_form(w_proj_a), row_form(w_proj_b)], axis=1),
                          w_out.astype(BF16)])
    gather_ffn2_gate = _Gather([row_form(ffn2_w_gate)])
    gather_ffn2_rest = _Gather([row_form(ffn2_w_up), ffn2_w_down.astype(BF16)])

    x0 = x.reshape(t, D_MODEL)
    tgt = loss_target.reshape(t, D_MODEL)
    gam1, gam2, gam3, gam4 = (g.reshape(1, D_MODEL) for g in (ffn1_norm, mix_norm, ffn2_norm, final_norm))

    h1, g1, u1, a1, x1 = _ffn_fwd(x0, gam1, wg1, wu1, wd1, "ffn1_fwd", hosted=[gather_mix])
    win_t, proj_t, wout = gather_mix.result
    h2, qkv_a, qkv_b, gates = _proj_fwd(x1, gam2, win_t, hosted=[gather_ffn2_gate])
    (wg2,) = gather_ffn2_gate.result
    qkv_a3 = qkv_a.reshape(bsz, s_len, QKV_A)
    qkv_b3 = qkv_b.reshape(bsz, s_len, QKV_B)

    far = jnp.broadcast_to(rel_bias[:, REL_TABLE - 1:REL_TABLE], (A_HEADS, REL_WRAP // 2))
    tv = jnp.concatenate([far, jnp.flip(rel_bias, axis=1), jnp.zeros((A_HEADS, REL_WRAP // 2 - REL_TABLE), F32)], axis=1)
    bias_a = _bias_a_build(tv.reshape(A_HEADS, 1, REL_WRAP))
    bias_b = jnp.asarray(_bias_b())
    sink_rows = jnp.broadcast_to(sinks.reshape(B_Q_HEADS, 1, 1), (B_Q_HEADS, 8, LANES))

    oa = _attn_a_fwd(qkv_a3, bias_a, hosted=[gather_ffn2_rest]).reshape(t, A_WIDTH)
    wu2, wd2 = gather_ffn2_rest.result
    ob = _attn_b_fwd(qkv_b3, bias_b, sink_rows).reshape(t, B_Q_WIDTH)
    x2, ya, yb, mg = _mix_out_fwd(x1, oa, ob, gates, proj_t, wout)
    h3, g2, u2, a2, x3 = _ffn_fwd(x2, gam3, wg2, wu2, wd2, "ffn2_fwd")

    dx3, dgam4, loss_part = _loss_head(x3, gam4, tgt)

    dx2, dg2, du2, db2, dgam3 = _ffn_bwd(dx3, x2, gam3, g2, u2, wg2, wu2, wd2, "ffn2_bwd")
    gw_ffn2 = [_mm_tn([dg2], h3, "grad_ffn2_gate"), _mm_tn([du2], h3, "grad_ffn2_up"),
               _mm_tn([a2], db2, "grad_ffn2_down")]
    pairx_ffn2 = _PairExchange(gw_ffn2)
    dxb, dya, dyb, doa, dob, dgates = _mix_out_bwd(dx2, gates, ya, yb, proj_t, wout, hosted=[pairx_ffn2])
    psum_ffn2 = _pair_sum(core, gw_ffn2, pairx_ffn2.result, "pair_sum_ffn2")
    gw_out = _mm_tn([mg], dxb, "grad_w_out")
    gw_proj = _mm_tn_proj(dya, dyb, oa, ob)

    chipx_ffn2 = _ChipExchange(psum_ffn2)
    dqa, dka, dva, dbias_a = _attn_a_bwd(qkv_a3, bias_a, doa.reshape(bsz, s_len, A_WIDTH), hosted=[chipx_ffn2])
    pairx_out = _PairExchange([gw_proj, gw_out])
    dqb, dkvb, dsink = _attn_b_bwd(qkv_b3, bias_b, sink_rows, dob.reshape(bsz, s_len, B_Q_WIDTH), hosted=[pairx_out])
    drel_lanes = _relbias_grad(dbias_a)
    dproj = [dqa.reshape(t, A_WIDTH), dka.reshape(t, A_WIDTH), dva.reshape(t, A_WIDTH), dqb.reshape(t, B_Q_WIDTH),
             dkvb.reshape(t, 2 * B_KV_WIDTH), dgates]

    gw_in = _mm_tn(dproj, h2, "grad_w_in")
    pairx_in = _PairExchange([gw_in])
    psum_out = _pair_sum(core, [gw_proj, gw_out], pairx_out.result, "pair_sum_mix")
    chipx_out = _ChipExchange(psum_out)
    dx1, db1, dgam2 = _proj_bwd(dx2, x1, gam2, dproj, win_t, hosted=[pairx_in, chipx_out])
    psum_in = _pair_sum(core, [gw_in], pairx_in.result, "pair_sum_w_in")
    gw_d1 = _mm_tn([a1], db1, "grad_ffn1_down")

    chipx_in = _ChipExchange(psum_in)
    pairx_d1 = _PairExchange([gw_d1])
    dg1, du1 = _ffn_bwd_act(dx1, g1, u1, wd1, "ffn1_bwd_act", hosted=[chipx_in, pairx_d1])
    psum_d1 = _pair_sum(core, [gw_d1], pairx_d1.result, "pair_sum_ffn1_down")
    chipx_d1 = _ChipExchange(psum_d1)
    gw_g1 = _mm_tn([dg1], h1, "grad_ffn1_gate", hosted=[chipx_d1])
    from_sibling_g1 = _exchange_alone(_PairExchange([gw_g1]), "pair_exchange_ffn1_gate")
    psum_g1 = _pair_sum(core, [gw_g1], from_sibling_g1, "pair_sum_ffn1_gate")
    chipx_g1 = _ChipExchange(psum_g1)
    gw_u1 = _mm_tn([du1], h1, "grad_ffn1_up", hosted=[chipx_g1])
    from_sibling_u1 = _exchange_alone(_PairExchange([gw_u1]), "pair_exchange_ffn1_up")
    psum_u1 = _pair_sum(core, [gw_u1], from_sibling_u1, "pair_sum_ffn1_up")
    chipx_u1 = _ChipExchange(psum_u1)
    dx0, dgam1 = _ffn_bwd_in(dx1, x0, gam1, dg1, du1, wg1, wu1, "ffn1_bwd_in", hosted=[chipx_u1])

    g_g1, g_u1, g_d1, g_g2, g_u2, g_d2 = _final_sum(
        chip, psum_g1 + psum_u1 + psum_d1 + psum_ffn2,
        chipx_g1.result + chipx_u1.result + chipx_d1.result + chipx_ffn2.result, "grad_sum_ffn")
    (g_in,) = _final_sum(chip, psum_in, chipx_in.result, "grad_sum_w_in")
    g_proj, g_out = _final_sum(chip, psum_out, chipx_out.result, "grad_sum_mix")
    grads = {
        "ffn1_w_gate": g_g1.T, "ffn1_w_up": g_u1.T, "ffn1_w_down": g_d1, "w_in": g_in.T,
        "w_proj_a": g_proj[:, 0:A_WIDTH].T, "w_proj_b": g_proj[:, A_WIDTH:].T, "w_out": g_out,
        "ffn2_w_gate": g_g2.T, "ffn2_w_up": g_u2.T, "ffn2_w_down": g_d2,
    }

    def row_of(v):
        return jnp.pad(v.reshape(1, -1), ((0, 0), (0, D_MODEL - v.size)))

    def table_rows(v):
        return jnp.pad(v, ((0, 0), (0, D_MODEL - REL_TABLE)))

    drel_local = jnp.flip(drel_lanes[:, 0, 0:REL_TABLE], axis=1)
    small_part = jnp.concatenate(
        [jnp.sum(dgam1, axis=0, keepdims=True), jnp.sum(dgam2, axis=0, keepdims=True),
         jnp.sum(dgam3, axis=0, keepdims=True), jnp.sum(dgam4, axis=0, keepdims=True),
         row_of(jnp.sum(loss_part)), row_of(dsink[:, 0, 0]), jnp.zeros((2, D_MODEL), F32),
         table_rows(drel_local)], axis=0)
    small = _all_reduce_small(small_part)
    loss = small[4, 0]

    def pack(n1, n2, n3, n4, sk, tb):
        return jnp.concatenate([n1.reshape(1, -1), n2.reshape(1, -1), n3.reshape(1, -1), n4.reshape(1, -1),
                                jnp.zeros((1, D_MODEL), F32), row_of(sk), jnp.zeros((2, D_MODEL), F32), table_rows(tb)],
                               axis=0)

    live = np.zeros((SMALL_ROWS, D_MODEL), np.float32)
    live[0:4] = 1.0
    live[5, 0:B_Q_HEADS] = 1.0
    live[8:16, 0:REL_TABLE] = 1.0
    small_g = small * jnp.asarray(live)
    sw = pack(ffn1_norm, mix_norm, ffn2_norm, final_norm, sinks, rel_bias)
    sm = pack(m_ffn1_norm, m_mix_norm, m_ffn2_norm, m_final_norm, m_sinks, m_rel_bias)
    sv = pack(v_ffn1_norm, v_mix_norm, v_ffn2_norm, v_final_norm, v_sinks, v_rel_bias)
    (sd,), (snm,), (snv,) = _adamw([sw], [small_g], [sm], [sv], "adamw_small")

    def unpack(p):
        return {"ffn1_norm": p[0], "mix_norm": p[1], "ffn2_norm": p[2], "final_norm": p[3],
                "sinks": p[5, 0:B_Q_HEADS], "rel_bias": p[8:16, 0:REL_TABLE]}

    grads.update(unpack(small_g))
    delta, new_m, new_v = unpack(sd), unpack(snm), unpack(snv)

    wmv = {
        "ffn1_w_gate": (ffn1_w_gate, m_ffn1_w_gate, v_ffn1_w_gate), "ffn1_w_up": (ffn1_w_up, m_ffn1_w_up, v_ffn1_w_up),
        "ffn1_w_down": (ffn1_w_down, m_ffn1_w_down, v_ffn1_w_down), "w_in": (w_in, m_w_in, v_w_in),
        "w_proj_a": (w_proj_a, m_w_proj_a, v_w_proj_a), "w_proj_b": (w_proj_b, m_w_proj_b, v_w_proj_b),
        "w_out": (w_out, m_w_out, v_w_out),
        "ffn2_w_gate": (ffn2_w_gate, m_ffn2_w_gate, v_ffn2_w_gate), "ffn2_w_up": (ffn2_w_up, m_ffn2_w_up, v_ffn2_w_up),
        "ffn2_w_down": (ffn2_w_down, m_ffn2_w_down, v_ffn2_w_down),
    }
    def adamw_group(gname, names, hosted=()):
        ds_, ms_, vs_ = _adamw([wmv[n][0] for n in names], [grads[n] for n in names], [wmv[n][1] for n in names],
                               [wmv[n][2] for n in names], gname, hosted=hosted)
        for n, d_, m_, v_ in zip(names, ds_, ms_, vs_):
            delta[n], new_m[n], new_v[n] = d_, m_, v_

    adamw_group("adamw_ffn_up", ["ffn1_w_gate", "ffn1_w_up", "ffn2_w_gate", "ffn2_w_up"])
    adamw_group("adamw_rest", ["ffn1_w_down", "ffn2_w_down", "w_in", "w_proj_a", "w_proj_b", "w_out"])

    order = ["ffn1_norm", "ffn1_w_gate", "ffn1_w_up", "ffn1_w_down", "mix_norm", "w_in", "rel_bias", "sinks",
             "w_proj_a", "w_proj_b", "w_out", "ffn2_norm", "ffn2_w_gate", "ffn2_w_up", "ffn2_w_down", "final_norm"]
    grad_x = dx0.reshape(bsz, s_len, D_MODEL)
    return (loss, grad_x, *[grads[n] for n in order], *[delta[n] for n in order], *[new_m[n] for n in order],
            *[new_v[n] for n in order])
```

```python
import numpy as np
import jax
import jax.numpy as jnp
from jax import lax
from jax.experimental import pallas as pl
from jax.experimental.pallas import tpu as pltpu

F32 = jnp.float32
BF16 = jnp.bfloat16

D_MODEL = 1024
D_FF = 2816
CHUNK = 64
D_HEAD = 64
A_HEADS = 8
A_PREV = 8
MAX_REL = 128
B_Q_HEADS = 8
B_KV_HEADS = 2
B_GROUP = B_Q_HEADS // B_KV_HEADS
B_PREV = 2
REL_TABLE = (CHUNK - 1) + MAX_REL + 1
A_WIDTH = A_HEADS * D_HEAD
B_Q_WIDTH = B_Q_HEADS * D_HEAD
B_KV_WIDTH = B_KV_HEADS * D_HEAD
QKV_A = 3 * A_WIDTH
QKV_B = B_Q_WIDTH + 2 * B_KV_WIDTH
IN_WIDTH = QKV_A + QKV_B + 2 * D_MODEL
EPS = 1e-6
NEG_INF = -1e30
SCALE = 1.0 / 8.0

ADAM_LR = 0.001
ADAM_B1 = 0.9
ADAM_B2 = 0.999
ADAM_EPS = 1e-08
ADAM_WD = 0.01
ADAM_STEP = 10

N_DEV = 8
N_CHIP = 4
MESH = pl.DeviceIdType.MESH

LANES = 128
TQ = 256
TM = 256
TM_FWD = 256
FC = 256
VMEM_LIMIT = 56 << 20


def _cparams(sem, vmem=VMEM_LIMIT):
    return pltpu.CompilerParams(dimension_semantics=sem, vmem_limit_bytes=vmem)


def _dot_nt(a, b):
    return lax.dot_general(a, b, (((1,), (1,)), ((), ())), preferred_element_type=F32)


def _dot_nn(a, b):
    return lax.dot_general(a, b, (((1,), (0,)), ((), ())), preferred_element_type=F32)


def _dot_tn(a, b):
    return lax.dot_general(a, b, (((0,), (0,)), ((), ())), preferred_element_type=F32)


def _resident(shape):
    nd = len(shape)
    return pl.BlockSpec(shape, lambda *_: (0,) * nd, pipeline_mode=pl.Buffered(1))


def _rows(tm, width):
    return pl.BlockSpec((tm, width), lambda i: (i, 0))


def _colsum8(v):
    tm, n = v.shape
    return jnp.sum(v.reshape(tm // 8, 8, n), axis=0)


def _rms(x):
    r = lax.rsqrt(jnp.mean(x * x, axis=-1, keepdims=True) + EPS)
    return x * r, r


def _rms_bwd(dh, xh, r, gamma):
    dxh = dh * gamma
    dx = r * (dxh - xh * jnp.mean(dxh * xh, axis=-1, keepdims=True))
    return dx, _colsum8(dh * xh)


def _hbm():
    return pl.BlockSpec(memory_space=pltpu.HBM)


def _call(body, *, name, grid, in_specs, out_specs, out_shape, args, sem, scratch_shapes=(), hosted=()):
    in_specs, out_specs, out_shape = list(in_specs), list(out_specs), list(out_shape)
    scratch_shapes = list(scratch_shapes)
    if not hosted:
        return pl.pallas_call(body, name=name, grid=grid, in_specs=in_specs, out_specs=out_specs, out_shape=out_shape,
                              scratch_shapes=scratch_shapes, compiler_params=_cparams(sem))(*args)
    n_in, n_out, n_scr = len(in_specs), len(out_specs), len(scratch_shapes)
    x_in = [a for x in hosted for a in x.inputs]
    x_out = [s for x in hosted for s in x.out_shape]
    x_scr = [s for x in hosted for s in x.scratch]
    steps = int(np.prod(grid))
    forward_step = max(steps - 3, 0)

    def wrapped(*refs):
        pos = [0]

        def take(k):
            pos[0] += k
            return refs[pos[0] - k:pos[0]]

        ins, xin, outs, xout, scr, xscr = (take(k) for k in (n_in, len(x_in), n_out, len(x_out), n_scr, len(x_scr)))
        step = 0
        for axis, extent in enumerate(grid):
            step = step * extent + pl.program_id(axis)
        own, oi, oo, osc = [], 0, 0, 0
        for x in hosted:
            own.append((xin[oi:oi + len(x.inputs)], xout[oo:oo + len(x.out_shape)], xscr[osc:osc + len(x.scratch)]))
            oi, oo, osc = oi + len(x.inputs), oo + len(x.out_shape), osc + len(x.scratch)

        def phase(method):
            for x, (i_, o_, s_) in zip(hosted, own):
                getattr(x, method)(i_, o_, s_)

        pl.when(step == 0)(lambda: phase("start"))
        body(*ins, *outs, *scr)
        pl.when(step == forward_step)(lambda: phase("forward"))
        pl.when(step == steps - 1)(lambda: phase("finish"))

    res = pl.pallas_call(
        wrapped, name=name, grid=grid, in_specs=in_specs + [_hbm()] * len(x_in),
        out_specs=out_specs + [_hbm()] * len(x_out), out_shape=out_shape + x_out,
        scratch_shapes=scratch_shapes + x_scr, compiler_params=_cparams(("arbitrary",) * len(grid)))(*args, *x_in)
    rest = list(res[n_out:])
    for x in hosted:
        x.result, rest = rest[:len(x.out_shape)], rest[len(x.out_shape):]
    return list(res[:n_out])


def _ffn_fwd(x, gamma, wg_t, wu_t, wd, name, hosted=()):
    t = x.shape[0]
    f = wg_t.shape[0]

    def body(x_ref, gam_ref, wg_ref, wu_ref, wd_ref, h_ref, g_ref, u_ref, a_ref, y_ref):
        xv = x_ref[...]
        xh, _ = _rms(xv)
        h = (xh * gam_ref[...]).astype(BF16)
        h_ref[...] = h
        for j in range(f // FC):
            sl = slice(j * FC, (j + 1) * FC)
            g = _dot_nt(h, wg_ref[sl, :])
            u = _dot_nt(h, wu_ref[sl, :])
            g_ref[:, sl] = g.astype(BF16)
            u_ref[:, sl] = u.astype(BF16)
            a_ref[:, sl] = (g * jax.nn.sigmoid(g) * u).astype(BF16)
        y_ref[...] = xv + 0.5 * _dot_nn(a_ref[...], wd_ref[...])

    return _call(
        body,
        name=name,
        grid=(t // TM_FWD,),
        in_specs=[_rows(TM_FWD, D_MODEL), _resident((1, D_MODEL)), _resident((f, D_MODEL)), _resident((f, D_MODEL)),
                  _resident((f, D_MODEL))],
        out_specs=[_rows(TM_FWD, D_MODEL), _rows(TM_FWD, f), _rows(TM_FWD, f), _rows(TM_FWD, f),
                   _rows(TM_FWD, D_MODEL)],
        out_shape=[jax.ShapeDtypeStruct((t, D_MODEL), BF16), jax.ShapeDtypeStruct((t, f), BF16),
                   jax.ShapeDtypeStruct((t, f), BF16), jax.ShapeDtypeStruct((t, f), BF16),
                   jax.ShapeDtypeStruct((t, D_MODEL), F32)],
        args=(x, gamma, wg_t, wu_t, wd), sem=("parallel",), hosted=hosted)


def _ffn_bwd(d, x, gamma, g_act, u_act, wg_t, wu_t, wd, name, hosted=()):
    t = x.shape[0]
    f = wg_t.shape[0]

    def body(d_ref, x_ref, gam_ref, g_ref, u_ref, wg_ref, wu_ref, wd_ref, dx_ref, dg_ref, du_ref, db_ref, dgam_ref):
        dv = d_ref[...]
        db = (0.5 * dv).astype(BF16)
        db_ref[...] = db
        for j in range(f // FC):
            sl = slice(j * FC, (j + 1) * FC)
            da = _dot_nt(db, wd_ref[sl, :])
            g = g_ref[:, sl].astype(F32)
            u = u_ref[:, sl].astype(F32)
            s = jax.nn.sigmoid(g)
            dg_ref[:, sl] = (da * u * (s * (1.0 + g * (1.0 - s)))).astype(BF16)
            du_ref[:, sl] = (da * (g * s)).astype(BF16)
        dh = _dot_nn(dg_ref[...], wg_ref[...]) + _dot_nn(du_ref[...], wu_ref[...])
        xh, r = _rms(x_ref[...])
        dxn, dgam = _rms_bwd(dh, xh, r, gam_ref[...])
        dx_ref[...] = dv + dxn

        @pl.when(pl.program_id(0) == 0)
        def _():
            dgam_ref[...] = jnp.zeros_like(dgam_ref)

        dgam_ref[...] += dgam

    return _call(
        body,
        name=name,
        grid=(t // TM,),
        in_specs=[_rows(TM, D_MODEL), _rows(TM, D_MODEL), _resident((1, D_MODEL)), _rows(TM, f), _rows(TM, f),
                  _resident((f, D_MODEL)), _resident((f, D_MODEL)), _resident((f, D_MODEL))],
        out_specs=[_rows(TM, D_MODEL), _rows(TM, f), _rows(TM, f), _rows(TM, D_MODEL),
                   pl.BlockSpec((8, D_MODEL), lambda i: (0, 0))],
        out_shape=[jax.ShapeDtypeStruct((t, D_MODEL), F32), jax.ShapeDtypeStruct((t, f), BF16),
                   jax.ShapeDtypeStruct((t, f), BF16), jax.ShapeDtypeStruct((t, D_MODEL), BF16),
                   jax.ShapeDtypeStruct((8, D_MODEL), F32)],
        args=(d, x, gamma, g_act, u_act, wg_t, wu_t, wd), sem=("arbitrary",), hosted=hosted)


def _ffn_bwd_act(d, g_act, u_act, wd, name, hosted=()):
    t = d.shape[0]
    f = wd.shape[0]

    def body(d_ref, g_ref, u_ref, wd_ref, dg_ref, du_ref):
        db = (0.5 * d_ref[...]).astype(BF16)
        for j in range(f // FC):
            sl = slice(j * FC, (j + 1) * FC)
            da = _dot_nt(db, wd_ref[sl, :])
            g = g_ref[:, sl].astype(F32)
            u = u_ref[:, sl].astype(F32)
            s = jax.nn.sigmoid(g)
            dg_ref[:, sl] = (da * u * (s * (1.0 + g * (1.0 - s)))).astype(BF16)
            du_ref[:, sl] = (da * (g * s)).astype(BF16)

    return _call(
        body,
        name=name,
        grid=(t // TM,),
        in_specs=[_rows(TM, D_MODEL), _rows(TM, f), _rows(TM, f), _resident((f, D_MODEL))],
        out_specs=[_rows(TM, f), _rows(TM, f)],
        out_shape=[jax.ShapeDtypeStruct((t, f), BF16), jax.ShapeDtypeStruct((t, f), BF16)],
        args=(d, g_act, u_act, wd), sem=("parallel",), hosted=hosted)


def _ffn_bwd_in(d, x, gamma, dg, du, wg_t, wu_t, name, hosted=()):
    t = x.shape[0]
    f = wg_t.shape[0]

    def body(d_ref, x_ref, gam_ref, dg_ref, du_ref, wg_ref, wu_ref, dx_ref, dgam_ref):
        dh = _dot_nn(dg_ref[...], wg_ref[...]) + _dot_nn(du_ref[...], wu_ref[...])
        xh, r = _rms(x_ref[...])
        dxn, dgam = _rms_bwd(dh, xh, r, gam_ref[...])
        dx_ref[...] = d_ref[...] + dxn

        @pl.when(pl.program_id(0) == 0)
        def _():
            dgam_ref[...] = jnp.zeros_like(dgam_ref)

        dgam_ref[...] += dgam

    return _call(
        body,
        name=name,
        grid=(t // TM,),
        in_specs=[_rows(TM, D_MODEL), _rows(TM, D_MODEL), _resident((1, D_MODEL)), _rows(TM, f), _rows(TM, f),
                  _resident((f, D_MODEL)), _resident((f, D_MODEL))],
        out_specs=[_rows(TM, D_MODEL), pl.BlockSpec((8, D_MODEL), lambda i: (0, 0))],
        out_shape=[jax.ShapeDtypeStruct((t, D_MODEL), F32), jax.ShapeDtypeStruct((8, D_MODEL), F32)],
        args=(d, x, gamma, dg, du, wg_t, wu_t), sem=("arbitrary",), hosted=hosted)


def _mm_tn(pieces, b, name, tile=256, hosted=()):
    t, n = b.shape
    npc = len(pieces)
    counts = [p.shape[1] // tile for p in pieces]
    los = [sum(counts[:k]) for k in range(npc)]
    total = sum(counts)

    def body(*refs):
        a_refs, b_ref, o_ref = refs[:npc], refs[npc], refs[npc + 1]
        i = pl.program_id(0)
        for k in range(npc):
            @pl.when(jnp.logical_and(i >= los[k], i < los[k] + counts[k]))
            def _(k=k):
                o_ref[...] = _dot_tn(a_refs[k][...], b_ref[...]).astype(BF16)

    def a_spec(k):
        return pl.BlockSpec((t, tile), lambda i: (0, jnp.clip(i - los[k], 0, counts[k] - 1)))

    return _call(
        body,
        name=name,
        grid=(total,),
        in_specs=[a_spec(k) for k in range(npc)] + [_resident((t, n))],
        out_specs=[pl.BlockSpec((tile, n), lambda i: (i, 0))],
        out_shape=[jax.ShapeDtypeStruct((total * tile, n), BF16)],
        args=(*pieces, b), sem=("parallel",), hosted=hosted)[0]


def _mm_tn_proj(dya, dyb, oa, ob, tile=256):
    t = dya.shape[0]

    def body(dya_ref, dyb_ref, oa_ref, ob_ref, o_ref):
        o_ref[:, 0:A_WIDTH] = _dot_tn(dya_ref[...], oa_ref[...]).astype(BF16)
        o_ref[:, A_WIDTH:A_WIDTH + B_Q_WIDTH] = _dot_tn(dyb_ref[...], ob_ref[...]).astype(BF16)

    col = pl.BlockSpec((t, tile), lambda i: (0, i))
    return pl.pallas_call(
        body,
        name="grad_proj",
        grid=(D_MODEL // tile,),
        in_specs=[col, col, _resident((t, A_WIDTH)), _resident((t, B_Q_WIDTH))],
        out_specs=pl.BlockSpec((tile, A_WIDTH + B_Q_WIDTH), lambda i: (i, 0)),
        out_shape=jax.ShapeDtypeStruct((D_MODEL, A_WIDTH + B_Q_WIDTH), BF16),
        compiler_params=_cparams(("parallel",)),
    )(dya, dyb, oa, ob)


def _proj_fwd(x, gamma, win_t, hosted=()):
    t = x.shape[0]

    def body(x_ref, gam_ref, w_ref, h_ref, qa_ref, qb_ref, gt_ref):
        xh, _ = _rms(x_ref[...])
        h = (xh * gam_ref[...]).astype(BF16)
        h_ref[...] = h
        for j in range(QKV_A // FC):
            qa_ref[:, j * FC:(j + 1) * FC] = _dot_nt(h, w_ref[j * FC:(j + 1) * FC, :]).astype(BF16)
        for j in range(QKV_B // FC):
            lo = QKV_A + j * FC
            qb_ref[:, j * FC:(j + 1) * FC] = _dot_nt(h, w_ref[lo:lo + FC, :]).astype(BF16)
        for j in range(2 * D_MODEL // FC):
            lo = QKV_A + QKV_B + j * FC
            gt_ref[:, j * FC:(j + 1) * FC] = _dot_nt(h, w_ref[lo:lo + FC, :])

    return _call(
        body,
        name="proj_fwd",
        grid=(t // TM_FWD,),
        in_specs=[_rows(TM_FWD, D_MODEL), _resident((1, D_MODEL)), _resident((IN_WIDTH, D_MODEL))],
        out_specs=[_rows(TM_FWD, D_MODEL), _rows(TM_FWD, QKV_A), _rows(TM_FWD, QKV_B), _rows(TM_FWD, 2 * D_MODEL)],
        out_shape=[jax.ShapeDtypeStruct((t, D_MODEL), BF16), jax.ShapeDtypeStruct((t, QKV_A), BF16),
                   jax.ShapeDtypeStruct((t, QKV_B), BF16), jax.ShapeDtypeStruct((t, 2 * D_MODEL), F32)],
        args=(x, gamma, win_t), sem=("parallel",), hosted=hosted)


def _proj_bwd(d, x, gamma, pieces, win_t, hosted=()):
    t = x.shape[0]
    npc = len(pieces)
    widths = [p.shape[1] for p in pieces]
    los = [sum(widths[:k]) for k in range(npc)]

    def body(*refs):
        d_ref, x_ref, gam_ref = refs[:3]
        p_refs = refs[3:3 + npc]
        w_ref, dx_ref, db_ref, dgam_ref = refs[3 + npc:]
        dh = _dot_nn(p_refs[0][...], w_ref[0:widths[0], :])
        for k in range(1, npc):
            dh += _dot_nn(p_refs[k][...], w_ref[los[k]:los[k] + widths[k], :])
        xh, r = _rms(x_ref[...])
        dxn, dgam = _rms_bwd(dh, xh, r, gam_ref[...])
        dx = d_ref[...] + dxn
        dx_ref[...] = dx
        db_ref[...] = (0.5 * dx).astype(BF16)

        @pl.when(pl.program_id(0) == 0)
        def _():
            dgam_ref[...] = jnp.zeros_like(dgam_ref)

        dgam_ref[...] += dgam

    return _call(
        body,
        name="proj_bwd",
        grid=(t // TM,),
        in_specs=[_rows(TM, D_MODEL), _rows(TM, D_MODEL), _resident((1, D_MODEL))] + [_rows(TM, w) for w in widths]
        + [_resident((IN_WIDTH, D_MODEL))],
        out_specs=[_rows(TM, D_MODEL), _rows(TM, D_MODEL), pl.BlockSpec((8, D_MODEL), lambda i: (0, 0))],
        out_shape=[jax.ShapeDtypeStruct((t, D_MODEL), F32), jax.ShapeDtypeStruct((t, D_MODEL), BF16),
                   jax.ShapeDtypeStruct((8, D_MODEL), F32)],
        args=(d, x, gamma, *pieces, win_t), sem=("arbitrary",), hosted=hosted)


def _lane_half(shape):
    return lax.broadcasted_iota(jnp.int32, shape, len(shape) - 1) // D_HEAD


def _band_softmax(q, kk, bias, sink, qs, pad):
    s = _dot_nt(q, kk) + bias
    if qs is not None:
        col = lax.broadcasted_iota(jnp.int32, s.shape, 1)
        s = jnp.where(col + qs >= pad, s, NEG_INF)
    m = jnp.max(s, axis=-1, keepdims=True)
    if sink is not None:
        m = jnp.maximum(m, sink)
    p = jnp.exp(s - m)
    den = jnp.sum(p, axis=-1, keepdims=True)
    if sink is not None:
        den = den + jnp.exp(sink - m)
    return p, m, 1.0 / den


def _fill_padded(dst, src, pad):
    dst[0:pad, :] = jnp.zeros((pad,) + dst.shape[1:], dst.dtype)
    dst[pad:, :] = src


FWD_PAIRS = 4
BWD_PAIRS = 2


def _attn_a_fwd(qkv, bias, hosted=()):
    bsz, s_len, _ = qkv.shape
    pad = A_PREV * CHUNK
    band = TQ + pad
    pp = FWD_PAIRS
    w = pp * LANES
    nb = A_WIDTH // w

    def body(q_ref, k_ref, v_ref, b_ref, o_ref, kp, vp):
        i = pl.program_id(2)

        @pl.when(i == 0)
        def _():
            _fill_padded(kp, k_ref[...], pad)
            _fill_padded(vp, v_ref[...], pad)

        qs = pl.multiple_of(i * TQ, TQ)
        half = _lane_half((1, LANES))

        def block(masked):
            for pr in range(pp):
                sl = slice(pr * LANES, (pr + 1) * LANES)
                kk = kp[pl.ds(qs, band), sl]
                vv = vp[pl.ds(qs, band), sl]
                q = q_ref[:, sl] * SCALE
                outs = []
                for j in range(2):
                    qm = jnp.where(half == j, q, jnp.zeros_like(q))
                    p, _, inv = _band_softmax(qm, kk, b_ref[2 * pr + j], None, qs if masked else None, pad)
                    outs.append(_dot_nn(p.astype(BF16), vv) * inv)
                o_ref[:, sl] = jnp.where(half == 0, outs[0], outs[1]).astype(BF16)

        pl.when(i < pad // TQ)(lambda: block(True))
        pl.when(i >= pad // TQ)(lambda: block(False))

    return _call(
        body,
        name="attn_a_fwd",
        grid=(bsz, nb, s_len // TQ),
        in_specs=[pl.BlockSpec((None, TQ, w), lambda b, g, i: (b, i, g)),
                  pl.BlockSpec((None, s_len, w), lambda b, g, i: (b, 0, nb + g)),
                  pl.BlockSpec((None, s_len, w), lambda b, g, i: (b, 0, 2 * nb + g)),
                  pl.BlockSpec((2 * pp, TQ, band), lambda b, g, i: (g, 0, 0))],
        out_specs=[pl.BlockSpec((None, TQ, w), lambda b, g, i: (b, i, g))],
        out_shape=[jax.ShapeDtypeStruct((bsz, s_len, A_WIDTH), BF16)],
        scratch_shapes=[pltpu.VMEM((pad + s_len, w), BF16), pltpu.VMEM((pad + s_len, w), BF16)],
        args=(qkv, qkv, qkv, bias), sem=("arbitrary", "arbitrary", "arbitrary"), hosted=hosted)[0]


def _attn_a_bwd(qkv, bias, do, hosted=()):
    bsz, s_len, _ = qkv.shape
    pad = A_PREV * CHUNK
    band = TQ + pad
    n_i = s_len // TQ
    pp = BWD_PAIRS
    w = pp * LANES
    nb = A_WIDTH // w

    def body(q_ref, k_ref, v_ref, b_ref, do_ref, dq_ref, dk_ref, dv_ref, dbias_ref, kp, vp, dk_acc, dv_acc):
        b = pl.program_id(1)
        i = pl.program_id(2)

        @pl.when(i == 0)
        def _():
            _fill_padded(kp, k_ref[...], pad)
            _fill_padded(vp, v_ref[...], pad)
            dk_acc[...] = jnp.zeros_like(dk_acc)
            dv_acc[...] = jnp.zeros_like(dv_acc)

        @pl.when(jnp.logical_and(b == 0, i == 0))
        def _():
            dbias_ref[...] = jnp.zeros_like(dbias_ref)

        qs = pl.multiple_of(i * TQ, TQ)
        half = _lane_half((1, LANES))

        def block(masked):
            for pr in range(pp):
                sl = slice(pr * LANES, (pr + 1) * LANES)
                kk = kp[pl.ds(qs, band), sl]
                vv = vp[pl.ds(qs, band), sl]
                q = q_ref[:, sl] * SCALE
                dd = do_ref[:, sl]
                dqs, dks, dvs = [], [], []
                for j in range(2):
                    qm = jnp.where(half == j, q, jnp.zeros_like(q))
                    dm = jnp.where(half == j, dd, jnp.zeros_like(dd))
                    p, _, inv = _band_softmax(qm, kk, b_ref[2 * pr + j], None, qs if masked else None, pad)
                    pn = p * inv
                    dp = _dot_nt(dm, vv)
                    delta = jnp.sum(pn * dp, axis=-1, keepdims=True)
                    ds = pn * (dp - delta)
                    dbias_ref[2 * pr + j] += ds[:, band - REL_COLS:]
                    dsb = ds.astype(BF16)
                    dqs.append(_dot_nn(dsb, kk))
                    dks.append(_dot_tn(dsb, q))
                    dvs.append(_dot_tn(pn.astype(BF16), dd))
                dq_ref[:, sl] = (jnp.where(half == 0, dqs[0], dqs[1]) * SCALE).astype(BF16)
                dk_acc[pl.ds(qs, band), sl] += jnp.where(half == 0, dks[0], dks[1])
                dv_acc[pl.ds(qs, band), sl] += jnp.where(half == 0, dvs[0], dvs[1])

        pl.when(i < pad // TQ)(lambda: block(True))
        pl.when(i >= pad // TQ)(lambda: block(False))

        @pl.when(i == n_i - 1)
        def _():
            dk_ref[...] = dk_acc[pad:, :].astype(BF16)
            dv_ref[...] = dv_acc[pad:, :].astype(BF16)

    qspec = pl.BlockSpec((None, TQ, w), lambda g, b, i: (b, i, g))
    kvout = pl.BlockSpec((None, s_len, w), lambda g, b, i: (b, 0, g))
    wide = jax.ShapeDtypeStruct((bsz, s_len, A_WIDTH), BF16)
    return _call(
        body,
        name="attn_a_bwd",
        grid=(nb, bsz, n_i),
        in_specs=[qspec,
                  pl.BlockSpec((None, s_len, w), lambda g, b, i: (b, 0, nb + g)),
                  pl.BlockSpec((None, s_len, w), lambda g, b, i: (b, 0, 2 * nb + g)),
                  pl.BlockSpec((2 * pp, TQ, band), lambda g, b, i: (g, 0, 0)),
                  qspec],
        out_specs=[qspec, kvout, kvout, pl.BlockSpec((2 * pp, TQ, REL_COLS), lambda g, b, i: (g, 0, 0))],
        out_shape=[wide, wide, wide, jax.ShapeDtypeStruct((A_HEADS, TQ, REL_COLS), F32)],
        scratch_shapes=[pltpu.VMEM((pad + s_len, w), BF16), pltpu.VMEM((pad + s_len, w), BF16),
                        pltpu.VMEM((pad + s_len, w), F32), pltpu.VMEM((pad + s_len, w), F32)],
        args=(qkv, qkv, qkv, bias, do), sem=("arbitrary", "arbitrary", "arbitrary"), hosted=hosted)


def _fill_padded_dup(dst, src, pad, h, half):
    other = pltpu.roll(src, D_HEAD, 1)
    _fill_padded(dst, jnp.where(half == h, src, other), pad)


def _attn_b_fwd(qkv, bias, sink):
    bsz, s_len, _ = qkv.shape
    pad = B_PREV * CHUNK
    band = TQ + pad
    kcol = B_Q_WIDTH // LANES
    npair = B_Q_HEADS // 2

    def body(q_ref, k_ref, v_ref, b_ref, s_ref, o_ref, kp, vp):
        i = pl.program_id(1)
        half = _lane_half((1, LANES))

        @pl.when(i == 0)
        def _():
            for h in range(B_KV_HEADS):
                _fill_padded_dup(kp.at[h], k_ref[...], pad, h, half)
                _fill_padded_dup(vp.at[h], v_ref[...], pad, h, half)

        qs = pl.multiple_of(i * TQ, TQ)

        def block(masked):
            for pr in range(npair):
                h = pr // (B_GROUP // 2)
                sl = slice(pr * LANES, (pr + 1) * LANES)
                kk = kp[h, pl.ds(qs, band), :]
                vv = vp[h, pl.ds(qs, band), :]
                q = q_ref[:, sl] * SCALE
                outs = []
                for j in range(2):
                    qm = jnp.where(half == j, q, jnp.zeros_like(q))
                    p, _, inv = _band_softmax(qm, kk, b_ref[2 * pr + j], s_ref[2 * pr + j][0:1, 0:1],
                                              qs if masked else None, pad)
                    outs.append(_dot_nn(p.astype(BF16), vv) * inv)
                o_ref[:, sl] = jnp.where(half == 0, outs[0], outs[1]).astype(BF16)

        pl.when(i < -(-pad // TQ))(lambda: block(True))
        pl.when(i >= -(-pad // TQ))(lambda: block(False))

    return pl.pallas_call(
        body,
        name="attn_b_fwd",
        grid=(bsz, s_len // TQ),
        in_specs=[pl.BlockSpec((None, TQ, B_Q_WIDTH), lambda b, i: (b, i, 0)),
                  pl.BlockSpec((None, s_len, LANES), lambda b, i: (b, 0, kcol)),
                  pl.BlockSpec((None, s_len, LANES), lambda b, i: (b, 0, kcol + 1)),
                  pl.BlockSpec((B_Q_HEADS, TQ, band), lambda b, i: (0, 0, 0)),
                  pl.BlockSpec((B_Q_HEADS, 8, LANES), lambda b, i: (0, 0, 0))],
        out_specs=pl.BlockSpec((None, TQ, B_Q_WIDTH), lambda b, i: (b, i, 0)),
        out_shape=jax.ShapeDtypeStruct((bsz, s_len, B_Q_WIDTH), BF16),
        scratch_shapes=[pltpu.VMEM((B_KV_HEADS, pad + s_len, LANES), BF16),
                        pltpu.VMEM((B_KV_HEADS, pad + s_len, LANES), BF16)],
        compiler_params=_cparams(("arbitrary", "arbitrary")),
    )(qkv, qkv, qkv, bias, sink)


def _attn_b_bwd(qkv, bias, sink, do, hosted=()):
    bsz, s_len, _ = qkv.shape
    pad = B_PREV * CHUNK
    band = TQ + pad
    kcol = B_Q_WIDTH // LANES
    npair = B_Q_HEADS // 2
    n_i = s_len // TQ

    pp = B_GROUP // 2
    w = pp * LANES

    def body(q_ref, k_ref, v_ref, b_ref, s_ref, do_ref, dq_ref, dkv_ref, dsink_ref, kp, vp, dk_acc, dv_acc):
        b = pl.program_id(0)
        h = pl.program_id(1)
        i = pl.program_id(2)
        half = _lane_half((1, LANES))

        @pl.when(i == 0)
        def _():
            _fill_padded_dup(kp, k_ref[...], pad, h, half)
            _fill_padded_dup(vp, v_ref[...], pad, h, half)

        @pl.when(jnp.logical_and(h == 0, i == 0))
        def _():
            dk_acc[...] = jnp.zeros_like(dk_acc)
            dv_acc[...] = jnp.zeros_like(dv_acc)

        @pl.when(jnp.logical_and(b == 0, jnp.logical_and(h == 0, i == 0)))
        def _():
            dsink_ref[...] = jnp.zeros_like(dsink_ref)

        qs = pl.multiple_of(i * TQ, TQ)

        def block(masked):
            kk = kp[pl.ds(qs, band), :]
            vv = vp[pl.ds(qs, band), :]
            dk2 = jnp.zeros((band, LANES), F32)
            dv2 = jnp.zeros((band, LANES), F32)
            for pr in range(pp):
                sl = slice(pr * LANES, (pr + 1) * LANES)
                q = q_ref[:, sl] * SCALE
                dd = do_ref[:, sl]
                dqs, dks, dvs = [], [], []
                for j in range(2):
                    qm = jnp.where(half == j, q, jnp.zeros_like(q))
                    dm = jnp.where(half == j, dd, jnp.zeros_like(dd))
                    sink = s_ref[2 * pr + j][0:1, 0:1]
                    p, m, inv = _band_softmax(qm, kk, b_ref[2 * pr + j], sink, qs if masked else None, pad)
                    pn = p * inv
                    dp = _dot_nt(dm, vv)
                    delta = jnp.sum(pn * dp, axis=-1, keepdims=True)
                    ds = pn * (dp - delta)
                    dsb = ds.astype(BF16)
                    dqs.append(_dot_nn(dsb, kk))
                    dks.append(_dot_tn(dsb, q))
                    dvs.append(_dot_tn(pn.astype(BF16), dd))
                    dsk = jnp.sum(-(jnp.exp(sink - m) * inv) * delta, axis=0, keepdims=True)
                    dsink_ref[2 * pp * h + 2 * pr + j] += jnp.broadcast_to(dsk, (8, LANES))
                dq_ref[:, sl] = (jnp.where(half == 0, dqs[0], dqs[1]) * SCALE).astype(BF16)
                dk2 = dk2 + jnp.where(half == 0, dks[0], dks[1])
                dv2 = dv2 + jnp.where(half == 0, dvs[0], dvs[1])
            dk_acc[pl.ds(qs, band), :] += jnp.where(half == h, dk2 + pltpu.roll(dk2, D_HEAD, 1), 0.0)
            dv_acc[pl.ds(qs, band), :] += jnp.where(half == h, dv2 + pltpu.roll(dv2, D_HEAD, 1), 0.0)

        pl.when(i < -(-pad // TQ))(lambda: block(True))
        pl.when(i >= -(-pad // TQ))(lambda: block(False))

        @pl.when(jnp.logical_and(h == B_KV_HEADS - 1, i == n_i - 1))
        def _():
            dkv_ref[:, 0:LANES] = dk_acc[pad:, :].astype(BF16)
            dkv_ref[:, LANES:2 * LANES] = dv_acc[pad:, :].astype(BF16)

    qspec = pl.BlockSpec((None, TQ, w), lambda b, h, i: (b, i, h))
    return _call(
        body,
        name="attn_b_bwd",
        grid=(bsz, B_KV_HEADS, n_i),
        in_specs=[qspec,
                  pl.BlockSpec((None, s_len, LANES), lambda b, h, i: (b, 0, kcol)),
                  pl.BlockSpec((None, s_len, LANES), lambda b, h, i: (b, 0, kcol + 1)),
                  pl.BlockSpec((2 * pp, TQ, band), lambda b, h, i: (h, 0, 0)),
                  pl.BlockSpec((2 * pp, 8, LANES), lambda b, h, i: (h, 0, 0)),
                  qspec],
        out_specs=[qspec, pl.BlockSpec((None, s_len, 2 * LANES), lambda b, h, i: (b, 0, 0)),
                   pl.BlockSpec((B_Q_HEADS, 8, LANES), lambda b, h, i: (0, 0, 0))],
        out_shape=[jax.ShapeDtypeStruct((bsz, s_len, B_Q_WIDTH), BF16),
                   jax.ShapeDtypeStruct((bsz, s_len, 2 * B_KV_WIDTH), BF16),
                   jax.ShapeDtypeStruct((B_Q_HEADS, 8, LANES), F32)],
        scratch_shapes=[pltpu.VMEM((pad + s_len, LANES), BF16), pltpu.VMEM((pad + s_len, LANES), BF16),
                        pltpu.VMEM((pad + s_len, LANES), F32), pltpu.VMEM((pad + s_len, LANES), F32)],
        args=(qkv, qkv, qkv, bias, sink, do), sem=("arbitrary", "arbitrary", "arbitrary"), hosted=hosted)


REL_COLS = 3 * 128
REL_WRAP = 512


def _bias_a_build(tv):
    h = tv.shape[0]
    pad = A_PREV * CHUNK
    band = TQ + pad

    def body(tv_ref, o_ref):
        row = tv_ref[...]
        x = jnp.broadcast_to(row, (TQ, REL_WRAP))
        r = lax.broadcasted_iota(jnp.int32, x.shape, 0)
        for bit in range(8):
            sh = 1 << bit
            x = jnp.where((r & sh) != 0, pltpu.roll(x, sh, 1), x)
        far = jnp.broadcast_to(row[:, 0:1], (TQ, band - REL_COLS))
        full = jnp.concatenate([far, x[:, REL_WRAP // 2:REL_WRAP], x[:, 0:REL_COLS - REL_WRAP // 2]], axis=1)
        qc = (lax.broadcasted_iota(jnp.int32, full.shape, 0) + pad) // CHUNK
        kc = lax.broadcasted_iota(jnp.int32, full.shape, 1) // CHUNK
        ok = jnp.logical_and(kc <= qc, kc >= qc - A_PREV)
        o_ref[...] = jnp.where(ok, full, NEG_INF)

    return pl.pallas_call(
        body,
        name="bias_a_build",
        grid=(h,),
        in_specs=[pl.BlockSpec((None, 1, REL_WRAP), lambda hh: (hh, 0, 0))],
        out_specs=pl.BlockSpec((None, TQ, band), lambda hh: (hh, 0, 0)),
        out_shape=jax.ShapeDtypeStruct((h, TQ, band), F32),
        compiler_params=_cparams(("parallel",)),
    )(tv)


def _relbias_grad(dbias):
    h, rows, _ = dbias.shape

    def body(d_ref, o_ref):
        x = d_ref[...]
        r = lax.broadcasted_iota(jnp.int32, x.shape, 0)
        c = lax.broadcasted_iota(jnp.int32, x.shape, 1) - r
        x = jnp.where(jnp.logical_and(c >= 1, c < REL_TABLE), x, 0.0)
        for bit in range(8):
            sh = 1 << bit
            x = jnp.where((r & sh) != 0, pltpu.roll(x, REL_COLS - sh, 1), x)
        diag = jnp.sum(x, axis=0, keepdims=True)
        lane = lax.broadcasted_iota(jnp.int32, diag.shape, 1)
        diag = jnp.where(jnp.logical_and(lane >= 1, lane < REL_TABLE), diag, 0.0)
        rest = -jnp.sum(diag, axis=1, keepdims=True)
        o_ref[...] = jnp.broadcast_to(jnp.where(lane == 0, rest, diag), o_ref.shape)

    return pl.pallas_call(
        body,
        name="relbias_grad",
        grid=(h,),
        in_specs=[pl.BlockSpec((None, rows, REL_COLS), lambda hh: (hh, 0, 0))],
        out_specs=pl.BlockSpec((None, 8, REL_COLS), lambda hh: (hh, 0, 0)),
        out_shape=jax.ShapeDtypeStruct((h, 8, REL_COLS), F32),
        compiler_params=_cparams(("parallel",)),
    )(dbias)


def _mix_out_fwd(x, oa, ob, gates, proj_t, wout):
    t = x.shape[0]

    def body(x_ref, oa_ref, ob_ref, gt_ref, pt_ref, wo_ref, y_ref, ya_ref, yb_ref, mg_ref):
        ya = _dot_nt(oa_ref[...], pt_ref[:, 0:A_WIDTH])
        yb = _dot_nt(ob_ref[...], pt_ref[:, A_WIDTH:A_WIDTH + B_Q_WIDTH])
        ya_ref[...] = ya.astype(BF16)
        yb_ref[...] = yb.astype(BF16)
        mg = jax.nn.sigmoid(gt_ref[:, 0:D_MODEL]) * ya + jax.nn.sigmoid(gt_ref[:, D_MODEL:2 * D_MODEL]) * yb
        mgb = mg.astype(BF16)
        mg_ref[...] = mgb
        y_ref[...] = x_ref[...] + _dot_nn(mgb, wo_ref[...])

    return pl.pallas_call(
        body,
        name="mix_out_fwd",
        grid=(t // TM,),
        in_specs=[_rows(TM, D_MODEL), _rows(TM, A_WIDTH), _rows(TM, B_Q_WIDTH), _rows(TM, 2 * D_MODEL),
                  _resident((D_MODEL, A_WIDTH + B_Q_WIDTH)), _resident((D_MODEL, D_MODEL))],
        out_specs=[_rows(TM, D_MODEL), _rows(TM, D_MODEL), _rows(TM, D_MODEL), _rows(TM, D_MODEL)],
        out_shape=[jax.ShapeDtypeStruct((t, D_MODEL), F32), jax.ShapeDtypeStruct((t, D_MODEL), BF16),
                   jax.ShapeDtypeStruct((t, D_MODEL), BF16), jax.ShapeDtypeStruct((t, D_MODEL), BF16)],
        compiler_params=_cparams(("parallel",)),
    )(x, oa, ob, gates, proj_t, wout)


def _mix_out_bwd(d, gates, ya, yb, proj_t, wout, hosted=()):
    t = d.shape[0]

    def body(d_ref, gt_ref, ya_ref, yb_ref, pt_ref, wo_ref, db_ref, dya_ref, dyb_ref, doa_ref, dob_ref, dgt_ref):
        db = d_ref[...].astype(BF16)
        db_ref[...] = db
        dmg = _dot_nt(db, wo_ref[...])
        sa = jax.nn.sigmoid(gt_ref[:, 0:D_MODEL])
        sb = jax.nn.sigmoid(gt_ref[:, D_MODEL:2 * D_MODEL])
        dya = (dmg * sa).astype(BF16)
        dyb = (dmg * sb).astype(BF16)
        dya_ref[...] = dya
        dyb_ref[...] = dyb
        dgt_ref[:, 0:D_MODEL] = (dmg * ya_ref[...].astype(F32) * (sa * (1.0 - sa))).astype(BF16)
        dgt_ref[:, D_MODEL:2 * D_MODEL] = (dmg * yb_ref[...].astype(F32) * (sb * (1.0 - sb))).astype(BF16)
        doa_ref[...] = _dot_nn(dya, pt_ref[:, 0:A_WIDTH]).astype(BF16)
        dob_ref[...] = _dot_nn(dyb, pt_ref[:, A_WIDTH:A_WIDTH + B_Q_WIDTH]).astype(BF16)

    return _call(
        body,
        name="mix_out_bwd",
        grid=(t // TM,),
        in_specs=[_rows(TM, D_MODEL), _rows(TM, 2 * D_MODEL), _rows(TM, D_MODEL), _rows(TM, D_MODEL),
                  _resident((D_MODEL, A_WIDTH + B_Q_WIDTH)), _resident((D_MODEL, D_MODEL))],
        out_specs=[_rows(TM, D_MODEL), _rows(TM, D_MODEL), _rows(TM, D_MODEL), _rows(TM, A_WIDTH),
                   _rows(TM, B_Q_WIDTH), _rows(TM, 2 * D_MODEL)],
        out_shape=[jax.ShapeDtypeStruct((t, D_MODEL), BF16), jax.ShapeDtypeStruct((t, D_MODEL), BF16),
                   jax.ShapeDtypeStruct((t, D_MODEL), BF16), jax.ShapeDtypeStruct((t, A_WIDTH), BF16),
                   jax.ShapeDtypeStruct((t, B_Q_WIDTH), BF16), jax.ShapeDtypeStruct((t, 2 * D_MODEL), BF16)],
        args=(d, gates, ya, yb, proj_t, wout), sem=("parallel",), hosted=hosted)


def _loss_head(x, gamma, target):
    t = x.shape[0]

    def body(x_ref, gam_ref, t_ref, dx_ref, dgam_ref, loss_ref):
        xh, r = _rms(x_ref[...])
        gam = gam_ref[...]
        e = xh * gam - t_ref[...]
        dy = e * (1.0 / D_MODEL)
        dxn, dgam = _rms_bwd(dy, xh, r, gam)
        dx_ref[...] = dxn

        @pl.when(pl.program_id(0) == 0)
        def _():
            dgam_ref[...] = jnp.zeros_like(dgam_ref)
            loss_ref[...] = jnp.zeros_like(loss_ref)

        dgam_ref[...] += dgam
        loss_ref[...] += _colsum8(e * e) * (0.5 / D_MODEL)

    return pl.pallas_call(
        body,
        name="loss_head",
        grid=(t // TM,),
        in_specs=[_rows(TM, D_MODEL), _resident((1, D_MODEL)), _rows(TM, D_MODEL)],
        out_specs=[_rows(TM, D_MODEL), pl.BlockSpec((8, D_MODEL), lambda i: (0, 0)),
                   pl.BlockSpec((8, D_MODEL), lambda i: (0, 0))],
        out_shape=[jax.ShapeDtypeStruct((t, D_MODEL), F32), jax.ShapeDtypeStruct((8, D_MODEL), F32),
                   jax.ShapeDtypeStruct((8, D_MODEL), F32)],
        compiler_params=_cparams(("arbitrary",)),
    )(x, gamma, target)


def _place():
    x, y, c = lax.axis_index("x"), lax.axis_index("y"), lax.axis_index("c")
    chips = [(1 - x, y), (x, 1 - y), (1 - x, 1 - y)]
    return x, y, c, chips


class _Gather:
    per = 7

    def __init__(self, shards):
        n = len(shards)
        self.inputs = list(shards)
        self.out_shape = [jax.ShapeDtypeStruct((N_DEV * s.shape[0], s.shape[1]), s.dtype) for s in shards]
        self.scratch = [pltpu.SemaphoreType.DMA((n * self.per,)), pltpu.SemaphoreType.DMA((n * self.per,)),
                        pltpu.SemaphoreType.DMA((n,))]
        self.result = None

    def _parts(self, ins, outs, sems):
        send_sems, recv_sems, local_sems = sems
        x, y, c, chips = _place()
        me, sibling = (x, y, c), (x, y, 1 - c)
        n = len(ins)

        def rows(k, p):
            r = ins[k].shape[0]
            return outs[k].at[pl.ds((4 * p[0] + 2 * p[1] + p[2]) * r, r), :]

        def copy(k, slot, block, to, src=None):
            return pltpu.make_async_remote_copy(
                src_ref=rows(k, block) if src is None else src, dst_ref=rows(k, block),
                send_sem=send_sems.at[k * self.per + slot], recv_sem=recv_sems.at[k * self.per + slot],
                device_id=to, device_id_type=MESH)

        mine = [pltpu.make_async_copy(ins[k], rows(k, me), local_sems.at[k]) for k in range(n)]
        first = []
        for k in range(n):
            first.append(copy(k, 0, me, sibling, src=ins[k]))
            first += [copy(k, 1 + j, me, (*chip, c), src=ins[k]) for j, chip in enumerate(chips)]
        passed = [copy(k, 4 + j, (*chip, c), sibling) for j, chip in enumerate(chips) for k in range(n)]
        return n, c, me, sibling, chips, copy, mine, first, passed

    def start(self, ins, outs, sems):
        _, _, _, _, _, _, mine, first, _ = self._parts(ins, outs, sems)
        for cp in mine + first:
            cp.start()

    def forward(self, ins, outs, sems):
        n, c, me, _, chips, copy, _, _, passed = self._parts(ins, outs, sems)
        for j, chip in enumerate(chips):
            for k in range(n):
                copy(k, 1 + j, (*chip, c), me).wait_recv()
                passed[j * n + k].start()

    def finish(self, ins, outs, sems):
        n, c, me, sibling, chips, copy, mine, first, passed = self._parts(ins, outs, sems)
        for k in range(n):
            copy(k, 0, sibling, me).wait_recv()
            for j, chip in enumerate(chips):
                copy(k, 4 + j, (*chip, 1 - c), me).wait_recv()
        for cp in first + passed:
            cp.wait_send()
        for cp in mine:
            cp.wait()


class _PairExchange:
    def __init__(self, grads):
        n = len(grads)
        self.inputs = list(grads)
        self.out_shape = [jax.ShapeDtypeStruct((g.shape[0] // 2, g.shape[1]), g.dtype) for g in grads]
        self.scratch = [pltpu.SemaphoreType.DMA((n * N_CHIP,)), pltpu.SemaphoreType.DMA((n * N_CHIP,))]
        self.result = None

    def _copies(self, ins, outs, sems):
        send_sems, recv_sems = sems
        x, y, c, _ = _place()
        copies = []
        for k in range(len(ins)):
            r = ins[k].shape[0] // N_DEV
            for q in range(N_CHIP):
                copies.append(pltpu.make_async_remote_copy(
                    src_ref=ins[k].at[pl.ds((2 * q + 1 - c) * r, r), :], dst_ref=outs[k].at[pl.ds(q * r, r), :],
                    send_sem=send_sems.at[k * N_CHIP + q], recv_sem=recv_sems.at[k * N_CHIP + q],
                    device_id=(x, y, 1 - c), device_id_type=MESH))
        return copies

    def start(self, ins, outs, sems):
        for cp in self._copies(ins, outs, sems):
            cp.start()

    def forward(self, ins, outs, sems):
        pass

    def finish(self, ins, outs, sems):
        copies = self._copies(ins, outs, sems)
        for cp in copies:
            cp.wait_recv()
        for cp in copies:
            cp.wait_send()


class _ChipExchange(_PairExchange):
    def __init__(self, psums):
        n = len(psums)
        self.inputs = list(psums)
        self.out_shape = [jax.ShapeDtypeStruct((3 * p.shape[0] // N_CHIP, p.shape[1]), p.dtype) for p in psums]
        self.scratch = [pltpu.SemaphoreType.DMA((n * 3,)), pltpu.SemaphoreType.DMA((n * 3,))]
        self.result = None

    def _copies(self, ins, outs, sems):
        send_sems, recv_sems = sems
        _, _, c, chips = _place()
        copies = []
        for k in range(len(ins)):
            r = ins[k].shape[0] // N_CHIP
            for j, chip in enumerate(chips):
                copies.append(pltpu.make_async_remote_copy(
                    src_ref=ins[k].at[pl.ds((2 * chip[0] + chip[1]) * r, r), :], dst_ref=outs[k].at[pl.ds(j * r, r), :],
                    send_sem=send_sems.at[k * 3 + j], recv_sem=recv_sems.at[k * 3 + j],
                    device_id=(*chip, c), device_id_type=MESH))
        return copies


def _exchange_alone(xchg, name):
    n_in, n_out = len(xchg.inputs), len(xchg.out_shape)

    def body(*refs):
        ins, outs, sems = refs[:n_in], refs[n_in:n_in + n_out], refs[n_in + n_out:]
        xchg.start(ins, outs, sems)
        xchg.forward(ins, outs, sems)
        xchg.finish(ins, outs, sems)

    xchg.result = list(pl.pallas_call(
        body, name=name, in_specs=[_hbm()] * n_in, out_specs=[_hbm()] * n_out, out_shape=xchg.out_shape,
        scratch_shapes=xchg.scratch)(*xchg.inputs))
    return xchg.result


def _pair_sum(core, grads, recvd, name):
    n = len(grads)
    r = grads[0].shape[0] // N_DEV
    cdim = grads[0].shape[1]
    tr = r // 2 if r % 32 == 0 else r
    nt = r // tr

    def body(core_ref, *refs):
        del core_ref
        for k in range(n):
            refs[2 * n + k][...] = (refs[k][...].astype(F32) + refs[n + k][...].astype(F32)).astype(BF16)

    gspec = pl.BlockSpec((tr, cdim), lambda q, i, core_ref: ((2 * q + core_ref[0]) * nt + i, 0))
    rspec = pl.BlockSpec((tr, cdim), lambda q, i, core_ref: (q * nt + i, 0))
    return pl.pallas_call(
        body,
        name=name,
        grid_spec=pltpu.PrefetchScalarGridSpec(
            num_scalar_prefetch=1, grid=(N_CHIP, nt), in_specs=[gspec] * n + [rspec] * n, out_specs=[rspec] * n),
        out_shape=[jax.ShapeDtypeStruct((N_CHIP * r, cdim), BF16) for _ in range(n)],
        compiler_params=_cparams(("parallel", "parallel")),
    )(core, *grads, *recvd)


def _final_sum(chip, psums, recvd, name):
    n = len(psums)
    r = psums[0].shape[0] // N_CHIP
    cdim = psums[0].shape[1]
    tr = r // 2 if r % 32 == 0 else r
    nt = r // tr

    def body(chip_ref, *refs):
        del chip_ref
        for k in range(n):
            got = refs[n + k]
            tot = refs[k][...].astype(F32) + got[0].astype(F32)
            tot = tot + got[1].astype(F32)
            tot = tot + got[2].astype(F32)
            refs[2 * n + k][...] = tot

    pspec = pl.BlockSpec((tr, cdim), lambda i, chip_ref: (chip_ref[0] * nt + i, 0))
    rspec = pl.BlockSpec((3, tr, cdim), lambda i, chip_ref: (0, i, 0))
    ospec = pl.BlockSpec((tr, cdim), lambda i, chip_ref: (i, 0))
    return pl.pallas_call(
        body,
        name=name,
        grid_spec=pltpu.PrefetchScalarGridSpec(
            num_scalar_prefetch=1, grid=(nt,), in_specs=[pspec] * n + [rspec] * n, out_specs=[ospec] * n),
        out_shape=[jax.ShapeDtypeStruct((r, cdim), F32) for _ in range(n)],
        compiler_params=_cparams(("parallel",)),
    )(chip, *psums, *[g.reshape(3, r, cdim) for g in recvd])


SMALL_ROWS = 16


def _all_reduce_small(part):
    def body(p_ref, o_ref, buf, send_sems, recv_sems):
        x, y, c, _ = _place()
        me = 4 * x + 2 * y + c
        buf[me] = p_ref[...]
        copies = []
        for d in range(1, N_DEV):
            peer = me ^ d
            copies.append(pltpu.make_async_remote_copy(
                src_ref=p_ref, dst_ref=buf.at[me], send_sem=send_sems.at[d - 1], recv_sem=recv_sems.at[d - 1],
                device_id=(peer // 4, (peer // 2) % 2, peer % 2), device_id_type=MESH))
        for cp in copies:
            cp.start()
        for cp in copies:
            cp.wait_recv()
        for cp in copies:
            cp.wait_send()
        tot = buf[0]
        for d in range(1, N_DEV):
            tot = tot + buf[d]
        o_ref[...] = tot

    return pl.pallas_call(
        body,
        name="all_reduce_small",
        in_specs=[pl.BlockSpec(memory_space=pltpu.VMEM)],
        out_specs=pl.BlockSpec(memory_space=pltpu.VMEM),
        out_shape=jax.ShapeDtypeStruct(part.shape, F32),
        scratch_shapes=[pltpu.VMEM((N_DEV,) + part.shape, F32), pltpu.SemaphoreType.DMA((N_DEV - 1,)),
                        pltpu.SemaphoreType.DMA((N_DEV - 1,))],
    )(part)


ADAMW_STEPS = 4


def _adamw(ws, gs, ms, vs, name, hosted=()):
    n = len(ws)
    steps = ADAMW_STEPS if all(w.shape[0] % (8 * ADAMW_STEPS) == 0 for w in ws) else 1
    c1 = 1.0 - ADAM_B1 ** ADAM_STEP
    c2 = 1.0 - ADAM_B2 ** ADAM_STEP

    def body(*refs):
        for k in range(n):
            w, g, m, v = (refs[j * n + k][...] for j in range(4))
            m2 = ADAM_B1 * m + (1.0 - ADAM_B1) * g
            v2 = ADAM_B2 * v + (1.0 - ADAM_B2) * (g * g)
            delta = -ADAM_LR * ((m2 / c1) / (jnp.sqrt(v2 / c2) + ADAM_EPS) + ADAM_WD * w)
            refs[4 * n + k][...] = delta
            refs[5 * n + k][...] = m2
            refs[6 * n + k][...] = v2

    specs = [pl.BlockSpec((w.shape[0] // steps, w.shape[1]), lambda i: (i, 0)) for w in ws]
    shapes = [jax.ShapeDtypeStruct(w.shape, F32) for w in ws]
    outs = _call(
        body,
        name=name,
        grid=(steps,),
        in_specs=specs * 4,
        out_specs=specs * 3,
        out_shape=shapes * 3,
        args=(*ws, *gs, *ms, *vs), sem=("parallel",), hosted=hosted)
    return outs[:n], outs[n:2 * n], outs[2 * n:]


def _bias_b():
    pad = B_PREV * CHUNK
    slopes = np.array([2.0 ** (-8.0 * (i + 1) / B_Q_HEADS) for i in range(B_Q_HEADS)], dtype=np.float32)
    dist = np.abs(np.arange(TQ)[:, None] - np.arange(TQ + pad)[None, :] + pad).astype(np.float32)
    bias = -slopes.reshape(B_Q_HEADS, 1, 1) * dist[None]
    qc = (np.arange(TQ)[:, None] + pad) // CHUNK
    kc = np.arange(TQ + pad)[None, :] // CHUNK
    allowed = (kc <= qc) & (kc >= qc - B_PREV)
    return np.where(allowed[None], bias, np.float32(NEG_INF)).astype(np.float32)


def kernel(x, ffn1_norm, ffn1_w_gate, ffn1_w_up, ffn1_w_down, mix_norm, w_in, rel_bias, sinks, w_proj_a, w_proj_b, w_out, ffn2_norm, ffn2_w_gate, ffn2_w_up, ffn2_w_down, final_norm, loss_target, m_ffn1_norm, m_ffn1_w_gate, m_ffn1_w_up, m_ffn1_w_down, m_mix_norm, m_w_in, m_rel_bias, m_sinks, m_w_proj_a, m_w_proj_b, m_w_out, m_ffn2_norm, m_ffn2_w_gate, m_ffn2_w_up, m_ffn2_w_down, m_final_norm, v_ffn1_norm, v_ffn1_w_gate, v_ffn1_w_up, v_ffn1_w_down, v_mix_norm, v_w_in, v_rel_bias, v_sinks, v_w_proj_a, v_w_proj_b, v_w_out, v_ffn2_norm, v_ffn2_w_gate, v_ffn2_w_up, v_ffn2_w_down, v_final_norm):
    bsz, s_len, _ = x.shape
    t = bsz * s_len
    core = lax.axis_index("c").astype(jnp.int32).reshape(1)
    chip = (2 * lax.axis_index("x") + lax.axis_index("y")).astype(jnp.int32).reshape(1)

    def row_form(w):
        return w.astype(BF16).T

    wg1, wu1, wd1 = _exchange_alone(
        _Gather([row_form(ffn1_w_gate), row_form(ffn1_w_up), ffn1_w_down.astype(BF16)]), "gather_ffn1")
    gather_mix = _Gather([row_form(w_in), jnp.concatenate([row_form(w_proj_a), row_form(w_proj_b)], axis=1),
                          w_out.astype(BF16)])
    gather_ffn2_gate = _Gather([row_form(ffn2_w_gate)])
    gather_ffn2_rest = _Gather([row_form(ffn2_w_up), ffn2_w_down.astype(BF16)])

    x0 = x.reshape(t, D_MODEL)
    tgt = loss_target.reshape(t, D_MODEL)
    gam1, gam2, gam3, gam4 = (g.reshape(1, D_MODEL) for g in (ffn1_norm, mix_norm, ffn2_norm, final_norm))

    h1, g1, u1, a1, x1 = _ffn_fwd(x0, gam1, wg1, wu1, wd1, "ffn1_fwd", hosted=[gather_mix])
    win_t, proj_t, wout = gather_mix.result
    h2, qkv_a, qkv_b, gates = _proj_fwd(x1, gam2, win_t, hosted=[gather_ffn2_gate])
    (wg2,) = gather_ffn2_gate.result
    qkv_a3 = qkv_a.reshape(bsz, s_len, QKV_A)
    qkv_b3 = qkv_b.reshape(bsz, s_len, QKV_B)

    far = jnp.broadcast_to(rel_bias[:, REL_TABLE - 1:REL_TABLE], (A_HEADS, REL_WRAP // 2))
    tv = jnp.concatenate([far, jnp.flip(rel_bias, axis=1), jnp.zeros((A_HEADS, REL_WRAP // 2 - REL_TABLE), F32)], axis=1)
    bias_a = _bias_a_build(tv.reshape(A_HEADS, 1, REL_WRAP))
    bias_b = jnp.asarray(_bias_b())
    sink_rows = jnp.broadcast_to(sinks.reshape(B_Q_HEADS, 1, 1), (B_Q_HEADS, 8, LANES))

    oa = _attn_a_fwd(qkv_a3, bias_a, hosted=[gather_ffn2_rest]).reshape(t, A_WIDTH)
    wu2, wd2 = gather_ffn2_rest.result
    ob = _attn_b_fwd(qkv_b3, bias_b, sink_rows).reshape(t, B_Q_WIDTH)
    x2, ya, yb, mg = _mix_out_fwd(x1, oa, ob, gates, proj_t, wout)
    h3, g2, u2, a2, x3 = _ffn_fwd(x2, gam3, wg2, wu2, wd2, "ffn2_fwd")

    dx3, dgam4, loss_part = _loss_head(x3, gam4, tgt)

    dx2, dg2, du2, db2, dgam3 = _ffn_bwd(dx3, x2, gam3, g2, u2, wg2, wu2, wd2, "ffn2_bwd")
    gw_ffn2 = [_mm_tn([dg2], h3, "grad_ffn2_gate"), _mm_tn([du2], h3, "grad_ffn2_up"),
               _mm_tn([a2], db2, "grad_ffn2_down")]
    pairx_ffn2 = _PairExchange(gw_ffn2)
    dxb, dya, dyb, doa, dob, dgates = _mix_out_bwd(dx2, gates, ya, yb, proj_t, wout, hosted=[pairx_ffn2])
    psum_ffn2 = _pair_sum(core, gw_ffn2, pairx_ffn2.result, "pair_sum_ffn2")
    gw_out = _mm_tn([mg], dxb, "grad_w_out")
    gw_proj = _mm_tn_proj(dya, dyb, oa, ob)

    chipx_ffn2 = _ChipExchange(psum_ffn2)
    dqa, dka, dva, dbias_a = _attn_a_bwd(qkv_a3, bias_a, doa.reshape(bsz, s_len, A_WIDTH), hosted=[chipx_ffn2])
    pairx_out = _PairExchange([gw_proj, gw_out])
    dqb, dkvb, dsink = _attn_b_bwd(qkv_b3, bias_b, sink_rows, dob.reshape(bsz, s_len, B_Q_WIDTH), hosted=[pairx_out])
    drel_lanes = _relbias_grad(dbias_a)
    dproj = [dqa.reshape(t, A_WIDTH), dka.reshape(t, A_WIDTH), dva.reshape(t, A_WIDTH), dqb.reshape(t, B_Q_WIDTH),
             dkvb.reshape(t, 2 * B_KV_WIDTH), dgates]

    gw_in = _mm_tn(dproj, h2, "grad_w_in")
    pairx_in = _PairExchange([gw_in])
    psum_out = _pair_sum(core, [gw_proj, gw_out], pairx_out.result, "pair_sum_mix")
    chipx_out = _ChipExchange(psum_out)
    dx1, db1, dgam2 = _proj_bwd(dx2, x1, gam2, dproj, win_t, hosted=[pairx_in, chipx_out])
    psum_in = _pair_sum(core, [gw_in], pairx_in.result, "pair_sum_w_in")
    gw_d1 = _mm_tn([a1], db1, "grad_ffn1_down")

    chipx_in = _ChipExchange(psum_in)
    pairx_d1 = _PairExchange([gw_d1])
    dg1, du1 = _ffn_bwd_act(dx1, g1, u1, wd1, "ffn1_bwd_act", hosted=[pairx_d1])
    psum_d1 = _pair_sum(core, [gw_d1], pairx_d1.result, "pair_sum_ffn1_down")
    chipx_d1 = _ChipExchange(psum_d1)
    gw_g1 = _mm_tn([dg1], h1, "grad_ffn1_gate", hosted=[chipx_d1])
    from_sibling_g1 = _exchange_alone(_PairExchange([gw_g1]), "pair_exchange_ffn1_gate")
    psum_g1 = _pair_sum(core, [gw_g1], from_sibling_g1, "pair_sum_ffn1_gate")
    chipx_g1 = _ChipExchange(psum_g1)
    gw_u1 = _mm_tn([du1], h1, "grad_ffn1_up", hosted=[chipx_g1])
    from_sibling_u1 = _exchange_alone(_PairExchange([gw_u1]), "pair_exchange_ffn1_up")
    psum_u1 = _pair_sum(core, [gw_u1], from_sibling_u1, "pair_sum_ffn1_up")
    chipx_u1 = _ChipExchange(psum_u1)
    dx0, dgam1 = _ffn_bwd_in(dx1, x0, gam1, dg1, du1, wg1, wu1, "ffn1_bwd_in", hosted=[chipx_in, chipx_u1])

    g_g1, g_u1, g_d1, g_g2, g_u2, g_d2 = _final_sum(
        chip, psum_g1 + psum_u1 + psum_d1 + psum_ffn2,
        chipx_g1.result + chipx_u1.result + chipx_d1.result + chipx_ffn2.result, "grad_sum_ffn")
    (g_in,) = _final_sum(chip, psum_in, chipx_in.result, "grad_sum_w_in")
    g_proj, g_out = _final_sum(chip, psum_out, chipx_out.result, "grad_sum_mix")
    grads = {
        "ffn1_w_gate": g_g1.T, "ffn1_w_up": g_u1.T, "ffn1_w_down": g_d1, "w_in": g_in.T,
        "w_proj_a": g_proj[:, 0:A_WIDTH].T, "w_proj_b": g_proj[:, A_WIDTH:].T, "w_out": g_out,
        "ffn2_w_gate": g_g2.T, "ffn2_w_up": g_u2.T, "ffn2_w_down": g_d2,
    }

    def row_of(v):
        return jnp.pad(v.reshape(1, -1), ((0, 0), (0, D_MODEL - v.size)))

    def table_rows(v):
        return jnp.pad(v, ((0, 0), (0, D_MODEL - REL_TABLE)))

    drel_local = jnp.flip(drel_lanes[:, 0, 0:REL_TABLE], axis=1)
    small_part = jnp.concatenate(
        [jnp.sum(dgam1, axis=0, keepdims=True), jnp.sum(dgam2, axis=0, keepdims=True),
         jnp.sum(dgam3, axis=0, keepdims=True), jnp.sum(dgam4, axis=0, keepdims=True),
         row_of(jnp.sum(loss_part)), row_of(dsink[:, 0, 0]), jnp.zeros((2, D_MODEL), F32),
         table_rows(drel_local)], axis=0)
    small = _all_reduce_small(small_part)
    loss = small[4, 0]

    def pack(n1, n2, n3, n4, sk, tb):
        return jnp.concatenate([n1.reshape(1, -1), n2.reshape(1, -1), n3.reshape(1, -1), n4.reshape(1, -1),
                                jnp.zeros((1, D_MODEL), F32), row_of(sk), jnp.zeros((2, D_MODEL), F32), table_rows(tb)],
                               axis=0)

    live = np.zeros((SMALL_ROWS, D_MODEL), np.float32)
    live[0:4] = 1.0
    live[5, 0:B_Q_HEADS] = 1.0
    live[8:16, 0:REL_TABLE] = 1.0
    small_g = small * jnp.asarray(live)
    sw = pack(ffn1_norm, mix_norm, ffn2_norm, final_norm, sinks, rel_bias)
    sm = pack(m_ffn1_norm, m_mix_norm, m_ffn2_norm, m_final_norm, m_sinks, m_rel_bias)
    sv = pack(v_ffn1_norm, v_mix_norm, v_ffn2_norm, v_final_norm, v_sinks, v_rel_bias)
    (sd,), (snm,), (snv,) = _adamw([sw], [small_g], [sm], [sv], "adamw_small")

    def unpack(p):
        return {"ffn1_norm": p[0], "mix_norm": p[1], "ffn2_norm": p[2], "final_norm": p[3],
                "sinks": p[5, 0:B_Q_HEADS], "rel_bias": p[8:16, 0:REL_TABLE]}

    grads.update(unpack(small_g))
    delta, new_m, new_v = unpack(sd), unpack(snm), unpack(snv)

    wmv = {
        "ffn1_w_gate": (ffn1_w_gate, m_ffn1_w_gate, v_ffn1_w_gate), "ffn1_w_up": (ffn1_w_up, m_ffn1_w_up, v_ffn1_w_up),
        "ffn1_w_down": (ffn1_w_down, m_ffn1_w_down, v_ffn1_w_down), "w_in": (w_in, m_w_in, v_w_in),
        "w_proj_a": (w_proj_a, m_w_proj_a, v_w_proj_a), "w_proj_b": (w_proj_b, m_w_proj_b, v_w_proj_b),
        "w_out": (w_out, m_w_out, v_w_out),
        "ffn2_w_gate": (ffn2_w_gate, m_ffn2_w_gate, v_ffn2_w_gate), "ffn2_w_up": (ffn2_w_up, m_ffn2_w_up, v_ffn2_w_up),
        "ffn2_w_down": (ffn2_w_down, m_ffn2_w_down, v_ffn2_w_down),
    }
    def adamw_group(gname, names, hosted=()):
        ds_, ms_, vs_ = _adamw([wmv[n][0] for n in names], [grads[n] for n in names], [wmv[n][1] for n in names],
                               [wmv[n][2] for n in names], gname, hosted=hosted)
        for n, d_, m_, v_ in zip(names, ds_, ms_, vs_):
            delta[n], new_m[n], new_v[n] = d_, m_, v_

    adamw_group("adamw_ffn_up", ["ffn1_w_gate", "ffn1_w_up", "ffn2_w_gate", "ffn2_w_up"])
    adamw_group("adamw_rest", ["ffn1_w_down", "ffn2_w_down", "w_in", "w_proj_a", "w_proj_b", "w_out"])

    order = ["ffn1_norm", "ffn1_w_gate", "ffn1_w_up", "ffn1_w_down", "mix_norm", "w_in", "rel_bias", "sinks",
             "w_proj_a", "w_proj_b", "w_out", "ffn2_norm", "ffn2_w_gate", "ffn2_w_up", "ffn2_w_down", "final_norm"]
    grad_x = dx0.reshape(bsz, s_len, D_MODEL)
    return (loss, grad_x, *[grads[n] for n in order], *[delta[n] for n in order], *[new_m[n] for n in order],
            *[new_v[n] for n in order])
```

```python
import numpy as np
import jax
import jax.numpy as jnp
from jax import lax
from jax.experimental import pallas as pl
from jax.experimental.pallas import tpu as pltpu

F32 = jnp.float32
BF16 = jnp.bfloat16

D_MODEL = 1024
D_FF = 2816
CHUNK = 64
D_HEAD = 64
A_HEADS = 8
A_PREV = 8
MAX_REL = 128
B_Q_HEADS = 8
B_KV_HEADS = 2
B_GROUP = B_Q_HEADS // B_KV_HEADS
B_PREV = 2
REL_TABLE = (CHUNK - 1) + MAX_REL + 1
A_WIDTH = A_HEADS * D_HEAD
B_Q_WIDTH = B_Q_HEADS * D_HEAD
B_KV_WIDTH = B_KV_HEADS * D_HEAD
QKV_A = 3 * A_WIDTH
QKV_B = B_Q_WIDTH + 2 * B_KV_WIDTH
IN_WIDTH = QKV_A + QKV_B + 2 * D_MODEL
EPS = 1e-6
NEG_INF = -1e30
SCALE = 1.0 / 8.0

ADAM_LR = 0.001
ADAM_B1 = 0.9
ADAM_B2 = 0.999
ADAM_EPS = 1e-08
ADAM_WD = 0.01
ADAM_STEP = 10

N_DEV = 8
N_CHIP = 4
MESH = pl.DeviceIdType.MESH

LANES = 128
TQ = 256
TM = 256
TM_FWD = 256
FC = 256
VMEM_LIMIT = 56 << 20


def _cparams(sem, vmem=VMEM_LIMIT):
    return pltpu.CompilerParams(dimension_semantics=sem, vmem_limit_bytes=vmem)


def _dot_nt(a, b):
    return lax.dot_general(a, b, (((1,), (1,)), ((), ())), preferred_element_type=F32)


def _dot_nn(a, b):
    return lax.dot_general(a, b, (((1,), (0,)), ((), ())), preferred_element_type=F32)


def _dot_tn(a, b):
    return lax.dot_general(a, b, (((0,), (0,)), ((), ())), preferred_element_type=F32)


def _resident(shape):
    nd = len(shape)
    return pl.BlockSpec(shape, lambda *_: (0,) * nd, pipeline_mode=pl.Buffered(1))


def _rows(tm, width):
    return pl.BlockSpec((tm, width), lambda i: (i, 0))


def _colsum8(v):
    tm, n = v.shape
    return jnp.sum(v.reshape(tm // 8, 8, n), axis=0)


def _rms(x):
    r = lax.rsqrt(jnp.mean(x * x, axis=-1, keepdims=True) + EPS)
    return x * r, r


def _rms_bwd(dh, xh, r, gamma):
    dxh = dh * gamma
    dx = r * (dxh - xh * jnp.mean(dxh * xh, axis=-1, keepdims=True))
    return dx, _colsum8(dh * xh)


def _hbm():
    return pl.BlockSpec(memory_space=pltpu.HBM)


def _call(body, *, name, grid, in_specs, out_specs, out_shape, args, sem, scratch_shapes=(), hosted=()):
    in_specs, out_specs, out_shape = list(in_specs), list(out_specs), list(out_shape)
    scratch_shapes = list(scratch_shapes)
    if not hosted:
        return pl.pallas_call(body, name=name, grid=grid, in_specs=in_specs, out_specs=out_specs, out_shape=out_shape,
                              scratch_shapes=scratch_shapes, compiler_params=_cparams(sem))(*args)
    n_in, n_out, n_scr = len(in_specs), len(out_specs), len(scratch_shapes)
    x_in = [a for x in hosted for a in x.inputs]
    x_out = [s for x in hosted for s in x.out_shape]
    x_scr = [s for x in hosted for s in x.scratch]
    steps = int(np.prod(grid))
    forward_step = max(steps - 3, 0)

    def wrapped(*refs):
        pos = [0]

        def take(k):
            pos[0] += k
            return refs[pos[0] - k:pos[0]]

        ins, xin, outs, xout, scr, xscr = (take(k) for k in (n_in, len(x_in), n_out, len(x_out), n_scr, len(x_scr)))
        step = 0
        for axis, extent in enumerate(grid):
            step = step * extent + pl.program_id(axis)
        own, oi, oo, osc = [], 0, 0, 0
        for x in hosted:
            own.append((xin[oi:oi + len(x.inputs)], xout[oo:oo + len(x.out_shape)], xscr[osc:osc + len(x.scratch)]))
            oi, oo, osc = oi + len(x.inputs), oo + len(x.out_shape), osc + len(x.scratch)

        def phase(method):
            for x, (i_, o_, s_) in zip(hosted, own):
                getattr(x, method)(i_, o_, s_)

        pl.when(step == 0)(lambda: phase("start"))
        body(*ins, *outs, *scr)
        pl.when(step == forward_step)(lambda: phase("forward"))
        pl.when(step == steps - 1)(lambda: phase("finish"))

    res = pl.pallas_call(
        wrapped, name=name, grid=grid, in_specs=in_specs + [_hbm()] * len(x_in),
        out_specs=out_specs + [_hbm()] * len(x_out), out_shape=out_shape + x_out,
        scratch_shapes=scratch_shapes + x_scr, compiler_params=_cparams(("arbitrary",) * len(grid)))(*args, *x_in)
    rest = list(res[n_out:])
    for x in hosted:
        x.result, rest = rest[:len(x.out_shape)], rest[len(x.out_shape):]
    return list(res[:n_out])


def _ffn_fwd(x, gamma, wg_t, wu_t, wd, name, hosted=()):
    t = x.shape[0]
    f = wg_t.shape[0]

    def body(x_ref, gam_ref, wg_ref, wu_ref, wd_ref, h_ref, g_ref, u_ref, a_ref, y_ref):
        xv = x_ref[...]
        xh, _ = _rms(xv)
        h = (xh * gam_ref[...]).astype(BF16)
        h_ref[...] = h
        for j in range(f // FC):
            sl = slice(j * FC, (j + 1) * FC)
            g = _dot_nt(h, wg_ref[sl, :])
            u = _dot_nt(h, wu_ref[sl, :])
            g_ref[:, sl] = g.astype(BF16)
            u_ref[:, sl] = u.astype(BF16)
            a_ref[:, sl] = (g * jax.nn.sigmoid(g) * u).astype(BF16)
        y_ref[...] = xv + 0.5 * _dot_nn(a_ref[...], wd_ref[...])

    return _call(
        body,
        name=name,
        grid=(t // TM_FWD,),
        in_specs=[_rows(TM_FWD, D_MODEL), _resident((1, D_MODEL)), _resident((f, D_MODEL)), _resident((f, D_MODEL)),
                  _resident((f, D_MODEL))],
        out_specs=[_rows(TM_FWD, D_MODEL), _rows(TM_FWD, f), _rows(TM_FWD, f), _rows(TM_FWD, f),
                   _rows(TM_FWD, D_MODEL)],
        out_shape=[jax.ShapeDtypeStruct((t, D_MODEL), BF16), jax.ShapeDtypeStruct((t, f), BF16),
                   jax.ShapeDtypeStruct((t, f), BF16), jax.ShapeDtypeStruct((t, f), BF16),
                   jax.ShapeDtypeStruct((t, D_MODEL), F32)],
        args=(x, gamma, wg_t, wu_t, wd), sem=("parallel",), hosted=hosted)


def _ffn_bwd(d, x, gamma, g_act, u_act, wg_t, wu_t, wd, name, hosted=()):
    t = x.shape[0]
    f = wg_t.shape[0]

    def body(d_ref, x_ref, gam_ref, g_ref, u_ref, wg_ref, wu_ref, wd_ref, dx_ref, dg_ref, du_ref, db_ref, dgam_ref):
        dv = d_ref[...]
        db = (0.5 * dv).astype(BF16)
        db_ref[...] = db
        for j in range(f // FC):
            sl = slice(j * FC, (j + 1) * FC)
            da = _dot_nt(db, wd_ref[sl, :])
            g = g_ref[:, sl].astype(F32)
            u = u_ref[:, sl].astype(F32)
            s = jax.nn.sigmoid(g)
            dg_ref[:, sl] = (da * u * (s * (1.0 + g * (1.0 - s)))).astype(BF16)
            du_ref[:, sl] = (da * (g * s)).astype(BF16)
        dh = _dot_nn(dg_ref[...], wg_ref[...]) + _dot_nn(du_ref[...], wu_ref[...])
        xh, r = _rms(x_ref[...])
        dxn, dgam = _rms_bwd(dh, xh, r, gam_ref[...])
        dx_ref[...] = dv + dxn

        @pl.when(pl.program_id(0) == 0)
        def _():
            dgam_ref[...] = jnp.zeros_like(dgam_ref)

        dgam_ref[...] += dgam

    return _call(
        body,
        name=name,
        grid=(t // TM,),
        in_specs=[_rows(TM, D_MODEL), _rows(TM, D_MODEL), _resident((1, D_MODEL)), _rows(TM, f), _rows(TM, f),
                  _resident((f, D_MODEL)), _resident((f, D_MODEL)), _resident((f, D_MODEL))],
        out_specs=[_rows(TM, D_MODEL), _rows(TM, f), _rows(TM, f), _rows(TM, D_MODEL),
                   pl.BlockSpec((8, D_MODEL), lambda i: (0, 0))],
        out_shape=[jax.ShapeDtypeStruct((t, D_MODEL), F32), jax.ShapeDtypeStruct((t, f), BF16),
                   jax.ShapeDtypeStruct((t, f), BF16), jax.ShapeDtypeStruct((t, D_MODEL), BF16),
                   jax.ShapeDtypeStruct((8, D_MODEL), F32)],
        args=(d, x, gamma, g_act, u_act, wg_t, wu_t, wd), sem=("arbitrary",), hosted=hosted)


def _ffn_bwd_act(d, g_act, u_act, wd, name, hosted=()):
    t = d.shape[0]
    f = wd.shape[0]

    def body(d_ref, g_ref, u_ref, wd_ref, dg_ref, du_ref):
        db = (0.5 * d_ref[...]).astype(BF16)
        for j in range(f // FC):
            sl = slice(j * FC, (j + 1) * FC)
            da = _dot_nt(db, wd_ref[sl, :])
            g = g_ref[:, sl].astype(F32)
            u = u_ref[:, sl].astype(F32)
            s = jax.nn.sigmoid(g)
            dg_ref[:, sl] = (da * u * (s * (1.0 + g * (1.0 - s)))).astype(BF16)
            du_ref[:, sl] = (da * (g * s)).astype(BF16)

    return _call(
        body,
        name=name,
        grid=(t // TM,),
        in_specs=[_rows(TM, D_MODEL), _rows(TM, f), _rows(TM, f), _resident((f, D_MODEL))],
        out_specs=[_rows(TM, f), _rows(TM, f)],
        out_shape=[jax.ShapeDtypeStruct((t, f), BF16), jax.ShapeDtypeStruct((t, f), BF16)],
        args=(d, g_act, u_act, wd), sem=("parallel",), hosted=hosted)


def _ffn_bwd_in(d, x, gamma, dg, du, wg_t, wu_t, name, hosted=()):
    t = x.shape[0]
    f = wg_t.shape[0]

    def body(d_ref, x_ref, gam_ref, dg_ref, du_ref, wg_ref, wu_ref, dx_ref, dgam_ref):
        dh = _dot_nn(dg_ref[...], wg_ref[...]) + _dot_nn(du_ref[...], wu_ref[...])
        xh, r = _rms(x_ref[...])
        dxn, dgam = _rms_bwd(dh, xh, r, gam_ref[...])
        dx_ref[...] = d_ref[...] + dxn

        @pl.when(pl.program_id(0) == 0)
        def _():
            dgam_ref[...] = jnp.zeros_like(dgam_ref)

        dgam_ref[...] += dgam

    return _call(
        body,
        name=name,
        grid=(t // TM,),
        in_specs=[_rows(TM, D_MODEL), _rows(TM, D_MODEL), _resident((1, D_MODEL)), _rows(TM, f), _rows(TM, f),
                  _resident((f, D_MODEL)), _resident((f, D_MODEL))],
        out_specs=[_rows(TM, D_MODEL), pl.BlockSpec((8, D_MODEL), lambda i: (0, 0))],
        out_shape=[jax.ShapeDtypeStruct((t, D_MODEL), F32), jax.ShapeDtypeStruct((8, D_MODEL), F32)],
        args=(d, x, gamma, dg, du, wg_t, wu_t), sem=("arbitrary",), hosted=hosted)


def _mm_tn(pieces, b, name, tile=256, hosted=()):
    t, n = b.shape
    npc = len(pieces)
    counts = [p.shape[1] // tile for p in pieces]
    los = [sum(counts[:k]) for k in range(npc)]
    total = sum(counts)

    def body(*refs):
        a_refs, b_ref, o_ref = refs[:npc], refs[npc], refs[npc + 1]
        i = pl.program_id(0)
        for k in range(npc):
            @pl.when(jnp.logical_and(i >= los[k], i < los[k] + counts[k]))
            def _(k=k):
                o_ref[...] = _dot_tn(a_refs[k][...], b_ref[...]).astype(BF16)

    def a_spec(k):
        return pl.BlockSpec((t, tile), lambda i: (0, jnp.clip(i - los[k], 0, counts[k] - 1)))

    return _call(
        body,
        name=name,
        grid=(total,),
        in_specs=[a_spec(k) for k in range(npc)] + [_resident((t, n))],
        out_specs=[pl.BlockSpec((tile, n), lambda i: (i, 0))],
        out_shape=[jax.ShapeDtypeStruct((total * tile, n), BF16)],
        args=(*pieces, b), sem=("parallel",), hosted=hosted)[0]


def _mm_tn_proj(dya, dyb, oa, ob, tile=256):
    t = dya.shape[0]

    def body(dya_ref, dyb_ref, oa_ref, ob_ref, o_ref):
        o_ref[:, 0:A_WIDTH] = _dot_tn(dya_ref[...], oa_ref[...]).astype(BF16)
        o_ref[:, A_WIDTH:A_WIDTH + B_Q_WIDTH] = _dot_tn(dyb_ref[...], ob_ref[...]).astype(BF16)

    col = pl.BlockSpec((t, tile), lambda i: (0, i))
    return pl.pallas_call(
        body,
        name="grad_proj",
        grid=(D_MODEL // tile,),
        in_specs=[col, col, _resident((t, A_WIDTH)), _resident((t, B_Q_WIDTH))],
        out_specs=pl.BlockSpec((tile, A_WIDTH + B_Q_WIDTH), lambda i: (i, 0)),
        out_shape=jax.ShapeDtypeStruct((D_MODEL, A_WIDTH + B_Q_WIDTH), BF16),
        compiler_params=_cparams(("parallel",)),
    )(dya, dyb, oa, ob)


def _proj_fwd(x, gamma, win_t, hosted=()):
    t = x.shape[0]

    def body(x_ref, gam_ref, w_ref, h_ref, qa_ref, qb_ref, gt_ref):
        xh, _ = _rms(x_ref[...])
        h = (xh * gam_ref[...]).astype(BF16)
        h_ref[...] = h
        for j in range(QKV_A // FC):
            qa_ref[:, j * FC:(j + 1) * FC] = _dot_nt(h, w_ref[j * FC:(j + 1) * FC, :]).astype(BF16)
        for j in range(QKV_B // FC):
            lo = QKV_A + j * FC
            qb_ref[:, j * FC:(j + 1) * FC] = _dot_nt(h, w_ref[lo:lo + FC, :]).astype(BF16)
        for j in range(2 * D_MODEL // FC):
            lo = QKV_A + QKV_B + j * FC
            gt_ref[:, j * FC:(j + 1) * FC] = _dot_nt(h, w_ref[lo:lo + FC, :])

    return _call(
        body,
        name="proj_fwd",
        grid=(t // TM_FWD,),
        in_specs=[_rows(TM_FWD, D_MODEL), _resident((1, D_MODEL)), _resident((IN_WIDTH, D_MODEL))],
        out_specs=[_rows(TM_FWD, D_MODEL), _rows(TM_FWD, QKV_A), _rows(TM_FWD, QKV_B), _rows(TM_FWD, 2 * D_MODEL)],
        out_shape=[jax.ShapeDtypeStruct((t, D_MODEL), BF16), jax.ShapeDtypeStruct((t, QKV_A), BF16),
                   jax.ShapeDtypeStruct((t, QKV_B), BF16), jax.ShapeDtypeStruct((t, 2 * D_MODEL), F32)],
        args=(x, gamma, win_t), sem=("parallel",), hosted=hosted)


def _proj_bwd(d, x, gamma, pieces, win_t, hosted=()):
    t = x.shape[0]
    npc = len(pieces)
    widths = [p.shape[1] for p in pieces]
    los = [sum(widths[:k]) for k in range(npc)]

    def body(*refs):
        d_ref, x_ref, gam_ref = refs[:3]
        p_refs = refs[3:3 + npc]
        w_ref, dx_ref, db_ref, dgam_ref = refs[3 + npc:]
        dh = _dot_nn(p_refs[0][...], w_ref[0:widths[0], :])
        for k in range(1, npc):
            dh += _dot_nn(p_refs[k][...], w_ref[los[k]:los[k] + widths[k], :])
        xh, r = _rms(x_ref[...])
        dxn, dgam = _rms_bwd(dh, xh, r, gam_ref[...])
        dx = d_ref[...] + dxn
        dx_ref[...] = dx
        db_ref[...] = (0.5 * dx).astype(BF16)

        @pl.when(pl.program_id(0) == 0)
        def _():
            dgam_ref[...] = jnp.zeros_like(dgam_ref)

        dgam_ref[...] += dgam

    return _call(
        body,
        name="proj_bwd",
        grid=(t // TM,),
        in_specs=[_rows(TM, D_MODEL), _rows(TM, D_MODEL), _resident((1, D_MODEL))] + [_rows(TM, w) for w in widths]
        + [_resident((IN_WIDTH, D_MODEL))],
        out_specs=[_rows(TM, D_MODEL), _rows(TM, D_MODEL), pl.BlockSpec((8, D_MODEL), lambda i: (0, 0))],
        out_shape=[jax.ShapeDtypeStruct((t, D_MODEL), F32), jax.ShapeDtypeStruct((t, D_MODEL), BF16),
                   jax.ShapeDtypeStruct((8, D_MODEL), F32)],
        args=(d, x, gamma, *pieces, win_t), sem=("arbitrary",), hosted=hosted)


def _lane_half(shape):
    return lax.broadcasted_iota(jnp.int32, shape, len(shape) - 1) // D_HEAD


def _band_softmax(q, kk, bias, sink, qs, pad):
    s = _dot_nt(q, kk) + bias
    if qs is not None:
        col = lax.broadcasted_iota(jnp.int32, s.shape, 1)
        s = jnp.where(col + qs >= pad, s, NEG_INF)
    m = jnp.max(s, axis=-1, keepdims=True)
    if sink is not None:
        m = jnp.maximum(m, sink)
    p = jnp.exp(s - m)
    den = jnp.sum(p, axis=-1, keepdims=True)
    if sink is not None:
        den = den + jnp.exp(sink - m)
    return p, m, 1.0 / den


def _fill_padded(dst, src, pad):
    dst[0:pad, :] = jnp.zeros((pad,) + dst.shape[1:], dst.dtype)
    dst[pad:, :] = src


FWD_PAIRS = 4
BWD_PAIRS = 2


def _attn_a_fwd(qkv, bias, hosted=()):
    bsz, s_len, _ = qkv.shape
    pad = A_PREV * CHUNK
    band = TQ + pad
    pp = FWD_PAIRS
    w = pp * LANES
    nb = A_WIDTH // w

    def body(q_ref, k_ref, v_ref, b_ref, o_ref, kp, vp):
        i = pl.program_id(2)

        @pl.when(i == 0)
        def _():
            _fill_padded(kp, k_ref[...], pad)
            _fill_padded(vp, v_ref[...], pad)

        qs = pl.multiple_of(i * TQ, TQ)
        half = _lane_half((1, LANES))

        def block(masked):
            for pr in range(pp):
                sl = slice(pr * LANES, (pr + 1) * LANES)
                kk = kp[pl.ds(qs, band), sl]
                vv = vp[pl.ds(qs, band), sl]
                q = q_ref[:, sl] * SCALE
                outs = []
                for j in range(2):
                    qm = jnp.where(half == j, q, jnp.zeros_like(q))
                    p, _, inv = _band_softmax(qm, kk, b_ref[2 * pr + j], None, qs if masked else None, pad)
                    outs.append(_dot_nn(p.astype(BF16), vv) * inv)
                o_ref[:, sl] = jnp.where(half == 0, outs[0], outs[1]).astype(BF16)

        pl.when(i < pad // TQ)(lambda: block(True))
        pl.when(i >= pad // TQ)(lambda: block(False))

    return _call(
        body,
        name="attn_a_fwd",
        grid=(bsz, nb, s_len // TQ),
        in_specs=[pl.BlockSpec((None, TQ, w), lambda b, g, i: (b, i, g)),
                  pl.BlockSpec((None, s_len, w), lambda b, g, i: (b, 0, nb + g)),
                  pl.BlockSpec((None, s_len, w), lambda b, g, i: (b, 0, 2 * nb + g)),
                  pl.BlockSpec((2 * pp, TQ, band), lambda b, g, i: (g, 0, 0))],
        out_specs=[pl.BlockSpec((None, TQ, w), lambda b, g, i: (b, i, g))],
        out_shape=[jax.ShapeDtypeStruct((bsz, s_len, A_WIDTH), BF16)],
        scratch_shapes=[pltpu.VMEM((pad + s_len, w), BF16), pltpu.VMEM((pad + s_len, w), BF16)],
        args=(qkv, qkv, qkv, bias), sem=("arbitrary", "arbitrary", "arbitrary"), hosted=hosted)[0]


def _attn_a_bwd(qkv, bias, do, hosted=()):
    bsz, s_len, _ = qkv.shape
    pad = A_PREV * CHUNK
    band = TQ + pad
    n_i = s_len // TQ
    pp = BWD_PAIRS
    w = pp * LANES
    nb = A_WIDTH // w

    def body(q_ref, k_ref, v_ref, b_ref, do_ref, dq_ref, dk_ref, dv_ref, dbias_ref, kp, vp, dk_acc, dv_acc):
        b = pl.program_id(1)
        i = pl.program_id(2)

        @pl.when(i == 0)
        def _():
            _fill_padded(kp, k_ref[...], pad)
            _fill_padded(vp, v_ref[...], pad)
            dk_acc[...] = jnp.zeros_like(dk_acc)
            dv_acc[...] = jnp.zeros_like(dv_acc)

        @pl.when(jnp.logical_and(b == 0, i == 0))
        def _():
            dbias_ref[...] = jnp.zeros_like(dbias_ref)

        qs = pl.multiple_of(i * TQ, TQ)
        half = _lane_half((1, LANES))

        def block(masked):
            for pr in range(pp):
                sl = slice(pr * LANES, (pr + 1) * LANES)
                kk = kp[pl.ds(qs, band), sl]
                vv = vp[pl.ds(qs, band), sl]
                q = q_ref[:, sl] * SCALE
                dd = do_ref[:, sl]
                dqs, dks, dvs = [], [], []
                for j in range(2):
                    qm = jnp.where(half == j, q, jnp.zeros_like(q))
                    dm = jnp.where(half == j, dd, jnp.zeros_like(dd))
                    p, _, inv = _band_softmax(qm, kk, b_ref[2 * pr + j], None, qs if masked else None, pad)
                    pn = p * inv
                    dp = _dot_nt(dm, vv)
                    delta = jnp.sum(pn * dp, axis=-1, keepdims=True)
                    ds = pn * (dp - delta)
                    dbias_ref[2 * pr + j] += ds[:, band - REL_COLS:]
                    dsb = ds.astype(BF16)
                    dqs.append(_dot_nn(dsb, kk))
                    dks.append(_dot_tn(dsb, q))
                    dvs.append(_dot_tn(pn.astype(BF16), dd))
                dq_ref[:, sl] = (jnp.where(half == 0, dqs[0], dqs[1]) * SCALE).astype(BF16)
                dk_acc[pl.ds(qs, band), sl] += jnp.where(half == 0, dks[0], dks[1])
                dv_acc[pl.ds(qs, band), sl] += jnp.where(half == 0, dvs[0], dvs[1])

        pl.when(i < pad // TQ)(lambda: block(True))
        pl.when(i >= pad // TQ)(lambda: block(False))

        @pl.when(i == n_i - 1)
        def _():
            dk_ref[...] = dk_acc[pad:, :].astype(BF16)
            dv_ref[...] = dv_acc[pad:, :].astype(BF16)

    qspec = pl.BlockSpec((None, TQ, w), lambda g, b, i: (b, i, g))
    kvout = pl.BlockSpec((None, s_len, w), lambda g, b, i: (b, 0, g))
    wide = jax.ShapeDtypeStruct((bsz, s_len, A_WIDTH), BF16)
    return _call(
        body,
        name="attn_a_bwd",
        grid=(nb, bsz, n_i),
        in_specs=[qspec,
                  pl.BlockSpec((None, s_len, w), lambda g, b, i: (b, 0, nb + g)),
                  pl.BlockSpec((None, s_len, w), lambda g, b, i: (b, 0, 2 * nb + g)),
                  pl.BlockSpec((2 * pp, TQ, band), lambda g, b, i: (g, 0, 0)),
                  qspec],
        out_specs=[qspec, kvout, kvout, pl.BlockSpec((2 * pp, TQ, REL_COLS), lambda g, b, i: (g, 0, 0))],
        out_shape=[wide, wide, wide, jax.ShapeDtypeStruct((A_HEADS, TQ, REL_COLS), F32)],
        scratch_shapes=[pltpu.VMEM((pad + s_len, w), BF16), pltpu.VMEM((pad + s_len, w), BF16),
                        pltpu.VMEM((pad + s_len, w), F32), pltpu.VMEM((pad + s_len, w), F32)],
        args=(qkv, qkv, qkv, bias, do), sem=("arbitrary", "arbitrary", "arbitrary"), hosted=hosted)


def _fill_padded_dup(dst, src, pad, h, half):
    other = pltpu.roll(src, D_HEAD, 1)
    _fill_padded(dst, jnp.where(half == h, src, other), pad)


def _attn_b_fwd(qkv, bias, sink):
    bsz, s_len, _ = qkv.shape
    pad = B_PREV * CHUNK
    band = TQ + pad
    kcol = B_Q_WIDTH // LANES
    npair = B_Q_HEADS // 2

    def body(q_ref, k_ref, v_ref, b_ref, s_ref, o_ref, kp, vp):
        i = pl.program_id(1)
        half = _lane_half((1, LANES))

        @pl.when(i == 0)
        def _():
            for h in range(B_KV_HEADS):
                _fill_padded_dup(kp.at[h], k_ref[...], pad, h, half)
                _fill_padded_dup(vp.at[h], v_ref[...], pad, h, half)

        qs = pl.multiple_of(i * TQ, TQ)

        def block(masked):
            for pr in range(npair):
                h = pr // (B_GROUP // 2)
                sl = slice(pr * LANES, (pr + 1) * LANES)
                kk = kp[h, pl.ds(qs, band), :]
                vv = vp[h, pl.ds(qs, band), :]
                q = q_ref[:, sl] * SCALE
                outs = []
                for j in range(2):
                    qm = jnp.where(half == j, q, jnp.zeros_like(q))
                    p, _, inv = _band_softmax(qm, kk, b_ref[2 * pr + j], s_ref[2 * pr + j][0:1, 0:1],
                                              qs if masked else None, pad)
                    outs.append(_dot_nn(p.astype(BF16), vv) * inv)
                o_ref[:, sl] = jnp.where(half == 0, outs[0], outs[1]).astype(BF16)

        pl.when(i < -(-pad // TQ))(lambda: block(True))
        pl.when(i >= -(-pad // TQ))(lambda: block(False))

    return pl.pallas_call(
        body,
        name="attn_b_fwd",
        grid=(bsz, s_len // TQ),
        in_specs=[pl.BlockSpec((None, TQ, B_Q_WIDTH), lambda b, i: (b, i, 0)),
                  pl.BlockSpec((None, s_len, LANES), lambda b, i: (b, 0, kcol)),
                  pl.BlockSpec((None, s_len, LANES), lambda b, i: (b, 0, kcol + 1)),
                  pl.BlockSpec((B_Q_HEADS, TQ, band), lambda b, i: (0, 0, 0)),
                  pl.BlockSpec((B_Q_HEADS, 8, LANES), lambda b, i: (0, 0, 0))],
        out_specs=pl.BlockSpec((None, TQ, B_Q_WIDTH), lambda b, i: (b, i, 0)),
        out_shape=jax.ShapeDtypeStruct((bsz, s_len, B_Q_WIDTH), BF16),
        scratch_shapes=[pltpu.VMEM((B_KV_HEADS, pad + s_len, LANES), BF16),
                        pltpu.VMEM((B_KV_HEADS, pad + s_len, LANES), BF16)],
        compiler_params=_cparams(("arbitrary", "arbitrary")),
    )(qkv, qkv, qkv, bias, sink)


def _attn_b_bwd(qkv, bias, sink, do, hosted=()):
    bsz, s_len, _ = qkv.shape
    pad = B_PREV * CHUNK
    band = TQ + pad
    kcol = B_Q_WIDTH // LANES
    npair = B_Q_HEADS // 2
    n_i = s_len // TQ

    pp = B_GROUP // 2
    w = pp * LANES

    def body(q_ref, k_ref, v_ref, b_ref, s_ref, do_ref, dq_ref, dkv_ref, dsink_ref, kp, vp, dk_acc, dv_acc):
        b = pl.program_id(0)
        h = pl.program_id(1)
        i = pl.program_id(2)
        half = _lane_half((1, LANES))

        @pl.when(i == 0)
        def _():
            _fill_padded_dup(kp, k_ref[...], pad, h, half)
            _fill_padded_dup(vp, v_ref[...], pad, h, half)

        @pl.when(jnp.logical_and(h == 0, i == 0))
        def _():
            dk_acc[...] = jnp.zeros_like(dk_acc)
            dv_acc[...] = jnp.zeros_like(dv_acc)

        @pl.when(jnp.logical_and(b == 0, jnp.logical_and(h == 0, i == 0)))
        def _():
            dsink_ref[...] = jnp.zeros_like(dsink_ref)

        qs = pl.multiple_of(i * TQ, TQ)

        def block(masked):
            kk = kp[pl.ds(qs, band), :]
            vv = vp[pl.ds(qs, band), :]
            dk2 = jnp.zeros((band, LANES), F32)
            dv2 = jnp.zeros((band, LANES), F32)
            for pr in range(pp):
                sl = slice(pr * LANES, (pr + 1) * LANES)
                q = q_ref[:, sl] * SCALE
                dd = do_ref[:, sl]
                dqs, dks, dvs = [], [], []
                for j in range(2):
                    qm = jnp.where(half == j, q, jnp.zeros_like(q))
                    dm = jnp.where(half == j, dd, jnp.zeros_like(dd))
                    sink = s_ref[2 * pr + j][0:1, 0:1]
                    p, m, inv = _band_softmax(qm, kk, b_ref[2 * pr + j], sink, qs if masked else None, pad)
                    pn = p * inv
                    dp = _dot_nt(dm, vv)
                    delta = jnp.sum(pn * dp, axis=-1, keepdims=True)
                    ds = pn * (dp - delta)
                    dsb = ds.astype(BF16)
                    dqs.append(_dot_nn(dsb, kk))
                    dks.append(_dot_tn(dsb, q))
                    dvs.append(_dot_tn(pn.astype(BF16), dd))
                    dsk = jnp.sum(-(jnp.exp(sink - m) * inv) * delta, axis=0, keepdims=True)
                    dsink_ref[2 * pp * h + 2 * pr + j] += jnp.broadcast_to(dsk, (8, LANES))
                dq_ref[:, sl] = (jnp.where(half == 0, dqs[0], dqs[1]) * SCALE).astype(BF16)
                dk2 = dk2 + jnp.where(half == 0, dks[0], dks[1])
                dv2 = dv2 + jnp.where(half == 0, dvs[0], dvs[1])
            dk_acc[pl.ds(qs, band), :] += jnp.where(half == h, dk2 + pltpu.roll(dk2, D_HEAD, 1), 0.0)
            dv_acc[pl.ds(qs, band), :] += jnp.where(half == h, dv2 + pltpu.roll(dv2, D_HEAD, 1), 0.0)

        pl.when(i < -(-pad // TQ))(lambda: block(True))
        pl.when(i >= -(-pad // TQ))(lambda: block(False))

        @pl.when(jnp.logical_and(h == B_KV_HEADS - 1, i == n_i - 1))
        def _():
            dkv_ref[:, 0:LANES] = dk_acc[pad:, :].astype(BF16)
            dkv_ref[:, LANES:2 * LANES] = dv_acc[pad:, :].astype(BF16)

    qspec = pl.BlockSpec((None, TQ, w), lambda b, h, i: (b, i, h))
    return _call(
        body,
        name="attn_b_bwd",
        grid=(bsz, B_KV_HEADS, n_i),
        in_specs=[qspec,
                  pl.BlockSpec((None, s_len, LANES), lambda b, h, i: (b, 0, kcol)),
                  pl.BlockSpec((None, s_len, LANES), lambda b, h, i: (b, 0, kcol + 1)),
                  pl.BlockSpec((2 * pp, TQ, band), lambda b, h, i: (h, 0, 0)),
                  pl.BlockSpec((2 * pp, 8, LANES), lambda b, h, i: (h, 0, 0)),
                  qspec],
        out_specs=[qspec, pl.BlockSpec((None, s_len, 2 * LANES), lambda b, h, i: (b, 0, 0)),
                   pl.BlockSpec((B_Q_HEADS, 8, LANES), lambda b, h, i: (0, 0, 0))],
        out_shape=[jax.ShapeDtypeStruct((bsz, s_len, B_Q_WIDTH), BF16),
                   jax.ShapeDtypeStruct((bsz, s_len, 2 * B_KV_WIDTH), BF16),
                   jax.ShapeDtypeStruct((B_Q_HEADS, 8, LANES), F32)],
        scratch_shapes=[pltpu.VMEM((pad + s_len, LANES), BF16), pltpu.VMEM((pad + s_len, LANES), BF16),
                        pltpu.VMEM((pad + s_len, LANES), F32), pltpu.VMEM((pad + s_len, LANES), F32)],
        args=(qkv, qkv, qkv, bias, sink, do), sem=("arbitrary", "arbitrary", "arbitrary"), hosted=hosted)


REL_COLS = 3 * 128
REL_WRAP = 512


def _bias_a_build(tv):
    h = tv.shape[0]
    pad = A_PREV * CHUNK
    band = TQ + pad

    def body(tv_ref, o_ref):
        row = tv_ref[...]
        x = jnp.broadcast_to(row, (TQ, REL_WRAP))
        r = lax.broadcasted_iota(jnp.int32, x.shape, 0)
        for bit in range(8):
            sh = 1 << bit
            x = jnp.where((r & sh) != 0, pltpu.roll(x, sh, 1), x)
        far = jnp.broadcast_to(row[:, 0:1], (TQ, band - REL_COLS))
        full = jnp.concatenate([far, x[:, REL_WRAP // 2:REL_WRAP], x[:, 0:REL_COLS - REL_WRAP // 2]], axis=1)
        qc = (lax.broadcasted_iota(jnp.int32, full.shape, 0) + pad) // CHUNK
        kc = lax.broadcasted_iota(jnp.int32, full.shape, 1) // CHUNK
        ok = jnp.logical_and(kc <= qc, kc >= qc - A_PREV)
        o_ref[...] = jnp.where(ok, full, NEG_INF)

    return pl.pallas_call(
        body,
        name="bias_a_build",
        grid=(h,),
        in_specs=[pl.BlockSpec((None, 1, REL_WRAP), lambda hh: (hh, 0, 0))],
        out_specs=pl.BlockSpec((None, TQ, band), lambda hh: (hh, 0, 0)),
        out_shape=jax.ShapeDtypeStruct((h, TQ, band), F32),
        compiler_params=_cparams(("parallel",)),
    )(tv)


def _relbias_grad(dbias):
    h, rows, _ = dbias.shape

    def body(d_ref, o_ref):
        x = d_ref[...]
        r = lax.broadcasted_iota(jnp.int32, x.shape, 0)
        c = lax.broadcasted_iota(jnp.int32, x.shape, 1) - r
        x = jnp.where(jnp.logical_and(c >= 1, c < REL_TABLE), x, 0.0)
        for bit in range(8):
            sh = 1 << bit
            x = jnp.where((r & sh) != 0, pltpu.roll(x, REL_COLS - sh, 1), x)
        diag = jnp.sum(x, axis=0, keepdims=True)
        lane = lax.broadcasted_iota(jnp.int32, diag.shape, 1)
        diag = jnp.where(jnp.logical_and(lane >= 1, lane < REL_TABLE), diag, 0.0)
        rest = -jnp.sum(diag, axis=1, keepdims=True)
        o_ref[...] = jnp.broadcast_to(jnp.where(lane == 0, rest, diag), o_ref.shape)

    return pl.pallas_call(
        body,
        name="relbias_grad",
        grid=(h,),
        in_specs=[pl.BlockSpec((None, rows, REL_COLS), lambda hh: (hh, 0, 0))],
        out_specs=pl.BlockSpec((None, 8, REL_COLS), lambda hh: (hh, 0, 0)),
        out_shape=jax.ShapeDtypeStruct((h, 8, REL_COLS), F32),
        compiler_params=_cparams(("parallel",)),
    )(dbias)


def _mix_out_fwd(x, oa, ob, gates, proj_t, wout):
    t = x.shape[0]

    def body(x_ref, oa_ref, ob_ref, gt_ref, pt_ref, wo_ref, y_ref, ya_ref, yb_ref, mg_ref):
        ya = _dot_nt(oa_ref[...], pt_ref[:, 0:A_WIDTH])
        yb = _dot_nt(ob_ref[...], pt_ref[:, A_WIDTH:A_WIDTH + B_Q_WIDTH])
        ya_ref[...] = ya.astype(BF16)
        yb_ref[...] = yb.astype(BF16)
        mg = jax.nn.sigmoid(gt_ref[:, 0:D_MODEL]) * ya + jax.nn.sigmoid(gt_ref[:, D_MODEL:2 * D_MODEL]) * yb
        mgb = mg.astype(BF16)
        mg_ref[...] = mgb
        y_ref[...] = x_ref[...] + _dot_nn(mgb, wo_ref[...])

    return pl.pallas_call(
        body,
        name="mix_out_fwd",
        grid=(t // TM,),
        in_specs=[_rows(TM, D_MODEL), _rows(TM, A_WIDTH), _rows(TM, B_Q_WIDTH), _rows(TM, 2 * D_MODEL),
                  _resident((D_MODEL, A_WIDTH + B_Q_WIDTH)), _resident((D_MODEL, D_MODEL))],
        out_specs=[_rows(TM, D_MODEL), _rows(TM, D_MODEL), _rows(TM, D_MODEL), _rows(TM, D_MODEL)],
        out_shape=[jax.ShapeDtypeStruct((t, D_MODEL), F32), jax.ShapeDtypeStruct((t, D_MODEL), BF16),
                   jax.ShapeDtypeStruct((t, D_MODEL), BF16), jax.ShapeDtypeStruct((t, D_MODEL), BF16)],
        compiler_params=_cparams(("parallel",)),
    )(x, oa, ob, gates, proj_t, wout)


def _mix_out_bwd(d, gates, ya, yb, proj_t, wout, hosted=()):
    t = d.shape[0]

    def body(d_ref, gt_ref, ya_ref, yb_ref, pt_ref, wo_ref, db_ref, dya_ref, dyb_ref, doa_ref, dob_ref, dgt_ref):
        db = d_ref[...].astype(BF16)
        db_ref[...] = db
        dmg = _dot_nt(db, wo_ref[...])
        sa = jax.nn.sigmoid(gt_ref[:, 0:D_MODEL])
        sb = jax.nn.sigmoid(gt_ref[:, D_MODEL:2 * D_MODEL])
        dya = (dmg * sa).astype(BF16)
        dyb = (dmg * sb).astype(BF16)
        dya_ref[...] = dya
        dyb_ref[...] = dyb
        dgt_ref[:, 0:D_MODEL] = (dmg * ya_ref[...].astype(F32) * (sa * (1.0 - sa))).astype(BF16)
        dgt_ref[:, D_MODEL:2 * D_MODEL] = (dmg * yb_ref[...].astype(F32) * (sb * (1.0 - sb))).astype(BF16)
        doa_ref[...] = _dot_nn(dya, pt_ref[:, 0:A_WIDTH]).astype(BF16)
        dob_ref[...] = _dot_nn(dyb, pt_ref[:, A_WIDTH:A_WIDTH + B_Q_WIDTH]).astype(BF16)

    return _call(
        body,
        name="mix_out_bwd",
        grid=(t // TM,),
        in_specs=[_rows(TM, D_MODEL), _rows(TM, 2 * D_MODEL), _rows(TM, D_MODEL), _rows(TM, D_MODEL),
                  _resident((D_MODEL, A_WIDTH + B_Q_WIDTH)), _resident((D_MODEL, D_MODEL))],
        out_specs=[_rows(TM, D_MODEL), _rows(TM, D_MODEL), _rows(TM, D_MODEL), _rows(TM, A_WIDTH),
                   _rows(TM, B_Q_WIDTH), _rows(TM, 2 * D_MODEL)],
        out_shape=[jax.ShapeDtypeStruct((t, D_MODEL), BF16), jax.ShapeDtypeStruct((t, D_MODEL), BF16),
                   jax.ShapeDtypeStruct((t, D_MODEL), BF16), jax.ShapeDtypeStruct((t, A_WIDTH), BF16),
                   jax.ShapeDtypeStruct((t, B_Q_WIDTH), BF16), jax.ShapeDtypeStruct((t, 2 * D_MODEL), BF16)],
        args=(d, gates, ya, yb, proj_t, wout), sem=("parallel",), hosted=hosted)


def _loss_head(x, gamma, target):
    t = x.shape[0]

    def body(x_ref, gam_ref, t_ref, dx_ref, dgam_ref, loss_ref):
        xh, r = _rms(x_ref[...])
        gam = gam_ref[...]
        e = xh * gam - t_ref[...]
        dy = e * (1.0 / D_MODEL)
        dxn, dgam = _rms_bwd(dy, xh, r, gam)
        dx_ref[...] = dxn

        @pl.when(pl.program_id(0) == 0)
        def _():
            dgam_ref[...] = jnp.zeros_like(dgam_ref)
            loss_ref[...] = jnp.zeros_like(loss_ref)

        dgam_ref[...] += dgam
        loss_ref[...] += _colsum8(e * e) * (0.5 / D_MODEL)

    return pl.pallas_call(
        body,
        name="loss_head",
        grid=(t // TM,),
        in_specs=[_rows(TM, D_MODEL), _resident((1, D_MODEL)), _rows(TM, D_MODEL)],
        out_specs=[_rows(TM, D_MODEL), pl.BlockSpec((8, D_MODEL), lambda i: (0, 0)),
                   pl.BlockSpec((8, D_MODEL), lambda i: (0, 0))],
        out_shape=[jax.ShapeDtypeStruct((t, D_MODEL), F32), jax.ShapeDtypeStruct((8, D_MODEL), F32),
                   jax.ShapeDtypeStruct((8, D_MODEL), F32)],
        compiler_params=_cparams(("arbitrary",)),
    )(x, gamma, target)


def _place():
    x, y, c = lax.axis_index("x"), lax.axis_index("y"), lax.axis_index("c")
    chips = [(1 - x, y), (x, 1 - y), (1 - x, 1 - y)]
    return x, y, c, chips


class _Gather:
    per = 7

    def __init__(self, shards):
        n = len(shards)
        self.inputs = list(shards)
        self.out_shape = [jax.ShapeDtypeStruct((N_DEV * s.shape[0], s.shape[1]), s.dtype) for s in shards]
        self.scratch = [pltpu.SemaphoreType.DMA((n * self.per,)), pltpu.SemaphoreType.DMA((n * self.per,)),
                        pltpu.SemaphoreType.DMA((n,))]
        self.result = None

    def _parts(self, ins, outs, sems):
        send_sems, recv_sems, local_sems = sems
        x, y, c, chips = _place()
        me, sibling = (x, y, c), (x, y, 1 - c)
        n = len(ins)

        def rows(k, p):
            r = ins[k].shape[0]
            return outs[k].at[pl.ds((4 * p[0] + 2 * p[1] + p[2]) * r, r), :]

        def copy(k, slot, block, to, src=None):
            return pltpu.make_async_remote_copy(
                src_ref=rows(k, block) if src is None else src, dst_ref=rows(k, block),
                send_sem=send_sems.at[k * self.per + slot], recv_sem=recv_sems.at[k * self.per + slot],
                device_id=to, device_id_type=MESH)

        mine = [pltpu.make_async_copy(ins[k], rows(k, me), local_sems.at[k]) for k in range(n)]
        first = []
        for k in range(n):
            first.append(copy(k, 0, me, sibling, src=ins[k]))
            first += [copy(k, 1 + j, me, (*chip, c), src=ins[k]) for j, chip in enumerate(chips)]
        passed = [copy(k, 4 + j, (*chip, c), sibling) for j, chip in enumerate(chips) for k in range(n)]
        return n, c, me, sibling, chips, copy, mine, first, passed

    def start(self, ins, outs, sems):
        _, _, _, _, _, _, mine, first, _ = self._parts(ins, outs, sems)
        for cp in mine + first:
            cp.start()

    def forward(self, ins, outs, sems):
        n, c, me, _, chips, copy, _, _, passed = self._parts(ins, outs, sems)
        for j, chip in enumerate(chips):
            for k in range(n):
                copy(k, 1 + j, (*chip, c), me).wait_recv()
                passed[j * n + k].start()

    def finish(self, ins, outs, sems):
        n, c, me, sibling, chips, copy, mine, first, passed = self._parts(ins, outs, sems)
        for k in range(n):
            copy(k, 0, sibling, me).wait_recv()
            for j, chip in enumerate(chips):
                copy(k, 4 + j, (*chip, 1 - c), me).wait_recv()
        for cp in first + passed:
            cp.wait_send()
        for cp in mine:
            cp.wait()


def _ffn1_fwd_streamed(x, gamma, shards, xcoord, hosted):
    t = x.shape[0]
    nt = t // TM
    half_f = D_FF // 2
    steps = 2 * nt
    own = _Gather(shards)
    n_own = len(shards)
    x_in = [a for e in hosted for a in e.inputs]
    x_out = [s for e in hosted for s in e.out_shape]
    x_scr = [s for e in hosted for s in e.scratch]
    chunks = [(lo, min(FC, half_f - lo)) for lo in range(0, half_f, FC)]

    def body(xc_ref, x_ref, gam_ref, *refs):
        pos = [0]

        def take(k):
            pos[0] += k
            return refs[pos[0] - k:pos[0]]

        sh = take(n_own)
        xin = take(len(x_in))
        h_ref, g_ref, u_ref, a_ref, y_ref = take(5)
        wout = take(n_own)
        xout = take(len(x_out))
        wv, yacc, wsem = take(3)
        own_sems = take(len(own.scratch))
        xscr = take(len(x_scr))
        grp = pl.program_id(0)
        tt = pl.program_id(1)
        step = grp * nt + tt
        xc = xc_ref[0]
        n, c, me, sibling, chips, copy, mine, first, passed = own._parts(sh, wout, own_sems)
        near, far = 1, (0, 2)

        parts, oi, oo, osc = [], 0, 0, 0
        for e in hosted:
            parts.append((xin[oi:oi + len(e.inputs)], xout[oo:oo + len(e.out_shape)], xscr[osc:osc + len(e.scratch)]))
            oi, oo, osc = oi + len(e.inputs), oo + len(e.out_shape), osc + len(e.scratch)

        def phase(method):
            for e, (i_, o_, s_) in zip(hosted, parts):
                getattr(e, method)(i_, o_, s_)

        def load_half(col):
            cps = [pltpu.make_async_copy(wout[k].at[pl.ds(col * half_f, half_f), :], wv.at[k], wsem.at[k])
                   for k in range(n)]
            for cp in cps:
                cp.start()
            for cp in cps:
                cp.wait()

        @pl.when(step == 0)
        def _():
            for cp in mine:
                cp.start()
            for slot in (0, 1 + near, 1, 3):
                for k in range(n):
                    first[4 * k + slot].start()
            phase("start")
            for cp in mine:
                cp.wait()
            for k in range(n):
                copy(k, 0, sibling, me).wait_recv()
            for k in range(n):
                copy(k, 1 + near, (*chips[near], c), me).wait_recv()
                passed[near * n + k].start()
            for k in range(n):
                copy(k, 4 + near, (*chips[near], 1 - c), me).wait_recv()
            load_half(xc)

        @pl.when(step == nt)
        def _():
            for j in far:
                for k in range(n):
                    copy(k, 1 + j, (*chips[j], c), me).wait_recv()
                    passed[j * n + k].start()
            for j in far:
                for k in range(n):
                    copy(k, 4 + j, (*chips[j], 1 - c), me).wait_recv()
            for cp in first + passed:
                cp.wait_send()
            load_half(1 - xc)

        xv = x_ref[...]
        xh, _ = _rms(xv)
        h = (xh * gam_ref[...]).astype(BF16)

        @pl.when(grp == 0)
        def _():
            h_ref[...] = h

        for lo, wd_ in chunks:
            sl = slice(lo, lo + wd_)
            g = _dot_nt(h, wv[0, sl, :])
            u = _dot_nt(h, wv[1, sl, :])
            g_ref[:, sl] = g.astype(BF16)
            u_ref[:, sl] = u.astype(BF16)
            a_ref[:, sl] = (g * jax.nn.sigmoid(g) * u).astype(BF16)
        part = _dot_nn(a_ref[...], wv[2])
        rows = pl.ds(pl.multiple_of(tt * TM, TM), TM)

        @pl.when(grp == 0)
        def _():
            yacc[rows, :] = part

        @pl.when(grp == 1)
        def _():
            y_ref[...] = xv + 0.5 * (yacc[rows, :] + part)

        pl.when(step == steps - 3)(lambda: phase("forward"))
        pl.when(step == steps - 1)(lambda: phase("finish"))

    def half_cols(g, i, xc_ref):
        return (i, jnp.where(g == 0, xc_ref[0], 1 - xc_ref[0]))

    res = pl.pallas_call(
        body,
        name="ffn1_fwd",
        grid_spec=pltpu.PrefetchScalarGridSpec(
            num_scalar_prefetch=1, grid=(2, nt),
            in_specs=[pl.BlockSpec((TM, D_MODEL), lambda g, i, xc_ref: (i, 0)),
                      pl.BlockSpec((1, D_MODEL), lambda g, i, xc_ref: (0, 0))] + [_hbm()] * (n_own + len(x_in)),
            out_specs=[pl.BlockSpec((TM, D_MODEL), lambda g, i, xc_ref: (jnp.where(g == 0, i, nt - 1), 0)),
                       pl.BlockSpec((TM, half_f), half_cols), pl.BlockSpec((TM, half_f), half_cols),
                       pl.BlockSpec((TM, half_f), half_cols),
                       pl.BlockSpec((TM, D_MODEL), lambda g, i, xc_ref: (jnp.where(g == 0, 0, i), 0))]
            + [_hbm()] * (n_own + len(x_out)),
            scratch_shapes=[pltpu.VMEM((n_own, half_f, D_MODEL), BF16), pltpu.VMEM((t, D_MODEL), F32),
                            pltpu.SemaphoreType.DMA((n_own,))] + own.scratch + x_scr),
        out_shape=[jax.ShapeDtypeStruct((t, D_MODEL), BF16), jax.ShapeDtypeStruct((t, D_FF), BF16),
                   jax.ShapeDtypeStruct((t, D_FF), BF16), jax.ShapeDtypeStruct((t, D_FF), BF16),
                   jax.ShapeDtypeStruct((t, D_MODEL), F32)] + own.out_shape + x_out,
        compiler_params=_cparams(("arbitrary", "arbitrary")),
    )(xcoord, x, gamma, *shards, *x_in)
    rest = list(res[5 + n_own:])
    for e in hosted:
        e.result, rest = rest[:len(e.out_shape)], rest[len(e.out_shape):]
    return list(res[:5]), list(res[5:5 + n_own])


class _PairExchange:
    def __init__(self, grads):
        n = len(grads)
        self.inputs = list(grads)
        self.out_shape = [jax.ShapeDtypeStruct((g.shape[0] // 2, g.shape[1]), g.dtype) for g in grads]
        self.scratch = [pltpu.SemaphoreType.DMA((n * N_CHIP,)), pltpu.SemaphoreType.DMA((n * N_CHIP,))]
        self.result = None

    def _copies(self, ins, outs, sems):
        send_sems, recv_sems = sems
        x, y, c, _ = _place()
        copies = []
        for k in range(len(ins)):
            r = ins[k].shape[0] // N_DEV
            for q in range(N_CHIP):
                copies.append(pltpu.make_async_remote_copy(
                    src_ref=ins[k].at[pl.ds((2 * q + 1 - c) * r, r), :], dst_ref=outs[k].at[pl.ds(q * r, r), :],
                    send_sem=send_sems.at[k * N_CHIP + q], recv_sem=recv_sems.at[k * N_CHIP + q],
                    device_id=(x, y, 1 - c), device_id_type=MESH))
        return copies

    def start(self, ins, outs, sems):
        for cp in self._copies(ins, outs, sems):
            cp.start()

    def forward(self, ins, outs, sems):
        pass

    def finish(self, ins, outs, sems):
        copies = self._copies(ins, outs, sems)
        for cp in copies:
            cp.wait_recv()
        for cp in copies:
            cp.wait_send()


class _ChipExchange(_PairExchange):
    def __init__(self, psums):
        n = len(psums)
        self.inputs = list(psums)
        self.out_shape = [jax.ShapeDtypeStruct((3 * p.shape[0] // N_CHIP, p.shape[1]), p.dtype) for p in psums]
        self.scratch = [pltpu.SemaphoreType.DMA((n * 3,)), pltpu.SemaphoreType.DMA((n * 3,))]
        self.result = None

    def _copies(self, ins, outs, sems):
        send_sems, recv_sems = sems
        _, _, c, chips = _place()
        copies = []
        for k in range(len(ins)):
            r = ins[k].shape[0] // N_CHIP
            for j, chip in enumerate(chips):
                copies.append(pltpu.make_async_remote_copy(
                    src_ref=ins[k].at[pl.ds((2 * chip[0] + chip[1]) * r, r), :], dst_ref=outs[k].at[pl.ds(j * r, r), :],
                    send_sem=send_sems.at[k * 3 + j], recv_sem=recv_sems.at[k * 3 + j],
                    device_id=(*chip, c), device_id_type=MESH))
        return copies


def _exchange_alone(xchg, name):
    n_in, n_out = len(xchg.inputs), len(xchg.out_shape)

    def body(*refs):
        ins, outs, sems = refs[:n_in], refs[n_in:n_in + n_out], refs[n_in + n_out:]
        xchg.start(ins, outs, sems)
        xchg.forward(ins, outs, sems)
        xchg.finish(ins, outs, sems)

    xchg.result = list(pl.pallas_call(
        body, name=name, in_specs=[_hbm()] * n_in, out_specs=[_hbm()] * n_out, out_shape=xchg.out_shape,
        scratch_shapes=xchg.scratch)(*xchg.inputs))
    return xchg.result


def _pair_sum(core, grads, recvd, name):
    n = len(grads)
    r = grads[0].shape[0] // N_DEV
    cdim = grads[0].shape[1]
    tr = r // 2 if r % 32 == 0 else r
    nt = r // tr

    def body(core_ref, *refs):
        del core_ref
        for k in range(n):
            refs[2 * n + k][...] = (refs[k][...].astype(F32) + refs[n + k][...].astype(F32)).astype(BF16)

    gspec = pl.BlockSpec((tr, cdim), lambda q, i, core_ref: ((2 * q + core_ref[0]) * nt + i, 0))
    rspec = pl.BlockSpec((tr, cdim), lambda q, i, core_ref: (q * nt + i, 0))
    return pl.pallas_call(
        body,
        name=name,
        grid_spec=pltpu.PrefetchScalarGridSpec(
            num_scalar_prefetch=1, grid=(N_CHIP, nt), in_specs=[gspec] * n + [rspec] * n, out_specs=[rspec] * n),
        out_shape=[jax.ShapeDtypeStruct((N_CHIP * r, cdim), BF16) for _ in range(n)],
        compiler_params=_cparams(("parallel", "parallel")),
    )(core, *grads, *recvd)


def _final_sum(chip, psums, recvd, name):
    n = len(psums)
    r = psums[0].shape[0] // N_CHIP
    cdim = psums[0].shape[1]
    tr = r // 2 if r % 32 == 0 else r
    nt = r // tr

    def body(chip_ref, *refs):
        del chip_ref
        for k in range(n):
            got = refs[n + k]
            tot = refs[k][...].astype(F32) + got[0].astype(F32)
            tot = tot + got[1].astype(F32)
            tot = tot + got[2].astype(F32)
            refs[2 * n + k][...] = tot

    pspec = pl.BlockSpec((tr, cdim), lambda i, chip_ref: (chip_ref[0] * nt + i, 0))
    rspec = pl.BlockSpec((3, tr, cdim), lambda i, chip_ref: (0, i, 0))
    ospec = pl.BlockSpec((tr, cdim), lambda i, chip_ref: (i, 0))
    return pl.pallas_call(
        body,
        name=name,
        grid_spec=pltpu.PrefetchScalarGridSpec(
            num_scalar_prefetch=1, grid=(nt,), in_specs=[pspec] * n + [rspec] * n, out_specs=[ospec] * n),
        out_shape=[jax.ShapeDtypeStruct((r, cdim), F32) for _ in range(n)],
        compiler_params=_cparams(("parallel",)),
    )(chip, *psums, *[g.reshape(3, r, cdim) for g in recvd])


SMALL_ROWS = 16


def _all_reduce_small(part):
    def body(p_ref, o_ref, buf, send_sems, recv_sems):
        x, y, c, _ = _place()
        me = 4 * x + 2 * y + c
        buf[me] = p_ref[...]
        copies = []
        for d in range(1, N_DEV):
            peer = me ^ d
            copies.append(pltpu.make_async_remote_copy(
                src_ref=p_ref, dst_ref=buf.at[me], send_sem=send_sems.at[d - 1], recv_sem=recv_sems.at[d - 1],
                device_id=(peer // 4, (peer // 2) % 2, peer % 2), device_id_type=MESH))
        for cp in copies:
            cp.start()
        for cp in copies:
            cp.wait_recv()
        for cp in copies:
            cp.wait_send()
        tot = buf[0]
        for d in range(1, N_DEV):
            tot = tot + buf[d]
        o_ref[...] = tot

    return pl.pallas_call(
        body,
        name="all_reduce_small",
        in_specs=[pl.BlockSpec(memory_space=pltpu.VMEM)],
        out_specs=pl.BlockSpec(memory_space=pltpu.VMEM),
        out_shape=jax.ShapeDtypeStruct(part.shape, F32),
        scratch_shapes=[pltpu.VMEM((N_DEV,) + part.shape, F32), pltpu.SemaphoreType.DMA((N_DEV - 1,)),
                        pltpu.SemaphoreType.DMA((N_DEV - 1,))],
    )(part)


ADAMW_STEPS = 4


def _adamw(ws, gs, ms, vs, name, hosted=()):
    n = len(ws)
    steps = ADAMW_STEPS if all(w.shape[0] % (8 * ADAMW_STEPS) == 0 for w in ws) else 1
    c1 = 1.0 - ADAM_B1 ** ADAM_STEP
    c2 = 1.0 - ADAM_B2 ** ADAM_STEP

    def body(*refs):
        for k in range(n):
            w, g, m, v = (refs[j * n + k][...] for j in range(4))
            m2 = ADAM_B1 * m + (1.0 - ADAM_B1) * g
            v2 = ADAM_B2 * v + (1.0 - ADAM_B2) * (g * g)
            delta = -ADAM_LR * ((m2 / c1) / (jnp.sqrt(v2 / c2) + ADAM_EPS) + ADAM_WD * w)
            refs[4 * n + k][...] = delta
            refs[5 * n + k][...] = m2
            refs[6 * n + k][...] = v2

    specs = [pl.BlockSpec((w.shape[0] // steps, w.shape[1]), lambda i: (i, 0)) for w in ws]
    shapes = [jax.ShapeDtypeStruct(w.shape, F32) for w in ws]
    outs = _call(
        body,
        name=name,
        grid=(steps,),
        in_specs=specs * 4,
        out_specs=specs * 3,
        out_shape=shapes * 3,
        args=(*ws, *gs, *ms, *vs), sem=("parallel",), hosted=hosted)
    return outs[:n], outs[n:2 * n], outs[2 * n:]


def _bias_b():
    pad = B_PREV * CHUNK
    slopes = np.array([2.0 ** (-8.0 * (i + 1) / B_Q_HEADS) for i in range(B_Q_HEADS)], dtype=np.float32)
    dist = np.abs(np.arange(TQ)[:, None] - np.arange(TQ + pad)[None, :] + pad).astype(np.float32)
    bias = -slopes.reshape(B_Q_HEADS, 1, 1) * dist[None]
    qc = (np.arange(TQ)[:, None] + pad) // CHUNK
    kc = np.arange(TQ + pad)[None, :] // CHUNK
    allowed = (kc <= qc) & (kc >= qc - B_PREV)
    return np.where(allowed[None], bias, np.float32(NEG_INF)).astype(np.float32)


def kernel(x, ffn1_norm, ffn1_w_gate, ffn1_w_up, ffn1_w_down, mix_norm, w_in, rel_bias, sinks, w_proj_a, w_proj_b, w_out, ffn2_norm, ffn2_w_gate, ffn2_w_up, ffn2_w_down, final_norm, loss_target, m_ffn1_norm, m_ffn1_w_gate, m_ffn1_w_up, m_ffn1_w_down, m_mix_norm, m_w_in, m_rel_bias, m_sinks, m_w_proj_a, m_w_proj_b, m_w_out, m_ffn2_norm, m_ffn2_w_gate, m_ffn2_w_up, m_ffn2_w_down, m_final_norm, v_ffn1_norm, v_ffn1_w_gate, v_ffn1_w_up, v_ffn1_w_down, v_mix_norm, v_w_in, v_rel_bias, v_sinks, v_w_proj_a, v_w_proj_b, v_w_out, v_ffn2_norm, v_ffn2_w_gate, v_ffn2_w_up, v_ffn2_w_down, v_final_norm):
    bsz, s_len, _ = x.shape
    t = bsz * s_len
    core = lax.axis_index("c").astype(jnp.int32).reshape(1)
    chip = (2 * lax.axis_index("x") + lax.axis_index("y")).astype(jnp.int32).reshape(1)

    def row_form(w):
        return w.astype(BF16).T

    gather_win = _Gather([row_form(w_in)])
    gather_out = _Gather([jnp.concatenate([row_form(w_proj_a), row_form(w_proj_b)], axis=1), w_out.astype(BF16)])
    gather_ffn2_gate = _Gather([row_form(ffn2_w_gate)])
    gather_ffn2_rest = _Gather([row_form(ffn2_w_up), ffn2_w_down.astype(BF16)])

    x0 = x.reshape(t, D_MODEL)
    tgt = loss_target.reshape(t, D_MODEL)
    gam1, gam2, gam3, gam4 = (g.reshape(1, D_MODEL) for g in (ffn1_norm, mix_norm, ffn2_norm, final_norm))

    xcoord = lax.axis_index("x").astype(jnp.int32).reshape(1)
    (h1, g1, u1, a1, x1), (wg1, wu1, wd1) = _ffn1_fwd_streamed(
        x0, gam1, [row_form(ffn1_w_gate), row_form(ffn1_w_up), ffn1_w_down.astype(BF16)], xcoord, [gather_win])
    (win_t,) = gather_win.result
    h2, qkv_a, qkv_b, gates = _proj_fwd(x1, gam2, win_t, hosted=[gather_out, gather_ffn2_gate])
    proj_t, wout = gather_out.result
    (wg2,) = gather_ffn2_gate.result
    qkv_a3 = qkv_a.reshape(bsz, s_len, QKV_A)
    qkv_b3 = qkv_b.reshape(bsz, s_len, QKV_B)

    far = jnp.broadcast_to(rel_bias[:, REL_TABLE - 1:REL_TABLE], (A_HEADS, REL_WRAP // 2))
    tv = jnp.concatenate([far, jnp.flip(rel_bias, axis=1), jnp.zeros((A_HEADS, REL_WRAP // 2 - REL_TABLE), F32)], axis=1)
    bias_a = _bias_a_build(tv.reshape(A_HEADS, 1, REL_WRAP))
    bias_b = jnp.asarray(_bias_b())
    sink_rows = jnp.broadcast_to(sinks.reshape(B_Q_HEADS, 1, 1), (B_Q_HEADS, 8, LANES))

    oa = _attn_a_fwd(qkv_a3, bias_a, hosted=[gather_ffn2_rest]).reshape(t, A_WIDTH)
    wu2, wd2 = gather_ffn2_rest.result
    ob = _attn_b_fwd(qkv_b3, bias_b, sink_rows).reshape(t, B_Q_WIDTH)
    x2, ya, yb, mg = _mix_out_fwd(x1, oa, ob, gates, proj_t, wout)
    h3, g2, u2, a2, x3 = _ffn_fwd(x2, gam3, wg2, wu2, wd2, "ffn2_fwd")

    dx3, dgam4, loss_part = _loss_head(x3, gam4, tgt)

    dx2, dg2, du2, db2, dgam3 = _ffn_bwd(dx3, x2, gam3, g2, u2, wg2, wu2, wd2, "ffn2_bwd")
    gw_ffn2 = [_mm_tn([dg2], h3, "grad_ffn2_gate"), _mm_tn([du2], h3, "grad_ffn2_up"),
               _mm_tn([a2], db2, "grad_ffn2_down")]
    pairx_ffn2 = _PairExchange(gw_ffn2)
    dxb, dya, dyb, doa, dob, dgates = _mix_out_bwd(dx2, gates, ya, yb, proj_t, wout, hosted=[pairx_ffn2])
    psum_ffn2 = _pair_sum(core, gw_ffn2, pairx_ffn2.result, "pair_sum_ffn2")
    gw_out = _mm_tn([mg], dxb, "grad_w_out")
    gw_proj = _mm_tn_proj(dya, dyb, oa, ob)

    chipx_ffn2 = _ChipExchange(psum_ffn2)
    dqa, dka, dva, dbias_a = _attn_a_bwd(qkv_a3, bias_a, doa.reshape(bsz, s_len, A_WIDTH), hosted=[chipx_ffn2])
    pairx_out = _PairExchange([gw_proj, gw_out])
    dqb, dkvb, dsink = _attn_b_bwd(qkv_b3, bias_b, sink_rows, dob.reshape(bsz, s_len, B_Q_WIDTH), hosted=[pairx_out])
    drel_lanes = _relbias_grad(dbias_a)
    dproj = [dqa.reshape(t, A_WIDTH), dka.reshape(t, A_WIDTH), dva.reshape(t, A_WIDTH), dqb.reshape(t, B_Q_WIDTH),
             dkvb.reshape(t, 2 * B_KV_WIDTH), dgates]

    gw_in = _mm_tn(dproj, h2, "grad_w_in")
    pairx_in = _PairExchange([gw_in])
    psum_out = _pair_sum(core, [gw_proj, gw_out], pairx_out.result, "pair_sum_mix")
    chipx_out = _ChipExchange(psum_out)
    dx1, db1, dgam2 = _proj_bwd(dx2, x1, gam2, dproj, win_t, hosted=[pairx_in, chipx_out])
    psum_in = _pair_sum(core, [gw_in], pairx_in.result, "pair_sum_w_in")
    gw_d1 = _mm_tn([a1], db1, "grad_ffn1_down")

    chipx_in = _ChipExchange(psum_in)
    pairx_d1 = _PairExchange([gw_d1])
    dg1, du1 = _ffn_bwd_act(dx1, g1, u1, wd1, "ffn1_bwd_act", hosted=[chipx_in, pairx_d1])
    psum_d1 = _pair_sum(core, [gw_d1], pairx_d1.result, "pair_sum_ffn1_down")
    chipx_d1 = _ChipExchange(psum_d1)
    gw_g1 = _mm_tn([dg1], h1, "grad_ffn1_gate", hosted=[chipx_d1])
    from_sibling_g1 = _exchange_alone(_PairExchange([gw_g1]), "pair_exchange_ffn1_gate")
    psum_g1 = _pair_sum(core, [gw_g1], from_sibling_g1, "pair_sum_ffn1_gate")
    chipx_g1 = _ChipExchange(psum_g1)
    gw_u1 = _mm_tn([du1], h1, "grad_ffn1_up", hosted=[chipx_g1])
    from_sibling_u1 = _exchange_alone(_PairExchange([gw_u1]), "pair_exchange_ffn1_up")
    psum_u1 = _pair_sum(core, [gw_u1], from_sibling_u1, "pair_sum_ffn1_up")
    chipx_u1 = _ChipExchange(psum_u1)
    dx0, dgam1 = _ffn_bwd_in(dx1, x0, gam1, dg1, du1, wg1, wu1, "ffn1_bwd_in", hosted=[chipx_u1])

    g_g1, g_u1, g_d1, g_g2, g_u2, g_d2 = _final_sum(
        chip, psum_g1 + psum_u1 + psum_d1 + psum_ffn2,
        chipx_g1.result + chipx_u1.result + chipx_d1.result + chipx_ffn2.result, "grad_sum_ffn")
    (g_in,) = _final_sum(chip, psum_in, chipx_in.result, "grad_sum_w_in")
    g_proj, g_out = _final_sum(chip, psum_out, chipx_out.result, "grad_sum_mix")
    grads = {
        "ffn1_w_gate": g_g1.T, "ffn1_w_up": g_u1.T, "ffn1_w_down": g_d1, "w_in": g_in.T,
        "w_proj_a": g_proj[:, 0:A_WIDTH].T, "w_proj_b": g_proj[:, A_WIDTH:].T, "w_out": g_out,
        "ffn2_w_gate": g_g2.T, "ffn2_w_up": g_u2.T, "ffn2_w_down": g_d2,
    }

    def row_of(v):
        return jnp.pad(v.reshape(1, -1), ((0, 0), (0, D_MODEL - v.size)))

    def table_rows(v):
        return jnp.pad(v, ((0, 0), (0, D_MODEL - REL_TABLE)))

    drel_local = jnp.flip(drel_lanes[:, 0, 0:REL_TABLE], axis=1)
    small_part = jnp.concatenate(
        [jnp.sum(dgam1, axis=0, keepdims=True), jnp.sum(dgam2, axis=0, keepdims=True),
         jnp.sum(dgam3, axis=0, keepdims=True), jnp.sum(dgam4, axis=0, keepdims=True),
         row_of(jnp.sum(loss_part)), row_of(dsink[:, 0, 0]), jnp.zeros((2, D_MODEL), F32),
         table_rows(drel_local)], axis=0)
    small = _all_reduce_small(small_part)
    loss = small[4, 0]

    def pack(n1, n2, n3, n4, sk, tb):
        return jnp.concatenate([n1.reshape(1, -1), n2.reshape(1, -1), n3.reshape(1, -1), n4.reshape(1, -1),
                                jnp.zeros((1, D_MODEL), F32), row_of(sk), jnp.zeros((2, D_MODEL), F32), table_rows(tb)],
                               axis=0)

    live = np.zeros((SMALL_ROWS, D_MODEL), np.float32)
    live[0:4] = 1.0
    live[5, 0:B_Q_HEADS] = 1.0
    live[8:16, 0:REL_TABLE] = 1.0
    small_g = small * jnp.asarray(live)
    sw = pack(ffn1_norm, mix_norm, ffn2_norm, final_norm, sinks, rel_bias)
    sm = pack(m_ffn1_norm, m_mix_norm, m_ffn2_norm, m_final_norm, m_sinks, m_rel_bias)
    sv = pack(v_ffn1_norm, v_mix_norm, v_ffn2_norm, v_final_norm, v_sinks, v_rel_bias)
    (sd,), (snm,), (snv,) = _adamw([sw], [small_g], [sm], [sv], "adamw_small")

    def unpack(p):
        return {"ffn1_norm": p[0], "mix_norm": p[1], "ffn2_norm": p[2], "final_norm": p[3],
                "sinks": p[5, 0:B_Q_HEADS], "rel_bias": p[8:16, 0:REL_TABLE]}

    grads.update(unpack(small_g))
    delta, new_m, new_v = unpack(sd), unpack(snm), unpack(snv)

    wmv = {
        "ffn1_w_gate": (ffn1_w_gate, m_ffn1_w_gate, v_ffn1_w_gate), "ffn1_w_up": (ffn1_w_up, m_ffn1_w_up, v_ffn1_w_up),
        "ffn1_w_down": (ffn1_w_down, m_ffn1_w_down, v_ffn1_w_down), "w_in": (w_in, m_w_in, v_w_in),
        "w_proj_a": (w_proj_a, m_w_proj_a, v_w_proj_a), "w_proj_b": (w_proj_b, m_w_proj_b, v_w_proj_b),
        "w_out": (w_out, m_w_out, v_w_out),
        "ffn2_w_gate": (ffn2_w_gate, m_ffn2_w_gate, v_ffn2_w_gate), "ffn2_w_up": (ffn2_w_up, m_ffn2_w_up, v_ffn2_w_up),
        "ffn2_w_down": (ffn2_w_down, m_ffn2_w_down, v_ffn2_w_down),
    }
    def adamw_group(gname, names, hosted=()):
        ds_, ms_, vs_ = _adamw([wmv[n][0] for n in names], [grads[n] for n in names], [wmv[n][1] for n in names],
                               [wmv[n][2] for n in names], gname, hosted=hosted)
        for n, d_, m_, v_ in zip(names, ds_, ms_, vs_):
            delta[n], new_m[n], new_v[n] = d_, m_, v_

    adamw_group("adamw_ffn_up", ["ffn1_w_gate", "ffn1_w_up", "ffn2_w_gate", "ffn2_w_up"])
    adamw_group("adamw_rest", ["ffn1_w_down", "ffn2_w_down", "w_in", "w_proj_a", "w_proj_b", "w_out"])

    order = ["ffn1_norm", "ffn1_w_gate", "ffn1_w_up", "ffn1_w_down", "mix_norm", "w_in", "rel_bias", "sinks",
             "w_proj_a", "w_proj_b", "w_out", "ffn2_norm", "ffn2_w_gate", "ffn2_w_up", "ffn2_w_down", "final_norm"]
    grad_x = dx0.reshape(bsz, s_len, D_MODEL)
    return (loss, grad_x, *[grads[n] for n in order], *[delta[n] for n in order], *[new_m[n] for n in order],
            *[new_v[n] for n in order])
```

```python
import numpy as np
import jax
import jax.numpy as jnp
from jax import lax
from jax.experimental import pallas as pl
from jax.experimental.pallas import tpu as pltpu

F32 = jnp.float32
BF16 = jnp.bfloat16

D_MODEL = 1024
D_FF = 2816
CHUNK = 64
D_HEAD = 64
A_HEADS = 8
A_PREV = 8
MAX_REL = 128
B_Q_HEADS = 8
B_KV_HEADS = 2
B_GROUP = B_Q_HEADS // B_KV_HEADS
B_PREV = 2
REL_TABLE = (CHUNK - 1) + MAX_REL + 1
A_WIDTH = A_HEADS * D_HEAD
B_Q_WIDTH = B_Q_HEADS * D_HEAD
B_KV_WIDTH = B_KV_HEADS * D_HEAD
QKV_A = 3 * A_WIDTH
QKV_B = B_Q_WIDTH + 2 * B_KV_WIDTH
IN_WIDTH = QKV_A + QKV_B + 2 * D_MODEL
EPS = 1e-6
NEG_INF = -1e30
SCALE = 1.0 / 8.0

ADAM_LR = 0.001
ADAM_B1 = 0.9
ADAM_B2 = 0.999
ADAM_EPS = 1e-08
ADAM_WD = 0.01
ADAM_STEP = 10

N_DEV = 8
N_CHIP = 4
MESH = pl.DeviceIdType.MESH

LANES = 128
TQ = 256
TM = 256
TM_FWD = 256
FC = 256
VMEM_LIMIT = 56 << 20


def _cparams(sem, vmem=VMEM_LIMIT):
    return pltpu.CompilerParams(dimension_semantics=sem, vmem_limit_bytes=vmem)


def _dot_nt(a, b):
    return lax.dot_general(a, b, (((1,), (1,)), ((), ())), preferred_element_type=F32)


def _dot_nn(a, b):
    return lax.dot_general(a, b, (((1,), (0,)), ((), ())), preferred_element_type=F32)


def _dot_tn(a, b):
    return lax.dot_general(a, b, (((0,), (0,)), ((), ())), preferred_element_type=F32)


def _resident(shape):
    nd = len(shape)
    return pl.BlockSpec(shape, lambda *_: (0,) * nd, pipeline_mode=pl.Buffered(1))


def _rows(tm, width):
    return pl.BlockSpec((tm, width), lambda i: (i, 0))


def _colsum8(v):
    tm, n = v.shape
    return jnp.sum(v.reshape(tm // 8, 8, n), axis=0)


def _rms(x):
    r = lax.rsqrt(jnp.mean(x * x, axis=-1, keepdims=True) + EPS)
    return x * r, r


def _rms_bwd(dh, xh, r, gamma):
    dxh = dh * gamma
    dx = r * (dxh - xh * jnp.mean(dxh * xh, axis=-1, keepdims=True))
    return dx, _colsum8(dh * xh)


def _hbm():
    return pl.BlockSpec(memory_space=pltpu.HBM)


def _call(body, *, name, grid, in_specs, out_specs, out_shape, args, sem, scratch_shapes=(), hosted=()):
    in_specs, out_specs, out_shape = list(in_specs), list(out_specs), list(out_shape)
    scratch_shapes = list(scratch_shapes)
    if not hosted:
        return pl.pallas_call(body, name=name, grid=grid, in_specs=in_specs, out_specs=out_specs, out_shape=out_shape,
                              scratch_shapes=scratch_shapes, compiler_params=_cparams(sem))(*args)
    n_in, n_out, n_scr = len(in_specs), len(out_specs), len(scratch_shapes)
    x_in = [a for x in hosted for a in x.inputs]
    x_out = [s for x in hosted for s in x.out_shape]
    x_scr = [s for x in hosted for s in x.scratch]
    steps = int(np.prod(grid))
    forward_step = max(steps - 3, 0)
    relay_step = min((5 * steps) // 8, forward_step)

    def wrapped(*refs):
        pos = [0]

        def take(k):
            pos[0] += k
            return refs[pos[0] - k:pos[0]]

        ins, xin, outs, xout, scr, xscr = (take(k) for k in (n_in, len(x_in), n_out, len(x_out), n_scr, len(x_scr)))
        step = 0
        for axis, extent in enumerate(grid):
            step = step * extent + pl.program_id(axis)
        own, oi, oo, osc = [], 0, 0, 0
        for x in hosted:
            own.append((xin[oi:oi + len(x.inputs)], xout[oo:oo + len(x.out_shape)], xscr[osc:osc + len(x.scratch)]))
            oi, oo, osc = oi + len(x.inputs), oo + len(x.out_shape), osc + len(x.scratch)

        def phase(method):
            for x, (i_, o_, s_) in zip(hosted, own):
                getattr(x, method)(i_, o_, s_)

        pl.when(step == 0)(lambda: phase("start"))
        body(*ins, *outs, *scr)
        pl.when(step == relay_step)(lambda: phase("relay"))
        pl.when(step == forward_step)(lambda: phase("forward"))
        pl.when(step == steps - 1)(lambda: phase("finish"))

    res = pl.pallas_call(
        wrapped, name=name, grid=grid, in_specs=in_specs + [_hbm()] * len(x_in),
        out_specs=out_specs + [_hbm()] * len(x_out), out_shape=out_shape + x_out,
        scratch_shapes=scratch_shapes + x_scr, compiler_params=_cparams(("arbitrary",) * len(grid)))(*args, *x_in)
    rest = list(res[n_out:])
    for x in hosted:
        x.result, rest = rest[:len(x.out_shape)], rest[len(x.out_shape):]
    return list(res[:n_out])


def _ffn_fwd(x, gamma, wg_t, wu_t, wd, name, hosted=()):
    t = x.shape[0]
    f = wg_t.shape[0]

    def body(x_ref, gam_ref, wg_ref, wu_ref, wd_ref, h_ref, g_ref, u_ref, a_ref, y_ref):
        xv = x_ref[...]
        xh, _ = _rms(xv)
        h = (xh * gam_ref[...]).astype(BF16)
        h_ref[...] = h
        for j in range(f // FC):
            sl = slice(j * FC, (j + 1) * FC)
            g = _dot_nt(h, wg_ref[sl, :])
            u = _dot_nt(h, wu_ref[sl, :])
            g_ref[:, sl] = g.astype(BF16)
            u_ref[:, sl] = u.astype(BF16)
            a_ref[:, sl] = (g * jax.nn.sigmoid(g) * u).astype(BF16)
        y_ref[...] = xv + 0.5 * _dot_nn(a_ref[...], wd_ref[...])

    return _call(
        body,
        name=name,
        grid=(t // TM_FWD,),
        in_specs=[_rows(TM_FWD, D_MODEL), _resident((1, D_MODEL)), _resident((f, D_MODEL)), _resident((f, D_MODEL)),
                  _resident((f, D_MODEL))],
        out_specs=[_rows(TM_FWD, D_MODEL), _rows(TM_FWD, f), _rows(TM_FWD, f), _rows(TM_FWD, f),
                   _rows(TM_FWD, D_MODEL)],
        out_shape=[jax.ShapeDtypeStruct((t, D_MODEL), BF16), jax.ShapeDtypeStruct((t, f), BF16),
                   jax.ShapeDtypeStruct((t, f), BF16), jax.ShapeDtypeStruct((t, f), BF16),
                   jax.ShapeDtypeStruct((t, D_MODEL), F32)],
        args=(x, gamma, wg_t, wu_t, wd), sem=("parallel",), hosted=hosted)


def _ffn_bwd(d, x, gamma, g_act, u_act, wg_t, wu_t, wd, name, hosted=()):
    t = x.shape[0]
    f = wg_t.shape[0]

    def body(d_ref, x_ref, gam_ref, g_ref, u_ref, wg_ref, wu_ref, wd_ref, dx_ref, dg_ref, du_ref, db_ref, dgam_ref):
        dv = d_ref[...]
        db = (0.5 * dv).astype(BF16)
        db_ref[...] = db
        for j in range(f // FC):
            sl = slice(j * FC, (j + 1) * FC)
            da = _dot_nt(db, wd_ref[sl, :])
            g = g_ref[:, sl].astype(F32)
            u = u_ref[:, sl].astype(F32)
            s = jax.nn.sigmoid(g)
            dg_ref[:, sl] = (da * u * (s * (1.0 + g * (1.0 - s)))).astype(BF16)
            du_ref[:, sl] = (da * (g * s)).astype(BF16)
        dh = _dot_nn(dg_ref[...], wg_ref[...]) + _dot_nn(du_ref[...], wu_ref[...])
        xh, r = _rms(x_ref[...])
        dxn, dgam = _rms_bwd(dh, xh, r, gam_ref[...])
        dx_ref[...] = dv + dxn

        @pl.when(pl.program_id(0) == 0)
        def _():
            dgam_ref[...] = jnp.zeros_like(dgam_ref)

        dgam_ref[...] += dgam

    return _call(
        body,
        name=name,
        grid=(t // TM,),
        in_specs=[_rows(TM, D_MODEL), _rows(TM, D_MODEL), _resident((1, D_MODEL)), _rows(TM, f), _rows(TM, f),
                  _resident((f, D_MODEL)), _resident((f, D_MODEL)), _resident((f, D_MODEL))],
        out_specs=[_rows(TM, D_MODEL), _rows(TM, f), _rows(TM, f), _rows(TM, D_MODEL),
                   pl.BlockSpec((8, D_MODEL), lambda i: (0, 0))],
        out_shape=[jax.ShapeDtypeStruct((t, D_MODEL), F32), jax.ShapeDtypeStruct((t, f), BF16),
                   jax.ShapeDtypeStruct((t, f), BF16), jax.ShapeDtypeStruct((t, D_MODEL), BF16),
                   jax.ShapeDtypeStruct((8, D_MODEL), F32)],
        args=(d, x, gamma, g_act, u_act, wg_t, wu_t, wd), sem=("arbitrary",), hosted=hosted)


def _ffn_bwd_act(d, g_act, u_act, wd, name, hosted=()):
    t = d.shape[0]
    f = wd.shape[0]

    def body(d_ref, g_ref, u_ref, wd_ref, dg_ref, du_ref):
        db = (0.5 * d_ref[...]).astype(BF16)
        for j in range(f // FC):
            sl = slice(j * FC, (j + 1) * FC)
            da = _dot_nt(db, wd_ref[sl, :])
            g = g_ref[:, sl].astype(F32)
            u = u_ref[:, sl].astype(F32)
            s = jax.nn.sigmoid(g)
            dg_ref[:, sl] = (da * u * (s * (1.0 + g * (1.0 - s)))).astype(BF16)
            du_ref[:, sl] = (da * (g * s)).astype(BF16)

    return _call(
        body,
        name=name,
        grid=(t // TM,),
        in_specs=[_rows(TM, D_MODEL), _rows(TM, f), _rows(TM, f), _resident((f, D_MODEL))],
        out_specs=[_rows(TM, f), _rows(TM, f)],
        out_shape=[jax.ShapeDtypeStruct((t, f), BF16), jax.ShapeDtypeStruct((t, f), BF16)],
        args=(d, g_act, u_act, wd), sem=("parallel",), hosted=hosted)


def _ffn_bwd_in(d, x, gamma, dg, du, wg_t, wu_t, name, hosted=()):
    t = x.shape[0]
    f = wg_t.shape[0]

    def body(d_ref, x_ref, gam_ref, dg_ref, du_ref, wg_ref, wu_ref, dx_ref, dgam_ref):
        dh = _dot_nn(dg_ref[...], wg_ref[...]) + _dot_nn(du_ref[...], wu_ref[...])
        xh, r = _rms(x_ref[...])
        dxn, dgam = _rms_bwd(dh, xh, r, gam_ref[...])
        dx_ref[...] = d_ref[...] + dxn

        @pl.when(pl.program_id(0) == 0)
        def _():
            dgam_ref[...] = jnp.zeros_like(dgam_ref)

        dgam_ref[...] += dgam

    return _call(
        body,
        name=name,
        grid=(t // TM,),
        in_specs=[_rows(TM, D_MODEL), _rows(TM, D_MODEL), _resident((1, D_MODEL)), _rows(TM, f), _rows(TM, f),
                  _resident((f, D_MODEL)), _resident((f, D_MODEL))],
        out_specs=[_rows(TM, D_MODEL), pl.BlockSpec((8, D_MODEL), lambda i: (0, 0))],
        out_shape=[jax.ShapeDtypeStruct((t, D_MODEL), F32), jax.ShapeDtypeStruct((8, D_MODEL), F32)],
        args=(d, x, gamma, dg, du, wg_t, wu_t), sem=("arbitrary",), hosted=hosted)


def _mm_tn(pieces, b, name, tile=256, hosted=()):
    t, n = b.shape
    npc = len(pieces)
    counts = [p.shape[1] // tile for p in pieces]
    los = [sum(counts[:k]) for k in range(npc)]
    total = sum(counts)

    def body(*refs):
        a_refs, b_ref, o_ref = refs[:npc], refs[npc], refs[npc + 1]
        i = pl.program_id(0)
        for k in range(npc):
            @pl.when(jnp.logical_and(i >= los[k], i < los[k] + counts[k]))
            def _(k=k):
                o_ref[...] = _dot_tn(a_refs[k][...], b_ref[...]).astype(BF16)

    def a_spec(k):
        return pl.BlockSpec((t, tile), lambda i: (0, jnp.clip(i - los[k], 0, counts[k] - 1)))

    return _call(
        body,
        name=name,
        grid=(total,),
        in_specs=[a_spec(k) for k in range(npc)] + [_resident((t, n))],
        out_specs=[pl.BlockSpec((tile, n), lambda i: (i, 0))],
        out_shape=[jax.ShapeDtypeStruct((total * tile, n), BF16)],
        args=(*pieces, b), sem=("parallel",), hosted=hosted)[0]


def _mm_tn_proj(dya, dyb, oa, ob, tile=256):
    t = dya.shape[0]

    def body(dya_ref, dyb_ref, oa_ref, ob_ref, o_ref):
        o_ref[:, 0:A_WIDTH] = _dot_tn(dya_ref[...], oa_ref[...]).astype(BF16)
        o_ref[:, A_WIDTH:A_WIDTH + B_Q_WIDTH] = _dot_tn(dyb_ref[...], ob_ref[...]).astype(BF16)

    col = pl.BlockSpec((t, tile), lambda i: (0, i))
    return pl.pallas_call(
        body,
        name="grad_proj",
        grid=(D_MODEL // tile,),
        in_specs=[col, col, _resident((t, A_WIDTH)), _resident((t, B_Q_WIDTH))],
        out_specs=pl.BlockSpec((tile, A_WIDTH + B_Q_WIDTH), lambda i: (i, 0)),
        out_shape=jax.ShapeDtypeStruct((D_MODEL, A_WIDTH + B_Q_WIDTH), BF16),
        compiler_params=_cparams(("parallel",)),
    )(dya, dyb, oa, ob)


def _proj_fwd(x, gamma, win_t, hosted=()):
    t = x.shape[0]

    def body(x_ref, gam_ref, w_ref, h_ref, qa_ref, qb_ref, gt_ref):
        xh, _ = _rms(x_ref[...])
        h = (xh * gam_ref[...]).astype(BF16)
        h_ref[...] = h
        for j in range(QKV_A // FC):
            qa_ref[:, j * FC:(j + 1) * FC] = _dot_nt(h, w_ref[j * FC:(j + 1) * FC, :]).astype(BF16)
        for j in range(QKV_B // FC):
            lo = QKV_A + j * FC
            qb_ref[:, j * FC:(j + 1) * FC] = _dot_nt(h, w_ref[lo:lo + FC, :]).astype(BF16)
        for j in range(2 * D_MODEL // FC):
            lo = QKV_A + QKV_B + j * FC
            gt_ref[:, j * FC:(j + 1) * FC] = _dot_nt(h, w_ref[lo:lo + FC, :])

    return _call(
        body,
        name="proj_fwd",
        grid=(t // TM_FWD,),
        in_specs=[_rows(TM_FWD, D_MODEL), _resident((1, D_MODEL)), _resident((IN_WIDTH, D_MODEL))],
        out_specs=[_rows(TM_FWD, D_MODEL), _rows(TM_FWD, QKV_A), _rows(TM_FWD, QKV_B), _rows(TM_FWD, 2 * D_MODEL)],
        out_shape=[jax.ShapeDtypeStruct((t, D_MODEL), BF16), jax.ShapeDtypeStruct((t, QKV_A), BF16),
                   jax.ShapeDtypeStruct((t, QKV_B), BF16), jax.ShapeDtypeStruct((t, 2 * D_MODEL), F32)],
        args=(x, gamma, win_t), sem=("parallel",), hosted=hosted)


def _proj_bwd(d, x, gamma, pieces, win_t, hosted=()):
    t = x.shape[0]
    npc = len(pieces)
    widths = [p.shape[1] for p in pieces]
    los = [sum(widths[:k]) for k in range(npc)]

    def body(*refs):
        d_ref, x_ref, gam_ref = refs[:3]
        p_refs = refs[3:3 + npc]
        w_ref, dx_ref, db_ref, dgam_ref = refs[3 + npc:]
        dh = _dot_nn(p_refs[0][...], w_ref[0:widths[0], :])
        for k in range(1, npc):
            dh += _dot_nn(p_refs[k][...], w_ref[los[k]:los[k] + widths[k], :])
        xh, r = _rms(x_ref[...])
        dxn, dgam = _rms_bwd(dh, xh, r, gam_ref[...])
        dx = d_ref[...] + dxn
        dx_ref[...] = dx
        db_ref[...] = (0.5 * dx).astype(BF16)

        @pl.when(pl.program_id(0) == 0)
        def _():
            dgam_ref[...] = jnp.zeros_like(dgam_ref)

        dgam_ref[...] += dgam

    return _call(
        body,
        name="proj_bwd",
        grid=(t // TM,),
        in_specs=[_rows(TM, D_MODEL), _rows(TM, D_MODEL), _resident((1, D_MODEL))] + [_rows(TM, w) for w in widths]
        + [_resident((IN_WIDTH, D_MODEL))],
        out_specs=[_rows(TM, D_MODEL), _rows(TM, D_MODEL), pl.BlockSpec((8, D_MODEL), lambda i: (0, 0))],
        out_shape=[jax.ShapeDtypeStruct((t, D_MODEL), F32), jax.ShapeDtypeStruct((t, D_MODEL), BF16),
                   jax.ShapeDtypeStruct((8, D_MODEL), F32)],
        args=(d, x, gamma, *pieces, win_t), sem=("arbitrary",), hosted=hosted)


def _lane_half(shape):
    return lax.broadcasted_iota(jnp.int32, shape, len(shape) - 1) // D_HEAD


def _band_softmax(q, kk, bias, sink, qs, pad):
    s = _dot_nt(q, kk) + bias
    if qs is not None:
        col = lax.broadcasted_iota(jnp.int32, s.shape, 1)
        s = jnp.where(col + qs >= pad, s, NEG_INF)
    m = jnp.max(s, axis=-1, keepdims=True)
    if sink is not None:
        m = jnp.maximum(m, sink)
    p = jnp.exp(s - m)
    den = jnp.sum(p, axis=-1, keepdims=True)
    if sink is not None:
        den = den + jnp.exp(sink - m)
    return p, m, 1.0 / den


def _fill_padded(dst, src, pad):
    dst[0:pad, :] = jnp.zeros((pad,) + dst.shape[1:], dst.dtype)
    dst[pad:, :] = src


FWD_PAIRS = 4
BWD_PAIRS = 2


def _attn_a_fwd(qkv, bias, hosted=()):
    bsz, s_len, _ = qkv.shape
    pad = A_PREV * CHUNK
    band = TQ + pad
    pp = FWD_PAIRS
    w = pp * LANES
    nb = A_WIDTH // w

    def body(q_ref, k_ref, v_ref, b_ref, o_ref, kp, vp):
        i = pl.program_id(2)

        @pl.when(i == 0)
        def _():
            _fill_padded(kp, k_ref[...], pad)
            _fill_padded(vp, v_ref[...], pad)

        qs = pl.multiple_of(i * TQ, TQ)
        half = _lane_half((1, LANES))

        def block(masked):
            for pr in range(pp):
                sl = slice(pr * LANES, (pr + 1) * LANES)
                kk = kp[pl.ds(qs, band), sl]
                vv = vp[pl.ds(qs, band), sl]
                q = q_ref[:, sl] * SCALE
                outs = []
                for j in range(2):
                    qm = jnp.where(half == j, q, jnp.zeros_like(q))
                    p, _, inv = _band_softmax(qm, kk, b_ref[2 * pr + j], None, qs if masked else None, pad)
                    outs.append(_dot_nn(p.astype(BF16), vv) * inv)
                o_ref[:, sl] = jnp.where(half == 0, outs[0], outs[1]).astype(BF16)

        pl.when(i < pad // TQ)(lambda: block(True))
        pl.when(i >= pad // TQ)(lambda: block(False))

    return _call(
        body,
        name="attn_a_fwd",
        grid=(bsz, nb, s_len // TQ),
        in_specs=[pl.BlockSpec((None, TQ, w), lambda b, g, i: (b, i, g)),
                  pl.BlockSpec((None, s_len, w), lambda b, g, i: (b, 0, nb + g)),
                  pl.BlockSpec((None, s_len, w), lambda b, g, i: (b, 0, 2 * nb + g)),
                  pl.BlockSpec((2 * pp, TQ, band), lambda b, g, i: (g, 0, 0))],
        out_specs=[pl.BlockSpec((None, TQ, w), lambda b, g, i: (b, i, g))],
        out_shape=[jax.ShapeDtypeStruct((bsz, s_len, A_WIDTH), BF16)],
        scratch_shapes=[pltpu.VMEM((pad + s_len, w), BF16), pltpu.VMEM((pad + s_len, w), BF16)],
        args=(qkv, qkv, qkv, bias), sem=("arbitrary", "arbitrary", "arbitrary"), hosted=hosted)[0]


def _attn_a_bwd(qkv, bias, do, hosted=()):
    bsz, s_len, _ = qkv.shape
    pad = A_PREV * CHUNK
    band = TQ + pad
    n_i = s_len // TQ
    pp = BWD_PAIRS
    w = pp * LANES
    nb = A_WIDTH // w

    def body(q_ref, k_ref, v_ref, b_ref, do_ref, dq_ref, dk_ref, dv_ref, dbias_ref, kp, vp, dk_acc, dv_acc):
        b = pl.program_id(1)
        i = pl.program_id(2)

        @pl.when(i == 0)
        def _():
            _fill_padded(kp, k_ref[...], pad)
            _fill_padded(vp, v_ref[...], pad)
            dk_acc[...] = jnp.zeros_like(dk_acc)
            dv_acc[...] = jnp.zeros_like(dv_acc)

        @pl.when(jnp.logical_and(b == 0, i == 0))
        def _():
            dbias_ref[...] = jnp.zeros_like(dbias_ref)

        qs = pl.multiple_of(i * TQ, TQ)
        half = _lane_half((1, LANES))

        def block(masked):
            for pr in range(pp):
                sl = slice(pr * LANES, (pr + 1) * LANES)
                kk = kp[pl.ds(qs, band), sl]
                vv = vp[pl.ds(qs, band), sl]
                q = q_ref[:, sl] * SCALE
                dd = do_ref[:, sl]
                dqs, dks, dvs = [], [], []
                for j in range(2):
                    qm = jnp.where(half == j, q, jnp.zeros_like(q))
                    dm = jnp.where(half == j, dd, jnp.zeros_like(dd))
                    p, _, inv = _band_softmax(qm, kk, b_ref[2 * pr + j], None, qs if masked else None, pad)
                    pn = p * inv
                    dp = _dot_nt(dm, vv)
                    delta = jnp.sum(pn * dp, axis=-1, keepdims=True)
                    ds = pn * (dp - delta)
                    dbias_ref[2 * pr + j] += ds[:, band - REL_COLS:]
                    dsb = ds.astype(BF16)
                    dqs.append(_dot_nn(dsb, kk))
                    dks.append(_dot_tn(dsb, q))
                    dvs.append(_dot_tn(pn.astype(BF16), dd))
                dq_ref[:, sl] = (jnp.where(half == 0, dqs[0], dqs[1]) * SCALE).astype(BF16)
                dk_acc[pl.ds(qs, band), sl] += jnp.where(half == 0, dks[0], dks[1])
                dv_acc[pl.ds(qs, band), sl] += jnp.where(half == 0, dvs[0], dvs[1])

        pl.when(i < pad // TQ)(lambda: block(True))
        pl.when(i >= pad // TQ)(lambda: block(False))

        @pl.when(i == n_i - 1)
        def _():
            dk_ref[...] = dk_acc[pad:, :].astype(BF16)
            dv_ref[...] = dv_acc[pad:, :].astype(BF16)

    qspec = pl.BlockSpec((None, TQ, w), lambda g, b, i: (b, i, g))
    kvout = pl.BlockSpec((None, s_len, w), lambda g, b, i: (b, 0, g))
    wide = jax.ShapeDtypeStruct((bsz, s_len, A_WIDTH), BF16)
    return _call(
        body,
        name="attn_a_bwd",
        grid=(nb, bsz, n_i),
        in_specs=[qspec,
                  pl.BlockSpec((None, s_len, w), lambda g, b, i: (b, 0, nb + g)),
                  pl.BlockSpec((None, s_len, w), lambda g, b, i: (b, 0, 2 * nb + g)),
                  pl.BlockSpec((2 * pp, TQ, band), lambda g, b, i: (g, 0, 0)),
                  qspec],
        out_specs=[qspec, kvout, kvout, pl.BlockSpec((2 * pp, TQ, REL_COLS), lambda g, b, i: (g, 0, 0))],
        out_shape=[wide, wide, wide, jax.ShapeDtypeStruct((A_HEADS, TQ, REL_COLS), F32)],
        scratch_shapes=[pltpu.VMEM((pad + s_len, w), BF16), pltpu.VMEM((pad + s_len, w), BF16),
                        pltpu.VMEM((pad + s_len, w), F32), pltpu.VMEM((pad + s_len, w), F32)],
        args=(qkv, qkv, qkv, bias, do), sem=("arbitrary", "arbitrary", "arbitrary"), hosted=hosted)


def _fill_padded_dup(dst, src, pad, h, half):
    other = pltpu.roll(src, D_HEAD, 1)
    _fill_padded(dst, jnp.where(half == h, src, other), pad)


def _attn_b_fwd(qkv, bias, sink):
    bsz, s_len, _ = qkv.shape
    pad = B_PREV * CHUNK
    band = TQ + pad
    kcol = B_Q_WIDTH // LANES
    npair = B_Q_HEADS // 2

    def body(q_ref, k_ref, v_ref, b_ref, s_ref, o_ref, kp, vp):
        i = pl.program_id(1)
        half = _lane_half((1, LANES))

        @pl.when(i == 0)
        def _():
            for h in range(B_KV_HEADS):
                _fill_padded_dup(kp.at[h], k_ref[...], pad, h, half)
                _fill_padded_dup(vp.at[h], v_ref[...], pad, h, half)

        qs = pl.multiple_of(i * TQ, TQ)

        def block(masked):
            for pr in range(npair):
                h = pr // (B_GROUP // 2)
                sl = slice(pr * LANES, (pr + 1) * LANES)
                kk = kp[h, pl.ds(qs, band), :]
                vv = vp[h, pl.ds(qs, band), :]
                q = q_ref[:, sl] * SCALE
                outs = []
                for j in range(2):
                    qm = jnp.where(half == j, q, jnp.zeros_like(q))
                    p, _, inv = _band_softmax(qm, kk, b_ref[2 * pr + j], s_ref[2 * pr + j][0:1, 0:1],
                                              qs if masked else None, pad)
                    outs.append(_dot_nn(p.astype(BF16), vv) * inv)
                o_ref[:, sl] = jnp.where(half == 0, outs[0], outs[1]).astype(BF16)

        pl.when(i < -(-pad // TQ))(lambda: block(True))
        pl.when(i >= -(-pad // TQ))(lambda: block(False))

    return pl.pallas_call(
        body,
        name="attn_b_fwd",
        grid=(bsz, s_len // TQ),
        in_specs=[pl.BlockSpec((None, TQ, B_Q_WIDTH), lambda b, i: (b, i, 0)),
                  pl.BlockSpec((None, s_len, LANES), lambda b, i: (b, 0, kcol)),
                  pl.BlockSpec((None, s_len, LANES), lambda b, i: (b, 0, kcol + 1)),
                  pl.BlockSpec((B_Q_HEADS, TQ, band), lambda b, i: (0, 0, 0)),
                  pl.BlockSpec((B_Q_HEADS, 8, LANES), lambda b, i: (0, 0, 0))],
        out_specs=pl.BlockSpec((None, TQ, B_Q_WIDTH), lambda b, i: (b, i, 0)),
        out_shape=jax.ShapeDtypeStruct((bsz, s_len, B_Q_WIDTH), BF16),
        scratch_shapes=[pltpu.VMEM((B_KV_HEADS, pad + s_len, LANES), BF16),
                        pltpu.VMEM((B_KV_HEADS, pad + s_len, LANES), BF16)],
        compiler_params=_cparams(("arbitrary", "arbitrary")),
    )(qkv, qkv, qkv, bias, sink)


def _attn_b_bwd(qkv, bias, sink, do, hosted=()):
    bsz, s_len, _ = qkv.shape
    pad = B_PREV * CHUNK
    band = TQ + pad
    kcol = B_Q_WIDTH // LANES
    npair = B_Q_HEADS // 2
    n_i = s_len // TQ

    pp = B_GROUP // 2
    w = pp * LANES

    def body(q_ref, k_ref, v_ref, b_ref, s_ref, do_ref, dq_ref, dkv_ref, dsink_ref, kp, vp, dk_acc, dv_acc):
        b = pl.program_id(0)
        h = pl.program_id(1)
        i = pl.program_id(2)
        half = _lane_half((1, LANES))

        @pl.when(i == 0)
        def _():
            _fill_padded_dup(kp, k_ref[...], pad, h, half)
            _fill_padded_dup(vp, v_ref[...], pad, h, half)

        @pl.when(jnp.logical_and(h == 0, i == 0))
        def _():
            dk_acc[...] = jnp.zeros_like(dk_acc)
            dv_acc[...] = jnp.zeros_like(dv_acc)

        @pl.when(jnp.logical_and(b == 0, jnp.logical_and(h == 0, i == 0)))
        def _():
            dsink_ref[...] = jnp.zeros_like(dsink_ref)

        qs = pl.multiple_of(i * TQ, TQ)

        def block(masked):
            kk = kp[pl.ds(qs, band), :]
            vv = vp[pl.ds(qs, band), :]
            dk2 = jnp.zeros((band, LANES), F32)
            dv2 = jnp.zeros((band, LANES), F32)
            for pr in range(pp):
                sl = slice(pr * LANES, (pr + 1) * LANES)
                q = q_ref[:, sl] * SCALE
                dd = do_ref[:, sl]
                dqs, dks, dvs = [], [], []
                for j in range(2):
                    qm = jnp.where(half == j, q, jnp.zeros_like(q))
                    dm = jnp.where(half == j, dd, jnp.zeros_like(dd))
                    sink = s_ref[2 * pr + j][0:1, 0:1]
                    p, m, inv = _band_softmax(qm, kk, b_ref[2 * pr + j], sink, qs if masked else None, pad)
                    pn = p * inv
                    dp = _dot_nt(dm, vv)
                    delta = jnp.sum(pn * dp, axis=-1, keepdims=True)
                    ds = pn * (dp - delta)
                    dsb = ds.astype(BF16)
                    dqs.append(_dot_nn(dsb, kk))
                    dks.append(_dot_tn(dsb, q))
                    dvs.append(_dot_tn(pn.astype(BF16), dd))
                    dsk = jnp.sum(-(jnp.exp(sink - m) * inv) * delta, axis=0, keepdims=True)
                    dsink_ref[2 * pp * h + 2 * pr + j] += jnp.broadcast_to(dsk, (8, LANES))
                dq_ref[:, sl] = (jnp.where(half == 0, dqs[0], dqs[1]) * SCALE).astype(BF16)
                dk2 = dk2 + jnp.where(half == 0, dks[0], dks[1])
                dv2 = dv2 + jnp.where(half == 0, dvs[0], dvs[1])
            dk_acc[pl.ds(qs, band), :] += jnp.where(half == h, dk2 + pltpu.roll(dk2, D_HEAD, 1), 0.0)
            dv_acc[pl.ds(qs, band), :] += jnp.where(half == h, dv2 + pltpu.roll(dv2, D_HEAD, 1), 0.0)

        pl.when(i < -(-pad // TQ))(lambda: block(True))
        pl.when(i >= -(-pad // TQ))(lambda: block(False))

        @pl.when(jnp.logical_and(h == B_KV_HEADS - 1, i == n_i - 1))
        def _():
            dkv_ref[:, 0:LANES] = dk_acc[pad:, :].astype(BF16)
            dkv_ref[:, LANES:2 * LANES] = dv_acc[pad:, :].astype(BF16)

    qspec = pl.BlockSpec((None, TQ, w), lambda b, h, i: (b, i, h))
    return _call(
        body,
        name="attn_b_bwd",
        grid=(bsz, B_KV_HEADS, n_i),
        in_specs=[qspec,
                  pl.BlockSpec((None, s_len, LANES), lambda b, h, i: (b, 0, kcol)),
                  pl.BlockSpec((None, s_len, LANES), lambda b, h, i: (b, 0, kcol + 1)),
                  pl.BlockSpec((2 * pp, TQ, band), lambda b, h, i: (h, 0, 0)),
                  pl.BlockSpec((2 * pp, 8, LANES), lambda b, h, i: (h, 0, 0)),
                  qspec],
        out_specs=[qspec, pl.BlockSpec((None, s_len, 2 * LANES), lambda b, h, i: (b, 0, 0)),
                   pl.BlockSpec((B_Q_HEADS, 8, LANES), lambda b, h, i: (0, 0, 0))],
        out_shape=[jax.ShapeDtypeStruct((bsz, s_len, B_Q_WIDTH), BF16),
                   jax.ShapeDtypeStruct((bsz, s_len, 2 * B_KV_WIDTH), BF16),
                   jax.ShapeDtypeStruct((B_Q_HEADS, 8, LANES), F32)],
        scratch_shapes=[pltpu.VMEM((pad + s_len, LANES), BF16), pltpu.VMEM((pad + s_len, LANES), BF16),
                        pltpu.VMEM((pad + s_len, LANES), F32), pltpu.VMEM((pad + s_len, LANES), F32)],
        args=(qkv, qkv, qkv, bias, sink, do), sem=("arbitrary", "arbitrary", "arbitrary"), hosted=hosted)


REL_COLS = 3 * 128
REL_WRAP = 512


def _bias_a_build(tv):
    h = tv.shape[0]
    pad = A_PREV * CHUNK
    band = TQ + pad

    def body(tv_ref, o_ref):
        row = tv_ref[...]
        x = jnp.broadcast_to(row, (TQ, REL_WRAP))
        r = lax.broadcasted_iota(jnp.int32, x.shape, 0)
        for bit in range(8):
            sh = 1 << bit
            x = jnp.where((r & sh) != 0, pltpu.roll(x, sh, 1), x)
        far = jnp.broadcast_to(row[:, 0:1], (TQ, band - REL_COLS))
        full = jnp.concatenate([far, x[:, REL_WRAP // 2:REL_WRAP], x[:, 0:REL_COLS - REL_WRAP // 2]], axis=1)
        qc = (lax.broadcasted_iota(jnp.int32, full.shape, 0) + pad) // CHUNK
        kc = lax.broadcasted_iota(jnp.int32, full.shape, 1) // CHUNK
        ok = jnp.logical_and(kc <= qc, kc >= qc - A_PREV)
        o_ref[...] = jnp.where(ok, full, NEG_INF)

    return pl.pallas_call(
        body,
        name="bias_a_build",
        grid=(h,),
        in_specs=[pl.BlockSpec((None, 1, REL_WRAP), lambda hh: (hh, 0, 0))],
        out_specs=pl.BlockSpec((None, TQ, band), lambda hh: (hh, 0, 0)),
        out_shape=jax.ShapeDtypeStruct((h, TQ, band), F32),
        compiler_params=_cparams(("parallel",)),
    )(tv)


def _relbias_grad(dbias):
    h, rows, _ = dbias.shape

    def body(d_ref, o_ref):
        x = d_ref[...]
        r = lax.broadcasted_iota(jnp.int32, x.shape, 0)
        c = lax.broadcasted_iota(jnp.int32, x.shape, 1) - r
        x = jnp.where(jnp.logical_and(c >= 1, c < REL_TABLE), x, 0.0)
        for bit in range(8):
            sh = 1 << bit
            x = jnp.where((r & sh) != 0, pltpu.roll(x, REL_COLS - sh, 1), x)
        diag = jnp.sum(x, axis=0, keepdims=True)
        lane = lax.broadcasted_iota(jnp.int32, diag.shape, 1)
        diag = jnp.where(jnp.logical_and(lane >= 1, lane < REL_TABLE), diag, 0.0)
        rest = -jnp.sum(diag, axis=1, keepdims=True)
        o_ref[...] = jnp.broadcast_to(jnp.where(lane == 0, rest, diag), o_ref.shape)

    return pl.pallas_call(
        body,
        name="relbias_grad",
        grid=(h,),
        in_specs=[pl.BlockSpec((None, rows, REL_COLS), lambda hh: (hh, 0, 0))],
        out_specs=pl.BlockSpec((None, 8, REL_COLS), lambda hh: (hh, 0, 0)),
        out_shape=jax.ShapeDtypeStruct((h, 8, REL_COLS), F32),
        compiler_params=_cparams(("parallel",)),
    )(dbias)


def _mix_out_fwd(x, oa, ob, gates, proj_t, wout):
    t = x.shape[0]

    def body(x_ref, oa_ref, ob_ref, gt_ref, pt_ref, wo_ref, y_ref, ya_ref, yb_ref, mg_ref):
        ya = _dot_nt(oa_ref[...], pt_ref[:, 0:A_WIDTH])
        yb = _dot_nt(ob_ref[...], pt_ref[:, A_WIDTH:A_WIDTH + B_Q_WIDTH])
        ya_ref[...] = ya.astype(BF16)
        yb_ref[...] = yb.astype(BF16)
        mg = jax.nn.sigmoid(gt_ref[:, 0:D_MODEL]) * ya + jax.nn.sigmoid(gt_ref[:, D_MODEL:2 * D_MODEL]) * yb
        mgb = mg.astype(BF16)
        mg_ref[...] = mgb
        y_ref[...] = x_ref[...] + _dot_nn(mgb, wo_ref[...])

    return pl.pallas_call(
        body,
        name="mix_out_fwd",
        grid=(t // TM,),
        in_specs=[_rows(TM, D_MODEL), _rows(TM, A_WIDTH), _rows(TM, B_Q_WIDTH), _rows(TM, 2 * D_MODEL),
                  _resident((D_MODEL, A_WIDTH + B_Q_WIDTH)), _resident((D_MODEL, D_MODEL))],
        out_specs=[_rows(TM, D_MODEL), _rows(TM, D_MODEL), _rows(TM, D_MODEL), _rows(TM, D_MODEL)],
        out_shape=[jax.ShapeDtypeStruct((t, D_MODEL), F32), jax.ShapeDtypeStruct((t, D_MODEL), BF16),
                   jax.ShapeDtypeStruct((t, D_MODEL), BF16), jax.ShapeDtypeStruct((t, D_MODEL), BF16)],
        compiler_params=_cparams(("parallel",)),
    )(x, oa, ob, gates, proj_t, wout)


def _mix_out_bwd(d, gates, ya, yb, proj_t, wout, hosted=()):
    t = d.shape[0]

    def body(d_ref, gt_ref, ya_ref, yb_ref, pt_ref, wo_ref, db_ref, dya_ref, dyb_ref, doa_ref, dob_ref, dgt_ref):
        db = d_ref[...].astype(BF16)
        db_ref[...] = db
        dmg = _dot_nt(db, wo_ref[...])
        sa = jax.nn.sigmoid(gt_ref[:, 0:D_MODEL])
        sb = jax.nn.sigmoid(gt_ref[:, D_MODEL:2 * D_MODEL])
        dya = (dmg * sa).astype(BF16)
        dyb = (dmg * sb).astype(BF16)
        dya_ref[...] = dya
        dyb_ref[...] = dyb
        dgt_ref[:, 0:D_MODEL] = (dmg * ya_ref[...].astype(F32) * (sa * (1.0 - sa))).astype(BF16)
        dgt_ref[:, D_MODEL:2 * D_MODEL] = (dmg * yb_ref[...].astype(F32) * (sb * (1.0 - sb))).astype(BF16)
        doa_ref[...] = _dot_nn(dya, pt_ref[:, 0:A_WIDTH]).astype(BF16)
        dob_ref[...] = _dot_nn(dyb, pt_ref[:, A_WIDTH:A_WIDTH + B_Q_WIDTH]).astype(BF16)

    return _call(
        body,
        name="mix_out_bwd",
        grid=(t // TM,),
        in_specs=[_rows(TM, D_MODEL), _rows(TM, 2 * D_MODEL), _rows(TM, D_MODEL), _rows(TM, D_MODEL),
                  _resident((D_MODEL, A_WIDTH + B_Q_WIDTH)), _resident((D_MODEL, D_MODEL))],
        out_specs=[_rows(TM, D_MODEL), _rows(TM, D_MODEL), _rows(TM, D_MODEL), _rows(TM, A_WIDTH),
                   _rows(TM, B_Q_WIDTH), _rows(TM, 2 * D_MODEL)],
        out_shape=[jax.ShapeDtypeStruct((t, D_MODEL), BF16), jax.ShapeDtypeStruct((t, D_MODEL), BF16),
                   jax.ShapeDtypeStruct((t, D_MODEL), BF16), jax.ShapeDtypeStruct((t, A_WIDTH), BF16),
                   jax.ShapeDtypeStruct((t, B_Q_WIDTH), BF16), jax.ShapeDtypeStruct((t, 2 * D_MODEL), BF16)],
        args=(d, gates, ya, yb, proj_t, wout), sem=("parallel",), hosted=hosted)


def _loss_head(x, gamma, target):
    t = x.shape[0]

    def body(x_ref, gam_ref, t_ref, dx_ref, dgam_ref, loss_ref):
        xh, r = _rms(x_ref[...])
        gam = gam_ref[...]
        e = xh * gam - t_ref[...]
        dy = e * (1.0 / D_MODEL)
        dxn, dgam = _rms_bwd(dy, xh, r, gam)
        dx_ref[...] = dxn

        @pl.when(pl.program_id(0) == 0)
        def _():
            dgam_ref[...] = jnp.zeros_like(dgam_ref)
            loss_ref[...] = jnp.zeros_like(loss_ref)

        dgam_ref[...] += dgam
        loss_ref[...] += _colsum8(e * e) * (0.5 / D_MODEL)

    return pl.pallas_call(
        body,
        name="loss_head",
        grid=(t // TM,),
        in_specs=[_rows(TM, D_MODEL), _resident((1, D_MODEL)), _rows(TM, D_MODEL)],
        out_specs=[_rows(TM, D_MODEL), pl.BlockSpec((8, D_MODEL), lambda i: (0, 0)),
                   pl.BlockSpec((8, D_MODEL), lambda i: (0, 0))],
        out_shape=[jax.ShapeDtypeStruct((t, D_MODEL), F32), jax.ShapeDtypeStruct((8, D_MODEL), F32),
                   jax.ShapeDtypeStruct((8, D_MODEL), F32)],
        compiler_params=_cparams(("arbitrary",)),
    )(x, gamma, target)


def _place():
    x, y, c = lax.axis_index("x"), lax.axis_index("y"), lax.axis_index("c")
    chips = [(1 - x, y), (x, 1 - y), (1 - x, 1 - y)]
    return x, y, c, chips


class _Gather:
    per = 8

    def __init__(self, shards):
        n = len(shards)
        self.inputs = list(shards)
        self.out_shape = [jax.ShapeDtypeStruct((N_DEV * s.shape[0], s.shape[1]), s.dtype) for s in shards]
        self.scratch = [pltpu.SemaphoreType.DMA((n * self.per,)), pltpu.SemaphoreType.DMA((n * self.per,)),
                        pltpu.SemaphoreType.DMA((n,))]
        self.result = None

    def _parts(self, ins, outs, sems):
        send_sems, recv_sems, local_sems = sems
        x, y, c, chips = _place()
        me, sibling = (x, y, c), (x, y, 1 - c)
        xn, yn, dg = chips
        n = len(ins)

        def rows(k, p, part=None):
            r = ins[k].shape[0]
            base = (4 * p[0] + 2 * p[1] + p[2]) * r
            if part is None:
                return outs[k].at[pl.ds(base, r), :]
            return outs[k].at[pl.ds(base + part * (r // 2), r // 2), :]

        def copy(k, slot, block, to, src=None, part=None):
            return pltpu.make_async_remote_copy(
                src_ref=rows(k, block, part) if src is None else src, dst_ref=rows(k, block, part),
                send_sem=send_sems.at[k * self.per + slot], recv_sem=recv_sems.at[k * self.per + slot],
                device_id=to, device_id_type=MESH)

        mine = [pltpu.make_async_copy(ins[k], rows(k, me), local_sems.at[k]) for k in range(n)]
        sends, lands = [], []
        for k in range(n):
            sends.append({
                0: copy(k, 0, me, sibling, src=ins[k]),
                1: copy(k, 1, me, (*xn, c), src=ins[k]),
                2: copy(k, 2, me, (*yn, c), src=ins[k]),
                3: copy(k, 3, (*xn, c), (*yn, c), part=0),
                4: copy(k, 4, (*yn, c), (*xn, c), part=1),
                5: copy(k, 5, (*xn, c), sibling),
                6: copy(k, 6, (*yn, c), sibling),
                7: copy(k, 7, (*dg, c), sibling)})
            lands.append({
                0: copy(k, 0, sibling, me),
                1: copy(k, 1, (*xn, c), me),
                2: copy(k, 2, (*yn, c), me),
                3: copy(k, 3, (*dg, c), me, part=0),
                4: copy(k, 4, (*dg, c), me, part=1),
                5: copy(k, 5, (*xn, 1 - c), me),
                6: copy(k, 6, (*yn, 1 - c), me),
                7: copy(k, 7, (*dg, 1 - c), me)})
        return n, mine, sends, lands

    def start(self, ins, outs, sems):
        n, mine, sends, _ = self._parts(ins, outs, sems)
        for cp in mine:
            cp.start()
        for slot in (0, 1, 2):
            for k in range(n):
                sends[k][slot].start()

    def relay(self, ins, outs, sems):
        n, _, sends, lands = self._parts(ins, outs, sems)
        for k in range(n):
            lands[k][1].wait_recv()
            sends[k][3].start()
            sends[k][5].start()
        for k in range(n):
            lands[k][2].wait_recv()
            sends[k][4].start()
            sends[k][6].start()

    def forward(self, ins, outs, sems):
        n, _, sends, lands = self._parts(ins, outs, sems)
        for k in range(n):
            lands[k][3].wait_recv()
            lands[k][4].wait_recv()
            sends[k][7].start()

    def finish(self, ins, outs, sems):
        n, mine, sends, lands = self._parts(ins, outs, sems)
        for k in range(n):
            for slot in (0, 5, 6, 7):
                lands[k][slot].wait_recv()
        for k in range(n):
            for slot in range(self.per):
                sends[k][slot].wait_send()
        for cp in mine:
            cp.wait()


class _PairExchange:
    def __init__(self, grads):
        n = len(grads)
        self.inputs = list(grads)
        self.out_shape = [jax.ShapeDtypeStruct((g.shape[0] // 2, g.shape[1]), g.dtype) for g in grads]
        self.scratch = [pltpu.SemaphoreType.DMA((n * N_CHIP,)), pltpu.SemaphoreType.DMA((n * N_CHIP,))]
        self.result = None

    def _copies(self, ins, outs, sems):
        send_sems, recv_sems = sems
        x, y, c, _ = _place()
        copies = []
        for k in range(len(ins)):
            r = ins[k].shape[0] // N_DEV
            for q in range(N_CHIP):
                copies.append(pltpu.make_async_remote_copy(
                    src_ref=ins[k].at[pl.ds((2 * q + 1 - c) * r, r), :], dst_ref=outs[k].at[pl.ds(q * r, r), :],
                    send_sem=send_sems.at[k * N_CHIP + q], recv_sem=recv_sems.at[k * N_CHIP + q],
                    device_id=(x, y, 1 - c), device_id_type=MESH))
        return copies

    def start(self, ins, outs, sems):
        for cp in self._copies(ins, outs, sems):
            cp.start()

    def relay(self, ins, outs, sems):
        pass

    def forward(self, ins, outs, sems):
        pass

    def finish(self, ins, outs, sems):
        copies = self._copies(ins, outs, sems)
        for cp in copies:
            cp.wait_recv()
        for cp in copies:
            cp.wait_send()


class _ChipExchange(_PairExchange):
    def __init__(self, psums):
        n = len(psums)
        self.inputs = list(psums)
        self.out_shape = [jax.ShapeDtypeStruct((3 * p.shape[0] // N_CHIP, p.shape[1]), p.dtype) for p in psums]
        self.scratch = [pltpu.SemaphoreType.DMA((n * 3,)), pltpu.SemaphoreType.DMA((n * 3,))]
        self.result = None

    def _copies(self, ins, outs, sems):
        send_sems, recv_sems = sems
        _, _, c, chips = _place()
        copies = []
        for k in range(len(ins)):
            r = ins[k].shape[0] // N_CHIP
            for j, chip in enumerate(chips):
                copies.append(pltpu.make_async_remote_copy(
                    src_ref=ins[k].at[pl.ds((2 * chip[0] + chip[1]) * r, r), :], dst_ref=outs[k].at[pl.ds(j * r, r), :],
                    send_sem=send_sems.at[k * 3 + j], recv_sem=recv_sems.at[k * 3 + j],
                    device_id=(*chip, c), device_id_type=MESH))
        return copies


def _exchange_alone(xchg, name):
    n_in, n_out = len(xchg.inputs), len(xchg.out_shape)

    def body(*refs):
        ins, outs, sems = refs[:n_in], refs[n_in:n_in + n_out], refs[n_in + n_out:]
        xchg.start(ins, outs, sems)
        xchg.relay(ins, outs, sems)
        xchg.forward(ins, outs, sems)
        xchg.finish(ins, outs, sems)

    xchg.result = list(pl.pallas_call(
        body, name=name, in_specs=[_hbm()] * n_in, out_specs=[_hbm()] * n_out, out_shape=xchg.out_shape,
        scratch_shapes=xchg.scratch)(*xchg.inputs))
    return xchg.result


def _pair_sum(core, grads, recvd, name):
    n = len(grads)
    r = grads[0].shape[0] // N_DEV
    cdim = grads[0].shape[1]
    tr = r // 2 if r % 32 == 0 else r
    nt = r // tr

    def body(core_ref, *refs):
        del core_ref
        for k in range(n):
            refs[2 * n + k][...] = (refs[k][...].astype(F32) + refs[n + k][...].astype(F32)).astype(BF16)

    gspec = pl.BlockSpec((tr, cdim), lambda q, i, core_ref: ((2 * q + core_ref[0]) * nt + i, 0))
    rspec = pl.BlockSpec((tr, cdim), lambda q, i, core_ref: (q * nt + i, 0))
    return pl.pallas_call(
        body,
        name=name,
        grid_spec=pltpu.PrefetchScalarGridSpec(
            num_scalar_prefetch=1, grid=(N_CHIP, nt), in_specs=[gspec] * n + [rspec] * n, out_specs=[rspec] * n),
        out_shape=[jax.ShapeDtypeStruct((N_CHIP * r, cdim), BF16) for _ in range(n)],
        compiler_params=_cparams(("parallel", "parallel")),
    )(core, *grads, *recvd)


def _final_sum(chip, psums, recvd, name):
    n = len(psums)
    r = psums[0].shape[0] // N_CHIP
    cdim = psums[0].shape[1]
    tr = r // 2 if r % 32 == 0 else r
    nt = r // tr

    def body(chip_ref, *refs):
        del chip_ref
        for k in range(n):
            got = refs[n + k]
            tot = refs[k][...].astype(F32) + got[0].astype(F32)
            tot = tot + got[1].astype(F32)
            tot = tot + got[2].astype(F32)
            refs[2 * n + k][...] = tot

    pspec = pl.BlockSpec((tr, cdim), lambda i, chip_ref: (chip_ref[0] * nt + i, 0))
    rspec = pl.BlockSpec((3, tr, cdim), lambda i, chip_ref: (0, i, 0))
    ospec = pl.BlockSpec((tr, cdim), lambda i, chip_ref: (i, 0))
    return pl.pallas_call(
        body,
        name=name,
        grid_spec=pltpu.PrefetchScalarGridSpec(
            num_scalar_prefetch=1, grid=(nt,), in_specs=[pspec] * n + [rspec] * n, out_specs=[ospec] * n),
        out_shape=[jax.ShapeDtypeStruct((r, cdim), F32) for _ in range(n)],
        compiler_params=_cparams(("parallel",)),
    )(chip, *psums, *[g.reshape(3, r, cdim) for g in recvd])


SMALL_ROWS = 16


def _all_reduce_small(part):
    def body(p_ref, o_ref, buf, send_sems, recv_sems):
        x, y, c, _ = _place()
        me = 4 * x + 2 * y + c
        buf[me] = p_ref[...]
        copies = []
        for d in range(1, N_DEV):
            peer = me ^ d
            copies.append(pltpu.make_async_remote_copy(
                src_ref=p_ref, dst_ref=buf.at[me], send_sem=send_sems.at[d - 1], recv_sem=recv_sems.at[d - 1],
                device_id=(peer // 4, (peer // 2) % 2, peer % 2), device_id_type=MESH))
        for cp in copies:
            cp.start()
        for cp in copies:
            cp.wait_recv()
        for cp in copies:
            cp.wait_send()
        tot = buf[0]
        for d in range(1, N_DEV):
            tot = tot + buf[d]
        o_ref[...] = tot

    return pl.pallas_call(
        body,
        name="all_reduce_small",
        in_specs=[pl.BlockSpec(memory_space=pltpu.VMEM)],
        out_specs=pl.BlockSpec(memory_space=pltpu.VMEM),
        out_shape=jax.ShapeDtypeStruct(part.shape, F32),
        scratch_shapes=[pltpu.VMEM((N_DEV,) + part.shape, F32), pltpu.SemaphoreType.DMA((N_DEV - 1,)),
                        pltpu.SemaphoreType.DMA((N_DEV - 1,))],
    )(part)


ADAMW_STEPS = 4


def _adamw(ws, gs, ms, vs, name, hosted=()):
    n = len(ws)
    steps = ADAMW_STEPS if all(w.shape[0] % (8 * ADAMW_STEPS) == 0 for w in ws) else 1
    c1 = 1.0 - ADAM_B1 ** ADAM_STEP
    c2 = 1.0 - ADAM_B2 ** ADAM_STEP

    def body(*refs):
        for k in range(n):
            w, g, m, v = (refs[j * n + k][...] for j in range(4))
            m2 = ADAM_B1 * m + (1.0 - ADAM_B1) * g
            v2 = ADAM_B2 * v + (1.0 - ADAM_B2) * (g * g)
            delta = -ADAM_LR * ((m2 / c1) / (jnp.sqrt(v2 / c2) + ADAM_EPS) + ADAM_WD * w)
            refs[4 * n + k][...] = delta
            refs[5 * n + k][...] = m2
            refs[6 * n + k][...] = v2

    specs = [pl.BlockSpec((w.shape[0] // steps, w.shape[1]), lambda i: (i, 0)) for w in ws]
    shapes = [jax.ShapeDtypeStruct(w.shape, F32) for w in ws]
    outs = _call(
        body,
        name=name,
        grid=(steps,),
        in_specs=specs * 4,
        out_specs=specs * 3,
        out_shape=shapes * 3,
        args=(*ws, *gs, *ms, *vs), sem=("parallel",), hosted=hosted)
    return outs[:n], outs[n:2 * n], outs[2 * n:]


def _bias_b():
    pad = B_PREV * CHUNK
    slopes = np.array([2.0 ** (-8.0 * (i + 1) / B_Q_HEADS) for i in range(B_Q_HEADS)], dtype=np.float32)
    dist = np.abs(np.arange(TQ)[:, None] - np.arange(TQ + pad)[None, :] + pad).astype(np.float32)
    bias = -slopes.reshape(B_Q_HEADS, 1, 1) * dist[None]
    qc = (np.arange(TQ)[:, None] + pad) // CHUNK
    kc = np.arange(TQ + pad)[None, :] // CHUNK
    allowed = (kc <= qc) & (kc >= qc - B_PREV)
    return np.where(allowed[None], bias, np.float32(NEG_INF)).astype(np.float32)


def kernel(x, ffn1_norm, ffn1_w_gate, ffn1_w_up, ffn1_w_down, mix_norm, w_in, rel_bias, sinks, w_proj_a, w_proj_b, w_out, ffn2_norm, ffn2_w_gate, ffn2_w_up, ffn2_w_down, final_norm, loss_target, m_ffn1_norm, m_ffn1_w_gate, m_ffn1_w_up, m_ffn1_w_down, m_mix_norm, m_w_in, m_rel_bias, m_sinks, m_w_proj_a, m_w_proj_b, m_w_out, m_ffn2_norm, m_ffn2_w_gate, m_ffn2_w_up, m_ffn2_w_down, m_final_norm, v_ffn1_norm, v_ffn1_w_gate, v_ffn1_w_up, v_ffn1_w_down, v_mix_norm, v_w_in, v_rel_bias, v_sinks, v_w_proj_a, v_w_proj_b, v_w_out, v_ffn2_norm, v_ffn2_w_gate, v_ffn2_w_up, v_ffn2_w_down, v_final_norm):
    bsz, s_len, _ = x.shape
    t = bsz * s_len
    core = lax.axis_index("c").astype(jnp.int32).reshape(1)
    chip = (2 * lax.axis_index("x") + lax.axis_index("y")).astype(jnp.int32).reshape(1)

    def row_form(w):
        return w.astype(BF16).T

    wg1, wu1, wd1 = _exchange_alone(
        _Gather([row_form(ffn1_w_gate), row_form(ffn1_w_up), ffn1_w_down.astype(BF16)]), "gather_ffn1")
    gather_mix = _Gather([row_form(w_in), jnp.concatenate([row_form(w_proj_a), row_form(w_proj_b)], axis=1),
                          w_out.astype(BF16)])
    gather_ffn2_gate = _Gather([row_form(ffn2_w_gate)])
    gather_ffn2_rest = _Gather([row_form(ffn2_w_up), ffn2_w_down.astype(BF16)])

    x0 = x.reshape(t, D_MODEL)
    tgt = loss_target.reshape(t, D_MODEL)
    gam1, gam2, gam3, gam4 = (g.reshape(1, D_MODEL) for g in (ffn1_norm, mix_norm, ffn2_norm, final_norm))

    h1, g1, u1, a1, x1 = _ffn_fwd(x0, gam1, wg1, wu1, wd1, "ffn1_fwd", hosted=[gather_mix])
    win_t, proj_t, wout = gather_mix.result
    h2, qkv_a, qkv_b, gates = _proj_fwd(x1, gam2, win_t, hosted=[gather_ffn2_gate])
    (wg2,) = gather_ffn2_gate.result
    qkv_a3 = qkv_a.reshape(bsz, s_len, QKV_A)
    qkv_b3 = qkv_b.reshape(bsz, s_len, QKV_B)

    far = jnp.broadcast_to(rel_bias[:, REL_TABLE - 1:REL_TABLE], (A_HEADS, REL_WRAP // 2))
    tv = jnp.concatenate([far, jnp.flip(rel_bias, axis=1), jnp.zeros((A_HEADS, REL_WRAP // 2 - REL_TABLE), F32)], axis=1)
    bias_a = _bias_a_build(tv.reshape(A_HEADS, 1, REL_WRAP))
    bias_b = jnp.asarray(_bias_b())
    sink_rows = jnp.broadcast_to(sinks.reshape(B_Q_HEADS, 1, 1), (B_Q_HEADS, 8, LANES))

    oa = _attn_a_fwd(qkv_a3, bias_a, hosted=[gather_ffn2_rest]).reshape(t, A_WIDTH)
    wu2, wd2 = gather_ffn2_rest.result
    ob = _attn_b_fwd(qkv_b3, bias_b, sink_rows).reshape(t, B_Q_WIDTH)
    x2, ya, yb, mg = _mix_out_fwd(x1, oa, ob, gates, proj_t, wout)
    h3, g2, u2, a2, x3 = _ffn_fwd(x2, gam3, wg2, wu2, wd2, "ffn2_fwd")

    dx3, dgam4, loss_part = _loss_head(x3, gam4, tgt)

    dx2, dg2, du2, db2, dgam3 = _ffn_bwd(dx3, x2, gam3, g2, u2, wg2, wu2, wd2, "ffn2_bwd")
    gw_ffn2 = [_mm_tn([dg2], h3, "grad_ffn2_gate"), _mm_tn([du2], h3, "grad_ffn2_up"),
               _mm_tn([a2], db2, "grad_ffn2_down")]
    pairx_ffn2 = _PairExchange(gw_ffn2)
    dxb, dya, dyb, doa, dob, dgates = _mix_out_bwd(dx2, gates, ya, yb, proj_t, wout, hosted=[pairx_ffn2])
    psum_ffn2 = _pair_sum(core, gw_ffn2, pairx_ffn2.result, "pair_sum_ffn2")
    gw_out = _mm_tn([mg], dxb, "grad_w_out")
    gw_proj = _mm_tn_proj(dya, dyb, oa, ob)

    chipx_ffn2 = _ChipExchange(psum_ffn2)
    dqa, dka, dva, dbias_a = _attn_a_bwd(qkv_a3, bias_a, doa.reshape(bsz, s_len, A_WIDTH), hosted=[chipx_ffn2])
    pairx_out = _PairExchange([gw_proj, gw_out])
    dqb, dkvb, dsink = _attn_b_bwd(qkv_b3, bias_b, sink_rows, dob.reshape(bsz, s_len, B_Q_WIDTH), hosted=[pairx_out])
    drel_lanes = _relbias_grad(dbias_a)
    dproj = [dqa.reshape(t, A_WIDTH), dka.reshape(t, A_WIDTH), dva.reshape(t, A_WIDTH), dqb.reshape(t, B_Q_WIDTH),
             dkvb.reshape(t, 2 * B_KV_WIDTH), dgates]

    gw_in = _mm_tn(dproj, h2, "grad_w_in")
    pairx_in = _PairExchange([gw_in])
    psum_out = _pair_sum(core, [gw_proj, gw_out], pairx_out.result, "pair_sum_mix")
    chipx_out = _ChipExchange(psum_out)
    dx1, db1, dgam2 = _proj_bwd(dx2, x1, gam2, dproj, win_t, hosted=[pairx_in, chipx_out])
    psum_in = _pair_sum(core, [gw_in], pairx_in.result, "pair_sum_w_in")
    gw_d1 = _mm_tn([a1], db1, "grad_ffn1_down")

    chipx_in = _ChipExchange(psum_in)
    pairx_d1 = _PairExchange([gw_d1])
    dg1, du1 = _ffn_bwd_act(dx1, g1, u1, wd1, "ffn1_bwd_act", hosted=[chipx_in, pairx_d1])
    psum_d1 = _pair_sum(core, [gw_d1], pairx_d1.result, "pair_sum_ffn1_down")
    chipx_d1 = _ChipExchange(psum_d1)
    gw_g1 = _mm_tn([dg1], h1, "grad_ffn1_gate", hosted=[chipx_d1])
    from_sibling_g1 = _exchange_alone(_PairExchange([gw_g1]), "pair_exchange_ffn1_gate")
    psum_g1 = _pair_sum(core, [gw_g1], from_sibling_g1, "pair_sum_ffn1_gate")
    chipx_g1 = _ChipExchange(psum_g1)
    gw_u1 = _mm_tn([du1], h1, "grad_ffn1_up", hosted=[chipx_g1])
    from_sibling_u1 = _exchange_alone(_PairExchange([gw_u1]), "pair_exchange_ffn1_up")
    psum_u1 = _pair_sum(core, [gw_u1], from_sibling_u1, "pair_sum_ffn1_up")
    chipx_u1 = _ChipExchange(psum_u1)
    dx0, dgam1 = _ffn_bwd_in(dx1, x0, gam1, dg1, du1, wg1, wu1, "ffn1_bwd_in", hosted=[chipx_u1])

    g_g1, g_u1, g_d1, g_g2, g_u2, g_d2 = _final_sum(
        chip, psum_g1 + psum_u1 + psum_d1 + psum_ffn2,
        chipx_g1.result + chipx_u1.result + chipx_d1.result + chipx_ffn2.result, "grad_sum_ffn")
    (g_in,) = _final_sum(chip, psum_in, chipx_in.result, "grad_sum_w_in")
    g_proj, g_out = _final_sum(chip, psum_out, chipx_out.result, "grad_sum_mix")
    grads = {
        "ffn1_w_gate": g_g1.T, "ffn1_w_up": g_u1.T, "ffn1_w_down": g_d1, "w_in": g_in.T,
        "w_proj_a": g_proj[:, 0:A_WIDTH].T, "w_proj_b": g_proj[:, A_WIDTH:].T, "w_out": g_out,
        "ffn2_w_gate": g_g2.T, "ffn2_w_up": g_u2.T, "ffn2_w_down": g_d2,
    }

    def row_of(v):
        return jnp.pad(v.reshape(1, -1), ((0, 0), (0, D_MODEL - v.size)))

    def table_rows(v):
        return jnp.pad(v, ((0, 0), (0, D_MODEL - REL_TABLE)))

    drel_local = jnp.flip(drel_lanes[:, 0, 0:REL_TABLE], axis=1)
    small_part = jnp.concatenate(
        [jnp.sum(dgam1, axis=0, keepdims=True), jnp.sum(dgam2, axis=0, keepdims=True),
         jnp.sum(dgam3, axis=0, keepdims=True), jnp.sum(dgam4, axis=0, keepdims=True),
         row_of(jnp.sum(loss_part)), row_of(dsink[:, 0, 0]), jnp.zeros((2, D_MODEL), F32),
         table_rows(drel_local)], axis=0)
    small = _all_reduce_small(small_part)
    loss = small[4, 0]

    def pack(n1, n2, n3, n4, sk, tb):
        return jnp.concatenate([n1.reshape(1, -1), n2.reshape(1, -1), n3.reshape(1, -1), n4.reshape(1, -1),
                                jnp.zeros((1, D_MODEL), F32), row_of(sk), jnp.zeros((2, D_MODEL), F32), table_rows(tb)],
                               axis=0)

    live = np.zeros((SMALL_ROWS, D_MODEL), np.float32)
    live[0:4] = 1.0
    live[5, 0:B_Q_HEADS] = 1.0
    live[8:16, 0:REL_TABLE] = 1.0
    small_g = small * jnp.asarray(live)
    sw = pack(ffn1_norm, mix_norm, ffn2_norm, final_norm, sinks, rel_bias)
    sm = pack(m_ffn1_norm, m_mix_norm, m_ffn2_norm, m_final_norm, m_sinks, m_rel_bias)
    sv = pack(v_ffn1_norm, v_mix_norm, v_ffn2_norm, v_final_norm, v_sinks, v_rel_bias)
    (sd,), (snm,), (snv,) = _adamw([sw], [small_g], [sm], [sv], "adamw_small")

    def unpack(p):
        return {"ffn1_norm": p[0], "mix_norm": p[1], "ffn2_norm": p[2], "final_norm": p[3],
                "sinks": p[5, 0:B_Q_HEADS], "rel_bias": p[8:16, 0:REL_TABLE]}

    grads.update(unpack(small_g))
    delta, new_m, new_v = unpack(sd), unpack(snm), unpack(snv)

    wmv = {
        "ffn1_w_gate": (ffn1_w_gate, m_ffn1_w_gate, v_ffn1_w_gate), "ffn1_w_up": (ffn1_w_up, m_ffn1_w_up, v_ffn1_w_up),
        "ffn1_w_down": (ffn1_w_down, m_ffn1_w_down, v_ffn1_w_down), "w_in": (w_in, m_w_in, v_w_in),
        "w_proj_a": (w_proj_a, m_w_proj_a, v_w_proj_a), "w_proj_b": (w_proj_b, m_w_proj_b, v_w_proj_b),
        "w_out": (w_out, m_w_out, v_w_out),
        "ffn2_w_gate": (ffn2_w_gate, m_ffn2_w_gate, v_ffn2_w_gate), "ffn2_w_up": (ffn2_w_up, m_ffn2_w_up, v_ffn2_w_up),
        "ffn2_w_down": (ffn2_w_down, m_ffn2_w_down, v_ffn2_w_down),
    }
    def adamw_group(gname, names, hosted=()):
        ds_, ms_, vs_ = _adamw([wmv[n][0] for n in names], [grads[n] for n in names], [wmv[n][1] for n in names],
                               [wmv[n][2] for n in names], gname, hosted=hosted)
        for n, d_, m_, v_ in zip(names, ds_, ms_, vs_):
            delta[n], new_m[n], new_v[n] = d_, m_, v_

    adamw_group("adamw_ffn_up", ["ffn1_w_gate", "ffn1_w_up", "ffn2_w_gate", "ffn2_w_up"])
    adamw_group("adamw_rest", ["ffn1_w_down", "ffn2_w_down", "w_in", "w_proj_a", "w_proj_b", "w_out"])

    order = ["ffn1_norm", "ffn1_w_gate", "ffn1_w_up", "ffn1_w_down", "mix_norm", "w_in", "rel_bias", "sinks",
             "w_proj_a", "w_proj_b", "w_out", "ffn2_norm", "ffn2_w_gate", "ffn2_w_up", "ffn2_w_down", "final_norm"]
    grad_x = dx0.reshape(bsz, s_len, D_MODEL)
    return (loss, grad_x, *[grads[n] for n in order], *[delta[n] for n in order], *[new_m[n] for n in order],
            *[new_v[n] for n in order])
```

```python
import numpy as np
import jax
import jax.numpy as jnp
from jax import lax
from jax.experimental import pallas as pl
from jax.experimental.pallas import tpu as pltpu

F32 = jnp.float32
BF16 = jnp.bfloat16

D_MODEL = 1024
D_FF = 2816
CHUNK = 64
D_HEAD = 64
A_HEADS = 8
A_PREV = 8
MAX_REL = 128
B_Q_HEADS = 8
B_KV_HEADS = 2
B_GROUP = B_Q_HEADS // B_KV_HEADS
B_PREV = 2
REL_TABLE = (CHUNK - 1) + MAX_REL + 1
A_WIDTH = A_HEADS * D_HEAD
B_Q_WIDTH = B_Q_HEADS * D_HEAD
B_KV_WIDTH = B_KV_HEADS * D_HEAD
QKV_A = 3 * A_WIDTH
QKV_B = B_Q_WIDTH + 2 * B_KV_WIDTH
IN_WIDTH = QKV_A + QKV_B + 2 * D_MODEL
EPS = 1e-6
NEG_INF = -1e30
SCALE = 1.0 / 8.0

ADAM_LR = 0.001
ADAM_B1 = 0.9
ADAM_B2 = 0.999
ADAM_EPS = 1e-08
ADAM_WD = 0.01
ADAM_STEP = 10

N_DEV = 8
N_CHIP = 4
MESH = pl.DeviceIdType.MESH

LANES = 128
TQ = 256
TM = 256
TM_FWD = 256
FC = 256
VMEM_LIMIT = 56 << 20


def _cparams(sem, vmem=VMEM_LIMIT):
    return pltpu.CompilerParams(dimension_semantics=sem, vmem_limit_bytes=vmem)


def _dot_nt(a, b):
    return lax.dot_general(a, b, (((1,), (1,)), ((), ())), preferred_element_type=F32)


def _dot_nn(a, b):
    return lax.dot_general(a, b, (((1,), (0,)), ((), ())), preferred_element_type=F32)


def _dot_tn(a, b):
    return lax.dot_general(a, b, (((0,), (0,)), ((), ())), preferred_element_type=F32)


def _resident(shape):
    nd = len(shape)
    return pl.BlockSpec(shape, lambda *_: (0,) * nd, pipeline_mode=pl.Buffered(1))


def _rows(tm, width):
    return pl.BlockSpec((tm, width), lambda i: (i, 0))


def _colsum8(v):
    tm, n = v.shape
    return jnp.sum(v.reshape(tm // 8, 8, n), axis=0)


def _rms(x):
    r = lax.rsqrt(jnp.mean(x * x, axis=-1, keepdims=True) + EPS)
    return x * r, r


def _rms_bwd(dh, xh, r, gamma):
    dxh = dh * gamma
    dx = r * (dxh - xh * jnp.mean(dxh * xh, axis=-1, keepdims=True))
    return dx, _colsum8(dh * xh)


def _hbm():
    return pl.BlockSpec(memory_space=pltpu.HBM)


def _call(body, *, name, grid, in_specs, out_specs, out_shape, args, sem, scratch_shapes=(), hosted=()):
    in_specs, out_specs, out_shape = list(in_specs), list(out_specs), list(out_shape)
    scratch_shapes = list(scratch_shapes)
    if not hosted:
        return pl.pallas_call(body, name=name, grid=grid, in_specs=in_specs, out_specs=out_specs, out_shape=out_shape,
                              scratch_shapes=scratch_shapes, compiler_params=_cparams(sem))(*args)
    n_in, n_out, n_scr = len(in_specs), len(out_specs), len(scratch_shapes)
    x_in = [a for x in hosted for a in x.inputs]
    x_out = [s for x in hosted for s in x.out_shape]
    x_scr = [s for x in hosted for s in x.scratch]
    steps = int(np.prod(grid))
    forward_step = max(steps - 3, 0)
    relay_step = min((5 * steps) // 8, forward_step)

    def wrapped(*refs):
        pos = [0]

        def take(k):
            pos[0] += k
            return refs[pos[0] - k:pos[0]]

        ins, xin, outs, xout, scr, xscr = (take(k) for k in (n_in, len(x_in), n_out, len(x_out), n_scr, len(x_scr)))
        step = 0
        for axis, extent in enumerate(grid):
            step = step * extent + pl.program_id(axis)
        own, oi, oo, osc = [], 0, 0, 0
        for x in hosted:
            own.append((xin[oi:oi + len(x.inputs)], xout[oo:oo + len(x.out_shape)], xscr[osc:osc + len(x.scratch)]))
            oi, oo, osc = oi + len(x.inputs), oo + len(x.out_shape), osc + len(x.scratch)

        def phase(method):
            for x, (i_, o_, s_) in zip(hosted, own):
                getattr(x, method)(i_, o_, s_)

        pl.when(step == 0)(lambda: phase("start"))
        body(*ins, *outs, *scr)
        pl.when(step == relay_step)(lambda: phase("relay"))
        pl.when(step == forward_step)(lambda: phase("forward"))
        pl.when(step == steps - 1)(lambda: phase("finish"))

    res = pl.pallas_call(
        wrapped, name=name, grid=grid, in_specs=in_specs + [_hbm()] * len(x_in),
        out_specs=out_specs + [_hbm()] * len(x_out), out_shape=out_shape + x_out,
        scratch_shapes=scratch_shapes + x_scr, compiler_params=_cparams(("arbitrary",) * len(grid)))(*args, *x_in)
    rest = list(res[n_out:])
    for x in hosted:
        x.result, rest = rest[:len(x.out_shape)], rest[len(x.out_shape):]
    return list(res[:n_out])


def _ffn_fwd(x, gamma, wg_t, wu_t, wd, name, hosted=()):
    t = x.shape[0]
    f = wg_t.shape[0]

    def body(x_ref, gam_ref, wg_ref, wu_ref, wd_ref, h_ref, g_ref, u_ref, a_ref, y_ref):
        xv = x_ref[...]
        xh, _ = _rms(xv)
        h = (xh * gam_ref[...]).astype(BF16)
        h_ref[...] = h
        for j in range(f // FC):
            sl = slice(j * FC, (j + 1) * FC)
            g = _dot_nt(h, wg_ref[sl, :])
            u = _dot_nt(h, wu_ref[sl, :])
            g_ref[:, sl] = g.astype(BF16)
            u_ref[:, sl] = u.astype(BF16)
            a_ref[:, sl] = (g * jax.nn.sigmoid(g) * u).astype(BF16)
        y_ref[...] = xv + 0.5 * _dot_nn(a_ref[...], wd_ref[...])

    return _call(
        body,
        name=name,
        grid=(t // TM_FWD,),
        in_specs=[_rows(TM_FWD, D_MODEL), _resident((1, D_MODEL)), _resident((f, D_MODEL)), _resident((f, D_MODEL)),
                  _resident((f, D_MODEL))],
        out_specs=[_rows(TM_FWD, D_MODEL), _rows(TM_FWD, f), _rows(TM_FWD, f), _rows(TM_FWD, f),
                   _rows(TM_FWD, D_MODEL)],
        out_shape=[jax.ShapeDtypeStruct((t, D_MODEL), BF16), jax.ShapeDtypeStruct((t, f), BF16),
                   jax.ShapeDtypeStruct((t, f), BF16), jax.ShapeDtypeStruct((t, f), BF16),
                   jax.ShapeDtypeStruct((t, D_MODEL), F32)],
        args=(x, gamma, wg_t, wu_t, wd), sem=("parallel",), hosted=hosted)


def _ffn_bwd(d, x, gamma, g_act, u_act, wg_t, wu_t, wd, name, hosted=()):
    t = x.shape[0]
    f = wg_t.shape[0]

    def body(d_ref, x_ref, gam_ref, g_ref, u_ref, wg_ref, wu_ref, wd_ref, dx_ref, dg_ref, du_ref, db_ref, dgam_ref):
        dv = d_ref[...]
        db = (0.5 * dv).astype(BF16)
        db_ref[...] = db
        for j in range(f // FC):
            sl = slice(j * FC, (j + 1) * FC)
            da = _dot_nt(db, wd_ref[sl, :])
            g = g_ref[:, sl].astype(F32)
            u = u_ref[:, sl].astype(F32)
            s = jax.nn.sigmoid(g)
            dg_ref[:, sl] = (da * u * (s * (1.0 + g * (1.0 - s)))).astype(BF16)
            du_ref[:, sl] = (da * (g * s)).astype(BF16)
        dh = _dot_nn(dg_ref[...], wg_ref[...]) + _dot_nn(du_ref[...], wu_ref[...])
        xh, r = _rms(x_ref[...])
        dxn, dgam = _rms_bwd(dh, xh, r, gam_ref[...])
        dx_ref[...] = dv + dxn

        @pl.when(pl.program_id(0) == 0)
        def _():
            dgam_ref[...] = jnp.zeros_like(dgam_ref)

        dgam_ref[...] += dgam

    return _call(
        body,
        name=name,
        grid=(t // TM,),
        in_specs=[_rows(TM, D_MODEL), _rows(TM, D_MODEL), _resident((1, D_MODEL)), _rows(TM, f), _rows(TM, f),
                  _resident((f, D_MODEL)), _resident((f, D_MODEL)), _resident((f, D_MODEL))],
        out_specs=[_rows(TM, D_MODEL), _rows(TM, f), _rows(TM, f), _rows(TM, D_MODEL),
                   pl.BlockSpec((8, D_MODEL), lambda i: (0, 0))],
        out_shape=[jax.ShapeDtypeStruct((t, D_MODEL), F32), jax.ShapeDtypeStruct((t, f), BF16),
                   jax.ShapeDtypeStruct((t, f), BF16), jax.ShapeDtypeStruct((t, D_MODEL), BF16),
                   jax.ShapeDtypeStruct((8, D_MODEL), F32)],
        args=(d, x, gamma, g_act, u_act, wg_t, wu_t, wd), sem=("arbitrary",), hosted=hosted)


def _ffn_bwd_act(d, g_act, u_act, wd, name, hosted=()):
    t = d.shape[0]
    f = wd.shape[0]

    def body(d_ref, g_ref, u_ref, wd_ref, dg_ref, du_ref):
        db = (0.5 * d_ref[...]).astype(BF16)
        for j in range(f // FC):
            sl = slice(j * FC, (j + 1) * FC)
            da = _dot_nt(db, wd_ref[sl, :])
            g = g_ref[:, sl].astype(F32)
            u = u_ref[:, sl].astype(F32)
            s = jax.nn.sigmoid(g)
            dg_ref[:, sl] = (da * u * (s * (1.0 + g * (1.0 - s)))).astype(BF16)
            du_ref[:, sl] = (da * (g * s)).astype(BF16)

    return _call(
        body,
        name=name,
        grid=(t // TM,),
        in_specs=[_rows(TM, D_MODEL), _rows(TM, f), _rows(TM, f), _resident((f, D_MODEL))],
        out_specs=[_rows(TM, f), _rows(TM, f)],
        out_shape=[jax.ShapeDtypeStruct((t, f), BF16), jax.ShapeDtypeStruct((t, f), BF16)],
        args=(d, g_act, u_act, wd), sem=("parallel",), hosted=hosted)


def _ffn_bwd_in(d, x, gamma, dg, du, wg_t, wu_t, name, hosted=()):
    t = x.shape[0]
    f = wg_t.shape[0]

    def body(d_ref, x_ref, gam_ref, dg_ref, du_ref, wg_ref, wu_ref, dx_ref, dgam_ref):
        dh = _dot_nn(dg_ref[...], wg_ref[...]) + _dot_nn(du_ref[...], wu_ref[...])
        xh, r = _rms(x_ref[...])
        dxn, dgam = _rms_bwd(dh, xh, r, gam_ref[...])
        dx_ref[...] = d_ref[...] + dxn

        @pl.when(pl.program_id(0) == 0)
        def _():
            dgam_ref[...] = jnp.zeros_like(dgam_ref)

        dgam_ref[...] += dgam

    return _call(
        body,
        name=name,
        grid=(t // TM,),
        in_specs=[_rows(TM, D_MODEL), _rows(TM, D_MODEL), _resident((1, D_MODEL)), _rows(TM, f), _rows(TM, f),
                  _resident((f, D_MODEL)), _resident((f, D_MODEL))],
        out_specs=[_rows(TM, D_MODEL), pl.BlockSpec((8, D_MODEL), lambda i: (0, 0))],
        out_shape=[jax.ShapeDtypeStruct((t, D_MODEL), F32), jax.ShapeDtypeStruct((8, D_MODEL), F32)],
        args=(d, x, gamma, dg, du, wg_t, wu_t), sem=("arbitrary",), hosted=hosted)


def _mm_tn(pieces, b, name, tile=256, hosted=()):
    t, n = b.shape
    npc = len(pieces)
    counts = [p.shape[1] // tile for p in pieces]
    los = [sum(counts[:k]) for k in range(npc)]
    total = sum(counts)

    def body(*refs):
        a_refs, b_ref, o_ref = refs[:npc], refs[npc], refs[npc + 1]
        i = pl.program_id(0)
        for k in range(npc):
            @pl.when(jnp.logical_and(i >= los[k], i < los[k] + counts[k]))
            def _(k=k):
                o_ref[...] = _dot_tn(a_refs[k][...], b_ref[...]).astype(BF16)

    def a_spec(k):
        return pl.BlockSpec((t, tile), lambda i: (0, jnp.clip(i - los[k], 0, counts[k] - 1)))

    return _call(
        body,
        name=name,
        grid=(total,),
        in_specs=[a_spec(k) for k in range(npc)] + [_resident((t, n))],
        out_specs=[pl.BlockSpec((tile, n), lambda i: (i, 0))],
        out_shape=[jax.ShapeDtypeStruct((total * tile, n), BF16)],
        args=(*pieces, b), sem=("parallel",), hosted=hosted)[0]


def _mm_tn_proj(dya, dyb, oa, ob, tile=256):
    t = dya.shape[0]

    def body(dya_ref, dyb_ref, oa_ref, ob_ref, o_ref):
        o_ref[:, 0:A_WIDTH] = _dot_tn(dya_ref[...], oa_ref[...]).astype(BF16)
        o_ref[:, A_WIDTH:A_WIDTH + B_Q_WIDTH] = _dot_tn(dyb_ref[...], ob_ref[...]).astype(BF16)

    col = pl.BlockSpec((t, tile), lambda i: (0, i))
    return pl.pallas_call(
        body,
        name="grad_proj",
        grid=(D_MODEL // tile,),
        in_specs=[col, col, _resident((t, A_WIDTH)), _resident((t, B_Q_WIDTH))],
        out_specs=pl.BlockSpec((tile, A_WIDTH + B_Q_WIDTH), lambda i: (i, 0)),
        out_shape=jax.ShapeDtypeStruct((D_MODEL, A_WIDTH + B_Q_WIDTH), BF16),
        compiler_params=_cparams(("parallel",)),
    )(dya, dyb, oa, ob)


def _proj_fwd(x, gamma, win_t, hosted=()):
    t = x.shape[0]

    def body(x_ref, gam_ref, w_ref, h_ref, qa_ref, qb_ref, gt_ref):
        xh, _ = _rms(x_ref[...])
        h = (xh * gam_ref[...]).astype(BF16)
        h_ref[...] = h
        for j in range(QKV_A // FC):
            qa_ref[:, j * FC:(j + 1) * FC] = _dot_nt(h, w_ref[j * FC:(j + 1) * FC, :]).astype(BF16)
        for j in range(QKV_B // FC):
            lo = QKV_A + j * FC
            qb_ref[:, j * FC:(j + 1) * FC] = _dot_nt(h, w_ref[lo:lo + FC, :]).astype(BF16)
        for j in range(2 * D_MODEL // FC):
            lo = QKV_A + QKV_B + j * FC
            gt_ref[:, j * FC:(j + 1) * FC] = _dot_nt(h, w_ref[lo:lo + FC, :])

    return _call(
        body,
        name="proj_fwd",
        grid=(t // TM_FWD,),
        in_specs=[_rows(TM_FWD, D_MODEL), _resident((1, D_MODEL)), _resident((IN_WIDTH, D_MODEL))],
        out_specs=[_rows(TM_FWD, D_MODEL), _rows(TM_FWD, QKV_A), _rows(TM_FWD, QKV_B), _rows(TM_FWD, 2 * D_MODEL)],
        out_shape=[jax.ShapeDtypeStruct((t, D_MODEL), BF16), jax.ShapeDtypeStruct((t, QKV_A), BF16),
                   jax.ShapeDtypeStruct((t, QKV_B), BF16), jax.ShapeDtypeStruct((t, 2 * D_MODEL), F32)],
        args=(x, gamma, win_t), sem=("parallel",), hosted=hosted)


def _proj_bwd(d, x, gamma, pieces, win_t, hosted=()):
    t = x.shape[0]
    npc = len(pieces)
    widths = [p.shape[1] for p in pieces]
    los = [sum(widths[:k]) for k in range(npc)]

    def body(*refs):
        d_ref, x_ref, gam_ref = refs[:3]
        p_refs = refs[3:3 + npc]
        w_ref, dx_ref, db_ref, dgam_ref = refs[3 + npc:]
        dh = _dot_nn(p_refs[0][...], w_ref[0:widths[0], :])
        for k in range(1, npc):
            dh += _dot_nn(p_refs[k][...], w_ref[los[k]:los[k] + widths[k], :])
        xh, r = _rms(x_ref[...])
        dxn, dgam = _rms_bwd(dh, xh, r, gam_ref[...])
        dx = d_ref[...] + dxn
        dx_ref[...] = dx
        db_ref[...] = (0.5 * dx).astype(BF16)

        @pl.when(pl.program_id(0) == 0)
        def _():
            dgam_ref[...] = jnp.zeros_like(dgam_ref)

        dgam_ref[...] += dgam

    return _call(
        body,
        name="proj_bwd",
        grid=(t // TM,),
        in_specs=[_rows(TM, D_MODEL), _rows(TM, D_MODEL), _resident((1, D_MODEL))] + [_rows(TM, w) for w in widths]
        + [_resident((IN_WIDTH, D_MODEL))],
        out_specs=[_rows(TM, D_MODEL), _rows(TM, D_MODEL), pl.BlockSpec((8, D_MODEL), lambda i: (0, 0))],
        out_shape=[jax.ShapeDtypeStruct((t, D_MODEL), F32), jax.ShapeDtypeStruct((t, D_MODEL), BF16),
                   jax.ShapeDtypeStruct((8, D_MODEL), F32)],
        args=(d, x, gamma, *pieces, win_t), sem=("arbitrary",), hosted=hosted)


def _lane_half(shape):
    return lax.broadcasted_iota(jnp.int32, shape, len(shape) - 1) // D_HEAD


def _band_softmax(q, kk, bias, sink, qs, pad):
    s = _dot_nt(q, kk) + bias
    if qs is not None:
        col = lax.broadcasted_iota(jnp.int32, s.shape, 1)
        s = jnp.where(col + qs >= pad, s, NEG_INF)
    m = jnp.max(s, axis=-1, keepdims=True)
    if sink is not None:
        m = jnp.maximum(m, sink)
    p = jnp.exp(s - m)
    den = jnp.sum(p, axis=-1, keepdims=True)
    if sink is not None:
        den = den + jnp.exp(sink - m)
    return p, m, 1.0 / den


def _fill_padded(dst, src, pad):
    dst[0:pad, :] = jnp.zeros((pad,) + dst.shape[1:], dst.dtype)
    dst[pad:, :] = src


FWD_PAIRS = 4
BWD_PAIRS = 2


def _attn_a_fwd(qkv, bias, hosted=()):
    bsz, s_len, _ = qkv.shape
    pad = A_PREV * CHUNK
    band = TQ + pad
    pp = FWD_PAIRS
    w = pp * LANES
    nb = A_WIDTH // w

    def body(q_ref, k_ref, v_ref, b_ref, o_ref, kp, vp):
        i = pl.program_id(2)

        @pl.when(i == 0)
        def _():
            _fill_padded(kp, k_ref[...], pad)
            _fill_padded(vp, v_ref[...], pad)

        qs = pl.multiple_of(i * TQ, TQ)
        half = _lane_half((1, LANES))

        def block(masked):
            for pr in range(pp):
                sl = slice(pr * LANES, (pr + 1) * LANES)
                kk = kp[pl.ds(qs, band), sl]
                vv = vp[pl.ds(qs, band), sl]
                q = q_ref[:, sl] * SCALE
                outs = []
                for j in range(2):
                    qm = jnp.where(half == j, q, jnp.zeros_like(q))
                    p, _, inv = _band_softmax(qm, kk, b_ref[2 * pr + j], None, qs if masked else None, pad)
                    outs.append(_dot_nn(p.astype(BF16), vv) * inv)
                o_ref[:, sl] = jnp.where(half == 0, outs[0], outs[1]).astype(BF16)

        pl.when(i < pad // TQ)(lambda: block(True))
        pl.when(i >= pad // TQ)(lambda: block(False))

    return _call(
        body,
        name="attn_a_fwd",
        grid=(bsz, nb, s_len // TQ),
        in_specs=[pl.BlockSpec((None, TQ, w), lambda b, g, i: (b, i, g)),
                  pl.BlockSpec((None, s_len, w), lambda b, g, i: (b, 0, nb + g)),
                  pl.BlockSpec((None, s_len, w), lambda b, g, i: (b, 0, 2 * nb + g)),
                  pl.BlockSpec((2 * pp, TQ, band), lambda b, g, i: (g, 0, 0))],
        out_specs=[pl.BlockSpec((None, TQ, w), lambda b, g, i: (b, i, g))],
        out_shape=[jax.ShapeDtypeStruct((bsz, s_len, A_WIDTH), BF16)],
        scratch_shapes=[pltpu.VMEM((pad + s_len, w), BF16), pltpu.VMEM((pad + s_len, w), BF16)],
        args=(qkv, qkv, qkv, bias), sem=("arbitrary", "arbitrary", "arbitrary"), hosted=hosted)[0]


def _attn_a_bwd(qkv, bias, do, hosted=()):
    bsz, s_len, _ = qkv.shape
    pad = A_PREV * CHUNK
    band = TQ + pad
    n_i = s_len // TQ
    pp = BWD_PAIRS
    w = pp * LANES
    nb = A_WIDTH // w

    def body(q_ref, k_ref, v_ref, b_ref, do_ref, dq_ref, dk_ref, dv_ref, dbias_ref, kp, vp, dk_acc, dv_acc):
        b = pl.program_id(1)
        i = pl.program_id(2)

        @pl.when(i == 0)
        def _():
            _fill_padded(kp, k_ref[...], pad)
            _fill_padded(vp, v_ref[...], pad)
            dk_acc[...] = jnp.zeros_like(dk_acc)
            dv_acc[...] = jnp.zeros_like(dv_acc)

        @pl.when(jnp.logical_and(b == 0, i == 0))
        def _():
            dbias_ref[...] = jnp.zeros_like(dbias_ref)

        qs = pl.multiple_of(i * TQ, TQ)
        half = _lane_half((1, LANES))

        def block(masked):
            for pr in range(pp):
                sl = slice(pr * LANES, (pr + 1) * LANES)
                kk = kp[pl.ds(qs, band), sl]
                vv = vp[pl.ds(qs, band), sl]
                q = q_ref[:, sl] * SCALE
                dd = do_ref[:, sl]
                dqs, dks, dvs = [], [], []
                for j in range(2):
                    qm = jnp.where(half == j, q, jnp.zeros_like(q))
                    dm = jnp.where(half == j, dd, jnp.zeros_like(dd))
                    p, _, inv = _band_softmax(qm, kk, b_ref[2 * pr + j], None, qs if masked else None, pad)
                    pn = p * inv
                    dp = _dot_nt(dm, vv)
                    delta = jnp.sum(pn * dp, axis=-1, keepdims=True)
                    ds = pn * (dp - delta)
                    dbias_ref[2 * pr + j] += ds[:, band - REL_COLS:]
                    dsb = ds.astype(BF16)
                    dqs.append(_dot_nn(dsb, kk))
                    dks.append(_dot_tn(dsb, q))
                    dvs.append(_dot_tn(pn.astype(BF16), dd))
                dq_ref[:, sl] = (jnp.where(half == 0, dqs[0], dqs[1]) * SCALE).astype(BF16)
                dk_acc[pl.ds(qs, band), sl] += jnp.where(half == 0, dks[0], dks[1])
                dv_acc[pl.ds(qs, band), sl] += jnp.where(half == 0, dvs[0], dvs[1])

        pl.when(i < pad // TQ)(lambda: block(True))
        pl.when(i >= pad // TQ)(lambda: block(False))

        @pl.when(i == n_i - 1)
        def _():
            dk_ref[...] = dk_acc[pad:, :].astype(BF16)
            dv_ref[...] = dv_acc[pad:, :].astype(BF16)

    qspec = pl.BlockSpec((None, TQ, w), lambda g, b, i: (b, i, g))
    kvout = pl.BlockSpec((None, s_len, w), lambda g, b, i: (b, 0, g))
    wide = jax.ShapeDtypeStruct((bsz, s_len, A_WIDTH), BF16)
    return _call(
        body,
        name="attn_a_bwd",
        grid=(nb, bsz, n_i),
        in_specs=[qspec,
                  pl.BlockSpec((None, s_len, w), lambda g, b, i: (b, 0, nb + g)),
                  pl.BlockSpec((None, s_len, w), lambda g, b, i: (b, 0, 2 * nb + g)),
                  pl.BlockSpec((2 * pp, TQ, band), lambda g, b, i: (g, 0, 0)),
                  qspec],
        out_specs=[qspec, kvout, kvout, pl.BlockSpec((2 * pp, TQ, REL_COLS), lambda g, b, i: (g, 0, 0))],
        out_shape=[wide, wide, wide, jax.ShapeDtypeStruct((A_HEADS, TQ, REL_COLS), F32)],
        scratch_shapes=[pltpu.VMEM((pad + s_len, w), BF16), pltpu.VMEM((pad + s_len, w), BF16),
                        pltpu.VMEM((pad + s_len, w), F32), pltpu.VMEM((pad + s_len, w), F32)],
        args=(qkv, qkv, qkv, bias, do), sem=("arbitrary", "arbitrary", "arbitrary"), hosted=hosted)


def _fill_padded_dup(dst, src, pad, h, half):
    other = pltpu.roll(src, D_HEAD, 1)
    _fill_padded(dst, jnp.where(half == h, src, other), pad)


def _attn_b_fwd(qkv, bias, sink):
    bsz, s_len, _ = qkv.shape
    pad = B_PREV * CHUNK
    band = TQ + pad
    kcol = B_Q_WIDTH // LANES
    npair = B_Q_HEADS // 2

    def body(q_ref, k_ref, v_ref, b_ref, s_ref, o_ref, kp, vp):
        i = pl.program_id(1)
        half = _lane_half((1, LANES))

        @pl.when(i == 0)
        def _():
            for h in range(B_KV_HEADS):
                _fill_padded_dup(kp.at[h], k_ref[...], pad, h, half)
                _fill_padded_dup(vp.at[h], v_ref[...], pad, h, half)

        qs = pl.multiple_of(i * TQ, TQ)

        def block(masked):
            for pr in range(npair):
                h = pr // (B_GROUP // 2)
                sl = slice(pr * LANES, (pr + 1) * LANES)
                kk = kp[h, pl.ds(qs, band), :]
                vv = vp[h, pl.ds(qs, band), :]
                q = q_ref[:, sl] * SCALE
                outs = []
                for j in range(2):
                    qm = jnp.where(half == j, q, jnp.zeros_like(q))
                    p, _, inv = _band_softmax(qm, kk, b_ref[2 * pr + j], s_ref[2 * pr + j][0:1, 0:1],
                                              qs if masked else None, pad)
                    outs.append(_dot_nn(p.astype(BF16), vv) * inv)
                o_ref[:, sl] = jnp.where(half == 0, outs[0], outs[1]).astype(BF16)

        pl.when(i < -(-pad // TQ))(lambda: block(True))
        pl.when(i >= -(-pad // TQ))(lambda: block(False))

    return pl.pallas_call(
        body,
        name="attn_b_fwd",
        grid=(bsz, s_len // TQ),
        in_specs=[pl.BlockSpec((None, TQ, B_Q_WIDTH), lambda b, i: (b, i, 0)),
                  pl.BlockSpec((None, s_len, LANES), lambda b, i: (b, 0, kcol)),
                  pl.BlockSpec((None, s_len, LANES), lambda b, i: (b, 0, kcol + 1)),
                  pl.BlockSpec((B_Q_HEADS, TQ, band), lambda b, i: (0, 0, 0)),
                  pl.BlockSpec((B_Q_HEADS, 8, LANES), lambda b, i: (0, 0, 0))],
        out_specs=pl.BlockSpec((None, TQ, B_Q_WIDTH), lambda b, i: (b, i, 0)),
        out_shape=jax.ShapeDtypeStruct((bsz, s_len, B_Q_WIDTH), BF16),
        scratch_shapes=[pltpu.VMEM((B_KV_HEADS, pad + s_len, LANES), BF16),
                        pltpu.VMEM((B_KV_HEADS, pad + s_len, LANES), BF16)],
        compiler_params=_cparams(("arbitrary", "arbitrary")),
    )(qkv, qkv, qkv, bias, sink)


def _attn_b_bwd(qkv, bias, sink, do, hosted=()):
    bsz, s_len, _ = qkv.shape
    pad = B_PREV * CHUNK
    band = TQ + pad
    kcol = B_Q_WIDTH // LANES
    npair = B_Q_HEADS // 2
    n_i = s_len // TQ

    pp = B_GROUP // 2
    w = pp * LANES

    def body(q_ref, k_ref, v_ref, b_ref, s_ref, do_ref, dq_ref, dkv_ref, dsink_ref, kp, vp, dk_acc, dv_acc):
        b = pl.program_id(0)
        h = pl.program_id(1)
        i = pl.program_id(2)
        half = _lane_half((1, LANES))

        @pl.when(i == 0)
        def _():
            _fill_padded_dup(kp, k_ref[...], pad, h, half)
            _fill_padded_dup(vp, v_ref[...], pad, h, half)

        @pl.when(jnp.logical_and(h == 0, i == 0))
        def _():
            dk_acc[...] = jnp.zeros_like(dk_acc)
            dv_acc[...] = jnp.zeros_like(dv_acc)

        @pl.when(jnp.logical_and(b == 0, jnp.logical_and(h == 0, i == 0)))
        def _():
            dsink_ref[...] = jnp.zeros_like(dsink_ref)

        qs = pl.multiple_of(i * TQ, TQ)

        def block(masked):
            kk = kp[pl.ds(qs, band), :]
            vv = vp[pl.ds(qs, band), :]
            dk2 = jnp.zeros((band, LANES), F32)
            dv2 = jnp.zeros((band, LANES), F32)
            for pr in range(pp):
                sl = slice(pr * LANES, (pr + 1) * LANES)
                q = q_ref[:, sl] * SCALE
                dd = do_ref[:, sl]
                dqs, dks, dvs = [], [], []
                for j in range(2):
                    qm = jnp.where(half == j, q, jnp.zeros_like(q))
                    dm = jnp.where(half == j, dd, jnp.zeros_like(dd))
                    sink = s_ref[2 * pr + j][0:1, 0:1]
                    p, m, inv = _band_softmax(qm, kk, b_ref[2 * pr + j], sink, qs if masked else None, pad)
                    pn = p * inv
                    dp = _dot_nt(dm, vv)
                    delta = jnp.sum(pn * dp, axis=-1, keepdims=True)
                    ds = pn * (dp - delta)
                    dsb = ds.astype(BF16)
                    dqs.append(_dot_nn(dsb, kk))
                    dks.append(_dot_tn(dsb, q))
                    dvs.append(_dot_tn(pn.astype(BF16), dd))
                    dsk = jnp.sum(-(jnp.exp(sink - m) * inv) * delta, axis=0, keepdims=True)
                    dsink_ref[2 * pp * h + 2 * pr + j] += jnp.broadcast_to(dsk, (8, LANES))
                dq_ref[:, sl] = (jnp.where(half == 0, dqs[0], dqs[1]) * SCALE).astype(BF16)
                dk2 = dk2 + jnp.where(half == 0, dks[0], dks[1])
                dv2 = dv2 + jnp.where(half == 0, dvs[0], dvs[1])
            dk_acc[pl.ds(qs, band), :] += jnp.where(half == h, dk2 + pltpu.roll(dk2, D_HEAD, 1), 0.0)
            dv_acc[pl.ds(qs, band), :] += jnp.where(half == h, dv2 + pltpu.roll(dv2, D_HEAD, 1), 0.0)

        pl.when(i < -(-pad // TQ))(lambda: block(True))
        pl.when(i >= -(-pad // TQ))(lambda: block(False))

        @pl.when(jnp.logical_and(h == B_KV_HEADS - 1, i == n_i - 1))
        def _():
            dkv_ref[:, 0:LANES] = dk_acc[pad:, :].astype(BF16)
            dkv_ref[:, LANES:2 * LANES] = dv_acc[pad:, :].astype(BF16)

    qspec = pl.BlockSpec((None, TQ, w), lambda b, h, i: (b, i, h))
    return _call(
        body,
        name="attn_b_bwd",
        grid=(bsz, B_KV_HEADS, n_i),
        in_specs=[qspec,
                  pl.BlockSpec((None, s_len, LANES), lambda b, h, i: (b, 0, kcol)),
                  pl.BlockSpec((None, s_len, LANES), lambda b, h, i: (b, 0, kcol + 1)),
                  pl.BlockSpec((2 * pp, TQ, band), lambda b, h, i: (h, 0, 0)),
                  pl.BlockSpec((2 * pp, 8, LANES), lambda b, h, i: (h, 0, 0)),
                  qspec],
        out_specs=[qspec, pl.BlockSpec((None, s_len, 2 * LANES), lambda b, h, i: (b, 0, 0)),
                   pl.BlockSpec((B_Q_HEADS, 8, LANES), lambda b, h, i: (0, 0, 0))],
        out_shape=[jax.ShapeDtypeStruct((bsz, s_len, B_Q_WIDTH), BF16),
                   jax.ShapeDtypeStruct((bsz, s_len, 2 * B_KV_WIDTH), BF16),
                   jax.ShapeDtypeStruct((B_Q_HEADS, 8, LANES), F32)],
        scratch_shapes=[pltpu.VMEM((pad + s_len, LANES), BF16), pltpu.VMEM((pad + s_len, LANES), BF16),
                        pltpu.VMEM((pad + s_len, LANES), F32), pltpu.VMEM((pad + s_len, LANES), F32)],
        args=(qkv, qkv, qkv, bias, sink, do), sem=("arbitrary", "arbitrary", "arbitrary"), hosted=hosted)


REL_COLS = 3 * 128
REL_WRAP = 512


def _bias_a_build(tv):
    h = tv.shape[0]
    pad = A_PREV * CHUNK
    band = TQ + pad

    def body(tv_ref, o_ref):
        row = tv_ref[...]
        x = jnp.broadcast_to(row, (TQ, REL_WRAP))
        r = lax.broadcasted_iota(jnp.int32, x.shape, 0)
        for bit in range(8):
            sh = 1 << bit
            x = jnp.where((r & sh) != 0, pltpu.roll(x, sh, 1), x)
        far = jnp.broadcast_to(row[:, 0:1], (TQ, band - REL_COLS))
        full = jnp.concatenate([far, x[:, REL_WRAP // 2:REL_WRAP], x[:, 0:REL_COLS - REL_WRAP // 2]], axis=1)
        qc = (lax.broadcasted_iota(jnp.int32, full.shape, 0) + pad) // CHUNK
        kc = lax.broadcasted_iota(jnp.int32, full.shape, 1) // CHUNK
        ok = jnp.logical_and(kc <= qc, kc >= qc - A_PREV)
        o_ref[...] = jnp.where(ok, full, NEG_INF)

    return pl.pallas_call(
        body,
        name="bias_a_build",
        grid=(h,),
        in_specs=[pl.BlockSpec((None, 1, REL_WRAP), lambda hh: (hh, 0, 0))],
        out_specs=pl.BlockSpec((None, TQ, band), lambda hh: (hh, 0, 0)),
        out_shape=jax.ShapeDtypeStruct((h, TQ, band), F32),
        compiler_params=_cparams(("parallel",)),
    )(tv)


def _relbias_grad(dbias):
    h, rows, _ = dbias.shape

    def body(d_ref, o_ref):
        x = d_ref[...]
        r = lax.broadcasted_iota(jnp.int32, x.shape, 0)
        c = lax.broadcasted_iota(jnp.int32, x.shape, 1) - r
        x = jnp.where(jnp.logical_and(c >= 1, c < REL_TABLE), x, 0.0)
        for bit in range(8):
            sh = 1 << bit
            x = jnp.where((r & sh) != 0, pltpu.roll(x, REL_COLS - sh, 1), x)
        diag = jnp.sum(x, axis=0, keepdims=True)
        lane = lax.broadcasted_iota(jnp.int32, diag.shape, 1)
        diag = jnp.where(jnp.logical_and(lane >= 1, lane < REL_TABLE), diag, 0.0)
        rest = -jnp.sum(diag, axis=1, keepdims=True)
        o_ref[...] = jnp.broadcast_to(jnp.where(lane == 0, rest, diag), o_ref.shape)

    return pl.pallas_call(
        body,
        name="relbias_grad",
        grid=(h,),
        in_specs=[pl.BlockSpec((None, rows, REL_COLS), lambda hh: (hh, 0, 0))],
        out_specs=pl.BlockSpec((None, 8, REL_COLS), lambda hh: (hh, 0, 0)),
        out_shape=jax.ShapeDtypeStruct((h, 8, REL_COLS), F32),
        compiler_params=_cparams(("parallel",)),
    )(dbias)


def _mix_out_fwd(x, oa, ob, gates, proj_t, wout):
    t = x.shape[0]

    def body(x_ref, oa_ref, ob_ref, gt_ref, pt_ref, wo_ref, y_ref, ya_ref, yb_ref, mg_ref):
        ya = _dot_nt(oa_ref[...], pt_ref[:, 0:A_WIDTH])
        yb = _dot_nt(ob_ref[...], pt_ref[:, A_WIDTH:A_WIDTH + B_Q_WIDTH])
        ya_ref[...] = ya.astype(BF16)
        yb_ref[...] = yb.astype(BF16)
        mg = jax.nn.sigmoid(gt_ref[:, 0:D_MODEL]) * ya + jax.nn.sigmoid(gt_ref[:, D_MODEL:2 * D_MODEL]) * yb
        mgb = mg.astype(BF16)
        mg_ref[...] = mgb
        y_ref[...] = x_ref[...] + _dot_nn(mgb, wo_ref[...])

    return pl.pallas_call(
        body,
        name="mix_out_fwd",
        grid=(t // TM,),
        in_specs=[_rows(TM, D_MODEL), _rows(TM, A_WIDTH), _rows(TM, B_Q_WIDTH), _rows(TM, 2 * D_MODEL),
                  _resident((D_MODEL, A_WIDTH + B_Q_WIDTH)), _resident((D_MODEL, D_MODEL))],
        out_specs=[_rows(TM, D_MODEL), _rows(TM, D_MODEL), _rows(TM, D_MODEL), _rows(TM, D_MODEL)],
        out_shape=[jax.ShapeDtypeStruct((t, D_MODEL), F32), jax.ShapeDtypeStruct((t, D_MODEL), BF16),
                   jax.ShapeDtypeStruct((t, D_MODEL), BF16), jax.ShapeDtypeStruct((t, D_MODEL), BF16)],
        compiler_params=_cparams(("parallel",)),
    )(x, oa, ob, gates, proj_t, wout)


def _mix_out_bwd(d, gates, ya, yb, proj_t, wout, hosted=()):
    t = d.shape[0]

    def body(d_ref, gt_ref, ya_ref, yb_ref, pt_ref, wo_ref, db_ref, dya_ref, dyb_ref, doa_ref, dob_ref, dgt_ref):
        db = d_ref[...].astype(BF16)
        db_ref[...] = db
        dmg = _dot_nt(db, wo_ref[...])
        sa = jax.nn.sigmoid(gt_ref[:, 0:D_MODEL])
        sb = jax.nn.sigmoid(gt_ref[:, D_MODEL:2 * D_MODEL])
        dya = (dmg * sa).astype(BF16)
        dyb = (dmg * sb).astype(BF16)
        dya_ref[...] = dya
        dyb_ref[...] = dyb
        dgt_ref[:, 0:D_MODEL] = (dmg * ya_ref[...].astype(F32) * (sa * (1.0 - sa))).astype(BF16)
        dgt_ref[:, D_MODEL:2 * D_MODEL] = (dmg * yb_ref[...].astype(F32) * (sb * (1.0 - sb))).astype(BF16)
        doa_ref[...] = _dot_nn(dya, pt_ref[:, 0:A_WIDTH]).astype(BF16)
        dob_ref[...] = _dot_nn(dyb, pt_ref[:, A_WIDTH:A_WIDTH + B_Q_WIDTH]).astype(BF16)

    return _call(
        body,
        name="mix_out_bwd",
        grid=(t // TM,),
        in_specs=[_rows(TM, D_MODEL), _rows(TM, 2 * D_MODEL), _rows(TM, D_MODEL), _rows(TM, D_MODEL),
                  _resident((D_MODEL, A_WIDTH + B_Q_WIDTH)), _resident((D_MODEL, D_MODEL))],
        out_specs=[_rows(TM, D_MODEL), _rows(TM, D_MODEL), _rows(TM, D_MODEL), _rows(TM, A_WIDTH),
                   _rows(TM, B_Q_WIDTH), _rows(TM, 2 * D_MODEL)],
        out_shape=[jax.ShapeDtypeStruct((t, D_MODEL), BF16), jax.ShapeDtypeStruct((t, D_MODEL), BF16),
                   jax.ShapeDtypeStruct((t, D_MODEL), BF16), jax.ShapeDtypeStruct((t, A_WIDTH), BF16),
                   jax.ShapeDtypeStruct((t, B_Q_WIDTH), BF16), jax.ShapeDtypeStruct((t, 2 * D_MODEL), BF16)],
        args=(d, gates, ya, yb, proj_t, wout), sem=("parallel",), hosted=hosted)


def _loss_head(x, gamma, target):
    t = x.shape[0]

    def body(x_ref, gam_ref, t_ref, dx_ref, dgam_ref, loss_ref):
        xh, r = _rms(x_ref[...])
        gam = gam_ref[...]
        e = xh * gam - t_ref[...]
        dy = e * (1.0 / D_MODEL)
        dxn, dgam = _rms_bwd(dy, xh, r, gam)
        dx_ref[...] = dxn

        @pl.when(pl.program_id(0) == 0)
        def _():
            dgam_ref[...] = jnp.zeros_like(dgam_ref)
            loss_ref[...] = jnp.zeros_like(loss_ref)

        dgam_ref[...] += dgam
        loss_ref[...] += _colsum8(e * e) * (0.5 / D_MODEL)

    return pl.pallas_call(
        body,
        name="loss_head",
        grid=(t // TM,),
        in_specs=[_rows(TM, D_MODEL), _resident((1, D_MODEL)), _rows(TM, D_MODEL)],
        out_specs=[_rows(TM, D_MODEL), pl.BlockSpec((8, D_MODEL), lambda i: (0, 0)),
                   pl.BlockSpec((8, D_MODEL), lambda i: (0, 0))],
        out_shape=[jax.ShapeDtypeStruct((t, D_MODEL), F32), jax.ShapeDtypeStruct((8, D_MODEL), F32),
                   jax.ShapeDtypeStruct((8, D_MODEL), F32)],
        compiler_params=_cparams(("arbitrary",)),
    )(x, gamma, target)


def _place():
    x, y, c = lax.axis_index("x"), lax.axis_index("y"), lax.axis_index("c")
    chips = [(1 - x, y), (x, 1 - y), (1 - x, 1 - y)]
    return x, y, c, chips


class _Gather:
    per = 8

    def __init__(self, shards):
        n = len(shards)
        self.inputs = list(shards)
        self.out_shape = [jax.ShapeDtypeStruct((N_DEV * s.shape[0], s.shape[1]), s.dtype) for s in shards]
        self.scratch = [pltpu.SemaphoreType.DMA((n * self.per,)), pltpu.SemaphoreType.DMA((n * self.per,)),
                        pltpu.SemaphoreType.DMA((n,))]
        self.result = None

    def _parts(self, ins, outs, sems):
        send_sems, recv_sems, local_sems = sems
        x, y, c, chips = _place()
        me, sibling = (x, y, c), (x, y, 1 - c)
        xn, yn, dg = chips
        n = len(ins)

        def rows(k, p, part=None):
            r = ins[k].shape[0]
            base = (4 * p[0] + 2 * p[1] + p[2]) * r
            if part is None:
                return outs[k].at[pl.ds(base, r), :]
            return outs[k].at[pl.ds(base + part * (r // 2), r // 2), :]

        def copy(k, slot, block, to, src=None, part=None):
            return pltpu.make_async_remote_copy(
                src_ref=rows(k, block, part) if src is None else src, dst_ref=rows(k, block, part),
                send_sem=send_sems.at[k * self.per + slot], recv_sem=recv_sems.at[k * self.per + slot],
                device_id=to, device_id_type=MESH)

        mine = [pltpu.make_async_copy(ins[k], rows(k, me), local_sems.at[k]) for k in range(n)]
        sends, lands = [], []
        for k in range(n):
            sends.append({
                0: copy(k, 0, me, sibling, src=ins[k]),
                1: copy(k, 1, me, (*xn, c), src=ins[k]),
                2: copy(k, 2, me, (*yn, c), src=ins[k]),
                3: copy(k, 3, (*xn, c), (*yn, c), part=0),
                4: copy(k, 4, (*yn, c), (*xn, c), part=1),
                5: copy(k, 5, (*xn, c), sibling),
                6: copy(k, 6, (*yn, c), sibling),
                7: copy(k, 7, (*dg, c), sibling)})
            lands.append({
                0: copy(k, 0, sibling, me),
                1: copy(k, 1, (*xn, c), me),
                2: copy(k, 2, (*yn, c), me),
                3: copy(k, 3, (*dg, c), me, part=0),
                4: copy(k, 4, (*dg, c), me, part=1),
                5: copy(k, 5, (*xn, 1 - c), me),
                6: copy(k, 6, (*yn, 1 - c), me),
                7: copy(k, 7, (*dg, 1 - c), me)})
        return n, mine, sends, lands

    def start(self, ins, outs, sems):
        n, mine, sends, _ = self._parts(ins, outs, sems)
        for cp in mine:
            cp.start()
        for slot in (0, 1, 2):
            for k in range(n):
                sends[k][slot].start()

    def relay(self, ins, outs, sems):
        n, _, sends, lands = self._parts(ins, outs, sems)
        for k in range(n):
            lands[k][1].wait_recv()
            sends[k][3].start()
            sends[k][5].start()
        for k in range(n):
            lands[k][2].wait_recv()
            sends[k][4].start()
            sends[k][6].start()

    def forward(self, ins, outs, sems):
        n, _, sends, lands = self._parts(ins, outs, sems)
        for k in range(n):
            lands[k][3].wait_recv()
            lands[k][4].wait_recv()
            sends[k][7].start()

    def finish(self, ins, outs, sems):
        n, mine, sends, lands = self._parts(ins, outs, sems)
        for k in range(n):
            for slot in (0, 5, 6, 7):
                lands[k][slot].wait_recv()
        for k in range(n):
            for slot in range(self.per):
                sends[k][slot].wait_send()
        for cp in mine:
            cp.wait()


class _PairExchange:
    def __init__(self, grads):
        n = len(grads)
        self.inputs = list(grads)
        self.out_shape = [jax.ShapeDtypeStruct((g.shape[0] // 2, g.shape[1]), g.dtype) for g in grads]
        self.scratch = [pltpu.SemaphoreType.DMA((n * N_CHIP,)), pltpu.SemaphoreType.DMA((n * N_CHIP,))]
        self.result = None

    def _copies(self, ins, outs, sems):
        send_sems, recv_sems = sems
        x, y, c, _ = _place()
        copies = []
        for k in range(len(ins)):
            r = ins[k].shape[0] // N_DEV
            for q in range(N_CHIP):
                copies.append(pltpu.make_async_remote_copy(
                    src_ref=ins[k].at[pl.ds((2 * q + 1 - c) * r, r), :], dst_ref=outs[k].at[pl.ds(q * r, r), :],
                    send_sem=send_sems.at[k * N_CHIP + q], recv_sem=recv_sems.at[k * N_CHIP + q],
                    device_id=(x, y, 1 - c), device_id_type=MESH))
        return copies

    def start(self, ins, outs, sems):
        for cp in self._copies(ins, outs, sems):
            cp.start()

    def relay(self, ins, outs, sems):
        pass

    def forward(self, ins, outs, sems):
        pass

    def finish(self, ins, outs, sems):
        copies = self._copies(ins, outs, sems)
        for cp in copies:
            cp.wait_recv()
        for cp in copies:
            cp.wait_send()


class _ChipExchange(_PairExchange):
    def __init__(self, psums):
        n = len(psums)
        self.inputs = list(psums)
        self.out_shape = [jax.ShapeDtypeStruct((3 * p.shape[0] // N_CHIP, p.shape[1]), p.dtype) for p in psums]
        self.scratch = [pltpu.SemaphoreType.DMA((n * 3,)), pltpu.SemaphoreType.DMA((n * 3,))]
        self.result = None

    def _copies(self, ins, outs, sems):
        send_sems, recv_sems = sems
        _, _, c, chips = _place()
        copies = []
        for k in range(len(ins)):
            r = ins[k].shape[0] // N_CHIP
            for j, chip in enumerate(chips):
                copies.append(pltpu.make_async_remote_copy(
                    src_ref=ins[k].at[pl.ds((2 * chip[0] + chip[1]) * r, r), :], dst_ref=outs[k].at[pl.ds(j * r, r), :],
                    send_sem=send_sems.at[k * 3 + j], recv_sem=recv_sems.at[k * 3 + j],
                    device_id=(*chip, c), device_id_type=MESH))
        return copies


def _exchange_alone(xchg, name):
    n_in, n_out = len(xchg.inputs), len(xchg.out_shape)

    def body(*refs):
        ins, outs, sems = refs[:n_in], refs[n_in:n_in + n_out], refs[n_in + n_out:]
        xchg.start(ins, outs, sems)
        xchg.relay(ins, outs, sems)
        xchg.forward(ins, outs, sems)
        xchg.finish(ins, outs, sems)

    xchg.result = list(pl.pallas_call(
        body, name=name, in_specs=[_hbm()] * n_in, out_specs=[_hbm()] * n_out, out_shape=xchg.out_shape,
        scratch_shapes=xchg.scratch)(*xchg.inputs))
    return xchg.result


def _pair_sum(core, grads, recvd, name):
    n = len(grads)
    r = grads[0].shape[0] // N_DEV
    cdim = grads[0].shape[1]
    tr = r // 2 if r % 32 == 0 else r
    nt = r // tr

    def body(core_ref, *refs):
        del core_ref
        for k in range(n):
            refs[2 * n + k][...] = (refs[k][...].astype(F32) + refs[n + k][...].astype(F32)).astype(BF16)

    gspec = pl.BlockSpec((tr, cdim), lambda q, i, core_ref: ((2 * q + core_ref[0]) * nt + i, 0))
    rspec = pl.BlockSpec((tr, cdim), lambda q, i, core_ref: (q * nt + i, 0))
    return pl.pallas_call(
        body,
        name=name,
        grid_spec=pltpu.PrefetchScalarGridSpec(
            num_scalar_prefetch=1, grid=(N_CHIP, nt), in_specs=[gspec] * n + [rspec] * n, out_specs=[rspec] * n),
        out_shape=[jax.ShapeDtypeStruct((N_CHIP * r, cdim), BF16) for _ in range(n)],
        compiler_params=_cparams(("parallel", "parallel")),
    )(core, *grads, *recvd)


def _final_sum(chip, psums, recvd, name):
    n = len(psums)
    r = psums[0].shape[0] // N_CHIP
    cdim = psums[0].shape[1]
    tr = r // 2 if r % 32 == 0 else r
    nt = r // tr

    def body(chip_ref, *refs):
        del chip_ref
        for k in range(n):
            got = refs[n + k]
            tot = refs[k][...].astype(F32) + got[0].astype(F32)
            tot = tot + got[1].astype(F32)
            tot = tot + got[2].astype(F32)
            refs[2 * n + k][...] = tot

    pspec = pl.BlockSpec((tr, cdim), lambda i, chip_ref: (chip_ref[0] * nt + i, 0))
    rspec = pl.BlockSpec((3, tr, cdim), lambda i, chip_ref: (0, i, 0))
    ospec = pl.BlockSpec((tr, cdim), lambda i, chip_ref: (i, 0))
    return pl.pallas_call(
        body,
        name=name,
        grid_spec=pltpu.PrefetchScalarGridSpec(
            num_scalar_prefetch=1, grid=(nt,), in_specs=[pspec] * n + [rspec] * n, out_specs=[ospec] * n),
        out_shape=[jax.ShapeDtypeStruct((r, cdim), F32) for _ in range(n)],
        compiler_params=_cparams(("parallel",)),
    )(chip, *psums, *[g.reshape(3, r, cdim) for g in recvd])


SMALL_ROWS = 16


def _all_reduce_small(part):
    def body(p_ref, o_ref, buf, send_sems, recv_sems):
        x, y, c, _ = _place()
        me = 4 * x + 2 * y + c
        buf[me] = p_ref[...]
        copies = []
        for d in range(1, N_DEV):
            peer = me ^ d
            copies.append(pltpu.make_async_remote_copy(
                src_ref=p_ref, dst_ref=buf.at[me], send_sem=send_sems.at[d - 1], recv_sem=recv_sems.at[d - 1],
                device_id=(peer // 4, (peer // 2) % 2, peer % 2), device_id_type=MESH))
        for cp in copies:
            cp.start()
        for cp in copies:
            cp.wait_recv()
        for cp in copies:
            cp.wait_send()
        tot = buf[0]
        for d in range(1, N_DEV):
            tot = tot + buf[d]
        o_ref[...] = tot

    return pl.pallas_call(
        body,
        name="all_reduce_small",
        in_specs=[pl.BlockSpec(memory_space=pltpu.VMEM)],
        out_specs=pl.BlockSpec(memory_space=pltpu.VMEM),
        out_shape=jax.ShapeDtypeStruct(part.shape, F32),
        scratch_shapes=[pltpu.VMEM((N_DEV,) + part.shape, F32), pltpu.SemaphoreType.DMA((N_DEV - 1,)),
                        pltpu.SemaphoreType.DMA((N_DEV - 1,))],
    )(part)


ADAMW_STEPS = 4


def _adamw(ws, gs, ms, vs, name, hosted=()):
    n = len(ws)
    steps = ADAMW_STEPS if all(w.shape[0] % (8 * ADAMW_STEPS) == 0 for w in ws) else 1
    c1 = 1.0 - ADAM_B1 ** ADAM_STEP
    c2 = 1.0 - ADAM_B2 ** ADAM_STEP

    def body(*refs):
        for k in range(n):
            w, g, m, v = (refs[j * n + k][...] for j in range(4))
            m2 = ADAM_B1 * m + (1.0 - ADAM_B1) * g
            v2 = ADAM_B2 * v + (1.0 - ADAM_B2) * (g * g)
            delta = -ADAM_LR * ((m2 / c1) / (jnp.sqrt(v2 / c2) + ADAM_EPS) + ADAM_WD * w)
            refs[4 * n + k][...] = delta
            refs[5 * n + k][...] = m2
            refs[6 * n + k][...] = v2

    specs = [pl.BlockSpec((w.shape[0] // steps, w.shape[1]), lambda i: (i, 0)) for w in ws]
    shapes = [jax.ShapeDtypeStruct(w.shape, F32) for w in ws]
    outs = _call(
        body,
        name=name,
        grid=(steps,),
        in_specs=specs * 4,
        out_specs=specs * 3,
        out_shape=shapes * 3,
        args=(*ws, *gs, *ms, *vs), sem=("parallel",), hosted=hosted)
    return outs[:n], outs[n:2 * n], outs[2 * n:]


def _bias_b():
    pad = B_PREV * CHUNK
    slopes = np.array([2.0 ** (-8.0 * (i + 1) / B_Q_HEADS) for i in range(B_Q_HEADS)], dtype=np.float32)
    dist = np.abs(np.arange(TQ)[:, None] - np.arange(TQ + pad)[None, :] + pad).astype(np.float32)
    bias = -slopes.reshape(B_Q_HEADS, 1, 1) * dist[None]
    qc = (np.arange(TQ)[:, None] + pad) // CHUNK
    kc = np.arange(TQ + pad)[None, :] // CHUNK
    allowed = (kc <= qc) & (kc >= qc - B_PREV)
    return np.where(allowed[None], bias, np.float32(NEG_INF)).astype(np.float32)


def kernel(x, ffn1_norm, ffn1_w_gate, ffn1_w_up, ffn1_w_down, mix_norm, w_in, rel_bias, sinks, w_proj_a, w_proj_b, w_out, ffn2_norm, ffn2_w_gate, ffn2_w_up, ffn2_w_down, final_norm, loss_target, m_ffn1_norm, m_ffn1_w_gate, m_ffn1_w_up, m_ffn1_w_down, m_mix_norm, m_w_in, m_rel_bias, m_sinks, m_w_proj_a, m_w_proj_b, m_w_out, m_ffn2_norm, m_ffn2_w_gate, m_ffn2_w_up, m_ffn2_w_down, m_final_norm, v_ffn1_norm, v_ffn1_w_gate, v_ffn1_w_up, v_ffn1_w_down, v_mix_norm, v_w_in, v_rel_bias, v_sinks, v_w_proj_a, v_w_proj_b, v_w_out, v_ffn2_norm, v_ffn2_w_gate, v_ffn2_w_up, v_ffn2_w_down, v_final_norm):
    bsz, s_len, _ = x.shape
    t = bsz * s_len
    core = lax.axis_index("c").astype(jnp.int32).reshape(1)
    chip = (2 * lax.axis_index("x") + lax.axis_index("y")).astype(jnp.int32).reshape(1)

    def row_form(w):
        return w.astype(BF16).T

    wg1, wu1, wd1 = _exchange_alone(
        _Gather([row_form(ffn1_w_gate), row_form(ffn1_w_up), ffn1_w_down.astype(BF16)]), "gather_ffn1")
    gather_mix = _Gather([row_form(w_in), jnp.concatenate([row_form(w_proj_a), row_form(w_proj_b)], axis=1),
                          w_out.astype(BF16)])
    gather_ffn2_gate = _Gather([row_form(ffn2_w_gate)])
    gather_ffn2_rest = _Gather([row_form(ffn2_w_up), ffn2_w_down.astype(BF16)])

    x0 = x.reshape(t, D_MODEL)
    tgt = loss_target.reshape(t, D_MODEL)
    gam1, gam2, gam3, gam4 = (g.reshape(1, D_MODEL) for g in (ffn1_norm, mix_norm, ffn2_norm, final_norm))

    h1, g1, u1, a1, x1 = _ffn_fwd(x0, gam1, wg1, wu1, wd1, "ffn1_fwd", hosted=[gather_mix])
    win_t, proj_t, wout = gather_mix.result
    h2, qkv_a, qkv_b, gates = _proj_fwd(x1, gam2, win_t, hosted=[gather_ffn2_gate])
    (wg2,) = gather_ffn2_gate.result
    qkv_a3 = qkv_a.reshape(bsz, s_len, QKV_A)
    qkv_b3 = qkv_b.reshape(bsz, s_len, QKV_B)

    far = jnp.broadcast_to(rel_bias[:, REL_TABLE - 1:REL_TABLE], (A_HEADS, REL_WRAP // 2))
    tv = jnp.concatenate([far, jnp.flip(rel_bias, axis=1), jnp.zeros((A_HEADS, REL_WRAP // 2 - REL_TABLE), F32)], axis=1)
    bias_a = _bias_a_build(tv.reshape(A_HEADS, 1, REL_WRAP))
    bias_b = jnp.asarray(_bias_b())
    sink_rows = jnp.broadcast_to(sinks.reshape(B_Q_HEADS, 1, 1), (B_Q_HEADS, 8, LANES))

    oa = _attn_a_fwd(qkv_a3, bias_a, hosted=[gather_ffn2_rest]).reshape(t, A_WIDTH)
    wu2, wd2 = gather_ffn2_rest.result
    ob = _attn_b_fwd(qkv_b3, bias_b, sink_rows).reshape(t, B_Q_WIDTH)
    x2, ya, yb, mg = _mix_out_fwd(x1, oa, ob, gates, proj_t, wout)
    h3, g2, u2, a2, x3 = _ffn_fwd(x2, gam3, wg2, wu2, wd2, "ffn2_fwd")

    dx3, dgam4, loss_part = _loss_head(x3, gam4, tgt)

    dx2, dg2, du2, db2, dgam3 = _ffn_bwd(dx3, x2, gam3, g2, u2, wg2, wu2, wd2, "ffn2_bwd")
    gw_ffn2 = [_mm_tn([dg2], h3, "grad_ffn2_gate"), _mm_tn([du2], h3, "grad_ffn2_up"),
               _mm_tn([a2], db2, "grad_ffn2_down")]
    pairx_ffn2 = _PairExchange(gw_ffn2)
    dxb, dya, dyb, doa, dob, dgates = _mix_out_bwd(dx2, gates, ya, yb, proj_t, wout, hosted=[pairx_ffn2])
    psum_ffn2 = _pair_sum(core, gw_ffn2, pairx_ffn2.result, "pair_sum_ffn2")
    gw_out = _mm_tn([mg], dxb, "grad_w_out")
    gw_proj = _mm_tn_proj(dya, dyb, oa, ob)

    chipx_ffn2 = _ChipExchange(psum_ffn2)
    dqa, dka, dva, dbias_a = _attn_a_bwd(qkv_a3, bias_a, doa.reshape(bsz, s_len, A_WIDTH), hosted=[chipx_ffn2])
    pairx_out = _PairExchange([gw_proj, gw_out])
    dqb, dkvb, dsink = _attn_b_bwd(qkv_b3, bias_b, sink_rows, dob.reshape(bsz, s_len, B_Q_WIDTH), hosted=[pairx_out])
    drel_lanes = _relbias_grad(dbias_a)
    dproj = [dqa.reshape(t, A_WIDTH), dka.reshape(t, A_WIDTH), dva.reshape(t, A_WIDTH), dqb.reshape(t, B_Q_WIDTH),
             dkvb.reshape(t, 2 * B_KV_WIDTH), dgates]

    gw_in = _mm_tn(dproj, h2, "grad_w_in")
    pairx_in = _PairExchange([gw_in])
    psum_out = _pair_sum(core, [gw_proj, gw_out], pairx_out.result, "pair_sum_mix")
    chipx_out = _ChipExchange(psum_out)
    dx1, db1, dgam2 = _proj_bwd(dx2, x1, gam2, dproj, win_t, hosted=[pairx_in, chipx_out])
    psum_in = _pair_sum(core, [gw_in], pairx_in.result, "pair_sum_w_in")
    gw_d1 = _mm_tn([a1], db1, "grad_ffn1_down")

    chipx_in = _ChipExchange(psum_in)
    pairx_d1 = _PairExchange([gw_d1])
    dg1, du1 = _ffn_bwd_act(dx1, g1, u1, wd1, "ffn1_bwd_act", hosted=[chipx_in, pairx_d1])
    psum_d1 = _pair_sum(core, [gw_d1], pairx_d1.result, "pair_sum_ffn1_down")
    chipx_d1 = _ChipExchange(psum_d1)
    gw_g1 = _mm_tn([dg1], h1, "grad_ffn1_gate", hosted=[chipx_d1])
    from_sibling_g1 = _exchange_alone(_PairExchange([gw_g1]), "pair_exchange_ffn1_gate")
    psum_g1 = _pair_sum(core, [gw_g1], from_sibling_g1, "pair_sum_ffn1_gate")
    chipx_g1 = _ChipExchange(psum_g1)
    gw_u1 = _mm_tn([du1], h1, "grad_ffn1_up", hosted=[chipx_g1])
    from_sibling_u1 = _exchange_alone(_PairExchange([gw_u1]), "pair_exchange_ffn1_up")
    psum_u1 = _pair_sum(core, [gw_u1], from_sibling_u1, "pair_sum_ffn1_up")
    chipx_u1 = _ChipExchange(psum_u1)
    dx0, dgam1 = _ffn_bwd_in(dx1, x0, gam1, dg1, du1, wg1, wu1, "ffn1_bwd_in", hosted=[chipx_u1])

    g_g1, g_u1, g_d1, g_g2, g_u2, g_d2 = _final_sum(
        chip, psum_g1 + psum_u1 + psum_d1 + psum_ffn2,
        chipx_g1.result + chipx_u1.result + chipx_d1.result + chipx_ffn2.result, "grad_sum_ffn")
    (g_in,) = _final_sum(chip, psum_in, chipx_in.result, "grad_sum_w_in")
    g_proj, g_out = _final_sum(chip, psum_out, chipx_out.result, "grad_sum_mix")
    row_form_names = ("ffn1_w_gate", "ffn1_w_up", "w_in", "ffn2_w_gate", "ffn2_w_up")
    grads = {
        "ffn1_w_gate": g_g1, "ffn1_w_up": g_u1, "ffn1_w_down": g_d1, "w_in": g_in,
        "w_proj_a": g_proj[:, 0:A_WIDTH].T, "w_proj_b": g_proj[:, A_WIDTH:].T, "w_out": g_out,
        "ffn2_w_gate": g_g2, "ffn2_w_up": g_u2, "ffn2_w_down": g_d2,
    }

    def row_of(v):
        return jnp.pad(v.reshape(1, -1), ((0, 0), (0, D_MODEL - v.size)))

    def table_rows(v):
        return jnp.pad(v, ((0, 0), (0, D_MODEL - REL_TABLE)))

    drel_local = jnp.flip(drel_lanes[:, 0, 0:REL_TABLE], axis=1)
    small_part = jnp.concatenate(
        [jnp.sum(dgam1, axis=0, keepdims=True), jnp.sum(dgam2, axis=0, keepdims=True),
         jnp.sum(dgam3, axis=0, keepdims=True), jnp.sum(dgam4, axis=0, keepdims=True),
         row_of(jnp.sum(loss_part)), row_of(dsink[:, 0, 0]), jnp.zeros((2, D_MODEL), F32),
         table_rows(drel_local)], axis=0)
    small = _all_reduce_small(small_part)
    loss = small[4, 0]

    def pack(n1, n2, n3, n4, sk, tb):
        return jnp.concatenate([n1.reshape(1, -1), n2.reshape(1, -1), n3.reshape(1, -1), n4.reshape(1, -1),
                                jnp.zeros((1, D_MODEL), F32), row_of(sk), jnp.zeros((2, D_MODEL), F32), table_rows(tb)],
                               axis=0)

    live = np.zeros((SMALL_ROWS, D_MODEL), np.float32)
    live[0:4] = 1.0
    live[5, 0:B_Q_HEADS] = 1.0
    live[8:16, 0:REL_TABLE] = 1.0
    small_g = small * jnp.asarray(live)
    sw = pack(ffn1_norm, mix_norm, ffn2_norm, final_norm, sinks, rel_bias)
    sm = pack(m_ffn1_norm, m_mix_norm, m_ffn2_norm, m_final_norm, m_sinks, m_rel_bias)
    sv = pack(v_ffn1_norm, v_mix_norm, v_ffn2_norm, v_final_norm, v_sinks, v_rel_bias)
    (sd,), (snm,), (snv,) = _adamw([sw], [small_g], [sm], [sv], "adamw_small")

    def unpack(p):
        return {"ffn1_norm": p[0], "mix_norm": p[1], "ffn2_norm": p[2], "final_norm": p[3],
                "sinks": p[5, 0:B_Q_HEADS], "rel_bias": p[8:16, 0:REL_TABLE]}

    grads.update(unpack(small_g))
    delta, new_m, new_v = unpack(sd), unpack(snm), unpack(snv)

    wmv = {
        "ffn1_w_gate": (ffn1_w_gate, m_ffn1_w_gate, v_ffn1_w_gate), "ffn1_w_up": (ffn1_w_up, m_ffn1_w_up, v_ffn1_w_up),
        "ffn1_w_down": (ffn1_w_down, m_ffn1_w_down, v_ffn1_w_down), "w_in": (w_in, m_w_in, v_w_in),
        "w_proj_a": (w_proj_a, m_w_proj_a, v_w_proj_a), "w_proj_b": (w_proj_b, m_w_proj_b, v_w_proj_b),
        "w_out": (w_out, m_w_out, v_w_out),
        "ffn2_w_gate": (ffn2_w_gate, m_ffn2_w_gate, v_ffn2_w_gate), "ffn2_w_up": (ffn2_w_up, m_ffn2_w_up, v_ffn2_w_up),
        "ffn2_w_down": (ffn2_w_down, m_ffn2_w_down, v_ffn2_w_down),
    }
    def adamw_group(gname, names):
        def form(n, a):
            return a.T if n in row_form_names else a

        ds_, ms_, vs_ = _adamw([form(n, wmv[n][0]) for n in names], [grads[n] for n in names],
                               [form(n, wmv[n][1]) for n in names], [form(n, wmv[n][2]) for n in names], gname)
        for n, d_, m_, v_ in zip(names, ds_, ms_, vs_):
            delta[n], new_m[n], new_v[n] = form(n, d_), form(n, m_), form(n, v_)

    adamw_group("adamw_ffn", ["ffn1_w_gate", "ffn1_w_up", "ffn1_w_down", "ffn2_w_gate", "ffn2_w_up", "ffn2_w_down"])
    adamw_group("adamw_rest", ["w_in", "w_proj_a", "w_proj_b", "w_out"])
    for n in row_form_names:
        grads[n] = grads[n].T

    order = ["ffn1_norm", "ffn1_w_gate", "ffn1_w_up", "ffn1_w_down", "mix_norm", "w_in", "rel_bias", "sinks",
             "w_proj_a", "w_proj_b", "w_out", "ffn2_norm", "ffn2_w_gate", "ffn2_w_up", "ffn2_w_down", "final_norm"]
    grad_x = dx0.reshape(bsz, s_len, D_MODEL)
    return (loss, grad_x, *[grads[n] for n in order], *[delta[n] for n in order], *[new_m[n] for n in order],
            *[new_v[n] for n in order])
```

```python
import numpy as np
import jax
import jax.numpy as jnp
from jax import lax
from jax.experimental import pallas as pl
from jax.experimental.pallas import tpu as pltpu

F32 = jnp.float32
BF16 = jnp.bfloat16

D_MODEL = 1024
D_FF = 2816
CHUNK = 64
D_HEAD = 64
A_HEADS = 8
A_PREV = 8
MAX_REL = 128
B_Q_HEADS = 8
B_KV_HEADS = 2
B_GROUP = B_Q_HEADS // B_KV_HEADS
B_PREV = 2
REL_TABLE = (CHUNK - 1) + MAX_REL + 1
A_WIDTH = A_HEADS * D_HEAD
B_Q_WIDTH = B_Q_HEADS * D_HEAD
B_KV_WIDTH = B_KV_HEADS * D_HEAD
QKV_A = 3 * A_WIDTH
QKV_B = B_Q_WIDTH + 2 * B_KV_WIDTH
IN_WIDTH = QKV_A + QKV_B + 2 * D_MODEL
EPS = 1e-6
NEG_INF = -1e30
SCALE = 1.0 / 8.0

ADAM_LR = 0.001
ADAM_B1 = 0.9
ADAM_B2 = 0.999
ADAM_EPS = 1e-08
ADAM_WD = 0.01
ADAM_STEP = 10

N_DEV = 8
N_CHIP = 4
MESH = pl.DeviceIdType.MESH

LANES = 128
TQ = 256
TM = 256
TM_FWD = 256
FC = 256
VMEM_LIMIT = 56 << 20


def _cparams(sem, vmem=VMEM_LIMIT):
    return pltpu.CompilerParams(dimension_semantics=sem, vmem_limit_bytes=vmem)


def _dot_nt(a, b):
    return lax.dot_general(a, b, (((1,), (1,)), ((), ())), preferred_element_type=F32)


def _dot_nn(a, b):
    return lax.dot_general(a, b, (((1,), (0,)), ((), ())), preferred_element_type=F32)


def _dot_tn(a, b):
    return lax.dot_general(a, b, (((0,), (0,)), ((), ())), preferred_element_type=F32)


def _resident(shape):
    nd = len(shape)
    return pl.BlockSpec(shape, lambda *_: (0,) * nd, pipeline_mode=pl.Buffered(1))


def _rows(tm, width):
    return pl.BlockSpec((tm, width), lambda i: (i, 0))


def _colsum8(v):
    tm, n = v.shape
    return jnp.sum(v.reshape(tm // 8, 8, n), axis=0)


def _rms(x):
    r = lax.rsqrt(jnp.mean(x * x, axis=-1, keepdims=True) + EPS)
    return x * r, r


def _rms_bwd(dh, xh, r, gamma):
    dxh = dh * gamma
    dx = r * (dxh - xh * jnp.mean(dxh * xh, axis=-1, keepdims=True))
    return dx, _colsum8(dh * xh)


def _hbm():
    return pl.BlockSpec(memory_space=pltpu.HBM)


def _call(body, *, name, grid, in_specs, out_specs, out_shape, args, sem, scratch_shapes=(), hosted=()):
    in_specs, out_specs, out_shape = list(in_specs), list(out_specs), list(out_shape)
    scratch_shapes = list(scratch_shapes)
    if not hosted:
        return pl.pallas_call(body, name=name, grid=grid, in_specs=in_specs, out_specs=out_specs, out_shape=out_shape,
                              scratch_shapes=scratch_shapes, compiler_params=_cparams(sem))(*args)
    n_in, n_out, n_scr = len(in_specs), len(out_specs), len(scratch_shapes)
    x_in = [a for x in hosted for a in x.inputs]
    x_out = [s for x in hosted for s in x.out_shape]
    x_scr = [s for x in hosted for s in x.scratch]
    steps = int(np.prod(grid))
    forward_step = max(steps - 3, 0)
    relay_step = min((5 * steps) // 8, forward_step)

    def wrapped(*refs):
        pos = [0]

        def take(k):
            pos[0] += k
            return refs[pos[0] - k:pos[0]]

        ins, xin, outs, xout, scr, xscr = (take(k) for k in (n_in, len(x_in), n_out, len(x_out), n_scr, len(x_scr)))
        step = 0
        for axis, extent in enumerate(grid):
            step = step * extent + pl.program_id(axis)
        own, oi, oo, osc = [], 0, 0, 0
        for x in hosted:
            own.append((xin[oi:oi + len(x.inputs)], xout[oo:oo + len(x.out_shape)], xscr[osc:osc + len(x.scratch)]))
            oi, oo, osc = oi + len(x.inputs), oo + len(x.out_shape), osc + len(x.scratch)

        def phase(method):
            for x, (i_, o_, s_) in zip(hosted, own):
                getattr(x, method)(i_, o_, s_)

        pl.when(step == 0)(lambda: phase("start"))
        body(*ins, *outs, *scr)
        pl.when(step == relay_step)(lambda: phase("relay"))
        pl.when(step == forward_step)(lambda: phase("forward"))
        pl.when(step == steps - 1)(lambda: phase("finish"))

    res = pl.pallas_call(
        wrapped, name=name, grid=grid, in_specs=in_specs + [_hbm()] * len(x_in),
        out_specs=out_specs + [_hbm()] * len(x_out), out_shape=out_shape + x_out,
        scratch_shapes=scratch_shapes + x_scr, compiler_params=_cparams(("arbitrary",) * len(grid)))(*args, *x_in)
    rest = list(res[n_out:])
    for x in hosted:
        x.result, rest = rest[:len(x.out_shape)], rest[len(x.out_shape):]
    return list(res[:n_out])


def _ffn_fwd(x, gamma, wg_t, wu_t, wd, name, hosted=()):
    t = x.shape[0]
    f = wg_t.shape[0]

    def body(x_ref, gam_ref, wg_ref, wu_ref, wd_ref, h_ref, g_ref, u_ref, a_ref, y_ref):
        xv = x_ref[...]
        xh, _ = _rms(xv)
        h = (xh * gam_ref[...]).astype(BF16)
        h_ref[...] = h
        for j in range(f // FC):
            sl = slice(j * FC, (j + 1) * FC)
            g = _dot_nt(h, wg_ref[sl, :])
            u = _dot_nt(h, wu_ref[sl, :])
            g_ref[:, sl] = g.astype(BF16)
            u_ref[:, sl] = u.astype(BF16)
            a_ref[:, sl] = (g * jax.nn.sigmoid(g) * u).astype(BF16)
        y_ref[...] = xv + 0.5 * _dot_nn(a_ref[...], wd_ref[...])

    return _call(
        body,
        name=name,
        grid=(t // TM_FWD,),
        in_specs=[_rows(TM_FWD, D_MODEL), _resident((1, D_MODEL)), _resident((f, D_MODEL)), _resident((f, D_MODEL)),
                  _resident((f, D_MODEL))],
        out_specs=[_rows(TM_FWD, D_MODEL), _rows(TM_FWD, f), _rows(TM_FWD, f), _rows(TM_FWD, f),
                   _rows(TM_FWD, D_MODEL)],
        out_shape=[jax.ShapeDtypeStruct((t, D_MODEL), BF16), jax.ShapeDtypeStruct((t, f), BF16),
                   jax.ShapeDtypeStruct((t, f), BF16), jax.ShapeDtypeStruct((t, f), BF16),
                   jax.ShapeDtypeStruct((t, D_MODEL), F32)],
        args=(x, gamma, wg_t, wu_t, wd), sem=("parallel",), hosted=hosted)


def _ffn_bwd(d, x, gamma, g_act, u_act, wg_t, wu_t, wd, name, hosted=()):
    t = x.shape[0]
    f = wg_t.shape[0]

    def body(d_ref, x_ref, gam_ref, g_ref, u_ref, wg_ref, wu_ref, wd_ref, dx_ref, dg_ref, du_ref, db_ref, dgam_ref):
        dv = d_ref[...]
        db = (0.5 * dv).astype(BF16)
        db_ref[...] = db
        for j in range(f // FC):
            sl = slice(j * FC, (j + 1) * FC)
            da = _dot_nt(db, wd_ref[sl, :])
            g = g_ref[:, sl].astype(F32)
            u = u_ref[:, sl].astype(F32)
            s = jax.nn.sigmoid(g)
            dg_ref[:, sl] = (da * u * (s * (1.0 + g * (1.0 - s)))).astype(BF16)
            du_ref[:, sl] = (da * (g * s)).astype(BF16)
        dh = _dot_nn(dg_ref[...], wg_ref[...]) + _dot_nn(du_ref[...], wu_ref[...])
        xh, r = _rms(x_ref[...])
        dxn, dgam = _rms_bwd(dh, xh, r, gam_ref[...])
        dx_ref[...] = dv + dxn

        @pl.when(pl.program_id(0) == 0)
        def _():
            dgam_ref[...] = jnp.zeros_like(dgam_ref)

        dgam_ref[...] += dgam

    return _call(
        body,
        name=name,
        grid=(t // TM,),
        in_specs=[_rows(TM, D_MODEL), _rows(TM, D_MODEL), _resident((1, D_MODEL)), _rows(TM, f), _rows(TM, f),
                  _resident((f, D_MODEL)), _resident((f, D_MODEL)), _resident((f, D_MODEL))],
        out_specs=[_rows(TM, D_MODEL), _rows(TM, f), _rows(TM, f), _rows(TM, D_MODEL),
                   pl.BlockSpec((8, D_MODEL), lambda i: (0, 0))],
        out_shape=[jax.ShapeDtypeStruct((t, D_MODEL), F32), jax.ShapeDtypeStruct((t, f), BF16),
                   jax.ShapeDtypeStruct((t, f), BF16), jax.ShapeDtypeStruct((t, D_MODEL), BF16),
                   jax.ShapeDtypeStruct((8, D_MODEL), F32)],
        args=(d, x, gamma, g_act, u_act, wg_t, wu_t, wd), sem=("arbitrary",), hosted=hosted)


def _ffn_bwd_act(d, g_act, u_act, wd, name, hosted=()):
    t = d.shape[0]
    f = wd.shape[0]

    def body(d_ref, g_ref, u_ref, wd_ref, dg_ref, du_ref):
        db = (0.5 * d_ref[...]).astype(BF16)
        for j in range(f // FC):
            sl = slice(j * FC, (j + 1) * FC)
            da = _dot_nt(db, wd_ref[sl, :])
            g = g_ref[:, sl].astype(F32)
            u = u_ref[:, sl].astype(F32)
            s = jax.nn.sigmoid(g)
            dg_ref[:, sl] = (da * u * (s * (1.0 + g * (1.0 - s)))).astype(BF16)
            du_ref[:, sl] = (da * (g * s)).astype(BF16)

    return _call(
        body,
        name=name,
        grid=(t // TM,),
        in_specs=[_rows(TM, D_MODEL), _rows(TM, f), _rows(TM, f), _resident((f, D_MODEL))],
        out_specs=[_rows(TM, f), _rows(TM, f)],
        out_shape=[jax.ShapeDtypeStruct((t, f), BF16), jax.ShapeDtypeStruct((t, f), BF16)],
        args=(d, g_act, u_act, wd), sem=("parallel",), hosted=hosted)


def _ffn_bwd_in(d, x, gamma, dg, du, wg_t, wu_t, name, hosted=()):
    t = x.shape[0]
    f = wg_t.shape[0]

    def body(d_ref, x_ref, gam_ref, dg_ref, du_ref, wg_ref, wu_ref, dx_ref, dgam_ref):
        dh = _dot_nn(dg_ref[...], wg_ref[...]) + _dot_nn(du_ref[...], wu_ref[...])
        xh, r = _rms(x_ref[...])
        dxn, dgam = _rms_bwd(dh, xh, r, gam_ref[...])
        dx_ref[...] = d_ref[...] + dxn

        @pl.when(pl.program_id(0) == 0)
        def _():
            dgam_ref[...] = jnp.zeros_like(dgam_ref)

        dgam_ref[...] += dgam

    return _call(
        body,
        name=name,
        grid=(t // TM,),
        in_specs=[_rows(TM, D_MODEL), _rows(TM, D_MODEL), _resident((1, D_MODEL)), _rows(TM, f), _rows(TM, f),
                  _resident((f, D_MODEL)), _resident((f, D_MODEL))],
        out_specs=[_rows(TM, D_MODEL), pl.BlockSpec((8, D_MODEL), lambda i: (0, 0))],
        out_shape=[jax.ShapeDtypeStruct((t, D_MODEL), F32), jax.ShapeDtypeStruct((8, D_MODEL), F32)],
        args=(d, x, gamma, dg, du, wg_t, wu_t), sem=("arbitrary",), hosted=hosted)


def _mm_tn(pieces, b, name, tile=256, hosted=()):
    t, n = b.shape
    npc = len(pieces)
    counts = [p.shape[1] // tile for p in pieces]
    los = [sum(counts[:k]) for k in range(npc)]
    total = sum(counts)

    def body(*refs):
        a_refs, b_ref, o_ref = refs[:npc], refs[npc], refs[npc + 1]
        i = pl.program_id(0)
        for k in range(npc):
            @pl.when(jnp.logical_and(i >= los[k], i < los[k] + counts[k]))
            def _(k=k):
                o_ref[...] = _dot_tn(a_refs[k][...], b_ref[...]).astype(BF16)

    def a_spec(k):
        return pl.BlockSpec((t, tile), lambda i: (0, jnp.clip(i - los[k], 0, counts[k] - 1)))

    return _call(
        body,
        name=name,
        grid=(total,),
        in_specs=[a_spec(k) for k in range(npc)] + [_resident((t, n))],
        out_specs=[pl.BlockSpec((tile, n), lambda i: (i, 0))],
        out_shape=[jax.ShapeDtypeStruct((total * tile, n), BF16)],
        args=(*pieces, b), sem=("parallel",), hosted=hosted)[0]


def _mm_tn_proj(dya, dyb, oa, ob, tile=256):
    t = dya.shape[0]

    def body(dya_ref, dyb_ref, oa_ref, ob_ref, o_ref):
        o_ref[:, 0:A_WIDTH] = _dot_tn(dya_ref[...], oa_ref[...]).astype(BF16)
        o_ref[:, A_WIDTH:A_WIDTH + B_Q_WIDTH] = _dot_tn(dyb_ref[...], ob_ref[...]).astype(BF16)

    col = pl.BlockSpec((t, tile), lambda i: (0, i))
    return pl.pallas_call(
        body,
        name="grad_proj",
        grid=(D_MODEL // tile,),
        in_specs=[col, col, _resident((t, A_WIDTH)), _resident((t, B_Q_WIDTH))],
        out_specs=pl.BlockSpec((tile, A_WIDTH + B_Q_WIDTH), lambda i: (i, 0)),
        out_shape=jax.ShapeDtypeStruct((D_MODEL, A_WIDTH + B_Q_WIDTH), BF16),
        compiler_params=_cparams(("parallel",)),
    )(dya, dyb, oa, ob)


def _proj_fwd(x, gamma, win_t, hosted=()):
    t = x.shape[0]

    def body(x_ref, gam_ref, w_ref, h_ref, qa_ref, qb_ref, gt_ref):
        xh, _ = _rms(x_ref[...])
        h = (xh * gam_ref[...]).astype(BF16)
        h_ref[...] = h
        for j in range(QKV_A // FC):
            qa_ref[:, j * FC:(j + 1) * FC] = _dot_nt(h, w_ref[j * FC:(j + 1) * FC, :]).astype(BF16)
        for j in range(QKV_B // FC):
            lo = QKV_A + j * FC
            qb_ref[:, j * FC:(j + 1) * FC] = _dot_nt(h, w_ref[lo:lo + FC, :]).astype(BF16)
        for j in range(2 * D_MODEL // FC):
            lo = QKV_A + QKV_B + j * FC
            gt_ref[:, j * FC:(j + 1) * FC] = _dot_nt(h, w_ref[lo:lo + FC, :])

    return _call(
        body,
        name="proj_fwd",
        grid=(t // TM_FWD,),
        in_specs=[_rows(TM_FWD, D_MODEL), _resident((1, D_MODEL)), _resident((IN_WIDTH, D_MODEL))],
        out_specs=[_rows(TM_FWD, D_MODEL), _rows(TM_FWD, QKV_A), _rows(TM_FWD, QKV_B), _rows(TM_FWD, 2 * D_MODEL)],
        out_shape=[jax.ShapeDtypeStruct((t, D_MODEL), BF16), jax.ShapeDtypeStruct((t, QKV_A), BF16),
                   jax.ShapeDtypeStruct((t, QKV_B), BF16), jax.ShapeDtypeStruct((t, 2 * D_MODEL), F32)],
        args=(x, gamma, win_t), sem=("parallel",), hosted=hosted)


def _proj_bwd(d, x, gamma, pieces, win_t, hosted=()):
    t = x.shape[0]
    npc = len(pieces)
    widths = [p.shape[1] for p in pieces]
    los = [sum(widths[:k]) for k in range(npc)]

    def body(*refs):
        d_ref, x_ref, gam_ref = refs[:3]
        p_refs = refs[3:3 + npc]
        w_ref, dx_ref, db_ref, dgam_ref = refs[3 + npc:]
        dh = _dot_nn(p_refs[0][...], w_ref[0:widths[0], :])
        for k in range(1, npc):
            dh += _dot_nn(p_refs[k][...], w_ref[los[k]:los[k] + widths[k], :])
        xh, r = _rms(x_ref[...])
        dxn, dgam = _rms_bwd(dh, xh, r, gam_ref[...])
        dx = d_ref[...] + dxn
        dx_ref[...] = dx
        db_ref[...] = (0.5 * dx).astype(BF16)

        @pl.when(pl.program_id(0) == 0)
        def _():
            dgam_ref[...] = jnp.zeros_like(dgam_ref)

        dgam_ref[...] += dgam

    return _call(
        body,
        name="proj_bwd",
        grid=(t // TM,),
        in_specs=[_rows(TM, D_MODEL), _rows(TM, D_MODEL), _resident((1, D_MODEL))] + [_rows(TM, w) for w in widths]
        + [_resident((IN_WIDTH, D_MODEL))],
        out_specs=[_rows(TM, D_MODEL), _rows(TM, D_MODEL), pl.BlockSpec((8, D_MODEL), lambda i: (0, 0))],
        out_shape=[jax.ShapeDtypeStruct((t, D_MODEL), F32), jax.ShapeDtypeStruct((t, D_MODEL), BF16),
                   jax.ShapeDtypeStruct((8, D_MODEL), F32)],
        args=(d, x, gamma, *pieces, win_t), sem=("arbitrary",), hosted=hosted)


def _lane_half(shape):
    return lax.broadcasted_iota(jnp.int32, shape, len(shape) - 1) // D_HEAD


def _band_weights(q, kk, bias, sink, qs, pad):
    s = _dot_nt(q, kk) + bias
    if qs is not None:
        col = lax.broadcasted_iota(jnp.int32, s.shape, 1)
        s = jnp.where(col + qs >= pad, s, NEG_INF)
    m = jnp.max(s, axis=-1, keepdims=True)
    if sink is not None:
        m = jnp.maximum(m, sink)
    return jnp.exp(s - m), m


def _weighted_values(p, vv_ones, sink, m):
    r = _dot_nn(p.astype(BF16), vv_ones)
    den = r[:, LANES:2 * LANES]
    if sink is not None:
        den = den + jnp.exp(sink - m)
    return r[:, 0:LANES] / den


def _band_softmax(q, kk, bias, sink, qs, pad):
    p, m = _band_weights(q, kk, bias, sink, qs, pad)
    den = jnp.sum(p, axis=-1, keepdims=True)
    if sink is not None:
        den = den + jnp.exp(sink - m)
    return p, m, 1.0 / den


def _fill_padded(dst, src, pad):
    dst[0:pad, :] = jnp.zeros((pad,) + dst.shape[1:], dst.dtype)
    dst[pad:, :] = src


FWD_PAIRS = 4
BWD_PAIRS = 2


def _attn_a_fwd(qkv, bias, hosted=()):
    bsz, s_len, _ = qkv.shape
    pad = A_PREV * CHUNK
    band = TQ + pad
    pp = FWD_PAIRS
    w = pp * LANES
    nb = A_WIDTH // w

    def body(q_ref, k_ref, v_ref, b_ref, o_ref, kp, vp):
        i = pl.program_id(2)

        @pl.when(i == 0)
        def _():
            _fill_padded(kp, k_ref[...], pad)
            _fill_padded(vp, v_ref[...], pad)

        qs = pl.multiple_of(i * TQ, TQ)
        half = _lane_half((1, LANES))

        ones = jnp.ones((band, LANES), BF16)

        def block(masked):
            for pr in range(pp):
                sl = slice(pr * LANES, (pr + 1) * LANES)
                kk = kp[pl.ds(qs, band), sl]
                vv = jnp.concatenate([vp[pl.ds(qs, band), sl], ones], axis=1)
                q = q_ref[:, sl] * SCALE
                outs = []
                for j in range(2):
                    qm = jnp.where(half == j, q, jnp.zeros_like(q))
                    p, m = _band_weights(qm, kk, b_ref[2 * pr + j], None, qs if masked else None, pad)
                    outs.append(_weighted_values(p, vv, None, m))
                o_ref[:, sl] = jnp.where(half == 0, outs[0], outs[1]).astype(BF16)

        pl.when(i < pad // TQ)(lambda: block(True))
        pl.when(i >= pad // TQ)(lambda: block(False))

    return _call(
        body,
        name="attn_a_fwd",
        grid=(bsz, nb, s_len // TQ),
        in_specs=[pl.BlockSpec((None, TQ, w), lambda b, g, i: (b, i, g)),
                  pl.BlockSpec((None, s_len, w), lambda b, g, i: (b, 0, nb + g)),
                  pl.BlockSpec((None, s_len, w), lambda b, g, i: (b, 0, 2 * nb + g)),
                  pl.BlockSpec((2 * pp, TQ, band), lambda b, g, i: (g, 0, 0))],
        out_specs=[pl.BlockSpec((None, TQ, w), lambda b, g, i: (b, i, g))],
        out_shape=[jax.ShapeDtypeStruct((bsz, s_len, A_WIDTH), BF16)],
        scratch_shapes=[pltpu.VMEM((pad + s_len, w), BF16), pltpu.VMEM((pad + s_len, w), BF16)],
        args=(qkv, qkv, qkv, bias), sem=("arbitrary", "arbitrary", "arbitrary"), hosted=hosted)[0]


def _attn_a_bwd(qkv, bias, do, hosted=()):
    bsz, s_len, _ = qkv.shape
    pad = A_PREV * CHUNK
    band = TQ + pad
    n_i = s_len // TQ
    pp = BWD_PAIRS
    w = pp * LANES
    nb = A_WIDTH // w

    def body(q_ref, k_ref, v_ref, b_ref, do_ref, dq_ref, dk_ref, dv_ref, dbias_ref, kp, vp, dk_acc, dv_acc):
        b = pl.program_id(1)
        i = pl.program_id(2)

        @pl.when(i == 0)
        def _():
            _fill_padded(kp, k_ref[...], pad)
            _fill_padded(vp, v_ref[...], pad)
            dk_acc[...] = jnp.zeros_like(dk_acc)
            dv_acc[...] = jnp.zeros_like(dv_acc)

        @pl.when(jnp.logical_and(b == 0, i == 0))
        def _():
            dbias_ref[...] = jnp.zeros_like(dbias_ref)

        qs = pl.multiple_of(i * TQ, TQ)
        half = _lane_half((1, LANES))

        def block(masked):
            for pr in range(pp):
                sl = slice(pr * LANES, (pr + 1) * LANES)
                kk = kp[pl.ds(qs, band), sl]
                vv = vp[pl.ds(qs, band), sl]
                q = q_ref[:, sl] * SCALE
                dd = do_ref[:, sl]
                dqs, dks, dvs = [], [], []
                for j in range(2):
                    qm = jnp.where(half == j, q, jnp.zeros_like(q))
                    dm = jnp.where(half == j, dd, jnp.zeros_like(dd))
                    p, _, inv = _band_softmax(qm, kk, b_ref[2 * pr + j], None, qs if masked else None, pad)
                    pn = p * inv
                    dp = _dot_nt(dm, vv)
                    delta = jnp.sum(pn * dp, axis=-1, keepdims=True)
                    ds = pn * (dp - delta)
                    dbias_ref[2 * pr + j] += ds[:, band - REL_COLS:]
                    dsb = ds.astype(BF16)
                    dqs.append(_dot_nn(dsb, kk))
                    dks.append(_dot_tn(dsb, q))
                    dvs.append(_dot_tn(pn.astype(BF16), dd))
                dq_ref[:, sl] = (jnp.where(half == 0, dqs[0], dqs[1]) * SCALE).astype(BF16)
                dk_acc[pl.ds(qs, band), sl] += jnp.where(half == 0, dks[0], dks[1])
                dv_acc[pl.ds(qs, band), sl] += jnp.where(half == 0, dvs[0], dvs[1])

        pl.when(i < pad // TQ)(lambda: block(True))
        pl.when(i >= pad // TQ)(lambda: block(False))

        @pl.when(i == n_i - 1)
        def _():
            dk_ref[...] = dk_acc[pad:, :].astype(BF16)
            dv_ref[...] = dv_acc[pad:, :].astype(BF16)

    qspec = pl.BlockSpec((None, TQ, w), lambda g, b, i: (b, i, g))
    kvout = pl.BlockSpec((None, s_len, w), lambda g, b, i: (b, 0, g))
    wide = jax.ShapeDtypeStruct((bsz, s_len, A_WIDTH), BF16)
    return _call(
        body,
        name="attn_a_bwd",
        grid=(nb, bsz, n_i),
        in_specs=[qspec,
                  pl.BlockSpec((None, s_len, w), lambda g, b, i: (b, 0, nb + g)),
                  pl.BlockSpec((None, s_len, w), lambda g, b, i: (b, 0, 2 * nb + g)),
                  pl.BlockSpec((2 * pp, TQ, band), lambda g, b, i: (g, 0, 0)),
                  qspec],
        out_specs=[qspec, kvout, kvout, pl.BlockSpec((2 * pp, TQ, REL_COLS), lambda g, b, i: (g, 0, 0))],
        out_shape=[wide, wide, wide, jax.ShapeDtypeStruct((A_HEADS, TQ, REL_COLS), F32)],
        scratch_shapes=[pltpu.VMEM((pad + s_len, w), BF16), pltpu.VMEM((pad + s_len, w), BF16),
                        pltpu.VMEM((pad + s_len, w), F32), pltpu.VMEM((pad + s_len, w), F32)],
        args=(qkv, qkv, qkv, bias, do), sem=("arbitrary", "arbitrary", "arbitrary"), hosted=hosted)


def _fill_padded_dup(dst, src, pad, h, half):
    other = pltpu.roll(src, D_HEAD, 1)
    _fill_padded(dst, jnp.where(half == h, src, other), pad)


def _attn_b_fwd(qkv, bias, sink):
    bsz, s_len, _ = qkv.shape
    pad = B_PREV * CHUNK
    band = TQ + pad
    kcol = B_Q_WIDTH // LANES
    npair = B_Q_HEADS // 2

    def body(q_ref, k_ref, v_ref, b_ref, s_ref, o_ref, kp, vp):
        i = pl.program_id(1)
        half = _lane_half((1, LANES))

        @pl.when(i == 0)
        def _():
            for h in range(B_KV_HEADS):
                _fill_padded_dup(kp.at[h], k_ref[...], pad, h, half)
                _fill_padded_dup(vp.at[h], v_ref[...], pad, h, half)

        qs = pl.multiple_of(i * TQ, TQ)

        ones = jnp.ones((band, LANES), BF16)

        def block(masked):
            for pr in range(npair):
                h = pr // (B_GROUP // 2)
                sl = slice(pr * LANES, (pr + 1) * LANES)
                kk = kp[h, pl.ds(qs, band), :]
                vv = jnp.concatenate([vp[h, pl.ds(qs, band), :], ones], axis=1)
                q = q_ref[:, sl] * SCALE
                outs = []
                for j in range(2):
                    qm = jnp.where(half == j, q, jnp.zeros_like(q))
                    sink = s_ref[2 * pr + j][0:1, 0:1]
                    p, m = _band_weights(qm, kk, b_ref[2 * pr + j], sink, qs if masked else None, pad)
                    outs.append(_weighted_values(p, vv, sink, m))
                o_ref[:, sl] = jnp.where(half == 0, outs[0], outs[1]).astype(BF16)

        pl.when(i < -(-pad // TQ))(lambda: block(True))
        pl.when(i >= -(-pad // TQ))(lambda: block(False))

    return pl.pallas_call(
        body,
        name="attn_b_fwd",
        grid=(bsz, s_len // TQ),
        in_specs=[pl.BlockSpec((None, TQ, B_Q_WIDTH), lambda b, i: (b, i, 0)),
                  pl.BlockSpec((None, s_len, LANES), lambda b, i: (b, 0, kcol)),
                  pl.BlockSpec((None, s_len, LANES), lambda b, i: (b, 0, kcol + 1)),
                  pl.BlockSpec((B_Q_HEADS, TQ, band), lambda b, i: (0, 0, 0)),
                  pl.BlockSpec((B_Q_HEADS, 8, LANES), lambda b, i: (0, 0, 0))],
        out_specs=pl.BlockSpec((None, TQ, B_Q_WIDTH), lambda b, i: (b, i, 0)),
        out_shape=jax.ShapeDtypeStruct((bsz, s_len, B_Q_WIDTH), BF16),
        scratch_shapes=[pltpu.VMEM((B_KV_HEADS, pad + s_len, LANES), BF16),
                        pltpu.VMEM((B_KV_HEADS, pad + s_len, LANES), BF16)],
        compiler_params=_cparams(("arbitrary", "arbitrary")),
    )(qkv, qkv, qkv, bias, sink)


def _attn_b_bwd(qkv, bias, sink, do, hosted=()):
    bsz, s_len, _ = qkv.shape
    pad = B_PREV * CHUNK
    band = TQ + pad
    kcol = B_Q_WIDTH // LANES
    npair = B_Q_HEADS // 2
    n_i = s_len // TQ

    pp = B_GROUP // 2
    w = pp * LANES

    def body(q_ref, k_ref, v_ref, b_ref, s_ref, do_ref, dq_ref, dkv_ref, dsink_ref, kp, vp, dk_acc, dv_acc):
        b = pl.program_id(0)
        h = pl.program_id(1)
        i = pl.program_id(2)
        half = _lane_half((1, LANES))

        @pl.when(i == 0)
        def _():
            _fill_padded_dup(kp, k_ref[...], pad, h, half)
            _fill_padded_dup(vp, v_ref[...], pad, h, half)

        @pl.when(jnp.logical_and(h == 0, i == 0))
        def _():
            dk_acc[...] = jnp.zeros_like(dk_acc)
            dv_acc[...] = jnp.zeros_like(dv_acc)

        @pl.when(jnp.logical_and(b == 0, jnp.logical_and(h == 0, i == 0)))
        def _():
            dsink_ref[...] = jnp.zeros_like(dsink_ref)

        qs = pl.multiple_of(i * TQ, TQ)

        def block(masked):
            kk = kp[pl.ds(qs, band), :]
            vv = vp[pl.ds(qs, band), :]
            dk2 = jnp.zeros((band, LANES), F32)
            dv2 = jnp.zeros((band, LANES), F32)
            for pr in range(pp):
                sl = slice(pr * LANES, (pr + 1) * LANES)
                q = q_ref[:, sl] * SCALE
                dd = do_ref[:, sl]
                dqs, dks, dvs = [], [], []
                for j in range(2):
                    qm = jnp.where(half == j, q, jnp.zeros_like(q))
                    dm = jnp.where(half == j, dd, jnp.zeros_like(dd))
                    sink = s_ref[2 * pr + j][0:1, 0:1]
                    p, m, inv = _band_softmax(qm, kk, b_ref[2 * pr + j], sink, qs if masked else None, pad)
                    pn = p * inv
                    dp = _dot_nt(dm, vv)
                    delta = jnp.sum(pn * dp, axis=-1, keepdims=True)
                    ds = pn * (dp - delta)
                    dsb = ds.astype(BF16)
                    dqs.append(_dot_nn(dsb, kk))
                    dks.append(_dot_tn(dsb, q))
                    dvs.append(_dot_tn(pn.astype(BF16), dd))
                    dsk = jnp.sum(-(jnp.exp(sink - m) * inv) * delta, axis=0, keepdims=True)
                    dsink_ref[2 * pp * h + 2 * pr + j] += jnp.broadcast_to(dsk, (8, LANES))
                dq_ref[:, sl] = (jnp.where(half == 0, dqs[0], dqs[1]) * SCALE).astype(BF16)
                dk2 = dk2 + jnp.where(half == 0, dks[0], dks[1])
                dv2 = dv2 + jnp.where(half == 0, dvs[0], dvs[1])
            dk_acc[pl.ds(qs, band), :] += jnp.where(half == h, dk2 + pltpu.roll(dk2, D_HEAD, 1), 0.0)
            dv_acc[pl.ds(qs, band), :] += jnp.where(half == h, dv2 + pltpu.roll(dv2, D_HEAD, 1), 0.0)

        pl.when(i < -(-pad // TQ))(lambda: block(True))
        pl.when(i >= -(-pad // TQ))(lambda: block(False))

        @pl.when(jnp.logical_and(h == B_KV_HEADS - 1, i == n_i - 1))
        def _():
            dkv_ref[:, 0:LANES] = dk_acc[pad:, :].astype(BF16)
            dkv_ref[:, LANES:2 * LANES] = dv_acc[pad:, :].astype(BF16)

    qspec = pl.BlockSpec((None, TQ, w), lambda b, h, i: (b, i, h))
    return _call(
        body,
        name="attn_b_bwd",
        grid=(bsz, B_KV_HEADS, n_i),
        in_specs=[qspec,
                  pl.BlockSpec((None, s_len, LANES), lambda b, h, i: (b, 0, kcol)),
                  pl.BlockSpec((None, s_len, LANES), lambda b, h, i: (b, 0, kcol + 1)),
                  pl.BlockSpec((2 * pp, TQ, band), lambda b, h, i: (h, 0, 0)),
                  pl.BlockSpec((2 * pp, 8, LANES), lambda b, h, i: (h, 0, 0)),
                  qspec],
        out_specs=[qspec, pl.BlockSpec((None, s_len, 2 * LANES), lambda b, h, i: (b, 0, 0)),
                   pl.BlockSpec((B_Q_HEADS, 8, LANES), lambda b, h, i: (0, 0, 0))],
        out_shape=[jax.ShapeDtypeStruct((bsz, s_len, B_Q_WIDTH), BF16),
                   jax.ShapeDtypeStruct((bsz, s_len, 2 * B_KV_WIDTH), BF16),
                   jax.ShapeDtypeStruct((B_Q_HEADS, 8, LANES), F32)],
        scratch_shapes=[pltpu.VMEM((pad + s_len, LANES), BF16), pltpu.VMEM((pad + s_len, LANES), BF16),
                        pltpu.VMEM((pad + s_len, LANES), F32), pltpu.VMEM((pad + s_len, LANES), F32)],
        args=(qkv, qkv, qkv, bias, sink, do), sem=("arbitrary", "arbitrary", "arbitrary"), hosted=hosted)


REL_COLS = 3 * 128
REL_WRAP = 512


def _bias_a_build(tv):
    h = tv.shape[0]
    pad = A_PREV * CHUNK
    band = TQ + pad

    def body(tv_ref, o_ref):
        row = tv_ref[...]
        x = jnp.broadcast_to(row, (TQ, REL_WRAP))
        r = lax.broadcasted_iota(jnp.int32, x.shape, 0)
        for bit in range(8):
            sh = 1 << bit
            x = jnp.where((r & sh) != 0, pltpu.roll(x, sh, 1), x)
        far = jnp.broadcast_to(row[:, 0:1], (TQ, band - REL_COLS))
        full = jnp.concatenate([far, x[:, REL_WRAP // 2:REL_WRAP], x[:, 0:REL_COLS - REL_WRAP // 2]], axis=1)
        qc = (lax.broadcasted_iota(jnp.int32, full.shape, 0) + pad) // CHUNK
        kc = lax.broadcasted_iota(jnp.int32, full.shape, 1) // CHUNK
        ok = jnp.logical_and(kc <= qc, kc >= qc - A_PREV)
        o_ref[...] = jnp.where(ok, full, NEG_INF)

    return pl.pallas_call(
        body,
        name="bias_a_build",
        grid=(h,),
        in_specs=[pl.BlockSpec((None, 1, REL_WRAP), lambda hh: (hh, 0, 0))],
        out_specs=pl.BlockSpec((None, TQ, band), lambda hh: (hh, 0, 0)),
        out_shape=jax.ShapeDtypeStruct((h, TQ, band), F32),
        compiler_params=_cparams(("parallel",)),
    )(tv)


def _relbias_grad(dbias):
    h, rows, _ = dbias.shape

    def body(d_ref, o_ref):
        x = d_ref[...]
        r = lax.broadcasted_iota(jnp.int32, x.shape, 0)
        c = lax.broadcasted_iota(jnp.int32, x.shape, 1) - r
        x = jnp.where(jnp.logical_and(c >= 1, c < REL_TABLE), x, 0.0)
        for bit in range(8):
            sh = 1 << bit
            x = jnp.where((r & sh) != 0, pltpu.roll(x, REL_COLS - sh, 1), x)
        diag = jnp.sum(x, axis=0, keepdims=True)
        lane = lax.broadcasted_iota(jnp.int32, diag.shape, 1)
        diag = jnp.where(jnp.logical_and(lane >= 1, lane < REL_TABLE), diag, 0.0)
        rest = -jnp.sum(diag, axis=1, keepdims=True)
        o_ref[...] = jnp.broadcast_to(jnp.where(lane == 0, rest, diag), o_ref.shape)

    return pl.pallas_call(
        body,
        name="relbias_grad",
        grid=(h,),
        in_specs=[pl.BlockSpec((None, rows, REL_COLS), lambda hh: (hh, 0, 0))],
        out_specs=pl.BlockSpec((None, 8, REL_COLS), lambda hh: (hh, 0, 0)),
        out_shape=jax.ShapeDtypeStruct((h, 8, REL_COLS), F32),
        compiler_params=_cparams(("parallel",)),
    )(dbias)


def _mix_out_fwd(x, oa, ob, gates, proj_t, wout):
    t = x.shape[0]

    def body(x_ref, oa_ref, ob_ref, gt_ref, pt_ref, wo_ref, y_ref, ya_ref, yb_ref, mg_ref):
        ya = _dot_nt(oa_ref[...], pt_ref[:, 0:A_WIDTH])
        yb = _dot_nt(ob_ref[...], pt_ref[:, A_WIDTH:A_WIDTH + B_Q_WIDTH])
        ya_ref[...] = ya.astype(BF16)
        yb_ref[...] = yb.astype(BF16)
        mg = jax.nn.sigmoid(gt_ref[:, 0:D_MODEL]) * ya + jax.nn.sigmoid(gt_ref[:, D_MODEL:2 * D_MODEL]) * yb
        mgb = mg.astype(BF16)
        mg_ref[...] = mgb
        y_ref[...] = x_ref[...] + _dot_nn(mgb, wo_ref[...])

    return pl.pallas_call(
        body,
        name="mix_out_fwd",
        grid=(t // TM,),
        in_specs=[_rows(TM, D_MODEL), _rows(TM, A_WIDTH), _rows(TM, B_Q_WIDTH), _rows(TM, 2 * D_MODEL),
                  _resident((D_MODEL, A_WIDTH + B_Q_WIDTH)), _resident((D_MODEL, D_MODEL))],
        out_specs=[_rows(TM, D_MODEL), _rows(TM, D_MODEL), _rows(TM, D_MODEL), _rows(TM, D_MODEL)],
        out_shape=[jax.ShapeDtypeStruct((t, D_MODEL), F32), jax.ShapeDtypeStruct((t, D_MODEL), BF16),
                   jax.ShapeDtypeStruct((t, D_MODEL), BF16), jax.ShapeDtypeStruct((t, D_MODEL), BF16)],
        compiler_params=_cparams(("parallel",)),
    )(x, oa, ob, gates, proj_t, wout)


def _mix_out_bwd(d, gates, ya, yb, proj_t, wout, hosted=()):
    t = d.shape[0]

    def body(d_ref, gt_ref, ya_ref, yb_ref, pt_ref, wo_ref, db_ref, dya_ref, dyb_ref, doa_ref, dob_ref, dgt_ref):
        db = d_ref[...].astype(BF16)
        db_ref[...] = db
        dmg = _dot_nt(db, wo_ref[...])
        sa = jax.nn.sigmoid(gt_ref[:, 0:D_MODEL])
        sb = jax.nn.sigmoid(gt_ref[:, D_MODEL:2 * D_MODEL])
        dya = (dmg * sa).astype(BF16)
        dyb = (dmg * sb).astype(BF16)
        dya_ref[...] = dya
        dyb_ref[...] = dyb
        dgt_ref[:, 0:D_MODEL] = (dmg * ya_ref[...].astype(F32) * (sa * (1.0 - sa))).astype(BF16)
        dgt_ref[:, D_MODEL:2 * D_MODEL] = (dmg * yb_ref[...].astype(F32) * (sb * (1.0 - sb))).astype(BF16)
        doa_ref[...] = _dot_nn(dya, pt_ref[:, 0:A_WIDTH]).astype(BF16)
        dob_ref[...] = _dot_nn(dyb, pt_ref[:, A_WIDTH:A_WIDTH + B_Q_WIDTH]).astype(BF16)

    return _call(
        body,
        name="mix_out_bwd",
        grid=(t // TM,),
        in_specs=[_rows(TM, D_MODEL), _rows(TM, 2 * D_MODEL), _rows(TM, D_MODEL), _rows(TM, D_MODEL),
                  _resident((D_MODEL, A_WIDTH + B_Q_WIDTH)), _resident((D_MODEL, D_MODEL))],
        out_specs=[_rows(TM, D_MODEL), _rows(TM, D_MODEL), _rows(TM, D_MODEL), _rows(TM, A_WIDTH),
                   _rows(TM, B_Q_WIDTH), _rows(TM, 2 * D_MODEL)],
        out_shape=[jax.ShapeDtypeStruct((t, D_MODEL), BF16), jax.ShapeDtypeStruct((t, D_MODEL), BF16),
                   jax.ShapeDtypeStruct((t, D_MODEL), BF16), jax.ShapeDtypeStruct((t, A_WIDTH), BF16),
                   jax.ShapeDtypeStruct((t, B_Q_WIDTH), BF16), jax.ShapeDtypeStruct((t, 2 * D_MODEL), BF16)],
        args=(d, gates, ya, yb, proj_t, wout), sem=("parallel",), hosted=hosted)


def _loss_head(x, gamma, target):
    t = x.shape[0]

    def body(x_ref, gam_ref, t_ref, dx_ref, dgam_ref, loss_ref):
        xh, r = _rms(x_ref[...])
        gam = gam_ref[...]
        e = xh * gam - t_ref[...]
        dy = e * (1.0 / D_MODEL)
        dxn, dgam = _rms_bwd(dy, xh, r, gam)
        dx_ref[...] = dxn

        @pl.when(pl.program_id(0) == 0)
        def _():
            dgam_ref[...] = jnp.zeros_like(dgam_ref)
            loss_ref[...] = jnp.zeros_like(loss_ref)

        dgam_ref[...] += dgam
        loss_ref[...] += _colsum8(e * e) * (0.5 / D_MODEL)

    return pl.pallas_call(
        body,
        name="loss_head",
        grid=(t // TM,),
        in_specs=[_rows(TM, D_MODEL), _resident((1, D_MODEL)), _rows(TM, D_MODEL)],
        out_specs=[_rows(TM, D_MODEL), pl.BlockSpec((8, D_MODEL), lambda i: (0, 0)),
                   pl.BlockSpec((8, D_MODEL), lambda i: (0, 0))],
        out_shape=[jax.ShapeDtypeStruct((t, D_MODEL), F32), jax.ShapeDtypeStruct((8, D_MODEL), F32),
                   jax.ShapeDtypeStruct((8, D_MODEL), F32)],
        compiler_params=_cparams(("arbitrary",)),
    )(x, gamma, target)


def _place():
    x, y, c = lax.axis_index("x"), lax.axis_index("y"), lax.axis_index("c")
    chips = [(1 - x, y), (x, 1 - y), (1 - x, 1 - y)]
    return x, y, c, chips


class _Gather:
    per = 8

    def __init__(self, shards):
        n = len(shards)
        self.inputs = list(shards)
        self.out_shape = [jax.ShapeDtypeStruct((N_DEV * s.shape[0], s.shape[1]), s.dtype) for s in shards]
        self.scratch = [pltpu.SemaphoreType.DMA((n * self.per,)), pltpu.SemaphoreType.DMA((n * self.per,)),
                        pltpu.SemaphoreType.DMA((n,))]
        self.result = None

    def _parts(self, ins, outs, sems):
        send_sems, recv_sems, local_sems = sems
        x, y, c, chips = _place()
        me, sibling = (x, y, c), (x, y, 1 - c)
        xn, yn, dg = chips
        n = len(ins)

        def rows(k, p, part=None):
            r = ins[k].shape[0]
            base = (4 * p[0] + 2 * p[1] + p[2]) * r
            if part is None:
                return outs[k].at[pl.ds(base, r), :]
            return outs[k].at[pl.ds(base + part * (r // 2), r // 2), :]

        def copy(k, slot, block, to, src=None, part=None):
            return pltpu.make_async_remote_copy(
                src_ref=rows(k, block, part) if src is None else src, dst_ref=rows(k, block, part),
                send_sem=send_sems.at[k * self.per + slot], recv_sem=recv_sems.at[k * self.per + slot],
                device_id=to, device_id_type=MESH)

        mine = [pltpu.make_async_copy(ins[k], rows(k, me), local_sems.at[k]) for k in range(n)]
        sends, lands = [], []
        for k in range(n):
            sends.append({
                0: copy(k, 0, me, sibling, src=ins[k]),
                1: copy(k, 1, me, (*xn, c), src=ins[k]),
                2: copy(k, 2, me, (*yn, c), src=ins[k]),
                3: copy(k, 3, (*xn, c), (*yn, c), part=0),
                4: copy(k, 4, (*yn, c), (*xn, c), part=1),
                5: copy(k, 5, (*xn, c), sibling),
                6: copy(k, 6, (*yn, c), sibling),
                7: copy(k, 7, (*dg, c), sibling)})
            lands.append({
                0: copy(k, 0, sibling, me),
                1: copy(k, 1, (*xn, c), me),
                2: copy(k, 2, (*yn, c), me),
                3: copy(k, 3, (*dg, c), me, part=0),
                4: copy(k, 4, (*dg, c), me, part=1),
                5: copy(k, 5, (*xn, 1 - c), me),
                6: copy(k, 6, (*yn, 1 - c), me),
                7: copy(k, 7, (*dg, 1 - c), me)})
        return n, mine, sends, lands

    def start(self, ins, outs, sems):
        n, mine, sends, _ = self._parts(ins, outs, sems)
        for cp in mine:
            cp.start()
        for slot in (0, 1, 2):
            for k in range(n):
                sends[k][slot].start()

    def relay(self, ins, outs, sems):
        n, _, sends, lands = self._parts(ins, outs, sems)
        for k in range(n):
            lands[k][1].wait_recv()
            sends[k][3].start()
            sends[k][5].start()
        for k in range(n):
            lands[k][2].wait_recv()
            sends[k][4].start()
            sends[k][6].start()

    def forward(self, ins, outs, sems):
        n, _, sends, lands = self._parts(ins, outs, sems)
        for k in range(n):
            lands[k][3].wait_recv()
            lands[k][4].wait_recv()
            sends[k][7].start()

    def finish(self, ins, outs, sems):
        n, mine, sends, lands = self._parts(ins, outs, sems)
        for k in range(n):
            for slot in (0, 5, 6, 7):
                lands[k][slot].wait_recv()
        for k in range(n):
            for slot in range(self.per):
                sends[k][slot].wait_send()
        for cp in mine:
            cp.wait()


class _PairExchange:
    def __init__(self, grads):
        n = len(grads)
        self.inputs = list(grads)
        self.out_shape = [jax.ShapeDtypeStruct((g.shape[0] // 2, g.shape[1]), g.dtype) for g in grads]
        self.scratch = [pltpu.SemaphoreType.DMA((n * N_CHIP,)), pltpu.SemaphoreType.DMA((n * N_CHIP,))]
        self.result = None

    def _copies(self, ins, outs, sems):
        send_sems, recv_sems = sems
        x, y, c, _ = _place()
        copies = []
        for k in range(len(ins)):
            r = ins[k].shape[0] // N_DEV
            for q in range(N_CHIP):
                copies.append(pltpu.make_async_remote_copy(
                    src_ref=ins[k].at[pl.ds((2 * q + 1 - c) * r, r), :], dst_ref=outs[k].at[pl.ds(q * r, r), :],
                    send_sem=send_sems.at[k * N_CHIP + q], recv_sem=recv_sems.at[k * N_CHIP + q],
                    device_id=(x, y, 1 - c), device_id_type=MESH))
        return copies

    def start(self, ins, outs, sems):
        for cp in self._copies(ins, outs, sems):
            cp.start()

    def relay(self, ins, outs, sems):
        pass

    def forward(self, ins, outs, sems):
        pass

    def finish(self, ins, outs, sems):
        copies = self._copies(ins, outs, sems)
        for cp in copies:
            cp.wait_recv()
        for cp in copies:
            cp.wait_send()


class _ChipExchange(_PairExchange):
    def __init__(self, psums):
        n = len(psums)
        self.inputs = list(psums)
        self.out_shape = [jax.ShapeDtypeStruct((3 * p.shape[0] // N_CHIP, p.shape[1]), p.dtype) for p in psums]
        self.scratch = [pltpu.SemaphoreType.DMA((n * 3,)), pltpu.SemaphoreType.DMA((n * 3,))]
        self.result = None

    def _copies(self, ins, outs, sems):
        send_sems, recv_sems = sems
        _, _, c, chips = _place()
        copies = []
        for k in range(len(ins)):
            r = ins[k].shape[0] // N_CHIP
            for j, chip in enumerate(chips):
                copies.append(pltpu.make_async_remote_copy(
                    src_ref=ins[k].at[pl.ds((2 * chip[0] + chip[1]) * r, r), :], dst_ref=outs[k].at[pl.ds(j * r, r), :],
                    send_sem=send_sems.at[k * 3 + j], recv_sem=recv_sems.at[k * 3 + j],
                    device_id=(*chip, c), device_id_type=MESH))
        return copies


def _exchange_alone(xchg, name):
    n_in, n_out = len(xchg.inputs), len(xchg.out_shape)

    def body(*refs):
        ins, outs, sems = refs[:n_in], refs[n_in:n_in + n_out], refs[n_in + n_out:]
        xchg.start(ins, outs, sems)
        xchg.relay(ins, outs, sems)
        xchg.forward(ins, outs, sems)
        xchg.finish(ins, outs, sems)

    xchg.result = list(pl.pallas_call(
        body, name=name, in_specs=[_hbm()] * n_in, out_specs=[_hbm()] * n_out, out_shape=xchg.out_shape,
        scratch_shapes=xchg.scratch)(*xchg.inputs))
    return xchg.result


def _pair_sum(core, grads, recvd, name):
    n = len(grads)
    r = grads[0].shape[0] // N_DEV
    cdim = grads[0].shape[1]
    tr = r // 2 if r % 32 == 0 else r
    nt = r // tr

    def body(core_ref, *refs):
        del core_ref
        for k in range(n):
            refs[2 * n + k][...] = (refs[k][...].astype(F32) + refs[n + k][...].astype(F32)).astype(BF16)

    gspec = pl.BlockSpec((tr, cdim), lambda q, i, core_ref: ((2 * q + core_ref[0]) * nt + i, 0))
    rspec = pl.BlockSpec((tr, cdim), lambda q, i, core_ref: (q * nt + i, 0))
    return pl.pallas_call(
        body,
        name=name,
        grid_spec=pltpu.PrefetchScalarGridSpec(
            num_scalar_prefetch=1, grid=(N_CHIP, nt), in_specs=[gspec] * n + [rspec] * n, out_specs=[rspec] * n),
        out_shape=[jax.ShapeDtypeStruct((N_CHIP * r, cdim), BF16) for _ in range(n)],
        compiler_params=_cparams(("parallel", "parallel")),
    )(core, *grads, *recvd)


def _final_sum(chip, psums, recvd, name):
    n = len(psums)
    r = psums[0].shape[0] // N_CHIP
    cdim = psums[0].shape[1]
    tr = r // 2 if r % 32 == 0 else r
    nt = r // tr

    def body(chip_ref, *refs):
        del chip_ref
        for k in range(n):
            got = refs[n + k]
            tot = refs[k][...].astype(F32) + got[0].astype(F32)
            tot = tot + got[1].astype(F32)
            tot = tot + got[2].astype(F32)
            refs[2 * n + k][...] = tot

    pspec = pl.BlockSpec((tr, cdim), lambda i, chip_ref: (chip_ref[0] * nt + i, 0))
    rspec = pl.BlockSpec((3, tr, cdim), lambda i, chip_ref: (0, i, 0))
    ospec = pl.BlockSpec((tr, cdim), lambda i, chip_ref: (i, 0))
    return pl.pallas_call(
        body,
        name=name,
        grid_spec=pltpu.PrefetchScalarGridSpec(
            num_scalar_prefetch=1, grid=(nt,), in_specs=[pspec] * n + [rspec] * n, out_specs=[ospec] * n),
        out_shape=[jax.ShapeDtypeStruct((r, cdim), F32) for _ in range(n)],
        compiler_params=_cparams(("parallel",)),
    )(chip, *psums, *[g.reshape(3, r, cdim) for g in recvd])


SMALL_ROWS = 16


def _all_reduce_small(part):
    def body(p_ref, o_ref, buf, send_sems, recv_sems):
        x, y, c, _ = _place()
        me = 4 * x + 2 * y + c
        buf[me] = p_ref[...]
        copies = []
        for d in range(1, N_DEV):
            peer = me ^ d
            copies.append(pltpu.make_async_remote_copy(
                src_ref=p_ref, dst_ref=buf.at[me], send_sem=send_sems.at[d - 1], recv_sem=recv_sems.at[d - 1],
                device_id=(peer // 4, (peer // 2) % 2, peer % 2), device_id_type=MESH))
        for cp in copies:
            cp.start()
        for cp in copies:
            cp.wait_recv()
        for cp in copies:
            cp.wait_send()
        tot = buf[0]
        for d in range(1, N_DEV):
            tot = tot + buf[d]
        o_ref[...] = tot

    return pl.pallas_call(
        body,
        name="all_reduce_small",
        in_specs=[pl.BlockSpec(memory_space=pltpu.VMEM)],
        out_specs=pl.BlockSpec(memory_space=pltpu.VMEM),
        out_shape=jax.ShapeDtypeStruct(part.shape, F32),
        scratch_shapes=[pltpu.VMEM((N_DEV,) + part.shape, F32), pltpu.SemaphoreType.DMA((N_DEV - 1,)),
                        pltpu.SemaphoreType.DMA((N_DEV - 1,))],
    )(part)


ADAMW_STEPS = 4


def _adamw(ws, gs, ms, vs, name, hosted=()):
    n = len(ws)
    steps = ADAMW_STEPS if all(w.shape[0] % (8 * ADAMW_STEPS) == 0 for w in ws) else 1
    c1 = 1.0 - ADAM_B1 ** ADAM_STEP
    c2 = 1.0 - ADAM_B2 ** ADAM_STEP

    def body(*refs):
        for k in range(n):
            w, g, m, v = (refs[j * n + k][...] for j in range(4))
            m2 = ADAM_B1 * m + (1.0 - ADAM_B1) * g
            v2 = ADAM_B2 * v + (1.0 - ADAM_B2) * (g * g)
            delta = -ADAM_LR * ((m2 / c1) / (jnp.sqrt(v2 / c2) + ADAM_EPS) + ADAM_WD * w)
            refs[4 * n + k][...] = delta
            refs[5 * n + k][...] = m2
            refs[6 * n + k][...] = v2

    specs = [pl.BlockSpec((w.shape[0] // steps, w.shape[1]), lambda i: (i, 0)) for w in ws]
    shapes = [jax.ShapeDtypeStruct(w.shape, F32) for w in ws]
    outs = _call(
        body,
        name=name,
        grid=(steps,),
        in_specs=specs * 4,
        out_specs=specs * 3,
        out_shape=shapes * 3,
        args=(*ws, *gs, *ms, *vs), sem=("parallel",), hosted=hosted)
    return outs[:n], outs[n:2 * n], outs[2 * n:]


def _bias_b():
    pad = B_PREV * CHUNK
    slopes = np.array([2.0 ** (-8.0 * (i + 1) / B_Q_HEADS) for i in range(B_Q_HEADS)], dtype=np.float32)
    dist = np.abs(np.arange(TQ)[:, None] - np.arange(TQ + pad)[None, :] + pad).astype(np.float32)
    bias = -slopes.reshape(B_Q_HEADS, 1, 1) * dist[None]
    qc = (np.arange(TQ)[:, None] + pad) // CHUNK
    kc = np.arange(TQ + pad)[None, :] // CHUNK
    allowed = (kc <= qc) & (kc >= qc - B_PREV)
    return np.where(allowed[None], bias, np.float32(NEG_INF)).astype(np.float32)


def kernel(x, ffn1_norm, ffn1_w_gate, ffn1_w_up, ffn1_w_down, mix_norm, w_in, rel_bias, sinks, w_proj_a, w_proj_b, w_out, ffn2_norm, ffn2_w_gate, ffn2_w_up, ffn2_w_down, final_norm, loss_target, m_ffn1_norm, m_ffn1_w_gate, m_ffn1_w_up, m_ffn1_w_down, m_mix_norm, m_w_in, m_rel_bias, m_sinks, m_w_proj_a, m_w_proj_b, m_w_out, m_ffn2_norm, m_ffn2_w_gate, m_ffn2_w_up, m_ffn2_w_down, m_final_norm, v_ffn1_norm, v_ffn1_w_gate, v_ffn1_w_up, v_ffn1_w_down, v_mix_norm, v_w_in, v_rel_bias, v_sinks, v_w_proj_a, v_w_proj_b, v_w_out, v_ffn2_norm, v_ffn2_w_gate, v_ffn2_w_up, v_ffn2_w_down, v_final_norm):
    bsz, s_len, _ = x.shape
    t = bsz * s_len
    core = lax.axis_index("c").astype(jnp.int32).reshape(1)
    chip = (2 * lax.axis_index("x") + lax.axis_index("y")).astype(jnp.int32).reshape(1)

    def row_form(w):
        return w.astype(BF16).T

    wg1, wu1, wd1 = _exchange_alone(
        _Gather([row_form(ffn1_w_gate), row_form(ffn1_w_up), ffn1_w_down.astype(BF16)]), "gather_ffn1")
    gather_mix = _Gather([row_form(w_in), jnp.concatenate([row_form(w_proj_a), row_form(w_proj_b)], axis=1),
                          w_out.astype(BF16)])
    gather_ffn2_gate = _Gather([row_form(ffn2_w_gate)])
    gather_ffn2_rest = _Gather([row_form(ffn2_w_up), ffn2_w_down.astype(BF16)])

    x0 = x.reshape(t, D_MODEL)
    tgt = loss_target.reshape(t, D_MODEL)
    gam1, gam2, gam3, gam4 = (g.reshape(1, D_MODEL) for g in (ffn1_norm, mix_norm, ffn2_norm, final_norm))

    h1, g1, u1, a1, x1 = _ffn_fwd(x0, gam1, wg1, wu1, wd1, "ffn1_fwd", hosted=[gather_mix])
    win_t, proj_t, wout = gather_mix.result
    h2, qkv_a, qkv_b, gates = _proj_fwd(x1, gam2, win_t, hosted=[gather_ffn2_gate])
    (wg2,) = gather_ffn2_gate.result
    qkv_a3 = qkv_a.reshape(bsz, s_len, QKV_A)
    qkv_b3 = qkv_b.reshape(bsz, s_len, QKV_B)

    far = jnp.broadcast_to(rel_bias[:, REL_TABLE - 1:REL_TABLE], (A_HEADS, REL_WRAP // 2))
    tv = jnp.concatenate([far, jnp.flip(rel_bias, axis=1), jnp.zeros((A_HEADS, REL_WRAP // 2 - REL_TABLE), F32)], axis=1)
    bias_a = _bias_a_build(tv.reshape(A_HEADS, 1, REL_WRAP))
    bias_b = jnp.asarray(_bias_b())
    sink_rows = jnp.broadcast_to(sinks.reshape(B_Q_HEADS, 1, 1), (B_Q_HEADS, 8, LANES))

    oa = _attn_a_fwd(qkv_a3, bias_a, hosted=[gather_ffn2_rest]).reshape(t, A_WIDTH)
    wu2, wd2 = gather_ffn2_rest.result
    ob = _attn_b_fwd(qkv_b3, bias_b, sink_rows).reshape(t, B_Q_WIDTH)
    x2, ya, yb, mg = _mix_out_fwd(x1, oa, ob, gates, proj_t, wout)
    h3, g2, u2, a2, x3 = _ffn_fwd(x2, gam3, wg2, wu2, wd2, "ffn2_fwd")

    dx3, dgam4, loss_part = _loss_head(x3, gam4, tgt)

    dx2, dg2, du2, db2, dgam3 = _ffn_bwd(dx3, x2, gam3, g2, u2, wg2, wu2, wd2, "ffn2_bwd")
    gw_ffn2 = [_mm_tn([dg2], h3, "grad_ffn2_gate"), _mm_tn([du2], h3, "grad_ffn2_up"),
               _mm_tn([a2], db2, "grad_ffn2_down")]
    pairx_ffn2 = _PairExchange(gw_ffn2)
    dxb, dya, dyb, doa, dob, dgates = _mix_out_bwd(dx2, gates, ya, yb, proj_t, wout, hosted=[pairx_ffn2])
    psum_ffn2 = _pair_sum(core, gw_ffn2, pairx_ffn2.result, "pair_sum_ffn2")
    gw_out = _mm_tn([mg], dxb, "grad_w_out")
    gw_proj = _mm_tn_proj(dya, dyb, oa, ob)

    chipx_ffn2 = _ChipExchange(psum_ffn2)
    dqa, dka, dva, dbias_a = _attn_a_bwd(qkv_a3, bias_a, doa.reshape(bsz, s_len, A_WIDTH), hosted=[chipx_ffn2])
    pairx_out = _PairExchange([gw_proj, gw_out])
    dqb, dkvb, dsink = _attn_b_bwd(qkv_b3, bias_b, sink_rows, dob.reshape(bsz, s_len, B_Q_WIDTH), hosted=[pairx_out])
    drel_lanes = _relbias_grad(dbias_a)
    dproj = [dqa.reshape(t, A_WIDTH), dka.reshape(t, A_WIDTH), dva.reshape(t, A_WIDTH), dqb.reshape(t, B_Q_WIDTH),
             dkvb.reshape(t, 2 * B_KV_WIDTH), dgates]

    gw_in = _mm_tn(dproj, h2, "grad_w_in")
    pairx_in = _PairExchange([gw_in])
    psum_out = _pair_sum(core, [gw_proj, gw_out], pairx_out.result, "pair_sum_mix")
    chipx_out = _ChipExchange(psum_out)
    dx1, db1, dgam2 = _proj_bwd(dx2, x1, gam2, dproj, win_t, hosted=[pairx_in, chipx_out])
    psum_in = _pair_sum(core, [gw_in], pairx_in.result, "pair_sum_w_in")
    gw_d1 = _mm_tn([a1], db1, "grad_ffn1_down")

    chipx_in = _ChipExchange(psum_in)
    pairx_d1 = _PairExchange([gw_d1])
    dg1, du1 = _ffn_bwd_act(dx1, g1, u1, wd1, "ffn1_bwd_act", hosted=[chipx_in, pairx_d1])
    psum_d1 = _pair_sum(core, [gw_d1], pairx_d1.result, "pair_sum_ffn1_down")
    chipx_d1 = _ChipExchange(psum_d1)
    gw_g1 = _mm_tn([dg1], h1, "grad_ffn1_gate", hosted=[chipx_d1])
    from_sibling_g1 = _exchange_alone(_PairExchange([gw_g1]), "pair_exchange_ffn1_gate")
    psum_g1 = _pair_sum(core, [gw_g1], from_sibling_g1, "pair_sum_ffn1_gate")
    chipx_g1 = _ChipExchange(psum_g1)
    gw_u1 = _mm_tn([du1], h1, "grad_ffn1_up", hosted=[chipx_g1])
    from_sibling_u1 = _exchange_alone(_PairExchange([gw_u1]), "pair_exchange_ffn1_up")
    psum_u1 = _pair_sum(core, [gw_u1], from_sibling_u1, "pair_sum_ffn1_up")
    chipx_u1 = _ChipExchange(psum_u1)
    dx0, dgam1 = _ffn_bwd_in(dx1, x0, gam1, dg1, du1, wg1, wu1, "ffn1_bwd_in", hosted=[chipx_u1])

    g_g1, g_u1, g_d1, g_g2, g_u2, g_d2 = _final_sum(
        chip, psum_g1 + psum_u1 + psum_d1 + psum_ffn2,
        chipx_g1.result + chipx_u1.result + chipx_d1.result + chipx_ffn2.result, "grad_sum_ffn")
    (g_in,) = _final_sum(chip, psum_in, chipx_in.result, "grad_sum_w_in")
    g_proj, g_out = _final_sum(chip, psum_out, chipx_out.result, "grad_sum_mix")
    row_form_names = ("ffn1_w_gate", "ffn1_w_up", "w_in", "ffn2_w_gate", "ffn2_w_up")
    grads = {
        "ffn1_w_gate": g_g1, "ffn1_w_up": g_u1, "ffn1_w_down": g_d1, "w_in": g_in,
        "w_proj_a": g_proj[:, 0:A_WIDTH].T, "w_proj_b": g_proj[:, A_WIDTH:].T, "w_out": g_out,
        "ffn2_w_gate": g_g2, "ffn2_w_up": g_u2, "ffn2_w_down": g_d2,
    }

    def row_of(v):
        return jnp.pad(v.reshape(1, -1), ((0, 0), (0, D_MODEL - v.size)))

    def table_rows(v):
        return jnp.pad(v, ((0, 0), (0, D_MODEL - REL_TABLE)))

    drel_local = jnp.flip(drel_lanes[:, 0, 0:REL_TABLE], axis=1)
    small_part = jnp.concatenate(
        [jnp.sum(dgam1, axis=0, keepdims=True), jnp.sum(dgam2, axis=0, keepdims=True),
         jnp.sum(dgam3, axis=0, keepdims=True), jnp.sum(dgam4, axis=0, keepdims=True),
         row_of(jnp.sum(loss_part)), row_of(dsink[:, 0, 0]), jnp.zeros((2, D_MODEL), F32),
         table_rows(drel_local)], axis=0)
    small = _all_reduce_small(small_part)
    loss = small[4, 0]

    def pack(n1, n2, n3, n4, sk, tb):
        return jnp.concatenate([n1.reshape(1, -1), n2.reshape(1, -1), n3.reshape(1, -1), n4.reshape(1, -1),
                                jnp.zeros((1, D_MODEL), F32), row_of(sk), jnp.zeros((2, D_MODEL), F32), table_rows(tb)],
                               axis=0)

    live = np.zeros((SMALL_ROWS, D_MODEL), np.float32)
    live[0:4] = 1.0
    live[5, 0:B_Q_HEADS] = 1.0
    live[8:16, 0:REL_TABLE] = 1.0
    small_g = small * jnp.asarray(live)
    sw = pack(ffn1_norm, mix_norm, ffn2_norm, final_norm, sinks, rel_bias)
    sm = pack(m_ffn1_norm, m_mix_norm, m_ffn2_norm, m_final_norm, m_sinks, m_rel_bias)
    sv = pack(v_ffn1_norm, v_mix_norm, v_ffn2_norm, v_final_norm, v_sinks, v_rel_bias)
    (sd,), (snm,), (snv,) = _adamw([sw], [small_g], [sm], [sv], "adamw_small")

    def unpack(p):
        return {"ffn1_norm": p[0], "mix_norm": p[1], "ffn2_norm": p[2], "final_norm": p[3],
                "sinks": p[5, 0:B_Q_HEADS], "rel_bias": p[8:16, 0:REL_TABLE]}

    grads.update(unpack(small_g))
    delta, new_m, new_v = unpack(sd), unpack(snm), unpack(snv)

    wmv = {
        "ffn1_w_gate": (ffn1_w_gate, m_ffn1_w_gate, v_ffn1_w_gate), "ffn1_w_up": (ffn1_w_up, m_ffn1_w_up, v_ffn1_w_up),
        "ffn1_w_down": (ffn1_w_down, m_ffn1_w_down, v_ffn1_w_down), "w_in": (w_in, m_w_in, v_w_in),
        "w_proj_a": (w_proj_a, m_w_proj_a, v_w_proj_a), "w_proj_b": (w_proj_b, m_w_proj_b, v_w_proj_b),
        "w_out": (w_out, m_w_out, v_w_out),
        "ffn2_w_gate": (ffn2_w_gate, m_ffn2_w_gate, v_ffn2_w_gate), "ffn2_w_up": (ffn2_w_up, m_ffn2_w_up, v_ffn2_w_up),
        "ffn2_w_down": (ffn2_w_down, m_ffn2_w_down, v_ffn2_w_down),
    }
    def adamw_group(gname, names):
        def form(n, a):
            return a.T if n in row_form_names else a

        ds_, ms_, vs_ = _adamw([form(n, wmv[n][0]) for n in names], [grads[n] for n in names],
                               [form(n, wmv[n][1]) for n in names], [form(n, wmv[n][2]) for n in names], gname)
        for n, d_, m_, v_ in zip(names, ds_, ms_, vs_):
            delta[n], new_m[n], new_v[n] = form(n, d_), form(n, m_), form(n, v_)

    adamw_group("adamw_ffn", ["ffn1_w_gate", "ffn1_w_up", "ffn1_w_down", "ffn2_w_gate", "ffn2_w_up", "ffn2_w_down"])
    adamw_group("adamw_rest", ["w_in", "w_proj_a", "w_proj_b", "w_out"])
    for n in row_form_names:
        grads[n] = grads[n].T

    order = ["ffn1_norm", "ffn1_w_gate", "ffn1_w_up", "ffn1_w_down", "mix_norm", "w_in", "rel_bias", "sinks",
             "w_proj_a", "w_proj_b", "w_out", "ffn2_norm", "ffn2_w_gate", "ffn2_w_up", "ffn2_w_down", "final_norm"]
    grad_x = dx0.reshape(bsz, s_len, D_MODEL)
    return (loss, grad_x, *[grads[n] for n in order], *[delta[n] for n in order], *[new_m[n] for n in order],
            *[new_v[n] for n in order])
```

```python
import numpy as np
import jax
import jax.numpy as jnp
from jax import lax
from jax.experimental import pallas as pl
from jax.experimental.pallas import tpu as pltpu

F32 = jnp.float32
BF16 = jnp.bfloat16

D_MODEL = 1024
D_FF = 2816
CHUNK = 64
D_HEAD = 64
A_HEADS = 8
A_PREV = 8
MAX_REL = 128
B_Q_HEADS = 8
B_KV_HEADS = 2
B_GROUP = B_Q_HEADS // B_KV_HEADS
B_PREV = 2
REL_TABLE = (CHUNK - 1) + MAX_REL + 1
A_WIDTH = A_HEADS * D_HEAD
B_Q_WIDTH = B_Q_HEADS * D_HEAD
B_KV_WIDTH = B_KV_HEADS * D_HEAD
QKV_A = 3 * A_WIDTH
QKV_B = B_Q_WIDTH + 2 * B_KV_WIDTH
IN_WIDTH = QKV_A + QKV_B + 2 * D_MODEL
EPS = 1e-6
NEG_INF = -1e30
SCALE = 1.0 / 8.0

ADAM_LR = 0.001
ADAM_B1 = 0.9
ADAM_B2 = 0.999
ADAM_EPS = 1e-08
ADAM_WD = 0.01
ADAM_STEP = 10

N_DEV = 8
N_CHIP = 4
MESH = pl.DeviceIdType.MESH

LANES = 128
TQ = 256
TM = 256
TM_FWD = 256
FC = 256
VMEM_LIMIT = 56 << 20


def _cparams(sem, vmem=VMEM_LIMIT):
    return pltpu.CompilerParams(dimension_semantics=sem, vmem_limit_bytes=vmem)


def _dot_nt(a, b):
    return lax.dot_general(a, b, (((1,), (1,)), ((), ())), preferred_element_type=F32)


def _dot_nn(a, b):
    return lax.dot_general(a, b, (((1,), (0,)), ((), ())), preferred_element_type=F32)


def _dot_tn(a, b):
    return lax.dot_general(a, b, (((0,), (0,)), ((), ())), preferred_element_type=F32)


def _resident(shape):
    nd = len(shape)
    return pl.BlockSpec(shape, lambda *_: (0,) * nd, pipeline_mode=pl.Buffered(1))


def _rows(tm, width):
    return pl.BlockSpec((tm, width), lambda i: (i, 0))


def _colsum8(v):
    tm, n = v.shape
    return jnp.sum(v.reshape(tm // 8, 8, n), axis=0)


def _rms(x):
    r = lax.rsqrt(jnp.mean(x * x, axis=-1, keepdims=True) + EPS)
    return x * r, r


def _rms_bwd(dh, xh, r, gamma):
    dxh = dh * gamma
    dx = r * (dxh - xh * jnp.mean(dxh * xh, axis=-1, keepdims=True))
    return dx, _colsum8(dh * xh)


def _hbm():
    return pl.BlockSpec(memory_space=pltpu.HBM)


def _call(body, *, name, grid, in_specs, out_specs, out_shape, args, sem, scratch_shapes=(), hosted=()):
    in_specs, out_specs, out_shape = list(in_specs), list(out_specs), list(out_shape)
    scratch_shapes = list(scratch_shapes)
    if not hosted:
        return pl.pallas_call(body, name=name, grid=grid, in_specs=in_specs, out_specs=out_specs, out_shape=out_shape,
                              scratch_shapes=scratch_shapes, compiler_params=_cparams(sem))(*args)
    n_in, n_out, n_scr = len(in_specs), len(out_specs), len(scratch_shapes)
    x_in = [a for x in hosted for a in x.inputs]
    x_out = [s for x in hosted for s in x.out_shape]
    x_scr = [s for x in hosted for s in x.scratch]
    steps = int(np.prod(grid))
    forward_step = max(steps - 3, 0)
    relay_step = min((5 * steps) // 8, forward_step)

    def wrapped(*refs):
        pos = [0]

        def take(k):
            pos[0] += k
            return refs[pos[0] - k:pos[0]]

        ins, xin, outs, xout, scr, xscr = (take(k) for k in (n_in, len(x_in), n_out, len(x_out), n_scr, len(x_scr)))
        step = 0
        for axis, extent in enumerate(grid):
            step = step * extent + pl.program_id(axis)
        own, oi, oo, osc = [], 0, 0, 0
        for x in hosted:
            own.append((xin[oi:oi + len(x.inputs)], xout[oo:oo + len(x.out_shape)], xscr[osc:osc + len(x.scratch)]))
            oi, oo, osc = oi + len(x.inputs), oo + len(x.out_shape), osc + len(x.scratch)

        def phase(method):
            for x, (i_, o_, s_) in zip(hosted, own):
                getattr(x, method)(i_, o_, s_)

        pl.when(step == 0)(lambda: phase("start"))
        body(*ins, *outs, *scr)
        pl.when(step == relay_step)(lambda: phase("relay"))
        pl.when(step == forward_step)(lambda: phase("forward"))
        pl.when(step == steps - 1)(lambda: phase("finish"))

    res = pl.pallas_call(
        wrapped, name=name, grid=grid, in_specs=in_specs + [_hbm()] * len(x_in),
        out_specs=out_specs + [_hbm()] * len(x_out), out_shape=out_shape + x_out,
        scratch_shapes=scratch_shapes + x_scr, compiler_params=_cparams(("arbitrary",) * len(grid)))(*args, *x_in)
    rest = list(res[n_out:])
    for x in hosted:
        x.result, rest = rest[:len(x.out_shape)], rest[len(x.out_shape):]
    return list(res[:n_out])


def _ffn_fwd(x, gamma, wg_t, wu_t, wd, name, hosted=()):
    t = x.shape[0]
    f = wg_t.shape[0]

    def body(x_ref, gam_ref, wg_ref, wu_ref, wd_ref, h_ref, g_ref, u_ref, a_ref, y_ref):
        xv = x_ref[...]
        xh, _ = _rms(xv)
        h = (xh * gam_ref[...]).astype(BF16)
        h_ref[...] = h
        for j in range(f // FC):
            sl = slice(j * FC, (j + 1) * FC)
            g = _dot_nt(h, wg_ref[sl, :])
            u = _dot_nt(h, wu_ref[sl, :])
            g_ref[:, sl] = g.astype(BF16)
            u_ref[:, sl] = u.astype(BF16)
            a_ref[:, sl] = (g * jax.nn.sigmoid(g) * u).astype(BF16)
        y_ref[...] = xv + 0.5 * _dot_nn(a_ref[...], wd_ref[...])

    return _call(
        body,
        name=name,
        grid=(t // TM_FWD,),
        in_specs=[_rows(TM_FWD, D_MODEL), _resident((1, D_MODEL)), _resident((f, D_MODEL)), _resident((f, D_MODEL)),
                  _resident((f, D_MODEL))],
        out_specs=[_rows(TM_FWD, D_MODEL), _rows(TM_FWD, f), _rows(TM_FWD, f), _rows(TM_FWD, f),
                   _rows(TM_FWD, D_MODEL)],
        out_shape=[jax.ShapeDtypeStruct((t, D_MODEL), BF16), jax.ShapeDtypeStruct((t, f), BF16),
                   jax.ShapeDtypeStruct((t, f), BF16), jax.ShapeDtypeStruct((t, f), BF16),
                   jax.ShapeDtypeStruct((t, D_MODEL), F32)],
        args=(x, gamma, wg_t, wu_t, wd), sem=("parallel",), hosted=hosted)


def _ffn_up(x, gamma, wg_t, wu_t, name, hosted=()):
    t = x.shape[0]
    f = wg_t.shape[0]

    def body(x_ref, gam_ref, wg_ref, wu_ref, h_ref, g_ref, u_ref, a_ref):
        xh, _ = _rms(x_ref[...])
        h = (xh * gam_ref[...]).astype(BF16)
        h_ref[...] = h
        for j in range(f // FC):
            sl = slice(j * FC, (j + 1) * FC)
            g = _dot_nt(h, wg_ref[sl, :])
            u = _dot_nt(h, wu_ref[sl, :])
            g_ref[:, sl] = g.astype(BF16)
            u_ref[:, sl] = u.astype(BF16)
            a_ref[:, sl] = (g * jax.nn.sigmoid(g) * u).astype(BF16)

    return _call(
        body,
        name=name,
        grid=(t // TM,),
        in_specs=[_rows(TM, D_MODEL), _resident((1, D_MODEL)), _resident((f, D_MODEL)), _resident((f, D_MODEL))],
        out_specs=[_rows(TM, D_MODEL), _rows(TM, f), _rows(TM, f), _rows(TM, f)],
        out_shape=[jax.ShapeDtypeStruct((t, D_MODEL), BF16), jax.ShapeDtypeStruct((t, f), BF16),
                   jax.ShapeDtypeStruct((t, f), BF16), jax.ShapeDtypeStruct((t, f), BF16)],
        args=(x, gamma, wg_t, wu_t), sem=("parallel",), hosted=hosted)


def _ffn_down(x, a_act, wd, name, hosted=()):
    t = x.shape[0]
    f = wd.shape[0]

    def body(x_ref, a_ref, wd_ref, y_ref):
        y_ref[...] = x_ref[...] + 0.5 * _dot_nn(a_ref[...], wd_ref[...])

    return _call(
        body,
        name=name,
        grid=(t // TM,),
        in_specs=[_rows(TM, D_MODEL), _rows(TM, f), _resident((f, D_MODEL))],
        out_specs=[_rows(TM, D_MODEL)],
        out_shape=[jax.ShapeDtypeStruct((t, D_MODEL), F32)],
        args=(x, a_act, wd), sem=("parallel",), hosted=hosted)[0]


def _ffn_bwd(d, x, gamma, g_act, u_act, wg_t, wu_t, wd, name, hosted=()):
    t = x.shape[0]
    f = wg_t.shape[0]

    def body(d_ref, x_ref, gam_ref, g_ref, u_ref, wg_ref, wu_ref, wd_ref, dx_ref, dg_ref, du_ref, db_ref, dgam_ref):
        dv = d_ref[...]
        db = (0.5 * dv).astype(BF16)
        db_ref[...] = db
        for j in range(f // FC):
            sl = slice(j * FC, (j + 1) * FC)
            da = _dot_nt(db, wd_ref[sl, :])
            g = g_ref[:, sl].astype(F32)
            u = u_ref[:, sl].astype(F32)
            s = jax.nn.sigmoid(g)
            dg_ref[:, sl] = (da * u * (s * (1.0 + g * (1.0 - s)))).astype(BF16)
            du_ref[:, sl] = (da * (g * s)).astype(BF16)
        dh = _dot_nn(dg_ref[...], wg_ref[...]) + _dot_nn(du_ref[...], wu_ref[...])
        xh, r = _rms(x_ref[...])
        dxn, dgam = _rms_bwd(dh, xh, r, gam_ref[...])
        dx_ref[...] = dv + dxn

        @pl.when(pl.program_id(0) == 0)
        def _():
            dgam_ref[...] = jnp.zeros_like(dgam_ref)

        dgam_ref[...] += dgam

    return _call(
        body,
        name=name,
        grid=(t // TM,),
        in_specs=[_rows(TM, D_MODEL), _rows(TM, D_MODEL), _resident((1, D_MODEL)), _rows(TM, f), _rows(TM, f),
                  _resident((f, D_MODEL)), _resident((f, D_MODEL)), _resident((f, D_MODEL))],
        out_specs=[_rows(TM, D_MODEL), _rows(TM, f), _rows(TM, f), _rows(TM, D_MODEL),
                   pl.BlockSpec((8, D_MODEL), lambda i: (0, 0))],
        out_shape=[jax.ShapeDtypeStruct((t, D_MODEL), F32), jax.ShapeDtypeStruct((t, f), BF16),
                   jax.ShapeDtypeStruct((t, f), BF16), jax.ShapeDtypeStruct((t, D_MODEL), BF16),
                   jax.ShapeDtypeStruct((8, D_MODEL), F32)],
        args=(d, x, gamma, g_act, u_act, wg_t, wu_t, wd), sem=("arbitrary",), hosted=hosted)


def _ffn_bwd_act(d, g_act, u_act, wd, name, hosted=()):
    t = d.shape[0]
    f = wd.shape[0]

    def body(d_ref, g_ref, u_ref, wd_ref, dg_ref, du_ref):
        db = (0.5 * d_ref[...]).astype(BF16)
        for j in range(f // FC):
            sl = slice(j * FC, (j + 1) * FC)
            da = _dot_nt(db, wd_ref[sl, :])
            g = g_ref[:, sl].astype(F32)
            u = u_ref[:, sl].astype(F32)
            s = jax.nn.sigmoid(g)
            dg_ref[:, sl] = (da * u * (s * (1.0 + g * (1.0 - s)))).astype(BF16)
            du_ref[:, sl] = (da * (g * s)).astype(BF16)

    return _call(
        body,
        name=name,
        grid=(t // TM,),
        in_specs=[_rows(TM, D_MODEL), _rows(TM, f), _rows(TM, f), _resident((f, D_MODEL))],
        out_specs=[_rows(TM, f), _rows(TM, f)],
        out_shape=[jax.ShapeDtypeStruct((t, f), BF16), jax.ShapeDtypeStruct((t, f), BF16)],
        args=(d, g_act, u_act, wd), sem=("parallel",), hosted=hosted)


def _ffn_bwd_in(d, x, gamma, dg, du, wg_t, wu_t, name, hosted=()):
    t = x.shape[0]
    f = wg_t.shape[0]

    def body(d_ref, x_ref, gam_ref, dg_ref, du_ref, wg_ref, wu_ref, dx_ref, dgam_ref):
        dh = _dot_nn(dg_ref[...], wg_ref[...]) + _dot_nn(du_ref[...], wu_ref[...])
        xh, r = _rms(x_ref[...])
        dxn, dgam = _rms_bwd(dh, xh, r, gam_ref[...])
        dx_ref[...] = d_ref[...] + dxn

        @pl.when(pl.program_id(0) == 0)
        def _():
            dgam_ref[...] = jnp.zeros_like(dgam_ref)

        dgam_ref[...] += dgam

    return _call(
        body,
        name=name,
        grid=(t // TM,),
        in_specs=[_rows(TM, D_MODEL), _rows(TM, D_MODEL), _resident((1, D_MODEL)), _rows(TM, f), _rows(TM, f),
                  _resident((f, D_MODEL)), _resident((f, D_MODEL))],
        out_specs=[_rows(TM, D_MODEL), pl.BlockSpec((8, D_MODEL), lambda i: (0, 0))],
        out_shape=[jax.ShapeDtypeStruct((t, D_MODEL), F32), jax.ShapeDtypeStruct((8, D_MODEL), F32)],
        args=(d, x, gamma, dg, du, wg_t, wu_t), sem=("arbitrary",), hosted=hosted)


def _mm_tn(pieces, b, name, tile=256, hosted=()):
    t, n = b.shape
    npc = len(pieces)
    counts = [p.shape[1] // tile for p in pieces]
    los = [sum(counts[:k]) for k in range(npc)]
    total = sum(counts)

    def body(*refs):
        a_refs, b_ref, o_ref = refs[:npc], refs[npc], refs[npc + 1]
        i = pl.program_id(0)
        for k in range(npc):
            @pl.when(jnp.logical_and(i >= los[k], i < los[k] + counts[k]))
            def _(k=k):
                o_ref[...] = _dot_tn(a_refs[k][...], b_ref[...]).astype(BF16)

    def a_spec(k):
        return pl.BlockSpec((t, tile), lambda i: (0, jnp.clip(i - los[k], 0, counts[k] - 1)))

    return _call(
        body,
        name=name,
        grid=(total,),
        in_specs=[a_spec(k) for k in range(npc)] + [_resident((t, n))],
        out_specs=[pl.BlockSpec((tile, n), lambda i: (i, 0))],
        out_shape=[jax.ShapeDtypeStruct((total * tile, n), BF16)],
        args=(*pieces, b), sem=("parallel",), hosted=hosted)[0]


def _mm_tn_proj(dya, dyb, oa, ob, tile=256):
    t = dya.shape[0]

    def body(dya_ref, dyb_ref, oa_ref, ob_ref, o_ref):
        o_ref[:, 0:A_WIDTH] = _dot_tn(dya_ref[...], oa_ref[...]).astype(BF16)
        o_ref[:, A_WIDTH:A_WIDTH + B_Q_WIDTH] = _dot_tn(dyb_ref[...], ob_ref[...]).astype(BF16)

    col = pl.BlockSpec((t, tile), lambda i: (0, i))
    return pl.pallas_call(
        body,
        name="grad_proj",
        grid=(D_MODEL // tile,),
        in_specs=[col, col, _resident((t, A_WIDTH)), _resident((t, B_Q_WIDTH))],
        out_specs=pl.BlockSpec((tile, A_WIDTH + B_Q_WIDTH), lambda i: (i, 0)),
        out_shape=jax.ShapeDtypeStruct((D_MODEL, A_WIDTH + B_Q_WIDTH), BF16),
        compiler_params=_cparams(("parallel",)),
    )(dya, dyb, oa, ob)


def _proj_fwd(x, gamma, win_t, hosted=()):
    t = x.shape[0]

    def body(x_ref, gam_ref, w_ref, h_ref, qa_ref, qb_ref, gt_ref):
        xh, _ = _rms(x_ref[...])
        h = (xh * gam_ref[...]).astype(BF16)
        h_ref[...] = h
        for j in range(QKV_A // FC):
            qa_ref[:, j * FC:(j + 1) * FC] = _dot_nt(h, w_ref[j * FC:(j + 1) * FC, :]).astype(BF16)
        for j in range(QKV_B // FC):
            lo = QKV_A + j * FC
            qb_ref[:, j * FC:(j + 1) * FC] = _dot_nt(h, w_ref[lo:lo + FC, :]).astype(BF16)
        for j in range(2 * D_MODEL // FC):
            lo = QKV_A + QKV_B + j * FC
            gt_ref[:, j * FC:(j + 1) * FC] = _dot_nt(h, w_ref[lo:lo + FC, :])

    return _call(
        body,
        name="proj_fwd",
        grid=(t // TM_FWD,),
        in_specs=[_rows(TM_FWD, D_MODEL), _resident((1, D_MODEL)), _resident((IN_WIDTH, D_MODEL))],
        out_specs=[_rows(TM_FWD, D_MODEL), _rows(TM_FWD, QKV_A), _rows(TM_FWD, QKV_B), _rows(TM_FWD, 2 * D_MODEL)],
        out_shape=[jax.ShapeDtypeStruct((t, D_MODEL), BF16), jax.ShapeDtypeStruct((t, QKV_A), BF16),
                   jax.ShapeDtypeStruct((t, QKV_B), BF16), jax.ShapeDtypeStruct((t, 2 * D_MODEL), F32)],
        args=(x, gamma, win_t), sem=("parallel",), hosted=hosted)


def _proj_bwd(d, x, gamma, pieces, win_t, hosted=()):
    t = x.shape[0]
    npc = len(pieces)
    widths = [p.shape[1] for p in pieces]
    los = [sum(widths[:k]) for k in range(npc)]

    def body(*refs):
        d_ref, x_ref, gam_ref = refs[:3]
        p_refs = refs[3:3 + npc]
        w_ref, dx_ref, db_ref, dgam_ref = refs[3 + npc:]
        dh = _dot_nn(p_refs[0][...], w_ref[0:widths[0], :])
        for k in range(1, npc):
            dh += _dot_nn(p_refs[k][...], w_ref[los[k]:los[k] + widths[k], :])
        xh, r = _rms(x_ref[...])
        dxn, dgam = _rms_bwd(dh, xh, r, gam_ref[...])
        dx = d_ref[...] + dxn
        dx_ref[...] = dx
        db_ref[...] = (0.5 * dx).astype(BF16)

        @pl.when(pl.program_id(0) == 0)
        def _():
            dgam_ref[...] = jnp.zeros_like(dgam_ref)

        dgam_ref[...] += dgam

    return _call(
        body,
        name="proj_bwd",
        grid=(t // TM,),
        in_specs=[_rows(TM, D_MODEL), _rows(TM, D_MODEL), _resident((1, D_MODEL))] + [_rows(TM, w) for w in widths]
        + [_resident((IN_WIDTH, D_MODEL))],
        out_specs=[_rows(TM, D_MODEL), _rows(TM, D_MODEL), pl.BlockSpec((8, D_MODEL), lambda i: (0, 0))],
        out_shape=[jax.ShapeDtypeStruct((t, D_MODEL), F32), jax.ShapeDtypeStruct((t, D_MODEL), BF16),
                   jax.ShapeDtypeStruct((8, D_MODEL), F32)],
        args=(d, x, gamma, *pieces, win_t), sem=("arbitrary",), hosted=hosted)


def _lane_half(shape):
    return lax.broadcasted_iota(jnp.int32, shape, len(shape) - 1) // D_HEAD


def _band_weights(q, kk, bias, sink, qs, pad):
    s = _dot_nt(q, kk) + bias
    if qs is not None:
        col = lax.broadcasted_iota(jnp.int32, s.shape, 1)
        s = jnp.where(col + qs >= pad, s, NEG_INF)
    m = jnp.max(s, axis=-1, keepdims=True)
    if sink is not None:
        m = jnp.maximum(m, sink)
    return jnp.exp(s - m), m


def _weighted_values(p, vv_ones, sink, m):
    r = _dot_nn(p.astype(BF16), vv_ones)
    den = r[:, LANES:2 * LANES]
    if sink is not None:
        den = den + jnp.exp(sink - m)
    return r[:, 0:LANES] / den


def _band_softmax(q, kk, bias, sink, qs, pad):
    p, m = _band_weights(q, kk, bias, sink, qs, pad)
    den = jnp.sum(p, axis=-1, keepdims=True)
    if sink is not None:
        den = den + jnp.exp(sink - m)
    return p, m, 1.0 / den


def _fill_padded(dst, src, pad):
    dst[0:pad, :] = jnp.zeros((pad,) + dst.shape[1:], dst.dtype)
    dst[pad:, :] = src


FWD_PAIRS = 4
BWD_PAIRS = 2


def _attn_a_fwd(qkv, bias, hosted=()):
    bsz, s_len, _ = qkv.shape
    pad = A_PREV * CHUNK
    band = TQ + pad
    pp = FWD_PAIRS
    w = pp * LANES
    nb = A_WIDTH // w

    def body(q_ref, k_ref, v_ref, b_ref, o_ref, kp, vp):
        i = pl.program_id(2)

        @pl.when(i == 0)
        def _():
            _fill_padded(kp, k_ref[...], pad)
            _fill_padded(vp, v_ref[...], pad)

        qs = pl.multiple_of(i * TQ, TQ)
        half = _lane_half((1, LANES))

        ones = jnp.ones((band, LANES), BF16)

        def block(masked):
            for pr in range(pp):
                sl = slice(pr * LANES, (pr + 1) * LANES)
                kk = kp[pl.ds(qs, band), sl]
                vv = jnp.concatenate([vp[pl.ds(qs, band), sl], ones], axis=1)
                q = q_ref[:, sl] * SCALE
                outs = []
                for j in range(2):
                    qm = jnp.where(half == j, q, jnp.zeros_like(q))
                    p, m = _band_weights(qm, kk, b_ref[2 * pr + j], None, qs if masked else None, pad)
                    outs.append(_weighted_values(p, vv, None, m))
                o_ref[:, sl] = jnp.where(half == 0, outs[0], outs[1]).astype(BF16)

        pl.when(i < pad // TQ)(lambda: block(True))
        pl.when(i >= pad // TQ)(lambda: block(False))

    return _call(
        body,
        name="attn_a_fwd",
        grid=(bsz, nb, s_len // TQ),
        in_specs=[pl.BlockSpec((None, TQ, w), lambda b, g, i: (b, i, g)),
                  pl.BlockSpec((None, s_len, w), lambda b, g, i: (b, 0, nb + g)),
                  pl.BlockSpec((None, s_len, w), lambda b, g, i: (b, 0, 2 * nb + g)),
                  pl.BlockSpec((2 * pp, TQ, band), lambda b, g, i: (g, 0, 0))],
        out_specs=[pl.BlockSpec((None, TQ, w), lambda b, g, i: (b, i, g))],
        out_shape=[jax.ShapeDtypeStruct((bsz, s_len, A_WIDTH), BF16)],
        scratch_shapes=[pltpu.VMEM((pad + s_len, w), BF16), pltpu.VMEM((pad + s_len, w), BF16)],
        args=(qkv, qkv, qkv, bias), sem=("arbitrary", "arbitrary", "arbitrary"), hosted=hosted)[0]


def _attn_a_bwd(qkv, bias, do, hosted=()):
    bsz, s_len, _ = qkv.shape
    pad = A_PREV * CHUNK
    band = TQ + pad
    n_i = s_len // TQ
    pp = BWD_PAIRS
    w = pp * LANES
    nb = A_WIDTH // w

    def body(q_ref, k_ref, v_ref, b_ref, do_ref, dq_ref, dk_ref, dv_ref, dbias_ref, kp, vp, dk_acc, dv_acc):
        b = pl.program_id(1)
        i = pl.program_id(2)

        @pl.when(i == 0)
        def _():
            _fill_padded(kp, k_ref[...], pad)
            _fill_padded(vp, v_ref[...], pad)
            dk_acc[...] = jnp.zeros_like(dk_acc)
            dv_acc[...] = jnp.zeros_like(dv_acc)

        @pl.when(jnp.logical_and(b == 0, i == 0))
        def _():
            dbias_ref[...] = jnp.zeros_like(dbias_ref)

        qs = pl.multiple_of(i * TQ, TQ)
        half = _lane_half((1, LANES))

        def block(masked):
            for pr in range(pp):
                sl = slice(pr * LANES, (pr + 1) * LANES)
                kk = kp[pl.ds(qs, band), sl]
                vv = vp[pl.ds(qs, band), sl]
                q = q_ref[:, sl] * SCALE
                dd = do_ref[:, sl]
                dqs, dks, dvs = [], [], []
                for j in range(2):
                    qm = jnp.where(half == j, q, jnp.zeros_like(q))
                    dm = jnp.where(half == j, dd, jnp.zeros_like(dd))
                    p, _, inv = _band_softmax(qm, kk, b_ref[2 * pr + j], None, qs if masked else None, pad)
                    pn = p * inv
                    dp = _dot_nt(dm, vv)
                    delta = jnp.sum(pn * dp, axis=-1, keepdims=True)
                    ds = pn * (dp - delta)
                    dbias_ref[2 * pr + j] += ds[:, band - REL_COLS:]
                    dsb = ds.astype(BF16)
                    dqs.append(_dot_nn(dsb, kk))
                    dks.append(_dot_tn(dsb, q))
                    dvs.append(_dot_tn(pn.astype(BF16), dd))
                dq_ref[:, sl] = (jnp.where(half == 0, dqs[0], dqs[1]) * SCALE).astype(BF16)
                dk_acc[pl.ds(qs, band), sl] += jnp.where(half == 0, dks[0], dks[1])
                dv_acc[pl.ds(qs, band), sl] += jnp.where(half == 0, dvs[0], dvs[1])

        pl.when(i < pad // TQ)(lambda: block(True))
        pl.when(i >= pad // TQ)(lambda: block(False))

        @pl.when(i == n_i - 1)
        def _():
            dk_ref[...] = dk_acc[pad:, :].astype(BF16)
            dv_ref[...] = dv_acc[pad:, :].astype(BF16)

    qspec = pl.BlockSpec((None, TQ, w), lambda g, b, i: (b, i, g))
    kvout = pl.BlockSpec((None, s_len, w), lambda g, b, i: (b, 0, g))
    wide = jax.ShapeDtypeStruct((bsz, s_len, A_WIDTH), BF16)
    return _call(
        body,
        name="attn_a_bwd",
        grid=(nb, bsz, n_i),
        in_specs=[qspec,
                  pl.BlockSpec((None, s_len, w), lambda g, b, i: (b, 0, nb + g)),
                  pl.BlockSpec((None, s_len, w), lambda g, b, i: (b, 0, 2 * nb + g)),
                  pl.BlockSpec((2 * pp, TQ, band), lambda g, b, i: (g, 0, 0)),
                  qspec],
        out_specs=[qspec, kvout, kvout, pl.BlockSpec((2 * pp, TQ, REL_COLS), lambda g, b, i: (g, 0, 0))],
        out_shape=[wide, wide, wide, jax.ShapeDtypeStruct((A_HEADS, TQ, REL_COLS), F32)],
        scratch_shapes=[pltpu.VMEM((pad + s_len, w), BF16), pltpu.VMEM((pad + s_len, w), BF16),
                        pltpu.VMEM((pad + s_len, w), F32), pltpu.VMEM((pad + s_len, w), F32)],
        args=(qkv, qkv, qkv, bias, do), sem=("arbitrary", "arbitrary", "arbitrary"), hosted=hosted)


def _fill_padded_dup(dst, src, pad, h, half):
    other = pltpu.roll(src, D_HEAD, 1)
    _fill_padded(dst, jnp.where(half == h, src, other), pad)


def _attn_b_fwd(qkv, bias, sink):
    bsz, s_len, _ = qkv.shape
    pad = B_PREV * CHUNK
    band = TQ + pad
    kcol = B_Q_WIDTH // LANES
    npair = B_Q_HEADS // 2

    def body(q_ref, k_ref, v_ref, b_ref, s_ref, o_ref, kp, vp):
        i = pl.program_id(1)
        half = _lane_half((1, LANES))

        @pl.when(i == 0)
        def _():
            for h in range(B_KV_HEADS):
                _fill_padded_dup(kp.at[h], k_ref[...], pad, h, half)
                _fill_padded_dup(vp.at[h], v_ref[...], pad, h, half)

        qs = pl.multiple_of(i * TQ, TQ)

        ones = jnp.ones((band, LANES), BF16)

        def block(masked):
            for pr in range(npair):
                h = pr // (B_GROUP // 2)
                sl = slice(pr * LANES, (pr + 1) * LANES)
                kk = kp[h, pl.ds(qs, band), :]
                vv = jnp.concatenate([vp[h, pl.ds(qs, band), :], ones], axis=1)
                q = q_ref[:, sl] * SCALE
                outs = []
                for j in range(2):
                    qm = jnp.where(half == j, q, jnp.zeros_like(q))
                    sink = s_ref[2 * pr + j][0:1, 0:1]
                    p, m = _band_weights(qm, kk, b_ref[2 * pr + j], sink, qs if masked else None, pad)
                    outs.append(_weighted_values(p, vv, sink, m))
                o_ref[:, sl] = jnp.where(half == 0, outs[0], outs[1]).astype(BF16)

        pl.when(i < -(-pad // TQ))(lambda: block(True))
        pl.when(i >= -(-pad // TQ))(lambda: block(False))

    return pl.pallas_call(
        body,
        name="attn_b_fwd",
        grid=(bsz, s_len // TQ),
        in_specs=[pl.BlockSpec((None, TQ, B_Q_WIDTH), lambda b, i: (b, i, 0)),
                  pl.BlockSpec((None, s_len, LANES), lambda b, i: (b, 0, kcol)),
                  pl.BlockSpec((None, s_len, LANES), lambda b, i: (b, 0, kcol + 1)),
                  pl.BlockSpec((B_Q_HEADS, TQ, band), lambda b, i: (0, 0, 0)),
                  pl.BlockSpec((B_Q_HEADS, 8, LANES), lambda b, i: (0, 0, 0))],
        out_specs=pl.BlockSpec((None, TQ, B_Q_WIDTH), lambda b, i: (b, i, 0)),
        out_shape=jax.ShapeDtypeStruct((bsz, s_len, B_Q_WIDTH), BF16),
        scratch_shapes=[pltpu.VMEM((B_KV_HEADS, pad + s_len, LANES), BF16),
                        pltpu.VMEM((B_KV_HEADS, pad + s_len, LANES), BF16)],
        compiler_params=_cparams(("arbitrary", "arbitrary")),
    )(qkv, qkv, qkv, bias, sink)


def _attn_b_bwd(qkv, bias, sink, do, hosted=()):
    bsz, s_len, _ = qkv.shape
    pad = B_PREV * CHUNK
    band = TQ + pad
    kcol = B_Q_WIDTH // LANES
    npair = B_Q_HEADS // 2
    n_i = s_len // TQ

    pp = B_GROUP // 2
    w = pp * LANES

    def body(q_ref, k_ref, v_ref, b_ref, s_ref, do_ref, dq_ref, dkv_ref, dsink_ref, kp, vp, dk_acc, dv_acc):
        b = pl.program_id(0)
        h = pl.program_id(1)
        i = pl.program_id(2)
        half = _lane_half((1, LANES))

        @pl.when(i == 0)
        def _():
            _fill_padded_dup(kp, k_ref[...], pad, h, half)
            _fill_padded_dup(vp, v_ref[...], pad, h, half)

        @pl.when(jnp.logical_and(h == 0, i == 0))
        def _():
            dk_acc[...] = jnp.zeros_like(dk_acc)
            dv_acc[...] = jnp.zeros_like(dv_acc)

        @pl.when(jnp.logical_and(b == 0, jnp.logical_and(h == 0, i == 0)))
        def _():
            dsink_ref[...] = jnp.zeros_like(dsink_ref)

        qs = pl.multiple_of(i * TQ, TQ)

        def block(masked):
            kk = kp[pl.ds(qs, band), :]
            vv = vp[pl.ds(qs, band), :]
            dk2 = jnp.zeros((band, LANES), F32)
            dv2 = jnp.zeros((band, LANES), F32)
            for pr in range(pp):
                sl = slice(pr * LANES, (pr + 1) * LANES)
                q = q_ref[:, sl] * SCALE
                dd = do_ref[:, sl]
                dqs, dks, dvs = [], [], []
                for j in range(2):
                    qm = jnp.where(half == j, q, jnp.zeros_like(q))
                    dm = jnp.where(half == j, dd, jnp.zeros_like(dd))
                    sink = s_ref[2 * pr + j][0:1, 0:1]
                    p, m, inv = _band_softmax(qm, kk, b_ref[2 * pr + j], sink, qs if masked else None, pad)
                    pn = p * inv
                    dp = _dot_nt(dm, vv)
                    delta = jnp.sum(pn * dp, axis=-1, keepdims=True)
                    ds = pn * (dp - delta)
                    dsb = ds.astype(BF16)
                    dqs.append(_dot_nn(dsb, kk))
                    dks.append(_dot_tn(dsb, q))
                    dvs.append(_dot_tn(pn.astype(BF16), dd))
                    dsk = jnp.sum(-(jnp.exp(sink - m) * inv) * delta, axis=0, keepdims=True)
                    dsink_ref[2 * pp * h + 2 * pr + j] += jnp.broadcast_to(dsk, (8, LANES))
                dq_ref[:, sl] = (jnp.where(half == 0, dqs[0], dqs[1]) * SCALE).astype(BF16)
                dk2 = dk2 + jnp.where(half == 0, dks[0], dks[1])
                dv2 = dv2 + jnp.where(half == 0, dvs[0], dvs[1])
            dk_acc[pl.ds(qs, band), :] += jnp.where(half == h, dk2 + pltpu.roll(dk2, D_HEAD, 1), 0.0)
            dv_acc[pl.ds(qs, band), :] += jnp.where(half == h, dv2 + pltpu.roll(dv2, D_HEAD, 1), 0.0)

        pl.when(i < -(-pad // TQ))(lambda: block(True))
        pl.when(i >= -(-pad // TQ))(lambda: block(False))

        @pl.when(jnp.logical_and(h == B_KV_HEADS - 1, i == n_i - 1))
        def _():
            dkv_ref[:, 0:LANES] = dk_acc[pad:, :].astype(BF16)
            dkv_ref[:, LANES:2 * LANES] = dv_acc[pad:, :].astype(BF16)

    qspec = pl.BlockSpec((None, TQ, w), lambda b, h, i: (b, i, h))
    return _call(
        body,
        name="attn_b_bwd",
        grid=(bsz, B_KV_HEADS, n_i),
        in_specs=[qspec,
                  pl.BlockSpec((None, s_len, LANES), lambda b, h, i: (b, 0, kcol)),
                  pl.BlockSpec((None, s_len, LANES), lambda b, h, i: (b, 0, kcol + 1)),
                  pl.BlockSpec((2 * pp, TQ, band), lambda b, h, i: (h, 0, 0)),
                  pl.BlockSpec((2 * pp, 8, LANES), lambda b, h, i: (h, 0, 0)),
                  qspec],
        out_specs=[qspec, pl.BlockSpec((None, s_len, 2 * LANES), lambda b, h, i: (b, 0, 0)),
                   pl.BlockSpec((B_Q_HEADS, 8, LANES), lambda b, h, i: (0, 0, 0))],
        out_shape=[jax.ShapeDtypeStruct((bsz, s_len, B_Q_WIDTH), BF16),
                   jax.ShapeDtypeStruct((bsz, s_len, 2 * B_KV_WIDTH), BF16),
                   jax.ShapeDtypeStruct((B_Q_HEADS, 8, LANES), F32)],
        scratch_shapes=[pltpu.VMEM((pad + s_len, LANES), BF16), pltpu.VMEM((pad + s_len, LANES), BF16),
                        pltpu.VMEM((pad + s_len, LANES), F32), pltpu.VMEM((pad + s_len, LANES), F32)],
        args=(qkv, qkv, qkv, bias, sink, do), sem=("arbitrary", "arbitrary", "arbitrary"), hosted=hosted)


REL_COLS = 3 * 128
REL_WRAP = 512


def _bias_a_build(tv):
    h = tv.shape[0]
    pad = A_PREV * CHUNK
    band = TQ + pad

    def body(tv_ref, o_ref):
        row = tv_ref[...]
        x = jnp.broadcast_to(row, (TQ, REL_WRAP))
        r = lax.broadcasted_iota(jnp.int32, x.shape, 0)
        for bit in range(8):
            sh = 1 << bit
            x = jnp.where((r & sh) != 0, pltpu.roll(x, sh, 1), x)
        far = jnp.broadcast_to(row[:, 0:1], (TQ, band - REL_COLS))
        full = jnp.concatenate([far, x[:, REL_WRAP // 2:REL_WRAP], x[:, 0:REL_COLS - REL_WRAP // 2]], axis=1)
        qc = (lax.broadcasted_iota(jnp.int32, full.shape, 0) + pad) // CHUNK
        kc = lax.broadcasted_iota(jnp.int32, full.shape, 1) // CHUNK
        ok = jnp.logical_and(kc <= qc, kc >= qc - A_PREV)
        o_ref[...] = jnp.where(ok, full, NEG_INF)

    return pl.pallas_call(
        body,
        name="bias_a_build",
        grid=(h,),
        in_specs=[pl.BlockSpec((None, 1, REL_WRAP), lambda hh: (hh, 0, 0))],
        out_specs=pl.BlockSpec((None, TQ, band), lambda hh: (hh, 0, 0)),
        out_shape=jax.ShapeDtypeStruct((h, TQ, band), F32),
        compiler_params=_cparams(("parallel",)),
    )(tv)


def _relbias_grad(dbias):
    h, rows, _ = dbias.shape

    def body(d_ref, o_ref):
        x = d_ref[...]
        r = lax.broadcasted_iota(jnp.int32, x.shape, 0)
        c = lax.broadcasted_iota(jnp.int32, x.shape, 1) - r
        x = jnp.where(jnp.logical_and(c >= 1, c < REL_TABLE), x, 0.0)
        for bit in range(8):
            sh = 1 << bit
            x = jnp.where((r & sh) != 0, pltpu.roll(x, REL_COLS - sh, 1), x)
        diag = jnp.sum(x, axis=0, keepdims=True)
        lane = lax.broadcasted_iota(jnp.int32, diag.shape, 1)
        diag = jnp.where(jnp.logical_and(lane >= 1, lane < REL_TABLE), diag, 0.0)
        rest = -jnp.sum(diag, axis=1, keepdims=True)
        o_ref[...] = jnp.broadcast_to(jnp.where(lane == 0, rest, diag), o_ref.shape)

    return pl.pallas_call(
        body,
        name="relbias_grad",
        grid=(h,),
        in_specs=[pl.BlockSpec((None, rows, REL_COLS), lambda hh: (hh, 0, 0))],
        out_specs=pl.BlockSpec((None, 8, REL_COLS), lambda hh: (hh, 0, 0)),
        out_shape=jax.ShapeDtypeStruct((h, 8, REL_COLS), F32),
        compiler_params=_cparams(("parallel",)),
    )(dbias)


def _mix_out_fwd(x, oa, ob, gates, proj_t, wout):
    t = x.shape[0]

    def body(x_ref, oa_ref, ob_ref, gt_ref, pt_ref, wo_ref, y_ref, ya_ref, yb_ref, mg_ref):
        ya = _dot_nt(oa_ref[...], pt_ref[:, 0:A_WIDTH])
        yb = _dot_nt(ob_ref[...], pt_ref[:, A_WIDTH:A_WIDTH + B_Q_WIDTH])
        ya_ref[...] = ya.astype(BF16)
        yb_ref[...] = yb.astype(BF16)
        mg = jax.nn.sigmoid(gt_ref[:, 0:D_MODEL]) * ya + jax.nn.sigmoid(gt_ref[:, D_MODEL:2 * D_MODEL]) * yb
        mgb = mg.astype(BF16)
        mg_ref[...] = mgb
        y_ref[...] = x_ref[...] + _dot_nn(mgb, wo_ref[...])

    return pl.pallas_call(
        body,
        name="mix_out_fwd",
        grid=(t // TM,),
        in_specs=[_rows(TM, D_MODEL), _rows(TM, A_WIDTH), _rows(TM, B_Q_WIDTH), _rows(TM, 2 * D_MODEL),
                  _resident((D_MODEL, A_WIDTH + B_Q_WIDTH)), _resident((D_MODEL, D_MODEL))],
        out_specs=[_rows(TM, D_MODEL), _rows(TM, D_MODEL), _rows(TM, D_MODEL), _rows(TM, D_MODEL)],
        out_shape=[jax.ShapeDtypeStruct((t, D_MODEL), F32), jax.ShapeDtypeStruct((t, D_MODEL), BF16),
                   jax.ShapeDtypeStruct((t, D_MODEL), BF16), jax.ShapeDtypeStruct((t, D_MODEL), BF16)],
        compiler_params=_cparams(("parallel",)),
    )(x, oa, ob, gates, proj_t, wout)


def _mix_out_bwd(d, gates, ya, yb, proj_t, wout, hosted=()):
    t = d.shape[0]

    def body(d_ref, gt_ref, ya_ref, yb_ref, pt_ref, wo_ref, db_ref, dya_ref, dyb_ref, doa_ref, dob_ref, dgt_ref):
        db = d_ref[...].astype(BF16)
        db_ref[...] = db
        dmg = _dot_nt(db, wo_ref[...])
        sa = jax.nn.sigmoid(gt_ref[:, 0:D_MODEL])
        sb = jax.nn.sigmoid(gt_ref[:, D_MODEL:2 * D_MODEL])
        dya = (dmg * sa).astype(BF16)
        dyb = (dmg * sb).astype(BF16)
        dya_ref[...] = dya
        dyb_ref[...] = dyb
        dgt_ref[:, 0:D_MODEL] = (dmg * ya_ref[...].astype(F32) * (sa * (1.0 - sa))).astype(BF16)
        dgt_ref[:, D_MODEL:2 * D_MODEL] = (dmg * yb_ref[...].astype(F32) * (sb * (1.0 - sb))).astype(BF16)
        doa_ref[...] = _dot_nn(dya, pt_ref[:, 0:A_WIDTH]).astype(BF16)
        dob_ref[...] = _dot_nn(dyb, pt_ref[:, A_WIDTH:A_WIDTH + B_Q_WIDTH]).astype(BF16)

    return _call(
        body,
        name="mix_out_bwd",
        grid=(t // TM,),
        in_specs=[_rows(TM, D_MODEL), _rows(TM, 2 * D_MODEL), _rows(TM, D_MODEL), _rows(TM, D_MODEL),
                  _resident((D_MODEL, A_WIDTH + B_Q_WIDTH)), _resident((D_MODEL, D_MODEL))],
        out_specs=[_rows(TM, D_MODEL), _rows(TM, D_MODEL), _rows(TM, D_MODEL), _rows(TM, A_WIDTH),
                   _rows(TM, B_Q_WIDTH), _rows(TM, 2 * D_MODEL)],
        out_shape=[jax.ShapeDtypeStruct((t, D_MODEL), BF16), jax.ShapeDtypeStruct((t, D_MODEL), BF16),
                   jax.ShapeDtypeStruct((t, D_MODEL), BF16), jax.ShapeDtypeStruct((t, A_WIDTH), BF16),
                   jax.ShapeDtypeStruct((t, B_Q_WIDTH), BF16), jax.ShapeDtypeStruct((t, 2 * D_MODEL), BF16)],
        args=(d, gates, ya, yb, proj_t, wout), sem=("parallel",), hosted=hosted)


def _loss_head(x, gamma, target):
    t = x.shape[0]

    def body(x_ref, gam_ref, t_ref, dx_ref, dgam_ref, loss_ref):
        xh, r = _rms(x_ref[...])
        gam = gam_ref[...]
        e = xh * gam - t_ref[...]
        dy = e * (1.0 / D_MODEL)
        dxn, dgam = _rms_bwd(dy, xh, r, gam)
        dx_ref[...] = dxn

        @pl.when(pl.program_id(0) == 0)
        def _():
            dgam_ref[...] = jnp.zeros_like(dgam_ref)
            loss_ref[...] = jnp.zeros_like(loss_ref)

        dgam_ref[...] += dgam
        loss_ref[...] += _colsum8(e * e) * (0.5 / D_MODEL)

    return pl.pallas_call(
        body,
        name="loss_head",
        grid=(t // TM,),
        in_specs=[_rows(TM, D_MODEL), _resident((1, D_MODEL)), _rows(TM, D_MODEL)],
        out_specs=[_rows(TM, D_MODEL), pl.BlockSpec((8, D_MODEL), lambda i: (0, 0)),
                   pl.BlockSpec((8, D_MODEL), lambda i: (0, 0))],
        out_shape=[jax.ShapeDtypeStruct((t, D_MODEL), F32), jax.ShapeDtypeStruct((8, D_MODEL), F32),
                   jax.ShapeDtypeStruct((8, D_MODEL), F32)],
        compiler_params=_cparams(("arbitrary",)),
    )(x, gamma, target)


def _place():
    x, y, c = lax.axis_index("x"), lax.axis_index("y"), lax.axis_index("c")
    chips = [(1 - x, y), (x, 1 - y), (1 - x, 1 - y)]
    return x, y, c, chips


class _Gather:
    per = 8

    def __init__(self, shards):
        n = len(shards)
        self.inputs = list(shards)
        self.out_shape = [jax.ShapeDtypeStruct((N_DEV * s.shape[0], s.shape[1]), s.dtype) for s in shards]
        self.scratch = [pltpu.SemaphoreType.DMA((n * self.per,)), pltpu.SemaphoreType.DMA((n * self.per,)),
                        pltpu.SemaphoreType.DMA((n,))]
        self.result = None

    def _parts(self, ins, outs, sems):
        send_sems, recv_sems, local_sems = sems
        x, y, c, chips = _place()
        me, sibling = (x, y, c), (x, y, 1 - c)
        xn, yn, dg = chips
        n = len(ins)

        def rows(k, p, part=None):
            r = ins[k].shape[0]
            base = (4 * p[0] + 2 * p[1] + p[2]) * r
            if part is None:
                return outs[k].at[pl.ds(base, r), :]
            return outs[k].at[pl.ds(base + part * (r // 2), r // 2), :]

        def copy(k, slot, block, to, src=None, part=None):
            return pltpu.make_async_remote_copy(
                src_ref=rows(k, block, part) if src is None else src, dst_ref=rows(k, block, part),
                send_sem=send_sems.at[k * self.per + slot], recv_sem=recv_sems.at[k * self.per + slot],
                device_id=to, device_id_type=MESH)

        mine = [pltpu.make_async_copy(ins[k], rows(k, me), local_sems.at[k]) for k in range(n)]
        sends, lands = [], []
        for k in range(n):
            sends.append({
                0: copy(k, 0, me, sibling, src=ins[k]),
                1: copy(k, 1, me, (*xn, c), src=ins[k]),
                2: copy(k, 2, me, (*yn, c), src=ins[k]),
                3: copy(k, 3, (*xn, c), (*yn, c), part=0),
                4: copy(k, 4, (*yn, c), (*xn, c), part=1),
                5: copy(k, 5, (*xn, c), sibling),
                6: copy(k, 6, (*yn, c), sibling),
                7: copy(k, 7, (*dg, c), sibling)})
            lands.append({
                0: copy(k, 0, sibling, me),
                1: copy(k, 1, (*xn, c), me),
                2: copy(k, 2, (*yn, c), me),
                3: copy(k, 3, (*dg, c), me, part=0),
                4: copy(k, 4, (*dg, c), me, part=1),
                5: copy(k, 5, (*xn, 1 - c), me),
                6: copy(k, 6, (*yn, 1 - c), me),
                7: copy(k, 7, (*dg, 1 - c), me)})
        return n, mine, sends, lands

    def start(self, ins, outs, sems):
        n, mine, sends, _ = self._parts(ins, outs, sems)
        for cp in mine:
            cp.start()
        for slot in (0, 1, 2):
            for k in range(n):
                sends[k][slot].start()

    def relay(self, ins, outs, sems):
        n, _, sends, lands = self._parts(ins, outs, sems)
        for k in range(n):
            lands[k][1].wait_recv()
            sends[k][3].start()
            sends[k][5].start()
        for k in range(n):
            lands[k][2].wait_recv()
            sends[k][4].start()
            sends[k][6].start()

    def forward(self, ins, outs, sems):
        n, _, sends, lands = self._parts(ins, outs, sems)
        for k in range(n):
            lands[k][3].wait_recv()
            lands[k][4].wait_recv()
            sends[k][7].start()

    def finish(self, ins, outs, sems):
        n, mine, sends, lands = self._parts(ins, outs, sems)
        for k in range(n):
            for slot in (0, 5, 6, 7):
                lands[k][slot].wait_recv()
        for k in range(n):
            for slot in range(self.per):
                sends[k][slot].wait_send()
        for cp in mine:
            cp.wait()


class _PairExchange:
    def __init__(self, grads):
        n = len(grads)
        self.inputs = list(grads)
        self.out_shape = [jax.ShapeDtypeStruct((g.shape[0] // 2, g.shape[1]), g.dtype) for g in grads]
        self.scratch = [pltpu.SemaphoreType.DMA((n * N_CHIP,)), pltpu.SemaphoreType.DMA((n * N_CHIP,))]
        self.result = None

    def _copies(self, ins, outs, sems):
        send_sems, recv_sems = sems
        x, y, c, _ = _place()
        copies = []
        for k in range(len(ins)):
            r = ins[k].shape[0] // N_DEV
            for q in range(N_CHIP):
                copies.append(pltpu.make_async_remote_copy(
                    src_ref=ins[k].at[pl.ds((2 * q + 1 - c) * r, r), :], dst_ref=outs[k].at[pl.ds(q * r, r), :],
                    send_sem=send_sems.at[k * N_CHIP + q], recv_sem=recv_sems.at[k * N_CHIP + q],
                    device_id=(x, y, 1 - c), device_id_type=MESH))
        return copies

    def start(self, ins, outs, sems):
        for cp in self._copies(ins, outs, sems):
            cp.start()

    def relay(self, ins, outs, sems):
        pass

    def forward(self, ins, outs, sems):
        pass

    def finish(self, ins, outs, sems):
        copies = self._copies(ins, outs, sems)
        for cp in copies:
            cp.wait_recv()
        for cp in copies:
            cp.wait_send()


class _ChipExchange(_PairExchange):
    def __init__(self, psums):
        n = len(psums)
        self.inputs = list(psums)
        self.out_shape = [jax.ShapeDtypeStruct((3 * p.shape[0] // N_CHIP, p.shape[1]), p.dtype) for p in psums]
        self.scratch = [pltpu.SemaphoreType.DMA((n * 3,)), pltpu.SemaphoreType.DMA((n * 3,))]
        self.result = None

    def _copies(self, ins, outs, sems):
        send_sems, recv_sems = sems
        _, _, c, chips = _place()
        copies = []
        for k in range(len(ins)):
            r = ins[k].shape[0] // N_CHIP
            for j, chip in enumerate(chips):
                copies.append(pltpu.make_async_remote_copy(
                    src_ref=ins[k].at[pl.ds((2 * chip[0] + chip[1]) * r, r), :], dst_ref=outs[k].at[pl.ds(j * r, r), :],
                    send_sem=send_sems.at[k * 3 + j], recv_sem=recv_sems.at[k * 3 + j],
                    device_id=(*chip, c), device_id_type=MESH))
        return copies


def _exchange_alone(xchg, name):
    n_in, n_out = len(xchg.inputs), len(xchg.out_shape)

    def body(*refs):
        ins, outs, sems = refs[:n_in], refs[n_in:n_in + n_out], refs[n_in + n_out:]
        xchg.start(ins, outs, sems)
        xchg.relay(ins, outs, sems)
        xchg.forward(ins, outs, sems)
        xchg.finish(ins, outs, sems)

    xchg.result = list(pl.pallas_call(
        body, name=name, in_specs=[_hbm()] * n_in, out_specs=[_hbm()] * n_out, out_shape=xchg.out_shape,
        scratch_shapes=xchg.scratch)(*xchg.inputs))
    return xchg.result


def _pair_sum(core, grads, recvd, name):
    n = len(grads)
    r = grads[0].shape[0] // N_DEV
    cdim = grads[0].shape[1]
    tr = r // 2 if r % 32 == 0 else r
    nt = r // tr

    def body(core_ref, *refs):
        del core_ref
        for k in range(n):
            refs[2 * n + k][...] = (refs[k][...].astype(F32) + refs[n + k][...].astype(F32)).astype(BF16)

    gspec = pl.BlockSpec((tr, cdim), lambda q, i, core_ref: ((2 * q + core_ref[0]) * nt + i, 0))
    rspec = pl.BlockSpec((tr, cdim), lambda q, i, core_ref: (q * nt + i, 0))
    return pl.pallas_call(
        body,
        name=name,
        grid_spec=pltpu.PrefetchScalarGridSpec(
            num_scalar_prefetch=1, grid=(N_CHIP, nt), in_specs=[gspec] * n + [rspec] * n, out_specs=[rspec] * n),
        out_shape=[jax.ShapeDtypeStruct((N_CHIP * r, cdim), BF16) for _ in range(n)],
        compiler_params=_cparams(("parallel", "parallel")),
    )(core, *grads, *recvd)


def _final_sum(chip, psums, recvd, name):
    n = len(psums)
    r = psums[0].shape[0] // N_CHIP
    cdim = psums[0].shape[1]
    tr = r // 2 if r % 32 == 0 else r
    nt = r // tr

    def body(chip_ref, *refs):
        del chip_ref
        for k in range(n):
            got = refs[n + k]
            tot = refs[k][...].astype(F32) + got[0].astype(F32)
            tot = tot + got[1].astype(F32)
            tot = tot + got[2].astype(F32)
            refs[2 * n + k][...] = tot

    pspec = pl.BlockSpec((tr, cdim), lambda i, chip_ref: (chip_ref[0] * nt + i, 0))
    rspec = pl.BlockSpec((3, tr, cdim), lambda i, chip_ref: (0, i, 0))
    ospec = pl.BlockSpec((tr, cdim), lambda i, chip_ref: (i, 0))
    return pl.pallas_call(
        body,
        name=name,
        grid_spec=pltpu.PrefetchScalarGridSpec(
            num_scalar_prefetch=1, grid=(nt,), in_specs=[pspec] * n + [rspec] * n, out_specs=[ospec] * n),
        out_shape=[jax.ShapeDtypeStruct((r, cdim), F32) for _ in range(n)],
        compiler_params=_cparams(("parallel",)),
    )(chip, *psums, *[g.reshape(3, r, cdim) for g in recvd])


SMALL_ROWS = 16


def _all_reduce_small(part):
    def body(p_ref, o_ref, buf, send_sems, recv_sems):
        x, y, c, _ = _place()
        me = 4 * x + 2 * y + c
        buf[me] = p_ref[...]
        copies = []
        for d in range(1, N_DEV):
            peer = me ^ d
            copies.append(pltpu.make_async_remote_copy(
                src_ref=p_ref, dst_ref=buf.at[me], send_sem=send_sems.at[d - 1], recv_sem=recv_sems.at[d - 1],
                device_id=(peer // 4, (peer // 2) % 2, peer % 2), device_id_type=MESH))
        for cp in copies:
            cp.start()
        for cp in copies:
            cp.wait_recv()
        for cp in copies:
            cp.wait_send()
        tot = buf[0]
        for d in range(1, N_DEV):
            tot = tot + buf[d]
        o_ref[...] = tot

    return pl.pallas_call(
        body,
        name="all_reduce_small",
        in_specs=[pl.BlockSpec(memory_space=pltpu.VMEM)],
        out_specs=pl.BlockSpec(memory_space=pltpu.VMEM),
        out_shape=jax.ShapeDtypeStruct(part.shape, F32),
        scratch_shapes=[pltpu.VMEM((N_DEV,) + part.shape, F32), pltpu.SemaphoreType.DMA((N_DEV - 1,)),
                        pltpu.SemaphoreType.DMA((N_DEV - 1,))],
    )(part)


ADAMW_STEPS = 4


def _adamw(ws, gs, ms, vs, name, hosted=()):
    n = len(ws)
    steps = ADAMW_STEPS if all(w.shape[0] % (8 * ADAMW_STEPS) == 0 for w in ws) else 1
    c1 = 1.0 - ADAM_B1 ** ADAM_STEP
    c2 = 1.0 - ADAM_B2 ** ADAM_STEP

    def body(*refs):
        for k in range(n):
            w, g, m, v = (refs[j * n + k][...] for j in range(4))
            m2 = ADAM_B1 * m + (1.0 - ADAM_B1) * g
            v2 = ADAM_B2 * v + (1.0 - ADAM_B2) * (g * g)
            delta = -ADAM_LR * ((m2 / c1) / (jnp.sqrt(v2 / c2) + ADAM_EPS) + ADAM_WD * w)
            refs[4 * n + k][...] = delta
            refs[5 * n + k][...] = m2
            refs[6 * n + k][...] = v2

    specs = [pl.BlockSpec((w.shape[0] // steps, w.shape[1]), lambda i: (i, 0)) for w in ws]
    shapes = [jax.ShapeDtypeStruct(w.shape, F32) for w in ws]
    outs = _call(
        body,
        name=name,
        grid=(steps,),
        in_specs=specs * 4,
        out_specs=specs * 3,
        out_shape=shapes * 3,
        args=(*ws, *gs, *ms, *vs), sem=("parallel",), hosted=hosted)
    return outs[:n], outs[n:2 * n], outs[2 * n:]


def _bias_b():
    pad = B_PREV * CHUNK
    slopes = np.array([2.0 ** (-8.0 * (i + 1) / B_Q_HEADS) for i in range(B_Q_HEADS)], dtype=np.float32)
    dist = np.abs(np.arange(TQ)[:, None] - np.arange(TQ + pad)[None, :] + pad).astype(np.float32)
    bias = -slopes.reshape(B_Q_HEADS, 1, 1) * dist[None]
    qc = (np.arange(TQ)[:, None] + pad) // CHUNK
    kc = np.arange(TQ + pad)[None, :] // CHUNK
    allowed = (kc <= qc) & (kc >= qc - B_PREV)
    return np.where(allowed[None], bias, np.float32(NEG_INF)).astype(np.float32)


def kernel(x, ffn1_norm, ffn1_w_gate, ffn1_w_up, ffn1_w_down, mix_norm, w_in, rel_bias, sinks, w_proj_a, w_proj_b, w_out, ffn2_norm, ffn2_w_gate, ffn2_w_up, ffn2_w_down, final_norm, loss_target, m_ffn1_norm, m_ffn1_w_gate, m_ffn1_w_up, m_ffn1_w_down, m_mix_norm, m_w_in, m_rel_bias, m_sinks, m_w_proj_a, m_w_proj_b, m_w_out, m_ffn2_norm, m_ffn2_w_gate, m_ffn2_w_up, m_ffn2_w_down, m_final_norm, v_ffn1_norm, v_ffn1_w_gate, v_ffn1_w_up, v_ffn1_w_down, v_mix_norm, v_w_in, v_rel_bias, v_sinks, v_w_proj_a, v_w_proj_b, v_w_out, v_ffn2_norm, v_ffn2_w_gate, v_ffn2_w_up, v_ffn2_w_down, v_final_norm):
    bsz, s_len, _ = x.shape
    t = bsz * s_len
    core = lax.axis_index("c").astype(jnp.int32).reshape(1)
    chip = (2 * lax.axis_index("x") + lax.axis_index("y")).astype(jnp.int32).reshape(1)

    def row_form(w):
        return w.astype(BF16).T

    wg1, wu1 = _exchange_alone(_Gather([row_form(ffn1_w_gate), row_form(ffn1_w_up)]), "gather_ffn1")
    gather_down1 = _Gather([ffn1_w_down.astype(BF16), row_form(w_in)])
    gather_out = _Gather([jnp.concatenate([row_form(w_proj_a), row_form(w_proj_b)], axis=1), w_out.astype(BF16)])
    gather_ffn2_gate = _Gather([row_form(ffn2_w_gate)])
    gather_ffn2_rest = _Gather([row_form(ffn2_w_up), ffn2_w_down.astype(BF16)])

    x0 = x.reshape(t, D_MODEL)
    tgt = loss_target.reshape(t, D_MODEL)
    gam1, gam2, gam3, gam4 = (g.reshape(1, D_MODEL) for g in (ffn1_norm, mix_norm, ffn2_norm, final_norm))

    h1, g1, u1, a1 = _ffn_up(x0, gam1, wg1, wu1, "ffn1_up", hosted=[gather_down1])
    wd1, win_t = gather_down1.result
    x1 = _ffn_down(x0, a1, wd1, "ffn1_down", hosted=[gather_out])
    proj_t, wout = gather_out.result
    h2, qkv_a, qkv_b, gates = _proj_fwd(x1, gam2, win_t, hosted=[gather_ffn2_gate])
    (wg2,) = gather_ffn2_gate.result
    qkv_a3 = qkv_a.reshape(bsz, s_len, QKV_A)
    qkv_b3 = qkv_b.reshape(bsz, s_len, QKV_B)

    far = jnp.broadcast_to(rel_bias[:, REL_TABLE - 1:REL_TABLE], (A_HEADS, REL_WRAP // 2))
    tv = jnp.concatenate([far, jnp.flip(rel_bias, axis=1), jnp.zeros((A_HEADS, REL_WRAP // 2 - REL_TABLE), F32)], axis=1)
    bias_a = _bias_a_build(tv.reshape(A_HEADS, 1, REL_WRAP))
    bias_b = jnp.asarray(_bias_b())
    sink_rows = jnp.broadcast_to(sinks.reshape(B_Q_HEADS, 1, 1), (B_Q_HEADS, 8, LANES))

    oa = _attn_a_fwd(qkv_a3, bias_a, hosted=[gather_ffn2_rest]).reshape(t, A_WIDTH)
    wu2, wd2 = gather_ffn2_rest.result
    ob = _attn_b_fwd(qkv_b3, bias_b, sink_rows).reshape(t, B_Q_WIDTH)
    x2, ya, yb, mg = _mix_out_fwd(x1, oa, ob, gates, proj_t, wout)
    h3, g2, u2, a2, x3 = _ffn_fwd(x2, gam3, wg2, wu2, wd2, "ffn2_fwd")

    dx3, dgam4, loss_part = _loss_head(x3, gam4, tgt)

    dx2, dg2, du2, db2, dgam3 = _ffn_bwd(dx3, x2, gam3, g2, u2, wg2, wu2, wd2, "ffn2_bwd")
    gw_ffn2 = [_mm_tn([dg2], h3, "grad_ffn2_gate"), _mm_tn([du2], h3, "grad_ffn2_up"),
               _mm_tn([a2], db2, "grad_ffn2_down")]
    pairx_ffn2 = _PairExchange(gw_ffn2)
    dxb, dya, dyb, doa, dob, dgates = _mix_out_bwd(dx2, gates, ya, yb, proj_t, wout, hosted=[pairx_ffn2])
    psum_ffn2 = _pair_sum(core, gw_ffn2, pairx_ffn2.result, "pair_sum_ffn2")
    gw_out = _mm_tn([mg], dxb, "grad_w_out")
    gw_proj = _mm_tn_proj(dya, dyb, oa, ob)

    chipx_ffn2 = _ChipExchange(psum_ffn2)
    dqa, dka, dva, dbias_a = _attn_a_bwd(qkv_a3, bias_a, doa.reshape(bsz, s_len, A_WIDTH), hosted=[chipx_ffn2])
    pairx_out = _PairExchange([gw_proj, gw_out])
    dqb, dkvb, dsink = _attn_b_bwd(qkv_b3, bias_b, sink_rows, dob.reshape(bsz, s_len, B_Q_WIDTH), hosted=[pairx_out])
    drel_lanes = _relbias_grad(dbias_a)
    dproj = [dqa.reshape(t, A_WIDTH), dka.reshape(t, A_WIDTH), dva.reshape(t, A_WIDTH), dqb.reshape(t, B_Q_WIDTH),
             dkvb.reshape(t, 2 * B_KV_WIDTH), dgates]

    gw_in = _mm_tn(dproj, h2, "grad_w_in")
    pairx_in = _PairExchange([gw_in])
    psum_out = _pair_sum(core, [gw_proj, gw_out], pairx_out.result, "pair_sum_mix")
    chipx_out = _ChipExchange(psum_out)
    dx1, db1, dgam2 = _proj_bwd(dx2, x1, gam2, dproj, win_t, hosted=[pairx_in, chipx_out])
    psum_in = _pair_sum(core, [gw_in], pairx_in.result, "pair_sum_w_in")
    gw_d1 = _mm_tn([a1], db1, "grad_ffn1_down")

    chipx_in = _ChipExchange(psum_in)
    pairx_d1 = _PairExchange([gw_d1])
    dg1, du1 = _ffn_bwd_act(dx1, g1, u1, wd1, "ffn1_bwd_act", hosted=[chipx_in, pairx_d1])
    psum_d1 = _pair_sum(core, [gw_d1], pairx_d1.result, "pair_sum_ffn1_down")
    chipx_d1 = _ChipExchange(psum_d1)
    gw_g1 = _mm_tn([dg1], h1, "grad_ffn1_gate", hosted=[chipx_d1])
    from_sibling_g1 = _exchange_alone(_PairExchange([gw_g1]), "pair_exchange_ffn1_gate")
    psum_g1 = _pair_sum(core, [gw_g1], from_sibling_g1, "pair_sum_ffn1_gate")
    chipx_g1 = _ChipExchange(psum_g1)
    gw_u1 = _mm_tn([du1], h1, "grad_ffn1_up", hosted=[chipx_g1])
    from_sibling_u1 = _exchange_alone(_PairExchange([gw_u1]), "pair_exchange_ffn1_up")
    psum_u1 = _pair_sum(core, [gw_u1], from_sibling_u1, "pair_sum_ffn1_up")
    chipx_u1 = _ChipExchange(psum_u1)
    dx0, dgam1 = _ffn_bwd_in(dx1, x0, gam1, dg1, du1, wg1, wu1, "ffn1_bwd_in", hosted=[chipx_u1])

    g_g1, g_u1, g_d1, g_g2, g_u2, g_d2 = _final_sum(
        chip, psum_g1 + psum_u1 + psum_d1 + psum_ffn2,
        chipx_g1.result + chipx_u1.result + chipx_d1.result + chipx_ffn2.result, "grad_sum_ffn")
    (g_in,) = _final_sum(chip, psum_in, chipx_in.result, "grad_sum_w_in")
    g_proj, g_out = _final_sum(chip, psum_out, chipx_out.result, "grad_sum_mix")
    row_form_names = ("ffn1_w_gate", "ffn1_w_up", "w_in", "ffn2_w_gate", "ffn2_w_up")
    grads = {
        "ffn1_w_gate": g_g1, "ffn1_w_up": g_u1, "ffn1_w_down": g_d1, "w_in": g_in,
        "w_proj_a": g_proj[:, 0:A_WIDTH].T, "w_proj_b": g_proj[:, A_WIDTH:].T, "w_out": g_out,
        "ffn2_w_gate": g_g2, "ffn2_w_up": g_u2, "ffn2_w_down": g_d2,
    }

    def row_of(v):
        return jnp.pad(v.reshape(1, -1), ((0, 0), (0, D_MODEL - v.size)))

    def table_rows(v):
        return jnp.pad(v, ((0, 0), (0, D_MODEL - REL_TABLE)))

    drel_local = jnp.flip(drel_lanes[:, 0, 0:REL_TABLE], axis=1)
    small_part = jnp.concatenate(
        [jnp.sum(dgam1, axis=0, keepdims=True), jnp.sum(dgam2, axis=0, keepdims=True),
         jnp.sum(dgam3, axis=0, keepdims=True), jnp.sum(dgam4, axis=0, keepdims=True),
         row_of(jnp.sum(loss_part)), row_of(dsink[:, 0, 0]), jnp.zeros((2, D_MODEL), F32),
         table_rows(drel_local)], axis=0)
    small = _all_reduce_small(small_part)
    loss = small[4, 0]

    def pack(n1, n2, n3, n4, sk, tb):
        return jnp.concatenate([n1.reshape(1, -1), n2.reshape(1, -1), n3.reshape(1, -1), n4.reshape(1, -1),
                                jnp.zeros((1, D_MODEL), F32), row_of(sk), jnp.zeros((2, D_MODEL), F32), table_rows(tb)],
                               axis=0)

    live = np.zeros((SMALL_ROWS, D_MODEL), np.float32)
    live[0:4] = 1.0
    live[5, 0:B_Q_HEADS] = 1.0
    live[8:16, 0:REL_TABLE] = 1.0
    small_g = small * jnp.asarray(live)
    sw = pack(ffn1_norm, mix_norm, ffn2_norm, final_norm, sinks, rel_bias)
    sm = pack(m_ffn1_norm, m_mix_norm, m_ffn2_norm, m_final_norm, m_sinks, m_rel_bias)
    sv = pack(v_ffn1_norm, v_mix_norm, v_ffn2_norm, v_final_norm, v_sinks, v_rel_bias)
    (sd,), (snm,), (snv,) = _adamw([sw], [small_g], [sm], [sv], "adamw_small")

    def unpack(p):
        return {"ffn1_norm": p[0], "mix_norm": p[1], "ffn2_norm": p[2], "final_norm": p[3],
                "sinks": p[5, 0:B_Q_HEADS], "rel_bias": p[8:16, 0:REL_TABLE]}

    grads.update(unpack(small_g))
    delta, new_m, new_v = unpack(sd), unpack(snm), unpack(snv)

    wmv = {
        "ffn1_w_gate": (ffn1_w_gate, m_ffn1_w_gate, v_ffn1_w_gate), "ffn1_w_up": (ffn1_w_up, m_ffn1_w_up, v_ffn1_w_up),
        "ffn1_w_down": (ffn1_w_down, m_ffn1_w_down, v_ffn1_w_down), "w_in": (w_in, m_w_in, v_w_in),
        "w_proj_a": (w_proj_a, m_w_proj_a, v_w_proj_a), "w_proj_b": (w_proj_b, m_w_proj_b, v_w_proj_b),
        "w_out": (w_out, m_w_out, v_w_out),
        "ffn2_w_gate": (ffn2_w_gate, m_ffn2_w_gate, v_ffn2_w_gate), "ffn2_w_up": (ffn2_w_up, m_ffn2_w_up, v_ffn2_w_up),
        "ffn2_w_down": (ffn2_w_down, m_ffn2_w_down, v_ffn2_w_down),
    }
    def adamw_group(gname, names):
        def form(n, a):
            return a.T if n in row_form_names else a

        ds_, ms_, vs_ = _adamw([form(n, wmv[n][0]) for n in names], [grads[n] for n in names],
                               [form(n, wmv[n][1]) for n in names], [form(n, wmv[n][2]) for n in names], gname)
        for n, d_, m_, v_ in zip(names, ds_, ms_, vs_):
            delta[n], new_m[n], new_v[n] = form(n, d_), form(n, m_), form(n, v_)

    adamw_group("adamw_ffn", ["ffn1_w_gate", "ffn1_w_up", "ffn1_w_down", "ffn2_w_gate", "ffn2_w_up", "ffn2_w_down"])
    adamw_group("adamw_rest", ["w_in", "w_proj_a", "w_proj_b", "w_out"])
    for n in row_form_names:
        grads[n] = grads[n].T

    order = ["ffn1_norm", "ffn1_w_gate", "ffn1_w_up", "ffn1_w_down", "mix_norm", "w_in", "rel_bias", "sinks",
             "w_proj_a", "w_proj_b", "w_out", "ffn2_norm", "ffn2_w_gate", "ffn2_w_up", "ffn2_w_down", "final_norm"]
    grad_x = dx0.reshape(bsz, s_len, D_MODEL)
    return (loss, grad_x, *[grads[n] for n in order], *[delta[n] for n in order], *[new_m[n] for n in order],
            *[new_v[n] for n in order])
```

```python
import numpy as np
import jax
import jax.numpy as jnp
from jax import lax
from jax.experimental import pallas as pl
from jax.experimental.pallas import tpu as pltpu

F32 = jnp.float32
BF16 = jnp.bfloat16

D_MODEL = 1024
D_FF = 2816
CHUNK = 64
D_HEAD = 64
A_HEADS = 8
A_PREV = 8
MAX_REL = 128
B_Q_HEADS = 8
B_KV_HEADS = 2
B_GROUP = B_Q_HEADS // B_KV_HEADS
B_PREV = 2
REL_TABLE = (CHUNK - 1) + MAX_REL + 1
A_WIDTH = A_HEADS * D_HEAD
B_Q_WIDTH = B_Q_HEADS * D_HEAD
B_KV_WIDTH = B_KV_HEADS * D_HEAD
QKV_A = 3 * A_WIDTH
QKV_B = B_Q_WIDTH + 2 * B_KV_WIDTH
IN_WIDTH = QKV_A + QKV_B + 2 * D_MODEL
EPS = 1e-6
NEG_INF = -1e30
SCALE = 1.0 / 8.0

ADAM_LR = 0.001
ADAM_B1 = 0.9
ADAM_B2 = 0.999
ADAM_EPS = 1e-08
ADAM_WD = 0.01
ADAM_STEP = 10

N_DEV = 8
N_CHIP = 4
MESH = pl.DeviceIdType.MESH

LANES = 128
TQ = 256
TM = 256
TM_FWD = 256
FC = 256
VMEM_LIMIT = 56 << 20


def _cparams(sem, vmem=VMEM_LIMIT):
    return pltpu.CompilerParams(dimension_semantics=sem, vmem_limit_bytes=vmem)


def _dot_nt(a, b):
    return lax.dot_general(a, b, (((1,), (1,)), ((), ())), preferred_element_type=F32)


def _dot_nn(a, b):
    return lax.dot_general(a, b, (((1,), (0,)), ((), ())), preferred_element_type=F32)


def _dot_tn(a, b):
    return lax.dot_general(a, b, (((0,), (0,)), ((), ())), preferred_element_type=F32)


def _resident(shape):
    nd = len(shape)
    return pl.BlockSpec(shape, lambda *_: (0,) * nd, pipeline_mode=pl.Buffered(1))


def _rows(tm, width):
    return pl.BlockSpec((tm, width), lambda i: (i, 0))


def _colsum8(v):
    tm, n = v.shape
    return jnp.sum(v.reshape(tm // 8, 8, n), axis=0)


def _rms(x):
    r = lax.rsqrt(jnp.mean(x * x, axis=-1, keepdims=True) + EPS)
    return x * r, r


def _rms_bwd(dh, xh, r, gamma):
    dxh = dh * gamma
    dx = r * (dxh - xh * jnp.mean(dxh * xh, axis=-1, keepdims=True))
    return dx, _colsum8(dh * xh)


def _hbm():
    return pl.BlockSpec(memory_space=pltpu.HBM)


def _call(body, *, name, grid, in_specs, out_specs, out_shape, args, sem, scratch_shapes=(), hosted=()):
    in_specs, out_specs, out_shape = list(in_specs), list(out_specs), list(out_shape)
    scratch_shapes = list(scratch_shapes)
    if not hosted:
        return pl.pallas_call(body, name=name, grid=grid, in_specs=in_specs, out_specs=out_specs, out_shape=out_shape,
                              scratch_shapes=scratch_shapes, compiler_params=_cparams(sem))(*args)
    n_in, n_out, n_scr = len(in_specs), len(out_specs), len(scratch_shapes)
    x_in = [a for x in hosted for a in x.inputs]
    x_out = [s for x in hosted for s in x.out_shape]
    x_scr = [s for x in hosted for s in x.scratch]
    steps = int(np.prod(grid))
    forward_step = max(steps - 3, 0)
    relay_step = min((5 * steps) // 8, forward_step)

    def wrapped(*refs):
        pos = [0]

        def take(k):
            pos[0] += k
            return refs[pos[0] - k:pos[0]]

        ins, xin, outs, xout, scr, xscr = (take(k) for k in (n_in, len(x_in), n_out, len(x_out), n_scr, len(x_scr)))
        step = 0
        for axis, extent in enumerate(grid):
            step = step * extent + pl.program_id(axis)
        own, oi, oo, osc = [], 0, 0, 0
        for x in hosted:
            own.append((xin[oi:oi + len(x.inputs)], xout[oo:oo + len(x.out_shape)], xscr[osc:osc + len(x.scratch)]))
            oi, oo, osc = oi + len(x.inputs), oo + len(x.out_shape), osc + len(x.scratch)

        def phase(method):
            for x, (i_, o_, s_) in zip(hosted, own):
                getattr(x, method)(i_, o_, s_)

        pl.when(step == 0)(lambda: phase("start"))
        body(*ins, *outs, *scr)
        pl.when(step == relay_step)(lambda: phase("relay"))
        pl.when(step == forward_step)(lambda: phase("forward"))
        pl.when(step == steps - 1)(lambda: phase("finish"))

    res = pl.pallas_call(
        wrapped, name=name, grid=grid, in_specs=in_specs + [_hbm()] * len(x_in),
        out_specs=out_specs + [_hbm()] * len(x_out), out_shape=out_shape + x_out,
        scratch_shapes=scratch_shapes + x_scr, compiler_params=_cparams(("arbitrary",) * len(grid)))(*args, *x_in)
    rest = list(res[n_out:])
    for x in hosted:
        x.result, rest = rest[:len(x.out_shape)], rest[len(x.out_shape):]
    return list(res[:n_out])


def _ffn_fwd(x, gamma, wg_t, wu_t, wd, name, hosted=()):
    t = x.shape[0]
    f = wg_t.shape[0]

    def body(x_ref, gam_ref, wg_ref, wu_ref, wd_ref, h_ref, g_ref, u_ref, a_ref, y_ref):
        xv = x_ref[...]
        xh, _ = _rms(xv)
        h = (xh * gam_ref[...]).astype(BF16)
        h_ref[...] = h
        for j in range(f // FC):
            sl = slice(j * FC, (j + 1) * FC)
            g = _dot_nt(h, wg_ref[sl, :])
            u = _dot_nt(h, wu_ref[sl, :])
            g_ref[:, sl] = g.astype(BF16)
            u_ref[:, sl] = u.astype(BF16)
            a_ref[:, sl] = (g * jax.nn.sigmoid(g) * u).astype(BF16)
        y_ref[...] = xv + 0.5 * _dot_nn(a_ref[...], wd_ref[...])

    return _call(
        body,
        name=name,
        grid=(t // TM_FWD,),
        in_specs=[_rows(TM_FWD, D_MODEL), _resident((1, D_MODEL)), _resident((f, D_MODEL)), _resident((f, D_MODEL)),
                  _resident((f, D_MODEL))],
        out_specs=[_rows(TM_FWD, D_MODEL), _rows(TM_FWD, f), _rows(TM_FWD, f), _rows(TM_FWD, f),
                   _rows(TM_FWD, D_MODEL)],
        out_shape=[jax.ShapeDtypeStruct((t, D_MODEL), BF16), jax.ShapeDtypeStruct((t, f), BF16),
                   jax.ShapeDtypeStruct((t, f), BF16), jax.ShapeDtypeStruct((t, f), BF16),
                   jax.ShapeDtypeStruct((t, D_MODEL), F32)],
        args=(x, gamma, wg_t, wu_t, wd), sem=("parallel",), hosted=hosted)


def _ffn_up(x, gamma, wg_t, wu_t, name, hosted=()):
    t = x.shape[0]
    f = wg_t.shape[0]

    def body(x_ref, gam_ref, wg_ref, wu_ref, h_ref, g_ref, u_ref, a_ref):
        xh, _ = _rms(x_ref[...])
        h = (xh * gam_ref[...]).astype(BF16)
        h_ref[...] = h
        for j in range(f // FC):
            sl = slice(j * FC, (j + 1) * FC)
            g = _dot_nt(h, wg_ref[sl, :])
            u = _dot_nt(h, wu_ref[sl, :])
            g_ref[:, sl] = g.astype(BF16)
            u_ref[:, sl] = u.astype(BF16)
            a_ref[:, sl] = (g * jax.nn.sigmoid(g) * u).astype(BF16)

    return _call(
        body,
        name=name,
        grid=(t // TM,),
        in_specs=[_rows(TM, D_MODEL), _resident((1, D_MODEL)), _resident((f, D_MODEL)), _resident((f, D_MODEL))],
        out_specs=[_rows(TM, D_MODEL), _rows(TM, f), _rows(TM, f), _rows(TM, f)],
        out_shape=[jax.ShapeDtypeStruct((t, D_MODEL), BF16), jax.ShapeDtypeStruct((t, f), BF16),
                   jax.ShapeDtypeStruct((t, f), BF16), jax.ShapeDtypeStruct((t, f), BF16)],
        args=(x, gamma, wg_t, wu_t), sem=("parallel",), hosted=hosted)


def _ffn_down(x, a_act, wd, name, hosted=()):
    t = x.shape[0]
    f = wd.shape[0]

    def body(x_ref, a_ref, wd_ref, y_ref):
        y_ref[...] = x_ref[...] + 0.5 * _dot_nn(a_ref[...], wd_ref[...])

    return _call(
        body,
        name=name,
        grid=(t // TM,),
        in_specs=[_rows(TM, D_MODEL), _rows(TM, f), _resident((f, D_MODEL))],
        out_specs=[_rows(TM, D_MODEL)],
        out_shape=[jax.ShapeDtypeStruct((t, D_MODEL), F32)],
        args=(x, a_act, wd), sem=("parallel",), hosted=hosted)[0]


def _ffn_bwd_head(y, gamma_f, target, x, gamma, g_act, u_act, wg_t, wu_t, wd, name):
    t = x.shape[0]
    f = wg_t.shape[0]

    def body(y_ref, gamf_ref, t_ref, x_ref, gam_ref, g_ref, u_ref, wg_ref, wu_ref, wd_ref, dx_ref, dg_ref, du_ref,
             db_ref, dgam_ref, dgamf_ref, loss_ref):
        yh, ry = _rms(y_ref[...])
        gam_f = gamf_ref[...]
        e = yh * gam_f - t_ref[...]
        dv, dgam_f = _rms_bwd(e * (1.0 / D_MODEL), yh, ry, gam_f)
        db = (0.5 * dv).astype(BF16)
        db_ref[...] = db
        for j in range(f // FC):
            sl = slice(j * FC, (j + 1) * FC)
            da = _dot_nt(db, wd_ref[sl, :])
            g = g_ref[:, sl].astype(F32)
            u = u_ref[:, sl].astype(F32)
            s = jax.nn.sigmoid(g)
            dg_ref[:, sl] = (da * u * (s * (1.0 + g * (1.0 - s)))).astype(BF16)
            du_ref[:, sl] = (da * (g * s)).astype(BF16)
        dh = _dot_nn(dg_ref[...], wg_ref[...]) + _dot_nn(du_ref[...], wu_ref[...])
        xh, r = _rms(x_ref[...])
        dxn, dgam = _rms_bwd(dh, xh, r, gam_ref[...])
        dx_ref[...] = dv + dxn

        @pl.when(pl.program_id(0) == 0)
        def _():
            dgam_ref[...] = jnp.zeros_like(dgam_ref)
            dgamf_ref[...] = jnp.zeros_like(dgamf_ref)
            loss_ref[...] = jnp.zeros_like(loss_ref)

        dgam_ref[...] += dgam
        dgamf_ref[...] += dgam_f
        loss_ref[...] += _colsum8(e * e) * (0.5 / D_MODEL)

    acc = pl.BlockSpec((8, D_MODEL), lambda i: (0, 0))
    return _call(
        body,
        name=name,
        grid=(t // TM,),
        in_specs=[_rows(TM, D_MODEL), _resident((1, D_MODEL)), _rows(TM, D_MODEL), _rows(TM, D_MODEL),
                  _resident((1, D_MODEL)), _rows(TM, f), _rows(TM, f),
                  _resident((f, D_MODEL)), _resident((f, D_MODEL)), _resident((f, D_MODEL))],
        out_specs=[_rows(TM, D_MODEL), _rows(TM, f), _rows(TM, f), _rows(TM, D_MODEL), acc, acc, acc],
        out_shape=[jax.ShapeDtypeStruct((t, D_MODEL), F32), jax.ShapeDtypeStruct((t, f), BF16),
                   jax.ShapeDtypeStruct((t, f), BF16), jax.ShapeDtypeStruct((t, D_MODEL), BF16),
                   jax.ShapeDtypeStruct((8, D_MODEL), F32), jax.ShapeDtypeStruct((8, D_MODEL), F32),
                   jax.ShapeDtypeStruct((8, D_MODEL), F32)],
        args=(y, gamma_f, target, x, gamma, g_act, u_act, wg_t, wu_t, wd), sem=("arbitrary",))


def _ffn_bwd(d, x, gamma, g_act, u_act, wg_t, wu_t, wd, name, hosted=()):
    t = x.shape[0]
    f = wg_t.shape[0]

    def body(d_ref, x_ref, gam_ref, g_ref, u_ref, wg_ref, wu_ref, wd_ref, dx_ref, dg_ref, du_ref, db_ref, dgam_ref):
        dv = d_ref[...]
        db = (0.5 * dv).astype(BF16)
        db_ref[...] = db
        for j in range(f // FC):
            sl = slice(j * FC, (j + 1) * FC)
            da = _dot_nt(db, wd_ref[sl, :])
            g = g_ref[:, sl].astype(F32)
            u = u_ref[:, sl].astype(F32)
            s = jax.nn.sigmoid(g)
            dg_ref[:, sl] = (da * u * (s * (1.0 + g * (1.0 - s)))).astype(BF16)
            du_ref[:, sl] = (da * (g * s)).astype(BF16)
        dh = _dot_nn(dg_ref[...], wg_ref[...]) + _dot_nn(du_ref[...], wu_ref[...])
        xh, r = _rms(x_ref[...])
        dxn, dgam = _rms_bwd(dh, xh, r, gam_ref[...])
        dx_ref[...] = dv + dxn

        @pl.when(pl.program_id(0) == 0)
        def _():
            dgam_ref[...] = jnp.zeros_like(dgam_ref)

        dgam_ref[...] += dgam

    return _call(
        body,
        name=name,
        grid=(t // TM,),
        in_specs=[_rows(TM, D_MODEL), _rows(TM, D_MODEL), _resident((1, D_MODEL)), _rows(TM, f), _rows(TM, f),
                  _resident((f, D_MODEL)), _resident((f, D_MODEL)), _resident((f, D_MODEL))],
        out_specs=[_rows(TM, D_MODEL), _rows(TM, f), _rows(TM, f), _rows(TM, D_MODEL),
                   pl.BlockSpec((8, D_MODEL), lambda i: (0, 0))],
        out_shape=[jax.ShapeDtypeStruct((t, D_MODEL), F32), jax.ShapeDtypeStruct((t, f), BF16),
                   jax.ShapeDtypeStruct((t, f), BF16), jax.ShapeDtypeStruct((t, D_MODEL), BF16),
                   jax.ShapeDtypeStruct((8, D_MODEL), F32)],
        args=(d, x, gamma, g_act, u_act, wg_t, wu_t, wd), sem=("arbitrary",), hosted=hosted)


def _ffn_bwd_act(d, g_act, u_act, wd, name, hosted=()):
    t = d.shape[0]
    f = wd.shape[0]

    def body(d_ref, g_ref, u_ref, wd_ref, dg_ref, du_ref):
        db = (0.5 * d_ref[...]).astype(BF16)
        for j in range(f // FC):
            sl = slice(j * FC, (j + 1) * FC)
            da = _dot_nt(db, wd_ref[sl, :])
            g = g_ref[:, sl].astype(F32)
            u = u_ref[:, sl].astype(F32)
            s = jax.nn.sigmoid(g)
            dg_ref[:, sl] = (da * u * (s * (1.0 + g * (1.0 - s)))).astype(BF16)
            du_ref[:, sl] = (da * (g * s)).astype(BF16)

    return _call(
        body,
        name=name,
        grid=(t // TM,),
        in_specs=[_rows(TM, D_MODEL), _rows(TM, f), _rows(TM, f), _resident((f, D_MODEL))],
        out_specs=[_rows(TM, f), _rows(TM, f)],
        out_shape=[jax.ShapeDtypeStruct((t, f), BF16), jax.ShapeDtypeStruct((t, f), BF16)],
        args=(d, g_act, u_act, wd), sem=("parallel",), hosted=hosted)


def _ffn_bwd_in(d, x, gamma, dg, du, wg_t, wu_t, name, hosted=()):
    t = x.shape[0]
    f = wg_t.shape[0]

    def body(d_ref, x_ref, gam_ref, dg_ref, du_ref, wg_ref, wu_ref, dx_ref, dgam_ref):
        dh = _dot_nn(dg_ref[...], wg_ref[...]) + _dot_nn(du_ref[...], wu_ref[...])
        xh, r = _rms(x_ref[...])
        dxn, dgam = _rms_bwd(dh, xh, r, gam_ref[...])
        dx_ref[...] = d_ref[...] + dxn

        @pl.when(pl.program_id(0) == 0)
        def _():
            dgam_ref[...] = jnp.zeros_like(dgam_ref)

        dgam_ref[...] += dgam

    return _call(
        body,
        name=name,
        grid=(t // TM,),
        in_specs=[_rows(TM, D_MODEL), _rows(TM, D_MODEL), _resident((1, D_MODEL)), _rows(TM, f), _rows(TM, f),
                  _resident((f, D_MODEL)), _resident((f, D_MODEL))],
        out_specs=[_rows(TM, D_MODEL), pl.BlockSpec((8, D_MODEL), lambda i: (0, 0))],
        out_shape=[jax.ShapeDtypeStruct((t, D_MODEL), F32), jax.ShapeDtypeStruct((8, D_MODEL), F32)],
        args=(d, x, gamma, dg, du, wg_t, wu_t), sem=("arbitrary",), hosted=hosted)


def _mm_tn(pieces, b, name, tile=256, hosted=()):
    t, n = b.shape
    npc = len(pieces)
    counts = [p.shape[1] // tile for p in pieces]
    los = [sum(counts[:k]) for k in range(npc)]
    total = sum(counts)

    def body(*refs):
        a_refs, b_ref, o_ref = refs[:npc], refs[npc], refs[npc + 1]
        i = pl.program_id(0)
        for k in range(npc):
            @pl.when(jnp.logical_and(i >= los[k], i < los[k] + counts[k]))
            def _(k=k):
                o_ref[...] = _dot_tn(a_refs[k][...], b_ref[...]).astype(BF16)

    def a_spec(k):
        return pl.BlockSpec((t, tile), lambda i: (0, jnp.clip(i - los[k], 0, counts[k] - 1)))

    return _call(
        body,
        name=name,
        grid=(total,),
        in_specs=[a_spec(k) for k in range(npc)] + [_resident((t, n))],
        out_specs=[pl.BlockSpec((tile, n), lambda i: (i, 0))],
        out_shape=[jax.ShapeDtypeStruct((total * tile, n), BF16)],
        args=(*pieces, b), sem=("parallel",), hosted=hosted)[0]


def _mm_tn_proj(dya, dyb, oa, ob, tile=256):
    t = dya.shape[0]

    def body(dya_ref, dyb_ref, oa_ref, ob_ref, o_ref):
        o_ref[:, 0:A_WIDTH] = _dot_tn(dya_ref[...], oa_ref[...]).astype(BF16)
        o_ref[:, A_WIDTH:A_WIDTH + B_Q_WIDTH] = _dot_tn(dyb_ref[...], ob_ref[...]).astype(BF16)

    col = pl.BlockSpec((t, tile), lambda i: (0, i))
    return pl.pallas_call(
        body,
        name="grad_proj",
        grid=(D_MODEL // tile,),
        in_specs=[col, col, _resident((t, A_WIDTH)), _resident((t, B_Q_WIDTH))],
        out_specs=pl.BlockSpec((tile, A_WIDTH + B_Q_WIDTH), lambda i: (i, 0)),
        out_shape=jax.ShapeDtypeStruct((D_MODEL, A_WIDTH + B_Q_WIDTH), BF16),
        compiler_params=_cparams(("parallel",)),
    )(dya, dyb, oa, ob)


def _proj_fwd(x, gamma, win_t, hosted=()):
    t = x.shape[0]

    def body(x_ref, gam_ref, w_ref, h_ref, qa_ref, qb_ref, gt_ref):
        xh, _ = _rms(x_ref[...])
        h = (xh * gam_ref[...]).astype(BF16)
        h_ref[...] = h
        for j in range(QKV_A // FC):
            qa_ref[:, j * FC:(j + 1) * FC] = _dot_nt(h, w_ref[j * FC:(j + 1) * FC, :]).astype(BF16)
        for j in range(QKV_B // FC):
            lo = QKV_A + j * FC
            qb_ref[:, j * FC:(j + 1) * FC] = _dot_nt(h, w_ref[lo:lo + FC, :]).astype(BF16)
        for j in range(2 * D_MODEL // FC):
            lo = QKV_A + QKV_B + j * FC
            gt_ref[:, j * FC:(j + 1) * FC] = _dot_nt(h, w_ref[lo:lo + FC, :])

    return _call(
        body,
        name="proj_fwd",
        grid=(t // TM_FWD,),
        in_specs=[_rows(TM_FWD, D_MODEL), _resident((1, D_MODEL)), _resident((IN_WIDTH, D_MODEL))],
        out_specs=[_rows(TM_FWD, D_MODEL), _rows(TM_FWD, QKV_A), _rows(TM_FWD, QKV_B), _rows(TM_FWD, 2 * D_MODEL)],
        out_shape=[jax.ShapeDtypeStruct((t, D_MODEL), BF16), jax.ShapeDtypeStruct((t, QKV_A), BF16),
                   jax.ShapeDtypeStruct((t, QKV_B), BF16), jax.ShapeDtypeStruct((t, 2 * D_MODEL), F32)],
        args=(x, gamma, win_t), sem=("parallel",), hosted=hosted)


def _proj_bwd(d, x, gamma, pieces, win_t, hosted=()):
    t = x.shape[0]
    npc = len(pieces)
    widths = [p.shape[1] for p in pieces]
    los = [sum(widths[:k]) for k in range(npc)]

    def body(*refs):
        d_ref, x_ref, gam_ref = refs[:3]
        p_refs = refs[3:3 + npc]
        w_ref, dx_ref, db_ref, dgam_ref = refs[3 + npc:]
        dh = _dot_nn(p_refs[0][...], w_ref[0:widths[0], :])
        for k in range(1, npc):
            dh += _dot_nn(p_refs[k][...], w_ref[los[k]:los[k] + widths[k], :])
        xh, r = _rms(x_ref[...])
        dxn, dgam = _rms_bwd(dh, xh, r, gam_ref[...])
        dx = d_ref[...] + dxn
        dx_ref[...] = dx
        db_ref[...] = (0.5 * dx).astype(BF16)

        @pl.when(pl.program_id(0) == 0)
        def _():
            dgam_ref[...] = jnp.zeros_like(dgam_ref)

        dgam_ref[...] += dgam

    return _call(
        body,
        name="proj_bwd",
        grid=(t // TM,),
        in_specs=[_rows(TM, D_MODEL), _rows(TM, D_MODEL), _resident((1, D_MODEL))] + [_rows(TM, w) for w in widths]
        + [_resident((IN_WIDTH, D_MODEL))],
        out_specs=[_rows(TM, D_MODEL), _rows(TM, D_MODEL), pl.BlockSpec((8, D_MODEL), lambda i: (0, 0))],
        out_shape=[jax.ShapeDtypeStruct((t, D_MODEL), F32), jax.ShapeDtypeStruct((t, D_MODEL), BF16),
                   jax.ShapeDtypeStruct((8, D_MODEL), F32)],
        args=(d, x, gamma, *pieces, win_t), sem=("arbitrary",), hosted=hosted)


def _lane_half(shape):
    return lax.broadcasted_iota(jnp.int32, shape, len(shape) - 1) // D_HEAD


def _band_weights(q, kk, bias, sink, qs, pad):
    s = _dot_nt(q, kk) + bias
    if qs is not None:
        col = lax.broadcasted_iota(jnp.int32, s.shape, 1)
        s = jnp.where(col + qs >= pad, s, NEG_INF)
    m = jnp.max(s, axis=-1, keepdims=True)
    if sink is not None:
        m = jnp.maximum(m, sink)
    return jnp.exp(s - m), m


def _weighted_values(p, vv_ones, sink, m):
    r = _dot_nn(p.astype(BF16), vv_ones)
    den = r[:, LANES:2 * LANES]
    if sink is not None:
        den = den + jnp.exp(sink - m)
    return r[:, 0:LANES] / den


def _band_softmax(q, kk, bias, sink, qs, pad):
    p, m = _band_weights(q, kk, bias, sink, qs, pad)
    den = jnp.sum(p, axis=-1, keepdims=True)
    if sink is not None:
        den = den + jnp.exp(sink - m)
    return p, m, 1.0 / den


def _fill_padded(dst, src, pad):
    dst[0:pad, :] = jnp.zeros((pad,) + dst.shape[1:], dst.dtype)
    dst[pad:, :] = src


FWD_PAIRS = 4
BWD_PAIRS = 2


def _attn_a_fwd(qkv, bias, hosted=()):
    bsz, s_len, _ = qkv.shape
    pad = A_PREV * CHUNK
    band = TQ + pad
    pp = FWD_PAIRS
    w = pp * LANES
    nb = A_WIDTH // w

    def body(q_ref, k_ref, v_ref, b_ref, o_ref, kp, vp):
        i = pl.program_id(2)

        @pl.when(i == 0)
        def _():
            _fill_padded(kp, k_ref[...], pad)
            _fill_padded(vp, v_ref[...], pad)

        qs = pl.multiple_of(i * TQ, TQ)
        half = _lane_half((1, LANES))

        ones = jnp.ones((band, LANES), BF16)

        def block(masked):
            for pr in range(pp):
                sl = slice(pr * LANES, (pr + 1) * LANES)
                kk = kp[pl.ds(qs, band), sl]
                vv = jnp.concatenate([vp[pl.ds(qs, band), sl], ones], axis=1)
                q = q_ref[:, sl] * SCALE
                outs = []
                for j in range(2):
                    qm = jnp.where(half == j, q, jnp.zeros_like(q))
                    p, m = _band_weights(qm, kk, b_ref[2 * pr + j], None, qs if masked else None, pad)
                    outs.append(_weighted_values(p, vv, None, m))
                o_ref[:, sl] = jnp.where(half == 0, outs[0], outs[1]).astype(BF16)

        pl.when(i < pad // TQ)(lambda: block(True))
        pl.when(i >= pad // TQ)(lambda: block(False))

    return _call(
        body,
        name="attn_a_fwd",
        grid=(bsz, nb, s_len // TQ),
        in_specs=[pl.BlockSpec((None, TQ, w), lambda b, g, i: (b, i, g)),
                  pl.BlockSpec((None, s_len, w), lambda b, g, i: (b, 0, nb + g)),
                  pl.BlockSpec((None, s_len, w), lambda b, g, i: (b, 0, 2 * nb + g)),
                  pl.BlockSpec((2 * pp, TQ, band), lambda b, g, i: (g, 0, 0))],
        out_specs=[pl.BlockSpec((None, TQ, w), lambda b, g, i: (b, i, g))],
        out_shape=[jax.ShapeDtypeStruct((bsz, s_len, A_WIDTH), BF16)],
        scratch_shapes=[pltpu.VMEM((pad + s_len, w), BF16), pltpu.VMEM((pad + s_len, w), BF16)],
        args=(qkv, qkv, qkv, bias), sem=("arbitrary", "arbitrary", "arbitrary"), hosted=hosted)[0]


def _attn_a_bwd(qkv, bias, do, hosted=()):
    bsz, s_len, _ = qkv.shape
    pad = A_PREV * CHUNK
    band = TQ + pad
    n_i = s_len // TQ
    pp = BWD_PAIRS
    w = pp * LANES
    nb = A_WIDTH // w

    def body(q_ref, k_ref, v_ref, b_ref, do_ref, dq_ref, dk_ref, dv_ref, dbias_ref, kp, vp, dk_acc, dv_acc):
        b = pl.program_id(1)
        i = pl.program_id(2)

        @pl.when(i == 0)
        def _():
            _fill_padded(kp, k_ref[...], pad)
            _fill_padded(vp, v_ref[...], pad)
            dk_acc[...] = jnp.zeros_like(dk_acc)
            dv_acc[...] = jnp.zeros_like(dv_acc)

        @pl.when(jnp.logical_and(b == 0, i == 0))
        def _():
            dbias_ref[...] = jnp.zeros_like(dbias_ref)

        qs = pl.multiple_of(i * TQ, TQ)
        half = _lane_half((1, LANES))

        def block(masked):
            for pr in range(pp):
                sl = slice(pr * LANES, (pr + 1) * LANES)
                kk = kp[pl.ds(qs, band), sl]
                vv = vp[pl.ds(qs, band), sl]
                q = q_ref[:, sl] * SCALE
                dd = do_ref[:, sl]
                dqs, dks, dvs = [], [], []
                for j in range(2):
                    qm = jnp.where(half == j, q, jnp.zeros_like(q))
                    dm = jnp.where(half == j, dd, jnp.zeros_like(dd))
                    p, _, inv = _band_softmax(qm, kk, b_ref[2 * pr + j], None, qs if masked else None, pad)
                    pn = p * inv
                    dp = _dot_nt(dm, vv)
                    delta = jnp.sum(pn * dp, axis=-1, keepdims=True)
                    ds = pn * (dp - delta)
                    dbias_ref[2 * pr + j] += ds[:, band - REL_COLS:]
                    dsb = ds.astype(BF16)
                    dqs.append(_dot_nn(dsb, kk))
                    dks.append(_dot_tn(dsb, q))
                    dvs.append(_dot_tn(pn.astype(BF16), dd))
                dq_ref[:, sl] = (jnp.where(half == 0, dqs[0], dqs[1]) * SCALE).astype(BF16)
                dk_acc[pl.ds(qs, band), sl] += jnp.where(half == 0, dks[0], dks[1])
                dv_acc[pl.ds(qs, band), sl] += jnp.where(half == 0, dvs[0], dvs[1])

        pl.when(i < pad // TQ)(lambda: block(True))
        pl.when(i >= pad // TQ)(lambda: block(False))

        @pl.when(i == n_i - 1)
        def _():
            dk_ref[...] = dk_acc[pad:, :].astype(BF16)
            dv_ref[...] = dv_acc[pad:, :].astype(BF16)

    qspec = pl.BlockSpec((None, TQ, w), lambda g, b, i: (b, i, g))
    kvout = pl.BlockSpec((None, s_len, w), lambda g, b, i: (b, 0, g))
    wide = jax.ShapeDtypeStruct((bsz, s_len, A_WIDTH), BF16)
    return _call(
        body,
        name="attn_a_bwd",
        grid=(nb, bsz, n_i),
        in_specs=[qspec,
                  pl.BlockSpec((None, s_len, w), lambda g, b, i: (b, 0, nb + g)),
                  pl.BlockSpec((None, s_len, w), lambda g, b, i: (b, 0, 2 * nb + g)),
                  pl.BlockSpec((2 * pp, TQ, band), lambda g, b, i: (g, 0, 0)),
                  qspec],
        out_specs=[qspec, kvout, kvout, pl.BlockSpec((2 * pp, TQ, REL_COLS), lambda g, b, i: (g, 0, 0))],
        out_shape=[wide, wide, wide, jax.ShapeDtypeStruct((A_HEADS, TQ, REL_COLS), F32)],
        scratch_shapes=[pltpu.VMEM((pad + s_len, w), BF16), pltpu.VMEM((pad + s_len, w), BF16),
                        pltpu.VMEM((pad + s_len, w), F32), pltpu.VMEM((pad + s_len, w), F32)],
        args=(qkv, qkv, qkv, bias, do), sem=("arbitrary", "arbitrary", "arbitrary"), hosted=hosted)


def _fill_padded_dup(dst, src, pad, h, half):
    other = pltpu.roll(src, D_HEAD, 1)
    _fill_padded(dst, jnp.where(half == h, src, other), pad)


def _attn_b_fwd(qkv, bias, sink):
    bsz, s_len, _ = qkv.shape
    pad = B_PREV * CHUNK
    band = TQ + pad
    kcol = B_Q_WIDTH // LANES
    npair = B_Q_HEADS // 2

    def body(q_ref, k_ref, v_ref, b_ref, s_ref, o_ref, kp, vp):
        i = pl.program_id(1)
        half = _lane_half((1, LANES))

        @pl.when(i == 0)
        def _():
            for h in range(B_KV_HEADS):
                _fill_padded_dup(kp.at[h], k_ref[...], pad, h, half)
                _fill_padded_dup(vp.at[h], v_ref[...], pad, h, half)

        qs = pl.multiple_of(i * TQ, TQ)

        ones = jnp.ones((band, LANES), BF16)

        def block(masked):
            for pr in range(npair):
                h = pr // (B_GROUP // 2)
                sl = slice(pr * LANES, (pr + 1) * LANES)
                kk = kp[h, pl.ds(qs, band), :]
                vv = jnp.concatenate([vp[h, pl.ds(qs, band), :], ones], axis=1)
                q = q_ref[:, sl] * SCALE
                outs = []
                for j in range(2):
                    qm = jnp.where(half == j, q, jnp.zeros_like(q))
                    sink = s_ref[2 * pr + j][0:1, 0:1]
                    p, m = _band_weights(qm, kk, b_ref[2 * pr + j], sink, qs if masked else None, pad)
                    outs.append(_weighted_values(p, vv, sink, m))
                o_ref[:, sl] = jnp.where(half == 0, outs[0], outs[1]).astype(BF16)

        pl.when(i < -(-pad // TQ))(lambda: block(True))
        pl.when(i >= -(-pad // TQ))(lambda: block(False))

    return pl.pallas_call(
        body,
        name="attn_b_fwd",
        grid=(bsz, s_len // TQ),
        in_specs=[pl.BlockSpec((None, TQ, B_Q_WIDTH), lambda b, i: (b, i, 0)),
                  pl.BlockSpec((None, s_len, LANES), lambda b, i: (b, 0, kcol)),
                  pl.BlockSpec((None, s_len, LANES), lambda b, i: (b, 0, kcol + 1)),
                  pl.BlockSpec((B_Q_HEADS, TQ, band), lambda b, i: (0, 0, 0)),
                  pl.BlockSpec((B_Q_HEADS, 8, LANES), lambda b, i: (0, 0, 0))],
        out_specs=pl.BlockSpec((None, TQ, B_Q_WIDTH), lambda b, i: (b, i, 0)),
        out_shape=jax.ShapeDtypeStruct((bsz, s_len, B_Q_WIDTH), BF16),
        scratch_shapes=[pltpu.VMEM((B_KV_HEADS, pad + s_len, LANES), BF16),
                        pltpu.VMEM((B_KV_HEADS, pad + s_len, LANES), BF16)],
        compiler_params=_cparams(("arbitrary", "arbitrary")),
    )(qkv, qkv, qkv, bias, sink)


def _attn_b_bwd(qkv, bias, sink, do, hosted=()):
    bsz, s_len, _ = qkv.shape
    pad = B_PREV * CHUNK
    band = TQ + pad
    kcol = B_Q_WIDTH // LANES
    npair = B_Q_HEADS // 2
    n_i = s_len // TQ

    pp = B_GROUP // 2
    w = pp * LANES

    def body(q_ref, k_ref, v_ref, b_ref, s_ref, do_ref, dq_ref, dkv_ref, dsink_ref, kp, vp, dk_acc, dv_acc):
        b = pl.program_id(0)
        h = pl.program_id(1)
        i = pl.program_id(2)
        half = _lane_half((1, LANES))

        @pl.when(i == 0)
        def _():
            _fill_padded_dup(kp, k_ref[...], pad, h, half)
            _fill_padded_dup(vp, v_ref[...], pad, h, half)

        @pl.when(jnp.logical_and(h == 0, i == 0))
        def _():
            dk_acc[...] = jnp.zeros_like(dk_acc)
            dv_acc[...] = jnp.zeros_like(dv_acc)

        @pl.when(jnp.logical_and(b == 0, jnp.logical_and(h == 0, i == 0)))
        def _():
            dsink_ref[...] = jnp.zeros_like(dsink_ref)

        qs = pl.multiple_of(i * TQ, TQ)

        def block(masked):
            kk = kp[pl.ds(qs, band), :]
            vv = vp[pl.ds(qs, band), :]
            dk2 = jnp.zeros((band, LANES), F32)
            dv2 = jnp.zeros((band, LANES), F32)
            for pr in range(pp):
                sl = slice(pr * LANES, (pr + 1) * LANES)
                q = q_ref[:, sl] * SCALE
                dd = do_ref[:, sl]
                dqs, dks, dvs = [], [], []
                for j in range(2):
                    qm = jnp.where(half == j, q, jnp.zeros_like(q))
                    dm = jnp.where(half == j, dd, jnp.zeros_like(dd))
                    sink = s_ref[2 * pr + j][0:1, 0:1]
                    p, m, inv = _band_softmax(qm, kk, b_ref[2 * pr + j], sink, qs if masked else None, pad)
                    pn = p * inv
                    dp = _dot_nt(dm, vv)
                    delta = jnp.sum(pn * dp, axis=-1, keepdims=True)
                    ds = pn * (dp - delta)
                    dsb = ds.astype(BF16)
                    dqs.append(_dot_nn(dsb, kk))
                    dks.append(_dot_tn(dsb, q))
                    dvs.append(_dot_tn(pn.astype(BF16), dd))
                    dsk = jnp.sum(-(jnp.exp(sink - m) * inv) * delta, axis=0, keepdims=True)
                    dsink_ref[2 * pp * h + 2 * pr + j] += jnp.broadcast_to(dsk, (8, LANES))
                dq_ref[:, sl] = (jnp.where(half == 0, dqs[0], dqs[1]) * SCALE).astype(BF16)
                dk2 = dk2 + jnp.where(half == 0, dks[0], dks[1])
                dv2 = dv2 + jnp.where(half == 0, dvs[0], dvs[1])
            dk_acc[pl.ds(qs, band), :] += jnp.where(half == h, dk2 + pltpu.roll(dk2, D_HEAD, 1), 0.0)
            dv_acc[pl.ds(qs, band), :] += jnp.where(half == h, dv2 + pltpu.roll(dv2, D_HEAD, 1), 0.0)

        pl.when(i < -(-pad // TQ))(lambda: block(True))
        pl.when(i >= -(-pad // TQ))(lambda: block(False))

        @pl.when(jnp.logical_and(h == B_KV_HEADS - 1, i == n_i - 1))
        def _():
            dkv_ref[:, 0:LANES] = dk_acc[pad:, :].astype(BF16)
            dkv_ref[:, LANES:2 * LANES] = dv_acc[pad:, :].astype(BF16)

    qspec = pl.BlockSpec((None, TQ, w), lambda b, h, i: (b, i, h))
    return _call(
        body,
        name="attn_b_bwd",
        grid=(bsz, B_KV_HEADS, n_i),
        in_specs=[qspec,
                  pl.BlockSpec((None, s_len, LANES), lambda b, h, i: (b, 0, kcol)),
                  pl.BlockSpec((None, s_len, LANES), lambda b, h, i: (b, 0, kcol + 1)),
                  pl.BlockSpec((2 * pp, TQ, band), lambda b, h, i: (h, 0, 0)),
                  pl.BlockSpec((2 * pp, 8, LANES), lambda b, h, i: (h, 0, 0)),
                  qspec],
        out_specs=[qspec, pl.BlockSpec((None, s_len, 2 * LANES), lambda b, h, i: (b, 0, 0)),
                   pl.BlockSpec((B_Q_HEADS, 8, LANES), lambda b, h, i: (0, 0, 0))],
        out_shape=[jax.ShapeDtypeStruct((bsz, s_len, B_Q_WIDTH), BF16),
                   jax.ShapeDtypeStruct((bsz, s_len, 2 * B_KV_WIDTH), BF16),
                   jax.ShapeDtypeStruct((B_Q_HEADS, 8, LANES), F32)],
        scratch_shapes=[pltpu.VMEM((pad + s_len, LANES), BF16), pltpu.VMEM((pad + s_len, LANES), BF16),
                        pltpu.VMEM((pad + s_len, LANES), F32), pltpu.VMEM((pad + s_len, LANES), F32)],
        args=(qkv, qkv, qkv, bias, sink, do), sem=("arbitrary", "arbitrary", "arbitrary"), hosted=hosted)


REL_COLS = 3 * 128
REL_WRAP = 512


def _bias_a_build(tv):
    h = tv.shape[0]
    pad = A_PREV * CHUNK
    band = TQ + pad

    def body(tv_ref, o_ref):
        row = tv_ref[...]
        x = jnp.broadcast_to(row, (TQ, REL_WRAP))
        r = lax.broadcasted_iota(jnp.int32, x.shape, 0)
        for bit in range(8):
            sh = 1 << bit
            x = jnp.where((r & sh) != 0, pltpu.roll(x, sh, 1), x)
        far = jnp.broadcast_to(row[:, 0:1], (TQ, band - REL_COLS))
        full = jnp.concatenate([far, x[:, REL_WRAP // 2:REL_WRAP], x[:, 0:REL_COLS - REL_WRAP // 2]], axis=1)
        qc = (lax.broadcasted_iota(jnp.int32, full.shape, 0) + pad) // CHUNK
        kc = lax.broadcasted_iota(jnp.int32, full.shape, 1) // CHUNK
        ok = jnp.logical_and(kc <= qc, kc >= qc - A_PREV)
        o_ref[...] = jnp.where(ok, full, NEG_INF)

    return pl.pallas_call(
        body,
        name="bias_a_build",
        grid=(h,),
        in_specs=[pl.BlockSpec((None, 1, REL_WRAP), lambda hh: (hh, 0, 0))],
        out_specs=pl.BlockSpec((None, TQ, band), lambda hh: (hh, 0, 0)),
        out_shape=jax.ShapeDtypeStruct((h, TQ, band), F32),
        compiler_params=_cparams(("parallel",)),
    )(tv)


def _relbias_grad(dbias):
    h, rows, _ = dbias.shape

    def body(d_ref, o_ref):
        x = d_ref[...]
        r = lax.broadcasted_iota(jnp.int32, x.shape, 0)
        c = lax.broadcasted_iota(jnp.int32, x.shape, 1) - r
        x = jnp.where(jnp.logical_and(c >= 1, c < REL_TABLE), x, 0.0)
        for bit in range(8):
            sh = 1 << bit
            x = jnp.where((r & sh) != 0, pltpu.roll(x, REL_COLS - sh, 1), x)
        diag = jnp.sum(x, axis=0, keepdims=True)
        lane = lax.broadcasted_iota(jnp.int32, diag.shape, 1)
        diag = jnp.where(jnp.logical_and(lane >= 1, lane < REL_TABLE), diag, 0.0)
        rest = -jnp.sum(diag, axis=1, keepdims=True)
        o_ref[...] = jnp.broadcast_to(jnp.where(lane == 0, rest, diag), o_ref.shape)

    return pl.pallas_call(
        body,
        name="relbias_grad",
        grid=(h,),
        in_specs=[pl.BlockSpec((None, rows, REL_COLS), lambda hh: (hh, 0, 0))],
        out_specs=pl.BlockSpec((None, 8, REL_COLS), lambda hh: (hh, 0, 0)),
        out_shape=jax.ShapeDtypeStruct((h, 8, REL_COLS), F32),
        compiler_params=_cparams(("parallel",)),
    )(dbias)


def _mix_out_fwd(x, oa, ob, gates, proj_t, wout):
    t = x.shape[0]

    def body(x_ref, oa_ref, ob_ref, gt_ref, pt_ref, wo_ref, y_ref, ya_ref, yb_ref, mg_ref):
        ya = _dot_nt(oa_ref[...], pt_ref[:, 0:A_WIDTH])
        yb = _dot_nt(ob_ref[...], pt_ref[:, A_WIDTH:A_WIDTH + B_Q_WIDTH])
        ya_ref[...] = ya.astype(BF16)
        yb_ref[...] = yb.astype(BF16)
        mg = jax.nn.sigmoid(gt_ref[:, 0:D_MODEL]) * ya + jax.nn.sigmoid(gt_ref[:, D_MODEL:2 * D_MODEL]) * yb
        mgb = mg.astype(BF16)
        mg_ref[...] = mgb
        y_ref[...] = x_ref[...] + _dot_nn(mgb, wo_ref[...])

    return pl.pallas_call(
        body,
        name="mix_out_fwd",
        grid=(t // TM,),
        in_specs=[_rows(TM, D_MODEL), _rows(TM, A_WIDTH), _rows(TM, B_Q_WIDTH), _rows(TM, 2 * D_MODEL),
                  _resident((D_MODEL, A_WIDTH + B_Q_WIDTH)), _resident((D_MODEL, D_MODEL))],
        out_specs=[_rows(TM, D_MODEL), _rows(TM, D_MODEL), _rows(TM, D_MODEL), _rows(TM, D_MODEL)],
        out_shape=[jax.ShapeDtypeStruct((t, D_MODEL), F32), jax.ShapeDtypeStruct((t, D_MODEL), BF16),
                   jax.ShapeDtypeStruct((t, D_MODEL), BF16), jax.ShapeDtypeStruct((t, D_MODEL), BF16)],
        compiler_params=_cparams(("parallel",)),
    )(x, oa, ob, gates, proj_t, wout)


def _mix_out_bwd(d, gates, ya, yb, proj_t, wout, hosted=()):
    t = d.shape[0]

    def body(d_ref, gt_ref, ya_ref, yb_ref, pt_ref, wo_ref, db_ref, dya_ref, dyb_ref, doa_ref, dob_ref, dgt_ref):
        db = d_ref[...].astype(BF16)
        db_ref[...] = db
        dmg = _dot_nt(db, wo_ref[...])
        sa = jax.nn.sigmoid(gt_ref[:, 0:D_MODEL])
        sb = jax.nn.sigmoid(gt_ref[:, D_MODEL:2 * D_MODEL])
        dya = (dmg * sa).astype(BF16)
        dyb = (dmg * sb).astype(BF16)
        dya_ref[...] = dya
        dyb_ref[...] = dyb
        dgt_ref[:, 0:D_MODEL] = (dmg * ya_ref[...].astype(F32) * (sa * (1.0 - sa))).astype(BF16)
        dgt_ref[:, D_MODEL:2 * D_MODEL] = (dmg * yb_ref[...].astype(F32) * (sb * (1.0 - sb))).astype(BF16)
        doa_ref[...] = _dot_nn(dya, pt_ref[:, 0:A_WIDTH]).astype(BF16)
        dob_ref[...] = _dot_nn(dyb, pt_ref[:, A_WIDTH:A_WIDTH + B_Q_WIDTH]).astype(BF16)

    return _call(
        body,
        name="mix_out_bwd",
        grid=(t // TM,),
        in_specs=[_rows(TM, D_MODEL), _rows(TM, 2 * D_MODEL), _rows(TM, D_MODEL), _rows(TM, D_MODEL),
                  _resident((D_MODEL, A_WIDTH + B_Q_WIDTH)), _resident((D_MODEL, D_MODEL))],
        out_specs=[_rows(TM, D_MODEL), _rows(TM, D_MODEL), _rows(TM, D_MODEL), _rows(TM, A_WIDTH),
                   _rows(TM, B_Q_WIDTH), _rows(TM, 2 * D_MODEL)],
        out_shape=[jax.ShapeDtypeStruct((t, D_MODEL), BF16), jax.ShapeDtypeStruct((t, D_MODEL), BF16),
                   jax.ShapeDtypeStruct((t, D_MODEL), BF16), jax.ShapeDtypeStruct((t, A_WIDTH), BF16),
                   jax.ShapeDtypeStruct((t, B_Q_WIDTH), BF16), jax.ShapeDtypeStruct((t, 2 * D_MODEL), BF16)],
        args=(d, gates, ya, yb, proj_t, wout), sem=("parallel",), hosted=hosted)


def _loss_head(x, gamma, target):
    t = x.shape[0]

    def body(x_ref, gam_ref, t_ref, dx_ref, dgam_ref, loss_ref):
        xh, r = _rms(x_ref[...])
        gam = gam_ref[...]
        e = xh * gam - t_ref[...]
        dy = e * (1.0 / D_MODEL)
        dxn, dgam = _rms_bwd(dy, xh, r, gam)
        dx_ref[...] = dxn

        @pl.when(pl.program_id(0) == 0)
        def _():
            dgam_ref[...] = jnp.zeros_like(dgam_ref)
            loss_ref[...] = jnp.zeros_like(loss_ref)

        dgam_ref[...] += dgam
        loss_ref[...] += _colsum8(e * e) * (0.5 / D_MODEL)

    return pl.pallas_call(
        body,
        name="loss_head",
        grid=(t // TM,),
        in_specs=[_rows(TM, D_MODEL), _resident((1, D_MODEL)), _rows(TM, D_MODEL)],
        out_specs=[_rows(TM, D_MODEL), pl.BlockSpec((8, D_MODEL), lambda i: (0, 0)),
                   pl.BlockSpec((8, D_MODEL), lambda i: (0, 0))],
        out_shape=[jax.ShapeDtypeStruct((t, D_MODEL), F32), jax.ShapeDtypeStruct((8, D_MODEL), F32),
                   jax.ShapeDtypeStruct((8, D_MODEL), F32)],
        compiler_params=_cparams(("arbitrary",)),
    )(x, gamma, target)


def _place():
    x, y, c = lax.axis_index("x"), lax.axis_index("y"), lax.axis_index("c")
    chips = [(1 - x, y), (x, 1 - y), (1 - x, 1 - y)]
    return x, y, c, chips


class _Gather:
    per = 8

    def __init__(self, shards):
        n = len(shards)
        self.inputs = list(shards)
        self.out_shape = [jax.ShapeDtypeStruct((N_DEV * s.shape[0], s.shape[1]), s.dtype) for s in shards]
        self.scratch = [pltpu.SemaphoreType.DMA((n * self.per,)), pltpu.SemaphoreType.DMA((n * self.per,)),
                        pltpu.SemaphoreType.DMA((n,))]
        self.result = None

    def _parts(self, ins, outs, sems):
        send_sems, recv_sems, local_sems = sems
        x, y, c, chips = _place()
        me, sibling = (x, y, c), (x, y, 1 - c)
        xn, yn, dg = chips
        n = len(ins)

        def rows(k, p, part=None):
            r = ins[k].shape[0]
            base = (4 * p[0] + 2 * p[1] + p[2]) * r
            if part is None:
                return outs[k].at[pl.ds(base, r), :]
            return outs[k].at[pl.ds(base + part * (r // 2), r // 2), :]

        def copy(k, slot, block, to, src=None, part=None):
            return pltpu.make_async_remote_copy(
                src_ref=rows(k, block, part) if src is None else src, dst_ref=rows(k, block, part),
                send_sem=send_sems.at[k * self.per + slot], recv_sem=recv_sems.at[k * self.per + slot],
                device_id=to, device_id_type=MESH)

        mine = [pltpu.make_async_copy(ins[k], rows(k, me), local_sems.at[k]) for k in range(n)]
        sends, lands = [], []
        for k in range(n):
            sends.append({
                0: copy(k, 0, me, sibling, src=ins[k]),
                1: copy(k, 1, me, (*xn, c), src=ins[k]),
                2: copy(k, 2, me, (*yn, c), src=ins[k]),
                3: copy(k, 3, (*xn, c), (*yn, c), part=0),
                4: copy(k, 4, (*yn, c), (*xn, c), part=1),
                5: copy(k, 5, (*xn, c), sibling),
                6: copy(k, 6, (*yn, c), sibling),
                7: copy(k, 7, (*dg, c), sibling)})
            lands.append({
                0: copy(k, 0, sibling, me),
                1: copy(k, 1, (*xn, c), me),
                2: copy(k, 2, (*yn, c), me),
                3: copy(k, 3, (*dg, c), me, part=0),
                4: copy(k, 4, (*dg, c), me, part=1),
                5: copy(k, 5, (*xn, 1 - c), me),
                6: copy(k, 6, (*yn, 1 - c), me),
                7: copy(k, 7, (*dg, 1 - c), me)})
        return n, mine, sends, lands

    def start(self, ins, outs, sems):
        n, mine, sends, _ = self._parts(ins, outs, sems)
        for cp in mine:
            cp.start()
        for slot in (0, 1, 2):
            for k in range(n):
                sends[k][slot].start()

    def relay(self, ins, outs, sems):
        n, _, sends, lands = self._parts(ins, outs, sems)
        for k in range(n):
            lands[k][1].wait_recv()
            sends[k][3].start()
            sends[k][5].start()
        for k in range(n):
            lands[k][2].wait_recv()
            sends[k][4].start()
            sends[k][6].start()

    def forward(self, ins, outs, sems):
        n, _, sends, lands = self._parts(ins, outs, sems)
        for k in range(n):
            lands[k][3].wait_recv()
            lands[k][4].wait_recv()
            sends[k][7].start()

    def finish(self, ins, outs, sems):
        n, mine, sends, lands = self._parts(ins, outs, sems)
        for k in range(n):
            for slot in (0, 5, 6, 7):
                lands[k][slot].wait_recv()
        for k in range(n):
            for slot in range(self.per):
                sends[k][slot].wait_send()
        for cp in mine:
            cp.wait()


class _PairExchange:
    def __init__(self, grads):
        n = len(grads)
        self.inputs = list(grads)
        self.out_shape = [jax.ShapeDtypeStruct((g.shape[0] // 2, g.shape[1]), g.dtype) for g in grads]
        self.scratch = [pltpu.SemaphoreType.DMA((n * N_CHIP,)), pltpu.SemaphoreType.DMA((n * N_CHIP,))]
        self.result = None

    def _copies(self, ins, outs, sems):
        send_sems, recv_sems = sems
        x, y, c, _ = _place()
        copies = []
        for k in range(len(ins)):
            r = ins[k].shape[0] // N_DEV
            for q in range(N_CHIP):
                copies.append(pltpu.make_async_remote_copy(
                    src_ref=ins[k].at[pl.ds((2 * q + 1 - c) * r, r), :], dst_ref=outs[k].at[pl.ds(q * r, r), :],
                    send_sem=send_sems.at[k * N_CHIP + q], recv_sem=recv_sems.at[k * N_CHIP + q],
                    device_id=(x, y, 1 - c), device_id_type=MESH))
        return copies

    def start(self, ins, outs, sems):
        for cp in self._copies(ins, outs, sems):
            cp.start()

    def relay(self, ins, outs, sems):
        pass

    def forward(self, ins, outs, sems):
        pass

    def finish(self, ins, outs, sems):
        copies = self._copies(ins, outs, sems)
        for cp in copies:
            cp.wait_recv()
        for cp in copies:
            cp.wait_send()


class _ChipExchange(_PairExchange):
    def __init__(self, psums):
        n = len(psums)
        self.inputs = list(psums)
        self.out_shape = [jax.ShapeDtypeStruct((3 * p.shape[0] // N_CHIP, p.shape[1]), p.dtype) for p in psums]
        self.scratch = [pltpu.SemaphoreType.DMA((n * 3,)), pltpu.SemaphoreType.DMA((n * 3,))]
        self.result = None

    def _copies(self, ins, outs, sems):
        send_sems, recv_sems = sems
        _, _, c, chips = _place()
        copies = []
        for k in range(len(ins)):
            r = ins[k].shape[0] // N_CHIP
            for j, chip in enumerate(chips):
                copies.append(pltpu.make_async_remote_copy(
                    src_ref=ins[k].at[pl.ds((2 * chip[0] + chip[1]) * r, r), :], dst_ref=outs[k].at[pl.ds(j * r, r), :],
                    send_sem=send_sems.at[k * 3 + j], recv_sem=recv_sems.at[k * 3 + j],
                    device_id=(*chip, c), device_id_type=MESH))
        return copies


def _exchange_alone(xchg, name):
    n_in, n_out = len(xchg.inputs), len(xchg.out_shape)

    def body(*refs):
        ins, outs, sems = refs[:n_in], refs[n_in:n_in + n_out], refs[n_in + n_out:]
        xchg.start(ins, outs, sems)
        xchg.relay(ins, outs, sems)
        xchg.forward(ins, outs, sems)
        xchg.finish(ins, outs, sems)

    xchg.result = list(pl.pallas_call(
        body, name=name, in_specs=[_hbm()] * n_in, out_specs=[_hbm()] * n_out, out_shape=xchg.out_shape,
        scratch_shapes=xchg.scratch)(*xchg.inputs))
    return xchg.result


def _pair_sum(core, grads, recvd, name):
    n = len(grads)
    r = grads[0].shape[0] // N_DEV
    cdim = grads[0].shape[1]
    tr = r // 2 if r % 32 == 0 else r
    nt = r // tr

    def body(core_ref, *refs):
        del core_ref
        for k in range(n):
            refs[2 * n + k][...] = (refs[k][...].astype(F32) + refs[n + k][...].astype(F32)).astype(BF16)

    gspec = pl.BlockSpec((tr, cdim), lambda q, i, core_ref: ((2 * q + core_ref[0]) * nt + i, 0))
    rspec = pl.BlockSpec((tr, cdim), lambda q, i, core_ref: (q * nt + i, 0))
    return pl.pallas_call(
        body,
        name=name,
        grid_spec=pltpu.PrefetchScalarGridSpec(
            num_scalar_prefetch=1, grid=(N_CHIP, nt), in_specs=[gspec] * n + [rspec] * n, out_specs=[rspec] * n),
        out_shape=[jax.ShapeDtypeStruct((N_CHIP * r, cdim), BF16) for _ in range(n)],
        compiler_params=_cparams(("parallel", "parallel")),
    )(core, *grads, *recvd)


def _final_sum(chip, psums, recvd, name):
    n = len(psums)
    r = psums[0].shape[0] // N_CHIP
    cdim = psums[0].shape[1]
    tr = r // 2 if r % 32 == 0 else r
    nt = r // tr

    def body(chip_ref, *refs):
        del chip_ref
        for k in range(n):
            got = refs[n + k]
            tot = refs[k][...].astype(F32) + got[0].astype(F32)
            tot = tot + got[1].astype(F32)
            tot = tot + got[2].astype(F32)
            refs[2 * n + k][...] = tot

    pspec = pl.BlockSpec((tr, cdim), lambda i, chip_ref: (chip_ref[0] * nt + i, 0))
    rspec = pl.BlockSpec((3, tr, cdim), lambda i, chip_ref: (0, i, 0))
    ospec = pl.BlockSpec((tr, cdim), lambda i, chip_ref: (i, 0))
    return pl.pallas_call(
        body,
        name=name,
        grid_spec=pltpu.PrefetchScalarGridSpec(
            num_scalar_prefetch=1, grid=(nt,), in_specs=[pspec] * n + [rspec] * n, out_specs=[ospec] * n),
        out_shape=[jax.ShapeDtypeStruct((r, cdim), F32) for _ in range(n)],
        compiler_params=_cparams(("parallel",)),
    )(chip, *psums, *[g.reshape(3, r, cdim) for g in recvd])


SMALL_ROWS = 16


def _all_reduce_small(part):
    def body(p_ref, o_ref, buf, send_sems, recv_sems):
        x, y, c, _ = _place()
        me = 4 * x + 2 * y + c
        buf[me] = p_ref[...]
        copies = []
        for d in range(1, N_DEV):
            peer = me ^ d
            copies.append(pltpu.make_async_remote_copy(
                src_ref=p_ref, dst_ref=buf.at[me], send_sem=send_sems.at[d - 1], recv_sem=recv_sems.at[d - 1],
                device_id=(peer // 4, (peer // 2) % 2, peer % 2), device_id_type=MESH))
        for cp in copies:
            cp.start()
        for cp in copies:
            cp.wait_recv()
        for cp in copies:
            cp.wait_send()
        tot = buf[0]
        for d in range(1, N_DEV):
            tot = tot + buf[d]
        o_ref[...] = tot

    return pl.pallas_call(
        body,
        name="all_reduce_small",
        in_specs=[pl.BlockSpec(memory_space=pltpu.VMEM)],
        out_specs=pl.BlockSpec(memory_space=pltpu.VMEM),
        out_shape=jax.ShapeDtypeStruct(part.shape, F32),
        scratch_shapes=[pltpu.VMEM((N_DEV,) + part.shape, F32), pltpu.SemaphoreType.DMA((N_DEV - 1,)),
                        pltpu.SemaphoreType.DMA((N_DEV - 1,))],
    )(part)


ADAMW_STEPS = 4


def _adamw(ws, gs, ms, vs, name, hosted=()):
    n = len(ws)
    steps = ADAMW_STEPS if all(w.shape[0] % (8 * ADAMW_STEPS) == 0 for w in ws) else 1
    c1 = 1.0 - ADAM_B1 ** ADAM_STEP
    c2 = 1.0 - ADAM_B2 ** ADAM_STEP

    def body(*refs):
        for k in range(n):
            w, g, m, v = (refs[j * n + k][...] for j in range(4))
            m2 = ADAM_B1 * m + (1.0 - ADAM_B1) * g
            v2 = ADAM_B2 * v + (1.0 - ADAM_B2) * (g * g)
            delta = -ADAM_LR * ((m2 / c1) / (jnp.sqrt(v2 / c2) + ADAM_EPS) + ADAM_WD * w)
            refs[4 * n + k][...] = delta
            refs[5 * n + k][...] = m2
            refs[6 * n + k][...] = v2

    specs = [pl.BlockSpec((w.shape[0] // steps, w.shape[1]), lambda i: (i, 0)) for w in ws]
    shapes = [jax.ShapeDtypeStruct(w.shape, F32) for w in ws]
    outs = _call(
        body,
        name=name,
        grid=(steps,),
        in_specs=specs * 4,
        out_specs=specs * 3,
        out_shape=shapes * 3,
        args=(*ws, *gs, *ms, *vs), sem=("parallel",), hosted=hosted)
    return outs[:n], outs[n:2 * n], outs[2 * n:]


def _adamw_reduced(chip, ws, psums, recvd, ms, vs, steps, name):
    n = len(ws)
    c1 = 1.0 - ADAM_B1 ** ADAM_STEP
    c2 = 1.0 - ADAM_B2 ** ADAM_STEP

    def body(chip_ref, *refs):
        del chip_ref
        for k in range(n):
            w, m, v = (refs[j * n + k][...] for j in (0, 3, 4))
            got = refs[2 * n + k]
            g = refs[n + k][...].astype(F32) + got[0].astype(F32)
            g = g + got[1].astype(F32)
            g = g + got[2].astype(F32)
            m2 = ADAM_B1 * m + (1.0 - ADAM_B1) * g
            v2 = ADAM_B2 * v + (1.0 - ADAM_B2) * (g * g)
            refs[5 * n + k][...] = g
            refs[6 * n + k][...] = -ADAM_LR * ((m2 / c1) / (jnp.sqrt(v2 / c2) + ADAM_EPS) + ADAM_WD * w)
            refs[7 * n + k][...] = m2
            refs[8 * n + k][...] = v2

    def blk(w):
        return (w.shape[0] // steps, w.shape[1])

    own = [pl.BlockSpec(blk(w), lambda i, chip_ref: (i, 0)) for w in ws]
    psum = [pl.BlockSpec(blk(w), lambda i, chip_ref: (chip_ref[0] * steps + i, 0)) for w in ws]
    recv = [pl.BlockSpec((3,) + blk(w), lambda i, chip_ref: (0, i, 0)) for w in ws]
    shapes = [jax.ShapeDtypeStruct(w.shape, F32) for w in ws]
    outs = pl.pallas_call(
        body,
        name=name,
        grid_spec=pltpu.PrefetchScalarGridSpec(
            num_scalar_prefetch=1, grid=(steps,), in_specs=own + psum + recv + own + own, out_specs=own * 4),
        out_shape=shapes * 4,
        compiler_params=_cparams(("parallel",)),
    )(chip, *ws, *psums, *[r.reshape((3,) + w.shape) for r, w in zip(recvd, ws)], *ms, *vs)
    return outs[:n], outs[n:2 * n], outs[2 * n:3 * n], outs[3 * n:]


def _bias_b():
    pad = B_PREV * CHUNK
    slopes = np.array([2.0 ** (-8.0 * (i + 1) / B_Q_HEADS) for i in range(B_Q_HEADS)], dtype=np.float32)
    dist = np.abs(np.arange(TQ)[:, None] - np.arange(TQ + pad)[None, :] + pad).astype(np.float32)
    bias = -slopes.reshape(B_Q_HEADS, 1, 1) * dist[None]
    qc = (np.arange(TQ)[:, None] + pad) // CHUNK
    kc = np.arange(TQ + pad)[None, :] // CHUNK
    allowed = (kc <= qc) & (kc >= qc - B_PREV)
    return np.where(allowed[None], bias, np.float32(NEG_INF)).astype(np.float32)


def kernel(x, ffn1_norm, ffn1_w_gate, ffn1_w_up, ffn1_w_down, mix_norm, w_in, rel_bias, sinks, w_proj_a, w_proj_b, w_out, ffn2_norm, ffn2_w_gate, ffn2_w_up, ffn2_w_down, final_norm, loss_target, m_ffn1_norm, m_ffn1_w_gate, m_ffn1_w_up, m_ffn1_w_down, m_mix_norm, m_w_in, m_rel_bias, m_sinks, m_w_proj_a, m_w_proj_b, m_w_out, m_ffn2_norm, m_ffn2_w_gate, m_ffn2_w_up, m_ffn2_w_down, m_final_norm, v_ffn1_norm, v_ffn1_w_gate, v_ffn1_w_up, v_ffn1_w_down, v_mix_norm, v_w_in, v_rel_bias, v_sinks, v_w_proj_a, v_w_proj_b, v_w_out, v_ffn2_norm, v_ffn2_w_gate, v_ffn2_w_up, v_ffn2_w_down, v_final_norm):
    bsz, s_len, _ = x.shape
    t = bsz * s_len
    core = lax.axis_index("c").astype(jnp.int32).reshape(1)
    chip = (2 * lax.axis_index("x") + lax.axis_index("y")).astype(jnp.int32).reshape(1)

    def row_form(w):
        return w.astype(BF16).T

    wg1, wu1 = _exchange_alone(_Gather([row_form(ffn1_w_gate), row_form(ffn1_w_up)]), "gather_ffn1")
    gather_down1 = _Gather([ffn1_w_down.astype(BF16), row_form(w_in)])
    gather_out = _Gather([jnp.concatenate([row_form(w_proj_a), row_form(w_proj_b)], axis=1), w_out.astype(BF16)])
    gather_ffn2_gate = _Gather([row_form(ffn2_w_gate)])
    gather_ffn2_rest = _Gather([row_form(ffn2_w_up), ffn2_w_down.astype(BF16)])

    x0 = x.reshape(t, D_MODEL)
    tgt = loss_target.reshape(t, D_MODEL)
    gam1, gam2, gam3, gam4 = (g.reshape(1, D_MODEL) for g in (ffn1_norm, mix_norm, ffn2_norm, final_norm))

    h1, g1, u1, a1 = _ffn_up(x0, gam1, wg1, wu1, "ffn1_up", hosted=[gather_down1])
    wd1, win_t = gather_down1.result
    x1 = _ffn_down(x0, a1, wd1, "ffn1_down", hosted=[gather_out])
    proj_t, wout = gather_out.result
    h2, qkv_a, qkv_b, gates = _proj_fwd(x1, gam2, win_t, hosted=[gather_ffn2_gate])
    (wg2,) = gather_ffn2_gate.result
    qkv_a3 = qkv_a.reshape(bsz, s_len, QKV_A)
    qkv_b3 = qkv_b.reshape(bsz, s_len, QKV_B)

    far = jnp.broadcast_to(rel_bias[:, REL_TABLE - 1:REL_TABLE], (A_HEADS, REL_WRAP // 2))
    tv = jnp.concatenate([far, jnp.flip(rel_bias, axis=1), jnp.zeros((A_HEADS, REL_WRAP // 2 - REL_TABLE), F32)], axis=1)
    bias_a = _bias_a_build(tv.reshape(A_HEADS, 1, REL_WRAP))
    bias_b = jnp.asarray(_bias_b())
    sink_rows = jnp.broadcast_to(sinks.reshape(B_Q_HEADS, 1, 1), (B_Q_HEADS, 8, LANES))

    oa = _attn_a_fwd(qkv_a3, bias_a, hosted=[gather_ffn2_rest]).reshape(t, A_WIDTH)
    wu2, wd2 = gather_ffn2_rest.result
    ob = _attn_b_fwd(qkv_b3, bias_b, sink_rows).reshape(t, B_Q_WIDTH)
    x2, ya, yb, mg = _mix_out_fwd(x1, oa, ob, gates, proj_t, wout)
    h3, g2, u2, a2, x3 = _ffn_fwd(x2, gam3, wg2, wu2, wd2, "ffn2_fwd")

    dx2, dg2, du2, db2, dgam3, dgam4, loss_part = _ffn_bwd_head(x3, gam4, tgt, x2, gam3, g2, u2, wg2, wu2, wd2,
                                                                "ffn2_bwd")
    gw_ffn2 = [_mm_tn([dg2], h3, "grad_ffn2_gate"), _mm_tn([du2], h3, "grad_ffn2_up"),
               _mm_tn([a2], db2, "grad_ffn2_down")]
    pairx_ffn2 = _PairExchange(gw_ffn2)
    dxb, dya, dyb, doa, dob, dgates = _mix_out_bwd(dx2, gates, ya, yb, proj_t, wout, hosted=[pairx_ffn2])
    psum_ffn2 = _pair_sum(core, gw_ffn2, pairx_ffn2.result, "pair_sum_ffn2")
    gw_out = _mm_tn([mg], dxb, "grad_w_out")
    gw_proj = _mm_tn_proj(dya, dyb, oa, ob)

    chipx_ffn2 = _ChipExchange(psum_ffn2)
    dqa, dka, dva, dbias_a = _attn_a_bwd(qkv_a3, bias_a, doa.reshape(bsz, s_len, A_WIDTH), hosted=[chipx_ffn2])
    pairx_out = _PairExchange([gw_proj, gw_out])
    dqb, dkvb, dsink = _attn_b_bwd(qkv_b3, bias_b, sink_rows, dob.reshape(bsz, s_len, B_Q_WIDTH), hosted=[pairx_out])
    drel_lanes = _relbias_grad(dbias_a)
    dproj = [dqa.reshape(t, A_WIDTH), dka.reshape(t, A_WIDTH), dva.reshape(t, A_WIDTH), dqb.reshape(t, B_Q_WIDTH),
             dkvb.reshape(t, 2 * B_KV_WIDTH), dgates]

    gw_in = _mm_tn(dproj, h2, "grad_w_in")
    pairx_in = _PairExchange([gw_in])
    psum_out = _pair_sum(core, [gw_proj, gw_out], pairx_out.result, "pair_sum_mix")
    chipx_out = _ChipExchange(psum_out)
    dx1, db1, dgam2 = _proj_bwd(dx2, x1, gam2, dproj, win_t, hosted=[pairx_in, chipx_out])
    psum_in = _pair_sum(core, [gw_in], pairx_in.result, "pair_sum_w_in")
    gw_d1 = _mm_tn([a1], db1, "grad_ffn1_down")

    chipx_in = _ChipExchange(psum_in)
    pairx_d1 = _PairExchange([gw_d1])
    dg1, du1 = _ffn_bwd_act(dx1, g1, u1, wd1, "ffn1_bwd_act", hosted=[chipx_in, pairx_d1])
    psum_d1 = _pair_sum(core, [gw_d1], pairx_d1.result, "pair_sum_ffn1_down")
    chipx_d1 = _ChipExchange(psum_d1)
    gw_g1 = _mm_tn([dg1], h1, "grad_ffn1_gate", hosted=[chipx_d1])
    from_sibling_g1 = _exchange_alone(_PairExchange([gw_g1]), "pair_exchange_ffn1_gate")
    psum_g1 = _pair_sum(core, [gw_g1], from_sibling_g1, "pair_sum_ffn1_gate")
    chipx_g1 = _ChipExchange(psum_g1)
    gw_u1 = _mm_tn([du1], h1, "grad_ffn1_up", hosted=[chipx_g1])
    from_sibling_u1 = _exchange_alone(_PairExchange([gw_u1]), "pair_exchange_ffn1_up")
    psum_u1 = _pair_sum(core, [gw_u1], from_sibling_u1, "pair_sum_ffn1_up")
    chipx_u1 = _ChipExchange(psum_u1)
    dx0, dgam1 = _ffn_bwd_in(dx1, x0, gam1, dg1, du1, wg1, wu1, "ffn1_bwd_in", hosted=[chipx_u1])

    (g_proj,) = _final_sum(chip, psum_out[0:1], chipx_out.result[0:1], "grad_sum_proj")
    grads = {"w_proj_a": g_proj[:, 0:A_WIDTH].T, "w_proj_b": g_proj[:, A_WIDTH:].T}

    def row_of(v):
        return jnp.pad(v.reshape(1, -1), ((0, 0), (0, D_MODEL - v.size)))

    def table_rows(v):
        return jnp.pad(v, ((0, 0), (0, D_MODEL - REL_TABLE)))

    drel_local = jnp.flip(drel_lanes[:, 0, 0:REL_TABLE], axis=1)
    small_part = jnp.concatenate(
        [jnp.sum(dgam1, axis=0, keepdims=True), jnp.sum(dgam2, axis=0, keepdims=True),
         jnp.sum(dgam3, axis=0, keepdims=True), jnp.sum(dgam4, axis=0, keepdims=True),
         row_of(jnp.sum(loss_part)), row_of(dsink[:, 0, 0]), jnp.zeros((2, D_MODEL), F32),
         table_rows(drel_local)], axis=0)
    small = _all_reduce_small(small_part)
    loss = small[4, 0]

    def pack(n1, n2, n3, n4, sk, tb):
        return jnp.concatenate([n1.reshape(1, -1), n2.reshape(1, -1), n3.reshape(1, -1), n4.reshape(1, -1),
                                jnp.zeros((1, D_MODEL), F32), row_of(sk), jnp.zeros((2, D_MODEL), F32), table_rows(tb)],
                               axis=0)

    live = np.zeros((SMALL_ROWS, D_MODEL), np.float32)
    live[0:4] = 1.0
    live[5, 0:B_Q_HEADS] = 1.0
    live[8:16, 0:REL_TABLE] = 1.0
    small_g = small * jnp.asarray(live)
    sw = pack(ffn1_norm, mix_norm, ffn2_norm, final_norm, sinks, rel_bias)
    sm = pack(m_ffn1_norm, m_mix_norm, m_ffn2_norm, m_final_norm, m_sinks, m_rel_bias)
    sv = pack(v_ffn1_norm, v_mix_norm, v_ffn2_norm, v_final_norm, v_sinks, v_rel_bias)
    (sd,), (snm,), (snv,) = _adamw([sw], [small_g], [sm], [sv], "adamw_small")

    def unpack(p):
        return {"ffn1_norm": p[0], "mix_norm": p[1], "ffn2_norm": p[2], "final_norm": p[3],
                "sinks": p[5, 0:B_Q_HEADS], "rel_bias": p[8:16, 0:REL_TABLE]}

    grads.update(unpack(small_g))
    delta, new_m, new_v = unpack(sd), unpack(snm), unpack(snv)

    wmv = {
        "ffn1_w_gate": (ffn1_w_gate, m_ffn1_w_gate, v_ffn1_w_gate), "ffn1_w_up": (ffn1_w_up, m_ffn1_w_up, v_ffn1_w_up),
        "ffn1_w_down": (ffn1_w_down, m_ffn1_w_down, v_ffn1_w_down), "w_in": (w_in, m_w_in, v_w_in),
        "w_proj_a": (w_proj_a, m_w_proj_a, v_w_proj_a), "w_proj_b": (w_proj_b, m_w_proj_b, v_w_proj_b),
        "w_out": (w_out, m_w_out, v_w_out),
        "ffn2_w_gate": (ffn2_w_gate, m_ffn2_w_gate, v_ffn2_w_gate), "ffn2_w_up": (ffn2_w_up, m_ffn2_w_up, v_ffn2_w_up),
        "ffn2_w_down": (ffn2_w_down, m_ffn2_w_down, v_ffn2_w_down),
    }
    row_form_names = ("ffn1_w_gate", "ffn1_w_up", "w_in", "ffn2_w_gate", "ffn2_w_up")

    def form(n, a):
        return a.T if n in row_form_names else a

    def reduced_group(gname, names, psums, recvd, steps):
        gs_, ds_, ms_, vs_ = _adamw_reduced(
            chip, [form(n, wmv[n][0]) for n in names], psums, recvd, [form(n, wmv[n][1]) for n in names],
            [form(n, wmv[n][2]) for n in names], steps, gname)
        for n, g_, d_, m_, v_ in zip(names, gs_, ds_, ms_, vs_):
            grads[n], delta[n], new_m[n], new_v[n] = form(n, g_), form(n, d_), form(n, m_), form(n, v_)

    reduced_group("adamw_ffn", ["ffn1_w_gate", "ffn1_w_up", "ffn1_w_down", "ffn2_w_gate", "ffn2_w_up", "ffn2_w_down"],
                  psum_g1 + psum_u1 + psum_d1 + psum_ffn2,
                  chipx_g1.result + chipx_u1.result + chipx_d1.result + chipx_ffn2.result, 11)
    reduced_group("adamw_w_in", ["w_in"], psum_in, chipx_in.result, 2)
    reduced_group("adamw_w_out", ["w_out"], psum_out[1:2], chipx_out.result[1:2], 2)
    names = ["w_proj_a", "w_proj_b"]
    ds_, ms_, vs_ = _adamw([wmv[n][0] for n in names], [grads[n] for n in names], [wmv[n][1] for n in names],
                           [wmv[n][2] for n in names], "adamw_proj")
    for n, d_, m_, v_ in zip(names, ds_, ms_, vs_):
        delta[n], new_m[n], new_v[n] = d_, m_, v_

    order = ["ffn1_norm", "ffn1_w_gate", "ffn1_w_up", "ffn1_w_down", "mix_norm", "w_in", "rel_bias", "sinks",
             "w_proj_a", "w_proj_b", "w_out", "ffn2_norm", "ffn2_w_gate", "ffn2_w_up", "ffn2_w_down", "final_norm"]
    grad_x = dx0.reshape(bsz, s_len, D_MODEL)
    return (loss, grad_x, *[grads[n] for n in order], *[delta[n] for n in order], *[new_m[n] for n in order],
            *[new_v[n] for n in order])
```

```python
import numpy as np
import jax
import jax.numpy as jnp
from jax import lax
from jax.experimental import pallas as pl
from jax.experimental.pallas import tpu as pltpu

F32 = jnp.float32
BF16 = jnp.bfloat16

D_MODEL = 1024
D_FF = 2816
CHUNK = 64
D_HEAD = 64
A_HEADS = 8
A_PREV = 8
MAX_REL = 128
B_Q_HEADS = 8
B_KV_HEADS = 2
B_GROUP = B_Q_HEADS // B_KV_HEADS
B_PREV = 2
REL_TABLE = (CHUNK - 1) + MAX_REL + 1
A_WIDTH = A_HEADS * D_HEAD
B_Q_WIDTH = B_Q_HEADS * D_HEAD
B_KV_WIDTH = B_KV_HEADS * D_HEAD
QKV_A = 3 * A_WIDTH
QKV_B = B_Q_WIDTH + 2 * B_KV_WIDTH
IN_WIDTH = QKV_A + QKV_B + 2 * D_MODEL
EPS = 1e-6
NEG_INF = -1e30
SCALE = 1.0 / 8.0

ADAM_LR = 0.001
ADAM_B1 = 0.9
ADAM_B2 = 0.999
ADAM_EPS = 1e-08
ADAM_WD = 0.01
ADAM_STEP = 10

N_DEV = 8
N_CHIP = 4
MESH = pl.DeviceIdType.MESH

LANES = 128
TQ = 256
TM = 256
FC = 256
VMEM_LIMIT = 56 << 20


def _cparams(sem, vmem=VMEM_LIMIT):
    return pltpu.CompilerParams(dimension_semantics=sem, vmem_limit_bytes=vmem)


def _dot_nt(a, b):
    return lax.dot_general(a, b, (((1,), (1,)), ((), ())), preferred_element_type=F32)


def _dot_nn(a, b):
    return lax.dot_general(a, b, (((1,), (0,)), ((), ())), preferred_element_type=F32)


def _dot_tn(a, b):
    return lax.dot_general(a, b, (((0,), (0,)), ((), ())), preferred_element_type=F32)


def _resident(shape):
    nd = len(shape)
    return pl.BlockSpec(shape, lambda *_: (0,) * nd, pipeline_mode=pl.Buffered(1))


def _rows(tm, width):
    return pl.BlockSpec((tm, width), lambda i: (i, 0))


def _colsum8(v):
    tm, n = v.shape
    return jnp.sum(v.reshape(tm // 8, 8, n), axis=0)


def _rms(x):
    r = lax.rsqrt(jnp.mean(x * x, axis=-1, keepdims=True) + EPS)
    return x * r, r


def _rms_bwd(dh, xh, r, gamma):
    dxh = dh * gamma
    dx = r * (dxh - xh * jnp.mean(dxh * xh, axis=-1, keepdims=True))
    return dx, _colsum8(dh * xh)


def _hbm():
    return pl.BlockSpec(memory_space=pltpu.HBM)


def _call(body, *, name, grid, in_specs, out_specs, out_shape, args, sem, scratch_shapes=(), hosted=()):
    in_specs, out_specs, out_shape = list(in_specs), list(out_specs), list(out_shape)
    scratch_shapes = list(scratch_shapes)
    if not hosted:
        return pl.pallas_call(body, name=name, grid=grid, in_specs=in_specs, out_specs=out_specs, out_shape=out_shape,
                              scratch_shapes=scratch_shapes, compiler_params=_cparams(sem))(*args)
    n_in, n_out, n_scr = len(in_specs), len(out_specs), len(scratch_shapes)
    x_in = [a for x in hosted for a in x.inputs]
    x_out = [s for x in hosted for s in x.out_shape]
    x_scr = [s for x in hosted for s in x.scratch]
    steps = int(np.prod(grid))
    forward_step = max(steps - 3, 0)
    relay_step = min((5 * steps) // 8, forward_step)

    def wrapped(*refs):
        pos = [0]

        def take(k):
            pos[0] += k
            return refs[pos[0] - k:pos[0]]

        ins, xin, outs, xout, scr, xscr = (take(k) for k in (n_in, len(x_in), n_out, len(x_out), n_scr, len(x_scr)))
        step = 0
        for axis, extent in enumerate(grid):
            step = step * extent + pl.program_id(axis)
        own, oi, oo, osc = [], 0, 0, 0
        for x in hosted:
            own.append((xin[oi:oi + len(x.inputs)], xout[oo:oo + len(x.out_shape)], xscr[osc:osc + len(x.scratch)]))
            oi, oo, osc = oi + len(x.inputs), oo + len(x.out_shape), osc + len(x.scratch)

        def phase(method):
            for x, (i_, o_, s_) in zip(hosted, own):
                getattr(x, method)(i_, o_, s_)

        pl.when(step == 0)(lambda: phase("start"))
        body(*ins, *outs, *scr)
        pl.when(step == relay_step)(lambda: phase("relay"))
        pl.when(step == forward_step)(lambda: phase("forward"))
        pl.when(step == steps - 1)(lambda: phase("finish"))

    res = pl.pallas_call(
        wrapped, name=name, grid=grid, in_specs=in_specs + [_hbm()] * len(x_in),
        out_specs=out_specs + [_hbm()] * len(x_out), out_shape=out_shape + x_out,
        scratch_shapes=scratch_shapes + x_scr, compiler_params=_cparams(("arbitrary",) * len(grid)))(*args, *x_in)
    rest = list(res[n_out:])
    for x in hosted:
        x.result, rest = rest[:len(x.out_shape)], rest[len(x.out_shape):]
    return list(res[:n_out])


def _ffn_fwd(x, gamma, wg_t, wu_t, wd, name, hosted=()):
    t = x.shape[0]
    f = wg_t.shape[0]

    def body(x_ref, gam_ref, wg_ref, wu_ref, wd_ref, h_ref, g_ref, u_ref, a_ref, y_ref):
        xv = x_ref[...]
        xh, _ = _rms(xv)
        h = (xh * gam_ref[...]).astype(BF16)
        h_ref[...] = h
        for j in range(f // FC):
            sl = slice(j * FC, (j + 1) * FC)
            g = _dot_nt(h, wg_ref[sl, :])
            u = _dot_nt(h, wu_ref[sl, :])
            g_ref[:, sl] = g.astype(BF16)
            u_ref[:, sl] = u.astype(BF16)
            a_ref[:, sl] = (g * jax.nn.sigmoid(g) * u).astype(BF16)
        y_ref[...] = xv + 0.5 * _dot_nn(a_ref[...], wd_ref[...])

    return _call(
        body,
        name=name,
        grid=(t // TM,),
        in_specs=[_rows(TM, D_MODEL), _resident((1, D_MODEL)), _resident((f, D_MODEL)), _resident((f, D_MODEL)),
                  _resident((f, D_MODEL))],
        out_specs=[_rows(TM, D_MODEL), _rows(TM, f), _rows(TM, f), _rows(TM, f),
                   _rows(TM, D_MODEL)],
        out_shape=[jax.ShapeDtypeStruct((t, D_MODEL), BF16), jax.ShapeDtypeStruct((t, f), BF16),
                   jax.ShapeDtypeStruct((t, f), BF16), jax.ShapeDtypeStruct((t, f), BF16),
                   jax.ShapeDtypeStruct((t, D_MODEL), F32)],
        args=(x, gamma, wg_t, wu_t, wd), sem=("parallel",), hosted=hosted)


def _ffn_up(x, gamma, wg_t, wu_t, name, hosted=()):
    t = x.shape[0]
    f = wg_t.shape[0]

    def body(x_ref, gam_ref, wg_ref, wu_ref, h_ref, g_ref, u_ref, a_ref):
        xh, _ = _rms(x_ref[...])
        h = (xh * gam_ref[...]).astype(BF16)
        h_ref[...] = h
        for j in range(f // FC):
            sl = slice(j * FC, (j + 1) * FC)
            g = _dot_nt(h, wg_ref[sl, :])
            u = _dot_nt(h, wu_ref[sl, :])
            g_ref[:, sl] = g.astype(BF16)
            u_ref[:, sl] = u.astype(BF16)
            a_ref[:, sl] = (g * jax.nn.sigmoid(g) * u).astype(BF16)

    return _call(
        body,
        name=name,
        grid=(t // TM,),
        in_specs=[_rows(TM, D_MODEL), _resident((1, D_MODEL)), _resident((f, D_MODEL)), _resident((f, D_MODEL))],
        out_specs=[_rows(TM, D_MODEL), _rows(TM, f), _rows(TM, f), _rows(TM, f)],
        out_shape=[jax.ShapeDtypeStruct((t, D_MODEL), BF16), jax.ShapeDtypeStruct((t, f), BF16),
                   jax.ShapeDtypeStruct((t, f), BF16), jax.ShapeDtypeStruct((t, f), BF16)],
        args=(x, gamma, wg_t, wu_t), sem=("parallel",), hosted=hosted)


def _ffn_down(x, a_act, wd, name, hosted=()):
    t = x.shape[0]
    f = wd.shape[0]

    def body(x_ref, a_ref, wd_ref, y_ref):
        y_ref[...] = x_ref[...] + 0.5 * _dot_nn(a_ref[...], wd_ref[...])

    return _call(
        body,
        name=name,
        grid=(t // TM,),
        in_specs=[_rows(TM, D_MODEL), _rows(TM, f), _resident((f, D_MODEL))],
        out_specs=[_rows(TM, D_MODEL)],
        out_shape=[jax.ShapeDtypeStruct((t, D_MODEL), F32)],
        args=(x, a_act, wd), sem=("parallel",), hosted=hosted)[0]


def _ffn_bwd_head(y, gamma_f, target, x, gamma, g_act, u_act, wg_t, wu_t, wd, name):
    t = x.shape[0]
    f = wg_t.shape[0]

    def body(y_ref, gamf_ref, t_ref, x_ref, gam_ref, g_ref, u_ref, wg_ref, wu_ref, wd_ref, dx_ref, dg_ref, du_ref,
             db_ref, dgam_ref, dgamf_ref, loss_ref):
        yh, ry = _rms(y_ref[...])
        gam_f = gamf_ref[...]
        e = yh * gam_f - t_ref[...]
        dv, dgam_f = _rms_bwd(e * (1.0 / D_MODEL), yh, ry, gam_f)
        db = (0.5 * dv).astype(BF16)
        db_ref[...] = db
        for j in range(f // FC):
            sl = slice(j * FC, (j + 1) * FC)
            da = _dot_nt(db, wd_ref[sl, :])
            g = g_ref[:, sl].astype(F32)
            u = u_ref[:, sl].astype(F32)
            s = jax.nn.sigmoid(g)
            dg_ref[:, sl] = (da * u * (s * (1.0 + g * (1.0 - s)))).astype(BF16)
            du_ref[:, sl] = (da * (g * s)).astype(BF16)
        dh = _dot_nn(dg_ref[...], wg_ref[...]) + _dot_nn(du_ref[...], wu_ref[...])
        xh, r = _rms(x_ref[...])
        dxn, dgam = _rms_bwd(dh, xh, r, gam_ref[...])
        dx_ref[...] = dv + dxn

        @pl.when(pl.program_id(0) == 0)
        def _():
            dgam_ref[...] = jnp.zeros_like(dgam_ref)
            dgamf_ref[...] = jnp.zeros_like(dgamf_ref)
            loss_ref[...] = jnp.zeros_like(loss_ref)

        dgam_ref[...] += dgam
        dgamf_ref[...] += dgam_f
        loss_ref[...] += _colsum8(e * e) * (0.5 / D_MODEL)

    acc = pl.BlockSpec((8, D_MODEL), lambda i: (0, 0))
    return _call(
        body,
        name=name,
        grid=(t // TM,),
        in_specs=[_rows(TM, D_MODEL), _resident((1, D_MODEL)), _rows(TM, D_MODEL), _rows(TM, D_MODEL),
                  _resident((1, D_MODEL)), _rows(TM, f), _rows(TM, f),
                  _resident((f, D_MODEL)), _resident((f, D_MODEL)), _resident((f, D_MODEL))],
        out_specs=[_rows(TM, D_MODEL), _rows(TM, f), _rows(TM, f), _rows(TM, D_MODEL), acc, acc, acc],
        out_shape=[jax.ShapeDtypeStruct((t, D_MODEL), F32), jax.ShapeDtypeStruct((t, f), BF16),
                   jax.ShapeDtypeStruct((t, f), BF16), jax.ShapeDtypeStruct((t, D_MODEL), BF16),
                   jax.ShapeDtypeStruct((8, D_MODEL), F32), jax.ShapeDtypeStruct((8, D_MODEL), F32),
                   jax.ShapeDtypeStruct((8, D_MODEL), F32)],
        args=(y, gamma_f, target, x, gamma, g_act, u_act, wg_t, wu_t, wd), sem=("arbitrary",))


def _ffn_bwd_act(d, g_act, u_act, wd, name, hosted=()):
    t = d.shape[0]
    f = wd.shape[0]

    def body(d_ref, g_ref, u_ref, wd_ref, dg_ref, du_ref):
        db = (0.5 * d_ref[...]).astype(BF16)
        for j in range(f // FC):
            sl = slice(j * FC, (j + 1) * FC)
            da = _dot_nt(db, wd_ref[sl, :])
            g = g_ref[:, sl].astype(F32)
            u = u_ref[:, sl].astype(F32)
            s = jax.nn.sigmoid(g)
            dg_ref[:, sl] = (da * u * (s * (1.0 + g * (1.0 - s)))).astype(BF16)
            du_ref[:, sl] = (da * (g * s)).astype(BF16)

    return _call(
        body,
        name=name,
        grid=(t // TM,),
        in_specs=[_rows(TM, D_MODEL), _rows(TM, f), _rows(TM, f), _resident((f, D_MODEL))],
        out_specs=[_rows(TM, f), _rows(TM, f)],
        out_shape=[jax.ShapeDtypeStruct((t, f), BF16), jax.ShapeDtypeStruct((t, f), BF16)],
        args=(d, g_act, u_act, wd), sem=("parallel",), hosted=hosted)


def _ffn_bwd_in(d, x, gamma, dg, du, wg_t, wu_t, name, hosted=()):
    t = x.shape[0]
    f = wg_t.shape[0]

    def body(d_ref, x_ref, gam_ref, dg_ref, du_ref, wg_ref, wu_ref, dx_ref, dgam_ref):
        dh = _dot_nn(dg_ref[...], wg_ref[...]) + _dot_nn(du_ref[...], wu_ref[...])
        xh, r = _rms(x_ref[...])
        dxn, dgam = _rms_bwd(dh, xh, r, gam_ref[...])
        dx_ref[...] = d_ref[...] + dxn

        @pl.when(pl.program_id(0) == 0)
        def _():
            dgam_ref[...] = jnp.zeros_like(dgam_ref)

        dgam_ref[...] += dgam

    return _call(
        body,
        name=name,
        grid=(t // TM,),
        in_specs=[_rows(TM, D_MODEL), _rows(TM, D_MODEL), _resident((1, D_MODEL)), _rows(TM, f), _rows(TM, f),
                  _resident((f, D_MODEL)), _resident((f, D_MODEL))],
        out_specs=[_rows(TM, D_MODEL), pl.BlockSpec((8, D_MODEL), lambda i: (0, 0))],
        out_shape=[jax.ShapeDtypeStruct((t, D_MODEL), F32), jax.ShapeDtypeStruct((8, D_MODEL), F32)],
        args=(d, x, gamma, dg, du, wg_t, wu_t), sem=("arbitrary",), hosted=hosted)


def _mm_tn(pieces, b, name, tile=256, hosted=()):
    t, n = b.shape
    npc = len(pieces)
    counts = [p.shape[1] // tile for p in pieces]
    los = [sum(counts[:k]) for k in range(npc)]
    total = sum(counts)

    def body(*refs):
        a_refs, b_ref, o_ref = refs[:npc], refs[npc], refs[npc + 1]
        i = pl.program_id(0)
        for k in range(npc):
            @pl.when(jnp.logical_and(i >= los[k], i < los[k] + counts[k]))
            def _(k=k):
                o_ref[...] = _dot_tn(a_refs[k][...], b_ref[...]).astype(BF16)

    def a_spec(k):
        return pl.BlockSpec((t, tile), lambda i: (0, jnp.clip(i - los[k], 0, counts[k] - 1)))

    return _call(
        body,
        name=name,
        grid=(total,),
        in_specs=[a_spec(k) for k in range(npc)] + [_resident((t, n))],
        out_specs=[pl.BlockSpec((tile, n), lambda i: (i, 0))],
        out_shape=[jax.ShapeDtypeStruct((total * tile, n), BF16)],
        args=(*pieces, b), sem=("parallel",), hosted=hosted)[0]


def _proj_fwd(x, gamma, win_t, hosted=()):
    t = x.shape[0]

    def body(x_ref, gam_ref, w_ref, h_ref, qa_ref, qb_ref, gt_ref):
        xh, _ = _rms(x_ref[...])
        h = (xh * gam_ref[...]).astype(BF16)
        h_ref[...] = h
        for j in range(QKV_A // FC):
            qa_ref[:, j * FC:(j + 1) * FC] = _dot_nt(h, w_ref[j * FC:(j + 1) * FC, :]).astype(BF16)
        for j in range(QKV_B // FC):
            lo = QKV_A + j * FC
            qb_ref[:, j * FC:(j + 1) * FC] = _dot_nt(h, w_ref[lo:lo + FC, :]).astype(BF16)
        for j in range(2 * D_MODEL // FC):
            lo = QKV_A + QKV_B + j * FC
            gt_ref[:, j * FC:(j + 1) * FC] = _dot_nt(h, w_ref[lo:lo + FC, :])

    return _call(
        body,
        name="proj_fwd",
        grid=(t // TM,),
        in_specs=[_rows(TM, D_MODEL), _resident((1, D_MODEL)), _resident((IN_WIDTH, D_MODEL))],
        out_specs=[_rows(TM, D_MODEL), _rows(TM, QKV_A), _rows(TM, QKV_B), _rows(TM, 2 * D_MODEL)],
        out_shape=[jax.ShapeDtypeStruct((t, D_MODEL), BF16), jax.ShapeDtypeStruct((t, QKV_A), BF16),
                   jax.ShapeDtypeStruct((t, QKV_B), BF16), jax.ShapeDtypeStruct((t, 2 * D_MODEL), F32)],
        args=(x, gamma, win_t), sem=("parallel",), hosted=hosted)


def _proj_bwd(d, x, gamma, pieces, win_t, hosted=()):
    t = x.shape[0]
    npc = len(pieces)
    widths = [p.shape[1] for p in pieces]
    los = [sum(widths[:k]) for k in range(npc)]

    def body(*refs):
        d_ref, x_ref, gam_ref = refs[:3]
        p_refs = refs[3:3 + npc]
        w_ref, dx_ref, db_ref, dgam_ref = refs[3 + npc:]
        dh = _dot_nn(p_refs[0][...], w_ref[0:widths[0], :])
        for k in range(1, npc):
            dh += _dot_nn(p_refs[k][...], w_ref[los[k]:los[k] + widths[k], :])
        xh, r = _rms(x_ref[...])
        dxn, dgam = _rms_bwd(dh, xh, r, gam_ref[...])
        dx = d_ref[...] + dxn
        dx_ref[...] = dx
        db_ref[...] = (0.5 * dx).astype(BF16)

        @pl.when(pl.program_id(0) == 0)
        def _():
            dgam_ref[...] = jnp.zeros_like(dgam_ref)

        dgam_ref[...] += dgam

    return _call(
        body,
        name="proj_bwd",
        grid=(t // TM,),
        in_specs=[_rows(TM, D_MODEL), _rows(TM, D_MODEL), _resident((1, D_MODEL))] + [_rows(TM, w) for w in widths]
        + [_resident((IN_WIDTH, D_MODEL))],
        out_specs=[_rows(TM, D_MODEL), _rows(TM, D_MODEL), pl.BlockSpec((8, D_MODEL), lambda i: (0, 0))],
        out_shape=[jax.ShapeDtypeStruct((t, D_MODEL), F32), jax.ShapeDtypeStruct((t, D_MODEL), BF16),
                   jax.ShapeDtypeStruct((8, D_MODEL), F32)],
        args=(d, x, gamma, *pieces, win_t), sem=("arbitrary",), hosted=hosted)


def _lane_half(shape):
    return lax.broadcasted_iota(jnp.int32, shape, len(shape) - 1) // D_HEAD


def _band_weights(q, kk, bias, sink, qs, pad):
    s = _dot_nt(q, kk) + bias
    if qs is not None:
        col = lax.broadcasted_iota(jnp.int32, s.shape, 1)
        s = jnp.where(col + qs >= pad, s, NEG_INF)
    m = jnp.max(s, axis=-1, keepdims=True)
    if sink is not None:
        m = jnp.maximum(m, sink)
    return jnp.exp(s - m), m


def _weighted_values(p, vv_ones, sink, m):
    r = _dot_nn(p.astype(BF16), vv_ones)
    den = r[:, LANES:2 * LANES]
    if sink is not None:
        den = den + jnp.exp(sink - m)
    return r[:, 0:LANES] / den


def _band_softmax(q, kk, bias, sink, qs, pad):
    p, m = _band_weights(q, kk, bias, sink, qs, pad)
    den = jnp.sum(p, axis=-1, keepdims=True)
    if sink is not None:
        den = den + jnp.exp(sink - m)
    return p, m, 1.0 / den


def _fill_padded(dst, src, pad):
    dst[0:pad, :] = jnp.zeros((pad,) + dst.shape[1:], dst.dtype)
    dst[pad:, :] = src


FWD_PAIRS = 4
BWD_PAIRS = 2


def _attn_a_fwd(qkv, bias, hosted=()):
    bsz, s_len, _ = qkv.shape
    pad = A_PREV * CHUNK
    band = TQ + pad
    pp = FWD_PAIRS
    w = pp * LANES
    nb = A_WIDTH // w

    def body(q_ref, k_ref, v_ref, b_ref, o_ref, kp, vp):
        i = pl.program_id(2)

        @pl.when(i == 0)
        def _():
            _fill_padded(kp, k_ref[...], pad)
            _fill_padded(vp, v_ref[...], pad)

        qs = pl.multiple_of(i * TQ, TQ)
        half = _lane_half((1, LANES))

        ones = jnp.ones((band, LANES), BF16)

        def block(masked):
            for pr in range(pp):
                sl = slice(pr * LANES, (pr + 1) * LANES)
                kk = kp[pl.ds(qs, band), sl]
                vv = jnp.concatenate([vp[pl.ds(qs, band), sl], ones], axis=1)
                q = q_ref[:, sl] * SCALE
                outs = []
                for j in range(2):
                    qm = jnp.where(half == j, q, jnp.zeros_like(q))
                    p, m = _band_weights(qm, kk, b_ref[2 * pr + j], None, qs if masked else None, pad)
                    outs.append(_weighted_values(p, vv, None, m))
                o_ref[:, sl] = jnp.where(half == 0, outs[0], outs[1]).astype(BF16)

        pl.when(i < pad // TQ)(lambda: block(True))
        pl.when(i >= pad // TQ)(lambda: block(False))

    return _call(
        body,
        name="attn_a_fwd",
        grid=(bsz, nb, s_len // TQ),
        in_specs=[pl.BlockSpec((None, TQ, w), lambda b, g, i: (b, i, g)),
                  pl.BlockSpec((None, s_len, w), lambda b, g, i: (b, 0, nb + g)),
                  pl.BlockSpec((None, s_len, w), lambda b, g, i: (b, 0, 2 * nb + g)),
                  pl.BlockSpec((2 * pp, TQ, band), lambda b, g, i: (g, 0, 0))],
        out_specs=[pl.BlockSpec((None, TQ, w), lambda b, g, i: (b, i, g))],
        out_shape=[jax.ShapeDtypeStruct((bsz, s_len, A_WIDTH), BF16)],
        scratch_shapes=[pltpu.VMEM((pad + s_len, w), BF16), pltpu.VMEM((pad + s_len, w), BF16)],
        args=(qkv, qkv, qkv, bias), sem=("arbitrary", "arbitrary", "arbitrary"), hosted=hosted)[0]


def _attn_a_bwd(qkv, bias, do, hosted=()):
    bsz, s_len, _ = qkv.shape
    pad = A_PREV * CHUNK
    band = TQ + pad
    n_i = s_len // TQ
    pp = BWD_PAIRS
    w = pp * LANES
    nb = A_WIDTH // w

    def body(q_ref, k_ref, v_ref, b_ref, do_ref, dq_ref, dk_ref, dv_ref, dbias_ref, kp, vp, dk_acc, dv_acc):
        b = pl.program_id(1)
        i = pl.program_id(2)

        @pl.when(i == 0)
        def _():
            _fill_padded(kp, k_ref[...], pad)
            _fill_padded(vp, v_ref[...], pad)
            dk_acc[...] = jnp.zeros_like(dk_acc)
            dv_acc[...] = jnp.zeros_like(dv_acc)

        @pl.when(jnp.logical_and(b == 0, i == 0))
        def _():
            dbias_ref[...] = jnp.zeros_like(dbias_ref)

        qs = pl.multiple_of(i * TQ, TQ)
        half = _lane_half((1, LANES))

        def block(masked):
            for pr in range(pp):
                sl = slice(pr * LANES, (pr + 1) * LANES)
                kk = kp[pl.ds(qs, band), sl]
                vv = vp[pl.ds(qs, band), sl]
                q = q_ref[:, sl] * SCALE
                dd = do_ref[:, sl]
                dqs, dks, dvs = [], [], []
                for j in range(2):
                    qm = jnp.where(half == j, q, jnp.zeros_like(q))
                    dm = jnp.where(half == j, dd, jnp.zeros_like(dd))
                    p, _, inv = _band_softmax(qm, kk, b_ref[2 * pr + j], None, qs if masked else None, pad)
                    pn = p * inv
                    dp = _dot_nt(dm, vv)
                    delta = jnp.sum(pn * dp, axis=-1, keepdims=True)
                    ds = pn * (dp - delta)
                    dbias_ref[2 * pr + j] += ds[:, band - REL_COLS:]
                    dsb = ds.astype(BF16)
                    dqs.append(_dot_nn(dsb, kk))
                    dks.append(_dot_tn(dsb, q))
                    dvs.append(_dot_tn(pn.astype(BF16), dd))
                dq_ref[:, sl] = (jnp.where(half == 0, dqs[0], dqs[1]) * SCALE).astype(BF16)
                dk_acc[pl.ds(qs, band), sl] += jnp.where(half == 0, dks[0], dks[1])
                dv_acc[pl.ds(qs, band), sl] += jnp.where(half == 0, dvs[0], dvs[1])

        pl.when(i < pad // TQ)(lambda: block(True))
        pl.when(i >= pad // TQ)(lambda: block(False))

        @pl.when(i == n_i - 1)
        def _():
            dk_ref[...] = dk_acc[pad:, :].astype(BF16)
            dv_ref[...] = dv_acc[pad:, :].astype(BF16)

    qspec = pl.BlockSpec((None, TQ, w), lambda g, b, i: (b, i, g))
    kvout = pl.BlockSpec((None, s_len, w), lambda g, b, i: (b, 0, g))
    wide = jax.ShapeDtypeStruct((bsz, s_len, A_WIDTH), BF16)
    return _call(
        body,
        name="attn_a_bwd",
        grid=(nb, bsz, n_i),
        in_specs=[qspec,
                  pl.BlockSpec((None, s_len, w), lambda g, b, i: (b, 0, nb + g)),
                  pl.BlockSpec((None, s_len, w), lambda g, b, i: (b, 0, 2 * nb + g)),
                  pl.BlockSpec((2 * pp, TQ, band), lambda g, b, i: (g, 0, 0)),
                  qspec],
        out_specs=[qspec, kvout, kvout, pl.BlockSpec((2 * pp, TQ, REL_COLS), lambda g, b, i: (g, 0, 0))],
        out_shape=[wide, wide, wide, jax.ShapeDtypeStruct((A_HEADS, TQ, REL_COLS), F32)],
        scratch_shapes=[pltpu.VMEM((pad + s_len, w), BF16), pltpu.VMEM((pad + s_len, w), BF16),
                        pltpu.VMEM((pad + s_len, w), F32), pltpu.VMEM((pad + s_len, w), F32)],
        args=(qkv, qkv, qkv, bias, do), sem=("arbitrary", "arbitrary", "arbitrary"), hosted=hosted)


def _fill_padded_dup(dst, src, pad, h, half):
    other = pltpu.roll(src, D_HEAD, 1)
    _fill_padded(dst, jnp.where(half == h, src, other), pad)


def _attn_b_fwd(qkv, bias, sink):
    bsz, s_len, _ = qkv.shape
    pad = B_PREV * CHUNK
    band = TQ + pad
    kcol = B_Q_WIDTH // LANES
    npair = B_Q_HEADS // 2

    def body(q_ref, k_ref, v_ref, b_ref, s_ref, o_ref, kp, vp):
        i = pl.program_id(1)
        half = _lane_half((1, LANES))

        @pl.when(i == 0)
        def _():
            for h in range(B_KV_HEADS):
                _fill_padded_dup(kp.at[h], k_ref[...], pad, h, half)
                _fill_padded_dup(vp.at[h], v_ref[...], pad, h, half)

        qs = pl.multiple_of(i * TQ, TQ)

        ones = jnp.ones((band, LANES), BF16)

        def block(masked):
            for pr in range(npair):
                h = pr // (B_GROUP // 2)
                sl = slice(pr * LANES, (pr + 1) * LANES)
                kk = kp[h, pl.ds(qs, band), :]
                vv = jnp.concatenate([vp[h, pl.ds(qs, band), :], ones], axis=1)
                q = q_ref[:, sl] * SCALE
                outs = []
                for j in range(2):
                    qm = jnp.where(half == j, q, jnp.zeros_like(q))
                    sink = s_ref[2 * pr + j][0:1, 0:1]
                    p, m = _band_weights(qm, kk, b_ref[2 * pr + j], sink, qs if masked else None, pad)
                    outs.append(_weighted_values(p, vv, sink, m))
                o_ref[:, sl] = jnp.where(half == 0, outs[0], outs[1]).astype(BF16)

        pl.when(i < -(-pad // TQ))(lambda: block(True))
        pl.when(i >= -(-pad // TQ))(lambda: block(False))

    return pl.pallas_call(
        body,
        name="attn_b_fwd",
        grid=(bsz, s_len // TQ),
        in_specs=[pl.BlockSpec((None, TQ, B_Q_WIDTH), lambda b, i: (b, i, 0)),
                  pl.BlockSpec((None, s_len, LANES), lambda b, i: (b, 0, kcol)),
                  pl.BlockSpec((None, s_len, LANES), lambda b, i: (b, 0, kcol + 1)),
                  pl.BlockSpec((B_Q_HEADS, TQ, band), lambda b, i: (0, 0, 0)),
                  pl.BlockSpec((B_Q_HEADS, 8, LANES), lambda b, i: (0, 0, 0))],
        out_specs=pl.BlockSpec((None, TQ, B_Q_WIDTH), lambda b, i: (b, i, 0)),
        out_shape=jax.ShapeDtypeStruct((bsz, s_len, B_Q_WIDTH), BF16),
        scratch_shapes=[pltpu.VMEM((B_KV_HEADS, pad + s_len, LANES), BF16),
                        pltpu.VMEM((B_KV_HEADS, pad + s_len, LANES), BF16)],
        compiler_params=_cparams(("arbitrary", "arbitrary")),
    )(qkv, qkv, qkv, bias, sink)


def _attn_b_bwd(qkv, bias, sink, do, hosted=()):
    bsz, s_len, _ = qkv.shape
    pad = B_PREV * CHUNK
    band = TQ + pad
    kcol = B_Q_WIDTH // LANES
    npair = B_Q_HEADS // 2
    n_i = s_len // TQ

    pp = B_GROUP // 2
    w = pp * LANES

    def body(q_ref, k_ref, v_ref, b_ref, s_ref, do_ref, dq_ref, dkv_ref, dsink_ref, kp, vp, dk_acc, dv_acc):
        b = pl.program_id(0)
        h = pl.program_id(1)
        i = pl.program_id(2)
        half = _lane_half((1, LANES))

        @pl.when(i == 0)
        def _():
            _fill_padded_dup(kp, k_ref[...], pad, h, half)
            _fill_padded_dup(vp, v_ref[...], pad, h, half)

        @pl.when(jnp.logical_and(h == 0, i == 0))
        def _():
            dk_acc[...] = jnp.zeros_like(dk_acc)
            dv_acc[...] = jnp.zeros_like(dv_acc)

        @pl.when(jnp.logical_and(b == 0, jnp.logical_and(h == 0, i == 0)))
        def _():
            dsink_ref[...] = jnp.zeros_like(dsink_ref)

        qs = pl.multiple_of(i * TQ, TQ)

        def block(masked):
            kk = kp[pl.ds(qs, band), :]
            vv = vp[pl.ds(qs, band), :]
            dk2 = jnp.zeros((band, LANES), F32)
            dv2 = jnp.zeros((band, LANES), F32)
            for pr in range(pp):
                sl = slice(pr * LANES, (pr + 1) * LANES)
                q = q_ref[:, sl] * SCALE
                dd = do_ref[:, sl]
                dqs, dks, dvs = [], [], []
                for j in range(2):
                    qm = jnp.where(half == j, q, jnp.zeros_like(q))
                    dm = jnp.where(half == j, dd, jnp.zeros_like(dd))
                    sink = s_ref[2 * pr + j][0:1, 0:1]
                    p, m, inv = _band_softmax(qm, kk, b_ref[2 * pr + j], sink, qs if masked else None, pad)
                    pn = p * inv
                    dp = _dot_nt(dm, vv)
                    delta = jnp.sum(pn * dp, axis=-1, keepdims=True)
                    ds = pn * (dp - delta)
                    dsb = ds.astype(BF16)
                    dqs.append(_dot_nn(dsb, kk))
                    dks.append(_dot_tn(dsb, q))
                    dvs.append(_dot_tn(pn.astype(BF16), dd))
                    dsk = jnp.sum(-(jnp.exp(sink - m) * inv) * delta, axis=0, keepdims=True)
                    dsink_ref[2 * pp * h + 2 * pr + j] += jnp.broadcast_to(dsk, (8, LANES))
                dq_ref[:, sl] = (jnp.where(half == 0, dqs[0], dqs[1]) * SCALE).astype(BF16)
                dk2 = dk2 + jnp.where(half == 0, dks[0], dks[1])
                dv2 = dv2 + jnp.where(half == 0, dvs[0], dvs[1])
            dk_acc[pl.ds(qs, band), :] += jnp.where(half == h, dk2 + pltpu.roll(dk2, D_HEAD, 1), 0.0)
            dv_acc[pl.ds(qs, band), :] += jnp.where(half == h, dv2 + pltpu.roll(dv2, D_HEAD, 1), 0.0)

        pl.when(i < -(-pad // TQ))(lambda: block(True))
        pl.when(i >= -(-pad // TQ))(lambda: block(False))

        @pl.when(jnp.logical_and(h == B_KV_HEADS - 1, i == n_i - 1))
        def _():
            dkv_ref[:, 0:LANES] = dk_acc[pad:, :].astype(BF16)
            dkv_ref[:, LANES:2 * LANES] = dv_acc[pad:, :].astype(BF16)

    qspec = pl.BlockSpec((None, TQ, w), lambda b, h, i: (b, i, h))
    return _call(
        body,
        name="attn_b_bwd",
        grid=(bsz, B_KV_HEADS, n_i),
        in_specs=[qspec,
                  pl.BlockSpec((None, s_len, LANES), lambda b, h, i: (b, 0, kcol)),
                  pl.BlockSpec((None, s_len, LANES), lambda b, h, i: (b, 0, kcol + 1)),
                  pl.BlockSpec((2 * pp, TQ, band), lambda b, h, i: (h, 0, 0)),
                  pl.BlockSpec((2 * pp, 8, LANES), lambda b, h, i: (h, 0, 0)),
                  qspec],
        out_specs=[qspec, pl.BlockSpec((None, s_len, 2 * LANES), lambda b, h, i: (b, 0, 0)),
                   pl.BlockSpec((B_Q_HEADS, 8, LANES), lambda b, h, i: (0, 0, 0))],
        out_shape=[jax.ShapeDtypeStruct((bsz, s_len, B_Q_WIDTH), BF16),
                   jax.ShapeDtypeStruct((bsz, s_len, 2 * B_KV_WIDTH), BF16),
                   jax.ShapeDtypeStruct((B_Q_HEADS, 8, LANES), F32)],
        scratch_shapes=[pltpu.VMEM((pad + s_len, LANES), BF16), pltpu.VMEM((pad + s_len, LANES), BF16),
                        pltpu.VMEM((pad + s_len, LANES), F32), pltpu.VMEM((pad + s_len, LANES), F32)],
        args=(qkv, qkv, qkv, bias, sink, do), sem=("arbitrary", "arbitrary", "arbitrary"), hosted=hosted)


REL_COLS = 3 * 128
REL_WRAP = 512


def _bias_a_build(tv, hosted=()):
    h = tv.shape[0]
    pad = A_PREV * CHUNK
    band = TQ + pad

    def body(tv_ref, o_ref):
        row = tv_ref[...]
        x = jnp.broadcast_to(row, (TQ, REL_WRAP))
        r = lax.broadcasted_iota(jnp.int32, x.shape, 0)
        for bit in range(8):
            sh = 1 << bit
            x = jnp.where((r & sh) != 0, pltpu.roll(x, sh, 1), x)
        far = jnp.broadcast_to(row[:, 0:1], (TQ, band - REL_COLS))
        full = jnp.concatenate([far, x[:, REL_WRAP // 2:REL_WRAP], x[:, 0:REL_COLS - REL_WRAP // 2]], axis=1)
        qc = (lax.broadcasted_iota(jnp.int32, full.shape, 0) + pad) // CHUNK
        kc = lax.broadcasted_iota(jnp.int32, full.shape, 1) // CHUNK
        ok = jnp.logical_and(kc <= qc, kc >= qc - A_PREV)
        o_ref[...] = jnp.where(ok, full, NEG_INF)

    return _call(
        body,
        name="bias_a_build",
        grid=(h,),
        in_specs=[pl.BlockSpec((None, 1, REL_WRAP), lambda hh: (hh, 0, 0))],
        out_specs=[pl.BlockSpec((None, TQ, band), lambda hh: (hh, 0, 0))],
        out_shape=[jax.ShapeDtypeStruct((h, TQ, band), F32)],
        args=(tv,), sem=("parallel",), hosted=hosted)[0]


def _relbias_grad(dbias):
    h, rows, _ = dbias.shape

    def body(d_ref, o_ref):
        x = d_ref[...]
        r = lax.broadcasted_iota(jnp.int32, x.shape, 0)
        c = lax.broadcasted_iota(jnp.int32, x.shape, 1) - r
        x = jnp.where(jnp.logical_and(c >= 1, c < REL_TABLE), x, 0.0)
        for bit in range(8):
            sh = 1 << bit
            x = jnp.where((r & sh) != 0, pltpu.roll(x, REL_COLS - sh, 1), x)
        diag = jnp.sum(x, axis=0, keepdims=True)
        lane = lax.broadcasted_iota(jnp.int32, diag.shape, 1)
        diag = jnp.where(jnp.logical_and(lane >= 1, lane < REL_TABLE), diag, 0.0)
        rest = -jnp.sum(diag, axis=1, keepdims=True)
        o_ref[...] = jnp.broadcast_to(jnp.where(lane == 0, rest, diag), o_ref.shape)

    return pl.pallas_call(
        body,
        name="relbias_grad",
        grid=(h,),
        in_specs=[pl.BlockSpec((None, rows, REL_COLS), lambda hh: (hh, 0, 0))],
        out_specs=pl.BlockSpec((None, 8, REL_COLS), lambda hh: (hh, 0, 0)),
        out_shape=jax.ShapeDtypeStruct((h, 8, REL_COLS), F32),
        compiler_params=_cparams(("parallel",)),
    )(dbias)


def _mix_out_fwd(x, oa, ob, gates, proj_t, wout):
    t = x.shape[0]

    def body(x_ref, oa_ref, ob_ref, gt_ref, pt_ref, wo_ref, y_ref, ya_ref, yb_ref, mg_ref):
        ya = _dot_nt(oa_ref[...], pt_ref[:, 0:A_WIDTH])
        yb = _dot_nt(ob_ref[...], pt_ref[:, A_WIDTH:A_WIDTH + B_Q_WIDTH])
        ya_ref[...] = ya.astype(BF16)
        yb_ref[...] = yb.astype(BF16)
        mg = jax.nn.sigmoid(gt_ref[:, 0:D_MODEL]) * ya + jax.nn.sigmoid(gt_ref[:, D_MODEL:2 * D_MODEL]) * yb
        mgb = mg.astype(BF16)
        mg_ref[...] = mgb
        y_ref[...] = x_ref[...] + _dot_nn(mgb, wo_ref[...])

    return pl.pallas_call(
        body,
        name="mix_out_fwd",
        grid=(t // TM,),
        in_specs=[_rows(TM, D_MODEL), _rows(TM, A_WIDTH), _rows(TM, B_Q_WIDTH), _rows(TM, 2 * D_MODEL),
                  _resident((D_MODEL, A_WIDTH + B_Q_WIDTH)), _resident((D_MODEL, D_MODEL))],
        out_specs=[_rows(TM, D_MODEL), _rows(TM, D_MODEL), _rows(TM, D_MODEL), _rows(TM, D_MODEL)],
        out_shape=[jax.ShapeDtypeStruct((t, D_MODEL), F32), jax.ShapeDtypeStruct((t, D_MODEL), BF16),
                   jax.ShapeDtypeStruct((t, D_MODEL), BF16), jax.ShapeDtypeStruct((t, D_MODEL), BF16)],
        compiler_params=_cparams(("parallel",)),
    )(x, oa, ob, gates, proj_t, wout)


def _mix_out_bwd(d, gates, ya, yb, mg, oa, ob, proj_t, wout, hosted=()):
    t = d.shape[0]
    nt = t // TM

    def body(d_ref, gt_ref, ya_ref, yb_ref, mg_ref, oa_ref, ob_ref, pt_ref, wo_ref,
             doa_ref, dob_ref, dgt_ref, gwo_ref, gwp_ref, acc_o, acc_p):
        i = pl.program_id(0)
        db = d_ref[...].astype(BF16)
        dmg = _dot_nt(db, wo_ref[...])
        sa = jax.nn.sigmoid(gt_ref[:, 0:D_MODEL])
        sb = jax.nn.sigmoid(gt_ref[:, D_MODEL:2 * D_MODEL])
        dya = (dmg * sa).astype(BF16)
        dyb = (dmg * sb).astype(BF16)
        dgt_ref[:, 0:D_MODEL] = (dmg * ya_ref[...].astype(F32) * (sa * (1.0 - sa))).astype(BF16)
        dgt_ref[:, D_MODEL:2 * D_MODEL] = (dmg * yb_ref[...].astype(F32) * (sb * (1.0 - sb))).astype(BF16)
        doa_ref[...] = _dot_nn(dya, pt_ref[:, 0:A_WIDTH]).astype(BF16)
        dob_ref[...] = _dot_nn(dyb, pt_ref[:, A_WIDTH:A_WIDTH + B_Q_WIDTH]).astype(BF16)

        @pl.when(i == 0)
        def _():
            acc_o[...] = jnp.zeros_like(acc_o)
            acc_p[...] = jnp.zeros_like(acc_p)

        acc_o[...] += _dot_tn(mg_ref[...], db)
        acc_p[:, 0:A_WIDTH] += _dot_tn(dya, oa_ref[...])
        acc_p[:, A_WIDTH:A_WIDTH + B_Q_WIDTH] += _dot_tn(dyb, ob_ref[...])

        @pl.when(i == nt - 1)
        def _():
            gwo_ref[...] = acc_o[...].astype(BF16)
            gwp_ref[...] = acc_p[...].astype(BF16)

    whole = pl.BlockSpec((D_MODEL, D_MODEL), lambda i: (0, 0))
    return _call(
        body,
        name="mix_out_bwd",
        grid=(nt,),
        in_specs=[_rows(TM, D_MODEL), _rows(TM, 2 * D_MODEL), _rows(TM, D_MODEL), _rows(TM, D_MODEL),
                  _rows(TM, D_MODEL), _rows(TM, A_WIDTH), _rows(TM, B_Q_WIDTH),
                  _resident((D_MODEL, A_WIDTH + B_Q_WIDTH)), _resident((D_MODEL, D_MODEL))],
        out_specs=[_rows(TM, A_WIDTH), _rows(TM, B_Q_WIDTH), _rows(TM, 2 * D_MODEL), whole, whole],
        out_shape=[jax.ShapeDtypeStruct((t, A_WIDTH), BF16), jax.ShapeDtypeStruct((t, B_Q_WIDTH), BF16),
                   jax.ShapeDtypeStruct((t, 2 * D_MODEL), BF16), jax.ShapeDtypeStruct((D_MODEL, D_MODEL), BF16),
                   jax.ShapeDtypeStruct((D_MODEL, D_MODEL), BF16)],
        scratch_shapes=[pltpu.VMEM((D_MODEL, D_MODEL), F32), pltpu.VMEM((D_MODEL, A_WIDTH + B_Q_WIDTH), F32)],
        args=(d, gates, ya, yb, mg, oa, ob, proj_t, wout), sem=("arbitrary",), hosted=hosted)


def _place():
    x, y, c = lax.axis_index("x"), lax.axis_index("y"), lax.axis_index("c")
    chips = [(1 - x, y), (x, 1 - y), (1 - x, 1 - y)]
    return x, y, c, chips


class _Gather:
    per = 8

    def __init__(self, shards):
        n = len(shards)
        self.inputs = list(shards)
        self.out_shape = [jax.ShapeDtypeStruct((N_DEV * s.shape[0], s.shape[1]), s.dtype) for s in shards]
        self.scratch = [pltpu.SemaphoreType.DMA((n * self.per,)), pltpu.SemaphoreType.DMA((n * self.per,)),
                        pltpu.SemaphoreType.DMA((n,))]
        self.result = None

    def _parts(self, ins, outs, sems):
        send_sems, recv_sems, local_sems = sems
        x, y, c, chips = _place()
        me, sibling = (x, y, c), (x, y, 1 - c)
        xn, yn, dg = chips
        n = len(ins)

        def rows(k, p, part=None):
            r = ins[k].shape[0]
            base = (4 * p[0] + 2 * p[1] + p[2]) * r
            if part is None:
                return outs[k].at[pl.ds(base, r), :]
            return outs[k].at[pl.ds(base + part * (r // 2), r // 2), :]

        def copy(k, slot, block, to, src=None, part=None):
            return pltpu.make_async_remote_copy(
                src_ref=rows(k, block, part) if src is None else src, dst_ref=rows(k, block, part),
                send_sem=send_sems.at[k * self.per + slot], recv_sem=recv_sems.at[k * self.per + slot],
                device_id=to, device_id_type=MESH)

        mine = [pltpu.make_async_copy(ins[k], rows(k, me), local_sems.at[k]) for k in range(n)]
        sends, lands = [], []
        for k in range(n):
            sends.append({
                0: copy(k, 0, me, sibling, src=ins[k]),
                1: copy(k, 1, me, (*xn, c), src=ins[k]),
                2: copy(k, 2, me, (*yn, c), src=ins[k]),
                3: copy(k, 3, (*xn, c), (*yn, c), part=0),
                4: copy(k, 4, (*yn, c), (*xn, c), part=1),
                5: copy(k, 5, (*xn, c), sibling),
                6: copy(k, 6, (*yn, c), sibling),
                7: copy(k, 7, (*dg, c), sibling)})
            lands.append({
                0: copy(k, 0, sibling, me),
                1: copy(k, 1, (*xn, c), me),
                2: copy(k, 2, (*yn, c), me),
                3: copy(k, 3, (*dg, c), me, part=0),
                4: copy(k, 4, (*dg, c), me, part=1),
                5: copy(k, 5, (*xn, 1 - c), me),
                6: copy(k, 6, (*yn, 1 - c), me),
                7: copy(k, 7, (*dg, 1 - c), me)})
        return n, mine, sends, lands

    def start(self, ins, outs, sems):
        n, mine, sends, _ = self._parts(ins, outs, sems)
        for cp in mine:
            cp.start()
        for slot in (0, 1, 2):
            for k in range(n):
                sends[k][slot].start()

    def relay(self, ins, outs, sems):
        n, _, sends, lands = self._parts(ins, outs, sems)
        for k in range(n):
            lands[k][1].wait_recv()
            sends[k][3].start()
            sends[k][5].start()
        for k in range(n):
            lands[k][2].wait_recv()
            sends[k][4].start()
            sends[k][6].start()

    def forward(self, ins, outs, sems):
        n, _, sends, lands = self._parts(ins, outs, sems)
        for k in range(n):
            lands[k][3].wait_recv()
            lands[k][4].wait_recv()
            sends[k][7].start()

    def finish(self, ins, outs, sems):
        n, mine, sends, lands = self._parts(ins, outs, sems)
        for k in range(n):
            for slot in (0, 5, 6, 7):
                lands[k][slot].wait_recv()
        for k in range(n):
            for slot in range(self.per):
                sends[k][slot].wait_send()
        for cp in mine:
            cp.wait()


class _PairExchange:
    def __init__(self, grads):
        n = len(grads)
        self.inputs = list(grads)
        self.out_shape = [jax.ShapeDtypeStruct((g.shape[0] // 2, g.shape[1]), g.dtype) for g in grads]
        self.scratch = [pltpu.SemaphoreType.DMA((n * N_CHIP,)), pltpu.SemaphoreType.DMA((n * N_CHIP,))]
        self.result = None

    def _copies(self, ins, outs, sems):
        send_sems, recv_sems = sems
        x, y, c, _ = _place()
        copies = []
        for k in range(len(ins)):
            r = ins[k].shape[0] // N_DEV
            for q in range(N_CHIP):
                copies.append(pltpu.make_async_remote_copy(
                    src_ref=ins[k].at[pl.ds((2 * q + 1 - c) * r, r), :], dst_ref=outs[k].at[pl.ds(q * r, r), :],
                    send_sem=send_sems.at[k * N_CHIP + q], recv_sem=recv_sems.at[k * N_CHIP + q],
                    device_id=(x, y, 1 - c), device_id_type=MESH))
        return copies

    def start(self, ins, outs, sems):
        for cp in self._copies(ins, outs, sems):
            cp.start()

    def relay(self, ins, outs, sems):
        pass

    def forward(self, ins, outs, sems):
        pass

    def finish(self, ins, outs, sems):
        copies = self._copies(ins, outs, sems)
        for cp in copies:
            cp.wait_recv()
        for cp in copies:
            cp.wait_send()


class _ChipExchange(_PairExchange):
    def __init__(self, psums):
        n = len(psums)
        self.inputs = list(psums)
        self.out_shape = [jax.ShapeDtypeStruct((3 * p.shape[0] // N_CHIP, p.shape[1]), p.dtype) for p in psums]
        self.scratch = [pltpu.SemaphoreType.DMA((n * 3,)), pltpu.SemaphoreType.DMA((n * 3,))]
        self.result = None

    def _copies(self, ins, outs, sems):
        send_sems, recv_sems = sems
        _, _, c, chips = _place()
        copies = []
        for k in range(len(ins)):
            r = ins[k].shape[0] // N_CHIP
            for j, chip in enumerate(chips):
                copies.append(pltpu.make_async_remote_copy(
                    src_ref=ins[k].at[pl.ds((2 * chip[0] + chip[1]) * r, r), :], dst_ref=outs[k].at[pl.ds(j * r, r), :],
                    send_sem=send_sems.at[k * 3 + j], recv_sem=recv_sems.at[k * 3 + j],
                    device_id=(*chip, c), device_id_type=MESH))
        return copies


def _exchange_alone(xchg, name):
    n_in, n_out = len(xchg.inputs), len(xchg.out_shape)

    def body(*refs):
        ins, outs, sems = refs[:n_in], refs[n_in:n_in + n_out], refs[n_in + n_out:]
        xchg.start(ins, outs, sems)
        xchg.relay(ins, outs, sems)
        xchg.forward(ins, outs, sems)
        xchg.finish(ins, outs, sems)

    xchg.result = list(pl.pallas_call(
        body, name=name, in_specs=[_hbm()] * n_in, out_specs=[_hbm()] * n_out, out_shape=xchg.out_shape,
        scratch_shapes=xchg.scratch)(*xchg.inputs))
    return xchg.result


def _pair_sum(core, grads, recvd, name):
    n = len(grads)
    r = grads[0].shape[0] // N_DEV
    cdim = grads[0].shape[1]
    tr = r // 2 if r % 32 == 0 else r
    nt = r // tr

    def body(core_ref, *refs):
        del core_ref
        for k in range(n):
            refs[2 * n + k][...] = (refs[k][...].astype(F32) + refs[n + k][...].astype(F32)).astype(BF16)

    gspec = pl.BlockSpec((tr, cdim), lambda q, i, core_ref: ((2 * q + core_ref[0]) * nt + i, 0))
    rspec = pl.BlockSpec((tr, cdim), lambda q, i, core_ref: (q * nt + i, 0))
    return pl.pallas_call(
        body,
        name=name,
        grid_spec=pltpu.PrefetchScalarGridSpec(
            num_scalar_prefetch=1, grid=(N_CHIP, nt), in_specs=[gspec] * n + [rspec] * n, out_specs=[rspec] * n),
        out_shape=[jax.ShapeDtypeStruct((N_CHIP * r, cdim), BF16) for _ in range(n)],
        compiler_params=_cparams(("parallel", "parallel")),
    )(core, *grads, *recvd)


def _final_sum(chip, psums, recvd, name):
    n = len(psums)
    r = psums[0].shape[0] // N_CHIP
    cdim = psums[0].shape[1]
    tr = r // 2 if r % 32 == 0 else r
    nt = r // tr

    def body(chip_ref, *refs):
        del chip_ref
        for k in range(n):
            got = refs[n + k]
            tot = refs[k][...].astype(F32) + got[0].astype(F32)
            tot = tot + got[1].astype(F32)
            tot = tot + got[2].astype(F32)
            refs[2 * n + k][...] = tot

    pspec = pl.BlockSpec((tr, cdim), lambda i, chip_ref: (chip_ref[0] * nt + i, 0))
    rspec = pl.BlockSpec((3, tr, cdim), lambda i, chip_ref: (0, i, 0))
    ospec = pl.BlockSpec((tr, cdim), lambda i, chip_ref: (i, 0))
    return pl.pallas_call(
        body,
        name=name,
        grid_spec=pltpu.PrefetchScalarGridSpec(
            num_scalar_prefetch=1, grid=(nt,), in_specs=[pspec] * n + [rspec] * n, out_specs=[ospec] * n),
        out_shape=[jax.ShapeDtypeStruct((r, cdim), F32) for _ in range(n)],
        compiler_params=_cparams(("parallel",)),
    )(chip, *psums, *[g.reshape(3, r, cdim) for g in recvd])


SMALL_ROWS = 16


def _all_reduce_small(part):
    def body(p_ref, o_ref, buf, send_sems, recv_sems):
        x, y, c, _ = _place()
        me = 4 * x + 2 * y + c
        buf[me] = p_ref[...]
        copies = []
        for d in range(1, N_DEV):
            peer = me ^ d
            copies.append(pltpu.make_async_remote_copy(
                src_ref=p_ref, dst_ref=buf.at[me], send_sem=send_sems.at[d - 1], recv_sem=recv_sems.at[d - 1],
                device_id=(peer // 4, (peer // 2) % 2, peer % 2), device_id_type=MESH))
        for cp in copies:
            cp.start()
        for cp in copies:
            cp.wait_recv()
        for cp in copies:
            cp.wait_send()
        tot = buf[0]
        for d in range(1, N_DEV):
            tot = tot + buf[d]
        o_ref[...] = tot

    return pl.pallas_call(
        body,
        name="all_reduce_small",
        in_specs=[pl.BlockSpec(memory_space=pltpu.VMEM)],
        out_specs=pl.BlockSpec(memory_space=pltpu.VMEM),
        out_shape=jax.ShapeDtypeStruct(part.shape, F32),
        scratch_shapes=[pltpu.VMEM((N_DEV,) + part.shape, F32), pltpu.SemaphoreType.DMA((N_DEV - 1,)),
                        pltpu.SemaphoreType.DMA((N_DEV - 1,))],
    )(part)


ADAMW_STEPS = 4


def _adamw(ws, gs, ms, vs, name, hosted=()):
    n = len(ws)
    steps = ADAMW_STEPS if all(w.shape[0] % (8 * ADAMW_STEPS) == 0 for w in ws) else 1
    c1 = 1.0 - ADAM_B1 ** ADAM_STEP
    c2 = 1.0 - ADAM_B2 ** ADAM_STEP

    def body(*refs):
        for k in range(n):
            w, g, m, v = (refs[j * n + k][...] for j in range(4))
            m2 = ADAM_B1 * m + (1.0 - ADAM_B1) * g
            v2 = ADAM_B2 * v + (1.0 - ADAM_B2) * (g * g)
            delta = -ADAM_LR * ((m2 / c1) / (jnp.sqrt(v2 / c2) + ADAM_EPS) + ADAM_WD * w)
            refs[4 * n + k][...] = delta
            refs[5 * n + k][...] = m2
            refs[6 * n + k][...] = v2

    specs = [pl.BlockSpec((w.shape[0] // steps, w.shape[1]), lambda i: (i, 0)) for w in ws]
    shapes = [jax.ShapeDtypeStruct(w.shape, F32) for w in ws]
    outs = _call(
        body,
        name=name,
        grid=(steps,),
        in_specs=specs * 4,
        out_specs=specs * 3,
        out_shape=shapes * 3,
        args=(*ws, *gs, *ms, *vs), sem=("parallel",), hosted=hosted)
    return outs[:n], outs[n:2 * n], outs[2 * n:]


def _adamw_reduced(chip, ws, psums, recvd, ms, vs, steps, name):
    n = len(ws)
    c1 = 1.0 - ADAM_B1 ** ADAM_STEP
    c2 = 1.0 - ADAM_B2 ** ADAM_STEP

    def body(chip_ref, *refs):
        del chip_ref
        for k in range(n):
            w, m, v = (refs[j * n + k][...] for j in (0, 3, 4))
            got = refs[2 * n + k]
            g = refs[n + k][...].astype(F32) + got[0].astype(F32)
            g = g + got[1].astype(F32)
            g = g + got[2].astype(F32)
            m2 = ADAM_B1 * m + (1.0 - ADAM_B1) * g
            v2 = ADAM_B2 * v + (1.0 - ADAM_B2) * (g * g)
            refs[5 * n + k][...] = g
            refs[6 * n + k][...] = -ADAM_LR * ((m2 / c1) / (jnp.sqrt(v2 / c2) + ADAM_EPS) + ADAM_WD * w)
            refs[7 * n + k][...] = m2
            refs[8 * n + k][...] = v2

    def blk(w):
        return (w.shape[0] // steps, w.shape[1])

    own = [pl.BlockSpec(blk(w), lambda i, chip_ref: (i, 0)) for w in ws]
    psum = [pl.BlockSpec(blk(w), lambda i, chip_ref: (chip_ref[0] * steps + i, 0)) for w in ws]
    recv = [pl.BlockSpec((3,) + blk(w), lambda i, chip_ref: (0, i, 0)) for w in ws]
    shapes = [jax.ShapeDtypeStruct(w.shape, F32) for w in ws]
    outs = pl.pallas_call(
        body,
        name=name,
        grid_spec=pltpu.PrefetchScalarGridSpec(
            num_scalar_prefetch=1, grid=(steps,), in_specs=own + psum + recv + own + own, out_specs=own * 4),
        out_shape=shapes * 4,
        compiler_params=_cparams(("parallel",)),
    )(chip, *ws, *psums, *[r.reshape((3,) + w.shape) for r, w in zip(recvd, ws)], *ms, *vs)
    return outs[:n], outs[n:2 * n], outs[2 * n:3 * n], outs[3 * n:]


def _bias_b():
    pad = B_PREV * CHUNK
    slopes = np.array([2.0 ** (-8.0 * (i + 1) / B_Q_HEADS) for i in range(B_Q_HEADS)], dtype=np.float32)
    dist = np.abs(np.arange(TQ)[:, None] - np.arange(TQ + pad)[None, :] + pad).astype(np.float32)
    bias = -slopes.reshape(B_Q_HEADS, 1, 1) * dist[None]
    qc = (np.arange(TQ)[:, None] + pad) // CHUNK
    kc = np.arange(TQ + pad)[None, :] // CHUNK
    allowed = (kc <= qc) & (kc >= qc - B_PREV)
    return np.where(allowed[None], bias, np.float32(NEG_INF)).astype(np.float32)


def kernel(x, ffn1_norm, ffn1_w_gate, ffn1_w_up, ffn1_w_down, mix_norm, w_in, rel_bias, sinks, w_proj_a, w_proj_b, w_out, ffn2_norm, ffn2_w_gate, ffn2_w_up, ffn2_w_down, final_norm, loss_target, m_ffn1_norm, m_ffn1_w_gate, m_ffn1_w_up, m_ffn1_w_down, m_mix_norm, m_w_in, m_rel_bias, m_sinks, m_w_proj_a, m_w_proj_b, m_w_out, m_ffn2_norm, m_ffn2_w_gate, m_ffn2_w_up, m_ffn2_w_down, m_final_norm, v_ffn1_norm, v_ffn1_w_gate, v_ffn1_w_up, v_ffn1_w_down, v_mix_norm, v_w_in, v_rel_bias, v_sinks, v_w_proj_a, v_w_proj_b, v_w_out, v_ffn2_norm, v_ffn2_w_gate, v_ffn2_w_up, v_ffn2_w_down, v_final_norm):
    bsz, s_len, _ = x.shape
    t = bsz * s_len
    core = lax.axis_index("c").astype(jnp.int32).reshape(1)
    chip = (2 * lax.axis_index("x") + lax.axis_index("y")).astype(jnp.int32).reshape(1)

    def row_form(w):
        return w.astype(BF16).T

    gather_up1 = _Gather([row_form(ffn1_w_gate), row_form(ffn1_w_up)])
    far = jnp.broadcast_to(rel_bias[:, REL_TABLE - 1:REL_TABLE], (A_HEADS, REL_WRAP // 2))
    tv = jnp.concatenate([far, jnp.flip(rel_bias, axis=1), jnp.zeros((A_HEADS, REL_WRAP // 2 - REL_TABLE), F32)], axis=1)
    bias_a = _bias_a_build(tv.reshape(A_HEADS, 1, REL_WRAP), hosted=[gather_up1])
    wg1, wu1 = gather_up1.result
    gather_down1 = _Gather([ffn1_w_down.astype(BF16), row_form(w_in)])
    gather_out = _Gather([jnp.concatenate([row_form(w_proj_a), row_form(w_proj_b)], axis=1), w_out.astype(BF16)])
    gather_ffn2_gate = _Gather([row_form(ffn2_w_gate)])
    gather_ffn2_rest = _Gather([row_form(ffn2_w_up), ffn2_w_down.astype(BF16)])

    x0 = x.reshape(t, D_MODEL)
    tgt = loss_target.reshape(t, D_MODEL)
    gam1, gam2, gam3, gam4 = (g.reshape(1, D_MODEL) for g in (ffn1_norm, mix_norm, ffn2_norm, final_norm))

    h1, g1, u1, a1 = _ffn_up(x0, gam1, wg1, wu1, "ffn1_up", hosted=[gather_down1])
    wd1, win_t = gather_down1.result
    x1 = _ffn_down(x0, a1, wd1, "ffn1_down", hosted=[gather_out])
    proj_t, wout = gather_out.result
    h2, qkv_a, qkv_b, gates = _proj_fwd(x1, gam2, win_t, hosted=[gather_ffn2_gate])
    (wg2,) = gather_ffn2_gate.result
    qkv_a3 = qkv_a.reshape(bsz, s_len, QKV_A)
    qkv_b3 = qkv_b.reshape(bsz, s_len, QKV_B)

    bias_b = jnp.asarray(_bias_b())
    sink_rows = jnp.broadcast_to(sinks.reshape(B_Q_HEADS, 1, 1), (B_Q_HEADS, 8, LANES))

    oa = _attn_a_fwd(qkv_a3, bias_a, hosted=[gather_ffn2_rest]).reshape(t, A_WIDTH)
    wu2, wd2 = gather_ffn2_rest.result
    ob = _attn_b_fwd(qkv_b3, bias_b, sink_rows).reshape(t, B_Q_WIDTH)
    x2, ya, yb, mg = _mix_out_fwd(x1, oa, ob, gates, proj_t, wout)
    h3, g2, u2, a2, x3 = _ffn_fwd(x2, gam3, wg2, wu2, wd2, "ffn2_fwd")

    dx2, dg2, du2, db2, dgam3, dgam4, loss_part = _ffn_bwd_head(x3, gam4, tgt, x2, gam3, g2, u2, wg2, wu2, wd2,
                                                                "ffn2_bwd")
    gw_ffn2 = [_mm_tn([dg2], h3, "grad_ffn2_gate"), _mm_tn([du2], h3, "grad_ffn2_up"),
               _mm_tn([a2], db2, "grad_ffn2_down")]
    pairx_ffn2 = _PairExchange(gw_ffn2)
    doa, dob, dgates, gw_out, gw_proj = _mix_out_bwd(dx2, gates, ya, yb, mg, oa, ob, proj_t, wout,
                                                     hosted=[pairx_ffn2])
    psum_ffn2 = _pair_sum(core, gw_ffn2, pairx_ffn2.result, "pair_sum_ffn2")

    chipx_ffn2 = _ChipExchange(psum_ffn2)
    dqa, dka, dva, dbias_a = _attn_a_bwd(qkv_a3, bias_a, doa.reshape(bsz, s_len, A_WIDTH), hosted=[chipx_ffn2])
    pairx_out = _PairExchange([gw_proj, gw_out])
    dqb, dkvb, dsink = _attn_b_bwd(qkv_b3, bias_b, sink_rows, dob.reshape(bsz, s_len, B_Q_WIDTH), hosted=[pairx_out])
    drel_lanes = _relbias_grad(dbias_a)
    dproj = [dqa.reshape(t, A_WIDTH), dka.reshape(t, A_WIDTH), dva.reshape(t, A_WIDTH), dqb.reshape(t, B_Q_WIDTH),
             dkvb.reshape(t, 2 * B_KV_WIDTH), dgates]

    gw_in = _mm_tn(dproj, h2, "grad_w_in")
    pairx_in = _PairExchange([gw_in])
    psum_out = _pair_sum(core, [gw_proj, gw_out], pairx_out.result, "pair_sum_mix")
    chipx_out = _ChipExchange(psum_out)
    dx1, db1, dgam2 = _proj_bwd(dx2, x1, gam2, dproj, win_t, hosted=[pairx_in, chipx_out])
    psum_in = _pair_sum(core, [gw_in], pairx_in.result, "pair_sum_w_in")
    gw_d1 = _mm_tn([a1], db1, "grad_ffn1_down")

    chipx_in = _ChipExchange(psum_in)
    pairx_d1 = _PairExchange([gw_d1])
    dg1, du1 = _ffn_bwd_act(dx1, g1, u1, wd1, "ffn1_bwd_act", hosted=[chipx_in, pairx_d1])
    psum_d1 = _pair_sum(core, [gw_d1], pairx_d1.result, "pair_sum_ffn1_down")
    chipx_d1 = _ChipExchange(psum_d1)
    gw_g1 = _mm_tn([dg1], h1, "grad_ffn1_gate", hosted=[chipx_d1])
    from_sibling_g1 = _exchange_alone(_PairExchange([gw_g1]), "pair_exchange_ffn1_gate")
    psum_g1 = _pair_sum(core, [gw_g1], from_sibling_g1, "pair_sum_ffn1_gate")
    chipx_g1 = _ChipExchange(psum_g1)
    gw_u1 = _mm_tn([du1], h1, "grad_ffn1_up", hosted=[chipx_g1])
    from_sibling_u1 = _exchange_alone(_PairExchange([gw_u1]), "pair_exchange_ffn1_up")
    psum_u1 = _pair_sum(core, [gw_u1], from_sibling_u1, "pair_sum_ffn1_up")
    chipx_u1 = _ChipExchange(psum_u1)
    dx0, dgam1 = _ffn_bwd_in(dx1, x0, gam1, dg1, du1, wg1, wu1, "ffn1_bwd_in", hosted=[chipx_u1])

    (g_proj,) = _final_sum(chip, psum_out[0:1], chipx_out.result[0:1], "grad_sum_proj")
    grads = {"w_proj_a": g_proj[:, 0:A_WIDTH].T, "w_proj_b": g_proj[:, A_WIDTH:].T}

    def row_of(v):
        return jnp.pad(v.reshape(1, -1), ((0, 0), (0, D_MODEL - v.size)))

    def table_rows(v):
        return jnp.pad(v, ((0, 0), (0, D_MODEL - REL_TABLE)))

    drel_local = jnp.flip(drel_lanes[:, 0, 0:REL_TABLE], axis=1)
    small_part = jnp.concatenate(
        [jnp.sum(dgam1, axis=0, keepdims=True), jnp.sum(dgam2, axis=0, keepdims=True),
         jnp.sum(dgam3, axis=0, keepdims=True), jnp.sum(dgam4, axis=0, keepdims=True),
         row_of(jnp.sum(loss_part)), row_of(dsink[:, 0, 0]), jnp.zeros((2, D_MODEL), F32),
         table_rows(drel_local)], axis=0)
    small = _all_reduce_small(small_part)
    loss = small[4, 0]

    def pack(n1, n2, n3, n4, sk, tb):
        return jnp.concatenate([n1.reshape(1, -1), n2.reshape(1, -1), n3.reshape(1, -1), n4.reshape(1, -1),
                                jnp.zeros((1, D_MODEL), F32), row_of(sk), jnp.zeros((2, D_MODEL), F32), table_rows(tb)],
                               axis=0)

    live = np.zeros((SMALL_ROWS, D_MODEL), np.float32)
    live[0:4] = 1.0
    live[5, 0:B_Q_HEADS] = 1.0
    live[8:16, 0:REL_TABLE] = 1.0
    small_g = small * jnp.asarray(live)
    sw = pack(ffn1_norm, mix_norm, ffn2_norm, final_norm, sinks, rel_bias)
    sm = pack(m_ffn1_norm, m_mix_norm, m_ffn2_norm, m_final_norm, m_sinks, m_rel_bias)
    sv = pack(v_ffn1_norm, v_mix_norm, v_ffn2_norm, v_final_norm, v_sinks, v_rel_bias)
    (sd,), (snm,), (snv,) = _adamw([sw], [small_g], [sm], [sv], "adamw_small")

    def unpack(p):
        return {"ffn1_norm": p[0], "mix_norm": p[1], "ffn2_norm": p[2], "final_norm": p[3],
                "sinks": p[5, 0:B_Q_HEADS], "rel_bias": p[8:16, 0:REL_TABLE]}

    grads.update(unpack(small_g))
    delta, new_m, new_v = unpack(sd), unpack(snm), unpack(snv)

    wmv = {
        "ffn1_w_gate": (ffn1_w_gate, m_ffn1_w_gate, v_ffn1_w_gate), "ffn1_w_up": (ffn1_w_up, m_ffn1_w_up, v_ffn1_w_up),
        "ffn1_w_down": (ffn1_w_down, m_ffn1_w_down, v_ffn1_w_down), "w_in": (w_in, m_w_in, v_w_in),
        "w_proj_a": (w_proj_a, m_w_proj_a, v_w_proj_a), "w_proj_b": (w_proj_b, m_w_proj_b, v_w_proj_b),
        "w_out": (w_out, m_w_out, v_w_out),
        "ffn2_w_gate": (ffn2_w_gate, m_ffn2_w_gate, v_ffn2_w_gate), "ffn2_w_up": (ffn2_w_up, m_ffn2_w_up, v_ffn2_w_up),
        "ffn2_w_down": (ffn2_w_down, m_ffn2_w_down, v_ffn2_w_down),
    }
    row_form_names = ("ffn1_w_gate", "ffn1_w_up", "w_in", "ffn2_w_gate", "ffn2_w_up")

    def form(n, a):
        return a.T if n in row_form_names else a

    def reduced_group(gname, names, psums, recvd, steps):
        gs_, ds_, ms_, vs_ = _adamw_reduced(
            chip, [form(n, wmv[n][0]) for n in names], psums, recvd, [form(n, wmv[n][1]) for n in names],
            [form(n, wmv[n][2]) for n in names], steps, gname)
        for n, g_, d_, m_, v_ in zip(names, gs_, ds_, ms_, vs_):
            grads[n], delta[n], new_m[n], new_v[n] = form(n, g_), form(n, d_), form(n, m_), form(n, v_)

    reduced_group("adamw_ffn", ["ffn1_w_gate", "ffn1_w_up", "ffn1_w_down", "ffn2_w_gate", "ffn2_w_up", "ffn2_w_down"],
                  psum_g1 + psum_u1 + psum_d1 + psum_ffn2,
                  chipx_g1.result + chipx_u1.result + chipx_d1.result + chipx_ffn2.result, 11)
    reduced_group("adamw_w_in", ["w_in"], psum_in, chipx_in.result, 2)
    reduced_group("adamw_w_out", ["w_out"], psum_out[1:2], chipx_out.result[1:2], 2)
    names = ["w_proj_a", "w_proj_b"]
    ds_, ms_, vs_ = _adamw([wmv[n][0] for n in names], [grads[n] for n in names], [wmv[n][1] for n in names],
                           [wmv[n][2] for n in names], "adamw_proj")
    for n, d_, m_, v_ in zip(names, ds_, ms_, vs_):
        delta[n], new_m[n], new_v[n] = d_, m_, v_

    order = ["ffn1_norm", "ffn1_w_gate", "ffn1_w_up", "ffn1_w_down", "mix_norm", "w_in", "rel_bias", "sinks",
             "w_proj_a", "w_proj_b", "w_out", "ffn2_norm", "ffn2_w_gate", "ffn2_w_up", "ffn2_w_down", "final_norm"]
    grad_x = dx0.reshape(bsz, s_len, D_MODEL)
    return (loss, grad_x, *[grads[n] for n in order], *[delta[n] for n in order], *[new_m[n] for n in order],
            *[new_v[n] for n in order])
```

```python
import numpy as np
import jax
import jax.numpy as jnp
from jax import lax
from jax.experimental import pallas as pl
from jax.experimental.pallas import tpu as pltpu

F32 = jnp.float32
BF16 = jnp.bfloat16

D_MODEL = 1024
D_FF = 2816
CHUNK = 64
D_HEAD = 64
A_HEADS = 8
A_PREV = 8
MAX_REL = 128
B_Q_HEADS = 8
B_KV_HEADS = 2
B_GROUP = B_Q_HEADS // B_KV_HEADS
B_PREV = 2
REL_TABLE = (CHUNK - 1) + MAX_REL + 1
A_WIDTH = A_HEADS * D_HEAD
B_Q_WIDTH = B_Q_HEADS * D_HEAD
B_KV_WIDTH = B_KV_HEADS * D_HEAD
QKV_A = 3 * A_WIDTH
QKV_B = B_Q_WIDTH + 2 * B_KV_WIDTH
IN_WIDTH = QKV_A + QKV_B + 2 * D_MODEL
EPS = 1e-6
NEG_INF = -1e30
SCALE = 1.0 / 8.0

ADAM_LR = 0.001
ADAM_B1 = 0.9
ADAM_B2 = 0.999
ADAM_EPS = 1e-08
ADAM_WD = 0.01
ADAM_STEP = 10

N_DEV = 8
N_CHIP = 4
MESH = pl.DeviceIdType.MESH

LANES = 128
TQ = 256
TM = 256
FC = 256
VMEM_LIMIT = 56 << 20


def _cparams(sem, vmem=VMEM_LIMIT):
    return pltpu.CompilerParams(dimension_semantics=sem, vmem_limit_bytes=vmem)


def _dot_nt(a, b):
    return lax.dot_general(a, b, (((1,), (1,)), ((), ())), preferred_element_type=F32)


def _dot_nn(a, b):
    return lax.dot_general(a, b, (((1,), (0,)), ((), ())), preferred_element_type=F32)


def _dot_tn(a, b):
    return lax.dot_general(a, b, (((0,), (0,)), ((), ())), preferred_element_type=F32)


def _resident(shape):
    nd = len(shape)
    return pl.BlockSpec(shape, lambda *_: (0,) * nd, pipeline_mode=pl.Buffered(1))


def _rows(tm, width):
    return pl.BlockSpec((tm, width), lambda i: (i, 0))


def _colsum8(v):
    tm, n = v.shape
    return jnp.sum(v.reshape(tm // 8, 8, n), axis=0)


def _rms(x):
    r = lax.rsqrt(jnp.mean(x * x, axis=-1, keepdims=True) + EPS)
    return x * r, r


def _rms_bwd(dh, xh, r, gamma):
    dxh = dh * gamma
    dx = r * (dxh - xh * jnp.mean(dxh * xh, axis=-1, keepdims=True))
    return dx, _colsum8(dh * xh)


def _hbm():
    return pl.BlockSpec(memory_space=pltpu.HBM)


def _call(body, *, name, grid, in_specs, out_specs, out_shape, args, sem, scratch_shapes=(), hosted=()):
    in_specs, out_specs, out_shape = list(in_specs), list(out_specs), list(out_shape)
    scratch_shapes = list(scratch_shapes)
    if not hosted:
        return pl.pallas_call(body, name=name, grid=grid, in_specs=in_specs, out_specs=out_specs, out_shape=out_shape,
                              scratch_shapes=scratch_shapes, compiler_params=_cparams(sem))(*args)
    n_in, n_out, n_scr = len(in_specs), len(out_specs), len(scratch_shapes)
    x_in = [a for x in hosted for a in x.inputs]
    x_out = [s for x in hosted for s in x.out_shape]
    x_scr = [s for x in hosted for s in x.scratch]
    steps = int(np.prod(grid))
    forward_step = max(steps - 3, 0)
    relay_step = min((5 * steps) // 8, forward_step)

    def wrapped(*refs):
        pos = [0]

        def take(k):
            pos[0] += k
            return refs[pos[0] - k:pos[0]]

        ins, xin, outs, xout, scr, xscr = (take(k) for k in (n_in, len(x_in), n_out, len(x_out), n_scr, len(x_scr)))
        step = 0
        for axis, extent in enumerate(grid):
            step = step * extent + pl.program_id(axis)
        own, oi, oo, osc = [], 0, 0, 0
        for x in hosted:
            own.append((xin[oi:oi + len(x.inputs)], xout[oo:oo + len(x.out_shape)], xscr[osc:osc + len(x.scratch)]))
            oi, oo, osc = oi + len(x.inputs), oo + len(x.out_shape), osc + len(x.scratch)

        def phase(method):
            for x, (i_, o_, s_) in zip(hosted, own):
                getattr(x, method)(i_, o_, s_)

        pl.when(step == 0)(lambda: phase("start"))
        body(*ins, *outs, *scr)
        pl.when(step == relay_step)(lambda: phase("relay"))
        pl.when(step == forward_step)(lambda: phase("forward"))
        pl.when(step == steps - 1)(lambda: phase("finish"))

    res = pl.pallas_call(
        wrapped, name=name, grid=grid, in_specs=in_specs + [_hbm()] * len(x_in),
        out_specs=out_specs + [_hbm()] * len(x_out), out_shape=out_shape + x_out,
        scratch_shapes=scratch_shapes + x_scr, compiler_params=_cparams(("arbitrary",) * len(grid)))(*args, *x_in)
    rest = list(res[n_out:])
    for x in hosted:
        x.result, rest = rest[:len(x.out_shape)], rest[len(x.out_shape):]
    return list(res[:n_out])


def _to_bf16(arrays, name):
    n = len(arrays)

    def body(*refs):
        for k in range(n):
            refs[n + k][...] = refs[k][...].astype(BF16)

    specs = [pl.BlockSpec(a.shape, lambda i: (0, 0)) for a in arrays]
    return pl.pallas_call(
        body, name=name, grid=(1,), in_specs=specs, out_specs=specs,
        out_shape=[jax.ShapeDtypeStruct(a.shape, BF16) for a in arrays],
        compiler_params=_cparams(("arbitrary",)))(*arrays)


def _ffn_fwd(x, gamma, wg_t, wu_t, wd, name, hosted=()):
    t = x.shape[0]
    f = wg_t.shape[0]

    def body(x_ref, gam_ref, wg_ref, wu_ref, wd_ref, h_ref, g_ref, u_ref, a_ref, y_ref):
        xv = x_ref[...]
        xh, _ = _rms(xv)
        h = (xh * gam_ref[...]).astype(BF16)
        h_ref[...] = h
        for j in range(f // FC):
            sl = slice(j * FC, (j + 1) * FC)
            g = _dot_nt(h, wg_ref[sl, :])
            u = _dot_nt(h, wu_ref[sl, :])
            g_ref[:, sl] = g.astype(BF16)
            u_ref[:, sl] = u.astype(BF16)
            a_ref[:, sl] = (g * jax.nn.sigmoid(g) * u).astype(BF16)
        y_ref[...] = xv + 0.5 * _dot_nn(a_ref[...], wd_ref[...])

    return _call(
        body,
        name=name,
        grid=(t // TM,),
        in_specs=[_rows(TM, D_MODEL), _resident((1, D_MODEL)), _resident((f, D_MODEL)), _resident((f, D_MODEL)),
                  _resident((f, D_MODEL))],
        out_specs=[_rows(TM, D_MODEL), _rows(TM, f), _rows(TM, f), _rows(TM, f),
                   _rows(TM, D_MODEL)],
        out_shape=[jax.ShapeDtypeStruct((t, D_MODEL), BF16), jax.ShapeDtypeStruct((t, f), BF16),
                   jax.ShapeDtypeStruct((t, f), BF16), jax.ShapeDtypeStruct((t, f), BF16),
                   jax.ShapeDtypeStruct((t, D_MODEL), F32)],
        args=(x, gamma, wg_t, wu_t, wd), sem=("parallel",), hosted=hosted)


def _ffn_up(x, gamma, wg_t, wu_t, name, hosted=()):
    t = x.shape[0]
    f = wg_t.shape[0]

    def body(x_ref, gam_ref, wg_ref, wu_ref, h_ref, g_ref, u_ref, a_ref):
        xh, _ = _rms(x_ref[...])
        h = (xh * gam_ref[...]).astype(BF16)
        h_ref[...] = h
        for j in range(f // FC):
            sl = slice(j * FC, (j + 1) * FC)
            g = _dot_nt(h, wg_ref[sl, :])
            u = _dot_nt(h, wu_ref[sl, :])
            g_ref[:, sl] = g.astype(BF16)
            u_ref[:, sl] = u.astype(BF16)
            a_ref[:, sl] = (g * jax.nn.sigmoid(g) * u).astype(BF16)

    return _call(
        body,
        name=name,
        grid=(t // TM,),
        in_specs=[_rows(TM, D_MODEL), _resident((1, D_MODEL)), _resident((f, D_MODEL)), _resident((f, D_MODEL))],
        out_specs=[_rows(TM, D_MODEL), _rows(TM, f), _rows(TM, f), _rows(TM, f)],
        out_shape=[jax.ShapeDtypeStruct((t, D_MODEL), BF16), jax.ShapeDtypeStruct((t, f), BF16),
                   jax.ShapeDtypeStruct((t, f), BF16), jax.ShapeDtypeStruct((t, f), BF16)],
        args=(x, gamma, wg_t, wu_t), sem=("parallel",), hosted=hosted)


def _ffn_down(x, a_act, wd, name, hosted=()):
    t = x.shape[0]
    f = wd.shape[0]

    def body(x_ref, a_ref, wd_ref, y_ref):
        y_ref[...] = x_ref[...] + 0.5 * _dot_nn(a_ref[...], wd_ref[...])

    return _call(
        body,
        name=name,
        grid=(t // TM,),
        in_specs=[_rows(TM, D_MODEL), _rows(TM, f), _resident((f, D_MODEL))],
        out_specs=[_rows(TM, D_MODEL)],
        out_shape=[jax.ShapeDtypeStruct((t, D_MODEL), F32)],
        args=(x, a_act, wd), sem=("parallel",), hosted=hosted)[0]


def _ffn_bwd_head(y, gamma_f, target, x, gamma, g_act, u_act, wg_t, wu_t, wd, name):
    t = x.shape[0]
    f = wg_t.shape[0]

    def body(y_ref, gamf_ref, t_ref, x_ref, gam_ref, g_ref, u_ref, wg_ref, wu_ref, wd_ref, dx_ref, dg_ref, du_ref,
             db_ref, dgam_ref, dgamf_ref, loss_ref):
        yh, ry = _rms(y_ref[...])
        gam_f = gamf_ref[...]
        e = yh * gam_f - t_ref[...]
        dv, dgam_f = _rms_bwd(e * (1.0 / D_MODEL), yh, ry, gam_f)
        db = (0.5 * dv).astype(BF16)
        db_ref[...] = db
        for j in range(f // FC):
            sl = slice(j * FC, (j + 1) * FC)
            da = _dot_nt(db, wd_ref[sl, :])
            g = g_ref[:, sl].astype(F32)
            u = u_ref[:, sl].astype(F32)
            s = jax.nn.sigmoid(g)
            dg_ref[:, sl] = (da * u * (s * (1.0 + g * (1.0 - s)))).astype(BF16)
            du_ref[:, sl] = (da * (g * s)).astype(BF16)
        dh = _dot_nn(dg_ref[...], wg_ref[...]) + _dot_nn(du_ref[...], wu_ref[...])
        xh, r = _rms(x_ref[...])
        dxn, dgam = _rms_bwd(dh, xh, r, gam_ref[...])
        dx_ref[...] = dv + dxn

        @pl.when(pl.program_id(0) == 0)
        def _():
            dgam_ref[...] = jnp.zeros_like(dgam_ref)
            dgamf_ref[...] = jnp.zeros_like(dgamf_ref)
            loss_ref[...] = jnp.zeros_like(loss_ref)

        dgam_ref[...] += dgam
        dgamf_ref[...] += dgam_f
        loss_ref[...] += _colsum8(e * e) * (0.5 / D_MODEL)

    acc = pl.BlockSpec((8, D_MODEL), lambda i: (0, 0))
    return _call(
        body,
        name=name,
        grid=(t // TM,),
        in_specs=[_rows(TM, D_MODEL), _resident((1, D_MODEL)), _rows(TM, D_MODEL), _rows(TM, D_MODEL),
                  _resident((1, D_MODEL)), _rows(TM, f), _rows(TM, f),
                  _resident((f, D_MODEL)), _resident((f, D_MODEL)), _resident((f, D_MODEL))],
        out_specs=[_rows(TM, D_MODEL), _rows(TM, f), _rows(TM, f), _rows(TM, D_MODEL), acc, acc, acc],
        out_shape=[jax.ShapeDtypeStruct((t, D_MODEL), F32), jax.ShapeDtypeStruct((t, f), BF16),
                   jax.ShapeDtypeStruct((t, f), BF16), jax.ShapeDtypeStruct((t, D_MODEL), BF16),
                   jax.ShapeDtypeStruct((8, D_MODEL), F32), jax.ShapeDtypeStruct((8, D_MODEL), F32),
                   jax.ShapeDtypeStruct((8, D_MODEL), F32)],
        args=(y, gamma_f, target, x, gamma, g_act, u_act, wg_t, wu_t, wd), sem=("arbitrary",))


def _ffn_bwd_act(d, g_act, u_act, wd, name, hosted=()):
    t = d.shape[0]
    f = wd.shape[0]

    def body(d_ref, g_ref, u_ref, wd_ref, dg_ref, du_ref):
        db = (0.5 * d_ref[...]).astype(BF16)
        for j in range(f // FC):
            sl = slice(j * FC, (j + 1) * FC)
            da = _dot_nt(db, wd_ref[sl, :])
            g = g_ref[:, sl].astype(F32)
            u = u_ref[:, sl].astype(F32)
            s = jax.nn.sigmoid(g)
            dg_ref[:, sl] = (da * u * (s * (1.0 + g * (1.0 - s)))).astype(BF16)
            du_ref[:, sl] = (da * (g * s)).astype(BF16)

    return _call(
        body,
        name=name,
        grid=(t // TM,),
        in_specs=[_rows(TM, D_MODEL), _rows(TM, f), _rows(TM, f), _resident((f, D_MODEL))],
        out_specs=[_rows(TM, f), _rows(TM, f)],
        out_shape=[jax.ShapeDtypeStruct((t, f), BF16), jax.ShapeDtypeStruct((t, f), BF16)],
        args=(d, g_act, u_act, wd), sem=("parallel",), hosted=hosted)


def _ffn_bwd_in(d, x, gamma, dg, du, wg_t, wu_t, name, hosted=()):
    t = x.shape[0]
    f = wg_t.shape[0]

    def body(d_ref, x_ref, gam_ref, dg_ref, du_ref, wg_ref, wu_ref, dx_ref, dgam_ref):
        dh = _dot_nn(dg_ref[...], wg_ref[...]) + _dot_nn(du_ref[...], wu_ref[...])
        xh, r = _rms(x_ref[...])
        dxn, dgam = _rms_bwd(dh, xh, r, gam_ref[...])
        dx_ref[...] = d_ref[...] + dxn

        @pl.when(pl.program_id(0) == 0)
        def _():
            dgam_ref[...] = jnp.zeros_like(dgam_ref)

        dgam_ref[...] += dgam

    return _call(
        body,
        name=name,
        grid=(t // TM,),
        in_specs=[_rows(TM, D_MODEL), _rows(TM, D_MODEL), _resident((1, D_MODEL)), _rows(TM, f), _rows(TM, f),
                  _resident((f, D_MODEL)), _resident((f, D_MODEL))],
        out_specs=[_rows(TM, D_MODEL), pl.BlockSpec((8, D_MODEL), lambda i: (0, 0))],
        out_shape=[jax.ShapeDtypeStruct((t, D_MODEL), F32), jax.ShapeDtypeStruct((8, D_MODEL), F32)],
        args=(d, x, gamma, dg, du, wg_t, wu_t), sem=("arbitrary",), hosted=hosted)


def _mm_tn(pieces, b, name, tile=256, hosted=()):
    t, n = b.shape
    npc = len(pieces)
    counts = [p.shape[1] // tile for p in pieces]
    los = [sum(counts[:k]) for k in range(npc)]
    total = sum(counts)

    def body(*refs):
        a_refs, b_ref, o_ref = refs[:npc], refs[npc], refs[npc + 1]
        i = pl.program_id(0)
        for k in range(npc):
            @pl.when(jnp.logical_and(i >= los[k], i < los[k] + counts[k]))
            def _(k=k):
                o_ref[...] = _dot_tn(a_refs[k][...], b_ref[...]).astype(BF16)

    def a_spec(k):
        return pl.BlockSpec((t, tile), lambda i: (0, jnp.clip(i - los[k], 0, counts[k] - 1)))

    return _call(
        body,
        name=name,
        grid=(total,),
        in_specs=[a_spec(k) for k in range(npc)] + [_resident((t, n))],
        out_specs=[pl.BlockSpec((tile, n), lambda i: (i, 0))],
        out_shape=[jax.ShapeDtypeStruct((total * tile, n), BF16)],
        args=(*pieces, b), sem=("parallel",), hosted=hosted)[0]


def _proj_fwd(x, gamma, win_t, hosted=()):
    t = x.shape[0]

    def body(x_ref, gam_ref, w_ref, h_ref, qa_ref, qb_ref, gt_ref):
        xh, _ = _rms(x_ref[...])
        h = (xh * gam_ref[...]).astype(BF16)
        h_ref[...] = h
        for j in range(QKV_A // FC):
            qa_ref[:, j * FC:(j + 1) * FC] = _dot_nt(h, w_ref[j * FC:(j + 1) * FC, :]).astype(BF16)
        for j in range(QKV_B // FC):
            lo = QKV_A + j * FC
            qb_ref[:, j * FC:(j + 1) * FC] = _dot_nt(h, w_ref[lo:lo + FC, :]).astype(BF16)
        for j in range(2 * D_MODEL // FC):
            lo = QKV_A + QKV_B + j * FC
            gt_ref[:, j * FC:(j + 1) * FC] = _dot_nt(h, w_ref[lo:lo + FC, :])

    return _call(
        body,
        name="proj_fwd",
        grid=(t // TM,),
        in_specs=[_rows(TM, D_MODEL), _resident((1, D_MODEL)), _resident((IN_WIDTH, D_MODEL))],
        out_specs=[_rows(TM, D_MODEL), _rows(TM, QKV_A), _rows(TM, QKV_B), _rows(TM, 2 * D_MODEL)],
        out_shape=[jax.ShapeDtypeStruct((t, D_MODEL), BF16), jax.ShapeDtypeStruct((t, QKV_A), BF16),
                   jax.ShapeDtypeStruct((t, QKV_B), BF16), jax.ShapeDtypeStruct((t, 2 * D_MODEL), F32)],
        args=(x, gamma, win_t), sem=("parallel",), hosted=hosted)


def _proj_bwd(d, x, gamma, pieces, win_t, hosted=()):
    t = x.shape[0]
    npc = len(pieces)
    widths = [p.shape[1] for p in pieces]
    los = [sum(widths[:k]) for k in range(npc)]

    def body(*refs):
        d_ref, x_ref, gam_ref = refs[:3]
        p_refs = refs[3:3 + npc]
        w_ref, dx_ref, db_ref, dgam_ref = refs[3 + npc:]
        dh = _dot_nn(p_refs[0][...], w_ref[0:widths[0], :])
        for k in range(1, npc):
            dh += _dot_nn(p_refs[k][...], w_ref[los[k]:los[k] + widths[k], :])
        xh, r = _rms(x_ref[...])
        dxn, dgam = _rms_bwd(dh, xh, r, gam_ref[...])
        dx = d_ref[...] + dxn
        dx_ref[...] = dx
        db_ref[...] = (0.5 * dx).astype(BF16)

        @pl.when(pl.program_id(0) == 0)
        def _():
            dgam_ref[...] = jnp.zeros_like(dgam_ref)

        dgam_ref[...] += dgam

    return _call(
        body,
        name="proj_bwd",
        grid=(t // TM,),
        in_specs=[_rows(TM, D_MODEL), _rows(TM, D_MODEL), _resident((1, D_MODEL))] + [_rows(TM, w) for w in widths]
        + [_resident((IN_WIDTH, D_MODEL))],
        out_specs=[_rows(TM, D_MODEL), _rows(TM, D_MODEL), pl.BlockSpec((8, D_MODEL), lambda i: (0, 0))],
        out_shape=[jax.ShapeDtypeStruct((t, D_MODEL), F32), jax.ShapeDtypeStruct((t, D_MODEL), BF16),
                   jax.ShapeDtypeStruct((8, D_MODEL), F32)],
        args=(d, x, gamma, *pieces, win_t), sem=("arbitrary",), hosted=hosted)


def _lane_half(shape):
    return lax.broadcasted_iota(jnp.int32, shape, len(shape) - 1) // D_HEAD


def _band_weights(q, kk, bias, sink, qs, pad):
    s = _dot_nt(q, kk) + bias
    if qs is not None:
        col = lax.broadcasted_iota(jnp.int32, s.shape, 1)
        s = jnp.where(col + qs >= pad, s, NEG_INF)
    m = jnp.max(s, axis=-1, keepdims=True)
    if sink is not None:
        m = jnp.maximum(m, sink)
    return jnp.exp(s - m), m


def _weighted_values(p, vv_ones, sink, m):
    r = _dot_nn(p.astype(BF16), vv_ones)
    den = r[:, LANES:2 * LANES]
    if sink is not None:
        den = den + jnp.exp(sink - m)
    return r[:, 0:LANES] / den


def _band_softmax(q, kk, bias, sink, qs, pad):
    p, m = _band_weights(q, kk, bias, sink, qs, pad)
    den = jnp.sum(p, axis=-1, keepdims=True)
    if sink is not None:
        den = den + jnp.exp(sink - m)
    return p, m, 1.0 / den


def _fill_padded(dst, src, pad):
    dst[0:pad, :] = jnp.zeros((pad,) + dst.shape[1:], dst.dtype)
    dst[pad:, :] = src


FWD_PAIRS = 4
BWD_PAIRS = 2


def _attn_a_fwd(qkv, bias, hosted=()):
    bsz, s_len, _ = qkv.shape
    pad = A_PREV * CHUNK
    band = TQ + pad
    pp = FWD_PAIRS
    w = pp * LANES
    nb = A_WIDTH // w

    def body(q_ref, k_ref, v_ref, b_ref, o_ref, kp, vp):
        i = pl.program_id(2)

        @pl.when(i == 0)
        def _():
            _fill_padded(kp, k_ref[...], pad)
            _fill_padded(vp, v_ref[...], pad)

        qs = pl.multiple_of(i * TQ, TQ)
        half = _lane_half((1, LANES))

        ones = jnp.ones((band, LANES), BF16)

        def block(masked):
            for pr in range(pp):
                sl = slice(pr * LANES, (pr + 1) * LANES)
                kk = kp[pl.ds(qs, band), sl]
                vv = jnp.concatenate([vp[pl.ds(qs, band), sl], ones], axis=1)
                q = q_ref[:, sl] * SCALE
                outs = []
                for j in range(2):
                    qm = jnp.where(half == j, q, jnp.zeros_like(q))
                    p, m = _band_weights(qm, kk, b_ref[2 * pr + j], None, qs if masked else None, pad)
                    outs.append(_weighted_values(p, vv, None, m))
                o_ref[:, sl] = jnp.where(half == 0, outs[0], outs[1]).astype(BF16)

        pl.when(i < pad // TQ)(lambda: block(True))
        pl.when(i >= pad // TQ)(lambda: block(False))

    return _call(
        body,
        name="attn_a_fwd",
        grid=(bsz, nb, s_len // TQ),
        in_specs=[pl.BlockSpec((None, TQ, w), lambda b, g, i: (b, i, g)),
                  pl.BlockSpec((None, s_len, w), lambda b, g, i: (b, 0, nb + g)),
                  pl.BlockSpec((None, s_len, w), lambda b, g, i: (b, 0, 2 * nb + g)),
                  pl.BlockSpec((2 * pp, TQ, band), lambda b, g, i: (g, 0, 0))],
        out_specs=[pl.BlockSpec((None, TQ, w), lambda b, g, i: (b, i, g))],
        out_shape=[jax.ShapeDtypeStruct((bsz, s_len, A_WIDTH), BF16)],
        scratch_shapes=[pltpu.VMEM((pad + s_len, w), BF16), pltpu.VMEM((pad + s_len, w), BF16)],
        args=(qkv, qkv, qkv, bias), sem=("arbitrary", "arbitrary", "arbitrary"), hosted=hosted)[0]


def _attn_a_bwd(qkv, bias, do, hosted=()):
    bsz, s_len, _ = qkv.shape
    pad = A_PREV * CHUNK
    band = TQ + pad
    n_i = s_len // TQ
    pp = BWD_PAIRS
    w = pp * LANES
    nb = A_WIDTH // w

    def body(q_ref, k_ref, v_ref, b_ref, do_ref, dq_ref, dk_ref, dv_ref, dbias_ref, kp, vp, dk_acc, dv_acc):
        b = pl.program_id(1)
        i = pl.program_id(2)

        @pl.when(i == 0)
        def _():
            _fill_padded(kp, k_ref[...], pad)
            _fill_padded(vp, v_ref[...], pad)
            dk_acc[...] = jnp.zeros_like(dk_acc)
            dv_acc[...] = jnp.zeros_like(dv_acc)

        @pl.when(jnp.logical_and(b == 0, i == 0))
        def _():
            dbias_ref[...] = jnp.zeros_like(dbias_ref)

        qs = pl.multiple_of(i * TQ, TQ)
        half = _lane_half((1, LANES))

        def block(masked):
            for pr in range(pp):
                sl = slice(pr * LANES, (pr + 1) * LANES)
                kk = kp[pl.ds(qs, band), sl]
                vv = vp[pl.ds(qs, band), sl]
                q = q_ref[:, sl] * SCALE
                dd = do_ref[:, sl]
                dqs, dks, dvs = [], [], []
                for j in range(2):
                    qm = jnp.where(half == j, q, jnp.zeros_like(q))
                    dm = jnp.where(half == j, dd, jnp.zeros_like(dd))
                    p, _, inv = _band_softmax(qm, kk, b_ref[2 * pr + j], None, qs if masked else None, pad)
                    pn = p * inv
                    dp = _dot_nt(dm, vv)
                    delta = jnp.sum(pn * dp, axis=-1, keepdims=True)
                    ds = pn * (dp - delta)
                    dbias_ref[2 * pr + j] += ds[:, band - REL_COLS:]
                    dsb = ds.astype(BF16)
                    dqs.append(_dot_nn(dsb, kk))
                    dks.append(_dot_tn(dsb, q))
                    dvs.append(_dot_tn(pn.astype(BF16), dd))
                dq_ref[:, sl] = (jnp.where(half == 0, dqs[0], dqs[1]) * SCALE).astype(BF16)
                dk_acc[pl.ds(qs, band), sl] += jnp.where(half == 0, dks[0], dks[1])
                dv_acc[pl.ds(qs, band), sl] += jnp.where(half == 0, dvs[0], dvs[1])

        pl.when(i < pad // TQ)(lambda: block(True))
        pl.when(i >= pad // TQ)(lambda: block(False))

        @pl.when(i == n_i - 1)
        def _():
            dk_ref[...] = dk_acc[pad:, :].astype(BF16)
            dv_ref[...] = dv_acc[pad:, :].astype(BF16)

    qspec = pl.BlockSpec((None, TQ, w), lambda g, b, i: (b, i, g))
    kvout = pl.BlockSpec((None, s_len, w), lambda g, b, i: (b, 0, g))
    wide = jax.ShapeDtypeStruct((bsz, s_len, A_WIDTH), BF16)
    return _call(
        body,
        name="attn_a_bwd",
        grid=(nb, bsz, n_i),
        in_specs=[qspec,
                  pl.BlockSpec((None, s_len, w), lambda g, b, i: (b, 0, nb + g)),
                  pl.BlockSpec((None, s_len, w), lambda g, b, i: (b, 0, 2 * nb + g)),
                  pl.BlockSpec((2 * pp, TQ, band), lambda g, b, i: (g, 0, 0)),
                  qspec],
        out_specs=[qspec, kvout, kvout, pl.BlockSpec((2 * pp, TQ, REL_COLS), lambda g, b, i: (g, 0, 0))],
        out_shape=[wide, wide, wide, jax.ShapeDtypeStruct((A_HEADS, TQ, REL_COLS), F32)],
        scratch_shapes=[pltpu.VMEM((pad + s_len, w), BF16), pltpu.VMEM((pad + s_len, w), BF16),
                        pltpu.VMEM((pad + s_len, w), F32), pltpu.VMEM((pad + s_len, w), F32)],
        args=(qkv, qkv, qkv, bias, do), sem=("arbitrary", "arbitrary", "arbitrary"), hosted=hosted)


def _fill_padded_dup(dst, src, pad, h, half):
    other = pltpu.roll(src, D_HEAD, 1)
    _fill_padded(dst, jnp.where(half == h, src, other), pad)


def _attn_b_fwd(qkv, bias, sink):
    bsz, s_len, _ = qkv.shape
    pad = B_PREV * CHUNK
    band = TQ + pad
    kcol = B_Q_WIDTH // LANES
    npair = B_Q_HEADS // 2

    def body(q_ref, k_ref, v_ref, b_ref, s_ref, o_ref, kp, vp):
        i = pl.program_id(1)
        half = _lane_half((1, LANES))

        @pl.when(i == 0)
        def _():
            for h in range(B_KV_HEADS):
                _fill_padded_dup(kp.at[h], k_ref[...], pad, h, half)
                _fill_padded_dup(vp.at[h], v_ref[...], pad, h, half)

        qs = pl.multiple_of(i * TQ, TQ)

        ones = jnp.ones((band, LANES), BF16)

        def block(masked):
            for pr in range(npair):
                h = pr // (B_GROUP // 2)
                sl = slice(pr * LANES, (pr + 1) * LANES)
                kk = kp[h, pl.ds(qs, band), :]
                vv = jnp.concatenate([vp[h, pl.ds(qs, band), :], ones], axis=1)
                q = q_ref[:, sl] * SCALE
                outs = []
                for j in range(2):
                    qm = jnp.where(half == j, q, jnp.zeros_like(q))
                    sink = s_ref[2 * pr + j][0:1, 0:1]
                    p, m = _band_weights(qm, kk, b_ref[2 * pr + j], sink, qs if masked else None, pad)
                    outs.append(_weighted_values(p, vv, sink, m))
                o_ref[:, sl] = jnp.where(half == 0, outs[0], outs[1]).astype(BF16)

        pl.when(i < -(-pad // TQ))(lambda: block(True))
        pl.when(i >= -(-pad // TQ))(lambda: block(False))

    return pl.pallas_call(
        body,
        name="attn_b_fwd",
        grid=(bsz, s_len // TQ),
        in_specs=[pl.BlockSpec((None, TQ, B_Q_WIDTH), lambda b, i: (b, i, 0)),
                  pl.BlockSpec((None, s_len, LANES), lambda b, i: (b, 0, kcol)),
                  pl.BlockSpec((None, s_len, LANES), lambda b, i: (b, 0, kcol + 1)),
                  pl.BlockSpec((B_Q_HEADS, TQ, band), lambda b, i: (0, 0, 0)),
                  pl.BlockSpec((B_Q_HEADS, 8, LANES), lambda b, i: (0, 0, 0))],
        out_specs=pl.BlockSpec((None, TQ, B_Q_WIDTH), lambda b, i: (b, i, 0)),
        out_shape=jax.ShapeDtypeStruct((bsz, s_len, B_Q_WIDTH), BF16),
        scratch_shapes=[pltpu.VMEM((B_KV_HEADS, pad + s_len, LANES), BF16),
                        pltpu.VMEM((B_KV_HEADS, pad + s_len, LANES), BF16)],
        compiler_params=_cparams(("arbitrary", "arbitrary")),
    )(qkv, qkv, qkv, bias, sink)


def _attn_b_bwd(qkv, bias, sink, do, hosted=()):
    bsz, s_len, _ = qkv.shape
    pad = B_PREV * CHUNK
    band = TQ + pad
    kcol = B_Q_WIDTH // LANES
    npair = B_Q_HEADS // 2
    n_i = s_len // TQ

    pp = B_GROUP // 2
    w = pp * LANES

    def body(q_ref, k_ref, v_ref, b_ref, s_ref, do_ref, dq_ref, dkv_ref, dsink_ref, kp, vp, dk_acc, dv_acc):
        b = pl.program_id(0)
        h = pl.program_id(1)
        i = pl.program_id(2)
        half = _lane_half((1, LANES))

        @pl.when(i == 0)
        def _():
            _fill_padded_dup(kp, k_ref[...], pad, h, half)
            _fill_padded_dup(vp, v_ref[...], pad, h, half)

        @pl.when(jnp.logical_and(h == 0, i == 0))
        def _():
            dk_acc[...] = jnp.zeros_like(dk_acc)
            dv_acc[...] = jnp.zeros_like(dv_acc)

        @pl.when(jnp.logical_and(b == 0, jnp.logical_and(h == 0, i == 0)))
        def _():
            dsink_ref[...] = jnp.zeros_like(dsink_ref)

        qs = pl.multiple_of(i * TQ, TQ)

        def block(masked):
            kk = kp[pl.ds(qs, band), :]
            vv = vp[pl.ds(qs, band), :]
            dk2 = jnp.zeros((band, LANES), F32)
            dv2 = jnp.zeros((band, LANES), F32)
            for pr in range(pp):
                sl = slice(pr * LANES, (pr + 1) * LANES)
                q = q_ref[:, sl] * SCALE
                dd = do_ref[:, sl]
                dqs, dks, dvs = [], [], []
                for j in range(2):
                    qm = jnp.where(half == j, q, jnp.zeros_like(q))
                    dm = jnp.where(half == j, dd, jnp.zeros_like(dd))
                    sink = s_ref[2 * pr + j][0:1, 0:1]
                    p, m, inv = _band_softmax(qm, kk, b_ref[2 * pr + j], sink, qs if masked else None, pad)
                    pn = p * inv
                    dp = _dot_nt(dm, vv)
                    delta = jnp.sum(pn * dp, axis=-1, keepdims=True)
                    ds = pn * (dp - delta)
                    dsb = ds.astype(BF16)
                    dqs.append(_dot_nn(dsb, kk))
                    dks.append(_dot_tn(dsb, q))
                    dvs.append(_dot_tn(pn.astype(BF16), dd))
                    dsk = jnp.sum(-(jnp.exp(sink - m) * inv) * delta, axis=0, keepdims=True)
                    dsink_ref[2 * pp * h + 2 * pr + j] += jnp.broadcast_to(dsk, (8, LANES))
                dq_ref[:, sl] = (jnp.where(half == 0, dqs[0], dqs[1]) * SCALE).astype(BF16)
                dk2 = dk2 + jnp.where(half == 0, dks[0], dks[1])
                dv2 = dv2 + jnp.where(half == 0, dvs[0], dvs[1])
            dk_acc[pl.ds(qs, band), :] += jnp.where(half == h, dk2 + pltpu.roll(dk2, D_HEAD, 1), 0.0)
            dv_acc[pl.ds(qs, band), :] += jnp.where(half == h, dv2 + pltpu.roll(dv2, D_HEAD, 1), 0.0)

        pl.when(i < -(-pad // TQ))(lambda: block(True))
        pl.when(i >= -(-pad // TQ))(lambda: block(False))

        @pl.when(jnp.logical_and(h == B_KV_HEADS - 1, i == n_i - 1))
        def _():
            dkv_ref[:, 0:LANES] = dk_acc[pad:, :].astype(BF16)
            dkv_ref[:, LANES:2 * LANES] = dv_acc[pad:, :].astype(BF16)

    qspec = pl.BlockSpec((None, TQ, w), lambda b, h, i: (b, i, h))
    return _call(
        body,
        name="attn_b_bwd",
        grid=(bsz, B_KV_HEADS, n_i),
        in_specs=[qspec,
                  pl.BlockSpec((None, s_len, LANES), lambda b, h, i: (b, 0, kcol)),
                  pl.BlockSpec((None, s_len, LANES), lambda b, h, i: (b, 0, kcol + 1)),
                  pl.BlockSpec((2 * pp, TQ, band), lambda b, h, i: (h, 0, 0)),
                  pl.BlockSpec((2 * pp, 8, LANES), lambda b, h, i: (h, 0, 0)),
                  qspec],
        out_specs=[qspec, pl.BlockSpec((None, s_len, 2 * LANES), lambda b, h, i: (b, 0, 0)),
                   pl.BlockSpec((B_Q_HEADS, 8, LANES), lambda b, h, i: (0, 0, 0))],
        out_shape=[jax.ShapeDtypeStruct((bsz, s_len, B_Q_WIDTH), BF16),
                   jax.ShapeDtypeStruct((bsz, s_len, 2 * B_KV_WIDTH), BF16),
                   jax.ShapeDtypeStruct((B_Q_HEADS, 8, LANES), F32)],
        scratch_shapes=[pltpu.VMEM((pad + s_len, LANES), BF16), pltpu.VMEM((pad + s_len, LANES), BF16),
                        pltpu.VMEM((pad + s_len, LANES), F32), pltpu.VMEM((pad + s_len, LANES), F32)],
        args=(qkv, qkv, qkv, bias, sink, do), sem=("arbitrary", "arbitrary", "arbitrary"), hosted=hosted)


REL_COLS = 3 * 128
REL_WRAP = 512


def _bias_a_build(tv, hosted=()):
    h = tv.shape[0]
    pad = A_PREV * CHUNK
    band = TQ + pad

    def body(tv_ref, o_ref):
        row = tv_ref[...]
        x = jnp.broadcast_to(row, (TQ, REL_WRAP))
        r = lax.broadcasted_iota(jnp.int32, x.shape, 0)
        for bit in range(8):
            sh = 1 << bit
            x = jnp.where((r & sh) != 0, pltpu.roll(x, sh, 1), x)
        far = jnp.broadcast_to(row[:, 0:1], (TQ, band - REL_COLS))
        full = jnp.concatenate([far, x[:, REL_WRAP // 2:REL_WRAP], x[:, 0:REL_COLS - REL_WRAP // 2]], axis=1)
        qc = (lax.broadcasted_iota(jnp.int32, full.shape, 0) + pad) // CHUNK
        kc = lax.broadcasted_iota(jnp.int32, full.shape, 1) // CHUNK
        ok = jnp.logical_and(kc <= qc, kc >= qc - A_PREV)
        o_ref[...] = jnp.where(ok, full, NEG_INF)

    return _call(
        body,
        name="bias_a_build",
        grid=(h,),
        in_specs=[pl.BlockSpec((None, 1, REL_WRAP), lambda hh: (hh, 0, 0))],
        out_specs=[pl.BlockSpec((None, TQ, band), lambda hh: (hh, 0, 0))],
        out_shape=[jax.ShapeDtypeStruct((h, TQ, band), F32)],
        args=(tv,), sem=("parallel",), hosted=hosted)[0]


def _relbias_grad(dbias):
    h, rows, _ = dbias.shape

    def body(d_ref, o_ref):
        x = d_ref[...]
        r = lax.broadcasted_iota(jnp.int32, x.shape, 0)
        c = lax.broadcasted_iota(jnp.int32, x.shape, 1) - r
        x = jnp.where(jnp.logical_and(c >= 1, c < REL_TABLE), x, 0.0)
        for bit in range(8):
            sh = 1 << bit
            x = jnp.where((r & sh) != 0, pltpu.roll(x, REL_COLS - sh, 1), x)
        diag = jnp.sum(x, axis=0, keepdims=True)
        lane = lax.broadcasted_iota(jnp.int32, diag.shape, 1)
        diag = jnp.where(jnp.logical_and(lane >= 1, lane < REL_TABLE), diag, 0.0)
        rest = -jnp.sum(diag, axis=1, keepdims=True)
        o_ref[...] = jnp.broadcast_to(jnp.where(lane == 0, rest, diag), o_ref.shape)

    return pl.pallas_call(
        body,
        name="relbias_grad",
        grid=(h,),
        in_specs=[pl.BlockSpec((None, rows, REL_COLS), lambda hh: (hh, 0, 0))],
        out_specs=pl.BlockSpec((None, 8, REL_COLS), lambda hh: (hh, 0, 0)),
        out_shape=jax.ShapeDtypeStruct((h, 8, REL_COLS), F32),
        compiler_params=_cparams(("parallel",)),
    )(dbias)


def _mix_out_fwd(x, oa, ob, gates, proj_t, wout):
    t = x.shape[0]

    def body(x_ref, oa_ref, ob_ref, gt_ref, pt_ref, wo_ref, y_ref, ya_ref, yb_ref, mg_ref):
        ya = _dot_nt(oa_ref[...], pt_ref[:, 0:A_WIDTH])
        yb = _dot_nt(ob_ref[...], pt_ref[:, A_WIDTH:A_WIDTH + B_Q_WIDTH])
        ya_ref[...] = ya.astype(BF16)
        yb_ref[...] = yb.astype(BF16)
        mg = jax.nn.sigmoid(gt_ref[:, 0:D_MODEL]) * ya + jax.nn.sigmoid(gt_ref[:, D_MODEL:2 * D_MODEL]) * yb
        mgb = mg.astype(BF16)
        mg_ref[...] = mgb
        y_ref[...] = x_ref[...] + _dot_nn(mgb, wo_ref[...])

    return pl.pallas_call(
        body,
        name="mix_out_fwd",
        grid=(t // TM,),
        in_specs=[_rows(TM, D_MODEL), _rows(TM, A_WIDTH), _rows(TM, B_Q_WIDTH), _rows(TM, 2 * D_MODEL),
                  _resident((D_MODEL, A_WIDTH + B_Q_WIDTH)), _resident((D_MODEL, D_MODEL))],
        out_specs=[_rows(TM, D_MODEL), _rows(TM, D_MODEL), _rows(TM, D_MODEL), _rows(TM, D_MODEL)],
        out_shape=[jax.ShapeDtypeStruct((t, D_MODEL), F32), jax.ShapeDtypeStruct((t, D_MODEL), BF16),
                   jax.ShapeDtypeStruct((t, D_MODEL), BF16), jax.ShapeDtypeStruct((t, D_MODEL), BF16)],
        compiler_params=_cparams(("parallel",)),
    )(x, oa, ob, gates, proj_t, wout)


def _mix_out_bwd(d, gates, ya, yb, mg, oa, ob, proj_t, wout, hosted=()):
    t = d.shape[0]
    nt = t // TM

    def body(d_ref, gt_ref, ya_ref, yb_ref, mg_ref, oa_ref, ob_ref, pt_ref, wo_ref,
             doa_ref, dob_ref, dgt_ref, gwo_ref, gwp_ref, acc_o, acc_p):
        i = pl.program_id(0)
        db = d_ref[...].astype(BF16)
        dmg = _dot_nt(db, wo_ref[...])
        sa = jax.nn.sigmoid(gt_ref[:, 0:D_MODEL])
        sb = jax.nn.sigmoid(gt_ref[:, D_MODEL:2 * D_MODEL])
        dya = (dmg * sa).astype(BF16)
        dyb = (dmg * sb).astype(BF16)
        dgt_ref[:, 0:D_MODEL] = (dmg * ya_ref[...].astype(F32) * (sa * (1.0 - sa))).astype(BF16)
        dgt_ref[:, D_MODEL:2 * D_MODEL] = (dmg * yb_ref[...].astype(F32) * (sb * (1.0 - sb))).astype(BF16)
        doa_ref[...] = _dot_nn(dya, pt_ref[:, 0:A_WIDTH]).astype(BF16)
        dob_ref[...] = _dot_nn(dyb, pt_ref[:, A_WIDTH:A_WIDTH + B_Q_WIDTH]).astype(BF16)

        @pl.when(i == 0)
        def _():
            acc_o[...] = jnp.zeros_like(acc_o)
            acc_p[...] = jnp.zeros_like(acc_p)

        acc_o[...] += _dot_tn(mg_ref[...], db)
        acc_p[:, 0:A_WIDTH] += _dot_tn(dya, oa_ref[...])
        acc_p[:, A_WIDTH:A_WIDTH + B_Q_WIDTH] += _dot_tn(dyb, ob_ref[...])

        @pl.when(i == nt - 1)
        def _():
            gwo_ref[...] = acc_o[...].astype(BF16)
            gwp_ref[...] = acc_p[...].astype(BF16)

    whole = pl.BlockSpec((D_MODEL, D_MODEL), lambda i: (0, 0))
    return _call(
        body,
        name="mix_out_bwd",
        grid=(nt,),
        in_specs=[_rows(TM, D_MODEL), _rows(TM, 2 * D_MODEL), _rows(TM, D_MODEL), _rows(TM, D_MODEL),
                  _rows(TM, D_MODEL), _rows(TM, A_WIDTH), _rows(TM, B_Q_WIDTH),
                  _resident((D_MODEL, A_WIDTH + B_Q_WIDTH)), _resident((D_MODEL, D_MODEL))],
        out_specs=[_rows(TM, A_WIDTH), _rows(TM, B_Q_WIDTH), _rows(TM, 2 * D_MODEL), whole, whole],
        out_shape=[jax.ShapeDtypeStruct((t, A_WIDTH), BF16), jax.ShapeDtypeStruct((t, B_Q_WIDTH), BF16),
                   jax.ShapeDtypeStruct((t, 2 * D_MODEL), BF16), jax.ShapeDtypeStruct((D_MODEL, D_MODEL), BF16),
                   jax.ShapeDtypeStruct((D_MODEL, D_MODEL), BF16)],
        scratch_shapes=[pltpu.VMEM((D_MODEL, D_MODEL), F32), pltpu.VMEM((D_MODEL, A_WIDTH + B_Q_WIDTH), F32)],
        args=(d, gates, ya, yb, mg, oa, ob, proj_t, wout), sem=("arbitrary",), hosted=hosted)


def _place():
    x, y, c = lax.axis_index("x"), lax.axis_index("y"), lax.axis_index("c")
    chips = [(1 - x, y), (x, 1 - y), (1 - x, 1 - y)]
    return x, y, c, chips


class _Gather:
    per = 8

    def __init__(self, shards):
        n = len(shards)
        self.inputs = list(shards)
        self.out_shape = [jax.ShapeDtypeStruct((N_DEV * s.shape[0], s.shape[1]), s.dtype) for s in shards]
        self.scratch = [pltpu.SemaphoreType.DMA((n * self.per,)), pltpu.SemaphoreType.DMA((n * self.per,)),
                        pltpu.SemaphoreType.DMA((n,))]
        self.result = None

    def _parts(self, ins, outs, sems):
        send_sems, recv_sems, local_sems = sems
        x, y, c, chips = _place()
        me, sibling = (x, y, c), (x, y, 1 - c)
        xn, yn, dg = chips
        n = len(ins)

        def rows(k, p, part=None):
            r = ins[k].shape[0]
            base = (4 * p[0] + 2 * p[1] + p[2]) * r
            if part is None:
                return outs[k].at[pl.ds(base, r), :]
            return outs[k].at[pl.ds(base + part * (r // 2), r // 2), :]

        def copy(k, slot, block, to, src=None, part=None):
            return pltpu.make_async_remote_copy(
                src_ref=rows(k, block, part) if src is None else src, dst_ref=rows(k, block, part),
                send_sem=send_sems.at[k * self.per + slot], recv_sem=recv_sems.at[k * self.per + slot],
                device_id=to, device_id_type=MESH)

        mine = [pltpu.make_async_copy(ins[k], rows(k, me), local_sems.at[k]) for k in range(n)]
        sends, lands = [], []
        for k in range(n):
            sends.append({
                0: copy(k, 0, me, sibling, src=ins[k]),
                1: copy(k, 1, me, (*xn, c), src=ins[k]),
                2: copy(k, 2, me, (*yn, c), src=ins[k]),
                3: copy(k, 3, (*xn, c), (*yn, c), part=0),
                4: copy(k, 4, (*yn, c), (*xn, c), part=1),
                5: copy(k, 5, (*xn, c), sibling),
                6: copy(k, 6, (*yn, c), sibling),
                7: copy(k, 7, (*dg, c), sibling)})
            lands.append({
                0: copy(k, 0, sibling, me),
                1: copy(k, 1, (*xn, c), me),
                2: copy(k, 2, (*yn, c), me),
                3: copy(k, 3, (*dg, c), me, part=0),
                4: copy(k, 4, (*dg, c), me, part=1),
                5: copy(k, 5, (*xn, 1 - c), me),
                6: copy(k, 6, (*yn, 1 - c), me),
                7: copy(k, 7, (*dg, 1 - c), me)})
        return n, mine, sends, lands

    def start(self, ins, outs, sems):
        n, mine, sends, _ = self._parts(ins, outs, sems)
        for cp in mine:
            cp.start()
        for slot in (0, 1, 2):
            for k in range(n):
                sends[k][slot].start()

    def relay(self, ins, outs, sems):
        n, _, sends, lands = self._parts(ins, outs, sems)
        for k in range(n):
            lands[k][1].wait_recv()
            sends[k][3].start()
            sends[k][5].start()
        for k in range(n):
            lands[k][2].wait_recv()
            sends[k][4].start()
            sends[k][6].start()

    def forward(self, ins, outs, sems):
        n, _, sends, lands = self._parts(ins, outs, sems)
        for k in range(n):
            lands[k][3].wait_recv()
            lands[k][4].wait_recv()
            sends[k][7].start()

    def finish(self, ins, outs, sems):
        n, mine, sends, lands = self._parts(ins, outs, sems)
        for k in range(n):
            for slot in (0, 5, 6, 7):
                lands[k][slot].wait_recv()
        for k in range(n):
            for slot in range(self.per):
                sends[k][slot].wait_send()
        for cp in mine:
            cp.wait()


class _PairExchange:
    def __init__(self, grads):
        n = len(grads)
        self.inputs = list(grads)
        self.out_shape = [jax.ShapeDtypeStruct((g.shape[0] // 2, g.shape[1]), g.dtype) for g in grads]
        self.scratch = [pltpu.SemaphoreType.DMA((n * N_CHIP,)), pltpu.SemaphoreType.DMA((n * N_CHIP,))]
        self.result = None

    def _copies(self, ins, outs, sems):
        send_sems, recv_sems = sems
        x, y, c, _ = _place()
        copies = []
        for k in range(len(ins)):
            r = ins[k].shape[0] // N_DEV
            for q in range(N_CHIP):
                copies.append(pltpu.make_async_remote_copy(
                    src_ref=ins[k].at[pl.ds((2 * q + 1 - c) * r, r), :], dst_ref=outs[k].at[pl.ds(q * r, r), :],
                    send_sem=send_sems.at[k * N_CHIP + q], recv_sem=recv_sems.at[k * N_CHIP + q],
                    device_id=(x, y, 1 - c), device_id_type=MESH))
        return copies

    def start(self, ins, outs, sems):
        for cp in self._copies(ins, outs, sems):
            cp.start()

    def relay(self, ins, outs, sems):
        pass

    def forward(self, ins, outs, sems):
        pass

    def finish(self, ins, outs, sems):
        copies = self._copies(ins, outs, sems)
        for cp in copies:
            cp.wait_recv()
        for cp in copies:
            cp.wait_send()


class _ChipExchange(_PairExchange):
    def __init__(self, psums):
        n = len(psums)
        self.inputs = list(psums)
        self.out_shape = [jax.ShapeDtypeStruct((3 * p.shape[0] // N_CHIP, p.shape[1]), p.dtype) for p in psums]
        self.scratch = [pltpu.SemaphoreType.DMA((n * 3,)), pltpu.SemaphoreType.DMA((n * 3,))]
        self.result = None

    def _copies(self, ins, outs, sems):
        send_sems, recv_sems = sems
        _, _, c, chips = _place()
        copies = []
        for k in range(len(ins)):
            r = ins[k].shape[0] // N_CHIP
            for j, chip in enumerate(chips):
                copies.append(pltpu.make_async_remote_copy(
                    src_ref=ins[k].at[pl.ds((2 * chip[0] + chip[1]) * r, r), :], dst_ref=outs[k].at[pl.ds(j * r, r), :],
                    send_sem=send_sems.at[k * 3 + j], recv_sem=recv_sems.at[k * 3 + j],
                    device_id=(*chip, c), device_id_type=MESH))
        return copies


def _exchange_alone(xchg, name):
    n_in, n_out = len(xchg.inputs), len(xchg.out_shape)

    def body(*refs):
        ins, outs, sems = refs[:n_in], refs[n_in:n_in + n_out], refs[n_in + n_out:]
        xchg.start(ins, outs, sems)
        xchg.relay(ins, outs, sems)
        xchg.forward(ins, outs, sems)
        xchg.finish(ins, outs, sems)

    xchg.result = list(pl.pallas_call(
        body, name=name, in_specs=[_hbm()] * n_in, out_specs=[_hbm()] * n_out, out_shape=xchg.out_shape,
        scratch_shapes=xchg.scratch)(*xchg.inputs))
    return xchg.result


def _pair_sum(core, grads, recvd, name):
    n = len(grads)
    r = grads[0].shape[0] // N_DEV
    cdim = grads[0].shape[1]
    tr = r // 2 if r % 32 == 0 else r
    nt = r // tr

    def body(core_ref, *refs):
        del core_ref
        for k in range(n):
            refs[2 * n + k][...] = (refs[k][...].astype(F32) + refs[n + k][...].astype(F32)).astype(BF16)

    gspec = pl.BlockSpec((tr, cdim), lambda q, i, core_ref: ((2 * q + core_ref[0]) * nt + i, 0))
    rspec = pl.BlockSpec((tr, cdim), lambda q, i, core_ref: (q * nt + i, 0))
    return pl.pallas_call(
        body,
        name=name,
        grid_spec=pltpu.PrefetchScalarGridSpec(
            num_scalar_prefetch=1, grid=(N_CHIP, nt), in_specs=[gspec] * n + [rspec] * n, out_specs=[rspec] * n),
        out_shape=[jax.ShapeDtypeStruct((N_CHIP * r, cdim), BF16) for _ in range(n)],
        compiler_params=_cparams(("parallel", "parallel")),
    )(core, *grads, *recvd)


def _final_sum(chip, psums, recvd, name):
    n = len(psums)
    r = psums[0].shape[0] // N_CHIP
    cdim = psums[0].shape[1]
    tr = r // 2 if r % 32 == 0 else r
    nt = r // tr

    def body(chip_ref, *refs):
        del chip_ref
        for k in range(n):
            got = refs[n + k]
            tot = refs[k][...].astype(F32) + got[0].astype(F32)
            tot = tot + got[1].astype(F32)
            tot = tot + got[2].astype(F32)
            refs[2 * n + k][...] = tot

    pspec = pl.BlockSpec((tr, cdim), lambda i, chip_ref: (chip_ref[0] * nt + i, 0))
    rspec = pl.BlockSpec((3, tr, cdim), lambda i, chip_ref: (0, i, 0))
    ospec = pl.BlockSpec((tr, cdim), lambda i, chip_ref: (i, 0))
    return pl.pallas_call(
        body,
        name=name,
        grid_spec=pltpu.PrefetchScalarGridSpec(
            num_scalar_prefetch=1, grid=(nt,), in_specs=[pspec] * n + [rspec] * n, out_specs=[ospec] * n),
        out_shape=[jax.ShapeDtypeStruct((r, cdim), F32) for _ in range(n)],
        compiler_params=_cparams(("parallel",)),
    )(chip, *psums, *[g.reshape(3, r, cdim) for g in recvd])


SMALL_ROWS = 16


def _all_reduce_small(part):
    def body(p_ref, o_ref, buf, send_sems, recv_sems):
        x, y, c, _ = _place()
        me = 4 * x + 2 * y + c
        buf[me] = p_ref[...]
        copies = []
        for d in range(1, N_DEV):
            peer = me ^ d
            copies.append(pltpu.make_async_remote_copy(
                src_ref=p_ref, dst_ref=buf.at[me], send_sem=send_sems.at[d - 1], recv_sem=recv_sems.at[d - 1],
                device_id=(peer // 4, (peer // 2) % 2, peer % 2), device_id_type=MESH))
        for cp in copies:
            cp.start()
        for cp in copies:
            cp.wait_recv()
        for cp in copies:
            cp.wait_send()
        tot = buf[0]
        for d in range(1, N_DEV):
            tot = tot + buf[d]
        o_ref[...] = tot

    return pl.pallas_call(
        body,
        name="all_reduce_small",
        in_specs=[pl.BlockSpec(memory_space=pltpu.VMEM)],
        out_specs=pl.BlockSpec(memory_space=pltpu.VMEM),
        out_shape=jax.ShapeDtypeStruct(part.shape, F32),
        scratch_shapes=[pltpu.VMEM((N_DEV,) + part.shape, F32), pltpu.SemaphoreType.DMA((N_DEV - 1,)),
                        pltpu.SemaphoreType.DMA((N_DEV - 1,))],
    )(part)


ADAMW_STEPS = 4


def _adamw(ws, gs, ms, vs, name, hosted=()):
    n = len(ws)
    steps = ADAMW_STEPS if all(w.shape[0] % (8 * ADAMW_STEPS) == 0 for w in ws) else 1
    c1 = 1.0 - ADAM_B1 ** ADAM_STEP
    c2 = 1.0 - ADAM_B2 ** ADAM_STEP

    def body(*refs):
        for k in range(n):
            w, g, m, v = (refs[j * n + k][...] for j in range(4))
            m2 = ADAM_B1 * m + (1.0 - ADAM_B1) * g
            v2 = ADAM_B2 * v + (1.0 - ADAM_B2) * (g * g)
            delta = -ADAM_LR * ((m2 / c1) / (jnp.sqrt(v2 / c2) + ADAM_EPS) + ADAM_WD * w)
            refs[4 * n + k][...] = delta
            refs[5 * n + k][...] = m2
            refs[6 * n + k][...] = v2

    specs = [pl.BlockSpec((w.shape[0] // steps, w.shape[1]), lambda i: (i, 0)) for w in ws]
    shapes = [jax.ShapeDtypeStruct(w.shape, F32) for w in ws]
    outs = _call(
        body,
        name=name,
        grid=(steps,),
        in_specs=specs * 4,
        out_specs=specs * 3,
        out_shape=shapes * 3,
        args=(*ws, *gs, *ms, *vs), sem=("parallel",), hosted=hosted)
    return outs[:n], outs[n:2 * n], outs[2 * n:]


def _adamw_reduced(chip, ws, psums, recvd, ms, vs, steps, name):
    n = len(ws)
    c1 = 1.0 - ADAM_B1 ** ADAM_STEP
    c2 = 1.0 - ADAM_B2 ** ADAM_STEP

    def body(chip_ref, *refs):
        del chip_ref
        for k in range(n):
            w, m, v = (refs[j * n + k][...] for j in (0, 3, 4))
            got = refs[2 * n + k]
            g = refs[n + k][...].astype(F32) + got[0].astype(F32)
            g = g + got[1].astype(F32)
            g = g + got[2].astype(F32)
            m2 = ADAM_B1 * m + (1.0 - ADAM_B1) * g
            v2 = ADAM_B2 * v + (1.0 - ADAM_B2) * (g * g)
            refs[5 * n + k][...] = g
            refs[6 * n + k][...] = -ADAM_LR * ((m2 / c1) / (jnp.sqrt(v2 / c2) + ADAM_EPS) + ADAM_WD * w)
            refs[7 * n + k][...] = m2
            refs[8 * n + k][...] = v2

    def blk(w):
        return (w.shape[0] // steps, w.shape[1])

    own = [pl.BlockSpec(blk(w), lambda i, chip_ref: (i, 0)) for w in ws]
    psum = [pl.BlockSpec(blk(w), lambda i, chip_ref: (chip_ref[0] * steps + i, 0)) for w in ws]
    recv = [pl.BlockSpec((3,) + blk(w), lambda i, chip_ref: (0, i, 0)) for w in ws]
    shapes = [jax.ShapeDtypeStruct(w.shape, F32) for w in ws]
    outs = pl.pallas_call(
        body,
        name=name,
        grid_spec=pltpu.PrefetchScalarGridSpec(
            num_scalar_prefetch=1, grid=(steps,), in_specs=own + psum + recv + own + own, out_specs=own * 4),
        out_shape=shapes * 4,
        compiler_params=_cparams(("parallel",)),
    )(chip, *ws, *psums, *[r.reshape((3,) + w.shape) for r, w in zip(recvd, ws)], *ms, *vs)
    return outs[:n], outs[n:2 * n], outs[2 * n:3 * n], outs[3 * n:]


def _bias_b():
    pad = B_PREV * CHUNK
    slopes = np.array([2.0 ** (-8.0 * (i + 1) / B_Q_HEADS) for i in range(B_Q_HEADS)], dtype=np.float32)
    dist = np.abs(np.arange(TQ)[:, None] - np.arange(TQ + pad)[None, :] + pad).astype(np.float32)
    bias = -slopes.reshape(B_Q_HEADS, 1, 1) * dist[None]
    qc = (np.arange(TQ)[:, None] + pad) // CHUNK
    kc = np.arange(TQ + pad)[None, :] // CHUNK
    allowed = (kc <= qc) & (kc >= qc - B_PREV)
    return np.where(allowed[None], bias, np.float32(NEG_INF)).astype(np.float32)


def kernel(x, ffn1_norm, ffn1_w_gate, ffn1_w_up, ffn1_w_down, mix_norm, w_in, rel_bias, sinks, w_proj_a, w_proj_b, w_out, ffn2_norm, ffn2_w_gate, ffn2_w_up, ffn2_w_down, final_norm, loss_target, m_ffn1_norm, m_ffn1_w_gate, m_ffn1_w_up, m_ffn1_w_down, m_mix_norm, m_w_in, m_rel_bias, m_sinks, m_w_proj_a, m_w_proj_b, m_w_out, m_ffn2_norm, m_ffn2_w_gate, m_ffn2_w_up, m_ffn2_w_down, m_final_norm, v_ffn1_norm, v_ffn1_w_gate, v_ffn1_w_up, v_ffn1_w_down, v_mix_norm, v_w_in, v_rel_bias, v_sinks, v_w_proj_a, v_w_proj_b, v_w_out, v_ffn2_norm, v_ffn2_w_gate, v_ffn2_w_up, v_ffn2_w_down, v_final_norm):
    bsz, s_len, _ = x.shape
    t = bsz * s_len
    core = lax.axis_index("c").astype(jnp.int32).reshape(1)
    chip = (2 * lax.axis_index("x") + lax.axis_index("y")).astype(jnp.int32).reshape(1)

    proj_rows = jnp.concatenate([w_proj_a.T, w_proj_b.T], axis=1)
    sh_g1, sh_u1, sh_d1, sh_in, sh_proj, sh_out, sh_g2, sh_u2, sh_d2 = _to_bf16(
        [ffn1_w_gate.T, ffn1_w_up.T, ffn1_w_down, w_in.T, proj_rows, w_out, ffn2_w_gate.T, ffn2_w_up.T, ffn2_w_down],
        "weights_to_bf16")

    gather_up1 = _Gather([sh_g1, sh_u1])
    far = jnp.broadcast_to(rel_bias[:, REL_TABLE - 1:REL_TABLE], (A_HEADS, REL_WRAP // 2))
    tv = jnp.concatenate([far, jnp.flip(rel_bias, axis=1), jnp.zeros((A_HEADS, REL_WRAP // 2 - REL_TABLE), F32)], axis=1)
    bias_a = _bias_a_build(tv.reshape(A_HEADS, 1, REL_WRAP), hosted=[gather_up1])
    wg1, wu1 = gather_up1.result
    gather_down1 = _Gather([sh_d1, sh_in])
    gather_out = _Gather([sh_proj, sh_out])
    gather_ffn2_gate = _Gather([sh_g2])
    gather_ffn2_rest = _Gather([sh_u2, sh_d2])

    x0 = x.reshape(t, D_MODEL)
    tgt = loss_target.reshape(t, D_MODEL)
    gam1, gam2, gam3, gam4 = (g.reshape(1, D_MODEL) for g in (ffn1_norm, mix_norm, ffn2_norm, final_norm))

    h1, g1, u1, a1 = _ffn_up(x0, gam1, wg1, wu1, "ffn1_up", hosted=[gather_down1])
    wd1, win_t = gather_down1.result
    x1 = _ffn_down(x0, a1, wd1, "ffn1_down", hosted=[gather_out])
    proj_t, wout = gather_out.result
    h2, qkv_a, qkv_b, gates = _proj_fwd(x1, gam2, win_t, hosted=[gather_ffn2_gate])
    (wg2,) = gather_ffn2_gate.result
    qkv_a3 = qkv_a.reshape(bsz, s_len, QKV_A)
    qkv_b3 = qkv_b.reshape(bsz, s_len, QKV_B)

    bias_b = jnp.asarray(_bias_b())
    sink_rows = jnp.broadcast_to(sinks.reshape(B_Q_HEADS, 1, 1), (B_Q_HEADS, 8, LANES))

    oa = _attn_a_fwd(qkv_a3, bias_a, hosted=[gather_ffn2_rest]).reshape(t, A_WIDTH)
    wu2, wd2 = gather_ffn2_rest.result
    ob = _attn_b_fwd(qkv_b3, bias_b, sink_rows).reshape(t, B_Q_WIDTH)
    x2, ya, yb, mg = _mix_out_fwd(x1, oa, ob, gates, proj_t, wout)
    h3, g2, u2, a2, x3 = _ffn_fwd(x2, gam3, wg2, wu2, wd2, "ffn2_fwd")

    dx2, dg2, du2, db2, dgam3, dgam4, loss_part = _ffn_bwd_head(x3, gam4, tgt, x2, gam3, g2, u2, wg2, wu2, wd2,
                                                                "ffn2_bwd")
    gw_ffn2 = [_mm_tn([dg2], h3, "grad_ffn2_gate"), _mm_tn([du2], h3, "grad_ffn2_up"),
               _mm_tn([a2], db2, "grad_ffn2_down")]
    pairx_ffn2 = _PairExchange(gw_ffn2)
    doa, dob, dgates, gw_out, gw_proj = _mix_out_bwd(dx2, gates, ya, yb, mg, oa, ob, proj_t, wout,
                                                     hosted=[pairx_ffn2])
    psum_ffn2 = _pair_sum(core, gw_ffn2, pairx_ffn2.result, "pair_sum_ffn2")

    chipx_ffn2 = _ChipExchange(psum_ffn2)
    dqa, dka, dva, dbias_a = _attn_a_bwd(qkv_a3, bias_a, doa.reshape(bsz, s_len, A_WIDTH), hosted=[chipx_ffn2])
    pairx_out = _PairExchange([gw_proj, gw_out])
    dqb, dkvb, dsink = _attn_b_bwd(qkv_b3, bias_b, sink_rows, dob.reshape(bsz, s_len, B_Q_WIDTH), hosted=[pairx_out])
    drel_lanes = _relbias_grad(dbias_a)
    dproj = [dqa.reshape(t, A_WIDTH), dka.reshape(t, A_WIDTH), dva.reshape(t, A_WIDTH), dqb.reshape(t, B_Q_WIDTH),
             dkvb.reshape(t, 2 * B_KV_WIDTH), dgates]

    gw_in = _mm_tn(dproj, h2, "grad_w_in")
    pairx_in = _PairExchange([gw_in])
    psum_out = _pair_sum(core, [gw_proj, gw_out], pairx_out.result, "pair_sum_mix")
    chipx_out = _ChipExchange(psum_out)
    dx1, db1, dgam2 = _proj_bwd(dx2, x1, gam2, dproj, win_t, hosted=[pairx_in, chipx_out])
    psum_in = _pair_sum(core, [gw_in], pairx_in.result, "pair_sum_w_in")
    gw_d1 = _mm_tn([a1], db1, "grad_ffn1_down")

    chipx_in = _ChipExchange(psum_in)
    pairx_d1 = _PairExchange([gw_d1])
    dg1, du1 = _ffn_bwd_act(dx1, g1, u1, wd1, "ffn1_bwd_act", hosted=[chipx_in, pairx_d1])
    psum_d1 = _pair_sum(core, [gw_d1], pairx_d1.result, "pair_sum_ffn1_down")
    chipx_d1 = _ChipExchange(psum_d1)
    gw_g1 = _mm_tn([dg1], h1, "grad_ffn1_gate", hosted=[chipx_d1])
    from_sibling_g1 = _exchange_alone(_PairExchange([gw_g1]), "pair_exchange_ffn1_gate")
    psum_g1 = _pair_sum(core, [gw_g1], from_sibling_g1, "pair_sum_ffn1_gate")
    chipx_g1 = _ChipExchange(psum_g1)
    gw_u1 = _mm_tn([du1], h1, "grad_ffn1_up", hosted=[chipx_g1])
    from_sibling_u1 = _exchange_alone(_PairExchange([gw_u1]), "pair_exchange_ffn1_up")
    psum_u1 = _pair_sum(core, [gw_u1], from_sibling_u1, "pair_sum_ffn1_up")
    chipx_u1 = _ChipExchange(psum_u1)
    dx0, dgam1 = _ffn_bwd_in(dx1, x0, gam1, dg1, du1, wg1, wu1, "ffn1_bwd_in", hosted=[chipx_u1])

    (g_proj,) = _final_sum(chip, psum_out[0:1], chipx_out.result[0:1], "grad_sum_proj")
    grads = {"w_proj_a": g_proj[:, 0:A_WIDTH].T, "w_proj_b": g_proj[:, A_WIDTH:].T}

    def row_of(v):
        return jnp.pad(v.reshape(1, -1), ((0, 0), (0, D_MODEL - v.size)))

    def table_rows(v):
        return jnp.pad(v, ((0, 0), (0, D_MODEL - REL_TABLE)))

    drel_local = jnp.flip(drel_lanes[:, 0, 0:REL_TABLE], axis=1)
    small_part = jnp.concatenate(
        [jnp.sum(dgam1, axis=0, keepdims=True), jnp.sum(dgam2, axis=0, keepdims=True),
         jnp.sum(dgam3, axis=0, keepdims=True), jnp.sum(dgam4, axis=0, keepdims=True),
         row_of(jnp.sum(loss_part)), row_of(dsink[:, 0, 0]), jnp.zeros((2, D_MODEL), F32),
         table_rows(drel_local)], axis=0)
    small = _all_reduce_small(small_part)
    loss = small[4, 0]

    def pack(n1, n2, n3, n4, sk, tb):
        return jnp.concatenate([n1.reshape(1, -1), n2.reshape(1, -1), n3.reshape(1, -1), n4.reshape(1, -1),
                                jnp.zeros((1, D_MODEL), F32), row_of(sk), jnp.zeros((2, D_MODEL), F32), table_rows(tb)],
                               axis=0)

    live = np.zeros((SMALL_ROWS, D_MODEL), np.float32)
    live[0:4] = 1.0
    live[5, 0:B_Q_HEADS] = 1.0
    live[8:16, 0:REL_TABLE] = 1.0
    small_g = small * jnp.asarray(live)
    sw = pack(ffn1_norm, mix_norm, ffn2_norm, final_norm, sinks, rel_bias)
    sm = pack(m_ffn1_norm, m_mix_norm, m_ffn2_norm, m_final_norm, m_sinks, m_rel_bias)
    sv = pack(v_ffn1_norm, v_mix_norm, v_ffn2_norm, v_final_norm, v_sinks, v_rel_bias)
    (sd,), (snm,), (snv,) = _adamw([sw], [small_g], [sm], [sv], "adamw_small")

    def unpack(p):
        return {"ffn1_norm": p[0], "mix_norm": p[1], "ffn2_norm": p[2], "final_norm": p[3],
                "sinks": p[5, 0:B_Q_HEADS], "rel_bias": p[8:16, 0:REL_TABLE]}

    grads.update(unpack(small_g))
    delta, new_m, new_v = unpack(sd), unpack(snm), unpack(snv)

    wmv = {
        "ffn1_w_gate": (ffn1_w_gate, m_ffn1_w_gate, v_ffn1_w_gate), "ffn1_w_up": (ffn1_w_up, m_ffn1_w_up, v_ffn1_w_up),
        "ffn1_w_down": (ffn1_w_down, m_ffn1_w_down, v_ffn1_w_down), "w_in": (w_in, m_w_in, v_w_in),
        "w_proj_a": (w_proj_a, m_w_proj_a, v_w_proj_a), "w_proj_b": (w_proj_b, m_w_proj_b, v_w_proj_b),
        "w_out": (w_out, m_w_out, v_w_out),
        "ffn2_w_gate": (ffn2_w_gate, m_ffn2_w_gate, v_ffn2_w_gate), "ffn2_w_up": (ffn2_w_up, m_ffn2_w_up, v_ffn2_w_up),
        "ffn2_w_down": (ffn2_w_down, m_ffn2_w_down, v_ffn2_w_down),
    }
    row_form_names = ("ffn1_w_gate", "ffn1_w_up", "w_in", "ffn2_w_gate", "ffn2_w_up")

    def form(n, a):
        return a.T if n in row_form_names else a

    def reduced_group(gname, names, psums, recvd, steps):
        gs_, ds_, ms_, vs_ = _adamw_reduced(
            chip, [form(n, wmv[n][0]) for n in names], psums, recvd, [form(n, wmv[n][1]) for n in names],
            [form(n, wmv[n][2]) for n in names], steps, gname)
        for n, g_, d_, m_, v_ in zip(names, gs_, ds_, ms_, vs_):
            grads[n], delta[n], new_m[n], new_v[n] = form(n, g_), form(n, d_), form(n, m_), form(n, v_)

    reduced_group("adamw_ffn", ["ffn1_w_gate", "ffn1_w_up", "ffn1_w_down", "ffn2_w_gate", "ffn2_w_up", "ffn2_w_down"],
                  psum_g1 + psum_u1 + psum_d1 + psum_ffn2,
                  chipx_g1.result + chipx_u1.result + chipx_d1.result + chipx_ffn2.result, 11)
    reduced_group("adamw_w_in", ["w_in"], psum_in, chipx_in.result, 2)
    reduced_group("adamw_w_out", ["w_out"], psum_out[1:2], chipx_out.result[1:2], 2)
    names = ["w_proj_a", "w_proj_b"]
    ds_, ms_, vs_ = _adamw([wmv[n][0] for n in names], [grads[n] for n in names], [wmv[n][1] for n in names],
                           [wmv[n][2] for n in names], "adamw_proj")
    for n, d_, m_, v_ in zip(names, ds_, ms_, vs_):
        delta[n], new_m[n], new_v[n] = d_, m_, v_

    order = ["ffn1_norm", "ffn1_w_gate", "ffn1_w_up", "ffn1_w_down", "mix_norm", "w_in", "rel_bias", "sinks",
             "w_proj_a", "w_proj_b", "w_out", "ffn2_norm", "ffn2_w_gate", "ffn2_w_up", "ffn2_w_down", "final_norm"]
    grad_x = dx0.reshape(bsz, s_len, D_MODEL)
    return (loss, grad_x, *[grads[n] for n in order], *[delta[n] for n in order], *[new_m[n] for n in order],
            *[new_v[n] for n in order])
```

```python
import numpy as np
import jax
import jax.numpy as jnp
from jax import lax
from jax.experimental import pallas as pl
from jax.experimental.pallas import tpu as pltpu

F32 = jnp.float32
BF16 = jnp.bfloat16

D_MODEL = 1024
D_FF = 2816
CHUNK = 64
D_HEAD = 64
A_HEADS = 8
A_PREV = 8
MAX_REL = 128
B_Q_HEADS = 8
B_KV_HEADS = 2
B_GROUP = B_Q_HEADS // B_KV_HEADS
B_PREV = 2
REL_TABLE = (CHUNK - 1) + MAX_REL + 1
A_WIDTH = A_HEADS * D_HEAD
B_Q_WIDTH = B_Q_HEADS * D_HEAD
B_KV_WIDTH = B_KV_HEADS * D_HEAD
QKV_A = 3 * A_WIDTH
QKV_B = B_Q_WIDTH + 2 * B_KV_WIDTH
IN_WIDTH = QKV_A + QKV_B + 2 * D_MODEL
EPS = 1e-6
NEG_INF = -1e30
SCALE = 1.0 / 8.0

ADAM_LR = 0.001
ADAM_B1 = 0.9
ADAM_B2 = 0.999
ADAM_EPS = 1e-08
ADAM_WD = 0.01
ADAM_STEP = 10

N_DEV = 8
N_CHIP = 4
MESH = pl.DeviceIdType.MESH

LANES = 128
TQ = 256
TM = 256
FC = 256
VMEM_LIMIT = 56 << 20


def _cparams(sem, vmem=VMEM_LIMIT):
    return pltpu.CompilerParams(dimension_semantics=sem, vmem_limit_bytes=vmem)


def _dot_nt(a, b):
    return lax.dot_general(a, b, (((1,), (1,)), ((), ())), preferred_element_type=F32)


def _dot_nn(a, b):
    return lax.dot_general(a, b, (((1,), (0,)), ((), ())), preferred_element_type=F32)


def _dot_tn(a, b):
    return lax.dot_general(a, b, (((0,), (0,)), ((), ())), preferred_element_type=F32)


def _resident(shape):
    nd = len(shape)
    return pl.BlockSpec(shape, lambda *_: (0,) * nd, pipeline_mode=pl.Buffered(1))


def _rows(tm, width):
    return pl.BlockSpec((tm, width), lambda i: (i, 0))


def _colsum8(v):
    tm, n = v.shape
    return jnp.sum(v.reshape(tm // 8, 8, n), axis=0)


def _rms(x):
    r = lax.rsqrt(jnp.mean(x * x, axis=-1, keepdims=True) + EPS)
    return x * r, r


def _rms_bwd(dh, xh, r, gamma):
    dxh = dh * gamma
    dx = r * (dxh - xh * jnp.mean(dxh * xh, axis=-1, keepdims=True))
    return dx, _colsum8(dh * xh)


def _hbm():
    return pl.BlockSpec(memory_space=pltpu.HBM)


def _call(body, *, name, grid, in_specs, out_specs, out_shape, args, sem, scratch_shapes=(), hosted=()):
    in_specs, out_specs, out_shape = list(in_specs), list(out_specs), list(out_shape)
    scratch_shapes = list(scratch_shapes)
    if not hosted:
        return pl.pallas_call(body, name=name, grid=grid, in_specs=in_specs, out_specs=out_specs, out_shape=out_shape,
                              scratch_shapes=scratch_shapes, compiler_params=_cparams(sem))(*args)
    n_in, n_out, n_scr = len(in_specs), len(out_specs), len(scratch_shapes)
    x_in = [a for x in hosted for a in x.inputs]
    x_out = [s for x in hosted for s in x.out_shape]
    x_scr = [s for x in hosted for s in x.scratch]
    steps = int(np.prod(grid))
    forward_step = max(steps - 3, 0)
    relay_step = min((5 * steps) // 8, forward_step)

    def wrapped(*refs):
        pos = [0]

        def take(k):
            pos[0] += k
            return refs[pos[0] - k:pos[0]]

        ins, xin, outs, xout, scr, xscr = (take(k) for k in (n_in, len(x_in), n_out, len(x_out), n_scr, len(x_scr)))
        step = 0
        for axis, extent in enumerate(grid):
            step = step * extent + pl.program_id(axis)
        own, oi, oo, osc = [], 0, 0, 0
        for x in hosted:
            own.append((xin[oi:oi + len(x.inputs)], xout[oo:oo + len(x.out_shape)], xscr[osc:osc + len(x.scratch)]))
            oi, oo, osc = oi + len(x.inputs), oo + len(x.out_shape), osc + len(x.scratch)

        def phase(method):
            for x, (i_, o_, s_) in zip(hosted, own):
                getattr(x, method)(i_, o_, s_)

        pl.when(step == 0)(lambda: phase("start"))
        body(*ins, *outs, *scr)
        pl.when(step == relay_step)(lambda: phase("relay"))
        pl.when(step == forward_step)(lambda: phase("forward"))
        pl.when(step == steps - 1)(lambda: phase("finish"))

    res = pl.pallas_call(
        wrapped, name=name, grid=grid, in_specs=in_specs + [_hbm()] * len(x_in),
        out_specs=out_specs + [_hbm()] * len(x_out), out_shape=out_shape + x_out,
        scratch_shapes=scratch_shapes + x_scr, compiler_params=_cparams(("arbitrary",) * len(grid)))(*args, *x_in)
    rest = list(res[n_out:])
    for x in hosted:
        x.result, rest = rest[:len(x.out_shape)], rest[len(x.out_shape):]
    return list(res[:n_out])


def _to_bf16(arrays, name):
    n = len(arrays)

    def body(*refs):
        for k in range(n):
            refs[n + k][...] = refs[k][...].astype(BF16)

    specs = [pl.BlockSpec(a.shape, lambda i: (0, 0)) for a in arrays]
    return pl.pallas_call(
        body, name=name, grid=(1,), in_specs=specs, out_specs=specs,
        out_shape=[jax.ShapeDtypeStruct(a.shape, BF16) for a in arrays],
        compiler_params=_cparams(("arbitrary",)))(*arrays)


def _ffn_fwd(x, gamma, wg_t, wu_t, wd, name, hosted=()):
    t = x.shape[0]
    f = wg_t.shape[0]

    def body(x_ref, gam_ref, wg_ref, wu_ref, wd_ref, h_ref, g_ref, u_ref, a_ref, y_ref):
        xv = x_ref[...]
        xh, _ = _rms(xv)
        h = (xh * gam_ref[...]).astype(BF16)
        h_ref[...] = h
        for j in range(f // FC):
            sl = slice(j * FC, (j + 1) * FC)
            g = _dot_nt(h, wg_ref[sl, :])
            u = _dot_nt(h, wu_ref[sl, :])
            g_ref[:, sl] = g.astype(BF16)
            u_ref[:, sl] = u.astype(BF16)
            a_ref[:, sl] = (g * jax.nn.sigmoid(g) * u).astype(BF16)
        y_ref[...] = xv + 0.5 * _dot_nn(a_ref[...], wd_ref[...])

    return _call(
        body,
        name=name,
        grid=(t // TM,),
        in_specs=[_rows(TM, D_MODEL), _resident((1, D_MODEL)), _resident((f, D_MODEL)), _resident((f, D_MODEL)),
                  _resident((f, D_MODEL))],
        out_specs=[_rows(TM, D_MODEL), _rows(TM, f), _rows(TM, f), _rows(TM, f),
                   _rows(TM, D_MODEL)],
        out_shape=[jax.ShapeDtypeStruct((t, D_MODEL), BF16), jax.ShapeDtypeStruct((t, f), BF16),
                   jax.ShapeDtypeStruct((t, f), BF16), jax.ShapeDtypeStruct((t, f), BF16),
                   jax.ShapeDtypeStruct((t, D_MODEL), F32)],
        args=(x, gamma, wg_t, wu_t, wd), sem=("parallel",), hosted=hosted)


def _ffn_up(x, gamma, wg_t, wu_t, name, hosted=()):
    t = x.shape[0]
    f = wg_t.shape[0]

    def body(x_ref, gam_ref, wg_ref, wu_ref, h_ref, g_ref, u_ref, a_ref):
        xh, _ = _rms(x_ref[...])
        h = (xh * gam_ref[...]).astype(BF16)
        h_ref[...] = h
        for j in range(f // FC):
            sl = slice(j * FC, (j + 1) * FC)
            g = _dot_nt(h, wg_ref[sl, :])
            u = _dot_nt(h, wu_ref[sl, :])
            g_ref[:, sl] = g.astype(BF16)
            u_ref[:, sl] = u.astype(BF16)
            a_ref[:, sl] = (g * jax.nn.sigmoid(g) * u).astype(BF16)

    return _call(
        body,
        name=name,
        grid=(t // TM,),
        in_specs=[_rows(TM, D_MODEL), _resident((1, D_MODEL)), _resident((f, D_MODEL)), _resident((f, D_MODEL))],
        out_specs=[_rows(TM, D_MODEL), _rows(TM, f), _rows(TM, f), _rows(TM, f)],
        out_shape=[jax.ShapeDtypeStruct((t, D_MODEL), BF16), jax.ShapeDtypeStruct((t, f), BF16),
                   jax.ShapeDtypeStruct((t, f), BF16), jax.ShapeDtypeStruct((t, f), BF16)],
        args=(x, gamma, wg_t, wu_t), sem=("parallel",), hosted=hosted)


def _ffn_down(x, a_act, wd, name, hosted=()):
    t = x.shape[0]
    f = wd.shape[0]

    def body(x_ref, a_ref, wd_ref, y_ref):
        y_ref[...] = x_ref[...] + 0.5 * _dot_nn(a_ref[...], wd_ref[...])

    return _call(
        body,
        name=name,
        grid=(t // TM,),
        in_specs=[_rows(TM, D_MODEL), _rows(TM, f), _resident((f, D_MODEL))],
        out_specs=[_rows(TM, D_MODEL)],
        out_shape=[jax.ShapeDtypeStruct((t, D_MODEL), F32)],
        args=(x, a_act, wd), sem=("parallel",), hosted=hosted)[0]


def _ffn_bwd_head(y, gamma_f, target, x, gamma, g_act, u_act, wg_t, wu_t, wd, name):
    t = x.shape[0]
    f = wg_t.shape[0]

    def body(y_ref, gamf_ref, t_ref, x_ref, gam_ref, g_ref, u_ref, wg_ref, wu_ref, wd_ref, dx_ref, dg_ref, du_ref,
             db_ref, dgam_ref, dgamf_ref, loss_ref):
        yh, ry = _rms(y_ref[...])
        gam_f = gamf_ref[...]
        e = yh * gam_f - t_ref[...]
        dv, dgam_f = _rms_bwd(e * (1.0 / D_MODEL), yh, ry, gam_f)
        db = (0.5 * dv).astype(BF16)
        db_ref[...] = db
        for j in range(f // FC):
            sl = slice(j * FC, (j + 1) * FC)
            da = _dot_nt(db, wd_ref[sl, :])
            g = g_ref[:, sl].astype(F32)
            u = u_ref[:, sl].astype(F32)
            s = jax.nn.sigmoid(g)
            dg_ref[:, sl] = (da * u * (s * (1.0 + g * (1.0 - s)))).astype(BF16)
            du_ref[:, sl] = (da * (g * s)).astype(BF16)
        dh = _dot_nn(dg_ref[...], wg_ref[...]) + _dot_nn(du_ref[...], wu_ref[...])
        xh, r = _rms(x_ref[...])
        dxn, dgam = _rms_bwd(dh, xh, r, gam_ref[...])
        dx_ref[...] = dv + dxn

        @pl.when(pl.program_id(0) == 0)
        def _():
            dgam_ref[...] = jnp.zeros_like(dgam_ref)
            dgamf_ref[...] = jnp.zeros_like(dgamf_ref)
            loss_ref[...] = jnp.zeros_like(loss_ref)

        dgam_ref[...] += dgam
        dgamf_ref[...] += dgam_f
        loss_ref[...] += _colsum8(e * e) * (0.5 / D_MODEL)

    acc = pl.BlockSpec((8, D_MODEL), lambda i: (0, 0))
    return _call(
        body,
        name=name,
        grid=(t // TM,),
        in_specs=[_rows(TM, D_MODEL), _resident((1, D_MODEL)), _rows(TM, D_MODEL), _rows(TM, D_MODEL),
                  _resident((1, D_MODEL)), _rows(TM, f), _rows(TM, f),
                  _resident((f, D_MODEL)), _resident((f, D_MODEL)), _resident((f, D_MODEL))],
        out_specs=[_rows(TM, D_MODEL), _rows(TM, f), _rows(TM, f), _rows(TM, D_MODEL), acc, acc, acc],
        out_shape=[jax.ShapeDtypeStruct((t, D_MODEL), F32), jax.ShapeDtypeStruct((t, f), BF16),
                   jax.ShapeDtypeStruct((t, f), BF16), jax.ShapeDtypeStruct((t, D_MODEL), BF16),
                   jax.ShapeDtypeStruct((8, D_MODEL), F32), jax.ShapeDtypeStruct((8, D_MODEL), F32),
                   jax.ShapeDtypeStruct((8, D_MODEL), F32)],
        args=(y, gamma_f, target, x, gamma, g_act, u_act, wg_t, wu_t, wd), sem=("arbitrary",))


def _ffn_bwd_act(d, g_act, u_act, wd, name, hosted=()):
    t = d.shape[0]
    f = wd.shape[0]

    def body(d_ref, g_ref, u_ref, wd_ref, dg_ref, du_ref):
        db = (0.5 * d_ref[...]).astype(BF16)
        for j in range(f // FC):
            sl = slice(j * FC, (j + 1) * FC)
            da = _dot_nt(db, wd_ref[sl, :])
            g = g_ref[:, sl].astype(F32)
            u = u_ref[:, sl].astype(F32)
            s = jax.nn.sigmoid(g)
            dg_ref[:, sl] = (da * u * (s * (1.0 + g * (1.0 - s)))).astype(BF16)
            du_ref[:, sl] = (da * (g * s)).astype(BF16)

    return _call(
        body,
        name=name,
        grid=(t // TM,),
        in_specs=[_rows(TM, D_MODEL), _rows(TM, f), _rows(TM, f), _resident((f, D_MODEL))],
        out_specs=[_rows(TM, f), _rows(TM, f)],
        out_shape=[jax.ShapeDtypeStruct((t, f), BF16), jax.ShapeDtypeStruct((t, f), BF16)],
        args=(d, g_act, u_act, wd), sem=("parallel",), hosted=hosted)


def _ffn_bwd_in(d, x, gamma, dg, du, wg_t, wu_t, name, hosted=()):
    t = x.shape[0]
    f = wg_t.shape[0]

    def body(d_ref, x_ref, gam_ref, dg_ref, du_ref, wg_ref, wu_ref, dx_ref, dgam_ref):
        dh = _dot_nn(dg_ref[...], wg_ref[...]) + _dot_nn(du_ref[...], wu_ref[...])
        xh, r = _rms(x_ref[...])
        dxn, dgam = _rms_bwd(dh, xh, r, gam_ref[...])
        dx_ref[...] = d_ref[...] + dxn

        @pl.when(pl.program_id(0) == 0)
        def _():
            dgam_ref[...] = jnp.zeros_like(dgam_ref)

        dgam_ref[...] += dgam

    return _call(
        body,
        name=name,
        grid=(t // TM,),
        in_specs=[_rows(TM, D_MODEL), _rows(TM, D_MODEL), _resident((1, D_MODEL)), _rows(TM, f), _rows(TM, f),
                  _resident((f, D_MODEL)), _resident((f, D_MODEL))],
        out_specs=[_rows(TM, D_MODEL), pl.BlockSpec((8, D_MODEL), lambda i: (0, 0))],
        out_shape=[jax.ShapeDtypeStruct((t, D_MODEL), F32), jax.ShapeDtypeStruct((8, D_MODEL), F32)],
        args=(d, x, gamma, dg, du, wg_t, wu_t), sem=("arbitrary",), hosted=hosted)


def _mm_tn(pieces, b, name, tile=256, hosted=()):
    t, n = b.shape
    npc = len(pieces)
    counts = [p.shape[1] // tile for p in pieces]
    los = [sum(counts[:k]) for k in range(npc)]
    total = sum(counts)

    def body(*refs):
        a_refs, b_ref, o_ref = refs[:npc], refs[npc], refs[npc + 1]
        i = pl.program_id(0)
        for k in range(npc):
            @pl.when(jnp.logical_and(i >= los[k], i < los[k] + counts[k]))
            def _(k=k):
                o_ref[...] = _dot_tn(a_refs[k][...], b_ref[...]).astype(BF16)

    def a_spec(k):
        return pl.BlockSpec((t, tile), lambda i: (0, jnp.clip(i - los[k], 0, counts[k] - 1)))

    return _call(
        body,
        name=name,
        grid=(total,),
        in_specs=[a_spec(k) for k in range(npc)] + [_resident((t, n))],
        out_specs=[pl.BlockSpec((tile, n), lambda i: (i, 0))],
        out_shape=[jax.ShapeDtypeStruct((total * tile, n), BF16)],
        args=(*pieces, b), sem=("parallel",), hosted=hosted)[0]


def _proj_fwd(x, gamma, win_t, hosted=()):
    t = x.shape[0]

    def body(x_ref, gam_ref, w_ref, h_ref, qa_ref, qb_ref, gt_ref):
        xh, _ = _rms(x_ref[...])
        h = (xh * gam_ref[...]).astype(BF16)
        h_ref[...] = h
        for j in range(QKV_A // FC):
            qa_ref[:, j * FC:(j + 1) * FC] = _dot_nt(h, w_ref[j * FC:(j + 1) * FC, :]).astype(BF16)
        for j in range(QKV_B // FC):
            lo = QKV_A + j * FC
            qb_ref[:, j * FC:(j + 1) * FC] = _dot_nt(h, w_ref[lo:lo + FC, :]).astype(BF16)
        for j in range(2 * D_MODEL // FC):
            lo = QKV_A + QKV_B + j * FC
            gt_ref[:, j * FC:(j + 1) * FC] = _dot_nt(h, w_ref[lo:lo + FC, :])

    return _call(
        body,
        name="proj_fwd",
        grid=(t // TM,),
        in_specs=[_rows(TM, D_MODEL), _resident((1, D_MODEL)), _resident((IN_WIDTH, D_MODEL))],
        out_specs=[_rows(TM, D_MODEL), _rows(TM, QKV_A), _rows(TM, QKV_B), _rows(TM, 2 * D_MODEL)],
        out_shape=[jax.ShapeDtypeStruct((t, D_MODEL), BF16), jax.ShapeDtypeStruct((t, QKV_A), BF16),
                   jax.ShapeDtypeStruct((t, QKV_B), BF16), jax.ShapeDtypeStruct((t, 2 * D_MODEL), F32)],
        args=(x, gamma, win_t), sem=("parallel",), hosted=hosted)


def _proj_bwd(d, x, gamma, pieces, win_t, hosted=()):
    t = x.shape[0]
    npc = len(pieces)
    widths = [p.shape[1] for p in pieces]
    los = [sum(widths[:k]) for k in range(npc)]

    def body(*refs):
        d_ref, x_ref, gam_ref = refs[:3]
        p_refs = refs[3:3 + npc]
        w_ref, dx_ref, db_ref, dgam_ref = refs[3 + npc:]
        dh = _dot_nn(p_refs[0][...], w_ref[0:widths[0], :])
        for k in range(1, npc):
            dh += _dot_nn(p_refs[k][...], w_ref[los[k]:los[k] + widths[k], :])
        xh, r = _rms(x_ref[...])
        dxn, dgam = _rms_bwd(dh, xh, r, gam_ref[...])
        dx = d_ref[...] + dxn
        dx_ref[...] = dx
        db_ref[...] = (0.5 * dx).astype(BF16)

        @pl.when(pl.program_id(0) == 0)
        def _():
            dgam_ref[...] = jnp.zeros_like(dgam_ref)

        dgam_ref[...] += dgam

    return _call(
        body,
        name="proj_bwd",
        grid=(t // TM,),
        in_specs=[_rows(TM, D_MODEL), _rows(TM, D_MODEL), _resident((1, D_MODEL))] + [_rows(TM, w) for w in widths]
        + [_resident((IN_WIDTH, D_MODEL))],
        out_specs=[_rows(TM, D_MODEL), _rows(TM, D_MODEL), pl.BlockSpec((8, D_MODEL), lambda i: (0, 0))],
        out_shape=[jax.ShapeDtypeStruct((t, D_MODEL), F32), jax.ShapeDtypeStruct((t, D_MODEL), BF16),
                   jax.ShapeDtypeStruct((8, D_MODEL), F32)],
        args=(d, x, gamma, *pieces, win_t), sem=("arbitrary",), hosted=hosted)


def _lane_half(shape):
    return lax.broadcasted_iota(jnp.int32, shape, len(shape) - 1) // D_HEAD


def _band_weights(q, kk, bias, sink, qs, pad):
    s = _dot_nt(q, kk) + bias
    if qs is not None:
        col = lax.broadcasted_iota(jnp.int32, s.shape, 1)
        s = jnp.where(col + qs >= pad, s, NEG_INF)
    m = jnp.max(s, axis=-1, keepdims=True)
    if sink is not None:
        m = jnp.maximum(m, sink)
    return jnp.exp(s - m), m


def _weighted_values(p, vv_ones, sink, m):
    r = _dot_nn(p.astype(BF16), vv_ones)
    den = r[:, LANES:2 * LANES]
    if sink is not None:
        den = den + jnp.exp(sink - m)
    return r[:, 0:LANES] / den


def _band_softmax(q, kk, bias, sink, qs, pad):
    p, m = _band_weights(q, kk, bias, sink, qs, pad)
    den = jnp.sum(p, axis=-1, keepdims=True)
    if sink is not None:
        den = den + jnp.exp(sink - m)
    return p, m, 1.0 / den


def _fill_padded(dst, src, pad):
    dst[0:pad, :] = jnp.zeros((pad,) + dst.shape[1:], dst.dtype)
    dst[pad:, :] = src


FWD_PAIRS = 4
BWD_PAIRS = 4


def _attn_a_fwd(qkv, bias, hosted=()):
    bsz, s_len, _ = qkv.shape
    pad = A_PREV * CHUNK
    band = TQ + pad
    pp = FWD_PAIRS
    w = pp * LANES
    nb = A_WIDTH // w

    def body(q_ref, k_ref, v_ref, b_ref, o_ref, kp, vp):
        i = pl.program_id(2)

        @pl.when(i == 0)
        def _():
            _fill_padded(kp, k_ref[...], pad)
            _fill_padded(vp, v_ref[...], pad)

        qs = pl.multiple_of(i * TQ, TQ)
        half = _lane_half((1, LANES))

        ones = jnp.ones((band, LANES), BF16)

        def block(masked):
            for pr in range(pp):
                sl = slice(pr * LANES, (pr + 1) * LANES)
                kk = kp[pl.ds(qs, band), sl]
                vv = jnp.concatenate([vp[pl.ds(qs, band), sl], ones], axis=1)
                q = q_ref[:, sl] * SCALE
                outs = []
                for j in range(2):
                    qm = jnp.where(half == j, q, jnp.zeros_like(q))
                    p, m = _band_weights(qm, kk, b_ref[2 * pr + j], None, qs if masked else None, pad)
                    outs.append(_weighted_values(p, vv, None, m))
                o_ref[:, sl] = jnp.where(half == 0, outs[0], outs[1]).astype(BF16)

        pl.when(i < pad // TQ)(lambda: block(True))
        pl.when(i >= pad // TQ)(lambda: block(False))

    return _call(
        body,
        name="attn_a_fwd",
        grid=(bsz, nb, s_len // TQ),
        in_specs=[pl.BlockSpec((None, TQ, w), lambda b, g, i: (b, i, g)),
                  pl.BlockSpec((None, s_len, w), lambda b, g, i: (b, 0, nb + g)),
                  pl.BlockSpec((None, s_len, w), lambda b, g, i: (b, 0, 2 * nb + g)),
                  pl.BlockSpec((2 * pp, TQ, band), lambda b, g, i: (g, 0, 0))],
        out_specs=[pl.BlockSpec((None, TQ, w), lambda b, g, i: (b, i, g))],
        out_shape=[jax.ShapeDtypeStruct((bsz, s_len, A_WIDTH), BF16)],
        scratch_shapes=[pltpu.VMEM((pad + s_len, w), BF16), pltpu.VMEM((pad + s_len, w), BF16)],
        args=(qkv, qkv, qkv, bias), sem=("arbitrary", "arbitrary", "arbitrary"), hosted=hosted)[0]


def _attn_a_bwd(qkv, bias, do, hosted=()):
    bsz, s_len, _ = qkv.shape
    pad = A_PREV * CHUNK
    band = TQ + pad
    n_i = s_len // TQ
    pp = BWD_PAIRS
    w = pp * LANES
    nb = A_WIDTH // w

    def body(q_ref, k_ref, v_ref, b_ref, do_ref, dq_ref, dk_ref, dv_ref, dbias_ref, kp, vp, dk_acc, dv_acc):
        b = pl.program_id(1)
        i = pl.program_id(2)

        @pl.when(i == 0)
        def _():
            _fill_padded(kp, k_ref[...], pad)
            _fill_padded(vp, v_ref[...], pad)
            dk_acc[...] = jnp.zeros_like(dk_acc)
            dv_acc[...] = jnp.zeros_like(dv_acc)

        @pl.when(jnp.logical_and(b == 0, i == 0))
        def _():
            dbias_ref[...] = jnp.zeros_like(dbias_ref)

        qs = pl.multiple_of(i * TQ, TQ)
        half = _lane_half((1, LANES))

        def block(masked):
            for pr in range(pp):
                sl = slice(pr * LANES, (pr + 1) * LANES)
                kk = kp[pl.ds(qs, band), sl]
                vv = vp[pl.ds(qs, band), sl]
                q = q_ref[:, sl] * SCALE
                dd = do_ref[:, sl]
                dqs, dks, dvs = [], [], []
                for j in range(2):
                    qm = jnp.where(half == j, q, jnp.zeros_like(q))
                    dm = jnp.where(half == j, dd, jnp.zeros_like(dd))
                    p, _, inv = _band_softmax(qm, kk, b_ref[2 * pr + j], None, qs if masked else None, pad)
                    pn = p * inv
                    dp = _dot_nt(dm, vv)
                    delta = jnp.sum(pn * dp, axis=-1, keepdims=True)
                    ds = pn * (dp - delta)
                    dbias_ref[2 * pr + j] += ds[:, band - REL_COLS:]
                    dsb = ds.astype(BF16)
                    dqs.append(_dot_nn(dsb, kk))
                    dks.append(_dot_tn(dsb, q))
                    dvs.append(_dot_tn(pn.astype(BF16), dd))
                dq_ref[:, sl] = (jnp.where(half == 0, dqs[0], dqs[1]) * SCALE).astype(BF16)
                dk_acc[pl.ds(qs, band), sl] += jnp.where(half == 0, dks[0], dks[1])
                dv_acc[pl.ds(qs, band), sl] += jnp.where(half == 0, dvs[0], dvs[1])

        pl.when(i < pad // TQ)(lambda: block(True))
        pl.when(i >= pad // TQ)(lambda: block(False))

        @pl.when(i == n_i - 1)
        def _():
            dk_ref[...] = dk_acc[pad:, :].astype(BF16)
            dv_ref[...] = dv_acc[pad:, :].astype(BF16)

    qspec = pl.BlockSpec((None, TQ, w), lambda g, b, i: (b, i, g))
    kvout = pl.BlockSpec((None, s_len, w), lambda g, b, i: (b, 0, g))
    wide = jax.ShapeDtypeStruct((bsz, s_len, A_WIDTH), BF16)
    return _call(
        body,
        name="attn_a_bwd",
        grid=(nb, bsz, n_i),
        in_specs=[qspec,
                  pl.BlockSpec((None, s_len, w), lambda g, b, i: (b, 0, nb + g)),
                  pl.BlockSpec((None, s_len, w), lambda g, b, i: (b, 0, 2 * nb + g)),
                  pl.BlockSpec((2 * pp, TQ, band), lambda g, b, i: (g, 0, 0)),
                  qspec],
        out_specs=[qspec, kvout, kvout, pl.BlockSpec((2 * pp, TQ, REL_COLS), lambda g, b, i: (g, 0, 0))],
        out_shape=[wide, wide, wide, jax.ShapeDtypeStruct((A_HEADS, TQ, REL_COLS), F32)],
        scratch_shapes=[pltpu.VMEM((pad + s_len, w), BF16), pltpu.VMEM((pad + s_len, w), BF16),
                        pltpu.VMEM((pad + s_len, w), F32), pltpu.VMEM((pad + s_len, w), F32)],
        args=(qkv, qkv, qkv, bias, do), sem=("arbitrary", "arbitrary", "arbitrary"), hosted=hosted)


def _fill_padded_dup(dst, src, pad, h, half):
    other = pltpu.roll(src, D_HEAD, 1)
    _fill_padded(dst, jnp.where(half == h, src, other), pad)


def _attn_b_fwd(qkv, bias, sink):
    bsz, s_len, _ = qkv.shape
    pad = B_PREV * CHUNK
    band = TQ + pad
    kcol = B_Q_WIDTH // LANES
    npair = B_Q_HEADS // 2

    def body(q_ref, k_ref, v_ref, b_ref, s_ref, o_ref, kp, vp):
        i = pl.program_id(1)
        half = _lane_half((1, LANES))

        @pl.when(i == 0)
        def _():
            for h in range(B_KV_HEADS):
                _fill_padded_dup(kp.at[h], k_ref[...], pad, h, half)
                _fill_padded_dup(vp.at[h], v_ref[...], pad, h, half)

        qs = pl.multiple_of(i * TQ, TQ)

        ones = jnp.ones((band, LANES), BF16)

        def block(masked):
            for pr in range(npair):
                h = pr // (B_GROUP // 2)
                sl = slice(pr * LANES, (pr + 1) * LANES)
                kk = kp[h, pl.ds(qs, band), :]
                vv = jnp.concatenate([vp[h, pl.ds(qs, band), :], ones], axis=1)
                q = q_ref[:, sl] * SCALE
                outs = []
                for j in range(2):
                    qm = jnp.where(half == j, q, jnp.zeros_like(q))
                    sink = s_ref[2 * pr + j][0:1, 0:1]
                    p, m = _band_weights(qm, kk, b_ref[2 * pr + j], sink, qs if masked else None, pad)
                    outs.append(_weighted_values(p, vv, sink, m))
                o_ref[:, sl] = jnp.where(half == 0, outs[0], outs[1]).astype(BF16)

        pl.when(i < -(-pad // TQ))(lambda: block(True))
        pl.when(i >= -(-pad // TQ))(lambda: block(False))

    return pl.pallas_call(
        body,
        name="attn_b_fwd",
        grid=(bsz, s_len // TQ),
        in_specs=[pl.BlockSpec((None, TQ, B_Q_WIDTH), lambda b, i: (b, i, 0)),
                  pl.BlockSpec((None, s_len, LANES), lambda b, i: (b, 0, kcol)),
                  pl.BlockSpec((None, s_len, LANES), lambda b, i: (b, 0, kcol + 1)),
                  pl.BlockSpec((B_Q_HEADS, TQ, band), lambda b, i: (0, 0, 0)),
                  pl.BlockSpec((B_Q_HEADS, 8, LANES), lambda b, i: (0, 0, 0))],
        out_specs=pl.BlockSpec((None, TQ, B_Q_WIDTH), lambda b, i: (b, i, 0)),
        out_shape=jax.ShapeDtypeStruct((bsz, s_len, B_Q_WIDTH), BF16),
        scratch_shapes=[pltpu.VMEM((B_KV_HEADS, pad + s_len, LANES), BF16),
                        pltpu.VMEM((B_KV_HEADS, pad + s_len, LANES), BF16)],
        compiler_params=_cparams(("arbitrary", "arbitrary")),
    )(qkv, qkv, qkv, bias, sink)


def _attn_b_bwd(qkv, bias, sink, do, hosted=()):
    bsz, s_len, _ = qkv.shape
    pad = B_PREV * CHUNK
    band = TQ + pad
    kcol = B_Q_WIDTH // LANES
    n_i = s_len // TQ
    pp = B_GROUP // 2

    def body(q_ref, k_ref, v_ref, b_ref, s_ref, do_ref, dq_ref, dkv_ref, dsink_ref, kp, vp, dk_acc, dv_acc):
        b = pl.program_id(0)
        i = pl.program_id(1)
        half = _lane_half((1, LANES))

        @pl.when(i == 0)
        def _():
            for h in range(B_KV_HEADS):
                _fill_padded_dup(kp.at[h], k_ref[...], pad, h, half)
                _fill_padded_dup(vp.at[h], v_ref[...], pad, h, half)
            dk_acc[...] = jnp.zeros_like(dk_acc)
            dv_acc[...] = jnp.zeros_like(dv_acc)

        @pl.when(jnp.logical_and(b == 0, i == 0))
        def _():
            dsink_ref[...] = jnp.zeros_like(dsink_ref)

        qs = pl.multiple_of(i * TQ, TQ)

        def block(masked):
            heads_dk, heads_dv = [], []
            for h in range(B_KV_HEADS):
                kk = kp[h, pl.ds(qs, band), :]
                vv = vp[h, pl.ds(qs, band), :]
                dk2 = jnp.zeros((band, LANES), F32)
                dv2 = jnp.zeros((band, LANES), F32)
                for pr in range(pp * h, pp * (h + 1)):
                    sl = slice(pr * LANES, (pr + 1) * LANES)
                    q = q_ref[:, sl] * SCALE
                    dd = do_ref[:, sl]
                    dqs, dks, dvs = [], [], []
                    for j in range(2):
                        qm = jnp.where(half == j, q, jnp.zeros_like(q))
                        dm = jnp.where(half == j, dd, jnp.zeros_like(dd))
                        sink = s_ref[2 * pr + j][0:1, 0:1]
                        p, m, inv = _band_softmax(qm, kk, b_ref[2 * pr + j], sink, qs if masked else None, pad)
                        pn = p * inv
                        dp = _dot_nt(dm, vv)
                        delta = jnp.sum(pn * dp, axis=-1, keepdims=True)
                        ds = pn * (dp - delta)
                        dsb = ds.astype(BF16)
                        dqs.append(_dot_nn(dsb, kk))
                        dks.append(_dot_tn(dsb, q))
                        dvs.append(_dot_tn(pn.astype(BF16), dd))
                        dsk = jnp.sum(-(jnp.exp(sink - m) * inv) * delta, axis=0, keepdims=True)
                        dsink_ref[2 * pr + j] += jnp.broadcast_to(dsk, (8, LANES))
                    dq_ref[:, sl] = (jnp.where(half == 0, dqs[0], dqs[1]) * SCALE).astype(BF16)
                    dk2 = dk2 + jnp.where(half == 0, dks[0], dks[1])
                    dv2 = dv2 + jnp.where(half == 0, dvs[0], dvs[1])
                heads_dk.append(dk2 + pltpu.roll(dk2, D_HEAD, 1))
                heads_dv.append(dv2 + pltpu.roll(dv2, D_HEAD, 1))
            dk_acc[pl.ds(qs, band), :] += jnp.where(half == 0, heads_dk[0], heads_dk[1])
            dv_acc[pl.ds(qs, band), :] += jnp.where(half == 0, heads_dv[0], heads_dv[1])

        pl.when(i < -(-pad // TQ))(lambda: block(True))
        pl.when(i >= -(-pad // TQ))(lambda: block(False))

        @pl.when(i == n_i - 1)
        def _():
            dkv_ref[:, 0:LANES] = dk_acc[pad:, :].astype(BF16)
            dkv_ref[:, LANES:2 * LANES] = dv_acc[pad:, :].astype(BF16)

    qspec = pl.BlockSpec((None, TQ, B_Q_WIDTH), lambda b, i: (b, i, 0))
    return _call(
        body,
        name="attn_b_bwd",
        grid=(bsz, n_i),
        in_specs=[qspec,
                  pl.BlockSpec((None, s_len, LANES), lambda b, i: (b, 0, kcol)),
                  pl.BlockSpec((None, s_len, LANES), lambda b, i: (b, 0, kcol + 1)),
                  pl.BlockSpec((B_Q_HEADS, TQ, band), lambda b, i: (0, 0, 0)),
                  pl.BlockSpec((B_Q_HEADS, 8, LANES), lambda b, i: (0, 0, 0)),
                  qspec],
        out_specs=[qspec, pl.BlockSpec((None, s_len, 2 * LANES), lambda b, i: (b, 0, 0)),
                   pl.BlockSpec((B_Q_HEADS, 8, LANES), lambda b, i: (0, 0, 0))],
        out_shape=[jax.ShapeDtypeStruct((bsz, s_len, B_Q_WIDTH), BF16),
                   jax.ShapeDtypeStruct((bsz, s_len, 2 * B_KV_WIDTH), BF16),
                   jax.ShapeDtypeStruct((B_Q_HEADS, 8, LANES), F32)],
        scratch_shapes=[pltpu.VMEM((B_KV_HEADS, pad + s_len, LANES), BF16),
                        pltpu.VMEM((B_KV_HEADS, pad + s_len, LANES), BF16),
                        pltpu.VMEM((pad + s_len, LANES), F32), pltpu.VMEM((pad + s_len, LANES), F32)],
        args=(qkv, qkv, qkv, bias, sink, do), sem=("arbitrary", "arbitrary"), hosted=hosted)


REL_COLS = 3 * 128
REL_WRAP = 512


def _bias_a_build(tv, hosted=()):
    h = tv.shape[0]
    pad = A_PREV * CHUNK
    band = TQ + pad

    def body(tv_ref, o_ref):
        row = tv_ref[...]
        x = jnp.broadcast_to(row, (TQ, REL_WRAP))
        r = lax.broadcasted_iota(jnp.int32, x.shape, 0)
        for bit in range(8):
            sh = 1 << bit
            x = jnp.where((r & sh) != 0, pltpu.roll(x, sh, 1), x)
        far = jnp.broadcast_to(row[:, 0:1], (TQ, band - REL_COLS))
        full = jnp.concatenate([far, x[:, REL_WRAP // 2:REL_WRAP], x[:, 0:REL_COLS - REL_WRAP // 2]], axis=1)
        qc = (lax.broadcasted_iota(jnp.int32, full.shape, 0) + pad) // CHUNK
        kc = lax.broadcasted_iota(jnp.int32, full.shape, 1) // CHUNK
        ok = jnp.logical_and(kc <= qc, kc >= qc - A_PREV)
        o_ref[...] = jnp.where(ok, full, NEG_INF)

    return _call(
        body,
        name="bias_a_build",
        grid=(h,),
        in_specs=[pl.BlockSpec((None, 1, REL_WRAP), lambda hh: (hh, 0, 0))],
        out_specs=[pl.BlockSpec((None, TQ, band), lambda hh: (hh, 0, 0))],
        out_shape=[jax.ShapeDtypeStruct((h, TQ, band), F32)],
        args=(tv,), sem=("parallel",), hosted=hosted)[0]


def _relbias_grad(dbias):
    h, rows, _ = dbias.shape

    def body(d_ref, o_ref):
        x = d_ref[...]
        r = lax.broadcasted_iota(jnp.int32, x.shape, 0)
        c = lax.broadcasted_iota(jnp.int32, x.shape, 1) - r
        x = jnp.where(jnp.logical_and(c >= 1, c < REL_TABLE), x, 0.0)
        for bit in range(8):
            sh = 1 << bit
            x = jnp.where((r & sh) != 0, pltpu.roll(x, REL_COLS - sh, 1), x)
        diag = jnp.sum(x, axis=0, keepdims=True)
        lane = lax.broadcasted_iota(jnp.int32, diag.shape, 1)
        diag = jnp.where(jnp.logical_and(lane >= 1, lane < REL_TABLE), diag, 0.0)
        rest = -jnp.sum(diag, axis=1, keepdims=True)
        o_ref[...] = jnp.broadcast_to(jnp.where(lane == 0, rest, diag), o_ref.shape)

    return pl.pallas_call(
        body,
        name="relbias_grad",
        grid=(h,),
        in_specs=[pl.BlockSpec((None, rows, REL_COLS), lambda hh: (hh, 0, 0))],
        out_specs=pl.BlockSpec((None, 8, REL_COLS), lambda hh: (hh, 0, 0)),
        out_shape=jax.ShapeDtypeStruct((h, 8, REL_COLS), F32),
        compiler_params=_cparams(("parallel",)),
    )(dbias)


def _mix_out_fwd(x, oa, ob, gates, proj_t, wout):
    t = x.shape[0]

    def body(x_ref, oa_ref, ob_ref, gt_ref, pt_ref, wo_ref, y_ref, ya_ref, yb_ref, mg_ref):
        ya = _dot_nt(oa_ref[...], pt_ref[:, 0:A_WIDTH])
        yb = _dot_nt(ob_ref[...], pt_ref[:, A_WIDTH:A_WIDTH + B_Q_WIDTH])
        ya_ref[...] = ya.astype(BF16)
        yb_ref[...] = yb.astype(BF16)
        mg = jax.nn.sigmoid(gt_ref[:, 0:D_MODEL]) * ya + jax.nn.sigmoid(gt_ref[:, D_MODEL:2 * D_MODEL]) * yb
        mgb = mg.astype(BF16)
        mg_ref[...] = mgb
        y_ref[...] = x_ref[...] + _dot_nn(mgb, wo_ref[...])

    return pl.pallas_call(
        body,
        name="mix_out_fwd",
        grid=(t // TM,),
        in_specs=[_rows(TM, D_MODEL), _rows(TM, A_WIDTH), _rows(TM, B_Q_WIDTH), _rows(TM, 2 * D_MODEL),
                  _resident((D_MODEL, A_WIDTH + B_Q_WIDTH)), _resident((D_MODEL, D_MODEL))],
        out_specs=[_rows(TM, D_MODEL), _rows(TM, D_MODEL), _rows(TM, D_MODEL), _rows(TM, D_MODEL)],
        out_shape=[jax.ShapeDtypeStruct((t, D_MODEL), F32), jax.ShapeDtypeStruct((t, D_MODEL), BF16),
                   jax.ShapeDtypeStruct((t, D_MODEL), BF16), jax.ShapeDtypeStruct((t, D_MODEL), BF16)],
        compiler_params=_cparams(("parallel",)),
    )(x, oa, ob, gates, proj_t, wout)


def _mix_out_bwd(d, gates, ya, yb, mg, oa, ob, proj_t, wout, hosted=()):
    t = d.shape[0]
    nt = t // TM

    def body(d_ref, gt_ref, ya_ref, yb_ref, mg_ref, oa_ref, ob_ref, pt_ref, wo_ref,
             doa_ref, dob_ref, dgt_ref, gwo_ref, gwp_ref, acc_o, acc_p):
        i = pl.program_id(0)
        db = d_ref[...].astype(BF16)
        dmg = _dot_nt(db, wo_ref[...])
        sa = jax.nn.sigmoid(gt_ref[:, 0:D_MODEL])
        sb = jax.nn.sigmoid(gt_ref[:, D_MODEL:2 * D_MODEL])
        dya = (dmg * sa).astype(BF16)
        dyb = (dmg * sb).astype(BF16)
        dgt_ref[:, 0:D_MODEL] = (dmg * ya_ref[...].astype(F32) * (sa * (1.0 - sa))).astype(BF16)
        dgt_ref[:, D_MODEL:2 * D_MODEL] = (dmg * yb_ref[...].astype(F32) * (sb * (1.0 - sb))).astype(BF16)
        doa_ref[...] = _dot_nn(dya, pt_ref[:, 0:A_WIDTH]).astype(BF16)
        dob_ref[...] = _dot_nn(dyb, pt_ref[:, A_WIDTH:A_WIDTH + B_Q_WIDTH]).astype(BF16)

        @pl.when(i == 0)
        def _():
            acc_o[...] = jnp.zeros_like(acc_o)
            acc_p[...] = jnp.zeros_like(acc_p)

        acc_o[...] += _dot_tn(mg_ref[...], db)
        acc_p[:, 0:A_WIDTH] += _dot_tn(dya, oa_ref[...])
        acc_p[:, A_WIDTH:A_WIDTH + B_Q_WIDTH] += _dot_tn(dyb, ob_ref[...])

        @pl.when(i == nt - 1)
        def _():
            gwo_ref[...] = acc_o[...].astype(BF16)
            gwp_ref[...] = acc_p[...].astype(BF16)

    whole = pl.BlockSpec((D_MODEL, D_MODEL), lambda i: (0, 0))
    return _call(
        body,
        name="mix_out_bwd",
        grid=(nt,),
        in_specs=[_rows(TM, D_MODEL), _rows(TM, 2 * D_MODEL), _rows(TM, D_MODEL), _rows(TM, D_MODEL),
                  _rows(TM, D_MODEL), _rows(TM, A_WIDTH), _rows(TM, B_Q_WIDTH),
                  _resident((D_MODEL, A_WIDTH + B_Q_WIDTH)), _resident((D_MODEL, D_MODEL))],
        out_specs=[_rows(TM, A_WIDTH), _rows(TM, B_Q_WIDTH), _rows(TM, 2 * D_MODEL), whole, whole],
        out_shape=[jax.ShapeDtypeStruct((t, A_WIDTH), BF16), jax.ShapeDtypeStruct((t, B_Q_WIDTH), BF16),
                   jax.ShapeDtypeStruct((t, 2 * D_MODEL), BF16), jax.ShapeDtypeStruct((D_MODEL, D_MODEL), BF16),
                   jax.ShapeDtypeStruct((D_MODEL, D_MODEL), BF16)],
        scratch_shapes=[pltpu.VMEM((D_MODEL, D_MODEL), F32), pltpu.VMEM((D_MODEL, A_WIDTH + B_Q_WIDTH), F32)],
        args=(d, gates, ya, yb, mg, oa, ob, proj_t, wout), sem=("arbitrary",), hosted=hosted)


def _place():
    x, y, c = lax.axis_index("x"), lax.axis_index("y"), lax.axis_index("c")
    chips = [(1 - x, y), (x, 1 - y), (1 - x, 1 - y)]
    return x, y, c, chips


class _Gather:
    per = 8

    def __init__(self, shards):
        n = len(shards)
        self.inputs = list(shards)
        self.out_shape = [jax.ShapeDtypeStruct((N_DEV * s.shape[0], s.shape[1]), s.dtype) for s in shards]
        self.scratch = [pltpu.SemaphoreType.DMA((n * self.per,)), pltpu.SemaphoreType.DMA((n * self.per,)),
                        pltpu.SemaphoreType.DMA((n,))]
        self.result = None

    def _parts(self, ins, outs, sems):
        send_sems, recv_sems, local_sems = sems
        x, y, c, chips = _place()
        me, sibling = (x, y, c), (x, y, 1 - c)
        xn, yn, dg = chips
        n = len(ins)

        def rows(k, p, part=None):
            r = ins[k].shape[0]
            base = (4 * p[0] + 2 * p[1] + p[2]) * r
            if part is None:
                return outs[k].at[pl.ds(base, r), :]
            return outs[k].at[pl.ds(base + part * (r // 2), r // 2), :]

        def copy(k, slot, block, to, src=None, part=None):
            return pltpu.make_async_remote_copy(
                src_ref=rows(k, block, part) if src is None else src, dst_ref=rows(k, block, part),
                send_sem=send_sems.at[k * self.per + slot], recv_sem=recv_sems.at[k * self.per + slot],
                device_id=to, device_id_type=MESH)

        mine = [pltpu.make_async_copy(ins[k], rows(k, me), local_sems.at[k]) for k in range(n)]
        sends, lands = [], []
        for k in range(n):
            sends.append({
                0: copy(k, 0, me, sibling, src=ins[k]),
                1: copy(k, 1, me, (*xn, c), src=ins[k]),
                2: copy(k, 2, me, (*yn, c), src=ins[k]),
                3: copy(k, 3, (*xn, c), (*yn, c), part=0),
                4: copy(k, 4, (*yn, c), (*xn, c), part=1),
                5: copy(k, 5, (*xn, c), sibling),
                6: copy(k, 6, (*yn, c), sibling),
                7: copy(k, 7, (*dg, c), sibling)})
            lands.append({
                0: copy(k, 0, sibling, me),
                1: copy(k, 1, (*xn, c), me),
                2: copy(k, 2, (*yn, c), me),
                3: copy(k, 3, (*dg, c), me, part=0),
                4: copy(k, 4, (*dg, c), me, part=1),
                5: copy(k, 5, (*xn, 1 - c), me),
                6: copy(k, 6, (*yn, 1 - c), me),
                7: copy(k, 7, (*dg, 1 - c), me)})
        return n, mine, sends, lands

    def start(self, ins, outs, sems):
        n, mine, sends, _ = self._parts(ins, outs, sems)
        for cp in mine:
            cp.start()
        for slot in (0, 1, 2):
            for k in range(n):
                sends[k][slot].start()

    def relay(self, ins, outs, sems):
        n, _, sends, lands = self._parts(ins, outs, sems)
        for k in range(n):
            lands[k][1].wait_recv()
            sends[k][3].start()
            sends[k][5].start()
        for k in range(n):
            lands[k][2].wait_recv()
            sends[k][4].start()
            sends[k][6].start()

    def forward(self, ins, outs, sems):
        n, _, sends, lands = self._parts(ins, outs, sems)
        for k in range(n):
            lands[k][3].wait_recv()
            lands[k][4].wait_recv()
            sends[k][7].start()

    def finish(self, ins, outs, sems):
        n, mine, sends, lands = self._parts(ins, outs, sems)
        for k in range(n):
            for slot in (0, 5, 6, 7):
                lands[k][slot].wait_recv()
        for k in range(n):
            for slot in range(self.per):
                sends[k][slot].wait_send()
        for cp in mine:
            cp.wait()


class _PairExchange:
    def __init__(self, grads):
        n = len(grads)
        self.inputs = list(grads)
        self.out_shape = [jax.ShapeDtypeStruct((g.shape[0] // 2, g.shape[1]), g.dtype) for g in grads]
        self.scratch = [pltpu.SemaphoreType.DMA((n * N_CHIP,)), pltpu.SemaphoreType.DMA((n * N_CHIP,))]
        self.result = None

    def _copies(self, ins, outs, sems):
        send_sems, recv_sems = sems
        x, y, c, _ = _place()
        copies = []
        for k in range(len(ins)):
            r = ins[k].shape[0] // N_DEV
            for q in range(N_CHIP):
                copies.append(pltpu.make_async_remote_copy(
                    src_ref=ins[k].at[pl.ds((2 * q + 1 - c) * r, r), :], dst_ref=outs[k].at[pl.ds(q * r, r), :],
                    send_sem=send_sems.at[k * N_CHIP + q], recv_sem=recv_sems.at[k * N_CHIP + q],
                    device_id=(x, y, 1 - c), device_id_type=MESH))
        return copies

    def start(self, ins, outs, sems):
        for cp in self._copies(ins, outs, sems):
            cp.start()

    def relay(self, ins, outs, sems):
        pass

    def forward(self, ins, outs, sems):
        pass

    def finish(self, ins, outs, sems):
        copies = self._copies(ins, outs, sems)
        for cp in copies:
            cp.wait_recv()
        for cp in copies:
            cp.wait_send()


class _ChipExchange(_PairExchange):
    def __init__(self, psums):
        n = len(psums)
        self.inputs = list(psums)
        self.out_shape = [jax.ShapeDtypeStruct((3 * p.shape[0] // N_CHIP, p.shape[1]), p.dtype) for p in psums]
        self.scratch = [pltpu.SemaphoreType.DMA((n * 3,)), pltpu.SemaphoreType.DMA((n * 3,))]
        self.result = None

    def _copies(self, ins, outs, sems):
        send_sems, recv_sems = sems
        _, _, c, chips = _place()
        copies = []
        for k in range(len(ins)):
            r = ins[k].shape[0] // N_CHIP
            for j, chip in enumerate(chips):
                copies.append(pltpu.make_async_remote_copy(
                    src_ref=ins[k].at[pl.ds((2 * chip[0] + chip[1]) * r, r), :], dst_ref=outs[k].at[pl.ds(j * r, r), :],
                    send_sem=send_sems.at[k * 3 + j], recv_sem=recv_sems.at[k * 3 + j],
                    device_id=(*chip, c), device_id_type=MESH))
        return copies


def _exchange_alone(xchg, name):
    n_in, n_out = len(xchg.inputs), len(xchg.out_shape)

    def body(*refs):
        ins, outs, sems = refs[:n_in], refs[n_in:n_in + n_out], refs[n_in + n_out:]
        xchg.start(ins, outs, sems)
        xchg.relay(ins, outs, sems)
        xchg.forward(ins, outs, sems)
        xchg.finish(ins, outs, sems)

    xchg.result = list(pl.pallas_call(
        body, name=name, in_specs=[_hbm()] * n_in, out_specs=[_hbm()] * n_out, out_shape=xchg.out_shape,
        scratch_shapes=xchg.scratch)(*xchg.inputs))
    return xchg.result


def _pair_sum(core, grads, recvd, name):
    n = len(grads)
    r = grads[0].shape[0] // N_DEV
    cdim = grads[0].shape[1]
    tr = r // 2 if r % 32 == 0 else r
    nt = r // tr

    def body(core_ref, *refs):
        del core_ref
        for k in range(n):
            refs[2 * n + k][...] = (refs[k][...].astype(F32) + refs[n + k][...].astype(F32)).astype(BF16)

    gspec = pl.BlockSpec((tr, cdim), lambda q, i, core_ref: ((2 * q + core_ref[0]) * nt + i, 0))
    rspec = pl.BlockSpec((tr, cdim), lambda q, i, core_ref: (q * nt + i, 0))
    return pl.pallas_call(
        body,
        name=name,
        grid_spec=pltpu.PrefetchScalarGridSpec(
            num_scalar_prefetch=1, grid=(N_CHIP, nt), in_specs=[gspec] * n + [rspec] * n, out_specs=[rspec] * n),
        out_shape=[jax.ShapeDtypeStruct((N_CHIP * r, cdim), BF16) for _ in range(n)],
        compiler_params=_cparams(("parallel", "parallel")),
    )(core, *grads, *recvd)


def _final_sum(chip, psums, recvd, name):
    n = len(psums)
    r = psums[0].shape[0] // N_CHIP
    cdim = psums[0].shape[1]
    tr = r // 2 if r % 32 == 0 else r
    nt = r // tr

    def body(chip_ref, *refs):
        del chip_ref
        for k in range(n):
            got = refs[n + k]
            tot = refs[k][...].astype(F32) + got[0].astype(F32)
            tot = tot + got[1].astype(F32)
            tot = tot + got[2].astype(F32)
            refs[2 * n + k][...] = tot

    pspec = pl.BlockSpec((tr, cdim), lambda i, chip_ref: (chip_ref[0] * nt + i, 0))
    rspec = pl.BlockSpec((3, tr, cdim), lambda i, chip_ref: (0, i, 0))
    ospec = pl.BlockSpec((tr, cdim), lambda i, chip_ref: (i, 0))
    return pl.pallas_call(
        body,
        name=name,
        grid_spec=pltpu.PrefetchScalarGridSpec(
            num_scalar_prefetch=1, grid=(nt,), in_specs=[pspec] * n + [rspec] * n, out_specs=[ospec] * n),
        out_shape=[jax.ShapeDtypeStruct((r, cdim), F32) for _ in range(n)],
        compiler_params=_cparams(("parallel",)),
    )(chip, *psums, *[g.reshape(3, r, cdim) for g in recvd])


SMALL_ROWS = 16


def _all_reduce_small(part):
    def body(p_ref, o_ref, buf, send_sems, recv_sems):
        x, y, c, _ = _place()
        me = 4 * x + 2 * y + c
        buf[me] = p_ref[...]
        copies = []
        for d in range(1, N_DEV):
            peer = me ^ d
            copies.append(pltpu.make_async_remote_copy(
                src_ref=p_ref, dst_ref=buf.at[me], send_sem=send_sems.at[d - 1], recv_sem=recv_sems.at[d - 1],
                device_id=(peer // 4, (peer // 2) % 2, peer % 2), device_id_type=MESH))
        for cp in copies:
            cp.start()
        for cp in copies:
            cp.wait_recv()
        for cp in copies:
            cp.wait_send()
        tot = buf[0]
        for d in range(1, N_DEV):
            tot = tot + buf[d]
        o_ref[...] = tot

    return pl.pallas_call(
        body,
        name="all_reduce_small",
        in_specs=[pl.BlockSpec(memory_space=pltpu.VMEM)],
        out_specs=pl.BlockSpec(memory_space=pltpu.VMEM),
        out_shape=jax.ShapeDtypeStruct(part.shape, F32),
        scratch_shapes=[pltpu.VMEM((N_DEV,) + part.shape, F32), pltpu.SemaphoreType.DMA((N_DEV - 1,)),
                        pltpu.SemaphoreType.DMA((N_DEV - 1,))],
    )(part)


ADAMW_STEPS = 4


def _adamw(ws, gs, ms, vs, name, hosted=()):
    n = len(ws)
    steps = ADAMW_STEPS if all(w.shape[0] % (8 * ADAMW_STEPS) == 0 for w in ws) else 1
    c1 = 1.0 - ADAM_B1 ** ADAM_STEP
    c2 = 1.0 - ADAM_B2 ** ADAM_STEP

    def body(*refs):
        for k in range(n):
            w, g, m, v = (refs[j * n + k][...] for j in range(4))
            m2 = ADAM_B1 * m + (1.0 - ADAM_B1) * g
            v2 = ADAM_B2 * v + (1.0 - ADAM_B2) * (g * g)
            delta = -ADAM_LR * ((m2 / c1) / (jnp.sqrt(v2 / c2) + ADAM_EPS) + ADAM_WD * w)
            refs[4 * n + k][...] = delta
            refs[5 * n + k][...] = m2
            refs[6 * n + k][...] = v2

    specs = [pl.BlockSpec((w.shape[0] // steps, w.shape[1]), lambda i: (i, 0)) for w in ws]
    shapes = [jax.ShapeDtypeStruct(w.shape, F32) for w in ws]
    outs = _call(
        body,
        name=name,
        grid=(steps,),
        in_specs=specs * 4,
        out_specs=specs * 3,
        out_shape=shapes * 3,
        args=(*ws, *gs, *ms, *vs), sem=("parallel",), hosted=hosted)
    return outs[:n], outs[n:2 * n], outs[2 * n:]


def _adamw_reduced(chip, ws, psums, recvd, ms, vs, steps, name):
    n = len(ws)
    c1 = 1.0 - ADAM_B1 ** ADAM_STEP
    c2 = 1.0 - ADAM_B2 ** ADAM_STEP

    def body(chip_ref, *refs):
        del chip_ref
        for k in range(n):
            w, m, v = (refs[j * n + k][...] for j in (0, 3, 4))
            got = refs[2 * n + k]
            g = refs[n + k][...].astype(F32) + got[0].astype(F32)
            g = g + got[1].astype(F32)
            g = g + got[2].astype(F32)
            m2 = ADAM_B1 * m + (1.0 - ADAM_B1) * g
            v2 = ADAM_B2 * v + (1.0 - ADAM_B2) * (g * g)
            refs[5 * n + k][...] = g
            refs[6 * n + k][...] = -ADAM_LR * ((m2 / c1) / (jnp.sqrt(v2 / c2) + ADAM_EPS) + ADAM_WD * w)
            refs[7 * n + k][...] = m2
            refs[8 * n + k][...] = v2

    def blk(w):
        return (w.shape[0] // steps, w.shape[1])

    own = [pl.BlockSpec(blk(w), lambda i, chip_ref: (i, 0)) for w in ws]
    psum = [pl.BlockSpec(blk(w), lambda i, chip_ref: (chip_ref[0] * steps + i, 0)) for w in ws]
    recv = [pl.BlockSpec((3,) + blk(w), lambda i, chip_ref: (0, i, 0)) for w in ws]
    shapes = [jax.ShapeDtypeStruct(w.shape, F32) for w in ws]
    outs = pl.pallas_call(
        body,
        name=name,
        grid_spec=pltpu.PrefetchScalarGridSpec(
            num_scalar_prefetch=1, grid=(steps,), in_specs=own + psum + recv + own + own, out_specs=own * 4),
        out_shape=shapes * 4,
        compiler_params=_cparams(("parallel",)),
    )(chip, *ws, *psums, *[r.reshape((3,) + w.shape) for r, w in zip(recvd, ws)], *ms, *vs)
    return outs[:n], outs[n:2 * n], outs[2 * n:3 * n], outs[3 * n:]


def _bias_b():
    pad = B_PREV * CHUNK
    slopes = np.array([2.0 ** (-8.0 * (i + 1) / B_Q_HEADS) for i in range(B_Q_HEADS)], dtype=np.float32)
    dist = np.abs(np.arange(TQ)[:, None] - np.arange(TQ + pad)[None, :] + pad).astype(np.float32)
    bias = -slopes.reshape(B_Q_HEADS, 1, 1) * dist[None]
    qc = (np.arange(TQ)[:, None] + pad) // CHUNK
    kc = np.arange(TQ + pad)[None, :] // CHUNK
    allowed = (kc <= qc) & (kc >= qc - B_PREV)
    return np.where(allowed[None], bias, np.float32(NEG_INF)).astype(np.float32)


def kernel(x, ffn1_norm, ffn1_w_gate, ffn1_w_up, ffn1_w_down, mix_norm, w_in, rel_bias, sinks, w_proj_a, w_proj_b, w_out, ffn2_norm, ffn2_w_gate, ffn2_w_up, ffn2_w_down, final_norm, loss_target, m_ffn1_norm, m_ffn1_w_gate, m_ffn1_w_up, m_ffn1_w_down, m_mix_norm, m_w_in, m_rel_bias, m_sinks, m_w_proj_a, m_w_proj_b, m_w_out, m_ffn2_norm, m_ffn2_w_gate, m_ffn2_w_up, m_ffn2_w_down, m_final_norm, v_ffn1_norm, v_ffn1_w_gate, v_ffn1_w_up, v_ffn1_w_down, v_mix_norm, v_w_in, v_rel_bias, v_sinks, v_w_proj_a, v_w_proj_b, v_w_out, v_ffn2_norm, v_ffn2_w_gate, v_ffn2_w_up, v_ffn2_w_down, v_final_norm):
    bsz, s_len, _ = x.shape
    t = bsz * s_len
    core = lax.axis_index("c").astype(jnp.int32).reshape(1)
    chip = (2 * lax.axis_index("x") + lax.axis_index("y")).astype(jnp.int32).reshape(1)

    proj_rows = jnp.concatenate([w_proj_a.T, w_proj_b.T], axis=1)
    sh_g1, sh_u1, sh_d1, sh_in, sh_proj, sh_out, sh_g2, sh_u2, sh_d2 = _to_bf16(
        [ffn1_w_gate.T, ffn1_w_up.T, ffn1_w_down, w_in.T, proj_rows, w_out, ffn2_w_gate.T, ffn2_w_up.T, ffn2_w_down],
        "weights_to_bf16")

    gather_up1 = _Gather([sh_g1, sh_u1])
    far = jnp.broadcast_to(rel_bias[:, REL_TABLE - 1:REL_TABLE], (A_HEADS, REL_WRAP // 2))
    tv = jnp.concatenate([far, jnp.flip(rel_bias, axis=1), jnp.zeros((A_HEADS, REL_WRAP // 2 - REL_TABLE), F32)], axis=1)
    bias_a = _bias_a_build(tv.reshape(A_HEADS, 1, REL_WRAP), hosted=[gather_up1])
    wg1, wu1 = gather_up1.result
    gather_down1 = _Gather([sh_d1, sh_in])
    gather_out = _Gather([sh_proj, sh_out])
    gather_ffn2_gate = _Gather([sh_g2])
    gather_ffn2_rest = _Gather([sh_u2, sh_d2])

    x0 = x.reshape(t, D_MODEL)
    tgt = loss_target.reshape(t, D_MODEL)
    gam1, gam2, gam3, gam4 = (g.reshape(1, D_MODEL) for g in (ffn1_norm, mix_norm, ffn2_norm, final_norm))

    h1, g1, u1, a1 = _ffn_up(x0, gam1, wg1, wu1, "ffn1_up", hosted=[gather_down1])
    wd1, win_t = gather_down1.result
    x1 = _ffn_down(x0, a1, wd1, "ffn1_down", hosted=[gather_out])
    proj_t, wout = gather_out.result
    h2, qkv_a, qkv_b, gates = _proj_fwd(x1, gam2, win_t, hosted=[gather_ffn2_gate])
    (wg2,) = gather_ffn2_gate.result
    qkv_a3 = qkv_a.reshape(bsz, s_len, QKV_A)
    qkv_b3 = qkv_b.reshape(bsz, s_len, QKV_B)

    bias_b = jnp.asarray(_bias_b())
    sink_rows = jnp.broadcast_to(sinks.reshape(B_Q_HEADS, 1, 1), (B_Q_HEADS, 8, LANES))

    oa = _attn_a_fwd(qkv_a3, bias_a, hosted=[gather_ffn2_rest]).reshape(t, A_WIDTH)
    wu2, wd2 = gather_ffn2_rest.result
    ob = _attn_b_fwd(qkv_b3, bias_b, sink_rows).reshape(t, B_Q_WIDTH)
    x2, ya, yb, mg = _mix_out_fwd(x1, oa, ob, gates, proj_t, wout)
    h3, g2, u2, a2, x3 = _ffn_fwd(x2, gam3, wg2, wu2, wd2, "ffn2_fwd")

    dx2, dg2, du2, db2, dgam3, dgam4, loss_part = _ffn_bwd_head(x3, gam4, tgt, x2, gam3, g2, u2, wg2, wu2, wd2,
                                                                "ffn2_bwd")
    gw_ffn2 = [_mm_tn([dg2], h3, "grad_ffn2_gate"), _mm_tn([du2], h3, "grad_ffn2_up"),
               _mm_tn([a2], db2, "grad_ffn2_down")]
    pairx_ffn2 = _PairExchange(gw_ffn2)
    doa, dob, dgates, gw_out, gw_proj = _mix_out_bwd(dx2, gates, ya, yb, mg, oa, ob, proj_t, wout,
                                                     hosted=[pairx_ffn2])
    psum_ffn2 = _pair_sum(core, gw_ffn2, pairx_ffn2.result, "pair_sum_ffn2")

    chipx_ffn2 = _ChipExchange(psum_ffn2)
    dqa, dka, dva, dbias_a = _attn_a_bwd(qkv_a3, bias_a, doa.reshape(bsz, s_len, A_WIDTH), hosted=[chipx_ffn2])
    pairx_out = _PairExchange([gw_proj, gw_out])
    dqb, dkvb, dsink = _attn_b_bwd(qkv_b3, bias_b, sink_rows, dob.reshape(bsz, s_len, B_Q_WIDTH), hosted=[pairx_out])
    drel_lanes = _relbias_grad(dbias_a)
    dproj = [dqa.reshape(t, A_WIDTH), dka.reshape(t, A_WIDTH), dva.reshape(t, A_WIDTH), dqb.reshape(t, B_Q_WIDTH),
             dkvb.reshape(t, 2 * B_KV_WIDTH), dgates]

    gw_in = _mm_tn(dproj, h2, "grad_w_in")
    pairx_in = _PairExchange([gw_in])
    psum_out = _pair_sum(core, [gw_proj, gw_out], pairx_out.result, "pair_sum_mix")
    chipx_out = _ChipExchange(psum_out)
    dx1, db1, dgam2 = _proj_bwd(dx2, x1, gam2, dproj, win_t, hosted=[pairx_in, chipx_out])
    psum_in = _pair_sum(core, [gw_in], pairx_in.result, "pair_sum_w_in")
    gw_d1 = _mm_tn([a1], db1, "grad_ffn1_down")

    chipx_in = _ChipExchange(psum_in)
    pairx_d1 = _PairExchange([gw_d1])
    dg1, du1 = _ffn_bwd_act(dx1, g1, u1, wd1, "ffn1_bwd_act", hosted=[chipx_in, pairx_d1])
    psum_d1 = _pair_sum(core, [gw_d1], pairx_d1.result, "pair_sum_ffn1_down")
    chipx_d1 = _ChipExchange(psum_d1)
    gw_g1 = _mm_tn([dg1], h1, "grad_ffn1_gate", hosted=[chipx_d1])
    from_sibling_g1 = _exchange_alone(_PairExchange([gw_g1]), "pair_exchange_ffn1_gate")
    psum_g1 = _pair_sum(core, [gw_g1], from_sibling_g1, "pair_sum_ffn1_gate")
    chipx_g1 = _ChipExchange(psum_g1)
    gw_u1 = _mm_tn([du1], h1, "grad_ffn1_up", hosted=[chipx_g1])
    from_sibling_u1 = _exchange_alone(_PairExchange([gw_u1]), "pair_exchange_ffn1_up")
    psum_u1 = _pair_sum(core, [gw_u1], from_sibling_u1, "pair_sum_ffn1_up")
    chipx_u1 = _ChipExchange(psum_u1)
    dx0, dgam1 = _ffn_bwd_in(dx1, x0, gam1, dg1, du1, wg1, wu1, "ffn1_bwd_in", hosted=[chipx_u1])

    (g_proj,) = _final_sum(chip, psum_out[0:1], chipx_out.result[0:1], "grad_sum_proj")
    grads = {"w_proj_a": g_proj[:, 0:A_WIDTH].T, "w_proj_b": g_proj[:, A_WIDTH:].T}

    def row_of(v):
        return jnp.pad(v.reshape(1, -1), ((0, 0), (0, D_MODEL - v.size)))

    def table_rows(v):
        return jnp.pad(v, ((0, 0), (0, D_MODEL - REL_TABLE)))

    drel_local = jnp.flip(drel_lanes[:, 0, 0:REL_TABLE], axis=1)
    small_part = jnp.concatenate(
        [jnp.sum(dgam1, axis=0, keepdims=True), jnp.sum(dgam2, axis=0, keepdims=True),
         jnp.sum(dgam3, axis=0, keepdims=True), jnp.sum(dgam4, axis=0, keepdims=True),
         row_of(jnp.sum(loss_part)), row_of(dsink[:, 0, 0]), jnp.zeros((2, D_MODEL), F32),
         table_rows(drel_local)], axis=0)
    small = _all_reduce_small(small_part)
    loss = small[4, 0]

    def pack(n1, n2, n3, n4, sk, tb):
        return jnp.concatenate([n1.reshape(1, -1), n2.reshape(1, -1), n3.reshape(1, -1), n4.reshape(1, -1),
                                jnp.zeros((1, D_MODEL), F32), row_of(sk), jnp.zeros((2, D_MODEL), F32), table_rows(tb)],
                               axis=0)

    live = np.zeros((SMALL_ROWS, D_MODEL), np.float32)
    live[0:4] = 1.0
    live[5, 0:B_Q_HEADS] = 1.0
    live[8:16, 0:REL_TABLE] = 1.0
    small_g = small * jnp.asarray(live)
    sw = pack(ffn1_norm, mix_norm, ffn2_norm, final_norm, sinks, rel_bias)
    sm = pack(m_ffn1_norm, m_mix_norm, m_ffn2_norm, m_final_norm, m_sinks, m_rel_bias)
    sv = pack(v_ffn1_norm, v_mix_norm, v_ffn2_norm, v_final_norm, v_sinks, v_rel_bias)
    (sd,), (snm,), (snv,) = _adamw([sw], [small_g], [sm], [sv], "adamw_small")

    def unpack(p):
        return {"ffn1_norm": p[0], "mix_norm": p[1], "ffn2_norm": p[2], "final_norm": p[3],
                "sinks": p[5, 0:B_Q_HEADS], "rel_bias": p[8:16, 0:REL_TABLE]}

    grads.update(unpack(small_g))
    delta, new_m, new_v = unpack(sd), unpack(snm), unpack(snv)

    wmv = {
        "ffn1_w_gate": (ffn1_w_gate, m_ffn1_w_gate, v_ffn1_w_gate), "ffn1_w_up": (ffn1_w_up, m_ffn1_w_up, v_ffn1_w_up),
        "ffn1_w_down": (ffn1_w_down, m_ffn1_w_down, v_ffn1_w_down), "w_in": (w_in, m_w_in, v_w_in),
        "w_proj_a": (w_proj_a, m_w_proj_a, v_w_proj_a), "w_proj_b": (w_proj_b, m_w_proj_b, v_w_proj_b),
        "w_out": (w_out, m_w_out, v_w_out),
        "ffn2_w_gate": (ffn2_w_gate, m_ffn2_w_gate, v_ffn2_w_gate), "ffn2_w_up": (ffn2_w_up, m_ffn2_w_up, v_ffn2_w_up),
        "ffn2_w_down": (ffn2_w_down, m_ffn2_w_down, v_ffn2_w_down),
    }
    row_form_names = ("ffn1_w_gate", "ffn1_w_up", "w_in", "ffn2_w_gate", "ffn2_w_up")

    def form(n, a):
        return a.T if n in row_form_names else a

    def reduced_group(gname, names, psums, recvd, steps):
        gs_, ds_, ms_, vs_ = _adamw_reduced(
            chip, [form(n, wmv[n][0]) for n in names], psums, recvd, [form(n, wmv[n][1]) for n in names],
            [form(n, wmv[n][2]) for n in names], steps, gname)
        for n, g_, d_, m_, v_ in zip(names, gs_, ds_, ms_, vs_):
            grads[n], delta[n], new_m[n], new_v[n] = form(n, g_), form(n, d_), form(n, m_), form(n, v_)

    reduced_group("adamw_ffn", ["ffn1_w_gate", "ffn1_w_up", "ffn1_w_down", "ffn2_w_gate", "ffn2_w_up", "ffn2_w_down"],
                  psum_g1 + psum_u1 + psum_d1 + psum_ffn2,
                  chipx_g1.result + chipx_u1.result + chipx_d1.result + chipx_ffn2.result, 11)
    reduced_group("adamw_w_in", ["w_in"], psum_in, chipx_in.result, 2)
    reduced_group("adamw_w_out", ["w_out"], psum_out[1:2], chipx_out.result[1:2], 2)
    names = ["w_proj_a", "w_proj_b"]
    ds_, ms_, vs_ = _adamw([wmv[n][0] for n in names], [grads[n] for n in names], [wmv[n][1] for n in names],
                           [wmv[n][2] for n in names], "adamw_proj")
    for n, d_, m_, v_ in zip(names, ds_, ms_, vs_):
        delta[n], new_m[n], new_v[n] = d_, m_, v_

    order = ["ffn1_norm", "ffn1_w_gate", "ffn1_w_up", "ffn1_w_down", "mix_norm", "w_in", "rel_bias", "sinks",
             "w_proj_a", "w_proj_b", "w_out", "ffn2_norm", "ffn2_w_gate", "ffn2_w_up", "ffn2_w_down", "final_norm"]
    grad_x = dx0.reshape(bsz, s_len, D_MODEL)
    return (loss, grad_x, *[grads[n] for n in order], *[delta[n] for n in order], *[new_m[n] for n in order],
            *[new_v[n] for n in order])
```

```python
import numpy as np
import jax
import jax.numpy as jnp
from jax import lax
from jax.experimental import pallas as pl
from jax.experimental.pallas import tpu as pltpu

F32 = jnp.float32
BF16 = jnp.bfloat16

D_MODEL = 1024
D_FF = 2816
CHUNK = 64
D_HEAD = 64
A_HEADS = 8
A_PREV = 8
MAX_REL = 128
B_Q_HEADS = 8
B_KV_HEADS = 2
B_GROUP = B_Q_HEADS // B_KV_HEADS
B_PREV = 2
REL_TABLE = (CHUNK - 1) + MAX_REL + 1
A_WIDTH = A_HEADS * D_HEAD
B_Q_WIDTH = B_Q_HEADS * D_HEAD
B_KV_WIDTH = B_KV_HEADS * D_HEAD
QKV_A = 3 * A_WIDTH
QKV_B = B_Q_WIDTH + 2 * B_KV_WIDTH
IN_WIDTH = QKV_A + QKV_B + 2 * D_MODEL
EPS = 1e-6
NEG_INF = -1e30
SCALE = 1.0 / 8.0

ADAM_LR = 0.001
ADAM_B1 = 0.9
ADAM_B2 = 0.999
ADAM_EPS = 1e-08
ADAM_WD = 0.01
ADAM_STEP = 10

N_DEV = 8
N_CHIP = 4
MESH = pl.DeviceIdType.MESH

LANES = 128
TQ = 256
TM = 256
TM_WIDE = 512
FC = 256
VMEM_LIMIT = 56 << 20


def _cparams(sem, vmem=VMEM_LIMIT):
    return pltpu.CompilerParams(dimension_semantics=sem, vmem_limit_bytes=vmem)


def _dot_nt(a, b):
    return lax.dot_general(a, b, (((1,), (1,)), ((), ())), preferred_element_type=F32)


def _dot_nn(a, b):
    return lax.dot_general(a, b, (((1,), (0,)), ((), ())), preferred_element_type=F32)


def _dot_tn(a, b):
    return lax.dot_general(a, b, (((0,), (0,)), ((), ())), preferred_element_type=F32)


def _resident(shape):
    nd = len(shape)
    return pl.BlockSpec(shape, lambda *_: (0,) * nd, pipeline_mode=pl.Buffered(1))


def _rows(tm, width):
    return pl.BlockSpec((tm, width), lambda i: (i, 0))


def _colsum8(v):
    tm, n = v.shape
    return jnp.sum(v.reshape(tm // 8, 8, n), axis=0)


def _rms(x):
    r = lax.rsqrt(jnp.mean(x * x, axis=-1, keepdims=True) + EPS)
    return x * r, r


def _rms_bwd(dh, xh, r, gamma):
    dxh = dh * gamma
    dx = r * (dxh - xh * jnp.mean(dxh * xh, axis=-1, keepdims=True))
    return dx, _colsum8(dh * xh)


def _hbm():
    return pl.BlockSpec(memory_space=pltpu.HBM)


def _call(body, *, name, grid, in_specs, out_specs, out_shape, args, sem, scratch_shapes=(), hosted=()):
    in_specs, out_specs, out_shape = list(in_specs), list(out_specs), list(out_shape)
    scratch_shapes = list(scratch_shapes)
    if not hosted:
        return pl.pallas_call(body, name=name, grid=grid, in_specs=in_specs, out_specs=out_specs, out_shape=out_shape,
                              scratch_shapes=scratch_shapes, compiler_params=_cparams(sem))(*args)
    n_in, n_out, n_scr = len(in_specs), len(out_specs), len(scratch_shapes)
    x_in = [a for x in hosted for a in x.inputs]
    x_out = [s for x in hosted for s in x.out_shape]
    x_scr = [s for x in hosted for s in x.scratch]
    steps = int(np.prod(grid))
    forward_step = max(steps - 3, 0)
    relay_step = min((5 * steps) // 8, forward_step)

    def wrapped(*refs):
        pos = [0]

        def take(k):
            pos[0] += k
            return refs[pos[0] - k:pos[0]]

        ins, xin, outs, xout, scr, xscr = (take(k) for k in (n_in, len(x_in), n_out, len(x_out), n_scr, len(x_scr)))
        step = 0
        for axis, extent in enumerate(grid):
            step = step * extent + pl.program_id(axis)
        own, oi, oo, osc = [], 0, 0, 0
        for x in hosted:
            own.append((xin[oi:oi + len(x.inputs)], xout[oo:oo + len(x.out_shape)], xscr[osc:osc + len(x.scratch)]))
            oi, oo, osc = oi + len(x.inputs), oo + len(x.out_shape), osc + len(x.scratch)

        def phase(method):
            for x, (i_, o_, s_) in zip(hosted, own):
                getattr(x, method)(i_, o_, s_)

        pl.when(step == 0)(lambda: phase("start"))
        body(*ins, *outs, *scr)
        pl.when(step == relay_step)(lambda: phase("relay"))
        pl.when(step == forward_step)(lambda: phase("forward"))
        pl.when(step == steps - 1)(lambda: phase("finish"))

    res = pl.pallas_call(
        wrapped, name=name, grid=grid, in_specs=in_specs + [_hbm()] * len(x_in),
        out_specs=out_specs + [_hbm()] * len(x_out), out_shape=out_shape + x_out,
        scratch_shapes=scratch_shapes + x_scr, compiler_params=_cparams(("arbitrary",) * len(grid)))(*args, *x_in)
    rest = list(res[n_out:])
    for x in hosted:
        x.result, rest = rest[:len(x.out_shape)], rest[len(x.out_shape):]
    return list(res[:n_out])


def _to_bf16(arrays, name):
    n = len(arrays)

    def body(*refs):
        for k in range(n):
            refs[n + k][...] = refs[k][...].astype(BF16)

    specs = [pl.BlockSpec(a.shape, lambda i: (0, 0)) for a in arrays]
    return pl.pallas_call(
        body, name=name, grid=(1,), in_specs=specs, out_specs=specs,
        out_shape=[jax.ShapeDtypeStruct(a.shape, BF16) for a in arrays],
        compiler_params=_cparams(("arbitrary",)))(*arrays)


def _ffn_fwd(x, gamma, wg_t, wu_t, wd, name, hosted=()):
    t = x.shape[0]
    f = wg_t.shape[0]

    def body(x_ref, gam_ref, wg_ref, wu_ref, wd_ref, h_ref, g_ref, u_ref, a_ref, y_ref):
        xv = x_ref[...]
        xh, _ = _rms(xv)
        h = (xh * gam_ref[...]).astype(BF16)
        h_ref[...] = h
        for j in range(f // FC):
            sl = slice(j * FC, (j + 1) * FC)
            g = _dot_nt(h, wg_ref[sl, :])
            u = _dot_nt(h, wu_ref[sl, :])
            g_ref[:, sl] = g.astype(BF16)
            u_ref[:, sl] = u.astype(BF16)
            a_ref[:, sl] = (g * jax.nn.sigmoid(g) * u).astype(BF16)
        y_ref[...] = xv + 0.5 * _dot_nn(a_ref[...], wd_ref[...])

    tm = TM if hosted else TM_WIDE
    return _call(
        body,
        name=name,
        grid=(t // tm,),
        in_specs=[_rows(tm, D_MODEL), _resident((1, D_MODEL)), _resident((f, D_MODEL)), _resident((f, D_MODEL)),
                  _resident((f, D_MODEL))],
        out_specs=[_rows(tm, D_MODEL), _rows(tm, f), _rows(tm, f), _rows(tm, f), _rows(tm, D_MODEL)],
        out_shape=[jax.ShapeDtypeStruct((t, D_MODEL), BF16), jax.ShapeDtypeStruct((t, f), BF16),
                   jax.ShapeDtypeStruct((t, f), BF16), jax.ShapeDtypeStruct((t, f), BF16),
                   jax.ShapeDtypeStruct((t, D_MODEL), F32)],
        args=(x, gamma, wg_t, wu_t, wd), sem=("parallel",), hosted=hosted)


def _ffn_up(x, gamma, wg_t, wu_t, name, hosted=()):
    t = x.shape[0]
    f = wg_t.shape[0]

    def body(x_ref, gam_ref, wg_ref, wu_ref, h_ref, g_ref, u_ref, a_ref):
        xh, _ = _rms(x_ref[...])
        h = (xh * gam_ref[...]).astype(BF16)
        h_ref[...] = h
        for j in range(f // FC):
            sl = slice(j * FC, (j + 1) * FC)
            g = _dot_nt(h, wg_ref[sl, :])
            u = _dot_nt(h, wu_ref[sl, :])
            g_ref[:, sl] = g.astype(BF16)
            u_ref[:, sl] = u.astype(BF16)
            a_ref[:, sl] = (g * jax.nn.sigmoid(g) * u).astype(BF16)

    return _call(
        body,
        name=name,
        grid=(t // TM,),
        in_specs=[_rows(TM, D_MODEL), _resident((1, D_MODEL)), _resident((f, D_MODEL)), _resident((f, D_MODEL))],
        out_specs=[_rows(TM, D_MODEL), _rows(TM, f), _rows(TM, f), _rows(TM, f)],
        out_shape=[jax.ShapeDtypeStruct((t, D_MODEL), BF16), jax.ShapeDtypeStruct((t, f), BF16),
                   jax.ShapeDtypeStruct((t, f), BF16), jax.ShapeDtypeStruct((t, f), BF16)],
        args=(x, gamma, wg_t, wu_t), sem=("parallel",), hosted=hosted)


def _ffn_down(x, a_act, wd, name, hosted=()):
    t = x.shape[0]
    f = wd.shape[0]

    def body(x_ref, a_ref, wd_ref, y_ref):
        y_ref[...] = x_ref[...] + 0.5 * _dot_nn(a_ref[...], wd_ref[...])

    return _call(
        body,
        name=name,
        grid=(t // TM,),
        in_specs=[_rows(TM, D_MODEL), _rows(TM, f), _resident((f, D_MODEL))],
        out_specs=[_rows(TM, D_MODEL)],
        out_shape=[jax.ShapeDtypeStruct((t, D_MODEL), F32)],
        args=(x, a_act, wd), sem=("parallel",), hosted=hosted)[0]


def _ffn_bwd_head(y, gamma_f, target, x, gamma, g_act, u_act, wg_t, wu_t, wd, name):
    t = x.shape[0]
    f = wg_t.shape[0]

    def body(y_ref, gamf_ref, t_ref, x_ref, gam_ref, g_ref, u_ref, wg_ref, wu_ref, wd_ref, dx_ref, dg_ref, du_ref,
             db_ref, dgam_ref, dgamf_ref, loss_ref):
        yh, ry = _rms(y_ref[...])
        gam_f = gamf_ref[...]
        e = yh * gam_f - t_ref[...]
        dv, dgam_f = _rms_bwd(e * (1.0 / D_MODEL), yh, ry, gam_f)
        db = (0.5 * dv).astype(BF16)
        db_ref[...] = db
        for j in range(f // FC):
            sl = slice(j * FC, (j + 1) * FC)
            da = _dot_nt(db, wd_ref[sl, :])
            g = g_ref[:, sl].astype(F32)
            u = u_ref[:, sl].astype(F32)
            s = jax.nn.sigmoid(g)
            dg_ref[:, sl] = (da * u * (s * (1.0 + g * (1.0 - s)))).astype(BF16)
            du_ref[:, sl] = (da * (g * s)).astype(BF16)
        dh = _dot_nn(dg_ref[...], wg_ref[...]) + _dot_nn(du_ref[...], wu_ref[...])
        xh, r = _rms(x_ref[...])
        dxn, dgam = _rms_bwd(dh, xh, r, gam_ref[...])
        dx_ref[...] = dv + dxn

        @pl.when(pl.program_id(0) == 0)
        def _():
            dgam_ref[...] = jnp.zeros_like(dgam_ref)
            dgamf_ref[...] = jnp.zeros_like(dgamf_ref)
            loss_ref[...] = jnp.zeros_like(loss_ref)

        dgam_ref[...] += dgam
        dgamf_ref[...] += dgam_f
        loss_ref[...] += _colsum8(e * e) * (0.5 / D_MODEL)

    acc = pl.BlockSpec((8, D_MODEL), lambda i: (0, 0))
    return _call(
        body,
        name=name,
        grid=(t // TM,),
        in_specs=[_rows(TM, D_MODEL), _resident((1, D_MODEL)), _rows(TM, D_MODEL), _rows(TM, D_MODEL),
                  _resident((1, D_MODEL)), _rows(TM, f), _rows(TM, f),
                  _resident((f, D_MODEL)), _resident((f, D_MODEL)), _resident((f, D_MODEL))],
        out_specs=[_rows(TM, D_MODEL), _rows(TM, f), _rows(TM, f), _rows(TM, D_MODEL), acc, acc, acc],
        out_shape=[jax.ShapeDtypeStruct((t, D_MODEL), F32), jax.ShapeDtypeStruct((t, f), BF16),
                   jax.ShapeDtypeStruct((t, f), BF16), jax.ShapeDtypeStruct((t, D_MODEL), BF16),
                   jax.ShapeDtypeStruct((8, D_MODEL), F32), jax.ShapeDtypeStruct((8, D_MODEL), F32),
                   jax.ShapeDtypeStruct((8, D_MODEL), F32)],
        args=(y, gamma_f, target, x, gamma, g_act, u_act, wg_t, wu_t, wd), sem=("arbitrary",))


def _ffn_bwd_act(d, g_act, u_act, wd, name, hosted=()):
    t = d.shape[0]
    f = wd.shape[0]

    def body(d_ref, g_ref, u_ref, wd_ref, dg_ref, du_ref):
        db = (0.5 * d_ref[...]).astype(BF16)
        for j in range(f // FC):
            sl = slice(j * FC, (j + 1) * FC)
            da = _dot_nt(db, wd_ref[sl, :])
            g = g_ref[:, sl].astype(F32)
            u = u_ref[:, sl].astype(F32)
            s = jax.nn.sigmoid(g)
            dg_ref[:, sl] = (da * u * (s * (1.0 + g * (1.0 - s)))).astype(BF16)
            du_ref[:, sl] = (da * (g * s)).astype(BF16)

    return _call(
        body,
        name=name,
        grid=(t // TM,),
        in_specs=[_rows(TM, D_MODEL), _rows(TM, f), _rows(TM, f), _resident((f, D_MODEL))],
        out_specs=[_rows(TM, f), _rows(TM, f)],
        out_shape=[jax.ShapeDtypeStruct((t, f), BF16), jax.ShapeDtypeStruct((t, f), BF16)],
        args=(d, g_act, u_act, wd), sem=("parallel",), hosted=hosted)


def _ffn_bwd_in(d, x, gamma, dg, du, wg_t, wu_t, name, hosted=()):
    t = x.shape[0]
    f = wg_t.shape[0]

    def body(d_ref, x_ref, gam_ref, dg_ref, du_ref, wg_ref, wu_ref, dx_ref, dgam_ref):
        dh = _dot_nn(dg_ref[...], wg_ref[...]) + _dot_nn(du_ref[...], wu_ref[...])
        xh, r = _rms(x_ref[...])
        dxn, dgam = _rms_bwd(dh, xh, r, gam_ref[...])
        dx_ref[...] = d_ref[...] + dxn

        @pl.when(pl.program_id(0) == 0)
        def _():
            dgam_ref[...] = jnp.zeros_like(dgam_ref)

        dgam_ref[...] += dgam

    return _call(
        body,
        name=name,
        grid=(t // TM,),
        in_specs=[_rows(TM, D_MODEL), _rows(TM, D_MODEL), _resident((1, D_MODEL)), _rows(TM, f), _rows(TM, f),
                  _resident((f, D_MODEL)), _resident((f, D_MODEL))],
        out_specs=[_rows(TM, D_MODEL), pl.BlockSpec((8, D_MODEL), lambda i: (0, 0))],
        out_shape=[jax.ShapeDtypeStruct((t, D_MODEL), F32), jax.ShapeDtypeStruct((8, D_MODEL), F32)],
        args=(d, x, gamma, dg, du, wg_t, wu_t), sem=("arbitrary",), hosted=hosted)


def _mm_tn(pieces, b, name, tile=256, hosted=()):
    t, n = b.shape
    npc = len(pieces)
    counts = [p.shape[1] // tile for p in pieces]
    los = [sum(counts[:k]) for k in range(npc)]
    total = sum(counts)

    def body(*refs):
        a_refs, b_ref, o_ref = refs[:npc], refs[npc], refs[npc + 1]
        i = pl.program_id(0)
        for k in range(npc):
            @pl.when(jnp.logical_and(i >= los[k], i < los[k] + counts[k]))
            def _(k=k):
                o_ref[...] = _dot_tn(a_refs[k][...], b_ref[...]).astype(BF16)

    def a_spec(k):
        return pl.BlockSpec((t, tile), lambda i: (0, jnp.clip(i - los[k], 0, counts[k] - 1)))

    return _call(
        body,
        name=name,
        grid=(total,),
        in_specs=[a_spec(k) for k in range(npc)] + [_resident((t, n))],
        out_specs=[pl.BlockSpec((tile, n), lambda i: (i, 0))],
        out_shape=[jax.ShapeDtypeStruct((total * tile, n), BF16)],
        args=(*pieces, b), sem=("parallel",), hosted=hosted)[0]


def _proj_fwd(x, gamma, win_t, hosted=()):
    t = x.shape[0]

    def body(x_ref, gam_ref, w_ref, h_ref, qa_ref, qb_ref, gt_ref):
        xh, _ = _rms(x_ref[...])
        h = (xh * gam_ref[...]).astype(BF16)
        h_ref[...] = h
        for j in range(QKV_A // FC):
            qa_ref[:, j * FC:(j + 1) * FC] = _dot_nt(h, w_ref[j * FC:(j + 1) * FC, :]).astype(BF16)
        for j in range(QKV_B // FC):
            lo = QKV_A + j * FC
            qb_ref[:, j * FC:(j + 1) * FC] = _dot_nt(h, w_ref[lo:lo + FC, :]).astype(BF16)
        for j in range(2 * D_MODEL // FC):
            lo = QKV_A + QKV_B + j * FC
            gt_ref[:, j * FC:(j + 1) * FC] = _dot_nt(h, w_ref[lo:lo + FC, :])

    return _call(
        body,
        name="proj_fwd",
        grid=(t // TM,),
        in_specs=[_rows(TM, D_MODEL), _resident((1, D_MODEL)), _resident((IN_WIDTH, D_MODEL))],
        out_specs=[_rows(TM, D_MODEL), _rows(TM, QKV_A), _rows(TM, QKV_B), _rows(TM, 2 * D_MODEL)],
        out_shape=[jax.ShapeDtypeStruct((t, D_MODEL), BF16), jax.ShapeDtypeStruct((t, QKV_A), BF16),
                   jax.ShapeDtypeStruct((t, QKV_B), BF16), jax.ShapeDtypeStruct((t, 2 * D_MODEL), F32)],
        args=(x, gamma, win_t), sem=("parallel",), hosted=hosted)


def _proj_bwd(d, x, gamma, pieces, win_t, hosted=()):
    t = x.shape[0]
    npc = len(pieces)
    widths = [p.shape[1] for p in pieces]
    los = [sum(widths[:k]) for k in range(npc)]

    def body(*refs):
        d_ref, x_ref, gam_ref = refs[:3]
        p_refs = refs[3:3 + npc]
        w_ref, dx_ref, db_ref, dgam_ref = refs[3 + npc:]
        dh = _dot_nn(p_refs[0][...], w_ref[0:widths[0], :])
        for k in range(1, npc):
            dh += _dot_nn(p_refs[k][...], w_ref[los[k]:los[k] + widths[k], :])
        xh, r = _rms(x_ref[...])
        dxn, dgam = _rms_bwd(dh, xh, r, gam_ref[...])
        dx = d_ref[...] + dxn
        dx_ref[...] = dx
        db_ref[...] = (0.5 * dx).astype(BF16)

        @pl.when(pl.program_id(0) == 0)
        def _():
            dgam_ref[...] = jnp.zeros_like(dgam_ref)

        dgam_ref[...] += dgam

    return _call(
        body,
        name="proj_bwd",
        grid=(t // TM,),
        in_specs=[_rows(TM, D_MODEL), _rows(TM, D_MODEL), _resident((1, D_MODEL))] + [_rows(TM, w) for w in widths]
        + [_resident((IN_WIDTH, D_MODEL))],
        out_specs=[_rows(TM, D_MODEL), _rows(TM, D_MODEL), pl.BlockSpec((8, D_MODEL), lambda i: (0, 0))],
        out_shape=[jax.ShapeDtypeStruct((t, D_MODEL), F32), jax.ShapeDtypeStruct((t, D_MODEL), BF16),
                   jax.ShapeDtypeStruct((8, D_MODEL), F32)],
        args=(d, x, gamma, *pieces, win_t), sem=("arbitrary",), hosted=hosted)


def _lane_half(shape):
    return lax.broadcasted_iota(jnp.int32, shape, len(shape) - 1) // D_HEAD


def _band_weights(q, kk, bias, sink, qs, pad):
    s = _dot_nt(q, kk) + bias
    if qs is not None:
        col = lax.broadcasted_iota(jnp.int32, s.shape, 1)
        s = jnp.where(col + qs >= pad, s, NEG_INF)
    m = jnp.max(s, axis=-1, keepdims=True)
    if sink is not None:
        m = jnp.maximum(m, sink)
    return jnp.exp(s - m), m


def _weighted_values(p, vv_ones, sink, m):
    r = _dot_nn(p.astype(BF16), vv_ones)
    den = r[:, LANES:2 * LANES]
    if sink is not None:
        den = den + jnp.exp(sink - m)
    return r[:, 0:LANES] / den


def _band_softmax(q, kk, bias, sink, qs, pad):
    p, m = _band_weights(q, kk, bias, sink, qs, pad)
    den = jnp.sum(p, axis=-1, keepdims=True)
    if sink is not None:
        den = den + jnp.exp(sink - m)
    return p, m, 1.0 / den


def _fill_padded(dst, src, pad):
    dst[0:pad, :] = jnp.zeros((pad,) + dst.shape[1:], dst.dtype)
    dst[pad:, :] = src


FWD_PAIRS = 4
BWD_PAIRS = 4


def _attn_a_fwd(qkv, bias, hosted=()):
    bsz, s_len, _ = qkv.shape
    pad = A_PREV * CHUNK
    band = TQ + pad
    pp = FWD_PAIRS
    w = pp * LANES
    nb = A_WIDTH // w

    def body(q_ref, k_ref, v_ref, b_ref, o_ref, kp, vp):
        i = pl.program_id(2)

        @pl.when(i == 0)
        def _():
            _fill_padded(kp, k_ref[...], pad)
            _fill_padded(vp, v_ref[...], pad)

        qs = pl.multiple_of(i * TQ, TQ)
        half = _lane_half((1, LANES))

        ones = jnp.ones((band, LANES), BF16)

        def block(masked):
            for pr in range(pp):
                sl = slice(pr * LANES, (pr + 1) * LANES)
                kk = kp[pl.ds(qs, band), sl]
                vv = jnp.concatenate([vp[pl.ds(qs, band), sl], ones], axis=1)
                q = q_ref[:, sl] * SCALE
                outs = []
                for j in range(2):
                    qm = jnp.where(half == j, q, jnp.zeros_like(q))
                    p, m = _band_weights(qm, kk, b_ref[2 * pr + j], None, qs if masked else None, pad)
                    outs.append(_weighted_values(p, vv, None, m))
                o_ref[:, sl] = jnp.where(half == 0, outs[0], outs[1]).astype(BF16)

        pl.when(i < pad // TQ)(lambda: block(True))
        pl.when(i >= pad // TQ)(lambda: block(False))

    return _call(
        body,
        name="attn_a_fwd",
        grid=(bsz, nb, s_len // TQ),
        in_specs=[pl.BlockSpec((None, TQ, w), lambda b, g, i: (b, i, g)),
                  pl.BlockSpec((None, s_len, w), lambda b, g, i: (b, 0, nb + g)),
                  pl.BlockSpec((None, s_len, w), lambda b, g, i: (b, 0, 2 * nb + g)),
                  pl.BlockSpec((2 * pp, TQ, band), lambda b, g, i: (g, 0, 0))],
        out_specs=[pl.BlockSpec((None, TQ, w), lambda b, g, i: (b, i, g))],
        out_shape=[jax.ShapeDtypeStruct((bsz, s_len, A_WIDTH), BF16)],
        scratch_shapes=[pltpu.VMEM((pad + s_len, w), BF16), pltpu.VMEM((pad + s_len, w), BF16)],
        args=(qkv, qkv, qkv, bias), sem=("arbitrary", "arbitrary", "arbitrary"), hosted=hosted)[0]


def _attn_a_bwd(qkv, bias, do, hosted=()):
    bsz, s_len, _ = qkv.shape
    pad = A_PREV * CHUNK
    band = TQ + pad
    n_i = s_len // TQ
    pp = BWD_PAIRS
    w = pp * LANES
    nb = A_WIDTH // w

    def body(q_ref, k_ref, v_ref, b_ref, do_ref, dq_ref, dk_ref, dv_ref, dbias_ref, kp, vp, dk_acc, dv_acc):
        b = pl.program_id(1)
        i = pl.program_id(2)

        @pl.when(i == 0)
        def _():
            _fill_padded(kp, k_ref[...], pad)
            _fill_padded(vp, v_ref[...], pad)
            dk_acc[...] = jnp.zeros_like(dk_acc)
            dv_acc[...] = jnp.zeros_like(dv_acc)

        @pl.when(jnp.logical_and(b == 0, i == 0))
        def _():
            dbias_ref[...] = jnp.zeros_like(dbias_ref)

        qs = pl.multiple_of(i * TQ, TQ)
        half = _lane_half((1, LANES))

        def block(masked):
            for pr in range(pp):
                sl = slice(pr * LANES, (pr + 1) * LANES)
                kk = kp[pl.ds(qs, band), sl]
                vv = vp[pl.ds(qs, band), sl]
                q = q_ref[:, sl] * SCALE
                dd = do_ref[:, sl]
                dqs, dks, dvs = [], [], []
                for j in range(2):
                    qm = jnp.where(half == j, q, jnp.zeros_like(q))
                    dm = jnp.where(half == j, dd, jnp.zeros_like(dd))
                    p, _, inv = _band_softmax(qm, kk, b_ref[2 * pr + j], None, qs if masked else None, pad)
                    pn = p * inv
                    dp = _dot_nt(dm, vv)
                    delta = jnp.sum(pn * dp, axis=-1, keepdims=True)
                    ds = pn * (dp - delta)
                    dbias_ref[2 * pr + j] += ds[:, band - REL_COLS:]
                    dsb = ds.astype(BF16)
                    dqs.append(_dot_nn(dsb, kk))
                    dks.append(_dot_tn(dsb, q))
                    dvs.append(_dot_tn(pn.astype(BF16), dd))
                dq_ref[:, sl] = (jnp.where(half == 0, dqs[0], dqs[1]) * SCALE).astype(BF16)
                dk_acc[pl.ds(qs, band), sl] += jnp.where(half == 0, dks[0], dks[1])
                dv_acc[pl.ds(qs, band), sl] += jnp.where(half == 0, dvs[0], dvs[1])

        pl.when(i < pad // TQ)(lambda: block(True))
        pl.when(i >= pad // TQ)(lambda: block(False))

        @pl.when(i == n_i - 1)
        def _():
            dk_ref[...] = dk_acc[pad:, :].astype(BF16)
            dv_ref[...] = dv_acc[pad:, :].astype(BF16)

    qspec = pl.BlockSpec((None, TQ, w), lambda g, b, i: (b, i, g))
    kvout = pl.BlockSpec((None, s_len, w), lambda g, b, i: (b, 0, g))
    wide = jax.ShapeDtypeStruct((bsz, s_len, A_WIDTH), BF16)
    return _call(
        body,
        name="attn_a_bwd",
        grid=(nb, bsz, n_i),
        in_specs=[qspec,
                  pl.BlockSpec((None, s_len, w), lambda g, b, i: (b, 0, nb + g)),
                  pl.BlockSpec((None, s_len, w), lambda g, b, i: (b, 0, 2 * nb + g)),
                  pl.BlockSpec((2 * pp, TQ, band), lambda g, b, i: (g, 0, 0)),
                  qspec],
        out_specs=[qspec, kvout, kvout, pl.BlockSpec((2 * pp, TQ, REL_COLS), lambda g, b, i: (g, 0, 0))],
        out_shape=[wide, wide, wide, jax.ShapeDtypeStruct((A_HEADS, TQ, REL_COLS), F32)],
        scratch_shapes=[pltpu.VMEM((pad + s_len, w), BF16), pltpu.VMEM((pad + s_len, w), BF16),
                        pltpu.VMEM((pad + s_len, w), F32), pltpu.VMEM((pad + s_len, w), F32)],
        args=(qkv, qkv, qkv, bias, do), sem=("arbitrary", "arbitrary", "arbitrary"), hosted=hosted)


def _fill_padded_dup(dst, src, pad, h, half):
    other = pltpu.roll(src, D_HEAD, 1)
    _fill_padded(dst, jnp.where(half == h, src, other), pad)


def _attn_b_fwd(qkv, bias, sink):
    bsz, s_len, _ = qkv.shape
    pad = B_PREV * CHUNK
    band = TQ + pad
    kcol = B_Q_WIDTH // LANES
    npair = B_Q_HEADS // 2

    def body(q_ref, k_ref, v_ref, b_ref, s_ref, o_ref, kp, vp):
        i = pl.program_id(1)
        half = _lane_half((1, LANES))

        @pl.when(i == 0)
        def _():
            for h in range(B_KV_HEADS):
                _fill_padded_dup(kp.at[h], k_ref[...], pad, h, half)
                _fill_padded_dup(vp.at[h], v_ref[...], pad, h, half)

        qs = pl.multiple_of(i * TQ, TQ)

        ones = jnp.ones((band, LANES), BF16)

        def block(masked):
            for pr in range(npair):
                h = pr // (B_GROUP // 2)
                sl = slice(pr * LANES, (pr + 1) * LANES)
                kk = kp[h, pl.ds(qs, band), :]
                vv = jnp.concatenate([vp[h, pl.ds(qs, band), :], ones], axis=1)
                q = q_ref[:, sl] * SCALE
                outs = []
                for j in range(2):
                    qm = jnp.where(half == j, q, jnp.zeros_like(q))
                    sink = s_ref[2 * pr + j][0:1, 0:1]
                    p, m = _band_weights(qm, kk, b_ref[2 * pr + j], sink, qs if masked else None, pad)
                    outs.append(_weighted_values(p, vv, sink, m))
                o_ref[:, sl] = jnp.where(half == 0, outs[0], outs[1]).astype(BF16)

        pl.when(i < -(-pad // TQ))(lambda: block(True))
        pl.when(i >= -(-pad // TQ))(lambda: block(False))

    return pl.pallas_call(
        body,
        name="attn_b_fwd",
        grid=(bsz, s_len // TQ),
        in_specs=[pl.BlockSpec((None, TQ, B_Q_WIDTH), lambda b, i: (b, i, 0)),
                  pl.BlockSpec((None, s_len, LANES), lambda b, i: (b, 0, kcol)),
                  pl.BlockSpec((None, s_len, LANES), lambda b, i: (b, 0, kcol + 1)),
                  pl.BlockSpec((B_Q_HEADS, TQ, band), lambda b, i: (0, 0, 0)),
                  pl.BlockSpec((B_Q_HEADS, 8, LANES), lambda b, i: (0, 0, 0))],
        out_specs=pl.BlockSpec((None, TQ, B_Q_WIDTH), lambda b, i: (b, i, 0)),
        out_shape=jax.ShapeDtypeStruct((bsz, s_len, B_Q_WIDTH), BF16),
        scratch_shapes=[pltpu.VMEM((B_KV_HEADS, pad + s_len, LANES), BF16),
                        pltpu.VMEM((B_KV_HEADS, pad + s_len, LANES), BF16)],
        compiler_params=_cparams(("arbitrary", "arbitrary")),
    )(qkv, qkv, qkv, bias, sink)


def _attn_b_bwd(qkv, bias, sink, do, hosted=()):
    bsz, s_len, _ = qkv.shape
    pad = B_PREV * CHUNK
    band = TQ + pad
    kcol = B_Q_WIDTH // LANES
    n_i = s_len // TQ
    pp = B_GROUP // 2

    def body(q_ref, k_ref, v_ref, b_ref, s_ref, do_ref, dq_ref, dkv_ref, dsink_ref, kp, vp, dk_acc, dv_acc):
        b = pl.program_id(0)
        i = pl.program_id(1)
        half = _lane_half((1, LANES))

        @pl.when(i == 0)
        def _():
            for h in range(B_KV_HEADS):
                _fill_padded_dup(kp.at[h], k_ref[...], pad, h, half)
                _fill_padded_dup(vp.at[h], v_ref[...], pad, h, half)
            dk_acc[...] = jnp.zeros_like(dk_acc)
            dv_acc[...] = jnp.zeros_like(dv_acc)

        @pl.when(jnp.logical_and(b == 0, i == 0))
        def _():
            dsink_ref[...] = jnp.zeros_like(dsink_ref)

        qs = pl.multiple_of(i * TQ, TQ)

        def block(masked):
            heads_dk, heads_dv = [], []
            for h in range(B_KV_HEADS):
                kk = kp[h, pl.ds(qs, band), :]
                vv = vp[h, pl.ds(qs, band), :]
                dk2 = jnp.zeros((band, LANES), F32)
                dv2 = jnp.zeros((band, LANES), F32)
                for pr in range(pp * h, pp * (h + 1)):
                    sl = slice(pr * LANES, (pr + 1) * LANES)
                    q = q_ref[:, sl] * SCALE
                    dd = do_ref[:, sl]
                    dqs, dks, dvs = [], [], []
                    for j in range(2):
                        qm = jnp.where(half == j, q, jnp.zeros_like(q))
                        dm = jnp.where(half == j, dd, jnp.zeros_like(dd))
                        sink = s_ref[2 * pr + j][0:1, 0:1]
                        p, m, inv = _band_softmax(qm, kk, b_ref[2 * pr + j], sink, qs if masked else None, pad)
                        pn = p * inv
                        dp = _dot_nt(dm, vv)
                        delta = jnp.sum(pn * dp, axis=-1, keepdims=True)
                        ds = pn * (dp - delta)
                        dsb = ds.astype(BF16)
                        dqs.append(_dot_nn(dsb, kk))
                        dks.append(_dot_tn(dsb, q))
                        dvs.append(_dot_tn(pn.astype(BF16), dd))
                        dsk = jnp.sum(-(jnp.exp(sink - m) * inv) * delta, axis=0, keepdims=True)
                        dsink_ref[2 * pr + j] += jnp.broadcast_to(dsk, (8, LANES))
                    dq_ref[:, sl] = (jnp.where(half == 0, dqs[0], dqs[1]) * SCALE).astype(BF16)
                    dk2 = dk2 + jnp.where(half == 0, dks[0], dks[1])
                    dv2 = dv2 + jnp.where(half == 0, dvs[0], dvs[1])
                heads_dk.append(dk2 + pltpu.roll(dk2, D_HEAD, 1))
                heads_dv.append(dv2 + pltpu.roll(dv2, D_HEAD, 1))
            dk_acc[pl.ds(qs, band), :] += jnp.where(half == 0, heads_dk[0], heads_dk[1])
            dv_acc[pl.ds(qs, band), :] += jnp.where(half == 0, heads_dv[0], heads_dv[1])

        pl.when(i < -(-pad // TQ))(lambda: block(True))
        pl.when(i >= -(-pad // TQ))(lambda: block(False))

        @pl.when(i == n_i - 1)
        def _():
            dkv_ref[:, 0:LANES] = dk_acc[pad:, :].astype(BF16)
            dkv_ref[:, LANES:2 * LANES] = dv_acc[pad:, :].astype(BF16)

    qspec = pl.BlockSpec((None, TQ, B_Q_WIDTH), lambda b, i: (b, i, 0))
    return _call(
        body,
        name="attn_b_bwd",
        grid=(bsz, n_i),
        in_specs=[qspec,
                  pl.BlockSpec((None, s_len, LANES), lambda b, i: (b, 0, kcol)),
                  pl.BlockSpec((None, s_len, LANES), lambda b, i: (b, 0, kcol + 1)),
                  pl.BlockSpec((B_Q_HEADS, TQ, band), lambda b, i: (0, 0, 0)),
                  pl.BlockSpec((B_Q_HEADS, 8, LANES), lambda b, i: (0, 0, 0)),
                  qspec],
        out_specs=[qspec, pl.BlockSpec((None, s_len, 2 * LANES), lambda b, i: (b, 0, 0)),
                   pl.BlockSpec((B_Q_HEADS, 8, LANES), lambda b, i: (0, 0, 0))],
        out_shape=[jax.ShapeDtypeStruct((bsz, s_len, B_Q_WIDTH), BF16),
                   jax.ShapeDtypeStruct((bsz, s_len, 2 * B_KV_WIDTH), BF16),
                   jax.ShapeDtypeStruct((B_Q_HEADS, 8, LANES), F32)],
        scratch_shapes=[pltpu.VMEM((B_KV_HEADS, pad + s_len, LANES), BF16),
                        pltpu.VMEM((B_KV_HEADS, pad + s_len, LANES), BF16),
                        pltpu.VMEM((pad + s_len, LANES), F32), pltpu.VMEM((pad + s_len, LANES), F32)],
        args=(qkv, qkv, qkv, bias, sink, do), sem=("arbitrary", "arbitrary"), hosted=hosted)


REL_COLS = 3 * 128
REL_WRAP = 512


def _bias_a_build(tv, hosted=()):
    h = tv.shape[0]
    pad = A_PREV * CHUNK
    band = TQ + pad

    def body(tv_ref, o_ref):
        row = tv_ref[...]
        x = jnp.broadcast_to(row, (TQ, REL_WRAP))
        r = lax.broadcasted_iota(jnp.int32, x.shape, 0)
        for bit in range(8):
            sh = 1 << bit
            x = jnp.where((r & sh) != 0, pltpu.roll(x, sh, 1), x)
        far = jnp.broadcast_to(row[:, 0:1], (TQ, band - REL_COLS))
        full = jnp.concatenate([far, x[:, REL_WRAP // 2:REL_WRAP], x[:, 0:REL_COLS - REL_WRAP // 2]], axis=1)
        qc = (lax.broadcasted_iota(jnp.int32, full.shape, 0) + pad) // CHUNK
        kc = lax.broadcasted_iota(jnp.int32, full.shape, 1) // CHUNK
        ok = jnp.logical_and(kc <= qc, kc >= qc - A_PREV)
        o_ref[...] = jnp.where(ok, full, NEG_INF)

    return _call(
        body,
        name="bias_a_build",
        grid=(h,),
        in_specs=[pl.BlockSpec((None, 1, REL_WRAP), lambda hh: (hh, 0, 0))],
        out_specs=[pl.BlockSpec((None, TQ, band), lambda hh: (hh, 0, 0))],
        out_shape=[jax.ShapeDtypeStruct((h, TQ, band), F32)],
        args=(tv,), sem=("parallel",), hosted=hosted)[0]


def _relbias_grad(dbias):
    h, rows, _ = dbias.shape

    def body(d_ref, o_ref):
        x = d_ref[...]
        r = lax.broadcasted_iota(jnp.int32, x.shape, 0)
        c = lax.broadcasted_iota(jnp.int32, x.shape, 1) - r
        x = jnp.where(jnp.logical_and(c >= 1, c < REL_TABLE), x, 0.0)
        for bit in range(8):
            sh = 1 << bit
            x = jnp.where((r & sh) != 0, pltpu.roll(x, REL_COLS - sh, 1), x)
        diag = jnp.sum(x, axis=0, keepdims=True)
        lane = lax.broadcasted_iota(jnp.int32, diag.shape, 1)
        diag = jnp.where(jnp.logical_and(lane >= 1, lane < REL_TABLE), diag, 0.0)
        rest = -jnp.sum(diag, axis=1, keepdims=True)
        o_ref[...] = jnp.broadcast_to(jnp.where(lane == 0, rest, diag), o_ref.shape)

    return pl.pallas_call(
        body,
        name="relbias_grad",
        grid=(h,),
        in_specs=[pl.BlockSpec((None, rows, REL_COLS), lambda hh: (hh, 0, 0))],
        out_specs=pl.BlockSpec((None, 8, REL_COLS), lambda hh: (hh, 0, 0)),
        out_shape=jax.ShapeDtypeStruct((h, 8, REL_COLS), F32),
        compiler_params=_cparams(("parallel",)),
    )(dbias)


def _mix_out_fwd(x, oa, ob, gates, proj_t, wout):
    t = x.shape[0]

    def body(x_ref, oa_ref, ob_ref, gt_ref, pt_ref, wo_ref, y_ref, ya_ref, yb_ref, mg_ref):
        ya = _dot_nt(oa_ref[...], pt_ref[:, 0:A_WIDTH])
        yb = _dot_nt(ob_ref[...], pt_ref[:, A_WIDTH:A_WIDTH + B_Q_WIDTH])
        ya_ref[...] = ya.astype(BF16)
        yb_ref[...] = yb.astype(BF16)
        mg = jax.nn.sigmoid(gt_ref[:, 0:D_MODEL]) * ya + jax.nn.sigmoid(gt_ref[:, D_MODEL:2 * D_MODEL]) * yb
        mgb = mg.astype(BF16)
        mg_ref[...] = mgb
        y_ref[...] = x_ref[...] + _dot_nn(mgb, wo_ref[...])

    return pl.pallas_call(
        body,
        name="mix_out_fwd",
        grid=(t // TM,),
        in_specs=[_rows(TM, D_MODEL), _rows(TM, A_WIDTH), _rows(TM, B_Q_WIDTH), _rows(TM, 2 * D_MODEL),
                  _resident((D_MODEL, A_WIDTH + B_Q_WIDTH)), _resident((D_MODEL, D_MODEL))],
        out_specs=[_rows(TM, D_MODEL), _rows(TM, D_MODEL), _rows(TM, D_MODEL), _rows(TM, D_MODEL)],
        out_shape=[jax.ShapeDtypeStruct((t, D_MODEL), F32), jax.ShapeDtypeStruct((t, D_MODEL), BF16),
                   jax.ShapeDtypeStruct((t, D_MODEL), BF16), jax.ShapeDtypeStruct((t, D_MODEL), BF16)],
        compiler_params=_cparams(("parallel",)),
    )(x, oa, ob, gates, proj_t, wout)


def _mix_out_bwd(d, gates, ya, yb, mg, oa, ob, proj_t, wout, hosted=()):
    t = d.shape[0]
    nt = t // TM

    def body(d_ref, gt_ref, ya_ref, yb_ref, mg_ref, oa_ref, ob_ref, pt_ref, wo_ref,
             doa_ref, dob_ref, dgt_ref, gwo_ref, gwp_ref, acc_o, acc_p):
        i = pl.program_id(0)
        db = d_ref[...].astype(BF16)
        dmg = _dot_nt(db, wo_ref[...])
        sa = jax.nn.sigmoid(gt_ref[:, 0:D_MODEL])
        sb = jax.nn.sigmoid(gt_ref[:, D_MODEL:2 * D_MODEL])
        dya = (dmg * sa).astype(BF16)
        dyb = (dmg * sb).astype(BF16)
        dgt_ref[:, 0:D_MODEL] = (dmg * ya_ref[...].astype(F32) * (sa * (1.0 - sa))).astype(BF16)
        dgt_ref[:, D_MODEL:2 * D_MODEL] = (dmg * yb_ref[...].astype(F32) * (sb * (1.0 - sb))).astype(BF16)
        doa_ref[...] = _dot_nn(dya, pt_ref[:, 0:A_WIDTH]).astype(BF16)
        dob_ref[...] = _dot_nn(dyb, pt_ref[:, A_WIDTH:A_WIDTH + B_Q_WIDTH]).astype(BF16)

        @pl.when(i == 0)
        def _():
            acc_o[...] = jnp.zeros_like(acc_o)
            acc_p[...] = jnp.zeros_like(acc_p)

        acc_o[...] += _dot_tn(mg_ref[...], db)
        acc_p[:, 0:A_WIDTH] += _dot_tn(dya, oa_ref[...])
        acc_p[:, A_WIDTH:A_WIDTH + B_Q_WIDTH] += _dot_tn(dyb, ob_ref[...])

        @pl.when(i == nt - 1)
        def _():
            gwo_ref[...] = acc_o[...].astype(BF16)
            gwp_ref[...] = acc_p[...].astype(BF16)

    whole = pl.BlockSpec((D_MODEL, D_MODEL), lambda i: (0, 0))
    return _call(
        body,
        name="mix_out_bwd",
        grid=(nt,),
        in_specs=[_rows(TM, D_MODEL), _rows(TM, 2 * D_MODEL), _rows(TM, D_MODEL), _rows(TM, D_MODEL),
                  _rows(TM, D_MODEL), _rows(TM, A_WIDTH), _rows(TM, B_Q_WIDTH),
                  _resident((D_MODEL, A_WIDTH + B_Q_WIDTH)), _resident((D_MODEL, D_MODEL))],
        out_specs=[_rows(TM, A_WIDTH), _rows(TM, B_Q_WIDTH), _rows(TM, 2 * D_MODEL), whole, whole],
        out_shape=[jax.ShapeDtypeStruct((t, A_WIDTH), BF16), jax.ShapeDtypeStruct((t, B_Q_WIDTH), BF16),
                   jax.ShapeDtypeStruct((t, 2 * D_MODEL), BF16), jax.ShapeDtypeStruct((D_MODEL, D_MODEL), BF16),
                   jax.ShapeDtypeStruct((D_MODEL, D_MODEL), BF16)],
        scratch_shapes=[pltpu.VMEM((D_MODEL, D_MODEL), F32), pltpu.VMEM((D_MODEL, A_WIDTH + B_Q_WIDTH), F32)],
        args=(d, gates, ya, yb, mg, oa, ob, proj_t, wout), sem=("arbitrary",), hosted=hosted)


def _place():
    x, y, c = lax.axis_index("x"), lax.axis_index("y"), lax.axis_index("c")
    chips = [(1 - x, y), (x, 1 - y), (1 - x, 1 - y)]
    return x, y, c, chips


class _Gather:
    per = 8

    def __init__(self, shards):
        n = len(shards)
        self.inputs = list(shards)
        self.out_shape = [jax.ShapeDtypeStruct((N_DEV * s.shape[0], s.shape[1]), s.dtype) for s in shards]
        self.scratch = [pltpu.SemaphoreType.DMA((n * self.per,)), pltpu.SemaphoreType.DMA((n * self.per,)),
                        pltpu.SemaphoreType.DMA((n,))]
        self.result = None

    def _parts(self, ins, outs, sems):
        send_sems, recv_sems, local_sems = sems
        x, y, c, chips = _place()
        me, sibling = (x, y, c), (x, y, 1 - c)
        xn, yn, dg = chips
        n = len(ins)

        def rows(k, p, part=None):
            r = ins[k].shape[0]
            base = (4 * p[0] + 2 * p[1] + p[2]) * r
            if part is None:
                return outs[k].at[pl.ds(base, r), :]
            return outs[k].at[pl.ds(base + part * (r // 2), r // 2), :]

        def copy(k, slot, block, to, src=None, part=None):
            return pltpu.make_async_remote_copy(
                src_ref=rows(k, block, part) if src is None else src, dst_ref=rows(k, block, part),
                send_sem=send_sems.at[k * self.per + slot], recv_sem=recv_sems.at[k * self.per + slot],
                device_id=to, device_id_type=MESH)

        mine = [pltpu.make_async_copy(ins[k], rows(k, me), local_sems.at[k]) for k in range(n)]
        sends, lands = [], []
        for k in range(n):
            sends.append({
                0: copy(k, 0, me, sibling, src=ins[k]),
                1: copy(k, 1, me, (*xn, c), src=ins[k]),
                2: copy(k, 2, me, (*yn, c), src=ins[k]),
                3: copy(k, 3, (*xn, c), (*yn, c), part=0),
                4: copy(k, 4, (*yn, c), (*xn, c), part=1),
                5: copy(k, 5, (*xn, c), sibling),
                6: copy(k, 6, (*yn, c), sibling),
                7: copy(k, 7, (*dg, c), sibling)})
            lands.append({
                0: copy(k, 0, sibling, me),
                1: copy(k, 1, (*xn, c), me),
                2: copy(k, 2, (*yn, c), me),
                3: copy(k, 3, (*dg, c), me, part=0),
                4: copy(k, 4, (*dg, c), me, part=1),
                5: copy(k, 5, (*xn, 1 - c), me),
                6: copy(k, 6, (*yn, 1 - c), me),
                7: copy(k, 7, (*dg, 1 - c), me)})
        return n, mine, sends, lands

    def start(self, ins, outs, sems):
        n, mine, sends, _ = self._parts(ins, outs, sems)
        for cp in mine:
            cp.start()
        for slot in (0, 1, 2):
            for k in range(n):
                sends[k][slot].start()

    def relay(self, ins, outs, sems):
        n, _, sends, lands = self._parts(ins, outs, sems)
        for k in range(n):
            lands[k][1].wait_recv()
            sends[k][3].start()
            sends[k][5].start()
        for k in range(n):
            lands[k][2].wait_recv()
            sends[k][4].start()
            sends[k][6].start()

    def forward(self, ins, outs, sems):
        n, _, sends, lands = self._parts(ins, outs, sems)
        for k in range(n):
            lands[k][3].wait_recv()
            lands[k][4].wait_recv()
            sends[k][7].start()

    def finish(self, ins, outs, sems):
        n, mine, sends, lands = self._parts(ins, outs, sems)
        for k in range(n):
            for slot in (0, 5, 6, 7):
                lands[k][slot].wait_recv()
        for k in range(n):
            for slot in range(self.per):
                sends[k][slot].wait_send()
        for cp in mine:
            cp.wait()


class _PairExchange:
    def __init__(self, grads):
        n = len(grads)
        self.inputs = list(grads)
        self.out_shape = [jax.ShapeDtypeStruct((g.shape[0] // 2, g.shape[1]), g.dtype) for g in grads]
        self.scratch = [pltpu.SemaphoreType.DMA((n * N_CHIP,)), pltpu.SemaphoreType.DMA((n * N_CHIP,))]
        self.result = None

    def _copies(self, ins, outs, sems):
        send_sems, recv_sems = sems
        x, y, c, _ = _place()
        copies = []
        for k in range(len(ins)):
            r = ins[k].shape[0] // N_DEV
            for q in range(N_CHIP):
                copies.append(pltpu.make_async_remote_copy(
                    src_ref=ins[k].at[pl.ds((2 * q + 1 - c) * r, r), :], dst_ref=outs[k].at[pl.ds(q * r, r), :],
                    send_sem=send_sems.at[k * N_CHIP + q], recv_sem=recv_sems.at[k * N_CHIP + q],
                    device_id=(x, y, 1 - c), device_id_type=MESH))
        return copies

    def start(self, ins, outs, sems):
        for cp in self._copies(ins, outs, sems):
            cp.start()

    def relay(self, ins, outs, sems):
        pass

    def forward(self, ins, outs, sems):
        pass

    def finish(self, ins, outs, sems):
        copies = self._copies(ins, outs, sems)
        for cp in copies:
            cp.wait_recv()
        for cp in copies:
            cp.wait_send()


class _ChipExchange(_PairExchange):
    def __init__(self, psums):
        n = len(psums)
        self.inputs = list(psums)
        self.out_shape = [jax.ShapeDtypeStruct((3 * p.shape[0] // N_CHIP, p.shape[1]), p.dtype) for p in psums]
        self.scratch = [pltpu.SemaphoreType.DMA((n * 3,)), pltpu.SemaphoreType.DMA((n * 3,))]
        self.result = None

    def _copies(self, ins, outs, sems):
        send_sems, recv_sems = sems
        _, _, c, chips = _place()
        copies = []
        for k in range(len(ins)):
            r = ins[k].shape[0] // N_CHIP
            for j, chip in enumerate(chips):
                copies.append(pltpu.make_async_remote_copy(
                    src_ref=ins[k].at[pl.ds((2 * chip[0] + chip[1]) * r, r), :], dst_ref=outs[k].at[pl.ds(j * r, r), :],
                    send_sem=send_sems.at[k * 3 + j], recv_sem=recv_sems.at[k * 3 + j],
                    device_id=(*chip, c), device_id_type=MESH))
        return copies


def _exchange_alone(xchg, name):
    n_in, n_out = len(xchg.inputs), len(xchg.out_shape)

    def body(*refs):
        ins, outs, sems = refs[:n_in], refs[n_in:n_in + n_out], refs[n_in + n_out:]
        xchg.start(ins, outs, sems)
        xchg.relay(ins, outs, sems)
        xchg.forward(ins, outs, sems)
        xchg.finish(ins, outs, sems)

    xchg.result = list(pl.pallas_call(
        body, name=name, in_specs=[_hbm()] * n_in, out_specs=[_hbm()] * n_out, out_shape=xchg.out_shape,
        scratch_shapes=xchg.scratch)(*xchg.inputs))
    return xchg.result


def _pair_sum(core, grads, recvd, name):
    n = len(grads)
    r = grads[0].shape[0] // N_DEV
    cdim = grads[0].shape[1]
    tr = r // 2 if r % 32 == 0 else r
    nt = r // tr

    def body(core_ref, *refs):
        del core_ref
        for k in range(n):
            refs[2 * n + k][...] = (refs[k][...].astype(F32) + refs[n + k][...].astype(F32)).astype(BF16)

    gspec = pl.BlockSpec((tr, cdim), lambda q, i, core_ref: ((2 * q + core_ref[0]) * nt + i, 0))
    rspec = pl.BlockSpec((tr, cdim), lambda q, i, core_ref: (q * nt + i, 0))
    return pl.pallas_call(
        body,
        name=name,
        grid_spec=pltpu.PrefetchScalarGridSpec(
            num_scalar_prefetch=1, grid=(N_CHIP, nt), in_specs=[gspec] * n + [rspec] * n, out_specs=[rspec] * n),
        out_shape=[jax.ShapeDtypeStruct((N_CHIP * r, cdim), BF16) for _ in range(n)],
        compiler_params=_cparams(("parallel", "parallel")),
    )(core, *grads, *recvd)


def _final_sum(chip, psums, recvd, name):
    n = len(psums)
    r = psums[0].shape[0] // N_CHIP
    cdim = psums[0].shape[1]
    tr = r // 2 if r % 32 == 0 else r
    nt = r // tr

    def body(chip_ref, *refs):
        del chip_ref
        for k in range(n):
            got = refs[n + k]
            tot = refs[k][...].astype(F32) + got[0].astype(F32)
            tot = tot + got[1].astype(F32)
            tot = tot + got[2].astype(F32)
            refs[2 * n + k][...] = tot

    pspec = pl.BlockSpec((tr, cdim), lambda i, chip_ref: (chip_ref[0] * nt + i, 0))
    rspec = pl.BlockSpec((3, tr, cdim), lambda i, chip_ref: (0, i, 0))
    ospec = pl.BlockSpec((tr, cdim), lambda i, chip_ref: (i, 0))
    return pl.pallas_call(
        body,
        name=name,
        grid_spec=pltpu.PrefetchScalarGridSpec(
            num_scalar_prefetch=1, grid=(nt,), in_specs=[pspec] * n + [rspec] * n, out_specs=[ospec] * n),
        out_shape=[jax.ShapeDtypeStruct((r, cdim), F32) for _ in range(n)],
        compiler_params=_cparams(("parallel",)),
    )(chip, *psums, *[g.reshape(3, r, cdim) for g in recvd])


SMALL_ROWS = 16


def _all_reduce_small(part):
    def body(p_ref, o_ref, buf, send_sems, recv_sems):
        x, y, c, _ = _place()
        me = 4 * x + 2 * y + c
        buf[me] = p_ref[...]
        copies = []
        for d in range(1, N_DEV):
            peer = me ^ d
            copies.append(pltpu.make_async_remote_copy(
                src_ref=p_ref, dst_ref=buf.at[me], send_sem=send_sems.at[d - 1], recv_sem=recv_sems.at[d - 1],
                device_id=(peer // 4, (peer // 2) % 2, peer % 2), device_id_type=MESH))
        for cp in copies:
            cp.start()
        for cp in copies:
            cp.wait_recv()
        for cp in copies:
            cp.wait_send()
        tot = buf[0]
        for d in range(1, N_DEV):
            tot = tot + buf[d]
        o_ref[...] = tot

    return pl.pallas_call(
        body,
        name="all_reduce_small",
        in_specs=[pl.BlockSpec(memory_space=pltpu.VMEM)],
        out_specs=pl.BlockSpec(memory_space=pltpu.VMEM),
        out_shape=jax.ShapeDtypeStruct(part.shape, F32),
        scratch_shapes=[pltpu.VMEM((N_DEV,) + part.shape, F32), pltpu.SemaphoreType.DMA((N_DEV - 1,)),
                        pltpu.SemaphoreType.DMA((N_DEV - 1,))],
    )(part)


ADAMW_STEPS = 4


def _adamw(ws, gs, ms, vs, name, hosted=()):
    n = len(ws)
    steps = ADAMW_STEPS if all(w.shape[0] % (8 * ADAMW_STEPS) == 0 for w in ws) else 1
    c1 = 1.0 - ADAM_B1 ** ADAM_STEP
    c2 = 1.0 - ADAM_B2 ** ADAM_STEP

    def body(*refs):
        for k in range(n):
            w, g, m, v = (refs[j * n + k][...] for j in range(4))
            m2 = ADAM_B1 * m + (1.0 - ADAM_B1) * g
            v2 = ADAM_B2 * v + (1.0 - ADAM_B2) * (g * g)
            delta = -ADAM_LR * ((m2 / c1) / (jnp.sqrt(v2 / c2) + ADAM_EPS) + ADAM_WD * w)
            refs[4 * n + k][...] = delta
            refs[5 * n + k][...] = m2
            refs[6 * n + k][...] = v2

    specs = [pl.BlockSpec((w.shape[0] // steps, w.shape[1]), lambda i: (i, 0)) for w in ws]
    shapes = [jax.ShapeDtypeStruct(w.shape, F32) for w in ws]
    outs = _call(
        body,
        name=name,
        grid=(steps,),
        in_specs=specs * 4,
        out_specs=specs * 3,
        out_shape=shapes * 3,
        args=(*ws, *gs, *ms, *vs), sem=("parallel",), hosted=hosted)
    return outs[:n], outs[n:2 * n], outs[2 * n:]


def _adamw_reduced(chip, ws, psums, recvd, ms, vs, steps, name):
    n = len(ws)
    c1 = 1.0 - ADAM_B1 ** ADAM_STEP
    c2 = 1.0 - ADAM_B2 ** ADAM_STEP

    def body(chip_ref, *refs):
        del chip_ref
        for k in range(n):
            w, m, v = (refs[j * n + k][...] for j in (0, 3, 4))
            got = refs[2 * n + k]
            g = refs[n + k][...].astype(F32) + got[0].astype(F32)
            g = g + got[1].astype(F32)
            g = g + got[2].astype(F32)
            m2 = ADAM_B1 * m + (1.0 - ADAM_B1) * g
            v2 = ADAM_B2 * v + (1.0 - ADAM_B2) * (g * g)
            refs[5 * n + k][...] = g
            refs[6 * n + k][...] = -ADAM_LR * ((m2 / c1) / (jnp.sqrt(v2 / c2) + ADAM_EPS) + ADAM_WD * w)
            refs[7 * n + k][...] = m2
            refs[8 * n + k][...] = v2

    def blk(w):
        return (w.shape[0] // steps, w.shape[1])

    own = [pl.BlockSpec(blk(w), lambda i, chip_ref: (i, 0)) for w in ws]
    psum = [pl.BlockSpec(blk(w), lambda i, chip_ref: (chip_ref[0] * steps + i, 0)) for w in ws]
    recv = [pl.BlockSpec((3,) + blk(w), lambda i, chip_ref: (0, i, 0)) for w in ws]
    shapes = [jax.ShapeDtypeStruct(w.shape, F32) for w in ws]
    outs = pl.pallas_call(
        body,
        name=name,
        grid_spec=pltpu.PrefetchScalarGridSpec(
            num_scalar_prefetch=1, grid=(steps,), in_specs=own + psum + recv + own + own, out_specs=own * 4),
        out_shape=shapes * 4,
        compiler_params=_cparams(("parallel",)),
    )(chip, *ws, *psums, *[r.reshape((3,) + w.shape) for r, w in zip(recvd, ws)], *ms, *vs)
    return outs[:n], outs[n:2 * n], outs[2 * n:3 * n], outs[3 * n:]


def _bias_b():
    pad = B_PREV * CHUNK
    slopes = np.array([2.0 ** (-8.0 * (i + 1) / B_Q_HEADS) for i in range(B_Q_HEADS)], dtype=np.float32)
    dist = np.abs(np.arange(TQ)[:, None] - np.arange(TQ + pad)[None, :] + pad).astype(np.float32)
    bias = -slopes.reshape(B_Q_HEADS, 1, 1) * dist[None]
    qc = (np.arange(TQ)[:, None] + pad) // CHUNK
    kc = np.arange(TQ + pad)[None, :] // CHUNK
    allowed = (kc <= qc) & (kc >= qc - B_PREV)
    return np.where(allowed[None], bias, np.float32(NEG_INF)).astype(np.float32)


def kernel(x, ffn1_norm, ffn1_w_gate, ffn1_w_up, ffn1_w_down, mix_norm, w_in, rel_bias, sinks, w_proj_a, w_proj_b, w_out, ffn2_norm, ffn2_w_gate, ffn2_w_up, ffn2_w_down, final_norm, loss_target, m_ffn1_norm, m_ffn1_w_gate, m_ffn1_w_up, m_ffn1_w_down, m_mix_norm, m_w_in, m_rel_bias, m_sinks, m_w_proj_a, m_w_proj_b, m_w_out, m_ffn2_norm, m_ffn2_w_gate, m_ffn2_w_up, m_ffn2_w_down, m_final_norm, v_ffn1_norm, v_ffn1_w_gate, v_ffn1_w_up, v_ffn1_w_down, v_mix_norm, v_w_in, v_rel_bias, v_sinks, v_w_proj_a, v_w_proj_b, v_w_out, v_ffn2_norm, v_ffn2_w_gate, v_ffn2_w_up, v_ffn2_w_down, v_final_norm):
    bsz, s_len, _ = x.shape
    t = bsz * s_len
    core = lax.axis_index("c").astype(jnp.int32).reshape(1)
    chip = (2 * lax.axis_index("x") + lax.axis_index("y")).astype(jnp.int32).reshape(1)

    proj_rows = jnp.concatenate([w_proj_a.T, w_proj_b.T], axis=1)
    sh_g1, sh_u1, sh_d1, sh_in, sh_proj, sh_out, sh_g2, sh_u2, sh_d2 = _to_bf16(
        [ffn1_w_gate.T, ffn1_w_up.T, ffn1_w_down, w_in.T, proj_rows, w_out, ffn2_w_gate.T, ffn2_w_up.T, ffn2_w_down],
        "weights_to_bf16")

    gather_up1 = _Gather([sh_g1, sh_u1])
    far = jnp.broadcast_to(rel_bias[:, REL_TABLE - 1:REL_TABLE], (A_HEADS, REL_WRAP // 2))
    tv = jnp.concatenate([far, jnp.flip(rel_bias, axis=1), jnp.zeros((A_HEADS, REL_WRAP // 2 - REL_TABLE), F32)], axis=1)
    bias_a = _bias_a_build(tv.reshape(A_HEADS, 1, REL_WRAP), hosted=[gather_up1])
    wg1, wu1 = gather_up1.result
    gather_down1 = _Gather([sh_d1, sh_in])
    gather_out = _Gather([sh_proj, sh_out])
    gather_ffn2_gate = _Gather([sh_g2])
    gather_ffn2_rest = _Gather([sh_u2, sh_d2])

    x0 = x.reshape(t, D_MODEL)
    tgt = loss_target.reshape(t, D_MODEL)
    gam1, gam2, gam3, gam4 = (g.reshape(1, D_MODEL) for g in (ffn1_norm, mix_norm, ffn2_norm, final_norm))

    h1, g1, u1, a1 = _ffn_up(x0, gam1, wg1, wu1, "ffn1_up", hosted=[gather_down1])
    wd1, win_t = gather_down1.result
    x1 = _ffn_down(x0, a1, wd1, "ffn1_down", hosted=[gather_out])
    proj_t, wout = gather_out.result
    h2, qkv_a, qkv_b, gates = _proj_fwd(x1, gam2, win_t, hosted=[gather_ffn2_gate])
    (wg2,) = gather_ffn2_gate.result
    qkv_a3 = qkv_a.reshape(bsz, s_len, QKV_A)
    qkv_b3 = qkv_b.reshape(bsz, s_len, QKV_B)

    bias_b = jnp.asarray(_bias_b())
    sink_rows = jnp.broadcast_to(sinks.reshape(B_Q_HEADS, 1, 1), (B_Q_HEADS, 8, LANES))

    oa = _attn_a_fwd(qkv_a3, bias_a, hosted=[gather_ffn2_rest]).reshape(t, A_WIDTH)
    wu2, wd2 = gather_ffn2_rest.result
    ob = _attn_b_fwd(qkv_b3, bias_b, sink_rows).reshape(t, B_Q_WIDTH)
    x2, ya, yb, mg = _mix_out_fwd(x1, oa, ob, gates, proj_t, wout)
    h3, g2, u2, a2, x3 = _ffn_fwd(x2, gam3, wg2, wu2, wd2, "ffn2_fwd")

    dx2, dg2, du2, db2, dgam3, dgam4, loss_part = _ffn_bwd_head(x3, gam4, tgt, x2, gam3, g2, u2, wg2, wu2, wd2,
                                                                "ffn2_bwd")
    gw_ffn2 = [_mm_tn([dg2], h3, "grad_ffn2_gate"), _mm_tn([du2], h3, "grad_ffn2_up"),
               _mm_tn([a2], db2, "grad_ffn2_down")]
    pairx_ffn2 = _PairExchange(gw_ffn2)
    doa, dob, dgates, gw_out, gw_proj = _mix_out_bwd(dx2, gates, ya, yb, mg, oa, ob, proj_t, wout,
                                                     hosted=[pairx_ffn2])
    psum_ffn2 = _pair_sum(core, gw_ffn2, pairx_ffn2.result, "pair_sum_ffn2")

    chipx_ffn2 = _ChipExchange(psum_ffn2)
    dqa, dka, dva, dbias_a = _attn_a_bwd(qkv_a3, bias_a, doa.reshape(bsz, s_len, A_WIDTH), hosted=[chipx_ffn2])
    pairx_out = _PairExchange([gw_proj, gw_out])
    dqb, dkvb, dsink = _attn_b_bwd(qkv_b3, bias_b, sink_rows, dob.reshape(bsz, s_len, B_Q_WIDTH), hosted=[pairx_out])
    drel_lanes = _relbias_grad(dbias_a)
    dproj = [dqa.reshape(t, A_WIDTH), dka.reshape(t, A_WIDTH), dva.reshape(t, A_WIDTH), dqb.reshape(t, B_Q_WIDTH),
             dkvb.reshape(t, 2 * B_KV_WIDTH), dgates]

    gw_in = _mm_tn(dproj, h2, "grad_w_in")
    pairx_in = _PairExchange([gw_in])
    psum_out = _pair_sum(core, [gw_proj, gw_out], pairx_out.result, "pair_sum_mix")
    chipx_out = _ChipExchange(psum_out)
    dx1, db1, dgam2 = _proj_bwd(dx2, x1, gam2, dproj, win_t, hosted=[pairx_in, chipx_out])
    psum_in = _pair_sum(core, [gw_in], pairx_in.result, "pair_sum_w_in")
    gw_d1 = _mm_tn([a1], db1, "grad_ffn1_down")

    chipx_in = _ChipExchange(psum_in)
    pairx_d1 = _PairExchange([gw_d1])
    dg1, du1 = _ffn_bwd_act(dx1, g1, u1, wd1, "ffn1_bwd_act", hosted=[chipx_in, pairx_d1])
    psum_d1 = _pair_sum(core, [gw_d1], pairx_d1.result, "pair_sum_ffn1_down")
    chipx_d1 = _ChipExchange(psum_d1)
    gw_g1 = _mm_tn([dg1], h1, "grad_ffn1_gate", hosted=[chipx_d1])
    from_sibling_g1 = _exchange_alone(_PairExchange([gw_g1]), "pair_exchange_ffn1_gate")
    psum_g1 = _pair_sum(core, [gw_g1], from_sibling_g1, "pair_sum_ffn1_gate")
    chipx_g1 = _ChipExchange(psum_g1)
    gw_u1 = _mm_tn([du1], h1, "grad_ffn1_up", hosted=[chipx_g1])
    from_sibling_u1 = _exchange_alone(_PairExchange([gw_u1]), "pair_exchange_ffn1_up")
    psum_u1 = _pair_sum(core, [gw_u1], from_sibling_u1, "pair_sum_ffn1_up")
    chipx_u1 = _ChipExchange(psum_u1)
    dx0, dgam1 = _ffn_bwd_in(dx1, x0, gam1, dg1, du1, wg1, wu1, "ffn1_bwd_in", hosted=[chipx_u1])

    (g_proj,) = _final_sum(chip, psum_out[0:1], chipx_out.result[0:1], "grad_sum_proj")
    grads = {"w_proj_a": g_proj[:, 0:A_WIDTH].T, "w_proj_b": g_proj[:, A_WIDTH:].T}

    def row_of(v):
        return jnp.pad(v.reshape(1, -1), ((0, 0), (0, D_MODEL - v.size)))

    def table_rows(v):
        return jnp.pad(v, ((0, 0), (0, D_MODEL - REL_TABLE)))

    drel_local = jnp.flip(drel_lanes[:, 0, 0:REL_TABLE], axis=1)
    small_part = jnp.concatenate(
        [jnp.sum(dgam1, axis=0, keepdims=True), jnp.sum(dgam2, axis=0, keepdims=True),
         jnp.sum(dgam3, axis=0, keepdims=True), jnp.sum(dgam4, axis=0, keepdims=True),
         row_of(jnp.sum(loss_part)), row_of(dsink[:, 0, 0]), jnp.zeros((2, D_MODEL), F32),
         table_rows(drel_local)], axis=0)
    small = _all_reduce_small(small_part)
    loss = small[4, 0]

    def pack(n1, n2, n3, n4, sk, tb):
        return jnp.concatenate([n1.reshape(1, -1), n2.reshape(1, -1), n3.reshape(1, -1), n4.reshape(1, -1),
                                jnp.zeros((1, D_MODEL), F32), row_of(sk), jnp.zeros((2, D_MODEL), F32), table_rows(tb)],
                               axis=0)

    live = np.zeros((SMALL_ROWS, D_MODEL), np.float32)
    live[0:4] = 1.0
    live[5, 0:B_Q_HEADS] = 1.0
    live[8:16, 0:REL_TABLE] = 1.0
    small_g = small * jnp.asarray(live)
    sw = pack(ffn1_norm, mix_norm, ffn2_norm, final_norm, sinks, rel_bias)
    sm = pack(m_ffn1_norm, m_mix_norm, m_ffn2_norm, m_final_norm, m_sinks, m_rel_bias)
    sv = pack(v_ffn1_norm, v_mix_norm, v_ffn2_norm, v_final_norm, v_sinks, v_rel_bias)
    (sd,), (snm,), (snv,) = _adamw([sw], [small_g], [sm], [sv], "adamw_small")

    def unpack(p):
        return {"ffn1_norm": p[0], "mix_norm": p[1], "ffn2_norm": p[2], "final_norm": p[3],
                "sinks": p[5, 0:B_Q_HEADS], "rel_bias": p[8:16, 0:REL_TABLE]}

    grads.update(unpack(small_g))
    delta, new_m, new_v = unpack(sd), unpack(snm), unpack(snv)

    wmv = {
        "ffn1_w_gate": (ffn1_w_gate, m_ffn1_w_gate, v_ffn1_w_gate), "ffn1_w_up": (ffn1_w_up, m_ffn1_w_up, v_ffn1_w_up),
        "ffn1_w_down": (ffn1_w_down, m_ffn1_w_down, v_ffn1_w_down), "w_in": (w_in, m_w_in, v_w_in),
        "w_proj_a": (w_proj_a, m_w_proj_a, v_w_proj_a), "w_proj_b": (w_proj_b, m_w_proj_b, v_w_proj_b),
        "w_out": (w_out, m_w_out, v_w_out),
        "ffn2_w_gate": (ffn2_w_gate, m_ffn2_w_gate, v_ffn2_w_gate), "ffn2_w_up": (ffn2_w_up, m_ffn2_w_up, v_ffn2_w_up),
        "ffn2_w_down": (ffn2_w_down, m_ffn2_w_down, v_ffn2_w_down),
    }
    row_form_names = ("ffn1_w_gate", "ffn1_w_up", "w_in", "ffn2_w_gate", "ffn2_w_up")

    def form(n, a):
        return a.T if n in row_form_names else a

    def reduced_group(gname, names, psums, recvd, steps):
        gs_, ds_, ms_, vs_ = _adamw_reduced(
            chip, [form(n, wmv[n][0]) for n in names], psums, recvd, [form(n, wmv[n][1]) for n in names],
            [form(n, wmv[n][2]) for n in names], steps, gname)
        for n, g_, d_, m_, v_ in zip(names, gs_, ds_, ms_, vs_):
            grads[n], delta[n], new_m[n], new_v[n] = form(n, g_), form(n, d_), form(n, m_), form(n, v_)

    reduced_group("adamw_ffn", ["ffn1_w_gate", "ffn1_w_up", "ffn1_w_down", "ffn2_w_gate", "ffn2_w_up", "ffn2_w_down"],
                  psum_g1 + psum_u1 + psum_d1 + psum_ffn2,
                  chipx_g1.result + chipx_u1.result + chipx_d1.result + chipx_ffn2.result, 11)
    reduced_group("adamw_in_out", ["w_in", "w_out"], psum_in + psum_out[1:2], chipx_in.result + chipx_out.result[1:2], 2)
    names = ["w_proj_a", "w_proj_b"]
    ds_, ms_, vs_ = _adamw([wmv[n][0] for n in names], [grads[n] for n in names], [wmv[n][1] for n in names],
                           [wmv[n][2] for n in names], "adamw_proj")
    for n, d_, m_, v_ in zip(names, ds_, ms_, vs_):
        delta[n], new_m[n], new_v[n] = d_, m_, v_

    order = ["ffn1_norm", "ffn1_w_gate", "ffn1_w_up", "ffn1_w_down", "mix_norm", "w_in", "rel_bias", "sinks",
             "w_proj_a", "w_proj_b", "w_out", "ffn2_norm", "ffn2_w_gate", "ffn2_w_up", "ffn2_w_down", "final_norm"]
    grad_x = dx0.reshape(bsz, s_len, D_MODEL)
    return (loss, grad_x, *[grads[n] for n in order], *[delta[n] for n in order], *[new_m[n] for n in order],
            *[new_v[n] for n in order])
```

```python
import numpy as np
import jax
import jax.numpy as jnp
from jax import lax
from jax.experimental import pallas as pl
from jax.experimental.pallas import tpu as pltpu

F32 = jnp.float32
BF16 = jnp.bfloat16

D_MODEL = 1024
D_FF = 2816
CHUNK = 64
D_HEAD = 64
A_HEADS = 8
A_PREV = 8
MAX_REL = 128
B_Q_HEADS = 8
B_KV_HEADS = 2
B_GROUP = B_Q_HEADS // B_KV_HEADS
B_PREV = 2
REL_TABLE = (CHUNK - 1) + MAX_REL + 1
A_WIDTH = A_HEADS * D_HEAD
B_Q_WIDTH = B_Q_HEADS * D_HEAD
B_KV_WIDTH = B_KV_HEADS * D_HEAD
QKV_A = 3 * A_WIDTH
QKV_B = B_Q_WIDTH + 2 * B_KV_WIDTH
IN_WIDTH = QKV_A + QKV_B + 2 * D_MODEL
EPS = 1e-6
NEG_INF = -1e30
SCALE = 1.0 / 8.0

ADAM_LR = 0.001
ADAM_B1 = 0.9
ADAM_B2 = 0.999
ADAM_EPS = 1e-08
ADAM_WD = 0.01
ADAM_STEP = 10

N_DEV = 8
N_CHIP = 4
MESH = pl.DeviceIdType.MESH

LANES = 128
TQ = 256
TM = 256
FC = 256
VMEM_LIMIT = 56 << 20


def _cparams(sem, vmem=VMEM_LIMIT):
    return pltpu.CompilerParams(dimension_semantics=sem, vmem_limit_bytes=vmem)


def _dot_nt(a, b):
    return lax.dot_general(a, b, (((1,), (1,)), ((), ())), preferred_element_type=F32)


def _dot_nn(a, b):
    return lax.dot_general(a, b, (((1,), (0,)), ((), ())), preferred_element_type=F32)


def _dot_tn(a, b):
    return lax.dot_general(a, b, (((0,), (0,)), ((), ())), preferred_element_type=F32)


def _resident(shape):
    nd = len(shape)
    return pl.BlockSpec(shape, lambda *_: (0,) * nd, pipeline_mode=pl.Buffered(1))


def _rows(tm, width):
    return pl.BlockSpec((tm, width), lambda i: (i, 0))


def _colsum8(v):
    tm, n = v.shape
    return jnp.sum(v.reshape(tm // 8, 8, n), axis=0)


def _rms(x):
    r = lax.rsqrt(jnp.mean(x * x, axis=-1, keepdims=True) + EPS)
    return x * r, r


def _rms_bwd(dh, xh, r, gamma):
    dxh = dh * gamma
    dx = r * (dxh - xh * jnp.mean(dxh * xh, axis=-1, keepdims=True))
    return dx, _colsum8(dh * xh)


def _hbm():
    return pl.BlockSpec(memory_space=pltpu.HBM)


def _call(body, *, name, grid, in_specs, out_specs, out_shape, args, sem, scratch_shapes=(), hosted=()):
    in_specs, out_specs, out_shape = list(in_specs), list(out_specs), list(out_shape)
    scratch_shapes = list(scratch_shapes)
    if not hosted:
        return pl.pallas_call(body, name=name, grid=grid, in_specs=in_specs, out_specs=out_specs, out_shape=out_shape,
                              scratch_shapes=scratch_shapes, compiler_params=_cparams(sem))(*args)
    n_in, n_out, n_scr = len(in_specs), len(out_specs), len(scratch_shapes)
    x_in = [a for x in hosted for a in x.inputs]
    x_out = [s for x in hosted for s in x.out_shape]
    x_scr = [s for x in hosted for s in x.scratch]
    steps = int(np.prod(grid))
    forward_step = max(steps - 3, 0)
    relay_step = min((5 * steps) // 8, forward_step)

    def wrapped(*refs):
        pos = [0]

        def take(k):
            pos[0] += k
            return refs[pos[0] - k:pos[0]]

        ins, xin, outs, xout, scr, xscr = (take(k) for k in (n_in, len(x_in), n_out, len(x_out), n_scr, len(x_scr)))
        step = 0
        for axis, extent in enumerate(grid):
            step = step * extent + pl.program_id(axis)
        own, oi, oo, osc = [], 0, 0, 0
        for x in hosted:
            own.append((xin[oi:oi + len(x.inputs)], xout[oo:oo + len(x.out_shape)], xscr[osc:osc + len(x.scratch)]))
            oi, oo, osc = oi + len(x.inputs), oo + len(x.out_shape), osc + len(x.scratch)

        def phase(method):
            for x, (i_, o_, s_) in zip(hosted, own):
                getattr(x, method)(i_, o_, s_)

        pl.when(step == 0)(lambda: phase("start"))
        body(*ins, *outs, *scr)
        pl.when(step == relay_step)(lambda: phase("relay"))
        pl.when(step == forward_step)(lambda: phase("forward"))
        pl.when(step == steps - 1)(lambda: phase("finish"))

    res = pl.pallas_call(
        wrapped, name=name, grid=grid, in_specs=in_specs + [_hbm()] * len(x_in),
        out_specs=out_specs + [_hbm()] * len(x_out), out_shape=out_shape + x_out,
        scratch_shapes=scratch_shapes + x_scr, compiler_params=_cparams(("arbitrary",) * len(grid)))(*args, *x_in)
    rest = list(res[n_out:])
    for x in hosted:
        x.result, rest = rest[:len(x.out_shape)], rest[len(x.out_shape):]
    return list(res[:n_out])


def _to_bf16(arrays, name):
    n = len(arrays)

    def body(*refs):
        for k in range(n):
            refs[n + k][...] = refs[k][...].astype(BF16)

    specs = [pl.BlockSpec(a.shape, lambda i: (0, 0)) for a in arrays]
    return pl.pallas_call(
        body, name=name, grid=(1,), in_specs=specs, out_specs=specs,
        out_shape=[jax.ShapeDtypeStruct(a.shape, BF16) for a in arrays],
        compiler_params=_cparams(("arbitrary",)))(*arrays)


def _ffn_fwd(x, gamma, wg_t, wu_t, wd, name, hosted=()):
    t = x.shape[0]
    f = wg_t.shape[0]

    def body(x_ref, gam_ref, wg_ref, wu_ref, wd_ref, h_ref, g_ref, u_ref, a_ref, y_ref):
        xv = x_ref[...]
        xh, _ = _rms(xv)
        h = (xh * gam_ref[...]).astype(BF16)
        h_ref[...] = h
        for j in range(f // FC):
            sl = slice(j * FC, (j + 1) * FC)
            g = _dot_nt(h, wg_ref[sl, :])
            u = _dot_nt(h, wu_ref[sl, :])
            g_ref[:, sl] = g.astype(BF16)
            u_ref[:, sl] = u.astype(BF16)
            a_ref[:, sl] = (g * jax.nn.sigmoid(g) * u).astype(BF16)
        y_ref[...] = xv + 0.5 * _dot_nn(a_ref[...], wd_ref[...])

    return _call(
        body,
        name=name,
        grid=(t // TM,),
        in_specs=[_rows(TM, D_MODEL), _resident((1, D_MODEL)), _resident((f, D_MODEL)), _resident((f, D_MODEL)),
                  _resident((f, D_MODEL))],
        out_specs=[_rows(TM, D_MODEL), _rows(TM, f), _rows(TM, f), _rows(TM, f), _rows(TM, D_MODEL)],
        out_shape=[jax.ShapeDtypeStruct((t, D_MODEL), BF16), jax.ShapeDtypeStruct((t, f), BF16),
                   jax.ShapeDtypeStruct((t, f), BF16), jax.ShapeDtypeStruct((t, f), BF16),
                   jax.ShapeDtypeStruct((t, D_MODEL), F32)],
        args=(x, gamma, wg_t, wu_t, wd), sem=("parallel",), hosted=hosted)


def _ffn_up(x, gamma, wg_t, wu_t, name, hosted=()):
    t = x.shape[0]
    f = wg_t.shape[0]

    def body(x_ref, gam_ref, wg_ref, wu_ref, h_ref, g_ref, u_ref, a_ref):
        xh, _ = _rms(x_ref[...])
        h = (xh * gam_ref[...]).astype(BF16)
        h_ref[...] = h
        for j in range(f // FC):
            sl = slice(j * FC, (j + 1) * FC)
            g = _dot_nt(h, wg_ref[sl, :])
            u = _dot_nt(h, wu_ref[sl, :])
            g_ref[:, sl] = g.astype(BF16)
            u_ref[:, sl] = u.astype(BF16)
            a_ref[:, sl] = (g * jax.nn.sigmoid(g) * u).astype(BF16)

    return _call(
        body,
        name=name,
        grid=(t // TM,),
        in_specs=[_rows(TM, D_MODEL), _resident((1, D_MODEL)), _resident((f, D_MODEL)), _resident((f, D_MODEL))],
        out_specs=[_rows(TM, D_MODEL), _rows(TM, f), _rows(TM, f), _rows(TM, f)],
        out_shape=[jax.ShapeDtypeStruct((t, D_MODEL), BF16), jax.ShapeDtypeStruct((t, f), BF16),
                   jax.ShapeDtypeStruct((t, f), BF16), jax.ShapeDtypeStruct((t, f), BF16)],
        args=(x, gamma, wg_t, wu_t), sem=("parallel",), hosted=hosted)


def _ffn_down(x, a_act, wd, name, hosted=()):
    t = x.shape[0]
    f = wd.shape[0]

    def body(x_ref, a_ref, wd_ref, y_ref):
        y_ref[...] = x_ref[...] + 0.5 * _dot_nn(a_ref[...], wd_ref[...])

    return _call(
        body,
        name=name,
        grid=(t // TM,),
        in_specs=[_rows(TM, D_MODEL), _rows(TM, f), _resident((f, D_MODEL))],
        out_specs=[_rows(TM, D_MODEL)],
        out_shape=[jax.ShapeDtypeStruct((t, D_MODEL), F32)],
        args=(x, a_act, wd), sem=("parallel",), hosted=hosted)[0]


def _ffn_bwd_head(y, gamma_f, target, x, gamma, g_act, u_act, wg_t, wu_t, wd, name):
    t = x.shape[0]
    f = wg_t.shape[0]

    def body(y_ref, gamf_ref, t_ref, x_ref, gam_ref, g_ref, u_ref, wg_ref, wu_ref, wd_ref, dx_ref, dg_ref, du_ref,
             db_ref, dgam_ref, dgamf_ref, loss_ref):
        yh, ry = _rms(y_ref[...])
        gam_f = gamf_ref[...]
        e = yh * gam_f - t_ref[...]
        dv, dgam_f = _rms_bwd(e * (1.0 / D_MODEL), yh, ry, gam_f)
        db = (0.5 * dv).astype(BF16)
        db_ref[...] = db
        for j in range(f // FC):
            sl = slice(j * FC, (j + 1) * FC)
            da = _dot_nt(db, wd_ref[sl, :])
            g = g_ref[:, sl].astype(F32)
            u = u_ref[:, sl].astype(F32)
            s = jax.nn.sigmoid(g)
            dg_ref[:, sl] = (da * u * (s * (1.0 + g * (1.0 - s)))).astype(BF16)
            du_ref[:, sl] = (da * (g * s)).astype(BF16)
        dh = _dot_nn(dg_ref[...], wg_ref[...]) + _dot_nn(du_ref[...], wu_ref[...])
        xh, r = _rms(x_ref[...])
        dxn, dgam = _rms_bwd(dh, xh, r, gam_ref[...])
        dx_ref[...] = dv + dxn

        @pl.when(pl.program_id(0) == 0)
        def _():
            dgam_ref[...] = jnp.zeros_like(dgam_ref)
            dgamf_ref[...] = jnp.zeros_like(dgamf_ref)
            loss_ref[...] = jnp.zeros_like(loss_ref)

        dgam_ref[...] += dgam
        dgamf_ref[...] += dgam_f
        loss_ref[...] += _colsum8(e * e) * (0.5 / D_MODEL)

    acc = pl.BlockSpec((8, D_MODEL), lambda i: (0, 0))
    return _call(
        body,
        name=name,
        grid=(t // TM,),
        in_specs=[_rows(TM, D_MODEL), _resident((1, D_MODEL)), _rows(TM, D_MODEL), _rows(TM, D_MODEL),
                  _resident((1, D_MODEL)), _rows(TM, f), _rows(TM, f),
                  _resident((f, D_MODEL)), _resident((f, D_MODEL)), _resident((f, D_MODEL))],
        out_specs=[_rows(TM, D_MODEL), _rows(TM, f), _rows(TM, f), _rows(TM, D_MODEL), acc, acc, acc],
        out_shape=[jax.ShapeDtypeStruct((t, D_MODEL), F32), jax.ShapeDtypeStruct((t, f), BF16),
                   jax.ShapeDtypeStruct((t, f), BF16), jax.ShapeDtypeStruct((t, D_MODEL), BF16),
                   jax.ShapeDtypeStruct((8, D_MODEL), F32), jax.ShapeDtypeStruct((8, D_MODEL), F32),
                   jax.ShapeDtypeStruct((8, D_MODEL), F32)],
        args=(y, gamma_f, target, x, gamma, g_act, u_act, wg_t, wu_t, wd), sem=("arbitrary",))


def _ffn_bwd_act(d, g_act, u_act, wd, name, hosted=()):
    t = d.shape[0]
    f = wd.shape[0]

    def body(d_ref, g_ref, u_ref, wd_ref, dg_ref, du_ref):
        db = (0.5 * d_ref[...]).astype(BF16)
        for j in range(f // FC):
            sl = slice(j * FC, (j + 1) * FC)
            da = _dot_nt(db, wd_ref[sl, :])
            g = g_ref[:, sl].astype(F32)
            u = u_ref[:, sl].astype(F32)
            s = jax.nn.sigmoid(g)
            dg_ref[:, sl] = (da * u * (s * (1.0 + g * (1.0 - s)))).astype(BF16)
            du_ref[:, sl] = (da * (g * s)).astype(BF16)

    return _call(
        body,
        name=name,
        grid=(t // TM,),
        in_specs=[_rows(TM, D_MODEL), _rows(TM, f), _rows(TM, f), _resident((f, D_MODEL))],
        out_specs=[_rows(TM, f), _rows(TM, f)],
        out_shape=[jax.ShapeDtypeStruct((t, f), BF16), jax.ShapeDtypeStruct((t, f), BF16)],
        args=(d, g_act, u_act, wd), sem=("parallel",), hosted=hosted)


def _ffn_bwd_in(d, x, gamma, dg, du, wg_t, wu_t, name, hosted=()):
    t = x.shape[0]
    f = wg_t.shape[0]

    def body(d_ref, x_ref, gam_ref, dg_ref, du_ref, wg_ref, wu_ref, dx_ref, dgam_ref):
        dh = _dot_nn(dg_ref[...], wg_ref[...]) + _dot_nn(du_ref[...], wu_ref[...])
        xh, r = _rms(x_ref[...])
        dxn, dgam = _rms_bwd(dh, xh, r, gam_ref[...])
        dx_ref[...] = d_ref[...] + dxn

        @pl.when(pl.program_id(0) == 0)
        def _():
            dgam_ref[...] = jnp.zeros_like(dgam_ref)

        dgam_ref[...] += dgam

    return _call(
        body,
        name=name,
        grid=(t // TM,),
        in_specs=[_rows(TM, D_MODEL), _rows(TM, D_MODEL), _resident((1, D_MODEL)), _rows(TM, f), _rows(TM, f),
                  _resident((f, D_MODEL)), _resident((f, D_MODEL))],
        out_specs=[_rows(TM, D_MODEL), pl.BlockSpec((8, D_MODEL), lambda i: (0, 0))],
        out_shape=[jax.ShapeDtypeStruct((t, D_MODEL), F32), jax.ShapeDtypeStruct((8, D_MODEL), F32)],
        args=(d, x, gamma, dg, du, wg_t, wu_t), sem=("arbitrary",), hosted=hosted)


def _mm_tn(pieces, b, name, tile=256, hosted=()):
    t, n = b.shape
    npc = len(pieces)
    counts = [p.shape[1] // tile for p in pieces]
    los = [sum(counts[:k]) for k in range(npc)]
    total = sum(counts)

    def body(*refs):
        a_refs, b_ref, o_ref = refs[:npc], refs[npc], refs[npc + 1]
        i = pl.program_id(0)
        for k in range(npc):
            @pl.when(jnp.logical_and(i >= los[k], i < los[k] + counts[k]))
            def _(k=k):
                o_ref[...] = _dot_tn(a_refs[k][...], b_ref[...]).astype(BF16)

    def a_spec(k):
        return pl.BlockSpec((t, tile), lambda i: (0, jnp.clip(i - los[k], 0, counts[k] - 1)))

    return _call(
        body,
        name=name,
        grid=(total,),
        in_specs=[a_spec(k) for k in range(npc)] + [_resident((t, n))],
        out_specs=[pl.BlockSpec((tile, n), lambda i: (i, 0))],
        out_shape=[jax.ShapeDtypeStruct((total * tile, n), BF16)],
        args=(*pieces, b), sem=("parallel",), hosted=hosted)[0]


def _proj_fwd(x, gamma, win_t, hosted=()):
    t = x.shape[0]

    def body(x_ref, gam_ref, w_ref, h_ref, qa_ref, qb_ref, gt_ref):
        xh, _ = _rms(x_ref[...])
        h = (xh * gam_ref[...]).astype(BF16)
        h_ref[...] = h
        for j in range(QKV_A // FC):
            qa_ref[:, j * FC:(j + 1) * FC] = _dot_nt(h, w_ref[j * FC:(j + 1) * FC, :]).astype(BF16)
        for j in range(QKV_B // FC):
            lo = QKV_A + j * FC
            qb_ref[:, j * FC:(j + 1) * FC] = _dot_nt(h, w_ref[lo:lo + FC, :]).astype(BF16)
        for j in range(2 * D_MODEL // FC):
            lo = QKV_A + QKV_B + j * FC
            gt_ref[:, j * FC:(j + 1) * FC] = _dot_nt(h, w_ref[lo:lo + FC, :])

    return _call(
        body,
        name="proj_fwd",
        grid=(t // TM,),
        in_specs=[_rows(TM, D_MODEL), _resident((1, D_MODEL)), _resident((IN_WIDTH, D_MODEL))],
        out_specs=[_rows(TM, D_MODEL), _rows(TM, QKV_A), _rows(TM, QKV_B), _rows(TM, 2 * D_MODEL)],
        out_shape=[jax.ShapeDtypeStruct((t, D_MODEL), BF16), jax.ShapeDtypeStruct((t, QKV_A), BF16),
                   jax.ShapeDtypeStruct((t, QKV_B), BF16), jax.ShapeDtypeStruct((t, 2 * D_MODEL), F32)],
        args=(x, gamma, win_t), sem=("parallel",), hosted=hosted)


def _proj_bwd(d, x, gamma, pieces, win_t, hosted=()):
    t = x.shape[0]
    npc = len(pieces)
    widths = [p.shape[1] for p in pieces]
    los = [sum(widths[:k]) for k in range(npc)]

    def body(*refs):
        d_ref, x_ref, gam_ref = refs[:3]
        p_refs = refs[3:3 + npc]
        w_ref, dx_ref, db_ref, dgam_ref = refs[3 + npc:]
        dh = _dot_nn(p_refs[0][...], w_ref[0:widths[0], :])
        for k in range(1, npc):
            dh += _dot_nn(p_refs[k][...], w_ref[los[k]:los[k] + widths[k], :])
        xh, r = _rms(x_ref[...])
        dxn, dgam = _rms_bwd(dh, xh, r, gam_ref[...])
        dx = d_ref[...] + dxn
        dx_ref[...] = dx
        db_ref[...] = (0.5 * dx).astype(BF16)

        @pl.when(pl.program_id(0) == 0)
        def _():
            dgam_ref[...] = jnp.zeros_like(dgam_ref)

        dgam_ref[...] += dgam

    return _call(
        body,
        name="proj_bwd",
        grid=(t // TM,),
        in_specs=[_rows(TM, D_MODEL), _rows(TM, D_MODEL), _resident((1, D_MODEL))] + [_rows(TM, w) for w in widths]
        + [_resident((IN_WIDTH, D_MODEL))],
        out_specs=[_rows(TM, D_MODEL), _rows(TM, D_MODEL), pl.BlockSpec((8, D_MODEL), lambda i: (0, 0))],
        out_shape=[jax.ShapeDtypeStruct((t, D_MODEL), F32), jax.ShapeDtypeStruct((t, D_MODEL), BF16),
                   jax.ShapeDtypeStruct((8, D_MODEL), F32)],
        args=(d, x, gamma, *pieces, win_t), sem=("arbitrary",), hosted=hosted)


def _lane_half(shape):
    return lax.broadcasted_iota(jnp.int32, shape, len(shape) - 1) // D_HEAD


def _band_weights(q, kk, bias, sink, qs, pad):
    s = _dot_nt(q, kk) + bias
    if qs is not None:
        col = lax.broadcasted_iota(jnp.int32, s.shape, 1)
        s = jnp.where(col + qs >= pad, s, NEG_INF)
    m = jnp.max(s, axis=-1, keepdims=True)
    if sink is not None:
        m = jnp.maximum(m, sink)
    return jnp.exp(s - m), m


def _weighted_values(p, vv_ones, sink, m):
    r = _dot_nn(p.astype(BF16), vv_ones)
    den = r[:, LANES:2 * LANES]
    if sink is not None:
        den = den + jnp.exp(sink - m)
    return r[:, 0:LANES] / den


def _band_softmax(q, kk, bias, sink, qs, pad):
    p, m = _band_weights(q, kk, bias, sink, qs, pad)
    den = jnp.sum(p, axis=-1, keepdims=True)
    if sink is not None:
        den = den + jnp.exp(sink - m)
    return p, m, 1.0 / den


def _fill_padded(dst, src, pad):
    dst[0:pad, :] = jnp.zeros((pad,) + dst.shape[1:], dst.dtype)
    dst[pad:, :] = src


FWD_PAIRS = 4
BWD_PAIRS = 4


def _attn_a_fwd(qkv, bias, hosted=()):
    bsz, s_len, _ = qkv.shape
    pad = A_PREV * CHUNK
    band = TQ + pad
    pp = FWD_PAIRS
    w = pp * LANES
    nb = A_WIDTH // w

    def body(q_ref, k_ref, v_ref, b_ref, o_ref, kp, vp):
        i = pl.program_id(2)

        @pl.when(i == 0)
        def _():
            _fill_padded(kp, k_ref[...], pad)
            _fill_padded(vp, v_ref[...], pad)

        qs = pl.multiple_of(i * TQ, TQ)
        half = _lane_half((1, LANES))

        ones = jnp.ones((band, LANES), BF16)

        def block(masked):
            for pr in range(pp):
                sl = slice(pr * LANES, (pr + 1) * LANES)
                kk = kp[pl.ds(qs, band), sl]
                vv = jnp.concatenate([vp[pl.ds(qs, band), sl], ones], axis=1)
                q = q_ref[:, sl] * SCALE
                outs = []
                for j in range(2):
                    qm = jnp.where(half == j, q, jnp.zeros_like(q))
                    p, m = _band_weights(qm, kk, b_ref[2 * pr + j], None, qs if masked else None, pad)
                    outs.append(_weighted_values(p, vv, None, m))
                o_ref[:, sl] = jnp.where(half == 0, outs[0], outs[1]).astype(BF16)

        pl.when(i < pad // TQ)(lambda: block(True))
        pl.when(i >= pad // TQ)(lambda: block(False))

    return _call(
        body,
        name="attn_a_fwd",
        grid=(bsz, nb, s_len // TQ),
        in_specs=[pl.BlockSpec((None, TQ, w), lambda b, g, i: (b, i, g)),
                  pl.BlockSpec((None, s_len, w), lambda b, g, i: (b, 0, nb + g)),
                  pl.BlockSpec((None, s_len, w), lambda b, g, i: (b, 0, 2 * nb + g)),
                  pl.BlockSpec((2 * pp, TQ, band), lambda b, g, i: (g, 0, 0))],
        out_specs=[pl.BlockSpec((None, TQ, w), lambda b, g, i: (b, i, g))],
        out_shape=[jax.ShapeDtypeStruct((bsz, s_len, A_WIDTH), BF16)],
        scratch_shapes=[pltpu.VMEM((pad + s_len, w), BF16), pltpu.VMEM((pad + s_len, w), BF16)],
        args=(qkv, qkv, qkv, bias), sem=("arbitrary", "arbitrary", "arbitrary"), hosted=hosted)[0]


def _attn_a_bwd(qkv, bias, do, hosted=()):
    bsz, s_len, _ = qkv.shape
    pad = A_PREV * CHUNK
    band = TQ + pad
    n_i = s_len // TQ
    pp = BWD_PAIRS
    w = pp * LANES
    nb = A_WIDTH // w

    def body(q_ref, k_ref, v_ref, b_ref, do_ref, dq_ref, dk_ref, dv_ref, dbias_ref, kp, vp, dk_acc, dv_acc):
        b = pl.program_id(1)
        i = pl.program_id(2)

        @pl.when(i == 0)
        def _():
            _fill_padded(kp, k_ref[...], pad)
            _fill_padded(vp, v_ref[...], pad)
            dk_acc[...] = jnp.zeros_like(dk_acc)
            dv_acc[...] = jnp.zeros_like(dv_acc)

        @pl.when(jnp.logical_and(b == 0, i == 0))
        def _():
            dbias_ref[...] = jnp.zeros_like(dbias_ref)

        qs = pl.multiple_of(i * TQ, TQ)
        half = _lane_half((1, LANES))

        def block(masked):
            for pr in range(pp):
                sl = slice(pr * LANES, (pr + 1) * LANES)
                kk = kp[pl.ds(qs, band), sl]
                vv = vp[pl.ds(qs, band), sl]
                q = q_ref[:, sl] * SCALE
                dd = do_ref[:, sl]
                dqs, dks, dvs = [], [], []
                for j in range(2):
                    qm = jnp.where(half == j, q, jnp.zeros_like(q))
                    dm = jnp.where(half == j, dd, jnp.zeros_like(dd))
                    p, _, inv = _band_softmax(qm, kk, b_ref[2 * pr + j], None, qs if masked else None, pad)
                    pn = p * inv
                    dp = _dot_nt(dm, vv)
                    delta = jnp.sum(pn * dp, axis=-1, keepdims=True)
                    ds = pn * (dp - delta)
                    dbias_ref[2 * pr + j] += ds[:, band - REL_COLS:]
                    dsb = ds.astype(BF16)
                    dqs.append(_dot_nn(dsb, kk))
                    dks.append(_dot_tn(dsb, q))
                    dvs.append(_dot_tn(pn.astype(BF16), dd))
                dq_ref[:, sl] = (jnp.where(half == 0, dqs[0], dqs[1]) * SCALE).astype(BF16)
                dk_acc[pl.ds(qs, band), sl] += jnp.where(half == 0, dks[0], dks[1])
                dv_acc[pl.ds(qs, band), sl] += jnp.where(half == 0, dvs[0], dvs[1])

        pl.when(i < pad // TQ)(lambda: block(True))
        pl.when(i >= pad // TQ)(lambda: block(False))

        @pl.when(i == n_i - 1)
        def _():
            dk_ref[...] = dk_acc[pad:, :].astype(BF16)
            dv_ref[...] = dv_acc[pad:, :].astype(BF16)

    qspec = pl.BlockSpec((None, TQ, w), lambda g, b, i: (b, i, g))
    kvout = pl.BlockSpec((None, s_len, w), lambda g, b, i: (b, 0, g))
    wide = jax.ShapeDtypeStruct((bsz, s_len, A_WIDTH), BF16)
    return _call(
        body,
        name="attn_a_bwd",
        grid=(nb, bsz, n_i),
        in_specs=[qspec,
                  pl.BlockSpec((None, s_len, w), lambda g, b, i: (b, 0, nb + g)),
                  pl.BlockSpec((None, s_len, w), lambda g, b, i: (b, 0, 2 * nb + g)),
                  pl.BlockSpec((2 * pp, TQ, band), lambda g, b, i: (g, 0, 0)),
                  qspec],
        out_specs=[qspec, kvout, kvout, pl.BlockSpec((2 * pp, TQ, REL_COLS), lambda g, b, i: (g, 0, 0))],
        out_shape=[wide, wide, wide, jax.ShapeDtypeStruct((A_HEADS, TQ, REL_COLS), F32)],
        scratch_shapes=[pltpu.VMEM((pad + s_len, w), BF16), pltpu.VMEM((pad + s_len, w), BF16),
                        pltpu.VMEM((pad + s_len, w), F32), pltpu.VMEM((pad + s_len, w), F32)],
        args=(qkv, qkv, qkv, bias, do), sem=("arbitrary", "arbitrary", "arbitrary"), hosted=hosted)


def _fill_padded_dup(dst, src, pad, h, half):
    other = pltpu.roll(src, D_HEAD, 1)
    _fill_padded(dst, jnp.where(half == h, src, other), pad)


def _attn_b_fwd(qkv, bias, sink):
    bsz, s_len, _ = qkv.shape
    pad = B_PREV * CHUNK
    band = TQ + pad
    kcol = B_Q_WIDTH // LANES
    npair = B_Q_HEADS // 2

    def body(q_ref, k_ref, v_ref, b_ref, s_ref, o_ref, kp, vp):
        i = pl.program_id(1)
        half = _lane_half((1, LANES))

        @pl.when(i == 0)
        def _():
            for h in range(B_KV_HEADS):
                _fill_padded_dup(kp.at[h], k_ref[...], pad, h, half)
                _fill_padded_dup(vp.at[h], v_ref[...], pad, h, half)

        qs = pl.multiple_of(i * TQ, TQ)

        ones = jnp.ones((band, LANES), BF16)

        def block(masked):
            for pr in range(npair):
                h = pr // (B_GROUP // 2)
                sl = slice(pr * LANES, (pr + 1) * LANES)
                kk = kp[h, pl.ds(qs, band), :]
                vv = jnp.concatenate([vp[h, pl.ds(qs, band), :], ones], axis=1)
                q = q_ref[:, sl] * SCALE
                outs = []
                for j in range(2):
                    qm = jnp.where(half == j, q, jnp.zeros_like(q))
                    sink = s_ref[2 * pr + j][0:1, 0:1]
                    p, m = _band_weights(qm, kk, b_ref[2 * pr + j], sink, qs if masked else None, pad)
                    outs.append(_weighted_values(p, vv, sink, m))
                o_ref[:, sl] = jnp.where(half == 0, outs[0], outs[1]).astype(BF16)

        pl.when(i < -(-pad // TQ))(lambda: block(True))
        pl.when(i >= -(-pad // TQ))(lambda: block(False))

    return pl.pallas_call(
        body,
        name="attn_b_fwd",
        grid=(bsz, s_len // TQ),
        in_specs=[pl.BlockSpec((None, TQ, B_Q_WIDTH), lambda b, i: (b, i, 0)),
                  pl.BlockSpec((None, s_len, LANES), lambda b, i: (b, 0, kcol)),
                  pl.BlockSpec((None, s_len, LANES), lambda b, i: (b, 0, kcol + 1)),
                  pl.BlockSpec((B_Q_HEADS, TQ, band), lambda b, i: (0, 0, 0)),
                  pl.BlockSpec((B_Q_HEADS, 8, LANES), lambda b, i: (0, 0, 0))],
        out_specs=pl.BlockSpec((None, TQ, B_Q_WIDTH), lambda b, i: (b, i, 0)),
        out_shape=jax.ShapeDtypeStruct((bsz, s_len, B_Q_WIDTH), BF16),
        scratch_shapes=[pltpu.VMEM((B_KV_HEADS, pad + s_len, LANES), BF16),
                        pltpu.VMEM((B_KV_HEADS, pad + s_len, LANES), BF16)],
        compiler_params=_cparams(("arbitrary", "arbitrary")),
    )(qkv, qkv, qkv, bias, sink)


def _attn_b_bwd(qkv, bias, sink, do, hosted=()):
    bsz, s_len, _ = qkv.shape
    pad = B_PREV * CHUNK
    band = TQ + pad
    kcol = B_Q_WIDTH // LANES
    n_i = s_len // TQ
    pp = B_GROUP // 2

    def body(q_ref, k_ref, v_ref, b_ref, s_ref, do_ref, dq_ref, dkv_ref, dsink_ref, kp, vp, dk_acc, dv_acc):
        b = pl.program_id(0)
        i = pl.program_id(1)
        half = _lane_half((1, LANES))

        @pl.when(i == 0)
        def _():
            for h in range(B_KV_HEADS):
                _fill_padded_dup(kp.at[h], k_ref[...], pad, h, half)
                _fill_padded_dup(vp.at[h], v_ref[...], pad, h, half)
            dk_acc[...] = jnp.zeros_like(dk_acc)
            dv_acc[...] = jnp.zeros_like(dv_acc)

        @pl.when(jnp.logical_and(b == 0, i == 0))
        def _():
            dsink_ref[...] = jnp.zeros_like(dsink_ref)

        qs = pl.multiple_of(i * TQ, TQ)

        def block(masked):
            heads_dk, heads_dv = [], []
            for h in range(B_KV_HEADS):
                kk = kp[h, pl.ds(qs, band), :]
                vv = vp[h, pl.ds(qs, band), :]
                dk2 = jnp.zeros((band, LANES), F32)
                dv2 = jnp.zeros((band, LANES), F32)
                for pr in range(pp * h, pp * (h + 1)):
                    sl = slice(pr * LANES, (pr + 1) * LANES)
                    q = q_ref[:, sl] * SCALE
                    dd = do_ref[:, sl]
                    dqs, dks, dvs = [], [], []
                    for j in range(2):
                        qm = jnp.where(half == j, q, jnp.zeros_like(q))
                        dm = jnp.where(half == j, dd, jnp.zeros_like(dd))
                        sink = s_ref[2 * pr + j][0:1, 0:1]
                        p, m, inv = _band_softmax(qm, kk, b_ref[2 * pr + j], sink, qs if masked else None, pad)
                        pn = p * inv
                        dp = _dot_nt(dm, vv)
                        delta = jnp.sum(pn * dp, axis=-1, keepdims=True)
                        ds = pn * (dp - delta)
                        dsb = ds.astype(BF16)
                        dqs.append(_dot_nn(dsb, kk))
                        dks.append(_dot_tn(dsb, q))
                        dvs.append(_dot_tn(pn.astype(BF16), dd))
                        dsk = jnp.sum(-(jnp.exp(sink - m) * inv) * delta, axis=0, keepdims=True)
                        dsink_ref[2 * pr + j] += jnp.broadcast_to(dsk, (8, LANES))
                    dq_ref[:, sl] = (jnp.where(half == 0, dqs[0], dqs[1]) * SCALE).astype(BF16)
                    dk2 = dk2 + jnp.where(half == 0, dks[0], dks[1])
                    dv2 = dv2 + jnp.where(half == 0, dvs[0], dvs[1])
                heads_dk.append(dk2 + pltpu.roll(dk2, D_HEAD, 1))
                heads_dv.append(dv2 + pltpu.roll(dv2, D_HEAD, 1))
            dk_acc[pl.ds(qs, band), :] += jnp.where(half == 0, heads_dk[0], heads_dk[1])
            dv_acc[pl.ds(qs, band), :] += jnp.where(half == 0, heads_dv[0], heads_dv[1])

        pl.when(i < -(-pad // TQ))(lambda: block(True))
        pl.when(i >= -(-pad // TQ))(lambda: block(False))

        @pl.when(i == n_i - 1)
        def _():
            dkv_ref[:, 0:LANES] = dk_acc[pad:, :].astype(BF16)
            dkv_ref[:, LANES:2 * LANES] = dv_acc[pad:, :].astype(BF16)

    qspec = pl.BlockSpec((None, TQ, B_Q_WIDTH), lambda b, i: (b, i, 0))
    return _call(
        body,
        name="attn_b_bwd",
        grid=(bsz, n_i),
        in_specs=[qspec,
                  pl.BlockSpec((None, s_len, LANES), lambda b, i: (b, 0, kcol)),
                  pl.BlockSpec((None, s_len, LANES), lambda b, i: (b, 0, kcol + 1)),
                  pl.BlockSpec((B_Q_HEADS, TQ, band), lambda b, i: (0, 0, 0)),
                  pl.BlockSpec((B_Q_HEADS, 8, LANES), lambda b, i: (0, 0, 0)),
                  qspec],
        out_specs=[qspec, pl.BlockSpec((None, s_len, 2 * LANES), lambda b, i: (b, 0, 0)),
                   pl.BlockSpec((B_Q_HEADS, 8, LANES), lambda b, i: (0, 0, 0))],
        out_shape=[jax.ShapeDtypeStruct((bsz, s_len, B_Q_WIDTH), BF16),
                   jax.ShapeDtypeStruct((bsz, s_len, 2 * B_KV_WIDTH), BF16),
                   jax.ShapeDtypeStruct((B_Q_HEADS, 8, LANES), F32)],
        scratch_shapes=[pltpu.VMEM((B_KV_HEADS, pad + s_len, LANES), BF16),
                        pltpu.VMEM((B_KV_HEADS, pad + s_len, LANES), BF16),
                        pltpu.VMEM((pad + s_len, LANES), F32), pltpu.VMEM((pad + s_len, LANES), F32)],
        args=(qkv, qkv, qkv, bias, sink, do), sem=("arbitrary", "arbitrary"), hosted=hosted)


REL_COLS = 3 * 128
REL_WRAP = 512


def _bias_a_build(tv, hosted=()):
    h = tv.shape[0]
    pad = A_PREV * CHUNK
    band = TQ + pad

    def body(tv_ref, o_ref):
        row = tv_ref[...]
        x = jnp.broadcast_to(row, (TQ, REL_WRAP))
        r = lax.broadcasted_iota(jnp.int32, x.shape, 0)
        for bit in range(8):
            sh = 1 << bit
            x = jnp.where((r & sh) != 0, pltpu.roll(x, sh, 1), x)
        far = jnp.broadcast_to(row[:, 0:1], (TQ, band - REL_COLS))
        full = jnp.concatenate([far, x[:, REL_WRAP // 2:REL_WRAP], x[:, 0:REL_COLS - REL_WRAP // 2]], axis=1)
        qc = (lax.broadcasted_iota(jnp.int32, full.shape, 0) + pad) // CHUNK
        kc = lax.broadcasted_iota(jnp.int32, full.shape, 1) // CHUNK
        ok = jnp.logical_and(kc <= qc, kc >= qc - A_PREV)
        o_ref[...] = jnp.where(ok, full, NEG_INF)

    return _call(
        body,
        name="bias_a_build",
        grid=(h,),
        in_specs=[pl.BlockSpec((None, 1, REL_WRAP), lambda hh: (hh, 0, 0))],
        out_specs=[pl.BlockSpec((None, TQ, band), lambda hh: (hh, 0, 0))],
        out_shape=[jax.ShapeDtypeStruct((h, TQ, band), F32)],
        args=(tv,), sem=("parallel",), hosted=hosted)[0]


def _relbias_grad(dbias, hosted=()):
    h, rows, _ = dbias.shape

    def body(d_ref, o_ref):
        x = d_ref[...]
        r = lax.broadcasted_iota(jnp.int32, x.shape, 0)
        c = lax.broadcasted_iota(jnp.int32, x.shape, 1) - r
        x = jnp.where(jnp.logical_and(c >= 1, c < REL_TABLE), x, 0.0)
        for bit in range(8):
            sh = 1 << bit
            x = jnp.where((r & sh) != 0, pltpu.roll(x, REL_COLS - sh, 1), x)
        diag = jnp.sum(x, axis=0, keepdims=True)
        lane = lax.broadcasted_iota(jnp.int32, diag.shape, 1)
        diag = jnp.where(jnp.logical_and(lane >= 1, lane < REL_TABLE), diag, 0.0)
        rest = -jnp.sum(diag, axis=1, keepdims=True)
        o_ref[...] = jnp.broadcast_to(jnp.where(lane == 0, rest, diag), o_ref.shape)

    return _call(
        body,
        name="relbias_grad",
        grid=(h,),
        in_specs=[pl.BlockSpec((None, rows, REL_COLS), lambda hh: (hh, 0, 0))],
        out_specs=[pl.BlockSpec((None, 8, REL_COLS), lambda hh: (hh, 0, 0))],
        out_shape=[jax.ShapeDtypeStruct((h, 8, REL_COLS), F32)],
        args=(dbias,), sem=("parallel",), hosted=hosted)[0]


def _mix_out_fwd(x, oa, ob, gates, proj_t, wout):
    t = x.shape[0]

    def body(x_ref, oa_ref, ob_ref, gt_ref, pt_ref, wo_ref, y_ref, ya_ref, yb_ref, mg_ref):
        ya = _dot_nt(oa_ref[...], pt_ref[:, 0:A_WIDTH])
        yb = _dot_nt(ob_ref[...], pt_ref[:, A_WIDTH:A_WIDTH + B_Q_WIDTH])
        ya_ref[...] = ya.astype(BF16)
        yb_ref[...] = yb.astype(BF16)
        mg = jax.nn.sigmoid(gt_ref[:, 0:D_MODEL]) * ya + jax.nn.sigmoid(gt_ref[:, D_MODEL:2 * D_MODEL]) * yb
        mgb = mg.astype(BF16)
        mg_ref[...] = mgb
        y_ref[...] = x_ref[...] + _dot_nn(mgb, wo_ref[...])

    return pl.pallas_call(
        body,
        name="mix_out_fwd",
        grid=(t // TM,),
        in_specs=[_rows(TM, D_MODEL), _rows(TM, A_WIDTH), _rows(TM, B_Q_WIDTH), _rows(TM, 2 * D_MODEL),
                  _resident((D_MODEL, A_WIDTH + B_Q_WIDTH)), _resident((D_MODEL, D_MODEL))],
        out_specs=[_rows(TM, D_MODEL), _rows(TM, D_MODEL), _rows(TM, D_MODEL), _rows(TM, D_MODEL)],
        out_shape=[jax.ShapeDtypeStruct((t, D_MODEL), F32), jax.ShapeDtypeStruct((t, D_MODEL), BF16),
                   jax.ShapeDtypeStruct((t, D_MODEL), BF16), jax.ShapeDtypeStruct((t, D_MODEL), BF16)],
        compiler_params=_cparams(("parallel",)),
    )(x, oa, ob, gates, proj_t, wout)


def _mix_out_bwd(d, gates, ya, yb, mg, oa, ob, proj_t, wout, hosted=()):
    t = d.shape[0]
    nt = t // TM

    def body(d_ref, gt_ref, ya_ref, yb_ref, mg_ref, oa_ref, ob_ref, pt_ref, wo_ref,
             doa_ref, dob_ref, dgt_ref, gwo_ref, gwp_ref, acc_o, acc_p):
        i = pl.program_id(0)
        db = d_ref[...].astype(BF16)
        dmg = _dot_nt(db, wo_ref[...])
        sa = jax.nn.sigmoid(gt_ref[:, 0:D_MODEL])
        sb = jax.nn.sigmoid(gt_ref[:, D_MODEL:2 * D_MODEL])
        dya = (dmg * sa).astype(BF16)
        dyb = (dmg * sb).astype(BF16)
        dgt_ref[:, 0:D_MODEL] = (dmg * ya_ref[...].astype(F32) * (sa * (1.0 - sa))).astype(BF16)
        dgt_ref[:, D_MODEL:2 * D_MODEL] = (dmg * yb_ref[...].astype(F32) * (sb * (1.0 - sb))).astype(BF16)
        doa_ref[...] = _dot_nn(dya, pt_ref[:, 0:A_WIDTH]).astype(BF16)
        dob_ref[...] = _dot_nn(dyb, pt_ref[:, A_WIDTH:A_WIDTH + B_Q_WIDTH]).astype(BF16)

        @pl.when(i == 0)
        def _():
            acc_o[...] = jnp.zeros_like(acc_o)
            acc_p[...] = jnp.zeros_like(acc_p)

        acc_o[...] += _dot_tn(mg_ref[...], db)
        acc_p[:, 0:A_WIDTH] += _dot_tn(dya, oa_ref[...])
        acc_p[:, A_WIDTH:A_WIDTH + B_Q_WIDTH] += _dot_tn(dyb, ob_ref[...])

        @pl.when(i == nt - 1)
        def _():
            gwo_ref[...] = acc_o[...].astype(BF16)
            gwp_ref[...] = acc_p[...].astype(BF16)

    whole = pl.BlockSpec((D_MODEL, D_MODEL), lambda i: (0, 0))
    return _call(
        body,
        name="mix_out_bwd",
        grid=(nt,),
        in_specs=[_rows(TM, D_MODEL), _rows(TM, 2 * D_MODEL), _rows(TM, D_MODEL), _rows(TM, D_MODEL),
                  _rows(TM, D_MODEL), _rows(TM, A_WIDTH), _rows(TM, B_Q_WIDTH),
                  _resident((D_MODEL, A_WIDTH + B_Q_WIDTH)), _resident((D_MODEL, D_MODEL))],
        out_specs=[_rows(TM, A_WIDTH), _rows(TM, B_Q_WIDTH), _rows(TM, 2 * D_MODEL), whole, whole],
        out_shape=[jax.ShapeDtypeStruct((t, A_WIDTH), BF16), jax.ShapeDtypeStruct((t, B_Q_WIDTH), BF16),
                   jax.ShapeDtypeStruct((t, 2 * D_MODEL), BF16), jax.ShapeDtypeStruct((D_MODEL, D_MODEL), BF16),
                   jax.ShapeDtypeStruct((D_MODEL, D_MODEL), BF16)],
        scratch_shapes=[pltpu.VMEM((D_MODEL, D_MODEL), F32), pltpu.VMEM((D_MODEL, A_WIDTH + B_Q_WIDTH), F32)],
        args=(d, gates, ya, yb, mg, oa, ob, proj_t, wout), sem=("arbitrary",), hosted=hosted)


def _place():
    x, y, c = lax.axis_index("x"), lax.axis_index("y"), lax.axis_index("c")
    chips = [(1 - x, y), (x, 1 - y), (1 - x, 1 - y)]
    return x, y, c, chips


class _Gather:
    per = 8

    def __init__(self, shards):
        n = len(shards)
        self.inputs = list(shards)
        self.out_shape = [jax.ShapeDtypeStruct((N_DEV * s.shape[0], s.shape[1]), s.dtype) for s in shards]
        self.scratch = [pltpu.SemaphoreType.DMA((n * self.per,)), pltpu.SemaphoreType.DMA((n * self.per,)),
                        pltpu.SemaphoreType.DMA((n,))]
        self.result = None

    def _parts(self, ins, outs, sems):
        send_sems, recv_sems, local_sems = sems
        x, y, c, chips = _place()
        me, sibling = (x, y, c), (x, y, 1 - c)
        xn, yn, dg = chips
        n = len(ins)

        def rows(k, p, part=None):
            r = ins[k].shape[0]
            base = (4 * p[0] + 2 * p[1] + p[2]) * r
            if part is None:
                return outs[k].at[pl.ds(base, r), :]
            return outs[k].at[pl.ds(base + part * (r // 2), r // 2), :]

        def copy(k, slot, block, to, src=None, part=None):
            return pltpu.make_async_remote_copy(
                src_ref=rows(k, block, part) if src is None else src, dst_ref=rows(k, block, part),
                send_sem=send_sems.at[k * self.per + slot], recv_sem=recv_sems.at[k * self.per + slot],
                device_id=to, device_id_type=MESH)

        mine = [pltpu.make_async_copy(ins[k], rows(k, me), local_sems.at[k]) for k in range(n)]
        sends, lands = [], []
        for k in range(n):
            sends.append({
                0: copy(k, 0, me, sibling, src=ins[k]),
                1: copy(k, 1, me, (*xn, c), src=ins[k]),
                2: copy(k, 2, me, (*yn, c), src=ins[k]),
                3: copy(k, 3, (*xn, c), (*yn, c), part=0),
                4: copy(k, 4, (*yn, c), (*xn, c), part=1),
                5: copy(k, 5, (*xn, c), sibling),
                6: copy(k, 6, (*yn, c), sibling),
                7: copy(k, 7, (*dg, c), sibling)})
            lands.append({
                0: copy(k, 0, sibling, me),
                1: copy(k, 1, (*xn, c), me),
                2: copy(k, 2, (*yn, c), me),
                3: copy(k, 3, (*dg, c), me, part=0),
                4: copy(k, 4, (*dg, c), me, part=1),
                5: copy(k, 5, (*xn, 1 - c), me),
                6: copy(k, 6, (*yn, 1 - c), me),
                7: copy(k, 7, (*dg, 1 - c), me)})
        return n, mine, sends, lands

    def start(self, ins, outs, sems):
        n, mine, sends, _ = self._parts(ins, outs, sems)
        for cp in mine:
            cp.start()
        for slot in (0, 1, 2):
            for k in range(n):
                sends[k][slot].start()

    def relay(self, ins, outs, sems):
        n, _, sends, lands = self._parts(ins, outs, sems)
        for k in range(n):
            lands[k][1].wait_recv()
            sends[k][3].start()
            sends[k][5].start()
        for k in range(n):
            lands[k][2].wait_recv()
            sends[k][4].start()
            sends[k][6].start()

    def forward(self, ins, outs, sems):
        n, _, sends, lands = self._parts(ins, outs, sems)
        for k in range(n):
            lands[k][3].wait_recv()
            lands[k][4].wait_recv()
            sends[k][7].start()

    def finish(self, ins, outs, sems):
        n, mine, sends, lands = self._parts(ins, outs, sems)
        for k in range(n):
            for slot in (0, 5, 6, 7):
                lands[k][slot].wait_recv()
        for k in range(n):
            for slot in range(self.per):
                sends[k][slot].wait_send()
        for cp in mine:
            cp.wait()


class _PairExchange:
    def __init__(self, grads):
        n = len(grads)
        self.inputs = list(grads)
        self.out_shape = [jax.ShapeDtypeStruct((g.shape[0] // 2, g.shape[1]), g.dtype) for g in grads]
        self.scratch = [pltpu.SemaphoreType.DMA((n * N_CHIP,)), pltpu.SemaphoreType.DMA((n * N_CHIP,))]
        self.result = None

    def _copies(self, ins, outs, sems):
        send_sems, recv_sems = sems
        x, y, c, _ = _place()
        copies = []
        for k in range(len(ins)):
            r = ins[k].shape[0] // N_DEV
            for q in range(N_CHIP):
                copies.append(pltpu.make_async_remote_copy(
                    src_ref=ins[k].at[pl.ds((2 * q + 1 - c) * r, r), :], dst_ref=outs[k].at[pl.ds(q * r, r), :],
                    send_sem=send_sems.at[k * N_CHIP + q], recv_sem=recv_sems.at[k * N_CHIP + q],
                    device_id=(x, y, 1 - c), device_id_type=MESH))
        return copies

    def start(self, ins, outs, sems):
        for cp in self._copies(ins, outs, sems):
            cp.start()

    def relay(self, ins, outs, sems):
        pass

    def forward(self, ins, outs, sems):
        pass

    def finish(self, ins, outs, sems):
        copies = self._copies(ins, outs, sems)
        for cp in copies:
            cp.wait_recv()
        for cp in copies:
            cp.wait_send()


class _ChipExchange(_PairExchange):
    def __init__(self, psums):
        n = len(psums)
        self.inputs = list(psums)
        self.out_shape = [jax.ShapeDtypeStruct((3 * p.shape[0] // N_CHIP, p.shape[1]), p.dtype) for p in psums]
        self.scratch = [pltpu.SemaphoreType.DMA((n * 3,)), pltpu.SemaphoreType.DMA((n * 3,))]
        self.result = None

    def _copies(self, ins, outs, sems):
        send_sems, recv_sems = sems
        _, _, c, chips = _place()
        copies = []
        for k in range(len(ins)):
            r = ins[k].shape[0] // N_CHIP
            for j, chip in enumerate(chips):
                copies.append(pltpu.make_async_remote_copy(
                    src_ref=ins[k].at[pl.ds((2 * chip[0] + chip[1]) * r, r), :], dst_ref=outs[k].at[pl.ds(j * r, r), :],
                    send_sem=send_sems.at[k * 3 + j], recv_sem=recv_sems.at[k * 3 + j],
                    device_id=(*chip, c), device_id_type=MESH))
        return copies


def _exchange_alone(xchg, name):
    n_in, n_out = len(xchg.inputs), len(xchg.out_shape)

    def body(*refs):
        ins, outs, sems = refs[:n_in], refs[n_in:n_in + n_out], refs[n_in + n_out:]
        xchg.start(ins, outs, sems)
        xchg.relay(ins, outs, sems)
        xchg.forward(ins, outs, sems)
        xchg.finish(ins, outs, sems)

    xchg.result = list(pl.pallas_call(
        body, name=name, in_specs=[_hbm()] * n_in, out_specs=[_hbm()] * n_out, out_shape=xchg.out_shape,
        scratch_shapes=xchg.scratch)(*xchg.inputs))
    return xchg.result


def _pair_sum(core, grads, recvd, name):
    n = len(grads)
    r = grads[0].shape[0] // N_DEV
    cdim = grads[0].shape[1]
    tr = r // 2 if r % 32 == 0 else r
    nt = r // tr

    def body(core_ref, *refs):
        del core_ref
        for k in range(n):
            refs[2 * n + k][...] = (refs[k][...].astype(F32) + refs[n + k][...].astype(F32)).astype(BF16)

    gspec = pl.BlockSpec((tr, cdim), lambda q, i, core_ref: ((2 * q + core_ref[0]) * nt + i, 0))
    rspec = pl.BlockSpec((tr, cdim), lambda q, i, core_ref: (q * nt + i, 0))
    return pl.pallas_call(
        body,
        name=name,
        grid_spec=pltpu.PrefetchScalarGridSpec(
            num_scalar_prefetch=1, grid=(N_CHIP, nt), in_specs=[gspec] * n + [rspec] * n, out_specs=[rspec] * n),
        out_shape=[jax.ShapeDtypeStruct((N_CHIP * r, cdim), BF16) for _ in range(n)],
        compiler_params=_cparams(("parallel", "parallel")),
    )(core, *grads, *recvd)


def _final_sum(chip, psums, recvd, name):
    n = len(psums)
    r = psums[0].shape[0] // N_CHIP
    cdim = psums[0].shape[1]
    tr = r // 2 if r % 32 == 0 else r
    nt = r // tr

    def body(chip_ref, *refs):
        del chip_ref
        for k in range(n):
            got = refs[n + k]
            tot = refs[k][...].astype(F32) + got[0].astype(F32)
            tot = tot + got[1].astype(F32)
            tot = tot + got[2].astype(F32)
            refs[2 * n + k][...] = tot

    pspec = pl.BlockSpec((tr, cdim), lambda i, chip_ref: (chip_ref[0] * nt + i, 0))
    rspec = pl.BlockSpec((3, tr, cdim), lambda i, chip_ref: (0, i, 0))
    ospec = pl.BlockSpec((tr, cdim), lambda i, chip_ref: (i, 0))
    return pl.pallas_call(
        body,
        name=name,
        grid_spec=pltpu.PrefetchScalarGridSpec(
            num_scalar_prefetch=1, grid=(nt,), in_specs=[pspec] * n + [rspec] * n, out_specs=[ospec] * n),
        out_shape=[jax.ShapeDtypeStruct((r, cdim), F32) for _ in range(n)],
        compiler_params=_cparams(("parallel",)),
    )(chip, *psums, *[g.reshape(3, r, cdim) for g in recvd])


SMALL_ROWS = 16


def _all_reduce_small(part):
    def body(p_ref, o_ref, buf, send_sems, recv_sems):
        x, y, c, _ = _place()
        me = 4 * x + 2 * y + c
        buf[me] = p_ref[...]
        copies = []
        for d in range(1, N_DEV):
            peer = me ^ d
            copies.append(pltpu.make_async_remote_copy(
                src_ref=p_ref, dst_ref=buf.at[me], send_sem=send_sems.at[d - 1], recv_sem=recv_sems.at[d - 1],
                device_id=(peer // 4, (peer // 2) % 2, peer % 2), device_id_type=MESH))
        for cp in copies:
            cp.start()
        for cp in copies:
            cp.wait_recv()
        for cp in copies:
            cp.wait_send()
        tot = buf[0]
        for d in range(1, N_DEV):
            tot = tot + buf[d]
        o_ref[...] = tot

    return pl.pallas_call(
        body,
        name="all_reduce_small",
        in_specs=[pl.BlockSpec(memory_space=pltpu.VMEM)],
        out_specs=pl.BlockSpec(memory_space=pltpu.VMEM),
        out_shape=jax.ShapeDtypeStruct(part.shape, F32),
        scratch_shapes=[pltpu.VMEM((N_DEV,) + part.shape, F32), pltpu.SemaphoreType.DMA((N_DEV - 1,)),
                        pltpu.SemaphoreType.DMA((N_DEV - 1,))],
    )(part)


ADAMW_STEPS = 4


def _adamw(ws, gs, ms, vs, name, hosted=()):
    n = len(ws)
    steps = ADAMW_STEPS if all(w.shape[0] % (8 * ADAMW_STEPS) == 0 for w in ws) else 1
    c1 = 1.0 - ADAM_B1 ** ADAM_STEP
    c2 = 1.0 - ADAM_B2 ** ADAM_STEP

    def body(*refs):
        for k in range(n):
            w, g, m, v = (refs[j * n + k][...] for j in range(4))
            m2 = ADAM_B1 * m + (1.0 - ADAM_B1) * g
            v2 = ADAM_B2 * v + (1.0 - ADAM_B2) * (g * g)
            delta = -ADAM_LR * ((m2 / c1) / (jnp.sqrt(v2 / c2) + ADAM_EPS) + ADAM_WD * w)
            refs[4 * n + k][...] = delta
            refs[5 * n + k][...] = m2
            refs[6 * n + k][...] = v2

    specs = [pl.BlockSpec((w.shape[0] // steps, w.shape[1]), lambda i: (i, 0)) for w in ws]
    shapes = [jax.ShapeDtypeStruct(w.shape, F32) for w in ws]
    outs = _call(
        body,
        name=name,
        grid=(steps,),
        in_specs=specs * 4,
        out_specs=specs * 3,
        out_shape=shapes * 3,
        args=(*ws, *gs, *ms, *vs), sem=("parallel",), hosted=hosted)
    return outs[:n], outs[n:2 * n], outs[2 * n:]


def _adamw_reduced(chip, ws, psums, recvd, ms, vs, steps, name):
    n = len(ws)
    c1 = 1.0 - ADAM_B1 ** ADAM_STEP
    c2 = 1.0 - ADAM_B2 ** ADAM_STEP

    def body(chip_ref, *refs):
        del chip_ref
        for k in range(n):
            w, m, v = (refs[j * n + k][...] for j in (0, 3, 4))
            got = refs[2 * n + k]
            g = refs[n + k][...].astype(F32) + got[0].astype(F32)
            g = g + got[1].astype(F32)
            g = g + got[2].astype(F32)
            m2 = ADAM_B1 * m + (1.0 - ADAM_B1) * g
            v2 = ADAM_B2 * v + (1.0 - ADAM_B2) * (g * g)
            refs[5 * n + k][...] = g
            refs[6 * n + k][...] = -ADAM_LR * ((m2 / c1) / (jnp.sqrt(v2 / c2) + ADAM_EPS) + ADAM_WD * w)
            refs[7 * n + k][...] = m2
            refs[8 * n + k][...] = v2

    def blk(w):
        return (w.shape[0] // steps, w.shape[1])

    own = [pl.BlockSpec(blk(w), lambda i, chip_ref: (i, 0)) for w in ws]
    psum = [pl.BlockSpec(blk(w), lambda i, chip_ref: (chip_ref[0] * steps + i, 0)) for w in ws]
    recv = [pl.BlockSpec((3,) + blk(w), lambda i, chip_ref: (0, i, 0)) for w in ws]
    shapes = [jax.ShapeDtypeStruct(w.shape, F32) for w in ws]
    outs = pl.pallas_call(
        body,
        name=name,
        grid_spec=pltpu.PrefetchScalarGridSpec(
            num_scalar_prefetch=1, grid=(steps,), in_specs=own + psum + recv + own + own, out_specs=own * 4),
        out_shape=shapes * 4,
        compiler_params=_cparams(("parallel",)),
    )(chip, *ws, *psums, *[r.reshape((3,) + w.shape) for r, w in zip(recvd, ws)], *ms, *vs)
    return outs[:n], outs[n:2 * n], outs[2 * n:3 * n], outs[3 * n:]


def _bias_b():
    pad = B_PREV * CHUNK
    slopes = np.array([2.0 ** (-8.0 * (i + 1) / B_Q_HEADS) for i in range(B_Q_HEADS)], dtype=np.float32)
    dist = np.abs(np.arange(TQ)[:, None] - np.arange(TQ + pad)[None, :] + pad).astype(np.float32)
    bias = -slopes.reshape(B_Q_HEADS, 1, 1) * dist[None]
    qc = (np.arange(TQ)[:, None] + pad) // CHUNK
    kc = np.arange(TQ + pad)[None, :] // CHUNK
    allowed = (kc <= qc) & (kc >= qc - B_PREV)
    return np.where(allowed[None], bias, np.float32(NEG_INF)).astype(np.float32)


def kernel(x, ffn1_norm, ffn1_w_gate, ffn1_w_up, ffn1_w_down, mix_norm, w_in, rel_bias, sinks, w_proj_a, w_proj_b, w_out, ffn2_norm, ffn2_w_gate, ffn2_w_up, ffn2_w_down, final_norm, loss_target, m_ffn1_norm, m_ffn1_w_gate, m_ffn1_w_up, m_ffn1_w_down, m_mix_norm, m_w_in, m_rel_bias, m_sinks, m_w_proj_a, m_w_proj_b, m_w_out, m_ffn2_norm, m_ffn2_w_gate, m_ffn2_w_up, m_ffn2_w_down, m_final_norm, v_ffn1_norm, v_ffn1_w_gate, v_ffn1_w_up, v_ffn1_w_down, v_mix_norm, v_w_in, v_rel_bias, v_sinks, v_w_proj_a, v_w_proj_b, v_w_out, v_ffn2_norm, v_ffn2_w_gate, v_ffn2_w_up, v_ffn2_w_down, v_final_norm):
    bsz, s_len, _ = x.shape
    t = bsz * s_len
    core = lax.axis_index("c").astype(jnp.int32).reshape(1)
    chip = (2 * lax.axis_index("x") + lax.axis_index("y")).astype(jnp.int32).reshape(1)

    proj_rows = jnp.concatenate([w_proj_a.T, w_proj_b.T], axis=1)
    sh_g1, sh_u1, sh_d1, sh_in, sh_proj, sh_out, sh_g2, sh_u2, sh_d2 = _to_bf16(
        [ffn1_w_gate.T, ffn1_w_up.T, ffn1_w_down, w_in.T, proj_rows, w_out, ffn2_w_gate.T, ffn2_w_up.T, ffn2_w_down],
        "weights_to_bf16")

    gather_up1 = _Gather([sh_g1, sh_u1])
    far = jnp.broadcast_to(rel_bias[:, REL_TABLE - 1:REL_TABLE], (A_HEADS, REL_WRAP // 2))
    tv = jnp.concatenate([far, jnp.flip(rel_bias, axis=1), jnp.zeros((A_HEADS, REL_WRAP // 2 - REL_TABLE), F32)], axis=1)
    bias_a = _bias_a_build(tv.reshape(A_HEADS, 1, REL_WRAP), hosted=[gather_up1])
    wg1, wu1 = gather_up1.result
    gather_down1 = _Gather([sh_d1, sh_proj, sh_out])
    gather_win = _Gather([sh_in])
    gather_ffn2_gate = _Gather([sh_g2])
    gather_ffn2_rest = _Gather([sh_u2, sh_d2])

    x0 = x.reshape(t, D_MODEL)
    tgt = loss_target.reshape(t, D_MODEL)
    gam1, gam2, gam3, gam4 = (g.reshape(1, D_MODEL) for g in (ffn1_norm, mix_norm, ffn2_norm, final_norm))

    h1, g1, u1, a1 = _ffn_up(x0, gam1, wg1, wu1, "ffn1_up", hosted=[gather_down1])
    wd1, proj_t, wout = gather_down1.result
    x1 = _ffn_down(x0, a1, wd1, "ffn1_down", hosted=[gather_win])
    (win_t,) = gather_win.result
    h2, qkv_a, qkv_b, gates = _proj_fwd(x1, gam2, win_t, hosted=[gather_ffn2_gate])
    (wg2,) = gather_ffn2_gate.result
    qkv_a3 = qkv_a.reshape(bsz, s_len, QKV_A)
    qkv_b3 = qkv_b.reshape(bsz, s_len, QKV_B)

    bias_b = jnp.asarray(_bias_b())
    sink_rows = jnp.broadcast_to(sinks.reshape(B_Q_HEADS, 1, 1), (B_Q_HEADS, 8, LANES))

    oa = _attn_a_fwd(qkv_a3, bias_a, hosted=[gather_ffn2_rest]).reshape(t, A_WIDTH)
    wu2, wd2 = gather_ffn2_rest.result
    ob = _attn_b_fwd(qkv_b3, bias_b, sink_rows).reshape(t, B_Q_WIDTH)
    x2, ya, yb, mg = _mix_out_fwd(x1, oa, ob, gates, proj_t, wout)
    h3, g2, u2, a2, x3 = _ffn_fwd(x2, gam3, wg2, wu2, wd2, "ffn2_fwd")

    dx2, dg2, du2, db2, dgam3, dgam4, loss_part = _ffn_bwd_head(x3, gam4, tgt, x2, gam3, g2, u2, wg2, wu2, wd2,
                                                                "ffn2_bwd")
    gw_ffn2 = [_mm_tn([dg2], h3, "grad_ffn2_gate"), _mm_tn([du2], h3, "grad_ffn2_up"),
               _mm_tn([a2], db2, "grad_ffn2_down")]
    pairx_ffn2 = _PairExchange(gw_ffn2)
    doa, dob, dgates, gw_out, gw_proj = _mix_out_bwd(dx2, gates, ya, yb, mg, oa, ob, proj_t, wout,
                                                     hosted=[pairx_ffn2])
    psum_ffn2 = _pair_sum(core, gw_ffn2, pairx_ffn2.result, "pair_sum_ffn2")

    chipx_ffn2 = _ChipExchange(psum_ffn2)
    dqa, dka, dva, dbias_a = _attn_a_bwd(qkv_a3, bias_a, doa.reshape(bsz, s_len, A_WIDTH), hosted=[chipx_ffn2])
    pairx_out = _PairExchange([gw_proj, gw_out])
    dqb, dkvb, dsink = _attn_b_bwd(qkv_b3, bias_b, sink_rows, dob.reshape(bsz, s_len, B_Q_WIDTH), hosted=[pairx_out])
    dproj = [dqa.reshape(t, A_WIDTH), dka.reshape(t, A_WIDTH), dva.reshape(t, A_WIDTH), dqb.reshape(t, B_Q_WIDTH),
             dkvb.reshape(t, 2 * B_KV_WIDTH), dgates]

    gw_in = _mm_tn(dproj, h2, "grad_w_in")
    pairx_in = _PairExchange([gw_in])
    drel_lanes = _relbias_grad(dbias_a, hosted=[pairx_in])
    psum_in = _pair_sum(core, [gw_in], pairx_in.result, "pair_sum_w_in")
    psum_out = _pair_sum(core, [gw_proj, gw_out], pairx_out.result, "pair_sum_mix")
    chipx_in = _ChipExchange(psum_in)
    dx1, db1, dgam2 = _proj_bwd(dx2, x1, gam2, dproj, win_t, hosted=[chipx_in])
    chipx_out = _ChipExchange(psum_out)
    gw_d1 = _mm_tn([a1], db1, "grad_ffn1_down", hosted=[chipx_out])

    pairx_d1 = _PairExchange([gw_d1])
    dg1, du1 = _ffn_bwd_act(dx1, g1, u1, wd1, "ffn1_bwd_act", hosted=[pairx_d1])
    psum_d1 = _pair_sum(core, [gw_d1], pairx_d1.result, "pair_sum_ffn1_down")
    chipx_d1 = _ChipExchange(psum_d1)
    gw_g1 = _mm_tn([dg1], h1, "grad_ffn1_gate", hosted=[chipx_d1])
    from_sibling_g1 = _exchange_alone(_PairExchange([gw_g1]), "pair_exchange_ffn1_gate")
    psum_g1 = _pair_sum(core, [gw_g1], from_sibling_g1, "pair_sum_ffn1_gate")
    chipx_g1 = _ChipExchange(psum_g1)
    gw_u1 = _mm_tn([du1], h1, "grad_ffn1_up", hosted=[chipx_g1])
    from_sibling_u1 = _exchange_alone(_PairExchange([gw_u1]), "pair_exchange_ffn1_up")
    psum_u1 = _pair_sum(core, [gw_u1], from_sibling_u1, "pair_sum_ffn1_up")
    chipx_u1 = _ChipExchange(psum_u1)
    dx0, dgam1 = _ffn_bwd_in(dx1, x0, gam1, dg1, du1, wg1, wu1, "ffn1_bwd_in", hosted=[chipx_u1])

    (g_proj,) = _final_sum(chip, psum_out[0:1], chipx_out.result[0:1], "grad_sum_proj")
    grads = {"w_proj_a": g_proj[:, 0:A_WIDTH].T, "w_proj_b": g_proj[:, A_WIDTH:].T}

    def row_of(v):
        return jnp.pad(v.reshape(1, -1), ((0, 0), (0, D_MODEL - v.size)))

    def table_rows(v):
        return jnp.pad(v, ((0, 0), (0, D_MODEL - REL_TABLE)))

    drel_local = jnp.flip(drel_lanes[:, 0, 0:REL_TABLE], axis=1)
    small_part = jnp.concatenate(
        [jnp.sum(dgam1, axis=0, keepdims=True), jnp.sum(dgam2, axis=0, keepdims=True),
         jnp.sum(dgam3, axis=0, keepdims=True), jnp.sum(dgam4, axis=0, keepdims=True),
         row_of(jnp.sum(loss_part)), row_of(dsink[:, 0, 0]), jnp.zeros((2, D_MODEL), F32),
         table_rows(drel_local)], axis=0)
    small = _all_reduce_small(small_part)
    loss = small[4, 0]

    def pack(n1, n2, n3, n4, sk, tb):
        return jnp.concatenate([n1.reshape(1, -1), n2.reshape(1, -1), n3.reshape(1, -1), n4.reshape(1, -1),
                                jnp.zeros((1, D_MODEL), F32), row_of(sk), jnp.zeros((2, D_MODEL), F32), table_rows(tb)],
                               axis=0)

    live = np.zeros((SMALL_ROWS, D_MODEL), np.float32)
    live[0:4] = 1.0
    live[5, 0:B_Q_HEADS] = 1.0
    live[8:16, 0:REL_TABLE] = 1.0
    small_g = small * jnp.asarray(live)
    sw = pack(ffn1_norm, mix_norm, ffn2_norm, final_norm, sinks, rel_bias)
    sm = pack(m_ffn1_norm, m_mix_norm, m_ffn2_norm, m_final_norm, m_sinks, m_rel_bias)
    sv = pack(v_ffn1_norm, v_mix_norm, v_ffn2_norm, v_final_norm, v_sinks, v_rel_bias)
    (sd,), (snm,), (snv,) = _adamw([sw], [small_g], [sm], [sv], "adamw_small")

    def unpack(p):
        return {"ffn1_norm": p[0], "mix_norm": p[1], "ffn2_norm": p[2], "final_norm": p[3],
                "sinks": p[5, 0:B_Q_HEADS], "rel_bias": p[8:16, 0:REL_TABLE]}

    grads.update(unpack(small_g))
    delta, new_m, new_v = unpack(sd), unpack(snm), unpack(snv)

    wmv = {
        "ffn1_w_gate": (ffn1_w_gate, m_ffn1_w_gate, v_ffn1_w_gate), "ffn1_w_up": (ffn1_w_up, m_ffn1_w_up, v_ffn1_w_up),
        "ffn1_w_down": (ffn1_w_down, m_ffn1_w_down, v_ffn1_w_down), "w_in": (w_in, m_w_in, v_w_in),
        "w_proj_a": (w_proj_a, m_w_proj_a, v_w_proj_a), "w_proj_b": (w_proj_b, m_w_proj_b, v_w_proj_b),
        "w_out": (w_out, m_w_out, v_w_out),
        "ffn2_w_gate": (ffn2_w_gate, m_ffn2_w_gate, v_ffn2_w_gate), "ffn2_w_up": (ffn2_w_up, m_ffn2_w_up, v_ffn2_w_up),
        "ffn2_w_down": (ffn2_w_down, m_ffn2_w_down, v_ffn2_w_down),
    }
    row_form_names = ("ffn1_w_gate", "ffn1_w_up", "w_in", "ffn2_w_gate", "ffn2_w_up")

    def form(n, a):
        return a.T if n in row_form_names else a

    def reduced_group(gname, names, psums, recvd, steps):
        gs_, ds_, ms_, vs_ = _adamw_reduced(
            chip, [form(n, wmv[n][0]) for n in names], psums, recvd, [form(n, wmv[n][1]) for n in names],
            [form(n, wmv[n][2]) for n in names], steps, gname)
        for n, g_, d_, m_, v_ in zip(names, gs_, ds_, ms_, vs_):
            grads[n], delta[n], new_m[n], new_v[n] = form(n, g_), form(n, d_), form(n, m_), form(n, v_)

    reduced_group("adamw_ffn", ["ffn1_w_gate", "ffn1_w_up", "ffn1_w_down", "ffn2_w_gate", "ffn2_w_up", "ffn2_w_down"],
                  psum_g1 + psum_u1 + psum_d1 + psum_ffn2,
                  chipx_g1.result + chipx_u1.result + chipx_d1.result + chipx_ffn2.result, 11)
    reduced_group("adamw_in_out", ["w_in", "w_out"], psum_in + psum_out[1:2], chipx_in.result + chipx_out.result[1:2], 2)
    names = ["w_proj_a", "w_proj_b"]
    ds_, ms_, vs_ = _adamw([wmv[n][0] for n in names], [grads[n] for n in names], [wmv[n][1] for n in names],
                           [wmv[n][2] for n in names], "adamw_proj")
    for n, d_, m_, v_ in zip(names, ds_, ms_, vs_):
        delta[n], new_m[n], new_v[n] = d_, m_, v_

    order = ["ffn1_norm", "ffn1_w_gate", "ffn1_w_up", "ffn1_w_down", "mix_norm", "w_in", "rel_bias", "sinks",
             "w_proj_a", "w_proj_b", "w_out", "ffn2_norm", "ffn2_w_gate", "ffn2_w_up", "ffn2_w_down", "final_norm"]
    grad_x = dx0.reshape(bsz, s_len, D_MODEL)
    return (loss, grad_x, *[grads[n] for n in order], *[delta[n] for n in order], *[new_m[n] for n in order],
            *[new_v[n] for n in order])
```

```python
import numpy as np
import jax
import jax.numpy as jnp
from jax import lax
from jax.experimental import pallas as pl
from jax.experimental.pallas import tpu as pltpu

F32 = jnp.float32
BF16 = jnp.bfloat16

D_MODEL = 1024
D_FF = 2816
CHUNK = 64
D_HEAD = 64
A_HEADS = 8
A_PREV = 8
MAX_REL = 128
B_Q_HEADS = 8
B_KV_HEADS = 2
B_GROUP = B_Q_HEADS // B_KV_HEADS
B_PREV = 2
REL_TABLE = (CHUNK - 1) + MAX_REL + 1
A_WIDTH = A_HEADS * D_HEAD
B_Q_WIDTH = B_Q_HEADS * D_HEAD
B_KV_WIDTH = B_KV_HEADS * D_HEAD
QKV_A = 3 * A_WIDTH
QKV_B = B_Q_WIDTH + 2 * B_KV_WIDTH
IN_WIDTH = QKV_A + QKV_B + 2 * D_MODEL
EPS = 1e-6
NEG_INF = -1e30
SCALE = 1.0 / 8.0

ADAM_LR = 0.001
ADAM_B1 = 0.9
ADAM_B2 = 0.999
ADAM_EPS = 1e-08
ADAM_WD = 0.01
ADAM_STEP = 10

N_DEV = 8
N_CHIP = 4
MESH = pl.DeviceIdType.MESH

LANES = 128
TQ = 256
TM = 256
FC = 256
VMEM_LIMIT = 56 << 20


def _cparams(sem, vmem=VMEM_LIMIT):
    return pltpu.CompilerParams(dimension_semantics=sem, vmem_limit_bytes=vmem)


def _dot_nt(a, b):
    return lax.dot_general(a, b, (((1,), (1,)), ((), ())), preferred_element_type=F32)


def _dot_nn(a, b):
    return lax.dot_general(a, b, (((1,), (0,)), ((), ())), preferred_element_type=F32)


def _dot_tn(a, b):
    return lax.dot_general(a, b, (((0,), (0,)), ((), ())), preferred_element_type=F32)


def _resident(shape):
    nd = len(shape)
    return pl.BlockSpec(shape, lambda *_: (0,) * nd, pipeline_mode=pl.Buffered(1))


def _rows(tm, width):
    return pl.BlockSpec((tm, width), lambda i: (i, 0))


def _colsum8(v):
    tm, n = v.shape
    return jnp.sum(v.reshape(tm // 8, 8, n), axis=0)


def _rms(x):
    r = lax.rsqrt(jnp.mean(x * x, axis=-1, keepdims=True) + EPS)
    return x * r, r


def _rms_bwd(dh, xh, r, gamma):
    dxh = dh * gamma
    dx = r * (dxh - xh * jnp.mean(dxh * xh, axis=-1, keepdims=True))
    return dx, _colsum8(dh * xh)


def _hbm():
    return pl.BlockSpec(memory_space=pltpu.HBM)


def _call(body, *, name, grid, in_specs, out_specs, out_shape, args, sem, scratch_shapes=(), hosted=()):
    in_specs, out_specs, out_shape = list(in_specs), list(out_specs), list(out_shape)
    scratch_shapes = list(scratch_shapes)
    if not hosted:
        return pl.pallas_call(body, name=name, grid=grid, in_specs=in_specs, out_specs=out_specs, out_shape=out_shape,
                              scratch_shapes=scratch_shapes, compiler_params=_cparams(sem))(*args)
    n_in, n_out, n_scr = len(in_specs), len(out_specs), len(scratch_shapes)
    x_in = [a for x in hosted for a in x.inputs]
    x_out = [s for x in hosted for s in x.out_shape]
    x_scr = [s for x in hosted for s in x.scratch]
    steps = int(np.prod(grid))
    forward_step = max(steps - 3, 0)
    relay_step = min((5 * steps) // 8, forward_step)

    def wrapped(*refs):
        pos = [0]

        def take(k):
            pos[0] += k
            return refs[pos[0] - k:pos[0]]

        ins, xin, outs, xout, scr, xscr = (take(k) for k in (n_in, len(x_in), n_out, len(x_out), n_scr, len(x_scr)))
        step = 0
        for axis, extent in enumerate(grid):
            step = step * extent + pl.program_id(axis)
        own, oi, oo, osc = [], 0, 0, 0
        for x in hosted:
            own.append((xin[oi:oi + len(x.inputs)], xout[oo:oo + len(x.out_shape)], xscr[osc:osc + len(x.scratch)]))
            oi, oo, osc = oi + len(x.inputs), oo + len(x.out_shape), osc + len(x.scratch)

        def phase(method):
            for x, (i_, o_, s_) in zip(hosted, own):
                getattr(x, method)(i_, o_, s_)

        pl.when(step == 0)(lambda: phase("start"))
        body(*ins, *outs, *scr)
        pl.when(step == relay_step)(lambda: phase("relay"))
        pl.when(step == forward_step)(lambda: phase("forward"))
        pl.when(step == steps - 1)(lambda: phase("finish"))

    res = pl.pallas_call(
        wrapped, name=name, grid=grid, in_specs=in_specs + [_hbm()] * len(x_in),
        out_specs=out_specs + [_hbm()] * len(x_out), out_shape=out_shape + x_out,
        scratch_shapes=scratch_shapes + x_scr, compiler_params=_cparams(("arbitrary",) * len(grid)))(*args, *x_in)
    rest = list(res[n_out:])
    for x in hosted:
        x.result, rest = rest[:len(x.out_shape)], rest[len(x.out_shape):]
    return list(res[:n_out])


def _to_bf16(arrays, name):
    n = len(arrays)

    def body(*refs):
        for k in range(n):
            refs[n + k][...] = refs[k][...].astype(BF16)

    specs = [pl.BlockSpec(a.shape, lambda i: (0, 0)) for a in arrays]
    return pl.pallas_call(
        body, name=name, grid=(1,), in_specs=specs, out_specs=specs,
        out_shape=[jax.ShapeDtypeStruct(a.shape, BF16) for a in arrays],
        compiler_params=_cparams(("arbitrary",)))(*arrays)


def _ffn_fwd(x, gamma, wg_t, wu_t, wd, name, hosted=()):
    t = x.shape[0]
    f = wg_t.shape[0]

    def body(x_ref, gam_ref, wg_ref, wu_ref, wd_ref, h_ref, g_ref, u_ref, a_ref, y_ref):
        xv = x_ref[...]
        xh, _ = _rms(xv)
        h = (xh * gam_ref[...]).astype(BF16)
        h_ref[...] = h
        for j in range(f // FC):
            sl = slice(j * FC, (j + 1) * FC)
            g = _dot_nt(h, wg_ref[sl, :])
            u = _dot_nt(h, wu_ref[sl, :])
            g_ref[:, sl] = g.astype(BF16)
            u_ref[:, sl] = u.astype(BF16)
            a_ref[:, sl] = (g * jax.nn.sigmoid(g) * u).astype(BF16)
        y_ref[...] = xv + 0.5 * _dot_nn(a_ref[...], wd_ref[...])

    return _call(
        body,
        name=name,
        grid=(t // TM,),
        in_specs=[_rows(TM, D_MODEL), _resident((1, D_MODEL)), _resident((f, D_MODEL)), _resident((f, D_MODEL)),
                  _resident((f, D_MODEL))],
        out_specs=[_rows(TM, D_MODEL), _rows(TM, f), _rows(TM, f), _rows(TM, f), _rows(TM, D_MODEL)],
        out_shape=[jax.ShapeDtypeStruct((t, D_MODEL), BF16), jax.ShapeDtypeStruct((t, f), BF16),
                   jax.ShapeDtypeStruct((t, f), BF16), jax.ShapeDtypeStruct((t, f), BF16),
                   jax.ShapeDtypeStruct((t, D_MODEL), F32)],
        args=(x, gamma, wg_t, wu_t, wd), sem=("parallel",), hosted=hosted)


def _ffn_up(x, gamma, wg_t, wu_t, name, hosted=()):
    t = x.shape[0]
    f = wg_t.shape[0]

    def body(x_ref, gam_ref, wg_ref, wu_ref, h_ref, g_ref, u_ref, a_ref):
        xh, _ = _rms(x_ref[...])
        h = (xh * gam_ref[...]).astype(BF16)
        h_ref[...] = h
        for j in range(f // FC):
            sl = slice(j * FC, (j + 1) * FC)
            g = _dot_nt(h, wg_ref[sl, :])
            u = _dot_nt(h, wu_ref[sl, :])
            g_ref[:, sl] = g.astype(BF16)
            u_ref[:, sl] = u.astype(BF16)
            a_ref[:, sl] = (g * jax.nn.sigmoid(g) * u).astype(BF16)

    return _call(
        body,
        name=name,
        grid=(t // TM,),
        in_specs=[_rows(TM, D_MODEL), _resident((1, D_MODEL)), _resident((f, D_MODEL)), _resident((f, D_MODEL))],
        out_specs=[_rows(TM, D_MODEL), _rows(TM, f), _rows(TM, f), _rows(TM, f)],
        out_shape=[jax.ShapeDtypeStruct((t, D_MODEL), BF16), jax.ShapeDtypeStruct((t, f), BF16),
                   jax.ShapeDtypeStruct((t, f), BF16), jax.ShapeDtypeStruct((t, f), BF16)],
        args=(x, gamma, wg_t, wu_t), sem=("parallel",), hosted=hosted)


def _ffn_down(x, a_act, wd, name, hosted=()):
    t = x.shape[0]
    f = wd.shape[0]

    def body(x_ref, a_ref, wd_ref, y_ref):
        y_ref[...] = x_ref[...] + 0.5 * _dot_nn(a_ref[...], wd_ref[...])

    return _call(
        body,
        name=name,
        grid=(t // TM,),
        in_specs=[_rows(TM, D_MODEL), _rows(TM, f), _resident((f, D_MODEL))],
        out_specs=[_rows(TM, D_MODEL)],
        out_shape=[jax.ShapeDtypeStruct((t, D_MODEL), F32)],
        args=(x, a_act, wd), sem=("parallel",), hosted=hosted)[0]


def _ffn_bwd_head(y, gamma_f, target, x, gamma, g_act, u_act, wg_t, wu_t, wd, name):
    t = x.shape[0]
    f = wg_t.shape[0]

    def body(y_ref, gamf_ref, t_ref, x_ref, gam_ref, g_ref, u_ref, wg_ref, wu_ref, wd_ref, dx_ref, dg_ref, du_ref,
             db_ref, dgam_ref, dgamf_ref, loss_ref):
        yh, ry = _rms(y_ref[...])
        gam_f = gamf_ref[...]
        e = yh * gam_f - t_ref[...]
        dv, dgam_f = _rms_bwd(e * (1.0 / D_MODEL), yh, ry, gam_f)
        db = (0.5 * dv).astype(BF16)
        db_ref[...] = db
        for j in range(f // FC):
            sl = slice(j * FC, (j + 1) * FC)
            da = _dot_nt(db, wd_ref[sl, :])
            g = g_ref[:, sl].astype(F32)
            u = u_ref[:, sl].astype(F32)
            s = jax.nn.sigmoid(g)
            dg_ref[:, sl] = (da * u * (s * (1.0 + g * (1.0 - s)))).astype(BF16)
            du_ref[:, sl] = (da * (g * s)).astype(BF16)
        dh = _dot_nn(dg_ref[...], wg_ref[...]) + _dot_nn(du_ref[...], wu_ref[...])
        xh, r = _rms(x_ref[...])
        dxn, dgam = _rms_bwd(dh, xh, r, gam_ref[...])
        dx_ref[...] = dv + dxn

        @pl.when(pl.program_id(0) == 0)
        def _():
            dgam_ref[...] = jnp.zeros_like(dgam_ref)
            dgamf_ref[...] = jnp.zeros_like(dgamf_ref)
            loss_ref[...] = jnp.zeros_like(loss_ref)

        dgam_ref[...] += dgam
        dgamf_ref[...] += dgam_f
        loss_ref[...] += _colsum8(e * e) * (0.5 / D_MODEL)

    acc = pl.BlockSpec((8, D_MODEL), lambda i: (0, 0))
    return _call(
        body,
        name=name,
        grid=(t // TM,),
        in_specs=[_rows(TM, D_MODEL), _resident((1, D_MODEL)), _rows(TM, D_MODEL), _rows(TM, D_MODEL),
                  _resident((1, D_MODEL)), _rows(TM, f), _rows(TM, f),
                  _resident((f, D_MODEL)), _resident((f, D_MODEL)), _resident((f, D_MODEL))],
        out_specs=[_rows(TM, D_MODEL), _rows(TM, f), _rows(TM, f), _rows(TM, D_MODEL), acc, acc, acc],
        out_shape=[jax.ShapeDtypeStruct((t, D_MODEL), F32), jax.ShapeDtypeStruct((t, f), BF16),
                   jax.ShapeDtypeStruct((t, f), BF16), jax.ShapeDtypeStruct((t, D_MODEL), BF16),
                   jax.ShapeDtypeStruct((8, D_MODEL), F32), jax.ShapeDtypeStruct((8, D_MODEL), F32),
                   jax.ShapeDtypeStruct((8, D_MODEL), F32)],
        args=(y, gamma_f, target, x, gamma, g_act, u_act, wg_t, wu_t, wd), sem=("arbitrary",))


def _ffn_bwd_act(d, g_act, u_act, wd, name, hosted=()):
    t = d.shape[0]
    f = wd.shape[0]

    def body(d_ref, g_ref, u_ref, wd_ref, dg_ref, du_ref):
        db = (0.5 * d_ref[...]).astype(BF16)
        for j in range(f // FC):
            sl = slice(j * FC, (j + 1) * FC)
            da = _dot_nt(db, wd_ref[sl, :])
            g = g_ref[:, sl].astype(F32)
            u = u_ref[:, sl].astype(F32)
            s = jax.nn.sigmoid(g)
            dg_ref[:, sl] = (da * u * (s * (1.0 + g * (1.0 - s)))).astype(BF16)
            du_ref[:, sl] = (da * (g * s)).astype(BF16)

    return _call(
        body,
        name=name,
        grid=(t // TM,),
        in_specs=[_rows(TM, D_MODEL), _rows(TM, f), _rows(TM, f), _resident((f, D_MODEL))],
        out_specs=[_rows(TM, f), _rows(TM, f)],
        out_shape=[jax.ShapeDtypeStruct((t, f), BF16), jax.ShapeDtypeStruct((t, f), BF16)],
        args=(d, g_act, u_act, wd), sem=("parallel",), hosted=hosted)


def _ffn_bwd_in(d, x, gamma, dg, du, wg_t, wu_t, name, hosted=()):
    t = x.shape[0]
    f = wg_t.shape[0]

    def body(d_ref, x_ref, gam_ref, dg_ref, du_ref, wg_ref, wu_ref, dx_ref, dgam_ref):
        dh = _dot_nn(dg_ref[...], wg_ref[...]) + _dot_nn(du_ref[...], wu_ref[...])
        xh, r = _rms(x_ref[...])
        dxn, dgam = _rms_bwd(dh, xh, r, gam_ref[...])
        dx_ref[...] = d_ref[...] + dxn

        @pl.when(pl.program_id(0) == 0)
        def _():
            dgam_ref[...] = jnp.zeros_like(dgam_ref)

        dgam_ref[...] += dgam

    return _call(
        body,
        name=name,
        grid=(t // TM,),
        in_specs=[_rows(TM, D_MODEL), _rows(TM, D_MODEL), _resident((1, D_MODEL)), _rows(TM, f), _rows(TM, f),
                  _resident((f, D_MODEL)), _resident((f, D_MODEL))],
        out_specs=[_rows(TM, D_MODEL), pl.BlockSpec((8, D_MODEL), lambda i: (0, 0))],
        out_shape=[jax.ShapeDtypeStruct((t, D_MODEL), F32), jax.ShapeDtypeStruct((8, D_MODEL), F32)],
        args=(d, x, gamma, dg, du, wg_t, wu_t), sem=("arbitrary",), hosted=hosted)


def _mm_tn(pieces, b, name, tile=256, hosted=()):
    t, n = b.shape
    npc = len(pieces)
    counts = [p.shape[1] // tile for p in pieces]
    los = [sum(counts[:k]) for k in range(npc)]
    total = sum(counts)

    def body(*refs):
        a_refs, b_ref, o_ref = refs[:npc], refs[npc], refs[npc + 1]
        i = pl.program_id(0)
        for k in range(npc):
            @pl.when(jnp.logical_and(i >= los[k], i < los[k] + counts[k]))
            def _(k=k):
                o_ref[...] = _dot_tn(a_refs[k][...], b_ref[...]).astype(BF16)

    def a_spec(k):
        return pl.BlockSpec((t, tile), lambda i: (0, jnp.clip(i - los[k], 0, counts[k] - 1)))

    return _call(
        body,
        name=name,
        grid=(total,),
        in_specs=[a_spec(k) for k in range(npc)] + [_resident((t, n))],
        out_specs=[pl.BlockSpec((tile, n), lambda i: (i, 0))],
        out_shape=[jax.ShapeDtypeStruct((total * tile, n), BF16)],
        args=(*pieces, b), sem=("parallel",), hosted=hosted)[0]


def _proj_fwd(x, gamma, win_t, hosted=()):
    t = x.shape[0]

    def body(x_ref, gam_ref, w_ref, h_ref, qa_ref, qb_ref, gt_ref):
        xh, _ = _rms(x_ref[...])
        h = (xh * gam_ref[...]).astype(BF16)
        h_ref[...] = h
        for j in range(QKV_A // FC):
            qa_ref[:, j * FC:(j + 1) * FC] = _dot_nt(h, w_ref[j * FC:(j + 1) * FC, :]).astype(BF16)
        for j in range(QKV_B // FC):
            lo = QKV_A + j * FC
            qb_ref[:, j * FC:(j + 1) * FC] = _dot_nt(h, w_ref[lo:lo + FC, :]).astype(BF16)
        for j in range(2 * D_MODEL // FC):
            lo = QKV_A + QKV_B + j * FC
            gt_ref[:, j * FC:(j + 1) * FC] = _dot_nt(h, w_ref[lo:lo + FC, :])

    return _call(
        body,
        name="proj_fwd",
        grid=(t // TM,),
        in_specs=[_rows(TM, D_MODEL), _resident((1, D_MODEL)), _resident((IN_WIDTH, D_MODEL))],
        out_specs=[_rows(TM, D_MODEL), _rows(TM, QKV_A), _rows(TM, QKV_B), _rows(TM, 2 * D_MODEL)],
        out_shape=[jax.ShapeDtypeStruct((t, D_MODEL), BF16), jax.ShapeDtypeStruct((t, QKV_A), BF16),
                   jax.ShapeDtypeStruct((t, QKV_B), BF16), jax.ShapeDtypeStruct((t, 2 * D_MODEL), F32)],
        args=(x, gamma, win_t), sem=("parallel",), hosted=hosted)


def _proj_bwd(d, x, gamma, pieces, win_t, hosted=()):
    t = x.shape[0]
    npc = len(pieces)
    widths = [p.shape[1] for p in pieces]
    los = [sum(widths[:k]) for k in range(npc)]

    def body(*refs):
        d_ref, x_ref, gam_ref = refs[:3]
        p_refs = refs[3:3 + npc]
        w_ref, dx_ref, db_ref, dgam_ref = refs[3 + npc:]
        dh = _dot_nn(p_refs[0][...], w_ref[0:widths[0], :])
        for k in range(1, npc):
            dh += _dot_nn(p_refs[k][...], w_ref[los[k]:los[k] + widths[k], :])
        xh, r = _rms(x_ref[...])
        dxn, dgam = _rms_bwd(dh, xh, r, gam_ref[...])
        dx = d_ref[...] + dxn
        dx_ref[...] = dx
        db_ref[...] = (0.5 * dx).astype(BF16)

        @pl.when(pl.program_id(0) == 0)
        def _():
            dgam_ref[...] = jnp.zeros_like(dgam_ref)

        dgam_ref[...] += dgam

    return _call(
        body,
        name="proj_bwd",
        grid=(t // TM,),
        in_specs=[_rows(TM, D_MODEL), _rows(TM, D_MODEL), _resident((1, D_MODEL))] + [_rows(TM, w) for w in widths]
        + [_resident((IN_WIDTH, D_MODEL))],
        out_specs=[_rows(TM, D_MODEL), _rows(TM, D_MODEL), pl.BlockSpec((8, D_MODEL), lambda i: (0, 0))],
        out_shape=[jax.ShapeDtypeStruct((t, D_MODEL), F32), jax.ShapeDtypeStruct((t, D_MODEL), BF16),
                   jax.ShapeDtypeStruct((8, D_MODEL), F32)],
        args=(d, x, gamma, *pieces, win_t), sem=("arbitrary",), hosted=hosted)


def _lane_half(shape):
    return lax.broadcasted_iota(jnp.int32, shape, len(shape) - 1) // D_HEAD


def _band_weights(q, kk, bias, sink, qs, pad):
    s = _dot_nt(q, kk) + bias
    if qs is not None:
        col = lax.broadcasted_iota(jnp.int32, s.shape, 1)
        s = jnp.where(col + qs >= pad, s, NEG_INF)
    m = jnp.max(s, axis=-1, keepdims=True)
    if sink is not None:
        m = jnp.maximum(m, sink)
    return jnp.exp(s - m), m


def _weighted_values(p, vv_ones, sink, m):
    r = _dot_nn(p.astype(BF16), vv_ones)
    den = r[:, LANES:2 * LANES]
    if sink is not None:
        den = den + jnp.exp(sink - m)
    return r[:, 0:LANES] / den


def _band_softmax(q, kk, bias, sink, qs, pad):
    p, m = _band_weights(q, kk, bias, sink, qs, pad)
    den = jnp.sum(p, axis=-1, keepdims=True)
    if sink is not None:
        den = den + jnp.exp(sink - m)
    return p, m, 1.0 / den


def _fill_padded(dst, src, pad):
    dst[0:pad, :] = jnp.zeros((pad,) + dst.shape[1:], dst.dtype)
    dst[pad:, :] = src


FWD_PAIRS = 4
BWD_PAIRS = 4


def _attn_a_fwd(qkv, bias, hosted=()):
    bsz, s_len, _ = qkv.shape
    pad = A_PREV * CHUNK
    band = TQ + pad
    pp = FWD_PAIRS
    w = pp * LANES
    nb = A_WIDTH // w

    def body(q_ref, k_ref, v_ref, b_ref, o_ref, kp, vp):
        i = pl.program_id(2)

        @pl.when(i == 0)
        def _():
            _fill_padded(kp, k_ref[...], pad)
            _fill_padded(vp, v_ref[...], pad)

        qs = pl.multiple_of(i * TQ, TQ)
        half = _lane_half((1, LANES))

        ones = jnp.ones((band, LANES), BF16)

        def block(masked):
            for pr in range(pp):
                sl = slice(pr * LANES, (pr + 1) * LANES)
                kk = kp[pl.ds(qs, band), sl]
                vv = jnp.concatenate([vp[pl.ds(qs, band), sl], ones], axis=1)
                q = q_ref[:, sl] * SCALE
                outs = []
                for j in range(2):
                    qm = jnp.where(half == j, q, jnp.zeros_like(q))
                    p, m = _band_weights(qm, kk, b_ref[2 * pr + j], None, qs if masked else None, pad)
                    outs.append(_weighted_values(p, vv, None, m))
                o_ref[:, sl] = jnp.where(half == 0, outs[0], outs[1]).astype(BF16)

        pl.when(i < pad // TQ)(lambda: block(True))
        pl.when(i >= pad // TQ)(lambda: block(False))

    return _call(
        body,
        name="attn_a_fwd",
        grid=(bsz, nb, s_len // TQ),
        in_specs=[pl.BlockSpec((None, TQ, w), lambda b, g, i: (b, i, g)),
                  pl.BlockSpec((None, s_len, w), lambda b, g, i: (b, 0, nb + g)),
                  pl.BlockSpec((None, s_len, w), lambda b, g, i: (b, 0, 2 * nb + g)),
                  pl.BlockSpec((2 * pp, TQ, band), lambda b, g, i: (g, 0, 0))],
        out_specs=[pl.BlockSpec((None, TQ, w), lambda b, g, i: (b, i, g))],
        out_shape=[jax.ShapeDtypeStruct((bsz, s_len, A_WIDTH), BF16)],
        scratch_shapes=[pltpu.VMEM((pad + s_len, w), BF16), pltpu.VMEM((pad + s_len, w), BF16)],
        args=(qkv, qkv, qkv, bias), sem=("arbitrary", "arbitrary", "arbitrary"), hosted=hosted)[0]


def _attn_a_bwd(qkv, bias, do, hosted=()):
    bsz, s_len, _ = qkv.shape
    pad = A_PREV * CHUNK
    band = TQ + pad
    n_i = s_len // TQ
    pp = BWD_PAIRS
    w = pp * LANES
    nb = A_WIDTH // w

    def body(q_ref, k_ref, v_ref, b_ref, do_ref, dq_ref, dk_ref, dv_ref, dbias_ref, kp, vp, dk_acc, dv_acc):
        b = pl.program_id(1)
        i = pl.program_id(2)

        @pl.when(i == 0)
        def _():
            _fill_padded(kp, k_ref[...], pad)
            _fill_padded(vp, v_ref[...], pad)
            dk_acc[...] = jnp.zeros_like(dk_acc)
            dv_acc[...] = jnp.zeros_like(dv_acc)

        @pl.when(jnp.logical_and(b == 0, i == 0))
        def _():
            dbias_ref[...] = jnp.zeros_like(dbias_ref)

        qs = pl.multiple_of(i * TQ, TQ)
        half = _lane_half((1, LANES))

        def block(masked):
            for pr in range(pp):
                sl = slice(pr * LANES, (pr + 1) * LANES)
                kk = kp[pl.ds(qs, band), sl]
                vv = vp[pl.ds(qs, band), sl]
                q = q_ref[:, sl] * SCALE
                dd = do_ref[:, sl]
                dqs, dks, dvs = [], [], []
                for j in range(2):
                    qm = jnp.where(half == j, q, jnp.zeros_like(q))
                    dm = jnp.where(half == j, dd, jnp.zeros_like(dd))
                    p, _, inv = _band_softmax(qm, kk, b_ref[2 * pr + j], None, qs if masked else None, pad)
                    pn = p * inv
                    dp = _dot_nt(dm, vv)
                    delta = jnp.sum(pn * dp, axis=-1, keepdims=True)
                    ds = pn * (dp - delta)
                    dbias_ref[2 * pr + j] += ds[:, band - REL_COLS:]
                    dsb = ds.astype(BF16)
                    dqs.append(_dot_nn(dsb, kk))
                    dks.append(_dot_tn(dsb, q))
                    dvs.append(_dot_tn(pn.astype(BF16), dd))
                dq_ref[:, sl] = (jnp.where(half == 0, dqs[0], dqs[1]) * SCALE).astype(BF16)
                dk_acc[pl.ds(qs, band), sl] += jnp.where(half == 0, dks[0], dks[1])
                dv_acc[pl.ds(qs, band), sl] += jnp.where(half == 0, dvs[0], dvs[1])

        pl.when(i < pad // TQ)(lambda: block(True))
        pl.when(i >= pad // TQ)(lambda: block(False))

        @pl.when(i == n_i - 1)
        def _():
            dk_ref[...] = dk_acc[pad:, :].astype(BF16)
            dv_ref[...] = dv_acc[pad:, :].astype(BF16)

    qspec = pl.BlockSpec((None, TQ, w), lambda g, b, i: (b, i, g))
    kvout = pl.BlockSpec((None, s_len, w), lambda g, b, i: (b, 0, g))
    wide = jax.ShapeDtypeStruct((bsz, s_len, A_WIDTH), BF16)
    return _call(
        body,
        name="attn_a_bwd",
        grid=(nb, bsz, n_i),
        in_specs=[qspec,
                  pl.BlockSpec((None, s_len, w), lambda g, b, i: (b, 0, nb + g)),
                  pl.BlockSpec((None, s_len, w), lambda g, b, i: (b, 0, 2 * nb + g)),
                  pl.BlockSpec((2 * pp, TQ, band), lambda g, b, i: (g, 0, 0)),
                  qspec],
        out_specs=[qspec, kvout, kvout, pl.BlockSpec((2 * pp, TQ, REL_COLS), lambda g, b, i: (g, 0, 0))],
        out_shape=[wide, wide, wide, jax.ShapeDtypeStruct((A_HEADS, TQ, REL_COLS), F32)],
        scratch_shapes=[pltpu.VMEM((pad + s_len, w), BF16), pltpu.VMEM((pad + s_len, w), BF16),
                        pltpu.VMEM((pad + s_len, w), F32), pltpu.VMEM((pad + s_len, w), F32)],
        args=(qkv, qkv, qkv, bias, do), sem=("arbitrary", "arbitrary", "arbitrary"), hosted=hosted)


def _fill_padded_dup(dst, src, pad, h, half):
    other = pltpu.roll(src, D_HEAD, 1)
    _fill_padded(dst, jnp.where(half == h, src, other), pad)


def _attn_b_fwd(qkv, bias, sink):
    bsz, s_len, _ = qkv.shape
    pad = B_PREV * CHUNK
    band = TQ + pad
    kcol = B_Q_WIDTH // LANES
    npair = B_Q_HEADS // 2

    def body(q_ref, k_ref, v_ref, b_ref, s_ref, o_ref, kp, vp):
        i = pl.program_id(1)
        half = _lane_half((1, LANES))

        @pl.when(i == 0)
        def _():
            for h in range(B_KV_HEADS):
                _fill_padded_dup(kp.at[h], k_ref[...], pad, h, half)
                _fill_padded_dup(vp.at[h], v_ref[...], pad, h, half)

        qs = pl.multiple_of(i * TQ, TQ)

        ones = jnp.ones((band, LANES), BF16)

        def block(masked):
            for pr in range(npair):
                h = pr // (B_GROUP // 2)
                sl = slice(pr * LANES, (pr + 1) * LANES)
                kk = kp[h, pl.ds(qs, band), :]
                vv = jnp.concatenate([vp[h, pl.ds(qs, band), :], ones], axis=1)
                q = q_ref[:, sl] * SCALE
                outs = []
                for j in range(2):
                    qm = jnp.where(half == j, q, jnp.zeros_like(q))
                    sink = s_ref[2 * pr + j][0:1, 0:1]
                    p, m = _band_weights(qm, kk, b_ref[2 * pr + j], sink, qs if masked else None, pad)
                    outs.append(_weighted_values(p, vv, sink, m))
                o_ref[:, sl] = jnp.where(half == 0, outs[0], outs[1]).astype(BF16)

        pl.when(i < -(-pad // TQ))(lambda: block(True))
        pl.when(i >= -(-pad // TQ))(lambda: block(False))

    return pl.pallas_call(
        body,
        name="attn_b_fwd",
        grid=(bsz, s_len // TQ),
        in_specs=[pl.BlockSpec((None, TQ, B_Q_WIDTH), lambda b, i: (b, i, 0)),
                  pl.BlockSpec((None, s_len, LANES), lambda b, i: (b, 0, kcol)),
                  pl.BlockSpec((None, s_len, LANES), lambda b, i: (b, 0, kcol + 1)),
                  pl.BlockSpec((B_Q_HEADS, TQ, band), lambda b, i: (0, 0, 0)),
                  pl.BlockSpec((B_Q_HEADS, 8, LANES), lambda b, i: (0, 0, 0))],
        out_specs=pl.BlockSpec((None, TQ, B_Q_WIDTH), lambda b, i: (b, i, 0)),
        out_shape=jax.ShapeDtypeStruct((bsz, s_len, B_Q_WIDTH), BF16),
        scratch_shapes=[pltpu.VMEM((B_KV_HEADS, pad + s_len, LANES), BF16),
                        pltpu.VMEM((B_KV_HEADS, pad + s_len, LANES), BF16)],
        compiler_params=_cparams(("arbitrary", "arbitrary")),
    )(qkv, qkv, qkv, bias, sink)


def _attn_b_bwd(qkv, bias, sink, do, hosted=()):
    bsz, s_len, _ = qkv.shape
    pad = B_PREV * CHUNK
    band = TQ + pad
    kcol = B_Q_WIDTH // LANES
    n_i = s_len // TQ
    pp = B_GROUP // 2

    def body(q_ref, k_ref, v_ref, b_ref, s_ref, do_ref, dq_ref, dkv_ref, dsink_ref, kp, vp, dk_acc, dv_acc):
        b = pl.program_id(0)
        i = pl.program_id(1)
        half = _lane_half((1, LANES))

        @pl.when(i == 0)
        def _():
            for h in range(B_KV_HEADS):
                _fill_padded_dup(kp.at[h], k_ref[...], pad, h, half)
                _fill_padded_dup(vp.at[h], v_ref[...], pad, h, half)
            dk_acc[...] = jnp.zeros_like(dk_acc)
            dv_acc[...] = jnp.zeros_like(dv_acc)

        @pl.when(jnp.logical_and(b == 0, i == 0))
        def _():
            dsink_ref[...] = jnp.zeros_like(dsink_ref)

        qs = pl.multiple_of(i * TQ, TQ)

        def block(masked):
            heads_dk, heads_dv = [], []
            for h in range(B_KV_HEADS):
                kk = kp[h, pl.ds(qs, band), :]
                vv = vp[h, pl.ds(qs, band), :]
                dk2 = jnp.zeros((band, LANES), F32)
                dv2 = jnp.zeros((band, LANES), F32)
                for pr in range(pp * h, pp * (h + 1)):
                    sl = slice(pr * LANES, (pr + 1) * LANES)
                    q = q_ref[:, sl] * SCALE
                    dd = do_ref[:, sl]
                    dqs, dks, dvs = [], [], []
                    for j in range(2):
                        qm = jnp.where(half == j, q, jnp.zeros_like(q))
                        dm = jnp.where(half == j, dd, jnp.zeros_like(dd))
                        sink = s_ref[2 * pr + j][0:1, 0:1]
                        p, m, inv = _band_softmax(qm, kk, b_ref[2 * pr + j], sink, qs if masked else None, pad)
                        pn = p * inv
                        dp = _dot_nt(dm, vv)
                        delta = jnp.sum(pn * dp, axis=-1, keepdims=True)
                        ds = pn * (dp - delta)
                        dsb = ds.astype(BF16)
                        dqs.append(_dot_nn(dsb, kk))
                        dks.append(_dot_tn(dsb, q))
                        dvs.append(_dot_tn(pn.astype(BF16), dd))
                        dsk = jnp.sum(-(jnp.exp(sink - m) * inv) * delta, axis=0, keepdims=True)
                        dsink_ref[2 * pr + j] += jnp.broadcast_to(dsk, (8, LANES))
                    dq_ref[:, sl] = (jnp.where(half == 0, dqs[0], dqs[1]) * SCALE).astype(BF16)
                    dk2 = dk2 + jnp.where(half == 0, dks[0], dks[1])
                    dv2 = dv2 + jnp.where(half == 0, dvs[0], dvs[1])
                heads_dk.append(dk2 + pltpu.roll(dk2, D_HEAD, 1))
                heads_dv.append(dv2 + pltpu.roll(dv2, D_HEAD, 1))
            dk_acc[pl.ds(qs, band), :] += jnp.where(half == 0, heads_dk[0], heads_dk[1])
            dv_acc[pl.ds(qs, band), :] += jnp.where(half == 0, heads_dv[0], heads_dv[1])

        pl.when(i < -(-pad // TQ))(lambda: block(True))
        pl.when(i >= -(-pad // TQ))(lambda: block(False))

        @pl.when(i == n_i - 1)
        def _():
            dkv_ref[:, 0:LANES] = dk_acc[pad:, :].astype(BF16)
            dkv_ref[:, LANES:2 * LANES] = dv_acc[pad:, :].astype(BF16)

    qspec = pl.BlockSpec((None, TQ, B_Q_WIDTH), lambda b, i: (b, i, 0))
    return _call(
        body,
        name="attn_b_bwd",
        grid=(bsz, n_i),
        in_specs=[qspec,
                  pl.BlockSpec((None, s_len, LANES), lambda b, i: (b, 0, kcol)),
                  pl.BlockSpec((None, s_len, LANES), lambda b, i: (b, 0, kcol + 1)),
                  pl.BlockSpec((B_Q_HEADS, TQ, band), lambda b, i: (0, 0, 0)),
                  pl.BlockSpec((B_Q_HEADS, 8, LANES), lambda b, i: (0, 0, 0)),
                  qspec],
        out_specs=[qspec, pl.BlockSpec((None, s_len, 2 * LANES), lambda b, i: (b, 0, 0)),
                   pl.BlockSpec((B_Q_HEADS, 8, LANES), lambda b, i: (0, 0, 0))],
        out_shape=[jax.ShapeDtypeStruct((bsz, s_len, B_Q_WIDTH), BF16),
                   jax.ShapeDtypeStruct((bsz, s_len, 2 * B_KV_WIDTH), BF16),
                   jax.ShapeDtypeStruct((B_Q_HEADS, 8, LANES), F32)],
        scratch_shapes=[pltpu.VMEM((B_KV_HEADS, pad + s_len, LANES), BF16),
                        pltpu.VMEM((B_KV_HEADS, pad + s_len, LANES), BF16),
                        pltpu.VMEM((pad + s_len, LANES), F32), pltpu.VMEM((pad + s_len, LANES), F32)],
        args=(qkv, qkv, qkv, bias, sink, do), sem=("arbitrary", "arbitrary"), hosted=hosted)


REL_COLS = 3 * 128
REL_WRAP = 512


def _bias_a_build(tv, hosted=()):
    h = tv.shape[0]
    pad = A_PREV * CHUNK
    band = TQ + pad

    def body(tv_ref, o_ref):
        row = tv_ref[...]
        x = jnp.broadcast_to(row, (TQ, REL_WRAP))
        r = lax.broadcasted_iota(jnp.int32, x.shape, 0)
        for bit in range(8):
            sh = 1 << bit
            x = jnp.where((r & sh) != 0, pltpu.roll(x, sh, 1), x)
        far = jnp.broadcast_to(row[:, 0:1], (TQ, band - REL_COLS))
        full = jnp.concatenate([far, x[:, REL_WRAP // 2:REL_WRAP], x[:, 0:REL_COLS - REL_WRAP // 2]], axis=1)
        qc = (lax.broadcasted_iota(jnp.int32, full.shape, 0) + pad) // CHUNK
        kc = lax.broadcasted_iota(jnp.int32, full.shape, 1) // CHUNK
        ok = jnp.logical_and(kc <= qc, kc >= qc - A_PREV)
        o_ref[...] = jnp.where(ok, full, NEG_INF)

    return _call(
        body,
        name="bias_a_build",
        grid=(h,),
        in_specs=[pl.BlockSpec((None, 1, REL_WRAP), lambda hh: (hh, 0, 0))],
        out_specs=[pl.BlockSpec((None, TQ, band), lambda hh: (hh, 0, 0))],
        out_shape=[jax.ShapeDtypeStruct((h, TQ, band), F32)],
        args=(tv,), sem=("parallel",), hosted=hosted)[0]


def _relbias_grad(dbias, hosted=()):
    h, rows, _ = dbias.shape

    def body(d_ref, o_ref):
        x = d_ref[...]
        r = lax.broadcasted_iota(jnp.int32, x.shape, 0)
        c = lax.broadcasted_iota(jnp.int32, x.shape, 1) - r
        x = jnp.where(jnp.logical_and(c >= 1, c < REL_TABLE), x, 0.0)
        for bit in range(8):
            sh = 1 << bit
            x = jnp.where((r & sh) != 0, pltpu.roll(x, REL_COLS - sh, 1), x)
        diag = jnp.sum(x, axis=0, keepdims=True)
        lane = lax.broadcasted_iota(jnp.int32, diag.shape, 1)
        diag = jnp.where(jnp.logical_and(lane >= 1, lane < REL_TABLE), diag, 0.0)
        rest = -jnp.sum(diag, axis=1, keepdims=True)
        o_ref[...] = jnp.broadcast_to(jnp.where(lane == 0, rest, diag), o_ref.shape)

    return _call(
        body,
        name="relbias_grad",
        grid=(h,),
        in_specs=[pl.BlockSpec((None, rows, REL_COLS), lambda hh: (hh, 0, 0))],
        out_specs=[pl.BlockSpec((None, 8, REL_COLS), lambda hh: (hh, 0, 0))],
        out_shape=[jax.ShapeDtypeStruct((h, 8, REL_COLS), F32)],
        args=(dbias,), sem=("parallel",), hosted=hosted)[0]


def _mix_out_fwd(x, oa, ob, gates, proj_t, wout):
    t = x.shape[0]

    def body(x_ref, oa_ref, ob_ref, gt_ref, pt_ref, wo_ref, y_ref, ya_ref, yb_ref, mg_ref):
        ya = _dot_nt(oa_ref[...], pt_ref[:, 0:A_WIDTH])
        yb = _dot_nt(ob_ref[...], pt_ref[:, A_WIDTH:A_WIDTH + B_Q_WIDTH])
        ya_ref[...] = ya.astype(BF16)
        yb_ref[...] = yb.astype(BF16)
        mg = jax.nn.sigmoid(gt_ref[:, 0:D_MODEL]) * ya + jax.nn.sigmoid(gt_ref[:, D_MODEL:2 * D_MODEL]) * yb
        mgb = mg.astype(BF16)
        mg_ref[...] = mgb
        y_ref[...] = x_ref[...] + _dot_nn(mgb, wo_ref[...])

    return pl.pallas_call(
        body,
        name="mix_out_fwd",
        grid=(t // TM,),
        in_specs=[_rows(TM, D_MODEL), _rows(TM, A_WIDTH), _rows(TM, B_Q_WIDTH), _rows(TM, 2 * D_MODEL),
                  _resident((D_MODEL, A_WIDTH + B_Q_WIDTH)), _resident((D_MODEL, D_MODEL))],
        out_specs=[_rows(TM, D_MODEL), _rows(TM, D_MODEL), _rows(TM, D_MODEL), _rows(TM, D_MODEL)],
        out_shape=[jax.ShapeDtypeStruct((t, D_MODEL), F32), jax.ShapeDtypeStruct((t, D_MODEL), BF16),
                   jax.ShapeDtypeStruct((t, D_MODEL), BF16), jax.ShapeDtypeStruct((t, D_MODEL), BF16)],
        compiler_params=_cparams(("parallel",)),
    )(x, oa, ob, gates, proj_t, wout)


def _mix_out_bwd(d, gates, ya, yb, mg, oa, ob, proj_t, wout, hosted=()):
    t = d.shape[0]
    nt = t // TM

    def body(d_ref, gt_ref, ya_ref, yb_ref, mg_ref, oa_ref, ob_ref, pt_ref, wo_ref,
             doa_ref, dob_ref, dgt_ref, gwo_ref, gwp_ref, acc_o, acc_p):
        i = pl.program_id(0)
        db = d_ref[...].astype(BF16)
        dmg = _dot_nt(db, wo_ref[...])
        sa = jax.nn.sigmoid(gt_ref[:, 0:D_MODEL])
        sb = jax.nn.sigmoid(gt_ref[:, D_MODEL:2 * D_MODEL])
        dya = (dmg * sa).astype(BF16)
        dyb = (dmg * sb).astype(BF16)
        dgt_ref[:, 0:D_MODEL] = (dmg * ya_ref[...].astype(F32) * (sa * (1.0 - sa))).astype(BF16)
        dgt_ref[:, D_MODEL:2 * D_MODEL] = (dmg * yb_ref[...].astype(F32) * (sb * (1.0 - sb))).astype(BF16)
        doa_ref[...] = _dot_nn(dya, pt_ref[:, 0:A_WIDTH]).astype(BF16)
        dob_ref[...] = _dot_nn(dyb, pt_ref[:, A_WIDTH:A_WIDTH + B_Q_WIDTH]).astype(BF16)

        @pl.when(i == 0)
        def _():
            acc_o[...] = jnp.zeros_like(acc_o)
            acc_p[...] = jnp.zeros_like(acc_p)

        acc_o[...] += _dot_tn(mg_ref[...], db)
        acc_p[:, 0:A_WIDTH] += _dot_tn(dya, oa_ref[...])
        acc_p[:, A_WIDTH:A_WIDTH + B_Q_WIDTH] += _dot_tn(dyb, ob_ref[...])

        @pl.when(i == nt - 1)
        def _():
            gwo_ref[...] = acc_o[...].astype(BF16)
            gwp_ref[...] = acc_p[...].astype(BF16)

    whole = pl.BlockSpec((D_MODEL, D_MODEL), lambda i: (0, 0))
    return _call(
        body,
        name="mix_out_bwd",
        grid=(nt,),
        in_specs=[_rows(TM, D_MODEL), _rows(TM, 2 * D_MODEL), _rows(TM, D_MODEL), _rows(TM, D_MODEL),
                  _rows(TM, D_MODEL), _rows(TM, A_WIDTH), _rows(TM, B_Q_WIDTH),
                  _resident((D_MODEL, A_WIDTH + B_Q_WIDTH)), _resident((D_MODEL, D_MODEL))],
        out_specs=[_rows(TM, A_WIDTH), _rows(TM, B_Q_WIDTH), _rows(TM, 2 * D_MODEL), whole, whole],
        out_shape=[jax.ShapeDtypeStruct((t, A_WIDTH), BF16), jax.ShapeDtypeStruct((t, B_Q_WIDTH), BF16),
                   jax.ShapeDtypeStruct((t, 2 * D_MODEL), BF16), jax.ShapeDtypeStruct((D_MODEL, D_MODEL), BF16),
                   jax.ShapeDtypeStruct((D_MODEL, D_MODEL), BF16)],
        scratch_shapes=[pltpu.VMEM((D_MODEL, D_MODEL), F32), pltpu.VMEM((D_MODEL, A_WIDTH + B_Q_WIDTH), F32)],
        args=(d, gates, ya, yb, mg, oa, ob, proj_t, wout), sem=("arbitrary",), hosted=hosted)


def _place():
    x, y, c = lax.axis_index("x"), lax.axis_index("y"), lax.axis_index("c")
    chips = [(1 - x, y), (x, 1 - y), (1 - x, 1 - y)]
    return x, y, c, chips


class _Gather:
    per = 8

    def __init__(self, shards):
        n = len(shards)
        self.inputs = list(shards)
        self.out_shape = [jax.ShapeDtypeStruct((N_DEV * s.shape[0], s.shape[1]), s.dtype) for s in shards]
        self.scratch = [pltpu.SemaphoreType.DMA((n * self.per,)), pltpu.SemaphoreType.DMA((n * self.per,)),
                        pltpu.SemaphoreType.DMA((n,))]
        self.result = None

    def _parts(self, ins, outs, sems):
        send_sems, recv_sems, local_sems = sems
        x, y, c, chips = _place()
        me, sibling = (x, y, c), (x, y, 1 - c)
        xn, yn, dg = chips
        n = len(ins)

        def rows(k, p, part=None):
            r = ins[k].shape[0]
            base = (4 * p[0] + 2 * p[1] + p[2]) * r
            if part is None:
                return outs[k].at[pl.ds(base, r), :]
            return outs[k].at[pl.ds(base + part * (r // 2), r // 2), :]

        def copy(k, slot, block, to, src=None, part=None):
            return pltpu.make_async_remote_copy(
                src_ref=rows(k, block, part) if src is None else src, dst_ref=rows(k, block, part),
                send_sem=send_sems.at[k * self.per + slot], recv_sem=recv_sems.at[k * self.per + slot],
                device_id=to, device_id_type=MESH)

        mine = [pltpu.make_async_copy(ins[k], rows(k, me), local_sems.at[k]) for k in range(n)]
        sends, lands = [], []
        for k in range(n):
            sends.append({
                0: copy(k, 0, me, sibling, src=ins[k]),
                1: copy(k, 1, me, (*xn, c), src=ins[k]),
                2: copy(k, 2, me, (*yn, c), src=ins[k]),
                3: copy(k, 3, (*xn, c), (*yn, c), part=0),
                4: copy(k, 4, (*yn, c), (*xn, c), part=1),
                5: copy(k, 5, (*xn, c), sibling),
                6: copy(k, 6, (*yn, c), sibling),
                7: copy(k, 7, (*dg, c), sibling)})
            lands.append({
                0: copy(k, 0, sibling, me),
                1: copy(k, 1, (*xn, c), me),
                2: copy(k, 2, (*yn, c), me),
                3: copy(k, 3, (*dg, c), me, part=0),
                4: copy(k, 4, (*dg, c), me, part=1),
                5: copy(k, 5, (*xn, 1 - c), me),
                6: copy(k, 6, (*yn, 1 - c), me),
                7: copy(k, 7, (*dg, 1 - c), me)})
        return n, mine, sends, lands

    def start(self, ins, outs, sems):
        n, mine, sends, _ = self._parts(ins, outs, sems)
        for cp in mine:
            cp.start()
        for slot in (0, 1, 2):
            for k in range(n):
                sends[k][slot].start()

    def relay(self, ins, outs, sems):
        n, _, sends, lands = self._parts(ins, outs, sems)
        for k in range(n):
            lands[k][1].wait_recv()
            sends[k][3].start()
            sends[k][5].start()
        for k in range(n):
            lands[k][2].wait_recv()
            sends[k][4].start()
            sends[k][6].start()

    def forward(self, ins, outs, sems):
        n, _, sends, lands = self._parts(ins, outs, sems)
        for k in range(n):
            lands[k][3].wait_recv()
            lands[k][4].wait_recv()
            sends[k][7].start()

    def finish(self, ins, outs, sems):
        n, mine, sends, lands = self._parts(ins, outs, sems)
        for k in range(n):
            for slot in (0, 5, 6, 7):
                lands[k][slot].wait_recv()
        for k in range(n):
            for slot in range(self.per):
                sends[k][slot].wait_send()
        for cp in mine:
            cp.wait()


class _PairExchange:
    def __init__(self, grads):
        n = len(grads)
        self.inputs = list(grads)
        self.out_shape = [jax.ShapeDtypeStruct((g.shape[0] // 2, g.shape[1]), g.dtype) for g in grads]
        self.scratch = [pltpu.SemaphoreType.DMA((n * N_CHIP,)), pltpu.SemaphoreType.DMA((n * N_CHIP,))]
        self.result = None

    def _copies(self, ins, outs, sems):
        send_sems, recv_sems = sems
        x, y, c, _ = _place()
        copies = []
        for k in range(len(ins)):
            r = ins[k].shape[0] // N_DEV
            for q in range(N_CHIP):
                copies.append(pltpu.make_async_remote_copy(
                    src_ref=ins[k].at[pl.ds((2 * q + 1 - c) * r, r), :], dst_ref=outs[k].at[pl.ds(q * r, r), :],
                    send_sem=send_sems.at[k * N_CHIP + q], recv_sem=recv_sems.at[k * N_CHIP + q],
                    device_id=(x, y, 1 - c), device_id_type=MESH))
        return copies

    def start(self, ins, outs, sems):
        for cp in self._copies(ins, outs, sems):
            cp.start()

    def relay(self, ins, outs, sems):
        pass

    def forward(self, ins, outs, sems):
        pass

    def finish(self, ins, outs, sems):
        copies = self._copies(ins, outs, sems)
        for cp in copies:
            cp.wait_recv()
        for cp in copies:
            cp.wait_send()


class _ChipExchange(_PairExchange):
    def __init__(self, psums):
        n = len(psums)
        self.inputs = list(psums)
        self.out_shape = [jax.ShapeDtypeStruct((3 * p.shape[0] // N_CHIP, p.shape[1]), p.dtype) for p in psums]
        self.scratch = [pltpu.SemaphoreType.DMA((n * 3,)), pltpu.SemaphoreType.DMA((n * 3,))]
        self.result = None

    def _copies(self, ins, outs, sems):
        send_sems, recv_sems = sems
        _, _, c, chips = _place()
        copies = []
        for k in range(len(ins)):
            r = ins[k].shape[0] // N_CHIP
            for j, chip in enumerate(chips):
                copies.append(pltpu.make_async_remote_copy(
                    src_ref=ins[k].at[pl.ds((2 * chip[0] + chip[1]) * r, r), :], dst_ref=outs[k].at[pl.ds(j * r, r), :],
                    send_sem=send_sems.at[k * 3 + j], recv_sem=recv_sems.at[k * 3 + j],
                    device_id=(*chip, c), device_id_type=MESH))
        return copies


def _exchange_alone(xchg, name):
    n_in, n_out = len(xchg.inputs), len(xchg.out_shape)

    def body(*refs):
        ins, outs, sems = refs[:n_in], refs[n_in:n_in + n_out], refs[n_in + n_out:]
        xchg.start(ins, outs, sems)
        xchg.relay(ins, outs, sems)
        xchg.forward(ins, outs, sems)
        xchg.finish(ins, outs, sems)

    xchg.result = list(pl.pallas_call(
        body, name=name, in_specs=[_hbm()] * n_in, out_specs=[_hbm()] * n_out, out_shape=xchg.out_shape,
        scratch_shapes=xchg.scratch)(*xchg.inputs))
    return xchg.result


def _pair_sum(core, grads, recvd, name):
    n = len(grads)
    r = grads[0].shape[0] // N_DEV
    cdim = grads[0].shape[1]
    tr = r // 2 if r % 32 == 0 else r
    nt = r // tr

    def body(core_ref, *refs):
        del core_ref
        for k in range(n):
            refs[2 * n + k][...] = (refs[k][...].astype(F32) + refs[n + k][...].astype(F32)).astype(BF16)

    gspec = pl.BlockSpec((tr, cdim), lambda q, i, core_ref: ((2 * q + core_ref[0]) * nt + i, 0))
    rspec = pl.BlockSpec((tr, cdim), lambda q, i, core_ref: (q * nt + i, 0))
    return pl.pallas_call(
        body,
        name=name,
        grid_spec=pltpu.PrefetchScalarGridSpec(
            num_scalar_prefetch=1, grid=(N_CHIP, nt), in_specs=[gspec] * n + [rspec] * n, out_specs=[rspec] * n),
        out_shape=[jax.ShapeDtypeStruct((N_CHIP * r, cdim), BF16) for _ in range(n)],
        compiler_params=_cparams(("parallel", "parallel")),
    )(core, *grads, *recvd)


def _final_sum(chip, psums, recvd, name):
    n = len(psums)
    r = psums[0].shape[0] // N_CHIP
    cdim = psums[0].shape[1]
    tr = r // 2 if r % 32 == 0 else r
    nt = r // tr

    def body(chip_ref, *refs):
        del chip_ref
        for k in range(n):
            got = refs[n + k]
            tot = refs[k][...].astype(F32) + got[0].astype(F32)
            tot = tot + got[1].astype(F32)
            tot = tot + got[2].astype(F32)
            refs[2 * n + k][...] = tot

    pspec = pl.BlockSpec((tr, cdim), lambda i, chip_ref: (chip_ref[0] * nt + i, 0))
    rspec = pl.BlockSpec((3, tr, cdim), lambda i, chip_ref: (0, i, 0))
    ospec = pl.BlockSpec((tr, cdim), lambda i, chip_ref: (i, 0))
    return pl.pallas_call(
        body,
        name=name,
        grid_spec=pltpu.PrefetchScalarGridSpec(
            num_scalar_prefetch=1, grid=(nt,), in_specs=[pspec] * n + [rspec] * n, out_specs=[ospec] * n),
        out_shape=[jax.ShapeDtypeStruct((r, cdim), F32) for _ in range(n)],
        compiler_params=_cparams(("parallel",)),
    )(chip, *psums, *[g.reshape(3, r, cdim) for g in recvd])


SMALL_ROWS = 16


def _all_reduce_small(part):
    def body(p_ref, o_ref, buf, send_sems, recv_sems):
        x, y, c, _ = _place()
        me = 4 * x + 2 * y + c
        buf[me] = p_ref[...]
        copies = []
        for d in range(1, N_DEV):
            peer = me ^ d
            copies.append(pltpu.make_async_remote_copy(
                src_ref=p_ref, dst_ref=buf.at[me], send_sem=send_sems.at[d - 1], recv_sem=recv_sems.at[d - 1],
                device_id=(peer // 4, (peer // 2) % 2, peer % 2), device_id_type=MESH))
        for cp in copies:
            cp.start()
        for cp in copies:
            cp.wait_recv()
        for cp in copies:
            cp.wait_send()
        tot = buf[0]
        for d in range(1, N_DEV):
            tot = tot + buf[d]
        o_ref[...] = tot

    return pl.pallas_call(
        body,
        name="all_reduce_small",
        in_specs=[pl.BlockSpec(memory_space=pltpu.VMEM)],
        out_specs=pl.BlockSpec(memory_space=pltpu.VMEM),
        out_shape=jax.ShapeDtypeStruct(part.shape, F32),
        scratch_shapes=[pltpu.VMEM((N_DEV,) + part.shape, F32), pltpu.SemaphoreType.DMA((N_DEV - 1,)),
                        pltpu.SemaphoreType.DMA((N_DEV - 1,))],
    )(part)


ADAMW_STEPS = 4


def _adamw(ws, gs, ms, vs, name, hosted=()):
    n = len(ws)
    steps = ADAMW_STEPS if all(w.shape[0] % (8 * ADAMW_STEPS) == 0 for w in ws) else 1
    c1 = 1.0 - ADAM_B1 ** ADAM_STEP
    c2 = 1.0 - ADAM_B2 ** ADAM_STEP

    def body(*refs):
        for k in range(n):
            w, g, m, v = (refs[j * n + k][...] for j in range(4))
            m2 = ADAM_B1 * m + (1.0 - ADAM_B1) * g
            v2 = ADAM_B2 * v + (1.0 - ADAM_B2) * (g * g)
            delta = -ADAM_LR * ((m2 * (1.0 / c1)) / (jnp.sqrt(v2 * (1.0 / c2)) + ADAM_EPS) + ADAM_WD * w)
            refs[4 * n + k][...] = delta
            refs[5 * n + k][...] = m2
            refs[6 * n + k][...] = v2

    specs = [pl.BlockSpec((w.shape[0] // steps, w.shape[1]), lambda i: (i, 0)) for w in ws]
    shapes = [jax.ShapeDtypeStruct(w.shape, F32) for w in ws]
    outs = _call(
        body,
        name=name,
        grid=(steps,),
        in_specs=specs * 4,
        out_specs=specs * 3,
        out_shape=shapes * 3,
        args=(*ws, *gs, *ms, *vs), sem=("parallel",), hosted=hosted)
    return outs[:n], outs[n:2 * n], outs[2 * n:]


def _adamw_reduced(chip, ws, psums, recvd, ms, vs, steps, name):
    n = len(ws)
    c1 = 1.0 - ADAM_B1 ** ADAM_STEP
    c2 = 1.0 - ADAM_B2 ** ADAM_STEP

    def body(chip_ref, *refs):
        del chip_ref
        for k in range(n):
            w, m, v = (refs[j * n + k][...] for j in (0, 3, 4))
            got = refs[2 * n + k]
            g = refs[n + k][...].astype(F32) + got[0].astype(F32)
            g = g + got[1].astype(F32)
            g = g + got[2].astype(F32)
            m2 = ADAM_B1 * m + (1.0 - ADAM_B1) * g
            v2 = ADAM_B2 * v + (1.0 - ADAM_B2) * (g * g)
            refs[5 * n + k][...] = g
            refs[6 * n + k][...] = -ADAM_LR * (
                (m2 * (1.0 / c1)) / (jnp.sqrt(v2 * (1.0 / c2)) + ADAM_EPS) + ADAM_WD * w)
            refs[7 * n + k][...] = m2
            refs[8 * n + k][...] = v2

    def blk(w):
        return (w.shape[0] // steps, w.shape[1])

    own = [pl.BlockSpec(blk(w), lambda i, chip_ref: (i, 0)) for w in ws]
    psum = [pl.BlockSpec(blk(w), lambda i, chip_ref: (chip_ref[0] * steps + i, 0)) for w in ws]
    recv = [pl.BlockSpec((3,) + blk(w), lambda i, chip_ref: (0, i, 0)) for w in ws]
    shapes = [jax.ShapeDtypeStruct(w.shape, F32) for w in ws]
    outs = pl.pallas_call(
        body,
        name=name,
        grid_spec=pltpu.PrefetchScalarGridSpec(
            num_scalar_prefetch=1, grid=(steps,), in_specs=own + psum + recv + own + own, out_specs=own * 4),
        out_shape=shapes * 4,
        compiler_params=_cparams(("parallel",)),
    )(chip, *ws, *psums, *[r.reshape((3,) + w.shape) for r, w in zip(recvd, ws)], *ms, *vs)
    return outs[:n], outs[n:2 * n], outs[2 * n:3 * n], outs[3 * n:]


def _bias_b():
    pad = B_PREV * CHUNK
    slopes = np.array([2.0 ** (-8.0 * (i + 1) / B_Q_HEADS) for i in range(B_Q_HEADS)], dtype=np.float32)
    dist = np.abs(np.arange(TQ)[:, None] - np.arange(TQ + pad)[None, :] + pad).astype(np.float32)
    bias = -slopes.reshape(B_Q_HEADS, 1, 1) * dist[None]
    qc = (np.arange(TQ)[:, None] + pad) // CHUNK
    kc = np.arange(TQ + pad)[None, :] // CHUNK
    allowed = (kc <= qc) & (kc >= qc - B_PREV)
    return np.where(allowed[None], bias, np.float32(NEG_INF)).astype(np.float32)


def kernel(x, ffn1_norm, ffn1_w_gate, ffn1_w_up, ffn1_w_down, mix_norm, w_in, rel_bias, sinks, w_proj_a, w_proj_b, w_out, ffn2_norm, ffn2_w_gate, ffn2_w_up, ffn2_w_down, final_norm, loss_target, m_ffn1_norm, m_ffn1_w_gate, m_ffn1_w_up, m_ffn1_w_down, m_mix_norm, m_w_in, m_rel_bias, m_sinks, m_w_proj_a, m_w_proj_b, m_w_out, m_ffn2_norm, m_ffn2_w_gate, m_ffn2_w_up, m_ffn2_w_down, m_final_norm, v_ffn1_norm, v_ffn1_w_gate, v_ffn1_w_up, v_ffn1_w_down, v_mix_norm, v_w_in, v_rel_bias, v_sinks, v_w_proj_a, v_w_proj_b, v_w_out, v_ffn2_norm, v_ffn2_w_gate, v_ffn2_w_up, v_ffn2_w_down, v_final_norm):
    bsz, s_len, _ = x.shape
    t = bsz * s_len
    core = lax.axis_index("c").astype(jnp.int32).reshape(1)
    chip = (2 * lax.axis_index("x") + lax.axis_index("y")).astype(jnp.int32).reshape(1)

    proj_rows = jnp.concatenate([w_proj_a.T, w_proj_b.T], axis=1)
    sh_g1, sh_u1, sh_d1, sh_in, sh_proj, sh_out, sh_g2, sh_u2, sh_d2 = _to_bf16(
        [ffn1_w_gate.T, ffn1_w_up.T, ffn1_w_down, w_in.T, proj_rows, w_out, ffn2_w_gate.T, ffn2_w_up.T, ffn2_w_down],
        "weights_to_bf16")

    gather_up1 = _Gather([sh_g1, sh_u1])
    far = jnp.broadcast_to(rel_bias[:, REL_TABLE - 1:REL_TABLE], (A_HEADS, REL_WRAP // 2))
    tv = jnp.concatenate([far, jnp.flip(rel_bias, axis=1), jnp.zeros((A_HEADS, REL_WRAP // 2 - REL_TABLE), F32)], axis=1)
    bias_a = _bias_a_build(tv.reshape(A_HEADS, 1, REL_WRAP), hosted=[gather_up1])
    wg1, wu1 = gather_up1.result
    gather_down1 = _Gather([sh_d1, sh_in])
    gather_out = _Gather([sh_proj, sh_out])
    gather_ffn2_gate = _Gather([sh_g2])
    gather_ffn2_rest = _Gather([sh_u2, sh_d2])

    x0 = x.reshape(t, D_MODEL)
    tgt = loss_target.reshape(t, D_MODEL)
    gam1, gam2, gam3, gam4 = (g.reshape(1, D_MODEL) for g in (ffn1_norm, mix_norm, ffn2_norm, final_norm))

    h1, g1, u1, a1 = _ffn_up(x0, gam1, wg1, wu1, "ffn1_up", hosted=[gather_down1])
    wd1, win_t = gather_down1.result
    x1 = _ffn_down(x0, a1, wd1, "ffn1_down", hosted=[gather_out])
    proj_t, wout = gather_out.result
    h2, qkv_a, qkv_b, gates = _proj_fwd(x1, gam2, win_t, hosted=[gather_ffn2_gate])
    (wg2,) = gather_ffn2_gate.result
    qkv_a3 = qkv_a.reshape(bsz, s_len, QKV_A)
    qkv_b3 = qkv_b.reshape(bsz, s_len, QKV_B)

    bias_b = jnp.asarray(_bias_b())
    sink_rows = jnp.broadcast_to(sinks.reshape(B_Q_HEADS, 1, 1), (B_Q_HEADS, 8, LANES))

    oa = _attn_a_fwd(qkv_a3, bias_a, hosted=[gather_ffn2_rest]).reshape(t, A_WIDTH)
    wu2, wd2 = gather_ffn2_rest.result
    ob = _attn_b_fwd(qkv_b3, bias_b, sink_rows).reshape(t, B_Q_WIDTH)
    x2, ya, yb, mg = _mix_out_fwd(x1, oa, ob, gates, proj_t, wout)
    h3, g2, u2, a2, x3 = _ffn_fwd(x2, gam3, wg2, wu2, wd2, "ffn2_fwd")

    dx2, dg2, du2, db2, dgam3, dgam4, loss_part = _ffn_bwd_head(x3, gam4, tgt, x2, gam3, g2, u2, wg2, wu2, wd2,
                                                                "ffn2_bwd")
    gw_ffn2 = [_mm_tn([dg2], h3, "grad_ffn2_gate"), _mm_tn([du2], h3, "grad_ffn2_up"),
               _mm_tn([a2], db2, "grad_ffn2_down")]
    pairx_ffn2 = _PairExchange(gw_ffn2)
    doa, dob, dgates, gw_out, gw_proj = _mix_out_bwd(dx2, gates, ya, yb, mg, oa, ob, proj_t, wout,
                                                     hosted=[pairx_ffn2])
    psum_ffn2 = _pair_sum(core, gw_ffn2, pairx_ffn2.result, "pair_sum_ffn2")

    chipx_ffn2 = _ChipExchange(psum_ffn2)
    dqa, dka, dva, dbias_a = _attn_a_bwd(qkv_a3, bias_a, doa.reshape(bsz, s_len, A_WIDTH), hosted=[chipx_ffn2])
    pairx_out = _PairExchange([gw_proj, gw_out])
    dqb, dkvb, dsink = _attn_b_bwd(qkv_b3, bias_b, sink_rows, dob.reshape(bsz, s_len, B_Q_WIDTH), hosted=[pairx_out])
    drel_lanes = _relbias_grad(dbias_a)
    dproj = [dqa.reshape(t, A_WIDTH), dka.reshape(t, A_WIDTH), dva.reshape(t, A_WIDTH), dqb.reshape(t, B_Q_WIDTH),
             dkvb.reshape(t, 2 * B_KV_WIDTH), dgates]

    gw_in = _mm_tn(dproj, h2, "grad_w_in")
    pairx_in = _PairExchange([gw_in])
    psum_out = _pair_sum(core, [gw_proj, gw_out], pairx_out.result, "pair_sum_mix")
    chipx_out = _ChipExchange(psum_out)
    dx1, db1, dgam2 = _proj_bwd(dx2, x1, gam2, dproj, win_t, hosted=[pairx_in, chipx_out])
    psum_in = _pair_sum(core, [gw_in], pairx_in.result, "pair_sum_w_in")
    gw_d1 = _mm_tn([a1], db1, "grad_ffn1_down")

    chipx_in = _ChipExchange(psum_in)
    pairx_d1 = _PairExchange([gw_d1])
    dg1, du1 = _ffn_bwd_act(dx1, g1, u1, wd1, "ffn1_bwd_act", hosted=[chipx_in, pairx_d1])
    psum_d1 = _pair_sum(core, [gw_d1], pairx_d1.result, "pair_sum_ffn1_down")
    chipx_d1 = _ChipExchange(psum_d1)
    gw_g1 = _mm_tn([dg1], h1, "grad_ffn1_gate", hosted=[chipx_d1])
    from_sibling_g1 = _exchange_alone(_PairExchange([gw_g1]), "pair_exchange_ffn1_gate")
    psum_g1 = _pair_sum(core, [gw_g1], from_sibling_g1, "pair_sum_ffn1_gate")
    chipx_g1 = _ChipExchange(psum_g1)
    gw_u1 = _mm_tn([du1], h1, "grad_ffn1_up", hosted=[chipx_g1])
    from_sibling_u1 = _exchange_alone(_PairExchange([gw_u1]), "pair_exchange_ffn1_up")
    psum_u1 = _pair_sum(core, [gw_u1], from_sibling_u1, "pair_sum_ffn1_up")
    chipx_u1 = _ChipExchange(psum_u1)
    dx0, dgam1 = _ffn_bwd_in(dx1, x0, gam1, dg1, du1, wg1, wu1, "ffn1_bwd_in", hosted=[chipx_u1])

    (g_proj,) = _final_sum(chip, psum_out[0:1], chipx_out.result[0:1], "grad_sum_proj")
    grads = {"w_proj_a": g_proj[:, 0:A_WIDTH].T, "w_proj_b": g_proj[:, A_WIDTH:].T}

    def row_of(v):
        return jnp.pad(v.reshape(1, -1), ((0, 0), (0, D_MODEL - v.size)))

    def table_rows(v):
        return jnp.pad(v, ((0, 0), (0, D_MODEL - REL_TABLE)))

    drel_local = jnp.flip(drel_lanes[:, 0, 0:REL_TABLE], axis=1)
    small_part = jnp.concatenate(
        [jnp.sum(dgam1, axis=0, keepdims=True), jnp.sum(dgam2, axis=0, keepdims=True),
         jnp.sum(dgam3, axis=0, keepdims=True), jnp.sum(dgam4, axis=0, keepdims=True),
         row_of(jnp.sum(loss_part)), row_of(dsink[:, 0, 0]), jnp.zeros((2, D_MODEL), F32),
         table_rows(drel_local)], axis=0)
    small = _all_reduce_small(small_part)
    loss = small[4, 0]

    def pack(n1, n2, n3, n4, sk, tb):
        return jnp.concatenate([n1.reshape(1, -1), n2.reshape(1, -1), n3.reshape(1, -1), n4.reshape(1, -1),
                                jnp.zeros((1, D_MODEL), F32), row_of(sk), jnp.zeros((2, D_MODEL), F32), table_rows(tb)],
                               axis=0)

    live = np.zeros((SMALL_ROWS, D_MODEL), np.float32)
    live[0:4] = 1.0
    live[5, 0:B_Q_HEADS] = 1.0
    live[8:16, 0:REL_TABLE] = 1.0
    small_g = small * jnp.asarray(live)
    sw = pack(ffn1_norm, mix_norm, ffn2_norm, final_norm, sinks, rel_bias)
    sm = pack(m_ffn1_norm, m_mix_norm, m_ffn2_norm, m_final_norm, m_sinks, m_rel_bias)
    sv = pack(v_ffn1_norm, v_mix_norm, v_ffn2_norm, v_final_norm, v_sinks, v_rel_bias)
    (sd,), (snm,), (snv,) = _adamw([sw], [small_g], [sm], [sv], "adamw_small")

    def unpack(p):
        return {"ffn1_norm": p[0], "mix_norm": p[1], "ffn2_norm": p[2], "final_norm": p[3],
                "sinks": p[5, 0:B_Q_HEADS], "rel_bias": p[8:16, 0:REL_TABLE]}

    grads.update(unpack(small_g))
    delta, new_m, new_v = unpack(sd), unpack(snm), unpack(snv)

    wmv = {
        "ffn1_w_gate": (ffn1_w_gate, m_ffn1_w_gate, v_ffn1_w_gate), "ffn1_w_up": (ffn1_w_up, m_ffn1_w_up, v_ffn1_w_up),
        "ffn1_w_down": (ffn1_w_down, m_ffn1_w_down, v_ffn1_w_down), "w_in": (w_in, m_w_in, v_w_in),
        "w_proj_a": (w_proj_a, m_w_proj_a, v_w_proj_a), "w_proj_b": (w_proj_b, m_w_proj_b, v_w_proj_b),
        "w_out": (w_out, m_w_out, v_w_out),
        "ffn2_w_gate": (ffn2_w_gate, m_ffn2_w_gate, v_ffn2_w_gate), "ffn2_w_up": (ffn2_w_up, m_ffn2_w_up, v_ffn2_w_up),
        "ffn2_w_down": (ffn2_w_down, m_ffn2_w_down, v_ffn2_w_down),
    }
    row_form_names = ("ffn1_w_gate", "ffn1_w_up", "w_in", "ffn2_w_gate", "ffn2_w_up")

    def form(n, a):
        return a.T if n in row_form_names else a

    def reduced_group(gname, names, psums, recvd, steps):
        gs_, ds_, ms_, vs_ = _adamw_reduced(
            chip, [form(n, wmv[n][0]) for n in names], psums, recvd, [form(n, wmv[n][1]) for n in names],
            [form(n, wmv[n][2]) for n in names], steps, gname)
        for n, g_, d_, m_, v_ in zip(names, gs_, ds_, ms_, vs_):
            grads[n], delta[n], new_m[n], new_v[n] = form(n, g_), form(n, d_), form(n, m_), form(n, v_)

    reduced_group("adamw_ffn", ["ffn1_w_gate", "ffn1_w_up", "ffn1_w_down", "ffn2_w_gate", "ffn2_w_up", "ffn2_w_down"],
                  psum_g1 + psum_u1 + psum_d1 + psum_ffn2,
                  chipx_g1.result + chipx_u1.result + chipx_d1.result + chipx_ffn2.result, 11)
    reduced_group("adamw_in_out", ["w_in", "w_out"], psum_in + psum_out[1:2], chipx_in.result + chipx_out.result[1:2], 2)
    names = ["w_proj_a", "w_proj_b"]
    ds_, ms_, vs_ = _adamw([wmv[n][0] for n in names], [grads[n] for n in names], [wmv[n][1] for n in names],
                           [wmv[n][2] for n in names], "adamw_proj")
    for n, d_, m_, v_ in zip(names, ds_, ms_, vs_):
        delta[n], new_m[n], new_v[n] = d_, m_, v_

    order = ["ffn1_norm", "ffn1_w_gate", "ffn1_w_up", "ffn1_w_down", "mix_norm", "w_in", "rel_bias", "sinks",
             "w_proj_a", "w_proj_b", "w_out", "ffn2_norm", "ffn2_w_gate", "ffn2_w_up", "ffn2_w_down", "final_norm"]
    grad_x = dx0.reshape(bsz, s_len, D_MODEL)
    return (loss, grad_x, *[grads[n] for n in order], *[delta[n] for n in order], *[new_m[n] for n in order],
            *[new_v[n] for n in order])
```

```python
import numpy as np
import jax
import jax.numpy as jnp
from jax import lax
from jax.experimental import pallas as pl
from jax.experimental.pallas import tpu as pltpu

F32 = jnp.float32
BF16 = jnp.bfloat16

D_MODEL = 1024
D_FF = 2816
CHUNK = 64
D_HEAD = 64
A_HEADS = 8
A_PREV = 8
MAX_REL = 128
B_Q_HEADS = 8
B_KV_HEADS = 2
B_GROUP = B_Q_HEADS // B_KV_HEADS
B_PREV = 2
REL_TABLE = (CHUNK - 1) + MAX_REL + 1
A_WIDTH = A_HEADS * D_HEAD
B_Q_WIDTH = B_Q_HEADS * D_HEAD
B_KV_WIDTH = B_KV_HEADS * D_HEAD
QKV_A = 3 * A_WIDTH
QKV_B = B_Q_WIDTH + 2 * B_KV_WIDTH
IN_WIDTH = QKV_A + QKV_B + 2 * D_MODEL
EPS = 1e-6
NEG_INF = -1e30
SCALE = 1.0 / 8.0

ADAM_LR = 0.001
ADAM_B1 = 0.9
ADAM_B2 = 0.999
ADAM_EPS = 1e-08
ADAM_WD = 0.01
ADAM_STEP = 10

N_DEV = 8
N_CHIP = 4
MESH = pl.DeviceIdType.MESH

LANES = 128
TQ = 256
TM = 256
FC = 256
VMEM_LIMIT = 56 << 20


def _cparams(sem, vmem=VMEM_LIMIT):
    return pltpu.CompilerParams(dimension_semantics=sem, vmem_limit_bytes=vmem)


def _dot_nt(a, b):
    return lax.dot_general(a, b, (((1,), (1,)), ((), ())), preferred_element_type=F32)


def _dot_nn(a, b):
    return lax.dot_general(a, b, (((1,), (0,)), ((), ())), preferred_element_type=F32)


def _dot_tn(a, b):
    return lax.dot_general(a, b, (((0,), (0,)), ((), ())), preferred_element_type=F32)


def _resident(shape):
    nd = len(shape)
    return pl.BlockSpec(shape, lambda *_: (0,) * nd, pipeline_mode=pl.Buffered(1))


def _rows(tm, width):
    return pl.BlockSpec((tm, width), lambda i: (i, 0))


def _colsum8(v):
    tm, n = v.shape
    return jnp.sum(v.reshape(tm // 8, 8, n), axis=0)


def _rms(x):
    r = lax.rsqrt(jnp.mean(x * x, axis=-1, keepdims=True) + EPS)
    return x * r, r


def _rms_bwd(dh, xh, r, gamma):
    dxh = dh * gamma
    dx = r * (dxh - xh * jnp.mean(dxh * xh, axis=-1, keepdims=True))
    return dx, _colsum8(dh * xh)


def _hbm():
    return pl.BlockSpec(memory_space=pltpu.HBM)


def _call(body, *, name, grid, in_specs, out_specs, out_shape, args, sem, scratch_shapes=(), hosted=()):
    in_specs, out_specs, out_shape = list(in_specs), list(out_specs), list(out_shape)
    scratch_shapes = list(scratch_shapes)
    if not hosted:
        return pl.pallas_call(body, name=name, grid=grid, in_specs=in_specs, out_specs=out_specs, out_shape=out_shape,
                              scratch_shapes=scratch_shapes, compiler_params=_cparams(sem))(*args)
    n_in, n_out, n_scr = len(in_specs), len(out_specs), len(scratch_shapes)
    x_in = [a for x in hosted for a in x.inputs]
    x_out = [s for x in hosted for s in x.out_shape]
    x_scr = [s for x in hosted for s in x.scratch]
    steps = int(np.prod(grid))
    forward_step = max(steps - 3, 0)
    relay_step = min((5 * steps) // 8, forward_step)

    def wrapped(*refs):
        pos = [0]

        def take(k):
            pos[0] += k
            return refs[pos[0] - k:pos[0]]

        ins, xin, outs, xout, scr, xscr = (take(k) for k in (n_in, len(x_in), n_out, len(x_out), n_scr, len(x_scr)))
        step = 0
        for axis, extent in enumerate(grid):
            step = step * extent + pl.program_id(axis)
        own, oi, oo, osc = [], 0, 0, 0
        for x in hosted:
            own.append((xin[oi:oi + len(x.inputs)], xout[oo:oo + len(x.out_shape)], xscr[osc:osc + len(x.scratch)]))
            oi, oo, osc = oi + len(x.inputs), oo + len(x.out_shape), osc + len(x.scratch)

        def phase(method):
            for x, (i_, o_, s_) in zip(hosted, own):
                getattr(x, method)(i_, o_, s_)

        pl.when(step == 0)(lambda: phase("start"))
        body(*ins, *outs, *scr)
        pl.when(step == relay_step)(lambda: phase("relay"))
        pl.when(step == forward_step)(lambda: phase("forward"))
        pl.when(step == steps - 1)(lambda: phase("finish"))

    res = pl.pallas_call(
        wrapped, name=name, grid=grid, in_specs=in_specs + [_hbm()] * len(x_in),
        out_specs=out_specs + [_hbm()] * len(x_out), out_shape=out_shape + x_out,
        scratch_shapes=scratch_shapes + x_scr, compiler_params=_cparams(("arbitrary",) * len(grid)))(*args, *x_in)
    rest = list(res[n_out:])
    for x in hosted:
        x.result, rest = rest[:len(x.out_shape)], rest[len(x.out_shape):]
    return list(res[:n_out])


def _to_bf16(arrays, name):
    n = len(arrays)

    def body(*refs):
        for k in range(n):
            refs[n + k][...] = refs[k][...].astype(BF16)

    specs = [pl.BlockSpec(a.shape, lambda i: (0, 0)) for a in arrays]
    return pl.pallas_call(
        body, name=name, grid=(1,), in_specs=specs, out_specs=specs,
        out_shape=[jax.ShapeDtypeStruct(a.shape, BF16) for a in arrays],
        compiler_params=_cparams(("arbitrary",)))(*arrays)


def _ffn_fwd(x, gamma, wg_t, wu_t, wd, name, hosted=()):
    t = x.shape[0]
    f = wg_t.shape[0]

    def body(x_ref, gam_ref, wg_ref, wu_ref, wd_ref, h_ref, g_ref, u_ref, a_ref, y_ref):
        xv = x_ref[...]
        xh, _ = _rms(xv)
        h = (xh * gam_ref[...]).astype(BF16)
        h_ref[...] = h
        for j in range(f // FC):
            sl = slice(j * FC, (j + 1) * FC)
            g = _dot_nt(h, wg_ref[sl, :])
            u = _dot_nt(h, wu_ref[sl, :])
            g_ref[:, sl] = g.astype(BF16)
            u_ref[:, sl] = u.astype(BF16)
            a_ref[:, sl] = (g * jax.nn.sigmoid(g) * u).astype(BF16)
        y_ref[...] = xv + 0.5 * _dot_nn(a_ref[...], wd_ref[...])

    return _call(
        body,
        name=name,
        grid=(t // TM,),
        in_specs=[_rows(TM, D_MODEL), _resident((1, D_MODEL)), _resident((f, D_MODEL)), _resident((f, D_MODEL)),
                  _resident((f, D_MODEL))],
        out_specs=[_rows(TM, D_MODEL), _rows(TM, f), _rows(TM, f), _rows(TM, f), _rows(TM, D_MODEL)],
        out_shape=[jax.ShapeDtypeStruct((t, D_MODEL), BF16), jax.ShapeDtypeStruct((t, f), BF16),
                   jax.ShapeDtypeStruct((t, f), BF16), jax.ShapeDtypeStruct((t, f), BF16),
                   jax.ShapeDtypeStruct((t, D_MODEL), F32)],
        args=(x, gamma, wg_t, wu_t, wd), sem=("parallel",), hosted=hosted)


def _ffn_up(x, gamma, wg_t, wu_t, name, hosted=()):
    t = x.shape[0]
    f = wg_t.shape[0]

    def body(x_ref, gam_ref, wg_ref, wu_ref, h_ref, g_ref, u_ref, a_ref):
        xh, _ = _rms(x_ref[...])
        h = (xh * gam_ref[...]).astype(BF16)
        h_ref[...] = h
        for j in range(f // FC):
            sl = slice(j * FC, (j + 1) * FC)
            g = _dot_nt(h, wg_ref[sl, :])
            u = _dot_nt(h, wu_ref[sl, :])
            g_ref[:, sl] = g.astype(BF16)
            u_ref[:, sl] = u.astype(BF16)
            a_ref[:, sl] = (g * jax.nn.sigmoid(g) * u).astype(BF16)

    return _call(
        body,
        name=name,
        grid=(t // TM,),
        in_specs=[_rows(TM, D_MODEL), _resident((1, D_MODEL)), _resident((f, D_MODEL)), _resident((f, D_MODEL))],
        out_specs=[_rows(TM, D_MODEL), _rows(TM, f), _rows(TM, f), _rows(TM, f)],
        out_shape=[jax.ShapeDtypeStruct((t, D_MODEL), BF16), jax.ShapeDtypeStruct((t, f), BF16),
                   jax.ShapeDtypeStruct((t, f), BF16), jax.ShapeDtypeStruct((t, f), BF16)],
        args=(x, gamma, wg_t, wu_t), sem=("parallel",), hosted=hosted)


def _ffn_down(x, a_act, wd, name, hosted=()):
    t = x.shape[0]
    f = wd.shape[0]

    def body(x_ref, a_ref, wd_ref, y_ref):
        y_ref[...] = x_ref[...] + 0.5 * _dot_nn(a_ref[...], wd_ref[...])

    return _call(
        body,
        name=name,
        grid=(t // TM,),
        in_specs=[_rows(TM, D_MODEL), _rows(TM, f), _resident((f, D_MODEL))],
        out_specs=[_rows(TM, D_MODEL)],
        out_shape=[jax.ShapeDtypeStruct((t, D_MODEL), F32)],
        args=(x, a_act, wd), sem=("parallel",), hosted=hosted)[0]


def _ffn_bwd_head(y, gamma_f, target, x, gamma, g_act, u_act, wg_t, wu_t, wd, name):
    t = x.shape[0]
    f = wg_t.shape[0]

    def body(y_ref, gamf_ref, t_ref, x_ref, gam_ref, g_ref, u_ref, wg_ref, wu_ref, wd_ref, dx_ref, dg_ref, du_ref,
             db_ref, dgam_ref, dgamf_ref, loss_ref):
        yh, ry = _rms(y_ref[...])
        gam_f = gamf_ref[...]
        e = yh * gam_f - t_ref[...]
        dv, dgam_f = _rms_bwd(e * (1.0 / D_MODEL), yh, ry, gam_f)
        db = (0.5 * dv).astype(BF16)
        db_ref[...] = db
        for j in range(f // FC):
            sl = slice(j * FC, (j + 1) * FC)
            da = _dot_nt(db, wd_ref[sl, :])
            g = g_ref[:, sl].astype(F32)
            u = u_ref[:, sl].astype(F32)
            s = jax.nn.sigmoid(g)
            dg_ref[:, sl] = (da * u * (s * (1.0 + g * (1.0 - s)))).astype(BF16)
            du_ref[:, sl] = (da * (g * s)).astype(BF16)
        dh = _dot_nn(dg_ref[...], wg_ref[...]) + _dot_nn(du_ref[...], wu_ref[...])
        xh, r = _rms(x_ref[...])
        dxn, dgam = _rms_bwd(dh, xh, r, gam_ref[...])
        dx_ref[...] = dv + dxn

        @pl.when(pl.program_id(0) == 0)
        def _():
            dgam_ref[...] = jnp.zeros_like(dgam_ref)
            dgamf_ref[...] = jnp.zeros_like(dgamf_ref)
            loss_ref[...] = jnp.zeros_like(loss_ref)

        dgam_ref[...] += dgam
        dgamf_ref[...] += dgam_f
        loss_ref[...] += _colsum8(e * e) * (0.5 / D_MODEL)

    acc = pl.BlockSpec((8, D_MODEL), lambda i: (0, 0))
    return _call(
        body,
        name=name,
        grid=(t // TM,),
        in_specs=[_rows(TM, D_MODEL), _resident((1, D_MODEL)), _rows(TM, D_MODEL), _rows(TM, D_MODEL),
                  _resident((1, D_MODEL)), _rows(TM, f), _rows(TM, f),
                  _resident((f, D_MODEL)), _resident((f, D_MODEL)), _resident((f, D_MODEL))],
        out_specs=[_rows(TM, D_MODEL), _rows(TM, f), _rows(TM, f), _rows(TM, D_MODEL), acc, acc, acc],
        out_shape=[jax.ShapeDtypeStruct((t, D_MODEL), F32), jax.ShapeDtypeStruct((t, f), BF16),
                   jax.ShapeDtypeStruct((t, f), BF16), jax.ShapeDtypeStruct((t, D_MODEL), BF16),
                   jax.ShapeDtypeStruct((8, D_MODEL), F32), jax.ShapeDtypeStruct((8, D_MODEL), F32),
                   jax.ShapeDtypeStruct((8, D_MODEL), F32)],
        args=(y, gamma_f, target, x, gamma, g_act, u_act, wg_t, wu_t, wd), sem=("arbitrary",))


def _ffn_bwd_act(d, g_act, u_act, wd, name, hosted=()):
    t = d.shape[0]
    f = wd.shape[0]

    def body(d_ref, g_ref, u_ref, wd_ref, dg_ref, du_ref):
        db = (0.5 * d_ref[...]).astype(BF16)
        for j in range(f // FC):
            sl = slice(j * FC, (j + 1) * FC)
            da = _dot_nt(db, wd_ref[sl, :])
            g = g_ref[:, sl].astype(F32)
            u = u_ref[:, sl].astype(F32)
            s = jax.nn.sigmoid(g)
            dg_ref[:, sl] = (da * u * (s * (1.0 + g * (1.0 - s)))).astype(BF16)
            du_ref[:, sl] = (da * (g * s)).astype(BF16)

    return _call(
        body,
        name=name,
        grid=(t // TM,),
        in_specs=[_rows(TM, D_MODEL), _rows(TM, f), _rows(TM, f), _resident((f, D_MODEL))],
        out_specs=[_rows(TM, f), _rows(TM, f)],
        out_shape=[jax.ShapeDtypeStruct((t, f), BF16), jax.ShapeDtypeStruct((t, f), BF16)],
        args=(d, g_act, u_act, wd), sem=("parallel",), hosted=hosted)


def _ffn_bwd_in(d, x, gamma, dg, du, wg_t, wu_t, name, hosted=()):
    t = x.shape[0]
    f = wg_t.shape[0]

    def body(d_ref, x_ref, gam_ref, dg_ref, du_ref, wg_ref, wu_ref, dx_ref, dgam_ref):
        dh = _dot_nn(dg_ref[...], wg_ref[...]) + _dot_nn(du_ref[...], wu_ref[...])
        xh, r = _rms(x_ref[...])
        dxn, dgam = _rms_bwd(dh, xh, r, gam_ref[...])
        dx_ref[...] = d_ref[...] + dxn

        @pl.when(pl.program_id(0) == 0)
        def _():
            dgam_ref[...] = jnp.zeros_like(dgam_ref)

        dgam_ref[...] += dgam

    return _call(
        body,
        name=name,
        grid=(t // TM,),
        in_specs=[_rows(TM, D_MODEL), _rows(TM, D_MODEL), _resident((1, D_MODEL)), _rows(TM, f), _rows(TM, f),
                  _resident((f, D_MODEL)), _resident((f, D_MODEL))],
        out_specs=[_rows(TM, D_MODEL), pl.BlockSpec((8, D_MODEL), lambda i: (0, 0))],
        out_shape=[jax.ShapeDtypeStruct((t, D_MODEL), F32), jax.ShapeDtypeStruct((8, D_MODEL), F32)],
        args=(d, x, gamma, dg, du, wg_t, wu_t), sem=("arbitrary",), hosted=hosted)


def _mm_tn(pieces, b, name, tile=256, hosted=()):
    t, n = b.shape
    npc = len(pieces)
    counts = [p.shape[1] // tile for p in pieces]
    los = [sum(counts[:k]) for k in range(npc)]
    total = sum(counts)

    def body(*refs):
        a_refs, b_ref, o_ref = refs[:npc], refs[npc], refs[npc + 1]
        i = pl.program_id(0)
        for k in range(npc):
            @pl.when(jnp.logical_and(i >= los[k], i < los[k] + counts[k]))
            def _(k=k):
                o_ref[...] = _dot_tn(a_refs[k][...], b_ref[...]).astype(BF16)

    def a_spec(k):
        return pl.BlockSpec((t, tile), lambda i: (0, jnp.clip(i - los[k], 0, counts[k] - 1)))

    return _call(
        body,
        name=name,
        grid=(total,),
        in_specs=[a_spec(k) for k in range(npc)] + [_resident((t, n))],
        out_specs=[pl.BlockSpec((tile, n), lambda i: (i, 0))],
        out_shape=[jax.ShapeDtypeStruct((total * tile, n), BF16)],
        args=(*pieces, b), sem=("parallel",), hosted=hosted)[0]


def _proj_fwd(x, gamma, win_t, hosted=()):
    t = x.shape[0]

    def body(x_ref, gam_ref, w_ref, h_ref, qa_ref, qb_ref, gt_ref):
        xh, _ = _rms(x_ref[...])
        h = (xh * gam_ref[...]).astype(BF16)
        h_ref[...] = h
        for j in range(QKV_A // FC):
            qa_ref[:, j * FC:(j + 1) * FC] = _dot_nt(h, w_ref[j * FC:(j + 1) * FC, :]).astype(BF16)
        for j in range(QKV_B // FC):
            lo = QKV_A + j * FC
            qb_ref[:, j * FC:(j + 1) * FC] = _dot_nt(h, w_ref[lo:lo + FC, :]).astype(BF16)
        for j in range(2 * D_MODEL // FC):
            lo = QKV_A + QKV_B + j * FC
            gt_ref[:, j * FC:(j + 1) * FC] = _dot_nt(h, w_ref[lo:lo + FC, :]).astype(BF16)

    return _call(
        body,
        name="proj_fwd",
        grid=(t // TM,),
        in_specs=[_rows(TM, D_MODEL), _resident((1, D_MODEL)), _resident((IN_WIDTH, D_MODEL))],
        out_specs=[_rows(TM, D_MODEL), _rows(TM, QKV_A), _rows(TM, QKV_B), _rows(TM, 2 * D_MODEL)],
        out_shape=[jax.ShapeDtypeStruct((t, D_MODEL), BF16), jax.ShapeDtypeStruct((t, QKV_A), BF16),
                   jax.ShapeDtypeStruct((t, QKV_B), BF16), jax.ShapeDtypeStruct((t, 2 * D_MODEL), BF16)],
        args=(x, gamma, win_t), sem=("parallel",), hosted=hosted)


def _proj_bwd(d, x, gamma, pieces, win_t, hosted=()):
    t = x.shape[0]
    npc = len(pieces)
    widths = [p.shape[1] for p in pieces]
    los = [sum(widths[:k]) for k in range(npc)]

    def body(*refs):
        d_ref, x_ref, gam_ref = refs[:3]
        p_refs = refs[3:3 + npc]
        w_ref, dx_ref, db_ref, dgam_ref = refs[3 + npc:]
        dh = _dot_nn(p_refs[0][...], w_ref[0:widths[0], :])
        for k in range(1, npc):
            dh += _dot_nn(p_refs[k][...], w_ref[los[k]:los[k] + widths[k], :])
        xh, r = _rms(x_ref[...])
        dxn, dgam = _rms_bwd(dh, xh, r, gam_ref[...])
        dx = d_ref[...] + dxn
        dx_ref[...] = dx
        db_ref[...] = (0.5 * dx).astype(BF16)

        @pl.when(pl.program_id(0) == 0)
        def _():
            dgam_ref[...] = jnp.zeros_like(dgam_ref)

        dgam_ref[...] += dgam

    return _call(
        body,
        name="proj_bwd",
        grid=(t // TM,),
        in_specs=[_rows(TM, D_MODEL), _rows(TM, D_MODEL), _resident((1, D_MODEL))] + [_rows(TM, w) for w in widths]
        + [_resident((IN_WIDTH, D_MODEL))],
        out_specs=[_rows(TM, D_MODEL), _rows(TM, D_MODEL), pl.BlockSpec((8, D_MODEL), lambda i: (0, 0))],
        out_shape=[jax.ShapeDtypeStruct((t, D_MODEL), F32), jax.ShapeDtypeStruct((t, D_MODEL), BF16),
                   jax.ShapeDtypeStruct((8, D_MODEL), F32)],
        args=(d, x, gamma, *pieces, win_t), sem=("arbitrary",), hosted=hosted)


def _lane_half(shape):
    return lax.broadcasted_iota(jnp.int32, shape, len(shape) - 1) // D_HEAD


def _band_weights(q, kk, bias, sink, qs, pad):
    s = _dot_nt(q, kk) + bias
    if qs is not None:
        col = lax.broadcasted_iota(jnp.int32, s.shape, 1)
        s = jnp.where(col + qs >= pad, s, NEG_INF)
    m = jnp.max(s, axis=-1, keepdims=True)
    if sink is not None:
        m = jnp.maximum(m, sink)
    return jnp.exp(s - m), m


def _weighted_values(p, vv_ones, sink, m):
    r = _dot_nn(p.astype(BF16), vv_ones)
    den = r[:, LANES:2 * LANES]
    if sink is not None:
        den = den + jnp.exp(sink - m)
    return r[:, 0:LANES] / den


def _band_softmax(q, kk, bias, sink, qs, pad):
    p, m = _band_weights(q, kk, bias, sink, qs, pad)
    den = jnp.sum(p, axis=-1, keepdims=True)
    if sink is not None:
        den = den + jnp.exp(sink - m)
    return p, m, 1.0 / den


def _fill_padded(dst, src, pad):
    dst[0:pad, :] = jnp.zeros((pad,) + dst.shape[1:], dst.dtype)
    dst[pad:, :] = src


FWD_PAIRS = 4
BWD_PAIRS = 4


def _attn_a_fwd(qkv, bias, hosted=()):
    bsz, s_len, _ = qkv.shape
    pad = A_PREV * CHUNK
    band = TQ + pad
    pp = FWD_PAIRS
    w = pp * LANES
    nb = A_WIDTH // w

    def body(q_ref, k_ref, v_ref, b_ref, o_ref, kp, vp):
        i = pl.program_id(2)

        @pl.when(i == 0)
        def _():
            _fill_padded(kp, k_ref[...], pad)
            _fill_padded(vp, v_ref[...], pad)

        qs = pl.multiple_of(i * TQ, TQ)
        half = _lane_half((1, LANES))

        ones = jnp.ones((band, LANES), BF16)

        def block(masked):
            for pr in range(pp):
                sl = slice(pr * LANES, (pr + 1) * LANES)
                kk = kp[pl.ds(qs, band), sl]
                vv = jnp.concatenate([vp[pl.ds(qs, band), sl], ones], axis=1)
                q = q_ref[:, sl] * SCALE
                outs = []
                for j in range(2):
                    qm = jnp.where(half == j, q, jnp.zeros_like(q))
                    p, m = _band_weights(qm, kk, b_ref[2 * pr + j], None, qs if masked else None, pad)
                    outs.append(_weighted_values(p, vv, None, m))
                o_ref[:, sl] = jnp.where(half == 0, outs[0], outs[1]).astype(BF16)

        pl.when(i < pad // TQ)(lambda: block(True))
        pl.when(i >= pad // TQ)(lambda: block(False))

    return _call(
        body,
        name="attn_a_fwd",
        grid=(bsz, nb, s_len // TQ),
        in_specs=[pl.BlockSpec((None, TQ, w), lambda b, g, i: (b, i, g)),
                  pl.BlockSpec((None, s_len, w), lambda b, g, i: (b, 0, nb + g)),
                  pl.BlockSpec((None, s_len, w), lambda b, g, i: (b, 0, 2 * nb + g)),
                  pl.BlockSpec((2 * pp, TQ, band), lambda b, g, i: (g, 0, 0))],
        out_specs=[pl.BlockSpec((None, TQ, w), lambda b, g, i: (b, i, g))],
        out_shape=[jax.ShapeDtypeStruct((bsz, s_len, A_WIDTH), BF16)],
        scratch_shapes=[pltpu.VMEM((pad + s_len, w), BF16), pltpu.VMEM((pad + s_len, w), BF16)],
        args=(qkv, qkv, qkv, bias), sem=("arbitrary", "arbitrary", "arbitrary"), hosted=hosted)[0]


def _attn_a_bwd(qkv, bias, do, hosted=()):
    bsz, s_len, _ = qkv.shape
    pad = A_PREV * CHUNK
    band = TQ + pad
    n_i = s_len // TQ
    pp = BWD_PAIRS
    w = pp * LANES
    nb = A_WIDTH // w

    def body(q_ref, k_ref, v_ref, b_ref, do_ref, dq_ref, dk_ref, dv_ref, dbias_ref, kp, vp, dk_acc, dv_acc):
        b = pl.program_id(1)
        i = pl.program_id(2)

        @pl.when(i == 0)
        def _():
            _fill_padded(kp, k_ref[...], pad)
            _fill_padded(vp, v_ref[...], pad)
            dk_acc[...] = jnp.zeros_like(dk_acc)
            dv_acc[...] = jnp.zeros_like(dv_acc)

        @pl.when(jnp.logical_and(b == 0, i == 0))
        def _():
            dbias_ref[...] = jnp.zeros_like(dbias_ref)

        qs = pl.multiple_of(i * TQ, TQ)
        half = _lane_half((1, LANES))

        def block(masked):
            for pr in range(pp):
                sl = slice(pr * LANES, (pr + 1) * LANES)
                kk = kp[pl.ds(qs, band), sl]
                vv = vp[pl.ds(qs, band), sl]
                q = q_ref[:, sl] * SCALE
                dd = do_ref[:, sl]
                dqs, dks, dvs = [], [], []
                for j in range(2):
                    qm = jnp.where(half == j, q, jnp.zeros_like(q))
                    dm = jnp.where(half == j, dd, jnp.zeros_like(dd))
                    p, _, inv = _band_softmax(qm, kk, b_ref[2 * pr + j], None, qs if masked else None, pad)
                    pn = p * inv
                    dp = _dot_nt(dm, vv)
                    delta = jnp.sum(pn * dp, axis=-1, keepdims=True)
                    ds = pn * (dp - delta)
                    dbias_ref[2 * pr + j] += ds[:, band - REL_COLS:]
                    dsb = ds.astype(BF16)
                    dqs.append(_dot_nn(dsb, kk))
                    dks.append(_dot_tn(dsb, q))
                    dvs.append(_dot_tn(pn.astype(BF16), dd))
                dq_ref[:, sl] = (jnp.where(half == 0, dqs[0], dqs[1]) * SCALE).astype(BF16)
                dk_acc[pl.ds(qs, band), sl] += jnp.where(half == 0, dks[0], dks[1])
                dv_acc[pl.ds(qs, band), sl] += jnp.where(half == 0, dvs[0], dvs[1])

        pl.when(i < pad // TQ)(lambda: block(True))
        pl.when(i >= pad // TQ)(lambda: block(False))

        @pl.when(i == n_i - 1)
        def _():
            dk_ref[...] = dk_acc[pad:, :].astype(BF16)
            dv_ref[...] = dv_acc[pad:, :].astype(BF16)

    qspec = pl.BlockSpec((None, TQ, w), lambda g, b, i: (b, i, g))
    kvout = pl.BlockSpec((None, s_len, w), lambda g, b, i: (b, 0, g))
    wide = jax.ShapeDtypeStruct((bsz, s_len, A_WIDTH), BF16)
    return _call(
        body,
        name="attn_a_bwd",
        grid=(nb, bsz, n_i),
        in_specs=[qspec,
                  pl.BlockSpec((None, s_len, w), lambda g, b, i: (b, 0, nb + g)),
                  pl.BlockSpec((None, s_len, w), lambda g, b, i: (b, 0, 2 * nb + g)),
                  pl.BlockSpec((2 * pp, TQ, band), lambda g, b, i: (g, 0, 0)),
                  qspec],
        out_specs=[qspec, kvout, kvout, pl.BlockSpec((2 * pp, TQ, REL_COLS), lambda g, b, i: (g, 0, 0))],
        out_shape=[wide, wide, wide, jax.ShapeDtypeStruct((A_HEADS, TQ, REL_COLS), F32)],
        scratch_shapes=[pltpu.VMEM((pad + s_len, w), BF16), pltpu.VMEM((pad + s_len, w), BF16),
                        pltpu.VMEM((pad + s_len, w), F32), pltpu.VMEM((pad + s_len, w), F32)],
        args=(qkv, qkv, qkv, bias, do), sem=("arbitrary", "arbitrary", "arbitrary"), hosted=hosted)


def _fill_padded_dup(dst, src, pad, h, half):
    other = pltpu.roll(src, D_HEAD, 1)
    _fill_padded(dst, jnp.where(half == h, src, other), pad)


def _attn_b_fwd(qkv, bias, sink):
    bsz, s_len, _ = qkv.shape
    pad = B_PREV * CHUNK
    band = TQ + pad
    kcol = B_Q_WIDTH // LANES
    npair = B_Q_HEADS // 2

    def body(q_ref, k_ref, v_ref, b_ref, s_ref, o_ref, kp, vp):
        i = pl.program_id(1)
        half = _lane_half((1, LANES))

        @pl.when(i == 0)
        def _():
            for h in range(B_KV_HEADS):
                _fill_padded_dup(kp.at[h], k_ref[...], pad, h, half)
                _fill_padded_dup(vp.at[h], v_ref[...], pad, h, half)

        qs = pl.multiple_of(i * TQ, TQ)

        ones = jnp.ones((band, LANES), BF16)

        def block(masked):
            for pr in range(npair):
                h = pr // (B_GROUP // 2)
                sl = slice(pr * LANES, (pr + 1) * LANES)
                kk = kp[h, pl.ds(qs, band), :]
                vv = jnp.concatenate([vp[h, pl.ds(qs, band), :], ones], axis=1)
                q = q_ref[:, sl] * SCALE
                outs = []
                for j in range(2):
                    qm = jnp.where(half == j, q, jnp.zeros_like(q))
                    sink = s_ref[2 * pr + j][0:1, 0:1]
                    p, m = _band_weights(qm, kk, b_ref[2 * pr + j], sink, qs if masked else None, pad)
                    outs.append(_weighted_values(p, vv, sink, m))
                o_ref[:, sl] = jnp.where(half == 0, outs[0], outs[1]).astype(BF16)

        pl.when(i < -(-pad // TQ))(lambda: block(True))
        pl.when(i >= -(-pad // TQ))(lambda: block(False))

    return pl.pallas_call(
        body,
        name="attn_b_fwd",
        grid=(bsz, s_len // TQ),
        in_specs=[pl.BlockSpec((None, TQ, B_Q_WIDTH), lambda b, i: (b, i, 0)),
                  pl.BlockSpec((None, s_len, LANES), lambda b, i: (b, 0, kcol)),
                  pl.BlockSpec((None, s_len, LANES), lambda b, i: (b, 0, kcol + 1)),
                  pl.BlockSpec((B_Q_HEADS, TQ, band), lambda b, i: (0, 0, 0)),
                  pl.BlockSpec((B_Q_HEADS, 8, LANES), lambda b, i: (0, 0, 0))],
        out_specs=pl.BlockSpec((None, TQ, B_Q_WIDTH), lambda b, i: (b, i, 0)),
        out_shape=jax.ShapeDtypeStruct((bsz, s_len, B_Q_WIDTH), BF16),
        scratch_shapes=[pltpu.VMEM((B_KV_HEADS, pad + s_len, LANES), BF16),
                        pltpu.VMEM((B_KV_HEADS, pad + s_len, LANES), BF16)],
        compiler_params=_cparams(("arbitrary", "arbitrary")),
    )(qkv, qkv, qkv, bias, sink)


def _attn_b_bwd(qkv, bias, sink, do, hosted=()):
    bsz, s_len, _ = qkv.shape
    pad = B_PREV * CHUNK
    band = TQ + pad
    kcol = B_Q_WIDTH // LANES
    n_i = s_len // TQ
    pp = B_GROUP // 2

    def body(q_ref, k_ref, v_ref, b_ref, s_ref, do_ref, dq_ref, dkv_ref, dsink_ref, kp, vp, dk_acc, dv_acc):
        b = pl.program_id(0)
        i = pl.program_id(1)
        half = _lane_half((1, LANES))

        @pl.when(i == 0)
        def _():
            for h in range(B_KV_HEADS):
                _fill_padded_dup(kp.at[h], k_ref[...], pad, h, half)
                _fill_padded_dup(vp.at[h], v_ref[...], pad, h, half)
            dk_acc[...] = jnp.zeros_like(dk_acc)
            dv_acc[...] = jnp.zeros_like(dv_acc)

        @pl.when(jnp.logical_and(b == 0, i == 0))
        def _():
            dsink_ref[...] = jnp.zeros_like(dsink_ref)

        qs = pl.multiple_of(i * TQ, TQ)

        def block(masked):
            heads_dk, heads_dv = [], []
            for h in range(B_KV_HEADS):
                kk = kp[h, pl.ds(qs, band), :]
                vv = vp[h, pl.ds(qs, band), :]
                dk2 = jnp.zeros((band, LANES), F32)
                dv2 = jnp.zeros((band, LANES), F32)
                for pr in range(pp * h, pp * (h + 1)):
                    sl = slice(pr * LANES, (pr + 1) * LANES)
                    q = q_ref[:, sl] * SCALE
                    dd = do_ref[:, sl]
                    dqs, dks, dvs = [], [], []
                    for j in range(2):
                        qm = jnp.where(half == j, q, jnp.zeros_like(q))
                        dm = jnp.where(half == j, dd, jnp.zeros_like(dd))
                        sink = s_ref[2 * pr + j][0:1, 0:1]
                        p, m, inv = _band_softmax(qm, kk, b_ref[2 * pr + j], sink, qs if masked else None, pad)
                        pn = p * inv
                        dp = _dot_nt(dm, vv)
                        delta = jnp.sum(pn * dp, axis=-1, keepdims=True)
                        ds = pn * (dp - delta)
                        dsb = ds.astype(BF16)
                        dqs.append(_dot_nn(dsb, kk))
                        dks.append(_dot_tn(dsb, q))
                        dvs.append(_dot_tn(pn.astype(BF16), dd))
                        dsk = jnp.sum(-(jnp.exp(sink - m) * inv) * delta, axis=0, keepdims=True)
                        dsink_ref[2 * pr + j] += jnp.broadcast_to(dsk, (8, LANES))
                    dq_ref[:, sl] = (jnp.where(half == 0, dqs[0], dqs[1]) * SCALE).astype(BF16)
                    dk2 = dk2 + jnp.where(half == 0, dks[0], dks[1])
                    dv2 = dv2 + jnp.where(half == 0, dvs[0], dvs[1])
                heads_dk.append(dk2 + pltpu.roll(dk2, D_HEAD, 1))
                heads_dv.append(dv2 + pltpu.roll(dv2, D_HEAD, 1))
            dk_acc[pl.ds(qs, band), :] += jnp.where(half == 0, heads_dk[0], heads_dk[1])
            dv_acc[pl.ds(qs, band), :] += jnp.where(half == 0, heads_dv[0], heads_dv[1])

        pl.when(i < -(-pad // TQ))(lambda: block(True))
        pl.when(i >= -(-pad // TQ))(lambda: block(False))

        @pl.when(i == n_i - 1)
        def _():
            dkv_ref[:, 0:LANES] = dk_acc[pad:, :].astype(BF16)
            dkv_ref[:, LANES:2 * LANES] = dv_acc[pad:, :].astype(BF16)

    qspec = pl.BlockSpec((None, TQ, B_Q_WIDTH), lambda b, i: (b, i, 0))
    return _call(
        body,
        name="attn_b_bwd",
        grid=(bsz, n_i),
        in_specs=[qspec,
                  pl.BlockSpec((None, s_len, LANES), lambda b, i: (b, 0, kcol)),
                  pl.BlockSpec((None, s_len, LANES), lambda b, i: (b, 0, kcol + 1)),
                  pl.BlockSpec((B_Q_HEADS, TQ, band), lambda b, i: (0, 0, 0)),
                  pl.BlockSpec((B_Q_HEADS, 8, LANES), lambda b, i: (0, 0, 0)),
                  qspec],
        out_specs=[qspec, pl.BlockSpec((None, s_len, 2 * LANES), lambda b, i: (b, 0, 0)),
                   pl.BlockSpec((B_Q_HEADS, 8, LANES), lambda b, i: (0, 0, 0))],
        out_shape=[jax.ShapeDtypeStruct((bsz, s_len, B_Q_WIDTH), BF16),
                   jax.ShapeDtypeStruct((bsz, s_len, 2 * B_KV_WIDTH), BF16),
                   jax.ShapeDtypeStruct((B_Q_HEADS, 8, LANES), F32)],
        scratch_shapes=[pltpu.VMEM((B_KV_HEADS, pad + s_len, LANES), BF16),
                        pltpu.VMEM((B_KV_HEADS, pad + s_len, LANES), BF16),
                        pltpu.VMEM((pad + s_len, LANES), F32), pltpu.VMEM((pad + s_len, LANES), F32)],
        args=(qkv, qkv, qkv, bias, sink, do), sem=("arbitrary", "arbitrary"), hosted=hosted)


REL_COLS = 3 * 128
REL_WRAP = 512


def _bias_a_build(tv, hosted=()):
    h = tv.shape[0]
    pad = A_PREV * CHUNK
    band = TQ + pad

    def body(tv_ref, o_ref):
        row = tv_ref[...]
        x = jnp.broadcast_to(row, (TQ, REL_WRAP))
        r = lax.broadcasted_iota(jnp.int32, x.shape, 0)
        for bit in range(8):
            sh = 1 << bit
            x = jnp.where((r & sh) != 0, pltpu.roll(x, sh, 1), x)
        far = jnp.broadcast_to(row[:, 0:1], (TQ, band - REL_COLS))
        full = jnp.concatenate([far, x[:, REL_WRAP // 2:REL_WRAP], x[:, 0:REL_COLS - REL_WRAP // 2]], axis=1)
        qc = (lax.broadcasted_iota(jnp.int32, full.shape, 0) + pad) // CHUNK
        kc = lax.broadcasted_iota(jnp.int32, full.shape, 1) // CHUNK
        ok = jnp.logical_and(kc <= qc, kc >= qc - A_PREV)
        o_ref[...] = jnp.where(ok, full, NEG_INF)

    return _call(
        body,
        name="bias_a_build",
        grid=(h,),
        in_specs=[pl.BlockSpec((None, 1, REL_WRAP), lambda hh: (hh, 0, 0))],
        out_specs=[pl.BlockSpec((None, TQ, band), lambda hh: (hh, 0, 0))],
        out_shape=[jax.ShapeDtypeStruct((h, TQ, band), F32)],
        args=(tv,), sem=("parallel",), hosted=hosted)[0]


def _relbias_grad(dbias, hosted=()):
    h, rows, _ = dbias.shape

    def body(d_ref, o_ref):
        x = d_ref[...]
        r = lax.broadcasted_iota(jnp.int32, x.shape, 0)
        c = lax.broadcasted_iota(jnp.int32, x.shape, 1) - r
        x = jnp.where(jnp.logical_and(c >= 1, c < REL_TABLE), x, 0.0)
        for bit in range(8):
            sh = 1 << bit
            x = jnp.where((r & sh) != 0, pltpu.roll(x, REL_COLS - sh, 1), x)
        diag = jnp.sum(x, axis=0, keepdims=True)
        lane = lax.broadcasted_iota(jnp.int32, diag.shape, 1)
        diag = jnp.where(jnp.logical_and(lane >= 1, lane < REL_TABLE), diag, 0.0)
        rest = -jnp.sum(diag, axis=1, keepdims=True)
        o_ref[...] = jnp.broadcast_to(jnp.where(lane == 0, rest, diag), o_ref.shape)

    return _call(
        body,
        name="relbias_grad",
        grid=(h,),
        in_specs=[pl.BlockSpec((None, rows, REL_COLS), lambda hh: (hh, 0, 0))],
        out_specs=[pl.BlockSpec((None, 8, REL_COLS), lambda hh: (hh, 0, 0))],
        out_shape=[jax.ShapeDtypeStruct((h, 8, REL_COLS), F32)],
        args=(dbias,), sem=("parallel",), hosted=hosted)[0]


def _mix_out_fwd(x, oa, ob, gates, proj_t, wout):
    t = x.shape[0]

    def body(x_ref, oa_ref, ob_ref, gt_ref, pt_ref, wo_ref, y_ref, ya_ref, yb_ref, mg_ref):
        ya = _dot_nt(oa_ref[...], pt_ref[:, 0:A_WIDTH])
        yb = _dot_nt(ob_ref[...], pt_ref[:, A_WIDTH:A_WIDTH + B_Q_WIDTH])
        ya_ref[...] = ya.astype(BF16)
        yb_ref[...] = yb.astype(BF16)
        mg = (jax.nn.sigmoid(gt_ref[:, 0:D_MODEL].astype(F32)) * ya
              + jax.nn.sigmoid(gt_ref[:, D_MODEL:2 * D_MODEL].astype(F32)) * yb)
        mgb = mg.astype(BF16)
        mg_ref[...] = mgb
        y_ref[...] = x_ref[...] + _dot_nn(mgb, wo_ref[...])

    return pl.pallas_call(
        body,
        name="mix_out_fwd",
        grid=(t // TM,),
        in_specs=[_rows(TM, D_MODEL), _rows(TM, A_WIDTH), _rows(TM, B_Q_WIDTH), _rows(TM, 2 * D_MODEL),
                  _resident((D_MODEL, A_WIDTH + B_Q_WIDTH)), _resident((D_MODEL, D_MODEL))],
        out_specs=[_rows(TM, D_MODEL), _rows(TM, D_MODEL), _rows(TM, D_MODEL), _rows(TM, D_MODEL)],
        out_shape=[jax.ShapeDtypeStruct((t, D_MODEL), F32), jax.ShapeDtypeStruct((t, D_MODEL), BF16),
                   jax.ShapeDtypeStruct((t, D_MODEL), BF16), jax.ShapeDtypeStruct((t, D_MODEL), BF16)],
        compiler_params=_cparams(("parallel",)),
    )(x, oa, ob, gates, proj_t, wout)


def _mix_out_bwd(d, gates, ya, yb, mg, oa, ob, proj_t, wout, hosted=()):
    t = d.shape[0]
    nt = t // TM

    def body(d_ref, gt_ref, ya_ref, yb_ref, mg_ref, oa_ref, ob_ref, pt_ref, wo_ref,
             doa_ref, dob_ref, dgt_ref, gwo_ref, gwp_ref, acc_o, acc_p):
        i = pl.program_id(0)
        db = d_ref[...].astype(BF16)
        dmg = _dot_nt(db, wo_ref[...])
        sa = jax.nn.sigmoid(gt_ref[:, 0:D_MODEL].astype(F32))
        sb = jax.nn.sigmoid(gt_ref[:, D_MODEL:2 * D_MODEL].astype(F32))
        dya = (dmg * sa).astype(BF16)
        dyb = (dmg * sb).astype(BF16)
        dgt_ref[:, 0:D_MODEL] = (dmg * ya_ref[...].astype(F32) * (sa * (1.0 - sa))).astype(BF16)
        dgt_ref[:, D_MODEL:2 * D_MODEL] = (dmg * yb_ref[...].astype(F32) * (sb * (1.0 - sb))).astype(BF16)
        doa_ref[...] = _dot_nn(dya, pt_ref[:, 0:A_WIDTH]).astype(BF16)
        dob_ref[...] = _dot_nn(dyb, pt_ref[:, A_WIDTH:A_WIDTH + B_Q_WIDTH]).astype(BF16)

        @pl.when(i == 0)
        def _():
            acc_o[...] = jnp.zeros_like(acc_o)
            acc_p[...] = jnp.zeros_like(acc_p)

        acc_o[...] += _dot_tn(mg_ref[...], db)
        acc_p[:, 0:A_WIDTH] += _dot_tn(dya, oa_ref[...])
        acc_p[:, A_WIDTH:A_WIDTH + B_Q_WIDTH] += _dot_tn(dyb, ob_ref[...])

        @pl.when(i == nt - 1)
        def _():
            gwo_ref[...] = acc_o[...].astype(BF16)
            gwp_ref[...] = acc_p[...].astype(BF16)

    whole = pl.BlockSpec((D_MODEL, D_MODEL), lambda i: (0, 0))
    return _call(
        body,
        name="mix_out_bwd",
        grid=(nt,),
        in_specs=[_rows(TM, D_MODEL), _rows(TM, 2 * D_MODEL), _rows(TM, D_MODEL), _rows(TM, D_MODEL),
                  _rows(TM, D_MODEL), _rows(TM, A_WIDTH), _rows(TM, B_Q_WIDTH),
                  _resident((D_MODEL, A_WIDTH + B_Q_WIDTH)), _resident((D_MODEL, D_MODEL))],
        out_specs=[_rows(TM, A_WIDTH), _rows(TM, B_Q_WIDTH), _rows(TM, 2 * D_MODEL), whole, whole],
        out_shape=[jax.ShapeDtypeStruct((t, A_WIDTH), BF16), jax.ShapeDtypeStruct((t, B_Q_WIDTH), BF16),
                   jax.ShapeDtypeStruct((t, 2 * D_MODEL), BF16), jax.ShapeDtypeStruct((D_MODEL, D_MODEL), BF16),
                   jax.ShapeDtypeStruct((D_MODEL, D_MODEL), BF16)],
        scratch_shapes=[pltpu.VMEM((D_MODEL, D_MODEL), F32), pltpu.VMEM((D_MODEL, A_WIDTH + B_Q_WIDTH), F32)],
        args=(d, gates, ya, yb, mg, oa, ob, proj_t, wout), sem=("arbitrary",), hosted=hosted)


def _place():
    x, y, c = lax.axis_index("x"), lax.axis_index("y"), lax.axis_index("c")
    chips = [(1 - x, y), (x, 1 - y), (1 - x, 1 - y)]
    return x, y, c, chips


class _Gather:
    per = 8

    def __init__(self, shards):
        n = len(shards)
        self.inputs = list(shards)
        self.out_shape = [jax.ShapeDtypeStruct((N_DEV * s.shape[0], s.shape[1]), s.dtype) for s in shards]
        self.scratch = [pltpu.SemaphoreType.DMA((n * self.per,)), pltpu.SemaphoreType.DMA((n * self.per,)),
                        pltpu.SemaphoreType.DMA((n,))]
        self.result = None

    def _parts(self, ins, outs, sems):
        send_sems, recv_sems, local_sems = sems
        x, y, c, chips = _place()
        me, sibling = (x, y, c), (x, y, 1 - c)
        xn, yn, dg = chips
        n = len(ins)

        def rows(k, p, part=None):
            r = ins[k].shape[0]
            base = (4 * p[0] + 2 * p[1] + p[2]) * r
            if part is None:
                return outs[k].at[pl.ds(base, r), :]
            return outs[k].at[pl.ds(base + part * (r // 2), r // 2), :]

        def copy(k, slot, block, to, src=None, part=None):
            return pltpu.make_async_remote_copy(
                src_ref=rows(k, block, part) if src is None else src, dst_ref=rows(k, block, part),
                send_sem=send_sems.at[k * self.per + slot], recv_sem=recv_sems.at[k * self.per + slot],
                device_id=to, device_id_type=MESH)

        mine = [pltpu.make_async_copy(ins[k], rows(k, me), local_sems.at[k]) for k in range(n)]
        sends, lands = [], []
        for k in range(n):
            sends.append({
                0: copy(k, 0, me, sibling, src=ins[k]),
                1: copy(k, 1, me, (*xn, c), src=ins[k]),
                2: copy(k, 2, me, (*yn, c), src=ins[k]),
                3: copy(k, 3, (*xn, c), (*yn, c), part=0),
                4: copy(k, 4, (*yn, c), (*xn, c), part=1),
                5: copy(k, 5, (*xn, c), sibling),
                6: copy(k, 6, (*yn, c), sibling),
                7: copy(k, 7, (*dg, c), sibling)})
            lands.append({
                0: copy(k, 0, sibling, me),
                1: copy(k, 1, (*xn, c), me),
                2: copy(k, 2, (*yn, c), me),
                3: copy(k, 3, (*dg, c), me, part=0),
                4: copy(k, 4, (*dg, c), me, part=1),
                5: copy(k, 5, (*xn, 1 - c), me),
                6: copy(k, 6, (*yn, 1 - c), me),
                7: copy(k, 7, (*dg, 1 - c), me)})
        return n, mine, sends, lands

    def start(self, ins, outs, sems):
        n, mine, sends, _ = self._parts(ins, outs, sems)
        for cp in mine:
            cp.start()
        for slot in (0, 1, 2):
            for k in range(n):
                sends[k][slot].start()

    def relay(self, ins, outs, sems):
        n, _, sends, lands = self._parts(ins, outs, sems)
        for k in range(n):
            lands[k][1].wait_recv()
            sends[k][3].start()
            sends[k][5].start()
        for k in range(n):
            lands[k][2].wait_recv()
            sends[k][4].start()
            sends[k][6].start()

    def forward(self, ins, outs, sems):
        n, _, sends, lands = self._parts(ins, outs, sems)
        for k in range(n):
            lands[k][3].wait_recv()
            lands[k][4].wait_recv()
            sends[k][7].start()

    def finish(self, ins, outs, sems):
        n, mine, sends, lands = self._parts(ins, outs, sems)
        for k in range(n):
            for slot in (0, 5, 6, 7):
                lands[k][slot].wait_recv()
        for k in range(n):
            for slot in range(self.per):
                sends[k][slot].wait_send()
        for cp in mine:
            cp.wait()


class _PairExchange:
    def __init__(self, grads):
        n = len(grads)
        self.inputs = list(grads)
        self.out_shape = [jax.ShapeDtypeStruct((g.shape[0] // 2, g.shape[1]), g.dtype) for g in grads]
        self.scratch = [pltpu.SemaphoreType.DMA((n * N_CHIP,)), pltpu.SemaphoreType.DMA((n * N_CHIP,))]
        self.result = None

    def _copies(self, ins, outs, sems):
        send_sems, recv_sems = sems
        x, y, c, _ = _place()
        copies = []
        for k in range(len(ins)):
            r = ins[k].shape[0] // N_DEV
            for q in range(N_CHIP):
                copies.append(pltpu.make_async_remote_copy(
                    src_ref=ins[k].at[pl.ds((2 * q + 1 - c) * r, r), :], dst_ref=outs[k].at[pl.ds(q * r, r), :],
                    send_sem=send_sems.at[k * N_CHIP + q], recv_sem=recv_sems.at[k * N_CHIP + q],
                    device_id=(x, y, 1 - c), device_id_type=MESH))
        return copies

    def start(self, ins, outs, sems):
        for cp in self._copies(ins, outs, sems):
            cp.start()

    def relay(self, ins, outs, sems):
        pass

    def forward(self, ins, outs, sems):
        pass

    def finish(self, ins, outs, sems):
        copies = self._copies(ins, outs, sems)
        for cp in copies:
            cp.wait_recv()
        for cp in copies:
            cp.wait_send()


class _ChipExchange(_PairExchange):
    def __init__(self, psums):
        n = len(psums)
        self.inputs = list(psums)
        self.out_shape = [jax.ShapeDtypeStruct((3 * p.shape[0] // N_CHIP, p.shape[1]), p.dtype) for p in psums]
        self.scratch = [pltpu.SemaphoreType.DMA((n * 3,)), pltpu.SemaphoreType.DMA((n * 3,))]
        self.result = None

    def _copies(self, ins, outs, sems):
        send_sems, recv_sems = sems
        _, _, c, chips = _place()
        copies = []
        for k in range(len(ins)):
            r = ins[k].shape[0] // N_CHIP
            for j, chip in enumerate(chips):
                copies.append(pltpu.make_async_remote_copy(
                    src_ref=ins[k].at[pl.ds((2 * chip[0] + chip[1]) * r, r), :], dst_ref=outs[k].at[pl.ds(j * r, r), :],
                    send_sem=send_sems.at[k * 3 + j], recv_sem=recv_sems.at[k * 3 + j],
                    device_id=(*chip, c), device_id_type=MESH))
        return copies


def _exchange_alone(xchg, name):
    n_in, n_out = len(xchg.inputs), len(xchg.out_shape)

    def body(*refs):
        ins, outs, sems = refs[:n_in], refs[n_in:n_in + n_out], refs[n_in + n_out:]
        xchg.start(ins, outs, sems)
        xchg.relay(ins, outs, sems)
        xchg.forward(ins, outs, sems)
        xchg.finish(ins, outs, sems)

    xchg.result = list(pl.pallas_call(
        body, name=name, in_specs=[_hbm()] * n_in, out_specs=[_hbm()] * n_out, out_shape=xchg.out_shape,
        scratch_shapes=xchg.scratch)(*xchg.inputs))
    return xchg.result


def _pair_sum(core, grads, recvd, name):
    n = len(grads)
    r = grads[0].shape[0] // N_DEV
    cdim = grads[0].shape[1]
    tr = r // 2 if r % 32 == 0 else r
    nt = r // tr

    def body(core_ref, *refs):
        del core_ref
        for k in range(n):
            refs[2 * n + k][...] = (refs[k][...].astype(F32) + refs[n + k][...].astype(F32)).astype(BF16)

    gspec = pl.BlockSpec((tr, cdim), lambda q, i, core_ref: ((2 * q + core_ref[0]) * nt + i, 0))
    rspec = pl.BlockSpec((tr, cdim), lambda q, i, core_ref: (q * nt + i, 0))
    return pl.pallas_call(
        body,
        name=name,
        grid_spec=pltpu.PrefetchScalarGridSpec(
            num_scalar_prefetch=1, grid=(N_CHIP, nt), in_specs=[gspec] * n + [rspec] * n, out_specs=[rspec] * n),
        out_shape=[jax.ShapeDtypeStruct((N_CHIP * r, cdim), BF16) for _ in range(n)],
        compiler_params=_cparams(("parallel", "parallel")),
    )(core, *grads, *recvd)


def _final_sum(chip, psums, recvd, name):
    n = len(psums)
    r = psums[0].shape[0] // N_CHIP
    cdim = psums[0].shape[1]
    tr = r // 2 if r % 32 == 0 else r
    nt = r // tr

    def body(chip_ref, *refs):
        del chip_ref
        for k in range(n):
            got = refs[n + k]
            tot = refs[k][...].astype(F32) + got[0].astype(F32)
            tot = tot + got[1].astype(F32)
            tot = tot + got[2].astype(F32)
            refs[2 * n + k][...] = tot

    pspec = pl.BlockSpec((tr, cdim), lambda i, chip_ref: (chip_ref[0] * nt + i, 0))
    rspec = pl.BlockSpec((3, tr, cdim), lambda i, chip_ref: (0, i, 0))
    ospec = pl.BlockSpec((tr, cdim), lambda i, chip_ref: (i, 0))
    return pl.pallas_call(
        body,
        name=name,
        grid_spec=pltpu.PrefetchScalarGridSpec(
            num_scalar_prefetch=1, grid=(nt,), in_specs=[pspec] * n + [rspec] * n, out_specs=[ospec] * n),
        out_shape=[jax.ShapeDtypeStruct((r, cdim), F32) for _ in range(n)],
        compiler_params=_cparams(("parallel",)),
    )(chip, *psums, *[g.reshape(3, r, cdim) for g in recvd])


SMALL_ROWS = 16


def _all_reduce_small(part):
    def body(p_ref, o_ref, buf, send_sems, recv_sems):
        x, y, c, _ = _place()
        me = 4 * x + 2 * y + c
        buf[me] = p_ref[...]
        copies = []
        for d in range(1, N_DEV):
            peer = me ^ d
            copies.append(pltpu.make_async_remote_copy(
                src_ref=p_ref, dst_ref=buf.at[me], send_sem=send_sems.at[d - 1], recv_sem=recv_sems.at[d - 1],
                device_id=(peer // 4, (peer // 2) % 2, peer % 2), device_id_type=MESH))
        for cp in copies:
            cp.start()
        for cp in copies:
            cp.wait_recv()
        for cp in copies:
            cp.wait_send()
        tot = buf[0]
        for d in range(1, N_DEV):
            tot = tot + buf[d]
        o_ref[...] = tot

    return pl.pallas_call(
        body,
        name="all_reduce_small",
        in_specs=[pl.BlockSpec(memory_space=pltpu.VMEM)],
        out_specs=pl.BlockSpec(memory_space=pltpu.VMEM),
        out_shape=jax.ShapeDtypeStruct(part.shape, F32),
        scratch_shapes=[pltpu.VMEM((N_DEV,) + part.shape, F32), pltpu.SemaphoreType.DMA((N_DEV - 1,)),
                        pltpu.SemaphoreType.DMA((N_DEV - 1,))],
    )(part)


ADAMW_STEPS = 4


def _adamw(ws, gs, ms, vs, name, hosted=()):
    n = len(ws)
    steps = ADAMW_STEPS if all(w.shape[0] % (8 * ADAMW_STEPS) == 0 for w in ws) else 1
    c1 = 1.0 - ADAM_B1 ** ADAM_STEP
    c2 = 1.0 - ADAM_B2 ** ADAM_STEP

    def body(*refs):
        for k in range(n):
            w, g, m, v = (refs[j * n + k][...] for j in range(4))
            m2 = ADAM_B1 * m + (1.0 - ADAM_B1) * g
            v2 = ADAM_B2 * v + (1.0 - ADAM_B2) * (g * g)
            delta = -ADAM_LR * ((m2 * (1.0 / c1)) / (jnp.sqrt(v2 * (1.0 / c2)) + ADAM_EPS) + ADAM_WD * w)
            refs[4 * n + k][...] = delta
            refs[5 * n + k][...] = m2
            refs[6 * n + k][...] = v2

    specs = [pl.BlockSpec((w.shape[0] // steps, w.shape[1]), lambda i: (i, 0)) for w in ws]
    shapes = [jax.ShapeDtypeStruct(w.shape, F32) for w in ws]
    outs = _call(
        body,
        name=name,
        grid=(steps,),
        in_specs=specs * 4,
        out_specs=specs * 3,
        out_shape=shapes * 3,
        args=(*ws, *gs, *ms, *vs), sem=("parallel",), hosted=hosted)
    return outs[:n], outs[n:2 * n], outs[2 * n:]


def _adamw_reduced(chip, ws, psums, recvd, ms, vs, steps, name):
    n = len(ws)
    c1 = 1.0 - ADAM_B1 ** ADAM_STEP
    c2 = 1.0 - ADAM_B2 ** ADAM_STEP

    def body(chip_ref, *refs):
        del chip_ref
        for k in range(n):
            w, m, v = (refs[j * n + k][...] for j in (0, 3, 4))
            got = refs[2 * n + k]
            g = refs[n + k][...].astype(F32) + got[0].astype(F32)
            g = g + got[1].astype(F32)
            g = g + got[2].astype(F32)
            m2 = ADAM_B1 * m + (1.0 - ADAM_B1) * g
            v2 = ADAM_B2 * v + (1.0 - ADAM_B2) * (g * g)
            refs[5 * n + k][...] = g
            refs[6 * n + k][...] = -ADAM_LR * (
                (m2 * (1.0 / c1)) / (jnp.sqrt(v2 * (1.0 / c2)) + ADAM_EPS) + ADAM_WD * w)
            refs[7 * n + k][...] = m2
            refs[8 * n + k][...] = v2

    def blk(w):
        return (w.shape[0] // steps, w.shape[1])

    own = [pl.BlockSpec(blk(w), lambda i, chip_ref: (i, 0)) for w in ws]
    psum = [pl.BlockSpec(blk(w), lambda i, chip_ref: (chip_ref[0] * steps + i, 0)) for w in ws]
    recv = [pl.BlockSpec((3,) + blk(w), lambda i, chip_ref: (0, i, 0)) for w in ws]
    shapes = [jax.ShapeDtypeStruct(w.shape, F32) for w in ws]
    outs = pl.pallas_call(
        body,
        name=name,
        grid_spec=pltpu.PrefetchScalarGridSpec(
            num_scalar_prefetch=1, grid=(steps,), in_specs=own + psum + recv + own + own, out_specs=own * 4),
        out_shape=shapes * 4,
        compiler_params=_cparams(("parallel",)),
    )(chip, *ws, *psums, *[r.reshape((3,) + w.shape) for r, w in zip(recvd, ws)], *ms, *vs)
    return outs[:n], outs[n:2 * n], outs[2 * n:3 * n], outs[3 * n:]


def _bias_b():
    pad = B_PREV * CHUNK
    slopes = np.array([2.0 ** (-8.0 * (i + 1) / B_Q_HEADS) for i in range(B_Q_HEADS)], dtype=np.float32)
    dist = np.abs(np.arange(TQ)[:, None] - np.arange(TQ + pad)[None, :] + pad).astype(np.float32)
    bias = -slopes.reshape(B_Q_HEADS, 1, 1) * dist[None]
    qc = (np.arange(TQ)[:, None] + pad) // CHUNK
    kc = np.arange(TQ + pad)[None, :] // CHUNK
    allowed = (kc <= qc) & (kc >= qc - B_PREV)
    return np.where(allowed[None], bias, np.float32(NEG_INF)).astype(np.float32)


def kernel(x, ffn1_norm, ffn1_w_gate, ffn1_w_up, ffn1_w_down, mix_norm, w_in, rel_bias, sinks, w_proj_a, w_proj_b, w_out, ffn2_norm, ffn2_w_gate, ffn2_w_up, ffn2_w_down, final_norm, loss_target, m_ffn1_norm, m_ffn1_w_gate, m_ffn1_w_up, m_ffn1_w_down, m_mix_norm, m_w_in, m_rel_bias, m_sinks, m_w_proj_a, m_w_proj_b, m_w_out, m_ffn2_norm, m_ffn2_w_gate, m_ffn2_w_up, m_ffn2_w_down, m_final_norm, v_ffn1_norm, v_ffn1_w_gate, v_ffn1_w_up, v_ffn1_w_down, v_mix_norm, v_w_in, v_rel_bias, v_sinks, v_w_proj_a, v_w_proj_b, v_w_out, v_ffn2_norm, v_ffn2_w_gate, v_ffn2_w_up, v_ffn2_w_down, v_final_norm):
    bsz, s_len, _ = x.shape
    t = bsz * s_len
    core = lax.axis_index("c").astype(jnp.int32).reshape(1)
    chip = (2 * lax.axis_index("x") + lax.axis_index("y")).astype(jnp.int32).reshape(1)

    proj_rows = jnp.concatenate([w_proj_a.T, w_proj_b.T], axis=1)
    sh_g1, sh_u1, sh_d1, sh_in, sh_proj, sh_out, sh_g2, sh_u2, sh_d2 = _to_bf16(
        [ffn1_w_gate.T, ffn1_w_up.T, ffn1_w_down, w_in.T, proj_rows, w_out, ffn2_w_gate.T, ffn2_w_up.T, ffn2_w_down],
        "weights_to_bf16")

    gather_up1 = _Gather([sh_g1, sh_u1])
    far = jnp.broadcast_to(rel_bias[:, REL_TABLE - 1:REL_TABLE], (A_HEADS, REL_WRAP // 2))
    tv = jnp.concatenate([far, jnp.flip(rel_bias, axis=1), jnp.zeros((A_HEADS, REL_WRAP // 2 - REL_TABLE), F32)], axis=1)
    bias_a = _bias_a_build(tv.reshape(A_HEADS, 1, REL_WRAP), hosted=[gather_up1])
    wg1, wu1 = gather_up1.result
    gather_down1 = _Gather([sh_d1, sh_in])
    gather_out = _Gather([sh_proj, sh_out])
    gather_ffn2_gate = _Gather([sh_g2])
    gather_ffn2_rest = _Gather([sh_u2, sh_d2])

    x0 = x.reshape(t, D_MODEL)
    tgt = loss_target.reshape(t, D_MODEL)
    gam1, gam2, gam3, gam4 = (g.reshape(1, D_MODEL) for g in (ffn1_norm, mix_norm, ffn2_norm, final_norm))

    h1, g1, u1, a1 = _ffn_up(x0, gam1, wg1, wu1, "ffn1_up", hosted=[gather_down1])
    wd1, win_t = gather_down1.result
    x1 = _ffn_down(x0, a1, wd1, "ffn1_down", hosted=[gather_out])
    proj_t, wout = gather_out.result
    h2, qkv_a, qkv_b, gates = _proj_fwd(x1, gam2, win_t, hosted=[gather_ffn2_gate])
    (wg2,) = gather_ffn2_gate.result
    qkv_a3 = qkv_a.reshape(bsz, s_len, QKV_A)
    qkv_b3 = qkv_b.reshape(bsz, s_len, QKV_B)

    bias_b = jnp.asarray(_bias_b())
    sink_rows = jnp.broadcast_to(sinks.reshape(B_Q_HEADS, 1, 1), (B_Q_HEADS, 8, LANES))

    oa = _attn_a_fwd(qkv_a3, bias_a, hosted=[gather_ffn2_rest]).reshape(t, A_WIDTH)
    wu2, wd2 = gather_ffn2_rest.result
    ob = _attn_b_fwd(qkv_b3, bias_b, sink_rows).reshape(t, B_Q_WIDTH)
    x2, ya, yb, mg = _mix_out_fwd(x1, oa, ob, gates, proj_t, wout)
    h3, g2, u2, a2, x3 = _ffn_fwd(x2, gam3, wg2, wu2, wd2, "ffn2_fwd")

    dx2, dg2, du2, db2, dgam3, dgam4, loss_part = _ffn_bwd_head(x3, gam4, tgt, x2, gam3, g2, u2, wg2, wu2, wd2,
                                                                "ffn2_bwd")
    gw_ffn2 = [_mm_tn([dg2], h3, "grad_ffn2_gate"), _mm_tn([du2], h3, "grad_ffn2_up"),
               _mm_tn([a2], db2, "grad_ffn2_down")]
    pairx_ffn2 = _PairExchange(gw_ffn2)
    doa, dob, dgates, gw_out, gw_proj = _mix_out_bwd(dx2, gates, ya, yb, mg, oa, ob, proj_t, wout,
                                                     hosted=[pairx_ffn2])
    psum_ffn2 = _pair_sum(core, gw_ffn2, pairx_ffn2.result, "pair_sum_ffn2")

    chipx_ffn2 = _ChipExchange(psum_ffn2)
    dqa, dka, dva, dbias_a = _attn_a_bwd(qkv_a3, bias_a, doa.reshape(bsz, s_len, A_WIDTH), hosted=[chipx_ffn2])
    pairx_out = _PairExchange([gw_proj, gw_out])
    dqb, dkvb, dsink = _attn_b_bwd(qkv_b3, bias_b, sink_rows, dob.reshape(bsz, s_len, B_Q_WIDTH), hosted=[pairx_out])
    drel_lanes = _relbias_grad(dbias_a)
    dproj = [dqa.reshape(t, A_WIDTH), dka.reshape(t, A_WIDTH), dva.reshape(t, A_WIDTH), dqb.reshape(t, B_Q_WIDTH),
             dkvb.reshape(t, 2 * B_KV_WIDTH), dgates]

    gw_in = _mm_tn(dproj, h2, "grad_w_in")
    pairx_in = _PairExchange([gw_in])
    psum_out = _pair_sum(core, [gw_proj, gw_out], pairx_out.result, "pair_sum_mix")
    chipx_out = _ChipExchange(psum_out)
    dx1, db1, dgam2 = _proj_bwd(dx2, x1, gam2, dproj, win_t, hosted=[pairx_in, chipx_out])
    psum_in = _pair_sum(core, [gw_in], pairx_in.result, "pair_sum_w_in")
    gw_d1 = _mm_tn([a1], db1, "grad_ffn1_down")

    chipx_in = _ChipExchange(psum_in)
    pairx_d1 = _PairExchange([gw_d1])
    dg1, du1 = _ffn_bwd_act(dx1, g1, u1, wd1, "ffn1_bwd_act", hosted=[chipx_in, pairx_d1])
    psum_d1 = _pair_sum(core, [gw_d1], pairx_d1.result, "pair_sum_ffn1_down")
    chipx_d1 = _ChipExchange(psum_d1)
    gw_g1 = _mm_tn([dg1], h1, "grad_ffn1_gate", hosted=[chipx_d1])
    from_sibling_g1 = _exchange_alone(_PairExchange([gw_g1]), "pair_exchange_ffn1_gate")
    psum_g1 = _pair_sum(core, [gw_g1], from_sibling_g1, "pair_sum_ffn1_gate")
    chipx_g1 = _ChipExchange(psum_g1)
    gw_u1 = _mm_tn([du1], h1, "grad_ffn1_up", hosted=[chipx_g1])
    from_sibling_u1 = _exchange_alone(_PairExchange([gw_u1]), "pair_exchange_ffn1_up")
    psum_u1 = _pair_sum(core, [gw_u1], from_sibling_u1, "pair_sum_ffn1_up")
    chipx_u1 = _ChipExchange(psum_u1)
    dx0, dgam1 = _ffn_bwd_in(dx1, x0, gam1, dg1, du1, wg1, wu1, "ffn1_bwd_in", hosted=[chipx_u1])

    (g_proj,) = _final_sum(chip, psum_out[0:1], chipx_out.result[0:1], "grad_sum_proj")
    grads = {"w_proj_a": g_proj[:, 0:A_WIDTH].T, "w_proj_b": g_proj[:, A_WIDTH:].T}

    def row_of(v):
        return jnp.pad(v.reshape(1, -1), ((0, 0), (0, D_MODEL - v.size)))

    def table_rows(v):
        return jnp.pad(v, ((0, 0), (0, D_MODEL - REL_TABLE)))

    drel_local = jnp.flip(drel_lanes[:, 0, 0:REL_TABLE], axis=1)
    small_part = jnp.concatenate(
        [jnp.sum(dgam1, axis=0, keepdims=True), jnp.sum(dgam2, axis=0, keepdims=True),
         jnp.sum(dgam3, axis=0, keepdims=True), jnp.sum(dgam4, axis=0, keepdims=True),
         row_of(jnp.sum(loss_part)), row_of(dsink[:, 0, 0]), jnp.zeros((2, D_MODEL), F32),
         table_rows(drel_local)], axis=0)
    small = _all_reduce_small(small_part)
    loss = small[4, 0]

    def pack(n1, n2, n3, n4, sk, tb):
        return jnp.concatenate([n1.reshape(1, -1), n2.reshape(1, -1), n3.reshape(1, -1), n4.reshape(1, -1),
                                jnp.zeros((1, D_MODEL), F32), row_of(sk), jnp.zeros((2, D_MODEL), F32), table_rows(tb)],
                               axis=0)

    live = np.zeros((SMALL_ROWS, D_MODEL), np.float32)
    live[0:4] = 1.0
    live[5, 0:B_Q_HEADS] = 1.0
    live[8:16, 0:REL_TABLE] = 1.0
    small_g = small * jnp.asarray(live)
    sw = pack(ffn1_norm, mix_norm, ffn2_norm, final_norm, sinks, rel_bias)
    sm = pack(m_ffn1_norm, m_mix_norm, m_ffn2_norm, m_final_norm, m_sinks, m_rel_bias)
    sv = pack(v_ffn1_norm, v_mix_norm, v_ffn2_norm, v_final_norm, v_sinks, v_rel_bias)
    (sd,), (snm,), (snv,) = _adamw([sw], [small_g], [sm], [sv], "adamw_small")

    def unpack(p):
        return {"ffn1_norm": p[0], "mix_norm": p[1], "ffn2_norm": p[2], "final_norm": p[3],
                "sinks": p[5, 0:B_Q_HEADS], "rel_bias": p[8:16, 0:REL_TABLE]}

    grads.update(unpack(small_g))
    delta, new_m, new_v = unpack(sd), unpack(snm), unpack(snv)

    wmv = {
        "ffn1_w_gate": (ffn1_w_gate, m_ffn1_w_gate, v_ffn1_w_gate), "ffn1_w_up": (ffn1_w_up, m_ffn1_w_up, v_ffn1_w_up),
        "ffn1_w_down": (ffn1_w_down, m_ffn1_w_down, v_ffn1_w_down), "w_in": (w_in, m_w_in, v_w_in),
        "w_proj_a": (w_proj_a, m_w_proj_a, v_w_proj_a), "w_proj_b": (w_proj_b, m_w_proj_b, v_w_proj_b),
        "w_out": (w_out, m_w_out, v_w_out),
        "ffn2_w_gate": (ffn2_w_gate, m_ffn2_w_gate, v_ffn2_w_gate), "ffn2_w_up": (ffn2_w_up, m_ffn2_w_up, v_ffn2_w_up),
        "ffn2_w_down": (ffn2_w_down, m_ffn2_w_down, v_ffn2_w_down),
    }
    row_form_names = ("ffn1_w_gate", "ffn1_w_up", "w_in", "ffn2_w_gate", "ffn2_w_up")

    def form(n, a):
        return a.T if n in row_form_names else a

    def reduced_group(gname, names, psums, recvd, steps):
        gs_, ds_, ms_, vs_ = _adamw_reduced(
            chip, [form(n, wmv[n][0]) for n in names], psums, recvd, [form(n, wmv[n][1]) for n in names],
            [form(n, wmv[n][2]) for n in names], steps, gname)
        for n, g_, d_, m_, v_ in zip(names, gs_, ds_, ms_, vs_):
            grads[n], delta[n], new_m[n], new_v[n] = form(n, g_), form(n, d_), form(n, m_), form(n, v_)

    reduced_group("adamw_ffn", ["ffn1_w_gate", "ffn1_w_up", "ffn1_w_down", "ffn2_w_gate", "ffn2_w_up", "ffn2_w_down"],
                  psum_g1 + psum_u1 + psum_d1 + psum_ffn2,
                  chipx_g1.result + chipx_u1.result + chipx_d1.result + chipx_ffn2.result, 11)
    reduced_group("adamw_in_out", ["w_in", "w_out"], psum_in + psum_out[1:2], chipx_in.result + chipx_out.result[1:2], 2)
    names = ["w_proj_a", "w_proj_b"]
    ds_, ms_, vs_ = _adamw([wmv[n][0] for n in names], [grads[n] for n in names], [wmv[n][1] for n in names],
                           [wmv[n][2] for n in names], "adamw_proj")
    for n, d_, m_, v_ in zip(names, ds_, ms_, vs_):
        delta[n], new_m[n], new_v[n] = d_, m_, v_

    order = ["ffn1_norm", "ffn1_w_gate", "ffn1_w_up", "ffn1_w_down", "mix_norm", "w_in", "rel_bias", "sinks",
             "w_proj_a", "w_proj_b", "w_out", "ffn2_norm", "ffn2_w_gate", "ffn2_w_up", "ffn2_w_down", "final_norm"]
    grad_x = dx0.reshape(bsz, s_len, D_MODEL)
    return (loss, grad_x, *[grads[n] for n in order], *[delta[n] for n in order], *[new_m[n] for n in order],
            *[new_v[n] for n in order])
```

```python
import numpy as np
import jax
import jax.numpy as jnp
from jax import lax
from jax.experimental import pallas as pl
from jax.experimental.pallas import tpu as pltpu

F32 = jnp.float32
BF16 = jnp.bfloat16

D_MODEL = 1024
D_FF = 2816
CHUNK = 64
D_HEAD = 64
A_HEADS = 8
A_PREV = 8
MAX_REL = 128
B_Q_HEADS = 8
B_KV_HEADS = 2
B_GROUP = B_Q_HEADS // B_KV_HEADS
B_PREV = 2
REL_TABLE = (CHUNK - 1) + MAX_REL + 1
A_WIDTH = A_HEADS * D_HEAD
B_Q_WIDTH = B_Q_HEADS * D_HEAD
B_KV_WIDTH = B_KV_HEADS * D_HEAD
QKV_A = 3 * A_WIDTH
QKV_B = B_Q_WIDTH + 2 * B_KV_WIDTH
IN_WIDTH = QKV_A + QKV_B + 2 * D_MODEL
EPS = 1e-6
NEG_INF = -1e30
SCALE = 1.0 / 8.0

ADAM_LR = 0.001
ADAM_B1 = 0.9
ADAM_B2 = 0.999
ADAM_EPS = 1e-08
ADAM_WD = 0.01
ADAM_STEP = 10

N_DEV = 8
N_CHIP = 4
MESH = pl.DeviceIdType.MESH

LANES = 128
TQ = 256
TM = 256
FC = 256
VMEM_LIMIT = 56 << 20


def _cparams(sem, vmem=VMEM_LIMIT):
    return pltpu.CompilerParams(dimension_semantics=sem, vmem_limit_bytes=vmem)


def _dot_nt(a, b):
    return lax.dot_general(a, b, (((1,), (1,)), ((), ())), preferred_element_type=F32)


def _dot_nn(a, b):
    return lax.dot_general(a, b, (((1,), (0,)), ((), ())), preferred_element_type=F32)


def _dot_tn(a, b):
    return lax.dot_general(a, b, (((0,), (0,)), ((), ())), preferred_element_type=F32)


def _resident(shape):
    nd = len(shape)
    return pl.BlockSpec(shape, lambda *_: (0,) * nd, pipeline_mode=pl.Buffered(1))


def _rows(tm, width):
    return pl.BlockSpec((tm, width), lambda i: (i, 0))


def _colsum8(v):
    tm, n = v.shape
    return jnp.sum(v.reshape(tm // 8, 8, n), axis=0)


def _rms(x):
    r = lax.rsqrt(jnp.mean(x * x, axis=-1, keepdims=True) + EPS)
    return x * r, r


def _rms_bwd(dh, xh, r, gamma):
    dxh = dh * gamma
    dx = r * (dxh - xh * jnp.mean(dxh * xh, axis=-1, keepdims=True))
    return dx, _colsum8(dh * xh)


def _hbm():
    return pl.BlockSpec(memory_space=pltpu.HBM)


def _call(body, *, name, grid, in_specs, out_specs, out_shape, args, sem, scratch_shapes=(), hosted=()):
    in_specs, out_specs, out_shape = list(in_specs), list(out_specs), list(out_shape)
    scratch_shapes = list(scratch_shapes)
    if not hosted:
        return pl.pallas_call(body, name=name, grid=grid, in_specs=in_specs, out_specs=out_specs, out_shape=out_shape,
                              scratch_shapes=scratch_shapes, compiler_params=_cparams(sem))(*args)
    n_in, n_out, n_scr = len(in_specs), len(out_specs), len(scratch_shapes)
    x_in = [a for x in hosted for a in x.inputs]
    x_out = [s for x in hosted for s in x.out_shape]
    x_scr = [s for x in hosted for s in x.scratch]
    steps = int(np.prod(grid))
    forward_step = max(steps - 3, 0)
    relay_step = min((5 * steps) // 8, forward_step)

    def wrapped(*refs):
        pos = [0]

        def take(k):
            pos[0] += k
            return refs[pos[0] - k:pos[0]]

        ins, xin, outs, xout, scr, xscr = (take(k) for k in (n_in, len(x_in), n_out, len(x_out), n_scr, len(x_scr)))
        step = 0
        for axis, extent in enumerate(grid):
            step = step * extent + pl.program_id(axis)
        own, oi, oo, osc = [], 0, 0, 0
        for x in hosted:
            own.append((xin[oi:oi + len(x.inputs)], xout[oo:oo + len(x.out_shape)], xscr[osc:osc + len(x.scratch)]))
            oi, oo, osc = oi + len(x.inputs), oo + len(x.out_shape), osc + len(x.scratch)

        def phase(method):
            for x, (i_, o_, s_) in zip(hosted, own):
                getattr(x, method)(i_, o_, s_)

        pl.when(step == 0)(lambda: phase("start"))
        body(*ins, *outs, *scr)
        pl.when(step == relay_step)(lambda: phase("relay"))
        pl.when(step == forward_step)(lambda: phase("forward"))
        pl.when(step == steps - 1)(lambda: phase("finish"))

    res = pl.pallas_call(
        wrapped, name=name, grid=grid, in_specs=in_specs + [_hbm()] * len(x_in),
        out_specs=out_specs + [_hbm()] * len(x_out), out_shape=out_shape + x_out,
        scratch_shapes=scratch_shapes + x_scr, compiler_params=_cparams(("arbitrary",) * len(grid)))(*args, *x_in)
    rest = list(res[n_out:])
    for x in hosted:
        x.result, rest = rest[:len(x.out_shape)], rest[len(x.out_shape):]
    return list(res[:n_out])


def _to_bf16(arrays, name):
    n = len(arrays)

    def body(*refs):
        for k in range(n):
            refs[n + k][...] = refs[k][...].astype(BF16)

    specs = [pl.BlockSpec(a.shape, lambda i: (0, 0)) for a in arrays]
    return pl.pallas_call(
        body, name=name, grid=(1,), in_specs=specs, out_specs=specs,
        out_shape=[jax.ShapeDtypeStruct(a.shape, BF16) for a in arrays],
        compiler_params=_cparams(("arbitrary",)))(*arrays)


def _ffn_fwd(x, gamma, wg_t, wu_t, wd, name, hosted=()):
    t = x.shape[0]
    f = wg_t.shape[0]

    def body(x_ref, gam_ref, wg_ref, wu_ref, wd_ref, h_ref, g_ref, u_ref, a_ref, y_ref):
        xv = x_ref[...]
        xh, _ = _rms(xv)
        h = (xh * gam_ref[...]).astype(BF16)
        h_ref[...] = h
        for j in range(f // FC):
            sl = slice(j * FC, (j + 1) * FC)
            g = _dot_nt(h, wg_ref[sl, :])
            u = _dot_nt(h, wu_ref[sl, :])
            g_ref[:, sl] = g.astype(BF16)
            u_ref[:, sl] = u.astype(BF16)
            a_ref[:, sl] = (g * jax.nn.sigmoid(g) * u).astype(BF16)
        y_ref[...] = xv + 0.5 * _dot_nn(a_ref[...], wd_ref[...])

    return _call(
        body,
        name=name,
        grid=(t // TM,),
        in_specs=[_rows(TM, D_MODEL), _resident((1, D_MODEL)), _resident((f, D_MODEL)), _resident((f, D_MODEL)),
                  _resident((f, D_MODEL))],
        out_specs=[_rows(TM, D_MODEL), _rows(TM, f), _rows(TM, f), _rows(TM, f), _rows(TM, D_MODEL)],
        out_shape=[jax.ShapeDtypeStruct((t, D_MODEL), BF16), jax.ShapeDtypeStruct((t, f), BF16),
                   jax.ShapeDtypeStruct((t, f), BF16), jax.ShapeDtypeStruct((t, f), BF16),
                   jax.ShapeDtypeStruct((t, D_MODEL), F32)],
        args=(x, gamma, wg_t, wu_t, wd), sem=("parallel",), hosted=hosted)


def _ffn_up(x, gamma, wg_t, wu_t, name, hosted=()):
    t = x.shape[0]
    f = wg_t.shape[0]

    def body(x_ref, gam_ref, wg_ref, wu_ref, h_ref, g_ref, u_ref, a_ref):
        xh, _ = _rms(x_ref[...])
        h = (xh * gam_ref[...]).astype(BF16)
        h_ref[...] = h
        for j in range(f // FC):
            sl = slice(j * FC, (j + 1) * FC)
            g = _dot_nt(h, wg_ref[sl, :])
            u = _dot_nt(h, wu_ref[sl, :])
            g_ref[:, sl] = g.astype(BF16)
            u_ref[:, sl] = u.astype(BF16)
            a_ref[:, sl] = (g * jax.nn.sigmoid(g) * u).astype(BF16)

    return _call(
        body,
        name=name,
        grid=(t // TM,),
        in_specs=[_rows(TM, D_MODEL), _resident((1, D_MODEL)), _resident((f, D_MODEL)), _resident((f, D_MODEL))],
        out_specs=[_rows(TM, D_MODEL), _rows(TM, f), _rows(TM, f), _rows(TM, f)],
        out_shape=[jax.ShapeDtypeStruct((t, D_MODEL), BF16), jax.ShapeDtypeStruct((t, f), BF16),
                   jax.ShapeDtypeStruct((t, f), BF16), jax.ShapeDtypeStruct((t, f), BF16)],
        args=(x, gamma, wg_t, wu_t), sem=("parallel",), hosted=hosted)


def _ffn_down(x, a_act, wd, name, hosted=()):
    t = x.shape[0]
    f = wd.shape[0]

    def body(x_ref, a_ref, wd_ref, y_ref):
        y_ref[...] = x_ref[...] + 0.5 * _dot_nn(a_ref[...], wd_ref[...])

    return _call(
        body,
        name=name,
        grid=(t // TM,),
        in_specs=[_rows(TM, D_MODEL), _rows(TM, f), _resident((f, D_MODEL))],
        out_specs=[_rows(TM, D_MODEL)],
        out_shape=[jax.ShapeDtypeStruct((t, D_MODEL), F32)],
        args=(x, a_act, wd), sem=("parallel",), hosted=hosted)[0]


def _ffn_bwd_head(y, gamma_f, target, x, gamma, g_act, u_act, wg_t, wu_t, wd, name):
    t = x.shape[0]
    f = wg_t.shape[0]

    def body(y_ref, gamf_ref, t_ref, x_ref, gam_ref, g_ref, u_ref, wg_ref, wu_ref, wd_ref, dx_ref, dg_ref, du_ref,
             db_ref, dgam_ref, dgamf_ref, loss_ref):
        yh, ry = _rms(y_ref[...])
        gam_f = gamf_ref[...]
        e = yh * gam_f - t_ref[...]
        dv, dgam_f = _rms_bwd(e * (1.0 / D_MODEL), yh, ry, gam_f)
        db = (0.5 * dv).astype(BF16)
        db_ref[...] = db
        for j in range(f // FC):
            sl = slice(j * FC, (j + 1) * FC)
            da = _dot_nt(db, wd_ref[sl, :])
            g = g_ref[:, sl].astype(F32)
            u = u_ref[:, sl].astype(F32)
            s = jax.nn.sigmoid(g)
            dg_ref[:, sl] = (da * u * (s * (1.0 + g * (1.0 - s)))).astype(BF16)
            du_ref[:, sl] = (da * (g * s)).astype(BF16)
        dh = _dot_nn(dg_ref[...], wg_ref[...]) + _dot_nn(du_ref[...], wu_ref[...])
        xh, r = _rms(x_ref[...])
        dxn, dgam = _rms_bwd(dh, xh, r, gam_ref[...])
        dx_ref[...] = dv + dxn

        @pl.when(pl.program_id(0) == 0)
        def _():
            dgam_ref[...] = jnp.zeros_like(dgam_ref)
            dgamf_ref[...] = jnp.zeros_like(dgamf_ref)
            loss_ref[...] = jnp.zeros_like(loss_ref)

        dgam_ref[...] += dgam
        dgamf_ref[...] += dgam_f
        loss_ref[...] += _colsum8(e * e) * (0.5 / D_MODEL)

    acc = pl.BlockSpec((8, D_MODEL), lambda i: (0, 0))
    return _call(
        body,
        name=name,
        grid=(t // TM,),
        in_specs=[_rows(TM, D_MODEL), _resident((1, D_MODEL)), _rows(TM, D_MODEL), _rows(TM, D_MODEL),
                  _resident((1, D_MODEL)), _rows(TM, f), _rows(TM, f),
                  _resident((f, D_MODEL)), _resident((f, D_MODEL)), _resident((f, D_MODEL))],
        out_specs=[_rows(TM, D_MODEL), _rows(TM, f), _rows(TM, f), _rows(TM, D_MODEL), acc, acc, acc],
        out_shape=[jax.ShapeDtypeStruct((t, D_MODEL), F32), jax.ShapeDtypeStruct((t, f), BF16),
                   jax.ShapeDtypeStruct((t, f), BF16), jax.ShapeDtypeStruct((t, D_MODEL), BF16),
                   jax.ShapeDtypeStruct((8, D_MODEL), F32), jax.ShapeDtypeStruct((8, D_MODEL), F32),
                   jax.ShapeDtypeStruct((8, D_MODEL), F32)],
        args=(y, gamma_f, target, x, gamma, g_act, u_act, wg_t, wu_t, wd), sem=("arbitrary",))


def _ffn_bwd_act(d, g_act, u_act, wd, name, hosted=()):
    t = d.shape[0]
    f = wd.shape[0]

    def body(d_ref, g_ref, u_ref, wd_ref, dg_ref, du_ref):
        db = (0.5 * d_ref[...]).astype(BF16)
        for j in range(f // FC):
            sl = slice(j * FC, (j + 1) * FC)
            da = _dot_nt(db, wd_ref[sl, :])
            g = g_ref[:, sl].astype(F32)
            u = u_ref[:, sl].astype(F32)
            s = jax.nn.sigmoid(g)
            dg_ref[:, sl] = (da * u * (s * (1.0 + g * (1.0 - s)))).astype(BF16)
            du_ref[:, sl] = (da * (g * s)).astype(BF16)

    return _call(
        body,
        name=name,
        grid=(t // TM,),
        in_specs=[_rows(TM, D_MODEL), _rows(TM, f), _rows(TM, f), _resident((f, D_MODEL))],
        out_specs=[_rows(TM, f), _rows(TM, f)],
        out_shape=[jax.ShapeDtypeStruct((t, f), BF16), jax.ShapeDtypeStruct((t, f), BF16)],
        args=(d, g_act, u_act, wd), sem=("parallel",), hosted=hosted)


def _ffn_bwd_in(d, x, gamma, dg, du, wg_t, wu_t, name, hosted=()):
    t = x.shape[0]
    f = wg_t.shape[0]

    def body(d_ref, x_ref, gam_ref, dg_ref, du_ref, wg_ref, wu_ref, dx_ref, dgam_ref):
        dh = _dot_nn(dg_ref[...], wg_ref[...]) + _dot_nn(du_ref[...], wu_ref[...])
        xh, r = _rms(x_ref[...])
        dxn, dgam = _rms_bwd(dh, xh, r, gam_ref[...])
        dx_ref[...] = d_ref[...] + dxn

        @pl.when(pl.program_id(0) == 0)
        def _():
            dgam_ref[...] = jnp.zeros_like(dgam_ref)

        dgam_ref[...] += dgam

    return _call(
        body,
        name=name,
        grid=(t // TM,),
        in_specs=[_rows(TM, D_MODEL), _rows(TM, D_MODEL), _resident((1, D_MODEL)), _rows(TM, f), _rows(TM, f),
                  _resident((f, D_MODEL)), _resident((f, D_MODEL))],
        out_specs=[_rows(TM, D_MODEL), pl.BlockSpec((8, D_MODEL), lambda i: (0, 0))],
        out_shape=[jax.ShapeDtypeStruct((t, D_MODEL), F32), jax.ShapeDtypeStruct((8, D_MODEL), F32)],
        args=(d, x, gamma, dg, du, wg_t, wu_t), sem=("arbitrary",), hosted=hosted)


def _mm_tn(pieces, b, name, tile=256, hosted=()):
    t, n = b.shape
    npc = len(pieces)
    counts = [p.shape[1] // tile for p in pieces]
    los = [sum(counts[:k]) for k in range(npc)]
    total = sum(counts)

    def body(*refs):
        a_refs, b_ref, o_ref = refs[:npc], refs[npc], refs[npc + 1]
        i = pl.program_id(0)
        for k in range(npc):
            @pl.when(jnp.logical_and(i >= los[k], i < los[k] + counts[k]))
            def _(k=k):
                o_ref[...] = _dot_tn(a_refs[k][...], b_ref[...]).astype(BF16)

    def a_spec(k):
        return pl.BlockSpec((t, tile), lambda i: (0, jnp.clip(i - los[k], 0, counts[k] - 1)))

    return _call(
        body,
        name=name,
        grid=(total,),
        in_specs=[a_spec(k) for k in range(npc)] + [_resident((t, n))],
        out_specs=[pl.BlockSpec((tile, n), lambda i: (i, 0))],
        out_shape=[jax.ShapeDtypeStruct((total * tile, n), BF16)],
        args=(*pieces, b), sem=("parallel",), hosted=hosted)[0]


def _proj_fwd(x, gamma, win_t, hosted=()):
    t = x.shape[0]

    def body(x_ref, gam_ref, w_ref, h_ref, qa_ref, qb_ref, gt_ref):
        xh, _ = _rms(x_ref[...])
        h = (xh * gam_ref[...]).astype(BF16)
        h_ref[...] = h
        for j in range(QKV_A // FC):
            qa_ref[:, j * FC:(j + 1) * FC] = _dot_nt(h, w_ref[j * FC:(j + 1) * FC, :]).astype(BF16)
        for j in range(QKV_B // FC):
            lo = QKV_A + j * FC
            qb_ref[:, j * FC:(j + 1) * FC] = _dot_nt(h, w_ref[lo:lo + FC, :]).astype(BF16)
        for j in range(2 * D_MODEL // FC):
            lo = QKV_A + QKV_B + j * FC
            gt_ref[:, j * FC:(j + 1) * FC] = _dot_nt(h, w_ref[lo:lo + FC, :])

    return _call(
        body,
        name="proj_fwd",
        grid=(t // TM,),
        in_specs=[_rows(TM, D_MODEL), _resident((1, D_MODEL)), _resident((IN_WIDTH, D_MODEL))],
        out_specs=[_rows(TM, D_MODEL), _rows(TM, QKV_A), _rows(TM, QKV_B), _rows(TM, 2 * D_MODEL)],
        out_shape=[jax.ShapeDtypeStruct((t, D_MODEL), BF16), jax.ShapeDtypeStruct((t, QKV_A), BF16),
                   jax.ShapeDtypeStruct((t, QKV_B), BF16), jax.ShapeDtypeStruct((t, 2 * D_MODEL), F32)],
        args=(x, gamma, win_t), sem=("parallel",), hosted=hosted)


def _proj_bwd(d, x, gamma, pieces, win_t, hosted=()):
    t = x.shape[0]
    npc = len(pieces)
    widths = [p.shape[1] for p in pieces]
    los = [sum(widths[:k]) for k in range(npc)]

    def body(*refs):
        d_ref, x_ref, gam_ref = refs[:3]
        p_refs = refs[3:3 + npc]
        w_ref, dx_ref, db_ref, dgam_ref = refs[3 + npc:]
        dh = _dot_nn(p_refs[0][...], w_ref[0:widths[0], :])
        for k in range(1, npc):
            dh += _dot_nn(p_refs[k][...], w_ref[los[k]:los[k] + widths[k], :])
        xh, r = _rms(x_ref[...])
        dxn, dgam = _rms_bwd(dh, xh, r, gam_ref[...])
        dx = d_ref[...] + dxn
        dx_ref[...] = dx
        db_ref[...] = (0.5 * dx).astype(BF16)

        @pl.when(pl.program_id(0) == 0)
        def _():
            dgam_ref[...] = jnp.zeros_like(dgam_ref)

        dgam_ref[...] += dgam

    return _call(
        body,
        name="proj_bwd",
        grid=(t // TM,),
        in_specs=[_rows(TM, D_MODEL), _rows(TM, D_MODEL), _resident((1, D_MODEL))] + [_rows(TM, w) for w in widths]
        + [_resident((IN_WIDTH, D_MODEL))],
        out_specs=[_rows(TM, D_MODEL), _rows(TM, D_MODEL), pl.BlockSpec((8, D_MODEL), lambda i: (0, 0))],
        out_shape=[jax.ShapeDtypeStruct((t, D_MODEL), F32), jax.ShapeDtypeStruct((t, D_MODEL), BF16),
                   jax.ShapeDtypeStruct((8, D_MODEL), F32)],
        args=(d, x, gamma, *pieces, win_t), sem=("arbitrary",), hosted=hosted)


def _lane_half(shape):
    return lax.broadcasted_iota(jnp.int32, shape, len(shape) - 1) // D_HEAD


def _band_weights(q, kk, bias, sink, qs, pad):
    s = _dot_nt(q, kk) + bias
    if qs is not None:
        col = lax.broadcasted_iota(jnp.int32, s.shape, 1)
        s = jnp.where(col + qs >= pad, s, NEG_INF)
    m = jnp.max(s, axis=-1, keepdims=True)
    if sink is not None:
        m = jnp.maximum(m, sink)
    return jnp.exp(s - m), m


def _weighted_values(p, vv_ones, sink, m):
    r = _dot_nn(p.astype(BF16), vv_ones)
    den = r[:, LANES:2 * LANES]
    if sink is not None:
        den = den + jnp.exp(sink - m)
    return r[:, 0:LANES] / den


def _band_softmax(q, kk, bias, sink, qs, pad):
    p, m = _band_weights(q, kk, bias, sink, qs, pad)
    den = jnp.sum(p, axis=-1, keepdims=True)
    if sink is not None:
        den = den + jnp.exp(sink - m)
    return p, m, 1.0 / den


def _fill_padded(dst, src, pad):
    dst[0:pad, :] = jnp.zeros((pad,) + dst.shape[1:], dst.dtype)
    dst[pad:, :] = src


FWD_PAIRS = 4
BWD_PAIRS = 4


def _attn_a_fwd(qkv, bias, hosted=()):
    bsz, s_len, _ = qkv.shape
    pad = A_PREV * CHUNK
    band = TQ + pad
    pp = FWD_PAIRS
    w = pp * LANES
    nb = A_WIDTH // w

    def body(q_ref, k_ref, v_ref, b_ref, o_ref, kp, vp):
        i = pl.program_id(2)

        @pl.when(i == 0)
        def _():
            _fill_padded(kp, k_ref[...], pad)
            _fill_padded(vp, v_ref[...], pad)

        qs = pl.multiple_of(i * TQ, TQ)
        half = _lane_half((1, LANES))

        ones = jnp.ones((band, LANES), BF16)

        def block(masked):
            for pr in range(pp):
                sl = slice(pr * LANES, (pr + 1) * LANES)
                kk = kp[pl.ds(qs, band), sl]
                vv = jnp.concatenate([vp[pl.ds(qs, band), sl], ones], axis=1)
                q = q_ref[:, sl] * SCALE
                outs = []
                for j in range(2):
                    qm = jnp.where(half == j, q, jnp.zeros_like(q))
                    p, m = _band_weights(qm, kk, b_ref[2 * pr + j], None, qs if masked else None, pad)
                    outs.append(_weighted_values(p, vv, None, m))
                o_ref[:, sl] = jnp.where(half == 0, outs[0], outs[1]).astype(BF16)

        pl.when(i < pad // TQ)(lambda: block(True))
        pl.when(i >= pad // TQ)(lambda: block(False))

    return _call(
        body,
        name="attn_a_fwd",
        grid=(bsz, nb, s_len // TQ),
        in_specs=[pl.BlockSpec((None, TQ, w), lambda b, g, i: (b, i, g)),
                  pl.BlockSpec((None, s_len, w), lambda b, g, i: (b, 0, nb + g)),
                  pl.BlockSpec((None, s_len, w), lambda b, g, i: (b, 0, 2 * nb + g)),
                  pl.BlockSpec((2 * pp, TQ, band), lambda b, g, i: (g, 0, 0))],
        out_specs=[pl.BlockSpec((None, TQ, w), lambda b, g, i: (b, i, g))],
        out_shape=[jax.ShapeDtypeStruct((bsz, s_len, A_WIDTH), BF16)],
        scratch_shapes=[pltpu.VMEM((pad + s_len, w), BF16), pltpu.VMEM((pad + s_len, w), BF16)],
        args=(qkv, qkv, qkv, bias), sem=("arbitrary", "arbitrary", "arbitrary"), hosted=hosted)[0]


def _attn_a_bwd(qkv, bias, do, hosted=()):
    bsz, s_len, _ = qkv.shape
    pad = A_PREV * CHUNK
    band = TQ + pad
    n_i = s_len // TQ
    pp = BWD_PAIRS
    w = pp * LANES
    nb = A_WIDTH // w

    def body(q_ref, k_ref, v_ref, b_ref, do_ref, dq_ref, dk_ref, dv_ref, dbias_ref, kp, vp, dk_acc, dv_acc):
        b = pl.program_id(1)
        i = pl.program_id(2)

        @pl.when(i == 0)
        def _():
            _fill_padded(kp, k_ref[...], pad)
            _fill_padded(vp, v_ref[...], pad)
            dk_acc[...] = jnp.zeros_like(dk_acc)
            dv_acc[...] = jnp.zeros_like(dv_acc)

        @pl.when(jnp.logical_and(b == 0, i == 0))
        def _():
            dbias_ref[...] = jnp.zeros_like(dbias_ref)

        qs = pl.multiple_of(i * TQ, TQ)
        half = _lane_half((1, LANES))

        def block(masked):
            for pr in range(pp):
                sl = slice(pr * LANES, (pr + 1) * LANES)
                kk = kp[pl.ds(qs, band), sl]
                vv = vp[pl.ds(qs, band), sl]
                q = q_ref[:, sl] * SCALE
                dd = do_ref[:, sl]
                dqs, dks, dvs = [], [], []
                for j in range(2):
                    qm = jnp.where(half == j, q, jnp.zeros_like(q))
                    dm = jnp.where(half == j, dd, jnp.zeros_like(dd))
                    p, _, inv = _band_softmax(qm, kk, b_ref[2 * pr + j], None, qs if masked else None, pad)
                    pn = p * inv
                    dp = _dot_nt(dm, vv)
                    delta = jnp.sum(pn * dp, axis=-1, keepdims=True)
                    ds = pn * (dp - delta)
                    dbias_ref[2 * pr + j] += ds[:, band - REL_COLS:]
                    dsb = ds.astype(BF16)
                    dqs.append(_dot_nn(dsb, kk))
                    dks.append(_dot_tn(dsb, q))
                    dvs.append(_dot_tn(pn.astype(BF16), dd))
                dq_ref[:, sl] = (jnp.where(half == 0, dqs[0], dqs[1]) * SCALE).astype(BF16)
                dk_acc[pl.ds(qs, band), sl] += jnp.where(half == 0, dks[0], dks[1])
                dv_acc[pl.ds(qs, band), sl] += jnp.where(half == 0, dvs[0], dvs[1])

        pl.when(i < pad // TQ)(lambda: block(True))
        pl.when(i >= pad // TQ)(lambda: block(False))

        @pl.when(i == n_i - 1)
        def _():
            dk_ref[...] = dk_acc[pad:, :].astype(BF16)
            dv_ref[...] = dv_acc[pad:, :].astype(BF16)

    qspec = pl.BlockSpec((None, TQ, w), lambda g, b, i: (b, i, g))
    kvout = pl.BlockSpec((None, s_len, w), lambda g, b, i: (b, 0, g))
    wide = jax.ShapeDtypeStruct((bsz, s_len, A_WIDTH), BF16)
    return _call(
        body,
        name="attn_a_bwd",
        grid=(nb, bsz, n_i),
        in_specs=[qspec,
                  pl.BlockSpec((None, s_len, w), lambda g, b, i: (b, 0, nb + g)),
                  pl.BlockSpec((None, s_len, w), lambda g, b, i: (b, 0, 2 * nb + g)),
                  pl.BlockSpec((2 * pp, TQ, band), lambda g, b, i: (g, 0, 0)),
                  qspec],
        out_specs=[qspec, kvout, kvout, pl.BlockSpec((2 * pp, TQ, REL_COLS), lambda g, b, i: (g, 0, 0))],
        out_shape=[wide, wide, wide, jax.ShapeDtypeStruct((A_HEADS, TQ, REL_COLS), F32)],
        scratch_shapes=[pltpu.VMEM((pad + s_len, w), BF16), pltpu.VMEM((pad + s_len, w), BF16),
                        pltpu.VMEM((pad + s_len, w), F32), pltpu.VMEM((pad + s_len, w), F32)],
        args=(qkv, qkv, qkv, bias, do), sem=("arbitrary", "arbitrary", "arbitrary"), hosted=hosted)


def _fill_padded_dup(dst, src, pad, h, half):
    other = pltpu.roll(src, D_HEAD, 1)
    _fill_padded(dst, jnp.where(half == h, src, other), pad)


def _attn_b_fwd(qkv, bias, sink):
    bsz, s_len, _ = qkv.shape
    pad = B_PREV * CHUNK
    band = TQ + pad
    kcol = B_Q_WIDTH // LANES
    npair = B_Q_HEADS // 2

    def body(q_ref, k_ref, v_ref, b_ref, s_ref, o_ref, kp, vp):
        i = pl.program_id(1)
        half = _lane_half((1, LANES))

        @pl.when(i == 0)
        def _():
            for h in range(B_KV_HEADS):
                _fill_padded_dup(kp.at[h], k_ref[...], pad, h, half)
                _fill_padded_dup(vp.at[h], v_ref[...], pad, h, half)

        qs = pl.multiple_of(i * TQ, TQ)

        ones = jnp.ones((band, LANES), BF16)

        def block(masked):
            for pr in range(npair):
                h = pr // (B_GROUP // 2)
                sl = slice(pr * LANES, (pr + 1) * LANES)
                kk = kp[h, pl.ds(qs, band), :]
                vv = jnp.concatenate([vp[h, pl.ds(qs, band), :], ones], axis=1)
                q = q_ref[:, sl] * SCALE
                outs = []
                for j in range(2):
                    qm = jnp.where(half == j, q, jnp.zeros_like(q))
                    sink = s_ref[2 * pr + j][0:1, 0:1]
                    p, m = _band_weights(qm, kk, b_ref[2 * pr + j], sink, qs if masked else None, pad)
                    outs.append(_weighted_values(p, vv, sink, m))
                o_ref[:, sl] = jnp.where(half == 0, outs[0], outs[1]).astype(BF16)

        pl.when(i < -(-pad // TQ))(lambda: block(True))
        pl.when(i >= -(-pad // TQ))(lambda: block(False))

    return pl.pallas_call(
        body,
        name="attn_b_fwd",
        grid=(bsz, s_len // TQ),
        in_specs=[pl.BlockSpec((None, TQ, B_Q_WIDTH), lambda b, i: (b, i, 0)),
                  pl.BlockSpec((None, s_len, LANES), lambda b, i: (b, 0, kcol)),
                  pl.BlockSpec((None, s_len, LANES), lambda b, i: (b, 0, kcol + 1)),
                  pl.BlockSpec((B_Q_HEADS, TQ, band), lambda b, i: (0, 0, 0)),
                  pl.BlockSpec((B_Q_HEADS, 8, LANES), lambda b, i: (0, 0, 0))],
        out_specs=pl.BlockSpec((None, TQ, B_Q_WIDTH), lambda b, i: (b, i, 0)),
        out_shape=jax.ShapeDtypeStruct((bsz, s_len, B_Q_WIDTH), BF16),
        scratch_shapes=[pltpu.VMEM((B_KV_HEADS, pad + s_len, LANES), BF16),
                        pltpu.VMEM((B_KV_HEADS, pad + s_len, LANES), BF16)],
        compiler_params=_cparams(("arbitrary", "arbitrary")),
    )(qkv, qkv, qkv, bias, sink)


def _attn_b_bwd(qkv, bias, sink, do, hosted=()):
    bsz, s_len, _ = qkv.shape
    pad = B_PREV * CHUNK
    band = TQ + pad
    kcol = B_Q_WIDTH // LANES
    n_i = s_len // TQ
    pp = B_GROUP // 2

    def body(q_ref, k_ref, v_ref, b_ref, s_ref, do_ref, dq_ref, dkv_ref, dsink_ref, kp, vp, dk_acc, dv_acc):
        b = pl.program_id(0)
        i = pl.program_id(1)
        half = _lane_half((1, LANES))

        @pl.when(i == 0)
        def _():
            for h in range(B_KV_HEADS):
                _fill_padded_dup(kp.at[h], k_ref[...], pad, h, half)
                _fill_padded_dup(vp.at[h], v_ref[...], pad, h, half)
            dk_acc[...] = jnp.zeros_like(dk_acc)
            dv_acc[...] = jnp.zeros_like(dv_acc)

        @pl.when(jnp.logical_and(b == 0, i == 0))
        def _():
            dsink_ref[...] = jnp.zeros_like(dsink_ref)

        qs = pl.multiple_of(i * TQ, TQ)

        def block(masked):
            heads_dk, heads_dv = [], []
            for h in range(B_KV_HEADS):
                kk = kp[h, pl.ds(qs, band), :]
                vv = vp[h, pl.ds(qs, band), :]
                dk2 = jnp.zeros((band, LANES), F32)
                dv2 = jnp.zeros((band, LANES), F32)
                for pr in range(pp * h, pp * (h + 1)):
                    sl = slice(pr * LANES, (pr + 1) * LANES)
                    q = q_ref[:, sl] * SCALE
                    dd = do_ref[:, sl]
                    dqs, dks, dvs = [], [], []
                    for j in range(2):
                        qm = jnp.where(half == j, q, jnp.zeros_like(q))
                        dm = jnp.where(half == j, dd, jnp.zeros_like(dd))
                        sink = s_ref[2 * pr + j][0:1, 0:1]
                        p, m, inv = _band_softmax(qm, kk, b_ref[2 * pr + j], sink, qs if masked else None, pad)
                        pn = p * inv
                        dp = _dot_nt(dm, vv)
                        delta = jnp.sum(pn * dp, axis=-1, keepdims=True)
                        ds = pn * (dp - delta)
                        dsb = ds.astype(BF16)
                        dqs.append(_dot_nn(dsb, kk))
                        dks.append(_dot_tn(dsb, q))
                        dvs.append(_dot_tn(pn.astype(BF16), dd))
                        dsk = jnp.sum(-(jnp.exp(sink - m) * inv) * delta, axis=0, keepdims=True)
                        dsink_ref[2 * pr + j] += jnp.broadcast_to(dsk, (8, LANES))
                    dq_ref[:, sl] = (jnp.where(half == 0, dqs[0], dqs[1]) * SCALE).astype(BF16)
                    dk2 = dk2 + jnp.where(half == 0, dks[0], dks[1])
                    dv2 = dv2 + jnp.where(half == 0, dvs[0], dvs[1])
                heads_dk.append(dk2 + pltpu.roll(dk2, D_HEAD, 1))
                heads_dv.append(dv2 + pltpu.roll(dv2, D_HEAD, 1))
            dk_acc[pl.ds(qs, band), :] += jnp.where(half == 0, heads_dk[0], heads_dk[1])
            dv_acc[pl.ds(qs, band), :] += jnp.where(half == 0, heads_dv[0], heads_dv[1])

        pl.when(i < -(-pad // TQ))(lambda: block(True))
        pl.when(i >= -(-pad // TQ))(lambda: block(False))

        @pl.when(i == n_i - 1)
        def _():
            dkv_ref[:, 0:LANES] = dk_acc[pad:, :].astype(BF16)
            dkv_ref[:, LANES:2 * LANES] = dv_acc[pad:, :].astype(BF16)

    qspec = pl.BlockSpec((None, TQ, B_Q_WIDTH), lambda b, i: (b, i, 0))
    return _call(
        body,
        name="attn_b_bwd",
        grid=(bsz, n_i),
        in_specs=[qspec,
                  pl.BlockSpec((None, s_len, LANES), lambda b, i: (b, 0, kcol)),
                  pl.BlockSpec((None, s_len, LANES), lambda b, i: (b, 0, kcol + 1)),
                  pl.BlockSpec((B_Q_HEADS, TQ, band), lambda b, i: (0, 0, 0)),
                  pl.BlockSpec((B_Q_HEADS, 8, LANES), lambda b, i: (0, 0, 0)),
                  qspec],
        out_specs=[qspec, pl.BlockSpec((None, s_len, 2 * LANES), lambda b, i: (b, 0, 0)),
                   pl.BlockSpec((B_Q_HEADS, 8, LANES), lambda b, i: (0, 0, 0))],
        out_shape=[jax.ShapeDtypeStruct((bsz, s_len, B_Q_WIDTH), BF16),
                   jax.ShapeDtypeStruct((bsz, s_len, 2 * B_KV_WIDTH), BF16),
                   jax.ShapeDtypeStruct((B_Q_HEADS, 8, LANES), F32)],
        scratch_shapes=[pltpu.VMEM((B_KV_HEADS, pad + s_len, LANES), BF16),
                        pltpu.VMEM((B_KV_HEADS, pad + s_len, LANES), BF16),
                        pltpu.VMEM((pad + s_len, LANES), F32), pltpu.VMEM((pad + s_len, LANES), F32)],
        args=(qkv, qkv, qkv, bias, sink, do), sem=("arbitrary", "arbitrary"), hosted=hosted)


REL_COLS = 3 * 128
REL_WRAP = 512


def _bias_a_build(tv, hosted=()):
    h = tv.shape[0]
    pad = A_PREV * CHUNK
    band = TQ + pad

    def body(tv_ref, o_ref):
        row = tv_ref[...]
        x = jnp.broadcast_to(row, (TQ, REL_WRAP))
        r = lax.broadcasted_iota(jnp.int32, x.shape, 0)
        for bit in range(8):
            sh = 1 << bit
            x = jnp.where((r & sh) != 0, pltpu.roll(x, sh, 1), x)
        far = jnp.broadcast_to(row[:, 0:1], (TQ, band - REL_COLS))
        full = jnp.concatenate([far, x[:, REL_WRAP // 2:REL_WRAP], x[:, 0:REL_COLS - REL_WRAP // 2]], axis=1)
        qc = (lax.broadcasted_iota(jnp.int32, full.shape, 0) + pad) // CHUNK
        kc = lax.broadcasted_iota(jnp.int32, full.shape, 1) // CHUNK
        ok = jnp.logical_and(kc <= qc, kc >= qc - A_PREV)
        o_ref[...] = jnp.where(ok, full, NEG_INF)

    return _call(
        body,
        name="bias_a_build",
        grid=(h,),
        in_specs=[pl.BlockSpec((None, 1, REL_WRAP), lambda hh: (hh, 0, 0))],
        out_specs=[pl.BlockSpec((None, TQ, band), lambda hh: (hh, 0, 0))],
        out_shape=[jax.ShapeDtypeStruct((h, TQ, band), F32)],
        args=(tv,), sem=("parallel",), hosted=hosted)[0]


def _relbias_grad(dbias, hosted=()):
    h, rows, _ = dbias.shape

    def body(d_ref, o_ref):
        x = d_ref[...]
        r = lax.broadcasted_iota(jnp.int32, x.shape, 0)
        c = lax.broadcasted_iota(jnp.int32, x.shape, 1) - r
        x = jnp.where(jnp.logical_and(c >= 1, c < REL_TABLE), x, 0.0)
        for bit in range(8):
            sh = 1 << bit
            x = jnp.where((r & sh) != 0, pltpu.roll(x, REL_COLS - sh, 1), x)
        diag = jnp.sum(x, axis=0, keepdims=True)
        lane = lax.broadcasted_iota(jnp.int32, diag.shape, 1)
        diag = jnp.where(jnp.logical_and(lane >= 1, lane < REL_TABLE), diag, 0.0)
        rest = -jnp.sum(diag, axis=1, keepdims=True)
        o_ref[...] = jnp.broadcast_to(jnp.where(lane == 0, rest, diag), o_ref.shape)

    return _call(
        body,
        name="relbias_grad",
        grid=(h,),
        in_specs=[pl.BlockSpec((None, rows, REL_COLS), lambda hh: (hh, 0, 0))],
        out_specs=[pl.BlockSpec((None, 8, REL_COLS), lambda hh: (hh, 0, 0))],
        out_shape=[jax.ShapeDtypeStruct((h, 8, REL_COLS), F32)],
        args=(dbias,), sem=("parallel",), hosted=hosted)[0]


def _mix_out_fwd(x, oa, ob, gates, proj_t, wout):
    t = x.shape[0]

    def body(x_ref, oa_ref, ob_ref, gt_ref, pt_ref, wo_ref, y_ref, ya_ref, yb_ref, mg_ref):
        ya = _dot_nt(oa_ref[...], pt_ref[:, 0:A_WIDTH])
        yb = _dot_nt(ob_ref[...], pt_ref[:, A_WIDTH:A_WIDTH + B_Q_WIDTH])
        ya_ref[...] = ya.astype(BF16)
        yb_ref[...] = yb.astype(BF16)
        mg = jax.nn.sigmoid(gt_ref[:, 0:D_MODEL]) * ya + jax.nn.sigmoid(gt_ref[:, D_MODEL:2 * D_MODEL]) * yb
        mgb = mg.astype(BF16)
        mg_ref[...] = mgb
        y_ref[...] = x_ref[...] + _dot_nn(mgb, wo_ref[...])

    return pl.pallas_call(
        body,
        name="mix_out_fwd",
        grid=(t // TM,),
        in_specs=[_rows(TM, D_MODEL), _rows(TM, A_WIDTH), _rows(TM, B_Q_WIDTH), _rows(TM, 2 * D_MODEL),
                  _resident((D_MODEL, A_WIDTH + B_Q_WIDTH)), _resident((D_MODEL, D_MODEL))],
        out_specs=[_rows(TM, D_MODEL), _rows(TM, D_MODEL), _rows(TM, D_MODEL), _rows(TM, D_MODEL)],
        out_shape=[jax.ShapeDtypeStruct((t, D_MODEL), F32), jax.ShapeDtypeStruct((t, D_MODEL), BF16),
                   jax.ShapeDtypeStruct((t, D_MODEL), BF16), jax.ShapeDtypeStruct((t, D_MODEL), BF16)],
        compiler_params=_cparams(("parallel",)),
    )(x, oa, ob, gates, proj_t, wout)


def _mix_out_bwd(d, gates, ya, yb, mg, oa, ob, proj_t, wout, hosted=()):
    t = d.shape[0]
    nt = t // TM

    def body(d_ref, gt_ref, ya_ref, yb_ref, mg_ref, oa_ref, ob_ref, pt_ref, wo_ref,
             doa_ref, dob_ref, dgt_ref, gwo_ref, gwp_ref, acc_o, acc_p):
        i = pl.program_id(0)
        db = d_ref[...].astype(BF16)
        dmg = _dot_nt(db, wo_ref[...])
        sa = jax.nn.sigmoid(gt_ref[:, 0:D_MODEL])
        sb = jax.nn.sigmoid(gt_ref[:, D_MODEL:2 * D_MODEL])
        dya = (dmg * sa).astype(BF16)
        dyb = (dmg * sb).astype(BF16)
        dgt_ref[:, 0:D_MODEL] = (dmg * ya_ref[...].astype(F32) * (sa * (1.0 - sa))).astype(BF16)
        dgt_ref[:, D_MODEL:2 * D_MODEL] = (dmg * yb_ref[...].astype(F32) * (sb * (1.0 - sb))).astype(BF16)
        doa_ref[...] = _dot_nn(dya, pt_ref[:, 0:A_WIDTH]).astype(BF16)
        dob_ref[...] = _dot_nn(dyb, pt_ref[:, A_WIDTH:A_WIDTH + B_Q_WIDTH]).astype(BF16)

        @pl.when(i == 0)
        def _():
            acc_o[...] = jnp.zeros_like(acc_o)
            acc_p[...] = jnp.zeros_like(acc_p)

        acc_o[...] += _dot_tn(mg_ref[...], db)
        acc_p[:, 0:A_WIDTH] += _dot_tn(dya, oa_ref[...])
        acc_p[:, A_WIDTH:A_WIDTH + B_Q_WIDTH] += _dot_tn(dyb, ob_ref[...])

        @pl.when(i == nt - 1)
        def _():
            gwo_ref[...] = acc_o[...].astype(BF16)
            gwp_ref[...] = acc_p[...].astype(BF16)

    whole = pl.BlockSpec((D_MODEL, D_MODEL), lambda i: (0, 0))
    return _call(
        body,
        name="mix_out_bwd",
        grid=(nt,),
        in_specs=[_rows(TM, D_MODEL), _rows(TM, 2 * D_MODEL), _rows(TM, D_MODEL), _rows(TM, D_MODEL),
                  _rows(TM, D_MODEL), _rows(TM, A_WIDTH), _rows(TM, B_Q_WIDTH),
                  _resident((D_MODEL, A_WIDTH + B_Q_WIDTH)), _resident((D_MODEL, D_MODEL))],
        out_specs=[_rows(TM, A_WIDTH), _rows(TM, B_Q_WIDTH), _rows(TM, 2 * D_MODEL), whole, whole],
        out_shape=[jax.ShapeDtypeStruct((t, A_WIDTH), BF16), jax.ShapeDtypeStruct((t, B_Q_WIDTH), BF16),
                   jax.ShapeDtypeStruct((t, 2 * D_MODEL), BF16), jax.ShapeDtypeStruct((D_MODEL, D_MODEL), BF16),
                   jax.ShapeDtypeStruct((D_MODEL, D_MODEL), BF16)],
        scratch_shapes=[pltpu.VMEM((D_MODEL, D_MODEL), F32), pltpu.VMEM((D_MODEL, A_WIDTH + B_Q_WIDTH), F32)],
        args=(d, gates, ya, yb, mg, oa, ob, proj_t, wout), sem=("arbitrary",), hosted=hosted)


def _place():
    x, y, c = lax.axis_index("x"), lax.axis_index("y"), lax.axis_index("c")
    chips = [(1 - x, y), (x, 1 - y), (1 - x, 1 - y)]
    return x, y, c, chips


class _Gather:
    per = 8

    def __init__(self, shards):
        n = len(shards)
        self.inputs = list(shards)
        self.out_shape = [jax.ShapeDtypeStruct((N_DEV * s.shape[0], s.shape[1]), s.dtype) for s in shards]
        self.scratch = [pltpu.SemaphoreType.DMA((n * self.per,)), pltpu.SemaphoreType.DMA((n * self.per,)),
                        pltpu.SemaphoreType.DMA((n,))]
        self.result = None

    def _parts(self, ins, outs, sems):
        send_sems, recv_sems, local_sems = sems
        x, y, c, chips = _place()
        me, sibling = (x, y, c), (x, y, 1 - c)
        xn, yn, dg = chips
        n = len(ins)

        def rows(k, p, part=None):
            r = ins[k].shape[0]
            base = (4 * p[0] + 2 * p[1] + p[2]) * r
            if part is None:
                return outs[k].at[pl.ds(base, r), :]
            return outs[k].at[pl.ds(base + part * (r // 2), r // 2), :]

        def copy(k, slot, block, to, src=None, part=None):
            return pltpu.make_async_remote_copy(
                src_ref=rows(k, block, part) if src is None else src, dst_ref=rows(k, block, part),
                send_sem=send_sems.at[k * self.per + slot], recv_sem=recv_sems.at[k * self.per + slot],
                device_id=to, device_id_type=MESH)

        mine = [pltpu.make_async_copy(ins[k], rows(k, me), local_sems.at[k]) for k in range(n)]
        sends, lands = [], []
        for k in range(n):
            sends.append({
                0: copy(k, 0, me, sibling, src=ins[k]),
                1: copy(k, 1, me, (*xn, c), src=ins[k]),
                2: copy(k, 2, me, (*yn, c), src=ins[k]),
                3: copy(k, 3, (*xn, c), (*yn, c), part=0),
                4: copy(k, 4, (*yn, c), (*xn, c), part=1),
                5: copy(k, 5, (*xn, c), sibling),
                6: copy(k, 6, (*yn, c), sibling),
                7: copy(k, 7, (*dg, c), sibling)})
            lands.append({
                0: copy(k, 0, sibling, me),
                1: copy(k, 1, (*xn, c), me),
                2: copy(k, 2, (*yn, c), me),
                3: copy(k, 3, (*dg, c), me, part=0),
                4: copy(k, 4, (*dg, c), me, part=1),
                5: copy(k, 5, (*xn, 1 - c), me),
                6: copy(k, 6, (*yn, 1 - c), me),
                7: copy(k, 7, (*dg, 1 - c), me)})
        return n, mine, sends, lands

    def start(self, ins, outs, sems):
        n, mine, sends, _ = self._parts(ins, outs, sems)
        for cp in mine:
            cp.start()
        for slot in (0, 1, 2):
            for k in range(n):
                sends[k][slot].start()

    def relay(self, ins, outs, sems):
        n, _, sends, lands = self._parts(ins, outs, sems)
        for k in range(n):
            lands[k][1].wait_recv()
            sends[k][3].start()
            sends[k][5].start()
        for k in range(n):
            lands[k][2].wait_recv()
            sends[k][4].start()
            sends[k][6].start()

    def forward(self, ins, outs, sems):
        n, _, sends, lands = self._parts(ins, outs, sems)
        for k in range(n):
            lands[k][3].wait_recv()
            lands[k][4].wait_recv()
            sends[k][7].start()

    def finish(self, ins, outs, sems):
        n, mine, sends, lands = self._parts(ins, outs, sems)
        for k in range(n):
            for slot in (0, 5, 6, 7):
                lands[k][slot].wait_recv()
        for k in range(n):
            for slot in range(self.per):
                sends[k][slot].wait_send()
        for cp in mine:
            cp.wait()


class _PairExchange:
    def __init__(self, grads):
        n = len(grads)
        self.inputs = list(grads)
        self.out_shape = [jax.ShapeDtypeStruct((g.shape[0] // 2, g.shape[1]), g.dtype) for g in grads]
        self.scratch = [pltpu.SemaphoreType.DMA((n * N_CHIP,)), pltpu.SemaphoreType.DMA((n * N_CHIP,))]
        self.result = None

    def _copies(self, ins, outs, sems):
        send_sems, recv_sems = sems
        x, y, c, _ = _place()
        copies = []
        for k in range(len(ins)):
            r = ins[k].shape[0] // N_DEV
            for q in range(N_CHIP):
                copies.append(pltpu.make_async_remote_copy(
                    src_ref=ins[k].at[pl.ds((2 * q + 1 - c) * r, r), :], dst_ref=outs[k].at[pl.ds(q * r, r), :],
                    send_sem=send_sems.at[k * N_CHIP + q], recv_sem=recv_sems.at[k * N_CHIP + q],
                    device_id=(x, y, 1 - c), device_id_type=MESH))
        return copies

    def start(self, ins, outs, sems):
        for cp in self._copies(ins, outs, sems):
            cp.start()

    def relay(self, ins, outs, sems):
        pass

    def forward(self, ins, outs, sems):
        pass

    def finish(self, ins, outs, sems):
        copies = self._copies(ins, outs, sems)
        for cp in copies:
            cp.wait_recv()
        for cp in copies:
            cp.wait_send()


class _ChipExchange(_PairExchange):
    def __init__(self, psums):
        n = len(psums)
        self.inputs = list(psums)
        self.out_shape = [jax.ShapeDtypeStruct((3 * p.shape[0] // N_CHIP, p.shape[1]), p.dtype) for p in psums]
        self.scratch = [pltpu.SemaphoreType.DMA((n * 3,)), pltpu.SemaphoreType.DMA((n * 3,))]
        self.result = None

    def _copies(self, ins, outs, sems):
        send_sems, recv_sems = sems
        _, _, c, chips = _place()
        copies = []
        for k in range(len(ins)):
            r = ins[k].shape[0] // N_CHIP
            for j, chip in enumerate(chips):
                copies.append(pltpu.make_async_remote_copy(
                    src_ref=ins[k].at[pl.ds((2 * chip[0] + chip[1]) * r, r), :], dst_ref=outs[k].at[pl.ds(j * r, r), :],
                    send_sem=send_sems.at[k * 3 + j], recv_sem=recv_sems.at[k * 3 + j],
                    device_id=(*chip, c), device_id_type=MESH))
        return copies


def _exchange_alone(xchg, name):
    n_in, n_out = len(xchg.inputs), len(xchg.out_shape)

    def body(*refs):
        ins, outs, sems = refs[:n_in], refs[n_in:n_in + n_out], refs[n_in + n_out:]
        xchg.start(ins, outs, sems)
        xchg.relay(ins, outs, sems)
        xchg.forward(ins, outs, sems)
        xchg.finish(ins, outs, sems)

    xchg.result = list(pl.pallas_call(
        body, name=name, in_specs=[_hbm()] * n_in, out_specs=[_hbm()] * n_out, out_shape=xchg.out_shape,
        scratch_shapes=xchg.scratch)(*xchg.inputs))
    return xchg.result


def _pair_sum(core, grads, recvd, name):
    n = len(grads)
    r = grads[0].shape[0] // N_DEV
    cdim = grads[0].shape[1]
    tr = r // 2 if r % 32 == 0 else r
    nt = r // tr

    def body(core_ref, *refs):
        del core_ref
        for k in range(n):
            refs[2 * n + k][...] = (refs[k][...].astype(F32) + refs[n + k][...].astype(F32)).astype(BF16)

    gspec = pl.BlockSpec((tr, cdim), lambda q, i, core_ref: ((2 * q + core_ref[0]) * nt + i, 0))
    rspec = pl.BlockSpec((tr, cdim), lambda q, i, core_ref: (q * nt + i, 0))
    return pl.pallas_call(
        body,
        name=name,
        grid_spec=pltpu.PrefetchScalarGridSpec(
            num_scalar_prefetch=1, grid=(N_CHIP, nt), in_specs=[gspec] * n + [rspec] * n, out_specs=[rspec] * n),
        out_shape=[jax.ShapeDtypeStruct((N_CHIP * r, cdim), BF16) for _ in range(n)],
        compiler_params=_cparams(("parallel", "parallel")),
    )(core, *grads, *recvd)


def _final_sum(chip, psums, recvd, name):
    n = len(psums)
    r = psums[0].shape[0] // N_CHIP
    cdim = psums[0].shape[1]
    tr = r // 2 if r % 32 == 0 else r
    nt = r // tr

    def body(chip_ref, *refs):
        del chip_ref
        for k in range(n):
            got = refs[n + k]
            tot = refs[k][...].astype(F32) + got[0].astype(F32)
            tot = tot + got[1].astype(F32)
            tot = tot + got[2].astype(F32)
            refs[2 * n + k][...] = tot

    pspec = pl.BlockSpec((tr, cdim), lambda i, chip_ref: (chip_ref[0] * nt + i, 0))
    rspec = pl.BlockSpec((3, tr, cdim), lambda i, chip_ref: (0, i, 0))
    ospec = pl.BlockSpec((tr, cdim), lambda i, chip_ref: (i, 0))
    return pl.pallas_call(
        body,
        name=name,
        grid_spec=pltpu.PrefetchScalarGridSpec(
            num_scalar_prefetch=1, grid=(nt,), in_specs=[pspec] * n + [rspec] * n, out_specs=[ospec] * n),
        out_shape=[jax.ShapeDtypeStruct((r, cdim), F32) for _ in range(n)],
        compiler_params=_cparams(("parallel",)),
    )(chip, *psums, *[g.reshape(3, r, cdim) for g in recvd])


SMALL_ROWS = 16


def _all_reduce_small(part):
    def body(p_ref, o_ref, buf, send_sems, recv_sems):
        x, y, c, _ = _place()
        me = 4 * x + 2 * y + c
        buf[me] = p_ref[...]
        copies = []
        for d in range(1, N_DEV):
            peer = me ^ d
            copies.append(pltpu.make_async_remote_copy(
                src_ref=p_ref, dst_ref=buf.at[me], send_sem=send_sems.at[d - 1], recv_sem=recv_sems.at[d - 1],
                device_id=(peer // 4, (peer // 2) % 2, peer % 2), device_id_type=MESH))
        for cp in copies:
            cp.start()
        for cp in copies:
            cp.wait_recv()
        for cp in copies:
            cp.wait_send()
        tot = buf[0]
        for d in range(1, N_DEV):
            tot = tot + buf[d]
        o_ref[...] = tot

    return pl.pallas_call(
        body,
        name="all_reduce_small",
        in_specs=[pl.BlockSpec(memory_space=pltpu.VMEM)],
        out_specs=pl.BlockSpec(memory_space=pltpu.VMEM),
        out_shape=jax.ShapeDtypeStruct(part.shape, F32),
        scratch_shapes=[pltpu.VMEM((N_DEV,) + part.shape, F32), pltpu.SemaphoreType.DMA((N_DEV - 1,)),
                        pltpu.SemaphoreType.DMA((N_DEV - 1,))],
    )(part)


ADAMW_STEPS = 4


def _adamw(ws, gs, ms, vs, name, hosted=()):
    n = len(ws)
    steps = ADAMW_STEPS if all(w.shape[0] % (8 * ADAMW_STEPS) == 0 for w in ws) else 1
    c1 = 1.0 - ADAM_B1 ** ADAM_STEP
    c2 = 1.0 - ADAM_B2 ** ADAM_STEP

    def body(*refs):
        for k in range(n):
            w, g, m, v = (refs[j * n + k][...] for j in range(4))
            m2 = ADAM_B1 * m + (1.0 - ADAM_B1) * g
            v2 = ADAM_B2 * v + (1.0 - ADAM_B2) * (g * g)
            delta = -ADAM_LR * ((m2 * (1.0 / c1)) / (jnp.sqrt(v2 * (1.0 / c2)) + ADAM_EPS) + ADAM_WD * w)
            refs[4 * n + k][...] = delta
            refs[5 * n + k][...] = m2
            refs[6 * n + k][...] = v2

    specs = [pl.BlockSpec((w.shape[0] // steps, w.shape[1]), lambda i: (i, 0)) for w in ws]
    shapes = [jax.ShapeDtypeStruct(w.shape, F32) for w in ws]
    outs = _call(
        body,
        name=name,
        grid=(steps,),
        in_specs=specs * 4,
        out_specs=specs * 3,
        out_shape=shapes * 3,
        args=(*ws, *gs, *ms, *vs), sem=("parallel",), hosted=hosted)
    return outs[:n], outs[n:2 * n], outs[2 * n:]


def _adamw_reduced(chip, ws, psums, recvd, ms, vs, steps, name):
    n = len(ws)
    c1 = 1.0 - ADAM_B1 ** ADAM_STEP
    c2 = 1.0 - ADAM_B2 ** ADAM_STEP

    def body(chip_ref, *refs):
        del chip_ref
        for k in range(n):
            w, m, v = (refs[j * n + k][...] for j in (0, 3, 4))
            got = refs[2 * n + k]
            g = refs[n + k][...].astype(F32) + got[0].astype(F32)
            g = g + got[1].astype(F32)
            g = g + got[2].astype(F32)
            m2 = ADAM_B1 * m + (1.0 - ADAM_B1) * g
            v2 = ADAM_B2 * v + (1.0 - ADAM_B2) * (g * g)
            refs[5 * n + k][...] = g
            refs[6 * n + k][...] = -ADAM_LR * (
                (m2 * (1.0 / c1)) / (jnp.sqrt(v2 * (1.0 / c2)) + ADAM_EPS) + ADAM_WD * w)
            refs[7 * n + k][...] = m2
            refs[8 * n + k][...] = v2

    def blk(w):
        return (w.shape[0] // steps, w.shape[1])

    own = [pl.BlockSpec(blk(w), lambda i, chip_ref: (i, 0)) for w in ws]
    psum = [pl.BlockSpec(blk(w), lambda i, chip_ref: (chip_ref[0] * steps + i, 0)) for w in ws]
    recv = [pl.BlockSpec((3,) + blk(w), lambda i, chip_ref: (0, i, 0)) for w in ws]
    shapes = [jax.ShapeDtypeStruct(w.shape, F32) for w in ws]
    outs = pl.pallas_call(
        body,
        name=name,
        grid_spec=pltpu.PrefetchScalarGridSpec(
            num_scalar_prefetch=1, grid=(steps,), in_specs=own + psum + recv + own + own, out_specs=own * 4),
        out_shape=shapes * 4,
        compiler_params=_cparams(("parallel",)),
    )(chip, *ws, *psums, *[r.reshape((3,) + w.shape) for r, w in zip(recvd, ws)], *ms, *vs)
    return outs[:n], outs[n:2 * n], outs[2 * n:3 * n], outs[3 * n:]


def _bias_b():
    pad = B_PREV * CHUNK
    slopes = np.array([2.0 ** (-8.0 * (i + 1) / B_Q_HEADS) for i in range(B_Q_HEADS)], dtype=np.float32)
    dist = np.abs(np.arange(TQ)[:, None] - np.arange(TQ + pad)[None, :] + pad).astype(np.float32)
    bias = -slopes.reshape(B_Q_HEADS, 1, 1) * dist[None]
    qc = (np.arange(TQ)[:, None] + pad) // CHUNK
    kc = np.arange(TQ + pad)[None, :] // CHUNK
    allowed = (kc <= qc) & (kc >= qc - B_PREV)
    return np.where(allowed[None], bias, np.float32(NEG_INF)).astype(np.float32)


def kernel(x, ffn1_norm, ffn1_w_gate, ffn1_w_up, ffn1_w_down, mix_norm, w_in, rel_bias, sinks, w_proj_a, w_proj_b, w_out, ffn2_norm, ffn2_w_gate, ffn2_w_up, ffn2_w_down, final_norm, loss_target, m_ffn1_norm, m_ffn1_w_gate, m_ffn1_w_up, m_ffn1_w_down, m_mix_norm, m_w_in, m_rel_bias, m_sinks, m_w_proj_a, m_w_proj_b, m_w_out, m_ffn2_norm, m_ffn2_w_gate, m_ffn2_w_up, m_ffn2_w_down, m_final_norm, v_ffn1_norm, v_ffn1_w_gate, v_ffn1_w_up, v_ffn1_w_down, v_mix_norm, v_w_in, v_rel_bias, v_sinks, v_w_proj_a, v_w_proj_b, v_w_out, v_ffn2_norm, v_ffn2_w_gate, v_ffn2_w_up, v_ffn2_w_down, v_final_norm):
    bsz, s_len, _ = x.shape
    t = bsz * s_len
    core = lax.axis_index("c").astype(jnp.int32).reshape(1)
    chip = (2 * lax.axis_index("x") + lax.axis_index("y")).astype(jnp.int32).reshape(1)

    proj_rows = jnp.concatenate([w_proj_a.T, w_proj_b.T], axis=1)
    sh_g1, sh_u1, sh_d1, sh_in, sh_proj, sh_out, sh_g2, sh_u2, sh_d2 = _to_bf16(
        [ffn1_w_gate.T, ffn1_w_up.T, ffn1_w_down, w_in.T, proj_rows, w_out, ffn2_w_gate.T, ffn2_w_up.T, ffn2_w_down],
        "weights_to_bf16")

    gather_up1 = _Gather([sh_g1, sh_u1])
    far = jnp.broadcast_to(rel_bias[:, REL_TABLE - 1:REL_TABLE], (A_HEADS, REL_WRAP // 2))
    tv = jnp.concatenate([far, jnp.flip(rel_bias, axis=1), jnp.zeros((A_HEADS, REL_WRAP // 2 - REL_TABLE), F32)], axis=1)
    bias_a = _bias_a_build(tv.reshape(A_HEADS, 1, REL_WRAP), hosted=[gather_up1])
    wg1, wu1 = gather_up1.result
    gather_down1 = _Gather([sh_d1, sh_in])
    gather_out = _Gather([sh_proj, sh_out])
    gather_ffn2_gate = _Gather([sh_g2])
    gather_ffn2_rest = _Gather([sh_u2, sh_d2])

    x0 = x.reshape(t, D_MODEL)
    tgt = loss_target.reshape(t, D_MODEL)
    gam1, gam2, gam3, gam4 = (g.reshape(1, D_MODEL) for g in (ffn1_norm, mix_norm, ffn2_norm, final_norm))

    h1, g1, u1, a1 = _ffn_up(x0, gam1, wg1, wu1, "ffn1_up", hosted=[gather_down1])
    wd1, win_t = gather_down1.result
    x1 = _ffn_down(x0, a1, wd1, "ffn1_down")
    h2, qkv_a, qkv_b, gates = _proj_fwd(x1, gam2, win_t, hosted=[gather_out, gather_ffn2_gate])
    proj_t, wout = gather_out.result
    (wg2,) = gather_ffn2_gate.result
    qkv_a3 = qkv_a.reshape(bsz, s_len, QKV_A)
    qkv_b3 = qkv_b.reshape(bsz, s_len, QKV_B)

    bias_b = jnp.asarray(_bias_b())
    sink_rows = jnp.broadcast_to(sinks.reshape(B_Q_HEADS, 1, 1), (B_Q_HEADS, 8, LANES))

    oa = _attn_a_fwd(qkv_a3, bias_a, hosted=[gather_ffn2_rest]).reshape(t, A_WIDTH)
    wu2, wd2 = gather_ffn2_rest.result
    ob = _attn_b_fwd(qkv_b3, bias_b, sink_rows).reshape(t, B_Q_WIDTH)
    x2, ya, yb, mg = _mix_out_fwd(x1, oa, ob, gates, proj_t, wout)
    h3, g2, u2, a2, x3 = _ffn_fwd(x2, gam3, wg2, wu2, wd2, "ffn2_fwd")

    dx2, dg2, du2, db2, dgam3, dgam4, loss_part = _ffn_bwd_head(x3, gam4, tgt, x2, gam3, g2, u2, wg2, wu2, wd2,
                                                                "ffn2_bwd")
    gw_ffn2 = [_mm_tn([dg2], h3, "grad_ffn2_gate"), _mm_tn([du2], h3, "grad_ffn2_up"),
               _mm_tn([a2], db2, "grad_ffn2_down")]
    pairx_ffn2 = _PairExchange(gw_ffn2)
    doa, dob, dgates, gw_out, gw_proj = _mix_out_bwd(dx2, gates, ya, yb, mg, oa, ob, proj_t, wout,
                                                     hosted=[pairx_ffn2])
    psum_ffn2 = _pair_sum(core, gw_ffn2, pairx_ffn2.result, "pair_sum_ffn2")

    chipx_ffn2 = _ChipExchange(psum_ffn2)
    dqa, dka, dva, dbias_a = _attn_a_bwd(qkv_a3, bias_a, doa.reshape(bsz, s_len, A_WIDTH), hosted=[chipx_ffn2])
    pairx_out = _PairExchange([gw_proj, gw_out])
    dqb, dkvb, dsink = _attn_b_bwd(qkv_b3, bias_b, sink_rows, dob.reshape(bsz, s_len, B_Q_WIDTH), hosted=[pairx_out])
    drel_lanes = _relbias_grad(dbias_a)
    dproj = [dqa.reshape(t, A_WIDTH), dka.reshape(t, A_WIDTH), dva.reshape(t, A_WIDTH), dqb.reshape(t, B_Q_WIDTH),
             dkvb.reshape(t, 2 * B_KV_WIDTH), dgates]

    gw_in = _mm_tn(dproj, h2, "grad_w_in")
    pairx_in = _PairExchange([gw_in])
    psum_out = _pair_sum(core, [gw_proj, gw_out], pairx_out.result, "pair_sum_mix")
    chipx_out = _ChipExchange(psum_out)
    dx1, db1, dgam2 = _proj_bwd(dx2, x1, gam2, dproj, win_t, hosted=[pairx_in, chipx_out])
    psum_in = _pair_sum(core, [gw_in], pairx_in.result, "pair_sum_w_in")
    gw_d1 = _mm_tn([a1], db1, "grad_ffn1_down")

    chipx_in = _ChipExchange(psum_in)
    pairx_d1 = _PairExchange([gw_d1])
    dg1, du1 = _ffn_bwd_act(dx1, g1, u1, wd1, "ffn1_bwd_act", hosted=[chipx_in, pairx_d1])
    psum_d1 = _pair_sum(core, [gw_d1], pairx_d1.result, "pair_sum_ffn1_down")
    chipx_d1 = _ChipExchange(psum_d1)
    gw_g1 = _mm_tn([dg1], h1, "grad_ffn1_gate", hosted=[chipx_d1])
    from_sibling_g1 = _exchange_alone(_PairExchange([gw_g1]), "pair_exchange_ffn1_gate")
    psum_g1 = _pair_sum(core, [gw_g1], from_sibling_g1, "pair_sum_ffn1_gate")
    chipx_g1 = _ChipExchange(psum_g1)
    gw_u1 = _mm_tn([du1], h1, "grad_ffn1_up", hosted=[chipx_g1])
    from_sibling_u1 = _exchange_alone(_PairExchange([gw_u1]), "pair_exchange_ffn1_up")
    psum_u1 = _pair_sum(core, [gw_u1], from_sibling_u1, "pair_sum_ffn1_up")
    chipx_u1 = _ChipExchange(psum_u1)
    dx0, dgam1 = _ffn_bwd_in(dx1, x0, gam1, dg1, du1, wg1, wu1, "ffn1_bwd_in", hosted=[chipx_u1])

    (g_proj,) = _final_sum(chip, psum_out[0:1], chipx_out.result[0:1], "grad_sum_proj")
    grads = {"w_proj_a": g_proj[:, 0:A_WIDTH].T, "w_proj_b": g_proj[:, A_WIDTH:].T}

    def row_of(v):
        return jnp.pad(v.reshape(1, -1), ((0, 0), (0, D_MODEL - v.size)))

    def table_rows(v):
        return jnp.pad(v, ((0, 0), (0, D_MODEL - REL_TABLE)))

    drel_local = jnp.flip(drel_lanes[:, 0, 0:REL_TABLE], axis=1)
    small_part = jnp.concatenate(
        [jnp.sum(dgam1, axis=0, keepdims=True), jnp.sum(dgam2, axis=0, keepdims=True),
         jnp.sum(dgam3, axis=0, keepdims=True), jnp.sum(dgam4, axis=0, keepdims=True),
         row_of(jnp.sum(loss_part)), row_of(dsink[:, 0, 0]), jnp.zeros((2, D_MODEL), F32),
         table_rows(drel_local)], axis=0)
    small = _all_reduce_small(small_part)
    loss = small[4, 0]

    def pack(n1, n2, n3, n4, sk, tb):
        return jnp.concatenate([n1.reshape(1, -1), n2.reshape(1, -1), n3.reshape(1, -1), n4.reshape(1, -1),
                                jnp.zeros((1, D_MODEL), F32), row_of(sk), jnp.zeros((2, D_MODEL), F32), table_rows(tb)],
                               axis=0)

    live = np.zeros((SMALL_ROWS, D_MODEL), np.float32)
    live[0:4] = 1.0
    live[5, 0:B_Q_HEADS] = 1.0
    live[8:16, 0:REL_TABLE] = 1.0
    small_g = small * jnp.asarray(live)
    sw = pack(ffn1_norm, mix_norm, ffn2_norm, final_norm, sinks, rel_bias)
    sm = pack(m_ffn1_norm, m_mix_norm, m_ffn2_norm, m_final_norm, m_sinks, m_rel_bias)
    sv = pack(v_ffn1_norm, v_mix_norm, v_ffn2_norm, v_final_norm, v_sinks, v_rel_bias)
    (sd,), (snm,), (snv,) = _adamw([sw], [small_g], [sm], [sv], "adamw_small")

    def unpack(p):
        return {"ffn1_norm": p[0], "mix_norm": p[1], "ffn2_norm": p[2], "final_norm": p[3],
                "sinks": p[5, 0:B_Q_HEADS], "rel_bias": p[8:16, 0:REL_TABLE]}

    grads.update(unpack(small_g))
    delta, new_m, new_v = unpack(sd), unpack(snm), unpack(snv)

    wmv = {
        "ffn1_w_gate": (ffn1_w_gate, m_ffn1_w_gate, v_ffn1_w_gate), "ffn1_w_up": (ffn1_w_up, m_ffn1_w_up, v_ffn1_w_up),
        "ffn1_w_down": (ffn1_w_down, m_ffn1_w_down, v_ffn1_w_down), "w_in": (w_in, m_w_in, v_w_in),
        "w_proj_a": (w_proj_a, m_w_proj_a, v_w_proj_a), "w_proj_b": (w_proj_b, m_w_proj_b, v_w_proj_b),
        "w_out": (w_out, m_w_out, v_w_out),
        "ffn2_w_gate": (ffn2_w_gate, m_ffn2_w_gate, v_ffn2_w_gate), "ffn2_w_up": (ffn2_w_up, m_ffn2_w_up, v_ffn2_w_up),
        "ffn2_w_down": (ffn2_w_down, m_ffn2_w_down, v_ffn2_w_down),
    }
    row_form_names = ("ffn1_w_gate", "ffn1_w_up", "w_in", "ffn2_w_gate", "ffn2_w_up")

    def form(n, a):
        return a.T if n in row_form_names else a

    def reduced_group(gname, names, psums, recvd, steps):
        gs_, ds_, ms_, vs_ = _adamw_reduced(
            chip, [form(n, wmv[n][0]) for n in names], psums, recvd, [form(n, wmv[n][1]) for n in names],
            [form(n, wmv[n][2]) for n in names], steps, gname)
        for n, g_, d_, m_, v_ in zip(names, gs_, ds_, ms_, vs_):
            grads[n], delta[n], new_m[n], new_v[n] = form(n, g_), form(n, d_), form(n, m_), form(n, v_)

    reduced_group("adamw_ffn", ["ffn1_w_gate", "ffn1_w_up", "ffn1_w_down", "ffn2_w_gate", "ffn2_w_up", "ffn2_w_down"],
                  psum_g1 + psum_u1 + psum_d1 + psum_ffn2,
                  chipx_g1.result + chipx_u1.result + chipx_d1.result + chipx_ffn2.result, 11)
    reduced_group("adamw_in_out", ["w_in", "w_out"], psum_in + psum_out[1:2], chipx_in.result + chipx_out.result[1:2], 2)
    names = ["w_proj_a", "w_proj_b"]
    ds_, ms_, vs_ = _adamw([wmv[n][0] for n in names], [grads[n] for n in names], [wmv[n][1] for n in names],
                           [wmv[n][2] for n in names], "adamw_proj")
    for n, d_, m_, v_ in zip(names, ds_, ms_, vs_):
        delta[n], new_m[n], new_v[n] = d_, m_, v_

    order = ["ffn1_norm", "ffn1_w_gate", "ffn1_w_up", "ffn1_w_down", "mix_norm", "w_in", "rel_bias", "sinks",
             "w_proj_a", "w_proj_b", "w_out", "ffn2_norm", "ffn2_w_gate", "ffn2_w_up", "ffn2_w_down", "final_norm"]
    grad_x = dx0.reshape(bsz, s_len, D_MODEL)
    return (loss, grad_x, *[grads[n] for n in order], *[delta[n] for n in order], *[new_m[n] for n in order],
            *[new_v[n] for n in order])
```

```python
import numpy as np
import jax
import jax.numpy as jnp
from jax import lax
from jax.experimental import pallas as pl
from jax.experimental.pallas import tpu as pltpu

F32 = jnp.float32
BF16 = jnp.bfloat16

D_MODEL = 1024
D_FF = 2816
CHUNK = 64
D_HEAD = 64
A_HEADS = 8
A_PREV = 8
MAX_REL = 128
B_Q_HEADS = 8
B_KV_HEADS = 2
B_GROUP = B_Q_HEADS // B_KV_HEADS
B_PREV = 2
REL_TABLE = (CHUNK - 1) + MAX_REL + 1
A_WIDTH = A_HEADS * D_HEAD
B_Q_WIDTH = B_Q_HEADS * D_HEAD
B_KV_WIDTH = B_KV_HEADS * D_HEAD
QKV_A = 3 * A_WIDTH
QKV_B = B_Q_WIDTH + 2 * B_KV_WIDTH
IN_WIDTH = QKV_A + QKV_B + 2 * D_MODEL
EPS = 1e-6
NEG_INF = -1e30
SCALE = 1.0 / 8.0

ADAM_LR = 0.001
ADAM_B1 = 0.9
ADAM_B2 = 0.999
ADAM_EPS = 1e-08
ADAM_WD = 0.01
ADAM_STEP = 10

N_DEV = 8
N_CHIP = 4
MESH = pl.DeviceIdType.MESH

LANES = 128
TQ = 256
TM = 256
FC = 256
VMEM_LIMIT = 56 << 20


def _cparams(sem, vmem=VMEM_LIMIT):
    return pltpu.CompilerParams(dimension_semantics=sem, vmem_limit_bytes=vmem)


def _dot_nt(a, b):
    return lax.dot_general(a, b, (((1,), (1,)), ((), ())), preferred_element_type=F32)


def _dot_nn(a, b):
    return lax.dot_general(a, b, (((1,), (0,)), ((), ())), preferred_element_type=F32)


def _dot_tn(a, b):
    return lax.dot_general(a, b, (((0,), (0,)), ((), ())), preferred_element_type=F32)


def _resident(shape):
    nd = len(shape)
    return pl.BlockSpec(shape, lambda *_: (0,) * nd, pipeline_mode=pl.Buffered(1))


def _rows(tm, width):
    return pl.BlockSpec((tm, width), lambda i: (i, 0))


def _colsum8(v):
    tm, n = v.shape
    return jnp.sum(v.reshape(tm // 8, 8, n), axis=0)


def _rms(x):
    r = lax.rsqrt(jnp.mean(x * x, axis=-1, keepdims=True) + EPS)
    return x * r, r


def _rms_bwd(dh, xh, r, gamma):
    dxh = dh * gamma
    dx = r * (dxh - xh * jnp.mean(dxh * xh, axis=-1, keepdims=True))
    return dx, _colsum8(dh * xh)


def _hbm():
    return pl.BlockSpec(memory_space=pltpu.HBM)


def _call(body, *, name, grid, in_specs, out_specs, out_shape, args, sem, scratch_shapes=(), hosted=()):
    in_specs, out_specs, out_shape = list(in_specs), list(out_specs), list(out_shape)
    scratch_shapes = list(scratch_shapes)
    if not hosted:
        return pl.pallas_call(body, name=name, grid=grid, in_specs=in_specs, out_specs=out_specs, out_shape=out_shape,
                              scratch_shapes=scratch_shapes, compiler_params=_cparams(sem))(*args)
    n_in, n_out, n_scr = len(in_specs), len(out_specs), len(scratch_shapes)
    x_in = [a for x in hosted for a in x.inputs]
    x_out = [s for x in hosted for s in x.out_shape]
    x_scr = [s for x in hosted for s in x.scratch]
    steps = int(np.prod(grid))
    forward_step = max(steps - 3, 0)
    relay_step = min((5 * steps) // 8, forward_step)

    def wrapped(*refs):
        pos = [0]

        def take(k):
            pos[0] += k
            return refs[pos[0] - k:pos[0]]

        ins, xin, outs, xout, scr, xscr = (take(k) for k in (n_in, len(x_in), n_out, len(x_out), n_scr, len(x_scr)))
        step = 0
        for axis, extent in enumerate(grid):
            step = step * extent + pl.program_id(axis)
        own, oi, oo, osc = [], 0, 0, 0
        for x in hosted:
            own.append((xin[oi:oi + len(x.inputs)], xout[oo:oo + len(x.out_shape)], xscr[osc:osc + len(x.scratch)]))
            oi, oo, osc = oi + len(x.inputs), oo + len(x.out_shape), osc + len(x.scratch)

        def phase(method):
            for x, (i_, o_, s_) in zip(hosted, own):
                getattr(x, method)(i_, o_, s_)

        pl.when(step == 0)(lambda: phase("start"))
        body(*ins, *outs, *scr)
        pl.when(step == relay_step)(lambda: phase("relay"))
        pl.when(step == forward_step)(lambda: phase("forward"))
        pl.when(step == steps - 1)(lambda: phase("finish"))

    res = pl.pallas_call(
        wrapped, name=name, grid=grid, in_specs=in_specs + [_hbm()] * len(x_in),
        out_specs=out_specs + [_hbm()] * len(x_out), out_shape=out_shape + x_out,
        scratch_shapes=scratch_shapes + x_scr, compiler_params=_cparams(("arbitrary",) * len(grid)))(*args, *x_in)
    rest = list(res[n_out:])
    for x in hosted:
        x.result, rest = rest[:len(x.out_shape)], rest[len(x.out_shape):]
    return list(res[:n_out])


def _to_bf16(arrays, name):
    n = len(arrays)

    def body(*refs):
        for k in range(n):
            refs[n + k][...] = refs[k][...].astype(BF16)

    specs = [pl.BlockSpec(a.shape, lambda i: (0, 0)) for a in arrays]
    return pl.pallas_call(
        body, name=name, grid=(1,), in_specs=specs, out_specs=specs,
        out_shape=[jax.ShapeDtypeStruct(a.shape, BF16) for a in arrays],
        compiler_params=_cparams(("arbitrary",)))(*arrays)


def _ffn_fwd(x, gamma, wg_t, wu_t, wd, name, hosted=()):
    t = x.shape[0]
    f = wg_t.shape[0]

    def body(x_ref, gam_ref, wg_ref, wu_ref, wd_ref, h_ref, g_ref, u_ref, a_ref, y_ref):
        xv = x_ref[...]
        xh, _ = _rms(xv)
        h = (xh * gam_ref[...]).astype(BF16)
        h_ref[...] = h
        for j in range(f // FC):
            sl = slice(j * FC, (j + 1) * FC)
            g = _dot_nt(h, wg_ref[sl, :])
            u = _dot_nt(h, wu_ref[sl, :])
            g_ref[:, sl] = g.astype(BF16)
            u_ref[:, sl] = u.astype(BF16)
            a_ref[:, sl] = (g * jax.nn.sigmoid(g) * u).astype(BF16)
        y_ref[...] = xv + 0.5 * _dot_nn(a_ref[...], wd_ref[...])

    return _call(
        body,
        name=name,
        grid=(t // TM,),
        in_specs=[_rows(TM, D_MODEL), _resident((1, D_MODEL)), _resident((f, D_MODEL)), _resident((f, D_MODEL)),
                  _resident((f, D_MODEL))],
        out_specs=[_rows(TM, D_MODEL), _rows(TM, f), _rows(TM, f), _rows(TM, f), _rows(TM, D_MODEL)],
        out_shape=[jax.ShapeDtypeStruct((t, D_MODEL), BF16), jax.ShapeDtypeStruct((t, f), BF16),
                   jax.ShapeDtypeStruct((t, f), BF16), jax.ShapeDtypeStruct((t, f), BF16),
                   jax.ShapeDtypeStruct((t, D_MODEL), F32)],
        args=(x, gamma, wg_t, wu_t, wd), sem=("parallel",), hosted=hosted)


def _ffn_up(x, gamma, wg_t, wu_t, name, hosted=()):
    t = x.shape[0]
    f = wg_t.shape[0]

    def body(x_ref, gam_ref, wg_ref, wu_ref, h_ref, g_ref, u_ref, a_ref):
        xh, _ = _rms(x_ref[...])
        h = (xh * gam_ref[...]).astype(BF16)
        h_ref[...] = h
        for j in range(f // FC):
            sl = slice(j * FC, (j + 1) * FC)
            g = _dot_nt(h, wg_ref[sl, :])
            u = _dot_nt(h, wu_ref[sl, :])
            g_ref[:, sl] = g.astype(BF16)
            u_ref[:, sl] = u.astype(BF16)
            a_ref[:, sl] = (g * jax.nn.sigmoid(g) * u).astype(BF16)

    return _call(
        body,
        name=name,
        grid=(t // TM,),
        in_specs=[_rows(TM, D_MODEL), _resident((1, D_MODEL)), _resident((f, D_MODEL)), _resident((f, D_MODEL))],
        out_specs=[_rows(TM, D_MODEL), _rows(TM, f), _rows(TM, f), _rows(TM, f)],
        out_shape=[jax.ShapeDtypeStruct((t, D_MODEL), BF16), jax.ShapeDtypeStruct((t, f), BF16),
                   jax.ShapeDtypeStruct((t, f), BF16), jax.ShapeDtypeStruct((t, f), BF16)],
        args=(x, gamma, wg_t, wu_t), sem=("parallel",), hosted=hosted)


def _ffn_down(x, a_act, wd, name, hosted=()):
    t = x.shape[0]
    f = wd.shape[0]

    def body(x_ref, a_ref, wd_ref, y_ref):
        y_ref[...] = x_ref[...] + 0.5 * _dot_nn(a_ref[...], wd_ref[...])

    return _call(
        body,
        name=name,
        grid=(t // TM,),
        in_specs=[_rows(TM, D_MODEL), _rows(TM, f), _resident((f, D_MODEL))],
        out_specs=[_rows(TM, D_MODEL)],
        out_shape=[jax.ShapeDtypeStruct((t, D_MODEL), F32)],
        args=(x, a_act, wd), sem=("parallel",), hosted=hosted)[0]


def _ffn_bwd_head(y, gamma_f, target, x, gamma, g_act, u_act, wg_t, wu_t, wd, name):
    t = x.shape[0]
    f = wg_t.shape[0]

    def body(y_ref, gamf_ref, t_ref, x_ref, gam_ref, g_ref, u_ref, wg_ref, wu_ref, wd_ref, dx_ref, dg_ref, du_ref,
             db_ref, dgam_ref, dgamf_ref, loss_ref):
        yh, ry = _rms(y_ref[...])
        gam_f = gamf_ref[...]
        e = yh * gam_f - t_ref[...]
        dv, dgam_f = _rms_bwd(e * (1.0 / D_MODEL), yh, ry, gam_f)
        db = (0.5 * dv).astype(BF16)
        db_ref[...] = db
        for j in range(f // FC):
            sl = slice(j * FC, (j + 1) * FC)
            da = _dot_nt(db, wd_ref[sl, :])
            g = g_ref[:, sl].astype(F32)
            u = u_ref[:, sl].astype(F32)
            s = jax.nn.sigmoid(g)
            dg_ref[:, sl] = (da * u * (s * (1.0 + g * (1.0 - s)))).astype(BF16)
            du_ref[:, sl] = (da * (g * s)).astype(BF16)
        dh = _dot_nn(dg_ref[...], wg_ref[...]) + _dot_nn(du_ref[...], wu_ref[...])
        xh, r = _rms(x_ref[...])
        dxn, dgam = _rms_bwd(dh, xh, r, gam_ref[...])
        dx_ref[...] = dv + dxn

        @pl.when(pl.program_id(0) == 0)
        def _():
            dgam_ref[...] = jnp.zeros_like(dgam_ref)
            dgamf_ref[...] = jnp.zeros_like(dgamf_ref)
            loss_ref[...] = jnp.zeros_like(loss_ref)

        dgam_ref[...] += dgam
        dgamf_ref[...] += dgam_f
        loss_ref[...] += _colsum8(e * e) * (0.5 / D_MODEL)

    acc = pl.BlockSpec((8, D_MODEL), lambda i: (0, 0))
    return _call(
        body,
        name=name,
        grid=(t // TM,),
        in_specs=[_rows(TM, D_MODEL), _resident((1, D_MODEL)), _rows(TM, D_MODEL), _rows(TM, D_MODEL),
                  _resident((1, D_MODEL)), _rows(TM, f), _rows(TM, f),
                  _resident((f, D_MODEL)), _resident((f, D_MODEL)), _resident((f, D_MODEL))],
        out_specs=[_rows(TM, D_MODEL), _rows(TM, f), _rows(TM, f), _rows(TM, D_MODEL), acc, acc, acc],
        out_shape=[jax.ShapeDtypeStruct((t, D_MODEL), F32), jax.ShapeDtypeStruct((t, f), BF16),
                   jax.ShapeDtypeStruct((t, f), BF16), jax.ShapeDtypeStruct((t, D_MODEL), BF16),
                   jax.ShapeDtypeStruct((8, D_MODEL), F32), jax.ShapeDtypeStruct((8, D_MODEL), F32),
                   jax.ShapeDtypeStruct((8, D_MODEL), F32)],
        args=(y, gamma_f, target, x, gamma, g_act, u_act, wg_t, wu_t, wd), sem=("arbitrary",))


def _ffn_bwd_act(d, g_act, u_act, wd, name, hosted=()):
    t = d.shape[0]
    f = wd.shape[0]

    def body(d_ref, g_ref, u_ref, wd_ref, dg_ref, du_ref):
        db = (0.5 * d_ref[...]).astype(BF16)
        for j in range(f // FC):
            sl = slice(j * FC, (j + 1) * FC)
            da = _dot_nt(db, wd_ref[sl, :])
            g = g_ref[:, sl].astype(F32)
            u = u_ref[:, sl].astype(F32)
            s = jax.nn.sigmoid(g)
            dg_ref[:, sl] = (da * u * (s * (1.0 + g * (1.0 - s)))).astype(BF16)
            du_ref[:, sl] = (da * (g * s)).astype(BF16)

    return _call(
        body,
        name=name,
        grid=(t // TM,),
        in_specs=[_rows(TM, D_MODEL), _rows(TM, f), _rows(TM, f), _resident((f, D_MODEL))],
        out_specs=[_rows(TM, f), _rows(TM, f)],
        out_shape=[jax.ShapeDtypeStruct((t, f), BF16), jax.ShapeDtypeStruct((t, f), BF16)],
        args=(d, g_act, u_act, wd), sem=("parallel",), hosted=hosted)


def _ffn_bwd_in(d, x, gamma, dg, du, wg_t, wu_t, name, hosted=()):
    t = x.shape[0]
    f = wg_t.shape[0]

    def body(d_ref, x_ref, gam_ref, dg_ref, du_ref, wg_ref, wu_ref, dx_ref, dgam_ref):
        dh = _dot_nn(dg_ref[...], wg_ref[...]) + _dot_nn(du_ref[...], wu_ref[...])
        xh, r = _rms(x_ref[...])
        dxn, dgam = _rms_bwd(dh, xh, r, gam_ref[...])
        dx_ref[...] = d_ref[...] + dxn

        @pl.when(pl.program_id(0) == 0)
        def _():
            dgam_ref[...] = jnp.zeros_like(dgam_ref)

        dgam_ref[...] += dgam

    return _call(
        body,
        name=name,
        grid=(t // TM,),
        in_specs=[_rows(TM, D_MODEL), _rows(TM, D_MODEL), _resident((1, D_MODEL)), _rows(TM, f), _rows(TM, f),
                  _resident((f, D_MODEL)), _resident((f, D_MODEL))],
        out_specs=[_rows(TM, D_MODEL), pl.BlockSpec((8, D_MODEL), lambda i: (0, 0))],
        out_shape=[jax.ShapeDtypeStruct((t, D_MODEL), F32), jax.ShapeDtypeStruct((8, D_MODEL), F32)],
        args=(d, x, gamma, dg, du, wg_t, wu_t), sem=("arbitrary",), hosted=hosted)


def _mm_tn(pieces, b, name, tile=256, hosted=()):
    t, n = b.shape
    npc = len(pieces)
    counts = [p.shape[1] // tile for p in pieces]
    los = [sum(counts[:k]) for k in range(npc)]
    total = sum(counts)

    def body(*refs):
        a_refs, b_ref, o_ref = refs[:npc], refs[npc], refs[npc + 1]
        i = pl.program_id(0)
        for k in range(npc):
            @pl.when(jnp.logical_and(i >= los[k], i < los[k] + counts[k]))
            def _(k=k):
                o_ref[...] = _dot_tn(a_refs[k][...], b_ref[...]).astype(BF16)

    def a_spec(k):
        return pl.BlockSpec((t, tile), lambda i: (0, jnp.clip(i - los[k], 0, counts[k] - 1)))

    return _call(
        body,
        name=name,
        grid=(total,),
        in_specs=[a_spec(k) for k in range(npc)] + [_resident((t, n))],
        out_specs=[pl.BlockSpec((tile, n), lambda i: (i, 0))],
        out_shape=[jax.ShapeDtypeStruct((total * tile, n), BF16)],
        args=(*pieces, b), sem=("parallel",), hosted=hosted)[0]


def _proj_fwd(x, gamma, win_t, hosted=()):
    t = x.shape[0]

    def body(x_ref, gam_ref, w_ref, h_ref, qa_ref, qb_ref, gt_ref):
        xh, _ = _rms(x_ref[...])
        h = (xh * gam_ref[...]).astype(BF16)
        h_ref[...] = h
        for j in range(QKV_A // FC):
            qa_ref[:, j * FC:(j + 1) * FC] = _dot_nt(h, w_ref[j * FC:(j + 1) * FC, :]).astype(BF16)
        for j in range(QKV_B // FC):
            lo = QKV_A + j * FC
            qb_ref[:, j * FC:(j + 1) * FC] = _dot_nt(h, w_ref[lo:lo + FC, :]).astype(BF16)
        for j in range(2 * D_MODEL // FC):
            lo = QKV_A + QKV_B + j * FC
            gt_ref[:, j * FC:(j + 1) * FC] = _dot_nt(h, w_ref[lo:lo + FC, :])

    return _call(
        body,
        name="proj_fwd",
        grid=(t // TM,),
        in_specs=[_rows(TM, D_MODEL), _resident((1, D_MODEL)), _resident((IN_WIDTH, D_MODEL))],
        out_specs=[_rows(TM, D_MODEL), _rows(TM, QKV_A), _rows(TM, QKV_B), _rows(TM, 2 * D_MODEL)],
        out_shape=[jax.ShapeDtypeStruct((t, D_MODEL), BF16), jax.ShapeDtypeStruct((t, QKV_A), BF16),
                   jax.ShapeDtypeStruct((t, QKV_B), BF16), jax.ShapeDtypeStruct((t, 2 * D_MODEL), F32)],
        args=(x, gamma, win_t), sem=("parallel",), hosted=hosted)


def _proj_bwd(d, x, gamma, pieces, win_t, hosted=()):
    t = x.shape[0]
    npc = len(pieces)
    widths = [p.shape[1] for p in pieces]
    los = [sum(widths[:k]) for k in range(npc)]

    def body(*refs):
        d_ref, x_ref, gam_ref = refs[:3]
        p_refs = refs[3:3 + npc]
        w_ref, dx_ref, db_ref, dgam_ref = refs[3 + npc:]
        dh = _dot_nn(p_refs[0][...], w_ref[0:widths[0], :])
        for k in range(1, npc):
            dh += _dot_nn(p_refs[k][...], w_ref[los[k]:los[k] + widths[k], :])
        xh, r = _rms(x_ref[...])
        dxn, dgam = _rms_bwd(dh, xh, r, gam_ref[...])
        dx = d_ref[...] + dxn
        dx_ref[...] = dx
        db_ref[...] = (0.5 * dx).astype(BF16)

        @pl.when(pl.program_id(0) == 0)
        def _():
            dgam_ref[...] = jnp.zeros_like(dgam_ref)

        dgam_ref[...] += dgam

    return _call(
        body,
        name="proj_bwd",
        grid=(t // TM,),
        in_specs=[_rows(TM, D_MODEL), _rows(TM, D_MODEL), _resident((1, D_MODEL))] + [_rows(TM, w) for w in widths]
        + [_resident((IN_WIDTH, D_MODEL))],
        out_specs=[_rows(TM, D_MODEL), _rows(TM, D_MODEL), pl.BlockSpec((8, D_MODEL), lambda i: (0, 0))],
        out_shape=[jax.ShapeDtypeStruct((t, D_MODEL), F32), jax.ShapeDtypeStruct((t, D_MODEL), BF16),
                   jax.ShapeDtypeStruct((8, D_MODEL), F32)],
        args=(d, x, gamma, *pieces, win_t), sem=("arbitrary",), hosted=hosted)


def _lane_half(shape):
    return lax.broadcasted_iota(jnp.int32, shape, len(shape) - 1) // D_HEAD


def _band_weights(q, kk, bias, sink, qs, pad):
    s = _band_scores(q, kk, bias, qs, pad)
    m = jnp.max(s, axis=-1, keepdims=True)
    if sink is not None:
        m = jnp.maximum(m, sink)
    return jnp.exp(s - m), m


def _band_scores(q, kk, bias, qs, pad):
    s = _dot_nt(q, kk) + bias
    if qs is not None:
        col = lax.broadcasted_iota(jnp.int32, s.shape, 1)
        s = jnp.where(col + qs >= pad, s, NEG_INF)
    return s


def _weighted_values(p, vv_ones, sink, m):
    r = _dot_nn(p.astype(BF16), vv_ones)
    den = r[:, LANES:2 * LANES]
    if sink is not None:
        den = den + jnp.exp(sink - m)
    return r[:, 0:LANES] / den, m + jnp.log(den[:, 0:1])


def _fill_padded(dst, src, pad):
    dst[0:pad, :] = jnp.zeros((pad,) + dst.shape[1:], dst.dtype)
    dst[pad:, :] = src


FWD_PAIRS = 4
BWD_PAIRS = 4


def _attn_a_fwd(qkv, bias, hosted=()):
    bsz, s_len, _ = qkv.shape
    pad = A_PREV * CHUNK
    band = TQ + pad
    pp = FWD_PAIRS
    w = pp * LANES
    nb = A_WIDTH // w

    def body(q_ref, k_ref, v_ref, b_ref, o_ref, l_ref, kp, vp):
        i = pl.program_id(2)

        @pl.when(i == 0)
        def _():
            _fill_padded(kp, k_ref[...], pad)
            _fill_padded(vp, v_ref[...], pad)

        qs = pl.multiple_of(i * TQ, TQ)
        half = _lane_half((1, LANES))

        ones = jnp.ones((band, LANES), BF16)

        def block(masked):
            for pr in range(pp):
                sl = slice(pr * LANES, (pr + 1) * LANES)
                kk = kp[pl.ds(qs, band), sl]
                vv = jnp.concatenate([vp[pl.ds(qs, band), sl], ones], axis=1)
                q = q_ref[:, sl] * SCALE
                outs = []
                for j in range(2):
                    qm = jnp.where(half == j, q, jnp.zeros_like(q))
                    p, m = _band_weights(qm, kk, b_ref[2 * pr + j], None, qs if masked else None, pad)
                    o, lse = _weighted_values(p, vv, None, m)
                    outs.append(o)
                    l_ref[:, 2 * pr + j:2 * pr + j + 1] = lse
                o_ref[:, sl] = jnp.where(half == 0, outs[0], outs[1]).astype(BF16)

        pl.when(i < pad // TQ)(lambda: block(True))
        pl.when(i >= pad // TQ)(lambda: block(False))

    return _call(
        body,
        name="attn_a_fwd",
        grid=(bsz, nb, s_len // TQ),
        in_specs=[pl.BlockSpec((None, TQ, w), lambda b, g, i: (b, i, g)),
                  pl.BlockSpec((None, s_len, w), lambda b, g, i: (b, 0, nb + g)),
                  pl.BlockSpec((None, s_len, w), lambda b, g, i: (b, 0, 2 * nb + g)),
                  pl.BlockSpec((2 * pp, TQ, band), lambda b, g, i: (g, 0, 0))],
        out_specs=[pl.BlockSpec((None, TQ, w), lambda b, g, i: (b, i, g)),
                   pl.BlockSpec((None, TQ, 2 * pp), lambda b, g, i: (b, i, g))],
        out_shape=[jax.ShapeDtypeStruct((bsz, s_len, A_WIDTH), BF16),
                   jax.ShapeDtypeStruct((bsz, s_len, A_HEADS), F32)],
        scratch_shapes=[pltpu.VMEM((pad + s_len, w), BF16), pltpu.VMEM((pad + s_len, w), BF16)],
        args=(qkv, qkv, qkv, bias), sem=("arbitrary", "arbitrary", "arbitrary"), hosted=hosted)


def _attn_a_bwd(qkv, bias, do, lse, hosted=()):
    bsz, s_len, _ = qkv.shape
    pad = A_PREV * CHUNK
    band = TQ + pad
    n_i = s_len // TQ
    pp = BWD_PAIRS
    w = pp * LANES
    nb = A_WIDTH // w

    def body(q_ref, k_ref, v_ref, b_ref, do_ref, l_ref, dq_ref, dk_ref, dv_ref, dbias_ref, kp, vp, dk_acc, dv_acc):
        b = pl.program_id(1)
        i = pl.program_id(2)

        @pl.when(i == 0)
        def _():
            _fill_padded(kp, k_ref[...], pad)
            _fill_padded(vp, v_ref[...], pad)
            dk_acc[...] = jnp.zeros_like(dk_acc)
            dv_acc[...] = jnp.zeros_like(dv_acc)

        @pl.when(jnp.logical_and(b == 0, i == 0))
        def _():
            dbias_ref[...] = jnp.zeros_like(dbias_ref)

        qs = pl.multiple_of(i * TQ, TQ)
        half = _lane_half((1, LANES))

        def block(masked):
            for pr in range(pp):
                sl = slice(pr * LANES, (pr + 1) * LANES)
                kk = kp[pl.ds(qs, band), sl]
                vv = vp[pl.ds(qs, band), sl]
                q = q_ref[:, sl] * SCALE
                dd = do_ref[:, sl]
                dqs, dks, dvs = [], [], []
                for j in range(2):
                    hd = 2 * pr + j
                    qm = jnp.where(half == j, q, jnp.zeros_like(q))
                    dm = jnp.where(half == j, dd, jnp.zeros_like(dd))
                    s = _band_scores(qm, kk, b_ref[hd], qs if masked else None, pad)
                    pn = jnp.exp(s - l_ref[:, hd:hd + 1])
                    dp = _dot_nt(dm, vv)
                    delta = jnp.sum(pn * dp, axis=-1, keepdims=True)
                    ds = pn * (dp - delta)
                    dbias_ref[hd] += ds[:, band - REL_COLS:]
                    dsb = ds.astype(BF16)
                    dqs.append(_dot_nn(dsb, kk))
                    dks.append(_dot_tn(dsb, q))
                    dvs.append(_dot_tn(pn.astype(BF16), dd))
                dq_ref[:, sl] = (jnp.where(half == 0, dqs[0], dqs[1]) * SCALE).astype(BF16)
                dk_acc[pl.ds(qs, band), sl] += jnp.where(half == 0, dks[0], dks[1])
                dv_acc[pl.ds(qs, band), sl] += jnp.where(half == 0, dvs[0], dvs[1])

        pl.when(i < pad // TQ)(lambda: block(True))
        pl.when(i >= pad // TQ)(lambda: block(False))

        @pl.when(i == n_i - 1)
        def _():
            dk_ref[...] = dk_acc[pad:, :].astype(BF16)
            dv_ref[...] = dv_acc[pad:, :].astype(BF16)

    qspec = pl.BlockSpec((None, TQ, w), lambda g, b, i: (b, i, g))
    kvout = pl.BlockSpec((None, s_len, w), lambda g, b, i: (b, 0, g))
    wide = jax.ShapeDtypeStruct((bsz, s_len, A_WIDTH), BF16)
    return _call(
        body,
        name="attn_a_bwd",
        grid=(nb, bsz, n_i),
        in_specs=[qspec,
                  pl.BlockSpec((None, s_len, w), lambda g, b, i: (b, 0, nb + g)),
                  pl.BlockSpec((None, s_len, w), lambda g, b, i: (b, 0, 2 * nb + g)),
                  pl.BlockSpec((2 * pp, TQ, band), lambda g, b, i: (g, 0, 0)),
                  qspec,
                  pl.BlockSpec((None, TQ, 2 * pp), lambda g, b, i: (b, i, g))],
        out_specs=[qspec, kvout, kvout, pl.BlockSpec((2 * pp, TQ, REL_COLS), lambda g, b, i: (g, 0, 0))],
        out_shape=[wide, wide, wide, jax.ShapeDtypeStruct((A_HEADS, TQ, REL_COLS), F32)],
        scratch_shapes=[pltpu.VMEM((pad + s_len, w), BF16), pltpu.VMEM((pad + s_len, w), BF16),
                        pltpu.VMEM((pad + s_len, w), F32), pltpu.VMEM((pad + s_len, w), F32)],
        args=(qkv, qkv, qkv, bias, do, lse), sem=("arbitrary", "arbitrary", "arbitrary"), hosted=hosted)


def _fill_padded_dup(dst, src, pad, h, half):
    other = pltpu.roll(src, D_HEAD, 1)
    _fill_padded(dst, jnp.where(half == h, src, other), pad)


def _attn_b_fwd(qkv, bias, sink):
    bsz, s_len, _ = qkv.shape
    pad = B_PREV * CHUNK
    band = TQ + pad
    kcol = B_Q_WIDTH // LANES
    npair = B_Q_HEADS // 2

    def body(q_ref, k_ref, v_ref, b_ref, s_ref, o_ref, l_ref, kp, vp):
        i = pl.program_id(1)
        half = _lane_half((1, LANES))

        @pl.when(i == 0)
        def _():
            for h in range(B_KV_HEADS):
                _fill_padded_dup(kp.at[h], k_ref[...], pad, h, half)
                _fill_padded_dup(vp.at[h], v_ref[...], pad, h, half)

        qs = pl.multiple_of(i * TQ, TQ)

        ones = jnp.ones((band, LANES), BF16)

        def block(masked):
            for pr in range(npair):
                h = pr // (B_GROUP // 2)
                sl = slice(pr * LANES, (pr + 1) * LANES)
                kk = kp[h, pl.ds(qs, band), :]
                vv = jnp.concatenate([vp[h, pl.ds(qs, band), :], ones], axis=1)
                q = q_ref[:, sl] * SCALE
                outs = []
                for j in range(2):
                    qm = jnp.where(half == j, q, jnp.zeros_like(q))
                    sink = s_ref[2 * pr + j][0:1, 0:1]
                    p, m = _band_weights(qm, kk, b_ref[2 * pr + j], sink, qs if masked else None, pad)
                    o, lse = _weighted_values(p, vv, sink, m)
                    outs.append(o)
                    l_ref[:, 2 * pr + j:2 * pr + j + 1] = lse
                o_ref[:, sl] = jnp.where(half == 0, outs[0], outs[1]).astype(BF16)

        pl.when(i < -(-pad // TQ))(lambda: block(True))
        pl.when(i >= -(-pad // TQ))(lambda: block(False))

    return pl.pallas_call(
        body,
        name="attn_b_fwd",
        grid=(bsz, s_len // TQ),
        in_specs=[pl.BlockSpec((None, TQ, B_Q_WIDTH), lambda b, i: (b, i, 0)),
                  pl.BlockSpec((None, s_len, LANES), lambda b, i: (b, 0, kcol)),
                  pl.BlockSpec((None, s_len, LANES), lambda b, i: (b, 0, kcol + 1)),
                  pl.BlockSpec((B_Q_HEADS, TQ, band), lambda b, i: (0, 0, 0)),
                  pl.BlockSpec((B_Q_HEADS, 8, LANES), lambda b, i: (0, 0, 0))],
        out_specs=[pl.BlockSpec((None, TQ, B_Q_WIDTH), lambda b, i: (b, i, 0)),
                   pl.BlockSpec((None, TQ, B_Q_HEADS), lambda b, i: (b, i, 0))],
        out_shape=[jax.ShapeDtypeStruct((bsz, s_len, B_Q_WIDTH), BF16),
                   jax.ShapeDtypeStruct((bsz, s_len, B_Q_HEADS), F32)],
        scratch_shapes=[pltpu.VMEM((B_KV_HEADS, pad + s_len, LANES), BF16),
                        pltpu.VMEM((B_KV_HEADS, pad + s_len, LANES), BF16)],
        compiler_params=_cparams(("arbitrary", "arbitrary")),
    )(qkv, qkv, qkv, bias, sink)


def _attn_b_bwd(qkv, bias, sink, do, lse, hosted=()):
    bsz, s_len, _ = qkv.shape
    pad = B_PREV * CHUNK
    band = TQ + pad
    kcol = B_Q_WIDTH // LANES
    n_i = s_len // TQ
    pp = B_GROUP // 2

    def body(q_ref, k_ref, v_ref, b_ref, s_ref, do_ref, l_ref, dq_ref, dkv_ref, dsink_ref, kp, vp, dk_acc, dv_acc):
        b = pl.program_id(0)
        i = pl.program_id(1)
        half = _lane_half((1, LANES))

        @pl.when(i == 0)
        def _():
            for h in range(B_KV_HEADS):
                _fill_padded_dup(kp.at[h], k_ref[...], pad, h, half)
                _fill_padded_dup(vp.at[h], v_ref[...], pad, h, half)
            dk_acc[...] = jnp.zeros_like(dk_acc)
            dv_acc[...] = jnp.zeros_like(dv_acc)

        @pl.when(jnp.logical_and(b == 0, i == 0))
        def _():
            dsink_ref[...] = jnp.zeros_like(dsink_ref)

        qs = pl.multiple_of(i * TQ, TQ)

        def block(masked):
            heads_dk, heads_dv = [], []
            for h in range(B_KV_HEADS):
                kk = kp[h, pl.ds(qs, band), :]
                vv = vp[h, pl.ds(qs, band), :]
                dk2 = jnp.zeros((band, LANES), F32)
                dv2 = jnp.zeros((band, LANES), F32)
                for pr in range(pp * h, pp * (h + 1)):
                    sl = slice(pr * LANES, (pr + 1) * LANES)
                    q = q_ref[:, sl] * SCALE
                    dd = do_ref[:, sl]
                    dqs, dks, dvs = [], [], []
                    for j in range(2):
                        qm = jnp.where(half == j, q, jnp.zeros_like(q))
                        dm = jnp.where(half == j, dd, jnp.zeros_like(dd))
                        hd = 2 * pr + j
                        lse = l_ref[:, hd:hd + 1]
                        s = _band_scores(qm, kk, b_ref[hd], qs if masked else None, pad)
                        pn = jnp.exp(s - lse)
                        dp = _dot_nt(dm, vv)
                        delta = jnp.sum(pn * dp, axis=-1, keepdims=True)
                        ds = pn * (dp - delta)
                        dsb = ds.astype(BF16)
                        dqs.append(_dot_nn(dsb, kk))
                        dks.append(_dot_tn(dsb, q))
                        dvs.append(_dot_tn(pn.astype(BF16), dd))
                        sink = s_ref[hd][0:1, 0:1]
                        dsk = jnp.sum(-jnp.exp(sink - lse) * delta, axis=0, keepdims=True)
                        dsink_ref[hd] += jnp.broadcast_to(dsk, (8, LANES))
                    dq_ref[:, sl] = (jnp.where(half == 0, dqs[0], dqs[1]) * SCALE).astype(BF16)
                    dk2 = dk2 + jnp.where(half == 0, dks[0], dks[1])
                    dv2 = dv2 + jnp.where(half == 0, dvs[0], dvs[1])
                heads_dk.append(dk2 + pltpu.roll(dk2, D_HEAD, 1))
                heads_dv.append(dv2 + pltpu.roll(dv2, D_HEAD, 1))
            dk_acc[pl.ds(qs, band), :] += jnp.where(half == 0, heads_dk[0], heads_dk[1])
            dv_acc[pl.ds(qs, band), :] += jnp.where(half == 0, heads_dv[0], heads_dv[1])

        pl.when(i < -(-pad // TQ))(lambda: block(True))
        pl.when(i >= -(-pad // TQ))(lambda: block(False))

        @pl.when(i == n_i - 1)
        def _():
            dkv_ref[:, 0:LANES] = dk_acc[pad:, :].astype(BF16)
            dkv_ref[:, LANES:2 * LANES] = dv_acc[pad:, :].astype(BF16)

    qspec = pl.BlockSpec((None, TQ, B_Q_WIDTH), lambda b, i: (b, i, 0))
    return _call(
        body,
        name="attn_b_bwd",
        grid=(bsz, n_i),
        in_specs=[qspec,
                  pl.BlockSpec((None, s_len, LANES), lambda b, i: (b, 0, kcol)),
                  pl.BlockSpec((None, s_len, LANES), lambda b, i: (b, 0, kcol + 1)),
                  pl.BlockSpec((B_Q_HEADS, TQ, band), lambda b, i: (0, 0, 0)),
                  pl.BlockSpec((B_Q_HEADS, 8, LANES), lambda b, i: (0, 0, 0)),
                  qspec,
                  pl.BlockSpec((None, TQ, B_Q_HEADS), lambda b, i: (b, i, 0))],
        out_specs=[qspec, pl.BlockSpec((None, s_len, 2 * LANES), lambda b, i: (b, 0, 0)),
                   pl.BlockSpec((B_Q_HEADS, 8, LANES), lambda b, i: (0, 0, 0))],
        out_shape=[jax.ShapeDtypeStruct((bsz, s_len, B_Q_WIDTH), BF16),
                   jax.ShapeDtypeStruct((bsz, s_len, 2 * B_KV_WIDTH), BF16),
                   jax.ShapeDtypeStruct((B_Q_HEADS, 8, LANES), F32)],
        scratch_shapes=[pltpu.VMEM((B_KV_HEADS, pad + s_len, LANES), BF16),
                        pltpu.VMEM((B_KV_HEADS, pad + s_len, LANES), BF16),
                        pltpu.VMEM((pad + s_len, LANES), F32), pltpu.VMEM((pad + s_len, LANES), F32)],
        args=(qkv, qkv, qkv, bias, sink, do, lse), sem=("arbitrary", "arbitrary"), hosted=hosted)


REL_COLS = 3 * 128
REL_WRAP = 512


def _bias_a_build(tv, hosted=()):
    h = tv.shape[0]
    pad = A_PREV * CHUNK
    band = TQ + pad

    def body(tv_ref, o_ref):
        row = tv_ref[...]
        x = jnp.broadcast_to(row, (TQ, REL_WRAP))
        r = lax.broadcasted_iota(jnp.int32, x.shape, 0)
        for bit in range(8):
            sh = 1 << bit
            x = jnp.where((r & sh) != 0, pltpu.roll(x, sh, 1), x)
        far = jnp.broadcast_to(row[:, 0:1], (TQ, band - REL_COLS))
        full = jnp.concatenate([far, x[:, REL_WRAP // 2:REL_WRAP], x[:, 0:REL_COLS - REL_WRAP // 2]], axis=1)
        qc = (lax.broadcasted_iota(jnp.int32, full.shape, 0) + pad) // CHUNK
        kc = lax.broadcasted_iota(jnp.int32, full.shape, 1) // CHUNK
        ok = jnp.logical_and(kc <= qc, kc >= qc - A_PREV)
        o_ref[...] = jnp.where(ok, full, NEG_INF)

    return _call(
        body,
        name="bias_a_build",
        grid=(h,),
        in_specs=[pl.BlockSpec((None, 1, REL_WRAP), lambda hh: (hh, 0, 0))],
        out_specs=[pl.BlockSpec((None, TQ, band), lambda hh: (hh, 0, 0))],
        out_shape=[jax.ShapeDtypeStruct((h, TQ, band), F32)],
        args=(tv,), sem=("parallel",), hosted=hosted)[0]


def _relbias_grad(dbias, hosted=()):
    h, rows, _ = dbias.shape

    def body(d_ref, o_ref):
        x = d_ref[...]
        r = lax.broadcasted_iota(jnp.int32, x.shape, 0)
        c = lax.broadcasted_iota(jnp.int32, x.shape, 1) - r
        x = jnp.where(jnp.logical_and(c >= 1, c < REL_TABLE), x, 0.0)
        for bit in range(8):
            sh = 1 << bit
            x = jnp.where((r & sh) != 0, pltpu.roll(x, REL_COLS - sh, 1), x)
        diag = jnp.sum(x, axis=0, keepdims=True)
        lane = lax.broadcasted_iota(jnp.int32, diag.shape, 1)
        diag = jnp.where(jnp.logical_and(lane >= 1, lane < REL_TABLE), diag, 0.0)
        rest = -jnp.sum(diag, axis=1, keepdims=True)
        o_ref[...] = jnp.broadcast_to(jnp.where(lane == 0, rest, diag), o_ref.shape)

    return _call(
        body,
        name="relbias_grad",
        grid=(h,),
        in_specs=[pl.BlockSpec((None, rows, REL_COLS), lambda hh: (hh, 0, 0))],
        out_specs=[pl.BlockSpec((None, 8, REL_COLS), lambda hh: (hh, 0, 0))],
        out_shape=[jax.ShapeDtypeStruct((h, 8, REL_COLS), F32)],
        args=(dbias,), sem=("parallel",), hosted=hosted)[0]


def _mix_out_fwd(x, oa, ob, gates, proj_t, wout):
    t = x.shape[0]

    def body(x_ref, oa_ref, ob_ref, gt_ref, pt_ref, wo_ref, y_ref, ya_ref, yb_ref, mg_ref):
        ya = _dot_nt(oa_ref[...], pt_ref[:, 0:A_WIDTH])
        yb = _dot_nt(ob_ref[...], pt_ref[:, A_WIDTH:A_WIDTH + B_Q_WIDTH])
        ya_ref[...] = ya.astype(BF16)
        yb_ref[...] = yb.astype(BF16)
        mg = jax.nn.sigmoid(gt_ref[:, 0:D_MODEL]) * ya + jax.nn.sigmoid(gt_ref[:, D_MODEL:2 * D_MODEL]) * yb
        mgb = mg.astype(BF16)
        mg_ref[...] = mgb
        y_ref[...] = x_ref[...] + _dot_nn(mgb, wo_ref[...])

    return pl.pallas_call(
        body,
        name="mix_out_fwd",
        grid=(t // TM,),
        in_specs=[_rows(TM, D_MODEL), _rows(TM, A_WIDTH), _rows(TM, B_Q_WIDTH), _rows(TM, 2 * D_MODEL),
                  _resident((D_MODEL, A_WIDTH + B_Q_WIDTH)), _resident((D_MODEL, D_MODEL))],
        out_specs=[_rows(TM, D_MODEL), _rows(TM, D_MODEL), _rows(TM, D_MODEL), _rows(TM, D_MODEL)],
        out_shape=[jax.ShapeDtypeStruct((t, D_MODEL), F32), jax.ShapeDtypeStruct((t, D_MODEL), BF16),
                   jax.ShapeDtypeStruct((t, D_MODEL), BF16), jax.ShapeDtypeStruct((t, D_MODEL), BF16)],
        compiler_params=_cparams(("parallel",)),
    )(x, oa, ob, gates, proj_t, wout)


def _mix_out_bwd(d, gates, ya, yb, mg, oa, ob, proj_t, wout, hosted=()):
    t = d.shape[0]
    nt = t // TM

    def body(d_ref, gt_ref, ya_ref, yb_ref, mg_ref, oa_ref, ob_ref, pt_ref, wo_ref,
             doa_ref, dob_ref, dgt_ref, gwo_ref, gwp_ref, acc_o, acc_p):
        i = pl.program_id(0)
        db = d_ref[...].astype(BF16)
        dmg = _dot_nt(db, wo_ref[...])
        sa = jax.nn.sigmoid(gt_ref[:, 0:D_MODEL])
        sb = jax.nn.sigmoid(gt_ref[:, D_MODEL:2 * D_MODEL])
        dya = (dmg * sa).astype(BF16)
        dyb = (dmg * sb).astype(BF16)
        dgt_ref[:, 0:D_MODEL] = (dmg * ya_ref[...].astype(F32) * (sa * (1.0 - sa))).astype(BF16)
        dgt_ref[:, D_MODEL:2 * D_MODEL] = (dmg * yb_ref[...].astype(F32) * (sb * (1.0 - sb))).astype(BF16)
        doa_ref[...] = _dot_nn(dya, pt_ref[:, 0:A_WIDTH]).astype(BF16)
        dob_ref[...] = _dot_nn(dyb, pt_ref[:, A_WIDTH:A_WIDTH + B_Q_WIDTH]).astype(BF16)

        @pl.when(i == 0)
        def _():
            acc_o[...] = jnp.zeros_like(acc_o)
            acc_p[...] = jnp.zeros_like(acc_p)

        acc_o[...] += _dot_tn(mg_ref[...], db)
        acc_p[:, 0:A_WIDTH] += _dot_tn(dya, oa_ref[...])
        acc_p[:, A_WIDTH:A_WIDTH + B_Q_WIDTH] += _dot_tn(dyb, ob_ref[...])

        @pl.when(i == nt - 1)
        def _():
            gwo_ref[...] = acc_o[...].astype(BF16)
            gwp_ref[...] = acc_p[...].astype(BF16)

    whole = pl.BlockSpec((D_MODEL, D_MODEL), lambda i: (0, 0))
    return _call(
        body,
        name="mix_out_bwd",
        grid=(nt,),
        in_specs=[_rows(TM, D_MODEL), _rows(TM, 2 * D_MODEL), _rows(TM, D_MODEL), _rows(TM, D_MODEL),
                  _rows(TM, D_MODEL), _rows(TM, A_WIDTH), _rows(TM, B_Q_WIDTH),
                  _resident((D_MODEL, A_WIDTH + B_Q_WIDTH)), _resident((D_MODEL, D_MODEL))],
        out_specs=[_rows(TM, A_WIDTH), _rows(TM, B_Q_WIDTH), _rows(TM, 2 * D_MODEL), whole, whole],
        out_shape=[jax.ShapeDtypeStruct((t, A_WIDTH), BF16), jax.ShapeDtypeStruct((t, B_Q_WIDTH), BF16),
                   jax.ShapeDtypeStruct((t, 2 * D_MODEL), BF16), jax.ShapeDtypeStruct((D_MODEL, D_MODEL), BF16),
                   jax.ShapeDtypeStruct((D_MODEL, D_MODEL), BF16)],
        scratch_shapes=[pltpu.VMEM((D_MODEL, D_MODEL), F32), pltpu.VMEM((D_MODEL, A_WIDTH + B_Q_WIDTH), F32)],
        args=(d, gates, ya, yb, mg, oa, ob, proj_t, wout), sem=("arbitrary",), hosted=hosted)


def _place():
    x, y, c = lax.axis_index("x"), lax.axis_index("y"), lax.axis_index("c")
    chips = [(1 - x, y), (x, 1 - y), (1 - x, 1 - y)]
    return x, y, c, chips


class _Gather:
    per = 8

    def __init__(self, shards):
        n = len(shards)
        self.inputs = list(shards)
        self.out_shape = [jax.ShapeDtypeStruct((N_DEV * s.shape[0], s.shape[1]), s.dtype) for s in shards]
        self.scratch = [pltpu.SemaphoreType.DMA((n * self.per,)), pltpu.SemaphoreType.DMA((n * self.per,)),
                        pltpu.SemaphoreType.DMA((n,))]
        self.result = None

    def _parts(self, ins, outs, sems):
        send_sems, recv_sems, local_sems = sems
        x, y, c, chips = _place()
        me, sibling = (x, y, c), (x, y, 1 - c)
        xn, yn, dg = chips
        n = len(ins)

        def rows(k, p, part=None):
            r = ins[k].shape[0]
            base = (4 * p[0] + 2 * p[1] + p[2]) * r
            if part is None:
                return outs[k].at[pl.ds(base, r), :]
            return outs[k].at[pl.ds(base + part * (r // 2), r // 2), :]

        def copy(k, slot, block, to, src=None, part=None):
            return pltpu.make_async_remote_copy(
                src_ref=rows(k, block, part) if src is None else src, dst_ref=rows(k, block, part),
                send_sem=send_sems.at[k * self.per + slot], recv_sem=recv_sems.at[k * self.per + slot],
                device_id=to, device_id_type=MESH)

        mine = [pltpu.make_async_copy(ins[k], rows(k, me), local_sems.at[k]) for k in range(n)]
        sends, lands = [], []
        for k in range(n):
            sends.append({
                0: copy(k, 0, me, sibling, src=ins[k]),
                1: copy(k, 1, me, (*xn, c), src=ins[k]),
                2: copy(k, 2, me, (*yn, c), src=ins[k]),
                3: copy(k, 3, (*xn, c), (*yn, c), part=0),
                4: copy(k, 4, (*yn, c), (*xn, c), part=1),
                5: copy(k, 5, (*xn, c), sibling),
                6: copy(k, 6, (*yn, c), sibling),
                7: copy(k, 7, (*dg, c), sibling)})
            lands.append({
                0: copy(k, 0, sibling, me),
                1: copy(k, 1, (*xn, c), me),
                2: copy(k, 2, (*yn, c), me),
                3: copy(k, 3, (*dg, c), me, part=0),
                4: copy(k, 4, (*dg, c), me, part=1),
                5: copy(k, 5, (*xn, 1 - c), me),
                6: copy(k, 6, (*yn, 1 - c), me),
                7: copy(k, 7, (*dg, 1 - c), me)})
        return n, mine, sends, lands

    def start(self, ins, outs, sems):
        n, mine, sends, _ = self._parts(ins, outs, sems)
        for cp in mine:
            cp.start()
        for slot in (0, 1, 2):
            for k in range(n):
                sends[k][slot].start()

    def relay(self, ins, outs, sems):
        n, _, sends, lands = self._parts(ins, outs, sems)
        for k in range(n):
            lands[k][1].wait_recv()
            sends[k][3].start()
            sends[k][5].start()
        for k in range(n):
            lands[k][2].wait_recv()
            sends[k][4].start()
            sends[k][6].start()

    def forward(self, ins, outs, sems):
        n, _, sends, lands = self._parts(ins, outs, sems)
        for k in range(n):
            lands[k][3].wait_recv()
            lands[k][4].wait_recv()
            sends[k][7].start()

    def finish(self, ins, outs, sems):
        n, mine, sends, lands = self._parts(ins, outs, sems)
        for k in range(n):
            for slot in (0, 5, 6, 7):
                lands[k][slot].wait_recv()
        for k in range(n):
            for slot in range(self.per):
                sends[k][slot].wait_send()
        for cp in mine:
            cp.wait()


class _PairExchange:
    def __init__(self, grads):
        n = len(grads)
        self.inputs = list(grads)
        self.out_shape = [jax.ShapeDtypeStruct((g.shape[0] // 2, g.shape[1]), g.dtype) for g in grads]
        self.scratch = [pltpu.SemaphoreType.DMA((n * N_CHIP,)), pltpu.SemaphoreType.DMA((n * N_CHIP,))]
        self.result = None

    def _copies(self, ins, outs, sems):
        send_sems, recv_sems = sems
        x, y, c, _ = _place()
        copies = []
        for k in range(len(ins)):
            r = ins[k].shape[0] // N_DEV
            for q in range(N_CHIP):
                copies.append(pltpu.make_async_remote_copy(
                    src_ref=ins[k].at[pl.ds((2 * q + 1 - c) * r, r), :], dst_ref=outs[k].at[pl.ds(q * r, r), :],
                    send_sem=send_sems.at[k * N_CHIP + q], recv_sem=recv_sems.at[k * N_CHIP + q],
                    device_id=(x, y, 1 - c), device_id_type=MESH))
        return copies

    def start(self, ins, outs, sems):
        for cp in self._copies(ins, outs, sems):
            cp.start()

    def relay(self, ins, outs, sems):
        pass

    def forward(self, ins, outs, sems):
        pass

    def finish(self, ins, outs, sems):
        copies = self._copies(ins, outs, sems)
        for cp in copies:
            cp.wait_recv()
        for cp in copies:
            cp.wait_send()


class _ChipExchange(_PairExchange):
    def __init__(self, psums):
        n = len(psums)
        self.inputs = list(psums)
        self.out_shape = [jax.ShapeDtypeStruct((3 * p.shape[0] // N_CHIP, p.shape[1]), p.dtype) for p in psums]
        self.scratch = [pltpu.SemaphoreType.DMA((n * 3,)), pltpu.SemaphoreType.DMA((n * 3,))]
        self.result = None

    def _copies(self, ins, outs, sems):
        send_sems, recv_sems = sems
        _, _, c, chips = _place()
        copies = []
        for k in range(len(ins)):
            r = ins[k].shape[0] // N_CHIP
            for j, chip in enumerate(chips):
                copies.append(pltpu.make_async_remote_copy(
                    src_ref=ins[k].at[pl.ds((2 * chip[0] + chip[1]) * r, r), :], dst_ref=outs[k].at[pl.ds(j * r, r), :],
                    send_sem=send_sems.at[k * 3 + j], recv_sem=recv_sems.at[k * 3 + j],
                    device_id=(*chip, c), device_id_type=MESH))
        return copies


def _exchange_alone(xchg, name):
    n_in, n_out = len(xchg.inputs), len(xchg.out_shape)

    def body(*refs):
        ins, outs, sems = refs[:n_in], refs[n_in:n_in + n_out], refs[n_in + n_out:]
        xchg.start(ins, outs, sems)
        xchg.relay(ins, outs, sems)
        xchg.forward(ins, outs, sems)
        xchg.finish(ins, outs, sems)

    xchg.result = list(pl.pallas_call(
        body, name=name, in_specs=[_hbm()] * n_in, out_specs=[_hbm()] * n_out, out_shape=xchg.out_shape,
        scratch_shapes=xchg.scratch)(*xchg.inputs))
    return xchg.result


def _pair_sum(core, grads, recvd, name):
    n = len(grads)
    r = grads[0].shape[0] // N_DEV
    cdim = grads[0].shape[1]
    tr = r // 2 if r % 32 == 0 else r
    nt = r // tr

    def body(core_ref, *refs):
        del core_ref
        for k in range(n):
            refs[2 * n + k][...] = (refs[k][...].astype(F32) + refs[n + k][...].astype(F32)).astype(BF16)

    gspec = pl.BlockSpec((tr, cdim), lambda q, i, core_ref: ((2 * q + core_ref[0]) * nt + i, 0))
    rspec = pl.BlockSpec((tr, cdim), lambda q, i, core_ref: (q * nt + i, 0))
    return pl.pallas_call(
        body,
        name=name,
        grid_spec=pltpu.PrefetchScalarGridSpec(
            num_scalar_prefetch=1, grid=(N_CHIP, nt), in_specs=[gspec] * n + [rspec] * n, out_specs=[rspec] * n),
        out_shape=[jax.ShapeDtypeStruct((N_CHIP * r, cdim), BF16) for _ in range(n)],
        compiler_params=_cparams(("parallel", "parallel")),
    )(core, *grads, *recvd)


def _final_sum(chip, psums, recvd, name):
    n = len(psums)
    r = psums[0].shape[0] // N_CHIP
    cdim = psums[0].shape[1]
    tr = r // 2 if r % 32 == 0 else r
    nt = r // tr

    def body(chip_ref, *refs):
        del chip_ref
        for k in range(n):
            got = refs[n + k]
            tot = refs[k][...].astype(F32) + got[0].astype(F32)
            tot = tot + got[1].astype(F32)
            tot = tot + got[2].astype(F32)
            refs[2 * n + k][...] = tot

    pspec = pl.BlockSpec((tr, cdim), lambda i, chip_ref: (chip_ref[0] * nt + i, 0))
    rspec = pl.BlockSpec((3, tr, cdim), lambda i, chip_ref: (0, i, 0))
    ospec = pl.BlockSpec((tr, cdim), lambda i, chip_ref: (i, 0))
    return pl.pallas_call(
        body,
        name=name,
        grid_spec=pltpu.PrefetchScalarGridSpec(
            num_scalar_prefetch=1, grid=(nt,), in_specs=[pspec] * n + [rspec] * n, out_specs=[ospec] * n),
        out_shape=[jax.ShapeDtypeStruct((r, cdim), F32) for _ in range(n)],
        compiler_params=_cparams(("parallel",)),
    )(chip, *psums, *[g.reshape(3, r, cdim) for g in recvd])


SMALL_ROWS = 16


def _all_reduce_small(part):
    def body(p_ref, o_ref, buf, send_sems, recv_sems):
        x, y, c, _ = _place()
        me = 4 * x + 2 * y + c
        buf[me] = p_ref[...]
        copies = []
        for d in range(1, N_DEV):
            peer = me ^ d
            copies.append(pltpu.make_async_remote_copy(
                src_ref=p_ref, dst_ref=buf.at[me], send_sem=send_sems.at[d - 1], recv_sem=recv_sems.at[d - 1],
                device_id=(peer // 4, (peer // 2) % 2, peer % 2), device_id_type=MESH))
        for cp in copies:
            cp.start()
        for cp in copies:
            cp.wait_recv()
        for cp in copies:
            cp.wait_send()
        tot = buf[0]
        for d in range(1, N_DEV):
            tot = tot + buf[d]
        o_ref[...] = tot

    return pl.pallas_call(
        body,
        name="all_reduce_small",
        in_specs=[pl.BlockSpec(memory_space=pltpu.VMEM)],
        out_specs=pl.BlockSpec(memory_space=pltpu.VMEM),
        out_shape=jax.ShapeDtypeStruct(part.shape, F32),
        scratch_shapes=[pltpu.VMEM((N_DEV,) + part.shape, F32), pltpu.SemaphoreType.DMA((N_DEV - 1,)),
                        pltpu.SemaphoreType.DMA((N_DEV - 1,))],
    )(part)


ADAMW_STEPS = 4


def _adamw(ws, gs, ms, vs, name, hosted=()):
    n = len(ws)
    steps = ADAMW_STEPS if all(w.shape[0] % (8 * ADAMW_STEPS) == 0 for w in ws) else 1
    c1 = 1.0 - ADAM_B1 ** ADAM_STEP
    c2 = 1.0 - ADAM_B2 ** ADAM_STEP

    def body(*refs):
        for k in range(n):
            w, g, m, v = (refs[j * n + k][...] for j in range(4))
            m2 = ADAM_B1 * m + (1.0 - ADAM_B1) * g
            v2 = ADAM_B2 * v + (1.0 - ADAM_B2) * (g * g)
            delta = -ADAM_LR * ((m2 * (1.0 / c1)) / (jnp.sqrt(v2 * (1.0 / c2)) + ADAM_EPS) + ADAM_WD * w)
            refs[4 * n + k][...] = delta
            refs[5 * n + k][...] = m2
            refs[6 * n + k][...] = v2

    specs = [pl.BlockSpec((w.shape[0] // steps, w.shape[1]), lambda i: (i, 0)) for w in ws]
    shapes = [jax.ShapeDtypeStruct(w.shape, F32) for w in ws]
    outs = _call(
        body,
        name=name,
        grid=(steps,),
        in_specs=specs * 4,
        out_specs=specs * 3,
        out_shape=shapes * 3,
        args=(*ws, *gs, *ms, *vs), sem=("parallel",), hosted=hosted)
    return outs[:n], outs[n:2 * n], outs[2 * n:]


def _adamw_reduced(chip, ws, psums, recvd, ms, vs, steps, name):
    n = len(ws)
    c1 = 1.0 - ADAM_B1 ** ADAM_STEP
    c2 = 1.0 - ADAM_B2 ** ADAM_STEP

    def body(chip_ref, *refs):
        del chip_ref
        for k in range(n):
            w, m, v = (refs[j * n + k][...] for j in (0, 3, 4))
            got = refs[2 * n + k]
            g = refs[n + k][...].astype(F32) + got[0].astype(F32)
            g = g + got[1].astype(F32)
            g = g + got[2].astype(F32)
            m2 = ADAM_B1 * m + (1.0 - ADAM_B1) * g
            v2 = ADAM_B2 * v + (1.0 - ADAM_B2) * (g * g)
            refs[5 * n + k][...] = g
            refs[6 * n + k][...] = -ADAM_LR * (
                (m2 * (1.0 / c1)) / (jnp.sqrt(v2 * (1.0 / c2)) + ADAM_EPS) + ADAM_WD * w)
            refs[7 * n + k][...] = m2
            refs[8 * n + k][...] = v2

    def blk(w):
        return (w.shape[0] // steps, w.shape[1])

    own = [pl.BlockSpec(blk(w), lambda i, chip_ref: (i, 0)) for w in ws]
    psum = [pl.BlockSpec(blk(w), lambda i, chip_ref: (chip_ref[0] * steps + i, 0)) for w in ws]
    recv = [pl.BlockSpec((3,) + blk(w), lambda i, chip_ref: (0, i, 0)) for w in ws]
    shapes = [jax.ShapeDtypeStruct(w.shape, F32) for w in ws]
    outs = pl.pallas_call(
        body,
        name=name,
        grid_spec=pltpu.PrefetchScalarGridSpec(
            num_scalar_prefetch=1, grid=(steps,), in_specs=own + psum + recv + own + own, out_specs=own * 4),
        out_shape=shapes * 4,
        compiler_params=_cparams(("parallel",)),
    )(chip, *ws, *psums, *[r.reshape((3,) + w.shape) for r, w in zip(recvd, ws)], *ms, *vs)
    return outs[:n], outs[n:2 * n], outs[2 * n:3 * n], outs[3 * n:]


def _bias_b():
    pad = B_PREV * CHUNK
    slopes = np.array([2.0 ** (-8.0 * (i + 1) / B_Q_HEADS) for i in range(B_Q_HEADS)], dtype=np.float32)
    dist = np.abs(np.arange(TQ)[:, None] - np.arange(TQ + pad)[None, :] + pad).astype(np.float32)
    bias = -slopes.reshape(B_Q_HEADS, 1, 1) * dist[None]
    qc = (np.arange(TQ)[:, None] + pad) // CHUNK
    kc = np.arange(TQ + pad)[None, :] // CHUNK
    allowed = (kc <= qc) & (kc >= qc - B_PREV)
    return np.where(allowed[None], bias, np.float32(NEG_INF)).astype(np.float32)


def kernel(x, ffn1_norm, ffn1_w_gate, ffn1_w_up, ffn1_w_down, mix_norm, w_in, rel_bias, sinks, w_proj_a, w_proj_b, w_out, ffn2_norm, ffn2_w_gate, ffn2_w_up, ffn2_w_down, final_norm, loss_target, m_ffn1_norm, m_ffn1_w_gate, m_ffn1_w_up, m_ffn1_w_down, m_mix_norm, m_w_in, m_rel_bias, m_sinks, m_w_proj_a, m_w_proj_b, m_w_out, m_ffn2_norm, m_ffn2_w_gate, m_ffn2_w_up, m_ffn2_w_down, m_final_norm, v_ffn1_norm, v_ffn1_w_gate, v_ffn1_w_up, v_ffn1_w_down, v_mix_norm, v_w_in, v_rel_bias, v_sinks, v_w_proj_a, v_w_proj_b, v_w_out, v_ffn2_norm, v_ffn2_w_gate, v_ffn2_w_up, v_ffn2_w_down, v_final_norm):
    bsz, s_len, _ = x.shape
    t = bsz * s_len
    core = lax.axis_index("c").astype(jnp.int32).reshape(1)
    chip = (2 * lax.axis_index("x") + lax.axis_index("y")).astype(jnp.int32).reshape(1)

    proj_rows = jnp.concatenate([w_proj_a.T, w_proj_b.T], axis=1)
    sh_g1, sh_u1, sh_d1, sh_in, sh_proj, sh_out, sh_g2, sh_u2, sh_d2 = _to_bf16(
        [ffn1_w_gate.T, ffn1_w_up.T, ffn1_w_down, w_in.T, proj_rows, w_out, ffn2_w_gate.T, ffn2_w_up.T, ffn2_w_down],
        "weights_to_bf16")

    gather_up1 = _Gather([sh_g1, sh_u1])
    far = jnp.broadcast_to(rel_bias[:, REL_TABLE - 1:REL_TABLE], (A_HEADS, REL_WRAP // 2))
    tv = jnp.concatenate([far, jnp.flip(rel_bias, axis=1), jnp.zeros((A_HEADS, REL_WRAP // 2 - REL_TABLE), F32)], axis=1)
    bias_a = _bias_a_build(tv.reshape(A_HEADS, 1, REL_WRAP), hosted=[gather_up1])
    wg1, wu1 = gather_up1.result
    gather_down1 = _Gather([sh_d1, sh_in])
    gather_out = _Gather([sh_proj, sh_out])
    gather_ffn2_gate = _Gather([sh_g2])
    gather_ffn2_rest = _Gather([sh_u2, sh_d2])

    x0 = x.reshape(t, D_MODEL)
    tgt = loss_target.reshape(t, D_MODEL)
    gam1, gam2, gam3, gam4 = (g.reshape(1, D_MODEL) for g in (ffn1_norm, mix_norm, ffn2_norm, final_norm))

    h1, g1, u1, a1 = _ffn_up(x0, gam1, wg1, wu1, "ffn1_up", hosted=[gather_down1])
    wd1, win_t = gather_down1.result
    x1 = _ffn_down(x0, a1, wd1, "ffn1_down", hosted=[gather_out])
    proj_t, wout = gather_out.result
    h2, qkv_a, qkv_b, gates = _proj_fwd(x1, gam2, win_t, hosted=[gather_ffn2_gate])
    (wg2,) = gather_ffn2_gate.result
    qkv_a3 = qkv_a.reshape(bsz, s_len, QKV_A)
    qkv_b3 = qkv_b.reshape(bsz, s_len, QKV_B)

    bias_b = jnp.asarray(_bias_b())
    sink_rows = jnp.broadcast_to(sinks.reshape(B_Q_HEADS, 1, 1), (B_Q_HEADS, 8, LANES))

    oa, lse_a = _attn_a_fwd(qkv_a3, bias_a, hosted=[gather_ffn2_rest])
    oa = oa.reshape(t, A_WIDTH)
    wu2, wd2 = gather_ffn2_rest.result
    ob, lse_b = _attn_b_fwd(qkv_b3, bias_b, sink_rows)
    ob = ob.reshape(t, B_Q_WIDTH)
    x2, ya, yb, mg = _mix_out_fwd(x1, oa, ob, gates, proj_t, wout)
    h3, g2, u2, a2, x3 = _ffn_fwd(x2, gam3, wg2, wu2, wd2, "ffn2_fwd")

    dx2, dg2, du2, db2, dgam3, dgam4, loss_part = _ffn_bwd_head(x3, gam4, tgt, x2, gam3, g2, u2, wg2, wu2, wd2,
                                                                "ffn2_bwd")
    gw_ffn2 = [_mm_tn([dg2], h3, "grad_ffn2_gate"), _mm_tn([du2], h3, "grad_ffn2_up"),
               _mm_tn([a2], db2, "grad_ffn2_down")]
    pairx_ffn2 = _PairExchange(gw_ffn2)
    doa, dob, dgates, gw_out, gw_proj = _mix_out_bwd(dx2, gates, ya, yb, mg, oa, ob, proj_t, wout,
                                                     hosted=[pairx_ffn2])
    psum_ffn2 = _pair_sum(core, gw_ffn2, pairx_ffn2.result, "pair_sum_ffn2")

    chipx_ffn2 = _ChipExchange(psum_ffn2)
    dqa, dka, dva, dbias_a = _attn_a_bwd(qkv_a3, bias_a, doa.reshape(bsz, s_len, A_WIDTH), lse_a,
                                         hosted=[chipx_ffn2])
    pairx_out = _PairExchange([gw_proj, gw_out])
    dqb, dkvb, dsink = _attn_b_bwd(qkv_b3, bias_b, sink_rows, dob.reshape(bsz, s_len, B_Q_WIDTH), lse_b,
                                   hosted=[pairx_out])
    drel_lanes = _relbias_grad(dbias_a)
    dproj = [dqa.reshape(t, A_WIDTH), dka.reshape(t, A_WIDTH), dva.reshape(t, A_WIDTH), dqb.reshape(t, B_Q_WIDTH),
             dkvb.reshape(t, 2 * B_KV_WIDTH), dgates]

    gw_in = _mm_tn(dproj, h2, "grad_w_in")
    pairx_in = _PairExchange([gw_in])
    psum_out = _pair_sum(core, [gw_proj, gw_out], pairx_out.result, "pair_sum_mix")
    chipx_out = _ChipExchange(psum_out)
    dx1, db1, dgam2 = _proj_bwd(dx2, x1, gam2, dproj, win_t, hosted=[pairx_in, chipx_out])
    psum_in = _pair_sum(core, [gw_in], pairx_in.result, "pair_sum_w_in")
    gw_d1 = _mm_tn([a1], db1, "grad_ffn1_down")

    chipx_in = _ChipExchange(psum_in)
    pairx_d1 = _PairExchange([gw_d1])
    dg1, du1 = _ffn_bwd_act(dx1, g1, u1, wd1, "ffn1_bwd_act", hosted=[chipx_in, pairx_d1])
    psum_d1 = _pair_sum(core, [gw_d1], pairx_d1.result, "pair_sum_ffn1_down")
    chipx_d1 = _ChipExchange(psum_d1)
    gw_g1 = _mm_tn([dg1], h1, "grad_ffn1_gate", hosted=[chipx_d1])
    from_sibling_g1 = _exchange_alone(_PairExchange([gw_g1]), "pair_exchange_ffn1_gate")
    psum_g1 = _pair_sum(core, [gw_g1], from_sibling_g1, "pair_sum_ffn1_gate")
    chipx_g1 = _ChipExchange(psum_g1)
    gw_u1 = _mm_tn([du1], h1, "grad_ffn1_up", hosted=[chipx_g1])
    from_sibling_u1 = _exchange_alone(_PairExchange([gw_u1]), "pair_exchange_ffn1_up")
    psum_u1 = _pair_sum(core, [gw_u1], from_sibling_u1, "pair_sum_ffn1_up")
    chipx_u1 = _ChipExchange(psum_u1)
    dx0, dgam1 = _ffn_bwd_in(dx1, x0, gam1, dg1, du1, wg1, wu1, "ffn1_bwd_in", hosted=[chipx_u1])

    (g_proj,) = _final_sum(chip, psum_out[0:1], chipx_out.result[0:1], "grad_sum_proj")
    grads = {"w_proj_a": g_proj[:, 0:A_WIDTH].T, "w_proj_b": g_proj[:, A_WIDTH:].T}

    def row_of(v):
        return jnp.pad(v.reshape(1, -1), ((0, 0), (0, D_MODEL - v.size)))

    def table_rows(v):
        return jnp.pad(v, ((0, 0), (0, D_MODEL - REL_TABLE)))

    drel_local = jnp.flip(drel_lanes[:, 0, 0:REL_TABLE], axis=1)
    small_part = jnp.concatenate(
        [jnp.sum(dgam1, axis=0, keepdims=True), jnp.sum(dgam2, axis=0, keepdims=True),
         jnp.sum(dgam3, axis=0, keepdims=True), jnp.sum(dgam4, axis=0, keepdims=True),
         row_of(jnp.sum(loss_part)), row_of(dsink[:, 0, 0]), jnp.zeros((2, D_MODEL), F32),
         table_rows(drel_local)], axis=0)
    small = _all_reduce_small(small_part)
    loss = small[4, 0]

    def pack(n1, n2, n3, n4, sk, tb):
        return jnp.concatenate([n1.reshape(1, -1), n2.reshape(1, -1), n3.reshape(1, -1), n4.reshape(1, -1),
                                jnp.zeros((1, D_MODEL), F32), row_of(sk), jnp.zeros((2, D_MODEL), F32), table_rows(tb)],
                               axis=0)

    live = np.zeros((SMALL_ROWS, D_MODEL), np.float32)
    live[0:4] = 1.0
    live[5, 0:B_Q_HEADS] = 1.0
    live[8:16, 0:REL_TABLE] = 1.0
    small_g = small * jnp.asarray(live)
    sw = pack(ffn1_norm, mix_norm, ffn2_norm, final_norm, sinks, rel_bias)
    sm = pack(m_ffn1_norm, m_mix_norm, m_ffn2_norm, m_final_norm, m_sinks, m_rel_bias)
    sv = pack(v_ffn1_norm, v_mix_norm, v_ffn2_norm, v_final_norm, v_sinks, v_rel_bias)
    (sd,), (snm,), (snv,) = _adamw([sw], [small_g], [sm], [sv], "adamw_small")

    def unpack(p):
        return {"ffn1_norm": p[0], "mix_norm": p[1], "ffn2_norm": p[2], "final_norm": p[3],
                "sinks": p[5, 0:B_Q_HEADS], "rel_bias": p[8:16, 0:REL_TABLE]}

    grads.update(unpack(small_g))
    delta, new_m, new_v = unpack(sd), unpack(snm), unpack(snv)

    wmv = {
        "ffn1_w_gate": (ffn1_w_gate, m_ffn1_w_gate, v_ffn1_w_gate), "ffn1_w_up": (ffn1_w_up, m_ffn1_w_up, v_ffn1_w_up),
        "ffn1_w_down": (ffn1_w_down, m_ffn1_w_down, v_ffn1_w_down), "w_in": (w_in, m_w_in, v_w_in),
        "w_proj_a": (w_proj_a, m_w_proj_a, v_w_proj_a), "w_proj_b": (w_proj_b, m_w_proj_b, v_w_proj_b),
        "w_out": (w_out, m_w_out, v_w_out),
        "ffn2_w_gate": (ffn2_w_gate, m_ffn2_w_gate, v_ffn2_w_gate), "ffn2_w_up": (ffn2_w_up, m_ffn2_w_up, v_ffn2_w_up),
        "ffn2_w_down": (ffn2_w_down, m_ffn2_w_down, v_ffn2_w_down),
    }
    row_form_names = ("ffn1_w_gate", "ffn1_w_up", "w_in", "ffn2_w_gate", "ffn2_w_up")

    def form(n, a):
        return a.T if n in row_form_names else a

    def reduced_group(gname, names, psums, recvd, steps):
        gs_, ds_, ms_, vs_ = _adamw_reduced(
            chip, [form(n, wmv[n][0]) for n in names], psums, recvd, [form(n, wmv[n][1]) for n in names],
            [form(n, wmv[n][2]) for n in names], steps, gname)
        for n, g_, d_, m_, v_ in zip(names, gs_, ds_, ms_, vs_):
            grads[n], delta[n], new_m[n], new_v[n] = form(n, g_), form(n, d_), form(n, m_), form(n, v_)

    reduced_group("adamw_ffn", ["ffn1_w_gate", "ffn1_w_up", "ffn1_w_down", "ffn2_w_gate", "ffn2_w_up", "ffn2_w_down"],
                  psum_g1 + psum_u1 + psum_d1 + psum_ffn2,
                  chipx_g1.result + chipx_u1.result + chipx_d1.result + chipx_ffn2.result, 11)
    reduced_group("adamw_in_out", ["w_in", "w_out"], psum_in + psum_out[1:2], chipx_in.result + chipx_out.result[1:2], 2)
    names = ["w_proj_a", "w_proj_b"]
    ds_, ms_, vs_ = _adamw([wmv[n][0] for n in names], [grads[n] for n in names], [wmv[n][1] for n in names],
                           [wmv[n][2] for n in names], "adamw_proj")
    for n, d_, m_, v_ in zip(names, ds_, ms_, vs_):
        delta[n], new_m[n], new_v[n] = d_, m_, v_

    order = ["ffn1_norm", "ffn1_w_gate", "ffn1_w_up", "ffn1_w_down", "mix_norm", "w_in", "rel_bias", "sinks",
             "w_proj_a", "w_proj_b", "w_out", "ffn2_norm", "ffn2_w_gate", "ffn2_w_up", "ffn2_w_down", "final_norm"]
    grad_x = dx0.reshape(bsz, s_len, D_MODEL)
    return (loss, grad_x, *[grads[n] for n in order], *[delta[n] for n in order], *[new_m[n] for n in order],
            *[new_v[n] for n in order])
```

```python
import numpy as np
import jax
import jax.numpy as jnp
from jax import lax
from jax.experimental import pallas as pl
from jax.experimental.pallas import tpu as pltpu

F32 = jnp.float32
BF16 = jnp.bfloat16

D_MODEL = 1024
D_FF = 2816
CHUNK = 64
D_HEAD = 64
A_HEADS = 8
A_PREV = 8
MAX_REL = 128
B_Q_HEADS = 8
B_KV_HEADS = 2
B_GROUP = B_Q_HEADS // B_KV_HEADS
B_PREV = 2
REL_TABLE = (CHUNK - 1) + MAX_REL + 1
A_WIDTH = A_HEADS * D_HEAD
B_Q_WIDTH = B_Q_HEADS * D_HEAD
B_KV_WIDTH = B_KV_HEADS * D_HEAD
QKV_A = 3 * A_WIDTH
QKV_B = B_Q_WIDTH + 2 * B_KV_WIDTH
IN_WIDTH = QKV_A + QKV_B + 2 * D_MODEL
EPS = 1e-6
NEG_INF = -1e30
SCALE = 1.0 / 8.0

ADAM_LR = 0.001
ADAM_B1 = 0.9
ADAM_B2 = 0.999
ADAM_EPS = 1e-08
ADAM_WD = 0.01
ADAM_STEP = 10

N_DEV = 8
N_CHIP = 4
MESH = pl.DeviceIdType.MESH

LANES = 128
TQ = 256
TM = 256
FC = 256
VMEM_LIMIT = 56 << 20


def _cparams(sem, vmem=VMEM_LIMIT):
    return pltpu.CompilerParams(dimension_semantics=sem, vmem_limit_bytes=vmem)


def _dot_nt(a, b):
    return lax.dot_general(a, b, (((1,), (1,)), ((), ())), preferred_element_type=F32)


def _dot_nn(a, b):
    return lax.dot_general(a, b, (((1,), (0,)), ((), ())), preferred_element_type=F32)


def _dot_tn(a, b):
    return lax.dot_general(a, b, (((0,), (0,)), ((), ())), preferred_element_type=F32)


def _resident(shape):
    nd = len(shape)
    return pl.BlockSpec(shape, lambda *_: (0,) * nd, pipeline_mode=pl.Buffered(1))


def _rows(tm, width):
    return pl.BlockSpec((tm, width), lambda i: (i, 0))


def _colsum8(v):
    tm, n = v.shape
    return jnp.sum(v.reshape(tm // 8, 8, n), axis=0)


def _rms(x):
    r = lax.rsqrt(jnp.mean(x * x, axis=-1, keepdims=True) + EPS)
    return x * r, r


def _rms_bwd(dh, xh, r, gamma):
    dxh = dh * gamma
    dx = r * (dxh - xh * jnp.mean(dxh * xh, axis=-1, keepdims=True))
    return dx, _colsum8(dh * xh)


def _hbm():
    return pl.BlockSpec(memory_space=pltpu.HBM)


def _call(body, *, name, grid, in_specs, out_specs, out_shape, args, sem, scratch_shapes=(), hosted=()):
    in_specs, out_specs, out_shape = list(in_specs), list(out_specs), list(out_shape)
    scratch_shapes = list(scratch_shapes)
    if not hosted:
        return pl.pallas_call(body, name=name, grid=grid, in_specs=in_specs, out_specs=out_specs, out_shape=out_shape,
                              scratch_shapes=scratch_shapes, compiler_params=_cparams(sem))(*args)
    n_in, n_out, n_scr = len(in_specs), len(out_specs), len(scratch_shapes)
    x_in = [a for x in hosted for a in x.inputs]
    x_out = [s for x in hosted for s in x.out_shape]
    x_scr = [s for x in hosted for s in x.scratch]
    steps = int(np.prod(grid))
    forward_step = max(steps - 3, 0)
    relay_step = min((5 * steps) // 8, forward_step)

    def wrapped(*refs):
        pos = [0]

        def take(k):
            pos[0] += k
            return refs[pos[0] - k:pos[0]]

        ins, xin, outs, xout, scr, xscr = (take(k) for k in (n_in, len(x_in), n_out, len(x_out), n_scr, len(x_scr)))
        step = 0
        for axis, extent in enumerate(grid):
            step = step * extent + pl.program_id(axis)
        own, oi, oo, osc = [], 0, 0, 0
        for x in hosted:
            own.append((xin[oi:oi + len(x.inputs)], xout[oo:oo + len(x.out_shape)], xscr[osc:osc + len(x.scratch)]))
            oi, oo, osc = oi + len(x.inputs), oo + len(x.out_shape), osc + len(x.scratch)

        def phase(method):
            for x, (i_, o_, s_) in zip(hosted, own):
                getattr(x, method)(i_, o_, s_)

        pl.when(step == 0)(lambda: phase("start"))
        body(*ins, *outs, *scr)
        pl.when(step == relay_step)(lambda: phase("relay"))
        pl.when(step == forward_step)(lambda: phase("forward"))
        pl.when(step == steps - 1)(lambda: phase("finish"))

    res = pl.pallas_call(
        wrapped, name=name, grid=grid, in_specs=in_specs + [_hbm()] * len(x_in),
        out_specs=out_specs + [_hbm()] * len(x_out), out_shape=out_shape + x_out,
        scratch_shapes=scratch_shapes + x_scr, compiler_params=_cparams(("arbitrary",) * len(grid)))(*args, *x_in)
    rest = list(res[n_out:])
    for x in hosted:
        x.result, rest = rest[:len(x.out_shape)], rest[len(x.out_shape):]
    return list(res[:n_out])


def _to_bf16(arrays, name):
    n = len(arrays)

    def body(*refs):
        for k in range(n):
            refs[n + k][...] = refs[k][...].astype(BF16)

    specs = [pl.BlockSpec(a.shape, lambda i: (0, 0)) for a in arrays]
    return pl.pallas_call(
        body, name=name, grid=(1,), in_specs=specs, out_specs=specs,
        out_shape=[jax.ShapeDtypeStruct(a.shape, BF16) for a in arrays],
        compiler_params=_cparams(("arbitrary",)))(*arrays)


def _ffn_fwd(x, gamma, wg_t, wu_t, wd, name, hosted=()):
    t = x.shape[0]
    f = wg_t.shape[0]

    def body(x_ref, gam_ref, wg_ref, wu_ref, wd_ref, h_ref, g_ref, u_ref, a_ref, y_ref):
        xv = x_ref[...]
        xh, _ = _rms(xv)
        h = (xh * gam_ref[...]).astype(BF16)
        h_ref[...] = h
        for j in range(f // FC):
            sl = slice(j * FC, (j + 1) * FC)
            g = _dot_nt(h, wg_ref[sl, :])
            u = _dot_nt(h, wu_ref[sl, :])
            g_ref[:, sl] = g.astype(BF16)
            u_ref[:, sl] = u.astype(BF16)
            a_ref[:, sl] = (g * jax.nn.sigmoid(g) * u).astype(BF16)
        y_ref[...] = xv + 0.5 * _dot_nn(a_ref[...], wd_ref[...])

    return _call(
        body,
        name=name,
        grid=(t // TM,),
        in_specs=[_rows(TM, D_MODEL), _resident((1, D_MODEL)), _resident((f, D_MODEL)), _resident((f, D_MODEL)),
                  _resident((f, D_MODEL))],
        out_specs=[_rows(TM, D_MODEL), _rows(TM, f), _rows(TM, f), _rows(TM, f), _rows(TM, D_MODEL)],
        out_shape=[jax.ShapeDtypeStruct((t, D_MODEL), BF16), jax.ShapeDtypeStruct((t, f), BF16),
                   jax.ShapeDtypeStruct((t, f), BF16), jax.ShapeDtypeStruct((t, f), BF16),
                   jax.ShapeDtypeStruct((t, D_MODEL), F32)],
        args=(x, gamma, wg_t, wu_t, wd), sem=("parallel",), hosted=hosted)


def _ffn_up(x, gamma, wg_t, wu_t, name, hosted=()):
    t = x.shape[0]
    f = wg_t.shape[0]

    def body(x_ref, gam_ref, wg_ref, wu_ref, h_ref, g_ref, u_ref, a_ref):
        xh, _ = _rms(x_ref[...])
        h = (xh * gam_ref[...]).astype(BF16)
        h_ref[...] = h
        for j in range(f // FC):
            sl = slice(j * FC, (j + 1) * FC)
            g = _dot_nt(h, wg_ref[sl, :])
            u = _dot_nt(h, wu_ref[sl, :])
            g_ref[:, sl] = g.astype(BF16)
            u_ref[:, sl] = u.astype(BF16)
            a_ref[:, sl] = (g * jax.nn.sigmoid(g) * u).astype(BF16)

    return _call(
        body,
        name=name,
        grid=(t // TM,),
        in_specs=[_rows(TM, D_MODEL), _resident((1, D_MODEL)), _resident((f, D_MODEL)), _resident((f, D_MODEL))],
        out_specs=[_rows(TM, D_MODEL), _rows(TM, f), _rows(TM, f), _rows(TM, f)],
        out_shape=[jax.ShapeDtypeStruct((t, D_MODEL), BF16), jax.ShapeDtypeStruct((t, f), BF16),
                   jax.ShapeDtypeStruct((t, f), BF16), jax.ShapeDtypeStruct((t, f), BF16)],
        args=(x, gamma, wg_t, wu_t), sem=("parallel",), hosted=hosted)


def _ffn_down(x, a_act, wd, name, hosted=()):
    t = x.shape[0]
    f = wd.shape[0]

    def body(x_ref, a_ref, wd_ref, y_ref):
        y_ref[...] = x_ref[...] + 0.5 * _dot_nn(a_ref[...], wd_ref[...])

    return _call(
        body,
        name=name,
        grid=(t // TM,),
        in_specs=[_rows(TM, D_MODEL), _rows(TM, f), _resident((f, D_MODEL))],
        out_specs=[_rows(TM, D_MODEL)],
        out_shape=[jax.ShapeDtypeStruct((t, D_MODEL), F32)],
        args=(x, a_act, wd), sem=("parallel",), hosted=hosted)[0]


def _ffn_bwd_head(y, gamma_f, target, x, gamma, g_act, u_act, wg_t, wu_t, wd, name):
    t = x.shape[0]
    f = wg_t.shape[0]

    def body(y_ref, gamf_ref, t_ref, x_ref, gam_ref, g_ref, u_ref, wg_ref, wu_ref, wd_ref, dx_ref, dg_ref, du_ref,
             db_ref, dgam_ref, dgamf_ref, loss_ref):
        yh, ry = _rms(y_ref[...])
        gam_f = gamf_ref[...]
        e = yh * gam_f - t_ref[...]
        dv, dgam_f = _rms_bwd(e * (1.0 / D_MODEL), yh, ry, gam_f)
        db = (0.5 * dv).astype(BF16)
        db_ref[...] = db
        for j in range(f // FC):
            sl = slice(j * FC, (j + 1) * FC)
            da = _dot_nt(db, wd_ref[sl, :])
            g = g_ref[:, sl].astype(F32)
            u = u_ref[:, sl].astype(F32)
            s = jax.nn.sigmoid(g)
            dg_ref[:, sl] = (da * u * (s * (1.0 + g * (1.0 - s)))).astype(BF16)
            du_ref[:, sl] = (da * (g * s)).astype(BF16)
        dh = _dot_nn(dg_ref[...], wg_ref[...]) + _dot_nn(du_ref[...], wu_ref[...])
        xh, r = _rms(x_ref[...])
        dxn, dgam = _rms_bwd(dh, xh, r, gam_ref[...])
        dx_ref[...] = dv + dxn

        @pl.when(pl.program_id(0) == 0)
        def _():
            dgam_ref[...] = jnp.zeros_like(dgam_ref)
            dgamf_ref[...] = jnp.zeros_like(dgamf_ref)
            loss_ref[...] = jnp.zeros_like(loss_ref)

        dgam_ref[...] += dgam
        dgamf_ref[...] += dgam_f
        loss_ref[...] += _colsum8(e * e) * (0.5 / D_MODEL)

    acc = pl.BlockSpec((8, D_MODEL), lambda i: (0, 0))
    return _call(
        body,
        name=name,
        grid=(t // TM,),
        in_specs=[_rows(TM, D_MODEL), _resident((1, D_MODEL)), _rows(TM, D_MODEL), _rows(TM, D_MODEL),
                  _resident((1, D_MODEL)), _rows(TM, f), _rows(TM, f),
                  _resident((f, D_MODEL)), _resident((f, D_MODEL)), _resident((f, D_MODEL))],
        out_specs=[_rows(TM, D_MODEL), _rows(TM, f), _rows(TM, f), _rows(TM, D_MODEL), acc, acc, acc],
        out_shape=[jax.ShapeDtypeStruct((t, D_MODEL), F32), jax.ShapeDtypeStruct((t, f), BF16),
                   jax.ShapeDtypeStruct((t, f), BF16), jax.ShapeDtypeStruct((t, D_MODEL), BF16),
                   jax.ShapeDtypeStruct((8, D_MODEL), F32), jax.ShapeDtypeStruct((8, D_MODEL), F32),
                   jax.ShapeDtypeStruct((8, D_MODEL), F32)],
        args=(y, gamma_f, target, x, gamma, g_act, u_act, wg_t, wu_t, wd), sem=("arbitrary",))


def _ffn_bwd_act(d, g_act, u_act, wd, name, hosted=()):
    t = d.shape[0]
    f = wd.shape[0]

    def body(d_ref, g_ref, u_ref, wd_ref, dg_ref, du_ref):
        db = (0.5 * d_ref[...]).astype(BF16)
        for j in range(f // FC):
            sl = slice(j * FC, (j + 1) * FC)
            da = _dot_nt(db, wd_ref[sl, :])
            g = g_ref[:, sl].astype(F32)
            u = u_ref[:, sl].astype(F32)
            s = jax.nn.sigmoid(g)
            dg_ref[:, sl] = (da * u * (s * (1.0 + g * (1.0 - s)))).astype(BF16)
            du_ref[:, sl] = (da * (g * s)).astype(BF16)

    return _call(
        body,
        name=name,
        grid=(t // TM,),
        in_specs=[_rows(TM, D_MODEL), _rows(TM, f), _rows(TM, f), _resident((f, D_MODEL))],
        out_specs=[_rows(TM, f), _rows(TM, f)],
        out_shape=[jax.ShapeDtypeStruct((t, f), BF16), jax.ShapeDtypeStruct((t, f), BF16)],
        args=(d, g_act, u_act, wd), sem=("parallel",), hosted=hosted)


def _ffn_bwd_in(d, x, gamma, dg, du, wg_t, wu_t, name, hosted=()):
    t = x.shape[0]
    f = wg_t.shape[0]

    def body(d_ref, x_ref, gam_ref, dg_ref, du_ref, wg_ref, wu_ref, dx_ref, dgam_ref):
        dh = _dot_nn(dg_ref[...], wg_ref[...]) + _dot_nn(du_ref[...], wu_ref[...])
        xh, r = _rms(x_ref[...])
        dxn, dgam = _rms_bwd(dh, xh, r, gam_ref[...])
        dx_ref[...] = d_ref[...] + dxn

        @pl.when(pl.program_id(0) == 0)
        def _():
            dgam_ref[...] = jnp.zeros_like(dgam_ref)

        dgam_ref[...] += dgam

    return _call(
        body,
        name=name,
        grid=(t // TM,),
        in_specs=[_rows(TM, D_MODEL), _rows(TM, D_MODEL), _resident((1, D_MODEL)), _rows(TM, f), _rows(TM, f),
                  _resident((f, D_MODEL)), _resident((f, D_MODEL))],
        out_specs=[_rows(TM, D_MODEL), pl.BlockSpec((8, D_MODEL), lambda i: (0, 0))],
        out_shape=[jax.ShapeDtypeStruct((t, D_MODEL), F32), jax.ShapeDtypeStruct((8, D_MODEL), F32)],
        args=(d, x, gamma, dg, du, wg_t, wu_t), sem=("arbitrary",), hosted=hosted)


def _mm_tn(pieces, b, name, tile=256, hosted=()):
    t, n = b.shape
    npc = len(pieces)
    counts = [p.shape[1] // tile for p in pieces]
    los = [sum(counts[:k]) for k in range(npc)]
    total = sum(counts)

    def body(*refs):
        a_refs, b_ref, o_ref = refs[:npc], refs[npc], refs[npc + 1]
        i = pl.program_id(0)
        for k in range(npc):
            @pl.when(jnp.logical_and(i >= los[k], i < los[k] + counts[k]))
            def _(k=k):
                o_ref[...] = _dot_tn(a_refs[k][...], b_ref[...]).astype(BF16)

    def a_spec(k):
        return pl.BlockSpec((t, tile), lambda i: (0, jnp.clip(i - los[k], 0, counts[k] - 1)))

    return _call(
        body,
        name=name,
        grid=(total,),
        in_specs=[a_spec(k) for k in range(npc)] + [_resident((t, n))],
        out_specs=[pl.BlockSpec((tile, n), lambda i: (i, 0))],
        out_shape=[jax.ShapeDtypeStruct((total * tile, n), BF16)],
        args=(*pieces, b), sem=("parallel",), hosted=hosted)[0]


def _proj_fwd(x, gamma, win_t, hosted=()):
    t = x.shape[0]

    def body(x_ref, gam_ref, w_ref, h_ref, qa_ref, qb_ref, gt_ref):
        xh, _ = _rms(x_ref[...])
        h = (xh * gam_ref[...]).astype(BF16)
        h_ref[...] = h
        for j in range(QKV_A // FC):
            qa_ref[:, j * FC:(j + 1) * FC] = _dot_nt(h, w_ref[j * FC:(j + 1) * FC, :]).astype(BF16)
        for j in range(QKV_B // FC):
            lo = QKV_A + j * FC
            qb_ref[:, j * FC:(j + 1) * FC] = _dot_nt(h, w_ref[lo:lo + FC, :]).astype(BF16)
        for j in range(2 * D_MODEL // FC):
            lo = QKV_A + QKV_B + j * FC
            gt_ref[:, j * FC:(j + 1) * FC] = _dot_nt(h, w_ref[lo:lo + FC, :])

    return _call(
        body,
        name="proj_fwd",
        grid=(t // TM,),
        in_specs=[_rows(TM, D_MODEL), _resident((1, D_MODEL)), _resident((IN_WIDTH, D_MODEL))],
        out_specs=[_rows(TM, D_MODEL), _rows(TM, QKV_A), _rows(TM, QKV_B), _rows(TM, 2 * D_MODEL)],
        out_shape=[jax.ShapeDtypeStruct((t, D_MODEL), BF16), jax.ShapeDtypeStruct((t, QKV_A), BF16),
                   jax.ShapeDtypeStruct((t, QKV_B), BF16), jax.ShapeDtypeStruct((t, 2 * D_MODEL), F32)],
        args=(x, gamma, win_t), sem=("parallel",), hosted=hosted)


def _proj_bwd(d, x, gamma, pieces, win_t, hosted=()):
    t = x.shape[0]
    npc = len(pieces)
    widths = [p.shape[1] for p in pieces]
    los = [sum(widths[:k]) for k in range(npc)]

    def body(*refs):
        d_ref, x_ref, gam_ref = refs[:3]
        p_refs = refs[3:3 + npc]
        w_ref, dx_ref, db_ref, dgam_ref = refs[3 + npc:]
        dh = _dot_nn(p_refs[0][...], w_ref[0:widths[0], :])
        for k in range(1, npc):
            dh += _dot_nn(p_refs[k][...], w_ref[los[k]:los[k] + widths[k], :])
        xh, r = _rms(x_ref[...])
        dxn, dgam = _rms_bwd(dh, xh, r, gam_ref[...])
        dx = d_ref[...] + dxn
        dx_ref[...] = dx
        db_ref[...] = (0.5 * dx).astype(BF16)

        @pl.when(pl.program_id(0) == 0)
        def _():
            dgam_ref[...] = jnp.zeros_like(dgam_ref)

        dgam_ref[...] += dgam

    return _call(
        body,
        name="proj_bwd",
        grid=(t // TM,),
        in_specs=[_rows(TM, D_MODEL), _rows(TM, D_MODEL), _resident((1, D_MODEL))] + [_rows(TM, w) for w in widths]
        + [_resident((IN_WIDTH, D_MODEL))],
        out_specs=[_rows(TM, D_MODEL), _rows(TM, D_MODEL), pl.BlockSpec((8, D_MODEL), lambda i: (0, 0))],
        out_shape=[jax.ShapeDtypeStruct((t, D_MODEL), F32), jax.ShapeDtypeStruct((t, D_MODEL), BF16),
                   jax.ShapeDtypeStruct((8, D_MODEL), F32)],
        args=(d, x, gamma, *pieces, win_t), sem=("arbitrary",), hosted=hosted)


def _lane_half(shape):
    return lax.broadcasted_iota(jnp.int32, shape, len(shape) - 1) // D_HEAD


def _band_weights(q, kk, bias, sink, qs, pad):
    s = _band_scores(q, kk, bias, qs, pad)
    m = jnp.max(s, axis=-1, keepdims=True)
    if sink is not None:
        m = jnp.maximum(m, sink)
    return jnp.exp(s - m), m


def _band_scores(q, kk, bias, qs, pad):
    s = _dot_nt(q, kk) + bias
    if qs is not None:
        col = lax.broadcasted_iota(jnp.int32, s.shape, 1)
        s = jnp.where(col + qs >= pad, s, NEG_INF)
    return s


def _weighted_values(p, vv_ones, sink, m):
    r = _dot_nn(p.astype(BF16), vv_ones)
    den = r[:, LANES:2 * LANES]
    if sink is not None:
        den = den + jnp.exp(sink - m)
    return r[:, 0:LANES] / den, m + jnp.log(den[:, 0:1])


def _fill_padded(dst, src, pad):
    dst[0:pad, :] = jnp.zeros((pad,) + dst.shape[1:], dst.dtype)
    dst[pad:, :] = src


FWD_PAIRS = 4
BWD_PAIRS = 4


def _attn_a_fwd(qkv, bias, hosted=()):
    bsz, s_len, _ = qkv.shape
    pad = A_PREV * CHUNK
    band = TQ + pad
    pp = FWD_PAIRS
    w = pp * LANES
    nb = A_WIDTH // w

    def body(q_ref, k_ref, v_ref, b_ref, o_ref, l_ref, kp, vp):
        i = pl.program_id(2)

        @pl.when(i == 0)
        def _():
            _fill_padded(kp, k_ref[...], pad)
            _fill_padded(vp, v_ref[...], pad)

        qs = pl.multiple_of(i * TQ, TQ)
        half = _lane_half((1, LANES))

        ones = jnp.ones((band, LANES), BF16)

        def block(masked):
            for pr in range(pp):
                sl = slice(pr * LANES, (pr + 1) * LANES)
                kk = kp[pl.ds(qs, band), sl]
                vv = jnp.concatenate([vp[pl.ds(qs, band), sl], ones], axis=1)
                q = q_ref[:, sl] * SCALE
                outs = []
                for j in range(2):
                    qm = jnp.where(half == j, q, jnp.zeros_like(q))
                    p, m = _band_weights(qm, kk, b_ref[2 * pr + j], None, qs if masked else None, pad)
                    o, lse = _weighted_values(p, vv, None, m)
                    outs.append(o)
                    l_ref[:, 2 * pr + j:2 * pr + j + 1] = lse
                o_ref[:, sl] = jnp.where(half == 0, outs[0], outs[1]).astype(BF16)

        pl.when(i < pad // TQ)(lambda: block(True))
        pl.when(i >= pad // TQ)(lambda: block(False))

    return _call(
        body,
        name="attn_a_fwd",
        grid=(bsz, nb, s_len // TQ),
        in_specs=[pl.BlockSpec((None, TQ, w), lambda b, g, i: (b, i, g)),
                  pl.BlockSpec((None, s_len, w), lambda b, g, i: (b, 0, nb + g)),
                  pl.BlockSpec((None, s_len, w), lambda b, g, i: (b, 0, 2 * nb + g)),
                  pl.BlockSpec((2 * pp, TQ, band), lambda b, g, i: (g, 0, 0))],
        out_specs=[pl.BlockSpec((None, TQ, w), lambda b, g, i: (b, i, g)),
                   pl.BlockSpec((None, TQ, 2 * pp), lambda b, g, i: (b, i, g))],
        out_shape=[jax.ShapeDtypeStruct((bsz, s_len, A_WIDTH), BF16),
                   jax.ShapeDtypeStruct((bsz, s_len, A_HEADS), F32)],
        scratch_shapes=[pltpu.VMEM((pad + s_len, w), BF16), pltpu.VMEM((pad + s_len, w), BF16)],
        args=(qkv, qkv, qkv, bias), sem=("arbitrary", "arbitrary", "arbitrary"), hosted=hosted)


def _attn_a_bwd(qkv, bias, do, o, lse, hosted=()):
    bsz, s_len, _ = qkv.shape
    pad = A_PREV * CHUNK
    band = TQ + pad
    n_i = s_len // TQ
    pp = BWD_PAIRS
    w = pp * LANES
    nb = A_WIDTH // w

    def body(q_ref, k_ref, v_ref, b_ref, do_ref, o_ref, l_ref, dq_ref, dk_ref, dv_ref, dbias_ref, kp, vp, dk_acc,
             dv_acc):
        b = pl.program_id(1)
        i = pl.program_id(2)

        @pl.when(i == 0)
        def _():
            _fill_padded(kp, k_ref[...], pad)
            _fill_padded(vp, v_ref[...], pad)
            dk_acc[...] = jnp.zeros_like(dk_acc)
            dv_acc[...] = jnp.zeros_like(dv_acc)

        @pl.when(jnp.logical_and(b == 0, i == 0))
        def _():
            dbias_ref[...] = jnp.zeros_like(dbias_ref)

        qs = pl.multiple_of(i * TQ, TQ)
        half = _lane_half((1, LANES))

        def block(masked):
            for pr in range(pp):
                sl = slice(pr * LANES, (pr + 1) * LANES)
                kk = kp[pl.ds(qs, band), sl]
                vv = vp[pl.ds(qs, band), sl]
                q = q_ref[:, sl] * SCALE
                dd = do_ref[:, sl]
                od = dd.astype(F32) * o_ref[:, sl].astype(F32)
                dqs, dks, dvs = [], [], []
                for j in range(2):
                    hd = 2 * pr + j
                    qm = jnp.where(half == j, q, jnp.zeros_like(q))
                    dm = jnp.where(half == j, dd, jnp.zeros_like(dd))
                    s = _band_scores(qm, kk, b_ref[hd], qs if masked else None, pad)
                    pn = jnp.exp(s - l_ref[:, hd:hd + 1])
                    dp = _dot_nt(dm, vv)
                    delta = jnp.sum(jnp.where(half == j, od, 0.0), axis=-1, keepdims=True)
                    ds = pn * (dp - delta)
                    dbias_ref[hd] += ds[:, band - REL_COLS:]
                    dsb = ds.astype(BF16)
                    dqs.append(_dot_nn(dsb, kk))
                    dks.append(_dot_tn(dsb, q))
                    dvs.append(_dot_tn(pn.astype(BF16), dd))
                dq_ref[:, sl] = (jnp.where(half == 0, dqs[0], dqs[1]) * SCALE).astype(BF16)
                dk_acc[pl.ds(qs, band), sl] += jnp.where(half == 0, dks[0], dks[1])
                dv_acc[pl.ds(qs, band), sl] += jnp.where(half == 0, dvs[0], dvs[1])

        pl.when(i < pad // TQ)(lambda: block(True))
        pl.when(i >= pad // TQ)(lambda: block(False))

        @pl.when(i == n_i - 1)
        def _():
            dk_ref[...] = dk_acc[pad:, :].astype(BF16)
            dv_ref[...] = dv_acc[pad:, :].astype(BF16)

    qspec = pl.BlockSpec((None, TQ, w), lambda g, b, i: (b, i, g))
    kvout = pl.BlockSpec((None, s_len, w), lambda g, b, i: (b, 0, g))
    wide = jax.ShapeDtypeStruct((bsz, s_len, A_WIDTH), BF16)
    return _call(
        body,
        name="attn_a_bwd",
        grid=(nb, bsz, n_i),
        in_specs=[qspec,
                  pl.BlockSpec((None, s_len, w), lambda g, b, i: (b, 0, nb + g)),
                  pl.BlockSpec((None, s_len, w), lambda g, b, i: (b, 0, 2 * nb + g)),
                  pl.BlockSpec((2 * pp, TQ, band), lambda g, b, i: (g, 0, 0)),
                  qspec, qspec,
                  pl.BlockSpec((None, TQ, 2 * pp), lambda g, b, i: (b, i, g))],
        out_specs=[qspec, kvout, kvout, pl.BlockSpec((2 * pp, TQ, REL_COLS), lambda g, b, i: (g, 0, 0))],
        out_shape=[wide, wide, wide, jax.ShapeDtypeStruct((A_HEADS, TQ, REL_COLS), F32)],
        scratch_shapes=[pltpu.VMEM((pad + s_len, w), BF16), pltpu.VMEM((pad + s_len, w), BF16),
                        pltpu.VMEM((pad + s_len, w), F32), pltpu.VMEM((pad + s_len, w), F32)],
        args=(qkv, qkv, qkv, bias, do, o, lse), sem=("arbitrary", "arbitrary", "arbitrary"), hosted=hosted)


def _fill_padded_dup(dst, src, pad, h, half):
    other = pltpu.roll(src, D_HEAD, 1)
    _fill_padded(dst, jnp.where(half == h, src, other), pad)


def _attn_b_fwd(qkv, bias, sink):
    bsz, s_len, _ = qkv.shape
    pad = B_PREV * CHUNK
    band = TQ + pad
    kcol = B_Q_WIDTH // LANES
    npair = B_Q_HEADS // 2

    def body(q_ref, k_ref, v_ref, b_ref, s_ref, o_ref, l_ref, kp, vp):
        i = pl.program_id(1)
        half = _lane_half((1, LANES))

        @pl.when(i == 0)
        def _():
            for h in range(B_KV_HEADS):
                _fill_padded_dup(kp.at[h], k_ref[...], pad, h, half)
                _fill_padded_dup(vp.at[h], v_ref[...], pad, h, half)

        qs = pl.multiple_of(i * TQ, TQ)

        ones = jnp.ones((band, LANES), BF16)

        def block(masked):
            for pr in range(npair):
                h = pr // (B_GROUP // 2)
                sl = slice(pr * LANES, (pr + 1) * LANES)
                kk = kp[h, pl.ds(qs, band), :]
                vv = jnp.concatenate([vp[h, pl.ds(qs, band), :], ones], axis=1)
                q = q_ref[:, sl] * SCALE
                outs = []
                for j in range(2):
                    qm = jnp.where(half == j, q, jnp.zeros_like(q))
                    sink = s_ref[2 * pr + j][0:1, 0:1]
                    p, m = _band_weights(qm, kk, b_ref[2 * pr + j], sink, qs if masked else None, pad)
                    o, lse = _weighted_values(p, vv, sink, m)
                    outs.append(o)
                    l_ref[:, 2 * pr + j:2 * pr + j + 1] = lse
                o_ref[:, sl] = jnp.where(half == 0, outs[0], outs[1]).astype(BF16)

        pl.when(i < -(-pad // TQ))(lambda: block(True))
        pl.when(i >= -(-pad // TQ))(lambda: block(False))

    return pl.pallas_call(
        body,
        name="attn_b_fwd",
        grid=(bsz, s_len // TQ),
        in_specs=[pl.BlockSpec((None, TQ, B_Q_WIDTH), lambda b, i: (b, i, 0)),
                  pl.BlockSpec((None, s_len, LANES), lambda b, i: (b, 0, kcol)),
                  pl.BlockSpec((None, s_len, LANES), lambda b, i: (b, 0, kcol + 1)),
                  pl.BlockSpec((B_Q_HEADS, TQ, band), lambda b, i: (0, 0, 0)),
                  pl.BlockSpec((B_Q_HEADS, 8, LANES), lambda b, i: (0, 0, 0))],
        out_specs=[pl.BlockSpec((None, TQ, B_Q_WIDTH), lambda b, i: (b, i, 0)),
                   pl.BlockSpec((None, TQ, B_Q_HEADS), lambda b, i: (b, i, 0))],
        out_shape=[jax.ShapeDtypeStruct((bsz, s_len, B_Q_WIDTH), BF16),
                   jax.ShapeDtypeStruct((bsz, s_len, B_Q_HEADS), F32)],
        scratch_shapes=[pltpu.VMEM((B_KV_HEADS, pad + s_len, LANES), BF16),
                        pltpu.VMEM((B_KV_HEADS, pad + s_len, LANES), BF16)],
        compiler_params=_cparams(("arbitrary", "arbitrary")),
    )(qkv, qkv, qkv, bias, sink)


def _attn_b_bwd(qkv, bias, sink, do, o, lse, hosted=()):
    bsz, s_len, _ = qkv.shape
    pad = B_PREV * CHUNK
    band = TQ + pad
    kcol = B_Q_WIDTH // LANES
    n_i = s_len // TQ
    pp = B_GROUP // 2

    def body(q_ref, k_ref, v_ref, b_ref, s_ref, do_ref, o_ref, l_ref, dq_ref, dkv_ref, dsink_ref, kp, vp, dk_acc,
             dv_acc):
        b = pl.program_id(0)
        i = pl.program_id(1)
        half = _lane_half((1, LANES))

        @pl.when(i == 0)
        def _():
            for h in range(B_KV_HEADS):
                _fill_padded_dup(kp.at[h], k_ref[...], pad, h, half)
                _fill_padded_dup(vp.at[h], v_ref[...], pad, h, half)
            dk_acc[...] = jnp.zeros_like(dk_acc)
            dv_acc[...] = jnp.zeros_like(dv_acc)

        @pl.when(jnp.logical_and(b == 0, i == 0))
        def _():
            dsink_ref[...] = jnp.zeros_like(dsink_ref)

        qs = pl.multiple_of(i * TQ, TQ)

        def block(masked):
            heads_dk, heads_dv = [], []
            for h in range(B_KV_HEADS):
                kk = kp[h, pl.ds(qs, band), :]
                vv = vp[h, pl.ds(qs, band), :]
                dk2 = jnp.zeros((band, LANES), F32)
                dv2 = jnp.zeros((band, LANES), F32)
                for pr in range(pp * h, pp * (h + 1)):
                    sl = slice(pr * LANES, (pr + 1) * LANES)
                    q = q_ref[:, sl] * SCALE
                    dd = do_ref[:, sl]
                    od = dd.astype(F32) * o_ref[:, sl].astype(F32)
                    dqs, dks, dvs = [], [], []
                    for j in range(2):
                        qm = jnp.where(half == j, q, jnp.zeros_like(q))
                        dm = jnp.where(half == j, dd, jnp.zeros_like(dd))
                        hd = 2 * pr + j
                        lse = l_ref[:, hd:hd + 1]
                        s = _band_scores(qm, kk, b_ref[hd], qs if masked else None, pad)
                        pn = jnp.exp(s - lse)
                        dp = _dot_nt(dm, vv)
                        delta = jnp.sum(jnp.where(half == j, od, 0.0), axis=-1, keepdims=True)
                        ds = pn * (dp - delta)
                        dsb = ds.astype(BF16)
                        dqs.append(_dot_nn(dsb, kk))
                        dks.append(_dot_tn(dsb, q))
                        dvs.append(_dot_tn(pn.astype(BF16), dd))
                        sink = s_ref[hd][0:1, 0:1]
                        dsk = jnp.sum(-jnp.exp(sink - lse) * delta, axis=0, keepdims=True)
                        dsink_ref[hd] += jnp.broadcast_to(dsk, (8, LANES))
                    dq_ref[:, sl] = (jnp.where(half == 0, dqs[0], dqs[1]) * SCALE).astype(BF16)
                    dk2 = dk2 + jnp.where(half == 0, dks[0], dks[1])
                    dv2 = dv2 + jnp.where(half == 0, dvs[0], dvs[1])
                heads_dk.append(dk2 + pltpu.roll(dk2, D_HEAD, 1))
                heads_dv.append(dv2 + pltpu.roll(dv2, D_HEAD, 1))
            dk_acc[pl.ds(qs, band), :] += jnp.where(half == 0, heads_dk[0], heads_dk[1])
            dv_acc[pl.ds(qs, band), :] += jnp.where(half == 0, heads_dv[0], heads_dv[1])

        pl.when(i < -(-pad // TQ))(lambda: block(True))
        pl.when(i >= -(-pad // TQ))(lambda: block(False))

        @pl.when(i == n_i - 1)
        def _():
            dkv_ref[:, 0:LANES] = dk_acc[pad:, :].astype(BF16)
            dkv_ref[:, LANES:2 * LANES] = dv_acc[pad:, :].astype(BF16)

    qspec = pl.BlockSpec((None, TQ, B_Q_WIDTH), lambda b, i: (b, i, 0))
    return _call(
        body,
        name="attn_b_bwd",
        grid=(bsz, n_i),
        in_specs=[qspec,
                  pl.BlockSpec((None, s_len, LANES), lambda b, i: (b, 0, kcol)),
                  pl.BlockSpec((None, s_len, LANES), lambda b, i: (b, 0, kcol + 1)),
                  pl.BlockSpec((B_Q_HEADS, TQ, band), lambda b, i: (0, 0, 0)),
                  pl.BlockSpec((B_Q_HEADS, 8, LANES), lambda b, i: (0, 0, 0)),
                  qspec, qspec,
                  pl.BlockSpec((None, TQ, B_Q_HEADS), lambda b, i: (b, i, 0))],
        out_specs=[qspec, pl.BlockSpec((None, s_len, 2 * LANES), lambda b, i: (b, 0, 0)),
                   pl.BlockSpec((B_Q_HEADS, 8, LANES), lambda b, i: (0, 0, 0))],
        out_shape=[jax.ShapeDtypeStruct((bsz, s_len, B_Q_WIDTH), BF16),
                   jax.ShapeDtypeStruct((bsz, s_len, 2 * B_KV_WIDTH), BF16),
                   jax.ShapeDtypeStruct((B_Q_HEADS, 8, LANES), F32)],
        scratch_shapes=[pltpu.VMEM((B_KV_HEADS, pad + s_len, LANES), BF16),
                        pltpu.VMEM((B_KV_HEADS, pad + s_len, LANES), BF16),
                        pltpu.VMEM((pad + s_len, LANES), F32), pltpu.VMEM((pad + s_len, LANES), F32)],
        args=(qkv, qkv, qkv, bias, sink, do, o, lse), sem=("arbitrary", "arbitrary"), hosted=hosted)


REL_COLS = 3 * 128
REL_WRAP = 512


def _bias_a_build(tv, hosted=()):
    h = tv.shape[0]
    pad = A_PREV * CHUNK
    band = TQ + pad

    def body(tv_ref, o_ref):
        row = tv_ref[...]
        x = jnp.broadcast_to(row, (TQ, REL_WRAP))
        r = lax.broadcasted_iota(jnp.int32, x.shape, 0)
        for bit in range(8):
            sh = 1 << bit
            x = jnp.where((r & sh) != 0, pltpu.roll(x, sh, 1), x)
        far = jnp.broadcast_to(row[:, 0:1], (TQ, band - REL_COLS))
        full = jnp.concatenate([far, x[:, REL_WRAP // 2:REL_WRAP], x[:, 0:REL_COLS - REL_WRAP // 2]], axis=1)
        qc = (lax.broadcasted_iota(jnp.int32, full.shape, 0) + pad) // CHUNK
        kc = lax.broadcasted_iota(jnp.int32, full.shape, 1) // CHUNK
        ok = jnp.logical_and(kc <= qc, kc >= qc - A_PREV)
        o_ref[...] = jnp.where(ok, full, NEG_INF)

    return _call(
        body,
        name="bias_a_build",
        grid=(h,),
        in_specs=[pl.BlockSpec((None, 1, REL_WRAP), lambda hh: (hh, 0, 0))],
        out_specs=[pl.BlockSpec((None, TQ, band), lambda hh: (hh, 0, 0))],
        out_shape=[jax.ShapeDtypeStruct((h, TQ, band), F32)],
        args=(tv,), sem=("parallel",), hosted=hosted)[0]


def _relbias_grad(dbias, hosted=()):
    h, rows, _ = dbias.shape

    def body(d_ref, o_ref):
        x = d_ref[...]
        r = lax.broadcasted_iota(jnp.int32, x.shape, 0)
        c = lax.broadcasted_iota(jnp.int32, x.shape, 1) - r
        x = jnp.where(jnp.logical_and(c >= 1, c < REL_TABLE), x, 0.0)
        for bit in range(8):
            sh = 1 << bit
            x = jnp.where((r & sh) != 0, pltpu.roll(x, REL_COLS - sh, 1), x)
        diag = jnp.sum(x, axis=0, keepdims=True)
        lane = lax.broadcasted_iota(jnp.int32, diag.shape, 1)
        diag = jnp.where(jnp.logical_and(lane >= 1, lane < REL_TABLE), diag, 0.0)
        rest = -jnp.sum(diag, axis=1, keepdims=True)
        o_ref[...] = jnp.broadcast_to(jnp.where(lane == 0, rest, diag), o_ref.shape)

    return _call(
        body,
        name="relbias_grad",
        grid=(h,),
        in_specs=[pl.BlockSpec((None, rows, REL_COLS), lambda hh: (hh, 0, 0))],
        out_specs=[pl.BlockSpec((None, 8, REL_COLS), lambda hh: (hh, 0, 0))],
        out_shape=[jax.ShapeDtypeStruct((h, 8, REL_COLS), F32)],
        args=(dbias,), sem=("parallel",), hosted=hosted)[0]


def _mix_out_fwd(x, oa, ob, gates, proj_t, wout):
    t = x.shape[0]

    def body(x_ref, oa_ref, ob_ref, gt_ref, pt_ref, wo_ref, y_ref, ya_ref, yb_ref, mg_ref):
        ya = _dot_nt(oa_ref[...], pt_ref[:, 0:A_WIDTH])
        yb = _dot_nt(ob_ref[...], pt_ref[:, A_WIDTH:A_WIDTH + B_Q_WIDTH])
        ya_ref[...] = ya.astype(BF16)
        yb_ref[...] = yb.astype(BF16)
        mg = jax.nn.sigmoid(gt_ref[:, 0:D_MODEL]) * ya + jax.nn.sigmoid(gt_ref[:, D_MODEL:2 * D_MODEL]) * yb
        mgb = mg.astype(BF16)
        mg_ref[...] = mgb
        y_ref[...] = x_ref[...] + _dot_nn(mgb, wo_ref[...])

    return pl.pallas_call(
        body,
        name="mix_out_fwd",
        grid=(t // TM,),
        in_specs=[_rows(TM, D_MODEL), _rows(TM, A_WIDTH), _rows(TM, B_Q_WIDTH), _rows(TM, 2 * D_MODEL),
                  _resident((D_MODEL, A_WIDTH + B_Q_WIDTH)), _resident((D_MODEL, D_MODEL))],
        out_specs=[_rows(TM, D_MODEL), _rows(TM, D_MODEL), _rows(TM, D_MODEL), _rows(TM, D_MODEL)],
        out_shape=[jax.ShapeDtypeStruct((t, D_MODEL), F32), jax.ShapeDtypeStruct((t, D_MODEL), BF16),
                   jax.ShapeDtypeStruct((t, D_MODEL), BF16), jax.ShapeDtypeStruct((t, D_MODEL), BF16)],
        compiler_params=_cparams(("parallel",)),
    )(x, oa, ob, gates, proj_t, wout)


def _mix_out_bwd(d, gates, ya, yb, mg, oa, ob, proj_t, wout, hosted=()):
    t = d.shape[0]
    nt = t // TM

    def body(d_ref, gt_ref, ya_ref, yb_ref, mg_ref, oa_ref, ob_ref, pt_ref, wo_ref,
             doa_ref, dob_ref, dgt_ref, gwo_ref, gwp_ref, acc_o, acc_p):
        i = pl.program_id(0)
        db = d_ref[...].astype(BF16)
        dmg = _dot_nt(db, wo_ref[...])
        sa = jax.nn.sigmoid(gt_ref[:, 0:D_MODEL])
        sb = jax.nn.sigmoid(gt_ref[:, D_MODEL:2 * D_MODEL])
        dya = (dmg * sa).astype(BF16)
        dyb = (dmg * sb).astype(BF16)
        dgt_ref[:, 0:D_MODEL] = (dmg * ya_ref[...].astype(F32) * (sa * (1.0 - sa))).astype(BF16)
        dgt_ref[:, D_MODEL:2 * D_MODEL] = (dmg * yb_ref[...].astype(F32) * (sb * (1.0 - sb))).astype(BF16)
        doa_ref[...] = _dot_nn(dya, pt_ref[:, 0:A_WIDTH]).astype(BF16)
        dob_ref[...] = _dot_nn(dyb, pt_ref[:, A_WIDTH:A_WIDTH + B_Q_WIDTH]).astype(BF16)

        @pl.when(i == 0)
        def _():
            acc_o[...] = jnp.zeros_like(acc_o)
            acc_p[...] = jnp.zeros_like(acc_p)

        acc_o[...] += _dot_tn(mg_ref[...], db)
        acc_p[:, 0:A_WIDTH] += _dot_tn(dya, oa_ref[...])
        acc_p[:, A_WIDTH:A_WIDTH + B_Q_WIDTH] += _dot_tn(dyb, ob_ref[...])

        @pl.when(i == nt - 1)
        def _():
            gwo_ref[...] = acc_o[...].astype(BF16)
            gwp_ref[...] = acc_p[...].astype(BF16)

    whole = pl.BlockSpec((D_MODEL, D_MODEL), lambda i: (0, 0))
    return _call(
        body,
        name="mix_out_bwd",
        grid=(nt,),
        in_specs=[_rows(TM, D_MODEL), _rows(TM, 2 * D_MODEL), _rows(TM, D_MODEL), _rows(TM, D_MODEL),
                  _rows(TM, D_MODEL), _rows(TM, A_WIDTH), _rows(TM, B_Q_WIDTH),
                  _resident((D_MODEL, A_WIDTH + B_Q_WIDTH)), _resident((D_MODEL, D_MODEL))],
        out_specs=[_rows(TM, A_WIDTH), _rows(TM, B_Q_WIDTH), _rows(TM, 2 * D_MODEL), whole, whole],
        out_shape=[jax.ShapeDtypeStruct((t, A_WIDTH), BF16), jax.ShapeDtypeStruct((t, B_Q_WIDTH), BF16),
                   jax.ShapeDtypeStruct((t, 2 * D_MODEL), BF16), jax.ShapeDtypeStruct((D_MODEL, D_MODEL), BF16),
                   jax.ShapeDtypeStruct((D_MODEL, D_MODEL), BF16)],
        scratch_shapes=[pltpu.VMEM((D_MODEL, D_MODEL), F32), pltpu.VMEM((D_MODEL, A_WIDTH + B_Q_WIDTH), F32)],
        args=(d, gates, ya, yb, mg, oa, ob, proj_t, wout), sem=("arbitrary",), hosted=hosted)


def _place():
    x, y, c = lax.axis_index("x"), lax.axis_index("y"), lax.axis_index("c")
    chips = [(1 - x, y), (x, 1 - y), (1 - x, 1 - y)]
    return x, y, c, chips


class _Gather:
    per = 8

    def __init__(self, shards):
        n = len(shards)
        self.inputs = list(shards)
        self.out_shape = [jax.ShapeDtypeStruct((N_DEV * s.shape[0], s.shape[1]), s.dtype) for s in shards]
        self.scratch = [pltpu.SemaphoreType.DMA((n * self.per,)), pltpu.SemaphoreType.DMA((n * self.per,)),
                        pltpu.SemaphoreType.DMA((n,))]
        self.result = None

    def _parts(self, ins, outs, sems):
        send_sems, recv_sems, local_sems = sems
        x, y, c, chips = _place()
        me, sibling = (x, y, c), (x, y, 1 - c)
        xn, yn, dg = chips
        n = len(ins)

        def rows(k, p, part=None):
            r = ins[k].shape[0]
            base = (4 * p[0] + 2 * p[1] + p[2]) * r
            if part is None:
                return outs[k].at[pl.ds(base, r), :]
            return outs[k].at[pl.ds(base + part * (r // 2), r // 2), :]

        def copy(k, slot, block, to, src=None, part=None):
            return pltpu.make_async_remote_copy(
                src_ref=rows(k, block, part) if src is None else src, dst_ref=rows(k, block, part),
                send_sem=send_sems.at[k * self.per + slot], recv_sem=recv_sems.at[k * self.per + slot],
                device_id=to, device_id_type=MESH)

        mine = [pltpu.make_async_copy(ins[k], rows(k, me), local_sems.at[k]) for k in range(n)]
        sends, lands = [], []
        for k in range(n):
            sends.append({
                0: copy(k, 0, me, sibling, src=ins[k]),
                1: copy(k, 1, me, (*xn, c), src=ins[k]),
                2: copy(k, 2, me, (*yn, c), src=ins[k]),
                3: copy(k, 3, (*xn, c), (*yn, c), part=0),
                4: copy(k, 4, (*yn, c), (*xn, c), part=1),
                5: copy(k, 5, (*xn, c), sibling),
                6: copy(k, 6, (*yn, c), sibling),
                7: copy(k, 7, (*dg, c), sibling)})
            lands.append({
                0: copy(k, 0, sibling, me),
                1: copy(k, 1, (*xn, c), me),
                2: copy(k, 2, (*yn, c), me),
                3: copy(k, 3, (*dg, c), me, part=0),
                4: copy(k, 4, (*dg, c), me, part=1),
                5: copy(k, 5, (*xn, 1 - c), me),
                6: copy(k, 6, (*yn, 1 - c), me),
                7: copy(k, 7, (*dg, 1 - c), me)})
        return n, mine, sends, lands

    def start(self, ins, outs, sems):
        n, mine, sends, _ = self._parts(ins, outs, sems)
        for cp in mine:
            cp.start()
        for slot in (0, 1, 2):
            for k in range(n):
                sends[k][slot].start()

    def relay(self, ins, outs, sems):
        n, _, sends, lands = self._parts(ins, outs, sems)
        for k in range(n):
            lands[k][1].wait_recv()
            sends[k][3].start()
            sends[k][5].start()
        for k in range(n):
            lands[k][2].wait_recv()
            sends[k][4].start()
            sends[k][6].start()

    def forward(self, ins, outs, sems):
        n, _, sends, lands = self._parts(ins, outs, sems)
        for k in range(n):
            lands[k][3].wait_recv()
            lands[k][4].wait_recv()
            sends[k][7].start()

    def finish(self, ins, outs, sems):
        n, mine, sends, lands = self._parts(ins, outs, sems)
        for k in range(n):
            for slot in (0, 5, 6, 7):
                lands[k][slot].wait_recv()
        for k in range(n):
            for slot in range(self.per):
                sends[k][slot].wait_send()
        for cp in mine:
            cp.wait()


class _PairExchange:
    def __init__(self, grads):
        n = len(grads)
        self.inputs = list(grads)
        self.out_shape = [jax.ShapeDtypeStruct((g.shape[0] // 2, g.shape[1]), g.dtype) for g in grads]
        self.scratch = [pltpu.SemaphoreType.DMA((n * N_CHIP,)), pltpu.SemaphoreType.DMA((n * N_CHIP,))]
        self.result = None

    def _copies(self, ins, outs, sems):
        send_sems, recv_sems = sems
        x, y, c, _ = _place()
        copies = []
        for k in range(len(ins)):
            r = ins[k].shape[0] // N_DEV
            for q in range(N_CHIP):
                copies.append(pltpu.make_async_remote_copy(
                    src_ref=ins[k].at[pl.ds((2 * q + 1 - c) * r, r), :], dst_ref=outs[k].at[pl.ds(q * r, r), :],
                    send_sem=send_sems.at[k * N_CHIP + q], recv_sem=recv_sems.at[k * N_CHIP + q],
                    device_id=(x, y, 1 - c), device_id_type=MESH))
        return copies

    def start(self, ins, outs, sems):
        for cp in self._copies(ins, outs, sems):
            cp.start()

    def relay(self, ins, outs, sems):
        pass

    def forward(self, ins, outs, sems):
        pass

    def finish(self, ins, outs, sems):
        copies = self._copies(ins, outs, sems)
        for cp in copies:
            cp.wait_recv()
        for cp in copies:
            cp.wait_send()


class _ChipExchange(_PairExchange):
    def __init__(self, psums):
        n = len(psums)
        self.inputs = list(psums)
        self.out_shape = [jax.ShapeDtypeStruct((3 * p.shape[0] // N_CHIP, p.shape[1]), p.dtype) for p in psums]
        self.scratch = [pltpu.SemaphoreType.DMA((n * 3,)), pltpu.SemaphoreType.DMA((n * 3,))]
        self.result = None

    def _copies(self, ins, outs, sems):
        send_sems, recv_sems = sems
        _, _, c, chips = _place()
        copies = []
        for k in range(len(ins)):
            r = ins[k].shape[0] // N_CHIP
            for j, chip in enumerate(chips):
                copies.append(pltpu.make_async_remote_copy(
                    src_ref=ins[k].at[pl.ds((2 * chip[0] + chip[1]) * r, r), :], dst_ref=outs[k].at[pl.ds(j * r, r), :],
                    send_sem=send_sems.at[k * 3 + j], recv_sem=recv_sems.at[k * 3 + j],
                    device_id=(*chip, c), device_id_type=MESH))
        return copies


def _exchange_alone(xchg, name):
    n_in, n_out = len(xchg.inputs), len(xchg.out_shape)

    def body(*refs):
        ins, outs, sems = refs[:n_in], refs[n_in:n_in + n_out], refs[n_in + n_out:]
        xchg.start(ins, outs, sems)
        xchg.relay(ins, outs, sems)
        xchg.forward(ins, outs, sems)
        xchg.finish(ins, outs, sems)

    xchg.result = list(pl.pallas_call(
        body, name=name, in_specs=[_hbm()] * n_in, out_specs=[_hbm()] * n_out, out_shape=xchg.out_shape,
        scratch_shapes=xchg.scratch)(*xchg.inputs))
    return xchg.result


def _pair_sum(core, grads, recvd, name):
    n = len(grads)
    r = grads[0].shape[0] // N_DEV
    cdim = grads[0].shape[1]
    tr = r // 2 if r % 32 == 0 else r
    nt = r // tr

    def body(core_ref, *refs):
        del core_ref
        for k in range(n):
            refs[2 * n + k][...] = (refs[k][...].astype(F32) + refs[n + k][...].astype(F32)).astype(BF16)

    gspec = pl.BlockSpec((tr, cdim), lambda q, i, core_ref: ((2 * q + core_ref[0]) * nt + i, 0))
    rspec = pl.BlockSpec((tr, cdim), lambda q, i, core_ref: (q * nt + i, 0))
    return pl.pallas_call(
        body,
        name=name,
        grid_spec=pltpu.PrefetchScalarGridSpec(
            num_scalar_prefetch=1, grid=(N_CHIP, nt), in_specs=[gspec] * n + [rspec] * n, out_specs=[rspec] * n),
        out_shape=[jax.ShapeDtypeStruct((N_CHIP * r, cdim), BF16) for _ in range(n)],
        compiler_params=_cparams(("parallel", "parallel")),
    )(core, *grads, *recvd)


def _final_sum(chip, psums, recvd, name):
    n = len(psums)
    r = psums[0].shape[0] // N_CHIP
    cdim = psums[0].shape[1]
    tr = r // 2 if r % 32 == 0 else r
    nt = r // tr

    def body(chip_ref, *refs):
        del chip_ref
        for k in range(n):
            got = refs[n + k]
            tot = refs[k][...].astype(F32) + got[0].astype(F32)
            tot = tot + got[1].astype(F32)
            tot = tot + got[2].astype(F32)
            refs[2 * n + k][...] = tot

    pspec = pl.BlockSpec((tr, cdim), lambda i, chip_ref: (chip_ref[0] * nt + i, 0))
    rspec = pl.BlockSpec((3, tr, cdim), lambda i, chip_ref: (0, i, 0))
    ospec = pl.BlockSpec((tr, cdim), lambda i, chip_ref: (i, 0))
    return pl.pallas_call(
        body,
        name=name,
        grid_spec=pltpu.PrefetchScalarGridSpec(
            num_scalar_prefetch=1, grid=(nt,), in_specs=[pspec] * n + [rspec] * n, out_specs=[ospec] * n),
        out_shape=[jax.ShapeDtypeStruct((r, cdim), F32) for _ in range(n)],
        compiler_params=_cparams(("parallel",)),
    )(chip, *psums, *[g.reshape(3, r, cdim) for g in recvd])


SMALL_ROWS = 16


def _all_reduce_small(part):
    def body(p_ref, o_ref, buf, send_sems, recv_sems):
        x, y, c, _ = _place()
        me = 4 * x + 2 * y + c
        buf[me] = p_ref[...]
        copies = []
        for d in range(1, N_DEV):
            peer = me ^ d
            copies.append(pltpu.make_async_remote_copy(
                src_ref=p_ref, dst_ref=buf.at[me], send_sem=send_sems.at[d - 1], recv_sem=recv_sems.at[d - 1],
                device_id=(peer // 4, (peer // 2) % 2, peer % 2), device_id_type=MESH))
        for cp in copies:
            cp.start()
        for cp in copies:
            cp.wait_recv()
        for cp in copies:
            cp.wait_send()
        tot = buf[0]
        for d in range(1, N_DEV):
            tot = tot + buf[d]
        o_ref[...] = tot

    return pl.pallas_call(
        body,
        name="all_reduce_small",
        in_specs=[pl.BlockSpec(memory_space=pltpu.VMEM)],
        out_specs=pl.BlockSpec(memory_space=pltpu.VMEM),
        out_shape=jax.ShapeDtypeStruct(part.shape, F32),
        scratch_shapes=[pltpu.VMEM((N_DEV,) + part.shape, F32), pltpu.SemaphoreType.DMA((N_DEV - 1,)),
                        pltpu.SemaphoreType.DMA((N_DEV - 1,))],
    )(part)


ADAMW_STEPS = 4


def _adamw(ws, gs, ms, vs, name, hosted=()):
    n = len(ws)
    steps = ADAMW_STEPS if all(w.shape[0] % (8 * ADAMW_STEPS) == 0 for w in ws) else 1
    c1 = 1.0 - ADAM_B1 ** ADAM_STEP
    c2 = 1.0 - ADAM_B2 ** ADAM_STEP

    def body(*refs):
        for k in range(n):
            w, g, m, v = (refs[j * n + k][...] for j in range(4))
            m2 = ADAM_B1 * m + (1.0 - ADAM_B1) * g
            v2 = ADAM_B2 * v + (1.0 - ADAM_B2) * (g * g)
            delta = -ADAM_LR * ((m2 * (1.0 / c1)) / (jnp.sqrt(v2 * (1.0 / c2)) + ADAM_EPS) + ADAM_WD * w)
            refs[4 * n + k][...] = delta
            refs[5 * n + k][...] = m2
            refs[6 * n + k][...] = v2

    specs = [pl.BlockSpec((w.shape[0] // steps, w.shape[1]), lambda i: (i, 0)) for w in ws]
    shapes = [jax.ShapeDtypeStruct(w.shape, F32) for w in ws]
    outs = _call(
        body,
        name=name,
        grid=(steps,),
        in_specs=specs * 4,
        out_specs=specs * 3,
        out_shape=shapes * 3,
        args=(*ws, *gs, *ms, *vs), sem=("parallel",), hosted=hosted)
    return outs[:n], outs[n:2 * n], outs[2 * n:]


def _adamw_reduced(chip, ws, psums, recvd, ms, vs, steps, name):
    n = len(ws)
    c1 = 1.0 - ADAM_B1 ** ADAM_STEP
    c2 = 1.0 - ADAM_B2 ** ADAM_STEP

    def body(chip_ref, *refs):
        del chip_ref
        for k in range(n):
            w, m, v = (refs[j * n + k][...] for j in (0, 3, 4))
            got = refs[2 * n + k]
            g = refs[n + k][...].astype(F32) + got[0].astype(F32)
            g = g + got[1].astype(F32)
            g = g + got[2].astype(F32)
            m2 = ADAM_B1 * m + (1.0 - ADAM_B1) * g
            v2 = ADAM_B2 * v + (1.0 - ADAM_B2) * (g * g)
            refs[5 * n + k][...] = g
            refs[6 * n + k][...] = -ADAM_LR * (
                (m2 * (1.0 / c1)) / (jnp.sqrt(v2 * (1.0 / c2)) + ADAM_EPS) + ADAM_WD * w)
            refs[7 * n + k][...] = m2
            refs[8 * n + k][...] = v2

    def blk(w):
        return (w.shape[0] // steps, w.shape[1])

    own = [pl.BlockSpec(blk(w), lambda i, chip_ref: (i, 0)) for w in ws]
    psum = [pl.BlockSpec(blk(w), lambda i, chip_ref: (chip_ref[0] * steps + i, 0)) for w in ws]
    recv = [pl.BlockSpec((3,) + blk(w), lambda i, chip_ref: (0, i, 0)) for w in ws]
    shapes = [jax.ShapeDtypeStruct(w.shape, F32) for w in ws]
    outs = pl.pallas_call(
        body,
        name=name,
        grid_spec=pltpu.PrefetchScalarGridSpec(
            num_scalar_prefetch=1, grid=(steps,), in_specs=own + psum + recv + own + own, out_specs=own * 4),
        out_shape=shapes * 4,
        compiler_params=_cparams(("parallel",)),
    )(chip, *ws, *psums, *[r.reshape((3,) + w.shape) for r, w in zip(recvd, ws)], *ms, *vs)
    return outs[:n], outs[n:2 * n], outs[2 * n:3 * n], outs[3 * n:]


def _bias_b():
    pad = B_PREV * CHUNK
    slopes = np.array([2.0 ** (-8.0 * (i + 1) / B_Q_HEADS) for i in range(B_Q_HEADS)], dtype=np.float32)
    dist = np.abs(np.arange(TQ)[:, None] - np.arange(TQ + pad)[None, :] + pad).astype(np.float32)
    bias = -slopes.reshape(B_Q_HEADS, 1, 1) * dist[None]
    qc = (np.arange(TQ)[:, None] + pad) // CHUNK
    kc = np.arange(TQ + pad)[None, :] // CHUNK
    allowed = (kc <= qc) & (kc >= qc - B_PREV)
    return np.where(allowed[None], bias, np.float32(NEG_INF)).astype(np.float32)


def kernel(x, ffn1_norm, ffn1_w_gate, ffn1_w_up, ffn1_w_down, mix_norm, w_in, rel_bias, sinks, w_proj_a, w_proj_b, w_out, ffn2_norm, ffn2_w_gate, ffn2_w_up, ffn2_w_down, final_norm, loss_target, m_ffn1_norm, m_ffn1_w_gate, m_ffn1_w_up, m_ffn1_w_down, m_mix_norm, m_w_in, m_rel_bias, m_sinks, m_w_proj_a, m_w_proj_b, m_w_out, m_ffn2_norm, m_ffn2_w_gate, m_ffn2_w_up, m_ffn2_w_down, m_final_norm, v_ffn1_norm, v_ffn1_w_gate, v_ffn1_w_up, v_ffn1_w_down, v_mix_norm, v_w_in, v_rel_bias, v_sinks, v_w_proj_a, v_w_proj_b, v_w_out, v_ffn2_norm, v_ffn2_w_gate, v_ffn2_w_up, v_ffn2_w_down, v_final_norm):
    bsz, s_len, _ = x.shape
    t = bsz * s_len
    core = lax.axis_index("c").astype(jnp.int32).reshape(1)
    chip = (2 * lax.axis_index("x") + lax.axis_index("y")).astype(jnp.int32).reshape(1)

    proj_rows = jnp.concatenate([w_proj_a.T, w_proj_b.T], axis=1)
    sh_g1, sh_u1, sh_d1, sh_in, sh_proj, sh_out, sh_g2, sh_u2, sh_d2 = _to_bf16(
        [ffn1_w_gate.T, ffn1_w_up.T, ffn1_w_down, w_in.T, proj_rows, w_out, ffn2_w_gate.T, ffn2_w_up.T, ffn2_w_down],
        "weights_to_bf16")

    gather_up1 = _Gather([sh_g1, sh_u1])
    far = jnp.broadcast_to(rel_bias[:, REL_TABLE - 1:REL_TABLE], (A_HEADS, REL_WRAP // 2))
    tv = jnp.concatenate([far, jnp.flip(rel_bias, axis=1), jnp.zeros((A_HEADS, REL_WRAP // 2 - REL_TABLE), F32)], axis=1)
    bias_a = _bias_a_build(tv.reshape(A_HEADS, 1, REL_WRAP), hosted=[gather_up1])
    wg1, wu1 = gather_up1.result
    gather_down1 = _Gather([sh_d1, sh_in])
    gather_out = _Gather([sh_proj, sh_out])
    gather_ffn2_gate = _Gather([sh_g2])
    gather_ffn2_rest = _Gather([sh_u2, sh_d2])

    x0 = x.reshape(t, D_MODEL)
    tgt = loss_target.reshape(t, D_MODEL)
    gam1, gam2, gam3, gam4 = (g.reshape(1, D_MODEL) for g in (ffn1_norm, mix_norm, ffn2_norm, final_norm))

    h1, g1, u1, a1 = _ffn_up(x0, gam1, wg1, wu1, "ffn1_up", hosted=[gather_down1])
    wd1, win_t = gather_down1.result
    x1 = _ffn_down(x0, a1, wd1, "ffn1_down", hosted=[gather_out])
    proj_t, wout = gather_out.result
    h2, qkv_a, qkv_b, gates = _proj_fwd(x1, gam2, win_t, hosted=[gather_ffn2_gate])
    (wg2,) = gather_ffn2_gate.result
    qkv_a3 = qkv_a.reshape(bsz, s_len, QKV_A)
    qkv_b3 = qkv_b.reshape(bsz, s_len, QKV_B)

    bias_b = jnp.asarray(_bias_b())
    sink_rows = jnp.broadcast_to(sinks.reshape(B_Q_HEADS, 1, 1), (B_Q_HEADS, 8, LANES))

    oa, lse_a = _attn_a_fwd(qkv_a3, bias_a, hosted=[gather_ffn2_rest])
    oa = oa.reshape(t, A_WIDTH)
    wu2, wd2 = gather_ffn2_rest.result
    ob, lse_b = _attn_b_fwd(qkv_b3, bias_b, sink_rows)
    ob = ob.reshape(t, B_Q_WIDTH)
    x2, ya, yb, mg = _mix_out_fwd(x1, oa, ob, gates, proj_t, wout)
    h3, g2, u2, a2, x3 = _ffn_fwd(x2, gam3, wg2, wu2, wd2, "ffn2_fwd")

    dx2, dg2, du2, db2, dgam3, dgam4, loss_part = _ffn_bwd_head(x3, gam4, tgt, x2, gam3, g2, u2, wg2, wu2, wd2,
                                                                "ffn2_bwd")
    gw_ffn2 = [_mm_tn([dg2], h3, "grad_ffn2_gate"), _mm_tn([du2], h3, "grad_ffn2_up"),
               _mm_tn([a2], db2, "grad_ffn2_down")]
    pairx_ffn2 = _PairExchange(gw_ffn2)
    doa, dob, dgates, gw_out, gw_proj = _mix_out_bwd(dx2, gates, ya, yb, mg, oa, ob, proj_t, wout,
                                                     hosted=[pairx_ffn2])
    psum_ffn2 = _pair_sum(core, gw_ffn2, pairx_ffn2.result, "pair_sum_ffn2")

    chipx_ffn2 = _ChipExchange(psum_ffn2)
    dqa, dka, dva, dbias_a = _attn_a_bwd(qkv_a3, bias_a, doa.reshape(bsz, s_len, A_WIDTH),
                                         oa.reshape(bsz, s_len, A_WIDTH), lse_a, hosted=[chipx_ffn2])
    pairx_out = _PairExchange([gw_proj, gw_out])
    dqb, dkvb, dsink = _attn_b_bwd(qkv_b3, bias_b, sink_rows, dob.reshape(bsz, s_len, B_Q_WIDTH),
                                   ob.reshape(bsz, s_len, B_Q_WIDTH), lse_b, hosted=[pairx_out])
    drel_lanes = _relbias_grad(dbias_a)
    dproj = [dqa.reshape(t, A_WIDTH), dka.reshape(t, A_WIDTH), dva.reshape(t, A_WIDTH), dqb.reshape(t, B_Q_WIDTH),
             dkvb.reshape(t, 2 * B_KV_WIDTH), dgates]

    gw_in = _mm_tn(dproj, h2, "grad_w_in")
    pairx_in = _PairExchange([gw_in])
    psum_out = _pair_sum(core, [gw_proj, gw_out], pairx_out.result, "pair_sum_mix")
    chipx_out = _ChipExchange(psum_out)
    dx1, db1, dgam2 = _proj_bwd(dx2, x1, gam2, dproj, win_t, hosted=[pairx_in, chipx_out])
    psum_in = _pair_sum(core, [gw_in], pairx_in.result, "pair_sum_w_in")
    gw_d1 = _mm_tn([a1], db1, "grad_ffn1_down")

    chipx_in = _ChipExchange(psum_in)
    pairx_d1 = _PairExchange([gw_d1])
    dg1, du1 = _ffn_bwd_act(dx1, g1, u1, wd1, "ffn1_bwd_act", hosted=[chipx_in, pairx_d1])
    psum_d1 = _pair_sum(core, [gw_d1], pairx_d1.result, "pair_sum_ffn1_down")
    chipx_d1 = _ChipExchange(psum_d1)
    gw_g1 = _mm_tn([dg1], h1, "grad_ffn1_gate", hosted=[chipx_d1])
    from_sibling_g1 = _exchange_alone(_PairExchange([gw_g1]), "pair_exchange_ffn1_gate")
    psum_g1 = _pair_sum(core, [gw_g1], from_sibling_g1, "pair_sum_ffn1_gate")
    chipx_g1 = _ChipExchange(psum_g1)
    gw_u1 = _mm_tn([du1], h1, "grad_ffn1_up", hosted=[chipx_g1])
    from_sibling_u1 = _exchange_alone(_PairExchange([gw_u1]), "pair_exchange_ffn1_up")
    psum_u1 = _pair_sum(core, [gw_u1], from_sibling_u1, "pair_sum_ffn1_up")
    chipx_u1 = _ChipExchange(psum_u1)
    dx0, dgam1 = _ffn_bwd_in(dx1, x0, gam1, dg1, du1, wg1, wu1, "ffn1_bwd_in", hosted=[chipx_u1])

    (g_proj,) = _final_sum(chip, psum_out[0:1], chipx_out.result[0:1], "grad_sum_proj")
    grads = {"w_proj_a": g_proj[:, 0:A_WIDTH].T, "w_proj_b": g_proj[:, A_WIDTH:].T}

    def row_of(v):
        return jnp.pad(v.reshape(1, -1), ((0, 0), (0, D_MODEL - v.size)))

    def table_rows(v):
        return jnp.pad(v, ((0, 0), (0, D_MODEL - REL_TABLE)))

    drel_local = jnp.flip(drel_lanes[:, 0, 0:REL_TABLE], axis=1)
    small_part = jnp.concatenate(
        [jnp.sum(dgam1, axis=0, keepdims=True), jnp.sum(dgam2, axis=0, keepdims=True),
         jnp.sum(dgam3, axis=0, keepdims=True), jnp.sum(dgam4, axis=0, keepdims=True),
         row_of(jnp.sum(loss_part)), row_of(dsink[:, 0, 0]), jnp.zeros((2, D_MODEL), F32),
         table_rows(drel_local)], axis=0)
    small = _all_reduce_small(small_part)
    loss = small[4, 0]

    def pack(n1, n2, n3, n4, sk, tb):
        return jnp.concatenate([n1.reshape(1, -1), n2.reshape(1, -1), n3.reshape(1, -1), n4.reshape(1, -1),
                                jnp.zeros((1, D_MODEL), F32), row_of(sk), jnp.zeros((2, D_MODEL), F32), table_rows(tb)],
                               axis=0)

    live = np.zeros((SMALL_ROWS, D_MODEL), np.float32)
    live[0:4] = 1.0
    live[5, 0:B_Q_HEADS] = 1.0
    live[8:16, 0:REL_TABLE] = 1.0
    small_g = small * jnp.asarray(live)
    sw = pack(ffn1_norm, mix_norm, ffn2_norm, final_norm, sinks, rel_bias)
    sm = pack(m_ffn1_norm, m_mix_norm, m_ffn2_norm, m_final_norm, m_sinks, m_rel_bias)
    sv = pack(v_ffn1_norm, v_mix_norm, v_ffn2_norm, v_final_norm, v_sinks, v_rel_bias)
    (sd,), (snm,), (snv,) = _adamw([sw], [small_g], [sm], [sv], "adamw_small")

    def unpack(p):
        return {"ffn1_norm": p[0], "mix_norm": p[1], "ffn2_norm": p[2], "final_norm": p[3],
                "sinks": p[5, 0:B_Q_HEADS], "rel_bias": p[8:16, 0:REL_TABLE]}

    grads.update(unpack(small_g))
    delta, new_m, new_v = unpack(sd), unpack(snm), unpack(snv)

    wmv = {
        "ffn1_w_gate": (ffn1_w_gate, m_ffn1_w_gate, v_ffn1_w_gate), "ffn1_w_up": (ffn1_w_up, m_ffn1_w_up, v_ffn1_w_up),
        "ffn1_w_down": (ffn1_w_down, m_ffn1_w_down, v_ffn1_w_down), "w_in": (w_in, m_w_in, v_w_in),
        "w_proj_a": (w_proj_a, m_w_proj_a, v_w_proj_a), "w_proj_b": (w_proj_b, m_w_proj_b, v_w_proj_b),
        "w_out": (w_out, m_w_out, v_w_out),
        "ffn2_w_gate": (ffn2_w_gate, m_ffn2_w_gate, v_ffn2_w_gate), "ffn2_w_up": (ffn2_w_up, m_ffn2_w_up, v_ffn2_w_up),
        "ffn2_w_down": (ffn2_w_down, m_ffn2_w_down, v_ffn2_w_down),
    }
    row_form_names = ("ffn1_w_gate", "ffn1_w_up", "w_in", "ffn2_w_gate", "ffn2_w_up")

    def form(n, a):
        return a.T if n in row_form_names else a

    def reduced_group(gname, names, psums, recvd, steps):
        gs_, ds_, ms_, vs_ = _adamw_reduced(
            chip, [form(n, wmv[n][0]) for n in names], psums, recvd, [form(n, wmv[n][1]) for n in names],
            [form(n, wmv[n][2]) for n in names], steps, gname)
        for n, g_, d_, m_, v_ in zip(names, gs_, ds_, ms_, vs_):
            grads[n], delta[n], new_m[n], new_v[n] = form(n, g_), form(n, d_), form(n, m_), form(n, v_)

    reduced_group("adamw_ffn", ["ffn1_w_gate", "ffn1_w_up", "ffn1_w_down", "ffn2_w_gate", "ffn2_w_up", "ffn2_w_down"],
                  psum_g1 + psum_u1 + psum_d1 + psum_ffn2,
                  chipx_g1.result + chipx_u1.result + chipx_d1.result + chipx_ffn2.result, 11)
    reduced_group("adamw_in_out", ["w_in", "w_out"], psum_in + psum_out[1:2], chipx_in.result + chipx_out.result[1:2], 2)
    names = ["w_proj_a", "w_proj_b"]
    ds_, ms_, vs_ = _adamw([wmv[n][0] for n in names], [grads[n] for n in names], [wmv[n][1] for n in names],
                           [wmv[n][2] for n in names], "adamw_proj")
    for n, d_, m_, v_ in zip(names, ds_, ms_, vs_):
        delta[n], new_m[n], new_v[n] = d_, m_, v_

    order = ["ffn1_norm", "ffn1_w_gate", "ffn1_w_up", "ffn1_w_down", "mix_norm", "w_in", "rel_bias", "sinks",
             "w_proj_a", "w_proj_b", "w_out", "ffn2_norm", "ffn2_w_gate", "ffn2_w_up", "ffn2_w_down", "final_norm"]
    grad_x = dx0.reshape(bsz, s_len, D_MODEL)
    return (loss, grad_x, *[grads[n] for n in order], *[delta[n] for n in order], *[new_m[n] for n in order],
            *[new_v[n] for n in order])
```

```python
import numpy as np
import jax
import jax.numpy as jnp
from jax import lax
from jax.experimental import pallas as pl
from jax.experimental.pallas import tpu as pltpu

F32 = jnp.float32
BF16 = jnp.bfloat16

D_MODEL = 1024
D_FF = 2816
CHUNK = 64
D_HEAD = 64
A_HEADS = 8
A_PREV = 8
MAX_REL = 128
B_Q_HEADS = 8
B_KV_HEADS = 2
B_GROUP = B_Q_HEADS // B_KV_HEADS
B_PREV = 2
REL_TABLE = (CHUNK - 1) + MAX_REL + 1
A_WIDTH = A_HEADS * D_HEAD
B_Q_WIDTH = B_Q_HEADS * D_HEAD
B_KV_WIDTH = B_KV_HEADS * D_HEAD
QKV_A = 3 * A_WIDTH
QKV_B = B_Q_WIDTH + 2 * B_KV_WIDTH
IN_WIDTH = QKV_A + QKV_B + 2 * D_MODEL
EPS = 1e-6
NEG_INF = -1e30
SCALE = 1.0 / 8.0

ADAM_LR = 0.001
ADAM_B1 = 0.9
ADAM_B2 = 0.999
ADAM_EPS = 1e-08
ADAM_WD = 0.01
ADAM_STEP = 10

N_DEV = 8
N_CHIP = 4
MESH = pl.DeviceIdType.MESH

LANES = 128
TQ = 256
TM = 256
FC = 256
VMEM_LIMIT = 56 << 20


def _cparams(sem, vmem=VMEM_LIMIT):
    return pltpu.CompilerParams(dimension_semantics=sem, vmem_limit_bytes=vmem)


def _dot_nt(a, b):
    return lax.dot_general(a, b, (((1,), (1,)), ((), ())), preferred_element_type=F32)


def _dot_nn(a, b):
    return lax.dot_general(a, b, (((1,), (0,)), ((), ())), preferred_element_type=F32)


def _dot_tn(a, b):
    return lax.dot_general(a, b, (((0,), (0,)), ((), ())), preferred_element_type=F32)


def _resident(shape):
    nd = len(shape)
    return pl.BlockSpec(shape, lambda *_: (0,) * nd, pipeline_mode=pl.Buffered(1))


def _rows(tm, width):
    return pl.BlockSpec((tm, width), lambda i: (i, 0))


def _colsum8(v):
    tm, n = v.shape
    return jnp.sum(v.reshape(tm // 8, 8, n), axis=0)


def _rms(x):
    r = lax.rsqrt(jnp.mean(x * x, axis=-1, keepdims=True) + EPS)
    return x * r, r


def _rms_bwd(dh, xh, r, gamma):
    dxh = dh * gamma
    dx = r * (dxh - xh * jnp.mean(dxh * xh, axis=-1, keepdims=True))
    return dx, _colsum8(dh * xh)


def _hbm():
    return pl.BlockSpec(memory_space=pltpu.HBM)


def _call(body, *, name, grid, in_specs, out_specs, out_shape, args, sem, scratch_shapes=(), hosted=()):
    in_specs, out_specs, out_shape = list(in_specs), list(out_specs), list(out_shape)
    scratch_shapes = list(scratch_shapes)
    if not hosted:
        return pl.pallas_call(body, name=name, grid=grid, in_specs=in_specs, out_specs=out_specs, out_shape=out_shape,
                              scratch_shapes=scratch_shapes, compiler_params=_cparams(sem))(*args)
    n_in, n_out, n_scr = len(in_specs), len(out_specs), len(scratch_shapes)
    x_in = [a for x in hosted for a in x.inputs]
    x_out = [s for x in hosted for s in x.out_shape]
    x_scr = [s for x in hosted for s in x.scratch]
    steps = int(np.prod(grid))
    forward_step = max(steps - 3, 0)
    relay_step = min((5 * steps) // 8, forward_step)

    def wrapped(*refs):
        pos = [0]

        def take(k):
            pos[0] += k
            return refs[pos[0] - k:pos[0]]

        ins, xin, outs, xout, scr, xscr = (take(k) for k in (n_in, len(x_in), n_out, len(x_out), n_scr, len(x_scr)))
        step = 0
        for axis, extent in enumerate(grid):
            step = step * extent + pl.program_id(axis)
        own, oi, oo, osc = [], 0, 0, 0
        for x in hosted:
            own.append((xin[oi:oi + len(x.inputs)], xout[oo:oo + len(x.out_shape)], xscr[osc:osc + len(x.scratch)]))
            oi, oo, osc = oi + len(x.inputs), oo + len(x.out_shape), osc + len(x.scratch)

        def phase(method):
            for x, (i_, o_, s_) in zip(hosted, own):
                getattr(x, method)(i_, o_, s_)

        pl.when(step == 0)(lambda: phase("start"))
        body(*ins, *outs, *scr)
        pl.when(step == relay_step)(lambda: phase("relay"))
        pl.when(step == forward_step)(lambda: phase("forward"))
        pl.when(step == steps - 1)(lambda: phase("finish"))

    res = pl.pallas_call(
        wrapped, name=name, grid=grid, in_specs=in_specs + [_hbm()] * len(x_in),
        out_specs=out_specs + [_hbm()] * len(x_out), out_shape=out_shape + x_out,
        scratch_shapes=scratch_shapes + x_scr, compiler_params=_cparams(("arbitrary",) * len(grid)))(*args, *x_in)
    rest = list(res[n_out:])
    for x in hosted:
        x.result, rest = rest[:len(x.out_shape)], rest[len(x.out_shape):]
    return list(res[:n_out])


def _to_bf16(arrays, name):
    n = len(arrays)

    def body(*refs):
        for k in range(n):
            refs[n + k][...] = refs[k][...].astype(BF16)

    specs = [pl.BlockSpec(a.shape, lambda i: (0, 0)) for a in arrays]
    return pl.pallas_call(
        body, name=name, grid=(1,), in_specs=specs, out_specs=specs,
        out_shape=[jax.ShapeDtypeStruct(a.shape, BF16) for a in arrays],
        compiler_params=_cparams(("arbitrary",)))(*arrays)


def _ffn_fwd(x, gamma, wg_t, wu_t, wd, name, hosted=()):
    t = x.shape[0]
    f = wg_t.shape[0]

    def body(x_ref, gam_ref, wg_ref, wu_ref, wd_ref, h_ref, g_ref, u_ref, a_ref, y_ref):
        xv = x_ref[...]
        xh, _ = _rms(xv)
        h = (xh * gam_ref[...]).astype(BF16)
        h_ref[...] = h
        for j in range(f // FC):
            sl = slice(j * FC, (j + 1) * FC)
            g = _dot_nt(h, wg_ref[sl, :])
            u = _dot_nt(h, wu_ref[sl, :])
            g_ref[:, sl] = g.astype(BF16)
            u_ref[:, sl] = u.astype(BF16)
            a_ref[:, sl] = (g * jax.nn.sigmoid(g) * u).astype(BF16)
        y_ref[...] = xv + 0.5 * _dot_nn(a_ref[...], wd_ref[...])

    return _call(
        body,
        name=name,
        grid=(t // TM,),
        in_specs=[_rows(TM, D_MODEL), _resident((1, D_MODEL)), _resident((f, D_MODEL)), _resident((f, D_MODEL)),
                  _resident((f, D_MODEL))],
        out_specs=[_rows(TM, D_MODEL), _rows(TM, f), _rows(TM, f), _rows(TM, f), _rows(TM, D_MODEL)],
        out_shape=[jax.ShapeDtypeStruct((t, D_MODEL), BF16), jax.ShapeDtypeStruct((t, f), BF16),
                   jax.ShapeDtypeStruct((t, f), BF16), jax.ShapeDtypeStruct((t, f), BF16),
                   jax.ShapeDtypeStruct((t, D_MODEL), F32)],
        args=(x, gamma, wg_t, wu_t, wd), sem=("parallel",), hosted=hosted)


def _ffn_up(x, gamma, wg_t, wu_t, name, hosted=()):
    t = x.shape[0]
    f = wg_t.shape[0]

    def body(x_ref, gam_ref, wg_ref, wu_ref, h_ref, g_ref, u_ref, a_ref):
        xh, _ = _rms(x_ref[...])
        h = (xh * gam_ref[...]).astype(BF16)
        h_ref[...] = h
        for j in range(f // FC):
            sl = slice(j * FC, (j + 1) * FC)
            g = _dot_nt(h, wg_ref[sl, :])
            u = _dot_nt(h, wu_ref[sl, :])
            g_ref[:, sl] = g.astype(BF16)
            u_ref[:, sl] = u.astype(BF16)
            a_ref[:, sl] = (g * jax.nn.sigmoid(g) * u).astype(BF16)

    return _call(
        body,
        name=name,
        grid=(t // TM,),
        in_specs=[_rows(TM, D_MODEL), _resident((1, D_MODEL)), _resident((f, D_MODEL)), _resident((f, D_MODEL))],
        out_specs=[_rows(TM, D_MODEL), _rows(TM, f), _rows(TM, f), _rows(TM, f)],
        out_shape=[jax.ShapeDtypeStruct((t, D_MODEL), BF16), jax.ShapeDtypeStruct((t, f), BF16),
                   jax.ShapeDtypeStruct((t, f), BF16), jax.ShapeDtypeStruct((t, f), BF16)],
        args=(x, gamma, wg_t, wu_t), sem=("parallel",), hosted=hosted)


def _ffn_down(x, a_act, wd, name, hosted=()):
    t = x.shape[0]
    f = wd.shape[0]

    def body(x_ref, a_ref, wd_ref, y_ref):
        y_ref[...] = x_ref[...] + 0.5 * _dot_nn(a_ref[...], wd_ref[...])

    return _call(
        body,
        name=name,
        grid=(t // TM,),
        in_specs=[_rows(TM, D_MODEL), _rows(TM, f), _resident((f, D_MODEL))],
        out_specs=[_rows(TM, D_MODEL)],
        out_shape=[jax.ShapeDtypeStruct((t, D_MODEL), F32)],
        args=(x, a_act, wd), sem=("parallel",), hosted=hosted)[0]


def _ffn_bwd_head(y, gamma_f, target, x, gamma, g_act, u_act, wg_t, wu_t, wd, name):
    t = x.shape[0]
    f = wg_t.shape[0]

    def body(y_ref, gamf_ref, t_ref, x_ref, gam_ref, g_ref, u_ref, wg_ref, wu_ref, wd_ref, dx_ref, dg_ref, du_ref,
             db_ref, dgam_ref, dgamf_ref, loss_ref):
        yh, ry = _rms(y_ref[...])
        gam_f = gamf_ref[...]
        e = yh * gam_f - t_ref[...]
        dv, dgam_f = _rms_bwd(e * (1.0 / D_MODEL), yh, ry, gam_f)
        db = (0.5 * dv).astype(BF16)
        db_ref[...] = db
        for j in range(f // FC):
            sl = slice(j * FC, (j + 1) * FC)
            da = _dot_nt(db, wd_ref[sl, :])
            g = g_ref[:, sl].astype(F32)
            u = u_ref[:, sl].astype(F32)
            s = jax.nn.sigmoid(g)
            dg_ref[:, sl] = (da * u * (s * (1.0 + g * (1.0 - s)))).astype(BF16)
            du_ref[:, sl] = (da * (g * s)).astype(BF16)
        dh = _dot_nn(dg_ref[...], wg_ref[...]) + _dot_nn(du_ref[...], wu_ref[...])
        xh, r = _rms(x_ref[...])
        dxn, dgam = _rms_bwd(dh, xh, r, gam_ref[...])
        dx_ref[...] = dv + dxn

        @pl.when(pl.program_id(0) == 0)
        def _():
            dgam_ref[...] = jnp.zeros_like(dgam_ref)
            dgamf_ref[...] = jnp.zeros_like(dgamf_ref)
            loss_ref[...] = jnp.zeros_like(loss_ref)

        dgam_ref[...] += dgam
        dgamf_ref[...] += dgam_f
        loss_ref[...] += _colsum8(e * e) * (0.5 / D_MODEL)

    acc = pl.BlockSpec((8, D_MODEL), lambda i: (0, 0))
    return _call(
        body,
        name=name,
        grid=(t // TM,),
        in_specs=[_rows(TM, D_MODEL), _resident((1, D_MODEL)), _rows(TM, D_MODEL), _rows(TM, D_MODEL),
                  _resident((1, D_MODEL)), _rows(TM, f), _rows(TM, f),
                  _resident((f, D_MODEL)), _resident((f, D_MODEL)), _resident((f, D_MODEL))],
        out_specs=[_rows(TM, D_MODEL), _rows(TM, f), _rows(TM, f), _rows(TM, D_MODEL), acc, acc, acc],
        out_shape=[jax.ShapeDtypeStruct((t, D_MODEL), F32), jax.ShapeDtypeStruct((t, f), BF16),
                   jax.ShapeDtypeStruct((t, f), BF16), jax.ShapeDtypeStruct((t, D_MODEL), BF16),
                   jax.ShapeDtypeStruct((8, D_MODEL), F32), jax.ShapeDtypeStruct((8, D_MODEL), F32),
                   jax.ShapeDtypeStruct((8, D_MODEL), F32)],
        args=(y, gamma_f, target, x, gamma, g_act, u_act, wg_t, wu_t, wd), sem=("arbitrary",))


def _ffn_bwd_act(d, g_act, u_act, wd, name, hosted=()):
    t = d.shape[0]
    f = wd.shape[0]

    def body(d_ref, g_ref, u_ref, wd_ref, dg_ref, du_ref):
        db = (0.5 * d_ref[...]).astype(BF16)
        for j in range(f // FC):
            sl = slice(j * FC, (j + 1) * FC)
            da = _dot_nt(db, wd_ref[sl, :])
            g = g_ref[:, sl].astype(F32)
            u = u_ref[:, sl].astype(F32)
            s = jax.nn.sigmoid(g)
            dg_ref[:, sl] = (da * u * (s * (1.0 + g * (1.0 - s)))).astype(BF16)
            du_ref[:, sl] = (da * (g * s)).astype(BF16)

    return _call(
        body,
        name=name,
        grid=(t // TM,),
        in_specs=[_rows(TM, D_MODEL), _rows(TM, f), _rows(TM, f), _resident((f, D_MODEL))],
        out_specs=[_rows(TM, f), _rows(TM, f)],
        out_shape=[jax.ShapeDtypeStruct((t, f), BF16), jax.ShapeDtypeStruct((t, f), BF16)],
        args=(d, g_act, u_act, wd), sem=("parallel",), hosted=hosted)


def _ffn_bwd_in(d, x, gamma, dg, du, wg_t, wu_t, name, hosted=()):
    t = x.shape[0]
    f = wg_t.shape[0]

    def body(d_ref, x_ref, gam_ref, dg_ref, du_ref, wg_ref, wu_ref, dx_ref, dgam_ref):
        dh = _dot_nn(dg_ref[...], wg_ref[...]) + _dot_nn(du_ref[...], wu_ref[...])
        xh, r = _rms(x_ref[...])
        dxn, dgam = _rms_bwd(dh, xh, r, gam_ref[...])
        dx_ref[...] = d_ref[...] + dxn

        @pl.when(pl.program_id(0) == 0)
        def _():
            dgam_ref[...] = jnp.zeros_like(dgam_ref)

        dgam_ref[...] += dgam

    return _call(
        body,
        name=name,
        grid=(t // TM,),
        in_specs=[_rows(TM, D_MODEL), _rows(TM, D_MODEL), _resident((1, D_MODEL)), _rows(TM, f), _rows(TM, f),
                  _resident((f, D_MODEL)), _resident((f, D_MODEL))],
        out_specs=[_rows(TM, D_MODEL), pl.BlockSpec((8, D_MODEL), lambda i: (0, 0))],
        out_shape=[jax.ShapeDtypeStruct((t, D_MODEL), F32), jax.ShapeDtypeStruct((8, D_MODEL), F32)],
        args=(d, x, gamma, dg, du, wg_t, wu_t), sem=("arbitrary",), hosted=hosted)


def _mm_tn(pieces, b, name, tile=256, hosted=()):
    t, n = b.shape
    npc = len(pieces)
    counts = [p.shape[1] // tile for p in pieces]
    los = [sum(counts[:k]) for k in range(npc)]
    total = sum(counts)

    def body(*refs):
        a_refs, b_ref, o_ref = refs[:npc], refs[npc], refs[npc + 1]
        i = pl.program_id(0)
        for k in range(npc):
            @pl.when(jnp.logical_and(i >= los[k], i < los[k] + counts[k]))
            def _(k=k):
                o_ref[...] = _dot_tn(a_refs[k][...], b_ref[...]).astype(BF16)

    def a_spec(k):
        return pl.BlockSpec((t, tile), lambda i: (0, jnp.clip(i - los[k], 0, counts[k] - 1)))

    return _call(
        body,
        name=name,
        grid=(total,),
        in_specs=[a_spec(k) for k in range(npc)] + [_resident((t, n))],
        out_specs=[pl.BlockSpec((tile, n), lambda i: (i, 0))],
        out_shape=[jax.ShapeDtypeStruct((total * tile, n), BF16)],
        args=(*pieces, b), sem=("parallel",), hosted=hosted)[0]


def _proj_fwd(x, gamma, win_t, hosted=()):
    t = x.shape[0]

    def body(x_ref, gam_ref, w_ref, h_ref, qa_ref, qb_ref, gt_ref):
        xh, _ = _rms(x_ref[...])
        h = (xh * gam_ref[...]).astype(BF16)
        h_ref[...] = h
        for j in range(QKV_A // FC):
            qa_ref[:, j * FC:(j + 1) * FC] = _dot_nt(h, w_ref[j * FC:(j + 1) * FC, :]).astype(BF16)
        for j in range(QKV_B // FC):
            lo = QKV_A + j * FC
            qb_ref[:, j * FC:(j + 1) * FC] = _dot_nt(h, w_ref[lo:lo + FC, :]).astype(BF16)
        for j in range(2 * D_MODEL // FC):
            lo = QKV_A + QKV_B + j * FC
            gt_ref[:, j * FC:(j + 1) * FC] = _dot_nt(h, w_ref[lo:lo + FC, :])

    return _call(
        body,
        name="proj_fwd",
        grid=(t // TM,),
        in_specs=[_rows(TM, D_MODEL), _resident((1, D_MODEL)), _resident((IN_WIDTH, D_MODEL))],
        out_specs=[_rows(TM, D_MODEL), _rows(TM, QKV_A), _rows(TM, QKV_B), _rows(TM, 2 * D_MODEL)],
        out_shape=[jax.ShapeDtypeStruct((t, D_MODEL), BF16), jax.ShapeDtypeStruct((t, QKV_A), BF16),
                   jax.ShapeDtypeStruct((t, QKV_B), BF16), jax.ShapeDtypeStruct((t, 2 * D_MODEL), F32)],
        args=(x, gamma, win_t), sem=("parallel",), hosted=hosted)


def _proj_bwd(d, x, gamma, pieces, win_t, hosted=()):
    t = x.shape[0]
    npc = len(pieces)
    widths = [p.shape[1] for p in pieces]
    los = [sum(widths[:k]) for k in range(npc)]

    def body(*refs):
        d_ref, x_ref, gam_ref = refs[:3]
        p_refs = refs[3:3 + npc]
        w_ref, dx_ref, db_ref, dgam_ref = refs[3 + npc:]
        dh = _dot_nn(p_refs[0][...], w_ref[0:widths[0], :])
        for k in range(1, npc):
            dh += _dot_nn(p_refs[k][...], w_ref[los[k]:los[k] + widths[k], :])
        xh, r = _rms(x_ref[...])
        dxn, dgam = _rms_bwd(dh, xh, r, gam_ref[...])
        dx = d_ref[...] + dxn
        dx_ref[...] = dx
        db_ref[...] = (0.5 * dx).astype(BF16)

        @pl.when(pl.program_id(0) == 0)
        def _():
            dgam_ref[...] = jnp.zeros_like(dgam_ref)

        dgam_ref[...] += dgam

    return _call(
        body,
        name="proj_bwd",
        grid=(t // TM,),
        in_specs=[_rows(TM, D_MODEL), _rows(TM, D_MODEL), _resident((1, D_MODEL))] + [_rows(TM, w) for w in widths]
        + [_resident((IN_WIDTH, D_MODEL))],
        out_specs=[_rows(TM, D_MODEL), _rows(TM, D_MODEL), pl.BlockSpec((8, D_MODEL), lambda i: (0, 0))],
        out_shape=[jax.ShapeDtypeStruct((t, D_MODEL), F32), jax.ShapeDtypeStruct((t, D_MODEL), BF16),
                   jax.ShapeDtypeStruct((8, D_MODEL), F32)],
        args=(d, x, gamma, *pieces, win_t), sem=("arbitrary",), hosted=hosted)


def _lane_half(shape):
    return lax.broadcasted_iota(jnp.int32, shape, len(shape) - 1) // D_HEAD


def _band_weights(q, kk, bias, sink, qs, pad):
    s = _band_scores(q, kk, bias, qs, pad)
    m = jnp.max(s, axis=-1, keepdims=True)
    if sink is not None:
        m = jnp.maximum(m, sink)
    return jnp.exp(s - m), m


def _band_scores(q, kk, bias, qs, pad):
    s = _dot_nt(q, kk) + bias
    if qs is not None:
        col = lax.broadcasted_iota(jnp.int32, s.shape, 1)
        s = jnp.where(col + qs >= pad, s, NEG_INF)
    return s


def _weighted_values(p, vv_ones, sink, m):
    r = _dot_nn(p.astype(BF16), vv_ones)
    den = r[:, LANES:2 * LANES]
    if sink is not None:
        den = den + jnp.exp(sink - m)
    return r[:, 0:LANES] / den, m + jnp.log(den[:, 0:1])


def _fill_padded(dst, src, pad):
    dst[0:pad, :] = jnp.zeros((pad,) + dst.shape[1:], dst.dtype)
    dst[pad:, :] = src


FWD_PAIRS = 4
BWD_PAIRS = 4


def _attn_a_fwd(qkv, bias, hosted=()):
    bsz, s_len, _ = qkv.shape
    pad = A_PREV * CHUNK
    band = TQ + pad
    pp = FWD_PAIRS
    w = pp * LANES
    nb = A_WIDTH // w

    def body(q_ref, k_ref, v_ref, b_ref, o_ref, l_ref, kp, vp):
        i = pl.program_id(2)

        @pl.when(i == 0)
        def _():
            _fill_padded(kp, k_ref[...], pad)
            _fill_padded(vp, v_ref[...], pad)

        qs = pl.multiple_of(i * TQ, TQ)
        half = _lane_half((1, LANES))

        ones = jnp.ones((band, LANES), BF16)

        def block(masked):
            for pr in range(pp):
                sl = slice(pr * LANES, (pr + 1) * LANES)
                kk = kp[pl.ds(qs, band), sl]
                vv = jnp.concatenate([vp[pl.ds(qs, band), sl], ones], axis=1)
                q = q_ref[:, sl] * SCALE
                outs = []
                for j in range(2):
                    qm = jnp.where(half == j, q, jnp.zeros_like(q))
                    p, m = _band_weights(qm, kk, b_ref[2 * pr + j], None, qs if masked else None, pad)
                    o, lse = _weighted_values(p, vv, None, m)
                    outs.append(o)
                    l_ref[:, 2 * pr + j:2 * pr + j + 1] = lse
                o_ref[:, sl] = jnp.where(half == 0, outs[0], outs[1]).astype(BF16)

        pl.when(i < pad // TQ)(lambda: block(True))
        pl.when(i >= pad // TQ)(lambda: block(False))

    return _call(
        body,
        name="attn_a_fwd",
        grid=(bsz, nb, s_len // TQ),
        in_specs=[pl.BlockSpec((None, TQ, w), lambda b, g, i: (b, i, g)),
                  pl.BlockSpec((None, s_len, w), lambda b, g, i: (b, 0, nb + g)),
                  pl.BlockSpec((None, s_len, w), lambda b, g, i: (b, 0, 2 * nb + g)),
                  pl.BlockSpec((2 * pp, TQ, band), lambda b, g, i: (g, 0, 0))],
        out_specs=[pl.BlockSpec((None, TQ, w), lambda b, g, i: (b, i, g)),
                   pl.BlockSpec((None, TQ, 2 * pp), lambda b, g, i: (b, i, g))],
        out_shape=[jax.ShapeDtypeStruct((bsz, s_len, A_WIDTH), BF16),
                   jax.ShapeDtypeStruct((bsz, s_len, A_HEADS), F32)],
        scratch_shapes=[pltpu.VMEM((pad + s_len, w), BF16), pltpu.VMEM((pad + s_len, w), BF16)],
        args=(qkv, qkv, qkv, bias), sem=("arbitrary", "arbitrary", "arbitrary"), hosted=hosted)


def _attn_a_bwd(qkv, bias, do, o, lse, hosted=()):
    bsz, s_len, _ = qkv.shape
    pad = A_PREV * CHUNK
    band = TQ + pad
    n_i = s_len // TQ
    pp = BWD_PAIRS
    w = pp * LANES
    nb = A_WIDTH // w

    def body(q_ref, k_ref, v_ref, b_ref, do_ref, o_ref, l_ref, dq_ref, dk_ref, dv_ref, dbias_ref, kp, vp, dk_acc,
             dv_acc):
        b = pl.program_id(1)
        i = pl.program_id(2)

        @pl.when(i == 0)
        def _():
            _fill_padded(kp, k_ref[...], pad)
            _fill_padded(vp, v_ref[...], pad)
            dk_acc[...] = jnp.zeros_like(dk_acc)
            dv_acc[...] = jnp.zeros_like(dv_acc)

        @pl.when(jnp.logical_and(b == 0, i == 0))
        def _():
            dbias_ref[...] = jnp.zeros_like(dbias_ref)

        qs = pl.multiple_of(i * TQ, TQ)
        half = _lane_half((1, LANES))
        half_t = lax.broadcasted_iota(jnp.int32, (LANES, 1), 0) // D_HEAD

        def block(masked):
            for pr in range(pp):
                sl = slice(pr * LANES, (pr + 1) * LANES)
                kk = kp[pl.ds(qs, band), sl]
                vv = vp[pl.ds(qs, band), sl]
                q = q_ref[:, sl] * SCALE
                dd = do_ref[:, sl]
                od = dd.astype(F32) * o_ref[:, sl].astype(F32)
                dqs, dks, dvs = [], [], []
                for j in range(2):
                    hd = 2 * pr + j
                    qm = jnp.where(half == j, q, jnp.zeros_like(q))
                    dm = jnp.where(half == j, dd, jnp.zeros_like(dd))
                    s = _band_scores(qm, kk, b_ref[hd], qs if masked else None, pad)
                    pn = jnp.exp(s - l_ref[:, hd:hd + 1])
                    dp = _dot_nt(dm, vv)
                    delta = jnp.sum(jnp.where(half == j, od, 0.0), axis=-1, keepdims=True)
                    ds = pn * (dp - delta)
                    dbias_ref[hd] += ds[:, band - REL_COLS:]
                    dsb = ds.astype(BF16)
                    dqs.append(_dot_nn(dsb, kk))
                    dks.append(_dot_tn(q, dsb))
                    dvs.append(_dot_tn(dd, pn.astype(BF16)))
                dq_ref[:, sl] = (jnp.where(half == 0, dqs[0], dqs[1]) * SCALE).astype(BF16)
                dk_acc[sl, pl.ds(qs, band)] += jnp.where(half_t == 0, dks[0], dks[1])
                dv_acc[sl, pl.ds(qs, band)] += jnp.where(half_t == 0, dvs[0], dvs[1])

        pl.when(i < pad // TQ)(lambda: block(True))
        pl.when(i >= pad // TQ)(lambda: block(False))

        @pl.when(i == n_i - 1)
        def _():
            dk_ref[...] = dk_acc[:, pad:].T.astype(BF16)
            dv_ref[...] = dv_acc[:, pad:].T.astype(BF16)

    qspec = pl.BlockSpec((None, TQ, w), lambda g, b, i: (b, i, g))
    kvout = pl.BlockSpec((None, s_len, w), lambda g, b, i: (b, 0, g))
    wide = jax.ShapeDtypeStruct((bsz, s_len, A_WIDTH), BF16)
    return _call(
        body,
        name="attn_a_bwd",
        grid=(nb, bsz, n_i),
        in_specs=[qspec,
                  pl.BlockSpec((None, s_len, w), lambda g, b, i: (b, 0, nb + g)),
                  pl.BlockSpec((None, s_len, w), lambda g, b, i: (b, 0, 2 * nb + g)),
                  pl.BlockSpec((2 * pp, TQ, band), lambda g, b, i: (g, 0, 0)),
                  qspec, qspec,
                  pl.BlockSpec((None, TQ, 2 * pp), lambda g, b, i: (b, i, g))],
        out_specs=[qspec, kvout, kvout, pl.BlockSpec((2 * pp, TQ, REL_COLS), lambda g, b, i: (g, 0, 0))],
        out_shape=[wide, wide, wide, jax.ShapeDtypeStruct((A_HEADS, TQ, REL_COLS), F32)],
        scratch_shapes=[pltpu.VMEM((pad + s_len, w), BF16), pltpu.VMEM((pad + s_len, w), BF16),
                        pltpu.VMEM((w, pad + s_len), F32), pltpu.VMEM((w, pad + s_len), F32)],
        args=(qkv, qkv, qkv, bias, do, o, lse), sem=("arbitrary", "arbitrary", "arbitrary"), hosted=hosted)


def _fill_padded_dup(dst, src, pad, h, half):
    other = pltpu.roll(src, D_HEAD, 1)
    _fill_padded(dst, jnp.where(half == h, src, other), pad)


def _attn_b_fwd(qkv, bias, sink):
    bsz, s_len, _ = qkv.shape
    pad = B_PREV * CHUNK
    band = TQ + pad
    kcol = B_Q_WIDTH // LANES
    npair = B_Q_HEADS // 2

    def body(q_ref, k_ref, v_ref, b_ref, s_ref, o_ref, l_ref, kp, vp):
        i = pl.program_id(1)
        half = _lane_half((1, LANES))

        @pl.when(i == 0)
        def _():
            for h in range(B_KV_HEADS):
                _fill_padded_dup(kp.at[h], k_ref[...], pad, h, half)
                _fill_padded_dup(vp.at[h], v_ref[...], pad, h, half)

        qs = pl.multiple_of(i * TQ, TQ)

        ones = jnp.ones((band, LANES), BF16)

        def block(masked):
            for pr in range(npair):
                h = pr // (B_GROUP // 2)
                sl = slice(pr * LANES, (pr + 1) * LANES)
                kk = kp[h, pl.ds(qs, band), :]
                vv = jnp.concatenate([vp[h, pl.ds(qs, band), :], ones], axis=1)
                q = q_ref[:, sl] * SCALE
                outs = []
                for j in range(2):
                    qm = jnp.where(half == j, q, jnp.zeros_like(q))
                    sink = s_ref[2 * pr + j][0:1, 0:1]
                    p, m = _band_weights(qm, kk, b_ref[2 * pr + j], sink, qs if masked else None, pad)
                    o, lse = _weighted_values(p, vv, sink, m)
                    outs.append(o)
                    l_ref[:, 2 * pr + j:2 * pr + j + 1] = lse
                o_ref[:, sl] = jnp.where(half == 0, outs[0], outs[1]).astype(BF16)

        pl.when(i < -(-pad // TQ))(lambda: block(True))
        pl.when(i >= -(-pad // TQ))(lambda: block(False))

    return pl.pallas_call(
        body,
        name="attn_b_fwd",
        grid=(bsz, s_len // TQ),
        in_specs=[pl.BlockSpec((None, TQ, B_Q_WIDTH), lambda b, i: (b, i, 0)),
                  pl.BlockSpec((None, s_len, LANES), lambda b, i: (b, 0, kcol)),
                  pl.BlockSpec((None, s_len, LANES), lambda b, i: (b, 0, kcol + 1)),
                  pl.BlockSpec((B_Q_HEADS, TQ, band), lambda b, i: (0, 0, 0)),
                  pl.BlockSpec((B_Q_HEADS, 8, LANES), lambda b, i: (0, 0, 0))],
        out_specs=[pl.BlockSpec((None, TQ, B_Q_WIDTH), lambda b, i: (b, i, 0)),
                   pl.BlockSpec((None, TQ, B_Q_HEADS), lambda b, i: (b, i, 0))],
        out_shape=[jax.ShapeDtypeStruct((bsz, s_len, B_Q_WIDTH), BF16),
                   jax.ShapeDtypeStruct((bsz, s_len, B_Q_HEADS), F32)],
        scratch_shapes=[pltpu.VMEM((B_KV_HEADS, pad + s_len, LANES), BF16),
                        pltpu.VMEM((B_KV_HEADS, pad + s_len, LANES), BF16)],
        compiler_params=_cparams(("arbitrary", "arbitrary")),
    )(qkv, qkv, qkv, bias, sink)


def _attn_b_bwd(qkv, bias, sink, do, o, lse, hosted=()):
    bsz, s_len, _ = qkv.shape
    pad = B_PREV * CHUNK
    band = TQ + pad
    kcol = B_Q_WIDTH // LANES
    n_i = s_len // TQ
    pp = B_GROUP // 2

    def body(q_ref, k_ref, v_ref, b_ref, s_ref, do_ref, o_ref, l_ref, dq_ref, dkv_ref, dsink_ref, kp, vp, dk_acc,
             dv_acc):
        b = pl.program_id(0)
        i = pl.program_id(1)
        half = _lane_half((1, LANES))

        @pl.when(i == 0)
        def _():
            for h in range(B_KV_HEADS):
                _fill_padded_dup(kp.at[h], k_ref[...], pad, h, half)
                _fill_padded_dup(vp.at[h], v_ref[...], pad, h, half)
            dk_acc[...] = jnp.zeros_like(dk_acc)
            dv_acc[...] = jnp.zeros_like(dv_acc)

        @pl.when(jnp.logical_and(b == 0, i == 0))
        def _():
            dsink_ref[...] = jnp.zeros_like(dsink_ref)

        qs = pl.multiple_of(i * TQ, TQ)

        def block(masked):
            heads_dk, heads_dv = [], []
            for h in range(B_KV_HEADS):
                kk = kp[h, pl.ds(qs, band), :]
                vv = vp[h, pl.ds(qs, band), :]
                dk2 = jnp.zeros((band, LANES), F32)
                dv2 = jnp.zeros((band, LANES), F32)
                for pr in range(pp * h, pp * (h + 1)):
                    sl = slice(pr * LANES, (pr + 1) * LANES)
                    q = q_ref[:, sl] * SCALE
                    dd = do_ref[:, sl]
                    od = dd.astype(F32) * o_ref[:, sl].astype(F32)
                    dqs, dks, dvs = [], [], []
                    for j in range(2):
                        qm = jnp.where(half == j, q, jnp.zeros_like(q))
                        dm = jnp.where(half == j, dd, jnp.zeros_like(dd))
                        hd = 2 * pr + j
                        lse = l_ref[:, hd:hd + 1]
                        s = _band_scores(qm, kk, b_ref[hd], qs if masked else None, pad)
                        pn = jnp.exp(s - lse)
                        dp = _dot_nt(dm, vv)
                        delta = jnp.sum(jnp.where(half == j, od, 0.0), axis=-1, keepdims=True)
                        ds = pn * (dp - delta)
                        dsb = ds.astype(BF16)
                        dqs.append(_dot_nn(dsb, kk))
                        dks.append(_dot_tn(dsb, q))
                        dvs.append(_dot_tn(pn.astype(BF16), dd))
                        sink = s_ref[hd][0:1, 0:1]
                        dsk = jnp.sum(-jnp.exp(sink - lse) * delta, axis=0, keepdims=True)
                        dsink_ref[hd] += jnp.broadcast_to(dsk, (8, LANES))
                    dq_ref[:, sl] = (jnp.where(half == 0, dqs[0], dqs[1]) * SCALE).astype(BF16)
                    dk2 = dk2 + jnp.where(half == 0, dks[0], dks[1])
                    dv2 = dv2 + jnp.where(half == 0, dvs[0], dvs[1])
                heads_dk.append(dk2 + pltpu.roll(dk2, D_HEAD, 1))
                heads_dv.append(dv2 + pltpu.roll(dv2, D_HEAD, 1))
            dk_acc[pl.ds(qs, band), :] += jnp.where(half == 0, heads_dk[0], heads_dk[1])
            dv_acc[pl.ds(qs, band), :] += jnp.where(half == 0, heads_dv[0], heads_dv[1])

        pl.when(i < -(-pad // TQ))(lambda: block(True))
        pl.when(i >= -(-pad // TQ))(lambda: block(False))

        @pl.when(i == n_i - 1)
        def _():
            dkv_ref[:, 0:LANES] = dk_acc[pad:, :].astype(BF16)
            dkv_ref[:, LANES:2 * LANES] = dv_acc[pad:, :].astype(BF16)

    qspec = pl.BlockSpec((None, TQ, B_Q_WIDTH), lambda b, i: (b, i, 0))
    return _call(
        body,
        name="attn_b_bwd",
        grid=(bsz, n_i),
        in_specs=[qspec,
                  pl.BlockSpec((None, s_len, LANES), lambda b, i: (b, 0, kcol)),
                  pl.BlockSpec((None, s_len, LANES), lambda b, i: (b, 0, kcol + 1)),
                  pl.BlockSpec((B_Q_HEADS, TQ, band), lambda b, i: (0, 0, 0)),
                  pl.BlockSpec((B_Q_HEADS, 8, LANES), lambda b, i: (0, 0, 0)),
                  qspec, qspec,
                  pl.BlockSpec((None, TQ, B_Q_HEADS), lambda b, i: (b, i, 0))],
        out_specs=[qspec, pl.BlockSpec((None, s_len, 2 * LANES), lambda b, i: (b, 0, 0)),
                   pl.BlockSpec((B_Q_HEADS, 8, LANES), lambda b, i: (0, 0, 0))],
        out_shape=[jax.ShapeDtypeStruct((bsz, s_len, B_Q_WIDTH), BF16),
                   jax.ShapeDtypeStruct((bsz, s_len, 2 * B_KV_WIDTH), BF16),
                   jax.ShapeDtypeStruct((B_Q_HEADS, 8, LANES), F32)],
        scratch_shapes=[pltpu.VMEM((B_KV_HEADS, pad + s_len, LANES), BF16),
                        pltpu.VMEM((B_KV_HEADS, pad + s_len, LANES), BF16),
                        pltpu.VMEM((pad + s_len, LANES), F32), pltpu.VMEM((pad + s_len, LANES), F32)],
        args=(qkv, qkv, qkv, bias, sink, do, o, lse), sem=("arbitrary", "arbitrary"), hosted=hosted)


REL_COLS = 3 * 128
REL_WRAP = 512


def _bias_a_build(tv, hosted=()):
    h = tv.shape[0]
    pad = A_PREV * CHUNK
    band = TQ + pad

    def body(tv_ref, o_ref):
        row = tv_ref[...]
        x = jnp.broadcast_to(row, (TQ, REL_WRAP))
        r = lax.broadcasted_iota(jnp.int32, x.shape, 0)
        for bit in range(8):
            sh = 1 << bit
            x = jnp.where((r & sh) != 0, pltpu.roll(x, sh, 1), x)
        far = jnp.broadcast_to(row[:, 0:1], (TQ, band - REL_COLS))
        full = jnp.concatenate([far, x[:, REL_WRAP // 2:REL_WRAP], x[:, 0:REL_COLS - REL_WRAP // 2]], axis=1)
        qc = (lax.broadcasted_iota(jnp.int32, full.shape, 0) + pad) // CHUNK
        kc = lax.broadcasted_iota(jnp.int32, full.shape, 1) // CHUNK
        ok = jnp.logical_and(kc <= qc, kc >= qc - A_PREV)
        o_ref[...] = jnp.where(ok, full, NEG_INF)

    return _call(
        body,
        name="bias_a_build",
        grid=(h,),
        in_specs=[pl.BlockSpec((None, 1, REL_WRAP), lambda hh: (hh, 0, 0))],
        out_specs=[pl.BlockSpec((None, TQ, band), lambda hh: (hh, 0, 0))],
        out_shape=[jax.ShapeDtypeStruct((h, TQ, band), F32)],
        args=(tv,), sem=("parallel",), hosted=hosted)[0]


def _relbias_grad(dbias, hosted=()):
    h, rows, _ = dbias.shape

    def body(d_ref, o_ref):
        x = d_ref[...]
        r = lax.broadcasted_iota(jnp.int32, x.shape, 0)
        c = lax.broadcasted_iota(jnp.int32, x.shape, 1) - r
        x = jnp.where(jnp.logical_and(c >= 1, c < REL_TABLE), x, 0.0)
        for bit in range(8):
            sh = 1 << bit
            x = jnp.where((r & sh) != 0, pltpu.roll(x, REL_COLS - sh, 1), x)
        diag = jnp.sum(x, axis=0, keepdims=True)
        lane = lax.broadcasted_iota(jnp.int32, diag.shape, 1)
        diag = jnp.where(jnp.logical_and(lane >= 1, lane < REL_TABLE), diag, 0.0)
        rest = -jnp.sum(diag, axis=1, keepdims=True)
        o_ref[...] = jnp.broadcast_to(jnp.where(lane == 0, rest, diag), o_ref.shape)

    return _call(
        body,
        name="relbias_grad",
        grid=(h,),
        in_specs=[pl.BlockSpec((None, rows, REL_COLS), lambda hh: (hh, 0, 0))],
        out_specs=[pl.BlockSpec((None, 8, REL_COLS), lambda hh: (hh, 0, 0))],
        out_shape=[jax.ShapeDtypeStruct((h, 8, REL_COLS), F32)],
        args=(dbias,), sem=("parallel",), hosted=hosted)[0]


def _mix_out_fwd(x, oa, ob, gates, proj_t, wout):
    t = x.shape[0]

    def body(x_ref, oa_ref, ob_ref, gt_ref, pt_ref, wo_ref, y_ref, ya_ref, yb_ref, mg_ref):
        ya = _dot_nt(oa_ref[...], pt_ref[:, 0:A_WIDTH])
        yb = _dot_nt(ob_ref[...], pt_ref[:, A_WIDTH:A_WIDTH + B_Q_WIDTH])
        ya_ref[...] = ya.astype(BF16)
        yb_ref[...] = yb.astype(BF16)
        mg = jax.nn.sigmoid(gt_ref[:, 0:D_MODEL]) * ya + jax.nn.sigmoid(gt_ref[:, D_MODEL:2 * D_MODEL]) * yb
        mgb = mg.astype(BF16)
        mg_ref[...] = mgb
        y_ref[...] = x_ref[...] + _dot_nn(mgb, wo_ref[...])

    return pl.pallas_call(
        body,
        name="mix_out_fwd",
        grid=(t // TM,),
        in_specs=[_rows(TM, D_MODEL), _rows(TM, A_WIDTH), _rows(TM, B_Q_WIDTH), _rows(TM, 2 * D_MODEL),
                  _resident((D_MODEL, A_WIDTH + B_Q_WIDTH)), _resident((D_MODEL, D_MODEL))],
        out_specs=[_rows(TM, D_MODEL), _rows(TM, D_MODEL), _rows(TM, D_MODEL), _rows(TM, D_MODEL)],
        out_shape=[jax.ShapeDtypeStruct((t, D_MODEL), F32), jax.ShapeDtypeStruct((t, D_MODEL), BF16),
                   jax.ShapeDtypeStruct((t, D_MODEL), BF16), jax.ShapeDtypeStruct((t, D_MODEL), BF16)],
        compiler_params=_cparams(("parallel",)),
    )(x, oa, ob, gates, proj_t, wout)


def _mix_out_bwd(d, gates, ya, yb, mg, oa, ob, proj_t, wout, hosted=()):
    t = d.shape[0]
    nt = t // TM

    def body(d_ref, gt_ref, ya_ref, yb_ref, mg_ref, oa_ref, ob_ref, pt_ref, wo_ref,
             doa_ref, dob_ref, dgt_ref, gwo_ref, gwp_ref, acc_o, acc_p):
        i = pl.program_id(0)
        db = d_ref[...].astype(BF16)
        dmg = _dot_nt(db, wo_ref[...])
        sa = jax.nn.sigmoid(gt_ref[:, 0:D_MODEL])
        sb = jax.nn.sigmoid(gt_ref[:, D_MODEL:2 * D_MODEL])
        dya = (dmg * sa).astype(BF16)
        dyb = (dmg * sb).astype(BF16)
        dgt_ref[:, 0:D_MODEL] = (dmg * ya_ref[...].astype(F32) * (sa * (1.0 - sa))).astype(BF16)
        dgt_ref[:, D_MODEL:2 * D_MODEL] = (dmg * yb_ref[...].astype(F32) * (sb * (1.0 - sb))).astype(BF16)
        doa_ref[...] = _dot_nn(dya, pt_ref[:, 0:A_WIDTH]).astype(BF16)
        dob_ref[...] = _dot_nn(dyb, pt_ref[:, A_WIDTH:A_WIDTH + B_Q_WIDTH]).astype(BF16)

        @pl.when(i == 0)
        def _():
            acc_o[...] = jnp.zeros_like(acc_o)
            acc_p[...] = jnp.zeros_like(acc_p)

        acc_o[...] += _dot_tn(mg_ref[...], db)
        acc_p[:, 0:A_WIDTH] += _dot_tn(dya, oa_ref[...])
        acc_p[:, A_WIDTH:A_WIDTH + B_Q_WIDTH] += _dot_tn(dyb, ob_ref[...])

        @pl.when(i == nt - 1)
        def _():
            gwo_ref[...] = acc_o[...].astype(BF16)
            gwp_ref[...] = acc_p[...].astype(BF16)

    whole = pl.BlockSpec((D_MODEL, D_MODEL), lambda i: (0, 0))
    return _call(
        body,
        name="mix_out_bwd",
        grid=(nt,),
        in_specs=[_rows(TM, D_MODEL), _rows(TM, 2 * D_MODEL), _rows(TM, D_MODEL), _rows(TM, D_MODEL),
                  _rows(TM, D_MODEL), _rows(TM, A_WIDTH), _rows(TM, B_Q_WIDTH),
                  _resident((D_MODEL, A_WIDTH + B_Q_WIDTH)), _resident((D_MODEL, D_MODEL))],
        out_specs=[_rows(TM, A_WIDTH), _rows(TM, B_Q_WIDTH), _rows(TM, 2 * D_MODEL), whole, whole],
        out_shape=[jax.ShapeDtypeStruct((t, A_WIDTH), BF16), jax.ShapeDtypeStruct((t, B_Q_WIDTH), BF16),
                   jax.ShapeDtypeStruct((t, 2 * D_MODEL), BF16), jax.ShapeDtypeStruct((D_MODEL, D_MODEL), BF16),
                   jax.ShapeDtypeStruct((D_MODEL, D_MODEL), BF16)],
        scratch_shapes=[pltpu.VMEM((D_MODEL, D_MODEL), F32), pltpu.VMEM((D_MODEL, A_WIDTH + B_Q_WIDTH), F32)],
        args=(d, gates, ya, yb, mg, oa, ob, proj_t, wout), sem=("arbitrary",), hosted=hosted)


def _place():
    x, y, c = lax.axis_index("x"), lax.axis_index("y"), lax.axis_index("c")
    chips = [(1 - x, y), (x, 1 - y), (1 - x, 1 - y)]
    return x, y, c, chips


class _Gather:
    per = 8

    def __init__(self, shards):
        n = len(shards)
        self.inputs = list(shards)
        self.out_shape = [jax.ShapeDtypeStruct((N_DEV * s.shape[0], s.shape[1]), s.dtype) for s in shards]
        self.scratch = [pltpu.SemaphoreType.DMA((n * self.per,)), pltpu.SemaphoreType.DMA((n * self.per,)),
                        pltpu.SemaphoreType.DMA((n,))]
        self.result = None

    def _parts(self, ins, outs, sems):
        send_sems, recv_sems, local_sems = sems
        x, y, c, chips = _place()
        me, sibling = (x, y, c), (x, y, 1 - c)
        xn, yn, dg = chips
        n = len(ins)

        def rows(k, p, part=None):
            r = ins[k].shape[0]
            base = (4 * p[0] + 2 * p[1] + p[2]) * r
            if part is None:
                return outs[k].at[pl.ds(base, r), :]
            return outs[k].at[pl.ds(base + part * (r // 2), r // 2), :]

        def copy(k, slot, block, to, src=None, part=None):
            return pltpu.make_async_remote_copy(
                src_ref=rows(k, block, part) if src is None else src, dst_ref=rows(k, block, part),
                send_sem=send_sems.at[k * self.per + slot], recv_sem=recv_sems.at[k * self.per + slot],
                device_id=to, device_id_type=MESH)

        mine = [pltpu.make_async_copy(ins[k], rows(k, me), local_sems.at[k]) for k in range(n)]
        sends, lands = [], []
        for k in range(n):
            sends.append({
                0: copy(k, 0, me, sibling, src=ins[k]),
                1: copy(k, 1, me, (*xn, c), src=ins[k]),
                2: copy(k, 2, me, (*yn, c), src=ins[k]),
                3: copy(k, 3, (*xn, c), (*yn, c), part=0),
                4: copy(k, 4, (*yn, c), (*xn, c), part=1),
                5: copy(k, 5, (*xn, c), sibling),
                6: copy(k, 6, (*yn, c), sibling),
                7: copy(k, 7, (*dg, c), sibling)})
            lands.append({
                0: copy(k, 0, sibling, me),
                1: copy(k, 1, (*xn, c), me),
                2: copy(k, 2, (*yn, c), me),
                3: copy(k, 3, (*dg, c), me, part=0),
                4: copy(k, 4, (*dg, c), me, part=1),
                5: copy(k, 5, (*xn, 1 - c), me),
                6: copy(k, 6, (*yn, 1 - c), me),
                7: copy(k, 7, (*dg, 1 - c), me)})
        return n, mine, sends, lands

    def start(self, ins, outs, sems):
        n, mine, sends, _ = self._parts(ins, outs, sems)
        for cp in mine:
            cp.start()
        for slot in (0, 1, 2):
            for k in range(n):
                sends[k][slot].start()

    def relay(self, ins, outs, sems):
        n, _, sends, lands = self._parts(ins, outs, sems)
        for k in range(n):
            lands[k][1].wait_recv()
            sends[k][3].start()
            sends[k][5].start()
        for k in range(n):
            lands[k][2].wait_recv()
            sends[k][4].start()
            sends[k][6].start()

    def forward(self, ins, outs, sems):
        n, _, sends, lands = self._parts(ins, outs, sems)
        for k in range(n):
            lands[k][3].wait_recv()
            lands[k][4].wait_recv()
            sends[k][7].start()

    def finish(self, ins, outs, sems):
        n, mine, sends, lands = self._parts(ins, outs, sems)
        for k in range(n):
            for slot in (0, 5, 6, 7):
                lands[k][slot].wait_recv()
        for k in range(n):
            for slot in range(self.per):
                sends[k][slot].wait_send()
        for cp in mine:
            cp.wait()


class _PairExchange:
    def __init__(self, grads):
        n = len(grads)
        self.inputs = list(grads)
        self.out_shape = [jax.ShapeDtypeStruct((g.shape[0] // 2, g.shape[1]), g.dtype) for g in grads]
        self.scratch = [pltpu.SemaphoreType.DMA((n * N_CHIP,)), pltpu.SemaphoreType.DMA((n * N_CHIP,))]
        self.result = None

    def _copies(self, ins, outs, sems):
        send_sems, recv_sems = sems
        x, y, c, _ = _place()
        copies = []
        for k in range(len(ins)):
            r = ins[k].shape[0] // N_DEV
            for q in range(N_CHIP):
                copies.append(pltpu.make_async_remote_copy(
                    src_ref=ins[k].at[pl.ds((2 * q + 1 - c) * r, r), :], dst_ref=outs[k].at[pl.ds(q * r, r), :],
                    send_sem=send_sems.at[k * N_CHIP + q], recv_sem=recv_sems.at[k * N_CHIP + q],
                    device_id=(x, y, 1 - c), device_id_type=MESH))
        return copies

    def start(self, ins, outs, sems):
        for cp in self._copies(ins, outs, sems):
            cp.start()

    def relay(self, ins, outs, sems):
        pass

    def forward(self, ins, outs, sems):
        pass

    def finish(self, ins, outs, sems):
        copies = self._copies(ins, outs, sems)
        for cp in copies:
            cp.wait_recv()
        for cp in copies:
            cp.wait_send()


class _ChipExchange(_PairExchange):
    def __init__(self, psums):
        n = len(psums)
        self.inputs = list(psums)
        self.out_shape = [jax.ShapeDtypeStruct((3 * p.shape[0] // N_CHIP, p.shape[1]), p.dtype) for p in psums]
        self.scratch = [pltpu.SemaphoreType.DMA((n * 3,)), pltpu.SemaphoreType.DMA((n * 3,))]
        self.result = None

    def _copies(self, ins, outs, sems):
        send_sems, recv_sems = sems
        _, _, c, chips = _place()
        copies = []
        for k in range(len(ins)):
            r = ins[k].shape[0] // N_CHIP
            for j, chip in enumerate(chips):
                copies.append(pltpu.make_async_remote_copy(
                    src_ref=ins[k].at[pl.ds((2 * chip[0] + chip[1]) * r, r), :], dst_ref=outs[k].at[pl.ds(j * r, r), :],
                    send_sem=send_sems.at[k * 3 + j], recv_sem=recv_sems.at[k * 3 + j],
                    device_id=(*chip, c), device_id_type=MESH))
        return copies


def _exchange_alone(xchg, name):
    n_in, n_out = len(xchg.inputs), len(xchg.out_shape)

    def body(*refs):
        ins, outs, sems = refs[:n_in], refs[n_in:n_in + n_out], refs[n_in + n_out:]
        xchg.start(ins, outs, sems)
        xchg.relay(ins, outs, sems)
        xchg.forward(ins, outs, sems)
        xchg.finish(ins, outs, sems)

    xchg.result = list(pl.pallas_call(
        body, name=name, in_specs=[_hbm()] * n_in, out_specs=[_hbm()] * n_out, out_shape=xchg.out_shape,
        scratch_shapes=xchg.scratch)(*xchg.inputs))
    return xchg.result


def _pair_sum(core, grads, recvd, name):
    n = len(grads)
    r = grads[0].shape[0] // N_DEV
    cdim = grads[0].shape[1]
    tr = r // 2 if r % 32 == 0 else r
    nt = r // tr

    def body(core_ref, *refs):
        del core_ref
        for k in range(n):
            refs[2 * n + k][...] = (refs[k][...].astype(F32) + refs[n + k][...].astype(F32)).astype(BF16)

    gspec = pl.BlockSpec((tr, cdim), lambda q, i, core_ref: ((2 * q + core_ref[0]) * nt + i, 0))
    rspec = pl.BlockSpec((tr, cdim), lambda q, i, core_ref: (q * nt + i, 0))
    return pl.pallas_call(
        body,
        name=name,
        grid_spec=pltpu.PrefetchScalarGridSpec(
            num_scalar_prefetch=1, grid=(N_CHIP, nt), in_specs=[gspec] * n + [rspec] * n, out_specs=[rspec] * n),
        out_shape=[jax.ShapeDtypeStruct((N_CHIP * r, cdim), BF16) for _ in range(n)],
        compiler_params=_cparams(("parallel", "parallel")),
    )(core, *grads, *recvd)


def _final_sum(chip, psums, recvd, name):
    n = len(psums)
    r = psums[0].shape[0] // N_CHIP
    cdim = psums[0].shape[1]
    tr = r // 2 if r % 32 == 0 else r
    nt = r // tr

    def body(chip_ref, *refs):
        del chip_ref
        for k in range(n):
            got = refs[n + k]
            tot = refs[k][...].astype(F32) + got[0].astype(F32)
            tot = tot + got[1].astype(F32)
            tot = tot + got[2].astype(F32)
            refs[2 * n + k][...] = tot

    pspec = pl.BlockSpec((tr, cdim), lambda i, chip_ref: (chip_ref[0] * nt + i, 0))
    rspec = pl.BlockSpec((3, tr, cdim), lambda i, chip_ref: (0, i, 0))
    ospec = pl.BlockSpec((tr, cdim), lambda i, chip_ref: (i, 0))
    return pl.pallas_call(
        body,
        name=name,
        grid_spec=pltpu.PrefetchScalarGridSpec(
            num_scalar_prefetch=1, grid=(nt,), in_specs=[pspec] * n + [rspec] * n, out_specs=[ospec] * n),
        out_shape=[jax.ShapeDtypeStruct((r, cdim), F32) for _ in range(n)],
        compiler_params=_cparams(("parallel",)),
    )(chip, *psums, *[g.reshape(3, r, cdim) for g in recvd])


SMALL_ROWS = 16


def _all_reduce_small(part):
    def body(p_ref, o_ref, buf, send_sems, recv_sems):
        x, y, c, _ = _place()
        me = 4 * x + 2 * y + c
        buf[me] = p_ref[...]
        copies = []
        for d in range(1, N_DEV):
            peer = me ^ d
            copies.append(pltpu.make_async_remote_copy(
                src_ref=p_ref, dst_ref=buf.at[me], send_sem=send_sems.at[d - 1], recv_sem=recv_sems.at[d - 1],
                device_id=(peer // 4, (peer // 2) % 2, peer % 2), device_id_type=MESH))
        for cp in copies:
            cp.start()
        for cp in copies:
            cp.wait_recv()
        for cp in copies:
            cp.wait_send()
        tot = buf[0]
        for d in range(1, N_DEV):
            tot = tot + buf[d]
        o_ref[...] = tot

    return pl.pallas_call(
        body,
        name="all_reduce_small",
        in_specs=[pl.BlockSpec(memory_space=pltpu.VMEM)],
        out_specs=pl.BlockSpec(memory_space=pltpu.VMEM),
        out_shape=jax.ShapeDtypeStruct(part.shape, F32),
        scratch_shapes=[pltpu.VMEM((N_DEV,) + part.shape, F32), pltpu.SemaphoreType.DMA((N_DEV - 1,)),
                        pltpu.SemaphoreType.DMA((N_DEV - 1,))],
    )(part)


ADAMW_STEPS = 4


def _adamw(ws, gs, ms, vs, name, hosted=()):
    n = len(ws)
    steps = ADAMW_STEPS if all(w.shape[0] % (8 * ADAMW_STEPS) == 0 for w in ws) else 1
    c1 = 1.0 - ADAM_B1 ** ADAM_STEP
    c2 = 1.0 - ADAM_B2 ** ADAM_STEP

    def body(*refs):
        for k in range(n):
            w, g, m, v = (refs[j * n + k][...] for j in range(4))
            m2 = ADAM_B1 * m + (1.0 - ADAM_B1) * g
            v2 = ADAM_B2 * v + (1.0 - ADAM_B2) * (g * g)
            delta = -ADAM_LR * ((m2 * (1.0 / c1)) / (jnp.sqrt(v2 * (1.0 / c2)) + ADAM_EPS) + ADAM_WD * w)
            refs[4 * n + k][...] = delta
            refs[5 * n + k][...] = m2
            refs[6 * n + k][...] = v2

    specs = [pl.BlockSpec((w.shape[0] // steps, w.shape[1]), lambda i: (i, 0)) for w in ws]
    shapes = [jax.ShapeDtypeStruct(w.shape, F32) for w in ws]
    outs = _call(
        body,
        name=name,
        grid=(steps,),
        in_specs=specs * 4,
        out_specs=specs * 3,
        out_shape=shapes * 3,
        args=(*ws, *gs, *ms, *vs), sem=("parallel",), hosted=hosted)
    return outs[:n], outs[n:2 * n], outs[2 * n:]


def _adamw_reduced(chip, ws, psums, recvd, ms, vs, steps, name):
    n = len(ws)
    c1 = 1.0 - ADAM_B1 ** ADAM_STEP
    c2 = 1.0 - ADAM_B2 ** ADAM_STEP

    def body(chip_ref, *refs):
        del chip_ref
        for k in range(n):
            w, m, v = (refs[j * n + k][...] for j in (0, 3, 4))
            got = refs[2 * n + k]
            g = refs[n + k][...].astype(F32) + got[0].astype(F32)
            g = g + got[1].astype(F32)
            g = g + got[2].astype(F32)
            m2 = ADAM_B1 * m + (1.0 - ADAM_B1) * g
            v2 = ADAM_B2 * v + (1.0 - ADAM_B2) * (g * g)
            refs[5 * n + k][...] = g
            refs[6 * n + k][...] = -ADAM_LR * (
                (m2 * (1.0 / c1)) / (jnp.sqrt(v2 * (1.0 / c2)) + ADAM_EPS) + ADAM_WD * w)
            refs[7 * n + k][...] = m2
            refs[8 * n + k][...] = v2

    def blk(w):
        return (w.shape[0] // steps, w.shape[1])

    own = [pl.BlockSpec(blk(w), lambda i, chip_ref: (i, 0)) for w in ws]
    psum = [pl.BlockSpec(blk(w), lambda i, chip_ref: (chip_ref[0] * steps + i, 0)) for w in ws]
    recv = [pl.BlockSpec((3,) + blk(w), lambda i, chip_ref: (0, i, 0)) for w in ws]
    shapes = [jax.ShapeDtypeStruct(w.shape, F32) for w in ws]
    outs = pl.pallas_call(
        body,
        name=name,
        grid_spec=pltpu.PrefetchScalarGridSpec(
            num_scalar_prefetch=1, grid=(steps,), in_specs=own + psum + recv + own + own, out_specs=own * 4),
        out_shape=shapes * 4,
        compiler_params=_cparams(("parallel",)),
    )(chip, *ws, *psums, *[r.reshape((3,) + w.shape) for r, w in zip(recvd, ws)], *ms, *vs)
    return outs[:n], outs[n:2 * n], outs[2 * n:3 * n], outs[3 * n:]


def _bias_b():
    pad = B_PREV * CHUNK
    slopes = np.array([2.0 ** (-8.0 * (i + 1) / B_Q_HEADS) for i in range(B_Q_HEADS)], dtype=np.float32)
    dist = np.abs(np.arange(TQ)[:, None] - np.arange(TQ + pad)[None, :] + pad).astype(np.float32)
    bias = -slopes.reshape(B_Q_HEADS, 1, 1) * dist[None]
    qc = (np.arange(TQ)[:, None] + pad) // CHUNK
    kc = np.arange(TQ + pad)[None, :] // CHUNK
    allowed = (kc <= qc) & (kc >= qc - B_PREV)
    return np.where(allowed[None], bias, np.float32(NEG_INF)).astype(np.float32)


def kernel(x, ffn1_norm, ffn1_w_gate, ffn1_w_up, ffn1_w_down, mix_norm, w_in, rel_bias, sinks, w_proj_a, w_proj_b, w_out, ffn2_norm, ffn2_w_gate, ffn2_w_up, ffn2_w_down, final_norm, loss_target, m_ffn1_norm, m_ffn1_w_gate, m_ffn1_w_up, m_ffn1_w_down, m_mix_norm, m_w_in, m_rel_bias, m_sinks, m_w_proj_a, m_w_proj_b, m_w_out, m_ffn2_norm, m_ffn2_w_gate, m_ffn2_w_up, m_ffn2_w_down, m_final_norm, v_ffn1_norm, v_ffn1_w_gate, v_ffn1_w_up, v_ffn1_w_down, v_mix_norm, v_w_in, v_rel_bias, v_sinks, v_w_proj_a, v_w_proj_b, v_w_out, v_ffn2_norm, v_ffn2_w_gate, v_ffn2_w_up, v_ffn2_w_down, v_final_norm):
    bsz, s_len, _ = x.shape
    t = bsz * s_len
    core = lax.axis_index("c").astype(jnp.int32).reshape(1)
    chip = (2 * lax.axis_index("x") + lax.axis_index("y")).astype(jnp.int32).reshape(1)

    proj_rows = jnp.concatenate([w_proj_a.T, w_proj_b.T], axis=1)
    sh_g1, sh_u1, sh_d1, sh_in, sh_proj, sh_out, sh_g2, sh_u2, sh_d2 = _to_bf16(
        [ffn1_w_gate.T, ffn1_w_up.T, ffn1_w_down, w_in.T, proj_rows, w_out, ffn2_w_gate.T, ffn2_w_up.T, ffn2_w_down],
        "weights_to_bf16")

    gather_up1 = _Gather([sh_g1, sh_u1])
    far = jnp.broadcast_to(rel_bias[:, REL_TABLE - 1:REL_TABLE], (A_HEADS, REL_WRAP // 2))
    tv = jnp.concatenate([far, jnp.flip(rel_bias, axis=1), jnp.zeros((A_HEADS, REL_WRAP // 2 - REL_TABLE), F32)], axis=1)
    bias_a = _bias_a_build(tv.reshape(A_HEADS, 1, REL_WRAP), hosted=[gather_up1])
    wg1, wu1 = gather_up1.result
    gather_down1 = _Gather([sh_d1, sh_in])
    gather_out = _Gather([sh_proj, sh_out])
    gather_ffn2_gate = _Gather([sh_g2])
    gather_ffn2_rest = _Gather([sh_u2, sh_d2])

    x0 = x.reshape(t, D_MODEL)
    tgt = loss_target.reshape(t, D_MODEL)
    gam1, gam2, gam3, gam4 = (g.reshape(1, D_MODEL) for g in (ffn1_norm, mix_norm, ffn2_norm, final_norm))

    h1, g1, u1, a1 = _ffn_up(x0, gam1, wg1, wu1, "ffn1_up", hosted=[gather_down1])
    wd1, win_t = gather_down1.result
    x1 = _ffn_down(x0, a1, wd1, "ffn1_down", hosted=[gather_out])
    proj_t, wout = gather_out.result
    h2, qkv_a, qkv_b, gates = _proj_fwd(x1, gam2, win_t, hosted=[gather_ffn2_gate])
    (wg2,) = gather_ffn2_gate.result
    qkv_a3 = qkv_a.reshape(bsz, s_len, QKV_A)
    qkv_b3 = qkv_b.reshape(bsz, s_len, QKV_B)

    bias_b = jnp.asarray(_bias_b())
    sink_rows = jnp.broadcast_to(sinks.reshape(B_Q_HEADS, 1, 1), (B_Q_HEADS, 8, LANES))

    oa, lse_a = _attn_a_fwd(qkv_a3, bias_a, hosted=[gather_ffn2_rest])
    oa = oa.reshape(t, A_WIDTH)
    wu2, wd2 = gather_ffn2_rest.result
    ob, lse_b = _attn_b_fwd(qkv_b3, bias_b, sink_rows)
    ob = ob.reshape(t, B_Q_WIDTH)
    x2, ya, yb, mg = _mix_out_fwd(x1, oa, ob, gates, proj_t, wout)
    h3, g2, u2, a2, x3 = _ffn_fwd(x2, gam3, wg2, wu2, wd2, "ffn2_fwd")

    dx2, dg2, du2, db2, dgam3, dgam4, loss_part = _ffn_bwd_head(x3, gam4, tgt, x2, gam3, g2, u2, wg2, wu2, wd2,
                                                                "ffn2_bwd")
    gw_ffn2 = [_mm_tn([dg2], h3, "grad_ffn2_gate"), _mm_tn([du2], h3, "grad_ffn2_up"),
               _mm_tn([a2], db2, "grad_ffn2_down")]
    pairx_ffn2 = _PairExchange(gw_ffn2)
    doa, dob, dgates, gw_out, gw_proj = _mix_out_bwd(dx2, gates, ya, yb, mg, oa, ob, proj_t, wout,
                                                     hosted=[pairx_ffn2])
    psum_ffn2 = _pair_sum(core, gw_ffn2, pairx_ffn2.result, "pair_sum_ffn2")

    chipx_ffn2 = _ChipExchange(psum_ffn2)
    dqa, dka, dva, dbias_a = _attn_a_bwd(qkv_a3, bias_a, doa.reshape(bsz, s_len, A_WIDTH),
                                         oa.reshape(bsz, s_len, A_WIDTH), lse_a, hosted=[chipx_ffn2])
    pairx_out = _PairExchange([gw_proj, gw_out])
    dqb, dkvb, dsink = _attn_b_bwd(qkv_b3, bias_b, sink_rows, dob.reshape(bsz, s_len, B_Q_WIDTH),
                                   ob.reshape(bsz, s_len, B_Q_WIDTH), lse_b, hosted=[pairx_out])
    drel_lanes = _relbias_grad(dbias_a)
    dproj = [dqa.reshape(t, A_WIDTH), dka.reshape(t, A_WIDTH), dva.reshape(t, A_WIDTH), dqb.reshape(t, B_Q_WIDTH),
             dkvb.reshape(t, 2 * B_KV_WIDTH), dgates]

    gw_in = _mm_tn(dproj, h2, "grad_w_in")
    pairx_in = _PairExchange([gw_in])
    psum_out = _pair_sum(core, [gw_proj, gw_out], pairx_out.result, "pair_sum_mix")
    chipx_out = _ChipExchange(psum_out)
    dx1, db1, dgam2 = _proj_bwd(dx2, x1, gam2, dproj, win_t, hosted=[pairx_in, chipx_out])
    psum_in = _pair_sum(core, [gw_in], pairx_in.result, "pair_sum_w_in")
    gw_d1 = _mm_tn([a1], db1, "grad_ffn1_down")

    chipx_in = _ChipExchange(psum_in)
    pairx_d1 = _PairExchange([gw_d1])
    dg1, du1 = _ffn_bwd_act(dx1, g1, u1, wd1, "ffn1_bwd_act", hosted=[chipx_in, pairx_d1])
    psum_d1 = _pair_sum(core, [gw_d1], pairx_d1.result, "pair_sum_ffn1_down")
    chipx_d1 = _ChipExchange(psum_d1)
    gw_g1 = _mm_tn([dg1], h1, "grad_ffn1_gate", hosted=[chipx_d1])
    from_sibling_g1 = _exchange_alone(_PairExchange([gw_g1]), "pair_exchange_ffn1_gate")
    psum_g1 = _pair_sum(core, [gw_g1], from_sibling_g1, "pair_sum_ffn1_gate")
    chipx_g1 = _ChipExchange(psum_g1)
    gw_u1 = _mm_tn([du1], h1, "grad_ffn1_up", hosted=[chipx_g1])
    from_sibling_u1 = _exchange_alone(_PairExchange([gw_u1]), "pair_exchange_ffn1_up")
    psum_u1 = _pair_sum(core, [gw_u1], from_sibling_u1, "pair_sum_ffn1_up")
    chipx_u1 = _ChipExchange(psum_u1)
    dx0, dgam1 = _ffn_bwd_in(dx1, x0, gam1, dg1, du1, wg1, wu1, "ffn1_bwd_in", hosted=[chipx_u1])

    (g_proj,) = _final_sum(chip, psum_out[0:1], chipx_out.result[0:1], "grad_sum_proj")
    grads = {"w_proj_a": g_proj[:, 0:A_WIDTH].T, "w_proj_b": g_proj[:, A_WIDTH:].T}

    def row_of(v):
        return jnp.pad(v.reshape(1, -1), ((0, 0), (0, D_MODEL - v.size)))

    def table_rows(v):
        return jnp.pad(v, ((0, 0), (0, D_MODEL - REL_TABLE)))

    drel_local = jnp.flip(drel_lanes[:, 0, 0:REL_TABLE], axis=1)
    small_part = jnp.concatenate(
        [jnp.sum(dgam1, axis=0, keepdims=True), jnp.sum(dgam2, axis=0, keepdims=True),
         jnp.sum(dgam3, axis=0, keepdims=True), jnp.sum(dgam4, axis=0, keepdims=True),
         row_of(jnp.sum(loss_part)), row_of(dsink[:, 0, 0]), jnp.zeros((2, D_MODEL), F32),
         table_rows(drel_local)], axis=0)
    small = _all_reduce_small(small_part)
    loss = small[4, 0]

    def pack(n1, n2, n3, n4, sk, tb):
        return jnp.concatenate([n1.reshape(1, -1), n2.reshape(1, -1), n3.reshape(1, -1), n4.reshape(1, -1),
                                jnp.zeros((1, D_MODEL), F32), row_of(sk), jnp.zeros((2, D_MODEL), F32), table_rows(tb)],
                               axis=0)

    live = np.zeros((SMALL_ROWS, D_MODEL), np.float32)
    live[0:4] = 1.0
    live[5, 0:B_Q_HEADS] = 1.0
    live[8:16, 0:REL_TABLE] = 1.0
    small_g = small * jnp.asarray(live)
    sw = pack(ffn1_norm, mix_norm, ffn2_norm, final_norm, sinks, rel_bias)
    sm = pack(m_ffn1_norm, m_mix_norm, m_ffn2_norm, m_final_norm, m_sinks, m_rel_bias)
    sv = pack(v_ffn1_norm, v_mix_norm, v_ffn2_norm, v_final_norm, v_sinks, v_rel_bias)
    (sd,), (snm,), (snv,) = _adamw([sw], [small_g], [sm], [sv], "adamw_small")

    def unpack(p):
        return {"ffn1_norm": p[0], "mix_norm": p[1], "ffn2_norm": p[2], "final_norm": p[3],
                "sinks": p[5, 0:B_Q_HEADS], "rel_bias": p[8:16, 0:REL_TABLE]}

    grads.update(unpack(small_g))
    delta, new_m, new_v = unpack(sd), unpack(snm), unpack(snv)

    wmv = {
        "ffn1_w_gate": (ffn1_w_gate, m_ffn1_w_gate, v_ffn1_w_gate), "ffn1_w_up": (ffn1_w_up, m_ffn1_w_up, v_ffn1_w_up),
        "ffn1_w_down": (ffn1_w_down, m_ffn1_w_down, v_ffn1_w_down), "w_in": (w_in, m_w_in, v_w_in),
        "w_proj_a": (w_proj_a, m_w_proj_a, v_w_proj_a), "w_proj_b": (w_proj_b, m_w_proj_b, v_w_proj_b),
        "w_out": (w_out, m_w_out, v_w_out),
        "ffn2_w_gate": (ffn2_w_gate, m_ffn2_w_gate, v_ffn2_w_gate), "ffn2_w_up": (ffn2_w_up, m_ffn2_w_up, v_ffn2_w_up),
        "ffn2_w_down": (ffn2_w_down, m_ffn2_w_down, v_ffn2_w_down),
    }
    row_form_names = ("ffn1_w_gate", "ffn1_w_up", "w_in", "ffn2_w_gate", "ffn2_w_up")

    def form(n, a):
        return a.T if n in row_form_names else a

    def reduced_group(gname, names, psums, recvd, steps):
        gs_, ds_, ms_, vs_ = _adamw_reduced(
            chip, [form(n, wmv[n][0]) for n in names], psums, recvd, [form(n, wmv[n][1]) for n in names],
            [form(n, wmv[n][2]) for n in names], steps, gname)
        for n, g_, d_, m_, v_ in zip(names, gs_, ds_, ms_, vs_):
            grads[n], delta[n], new_m[n], new_v[n] = form(n, g_), form(n, d_), form(n, m_), form(n, v_)

    reduced_group("adamw_ffn", ["ffn1_w_gate", "ffn1_w_up", "ffn1_w_down", "ffn2_w_gate", "ffn2_w_up", "ffn2_w_down"],
                  psum_g1 + psum_u1 + psum_d1 + psum_ffn2,
                  chipx_g1.result + chipx_u1.result + chipx_d1.result + chipx_ffn2.result, 11)
    reduced_group("adamw_in_out", ["w_in", "w_out"], psum_in + psum_out[1:2], chipx_in.result + chipx_out.result[1:2], 2)
    names = ["w_proj_a", "w_proj_b"]
    ds_, ms_, vs_ = _adamw([wmv[n][0] for n in names], [grads[n] for n in names], [wmv[n][1] for n in names],
                           [wmv[n][2] for n in names], "adamw_proj")
    for n, d_, m_, v_ in zip(names, ds_, ms_, vs_):
        delta[n], new_m[n], new_v[n] = d_, m_, v_

    order = ["ffn1_norm", "ffn1_w_gate", "ffn1_w_up", "ffn1_w_down", "mix_norm", "w_in", "rel_bias", "sinks",
             "w_proj_a", "w_proj_b", "w_out", "ffn2_norm", "ffn2_w_gate", "ffn2_w_up", "ffn2_w_down", "final_norm"]
    grad_x = dx0.reshape(bsz, s_len, D_MODEL)
    return (loss, grad_x, *[grads[n] for n in order], *[delta[n] for n in order], *[new_m[n] for n in order],
            *[new_v[n] for n in order])
```

```python
import numpy as np
import jax
import jax.numpy as jnp
from jax import lax
from jax.experimental import pallas as pl
from jax.experimental.pallas import tpu as pltpu

F32 = jnp.float32
BF16 = jnp.bfloat16

D_MODEL = 1024
D_FF = 2816
CHUNK = 64
D_HEAD = 64
A_HEADS = 8
A_PREV = 8
MAX_REL = 128
B_Q_HEADS = 8
B_KV_HEADS = 2
B_GROUP = B_Q_HEADS // B_KV_HEADS
B_PREV = 2
REL_TABLE = (CHUNK - 1) + MAX_REL + 1
A_WIDTH = A_HEADS * D_HEAD
B_Q_WIDTH = B_Q_HEADS * D_HEAD
B_KV_WIDTH = B_KV_HEADS * D_HEAD
QKV_A = 3 * A_WIDTH
QKV_B = B_Q_WIDTH + 2 * B_KV_WIDTH
IN_WIDTH = QKV_A + QKV_B + 2 * D_MODEL
EPS = 1e-6
NEG_INF = -1e30
SCALE = 1.0 / 8.0

ADAM_LR = 0.001
ADAM_B1 = 0.9
ADAM_B2 = 0.999
ADAM_EPS = 1e-08
ADAM_WD = 0.01
ADAM_STEP = 10

N_DEV = 8
N_CHIP = 4
MESH = pl.DeviceIdType.MESH

LANES = 128
TQ = 256
TM = 256
FC = 256
VMEM_LIMIT = 56 << 20


def _cparams(sem, vmem=VMEM_LIMIT):
    return pltpu.CompilerParams(dimension_semantics=sem, vmem_limit_bytes=vmem)


def _dot_nt(a, b):
    return lax.dot_general(a, b, (((1,), (1,)), ((), ())), preferred_element_type=F32)


def _dot_nn(a, b):
    return lax.dot_general(a, b, (((1,), (0,)), ((), ())), preferred_element_type=F32)


def _dot_tn(a, b):
    return lax.dot_general(a, b, (((0,), (0,)), ((), ())), preferred_element_type=F32)


def _resident(shape):
    nd = len(shape)
    return pl.BlockSpec(shape, lambda *_: (0,) * nd, pipeline_mode=pl.Buffered(1))


def _rows(tm, width):
    return pl.BlockSpec((tm, width), lambda i: (i, 0))


def _colsum8(v):
    tm, n = v.shape
    return jnp.sum(v.reshape(tm // 8, 8, n), axis=0)


def _rms(x):
    r = lax.rsqrt(jnp.mean(x * x, axis=-1, keepdims=True) + EPS)
    return x * r, r


def _rms_bwd(dh, xh, r, gamma):
    dxh = dh * gamma
    dx = r * (dxh - xh * jnp.mean(dxh * xh, axis=-1, keepdims=True))
    return dx, _colsum8(dh * xh)


def _hbm():
    return pl.BlockSpec(memory_space=pltpu.HBM)


def _call(body, *, name, grid, in_specs, out_specs, out_shape, args, sem, scratch_shapes=(), hosted=()):
    in_specs, out_specs, out_shape = list(in_specs), list(out_specs), list(out_shape)
    scratch_shapes = list(scratch_shapes)
    if not hosted:
        return pl.pallas_call(body, name=name, grid=grid, in_specs=in_specs, out_specs=out_specs, out_shape=out_shape,
                              scratch_shapes=scratch_shapes, compiler_params=_cparams(sem))(*args)
    n_in, n_out, n_scr = len(in_specs), len(out_specs), len(scratch_shapes)
    x_in = [a for x in hosted for a in x.inputs]
    x_out = [s for x in hosted for s in x.out_shape]
    x_scr = [s for x in hosted for s in x.scratch]
    steps = int(np.prod(grid))
    forward_step = max(steps - 3, 0)
    relay_step = min((5 * steps) // 8, forward_step)

    def wrapped(*refs):
        pos = [0]

        def take(k):
            pos[0] += k
            return refs[pos[0] - k:pos[0]]

        ins, xin, outs, xout, scr, xscr = (take(k) for k in (n_in, len(x_in), n_out, len(x_out), n_scr, len(x_scr)))
        step = 0
        for axis, extent in enumerate(grid):
            step = step * extent + pl.program_id(axis)
        own, oi, oo, osc = [], 0, 0, 0
        for x in hosted:
            own.append((xin[oi:oi + len(x.inputs)], xout[oo:oo + len(x.out_shape)], xscr[osc:osc + len(x.scratch)]))
            oi, oo, osc = oi + len(x.inputs), oo + len(x.out_shape), osc + len(x.scratch)

        def phase(method):
            for x, (i_, o_, s_) in zip(hosted, own):
                getattr(x, method)(i_, o_, s_)

        pl.when(step == 0)(lambda: phase("start"))
        body(*ins, *outs, *scr)
        pl.when(step == relay_step)(lambda: phase("relay"))
        pl.when(step == forward_step)(lambda: phase("forward"))
        pl.when(step == steps - 1)(lambda: phase("finish"))

    res = pl.pallas_call(
        wrapped, name=name, grid=grid, in_specs=in_specs + [_hbm()] * len(x_in),
        out_specs=out_specs + [_hbm()] * len(x_out), out_shape=out_shape + x_out,
        scratch_shapes=scratch_shapes + x_scr, compiler_params=_cparams(("arbitrary",) * len(grid)))(*args, *x_in)
    rest = list(res[n_out:])
    for x in hosted:
        x.result, rest = rest[:len(x.out_shape)], rest[len(x.out_shape):]
    return list(res[:n_out])


def _to_bf16(arrays, name):
    n = len(arrays)

    def body(*refs):
        for k in range(n):
            refs[n + k][...] = refs[k][...].astype(BF16)

    specs = [pl.BlockSpec(a.shape, lambda i: (0, 0)) for a in arrays]
    return pl.pallas_call(
        body, name=name, grid=(1,), in_specs=specs, out_specs=specs,
        out_shape=[jax.ShapeDtypeStruct(a.shape, BF16) for a in arrays],
        compiler_params=_cparams(("arbitrary",)))(*arrays)


def _ffn_fwd(x, gamma, wg_t, wu_t, wd, name, hosted=()):
    t = x.shape[0]
    f = wg_t.shape[0]

    def body(x_ref, gam_ref, wg_ref, wu_ref, wd_ref, h_ref, g_ref, u_ref, a_ref, y_ref):
        xv = x_ref[...]
        xh, _ = _rms(xv)
        h = (xh * gam_ref[...]).astype(BF16)
        h_ref[...] = h
        for j in range(f // FC):
            sl = slice(j * FC, (j + 1) * FC)
            g = _dot_nt(h, wg_ref[sl, :])
            u = _dot_nt(h, wu_ref[sl, :])
            g_ref[:, sl] = g.astype(BF16)
            u_ref[:, sl] = u.astype(BF16)
            a_ref[:, sl] = (g * jax.nn.sigmoid(g) * u).astype(BF16)
        y_ref[...] = xv + 0.5 * _dot_nn(a_ref[...], wd_ref[...])

    return _call(
        body,
        name=name,
        grid=(t // TM,),
        in_specs=[_rows(TM, D_MODEL), _resident((1, D_MODEL)), _resident((f, D_MODEL)), _resident((f, D_MODEL)),
                  _resident((f, D_MODEL))],
        out_specs=[_rows(TM, D_MODEL), _rows(TM, f), _rows(TM, f), _rows(TM, f), _rows(TM, D_MODEL)],
        out_shape=[jax.ShapeDtypeStruct((t, D_MODEL), BF16), jax.ShapeDtypeStruct((t, f), BF16),
                   jax.ShapeDtypeStruct((t, f), BF16), jax.ShapeDtypeStruct((t, f), BF16),
                   jax.ShapeDtypeStruct((t, D_MODEL), F32)],
        args=(x, gamma, wg_t, wu_t, wd), sem=("parallel",), hosted=hosted)


def _ffn_up(x, gamma, wg_t, wu_t, name, hosted=()):
    t = x.shape[0]
    f = wg_t.shape[0]

    def body(x_ref, gam_ref, wg_ref, wu_ref, h_ref, g_ref, u_ref, a_ref):
        xh, _ = _rms(x_ref[...])
        h = (xh * gam_ref[...]).astype(BF16)
        h_ref[...] = h
        for j in range(f // FC):
            sl = slice(j * FC, (j + 1) * FC)
            g = _dot_nt(h, wg_ref[sl, :])
            u = _dot_nt(h, wu_ref[sl, :])
            g_ref[:, sl] = g.astype(BF16)
            u_ref[:, sl] = u.astype(BF16)
            a_ref[:, sl] = (g * jax.nn.sigmoid(g) * u).astype(BF16)

    return _call(
        body,
        name=name,
        grid=(t // TM,),
        in_specs=[_rows(TM, D_MODEL), _resident((1, D_MODEL)), _resident((f, D_MODEL)), _resident((f, D_MODEL))],
        out_specs=[_rows(TM, D_MODEL), _rows(TM, f), _rows(TM, f), _rows(TM, f)],
        out_shape=[jax.ShapeDtypeStruct((t, D_MODEL), BF16), jax.ShapeDtypeStruct((t, f), BF16),
                   jax.ShapeDtypeStruct((t, f), BF16), jax.ShapeDtypeStruct((t, f), BF16)],
        args=(x, gamma, wg_t, wu_t), sem=("parallel",), hosted=hosted)


def _ffn_down(x, a_act, wd, name, hosted=()):
    t = x.shape[0]
    f = wd.shape[0]

    def body(x_ref, a_ref, wd_ref, y_ref):
        y_ref[...] = x_ref[...] + 0.5 * _dot_nn(a_ref[...], wd_ref[...])

    return _call(
        body,
        name=name,
        grid=(t // TM,),
        in_specs=[_rows(TM, D_MODEL), _rows(TM, f), _resident((f, D_MODEL))],
        out_specs=[_rows(TM, D_MODEL)],
        out_shape=[jax.ShapeDtypeStruct((t, D_MODEL), F32)],
        args=(x, a_act, wd), sem=("parallel",), hosted=hosted)[0]


def _ffn_bwd_head(y, gamma_f, target, x, gamma, g_act, u_act, wg_t, wu_t, wd, name):
    t = x.shape[0]
    f = wg_t.shape[0]

    def body(y_ref, gamf_ref, t_ref, x_ref, gam_ref, g_ref, u_ref, wg_ref, wu_ref, wd_ref, dx_ref, dg_ref, du_ref,
             db_ref, dgam_ref, dgamf_ref, loss_ref):
        yh, ry = _rms(y_ref[...])
        gam_f = gamf_ref[...]
        e = yh * gam_f - t_ref[...]
        dv, dgam_f = _rms_bwd(e * (1.0 / D_MODEL), yh, ry, gam_f)
        db = (0.5 * dv).astype(BF16)
        db_ref[...] = db
        for j in range(f // FC):
            sl = slice(j * FC, (j + 1) * FC)
            da = _dot_nt(db, wd_ref[sl, :])
            g = g_ref[:, sl].astype(F32)
            u = u_ref[:, sl].astype(F32)
            s = jax.nn.sigmoid(g)
            dg_ref[:, sl] = (da * u * (s * (1.0 + g * (1.0 - s)))).astype(BF16)
            du_ref[:, sl] = (da * (g * s)).astype(BF16)
        dh = _dot_nn(dg_ref[...], wg_ref[...]) + _dot_nn(du_ref[...], wu_ref[...])
        xh, r = _rms(x_ref[...])
        dxn, dgam = _rms_bwd(dh, xh, r, gam_ref[...])
        dx_ref[...] = dv + dxn

        @pl.when(pl.program_id(0) == 0)
        def _():
            dgam_ref[...] = jnp.zeros_like(dgam_ref)
            dgamf_ref[...] = jnp.zeros_like(dgamf_ref)
            loss_ref[...] = jnp.zeros_like(loss_ref)

        dgam_ref[...] += dgam
        dgamf_ref[...] += dgam_f
        loss_ref[...] += _colsum8(e * e) * (0.5 / D_MODEL)

    acc = pl.BlockSpec((8, D_MODEL), lambda i: (0, 0))
    return _call(
        body,
        name=name,
        grid=(t // TM,),
        in_specs=[_rows(TM, D_MODEL), _resident((1, D_MODEL)), _rows(TM, D_MODEL), _rows(TM, D_MODEL),
                  _resident((1, D_MODEL)), _rows(TM, f), _rows(TM, f),
                  _resident((f, D_MODEL)), _resident((f, D_MODEL)), _resident((f, D_MODEL))],
        out_specs=[_rows(TM, D_MODEL), _rows(TM, f), _rows(TM, f), _rows(TM, D_MODEL), acc, acc, acc],
        out_shape=[jax.ShapeDtypeStruct((t, D_MODEL), F32), jax.ShapeDtypeStruct((t, f), BF16),
                   jax.ShapeDtypeStruct((t, f), BF16), jax.ShapeDtypeStruct((t, D_MODEL), BF16),
                   jax.ShapeDtypeStruct((8, D_MODEL), F32), jax.ShapeDtypeStruct((8, D_MODEL), F32),
                   jax.ShapeDtypeStruct((8, D_MODEL), F32)],
        args=(y, gamma_f, target, x, gamma, g_act, u_act, wg_t, wu_t, wd), sem=("arbitrary",))


def _ffn_bwd_act(d, g_act, u_act, wd, name, hosted=()):
    t = d.shape[0]
    f = wd.shape[0]

    def body(d_ref, g_ref, u_ref, wd_ref, dg_ref, du_ref):
        db = (0.5 * d_ref[...]).astype(BF16)
        for j in range(f // FC):
            sl = slice(j * FC, (j + 1) * FC)
            da = _dot_nt(db, wd_ref[sl, :])
            g = g_ref[:, sl].astype(F32)
            u = u_ref[:, sl].astype(F32)
            s = jax.nn.sigmoid(g)
            dg_ref[:, sl] = (da * u * (s * (1.0 + g * (1.0 - s)))).astype(BF16)
            du_ref[:, sl] = (da * (g * s)).astype(BF16)

    return _call(
        body,
        name=name,
        grid=(t // TM,),
        in_specs=[_rows(TM, D_MODEL), _rows(TM, f), _rows(TM, f), _resident((f, D_MODEL))],
        out_specs=[_rows(TM, f), _rows(TM, f)],
        out_shape=[jax.ShapeDtypeStruct((t, f), BF16), jax.ShapeDtypeStruct((t, f), BF16)],
        args=(d, g_act, u_act, wd), sem=("parallel",), hosted=hosted)


def _ffn_bwd_in(d, x, gamma, dg, du, wg_t, wu_t, name, hosted=()):
    t = x.shape[0]
    f = wg_t.shape[0]

    def body(d_ref, x_ref, gam_ref, dg_ref, du_ref, wg_ref, wu_ref, dx_ref, dgam_ref):
        dh = _dot_nn(dg_ref[...], wg_ref[...]) + _dot_nn(du_ref[...], wu_ref[...])
        xh, r = _rms(x_ref[...])
        dxn, dgam = _rms_bwd(dh, xh, r, gam_ref[...])
        dx_ref[...] = d_ref[...] + dxn

        @pl.when(pl.program_id(0) == 0)
        def _():
            dgam_ref[...] = jnp.zeros_like(dgam_ref)

        dgam_ref[...] += dgam

    return _call(
        body,
        name=name,
        grid=(t // TM,),
        in_specs=[_rows(TM, D_MODEL), _rows(TM, D_MODEL), _resident((1, D_MODEL)), _rows(TM, f), _rows(TM, f),
                  _resident((f, D_MODEL)), _resident((f, D_MODEL))],
        out_specs=[_rows(TM, D_MODEL), pl.BlockSpec((8, D_MODEL), lambda i: (0, 0))],
        out_shape=[jax.ShapeDtypeStruct((t, D_MODEL), F32), jax.ShapeDtypeStruct((8, D_MODEL), F32)],
        args=(d, x, gamma, dg, du, wg_t, wu_t), sem=("arbitrary",), hosted=hosted)


def _mm_tn(pieces, b, name, tile=256, hosted=()):
    t, n = b.shape
    npc = len(pieces)
    counts = [p.shape[1] // tile for p in pieces]
    los = [sum(counts[:k]) for k in range(npc)]
    total = sum(counts)

    def body(*refs):
        a_refs, b_ref, o_ref = refs[:npc], refs[npc], refs[npc + 1]
        i = pl.program_id(0)
        for k in range(npc):
            @pl.when(jnp.logical_and(i >= los[k], i < los[k] + counts[k]))
            def _(k=k):
                o_ref[...] = _dot_tn(a_refs[k][...], b_ref[...]).astype(BF16)

    def a_spec(k):
        return pl.BlockSpec((t, tile), lambda i: (0, jnp.clip(i - los[k], 0, counts[k] - 1)))

    return _call(
        body,
        name=name,
        grid=(total,),
        in_specs=[a_spec(k) for k in range(npc)] + [_resident((t, n))],
        out_specs=[pl.BlockSpec((tile, n), lambda i: (i, 0))],
        out_shape=[jax.ShapeDtypeStruct((total * tile, n), BF16)],
        args=(*pieces, b), sem=("parallel",), hosted=hosted)[0]


def _proj_fwd(x, gamma, win_t, hosted=()):
    t = x.shape[0]

    def body(x_ref, gam_ref, w_ref, h_ref, qa_ref, qb_ref, gt_ref):
        xh, _ = _rms(x_ref[...])
        h = (xh * gam_ref[...]).astype(BF16)
        h_ref[...] = h
        for j in range(QKV_A // FC):
            qa_ref[:, j * FC:(j + 1) * FC] = _dot_nt(h, w_ref[j * FC:(j + 1) * FC, :]).astype(BF16)
        for j in range(QKV_B // FC):
            lo = QKV_A + j * FC
            qb_ref[:, j * FC:(j + 1) * FC] = _dot_nt(h, w_ref[lo:lo + FC, :]).astype(BF16)
        for j in range(2 * D_MODEL // FC):
            lo = QKV_A + QKV_B + j * FC
            gt_ref[:, j * FC:(j + 1) * FC] = _dot_nt(h, w_ref[lo:lo + FC, :])

    return _call(
        body,
        name="proj_fwd",
        grid=(t // TM,),
        in_specs=[_rows(TM, D_MODEL), _resident((1, D_MODEL)), _resident((IN_WIDTH, D_MODEL))],
        out_specs=[_rows(TM, D_MODEL), _rows(TM, QKV_A), _rows(TM, QKV_B), _rows(TM, 2 * D_MODEL)],
        out_shape=[jax.ShapeDtypeStruct((t, D_MODEL), BF16), jax.ShapeDtypeStruct((t, QKV_A), BF16),
                   jax.ShapeDtypeStruct((t, QKV_B), BF16), jax.ShapeDtypeStruct((t, 2 * D_MODEL), F32)],
        args=(x, gamma, win_t), sem=("parallel",), hosted=hosted)


def _proj_bwd(d, x, gamma, pieces, win_t, hosted=()):
    t = x.shape[0]
    npc = len(pieces)
    widths = [p.shape[1] for p in pieces]
    los = [sum(widths[:k]) for k in range(npc)]

    def body(*refs):
        d_ref, x_ref, gam_ref = refs[:3]
        p_refs = refs[3:3 + npc]
        w_ref, dx_ref, db_ref, dgam_ref = refs[3 + npc:]
        dh = _dot_nn(p_refs[0][...], w_ref[0:widths[0], :])
        for k in range(1, npc):
            dh += _dot_nn(p_refs[k][...], w_ref[los[k]:los[k] + widths[k], :])
        xh, r = _rms(x_ref[...])
        dxn, dgam = _rms_bwd(dh, xh, r, gam_ref[...])
        dx = d_ref[...] + dxn
        dx_ref[...] = dx
        db_ref[...] = (0.5 * dx).astype(BF16)

        @pl.when(pl.program_id(0) == 0)
        def _():
            dgam_ref[...] = jnp.zeros_like(dgam_ref)

        dgam_ref[...] += dgam

    return _call(
        body,
        name="proj_bwd",
        grid=(t // TM,),
        in_specs=[_rows(TM, D_MODEL), _rows(TM, D_MODEL), _resident((1, D_MODEL))] + [_rows(TM, w) for w in widths]
        + [_resident((IN_WIDTH, D_MODEL))],
        out_specs=[_rows(TM, D_MODEL), _rows(TM, D_MODEL), pl.BlockSpec((8, D_MODEL), lambda i: (0, 0))],
        out_shape=[jax.ShapeDtypeStruct((t, D_MODEL), F32), jax.ShapeDtypeStruct((t, D_MODEL), BF16),
                   jax.ShapeDtypeStruct((8, D_MODEL), F32)],
        args=(d, x, gamma, *pieces, win_t), sem=("arbitrary",), hosted=hosted)


def _lane_half(shape):
    return lax.broadcasted_iota(jnp.int32, shape, len(shape) - 1) // D_HEAD


def _band_weights(q, kk, bias, sink, qs, pad):
    s = _band_scores(q, kk, bias, qs, pad)
    m = jnp.max(s, axis=-1, keepdims=True)
    if sink is not None:
        m = jnp.maximum(m, sink)
    return jnp.exp(s - m), m


def _band_scores(q, kk, bias, qs, pad):
    s = _dot_nt(q, kk) + bias
    if qs is not None:
        col = lax.broadcasted_iota(jnp.int32, s.shape, 1)
        s = jnp.where(col + qs >= pad, s, NEG_INF)
    return s


def _weighted_values(p, vv_ones, sink, m):
    r = _dot_nn(p.astype(BF16), vv_ones)
    den = r[:, LANES:2 * LANES]
    if sink is not None:
        den = den + jnp.exp(sink - m)
    return r[:, 0:LANES] / den, m + jnp.log(den[:, 0:1])


def _fill_padded(dst, src, pad):
    dst[0:pad, :] = jnp.zeros((pad,) + dst.shape[1:], dst.dtype)
    dst[pad:, :] = src


FWD_PAIRS = 4
BWD_PAIRS = 4


def _attn_a_fwd(qkv, bias, hosted=()):
    bsz, s_len, _ = qkv.shape
    pad = A_PREV * CHUNK
    band = TQ + pad
    pp = FWD_PAIRS
    w = pp * LANES
    nb = A_WIDTH // w

    def body(q_ref, k_ref, v_ref, b_ref, o_ref, l_ref, kp, vp):
        i = pl.program_id(2)

        @pl.when(i == 0)
        def _():
            _fill_padded(kp, k_ref[...], pad)
            _fill_padded(vp, v_ref[...], pad)

        qs = pl.multiple_of(i * TQ, TQ)
        half = _lane_half((1, LANES))

        ones = jnp.ones((band, LANES), BF16)

        def block(masked):
            for pr in range(pp):
                sl = slice(pr * LANES, (pr + 1) * LANES)
                kk = kp[pl.ds(qs, band), sl]
                vv = jnp.concatenate([vp[pl.ds(qs, band), sl], ones], axis=1)
                q = q_ref[:, sl] * SCALE
                outs = []
                for j in range(2):
                    qm = jnp.where(half == j, q, jnp.zeros_like(q))
                    p, m = _band_weights(qm, kk, b_ref[2 * pr + j], None, qs if masked else None, pad)
                    o, lse = _weighted_values(p, vv, None, m)
                    outs.append(o)
                    l_ref[:, 2 * pr + j:2 * pr + j + 1] = lse
                o_ref[:, sl] = jnp.where(half == 0, outs[0], outs[1]).astype(BF16)

        pl.when(i < pad // TQ)(lambda: block(True))
        pl.when(i >= pad // TQ)(lambda: block(False))

    return _call(
        body,
        name="attn_a_fwd",
        grid=(bsz, nb, s_len // TQ),
        in_specs=[pl.BlockSpec((None, TQ, w), lambda b, g, i: (b, i, g)),
                  pl.BlockSpec((None, s_len, w), lambda b, g, i: (b, 0, nb + g)),
                  pl.BlockSpec((None, s_len, w), lambda b, g, i: (b, 0, 2 * nb + g)),
                  pl.BlockSpec((2 * pp, TQ, band), lambda b, g, i: (g, 0, 0))],
        out_specs=[pl.BlockSpec((None, TQ, w), lambda b, g, i: (b, i, g)),
                   pl.BlockSpec((None, TQ, 2 * pp), lambda b, g, i: (b, i, g))],
        out_shape=[jax.ShapeDtypeStruct((bsz, s_len, A_WIDTH), BF16),
                   jax.ShapeDtypeStruct((bsz, s_len, A_HEADS), F32)],
        scratch_shapes=[pltpu.VMEM((pad + s_len, w), BF16), pltpu.VMEM((pad + s_len, w), BF16)],
        args=(qkv, qkv, qkv, bias), sem=("arbitrary", "arbitrary", "arbitrary"), hosted=hosted)


def _attn_a_bwd(qkv, bias, do, o, lse, hosted=()):
    bsz, s_len, _ = qkv.shape
    pad = A_PREV * CHUNK
    band = TQ + pad
    n_i = s_len // TQ
    pp = BWD_PAIRS
    w = pp * LANES
    nb = A_WIDTH // w

    def body(q_ref, k_ref, v_ref, b_ref, do_ref, o_ref, l_ref, dq_ref, dk_ref, dv_ref, dbias_ref, kp, vp, dk_acc,
             dv_acc):
        b = pl.program_id(1)
        i = pl.program_id(2)

        @pl.when(i == 0)
        def _():
            _fill_padded(kp, k_ref[...], pad)
            _fill_padded(vp, v_ref[...], pad)
            dk_acc[...] = jnp.zeros_like(dk_acc)
            dv_acc[...] = jnp.zeros_like(dv_acc)

        @pl.when(jnp.logical_and(b == 0, i == 0))
        def _():
            dbias_ref[...] = jnp.zeros_like(dbias_ref)

        qs = pl.multiple_of(i * TQ, TQ)
        half = _lane_half((1, LANES))
        half_t = lax.broadcasted_iota(jnp.int32, (LANES, 1), 0) // D_HEAD

        def block(masked):
            for pr in range(pp):
                sl = slice(pr * LANES, (pr + 1) * LANES)
                kk = kp[pl.ds(qs, band), sl]
                vv = vp[pl.ds(qs, band), sl]
                q = q_ref[:, sl] * SCALE
                dd = do_ref[:, sl]
                od = dd.astype(F32) * o_ref[:, sl].astype(F32)
                dqs, dks, dvs = [], [], []
                for j in range(2):
                    hd = 2 * pr + j
                    qm = jnp.where(half == j, q, jnp.zeros_like(q))
                    dm = jnp.where(half == j, dd, jnp.zeros_like(dd))
                    s = _band_scores(qm, kk, b_ref[hd], qs if masked else None, pad)
                    pn = jnp.exp(s - l_ref[:, hd:hd + 1])
                    dp = _dot_nt(dm, vv)
                    delta = jnp.sum(jnp.where(half == j, od, 0.0), axis=-1, keepdims=True)
                    ds = pn * (dp - delta)
                    dbias_ref[hd] += ds[:, band - REL_COLS:]
                    dsb = ds.astype(BF16)
                    dqs.append(_dot_nn(dsb, kk))
                    dks.append(_dot_tn(q, dsb))
                    dvs.append(_dot_tn(dd, pn.astype(BF16)))
                dq_ref[:, sl] = (jnp.where(half == 0, dqs[0], dqs[1]) * SCALE).astype(BF16)
                dk_acc[sl, pl.ds(qs, band)] += jnp.where(half_t == 0, dks[0], dks[1])
                dv_acc[sl, pl.ds(qs, band)] += jnp.where(half_t == 0, dvs[0], dvs[1])

        pl.when(i < pad // TQ)(lambda: block(True))
        pl.when(i >= pad // TQ)(lambda: block(False))

        @pl.when(i == n_i - 1)
        def _():
            dk_ref[...] = dk_acc[:, pad:].T.astype(BF16)
            dv_ref[...] = dv_acc[:, pad:].T.astype(BF16)

    qspec = pl.BlockSpec((None, TQ, w), lambda g, b, i: (b, i, g))
    kvout = pl.BlockSpec((None, s_len, w), lambda g, b, i: (b, 0, g))
    wide = jax.ShapeDtypeStruct((bsz, s_len, A_WIDTH), BF16)
    return _call(
        body,
        name="attn_a_bwd",
        grid=(nb, bsz, n_i),
        in_specs=[qspec,
                  pl.BlockSpec((None, s_len, w), lambda g, b, i: (b, 0, nb + g)),
                  pl.BlockSpec((None, s_len, w), lambda g, b, i: (b, 0, 2 * nb + g)),
                  pl.BlockSpec((2 * pp, TQ, band), lambda g, b, i: (g, 0, 0)),
                  qspec, qspec,
                  pl.BlockSpec((None, TQ, 2 * pp), lambda g, b, i: (b, i, g))],
        out_specs=[qspec, kvout, kvout, pl.BlockSpec((2 * pp, TQ, REL_COLS), lambda g, b, i: (g, 0, 0))],
        out_shape=[wide, wide, wide, jax.ShapeDtypeStruct((A_HEADS, TQ, REL_COLS), F32)],
        scratch_shapes=[pltpu.VMEM((pad + s_len, w), BF16), pltpu.VMEM((pad + s_len, w), BF16),
                        pltpu.VMEM((w, pad + s_len), F32), pltpu.VMEM((w, pad + s_len), F32)],
        args=(qkv, qkv, qkv, bias, do, o, lse), sem=("arbitrary", "arbitrary", "arbitrary"), hosted=hosted)


def _fill_padded_dup(dst, src, pad, h, half):
    other = pltpu.roll(src, D_HEAD, 1)
    _fill_padded(dst, jnp.where(half == h, src, other), pad)


def _attn_b_fwd(qkv, bias, sink):
    bsz, s_len, _ = qkv.shape
    pad = B_PREV * CHUNK
    band = TQ + pad
    kcol = B_Q_WIDTH // LANES
    npair = B_Q_HEADS // 2

    def body(q_ref, k_ref, v_ref, b_ref, s_ref, o_ref, l_ref, kp, vp):
        i = pl.program_id(1)
        half = _lane_half((1, LANES))

        @pl.when(i == 0)
        def _():
            for h in range(B_KV_HEADS):
                _fill_padded_dup(kp.at[h], k_ref[...], pad, h, half)
                _fill_padded_dup(vp.at[h], v_ref[...], pad, h, half)

        qs = pl.multiple_of(i * TQ, TQ)

        ones = jnp.ones((band, LANES), BF16)

        def block(masked):
            for pr in range(npair):
                h = pr // (B_GROUP // 2)
                sl = slice(pr * LANES, (pr + 1) * LANES)
                kk = kp[h, pl.ds(qs, band), :]
                vv = jnp.concatenate([vp[h, pl.ds(qs, band), :], ones], axis=1)
                q = q_ref[:, sl] * SCALE
                outs = []
                for j in range(2):
                    qm = jnp.where(half == j, q, jnp.zeros_like(q))
                    sink = s_ref[2 * pr + j][0:1, 0:1]
                    p, m = _band_weights(qm, kk, b_ref[2 * pr + j], sink, qs if masked else None, pad)
                    o, lse = _weighted_values(p, vv, sink, m)
                    outs.append(o)
                    l_ref[:, 2 * pr + j:2 * pr + j + 1] = lse
                o_ref[:, sl] = jnp.where(half == 0, outs[0], outs[1]).astype(BF16)

        pl.when(i < -(-pad // TQ))(lambda: block(True))
        pl.when(i >= -(-pad // TQ))(lambda: block(False))

    return pl.pallas_call(
        body,
        name="attn_b_fwd",
        grid=(bsz, s_len // TQ),
        in_specs=[pl.BlockSpec((None, TQ, B_Q_WIDTH), lambda b, i: (b, i, 0)),
                  pl.BlockSpec((None, s_len, LANES), lambda b, i: (b, 0, kcol)),
                  pl.BlockSpec((None, s_len, LANES), lambda b, i: (b, 0, kcol + 1)),
                  pl.BlockSpec((B_Q_HEADS, TQ, band), lambda b, i: (0, 0, 0)),
                  pl.BlockSpec((B_Q_HEADS, 8, LANES), lambda b, i: (0, 0, 0))],
        out_specs=[pl.BlockSpec((None, TQ, B_Q_WIDTH), lambda b, i: (b, i, 0)),
                   pl.BlockSpec((None, TQ, B_Q_HEADS), lambda b, i: (b, i, 0))],
        out_shape=[jax.ShapeDtypeStruct((bsz, s_len, B_Q_WIDTH), BF16),
                   jax.ShapeDtypeStruct((bsz, s_len, B_Q_HEADS), F32)],
        scratch_shapes=[pltpu.VMEM((B_KV_HEADS, pad + s_len, LANES), BF16),
                        pltpu.VMEM((B_KV_HEADS, pad + s_len, LANES), BF16)],
        compiler_params=_cparams(("arbitrary", "arbitrary")),
    )(qkv, qkv, qkv, bias, sink)


def _attn_b_bwd(qkv, bias, sink, do, o, lse, hosted=()):
    bsz, s_len, _ = qkv.shape
    pad = B_PREV * CHUNK
    band = TQ + pad
    kcol = B_Q_WIDTH // LANES
    n_i = s_len // TQ
    pp = B_GROUP // 2

    def body(q_ref, k_ref, v_ref, b_ref, s_ref, do_ref, o_ref, l_ref, dq_ref, dkv_ref, dsink_ref, kp, vp, dk_acc,
             dv_acc):
        b = pl.program_id(0)
        i = pl.program_id(1)
        half = _lane_half((1, LANES))

        @pl.when(i == 0)
        def _():
            for h in range(B_KV_HEADS):
                _fill_padded_dup(kp.at[h], k_ref[...], pad, h, half)
                _fill_padded_dup(vp.at[h], v_ref[...], pad, h, half)
            dk_acc[...] = jnp.zeros_like(dk_acc)
            dv_acc[...] = jnp.zeros_like(dv_acc)

        @pl.when(jnp.logical_and(b == 0, i == 0))
        def _():
            dsink_ref[...] = jnp.zeros_like(dsink_ref)

        qs = pl.multiple_of(i * TQ, TQ)

        def block(masked):
            heads_dk, heads_dv = [], []
            for h in range(B_KV_HEADS):
                kk = kp[h, pl.ds(qs, band), :]
                vv = vp[h, pl.ds(qs, band), :]
                dk2 = jnp.zeros((band, LANES), F32)
                dv2 = jnp.zeros((band, LANES), F32)
                for pr in range(pp * h, pp * (h + 1)):
                    sl = slice(pr * LANES, (pr + 1) * LANES)
                    q = q_ref[:, sl] * SCALE
                    dd = do_ref[:, sl]
                    od = dd.astype(F32) * o_ref[:, sl].astype(F32)
                    dqs, dks, dvs = [], [], []
                    for j in range(2):
                        qm = jnp.where(half == j, q, jnp.zeros_like(q))
                        dm = jnp.where(half == j, dd, jnp.zeros_like(dd))
                        hd = 2 * pr + j
                        lse = l_ref[:, hd:hd + 1]
                        s = _band_scores(qm, kk, b_ref[hd], qs if masked else None, pad)
                        pn = jnp.exp(s - lse)
                        dp = _dot_nt(dm, vv)
                        delta = jnp.sum(jnp.where(half == j, od, 0.0), axis=-1, keepdims=True)
                        ds = pn * (dp - delta)
                        dsb = ds.astype(BF16)
                        dqs.append(_dot_nn(dsb, kk))
                        dks.append(_dot_tn(dsb, q))
                        dvs.append(_dot_tn(pn.astype(BF16), dd))
                        sink = s_ref[hd][0:1, 0:1]
                        dsk = jnp.sum(-jnp.exp(sink - lse) * delta, axis=0, keepdims=True)
                        dsink_ref[hd] += jnp.broadcast_to(dsk, (8, LANES))
                    dq_ref[:, sl] = (jnp.where(half == 0, dqs[0], dqs[1]) * SCALE).astype(BF16)
                    dk2 = dk2 + jnp.where(half == 0, dks[0], dks[1])
                    dv2 = dv2 + jnp.where(half == 0, dvs[0], dvs[1])
                heads_dk.append(dk2 + pltpu.roll(dk2, D_HEAD, 1))
                heads_dv.append(dv2 + pltpu.roll(dv2, D_HEAD, 1))
            dk_acc[pl.ds(qs, band), :] += jnp.where(half == 0, heads_dk[0], heads_dk[1])
            dv_acc[pl.ds(qs, band), :] += jnp.where(half == 0, heads_dv[0], heads_dv[1])

        pl.when(i < -(-pad // TQ))(lambda: block(True))
        pl.when(i >= -(-pad // TQ))(lambda: block(False))

        @pl.when(i == n_i - 1)
        def _():
            dkv_ref[:, 0:LANES] = dk_acc[pad:, :].astype(BF16)
            dkv_ref[:, LANES:2 * LANES] = dv_acc[pad:, :].astype(BF16)

    qspec = pl.BlockSpec((None, TQ, B_Q_WIDTH), lambda b, i: (b, i, 0))
    return _call(
        body,
        name="attn_b_bwd",
        grid=(bsz, n_i),
        in_specs=[qspec,
                  pl.BlockSpec((None, s_len, LANES), lambda b, i: (b, 0, kcol)),
                  pl.BlockSpec((None, s_len, LANES), lambda b, i: (b, 0, kcol + 1)),
                  pl.BlockSpec((B_Q_HEADS, TQ, band), lambda b, i: (0, 0, 0)),
                  pl.BlockSpec((B_Q_HEADS, 8, LANES), lambda b, i: (0, 0, 0)),
                  qspec, qspec,
                  pl.BlockSpec((None, TQ, B_Q_HEADS), lambda b, i: (b, i, 0))],
        out_specs=[qspec, pl.BlockSpec((None, s_len, 2 * LANES), lambda b, i: (b, 0, 0)),
                   pl.BlockSpec((B_Q_HEADS, 8, LANES), lambda b, i: (0, 0, 0))],
        out_shape=[jax.ShapeDtypeStruct((bsz, s_len, B_Q_WIDTH), BF16),
                   jax.ShapeDtypeStruct((bsz, s_len, 2 * B_KV_WIDTH), BF16),
                   jax.ShapeDtypeStruct((B_Q_HEADS, 8, LANES), F32)],
        scratch_shapes=[pltpu.VMEM((B_KV_HEADS, pad + s_len, LANES), BF16),
                        pltpu.VMEM((B_KV_HEADS, pad + s_len, LANES), BF16),
                        pltpu.VMEM((pad + s_len, LANES), F32), pltpu.VMEM((pad + s_len, LANES), F32)],
        args=(qkv, qkv, qkv, bias, sink, do, o, lse), sem=("arbitrary", "arbitrary"), hosted=hosted)


REL_COLS = 3 * 128
REL_WRAP = 512


def _bias_a_build(tv, hosted=()):
    h = tv.shape[0]
    pad = A_PREV * CHUNK
    band = TQ + pad

    def body(tv_ref, o_ref):
        row = tv_ref[...]
        x = jnp.broadcast_to(row, (TQ, REL_WRAP))
        r = lax.broadcasted_iota(jnp.int32, x.shape, 0)
        for bit in range(8):
            sh = 1 << bit
            x = jnp.where((r & sh) != 0, pltpu.roll(x, sh, 1), x)
        far = jnp.broadcast_to(row[:, 0:1], (TQ, band - REL_COLS))
        full = jnp.concatenate([far, x[:, REL_WRAP // 2:REL_WRAP], x[:, 0:REL_COLS - REL_WRAP // 2]], axis=1)
        qc = (lax.broadcasted_iota(jnp.int32, full.shape, 0) + pad) // CHUNK
        kc = lax.broadcasted_iota(jnp.int32, full.shape, 1) // CHUNK
        ok = jnp.logical_and(kc <= qc, kc >= qc - A_PREV)
        o_ref[...] = jnp.where(ok, full, NEG_INF)

    return _call(
        body,
        name="bias_a_build",
        grid=(h,),
        in_specs=[pl.BlockSpec((None, 1, REL_WRAP), lambda hh: (hh, 0, 0))],
        out_specs=[pl.BlockSpec((None, TQ, band), lambda hh: (hh, 0, 0))],
        out_shape=[jax.ShapeDtypeStruct((h, TQ, band), F32)],
        args=(tv,), sem=("parallel",), hosted=hosted)[0]


def _relbias_grad(dbias, hosted=()):
    h, rows, _ = dbias.shape

    def body(d_ref, o_ref):
        x = d_ref[...]
        r = lax.broadcasted_iota(jnp.int32, x.shape, 0)
        c = lax.broadcasted_iota(jnp.int32, x.shape, 1) - r
        x = jnp.where(jnp.logical_and(c >= 1, c < REL_TABLE), x, 0.0)
        for bit in range(8):
            sh = 1 << bit
            x = jnp.where((r & sh) != 0, pltpu.roll(x, REL_COLS - sh, 1), x)
        diag = jnp.sum(x, axis=0, keepdims=True)
        lane = lax.broadcasted_iota(jnp.int32, diag.shape, 1)
        diag = jnp.where(jnp.logical_and(lane >= 1, lane < REL_TABLE), diag, 0.0)
        rest = -jnp.sum(diag, axis=1, keepdims=True)
        o_ref[...] = jnp.broadcast_to(jnp.where(lane == 0, rest, diag), o_ref.shape)

    return _call(
        body,
        name="relbias_grad",
        grid=(h,),
        in_specs=[pl.BlockSpec((None, rows, REL_COLS), lambda hh: (hh, 0, 0))],
        out_specs=[pl.BlockSpec((None, 8, REL_COLS), lambda hh: (hh, 0, 0))],
        out_shape=[jax.ShapeDtypeStruct((h, 8, REL_COLS), F32)],
        args=(dbias,), sem=("parallel",), hosted=hosted)[0]


def _mix_out_fwd(x, oa, ob, gates, proj_t, wout):
    t = x.shape[0]

    def body(x_ref, oa_ref, ob_ref, gt_ref, pt_ref, wo_ref, y_ref, ya_ref, yb_ref, mg_ref):
        ya = _dot_nt(oa_ref[...], pt_ref[:, 0:A_WIDTH])
        yb = _dot_nt(ob_ref[...], pt_ref[:, A_WIDTH:A_WIDTH + B_Q_WIDTH])
        ya_ref[...] = ya.astype(BF16)
        yb_ref[...] = yb.astype(BF16)
        mg = jax.nn.sigmoid(gt_ref[:, 0:D_MODEL]) * ya + jax.nn.sigmoid(gt_ref[:, D_MODEL:2 * D_MODEL]) * yb
        mgb = mg.astype(BF16)
        mg_ref[...] = mgb
        y_ref[...] = x_ref[...] + _dot_nn(mgb, wo_ref[...])

    return pl.pallas_call(
        body,
        name="mix_out_fwd",
        grid=(t // TM,),
        in_specs=[_rows(TM, D_MODEL), _rows(TM, A_WIDTH), _rows(TM, B_Q_WIDTH), _rows(TM, 2 * D_MODEL),
                  _resident((D_MODEL, A_WIDTH + B_Q_WIDTH)), _resident((D_MODEL, D_MODEL))],
        out_specs=[_rows(TM, D_MODEL), _rows(TM, D_MODEL), _rows(TM, D_MODEL), _rows(TM, D_MODEL)],
        out_shape=[jax.ShapeDtypeStruct((t, D_MODEL), F32), jax.ShapeDtypeStruct((t, D_MODEL), BF16),
                   jax.ShapeDtypeStruct((t, D_MODEL), BF16), jax.ShapeDtypeStruct((t, D_MODEL), BF16)],
        compiler_params=_cparams(("parallel",)),
    )(x, oa, ob, gates, proj_t, wout)


def _mix_out_bwd(d, gates, ya, yb, mg, oa, ob, proj_t, wout, hosted=()):
    t = d.shape[0]
    nt = t // TM

    def body(d_ref, gt_ref, ya_ref, yb_ref, mg_ref, oa_ref, ob_ref, pt_ref, wo_ref,
             doa_ref, dob_ref, dgt_ref, gwo_ref, gwp_ref, acc_o, acc_p):
        i = pl.program_id(0)
        db = d_ref[...].astype(BF16)
        dmg = _dot_nt(db, wo_ref[...])
        sa = jax.nn.sigmoid(gt_ref[:, 0:D_MODEL])
        sb = jax.nn.sigmoid(gt_ref[:, D_MODEL:2 * D_MODEL])
        dya = (dmg * sa).astype(BF16)
        dyb = (dmg * sb).astype(BF16)
        dgt_ref[:, 0:D_MODEL] = (dmg * ya_ref[...].astype(F32) * (sa * (1.0 - sa))).astype(BF16)
        dgt_ref[:, D_MODEL:2 * D_MODEL] = (dmg * yb_ref[...].astype(F32) * (sb * (1.0 - sb))).astype(BF16)
        doa_ref[...] = _dot_nn(dya, pt_ref[:, 0:A_WIDTH]).astype(BF16)
        dob_ref[...] = _dot_nn(dyb, pt_ref[:, A_WIDTH:A_WIDTH + B_Q_WIDTH]).astype(BF16)

        @pl.when(i == 0)
        def _():
            acc_o[...] = jnp.zeros_like(acc_o)
            acc_p[...] = jnp.zeros_like(acc_p)

        acc_o[...] += _dot_tn(mg_ref[...], db)
        acc_p[:, 0:A_WIDTH] += _dot_tn(dya, oa_ref[...])
        acc_p[:, A_WIDTH:A_WIDTH + B_Q_WIDTH] += _dot_tn(dyb, ob_ref[...])

        @pl.when(i == nt - 1)
        def _():
            gwo_ref[...] = acc_o[...].astype(BF16)
            gwp_ref[...] = acc_p[...].astype(BF16)

    whole = pl.BlockSpec((D_MODEL, D_MODEL), lambda i: (0, 0))
    return _call(
        body,
        name="mix_out_bwd",
        grid=(nt,),
        in_specs=[_rows(TM, D_MODEL), _rows(TM, 2 * D_MODEL), _rows(TM, D_MODEL), _rows(TM, D_MODEL),
                  _rows(TM, D_MODEL), _rows(TM, A_WIDTH), _rows(TM, B_Q_WIDTH),
                  _resident((D_MODEL, A_WIDTH + B_Q_WIDTH)), _resident((D_MODEL, D_MODEL))],
        out_specs=[_rows(TM, A_WIDTH), _rows(TM, B_Q_WIDTH), _rows(TM, 2 * D_MODEL), whole, whole],
        out_shape=[jax.ShapeDtypeStruct((t, A_WIDTH), BF16), jax.ShapeDtypeStruct((t, B_Q_WIDTH), BF16),
                   jax.ShapeDtypeStruct((t, 2 * D_MODEL), BF16), jax.ShapeDtypeStruct((D_MODEL, D_MODEL), BF16),
                   jax.ShapeDtypeStruct((D_MODEL, D_MODEL), BF16)],
        scratch_shapes=[pltpu.VMEM((D_MODEL, D_MODEL), F32), pltpu.VMEM((D_MODEL, A_WIDTH + B_Q_WIDTH), F32)],
        args=(d, gates, ya, yb, mg, oa, ob, proj_t, wout), sem=("arbitrary",), hosted=hosted)


def _place():
    x, y, c = lax.axis_index("x"), lax.axis_index("y"), lax.axis_index("c")
    chips = [(1 - x, y), (x, 1 - y), (1 - x, 1 - y)]
    return x, y, c, chips


class _Gather:
    per = 8

    def __init__(self, shards):
        n = len(shards)
        self.inputs = list(shards)
        self.out_shape = [jax.ShapeDtypeStruct((N_DEV * s.shape[0], s.shape[1]), s.dtype) for s in shards]
        self.scratch = [pltpu.SemaphoreType.DMA((n * self.per,)), pltpu.SemaphoreType.DMA((n * self.per,)),
                        pltpu.SemaphoreType.DMA((n,))]
        self.result = None

    def _parts(self, ins, outs, sems):
        send_sems, recv_sems, local_sems = sems
        x, y, c, chips = _place()
        me, sibling = (x, y, c), (x, y, 1 - c)
        xn, yn, dg = chips
        n = len(ins)

        def rows(k, p, part=None):
            r = ins[k].shape[0]
            base = (4 * p[0] + 2 * p[1] + p[2]) * r
            if part is None:
                return outs[k].at[pl.ds(base, r), :]
            return outs[k].at[pl.ds(base + part * (r // 2), r // 2), :]

        def copy(k, slot, block, to, src=None, part=None):
            return pltpu.make_async_remote_copy(
                src_ref=rows(k, block, part) if src is None else src, dst_ref=rows(k, block, part),
                send_sem=send_sems.at[k * self.per + slot], recv_sem=recv_sems.at[k * self.per + slot],
                device_id=to, device_id_type=MESH)

        mine = [pltpu.make_async_copy(ins[k], rows(k, me), local_sems.at[k]) for k in range(n)]
        sends, lands = [], []
        for k in range(n):
            sends.append({
                0: copy(k, 0, me, sibling, src=ins[k]),
                1: copy(k, 1, me, (*xn, c), src=ins[k]),
                2: copy(k, 2, me, (*yn, c), src=ins[k]),
                3: copy(k, 3, (*xn, c), (*yn, c), part=0),
                4: copy(k, 4, (*yn, c), (*xn, c), part=1),
                5: copy(k, 5, (*xn, c), sibling),
                6: copy(k, 6, (*yn, c), sibling),
                7: copy(k, 7, (*dg, c), sibling)})
            lands.append({
                0: copy(k, 0, sibling, me),
                1: copy(k, 1, (*xn, c), me),
                2: copy(k, 2, (*yn, c), me),
                3: copy(k, 3, (*dg, c), me, part=0),
                4: copy(k, 4, (*dg, c), me, part=1),
                5: copy(k, 5, (*xn, 1 - c), me),
                6: copy(k, 6, (*yn, 1 - c), me),
                7: copy(k, 7, (*dg, 1 - c), me)})
        return n, mine, sends, lands

    def start(self, ins, outs, sems):
        n, mine, sends, _ = self._parts(ins, outs, sems)
        for cp in mine:
            cp.start()
        for slot in (0, 1, 2):
            for k in range(n):
                sends[k][slot].start()

    def relay(self, ins, outs, sems):
        n, _, sends, lands = self._parts(ins, outs, sems)
        for k in range(n):
            lands[k][1].wait_recv()
            sends[k][3].start()
            sends[k][5].start()
        for k in range(n):
            lands[k][2].wait_recv()
            sends[k][4].start()
            sends[k][6].start()

    def forward(self, ins, outs, sems):
        n, _, sends, lands = self._parts(ins, outs, sems)
        for k in range(n):
            lands[k][3].wait_recv()
            lands[k][4].wait_recv()
            sends[k][7].start()

    def finish(self, ins, outs, sems):
        n, mine, sends, lands = self._parts(ins, outs, sems)
        for k in range(n):
            for slot in (0, 5, 6, 7):
                lands[k][slot].wait_recv()
        for k in range(n):
            for slot in range(self.per):
                sends[k][slot].wait_send()
        for cp in mine:
            cp.wait()


class _PairExchange:
    def __init__(self, grads):
        n = len(grads)
        self.inputs = list(grads)
        self.out_shape = [jax.ShapeDtypeStruct((g.shape[0] // 2, g.shape[1]), g.dtype) for g in grads]
        self.scratch = [pltpu.SemaphoreType.DMA((n * N_CHIP,)), pltpu.SemaphoreType.DMA((n * N_CHIP,))]
        self.result = None

    def _copies(self, ins, outs, sems):
        send_sems, recv_sems = sems
        x, y, c, _ = _place()
        copies = []
        for k in range(len(ins)):
            r = ins[k].shape[0] // N_DEV
            for q in range(N_CHIP):
                copies.append(pltpu.make_async_remote_copy(
                    src_ref=ins[k].at[pl.ds((2 * q + 1 - c) * r, r), :], dst_ref=outs[k].at[pl.ds(q * r, r), :],
                    send_sem=send_sems.at[k * N_CHIP + q], recv_sem=recv_sems.at[k * N_CHIP + q],
                    device_id=(x, y, 1 - c), device_id_type=MESH))
        return copies

    def start(self, ins, outs, sems):
        for cp in self._copies(ins, outs, sems):
            cp.start()

    def relay(self, ins, outs, sems):
        pass

    def forward(self, ins, outs, sems):
        pass

    def finish(self, ins, outs, sems):
        copies = self._copies(ins, outs, sems)
        for cp in copies:
            cp.wait_recv()
        for cp in copies:
            cp.wait_send()


class _ChipExchange(_PairExchange):
    def __init__(self, psums):
        n = len(psums)
        self.inputs = list(psums)
        self.out_shape = [jax.ShapeDtypeStruct((3 * p.shape[0] // N_CHIP, p.shape[1]), p.dtype) for p in psums]
        self.scratch = [pltpu.SemaphoreType.DMA((n * 3,)), pltpu.SemaphoreType.DMA((n * 3,))]
        self.result = None

    def _copies(self, ins, outs, sems):
        send_sems, recv_sems = sems
        _, _, c, chips = _place()
        copies = []
        for k in range(len(ins)):
            r = ins[k].shape[0] // N_CHIP
            for j, chip in enumerate(chips):
                copies.append(pltpu.make_async_remote_copy(
                    src_ref=ins[k].at[pl.ds((2 * chip[0] + chip[1]) * r, r), :], dst_ref=outs[k].at[pl.ds(j * r, r), :],
                    send_sem=send_sems.at[k * 3 + j], recv_sem=recv_sems.at[k * 3 + j],
                    device_id=(*chip, c), device_id_type=MESH))
        return copies


def _exchange_alone(xchg, name):
    n_in, n_out = len(xchg.inputs), len(xchg.out_shape)

    def body(*refs):
        ins, outs, sems = refs[:n_in], refs[n_in:n_in + n_out], refs[n_in + n_out:]
        xchg.start(ins, outs, sems)
        xchg.relay(ins, outs, sems)
        xchg.forward(ins, outs, sems)
        xchg.finish(ins, outs, sems)

    xchg.result = list(pl.pallas_call(
        body, name=name, in_specs=[_hbm()] * n_in, out_specs=[_hbm()] * n_out, out_shape=xchg.out_shape,
        scratch_shapes=xchg.scratch)(*xchg.inputs))
    return xchg.result


def _pair_sum(core, grads, recvd, name):
    n = len(grads)
    r = grads[0].shape[0] // N_DEV
    cdim = grads[0].shape[1]
    tr = r // 2 if r % 32 == 0 else r
    nt = r // tr

    def body(core_ref, *refs):
        del core_ref
        for k in range(n):
            refs[2 * n + k][...] = (refs[k][...].astype(F32) + refs[n + k][...].astype(F32)).astype(BF16)

    gspec = pl.BlockSpec((tr, cdim), lambda q, i, core_ref: ((2 * q + core_ref[0]) * nt + i, 0))
    rspec = pl.BlockSpec((tr, cdim), lambda q, i, core_ref: (q * nt + i, 0))
    return pl.pallas_call(
        body,
        name=name,
        grid_spec=pltpu.PrefetchScalarGridSpec(
            num_scalar_prefetch=1, grid=(N_CHIP, nt), in_specs=[gspec] * n + [rspec] * n, out_specs=[rspec] * n),
        out_shape=[jax.ShapeDtypeStruct((N_CHIP * r, cdim), BF16) for _ in range(n)],
        compiler_params=_cparams(("parallel", "parallel")),
    )(core, *grads, *recvd)


def _final_sum(chip, psums, recvd, name):
    n = len(psums)
    r = psums[0].shape[0] // N_CHIP
    cdim = psums[0].shape[1]
    tr = r // 2 if r % 32 == 0 else r
    nt = r // tr

    def body(chip_ref, *refs):
        del chip_ref
        for k in range(n):
            got = refs[n + k]
            tot = refs[k][...].astype(F32) + got[0].astype(F32)
            tot = tot + got[1].astype(F32)
            tot = tot + got[2].astype(F32)
            refs[2 * n + k][...] = tot

    pspec = pl.BlockSpec((tr, cdim), lambda i, chip_ref: (chip_ref[0] * nt + i, 0))
    rspec = pl.BlockSpec((3, tr, cdim), lambda i, chip_ref: (0, i, 0))
    ospec = pl.BlockSpec((tr, cdim), lambda i, chip_ref: (i, 0))
    return pl.pallas_call(
        body,
        name=name,
        grid_spec=pltpu.PrefetchScalarGridSpec(
            num_scalar_prefetch=1, grid=(nt,), in_specs=[pspec] * n + [rspec] * n, out_specs=[ospec] * n),
        out_shape=[jax.ShapeDtypeStruct((r, cdim), F32) for _ in range(n)],
        compiler_params=_cparams(("parallel",)),
    )(chip, *psums, *[g.reshape(3, r, cdim) for g in recvd])


SMALL_ROWS = 16


def _all_reduce_small(part):
    def body(p_ref, o_ref, buf, send_sems, recv_sems):
        x, y, c, _ = _place()
        me = 4 * x + 2 * y + c
        buf[me] = p_ref[...]
        copies = []
        for d in range(1, N_DEV):
            peer = me ^ d
            copies.append(pltpu.make_async_remote_copy(
                src_ref=p_ref, dst_ref=buf.at[me], send_sem=send_sems.at[d - 1], recv_sem=recv_sems.at[d - 1],
                device_id=(peer // 4, (peer // 2) % 2, peer % 2), device_id_type=MESH))
        for cp in copies:
            cp.start()
        for cp in copies:
            cp.wait_recv()
        for cp in copies:
            cp.wait_send()
        tot = buf[0]
        for d in range(1, N_DEV):
            tot = tot + buf[d]
        o_ref[...] = tot

    return pl.pallas_call(
        body,
        name="all_reduce_small",
        in_specs=[pl.BlockSpec(memory_space=pltpu.VMEM)],
        out_specs=pl.BlockSpec(memory_space=pltpu.VMEM),
        out_shape=jax.ShapeDtypeStruct(part.shape, F32),
        scratch_shapes=[pltpu.VMEM((N_DEV,) + part.shape, F32), pltpu.SemaphoreType.DMA((N_DEV - 1,)),
                        pltpu.SemaphoreType.DMA((N_DEV - 1,))],
    )(part)


ADAMW_STEPS = 4


def _adamw(ws, gs, ms, vs, name, hosted=()):
    n = len(ws)
    steps = ADAMW_STEPS if all(w.shape[0] % (8 * ADAMW_STEPS) == 0 for w in ws) else 1
    c1 = 1.0 - ADAM_B1 ** ADAM_STEP
    c2 = 1.0 - ADAM_B2 ** ADAM_STEP

    def body(*refs):
        for k in range(n):
            w, g, m, v = (refs[j * n + k][...] for j in range(4))
            m2 = ADAM_B1 * m + (1.0 - ADAM_B1) * g
            v2 = ADAM_B2 * v + (1.0 - ADAM_B2) * (g * g)
            delta = -ADAM_LR * ((m2 * (1.0 / c1)) / (jnp.sqrt(v2 * (1.0 / c2)) + ADAM_EPS) + ADAM_WD * w)
            refs[4 * n + k][...] = delta
            refs[5 * n + k][...] = m2
            refs[6 * n + k][...] = v2

    specs = [pl.BlockSpec((w.shape[0] // steps, w.shape[1]), lambda i: (i, 0)) for w in ws]
    shapes = [jax.ShapeDtypeStruct(w.shape, F32) for w in ws]
    outs = _call(
        body,
        name=name,
        grid=(steps,),
        in_specs=specs * 4,
        out_specs=specs * 3,
        out_shape=shapes * 3,
        args=(*ws, *gs, *ms, *vs), sem=("parallel",), hosted=hosted)
    return outs[:n], outs[n:2 * n], outs[2 * n:]


def _adamw_reduced(chip, ws, psums, recvd, ms, vs, steps, name):
    n = len(ws)
    c1 = 1.0 - ADAM_B1 ** ADAM_STEP
    c2 = 1.0 - ADAM_B2 ** ADAM_STEP

    def body(chip_ref, *refs):
        del chip_ref
        for k in range(n):
            w, m, v = (refs[j * n + k][...] for j in (0, 3, 4))
            got = refs[2 * n + k]
            g = refs[n + k][...].astype(F32) + got[0].astype(F32)
            g = g + got[1].astype(F32)
            g = g + got[2].astype(F32)
            m2 = ADAM_B1 * m + (1.0 - ADAM_B1) * g
            v2 = ADAM_B2 * v + (1.0 - ADAM_B2) * (g * g)
            refs[5 * n + k][...] = g
            refs[6 * n + k][...] = -ADAM_LR * (
                (m2 * (1.0 / c1)) / (jnp.sqrt(v2 * (1.0 / c2)) + ADAM_EPS) + ADAM_WD * w)
            refs[7 * n + k][...] = m2
            refs[8 * n + k][...] = v2

    def blk(w):
        return (w.shape[0] // steps, w.shape[1])

    own = [pl.BlockSpec(blk(w), lambda i, chip_ref: (i, 0)) for w in ws]
    psum = [pl.BlockSpec(blk(w), lambda i, chip_ref: (chip_ref[0] * steps + i, 0)) for w in ws]
    recv = [pl.BlockSpec((3,) + blk(w), lambda i, chip_ref: (0, i, 0)) for w in ws]
    shapes = [jax.ShapeDtypeStruct(w.shape, F32) for w in ws]
    outs = pl.pallas_call(
        body,
        name=name,
        grid_spec=pltpu.PrefetchScalarGridSpec(
            num_scalar_prefetch=1, grid=(steps,), in_specs=own + psum + recv + own + own, out_specs=own * 4),
        out_shape=shapes * 4,
        compiler_params=_cparams(("parallel",)),
    )(chip, *ws, *psums, *[r.reshape((3,) + w.shape) for r, w in zip(recvd, ws)], *ms, *vs)
    return outs[:n], outs[n:2 * n], outs[2 * n:3 * n], outs[3 * n:]


def _bias_b():
    pad = B_PREV * CHUNK
    slopes = np.array([2.0 ** (-8.0 * (i + 1) / B_Q_HEADS) for i in range(B_Q_HEADS)], dtype=np.float32)
    dist = np.abs(np.arange(TQ)[:, None] - np.arange(TQ + pad)[None, :] + pad).astype(np.float32)
    bias = -slopes.reshape(B_Q_HEADS, 1, 1) * dist[None]
    qc = (np.arange(TQ)[:, None] + pad) // CHUNK
    kc = np.arange(TQ + pad)[None, :] // CHUNK
    allowed = (kc <= qc) & (kc >= qc - B_PREV)
    return np.where(allowed[None], bias, np.float32(NEG_INF)).astype(np.float32)


def kernel(x, ffn1_norm, ffn1_w_gate, ffn1_w_up, ffn1_w_down, mix_norm, w_in, rel_bias, sinks, w_proj_a, w_proj_b, w_out, ffn2_norm, ffn2_w_gate, ffn2_w_up, ffn2_w_down, final_norm, loss_target, m_ffn1_norm, m_ffn1_w_gate, m_ffn1_w_up, m_ffn1_w_down, m_mix_norm, m_w_in, m_rel_bias, m_sinks, m_w_proj_a, m_w_proj_b, m_w_out, m_ffn2_norm, m_ffn2_w_gate, m_ffn2_w_up, m_ffn2_w_down, m_final_norm, v_ffn1_norm, v_ffn1_w_gate, v_ffn1_w_up, v_ffn1_w_down, v_mix_norm, v_w_in, v_rel_bias, v_sinks, v_w_proj_a, v_w_proj_b, v_w_out, v_ffn2_norm, v_ffn2_w_gate, v_ffn2_w_up, v_ffn2_w_down, v_final_norm):
    bsz, s_len, _ = x.shape
    t = bsz * s_len
    core = lax.axis_index("c").astype(jnp.int32).reshape(1)
    chip = (2 * lax.axis_index("x") + lax.axis_index("y")).astype(jnp.int32).reshape(1)

    proj_rows = jnp.concatenate([w_proj_a.T, w_proj_b.T], axis=1)
    sh_g1, sh_u1, sh_d1, sh_in, sh_proj, sh_out, sh_g2, sh_u2, sh_d2 = _to_bf16(
        [ffn1_w_gate.T, ffn1_w_up.T, ffn1_w_down, w_in.T, proj_rows, w_out, ffn2_w_gate.T, ffn2_w_up.T, ffn2_w_down],
        "weights_to_bf16")

    gather_up1 = _Gather([sh_g1, sh_u1])
    far = jnp.broadcast_to(rel_bias[:, REL_TABLE - 1:REL_TABLE], (A_HEADS, REL_WRAP // 2))
    tv = jnp.concatenate([far, jnp.flip(rel_bias, axis=1), jnp.zeros((A_HEADS, REL_WRAP // 2 - REL_TABLE), F32)], axis=1)
    bias_a = _bias_a_build(tv.reshape(A_HEADS, 1, REL_WRAP), hosted=[gather_up1])
    wg1, wu1 = gather_up1.result
    gather_down1 = _Gather([sh_d1, sh_in])
    gather_out = _Gather([sh_proj, sh_out])
    gather_ffn2_gate = _Gather([sh_g2])
    gather_ffn2_rest = _Gather([sh_u2, sh_d2])

    x0 = x.reshape(t, D_MODEL)
    tgt = loss_target.reshape(t, D_MODEL)
    gam1, gam2, gam3, gam4 = (g.reshape(1, D_MODEL) for g in (ffn1_norm, mix_norm, ffn2_norm, final_norm))

    h1, g1, u1, a1 = _ffn_up(x0, gam1, wg1, wu1, "ffn1_up", hosted=[gather_down1])
    wd1, win_t = gather_down1.result
    x1 = _ffn_down(x0, a1, wd1, "ffn1_down", hosted=[gather_out])
    proj_t, wout = gather_out.result
    h2, qkv_a, qkv_b, gates = _proj_fwd(x1, gam2, win_t, hosted=[gather_ffn2_gate])
    (wg2,) = gather_ffn2_gate.result
    qkv_a3 = qkv_a.reshape(bsz, s_len, QKV_A)
    qkv_b3 = qkv_b.reshape(bsz, s_len, QKV_B)

    bias_b = jnp.asarray(_bias_b())
    sink_rows = jnp.broadcast_to(sinks.reshape(B_Q_HEADS, 1, 1), (B_Q_HEADS, 8, LANES))

    oa, lse_a = _attn_a_fwd(qkv_a3, bias_a, hosted=[gather_ffn2_rest])
    oa = oa.reshape(t, A_WIDTH)
    wu2, wd2 = gather_ffn2_rest.result
    ob, lse_b = _attn_b_fwd(qkv_b3, bias_b, sink_rows)
    ob = ob.reshape(t, B_Q_WIDTH)
    x2, ya, yb, mg = _mix_out_fwd(x1, oa, ob, gates, proj_t, wout)
    h3, g2, u2, a2, x3 = _ffn_fwd(x2, gam3, wg2, wu2, wd2, "ffn2_fwd")

    dx2, dg2, du2, db2, dgam3, dgam4, loss_part = _ffn_bwd_head(x3, gam4, tgt, x2, gam3, g2, u2, wg2, wu2, wd2,
                                                                "ffn2_bwd")
    gw_ffn2 = [_mm_tn([dg2], h3, "grad_ffn2_gate"), _mm_tn([du2], h3, "grad_ffn2_up"),
               _mm_tn([a2], db2, "grad_ffn2_down")]
    pairx_ffn2 = _PairExchange(gw_ffn2)
    doa, dob, dgates, gw_out, gw_proj = _mix_out_bwd(dx2, gates, ya, yb, mg, oa, ob, proj_t, wout,
                                                     hosted=[pairx_ffn2])
    psum_ffn2 = _pair_sum(core, gw_ffn2, pairx_ffn2.result, "pair_sum_ffn2")

    chipx_ffn2 = _ChipExchange(psum_ffn2[0:2])
    dqa, dka, dva, dbias_a = _attn_a_bwd(qkv_a3, bias_a, doa.reshape(bsz, s_len, A_WIDTH),
                                         oa.reshape(bsz, s_len, A_WIDTH), lse_a, hosted=[chipx_ffn2])
    pairx_out = _PairExchange([gw_proj, gw_out])
    chipx_ffn2_down = _ChipExchange(psum_ffn2[2:3])
    dqb, dkvb, dsink = _attn_b_bwd(qkv_b3, bias_b, sink_rows, dob.reshape(bsz, s_len, B_Q_WIDTH),
                                   ob.reshape(bsz, s_len, B_Q_WIDTH), lse_b, hosted=[pairx_out, chipx_ffn2_down])
    drel_lanes = _relbias_grad(dbias_a)
    dproj = [dqa.reshape(t, A_WIDTH), dka.reshape(t, A_WIDTH), dva.reshape(t, A_WIDTH), dqb.reshape(t, B_Q_WIDTH),
             dkvb.reshape(t, 2 * B_KV_WIDTH), dgates]

    gw_in = _mm_tn(dproj, h2, "grad_w_in")
    pairx_in = _PairExchange([gw_in])
    psum_out = _pair_sum(core, [gw_proj, gw_out], pairx_out.result, "pair_sum_mix")
    chipx_out = _ChipExchange(psum_out)
    dx1, db1, dgam2 = _proj_bwd(dx2, x1, gam2, dproj, win_t, hosted=[pairx_in, chipx_out])
    psum_in = _pair_sum(core, [gw_in], pairx_in.result, "pair_sum_w_in")
    gw_d1 = _mm_tn([a1], db1, "grad_ffn1_down")

    chipx_in = _ChipExchange(psum_in)
    pairx_d1 = _PairExchange([gw_d1])
    dg1, du1 = _ffn_bwd_act(dx1, g1, u1, wd1, "ffn1_bwd_act", hosted=[chipx_in, pairx_d1])
    psum_d1 = _pair_sum(core, [gw_d1], pairx_d1.result, "pair_sum_ffn1_down")
    chipx_d1 = _ChipExchange(psum_d1)
    gw_g1 = _mm_tn([dg1], h1, "grad_ffn1_gate", hosted=[chipx_d1])
    from_sibling_g1 = _exchange_alone(_PairExchange([gw_g1]), "pair_exchange_ffn1_gate")
    psum_g1 = _pair_sum(core, [gw_g1], from_sibling_g1, "pair_sum_ffn1_gate")
    chipx_g1 = _ChipExchange(psum_g1)
    gw_u1 = _mm_tn([du1], h1, "grad_ffn1_up", hosted=[chipx_g1])
    from_sibling_u1 = _exchange_alone(_PairExchange([gw_u1]), "pair_exchange_ffn1_up")
    psum_u1 = _pair_sum(core, [gw_u1], from_sibling_u1, "pair_sum_ffn1_up")
    chipx_u1 = _ChipExchange(psum_u1)
    dx0, dgam1 = _ffn_bwd_in(dx1, x0, gam1, dg1, du1, wg1, wu1, "ffn1_bwd_in", hosted=[chipx_u1])

    (g_proj,) = _final_sum(chip, psum_out[0:1], chipx_out.result[0:1], "grad_sum_proj")
    grads = {"w_proj_a": g_proj[:, 0:A_WIDTH].T, "w_proj_b": g_proj[:, A_WIDTH:].T}

    def row_of(v):
        return jnp.pad(v.reshape(1, -1), ((0, 0), (0, D_MODEL - v.size)))

    def table_rows(v):
        return jnp.pad(v, ((0, 0), (0, D_MODEL - REL_TABLE)))

    drel_local = jnp.flip(drel_lanes[:, 0, 0:REL_TABLE], axis=1)
    small_part = jnp.concatenate(
        [jnp.sum(dgam1, axis=0, keepdims=True), jnp.sum(dgam2, axis=0, keepdims=True),
         jnp.sum(dgam3, axis=0, keepdims=True), jnp.sum(dgam4, axis=0, keepdims=True),
         row_of(jnp.sum(loss_part)), row_of(dsink[:, 0, 0]), jnp.zeros((2, D_MODEL), F32),
         table_rows(drel_local)], axis=0)
    small = _all_reduce_small(small_part)
    loss = small[4, 0]

    def pack(n1, n2, n3, n4, sk, tb):
        return jnp.concatenate([n1.reshape(1, -1), n2.reshape(1, -1), n3.reshape(1, -1), n4.reshape(1, -1),
                                jnp.zeros((1, D_MODEL), F32), row_of(sk), jnp.zeros((2, D_MODEL), F32), table_rows(tb)],
                               axis=0)

    live = np.zeros((SMALL_ROWS, D_MODEL), np.float32)
    live[0:4] = 1.0
    live[5, 0:B_Q_HEADS] = 1.0
    live[8:16, 0:REL_TABLE] = 1.0
    small_g = small * jnp.asarray(live)
    sw = pack(ffn1_norm, mix_norm, ffn2_norm, final_norm, sinks, rel_bias)
    sm = pack(m_ffn1_norm, m_mix_norm, m_ffn2_norm, m_final_norm, m_sinks, m_rel_bias)
    sv = pack(v_ffn1_norm, v_mix_norm, v_ffn2_norm, v_final_norm, v_sinks, v_rel_bias)
    (sd,), (snm,), (snv,) = _adamw([sw], [small_g], [sm], [sv], "adamw_small")

    def unpack(p):
        return {"ffn1_norm": p[0], "mix_norm": p[1], "ffn2_norm": p[2], "final_norm": p[3],
                "sinks": p[5, 0:B_Q_HEADS], "rel_bias": p[8:16, 0:REL_TABLE]}

    grads.update(unpack(small_g))
    delta, new_m, new_v = unpack(sd), unpack(snm), unpack(snv)

    wmv = {
        "ffn1_w_gate": (ffn1_w_gate, m_ffn1_w_gate, v_ffn1_w_gate), "ffn1_w_up": (ffn1_w_up, m_ffn1_w_up, v_ffn1_w_up),
        "ffn1_w_down": (ffn1_w_down, m_ffn1_w_down, v_ffn1_w_down), "w_in": (w_in, m_w_in, v_w_in),
        "w_proj_a": (w_proj_a, m_w_proj_a, v_w_proj_a), "w_proj_b": (w_proj_b, m_w_proj_b, v_w_proj_b),
        "w_out": (w_out, m_w_out, v_w_out),
        "ffn2_w_gate": (ffn2_w_gate, m_ffn2_w_gate, v_ffn2_w_gate), "ffn2_w_up": (ffn2_w_up, m_ffn2_w_up, v_ffn2_w_up),
        "ffn2_w_down": (ffn2_w_down, m_ffn2_w_down, v_ffn2_w_down),
    }
    row_form_names = ("ffn1_w_gate", "ffn1_w_up", "w_in", "ffn2_w_gate", "ffn2_w_up")

    def form(n, a):
        return a.T if n in row_form_names else a

    def reduced_group(gname, names, psums, recvd, steps):
        gs_, ds_, ms_, vs_ = _adamw_reduced(
            chip, [form(n, wmv[n][0]) for n in names], psums, recvd, [form(n, wmv[n][1]) for n in names],
            [form(n, wmv[n][2]) for n in names], steps, gname)
        for n, g_, d_, m_, v_ in zip(names, gs_, ds_, ms_, vs_):
            grads[n], delta[n], new_m[n], new_v[n] = form(n, g_), form(n, d_), form(n, m_), form(n, v_)

    reduced_group("adamw_ffn", ["ffn1_w_gate", "ffn1_w_up", "ffn1_w_down", "ffn2_w_gate", "ffn2_w_up", "ffn2_w_down"],
                  psum_g1 + psum_u1 + psum_d1 + psum_ffn2,
                  chipx_g1.result + chipx_u1.result + chipx_d1.result + chipx_ffn2.result + chipx_ffn2_down.result, 11)
    reduced_group("adamw_in_out", ["w_in", "w_out"], psum_in + psum_out[1:2], chipx_in.result + chipx_out.result[1:2], 2)
    names = ["w_proj_a", "w_proj_b"]
    ds_, ms_, vs_ = _adamw([wmv[n][0] for n in names], [grads[n] for n in names], [wmv[n][1] for n in names],
                           [wmv[n][2] for n in names], "adamw_proj")
    for n, d_, m_, v_ in zip(names, ds_, ms_, vs_):
        delta[n], new_m[n], new_v[n] = d_, m_, v_

    order = ["ffn1_norm", "ffn1_w_gate", "ffn1_w_up", "ffn1_w_down", "mix_norm", "w_in", "rel_bias", "sinks",
             "w_proj_a", "w_proj_b", "w_out", "ffn2_norm", "ffn2_w_gate", "ffn2_w_up", "ffn2_w_down", "final_norm"]
    grad_x = dx0.reshape(bsz, s_len, D_MODEL)
    return (loss, grad_x, *[grads[n] for n in order], *[delta[n] for n in order], *[new_m[n] for n in order],
            *[new_v[n] for n in order])
```

```python
import numpy as np
import jax
import jax.numpy as jnp
from jax import lax
from jax.experimental import pallas as pl
from jax.experimental.pallas import tpu as pltpu

F32 = jnp.float32
BF16 = jnp.bfloat16

D_MODEL = 1024
D_FF = 2816
CHUNK = 64
D_HEAD = 64
A_HEADS = 8
A_PREV = 8
MAX_REL = 128
B_Q_HEADS = 8
B_KV_HEADS = 2
B_GROUP = B_Q_HEADS // B_KV_HEADS
B_PREV = 2
REL_TABLE = (CHUNK - 1) + MAX_REL + 1
A_WIDTH = A_HEADS * D_HEAD
B_Q_WIDTH = B_Q_HEADS * D_HEAD
B_KV_WIDTH = B_KV_HEADS * D_HEAD
QKV_A = 3 * A_WIDTH
QKV_B = B_Q_WIDTH + 2 * B_KV_WIDTH
IN_WIDTH = QKV_A + QKV_B + 2 * D_MODEL
EPS = 1e-6
NEG_INF = -1e30
SCALE = 1.0 / 8.0

ADAM_LR = 0.001
ADAM_B1 = 0.9
ADAM_B2 = 0.999
ADAM_EPS = 1e-08
ADAM_WD = 0.01
ADAM_STEP = 10

N_DEV = 8
N_CHIP = 4
MESH = pl.DeviceIdType.MESH

LANES = 128
TQ = 256
TM = 256
FC = 256
VMEM_LIMIT = 56 << 20


def _cparams(sem, vmem=VMEM_LIMIT):
    return pltpu.CompilerParams(dimension_semantics=sem, vmem_limit_bytes=vmem)


def _dot_nt(a, b):
    return lax.dot_general(a, b, (((1,), (1,)), ((), ())), preferred_element_type=F32)


def _dot_nn(a, b):
    return lax.dot_general(a, b, (((1,), (0,)), ((), ())), preferred_element_type=F32)


def _dot_tn(a, b):
    return lax.dot_general(a, b, (((0,), (0,)), ((), ())), preferred_element_type=F32)


def _resident(shape):
    nd = len(shape)
    return pl.BlockSpec(shape, lambda *_: (0,) * nd, pipeline_mode=pl.Buffered(1))


def _rows(tm, width):
    return pl.BlockSpec((tm, width), lambda i: (i, 0))


def _colsum8(v):
    tm, n = v.shape
    return jnp.sum(v.reshape(tm // 8, 8, n), axis=0)


def _rms(x):
    r = lax.rsqrt(jnp.mean(x * x, axis=-1, keepdims=True) + EPS)
    return x * r, r


def _rms_bwd(dh, xh, r, gamma):
    dxh = dh * gamma
    dx = r * (dxh - xh * jnp.mean(dxh * xh, axis=-1, keepdims=True))
    return dx, _colsum8(dh * xh)


def _hbm():
    return pl.BlockSpec(memory_space=pltpu.HBM)


def _call(body, *, name, grid, in_specs, out_specs, out_shape, args, sem, scratch_shapes=(), hosted=()):
    in_specs, out_specs, out_shape = list(in_specs), list(out_specs), list(out_shape)
    scratch_shapes = list(scratch_shapes)
    if not hosted:
        return pl.pallas_call(body, name=name, grid=grid, in_specs=in_specs, out_specs=out_specs, out_shape=out_shape,
                              scratch_shapes=scratch_shapes, compiler_params=_cparams(sem))(*args)
    n_in, n_out, n_scr = len(in_specs), len(out_specs), len(scratch_shapes)
    x_in = [a for x in hosted for a in x.inputs]
    x_out = [s for x in hosted for s in x.out_shape]
    x_scr = [s for x in hosted for s in x.scratch]
    steps = int(np.prod(grid))
    forward_step = max(steps - 3, 0)
    relay_step = min((5 * steps) // 8, forward_step)

    def wrapped(*refs):
        pos = [0]

        def take(k):
            pos[0] += k
            return refs[pos[0] - k:pos[0]]

        ins, xin, outs, xout, scr, xscr = (take(k) for k in (n_in, len(x_in), n_out, len(x_out), n_scr, len(x_scr)))
        step = 0
        for axis, extent in enumerate(grid):
            step = step * extent + pl.program_id(axis)
        own, oi, oo, osc = [], 0, 0, 0
        for x in hosted:
            own.append((xin[oi:oi + len(x.inputs)], xout[oo:oo + len(x.out_shape)], xscr[osc:osc + len(x.scratch)]))
            oi, oo, osc = oi + len(x.inputs), oo + len(x.out_shape), osc + len(x.scratch)

        def phase(method):
            for x, (i_, o_, s_) in zip(hosted, own):
                getattr(x, method)(i_, o_, s_)

        pl.when(step == 0)(lambda: phase("start"))
        body(*ins, *outs, *scr)
        pl.when(step == relay_step)(lambda: phase("relay"))
        pl.when(step == forward_step)(lambda: phase("forward"))
        pl.when(step == steps - 1)(lambda: phase("finish"))

    res = pl.pallas_call(
        wrapped, name=name, grid=grid, in_specs=in_specs + [_hbm()] * len(x_in),
        out_specs=out_specs + [_hbm()] * len(x_out), out_shape=out_shape + x_out,
        scratch_shapes=scratch_shapes + x_scr, compiler_params=_cparams(("arbitrary",) * len(grid)))(*args, *x_in)
    rest = list(res[n_out:])
    for x in hosted:
        x.result, rest = rest[:len(x.out_shape)], rest[len(x.out_shape):]
    return list(res[:n_out])


def _to_bf16(arrays, name):
    n = len(arrays)

    def body(*refs):
        for k in range(n):
            refs[n + k][...] = refs[k][...].astype(BF16)

    specs = [pl.BlockSpec(a.shape, lambda i: (0, 0)) for a in arrays]
    return pl.pallas_call(
        body, name=name, grid=(1,), in_specs=specs, out_specs=specs,
        out_shape=[jax.ShapeDtypeStruct(a.shape, BF16) for a in arrays],
        compiler_params=_cparams(("arbitrary",)))(*arrays)


def _ffn_fwd(x, gamma, wg_t, wu_t, wd, name, hosted=()):
    t = x.shape[0]
    f = wg_t.shape[0]

    def body(x_ref, gam_ref, wg_ref, wu_ref, wd_ref, h_ref, g_ref, u_ref, a_ref, y_ref):
        xv = x_ref[...]
        xh, _ = _rms(xv)
        h = (xh * gam_ref[...]).astype(BF16)
        h_ref[...] = h
        for j in range(f // FC):
            sl = slice(j * FC, (j + 1) * FC)
            g = _dot_nt(h, wg_ref[sl, :])
            u = _dot_nt(h, wu_ref[sl, :])
            g_ref[:, sl] = g.astype(BF16)
            u_ref[:, sl] = u.astype(BF16)
            a_ref[:, sl] = (g * jax.nn.sigmoid(g) * u).astype(BF16)
        y_ref[...] = xv + 0.5 * _dot_nn(a_ref[...], wd_ref[...])

    return _call(
        body,
        name=name,
        grid=(t // TM,),
        in_specs=[_rows(TM, D_MODEL), _resident((1, D_MODEL)), _resident((f, D_MODEL)), _resident((f, D_MODEL)),
                  _resident((f, D_MODEL))],
        out_specs=[_rows(TM, D_MODEL), _rows(TM, f), _rows(TM, f), _rows(TM, f), _rows(TM, D_MODEL)],
        out_shape=[jax.ShapeDtypeStruct((t, D_MODEL), BF16), jax.ShapeDtypeStruct((t, f), BF16),
                   jax.ShapeDtypeStruct((t, f), BF16), jax.ShapeDtypeStruct((t, f), BF16),
                   jax.ShapeDtypeStruct((t, D_MODEL), F32)],
        args=(x, gamma, wg_t, wu_t, wd), sem=("parallel",), hosted=hosted)


def _ffn_up(x, gamma, wg_t, wu_t, name, hosted=()):
    t = x.shape[0]
    f = wg_t.shape[0]

    def body(x_ref, gam_ref, wg_ref, wu_ref, h_ref, g_ref, u_ref, a_ref):
        xh, _ = _rms(x_ref[...])
        h = (xh * gam_ref[...]).astype(BF16)
        h_ref[...] = h
        for j in range(f // FC):
            sl = slice(j * FC, (j + 1) * FC)
            g = _dot_nt(h, wg_ref[sl, :])
            u = _dot_nt(h, wu_ref[sl, :])
            g_ref[:, sl] = g.astype(BF16)
            u_ref[:, sl] = u.astype(BF16)
            a_ref[:, sl] = (g * jax.nn.sigmoid(g) * u).astype(BF16)

    return _call(
        body,
        name=name,
        grid=(t // TM,),
        in_specs=[_rows(TM, D_MODEL), _resident((1, D_MODEL)), _resident((f, D_MODEL)), _resident((f, D_MODEL))],
        out_specs=[_rows(TM, D_MODEL), _rows(TM, f), _rows(TM, f), _rows(TM, f)],
        out_shape=[jax.ShapeDtypeStruct((t, D_MODEL), BF16), jax.ShapeDtypeStruct((t, f), BF16),
                   jax.ShapeDtypeStruct((t, f), BF16), jax.ShapeDtypeStruct((t, f), BF16)],
        args=(x, gamma, wg_t, wu_t), sem=("parallel",), hosted=hosted)


def _ffn_down(x, a_act, wd, name, hosted=()):
    t = x.shape[0]
    f = wd.shape[0]

    def body(x_ref, a_ref, wd_ref, y_ref):
        y_ref[...] = x_ref[...] + 0.5 * _dot_nn(a_ref[...], wd_ref[...])

    return _call(
        body,
        name=name,
        grid=(t // TM,),
        in_specs=[_rows(TM, D_MODEL), _rows(TM, f), _resident((f, D_MODEL))],
        out_specs=[_rows(TM, D_MODEL)],
        out_shape=[jax.ShapeDtypeStruct((t, D_MODEL), F32)],
        args=(x, a_act, wd), sem=("parallel",), hosted=hosted)[0]


def _ffn_bwd_head(y, gamma_f, target, x, gamma, g_act, u_act, wg_t, wu_t, wd, name):
    t = x.shape[0]
    f = wg_t.shape[0]

    def body(y_ref, gamf_ref, t_ref, x_ref, gam_ref, g_ref, u_ref, wg_ref, wu_ref, wd_ref, dx_ref, dg_ref, du_ref,
             db_ref, dgam_ref, dgamf_ref, loss_ref):
        yh, ry = _rms(y_ref[...])
        gam_f = gamf_ref[...]
        e = yh * gam_f - t_ref[...]
        dv, dgam_f = _rms_bwd(e * (1.0 / D_MODEL), yh, ry, gam_f)
        db = (0.5 * dv).astype(BF16)
        db_ref[...] = db
        for j in range(f // FC):
            sl = slice(j * FC, (j + 1) * FC)
            da = _dot_nt(db, wd_ref[sl, :])
            g = g_ref[:, sl].astype(F32)
            u = u_ref[:, sl].astype(F32)
            s = jax.nn.sigmoid(g)
            dg_ref[:, sl] = (da * u * (s * (1.0 + g * (1.0 - s)))).astype(BF16)
            du_ref[:, sl] = (da * (g * s)).astype(BF16)
        dh = _dot_nn(dg_ref[...], wg_ref[...]) + _dot_nn(du_ref[...], wu_ref[...])
        xh, r = _rms(x_ref[...])
        dxn, dgam = _rms_bwd(dh, xh, r, gam_ref[...])
        dx_ref[...] = dv + dxn

        @pl.when(pl.program_id(0) == 0)
        def _():
            dgam_ref[...] = jnp.zeros_like(dgam_ref)
            dgamf_ref[...] = jnp.zeros_like(dgamf_ref)
            loss_ref[...] = jnp.zeros_like(loss_ref)

        dgam_ref[...] += dgam
        dgamf_ref[...] += dgam_f
        loss_ref[...] += _colsum8(e * e) * (0.5 / D_MODEL)

    acc = pl.BlockSpec((8, D_MODEL), lambda i: (0, 0))
    return _call(
        body,
        name=name,
        grid=(t // TM,),
        in_specs=[_rows(TM, D_MODEL), _resident((1, D_MODEL)), _rows(TM, D_MODEL), _rows(TM, D_MODEL),
                  _resident((1, D_MODEL)), _rows(TM, f), _rows(TM, f),
                  _resident((f, D_MODEL)), _resident((f, D_MODEL)), _resident((f, D_MODEL))],
        out_specs=[_rows(TM, D_MODEL), _rows(TM, f), _rows(TM, f), _rows(TM, D_MODEL), acc, acc, acc],
        out_shape=[jax.ShapeDtypeStruct((t, D_MODEL), F32), jax.ShapeDtypeStruct((t, f), BF16),
                   jax.ShapeDtypeStruct((t, f), BF16), jax.ShapeDtypeStruct((t, D_MODEL), BF16),
                   jax.ShapeDtypeStruct((8, D_MODEL), F32), jax.ShapeDtypeStruct((8, D_MODEL), F32),
                   jax.ShapeDtypeStruct((8, D_MODEL), F32)],
        args=(y, gamma_f, target, x, gamma, g_act, u_act, wg_t, wu_t, wd), sem=("arbitrary",))


def _ffn_bwd_act(d, g_act, u_act, wd, name, hosted=()):
    t = d.shape[0]
    f = wd.shape[0]

    def body(d_ref, g_ref, u_ref, wd_ref, dg_ref, du_ref):
        db = (0.5 * d_ref[...]).astype(BF16)
        for j in range(f // FC):
            sl = slice(j * FC, (j + 1) * FC)
            da = _dot_nt(db, wd_ref[sl, :])
            g = g_ref[:, sl].astype(F32)
            u = u_ref[:, sl].astype(F32)
            s = jax.nn.sigmoid(g)
            dg_ref[:, sl] = (da * u * (s * (1.0 + g * (1.0 - s)))).astype(BF16)
            du_ref[:, sl] = (da * (g * s)).astype(BF16)

    return _call(
        body,
        name=name,
        grid=(t // TM,),
        in_specs=[_rows(TM, D_MODEL), _rows(TM, f), _rows(TM, f), _resident((f, D_MODEL))],
        out_specs=[_rows(TM, f), _rows(TM, f)],
        out_shape=[jax.ShapeDtypeStruct((t, f), BF16), jax.ShapeDtypeStruct((t, f), BF16)],
        args=(d, g_act, u_act, wd), sem=("parallel",), hosted=hosted)


def _ffn_bwd_in(d, x, gamma, dg, du, wg_t, wu_t, name, hosted=()):
    t = x.shape[0]
    f = wg_t.shape[0]

    def body(d_ref, x_ref, gam_ref, dg_ref, du_ref, wg_ref, wu_ref, dx_ref, dgam_ref):
        dh = _dot_nn(dg_ref[...], wg_ref[...]) + _dot_nn(du_ref[...], wu_ref[...])
        xh, r = _rms(x_ref[...])
        dxn, dgam = _rms_bwd(dh, xh, r, gam_ref[...])
        dx_ref[...] = d_ref[...] + dxn

        @pl.when(pl.program_id(0) == 0)
        def _():
            dgam_ref[...] = jnp.zeros_like(dgam_ref)

        dgam_ref[...] += dgam

    return _call(
        body,
        name=name,
        grid=(t // TM,),
        in_specs=[_rows(TM, D_MODEL), _rows(TM, D_MODEL), _resident((1, D_MODEL)), _rows(TM, f), _rows(TM, f),
                  _resident((f, D_MODEL)), _resident((f, D_MODEL))],
        out_specs=[_rows(TM, D_MODEL), pl.BlockSpec((8, D_MODEL), lambda i: (0, 0))],
        out_shape=[jax.ShapeDtypeStruct((t, D_MODEL), F32), jax.ShapeDtypeStruct((8, D_MODEL), F32)],
        args=(d, x, gamma, dg, du, wg_t, wu_t), sem=("arbitrary",), hosted=hosted)


def _mm_tn(pieces, b, name, tile=256, hosted=()):
    t, n = b.shape
    npc = len(pieces)
    counts = [p.shape[1] // tile for p in pieces]
    los = [sum(counts[:k]) for k in range(npc)]
    total = sum(counts)

    def body(*refs):
        a_refs, b_ref, o_ref = refs[:npc], refs[npc], refs[npc + 1]
        i = pl.program_id(0)
        for k in range(npc):
            @pl.when(jnp.logical_and(i >= los[k], i < los[k] + counts[k]))
            def _(k=k):
                o_ref[...] = _dot_tn(a_refs[k][...], b_ref[...]).astype(BF16)

    def a_spec(k):
        return pl.BlockSpec((t, tile), lambda i: (0, jnp.clip(i - los[k], 0, counts[k] - 1)))

    return _call(
        body,
        name=name,
        grid=(total,),
        in_specs=[a_spec(k) for k in range(npc)] + [_resident((t, n))],
        out_specs=[pl.BlockSpec((tile, n), lambda i: (i, 0))],
        out_shape=[jax.ShapeDtypeStruct((total * tile, n), BF16)],
        args=(*pieces, b), sem=("parallel",), hosted=hosted)[0]


def _proj_fwd(x, gamma, win_t, hosted=()):
    t = x.shape[0]

    def body(x_ref, gam_ref, w_ref, h_ref, qa_ref, qb_ref, gt_ref):
        xh, _ = _rms(x_ref[...])
        h = (xh * gam_ref[...]).astype(BF16)
        h_ref[...] = h
        for j in range(QKV_A // FC):
            qa_ref[:, j * FC:(j + 1) * FC] = _dot_nt(h, w_ref[j * FC:(j + 1) * FC, :]).astype(BF16)
        for j in range(QKV_B // FC):
            lo = QKV_A + j * FC
            qb_ref[:, j * FC:(j + 1) * FC] = _dot_nt(h, w_ref[lo:lo + FC, :]).astype(BF16)
        for j in range(2 * D_MODEL // FC):
            lo = QKV_A + QKV_B + j * FC
            gt_ref[:, j * FC:(j + 1) * FC] = _dot_nt(h, w_ref[lo:lo + FC, :])

    return _call(
        body,
        name="proj_fwd",
        grid=(t // TM,),
        in_specs=[_rows(TM, D_MODEL), _resident((1, D_MODEL)), _resident((IN_WIDTH, D_MODEL))],
        out_specs=[_rows(TM, D_MODEL), _rows(TM, QKV_A), _rows(TM, QKV_B), _rows(TM, 2 * D_MODEL)],
        out_shape=[jax.ShapeDtypeStruct((t, D_MODEL), BF16), jax.ShapeDtypeStruct((t, QKV_A), BF16),
                   jax.ShapeDtypeStruct((t, QKV_B), BF16), jax.ShapeDtypeStruct((t, 2 * D_MODEL), F32)],
        args=(x, gamma, win_t), sem=("parallel",), hosted=hosted)


def _proj_bwd(d, x, gamma, pieces, win_t, hosted=()):
    t = x.shape[0]
    npc = len(pieces)
    widths = [p.shape[1] for p in pieces]
    los = [sum(widths[:k]) for k in range(npc)]

    def body(*refs):
        d_ref, x_ref, gam_ref = refs[:3]
        p_refs = refs[3:3 + npc]
        w_ref, dx_ref, db_ref, dgam_ref = refs[3 + npc:]
        dh = _dot_nn(p_refs[0][...], w_ref[0:widths[0], :])
        for k in range(1, npc):
            dh += _dot_nn(p_refs[k][...], w_ref[los[k]:los[k] + widths[k], :])
        xh, r = _rms(x_ref[...])
        dxn, dgam = _rms_bwd(dh, xh, r, gam_ref[...])
        dx = d_ref[...] + dxn
        dx_ref[...] = dx
        db_ref[...] = (0.5 * dx).astype(BF16)

        @pl.when(pl.program_id(0) == 0)
        def _():
            dgam_ref[...] = jnp.zeros_like(dgam_ref)

        dgam_ref[...] += dgam

    return _call(
        body,
        name="proj_bwd",
        grid=(t // TM,),
        in_specs=[_rows(TM, D_MODEL), _rows(TM, D_MODEL), _resident((1, D_MODEL))] + [_rows(TM, w) for w in widths]
        + [_resident((IN_WIDTH, D_MODEL))],
        out_specs=[_rows(TM, D_MODEL), _rows(TM, D_MODEL), pl.BlockSpec((8, D_MODEL), lambda i: (0, 0))],
        out_shape=[jax.ShapeDtypeStruct((t, D_MODEL), F32), jax.ShapeDtypeStruct((t, D_MODEL), BF16),
                   jax.ShapeDtypeStruct((8, D_MODEL), F32)],
        args=(d, x, gamma, *pieces, win_t), sem=("arbitrary",), hosted=hosted)


def _lane_half(shape):
    return lax.broadcasted_iota(jnp.int32, shape, len(shape) - 1) // D_HEAD


def _band_weights(q, kk, bias, sink, qs, pad):
    s = _band_scores(q, kk, bias, qs, pad)
    m = jnp.max(s, axis=-1, keepdims=True)
    if sink is not None:
        m = jnp.maximum(m, sink)
    return jnp.exp(s - m), m


def _band_scores(q, kk, bias, qs, pad):
    s = _dot_nt(q, kk) + bias
    if qs is not None:
        col = lax.broadcasted_iota(jnp.int32, s.shape, 1)
        s = jnp.where(col + qs >= pad, s, NEG_INF)
    return s


def _weighted_values(p, vv_ones, sink, m):
    r = _dot_nn(p.astype(BF16), vv_ones)
    den = r[:, LANES:2 * LANES]
    if sink is not None:
        den = den + jnp.exp(sink - m)
    return r[:, 0:LANES] / den, m + jnp.log(den[:, 0:1])


def _fill_padded(dst, src, pad):
    dst[0:pad, :] = jnp.zeros((pad,) + dst.shape[1:], dst.dtype)
    dst[pad:, :] = src


FWD_PAIRS = 4
BWD_PAIRS = 4


def _attn_a_fwd(qkv, bias, hosted=()):
    bsz, s_len, _ = qkv.shape
    pad = A_PREV * CHUNK
    band = TQ + pad
    pp = FWD_PAIRS
    w = pp * LANES
    nb = A_WIDTH // w

    def body(q_ref, k_ref, v_ref, b_ref, o_ref, l_ref, kp, vp):
        i = pl.program_id(2)

        @pl.when(i == 0)
        def _():
            _fill_padded(kp, k_ref[...], pad)
            _fill_padded(vp, v_ref[...], pad)

        qs = pl.multiple_of(i * TQ, TQ)
        half = _lane_half((1, LANES))

        ones = jnp.ones((band, LANES), BF16)

        def block(masked):
            for pr in range(pp):
                sl = slice(pr * LANES, (pr + 1) * LANES)
                kk = kp[pl.ds(qs, band), sl]
                vv = jnp.concatenate([vp[pl.ds(qs, band), sl], ones], axis=1)
                q = q_ref[:, sl] * SCALE
                outs = []
                for j in range(2):
                    qm = jnp.where(half == j, q, jnp.zeros_like(q))
                    p, m = _band_weights(qm, kk, b_ref[2 * pr + j], None, qs if masked else None, pad)
                    o, lse = _weighted_values(p, vv, None, m)
                    outs.append(o)
                    l_ref[:, 2 * pr + j:2 * pr + j + 1] = lse
                o_ref[:, sl] = jnp.where(half == 0, outs[0], outs[1]).astype(BF16)

        pl.when(i < pad // TQ)(lambda: block(True))
        pl.when(i >= pad // TQ)(lambda: block(False))

    return _call(
        body,
        name="attn_a_fwd",
        grid=(bsz, nb, s_len // TQ),
        in_specs=[pl.BlockSpec((None, TQ, w), lambda b, g, i: (b, i, g)),
                  pl.BlockSpec((None, s_len, w), lambda b, g, i: (b, 0, nb + g)),
                  pl.BlockSpec((None, s_len, w), lambda b, g, i: (b, 0, 2 * nb + g)),
                  pl.BlockSpec((2 * pp, TQ, band), lambda b, g, i: (g, 0, 0))],
        out_specs=[pl.BlockSpec((None, TQ, w), lambda b, g, i: (b, i, g)),
                   pl.BlockSpec((None, TQ, 2 * pp), lambda b, g, i: (b, i, g))],
        out_shape=[jax.ShapeDtypeStruct((bsz, s_len, A_WIDTH), BF16),
                   jax.ShapeDtypeStruct((bsz, s_len, A_HEADS), F32)],
        scratch_shapes=[pltpu.VMEM((pad + s_len, w), BF16), pltpu.VMEM((pad + s_len, w), BF16)],
        args=(qkv, qkv, qkv, bias), sem=("arbitrary", "arbitrary", "arbitrary"), hosted=hosted)


def _attn_a_bwd(qkv, bias, do, o, lse, hosted=()):
    bsz, s_len, _ = qkv.shape
    pad = A_PREV * CHUNK
    band = TQ + pad
    n_i = s_len // TQ
    pp = BWD_PAIRS
    w = pp * LANES
    nb = A_WIDTH // w

    def body(q_ref, k_ref, v_ref, b_ref, do_ref, o_ref, l_ref, dq_ref, dk_ref, dv_ref, dbias_ref, kp, vp, dk_acc,
             dv_acc):
        b = pl.program_id(1)
        i = pl.program_id(2)

        @pl.when(i == 0)
        def _():
            _fill_padded(kp, k_ref[...], pad)
            _fill_padded(vp, v_ref[...], pad)
            dk_acc[...] = jnp.zeros_like(dk_acc)
            dv_acc[...] = jnp.zeros_like(dv_acc)

        @pl.when(jnp.logical_and(b == 0, i == 0))
        def _():
            dbias_ref[...] = jnp.zeros_like(dbias_ref)

        qs = pl.multiple_of(i * TQ, TQ)
        half = _lane_half((1, LANES))
        half_t = lax.broadcasted_iota(jnp.int32, (LANES, 1), 0) // D_HEAD

        def block(masked):
            for pr in range(pp):
                sl = slice(pr * LANES, (pr + 1) * LANES)
                kk = kp[pl.ds(qs, band), sl]
                vv = vp[pl.ds(qs, band), sl]
                q = q_ref[:, sl] * SCALE
                dd = do_ref[:, sl]
                od = dd.astype(F32) * o_ref[:, sl].astype(F32)
                dqs, dks, dvs = [], [], []
                for j in range(2):
                    hd = 2 * pr + j
                    qm = jnp.where(half == j, q, jnp.zeros_like(q))
                    dm = jnp.where(half == j, dd, jnp.zeros_like(dd))
                    s = _band_scores(qm, kk, b_ref[hd], qs if masked else None, pad)
                    pn = jnp.exp(s - l_ref[:, hd:hd + 1])
                    dp = _dot_nt(dm, vv)
                    delta = jnp.sum(jnp.where(half == j, od, 0.0), axis=-1, keepdims=True)
                    ds = pn * (dp - delta)
                    dbias_ref[hd] += ds[:, band - REL_COLS:]
                    dsb = ds.astype(BF16)
                    dqs.append(_dot_nn(dsb, kk))
                    dks.append(_dot_tn(q, dsb))
                    dvs.append(_dot_tn(dd, pn.astype(BF16)))
                dq_ref[:, sl] = (jnp.where(half == 0, dqs[0], dqs[1]) * SCALE).astype(BF16)
                dk_acc[sl, pl.ds(qs, band)] += jnp.where(half_t == 0, dks[0], dks[1])
                dv_acc[sl, pl.ds(qs, band)] += jnp.where(half_t == 0, dvs[0], dvs[1])

        pl.when(i < pad // TQ)(lambda: block(True))
        pl.when(i >= pad // TQ)(lambda: block(False))

        @pl.when(i == n_i - 1)
        def _():
            dk_ref[...] = dk_acc[:, pad:].T.astype(BF16)
            dv_ref[...] = dv_acc[:, pad:].T.astype(BF16)

    qspec = pl.BlockSpec((None, TQ, w), lambda g, b, i: (b, i, g))
    kvout = pl.BlockSpec((None, s_len, w), lambda g, b, i: (b, 0, g))
    wide = jax.ShapeDtypeStruct((bsz, s_len, A_WIDTH), BF16)
    return _call(
        body,
        name="attn_a_bwd",
        grid=(nb, bsz, n_i),
        in_specs=[qspec,
                  pl.BlockSpec((None, s_len, w), lambda g, b, i: (b, 0, nb + g)),
                  pl.BlockSpec((None, s_len, w), lambda g, b, i: (b, 0, 2 * nb + g)),
                  pl.BlockSpec((2 * pp, TQ, band), lambda g, b, i: (g, 0, 0)),
                  qspec, qspec,
                  pl.BlockSpec((None, TQ, 2 * pp), lambda g, b, i: (b, i, g))],
        out_specs=[qspec, kvout, kvout, pl.BlockSpec((2 * pp, TQ, REL_COLS), lambda g, b, i: (g, 0, 0))],
        out_shape=[wide, wide, wide, jax.ShapeDtypeStruct((A_HEADS, TQ, REL_COLS), F32)],
        scratch_shapes=[pltpu.VMEM((pad + s_len, w), BF16), pltpu.VMEM((pad + s_len, w), BF16),
                        pltpu.VMEM((w, pad + s_len), F32), pltpu.VMEM((w, pad + s_len), F32)],
        args=(qkv, qkv, qkv, bias, do, o, lse), sem=("arbitrary", "arbitrary", "arbitrary"), hosted=hosted)


def _fill_padded_dup(dst, src, pad, h, half):
    other = pltpu.roll(src, D_HEAD, 1)
    _fill_padded(dst, jnp.where(half == h, src, other), pad)


def _attn_b_fwd(qkv, bias, sink):
    bsz, s_len, _ = qkv.shape
    pad = B_PREV * CHUNK
    band = TQ + pad
    kcol = B_Q_WIDTH // LANES
    npair = B_Q_HEADS // 2

    def body(q_ref, k_ref, v_ref, b_ref, s_ref, o_ref, l_ref, kp, vp):
        i = pl.program_id(1)
        half = _lane_half((1, LANES))

        @pl.when(i == 0)
        def _():
            for h in range(B_KV_HEADS):
                _fill_padded_dup(kp.at[h], k_ref[...], pad, h, half)
                _fill_padded_dup(vp.at[h], v_ref[...], pad, h, half)

        qs = pl.multiple_of(i * TQ, TQ)

        ones = jnp.ones((band, LANES), BF16)

        def block(masked):
            for pr in range(npair):
                h = pr // (B_GROUP // 2)
                sl = slice(pr * LANES, (pr + 1) * LANES)
                kk = kp[h, pl.ds(qs, band), :]
                vv = jnp.concatenate([vp[h, pl.ds(qs, band), :], ones], axis=1)
                q = q_ref[:, sl] * SCALE
                outs = []
                for j in range(2):
                    qm = jnp.where(half == j, q, jnp.zeros_like(q))
                    sink = s_ref[2 * pr + j][0:1, 0:1]
                    p, m = _band_weights(qm, kk, b_ref[2 * pr + j], sink, qs if masked else None, pad)
                    o, lse = _weighted_values(p, vv, sink, m)
                    outs.append(o)
                    l_ref[:, 2 * pr + j:2 * pr + j + 1] = lse
                o_ref[:, sl] = jnp.where(half == 0, outs[0], outs[1]).astype(BF16)

        pl.when(i < -(-pad // TQ))(lambda: block(True))
        pl.when(i >= -(-pad // TQ))(lambda: block(False))

    return pl.pallas_call(
        body,
        name="attn_b_fwd",
        grid=(bsz, s_len // TQ),
        in_specs=[pl.BlockSpec((None, TQ, B_Q_WIDTH), lambda b, i: (b, i, 0)),
                  pl.BlockSpec((None, s_len, LANES), lambda b, i: (b, 0, kcol)),
                  pl.BlockSpec((None, s_len, LANES), lambda b, i: (b, 0, kcol + 1)),
                  pl.BlockSpec((B_Q_HEADS, TQ, band), lambda b, i: (0, 0, 0)),
                  pl.BlockSpec((B_Q_HEADS, 8, LANES), lambda b, i: (0, 0, 0))],
        out_specs=[pl.BlockSpec((None, TQ, B_Q_WIDTH), lambda b, i: (b, i, 0)),
                   pl.BlockSpec((None, TQ, B_Q_HEADS), lambda b, i: (b, i, 0))],
        out_shape=[jax.ShapeDtypeStruct((bsz, s_len, B_Q_WIDTH), BF16),
                   jax.ShapeDtypeStruct((bsz, s_len, B_Q_HEADS), F32)],
        scratch_shapes=[pltpu.VMEM((B_KV_HEADS, pad + s_len, LANES), BF16),
                        pltpu.VMEM((B_KV_HEADS, pad + s_len, LANES), BF16)],
        compiler_params=_cparams(("arbitrary", "arbitrary")),
    )(qkv, qkv, qkv, bias, sink)


def _attn_b_bwd(qkv, bias, sink, do, o, lse, hosted=()):
    bsz, s_len, _ = qkv.shape
    pad = B_PREV * CHUNK
    band = TQ + pad
    kcol = B_Q_WIDTH // LANES
    n_i = s_len // TQ
    pp = B_GROUP // 2

    def body(q_ref, k_ref, v_ref, b_ref, s_ref, do_ref, o_ref, l_ref, dq_ref, dkv_ref, dsink_ref, kp, vp, dk_acc,
             dv_acc):
        b = pl.program_id(0)
        i = pl.program_id(1)
        half = _lane_half((1, LANES))
        half_t = lax.broadcasted_iota(jnp.int32, (LANES, 1), 0) // D_HEAD

        @pl.when(i == 0)
        def _():
            for h in range(B_KV_HEADS):
                _fill_padded_dup(kp.at[h], k_ref[...], pad, h, half)
                _fill_padded_dup(vp.at[h], v_ref[...], pad, h, half)
            dk_acc[...] = jnp.zeros_like(dk_acc)
            dv_acc[...] = jnp.zeros_like(dv_acc)

        @pl.when(jnp.logical_and(b == 0, i == 0))
        def _():
            dsink_ref[...] = jnp.zeros_like(dsink_ref)

        qs = pl.multiple_of(i * TQ, TQ)

        def block(masked):
            heads_dk, heads_dv = [], []
            for h in range(B_KV_HEADS):
                kk = kp[h, pl.ds(qs, band), :]
                vv = vp[h, pl.ds(qs, band), :]
                dk2 = jnp.zeros((LANES, band), F32)
                dv2 = jnp.zeros((LANES, band), F32)
                for pr in range(pp * h, pp * (h + 1)):
                    sl = slice(pr * LANES, (pr + 1) * LANES)
                    q = q_ref[:, sl] * SCALE
                    dd = do_ref[:, sl]
                    od = dd.astype(F32) * o_ref[:, sl].astype(F32)
                    dqs, dks, dvs = [], [], []
                    for j in range(2):
                        qm = jnp.where(half == j, q, jnp.zeros_like(q))
                        dm = jnp.where(half == j, dd, jnp.zeros_like(dd))
                        hd = 2 * pr + j
                        lse = l_ref[:, hd:hd + 1]
                        s = _band_scores(qm, kk, b_ref[hd], qs if masked else None, pad)
                        pn = jnp.exp(s - lse)
                        dp = _dot_nt(dm, vv)
                        delta = jnp.sum(jnp.where(half == j, od, 0.0), axis=-1, keepdims=True)
                        ds = pn * (dp - delta)
                        dsb = ds.astype(BF16)
                        dqs.append(_dot_nn(dsb, kk))
                        dks.append(_dot_tn(q, dsb))
                        dvs.append(_dot_tn(dd, pn.astype(BF16)))
                        sink = s_ref[hd][0:1, 0:1]
                        dsk = jnp.sum(-jnp.exp(sink - lse) * delta, axis=0, keepdims=True)
                        dsink_ref[hd] += jnp.broadcast_to(dsk, (8, LANES))
                    dq_ref[:, sl] = (jnp.where(half == 0, dqs[0], dqs[1]) * SCALE).astype(BF16)
                    dk2 = dk2 + jnp.where(half_t == 0, dks[0], dks[1])
                    dv2 = dv2 + jnp.where(half_t == 0, dvs[0], dvs[1])
                heads_dk.append(dk2 + jnp.concatenate([dk2[D_HEAD:], dk2[:D_HEAD]], axis=0))
                heads_dv.append(dv2 + jnp.concatenate([dv2[D_HEAD:], dv2[:D_HEAD]], axis=0))
            dk_acc[:, pl.ds(qs, band)] += jnp.where(half_t == 0, heads_dk[0], heads_dk[1])
            dv_acc[:, pl.ds(qs, band)] += jnp.where(half_t == 0, heads_dv[0], heads_dv[1])

        pl.when(i < -(-pad // TQ))(lambda: block(True))
        pl.when(i >= -(-pad // TQ))(lambda: block(False))

        @pl.when(i == n_i - 1)
        def _():
            dkv_ref[:, 0:LANES] = dk_acc[:, pad:].T.astype(BF16)
            dkv_ref[:, LANES:2 * LANES] = dv_acc[:, pad:].T.astype(BF16)

    qspec = pl.BlockSpec((None, TQ, B_Q_WIDTH), lambda b, i: (b, i, 0))
    return _call(
        body,
        name="attn_b_bwd",
        grid=(bsz, n_i),
        in_specs=[qspec,
                  pl.BlockSpec((None, s_len, LANES), lambda b, i: (b, 0, kcol)),
                  pl.BlockSpec((None, s_len, LANES), lambda b, i: (b, 0, kcol + 1)),
                  pl.BlockSpec((B_Q_HEADS, TQ, band), lambda b, i: (0, 0, 0)),
                  pl.BlockSpec((B_Q_HEADS, 8, LANES), lambda b, i: (0, 0, 0)),
                  qspec, qspec,
                  pl.BlockSpec((None, TQ, B_Q_HEADS), lambda b, i: (b, i, 0))],
        out_specs=[qspec, pl.BlockSpec((None, s_len, 2 * LANES), lambda b, i: (b, 0, 0)),
                   pl.BlockSpec((B_Q_HEADS, 8, LANES), lambda b, i: (0, 0, 0))],
        out_shape=[jax.ShapeDtypeStruct((bsz, s_len, B_Q_WIDTH), BF16),
                   jax.ShapeDtypeStruct((bsz, s_len, 2 * B_KV_WIDTH), BF16),
                   jax.ShapeDtypeStruct((B_Q_HEADS, 8, LANES), F32)],
        scratch_shapes=[pltpu.VMEM((B_KV_HEADS, pad + s_len, LANES), BF16),
                        pltpu.VMEM((B_KV_HEADS, pad + s_len, LANES), BF16),
                        pltpu.VMEM((LANES, pad + s_len), F32), pltpu.VMEM((LANES, pad + s_len), F32)],
        args=(qkv, qkv, qkv, bias, sink, do, o, lse), sem=("arbitrary", "arbitrary"), hosted=hosted)


REL_COLS = 3 * 128
REL_WRAP = 512


def _bias_a_build(tv, hosted=()):
    h = tv.shape[0]
    pad = A_PREV * CHUNK
    band = TQ + pad

    def body(tv_ref, o_ref):
        row = tv_ref[...]
        x = jnp.broadcast_to(row, (TQ, REL_WRAP))
        r = lax.broadcasted_iota(jnp.int32, x.shape, 0)
        for bit in range(8):
            sh = 1 << bit
            x = jnp.where((r & sh) != 0, pltpu.roll(x, sh, 1), x)
        far = jnp.broadcast_to(row[:, 0:1], (TQ, band - REL_COLS))
        full = jnp.concatenate([far, x[:, REL_WRAP // 2:REL_WRAP], x[:, 0:REL_COLS - REL_WRAP // 2]], axis=1)
        qc = (lax.broadcasted_iota(jnp.int32, full.shape, 0) + pad) // CHUNK
        kc = lax.broadcasted_iota(jnp.int32, full.shape, 1) // CHUNK
        ok = jnp.logical_and(kc <= qc, kc >= qc - A_PREV)
        o_ref[...] = jnp.where(ok, full, NEG_INF)

    return _call(
        body,
        name="bias_a_build",
        grid=(h,),
        in_specs=[pl.BlockSpec((None, 1, REL_WRAP), lambda hh: (hh, 0, 0))],
        out_specs=[pl.BlockSpec((None, TQ, band), lambda hh: (hh, 0, 0))],
        out_shape=[jax.ShapeDtypeStruct((h, TQ, band), F32)],
        args=(tv,), sem=("parallel",), hosted=hosted)[0]


def _relbias_grad(dbias, hosted=()):
    h, rows, _ = dbias.shape

    def body(d_ref, o_ref):
        x = d_ref[...]
        r = lax.broadcasted_iota(jnp.int32, x.shape, 0)
        c = lax.broadcasted_iota(jnp.int32, x.shape, 1) - r
        x = jnp.where(jnp.logical_and(c >= 1, c < REL_TABLE), x, 0.0)
        for bit in range(8):
            sh = 1 << bit
            x = jnp.where((r & sh) != 0, pltpu.roll(x, REL_COLS - sh, 1), x)
        diag = jnp.sum(x, axis=0, keepdims=True)
        lane = lax.broadcasted_iota(jnp.int32, diag.shape, 1)
        diag = jnp.where(jnp.logical_and(lane >= 1, lane < REL_TABLE), diag, 0.0)
        rest = -jnp.sum(diag, axis=1, keepdims=True)
        o_ref[...] = jnp.broadcast_to(jnp.where(lane == 0, rest, diag), o_ref.shape)

    return _call(
        body,
        name="relbias_grad",
        grid=(h,),
        in_specs=[pl.BlockSpec((None, rows, REL_COLS), lambda hh: (hh, 0, 0))],
        out_specs=[pl.BlockSpec((None, 8, REL_COLS), lambda hh: (hh, 0, 0))],
        out_shape=[jax.ShapeDtypeStruct((h, 8, REL_COLS), F32)],
        args=(dbias,), sem=("parallel",), hosted=hosted)[0]


def _mix_out_fwd(x, oa, ob, gates, proj_t, wout):
    t = x.shape[0]

    def body(x_ref, oa_ref, ob_ref, gt_ref, pt_ref, wo_ref, y_ref, ya_ref, yb_ref, mg_ref):
        ya = _dot_nt(oa_ref[...], pt_ref[:, 0:A_WIDTH])
        yb = _dot_nt(ob_ref[...], pt_ref[:, A_WIDTH:A_WIDTH + B_Q_WIDTH])
        ya_ref[...] = ya.astype(BF16)
        yb_ref[...] = yb.astype(BF16)
        mg = jax.nn.sigmoid(gt_ref[:, 0:D_MODEL]) * ya + jax.nn.sigmoid(gt_ref[:, D_MODEL:2 * D_MODEL]) * yb
        mgb = mg.astype(BF16)
        mg_ref[...] = mgb
        y_ref[...] = x_ref[...] + _dot_nn(mgb, wo_ref[...])

    return pl.pallas_call(
        body,
        name="mix_out_fwd",
        grid=(t // TM,),
        in_specs=[_rows(TM, D_MODEL), _rows(TM, A_WIDTH), _rows(TM, B_Q_WIDTH), _rows(TM, 2 * D_MODEL),
                  _resident((D_MODEL, A_WIDTH + B_Q_WIDTH)), _resident((D_MODEL, D_MODEL))],
        out_specs=[_rows(TM, D_MODEL), _rows(TM, D_MODEL), _rows(TM, D_MODEL), _rows(TM, D_MODEL)],
        out_shape=[jax.ShapeDtypeStruct((t, D_MODEL), F32), jax.ShapeDtypeStruct((t, D_MODEL), BF16),
                   jax.ShapeDtypeStruct((t, D_MODEL), BF16), jax.ShapeDtypeStruct((t, D_MODEL), BF16)],
        compiler_params=_cparams(("parallel",)),
    )(x, oa, ob, gates, proj_t, wout)


def _mix_out_bwd(d, gates, ya, yb, mg, oa, ob, proj_t, wout, hosted=()):
    t = d.shape[0]
    nt = t // TM

    def body(d_ref, gt_ref, ya_ref, yb_ref, mg_ref, oa_ref, ob_ref, pt_ref, wo_ref,
             doa_ref, dob_ref, dgt_ref, gwo_ref, gwp_ref, acc_o, acc_p):
        i = pl.program_id(0)
        db = d_ref[...].astype(BF16)
        dmg = _dot_nt(db, wo_ref[...])
        sa = jax.nn.sigmoid(gt_ref[:, 0:D_MODEL])
        sb = jax.nn.sigmoid(gt_ref[:, D_MODEL:2 * D_MODEL])
        dya = (dmg * sa).astype(BF16)
        dyb = (dmg * sb).astype(BF16)
        dgt_ref[:, 0:D_MODEL] = (dmg * ya_ref[...].astype(F32) * (sa * (1.0 - sa))).astype(BF16)
        dgt_ref[:, D_MODEL:2 * D_MODEL] = (dmg * yb_ref[...].astype(F32) * (sb * (1.0 - sb))).astype(BF16)
        doa_ref[...] = _dot_nn(dya, pt_ref[:, 0:A_WIDTH]).astype(BF16)
        dob_ref[...] = _dot_nn(dyb, pt_ref[:, A_WIDTH:A_WIDTH + B_Q_WIDTH]).astype(BF16)

        @pl.when(i == 0)
        def _():
            acc_o[...] = jnp.zeros_like(acc_o)
            acc_p[...] = jnp.zeros_like(acc_p)

        acc_o[...] += _dot_tn(mg_ref[...], db)
        acc_p[:, 0:A_WIDTH] += _dot_tn(dya, oa_ref[...])
        acc_p[:, A_WIDTH:A_WIDTH + B_Q_WIDTH] += _dot_tn(dyb, ob_ref[...])

        @pl.when(i == nt - 1)
        def _():
            gwo_ref[...] = acc_o[...].astype(BF16)
            gwp_ref[...] = acc_p[...].astype(BF16)

    whole = pl.BlockSpec((D_MODEL, D_MODEL), lambda i: (0, 0))
    return _call(
        body,
        name="mix_out_bwd",
        grid=(nt,),
        in_specs=[_rows(TM, D_MODEL), _rows(TM, 2 * D_MODEL), _rows(TM, D_MODEL), _rows(TM, D_MODEL),
                  _rows(TM, D_MODEL), _rows(TM, A_WIDTH), _rows(TM, B_Q_WIDTH),
                  _resident((D_MODEL, A_WIDTH + B_Q_WIDTH)), _resident((D_MODEL, D_MODEL))],
        out_specs=[_rows(TM, A_WIDTH), _rows(TM, B_Q_WIDTH), _rows(TM, 2 * D_MODEL), whole, whole],
        out_shape=[jax.ShapeDtypeStruct((t, A_WIDTH), BF16), jax.ShapeDtypeStruct((t, B_Q_WIDTH), BF16),
                   jax.ShapeDtypeStruct((t, 2 * D_MODEL), BF16), jax.ShapeDtypeStruct((D_MODEL, D_MODEL), BF16),
                   jax.ShapeDtypeStruct((D_MODEL, D_MODEL), BF16)],
        scratch_shapes=[pltpu.VMEM((D_MODEL, D_MODEL), F32), pltpu.VMEM((D_MODEL, A_WIDTH + B_Q_WIDTH), F32)],
        args=(d, gates, ya, yb, mg, oa, ob, proj_t, wout), sem=("arbitrary",), hosted=hosted)


def _place():
    x, y, c = lax.axis_index("x"), lax.axis_index("y"), lax.axis_index("c")
    chips = [(1 - x, y), (x, 1 - y), (1 - x, 1 - y)]
    return x, y, c, chips


class _Gather:
    per = 8

    def __init__(self, shards):
        n = len(shards)
        self.inputs = list(shards)
        self.out_shape = [jax.ShapeDtypeStruct((N_DEV * s.shape[0], s.shape[1]), s.dtype) for s in shards]
        self.scratch = [pltpu.SemaphoreType.DMA((n * self.per,)), pltpu.SemaphoreType.DMA((n * self.per,)),
                        pltpu.SemaphoreType.DMA((n,))]
        self.result = None

    def _parts(self, ins, outs, sems):
        send_sems, recv_sems, local_sems = sems
        x, y, c, chips = _place()
        me, sibling = (x, y, c), (x, y, 1 - c)
        xn, yn, dg = chips
        n = len(ins)

        def rows(k, p, part=None):
            r = ins[k].shape[0]
            base = (4 * p[0] + 2 * p[1] + p[2]) * r
            if part is None:
                return outs[k].at[pl.ds(base, r), :]
            return outs[k].at[pl.ds(base + part * (r // 2), r // 2), :]

        def copy(k, slot, block, to, src=None, part=None):
            return pltpu.make_async_remote_copy(
                src_ref=rows(k, block, part) if src is None else src, dst_ref=rows(k, block, part),
                send_sem=send_sems.at[k * self.per + slot], recv_sem=recv_sems.at[k * self.per + slot],
                device_id=to, device_id_type=MESH)

        mine = [pltpu.make_async_copy(ins[k], rows(k, me), local_sems.at[k]) for k in range(n)]
        sends, lands = [], []
        for k in range(n):
            sends.append({
                0: copy(k, 0, me, sibling, src=ins[k]),
                1: copy(k, 1, me, (*xn, c), src=ins[k]),
                2: copy(k, 2, me, (*yn, c), src=ins[k]),
                3: copy(k, 3, (*xn, c), (*yn, c), part=0),
                4: copy(k, 4, (*yn, c), (*xn, c), part=1),
                5: copy(k, 5, (*xn, c), sibling),
                6: copy(k, 6, (*yn, c), sibling),
                7: copy(k, 7, (*dg, c), sibling)})
            lands.append({
                0: copy(k, 0, sibling, me),
                1: copy(k, 1, (*xn, c), me),
                2: copy(k, 2, (*yn, c), me),
                3: copy(k, 3, (*dg, c), me, part=0),
                4: copy(k, 4, (*dg, c), me, part=1),
                5: copy(k, 5, (*xn, 1 - c), me),
                6: copy(k, 6, (*yn, 1 - c), me),
                7: copy(k, 7, (*dg, 1 - c), me)})
        return n, mine, sends, lands

    def start(self, ins, outs, sems):
        n, mine, sends, _ = self._parts(ins, outs, sems)
        for cp in mine:
            cp.start()
        for slot in (0, 1, 2):
            for k in range(n):
                sends[k][slot].start()

    def relay(self, ins, outs, sems):
        n, _, sends, lands = self._parts(ins, outs, sems)
        for k in range(n):
            lands[k][1].wait_recv()
            sends[k][3].start()
            sends[k][5].start()
        for k in range(n):
            lands[k][2].wait_recv()
            sends[k][4].start()
            sends[k][6].start()

    def forward(self, ins, outs, sems):
        n, _, sends, lands = self._parts(ins, outs, sems)
        for k in range(n):
            lands[k][3].wait_recv()
            lands[k][4].wait_recv()
            sends[k][7].start()

    def finish(self, ins, outs, sems):
        n, mine, sends, lands = self._parts(ins, outs, sems)
        for k in range(n):
            for slot in (0, 5, 6, 7):
                lands[k][slot].wait_recv()
        for k in range(n):
            for slot in range(self.per):
                sends[k][slot].wait_send()
        for cp in mine:
            cp.wait()


class _PairExchange:
    def __init__(self, grads):
        n = len(grads)
        self.inputs = list(grads)
        self.out_shape = [jax.ShapeDtypeStruct((g.shape[0] // 2, g.shape[1]), g.dtype) for g in grads]
        self.scratch = [pltpu.SemaphoreType.DMA((n * N_CHIP,)), pltpu.SemaphoreType.DMA((n * N_CHIP,))]
        self.result = None

    def _copies(self, ins, outs, sems):
        send_sems, recv_sems = sems
        x, y, c, _ = _place()
        copies = []
        for k in range(len(ins)):
            r = ins[k].shape[0] // N_DEV
            for q in range(N_CHIP):
                copies.append(pltpu.make_async_remote_copy(
                    src_ref=ins[k].at[pl.ds((2 * q + 1 - c) * r, r), :], dst_ref=outs[k].at[pl.ds(q * r, r), :],
                    send_sem=send_sems.at[k * N_CHIP + q], recv_sem=recv_sems.at[k * N_CHIP + q],
                    device_id=(x, y, 1 - c), device_id_type=MESH))
        return copies

    def start(self, ins, outs, sems):
        for cp in self._copies(ins, outs, sems):
            cp.start()

    def relay(self, ins, outs, sems):
        pass

    def forward(self, ins, outs, sems):
        pass

    def finish(self, ins, outs, sems):
        copies = self._copies(ins, outs, sems)
        for cp in copies:
            cp.wait_recv()
        for cp in copies:
            cp.wait_send()


class _ChipExchange(_PairExchange):
    def __init__(self, psums):
        n = len(psums)
        self.inputs = list(psums)
        self.out_shape = [jax.ShapeDtypeStruct((3 * p.shape[0] // N_CHIP, p.shape[1]), p.dtype) for p in psums]
        self.scratch = [pltpu.SemaphoreType.DMA((n * 3,)), pltpu.SemaphoreType.DMA((n * 3,))]
        self.result = None

    def _copies(self, ins, outs, sems):
        send_sems, recv_sems = sems
        _, _, c, chips = _place()
        copies = []
        for k in range(len(ins)):
            r = ins[k].shape[0] // N_CHIP
            for j, chip in enumerate(chips):
                copies.append(pltpu.make_async_remote_copy(
                    src_ref=ins[k].at[pl.ds((2 * chip[0] + chip[1]) * r, r), :], dst_ref=outs[k].at[pl.ds(j * r, r), :],
                    send_sem=send_sems.at[k * 3 + j], recv_sem=recv_sems.at[k * 3 + j],
                    device_id=(*chip, c), device_id_type=MESH))
        return copies


def _exchange_alone(xchg, name):
    n_in, n_out = len(xchg.inputs), len(xchg.out_shape)

    def body(*refs):
        ins, outs, sems = refs[:n_in], refs[n_in:n_in + n_out], refs[n_in + n_out:]
        xchg.start(ins, outs, sems)
        xchg.relay(ins, outs, sems)
        xchg.forward(ins, outs, sems)
        xchg.finish(ins, outs, sems)

    xchg.result = list(pl.pallas_call(
        body, name=name, in_specs=[_hbm()] * n_in, out_specs=[_hbm()] * n_out, out_shape=xchg.out_shape,
        scratch_shapes=xchg.scratch)(*xchg.inputs))
    return xchg.result


def _pair_sum(core, grads, recvd, name):
    n = len(grads)
    r = grads[0].shape[0] // N_DEV
    cdim = grads[0].shape[1]
    tr = r // 2 if r % 32 == 0 else r
    nt = r // tr

    def body(core_ref, *refs):
        del core_ref
        for k in range(n):
            refs[2 * n + k][...] = (refs[k][...].astype(F32) + refs[n + k][...].astype(F32)).astype(BF16)

    gspec = pl.BlockSpec((tr, cdim), lambda q, i, core_ref: ((2 * q + core_ref[0]) * nt + i, 0))
    rspec = pl.BlockSpec((tr, cdim), lambda q, i, core_ref: (q * nt + i, 0))
    return pl.pallas_call(
        body,
        name=name,
        grid_spec=pltpu.PrefetchScalarGridSpec(
            num_scalar_prefetch=1, grid=(N_CHIP, nt), in_specs=[gspec] * n + [rspec] * n, out_specs=[rspec] * n),
        out_shape=[jax.ShapeDtypeStruct((N_CHIP * r, cdim), BF16) for _ in range(n)],
        compiler_params=_cparams(("parallel", "parallel")),
    )(core, *grads, *recvd)


def _final_sum(chip, psums, recvd, name):
    n = len(psums)
    r = psums[0].shape[0] // N_CHIP
    cdim = psums[0].shape[1]
    tr = r // 2 if r % 32 == 0 else r
    nt = r // tr

    def body(chip_ref, *refs):
        del chip_ref
        for k in range(n):
            got = refs[n + k]
            tot = refs[k][...].astype(F32) + got[0].astype(F32)
            tot = tot + got[1].astype(F32)
            tot = tot + got[2].astype(F32)
            refs[2 * n + k][...] = tot

    pspec = pl.BlockSpec((tr, cdim), lambda i, chip_ref: (chip_ref[0] * nt + i, 0))
    rspec = pl.BlockSpec((3, tr, cdim), lambda i, chip_ref: (0, i, 0))
    ospec = pl.BlockSpec((tr, cdim), lambda i, chip_ref: (i, 0))
    return pl.pallas_call(
        body,
        name=name,
        grid_spec=pltpu.PrefetchScalarGridSpec(
            num_scalar_prefetch=1, grid=(nt,), in_specs=[pspec] * n + [rspec] * n, out_specs=[ospec] * n),
        out_shape=[jax.ShapeDtypeStruct((r, cdim), F32) for _ in range(n)],
        compiler_params=_cparams(("parallel",)),
    )(chip, *psums, *[g.reshape(3, r, cdim) for g in recvd])


SMALL_ROWS = 16


def _all_reduce_small(part):
    def body(p_ref, o_ref, buf, send_sems, recv_sems):
        x, y, c, _ = _place()
        me = 4 * x + 2 * y + c
        buf[me] = p_ref[...]
        copies = []
        for d in range(1, N_DEV):
            peer = me ^ d
            copies.append(pltpu.make_async_remote_copy(
                src_ref=p_ref, dst_ref=buf.at[me], send_sem=send_sems.at[d - 1], recv_sem=recv_sems.at[d - 1],
                device_id=(peer // 4, (peer // 2) % 2, peer % 2), device_id_type=MESH))
        for cp in copies:
            cp.start()
        for cp in copies:
            cp.wait_recv()
        for cp in copies:
            cp.wait_send()
        tot = buf[0]
        for d in range(1, N_DEV):
            tot = tot + buf[d]
        o_ref[...] = tot

    return pl.pallas_call(
        body,
        name="all_reduce_small",
        in_specs=[pl.BlockSpec(memory_space=pltpu.VMEM)],
        out_specs=pl.BlockSpec(memory_space=pltpu.VMEM),
        out_shape=jax.ShapeDtypeStruct(part.shape, F32),
        scratch_shapes=[pltpu.VMEM((N_DEV,) + part.shape, F32), pltpu.SemaphoreType.DMA((N_DEV - 1,)),
                        pltpu.SemaphoreType.DMA((N_DEV - 1,))],
    )(part)


ADAMW_STEPS = 4


def _adamw(ws, gs, ms, vs, name, hosted=()):
    n = len(ws)
    steps = ADAMW_STEPS if all(w.shape[0] % (8 * ADAMW_STEPS) == 0 for w in ws) else 1
    c1 = 1.0 - ADAM_B1 ** ADAM_STEP
    c2 = 1.0 - ADAM_B2 ** ADAM_STEP

    def body(*refs):
        for k in range(n):
            w, g, m, v = (refs[j * n + k][...] for j in range(4))
            m2 = ADAM_B1 * m + (1.0 - ADAM_B1) * g
            v2 = ADAM_B2 * v + (1.0 - ADAM_B2) * (g * g)
            delta = -ADAM_LR * ((m2 * (1.0 / c1)) / (jnp.sqrt(v2 * (1.0 / c2)) + ADAM_EPS) + ADAM_WD * w)
            refs[4 * n + k][...] = delta
            refs[5 * n + k][...] = m2
            refs[6 * n + k][...] = v2

    specs = [pl.BlockSpec((w.shape[0] // steps, w.shape[1]), lambda i: (i, 0)) for w in ws]
    shapes = [jax.ShapeDtypeStruct(w.shape, F32) for w in ws]
    outs = _call(
        body,
        name=name,
        grid=(steps,),
        in_specs=specs * 4,
        out_specs=specs * 3,
        out_shape=shapes * 3,
        args=(*ws, *gs, *ms, *vs), sem=("parallel",), hosted=hosted)
    return outs[:n], outs[n:2 * n], outs[2 * n:]


def _adamw_reduced(chip, ws, psums, recvd, ms, vs, steps, name):
    n = len(ws)
    c1 = 1.0 - ADAM_B1 ** ADAM_STEP
    c2 = 1.0 - ADAM_B2 ** ADAM_STEP

    def body(chip_ref, *refs):
        del chip_ref
        for k in range(n):
            w, m, v = (refs[j * n + k][...] for j in (0, 3, 4))
            got = refs[2 * n + k]
            g = refs[n + k][...].astype(F32) + got[0].astype(F32)
            g = g + got[1].astype(F32)
            g = g + got[2].astype(F32)
            m2 = ADAM_B1 * m + (1.0 - ADAM_B1) * g
            v2 = ADAM_B2 * v + (1.0 - ADAM_B2) * (g * g)
            refs[5 * n + k][...] = g
            refs[6 * n + k][...] = -ADAM_LR * (
                (m2 * (1.0 / c1)) / (jnp.sqrt(v2 * (1.0 / c2)) + ADAM_EPS) + ADAM_WD * w)
            refs[7 * n + k][...] = m2
            refs[8 * n + k][...] = v2

    def blk(w):
        return (w.shape[0] // steps, w.shape[1])

    own = [pl.BlockSpec(blk(w), lambda i, chip_ref: (i, 0)) for w in ws]
    psum = [pl.BlockSpec(blk(w), lambda i, chip_ref: (chip_ref[0] * steps + i, 0)) for w in ws]
    recv = [pl.BlockSpec((3,) + blk(w), lambda i, chip_ref: (0, i, 0)) for w in ws]
    shapes = [jax.ShapeDtypeStruct(w.shape, F32) for w in ws]
    outs = pl.pallas_call(
        body,
        name=name,
        grid_spec=pltpu.PrefetchScalarGridSpec(
            num_scalar_prefetch=1, grid=(steps,), in_specs=own + psum + recv + own + own, out_specs=own * 4),
        out_shape=shapes * 4,
        compiler_params=_cparams(("parallel",)),
    )(chip, *ws, *psums, *[r.reshape((3,) + w.shape) for r, w in zip(recvd, ws)], *ms, *vs)
    return outs[:n], outs[n:2 * n], outs[2 * n:3 * n], outs[3 * n:]


def _bias_b():
    pad = B_PREV * CHUNK
    slopes = np.array([2.0 ** (-8.0 * (i + 1) / B_Q_HEADS) for i in range(B_Q_HEADS)], dtype=np.float32)
    dist = np.abs(np.arange(TQ)[:, None] - np.arange(TQ + pad)[None, :] + pad).astype(np.float32)
    bias = -slopes.reshape(B_Q_HEADS, 1, 1) * dist[None]
    qc = (np.arange(TQ)[:, None] + pad) // CHUNK
    kc = np.arange(TQ + pad)[None, :] // CHUNK
    allowed = (kc <= qc) & (kc >= qc - B_PREV)
    return np.where(allowed[None], bias, np.float32(NEG_INF)).astype(np.float32)


def kernel(x, ffn1_norm, ffn1_w_gate, ffn1_w_up, ffn1_w_down, mix_norm, w_in, rel_bias, sinks, w_proj_a, w_proj_b, w_out, ffn2_norm, ffn2_w_gate, ffn2_w_up, ffn2_w_down, final_norm, loss_target, m_ffn1_norm, m_ffn1_w_gate, m_ffn1_w_up, m_ffn1_w_down, m_mix_norm, m_w_in, m_rel_bias, m_sinks, m_w_proj_a, m_w_proj_b, m_w_out, m_ffn2_norm, m_ffn2_w_gate, m_ffn2_w_up, m_ffn2_w_down, m_final_norm, v_ffn1_norm, v_ffn1_w_gate, v_ffn1_w_up, v_ffn1_w_down, v_mix_norm, v_w_in, v_rel_bias, v_sinks, v_w_proj_a, v_w_proj_b, v_w_out, v_ffn2_norm, v_ffn2_w_gate, v_ffn2_w_up, v_ffn2_w_down, v_final_norm):
    bsz, s_len, _ = x.shape
    t = bsz * s_len
    core = lax.axis_index("c").astype(jnp.int32).reshape(1)
    chip = (2 * lax.axis_index("x") + lax.axis_index("y")).astype(jnp.int32).reshape(1)

    proj_rows = jnp.concatenate([w_proj_a.T, w_proj_b.T], axis=1)
    sh_g1, sh_u1, sh_d1, sh_in, sh_proj, sh_out, sh_g2, sh_u2, sh_d2 = _to_bf16(
        [ffn1_w_gate.T, ffn1_w_up.T, ffn1_w_down, w_in.T, proj_rows, w_out, ffn2_w_gate.T, ffn2_w_up.T, ffn2_w_down],
        "weights_to_bf16")

    gather_up1 = _Gather([sh_g1, sh_u1])
    far = jnp.broadcast_to(rel_bias[:, REL_TABLE - 1:REL_TABLE], (A_HEADS, REL_WRAP // 2))
    tv = jnp.concatenate([far, jnp.flip(rel_bias, axis=1), jnp.zeros((A_HEADS, REL_WRAP // 2 - REL_TABLE), F32)], axis=1)
    bias_a = _bias_a_build(tv.reshape(A_HEADS, 1, REL_WRAP), hosted=[gather_up1])
    wg1, wu1 = gather_up1.result
    gather_down1 = _Gather([sh_d1, sh_in])
    gather_out = _Gather([sh_proj, sh_out])
    gather_ffn2_gate = _Gather([sh_g2])
    gather_ffn2_rest = _Gather([sh_u2, sh_d2])

    x0 = x.reshape(t, D_MODEL)
    tgt = loss_target.reshape(t, D_MODEL)
    gam1, gam2, gam3, gam4 = (g.reshape(1, D_MODEL) for g in (ffn1_norm, mix_norm, ffn2_norm, final_norm))

    h1, g1, u1, a1 = _ffn_up(x0, gam1, wg1, wu1, "ffn1_up", hosted=[gather_down1])
    wd1, win_t = gather_down1.result
    x1 = _ffn_down(x0, a1, wd1, "ffn1_down", hosted=[gather_out])
    proj_t, wout = gather_out.result
    h2, qkv_a, qkv_b, gates = _proj_fwd(x1, gam2, win_t, hosted=[gather_ffn2_gate])
    (wg2,) = gather_ffn2_gate.result
    qkv_a3 = qkv_a.reshape(bsz, s_len, QKV_A)
    qkv_b3 = qkv_b.reshape(bsz, s_len, QKV_B)

    bias_b = jnp.asarray(_bias_b())
    sink_rows = jnp.broadcast_to(sinks.reshape(B_Q_HEADS, 1, 1), (B_Q_HEADS, 8, LANES))

    oa, lse_a = _attn_a_fwd(qkv_a3, bias_a, hosted=[gather_ffn2_rest])
    oa = oa.reshape(t, A_WIDTH)
    wu2, wd2 = gather_ffn2_rest.result
    ob, lse_b = _attn_b_fwd(qkv_b3, bias_b, sink_rows)
    ob = ob.reshape(t, B_Q_WIDTH)
    x2, ya, yb, mg = _mix_out_fwd(x1, oa, ob, gates, proj_t, wout)
    h3, g2, u2, a2, x3 = _ffn_fwd(x2, gam3, wg2, wu2, wd2, "ffn2_fwd")

    dx2, dg2, du2, db2, dgam3, dgam4, loss_part = _ffn_bwd_head(x3, gam4, tgt, x2, gam3, g2, u2, wg2, wu2, wd2,
                                                                "ffn2_bwd")
    gw_ffn2 = [_mm_tn([dg2], h3, "grad_ffn2_gate"), _mm_tn([du2], h3, "grad_ffn2_up"),
               _mm_tn([a2], db2, "grad_ffn2_down")]
    pairx_ffn2 = _PairExchange(gw_ffn2)
    doa, dob, dgates, gw_out, gw_proj = _mix_out_bwd(dx2, gates, ya, yb, mg, oa, ob, proj_t, wout,
                                                     hosted=[pairx_ffn2])
    psum_ffn2 = _pair_sum(core, gw_ffn2, pairx_ffn2.result, "pair_sum_ffn2")

    chipx_ffn2 = _ChipExchange(psum_ffn2[0:2])
    dqa, dka, dva, dbias_a = _attn_a_bwd(qkv_a3, bias_a, doa.reshape(bsz, s_len, A_WIDTH),
                                         oa.reshape(bsz, s_len, A_WIDTH), lse_a, hosted=[chipx_ffn2])
    pairx_out = _PairExchange([gw_proj, gw_out])
    chipx_ffn2_down = _ChipExchange(psum_ffn2[2:3])
    dqb, dkvb, dsink = _attn_b_bwd(qkv_b3, bias_b, sink_rows, dob.reshape(bsz, s_len, B_Q_WIDTH),
                                   ob.reshape(bsz, s_len, B_Q_WIDTH), lse_b, hosted=[pairx_out, chipx_ffn2_down])
    drel_lanes = _relbias_grad(dbias_a)
    dproj = [dqa.reshape(t, A_WIDTH), dka.reshape(t, A_WIDTH), dva.reshape(t, A_WIDTH), dqb.reshape(t, B_Q_WIDTH),
             dkvb.reshape(t, 2 * B_KV_WIDTH), dgates]

    gw_in = _mm_tn(dproj, h2, "grad_w_in")
    pairx_in = _PairExchange([gw_in])
    psum_out = _pair_sum(core, [gw_proj, gw_out], pairx_out.result, "pair_sum_mix")
    chipx_out = _ChipExchange(psum_out)
    dx1, db1, dgam2 = _proj_bwd(dx2, x1, gam2, dproj, win_t, hosted=[pairx_in, chipx_out])
    psum_in = _pair_sum(core, [gw_in], pairx_in.result, "pair_sum_w_in")
    gw_d1 = _mm_tn([a1], db1, "grad_ffn1_down")

    chipx_in = _ChipExchange(psum_in)
    pairx_d1 = _PairExchange([gw_d1])
    dg1, du1 = _ffn_bwd_act(dx1, g1, u1, wd1, "ffn1_bwd_act", hosted=[chipx_in, pairx_d1])
    psum_d1 = _pair_sum(core, [gw_d1], pairx_d1.result, "pair_sum_ffn1_down")
    chipx_d1 = _ChipExchange(psum_d1)
    gw_g1 = _mm_tn([dg1], h1, "grad_ffn1_gate", hosted=[chipx_d1])
    from_sibling_g1 = _exchange_alone(_PairExchange([gw_g1]), "pair_exchange_ffn1_gate")
    psum_g1 = _pair_sum(core, [gw_g1], from_sibling_g1, "pair_sum_ffn1_gate")
    chipx_g1 = _ChipExchange(psum_g1)
    gw_u1 = _mm_tn([du1], h1, "grad_ffn1_up", hosted=[chipx_g1])
    from_sibling_u1 = _exchange_alone(_PairExchange([gw_u1]), "pair_exchange_ffn1_up")
    psum_u1 = _pair_sum(core, [gw_u1], from_sibling_u1, "pair_sum_ffn1_up")
    chipx_u1 = _ChipExchange(psum_u1)
    dx0, dgam1 = _ffn_bwd_in(dx1, x0, gam1, dg1, du1, wg1, wu1, "ffn1_bwd_in", hosted=[chipx_u1])

    (g_proj,) = _final_sum(chip, psum_out[0:1], chipx_out.result[0:1], "grad_sum_proj")
    grads = {"w_proj_a": g_proj[:, 0:A_WIDTH].T, "w_proj_b": g_proj[:, A_WIDTH:].T}

    def row_of(v):
        return jnp.pad(v.reshape(1, -1), ((0, 0), (0, D_MODEL - v.size)))

    def table_rows(v):
        return jnp.pad(v, ((0, 0), (0, D_MODEL - REL_TABLE)))

    drel_local = jnp.flip(drel_lanes[:, 0, 0:REL_TABLE], axis=1)
    small_part = jnp.concatenate(
        [jnp.sum(dgam1, axis=0, keepdims=True), jnp.sum(dgam2, axis=0, keepdims=True),
         jnp.sum(dgam3, axis=0, keepdims=True), jnp.sum(dgam4, axis=0, keepdims=True),
         row_of(jnp.sum(loss_part)), row_of(dsink[:, 0, 0]), jnp.zeros((2, D_MODEL), F32),
         table_rows(drel_local)], axis=0)
    small = _all_reduce_small(small_part)
    loss = small[4, 0]

    def pack(n1, n2, n3, n4, sk, tb):
        return jnp.concatenate([n1.reshape(1, -1), n2.reshape(1, -1), n3.reshape(1, -1), n4.reshape(1, -1),
                                jnp.zeros((1, D_MODEL), F32), row_of(sk), jnp.zeros((2, D_MODEL), F32), table_rows(tb)],
                               axis=0)

    live = np.zeros((SMALL_ROWS, D_MODEL), np.float32)
    live[0:4] = 1.0
    live[5, 0:B_Q_HEADS] = 1.0
    live[8:16, 0:REL_TABLE] = 1.0
    small_g = small * jnp.asarray(live)
    sw = pack(ffn1_norm, mix_norm, ffn2_norm, final_norm, sinks, rel_bias)
    sm = pack(m_ffn1_norm, m_mix_norm, m_ffn2_norm, m_final_norm, m_sinks, m_rel_bias)
    sv = pack(v_ffn1_norm, v_mix_norm, v_ffn2_norm, v_final_norm, v_sinks, v_rel_bias)
    (sd,), (snm,), (snv,) = _adamw([sw], [small_g], [sm], [sv], "adamw_small")

    def unpack(p):
        return {"ffn1_norm": p[0], "mix_norm": p[1], "ffn2_norm": p[2], "final_norm": p[3],
                "sinks": p[5, 0:B_Q_HEADS], "rel_bias": p[8:16, 0:REL_TABLE]}

    grads.update(unpack(small_g))
    delta, new_m, new_v = unpack(sd), unpack(snm), unpack(snv)

    wmv = {
        "ffn1_w_gate": (ffn1_w_gate, m_ffn1_w_gate, v_ffn1_w_gate), "ffn1_w_up": (ffn1_w_up, m_ffn1_w_up, v_ffn1_w_up),
        "ffn1_w_down": (ffn1_w_down, m_ffn1_w_down, v_ffn1_w_down), "w_in": (w_in, m_w_in, v_w_in),
        "w_proj_a": (w_proj_a, m_w_proj_a, v_w_proj_a), "w_proj_b": (w_proj_b, m_w_proj_b, v_w_proj_b),
        "w_out": (w_out, m_w_out, v_w_out),
        "ffn2_w_gate": (ffn2_w_gate, m_ffn2_w_gate, v_ffn2_w_gate), "ffn2_w_up": (ffn2_w_up, m_ffn2_w_up, v_ffn2_w_up),
        "ffn2_w_down": (ffn2_w_down, m_ffn2_w_down, v_ffn2_w_down),
    }
    row_form_names = ("ffn1_w_gate", "ffn1_w_up", "w_in", "ffn2_w_gate", "ffn2_w_up")

    def form(n, a):
        return a.T if n in row_form_names else a

    def reduced_group(gname, names, psums, recvd, steps):
        gs_, ds_, ms_, vs_ = _adamw_reduced(
            chip, [form(n, wmv[n][0]) for n in names], psums, recvd, [form(n, wmv[n][1]) for n in names],
            [form(n, wmv[n][2]) for n in names], steps, gname)
        for n, g_, d_, m_, v_ in zip(names, gs_, ds_, ms_, vs_):
            grads[n], delta[n], new_m[n], new_v[n] = form(n, g_), form(n, d_), form(n, m_), form(n, v_)

    reduced_group("adamw_ffn", ["ffn1_w_gate", "ffn1_w_up", "ffn1_w_down", "ffn2_w_gate", "ffn2_w_up", "ffn2_w_down"],
                  psum_g1 + psum_u1 + psum_d1 + psum_ffn2,
                  chipx_g1.result + chipx_u1.result + chipx_d1.result + chipx_ffn2.result + chipx_ffn2_down.result, 11)
    reduced_group("adamw_in_out", ["w_in", "w_out"], psum_in + psum_out[1:2], chipx_in.result + chipx_out.result[1:2], 2)
    names = ["w_proj_a", "w_proj_b"]
    ds_, ms_, vs_ = _adamw([wmv[n][0] for n in names], [grads[n] for n in names], [wmv[n][1] for n in names],
                           [wmv[n][2] for n in names], "adamw_proj")
    for n, d_, m_, v_ in zip(names, ds_, ms_, vs_):
        delta[n], new_m[n], new_v[n] = d_, m_, v_

    order = ["ffn1_norm", "ffn1_w_gate", "ffn1_w_up", "ffn1_w_down", "mix_norm", "w_in", "rel_bias", "sinks",
             "w_proj_a", "w_proj_b", "w_out", "ffn2_norm", "ffn2_w_gate", "ffn2_w_up", "ffn2_w_down", "final_norm"]
    grad_x = dx0.reshape(bsz, s_len, D_MODEL)
    return (loss, grad_x, *[grads[n] for n in order], *[delta[n] for n in order], *[new_m[n] for n in order],
            *[new_v[n] for n in order])
```
